```python
import jax
import jax.numpy as jnp
from jax import lax
import numpy as np

D_MODEL = 1024
BATCH = 8
SEQ = 8192
DEPTH = 2

MIX_WIDTH = D_MODEL // 2
N_BRANCH = 3
NORM_EPS = 1e-6

SGU_GROUPS = 4
SGU_CHUNK = 128
SGU_WIDTH = MIX_WIDTH
SGU_GROUP_DIM = SGU_WIDTH // SGU_GROUPS

SWA_HEADS = 8
SWA_KV_HEADS = 2
SWA_HEAD_DIM = MIX_WIDTH // SWA_HEADS
SWA_GROUP = SWA_HEADS // SWA_KV_HEADS
WINDOW = 128
ROPE_THETA = 500000.0
ROPE_DIM = SWA_HEAD_DIM // 4

DN_HEADS = 4
DN_HEAD_DIM = MIX_WIDTH // DN_HEADS
DN_CONV = 4
DN_CHUNK = 64

D_FF = ((8 * D_MODEL // 3 + 255) // 256) * 256

IN_WIDTHS = (SGU_WIDTH, SGU_WIDTH,
             SWA_HEADS * SWA_HEAD_DIM, SWA_KV_HEADS * SWA_HEAD_DIM, SWA_KV_HEADS * SWA_HEAD_DIM,
             3 * MIX_WIDTH, MIX_WIDTH, DN_HEADS, DN_HEADS,
             N_BRANCH * D_MODEL)
IN_COLS = sum(IN_WIDTHS)

kernel_name = 'hybrid_gated_parallel_mixers'


def rmsnorm(x, g):
    xf = x.astype(jnp.float32)
    y = xf * lax.rsqrt(jnp.mean(xf * xf, axis=-1, keepdims=True) + NORM_EPS)
    return (y * g.astype(jnp.float32)).astype(x.dtype)


def layernorm(x, g, b):
    xf = x.astype(jnp.float32)
    xc = xf - jnp.mean(xf, axis=-1, keepdims=True)
    y = xc * lax.rsqrt(jnp.mean(xc * xc, axis=-1, keepdims=True) + NORM_EPS)
    return (y * g.astype(jnp.float32) + b.astype(jnp.float32)).astype(x.dtype)


def l2norm(x):
    return x * lax.rsqrt(jnp.sum(x * x, axis=-1, keepdims=True) + NORM_EPS)


def split_columns(t):
    parts, start = [], 0
    for w in IN_WIDTHS:
        parts.append(t[..., start:start + w])
        start += w
    return parts


def rotary_tables(positions):
    inv_freq = ROPE_THETA ** (-jnp.arange(0, ROPE_DIM, 2, dtype=jnp.float32) / ROPE_DIM)
    ang = positions.astype(jnp.float32)[..., None] * inv_freq
    return jnp.cos(ang)[:, :, None, :], jnp.sin(ang)[:, :, None, :]


def apply_partial_rope(x, cos, sin):
    half = ROPE_DIM // 2
    x1, x2, rest = x[..., :half], x[..., half:ROPE_DIM], x[..., ROPE_DIM:]
    c, s = cos.astype(x.dtype), sin.astype(x.dtype)
    return jnp.concatenate([x1 * c - x2 * s, x2 * c + x1 * s, rest], axis=-1)


def spatial_gating(u, v, ln_g, ln_b, w_s, b_s):
    B_, S_ = u.shape[:2]
    nc = S_ // SGU_CHUNK
    vn = layernorm(v, ln_g, ln_b).reshape(B_, nc, SGU_CHUNK, SGU_GROUPS, SGU_GROUP_DIM)
    causal = jnp.tril(jnp.ones((SGU_CHUNK, SGU_CHUNK), dtype=bool))
    w_causal = jnp.where(causal, w_s, 0.0).astype(vn.dtype)
    mixed = jnp.einsum('gts,bnsgc->bntgc', w_causal, vn) + b_s.T.astype(vn.dtype)[None, None, :, :, None]
    return u * mixed.reshape(B_, S_, SGU_WIDTH)


def sliding_window_attention(q, k, v, sinks, cos, sin):
    B_, S_ = q.shape[:2]
    nc = S_ // WINDOW
    q = apply_partial_rope(q, cos, sin) * (SWA_HEAD_DIM ** -0.5)
    k = apply_partial_rope(k, cos, sin)
    qb = q.reshape(B_, nc, WINDOW, SWA_KV_HEADS, SWA_GROUP, SWA_HEAD_DIM)

    def band(t):
        cur = t.reshape(B_, nc, WINDOW, SWA_KV_HEADS, SWA_HEAD_DIM)
        prev = jnp.concatenate([jnp.zeros_like(cur[:, :1]), cur[:, :-1]], axis=1)
        return jnp.concatenate([prev, cur], axis=2)

    kb, vb = band(k), band(v)
    logits = jnp.einsum('bnqkgd,bnskd->bnkgqs', qb, kb).astype(jnp.float32)
    qi = jnp.arange(WINDOW)[:, None]
    sj = jnp.arange(2 * WINDOW)[None, :]
    diff = qi + WINDOW - sj
    in_band = (diff >= 0) & (diff < WINDOW)
    valid = (jnp.arange(nc) > 0)[:, None, None] | (sj >= WINDOW)[None]
    mask = in_band[None] & valid
    logits = jnp.where(mask[None, :, None, None], logits, -jnp.inf)
    sink = jnp.broadcast_to(sinks.astype(jnp.float32).reshape(1, 1, SWA_KV_HEADS, SWA_GROUP, 1, 1),
                            logits.shape[:-1] + (1,))
    probs = jax.nn.softmax(jnp.concatenate([logits, sink], axis=-1), axis=-1)[..., :-1]
    out = jnp.einsum('bnkgqs,bnskd->bnqkgd', probs.astype(vb.dtype), vb)
    return out.reshape(B_, S_, SWA_HEADS * SWA_HEAD_DIM)


def causal_short_conv(x, w):
    S_ = x.shape[1]
    xp = jnp.pad(x, ((0, 0), (DN_CONV - 1, 0), (0, 0)))
    out = xp[:, 0:S_] * w[0]
    for i in range(1, DN_CONV):
        out = out + xp[:, i:i + S_] * w[i]
    return jax.nn.silu(out)


def gated_deltanet(qkv, z, beta_logit, a_logit, conv_w, a_log, dt_bias, norm_g):
    B_, S_ = qkv.shape[:2]
    in_dtype = qkv.dtype
    nt = S_ // DN_CHUNK
    H, hd, C = DN_HEADS, DN_HEAD_DIM, DN_CHUNK
    qkv = causal_short_conv(qkv, conv_w).astype(jnp.float32)
    q = l2norm(qkv[..., :MIX_WIDTH].reshape(B_, S_, H, hd)) * (hd ** -0.5)
    k = l2norm(qkv[..., MIX_WIDTH:2 * MIX_WIDTH].reshape(B_, S_, H, hd))
    v = qkv[..., 2 * MIX_WIDTH:].reshape(B_, S_, H, hd)
    beta = jax.nn.sigmoid(beta_logit.astype(jnp.float32))
    g = -jnp.exp(a_log.astype(jnp.float32)) * jax.nn.softplus(a_logit.astype(jnp.float32) + dt_bias.astype(jnp.float32))

    def to_chunks(t):
        t = t.reshape((B_, nt, C) + t.shape[2:])
        return jnp.swapaxes(jnp.swapaxes(t, 0, 1), 2, 3)

    q, k, v, beta, g = to_chunks(q), to_chunks(k), to_chunks(v), to_chunks(beta), to_chunks(g)
    gc = jnp.cumsum(g, axis=-1)
    tril = jnp.tril(jnp.ones((C, C), dtype=bool))
    strict = jnp.tril(jnp.ones((C, C), dtype=bool), -1)
    decay = jnp.exp(jnp.where(tril, gc[..., :, None] - gc[..., None, :], -jnp.inf))
    k_beta = k * beta[..., None]
    lower = jnp.where(strict, jnp.einsum('nbhid,nbhjd->nbhij', k_beta, k) * decay, 0.0)
    a_mat = lower + jnp.eye(C, dtype=jnp.float32)
    rhs = jnp.concatenate([v * beta[..., None], k_beta * jnp.exp(gc)[..., None]], axis=-1)
    sol = lax.linalg.triangular_solve(a_mat, rhs, left_side=True, lower=True, unit_diagonal=True)
    u_c, w_c = sol[..., :hd], sol[..., hd:]
    attn = jnp.einsum('nbhid,nbhjd->nbhij', q, k) * decay
    q_dec = q * jnp.exp(gc)[..., None]
    k_dec = k * jnp.exp(gc[..., -1:] - gc)[..., None]
    c_dec = jnp.exp(gc[..., -1])

    def step(state, xs):
        qd, wc, uc, at, kd, cd = xs
        v_new = uc - jnp.einsum('bhcd,bhde->bhce', wc, state)
        o_c = jnp.einsum('bhcd,bhde->bhce', qd, state) + jnp.einsum('bhij,bhje->bhie', at, v_new)
        state = state * cd[..., None, None] + jnp.einsum('bhcd,bhce->bhde', kd, v_new)
        return state, o_c

    state0 = jnp.zeros((B_, H, hd, hd), dtype=jnp.float32)
    _, o = lax.scan(step, state0, (q_dec, w_c, u_c, attn, k_dec, c_dec))
    o = jnp.swapaxes(jnp.swapaxes(o, 0, 1), 2, 3).reshape(B_, S_, H, hd)
    o = rmsnorm(o, norm_g) * jax.nn.silu(z.astype(jnp.float32).reshape(B_, S_, H, hd))
    return o.reshape(B_, S_, MIX_WIDTH).astype(in_dtype)


def _fwd_setup_inputs(seed: int = 0) -> dict:
    key = jax.random.key(seed)
    ks = jax.random.split(key, 20)
    f32 = jnp.float32

    def nrm(k, shape, scale):
        return jax.random.normal(k, shape, dtype=f32) * scale

    dt = jnp.exp(jax.random.uniform(ks[11], (DEPTH, DN_HEADS), dtype=f32,
                                    minval=np.log(1e-3), maxval=np.log(1e-1)))
    return {
        'x': nrm(ks[0], (BATCH, SEQ, D_MODEL), 1.0),
        'positions': jnp.broadcast_to(jnp.arange(SEQ, dtype=jnp.int32), (BATCH, SEQ)),
        'attn_norm': 1.0 + nrm(ks[1], (DEPTH, D_MODEL), 0.02),
        'w_in': nrm(ks[2], (DEPTH, D_MODEL, IN_COLS), D_MODEL ** -0.5),
        'sgu_ln_g': 1.0 + nrm(ks[3], (DEPTH, SGU_WIDTH), 0.02),
        'sgu_ln_b': nrm(ks[4], (DEPTH, SGU_WIDTH), 0.02),
        'sgu_w': nrm(ks[5], (DEPTH, SGU_GROUPS, SGU_CHUNK, SGU_CHUNK), SGU_CHUNK ** -0.5),
        'sgu_b': 1.0 + nrm(ks[6], (DEPTH, SGU_GROUPS, SGU_CHUNK), 0.02),
        'attn_sinks': nrm(ks[7], (DEPTH, SWA_HEADS), 0.5),
        'dn_conv_w': nrm(ks[8], (DEPTH, DN_CONV, 3 * MIX_WIDTH), DN_CONV ** -0.5),
        'dn_a_log': jnp.log(jax.random.uniform(ks[9], (DEPTH, DN_HEADS), dtype=f32, minval=1.0, maxval=16.0)),
        'dn_dt_bias': dt + jnp.log(-jnp.expm1(-dt)),
        'dn_norm': 1.0 + nrm(ks[10], (DEPTH, DN_HEAD_DIM), 0.02),
        'w_branch': nrm(ks[12], (DEPTH, N_BRANCH, MIX_WIDTH, D_MODEL), MIX_WIDTH ** -0.5),
        'w_out': nrm(ks[13], (DEPTH, D_MODEL, D_MODEL), D_MODEL ** -0.5),
        'ffn_norm': 1.0 + nrm(ks[14], (DEPTH, D_MODEL), 0.02),
        'w_gate_up': nrm(ks[15], (DEPTH, D_MODEL, 2 * D_FF), D_MODEL ** -0.5),
        'w_down': nrm(ks[16], (DEPTH, D_FF, D_MODEL), D_FF ** -0.5),
        'final_norm': 1.0 + nrm(ks[17], (D_MODEL,), 0.02),
    }


def _fwd_reference(x, positions, attn_norm, w_in, sgu_ln_g, sgu_ln_b, sgu_w, sgu_b, attn_sinks,
              dn_conv_w, dn_a_log, dn_dt_bias, dn_norm, w_branch, w_out, ffn_norm,
              w_gate_up, w_down, final_norm):
    B_, S_ = x.shape[:2]
    cos, sin = rotary_tables(positions)
    for layer in range(DEPTH):
        h = rmsnorm(x, attn_norm[layer])
        proj = jnp.einsum('bsd,dc->bsc', h, w_in[layer])
        u_a, v_a, q_b, k_b, v_b, qkv_c, z_c, beta_c, a_c, gate_pre = split_columns(proj)
        out_a = spatial_gating(jax.nn.gelu(u_a), jax.nn.gelu(v_a), sgu_ln_g[layer], sgu_ln_b[layer],
                               sgu_w[layer], sgu_b[layer])
        out_b = sliding_window_attention(q_b.reshape(B_, S_, SWA_HEADS, SWA_HEAD_DIM),
                                         k_b.reshape(B_, S_, SWA_KV_HEADS, SWA_HEAD_DIM),
                                         v_b.reshape(B_, S_, SWA_KV_HEADS, SWA_HEAD_DIM),
                                         attn_sinks[layer], cos, sin)
        out_c = gated_deltanet(qkv_c, z_c, beta_c, a_c, dn_conv_w[layer], dn_a_log[layer],
                               dn_dt_bias[layer], dn_norm[layer])
        branches = jnp.stack([out_a, out_b, out_c], axis=0)
        branch_d = jnp.einsum('nbsc,ncd->nbsd', branches, w_branch[layer])
        gates = jax.nn.sigmoid(gate_pre.reshape(B_, S_, N_BRANCH, D_MODEL))
        merged = jnp.einsum('bsnd,nbsd->bsd', gates, branch_d)
        x = x + jnp.einsum('bsd,de->bse', merged, w_out[layer])
        h2 = rmsnorm(x, ffn_norm[layer])
        gu = jnp.einsum('bsd,df->bsf', h2, w_gate_up[layer])
        x = x + jnp.einsum('bsf,fd->bsd', jax.nn.silu(gu[..., :D_FF]) * gu[..., D_FF:], w_down[layer])
    return rmsnorm(x, final_norm)


import jax as _jax
import jax.numpy as _jnp

TWIN_FORMAT = 'train_step'
FWD_PARAMS = ['x', 'positions', 'attn_norm', 'w_in', 'sgu_ln_g', 'sgu_ln_b', 'sgu_w', 'sgu_b', 'attn_sinks', 'dn_conv_w', 'dn_a_log', 'dn_dt_bias', 'dn_norm', 'w_branch', 'w_out', 'ffn_norm', 'w_gate_up', 'w_down', 'final_norm']
TWIN_WEIGHTS = ['attn_norm', 'w_in', 'sgu_ln_g', 'sgu_ln_b', 'sgu_w', 'sgu_b', 'attn_sinks', 'dn_conv_w', 'dn_a_log', 'dn_dt_bias', 'dn_norm', 'w_branch', 'w_out', 'ffn_norm', 'w_gate_up', 'w_down', 'final_norm']
TWIN_DIFF_INPUT = 'x'
TWIN_INPUTS = ['x', 'positions', 'attn_norm', 'w_in', 'sgu_ln_g', 'sgu_ln_b', 'sgu_w', 'sgu_b', 'attn_sinks', 'dn_conv_w', 'dn_a_log', 'dn_dt_bias', 'dn_norm', 'w_branch', 'w_out', 'ffn_norm', 'w_gate_up', 'w_down', 'final_norm', 'loss_target', 'm_attn_norm', 'm_w_in', 'm_sgu_ln_g', 'm_sgu_ln_b', 'm_sgu_w', 'm_sgu_b', 'm_attn_sinks', 'm_dn_conv_w', 'm_dn_a_log', 'm_dn_dt_bias', 'm_dn_norm', 'm_w_branch', 'm_w_out', 'm_ffn_norm', 'm_w_gate_up', 'm_w_down', 'm_final_norm', 'v_attn_norm', 'v_w_in', 'v_sgu_ln_g', 'v_sgu_ln_b', 'v_sgu_w', 'v_sgu_b', 'v_attn_sinks', 'v_dn_conv_w', 'v_dn_a_log', 'v_dn_dt_bias', 'v_dn_norm', 'v_w_branch', 'v_w_out', 'v_ffn_norm', 'v_w_gate_up', 'v_w_down', 'v_final_norm']
TWIN_OUTPUTS = ['loss', 'grad_x', 'grad_attn_norm', 'grad_w_in', 'grad_sgu_ln_g', 'grad_sgu_ln_b', 'grad_sgu_w', 'grad_sgu_b', 'grad_attn_sinks', 'grad_dn_conv_w', 'grad_dn_a_log', 'grad_dn_dt_bias', 'grad_dn_norm', 'grad_w_branch', 'grad_w_out', 'grad_ffn_norm', 'grad_w_gate_up', 'grad_w_down', 'grad_final_norm', 'delta_attn_norm', 'delta_w_in', 'delta_sgu_ln_g', 'delta_sgu_ln_b', 'delta_sgu_w', 'delta_sgu_b', 'delta_attn_sinks', 'delta_dn_conv_w', 'delta_dn_a_log', 'delta_dn_dt_bias', 'delta_dn_norm', 'delta_w_branch', 'delta_w_out', 'delta_ffn_norm', 'delta_w_gate_up', 'delta_w_down', 'delta_final_norm', 'new_m_attn_norm', 'new_m_w_in', 'new_m_sgu_ln_g', 'new_m_sgu_ln_b', 'new_m_sgu_w', 'new_m_sgu_b', 'new_m_attn_sinks', 'new_m_dn_conv_w', 'new_m_dn_a_log', 'new_m_dn_dt_bias', 'new_m_dn_norm', 'new_m_w_branch', 'new_m_w_out', 'new_m_ffn_norm', 'new_m_w_gate_up', 'new_m_w_down', 'new_m_final_norm', 'new_v_attn_norm', 'new_v_w_in', 'new_v_sgu_ln_g', 'new_v_sgu_ln_b', 'new_v_sgu_w', 'new_v_sgu_b', 'new_v_attn_sinks', 'new_v_dn_conv_w', 'new_v_dn_a_log', 'new_v_dn_dt_bias', 'new_v_dn_norm', 'new_v_w_branch', 'new_v_w_out', 'new_v_ffn_norm', 'new_v_w_gate_up', 'new_v_w_down', 'new_v_final_norm']
TWIN_LEAF_KINDS = {'loss': 'loss', 'grad_x': 'grad_x', 'grad_attn_norm': 'grad_w', 'grad_w_in': 'grad_w', 'grad_sgu_ln_g': 'grad_w', 'grad_sgu_ln_b': 'grad_w', 'grad_sgu_w': 'grad_w', 'grad_sgu_b': 'grad_w', 'grad_attn_sinks': 'grad_w', 'grad_dn_conv_w': 'grad_w', 'grad_dn_a_log': 'grad_w', 'grad_dn_dt_bias': 'grad_w', 'grad_dn_norm': 'grad_w', 'grad_w_branch': 'grad_w', 'grad_w_out': 'grad_w', 'grad_ffn_norm': 'grad_w', 'grad_w_gate_up': 'grad_w', 'grad_w_down': 'grad_w', 'grad_final_norm': 'grad_w', 'delta_attn_norm': 'delta_w', 'delta_w_in': 'delta_w', 'delta_sgu_ln_g': 'delta_w', 'delta_sgu_ln_b': 'delta_w', 'delta_sgu_w': 'delta_w', 'delta_sgu_b': 'delta_w', 'delta_attn_sinks': 'delta_w', 'delta_dn_conv_w': 'delta_w', 'delta_dn_a_log': 'delta_w', 'delta_dn_dt_bias': 'delta_w', 'delta_dn_norm': 'delta_w', 'delta_w_branch': 'delta_w', 'delta_w_out': 'delta_w', 'delta_ffn_norm': 'delta_w', 'delta_w_gate_up': 'delta_w', 'delta_w_down': 'delta_w', 'delta_final_norm': 'delta_w', 'new_m_attn_norm': 'new_m', 'new_m_w_in': 'new_m', 'new_m_sgu_ln_g': 'new_m', 'new_m_sgu_ln_b': 'new_m', 'new_m_sgu_w': 'new_m', 'new_m_sgu_b': 'new_m', 'new_m_attn_sinks': 'new_m', 'new_m_dn_conv_w': 'new_m', 'new_m_dn_a_log': 'new_m', 'new_m_dn_dt_bias': 'new_m', 'new_m_dn_norm': 'new_m', 'new_m_w_branch': 'new_m', 'new_m_w_out': 'new_m', 'new_m_ffn_norm': 'new_m', 'new_m_w_gate_up': 'new_m', 'new_m_w_down': 'new_m', 'new_m_final_norm': 'new_m', 'new_v_attn_norm': 'new_v', 'new_v_w_in': 'new_v', 'new_v_sgu_ln_g': 'new_v', 'new_v_sgu_ln_b': 'new_v', 'new_v_sgu_w': 'new_v', 'new_v_sgu_b': 'new_v', 'new_v_attn_sinks': 'new_v', 'new_v_dn_conv_w': 'new_v', 'new_v_dn_a_log': 'new_v', 'new_v_dn_dt_bias': 'new_v', 'new_v_dn_norm': 'new_v', 'new_v_w_branch': 'new_v', 'new_v_w_out': 'new_v', 'new_v_ffn_norm': 'new_v', 'new_v_w_gate_up': 'new_v', 'new_v_w_down': 'new_v', 'new_v_final_norm': 'new_v'}


def _forward(args):
    return _fwd_reference(*[args[k] for k in FWD_PARAMS])


def _output_shape():
    def fwd():
        inp = _fwd_setup_inputs(0)
        return _fwd_reference(*[inp[k] for k in FWD_PARAMS])
    out = _jax.eval_shape(fwd)
    return out.shape, out.dtype

N_MICROBATCH = 1
ADAM_LR = 0.001
ADAM_B1 = 0.9
ADAM_B2 = 0.999
ADAM_EPS = 1e-08
ADAM_WD = 0.01
ADAM_STEP = 10
PER_EXAMPLE_BATCH_AXIS = {'x': 0, 'positions': 0, 'loss_target': 0}
SHARED_INPUTS = []
_WEIGHT_DTYPES = {'attn_norm': _jnp.float32, 'w_in': _jnp.float32, 'sgu_ln_g': _jnp.float32, 'sgu_ln_b': _jnp.float32, 'sgu_w': _jnp.float32, 'sgu_b': _jnp.float32, 'attn_sinks': _jnp.float32, 'dn_conv_w': _jnp.float32, 'dn_a_log': _jnp.float32, 'dn_dt_bias': _jnp.float32, 'dn_norm': _jnp.float32, 'w_branch': _jnp.float32, 'w_out': _jnp.float32, 'ffn_norm': _jnp.float32, 'w_gate_up': _jnp.float32, 'w_down': _jnp.float32, 'final_norm': _jnp.float32}
MOMENT_SCALE = {'attn_norm': 1.950564e-01, 'w_in': 7.498705e-02, 'sgu_ln_g': 9.390246e-02, 'sgu_ln_b': 8.362273e-02, 'sgu_w': 8.538801e-02, 'sgu_b': 1.238069e-01, 'attn_sinks': 3.705894e-02, 'dn_conv_w': 8.624623e-02, 'dn_a_log': 5.246449e-01, 'dn_dt_bias': 5.206302e-01, 'dn_norm': 2.250135e-01, 'w_branch': 7.812052e-02, 'w_out': 1.342025e-01, 'ffn_norm': 1.843274e-01, 'w_gate_up': 7.332539e-02, 'w_down': 1.197177e-01, 'final_norm': 6.409353e+01}


def _to_microbatches(a, axis):
    t = _jnp.moveaxis(a, axis, 0)
    t = t.reshape((N_MICROBATCH, t.shape[0] // N_MICROBATCH) + t.shape[1:])
    return _jnp.moveaxis(t, 1, axis + 1)


def setup_inputs(seed: int = 0) -> dict:
    inp = _fwd_setup_inputs(seed)
    key = _jax.random.fold_in(_jax.random.key(seed), 7919)
    shape, _ = _output_shape()
    out = dict(inp)
    out["loss_target"] = _jax.random.normal(_jax.random.fold_in(key, 0), shape, _jnp.float32)
    for i, name in enumerate(TWIN_WEIGHTS):
        w = inp[name].astype(_jnp.float32)
        if MOMENT_SCALE is None:
            s = _jnp.sqrt(_jnp.mean(_jnp.square(w)) + 1e-30)
        else:
            s = MOMENT_SCALE[name]
        km, kv = _jax.random.split(_jax.random.fold_in(key, i + 1))
        out[name] = w
        out["m_" + name] = s * _jax.random.normal(km, w.shape, _jnp.float32)
        out["v_" + name] = (s * s) * _jax.random.uniform(kv, w.shape, _jnp.float32, 0.5, 1.5)
    if N_MICROBATCH > 1:
        for name, axis in PER_EXAMPLE_BATCH_AXIS.items():
            out[name] = _to_microbatches(out[name], axis)
    return {'x': out['x'], 'positions': out['positions'], 'attn_norm': out['attn_norm'], 'w_in': out['w_in'], 'sgu_ln_g': out['sgu_ln_g'], 'sgu_ln_b': out['sgu_ln_b'], 'sgu_w': out['sgu_w'], 'sgu_b': out['sgu_b'], 'attn_sinks': out['attn_sinks'], 'dn_conv_w': out['dn_conv_w'], 'dn_a_log': out['dn_a_log'], 'dn_dt_bias': out['dn_dt_bias'], 'dn_norm': out['dn_norm'], 'w_branch': out['w_branch'], 'w_out': out['w_out'], 'ffn_norm': out['ffn_norm'], 'w_gate_up': out['w_gate_up'], 'w_down': out['w_down'], 'final_norm': out['final_norm'], 'loss_target': out['loss_target'], 'm_attn_norm': out['m_attn_norm'], 'm_w_in': out['m_w_in'], 'm_sgu_ln_g': out['m_sgu_ln_g'], 'm_sgu_ln_b': out['m_sgu_ln_b'], 'm_sgu_w': out['m_sgu_w'], 'm_sgu_b': out['m_sgu_b'], 'm_attn_sinks': out['m_attn_sinks'], 'm_dn_conv_w': out['m_dn_conv_w'], 'm_dn_a_log': out['m_dn_a_log'], 'm_dn_dt_bias': out['m_dn_dt_bias'], 'm_dn_norm': out['m_dn_norm'], 'm_w_branch': out['m_w_branch'], 'm_w_out': out['m_w_out'], 'm_ffn_norm': out['m_ffn_norm'], 'm_w_gate_up': out['m_w_gate_up'], 'm_w_down': out['m_w_down'], 'm_final_norm': out['m_final_norm'], 'v_attn_norm': out['v_attn_norm'], 'v_w_in': out['v_w_in'], 'v_sgu_ln_g': out['v_sgu_ln_g'], 'v_sgu_ln_b': out['v_sgu_ln_b'], 'v_sgu_w': out['v_sgu_w'], 'v_sgu_b': out['v_sgu_b'], 'v_attn_sinks': out['v_attn_sinks'], 'v_dn_conv_w': out['v_dn_conv_w'], 'v_dn_a_log': out['v_dn_a_log'], 'v_dn_dt_bias': out['v_dn_dt_bias'], 'v_dn_norm': out['v_dn_norm'], 'v_w_branch': out['v_w_branch'], 'v_w_out': out['v_w_out'], 'v_ffn_norm': out['v_ffn_norm'], 'v_w_gate_up': out['v_w_gate_up'], 'v_w_down': out['v_w_down'], 'v_final_norm': out['v_final_norm']}


def _loss(weights, diff, rest, loss_target):
    with _jax.named_scope("forward"):
        args = {**rest, TWIN_DIFF_INPUT: diff, **{k: w.astype(_WEIGHT_DTYPES[k]) for k, w in weights.items()}}
        y = _forward(args)
    with _jax.named_scope("loss_head"):
        err = _jnp.square(y.astype(_jnp.float32) - loss_target)
        return 0.5 * _jnp.sum(_jnp.mean(err, axis=-1)) if err.ndim else 0.5 * err


def _adamw(w, g, m, v):
    m = ADAM_B1 * m + (1.0 - ADAM_B1) * g
    v = ADAM_B2 * v + (1.0 - ADAM_B2) * _jnp.square(g)
    m_hat = m / (1.0 - ADAM_B1 ** ADAM_STEP)
    v_hat = v / (1.0 - ADAM_B2 ** ADAM_STEP)
    delta = -ADAM_LR * (m_hat / (_jnp.sqrt(v_hat) + ADAM_EPS) + ADAM_WD * w)
    return delta, m, v


def reference(x, positions, attn_norm, w_in, sgu_ln_g, sgu_ln_b, sgu_w, sgu_b, attn_sinks, dn_conv_w, dn_a_log, dn_dt_bias, dn_norm, w_branch, w_out, ffn_norm, w_gate_up, w_down, final_norm, loss_target, m_attn_norm, m_w_in, m_sgu_ln_g, m_sgu_ln_b, m_sgu_w, m_sgu_b, m_attn_sinks, m_dn_conv_w, m_dn_a_log, m_dn_dt_bias, m_dn_norm, m_w_branch, m_w_out, m_ffn_norm, m_w_gate_up, m_w_down, m_final_norm, v_attn_norm, v_w_in, v_sgu_ln_g, v_sgu_ln_b, v_sgu_w, v_sgu_b, v_attn_sinks, v_dn_conv_w, v_dn_a_log, v_dn_dt_bias, v_dn_norm, v_w_branch, v_w_out, v_ffn_norm, v_w_gate_up, v_w_down, v_final_norm):
    given = dict(x=x, positions=positions, attn_norm=attn_norm, w_in=w_in, sgu_ln_g=sgu_ln_g, sgu_ln_b=sgu_ln_b, sgu_w=sgu_w, sgu_b=sgu_b, attn_sinks=attn_sinks, dn_conv_w=dn_conv_w, dn_a_log=dn_a_log, dn_dt_bias=dn_dt_bias, dn_norm=dn_norm, w_branch=w_branch, w_out=w_out, ffn_norm=ffn_norm, w_gate_up=w_gate_up, w_down=w_down, final_norm=final_norm, loss_target=loss_target, m_attn_norm=m_attn_norm, m_w_in=m_w_in, m_sgu_ln_g=m_sgu_ln_g, m_sgu_ln_b=m_sgu_ln_b, m_sgu_w=m_sgu_w, m_sgu_b=m_sgu_b, m_attn_sinks=m_attn_sinks, m_dn_conv_w=m_dn_conv_w, m_dn_a_log=m_dn_a_log, m_dn_dt_bias=m_dn_dt_bias, m_dn_norm=m_dn_norm, m_w_branch=m_w_branch, m_w_out=m_w_out, m_ffn_norm=m_ffn_norm, m_w_gate_up=m_w_gate_up, m_w_down=m_w_down, m_final_norm=m_final_norm, v_attn_norm=v_attn_norm, v_w_in=v_w_in, v_sgu_ln_g=v_sgu_ln_g, v_sgu_ln_b=v_sgu_ln_b, v_sgu_w=v_sgu_w, v_sgu_b=v_sgu_b, v_attn_sinks=v_attn_sinks, v_dn_conv_w=v_dn_conv_w, v_dn_a_log=v_dn_a_log, v_dn_dt_bias=v_dn_dt_bias, v_dn_norm=v_dn_norm, v_w_branch=v_w_branch, v_w_out=v_w_out, v_ffn_norm=v_ffn_norm, v_w_gate_up=v_w_gate_up, v_w_down=v_w_down, v_final_norm=v_final_norm)
    weights = {n: given[n] for n in TWIN_WEIGHTS}
    shared = {n: given[n] for n in SHARED_INPUTS}
    per_example = {n: given[n] for n in ['x', 'positions']}
    grad_fn = _jax.value_and_grad(_loss, argnums=(0, 1))

    def one_microbatch(ex, loss_target):
        ex = dict(ex)
        diff = ex.pop(TWIN_DIFF_INPUT)
        return grad_fn(weights, diff, {**shared, **ex}, loss_target)

    if N_MICROBATCH == 1:
        loss, (grad_w, grad_x) = one_microbatch(per_example, given["loss_target"])
    else:
        def body(carry, xs):
            loss_sum, grad_sum = carry
            l_k, (gw_k, gx_k) = one_microbatch(xs[0], xs[1])
            with _jax.named_scope("update"):
                return (loss_sum + l_k, _jax.tree.map(_jnp.add, grad_sum, gw_k)), gx_k

        init = (_jnp.zeros((), _jnp.float32), _jax.tree.map(_jnp.zeros_like, weights))
        (loss, grad_w), grad_x = _jax.lax.scan(body, init, (per_example, given["loss_target"]))
    with _jax.named_scope("update"):
        delta_w, new_m, new_v = {}, {}, {}
        for n in TWIN_WEIGHTS:
            delta_w[n], new_m[n], new_v[n] = _adamw(weights[n], grad_w[n], given["m_" + n], given["v_" + n])
    return (loss, grad_x, *[grad_w[n] for n in TWIN_WEIGHTS], *[delta_w[n] for n in TWIN_WEIGHTS],
            *[new_m[n] for n in TWIN_WEIGHTS], *[new_v[n] for n in TWIN_WEIGHTS])
```

```python
import functools
import math

import jax
import jax.numpy as jnp
from jax import lax
from jax.experimental import pallas as pl
from jax.experimental.pallas import tpu as pltpu

F32 = jnp.float32
BF16 = jnp.bfloat16
HI = lax.Precision.HIGHEST

D_MODEL = 1024
DEPTH = 2
MIX = 512
EPS = 1e-6
SGU_G, SGU_T = 4, 128
SWA_H, SWA_KV, SWA_HD, WINDOW = 8, 2, 64, 128
ROPE_THETA, ROPE_DIM = 500000.0, 16
DN_H, DN_HD, DN_CONV, DN_C = 4, 128, 4, 64
D_FF = 2816
IN_COLS = 6920
IN_PIECES = ((3848, 3072), (1792, 1536), (0, 512), (512, 512), (1024, 512), (3328, 512), (1536, 128), (1664, 128),
             (3840, 8))
IN_PAD = 120
IN_R = 7040
C_GATE, C_QKV, C_UA, C_VA, C_QB, C_ZC, C_KB, C_VB, C_SM = 0, 3072, 4608, 5120, 5632, 6144, 6656, 6784, 6912

ADAM_LR, ADAM_B1, ADAM_B2, ADAM_EPS, ADAM_WD, ADAM_STEP = 0.001, 0.9, 0.999, 1e-08, 0.01, 10
VMEM_LIMIT = 56 * 1024 * 1024


def _cparams(sem):
    return pltpu.CompilerParams(dimension_semantics=sem, vmem_limit_bytes=VMEM_LIMIT)


def _dg(a, b, ca, cb, prec=None):
    return lax.dot_general(a, b, (((ca,), (cb,)), ((), ())), precision=prec, preferred_element_type=F32)


def _mm(a, b):
    return _dg(a.astype(BF16), b.astype(BF16), 1, 0)


def _mm_nt(a, b):
    return _dg(a.astype(BF16), b.astype(BF16), 1, 1)


def _mm_tn(a, b):
    return _dg(a.astype(BF16), b.astype(BF16), 0, 0)


def _sigmoid(x):
    return 1.0 / (1.0 + jnp.exp(-x))


def _silu(x):
    return x * _sigmoid(x)


def _dsilu(x):
    s = _sigmoid(x)
    return s * (1.0 + x * (1.0 - s))


_GC = math.sqrt(2.0 / math.pi)


def _gelu(x):
    return 0.5 * x * (1.0 + jnp.tanh(_GC * (x + 0.044715 * x * x * x)))


def _dgelu(x):
    t = jnp.tanh(_GC * (x + 0.044715 * x * x * x))
    return 0.5 * (1.0 + t) + 0.5 * x * (1.0 - t * t) * _GC * (1.0 + 3.0 * 0.044715 * x * x)


def _softplus(x):
    return jnp.maximum(x, 0.0) + jnp.log(1.0 + jnp.exp(-jnp.abs(x)))


def _acc(ref, val, i):
    @pl.when(i == 0)
    def _():
        ref[...] = val

    @pl.when(i > 0)
    def _():
        ref[...] += val


def _tok_call(body, name, S, TB, tok_in, const_in=(), tok_out=(), acc_out=(), prev_in=(), next_in=(), smem_in=()):
    nb = S // TB
    r8 = TB // 8
    in_specs, args = [], []
    for a, w, cb in tok_in:
        in_specs.append(pl.BlockSpec((TB, w), functools.partial(lambda i, cb: (i, cb), cb=cb)))
        args.append(a)
    for a, w, cb in prev_in:
        in_specs.append(pl.BlockSpec((8, w), functools.partial(lambda i, cb: (jnp.maximum(i * r8 - 1, 0), cb), cb=cb)))
        args.append(a)
    for a, w, cb in next_in:
        in_specs.append(pl.BlockSpec((8, w), functools.partial(
            lambda i, cb: (jnp.minimum((i + 1) * r8, S // 8 - 1), cb), cb=cb)))
        args.append(a)
    for a in const_in:
        in_specs.append(pl.BlockSpec(a.shape, lambda i: (0, 0)))
        args.append(a)
    for a in smem_in:
        in_specs.append(pl.BlockSpec(memory_space=pltpu.SMEM))
        args.append(a)
    out_specs, out_shape = [], []
    for w, dt in tok_out:
        out_specs.append(pl.BlockSpec((TB, w), lambda i: (i, 0)))
        out_shape.append(jax.ShapeDtypeStruct((S, w), dt))
    for shp, dt in acc_out:
        out_specs.append(pl.BlockSpec(shp, lambda i: (0, 0)))
        out_shape.append(jax.ShapeDtypeStruct(shp, dt))

    def kern(*refs):
        body(pl.program_id(0), *refs)

    return pl.pallas_call(
        kern, name=name, grid=(nb,), in_specs=in_specs, out_specs=out_specs, out_shape=out_shape,
        compiler_params=_cparams(("arbitrary",)),
    )(*args)


MM_BLOCKS = (1024, 1408, 640, 512, 256, 128)


def _pick(n, cands):
    for c in cands:
        if n % c == 0:
            return c
    return n


def _matmul(a, b, *, ta=False, tb=False, add=None, out_dtype=F32, name):
    M, K = (a.shape[1], a.shape[0]) if ta else a.shape
    N = b.shape[0] if tb else b.shape[1]
    bm, bn, bk = _pick(M, MM_BLOCKS), _pick(N, MM_BLOCKS), _pick(K, MM_BLOCKS)
    nk = K // bk
    a_spec = pl.BlockSpec((bk, bm), lambda i, j, k: (k, i)) if ta else pl.BlockSpec((bm, bk), lambda i, j, k: (i, k))
    b_spec = pl.BlockSpec((bn, bk), lambda i, j, k: (j, k)) if tb else pl.BlockSpec((bk, bn), lambda i, j, k: (k, j))
    o_spec = pl.BlockSpec((bm, bn), lambda i, j, k: (i, j))
    ca, cb = (0 if ta else 1), (1 if tb else 0)

    def kern(*refs):
        if add is None:
            a_ref, b_ref, o_ref, acc_ref = refs
        else:
            a_ref, b_ref, add_ref, o_ref, acc_ref = refs
        k = pl.program_id(2)
        p = _dg(a_ref[...].astype(BF16), b_ref[...].astype(BF16), ca, cb)

        @pl.when(k == 0)
        def _():
            acc_ref[...] = p

        @pl.when(k > 0)
        def _():
            acc_ref[...] += p

        @pl.when(k == nk - 1)
        def _():
            r = acc_ref[...]
            if add is not None:
                r = r + add_ref[...].astype(F32)
            o_ref[...] = r.astype(out_dtype)

    in_specs = [a_spec, b_spec] + ([o_spec] if add is not None else [])
    args = (a, b) + ((add,) if add is not None else ())
    return pl.pallas_call(
        kern, name=name, grid=(M // bm, N // bn, nk), in_specs=in_specs, out_specs=o_spec,
        out_shape=jax.ShapeDtypeStruct((M, N), out_dtype), scratch_shapes=[pltpu.VMEM((bm, bn), F32)],
        compiler_params=_cparams(("parallel", "parallel", "arbitrary")),
    )(*args)


def _rms_fwd(x, g, name):
    S = x.shape[0]

    def body(i, x_ref, g_ref, h_ref):
        xv = x_ref[...]
        r = lax.rsqrt(jnp.mean(xv * xv, axis=-1, keepdims=True) + EPS)
        h_ref[...] = (xv * r * g_ref[...]).astype(BF16)

    return _tok_call(body, name, S, min(S, 512), [(x, D_MODEL, 0)], [g], [(D_MODEL, BF16)])[0]


def _rms_bwd_vals(xv, g, dh):
    r = lax.rsqrt(jnp.mean(xv * xv, axis=-1, keepdims=True) + EPS)
    u = dh * g
    dx = r * u - xv * (r * r * r) * jnp.mean(u * xv, axis=-1, keepdims=True)
    dg = jnp.sum(dh * xv * r, axis=0, keepdims=True)
    return dx, dg


def _rms_bwd_add(x, g, dh, dres, name):
    S = x.shape[0]

    def body(i, x_ref, dh_ref, dr_ref, g_ref, dx_ref, dg_ref):
        dx, dg = _rms_bwd_vals(x_ref[...], g_ref[...], dh_ref[...].astype(F32))
        dx_ref[...] = dr_ref[...] + dx
        _acc(dg_ref, dg, i)

    return _tok_call(body, name, S, min(S, 512), [(x, D_MODEL, 0), (dh, D_MODEL, 0), (dres, D_MODEL, 0)], [g],
                     [(D_MODEL, F32)], [((1, D_MODEL), F32)])


def _final_loss(x, g, target):
    S = x.shape[0]

    def body(i, x_ref, t_ref, g_ref, dx_ref, loss_ref, dg_ref):
        xv, gv = x_ref[...], g_ref[...]
        r = lax.rsqrt(jnp.mean(xv * xv, axis=-1, keepdims=True) + EPS)
        e = xv * r * gv - t_ref[...]
        part = 0.5 * jnp.sum(jnp.mean(e * e, axis=-1, keepdims=True), axis=0, keepdims=True)
        dx, dg = _rms_bwd_vals(xv, gv, e * (1.0 / D_MODEL))
        dx_ref[...] = dx
        _acc(loss_ref, jnp.broadcast_to(part, (1, 128)), i)
        _acc(dg_ref, dg, i)

    return _tok_call(body, "final_loss", S, min(S, 512), [(x, D_MODEL, 0), (target, D_MODEL, 0)], [g],
                     [(D_MODEL, F32)], [((1, 128), F32), ((1, D_MODEL), F32)])


def _swiglu_fwd(gu, name):
    S = gu.shape[0]

    def body(i, gu_ref, a_ref):
        a_ref[...] = (_silu(gu_ref[:, :D_FF]) * gu_ref[:, D_FF:]).astype(BF16)

    return _tok_call(body, name, S, min(S, 256), [(gu, 2 * D_FF, 0)], [], [(D_FF, BF16)])[0]


def _swiglu_bwd(gu, dact, name):
    S = gu.shape[0]

    def body(i, gu_ref, da_ref, dgu_ref):
        gg, uu, da = gu_ref[:, :D_FF], gu_ref[:, D_FF:], da_ref[...]
        dgu_ref[:, :D_FF] = (da * uu * _dsilu(gg)).astype(BF16)
        dgu_ref[:, D_FF:] = (da * _silu(gg)).astype(BF16)

    return _tok_call(body, name, S, min(S, 256), [(gu, 2 * D_FF, 0), (dact, D_FF, 0)], [], [(2 * D_FF, BF16)])[0]


def _merge_fwd(proj, bds, name):
    S = proj.shape[0]

    def body(i, g0, g1, g2, b0, b1, b2, m_ref):
        m = _sigmoid(g0[...]) * b0[...] + _sigmoid(g1[...]) * b1[...] + _sigmoid(g2[...]) * b2[...]
        m_ref[...] = m.astype(BF16)

    tok = [(proj, D_MODEL, n) for n in range(3)] + [(b, D_MODEL, 0) for b in bds]
    return _tok_call(body, name, S, min(S, 512), tok, [], [(D_MODEL, BF16)])[0]


def _merge_bwd(proj, bds, dm, name):
    S = proj.shape[0]

    def body(i, g0, g1, g2, b0, b1, b2, dm_ref, d0, d1, d2, dgp_ref):
        dmv = dm_ref[...]
        for n, (gr, br, dr) in enumerate(((g0, b0, d0), (g1, b1, d1), (g2, b2, d2))):
            s = _sigmoid(gr[...])
            dr[...] = (dmv * s).astype(BF16)
            dgp_ref[:, n * D_MODEL:(n + 1) * D_MODEL] = (dmv * br[...] * s * (1.0 - s)).astype(BF16)

    tok = [(proj, D_MODEL, n) for n in range(3)] + [(b, D_MODEL, 0) for b in bds] + [(dm, D_MODEL, 0)]
    return _tok_call(body, name, S, min(S, 512), tok, [],
                     [(D_MODEL, BF16)] * 3 + [(3 * D_MODEL, BF16)])


def _sgu_ln(v, lg, lb):
    mu = jnp.mean(v, axis=-1, keepdims=True)
    vc = v - mu
    rstd = lax.rsqrt(jnp.mean(vc * vc, axis=-1, keepdims=True) + EPS)
    vhat = vc * rstd
    return vhat, rstd, vhat * lg + lb


def _sgu_fwd(proj, lg, lb, wc, bst, name):
    S = proj.shape[0]

    def body(i, ua_ref, va_ref, lg_ref, lb_ref, wc_ref, bs_ref, o_ref):
        u = _gelu(ua_ref[...])
        _, _, vn = _sgu_ln(_gelu(va_ref[...]), lg_ref[...], lb_ref[...])
        for g in range(SGU_G):
            sl = slice(g * 128, (g + 1) * 128)
            mixed = _mm(wc_ref[sl, :], vn[:, sl]) + bs_ref[:, g:g + 1]
            o_ref[:, sl] = (u[:, sl] * mixed).astype(BF16)

    return _tok_call(body, name, S, SGU_T, [(proj, MIX, C_UA // MIX), (proj, MIX, C_VA // MIX)], [lg, lb, wc, bst],
                     [(MIX, BF16)])[0]


def _sgu_bwd(proj, lg, lb, wc, bst, dout, name):
    S = proj.shape[0]

    def body(i, ua_ref, va_ref, do_ref, lg_ref, lb_ref, wc_ref, bs_ref, dua_ref, dva_ref, dlg_ref, dlb_ref, dwc_ref,
             dbs_ref):
        ua, va, do = ua_ref[...], va_ref[...], do_ref[...].astype(F32)
        u = _gelu(ua)
        lgv = lg_ref[...]
        vhat, rstd, vn = _sgu_ln(_gelu(va), lgv, lb_ref[...])
        tril = lax.broadcasted_iota(jnp.int32, (128, 128), 0) >= lax.broadcasted_iota(jnp.int32, (128, 128), 1)
        lane4 = lax.broadcasted_iota(jnp.int32, (128, 4), 1)
        dvn_parts, dbs = [], jnp.zeros((128, 4), F32)
        for g in range(SGU_G):
            sl = slice(g * 128, (g + 1) * 128)
            wg = wc_ref[sl, :]
            mixed = _mm(wg, vn[:, sl]) + bs_ref[:, g:g + 1]
            dua_ref[:, sl] = (do[:, sl] * mixed * _dgelu(ua[:, sl])).astype(BF16)
            dmix = do[:, sl] * u[:, sl]
            dbs = dbs + jnp.where(lane4 == g, jnp.sum(dmix, axis=-1, keepdims=True), 0.0)
            dwg = jnp.where(tril, _mm_nt(dmix, vn[:, sl]), 0.0)
            _acc(dwc_ref.at[sl, :], dwg, i)
            dvn_parts.append(_mm_tn(wg, dmix))
        dvn = jnp.concatenate(dvn_parts, axis=1)
        _acc(dbs_ref, dbs, i)
        _acc(dlg_ref, jnp.sum(dvn * vhat, axis=0, keepdims=True), i)
        _acc(dlb_ref, jnp.sum(dvn, axis=0, keepdims=True), i)
        dvh = dvn * lgv
        dv = rstd * (dvh - jnp.mean(dvh, axis=-1, keepdims=True) - vhat * jnp.mean(dvh * vhat, axis=-1, keepdims=True))
        dva_ref[...] = (dv * _dgelu(va)).astype(BF16)

    return _tok_call(body, name, S, SGU_T, [(proj, MIX, C_UA // MIX), (proj, MIX, C_VA // MIX), (dout, MIX, 0)],
                     [lg, lb, wc, bst], [(MIX, BF16), (MIX, BF16)],
                     [((1, MIX), F32), ((1, MIX), F32), ((SGU_G * 128, 128), F32), ((128, 4), F32)])


def _rope_tables(positions):
    S = positions.shape[0]
    inv_freq = ROPE_THETA ** (-jnp.arange(0, ROPE_DIM, 2, dtype=F32) / ROPE_DIM)
    ang = positions.astype(F32)[:, None] * inv_freq
    c, s = jnp.cos(ang), jnp.sin(ang)
    c64 = jnp.concatenate([c, c, jnp.ones((S, SWA_HD - ROPE_DIM), F32)], axis=1)
    s64 = jnp.concatenate([-s, s, jnp.zeros((S, SWA_HD - ROPE_DIM), F32)], axis=1)
    return jnp.tile(c64, (1, 2)), jnp.tile(s64, (1, 2))


def _rope128(x, c, s):
    lane = lax.broadcasted_iota(jnp.int32, x.shape, 1) % SWA_HD
    swapped = jnp.where(lane < ROPE_DIM // 2, pltpu.roll(x, 128 - ROPE_DIM // 2, 1), pltpu.roll(x, ROPE_DIM // 2, 1))
    return x * c + swapped * s


def _rope_t128(y, c, s):
    ys = y * s
    lane = lax.broadcasted_iota(jnp.int32, y.shape, 1) % SWA_HD
    swapped = jnp.where(lane < ROPE_DIM // 2, pltpu.roll(ys, 128 - ROPE_DIM // 2, 1), pltpu.roll(ys, ROPE_DIM // 2, 1))
    return y * c + jnp.where(lane < ROPE_DIM, swapped, 0.0)


def _rope_fwd(proj, cos, sin, name):
    S = proj.shape[0]
    scale = SWA_HD ** -0.5

    def body(i, q_ref, k_ref, v_ref, c_ref, s_ref, qo_ref, ko_ref, vo_ref):
        c, s = c_ref[...], s_ref[...]
        for j in range(4):
            sl = slice(j * 128, (j + 1) * 128)
            qo_ref[:, sl] = (_rope128(q_ref[:, sl], c, s) * scale).astype(BF16)
        ko_ref[...] = _rope128(k_ref[...], c, s).astype(BF16)
        vo_ref[...] = v_ref[...].astype(BF16)

    return _tok_call(body, name, S, min(S, 512),
                     [(proj, MIX, C_QB // MIX), (proj, 128, C_KB // 128), (proj, 128, C_VB // 128), (cos, 128, 0),
                      (sin, 128, 0)], [], [(MIX, BF16), (128, BF16), (128, BF16)])


def _rope_bwd(dq, dk, dv, cos, sin, name):
    S = dq.shape[0]
    scale = SWA_HD ** -0.5

    def body(i, dq_ref, dk_ref, dv_ref, c_ref, s_ref, qo_ref, ko_ref, vo_ref):
        c, s = c_ref[...], s_ref[...]
        for j in range(4):
            sl = slice(j * 128, (j + 1) * 128)
            qo_ref[:, sl] = _rope_t128(dq_ref[:, sl] * scale, c, s).astype(BF16)
        ko_ref[...] = _rope_t128(dk_ref[...], c, s).astype(BF16)
        vo_ref[...] = dv_ref[...].astype(BF16)

    return _tok_call(body, name, S, min(S, 512),
                     [(dq, MIX, 0), (dk, 128, 0), (dv, 128, 0), (cos, 128, 0), (sin, 128, 0)], [],
                     [(MIX, BF16), (128, BF16), (128, BF16)])


def _swa_band(i, k_ref, v_ref):
    pstart = pl.multiple_of(jnp.maximum(i - 1, 0) * WINDOW, WINDOW)
    cstart = pl.multiple_of(i * WINDOW, WINDOW)
    kb = jnp.concatenate([k_ref[pl.ds(pstart, WINDOW), :], k_ref[pl.ds(cstart, WINDOW), :]], axis=0)
    vb = jnp.concatenate([v_ref[pl.ds(pstart, WINDOW), :], v_ref[pl.ds(cstart, WINDOW), :]], axis=0)
    qi = lax.broadcasted_iota(jnp.int32, (WINDOW, 2 * WINDOW), 0)
    sj = lax.broadcasted_iota(jnp.int32, (WINDOW, 2 * WINDOW), 1)
    mask = (sj > qi) & (sj <= qi + WINDOW) & ((i > 0) | (sj >= WINDOW))
    return kb, vb, mask, pstart, cstart


def _swa_probs(qh, kh, mask, sink):
    logits = jnp.where(mask, _dg(qh, kh, 1, 1), -1e30)
    m = jnp.maximum(jnp.max(logits, axis=-1, keepdims=True), sink)
    p = jnp.exp(logits - m)
    ps = jnp.exp(sink - m)
    inv = 1.0 / (jnp.sum(p, axis=-1, keepdims=True) + ps)
    return p * inv, ps * inv


def _swa_fwd(q, k, v, sinks, name):
    S = q.shape[0]

    def body(i, q_ref, k_ref, v_ref, s_ref, o_ref):
        kb, vb, mask, _, _ = _swa_band(i, k_ref, v_ref)
        qv = q_ref[...]
        for h in range(SWA_H):
            kv = h // (SWA_H // SWA_KV)
            ksl = slice(kv * SWA_HD, (kv + 1) * SWA_HD)
            hsl = slice(h * SWA_HD, (h + 1) * SWA_HD)
            pn, _ = _swa_probs(qv[:, hsl], kb[:, ksl], mask, s_ref[0, h])
            o_ref[:, hsl] = _dg(pn.astype(BF16), vb[:, ksl], 1, 0).astype(BF16)

    return _tok_call(body, name, S, WINDOW, [(q, MIX, 0)], [k, v], [(MIX, BF16)], smem_in=[sinks])[0]


def _swa_bwd(q, k, v, sinks, dout, name):
    S = q.shape[0]

    def body(i, q_ref, do_ref, k_ref, v_ref, s_ref, dq_ref, dk_ref, dv_ref, ds_ref):
        kb, vb, mask, pstart, cstart = _swa_band(i, k_ref, v_ref)
        qv, dov = q_ref[...], do_ref[...]
        lane = lax.broadcasted_iota(jnp.int32, (1, 128), 1)
        dsink = jnp.zeros((1, 128), F32)
        dkb, dvb = [], []
        for kv in range(SWA_KV):
            ksl = slice(kv * SWA_HD, (kv + 1) * SWA_HD)
            dk_kv = jnp.zeros((2 * WINDOW, SWA_HD), F32)
            dv_kv = jnp.zeros((2 * WINDOW, SWA_HD), F32)
            for gq in range(SWA_H // SWA_KV):
                h = kv * (SWA_H // SWA_KV) + gq
                hsl = slice(h * SWA_HD, (h + 1) * SWA_HD)
                qh, doh = qv[:, hsl], dov[:, hsl].astype(BF16)
                pn, psn = _swa_probs(qh, kb[:, ksl], mask, s_ref[0, h])
                dp = _dg(doh, vb[:, ksl], 1, 1)
                delta = jnp.sum(dp * pn, axis=-1, keepdims=True)
                dsc = (pn * (dp - delta)).astype(BF16)
                dq_ref[:, hsl] = _dg(dsc, kb[:, ksl], 1, 0)
                dk_kv = dk_kv + _dg(dsc, qh, 0, 0)
                dv_kv = dv_kv + _dg(pn.astype(BF16), doh, 0, 0)
                dsink = dsink + jnp.where(lane == h, -jnp.sum(psn * delta, axis=0, keepdims=True), 0.0)
            dkb.append(dk_kv)
            dvb.append(dv_kv)
        dkb = jnp.concatenate(dkb, axis=1)
        dvb = jnp.concatenate(dvb, axis=1)

        @pl.when(i == 0)
        def _():
            dk_ref[...] = jnp.zeros_like(dk_ref)
            dv_ref[...] = jnp.zeros_like(dv_ref)

        dk_ref[pl.ds(pstart, WINDOW), :] += dkb[:WINDOW]
        dv_ref[pl.ds(pstart, WINDOW), :] += dvb[:WINDOW]
        dk_ref[pl.ds(cstart, WINDOW), :] += dkb[WINDOW:]
        dv_ref[pl.ds(cstart, WINDOW), :] += dvb[WINDOW:]
        _acc(ds_ref, dsink, i)

    return _tok_call(body, name, S, WINDOW, [(q, MIX, 0), (dout, MIX, 0)], [k, v], [(MIX, F32)],
                     [((S, 128), F32), ((S, 128), F32), ((1, 128), F32)], smem_in=[sinks])


def _shift_rows(xs, k):
    return xs if k == 0 else pltpu.roll(xs, k, 0)


def _dn_conv(x_ref, p_ref, w_ref, i):
    halo = jnp.where(i > 0, p_ref[...], 0.0)
    xs = jnp.concatenate([halo, x_ref[...]], axis=0)
    sh = [_shift_rows(xs, DN_CONV - 1 - t)[8:] for t in range(DN_CONV)]
    pre = sh[0] * w_ref[0:1, :]
    for t in range(1, DN_CONV):
        pre = pre + sh[t] * w_ref[t:t + 1, :]
    return pre, sh


def _dn_gates(sm, alog, dtb):
    lane = lax.broadcasted_iota(jnp.int32, sm.shape, 1)
    return jnp.where(lane < DN_H, _sigmoid(sm), -jnp.exp(alog) * _softplus(sm + dtb))


def _dn_pre_fwd(proj, conv_w, alog_l, dtb_l, name):
    S = proj.shape[0]
    scale = DN_HD ** -0.5

    def body(i, x_ref, sm_ref, p_ref, w_ref, al_ref, db_ref, q_ref, k_ref, v_ref, bg_ref):
        pre, _ = _dn_conv(x_ref, p_ref, w_ref, i)
        a = _silu(pre)
        for h in range(DN_H):
            sl = slice(h * DN_HD, (h + 1) * DN_HD)
            qh, kh = a[:, sl], a[:, MIX + h * DN_HD:MIX + (h + 1) * DN_HD]
            q_ref[:, sl] = qh * (lax.rsqrt(jnp.sum(qh * qh, axis=-1, keepdims=True) + EPS) * scale)
            k_ref[:, sl] = kh * lax.rsqrt(jnp.sum(kh * kh, axis=-1, keepdims=True) + EPS)
        v_ref[...] = a[:, 2 * MIX:]
        bg_ref[...] = _dn_gates(sm_ref[...], al_ref[...], db_ref[...])

    TB = min(S, 256)
    return _tok_call(body, name, S, TB, [(proj, 3 * MIX, C_QKV // (3 * MIX)), (proj, 128, C_SM // 128)],
                     [conv_w, alog_l, dtb_l], [(MIX, F32), (MIX, F32), (MIX, F32), (128, F32)],
                     prev_in=[(proj, 3 * MIX, C_QKV // (3 * MIX))])


def _dn_pre_bwd1(proj, conv_w, alog_l, dtb_l, dq, dk, dv, dbg, name):
    S = proj.shape[0]
    scale = DN_HD ** -0.5

    def body(i, x_ref, sm_ref, dq_ref, dk_ref, dv_ref, dbg_ref, p_ref, w_ref, al_ref, db_ref, dpre_ref, dsm_ref,
             dw_ref, dal_ref, ddb_ref):
        pre, sh = _dn_conv(x_ref, p_ref, w_ref, i)
        a = _silu(pre)
        da_parts = []
        for part, (g_ref, sc) in enumerate(((dq_ref, scale), (dk_ref, 1.0))):
            for h in range(DN_H):
                xh = a[:, part * MIX + h * DN_HD:part * MIX + (h + 1) * DN_HD]
                rs = lax.rsqrt(jnp.sum(xh * xh, axis=-1, keepdims=True) + EPS)
                y = xh * rs
                dy = g_ref[:, h * DN_HD:(h + 1) * DN_HD] * sc
                da_parts.append(rs * (dy - y * jnp.sum(dy * y, axis=-1, keepdims=True)))
        da_parts.append(dv_ref[...])
        dpre = jnp.concatenate(da_parts, axis=1) * _dsilu(pre)
        dpre_ref[...] = dpre
        dw = jnp.concatenate([jnp.sum(dpre * sh[t], axis=0, keepdims=True) for t in range(DN_CONV)], axis=0)
        _acc(dw_ref, dw, i)
        sm, al, db, dbg_v = sm_ref[...], al_ref[...], db_ref[...], dbg_ref[...]
        lane = lax.broadcasted_iota(jnp.int32, sm.shape, 1)
        sg = _sigmoid(sm)
        gneg = -jnp.exp(al)
        is_g = (lane >= DN_H) & (lane < 2 * DN_H)
        d_al = jnp.where(is_g, dbg_v * gneg * _sigmoid(sm + db), 0.0)
        dsm_ref[...] = jnp.where(lane < DN_H, dbg_v * sg * (1.0 - sg), d_al).astype(BF16)
        _acc(ddb_ref, jnp.sum(d_al, axis=0, keepdims=True), i)
        _acc(dal_ref, jnp.sum(jnp.where(is_g, dbg_v * gneg * _softplus(sm + db), 0.0), axis=0, keepdims=True), i)

    TB = min(S, 256)
    return _tok_call(body, name, S, TB,
                     [(proj, 3 * MIX, C_QKV // (3 * MIX)), (proj, 128, C_SM // 128), (dq, MIX, 0), (dk, MIX, 0),
                      (dv, MIX, 0), (dbg, 128, 0)], [conv_w, alog_l, dtb_l],
                     [(3 * MIX, F32), (128, BF16)], [((DN_CONV, 3 * MIX), F32), ((1, 128), F32), ((1, 128), F32)],
                     prev_in=[(proj, 3 * MIX, C_QKV // (3 * MIX))])


def _dn_pre_bwd2(dpre, conv_w, name):
    S = dpre.shape[0]
    TB = min(S, 256)
    nb = S // TB

    def body(i, d_ref, n_ref, w_ref, o_ref):
        halo = jnp.where(i < nb - 1, n_ref[...], 0.0)
        ds = jnp.concatenate([d_ref[...], halo], axis=0)
        out = ds[:TB] * w_ref[DN_CONV - 1:DN_CONV, :]
        for t in range(DN_CONV - 1):
            k = DN_CONV - 1 - t
            out = out + pltpu.roll(ds, TB + 8 - k, 0)[:TB] * w_ref[t:t + 1, :]
        o_ref[...] = out.astype(BF16)

    return _tok_call(body, name, S, TB, [(dpre, 3 * MIX, 0)], [conv_w], [(3 * MIX, BF16)],
                     next_in=[(dpre, 3 * MIX, 0)])[0]


def _dn_chunk_terms(bg, h):
    C = DN_C
    beta = bg[:, h:h + 1]
    gb = jnp.broadcast_to(bg[:, DN_H + h:DN_H + h + 1], (C, C))
    ri = lax.broadcasted_iota(jnp.int32, (C, C), 0)
    ci = lax.broadcasted_iota(jnp.int32, (C, C), 1)
    tril, eye = ri >= ci, ri == ci
    gc_col = _dg(tril.astype(F32), gb, 1, 0, HI)
    gc_row = jnp.sum(jnp.where(eye, gc_col, 0.0), axis=0, keepdims=True)
    decay = jnp.exp(jnp.where(tril, gc_col - gc_row, -1e30))
    gci = gc_col[:, 0:1]
    gl = gc_col[C - 1:C, 0:1]
    return beta, decay, jnp.exp(gci), jnp.exp(gl - gci), jnp.exp(gl), tril, eye, ri, ci


def _dn_core_fwd(q, k, v, bg, name):
    S = q.shape[0]
    C = DN_C
    nt = S // C

    def kern(q_ref, k_ref, v_ref, bg_ref, o_ref, t_ref, uw_ref, vn_ref, st_ref, state):
        i = pl.program_id(0)

        @pl.when(i == 0)
        def _():
            state[...] = jnp.zeros_like(state)

        bg_v = bg_ref[...]
        for h in range(DN_H):
            sl = slice(h * DN_HD, (h + 1) * DN_HD)
            qh, kh, vh = q_ref[:, sl], k_ref[:, sl], v_ref[:, sl]
            beta, decay, e_gc, e_kd, cdec, tril, eye, ri, ci = _dn_chunk_terms(bg_v, h)
            kb = kh * beta
            lower = jnp.where(ri > ci, _mm_nt(kb, kh) * decay, 0.0)
            x = -lower
            tm = jnp.where(eye, 1.0, 0.0) + x
            p = x
            for _ in range(5):
                p = _dg(p, p, 1, 0, HI)
                tm = tm + _dg(tm, p, 1, 0, HI)
            rhs = jnp.concatenate([vh * beta, kb * e_gc], axis=1)
            sol = _dg(tm, rhs, 1, 0, HI)
            u, w = sol[:, :DN_HD], sol[:, DN_HD:]
            attn = _mm_nt(qh, kh) * decay
            s_in = state[sl, :]
            vnew = u - _mm(w, s_in)
            o_ref[:, sl] = _mm(qh * e_gc, s_in) + _mm(attn, vnew)
            state[sl, :] = s_in * cdec + _mm_tn(kh * e_kd, vnew)
            st_ref[sl, :] = s_in
            t_ref[:, h * C:(h + 1) * C] = tm
            uw_ref[:, sl] = u
            uw_ref[:, MIX + h * DN_HD:MIX + (h + 1) * DN_HD] = w
            vn_ref[:, sl] = vnew

    tok = lambda w: pl.BlockSpec((C, w), lambda i: (i, 0))
    return pl.pallas_call(
        kern, name=name, grid=(nt,), in_specs=[tok(MIX), tok(MIX), tok(MIX), tok(128)],
        out_specs=[tok(MIX), tok(DN_H * C), tok(2 * MIX), tok(MIX), pl.BlockSpec((DN_H * DN_HD, DN_HD), lambda i: (i, 0))],
        out_shape=[jax.ShapeDtypeStruct((S, MIX), F32), jax.ShapeDtypeStruct((S, DN_H * C), F32),
                   jax.ShapeDtypeStruct((S, 2 * MIX), F32), jax.ShapeDtypeStruct((S, MIX), F32),
                   jax.ShapeDtypeStruct((nt * DN_H * DN_HD, DN_HD), F32)],
        scratch_shapes=[pltpu.VMEM((DN_H * DN_HD, DN_HD), F32)],
        compiler_params=_cparams(("arbitrary",)),
    )(q, k, v, bg)


def _dn_core_bwd(q, k, v, bg, tm, uw, vn, st, do, name):
    S = q.shape[0]
    C = DN_C
    nt = S // C

    def kern(q_ref, k_ref, v_ref, bg_ref, t_ref, uw_ref, vn_ref, st_ref, do_ref, dq_ref, dk_ref, dv_ref, dbg_ref,
             dstate):
        i = pl.program_id(0)

        @pl.when(i == 0)
        def _():
            dstate[...] = jnp.zeros_like(dstate)

        bg_v = bg_ref[...]
        lane = lax.broadcasted_iota(jnp.int32, (C, 128), 1)
        dbg = jnp.zeros((C, 128), F32)
        for h in range(DN_H):
            sl = slice(h * DN_HD, (h + 1) * DN_HD)
            qh, kh, vh, doh = q_ref[:, sl], k_ref[:, sl], v_ref[:, sl], do_ref[:, sl]
            beta, decay, e_gc, e_kd, cdec, tril, eye, ri, ci = _dn_chunk_terms(bg_v, h)
            th = t_ref[:, h * C:(h + 1) * C]
            u, w = uw_ref[:, sl], uw_ref[:, MIX + h * DN_HD:MIX + (h + 1) * DN_HD]
            vnew, s_in, ds_o = vn_ref[:, sl], st_ref[sl, :], dstate[sl, :]
            kb = kh * beta
            kk = _mm_nt(kb, kh)
            attn = _mm_nt(qh, kh) * decay
            qd, kd = qh * e_gc, kh * e_kd
            d_vnew = _mm_tn(attn, doh) + _mm(kd, ds_o)
            d_qd = _mm_nt(doh, s_in)
            d_attn = _mm_nt(doh, vnew)
            d_kd = _mm_nt(vnew, ds_o)
            d_c = jnp.sum(jnp.sum(ds_o * s_in, axis=1, keepdims=True), axis=0, keepdims=True)
            d_w = -_mm_nt(d_vnew, s_in)
            dstate[sl, :] = ds_o * cdec + _mm_tn(qd, doh) - _mm_tn(w, d_vnew)
            d_rhs = _dg(th, jnp.concatenate([d_vnew, d_w], axis=1), 0, 0, HI)
            d_a = -_dg(d_rhs, jnp.concatenate([u, w], axis=1), 1, 1, HI)
            d_lower = jnp.where(ri > ci, d_a, 0.0)
            d_vb, dz = d_rhs[:, :DN_HD], d_rhs[:, DN_HD:]
            dv_ref[:, sl] = d_vb * beta
            d_beta = jnp.sum(d_vb * vh, axis=-1, keepdims=True)
            d_kb = dz * e_gc
            d_gc = jnp.sum(dz * kb, axis=-1, keepdims=True) * e_gc
            d_kk = d_lower * decay
            d_qk = d_attn * decay
            dm = d_kk * kk + d_attn * attn
            d_kb = d_kb + _mm(d_kk, kh)
            d_k = _mm_tn(d_kk, kb) + _mm_tn(d_qk, qh)
            d_q = _mm(d_qk, kh) + d_qd * e_gc
            colsum = jnp.sum(dm, axis=0, keepdims=True)
            d_gc = d_gc + jnp.sum(dm, axis=-1, keepdims=True) - jnp.sum(jnp.where(eye, colsum, 0.0), axis=-1, keepdims=True)
            d_gc = d_gc + jnp.sum(d_qd * qd, axis=-1, keepdims=True)
            t_kd = jnp.sum(d_kd * kd, axis=-1, keepdims=True)
            d_gl = jnp.sum(t_kd, axis=0, keepdims=True) + d_c * cdec
            d_gc = d_gc - t_kd + jnp.where(ri[:, 0:1] == C - 1, d_gl, 0.0)
            d_k = d_k + d_kd * e_kd + d_kb * beta
            d_beta = d_beta + jnp.sum(d_kb * kh, axis=-1, keepdims=True)
            d_g = _dg((ri <= ci).astype(F32), jnp.broadcast_to(d_gc, (C, 128)), 1, 0, HI)
            dq_ref[:, sl] = d_q
            dk_ref[:, sl] = d_k
            dbg = dbg + jnp.where(lane == h, d_beta, 0.0) + jnp.where(lane == DN_H + h, d_g, 0.0)
        dbg_ref[...] = dbg

    tok = lambda w: pl.BlockSpec((C, w), lambda i: (nt - 1 - i, 0))
    return pl.pallas_call(
        kern, name=name, grid=(nt,),
        in_specs=[tok(MIX), tok(MIX), tok(MIX), tok(128), tok(DN_H * C), tok(2 * MIX), tok(MIX),
                  pl.BlockSpec((DN_H * DN_HD, DN_HD), lambda i: (nt - 1 - i, 0)), tok(MIX)],
        out_specs=[tok(MIX), tok(MIX), tok(MIX), tok(128)],
        out_shape=[jax.ShapeDtypeStruct((S, MIX), F32)] * 3 + [jax.ShapeDtypeStruct((S, 128), F32)],
        scratch_shapes=[pltpu.VMEM((DN_H * DN_HD, DN_HD), F32)],
        compiler_params=_cparams(("arbitrary",)),
    )(q, k, v, bg, tm, uw, vn, st, do)


def _dn_post_fwd(o, proj, ng, name):
    S = o.shape[0]

    def body(i, o_ref, z_ref, g_ref, out_ref):
        gv = g_ref[...]
        for h in range(DN_H):
            sl = slice(h * DN_HD, (h + 1) * DN_HD)
            oh = o_ref[:, sl]
            r = lax.rsqrt(jnp.mean(oh * oh, axis=-1, keepdims=True) + EPS)
            out_ref[:, sl] = (oh * r * gv * _silu(z_ref[:, sl])).astype(BF16)

    return _tok_call(body, name, S, min(S, 512), [(o, MIX, 0), (proj, MIX, C_ZC // MIX)], [ng], [(MIX, BF16)])[0]


def _dn_post_bwd(o, proj, ng, dout, name):
    S = o.shape[0]

    def body(i, o_ref, z_ref, do_ref, g_ref, dov_ref, dz_ref, dg_ref):
        gv = g_ref[...]
        dg = jnp.zeros((1, DN_HD), F32)
        for h in range(DN_H):
            sl = slice(h * DN_HD, (h + 1) * DN_HD)
            oh, zh, dh = o_ref[:, sl], z_ref[:, sl], do_ref[:, sl].astype(F32)
            r = lax.rsqrt(jnp.mean(oh * oh, axis=-1, keepdims=True) + EPS)
            dz_ref[:, sl] = (dh * oh * r * gv * _dsilu(zh)).astype(BF16)
            dx, dgh = _rms_bwd_vals(oh, gv, dh * _silu(zh))
            dov_ref[:, sl] = dx
            dg = dg + dgh
        _acc(dg_ref, dg, i)

    return _tok_call(body, name, S, min(S, 512), [(o, MIX, 0), (proj, MIX, C_ZC // MIX), (dout, MIX, 0)], [ng],
                     [(MIX, F32), (MIX, BF16)], [((1, DN_HD), F32)])


def _layer_params(w, l):
    lane = jnp.arange(128)
    is_g = (lane >= DN_H) & (lane < 2 * DN_H)
    spread = lambda t: jnp.where(is_g, jnp.tile(t, 128 // DN_H), 0.0).reshape(1, 128)
    tril = jnp.tril(jnp.ones((SGU_T, SGU_T), bool))
    return dict(
        win=w["w_in"][l], wb=w["w_branch"][l], wout=w["w_out"][l], wgu=w["w_gate_up"][l], wdown=w["w_down"][l],
        conv=w["dn_conv_w"][l], attn_norm=w["attn_norm"][l].reshape(1, -1), ffn_norm=w["ffn_norm"][l].reshape(1, -1),
        lg=w["sgu_ln_g"][l].reshape(1, -1), lb=w["sgu_ln_b"][l].reshape(1, -1),
        wc=jnp.where(tril, w["sgu_w"][l], 0.0).reshape(SGU_G * SGU_T, SGU_T), bst=w["sgu_b"][l].T,
        sinks=w["attn_sinks"][l].reshape(1, -1), alog=spread(w["dn_a_log"][l]), dtb=spread(w["dn_dt_bias"][l]),
        ng=w["dn_norm"][l].reshape(1, -1))


def _layer_fwd(x, p, cos, sin, l):
    n = lambda s: f"l{l}_{s}"
    h = _rms_fwd(x, p["attn_norm"], n("rms1"))
    proj = _matmul(h, p["win"], name=n("mm_in"))
    out_a = _sgu_fwd(proj, p["lg"], p["lb"], p["wc"], p["bst"], n("sgu_fwd"))
    qr, kr, vr = _rope_fwd(proj, cos, sin, n("rope_fwd"))
    out_b = _swa_fwd(qr, kr, vr, p["sinks"], n("swa_fwd"))
    q, k, v, bg = _dn_pre_fwd(proj, p["conv"], p["alog"], p["dtb"], n("dn_pre_fwd"))
    o, tm, uw, vn, st = _dn_core_fwd(q, k, v, bg, n("dn_core_fwd"))
    out_c = _dn_post_fwd(o, proj, p["ng"], n("dn_post_fwd"))
    outs = (out_a, out_b, out_c)
    bds = [_matmul(outs[j], p["wb"][j], name=n(f"mm_branch{j}")) for j in range(3)]
    merged = _merge_fwd(proj, bds, n("merge_fwd"))
    x1 = _matmul(merged, p["wout"], add=x, name=n("mm_out"))
    h2 = _rms_fwd(x1, p["ffn_norm"], n("rms2"))
    gu = _matmul(h2, p["wgu"], name=n("mm_gu"))
    act = _swiglu_fwd(gu, n("swiglu_fwd"))
    x2 = _matmul(act, p["wdown"], add=x1, name=n("mm_down"))
    saved = dict(x=x, h=h, proj=proj, outs=outs, qr=qr, kr=kr, vr=vr, q=q, k=k, v=v, bg=bg, o=o, tm=tm, uw=uw, vn=vn,
                 st=st, bds=bds, merged=merged, x1=x1, h2=h2, gu=gu, act=act)
    return x2, saved


def _layer_bwd(dx2, s, p, cos, sin, l):
    n = lambda t: f"l{l}_{t}"
    proj = s["proj"]
    g = {}
    g["w_down"] = _matmul(s["act"], dx2, ta=True, name=n("wg_down"))
    dact = _matmul(dx2, p["wdown"], tb=True, name=n("dg_down"))
    dgu = _swiglu_bwd(s["gu"], dact, n("swiglu_bwd"))
    g["w_gate_up"] = _matmul(s["h2"], dgu, ta=True, name=n("wg_gu"))
    dh2 = _matmul(dgu, p["wgu"], tb=True, name=n("dg_gu"))
    dx1, g["ffn_norm"] = _rms_bwd_add(s["x1"], p["ffn_norm"], dh2, dx2, n("rms2_bwd"))
    g["w_out"] = _matmul(s["merged"], dx1, ta=True, name=n("wg_out"))
    dm = _matmul(dx1, p["wout"], tb=True, name=n("dg_out"))
    dbd0, dbd1, dbd2, dgp = _merge_bwd(proj, s["bds"], dm, n("merge_bwd"))
    dbds = (dbd0, dbd1, dbd2)
    g["w_branch"] = jnp.stack([_matmul(s["outs"][j], dbds[j], ta=True, name=n(f"wg_branch{j}")) for j in range(3)])
    douts = [_matmul(dbds[j], p["wb"][j], tb=True, name=n(f"dg_branch{j}")) for j in range(3)]
    dua, dva, g["sgu_ln_g"], g["sgu_ln_b"], dwc, dbs = _sgu_bwd(proj, p["lg"], p["lb"], p["wc"], p["bst"], douts[0],
                                                                n("sgu_bwd"))
    g["sgu_w"] = dwc.reshape(SGU_G, SGU_T, SGU_T)
    g["sgu_b"] = dbs.T
    dqr, dkr, dvr, dsink = _swa_bwd(s["qr"], s["kr"], s["vr"], p["sinks"], douts[1], n("swa_bwd"))
    g["attn_sinks"] = dsink[0, :SWA_H]
    dqb, dkb, dvb = _rope_bwd(dqr, dkr, dvr, cos, sin, n("rope_bwd"))
    do, dz, dng = _dn_post_bwd(s["o"], proj, p["ng"], douts[2], n("dn_post_bwd"))
    g["dn_norm"] = dng[0]
    dq, dk, dv, dbg = _dn_core_bwd(s["q"], s["k"], s["v"], s["bg"], s["tm"], s["uw"], s["vn"], s["st"], do,
                                   n("dn_core_bwd"))
    dpre, dsm, g["dn_conv_w"], dal, ddb = _dn_pre_bwd1(proj, p["conv"], p["alog"], p["dtb"], dq, dk, dv, dbg,
                                                       n("dn_pre_bwd1"))
    g["dn_a_log"] = dal[0, DN_H:2 * DN_H]
    g["dn_dt_bias"] = ddb[0, DN_H:2 * DN_H]
    dqkv = _dn_pre_bwd2(dpre, p["conv"], n("dn_pre_bwd2"))
    dproj = jnp.concatenate([dgp, dqkv, dua, dva, dqb, dz, dkb, dvb, dsm], axis=1)
    g["w_in"] = _matmul(s["h"], dproj, ta=True, name=n("wg_in"))
    dh = _matmul(dproj, p["win"], tb=True, name=n("dg_in"))
    dx, g["attn_norm"] = _rms_bwd_add(s["x"], p["attn_norm"], dh, dx1, n("rms1_bwd"))
    g["attn_norm"], g["ffn_norm"] = g["attn_norm"][0], g["ffn_norm"][0]
    g["sgu_ln_g"], g["sgu_ln_b"] = g["sgu_ln_g"][0], g["sgu_ln_b"][0]
    return dx, g


def _local_step(x, positions, target, w):
    cos, sin = _rope_tables(positions)
    params = [_layer_params(w, l) for l in range(DEPTH)]
    saves, xs = [], x
    for l in range(DEPTH):
        xs, sv = _layer_fwd(xs, params[l], cos, sin, l)
        saves.append(sv)
    dx, loss_row, dgf = _final_loss(xs, w["final_norm"].reshape(1, -1), target)
    grads = [None] * DEPTH
    for l in reversed(range(DEPTH)):
        dx, grads[l] = _layer_bwd(dx, saves[l], params[l], cos, sin, l)
    stacked = {k: jnp.stack([grads[l][k] for l in range(DEPTH)]) for k in grads[0]}
    stacked["final_norm"] = dgf[0]
    return loss_row[0, 0], dx, stacked


MESH = pl.DeviceIdType.MESH
HBM_SPEC = pl.BlockSpec(memory_space=pltpu.HBM)
VMEM_SPEC = pl.BlockSpec(memory_space=pltpu.VMEM)
BIG = ("w_in", "w_branch", "w_out", "w_gate_up", "w_down")
SHARD_AXIS = {"w_in": 2, "w_branch": 3, "w_out": 1, "w_gate_up": 2, "w_down": 1, "dn_conv_w": 2}
PACK_ROWS = 9216
HALF_ROWS = PACK_ROWS // 2
FLIPS = tuple((fx, fy, fc) for fx in (0, 1) for fy in (0, 1) for fc in (0, 1))[1:]


def _me():
    return lax.axis_index("x"), lax.axis_index("y"), lax.axis_index("c")


def _peer(x, y, c, flip):
    fx, fy, fc = flip
    return (1 - x if fx else x, 1 - y if fy else y, 1 - c if fc else c)


def _pack_big(shards, conv_words=None):
    flat = [shards[k].astype(BF16).reshape(-1) for k in BIG]
    n = sum(f.shape[0] for f in flat)
    tail = PACK_ROWS * 1024 - n
    if conv_words is not None:
        flat.append(conv_words.reshape(-1))
        tail -= conv_words.size
    flat.append(jnp.zeros((tail,), BF16))
    return jnp.concatenate(flat).reshape(PACK_ROWS, 1024)


def _unpack_big(slab, shapes, dtype):
    flat = slab.reshape(-1)
    out, o = {}, 0
    for k in BIG:
        n = math.prod(shapes[k])
        out[k] = flat[o:o + n].reshape(shapes[k]).astype(dtype)
        o += n
    return out, flat[o:]


def _allgather_slabs(slab):
    def body(slab_ref, out_ref, send_sems, recv_sems, local_sem):
        x, y, c = _me()
        chips = [(1 - x, y), (x, 1 - y), (1 - x, 1 - y)]

        def rows(px, py, pc):
            return out_ref.at[2 * px + py, pl.ds(pl.multiple_of(pc * HALF_ROWS, 16), HALF_ROWS), :]

        def copy(k, block, to, src=None):
            return pltpu.make_async_remote_copy(
                src_ref=rows(*block) if src is None else src, dst_ref=rows(*block), send_sem=send_sems.at[k],
                recv_sem=recv_sems.at[k], device_id=to, device_id_type=MESH)

        mine = pltpu.make_async_copy(slab_ref, out_ref.at[2 * x + y], local_sem)
        mine.start()
        my_half = slab_ref.at[pl.ds(pl.multiple_of(c * HALF_ROWS, 16), HALF_ROWS), :]
        first = [copy(j, (x, y, c), (*chip, c), src=my_half) for j, chip in enumerate(chips)]
        for cp in first:
            cp.start()
        passed = [copy(3 + j, (*chip, c), (x, y, 1 - c)) for j, chip in enumerate(chips)]
        for j, chip in enumerate(chips):
            copy(j, (*chip, c), (x, y, c)).wait_recv()
            passed[j].start()
        for j, chip in enumerate(chips):
            copy(3 + j, (*chip, 1 - c), (x, y, c)).wait_recv()
        for cp in first + passed:
            cp.wait_send()
        mine.wait()

    return pl.pallas_call(
        body, name="allgather_weights", out_shape=jax.ShapeDtypeStruct((4, PACK_ROWS, 1024), BF16),
        in_specs=[HBM_SPEC], out_specs=HBM_SPEC,
        scratch_shapes=[pltpu.SemaphoreType.DMA((6,)), pltpu.SemaphoreType.DMA((6,)), pltpu.SemaphoreType.DMA],
    )(slab)


def _scatter_grad_slabs(gslabs):
    def body(g_ref, out_ref, send_sems, recv_sems, local_sem):
        x, y, c = _me()

        def part(chip, pc):
            return g_ref.at[chip, pl.ds(pl.multiple_of(pc * HALF_ROWS, 16), HALF_ROWS), :]

        def copy(k, to, src):
            sx, sy, sc = src
            tx, ty, tc = to
            return pltpu.make_async_remote_copy(
                src_ref=part(2 * tx + ty, tc), dst_ref=out_ref.at[4 * sx + 2 * sy + sc], send_sem=send_sems.at[k],
                recv_sem=recv_sems.at[k], device_id=to, device_id_type=MESH)

        mine = pltpu.make_async_copy(part(2 * x + y, c), out_ref.at[4 * x + 2 * y + c], local_sem)
        mine.start()
        sends = [copy(k, _peer(x, y, c, f), (x, y, c)) for k, f in enumerate(FLIPS)]
        for cp in sends:
            cp.start()
        for k, f in enumerate(FLIPS):
            copy(k, (x, y, c), _peer(x, y, c, f)).wait_recv()
        for cp in sends:
            cp.wait_send()
        mine.wait()

    return pl.pallas_call(
        body, name="scatter_grads", out_shape=jax.ShapeDtypeStruct((8, HALF_ROWS, 1024), BF16),
        in_specs=[HBM_SPEC], out_specs=HBM_SPEC,
        scratch_shapes=[pltpu.SemaphoreType.DMA((7,)), pltpu.SemaphoreType.DMA((7,)), pltpu.SemaphoreType.DMA],
    )(gslabs)


def _sum_slots(parts):
    rows = parts.shape[1]
    tr = _pick(rows, (256, 16))

    def kern(p_ref, o_ref):
        tot = p_ref[0].astype(F32)
        for d in range(1, 8):
            tot = tot + p_ref[d].astype(F32)
        o_ref[...] = tot

    return pl.pallas_call(
        kern, name="sum_grads", grid=(rows // tr,), in_specs=[pl.BlockSpec((8, tr, 1024), lambda i: (0, i, 0))],
        out_specs=pl.BlockSpec((tr, 1024), lambda i: (i, 0)), out_shape=jax.ShapeDtypeStruct((rows, 1024), F32),
        compiler_params=_cparams(("parallel",)),
    )(parts)


def _exchange_halves(half):
    def body(h_ref, out_ref, send_sem, recv_sem, local_sem):
        x, y, c = _me()

        def rows(pc):
            return out_ref.at[pl.ds(pl.multiple_of(pc * HALF_ROWS, 8), HALF_ROWS), :]

        mine = pltpu.make_async_copy(h_ref, rows(c), local_sem)
        mine.start()
        send = pltpu.make_async_remote_copy(src_ref=h_ref, dst_ref=rows(c), send_sem=send_sem, recv_sem=recv_sem,
                                            device_id=(x, y, 1 - c), device_id_type=MESH)
        send.start()
        pltpu.make_async_remote_copy(src_ref=h_ref, dst_ref=rows(1 - c), send_sem=send_sem, recv_sem=recv_sem,
                                     device_id=(x, y, 1 - c), device_id_type=MESH).wait_recv()
        send.wait_send()
        mine.wait()

    return pl.pallas_call(
        body, name="exchange_halves", out_shape=jax.ShapeDtypeStruct((PACK_ROWS, 1024), F32),
        in_specs=[HBM_SPEC], out_specs=HBM_SPEC,
        scratch_shapes=[pltpu.SemaphoreType.DMA, pltpu.SemaphoreType.DMA, pltpu.SemaphoreType.DMA],
    )(half)


def _adam_vals(g, w, m, v):
    m2 = ADAM_B1 * m + (1.0 - ADAM_B1) * g
    v2 = ADAM_B2 * v + (1.0 - ADAM_B2) * (g * g)
    m_hat = m2 / (1.0 - ADAM_B1 ** ADAM_STEP)
    v_hat = v2 / (1.0 - ADAM_B2 ** ADAM_STEP)
    return -ADAM_LR * (m_hat / (jnp.sqrt(v_hat) + ADAM_EPS) + ADAM_WD * w), m2, v2


def _allreduce_small_adam(gp, wp, mp, vp):
    rows = gp.shape[0]

    def body(g_ref, w_ref, m_ref, v_ref, gs_ref, d_ref, nm_ref, nv_ref, buf, send_sems, recv_sems):
        x, y, c = _me()
        buf[4 * x + 2 * y + c] = g_ref[...]

        def copy(k, to, src):
            sx, sy, sc = src
            return pltpu.make_async_remote_copy(
                src_ref=g_ref, dst_ref=buf.at[4 * sx + 2 * sy + sc], send_sem=send_sems.at[k],
                recv_sem=recv_sems.at[k], device_id=to, device_id_type=MESH)

        sends = [copy(k, _peer(x, y, c, f), (x, y, c)) for k, f in enumerate(FLIPS)]
        for cp in sends:
            cp.start()
        for k, f in enumerate(FLIPS):
            copy(k, (x, y, c), _peer(x, y, c, f)).wait_recv()
        for cp in sends:
            cp.wait_send()
        tot = buf[0]
        for d in range(1, 8):
            tot = tot + buf[d]
        gs_ref[...] = tot
        d_ref[...], nm_ref[...], nv_ref[...] = _adam_vals(tot, w_ref[...], m_ref[...], v_ref[...])

    return pl.pallas_call(
        body, name="allreduce_small", out_shape=[jax.ShapeDtypeStruct((rows, 128), F32)] * 4,
        in_specs=[VMEM_SPEC] * 4, out_specs=[VMEM_SPEC] * 4,
        scratch_shapes=[pltpu.VMEM((8, rows, 128), F32), pltpu.SemaphoreType.DMA((7,)), pltpu.SemaphoreType.DMA((7,))],
        compiler_params=pltpu.CompilerParams(vmem_limit_bytes=VMEM_LIMIT),
    )(gp, wp, mp, vp)


def _adam(g, w, m, v, name):
    shape = w.shape
    cols = shape[-1]
    rows = w.size // cols
    tr = _pick(rows, (256, 8))
    spec = pl.BlockSpec((tr, cols), lambda i: (i, 0))

    def kern(g_ref, w_ref, m_ref, v_ref, d_ref, nm_ref, nv_ref):
        d_ref[...], nm_ref[...], nv_ref[...] = _adam_vals(g_ref[...], w_ref[...], m_ref[...], v_ref[...])

    outs = pl.pallas_call(
        kern, name=name, grid=(rows // tr,), in_specs=[spec] * 4, out_specs=[spec] * 3,
        out_shape=[jax.ShapeDtypeStruct((rows, cols), F32)] * 3, compiler_params=_cparams(("parallel",)),
    )(*[t.reshape(rows, cols) for t in (g, w, m, v)])
    return [o.reshape(shape) for o in outs]


SMALL = ("attn_norm", "sgu_ln_g", "sgu_ln_b", "sgu_w", "sgu_b", "attn_sinks", "dn_a_log", "dn_dt_bias", "dn_norm",
         "ffn_norm", "final_norm")
SMALL_ROWS = 1200


def _pack_small(vals, extra=()):
    flat = [vals[k].astype(F32).reshape(-1) for k in SMALL] + [e.astype(F32).reshape(-1) for e in extra]
    n = sum(f.shape[0] for f in flat)
    flat.append(jnp.zeros((SMALL_ROWS * 128 - n,), F32))
    return jnp.concatenate(flat).reshape(SMALL_ROWS, 128)


def _unpack_small(slab, shapes):
    flat = slab.reshape(-1)
    out, o = {}, 0
    for k in SMALL:
        n = math.prod(shapes[k])
        out[k] = flat[o:o + n].reshape(shapes[k])
        o += n
    return out, flat[o:]


def _permute_in_cols(w_in, dtype):
    parts = [w_in[..., a:a + n] for a, n in IN_PIECES]
    parts.append(jnp.zeros(w_in.shape[:-1] + (IN_PAD,), w_in.dtype))
    return jnp.concatenate(parts, axis=-1).astype(dtype)


def _unpermute_in_cols(g):
    offs, o = {}, 0
    for a, n in IN_PIECES:
        offs[a] = (o, n)
        o += n
    return jnp.concatenate([g[..., offs[a][0]:offs[a][0] + offs[a][1]] for a in sorted(offs)], axis=-1)


WEIGHTS = ("attn_norm", "w_in", "sgu_ln_g", "sgu_ln_b", "sgu_w", "sgu_b", "attn_sinks", "dn_conv_w", "dn_a_log",
           "dn_dt_bias", "dn_norm", "w_branch", "w_out", "ffn_norm", "w_gate_up", "w_down", "final_norm")
N_CHIPS = 4


def _shard(t, axis, s):
    n = t.shape[axis] // N_CHIPS
    return lax.slice_in_dim(t, s * n, (s + 1) * n, axis=axis)


def kernel(x, positions, attn_norm, w_in, sgu_ln_g, sgu_ln_b, sgu_w, sgu_b, attn_sinks, dn_conv_w, dn_a_log, dn_dt_bias, dn_norm, w_branch, w_out, ffn_norm, w_gate_up, w_down, final_norm, loss_target, m_attn_norm, m_w_in, m_sgu_ln_g, m_sgu_ln_b, m_sgu_w, m_sgu_b, m_attn_sinks, m_dn_conv_w, m_dn_a_log, m_dn_dt_bias, m_dn_norm, m_w_branch, m_w_out, m_ffn_norm, m_w_gate_up, m_w_down, m_final_norm, v_attn_norm, v_w_in, v_sgu_ln_g, v_sgu_ln_b, v_sgu_w, v_sgu_b, v_attn_sinks, v_dn_conv_w, v_dn_a_log, v_dn_dt_bias, v_dn_norm, v_w_branch, v_w_out, v_ffn_norm, v_w_gate_up, v_w_down, v_final_norm):
    given = dict(locals())
    W = {k: given[k] for k in WEIGHTS}
    M = {k: given["m_" + k] for k in WEIGHTS}
    V = {k: given["v_" + k] for k in WEIGHTS}
    chip = 2 * lax.axis_index("x") + lax.axis_index("y")

    conv_words = lax.bitcast_convert_type(dn_conv_w, BF16)
    slabs = _allgather_slabs(_pack_big(W, conv_words))
    shapes = {k: W[k].shape for k in BIG}
    per_chip = [_unpack_big(slabs[s], shapes, BF16) for s in range(N_CHIPS)]
    full = {k: jnp.concatenate([per_chip[s][0][k] for s in range(N_CHIPS)], axis=SHARD_AXIS[k]) for k in BIG}
    full["w_in"] = _permute_in_cols(full["w_in"], BF16)
    full["dn_conv_w"] = jnp.concatenate(
        [lax.bitcast_convert_type(per_chip[s][1][:conv_words.size].reshape(conv_words.shape), F32)
         for s in range(N_CHIPS)], axis=SHARD_AXIS["dn_conv_w"])
    w = {**{k: W[k] for k in SMALL}, **full}

    loss, dx, g = _local_step(x[0], positions[0], loss_target[0], w)

    g["w_in"] = _unpermute_in_cols(g["w_in"])
    gslabs = jnp.stack([_pack_big({k: _shard(g[k], SHARD_AXIS[k], s) for k in BIG}) for s in range(N_CHIPS)])
    reduced = _exchange_halves(_sum_slots(_scatter_grad_slabs(gslabs)))
    grads, _ = _unpack_big(reduced, shapes, F32)

    small_shapes = {k: W[k].shape for k in SMALL}
    gs, ds, nms, nvs = _allreduce_small_adam(_pack_small(g, (g["dn_conv_w"], loss.reshape(1))), _pack_small(W),
                                             _pack_small(M), _pack_small(V))
    gsm, rest = _unpack_small(gs, small_shapes)
    grads.update(gsm)
    n_conv = g["dn_conv_w"].size
    conv_full = rest[:n_conv].reshape(g["dn_conv_w"].shape)
    grads["dn_conv_w"] = lax.dynamic_slice_in_dim(conv_full, chip * dn_conv_w.shape[2], dn_conv_w.shape[2], axis=2)
    loss_total = rest[n_conv]
    delta, new_m, new_v = (_unpack_small(t, small_shapes)[0] for t in (ds, nms, nvs))
    for k in BIG + ("dn_conv_w",):
        delta[k], new_m[k], new_v[k] = _adam(grads[k], W[k], M[k], V[k], "adam_" + k)

    return (loss_total, dx[None], *[grads[k] for k in WEIGHTS], *[delta[k] for k in WEIGHTS],
            *[new_m[k] for k in WEIGHTS], *[new_v[k] for k in WEIGHTS])
```

```python
import functools
import math

import jax
import jax.numpy as jnp
from jax import lax
from jax.experimental import pallas as pl
from jax.experimental.pallas import tpu as pltpu

F32 = jnp.float32
BF16 = jnp.bfloat16
HI = lax.Precision.HIGHEST

D_MODEL = 1024
DEPTH = 2
MIX = 512
EPS = 1e-6
SGU_G, SGU_T = 4, 128
SWA_H, SWA_KV, SWA_HD, WINDOW = 8, 2, 64, 128
ROPE_THETA, ROPE_DIM = 500000.0, 16
DN_H, DN_HD, DN_CONV, DN_C = 4, 128, 4, 64
D_FF = 2816
IN_COLS = 6920
IN_PIECES = ((3848, 3072), (1792, 1536), (0, 512), (512, 512), (1024, 512), (3328, 512), (1536, 128), (1664, 128),
             (3840, 8))
IN_PAD = 120
IN_R = 7040
C_GATE, C_QKV, C_UA, C_VA, C_QB, C_ZC, C_KB, C_VB, C_SM = 0, 3072, 4608, 5120, 5632, 6144, 6656, 6784, 6912

ADAM_LR, ADAM_B1, ADAM_B2, ADAM_EPS, ADAM_WD, ADAM_STEP = 0.001, 0.9, 0.999, 1e-08, 0.01, 10
VMEM_LIMIT = 56 * 1024 * 1024


def _cparams(sem):
    return pltpu.CompilerParams(dimension_semantics=sem, vmem_limit_bytes=VMEM_LIMIT)


def _dg(a, b, ca, cb, prec=None):
    return lax.dot_general(a, b, (((ca,), (cb,)), ((), ())), precision=prec, preferred_element_type=F32)


def _split(x):
    hi = x.astype(BF16)
    return hi, (x - hi.astype(F32)).astype(BF16)


def _dg3(a, b, ca, cb):
    ah, al = _split(a)
    bh, bl = _split(b)
    return _dg(ah, bh, ca, cb) + (_dg(ah, bl, ca, cb) + _dg(al, bh, ca, cb))


def _dg_exact_lhs(a01, b, ca, cb):
    a = a01.astype(BF16)
    b1 = b.astype(BF16)
    r1 = b - b1.astype(F32)
    b2 = r1.astype(BF16)
    b3 = (r1 - b2.astype(F32)).astype(BF16)
    return _dg(a, b1, ca, cb) + (_dg(a, b2, ca, cb) + _dg(a, b3, ca, cb))


def _mm(a, b):
    return _dg(a.astype(BF16), b.astype(BF16), 1, 0)


def _mm_nt(a, b):
    return _dg(a.astype(BF16), b.astype(BF16), 1, 1)


def _mm_tn(a, b):
    return _dg(a.astype(BF16), b.astype(BF16), 0, 0)


def _sigmoid(x):
    return 1.0 / (1.0 + jnp.exp(-x))


def _silu(x):
    return x * _sigmoid(x)


def _dsilu(x):
    s = _sigmoid(x)
    return s * (1.0 + x * (1.0 - s))


_GC = math.sqrt(2.0 / math.pi)


def _gelu(x):
    return 0.5 * x * (1.0 + jnp.tanh(_GC * (x + 0.044715 * x * x * x)))


def _dgelu(x):
    t = jnp.tanh(_GC * (x + 0.044715 * x * x * x))
    return 0.5 * (1.0 + t) + 0.5 * x * (1.0 - t * t) * _GC * (1.0 + 3.0 * 0.044715 * x * x)


def _softplus(x):
    return jnp.maximum(x, 0.0) + jnp.log(1.0 + jnp.exp(-jnp.abs(x)))


def _acc(ref, val, i):
    @pl.when(i == 0)
    def _():
        ref[...] = val

    @pl.when(i > 0)
    def _():
        ref[...] += val


def _tok_call(body, name, S, TB, tok_in, const_in=(), tok_out=(), acc_out=(), prev_in=(), next_in=(), smem_in=()):
    nb = S // TB
    r8 = TB // 8
    in_specs, args = [], []
    for a, w, cb in tok_in:
        in_specs.append(pl.BlockSpec((TB, w), functools.partial(lambda i, cb: (i, cb), cb=cb)))
        args.append(a)
    for a, w, cb in prev_in:
        in_specs.append(pl.BlockSpec((8, w), functools.partial(lambda i, cb: (jnp.maximum(i * r8 - 1, 0), cb), cb=cb)))
        args.append(a)
    for a, w, cb in next_in:
        in_specs.append(pl.BlockSpec((8, w), functools.partial(
            lambda i, cb: (jnp.minimum((i + 1) * r8, S // 8 - 1), cb), cb=cb)))
        args.append(a)
    for a in const_in:
        in_specs.append(pl.BlockSpec(a.shape, lambda i: (0, 0)))
        args.append(a)
    for a in smem_in:
        in_specs.append(pl.BlockSpec(memory_space=pltpu.SMEM))
        args.append(a)
    out_specs, out_shape = [], []
    for w, dt in tok_out:
        out_specs.append(pl.BlockSpec((TB, w), lambda i: (i, 0)))
        out_shape.append(jax.ShapeDtypeStruct((S, w), dt))
    for shp, dt in acc_out:
        out_specs.append(pl.BlockSpec(shp, lambda i: (0, 0)))
        out_shape.append(jax.ShapeDtypeStruct(shp, dt))

    def kern(*refs):
        body(pl.program_id(0), *refs)

    return pl.pallas_call(
        kern, name=name, grid=(nb,), in_specs=in_specs, out_specs=out_specs, out_shape=out_shape,
        compiler_params=_cparams(("arbitrary",)),
    )(*args)


MM_BLOCKS = (1024, 1408, 640, 512, 256, 128)


def _pick(n, cands):
    for c in cands:
        if n % c == 0:
            return c
    return n


def _matmul(a, b, *, ta=False, tb=False, add=None, out_dtype=F32, name):
    M, K = (a.shape[1], a.shape[0]) if ta else a.shape
    N = b.shape[0] if tb else b.shape[1]
    bm, bn, bk = _pick(M, MM_BLOCKS), _pick(N, MM_BLOCKS), _pick(K, MM_BLOCKS)
    nk = K // bk
    a_spec = pl.BlockSpec((bk, bm), lambda i, j, k: (k, i)) if ta else pl.BlockSpec((bm, bk), lambda i, j, k: (i, k))
    b_spec = pl.BlockSpec((bn, bk), lambda i, j, k: (j, k)) if tb else pl.BlockSpec((bk, bn), lambda i, j, k: (k, j))
    o_spec = pl.BlockSpec((bm, bn), lambda i, j, k: (i, j))
    ca, cb = (0 if ta else 1), (1 if tb else 0)

    def kern(*refs):
        if add is None:
            a_ref, b_ref, o_ref, acc_ref = refs
        else:
            a_ref, b_ref, add_ref, o_ref, acc_ref = refs
        k = pl.program_id(2)
        p = _dg(a_ref[...].astype(BF16), b_ref[...].astype(BF16), ca, cb)

        @pl.when(k == 0)
        def _():
            acc_ref[...] = p

        @pl.when(k > 0)
        def _():
            acc_ref[...] += p

        @pl.when(k == nk - 1)
        def _():
            r = acc_ref[...]
            if add is not None:
                r = r + add_ref[...].astype(F32)
            o_ref[...] = r.astype(out_dtype)

    in_specs = [a_spec, b_spec] + ([o_spec] if add is not None else [])
    args = (a, b) + ((add,) if add is not None else ())
    return pl.pallas_call(
        kern, name=name, grid=(M // bm, N // bn, nk), in_specs=in_specs, out_specs=o_spec,
        out_shape=jax.ShapeDtypeStruct((M, N), out_dtype), scratch_shapes=[pltpu.VMEM((bm, bn), F32)],
        compiler_params=_cparams(("parallel", "parallel", "arbitrary")),
    )(*args)


def _rms_fwd(x, g, name):
    S = x.shape[0]

    def body(i, x_ref, g_ref, h_ref):
        xv = x_ref[...]
        r = lax.rsqrt(jnp.mean(xv * xv, axis=-1, keepdims=True) + EPS)
        h_ref[...] = (xv * r * g_ref[...]).astype(BF16)

    return _tok_call(body, name, S, min(S, 512), [(x, D_MODEL, 0)], [g], [(D_MODEL, BF16)])[0]


def _rms_bwd_vals(xv, g, dh):
    r = lax.rsqrt(jnp.mean(xv * xv, axis=-1, keepdims=True) + EPS)
    u = dh * g
    dx = r * u - xv * (r * r * r) * jnp.mean(u * xv, axis=-1, keepdims=True)
    dg = jnp.sum(dh * xv * r, axis=0, keepdims=True)
    return dx, dg


def _rms_bwd_add(x, g, dh, dres, name):
    S = x.shape[0]

    def body(i, x_ref, dh_ref, dr_ref, g_ref, dx_ref, dg_ref):
        dx, dg = _rms_bwd_vals(x_ref[...], g_ref[...], dh_ref[...].astype(F32))
        dx_ref[...] = dr_ref[...] + dx
        _acc(dg_ref, dg, i)

    return _tok_call(body, name, S, min(S, 512), [(x, D_MODEL, 0), (dh, D_MODEL, 0), (dres, D_MODEL, 0)], [g],
                     [(D_MODEL, F32)], [((1, D_MODEL), F32)])


def _final_loss(x, g, target):
    S = x.shape[0]

    def body(i, x_ref, t_ref, g_ref, dx_ref, loss_ref, dg_ref):
        xv, gv = x_ref[...], g_ref[...]
        r = lax.rsqrt(jnp.mean(xv * xv, axis=-1, keepdims=True) + EPS)
        e = xv * r * gv - t_ref[...]
        part = 0.5 * jnp.sum(jnp.mean(e * e, axis=-1, keepdims=True), axis=0, keepdims=True)
        dx, dg = _rms_bwd_vals(xv, gv, e * (1.0 / D_MODEL))
        dx_ref[...] = dx
        _acc(loss_ref, jnp.broadcast_to(part, (1, 128)), i)
        _acc(dg_ref, dg, i)

    return _tok_call(body, "final_loss", S, min(S, 512), [(x, D_MODEL, 0), (target, D_MODEL, 0)], [g],
                     [(D_MODEL, F32)], [((1, 128), F32), ((1, D_MODEL), F32)])


def _swiglu_fwd(gu, name):
    S = gu.shape[0]

    def body(i, gu_ref, a_ref):
        a_ref[...] = (_silu(gu_ref[:, :D_FF]) * gu_ref[:, D_FF:]).astype(BF16)

    return _tok_call(body, name, S, min(S, 256), [(gu, 2 * D_FF, 0)], [], [(D_FF, BF16)])[0]


def _swiglu_bwd(gu, dact, name):
    S = gu.shape[0]

    def body(i, gu_ref, da_ref, dgu_ref):
        gg, uu, da = gu_ref[:, :D_FF], gu_ref[:, D_FF:], da_ref[...]
        dgu_ref[:, :D_FF] = (da * uu * _dsilu(gg)).astype(BF16)
        dgu_ref[:, D_FF:] = (da * _silu(gg)).astype(BF16)

    return _tok_call(body, name, S, min(S, 256), [(gu, 2 * D_FF, 0), (dact, D_FF, 0)], [], [(2 * D_FF, BF16)])[0]


def _merge_fwd(proj, bds, name):
    S = proj.shape[0]

    def body(i, g0, g1, g2, b0, b1, b2, m_ref):
        m = _sigmoid(g0[...]) * b0[...] + _sigmoid(g1[...]) * b1[...] + _sigmoid(g2[...]) * b2[...]
        m_ref[...] = m.astype(BF16)

    tok = [(proj, D_MODEL, n) for n in range(3)] + [(b, D_MODEL, 0) for b in bds]
    return _tok_call(body, name, S, min(S, 512), tok, [], [(D_MODEL, BF16)])[0]


def _merge_bwd(proj, bds, dm, name):
    S = proj.shape[0]

    def body(i, g0, g1, g2, b0, b1, b2, dm_ref, d0, d1, d2, dgp_ref):
        dmv = dm_ref[...]
        for n, (gr, br, dr) in enumerate(((g0, b0, d0), (g1, b1, d1), (g2, b2, d2))):
            s = _sigmoid(gr[...])
            dr[...] = (dmv * s).astype(BF16)
            dgp_ref[:, n * D_MODEL:(n + 1) * D_MODEL] = (dmv * br[...] * s * (1.0 - s)).astype(BF16)

    tok = [(proj, D_MODEL, n) for n in range(3)] + [(b, D_MODEL, 0) for b in bds] + [(dm, D_MODEL, 0)]
    return _tok_call(body, name, S, min(S, 512), tok, [],
                     [(D_MODEL, BF16)] * 3 + [(3 * D_MODEL, BF16)])


def _sgu_ln(v, lg, lb):
    mu = jnp.mean(v, axis=-1, keepdims=True)
    vc = v - mu
    rstd = lax.rsqrt(jnp.mean(vc * vc, axis=-1, keepdims=True) + EPS)
    vhat = vc * rstd
    return vhat, rstd, vhat * lg + lb


def _sgu_fwd(proj, lg, lb, wc, bst, name):
    S = proj.shape[0]

    def body(i, ua_ref, va_ref, lg_ref, lb_ref, wc_ref, bs_ref, o_ref):
        u = _gelu(ua_ref[...])
        _, _, vn = _sgu_ln(_gelu(va_ref[...]), lg_ref[...], lb_ref[...])
        for g in range(SGU_G):
            sl = slice(g * 128, (g + 1) * 128)
            mixed = _mm(wc_ref[sl, :], vn[:, sl]) + bs_ref[:, g:g + 1]
            o_ref[:, sl] = (u[:, sl] * mixed).astype(BF16)

    return _tok_call(body, name, S, SGU_T, [(proj, MIX, C_UA // MIX), (proj, MIX, C_VA // MIX)], [lg, lb, wc, bst],
                     [(MIX, BF16)])[0]


def _sgu_bwd(proj, lg, lb, wc, bst, dout, name):
    S = proj.shape[0]

    def body(i, ua_ref, va_ref, do_ref, lg_ref, lb_ref, wc_ref, bs_ref, dua_ref, dva_ref, dlg_ref, dlb_ref, dwc_ref,
             dbs_ref):
        ua, va, do = ua_ref[...], va_ref[...], do_ref[...].astype(F32)
        u = _gelu(ua)
        lgv = lg_ref[...]
        vhat, rstd, vn = _sgu_ln(_gelu(va), lgv, lb_ref[...])
        tril = lax.broadcasted_iota(jnp.int32, (128, 128), 0) >= lax.broadcasted_iota(jnp.int32, (128, 128), 1)
        lane4 = lax.broadcasted_iota(jnp.int32, (128, 4), 1)
        dvn_parts, dbs = [], jnp.zeros((128, 4), F32)
        for g in range(SGU_G):
            sl = slice(g * 128, (g + 1) * 128)
            wg = wc_ref[sl, :]
            mixed = _mm(wg, vn[:, sl]) + bs_ref[:, g:g + 1]
            dua_ref[:, sl] = (do[:, sl] * mixed * _dgelu(ua[:, sl])).astype(BF16)
            dmix = do[:, sl] * u[:, sl]
            dbs = dbs + jnp.where(lane4 == g, jnp.sum(dmix, axis=-1, keepdims=True), 0.0)
            dwg = jnp.where(tril, _mm_nt(dmix, vn[:, sl]), 0.0)
            _acc(dwc_ref.at[sl, :], dwg, i)
            dvn_parts.append(_mm_tn(wg, dmix))
        dvn = jnp.concatenate(dvn_parts, axis=1)
        _acc(dbs_ref, dbs, i)
        _acc(dlg_ref, jnp.sum(dvn * vhat, axis=0, keepdims=True), i)
        _acc(dlb_ref, jnp.sum(dvn, axis=0, keepdims=True), i)
        dvh = dvn * lgv
        dv = rstd * (dvh - jnp.mean(dvh, axis=-1, keepdims=True) - vhat * jnp.mean(dvh * vhat, axis=-1, keepdims=True))
        dva_ref[...] = (dv * _dgelu(va)).astype(BF16)

    return _tok_call(body, name, S, SGU_T, [(proj, MIX, C_UA // MIX), (proj, MIX, C_VA // MIX), (dout, MIX, 0)],
                     [lg, lb, wc, bst], [(MIX, BF16), (MIX, BF16)],
                     [((1, MIX), F32), ((1, MIX), F32), ((SGU_G * 128, 128), F32), ((128, 4), F32)])


def _rope_tables(positions):
    S = positions.shape[0]
    inv_freq = ROPE_THETA ** (-jnp.arange(0, ROPE_DIM, 2, dtype=F32) / ROPE_DIM)
    ang = positions.astype(F32)[:, None] * inv_freq
    c, s = jnp.cos(ang), jnp.sin(ang)
    c64 = jnp.concatenate([c, c, jnp.ones((S, SWA_HD - ROPE_DIM), F32)], axis=1)
    s64 = jnp.concatenate([-s, s, jnp.zeros((S, SWA_HD - ROPE_DIM), F32)], axis=1)
    return jnp.tile(c64, (1, 2)), jnp.tile(s64, (1, 2))


def _rope128(x, c, s):
    lane = lax.broadcasted_iota(jnp.int32, x.shape, 1) % SWA_HD
    swapped = jnp.where(lane < ROPE_DIM // 2, pltpu.roll(x, 128 - ROPE_DIM // 2, 1), pltpu.roll(x, ROPE_DIM // 2, 1))
    return x * c + swapped * s


def _rope_t128(y, c, s):
    ys = y * s
    lane = lax.broadcasted_iota(jnp.int32, y.shape, 1) % SWA_HD
    swapped = jnp.where(lane < ROPE_DIM // 2, pltpu.roll(ys, 128 - ROPE_DIM // 2, 1), pltpu.roll(ys, ROPE_DIM // 2, 1))
    return y * c + jnp.where(lane < ROPE_DIM, swapped, 0.0)


def _rope_fwd(proj, cos, sin, name):
    S = proj.shape[0]
    scale = SWA_HD ** -0.5

    def body(i, q_ref, k_ref, v_ref, c_ref, s_ref, qo_ref, ko_ref, vo_ref):
        c, s = c_ref[...], s_ref[...]
        for j in range(4):
            sl = slice(j * 128, (j + 1) * 128)
            qo_ref[:, sl] = (_rope128(q_ref[:, sl], c, s) * scale).astype(BF16)
        ko_ref[...] = _rope128(k_ref[...], c, s).astype(BF16)
        vo_ref[...] = v_ref[...].astype(BF16)

    return _tok_call(body, name, S, min(S, 512),
                     [(proj, MIX, C_QB // MIX), (proj, 128, C_KB // 128), (proj, 128, C_VB // 128), (cos, 128, 0),
                      (sin, 128, 0)], [], [(MIX, BF16), (128, BF16), (128, BF16)])


def _rope_bwd(dq, dk, dv, cos, sin, name):
    S = dq.shape[0]
    scale = SWA_HD ** -0.5

    def body(i, dq_ref, dk_ref, dv_ref, c_ref, s_ref, qo_ref, ko_ref, vo_ref):
        c, s = c_ref[...], s_ref[...]
        for j in range(4):
            sl = slice(j * 128, (j + 1) * 128)
            qo_ref[:, sl] = _rope_t128(dq_ref[:, sl] * scale, c, s).astype(BF16)
        ko_ref[...] = _rope_t128(dk_ref[...], c, s).astype(BF16)
        vo_ref[...] = dv_ref[...].astype(BF16)

    return _tok_call(body, name, S, min(S, 512),
                     [(dq, MIX, 0), (dk, 128, 0), (dv, 128, 0), (cos, 128, 0), (sin, 128, 0)], [],
                     [(MIX, BF16), (128, BF16), (128, BF16)])


def _swa_band(i, k_ref, v_ref):
    pstart = pl.multiple_of(jnp.maximum(i - 1, 0) * WINDOW, WINDOW)
    cstart = pl.multiple_of(i * WINDOW, WINDOW)
    kb = jnp.concatenate([k_ref[pl.ds(pstart, WINDOW), :], k_ref[pl.ds(cstart, WINDOW), :]], axis=0)
    vb = jnp.concatenate([v_ref[pl.ds(pstart, WINDOW), :], v_ref[pl.ds(cstart, WINDOW), :]], axis=0)
    qi = lax.broadcasted_iota(jnp.int32, (WINDOW, 2 * WINDOW), 0)
    sj = lax.broadcasted_iota(jnp.int32, (WINDOW, 2 * WINDOW), 1)
    mask = (sj > qi) & (sj <= qi + WINDOW) & ((i > 0) | (sj >= WINDOW))
    return kb, vb, mask, pstart, cstart


def _swa_probs(qh, kh, mask, sink):
    logits = jnp.where(mask, _dg(qh, kh, 1, 1), -1e30)
    m = jnp.maximum(jnp.max(logits, axis=-1, keepdims=True), sink)
    p = jnp.exp(logits - m)
    ps = jnp.exp(sink - m)
    inv = 1.0 / (jnp.sum(p, axis=-1, keepdims=True) + ps)
    return p * inv, ps * inv


def _swa_fwd(q, k, v, sinks, name):
    S = q.shape[0]

    def body(i, q_ref, k_ref, v_ref, s_ref, o_ref):
        kb, vb, mask, _, _ = _swa_band(i, k_ref, v_ref)
        qv = q_ref[...]
        for h in range(SWA_H):
            kv = h // (SWA_H // SWA_KV)
            ksl = slice(kv * SWA_HD, (kv + 1) * SWA_HD)
            hsl = slice(h * SWA_HD, (h + 1) * SWA_HD)
            pn, _ = _swa_probs(qv[:, hsl], kb[:, ksl], mask, s_ref[0, h])
            o_ref[:, hsl] = _dg(pn.astype(BF16), vb[:, ksl], 1, 0).astype(BF16)

    return _tok_call(body, name, S, WINDOW, [(q, MIX, 0)], [k, v], [(MIX, BF16)], smem_in=[sinks])[0]


def _swa_bwd(q, k, v, sinks, dout, name):
    S = q.shape[0]

    def body(i, q_ref, do_ref, k_ref, v_ref, s_ref, dq_ref, dk_ref, dv_ref, ds_ref):
        kb, vb, mask, pstart, cstart = _swa_band(i, k_ref, v_ref)
        qv, dov = q_ref[...], do_ref[...]
        lane = lax.broadcasted_iota(jnp.int32, (1, 128), 1)
        dsink = jnp.zeros((1, 128), F32)
        dkb, dvb = [], []
        for kv in range(SWA_KV):
            ksl = slice(kv * SWA_HD, (kv + 1) * SWA_HD)
            dk_kv = jnp.zeros((2 * WINDOW, SWA_HD), F32)
            dv_kv = jnp.zeros((2 * WINDOW, SWA_HD), F32)
            for gq in range(SWA_H // SWA_KV):
                h = kv * (SWA_H // SWA_KV) + gq
                hsl = slice(h * SWA_HD, (h + 1) * SWA_HD)
                qh, doh = qv[:, hsl], dov[:, hsl].astype(BF16)
                pn, psn = _swa_probs(qh, kb[:, ksl], mask, s_ref[0, h])
                dp = _dg(doh, vb[:, ksl], 1, 1)
                delta = jnp.sum(dp * pn, axis=-1, keepdims=True)
                dsc = (pn * (dp - delta)).astype(BF16)
                dq_ref[:, hsl] = _dg(dsc, kb[:, ksl], 1, 0)
                dk_kv = dk_kv + _dg(dsc, qh, 0, 0)
                dv_kv = dv_kv + _dg(pn.astype(BF16), doh, 0, 0)
                dsink = dsink + jnp.where(lane == h, -jnp.sum(psn * delta, axis=0, keepdims=True), 0.0)
            dkb.append(dk_kv)
            dvb.append(dv_kv)
        dkb = jnp.concatenate(dkb, axis=1)
        dvb = jnp.concatenate(dvb, axis=1)

        @pl.when(i == 0)
        def _():
            dk_ref[...] = jnp.zeros_like(dk_ref)
            dv_ref[...] = jnp.zeros_like(dv_ref)

        dk_ref[pl.ds(pstart, WINDOW), :] += dkb[:WINDOW]
        dv_ref[pl.ds(pstart, WINDOW), :] += dvb[:WINDOW]
        dk_ref[pl.ds(cstart, WINDOW), :] += dkb[WINDOW:]
        dv_ref[pl.ds(cstart, WINDOW), :] += dvb[WINDOW:]
        _acc(ds_ref, dsink, i)

    return _tok_call(body, name, S, WINDOW, [(q, MIX, 0), (dout, MIX, 0)], [k, v], [(MIX, F32)],
                     [((S, 128), F32), ((S, 128), F32), ((1, 128), F32)], smem_in=[sinks])


def _shift_rows(xs, k):
    return xs if k == 0 else pltpu.roll(xs, k, 0)


def _dn_conv(x_ref, p_ref, w_ref, i):
    halo = jnp.where(i > 0, p_ref[...], 0.0)
    xs = jnp.concatenate([halo, x_ref[...]], axis=0)
    sh = [_shift_rows(xs, DN_CONV - 1 - t)[8:] for t in range(DN_CONV)]
    pre = sh[0] * w_ref[0:1, :]
    for t in range(1, DN_CONV):
        pre = pre + sh[t] * w_ref[t:t + 1, :]
    return pre, sh


def _dn_gates(sm, alog, dtb):
    lane = lax.broadcasted_iota(jnp.int32, sm.shape, 1)
    return jnp.where(lane < DN_H, _sigmoid(sm), -jnp.exp(alog) * _softplus(sm + dtb))


def _dn_pre_fwd(proj, conv_w, alog_l, dtb_l, name):
    S = proj.shape[0]
    scale = DN_HD ** -0.5

    def body(i, x_ref, sm_ref, p_ref, w_ref, al_ref, db_ref, q_ref, k_ref, v_ref, bg_ref):
        pre, _ = _dn_conv(x_ref, p_ref, w_ref, i)
        a = _silu(pre)
        for h in range(DN_H):
            sl = slice(h * DN_HD, (h + 1) * DN_HD)
            qh, kh = a[:, sl], a[:, MIX + h * DN_HD:MIX + (h + 1) * DN_HD]
            q_ref[:, sl] = qh * (lax.rsqrt(jnp.sum(qh * qh, axis=-1, keepdims=True) + EPS) * scale)
            k_ref[:, sl] = kh * lax.rsqrt(jnp.sum(kh * kh, axis=-1, keepdims=True) + EPS)
        v_ref[...] = a[:, 2 * MIX:]
        bg_ref[...] = _dn_gates(sm_ref[...], al_ref[...], db_ref[...])

    TB = min(S, 256)
    return _tok_call(body, name, S, TB, [(proj, 3 * MIX, C_QKV // (3 * MIX)), (proj, 128, C_SM // 128)],
                     [conv_w, alog_l, dtb_l], [(MIX, F32), (MIX, F32), (MIX, F32), (128, F32)],
                     prev_in=[(proj, 3 * MIX, C_QKV // (3 * MIX))])


def _dn_pre_bwd1(proj, conv_w, alog_l, dtb_l, dq, dk, dv, dbg, name):
    S = proj.shape[0]
    scale = DN_HD ** -0.5

    def body(i, x_ref, sm_ref, dq_ref, dk_ref, dv_ref, dbg_ref, p_ref, w_ref, al_ref, db_ref, dpre_ref, dsm_ref,
             dw_ref, dal_ref, ddb_ref):
        pre, sh = _dn_conv(x_ref, p_ref, w_ref, i)
        a = _silu(pre)
        da_parts = []
        for part, (g_ref, sc) in enumerate(((dq_ref, scale), (dk_ref, 1.0))):
            for h in range(DN_H):
                xh = a[:, part * MIX + h * DN_HD:part * MIX + (h + 1) * DN_HD]
                rs = lax.rsqrt(jnp.sum(xh * xh, axis=-1, keepdims=True) + EPS)
                y = xh * rs
                dy = g_ref[:, h * DN_HD:(h + 1) * DN_HD] * sc
                da_parts.append(rs * (dy - y * jnp.sum(dy * y, axis=-1, keepdims=True)))
        da_parts.append(dv_ref[...])
        dpre = jnp.concatenate(da_parts, axis=1) * _dsilu(pre)
        dpre_ref[...] = dpre
        dw = jnp.concatenate([jnp.sum(dpre * sh[t], axis=0, keepdims=True) for t in range(DN_CONV)], axis=0)
        _acc(dw_ref, dw, i)
        sm, al, db, dbg_v = sm_ref[...], al_ref[...], db_ref[...], dbg_ref[...]
        lane = lax.broadcasted_iota(jnp.int32, sm.shape, 1)
        sg = _sigmoid(sm)
        gneg = -jnp.exp(al)
        is_g = (lane >= DN_H) & (lane < 2 * DN_H)
        d_al = jnp.where(is_g, dbg_v * gneg * _sigmoid(sm + db), 0.0)
        dsm_ref[...] = jnp.where(lane < DN_H, dbg_v * sg * (1.0 - sg), d_al).astype(BF16)
        _acc(ddb_ref, jnp.sum(d_al, axis=0, keepdims=True), i)
        _acc(dal_ref, jnp.sum(jnp.where(is_g, dbg_v * gneg * _softplus(sm + db), 0.0), axis=0, keepdims=True), i)

    TB = min(S, 256)
    return _tok_call(body, name, S, TB,
                     [(proj, 3 * MIX, C_QKV // (3 * MIX)), (proj, 128, C_SM // 128), (dq, MIX, 0), (dk, MIX, 0),
                      (dv, MIX, 0), (dbg, 128, 0)], [conv_w, alog_l, dtb_l],
                     [(3 * MIX, F32), (128, BF16)], [((DN_CONV, 3 * MIX), F32), ((1, 128), F32), ((1, 128), F32)],
                     prev_in=[(proj, 3 * MIX, C_QKV // (3 * MIX))])


def _dn_pre_bwd2(dpre, conv_w, name):
    S = dpre.shape[0]
    TB = min(S, 256)
    nb = S // TB

    def body(i, d_ref, n_ref, w_ref, o_ref):
        halo = jnp.where(i < nb - 1, n_ref[...], 0.0)
        ds = jnp.concatenate([d_ref[...], halo], axis=0)
        out = ds[:TB] * w_ref[DN_CONV - 1:DN_CONV, :]
        for t in range(DN_CONV - 1):
            k = DN_CONV - 1 - t
            out = out + pltpu.roll(ds, TB + 8 - k, 0)[:TB] * w_ref[t:t + 1, :]
        o_ref[...] = out.astype(BF16)

    return _tok_call(body, name, S, TB, [(dpre, 3 * MIX, 0)], [conv_w], [(3 * MIX, BF16)],
                     next_in=[(dpre, 3 * MIX, 0)])[0]


def _dn_chunk_terms(bg, h):
    C = DN_C
    beta = bg[:, h:h + 1]
    gb = jnp.broadcast_to(bg[:, DN_H + h:DN_H + h + 1], (C, C))
    ri = lax.broadcasted_iota(jnp.int32, (C, C), 0)
    ci = lax.broadcasted_iota(jnp.int32, (C, C), 1)
    tril, eye = ri >= ci, ri == ci
    gc_col = _dg_exact_lhs(tril, gb, 1, 0)
    gc_row = jnp.sum(jnp.where(eye, gc_col, 0.0), axis=0, keepdims=True)
    decay = jnp.exp(jnp.where(tril, gc_col - gc_row, -1e30))
    gci = gc_col[:, 0:1]
    gl = gc_col[C - 1:C, 0:1]
    return beta, decay, jnp.exp(gci), jnp.exp(gl - gci), jnp.exp(gl), tril, eye, ri, ci


def _dn_core_fwd(q, k, v, bg, name):
    S = q.shape[0]
    C = DN_C
    nt = S // C

    def kern(q_ref, k_ref, v_ref, bg_ref, o_ref, t_ref, uw_ref, vn_ref, st_ref, state):
        i = pl.program_id(0)

        @pl.when(i == 0)
        def _():
            state[...] = jnp.zeros_like(state)

        bg_v = bg_ref[...]
        for h in range(DN_H):
            sl = slice(h * DN_HD, (h + 1) * DN_HD)
            qh, kh, vh = q_ref[:, sl], k_ref[:, sl], v_ref[:, sl]
            beta, decay, e_gc, e_kd, cdec, tril, eye, ri, ci = _dn_chunk_terms(bg_v, h)
            kb = kh * beta
            lower = jnp.where(ri > ci, _mm_nt(kb, kh) * decay, 0.0)
            x = -lower
            tm = jnp.where(eye, 1.0, 0.0) + x
            p = x
            for _ in range(5):
                p = _dg3(p, p, 1, 0)
                tm = tm + _dg3(tm, p, 1, 0)
            rhs = jnp.concatenate([vh * beta, kb * e_gc], axis=1)
            sol = _dg3(tm, rhs, 1, 0)
            u, w = sol[:, :DN_HD], sol[:, DN_HD:]
            attn = _mm_nt(qh, kh) * decay
            s_in = state[sl, :]
            vnew = u - _mm(w, s_in)
            o_ref[:, sl] = _mm(qh * e_gc, s_in) + _mm(attn, vnew)
            state[sl, :] = s_in * cdec + _mm_tn(kh * e_kd, vnew)
            st_ref[sl, :] = s_in
            t_ref[:, h * C:(h + 1) * C] = tm
            uw_ref[:, sl] = u
            uw_ref[:, MIX + h * DN_HD:MIX + (h + 1) * DN_HD] = w
            vn_ref[:, sl] = vnew

    tok = lambda w: pl.BlockSpec((C, w), lambda i: (i, 0))
    return pl.pallas_call(
        kern, name=name, grid=(nt,), in_specs=[tok(MIX), tok(MIX), tok(MIX), tok(128)],
        out_specs=[tok(MIX), tok(DN_H * C), tok(2 * MIX), tok(MIX), pl.BlockSpec((DN_H * DN_HD, DN_HD), lambda i: (i, 0))],
        out_shape=[jax.ShapeDtypeStruct((S, MIX), F32), jax.ShapeDtypeStruct((S, DN_H * C), F32),
                   jax.ShapeDtypeStruct((S, 2 * MIX), F32), jax.ShapeDtypeStruct((S, MIX), F32),
                   jax.ShapeDtypeStruct((nt * DN_H * DN_HD, DN_HD), F32)],
        scratch_shapes=[pltpu.VMEM((DN_H * DN_HD, DN_HD), F32)],
        compiler_params=_cparams(("arbitrary",)),
    )(q, k, v, bg)


def _dn_core_bwd(q, k, v, bg, tm, uw, vn, st, do, name):
    S = q.shape[0]
    C = DN_C
    nt = S // C

    def kern(q_ref, k_ref, v_ref, bg_ref, t_ref, uw_ref, vn_ref, st_ref, do_ref, dq_ref, dk_ref, dv_ref, dbg_ref,
             dstate):
        i = pl.program_id(0)

        @pl.when(i == 0)
        def _():
            dstate[...] = jnp.zeros_like(dstate)

        bg_v = bg_ref[...]
        lane = lax.broadcasted_iota(jnp.int32, (C, 128), 1)
        dbg = jnp.zeros((C, 128), F32)
        for h in range(DN_H):
            sl = slice(h * DN_HD, (h + 1) * DN_HD)
            qh, kh, vh, doh = q_ref[:, sl], k_ref[:, sl], v_ref[:, sl], do_ref[:, sl]
            beta, decay, e_gc, e_kd, cdec, tril, eye, ri, ci = _dn_chunk_terms(bg_v, h)
            th = t_ref[:, h * C:(h + 1) * C]
            u, w = uw_ref[:, sl], uw_ref[:, MIX + h * DN_HD:MIX + (h + 1) * DN_HD]
            vnew, s_in, ds_o = vn_ref[:, sl], st_ref[sl, :], dstate[sl, :]
            kb = kh * beta
            kk = _mm_nt(kb, kh)
            attn = _mm_nt(qh, kh) * decay
            qd, kd = qh * e_gc, kh * e_kd
            d_vnew = _mm_tn(attn, doh) + _mm(kd, ds_o)
            d_qd = _mm_nt(doh, s_in)
            d_attn = _mm_nt(doh, vnew)
            d_kd = _mm_nt(vnew, ds_o)
            d_c = jnp.sum(jnp.sum(ds_o * s_in, axis=1, keepdims=True), axis=0, keepdims=True)
            d_w = -_mm_nt(d_vnew, s_in)
            dstate[sl, :] = ds_o * cdec + _mm_tn(qd, doh) - _mm_tn(w, d_vnew)
            d_rhs = _dg3(th, jnp.concatenate([d_vnew, d_w], axis=1), 0, 0)
            d_a = -_dg3(d_rhs, jnp.concatenate([u, w], axis=1), 1, 1)
            d_lower = jnp.where(ri > ci, d_a, 0.0)
            d_vb, dz = d_rhs[:, :DN_HD], d_rhs[:, DN_HD:]
            dv_ref[:, sl] = d_vb * beta
            d_beta = jnp.sum(d_vb * vh, axis=-1, keepdims=True)
            d_kb = dz * e_gc
            d_gc = jnp.sum(dz * kb, axis=-1, keepdims=True) * e_gc
            d_kk = d_lower * decay
            d_qk = d_attn * decay
            dm = d_kk * kk + d_attn * attn
            d_kb = d_kb + _mm(d_kk, kh)
            d_k = _mm_tn(d_kk, kb) + _mm_tn(d_qk, qh)
            d_q = _mm(d_qk, kh) + d_qd * e_gc
            colsum = jnp.sum(dm, axis=0, keepdims=True)
            d_gc = d_gc + jnp.sum(dm, axis=-1, keepdims=True) - jnp.sum(jnp.where(eye, colsum, 0.0), axis=-1, keepdims=True)
            d_gc = d_gc + jnp.sum(d_qd * qd, axis=-1, keepdims=True)
            t_kd = jnp.sum(d_kd * kd, axis=-1, keepdims=True)
            d_gl = jnp.sum(t_kd, axis=0, keepdims=True) + d_c * cdec
            d_gc = d_gc - t_kd + jnp.where(ri[:, 0:1] == C - 1, d_gl, 0.0)
            d_k = d_k + d_kd * e_kd + d_kb * beta
            d_beta = d_beta + jnp.sum(d_kb * kh, axis=-1, keepdims=True)
            d_g = _dg_exact_lhs(ri <= ci, jnp.broadcast_to(d_gc, (C, 128)), 1, 0)
            dq_ref[:, sl] = d_q
            dk_ref[:, sl] = d_k
            dbg = dbg + jnp.where(lane == h, d_beta, 0.0) + jnp.where(lane == DN_H + h, d_g, 0.0)
        dbg_ref[...] = dbg

    tok = lambda w: pl.BlockSpec((C, w), lambda i: (nt - 1 - i, 0))
    return pl.pallas_call(
        kern, name=name, grid=(nt,),
        in_specs=[tok(MIX), tok(MIX), tok(MIX), tok(128), tok(DN_H * C), tok(2 * MIX), tok(MIX),
                  pl.BlockSpec((DN_H * DN_HD, DN_HD), lambda i: (nt - 1 - i, 0)), tok(MIX)],
        out_specs=[tok(MIX), tok(MIX), tok(MIX), tok(128)],
        out_shape=[jax.ShapeDtypeStruct((S, MIX), F32)] * 3 + [jax.ShapeDtypeStruct((S, 128), F32)],
        scratch_shapes=[pltpu.VMEM((DN_H * DN_HD, DN_HD), F32)],
        compiler_params=_cparams(("arbitrary",)),
    )(q, k, v, bg, tm, uw, vn, st, do)


def _dn_post_fwd(o, proj, ng, name):
    S = o.shape[0]

    def body(i, o_ref, z_ref, g_ref, out_ref):
        gv = g_ref[...]
        for h in range(DN_H):
            sl = slice(h * DN_HD, (h + 1) * DN_HD)
            oh = o_ref[:, sl]
            r = lax.rsqrt(jnp.mean(oh * oh, axis=-1, keepdims=True) + EPS)
            out_ref[:, sl] = (oh * r * gv * _silu(z_ref[:, sl])).astype(BF16)

    return _tok_call(body, name, S, min(S, 512), [(o, MIX, 0), (proj, MIX, C_ZC // MIX)], [ng], [(MIX, BF16)])[0]


def _dn_post_bwd(o, proj, ng, dout, name):
    S = o.shape[0]

    def body(i, o_ref, z_ref, do_ref, g_ref, dov_ref, dz_ref, dg_ref):
        gv = g_ref[...]
        dg = jnp.zeros((1, DN_HD), F32)
        for h in range(DN_H):
            sl = slice(h * DN_HD, (h + 1) * DN_HD)
            oh, zh, dh = o_ref[:, sl], z_ref[:, sl], do_ref[:, sl].astype(F32)
            r = lax.rsqrt(jnp.mean(oh * oh, axis=-1, keepdims=True) + EPS)
            dz_ref[:, sl] = (dh * oh * r * gv * _dsilu(zh)).astype(BF16)
            dx, dgh = _rms_bwd_vals(oh, gv, dh * _silu(zh))
            dov_ref[:, sl] = dx
            dg = dg + dgh
        _acc(dg_ref, dg, i)

    return _tok_call(body, name, S, min(S, 512), [(o, MIX, 0), (proj, MIX, C_ZC // MIX), (dout, MIX, 0)], [ng],
                     [(MIX, F32), (MIX, BF16)], [((1, DN_HD), F32)])


def _layer_params(w, l):
    lane = jnp.arange(128)
    is_g = (lane >= DN_H) & (lane < 2 * DN_H)
    spread = lambda t: jnp.where(is_g, jnp.tile(t, 128 // DN_H), 0.0).reshape(1, 128)
    tril = jnp.tril(jnp.ones((SGU_T, SGU_T), bool))
    return dict(
        win=w["w_in"][l], wb=w["w_branch"][l], wout=w["w_out"][l], wgu=w["w_gate_up"][l], wdown=w["w_down"][l],
        conv=w["dn_conv_w"][l], attn_norm=w["attn_norm"][l].reshape(1, -1), ffn_norm=w["ffn_norm"][l].reshape(1, -1),
        lg=w["sgu_ln_g"][l].reshape(1, -1), lb=w["sgu_ln_b"][l].reshape(1, -1),
        wc=jnp.where(tril, w["sgu_w"][l], 0.0).reshape(SGU_G * SGU_T, SGU_T), bst=w["sgu_b"][l].T,
        sinks=w["attn_sinks"][l].reshape(1, -1), alog=spread(w["dn_a_log"][l]), dtb=spread(w["dn_dt_bias"][l]),
        ng=w["dn_norm"][l].reshape(1, -1))


def _layer_fwd(x, p, cos, sin, l):
    n = lambda s: f"l{l}_{s}"
    h = _rms_fwd(x, p["attn_norm"], n("rms1"))
    proj = _matmul(h, p["win"], name=n("mm_in"))
    out_a = _sgu_fwd(proj, p["lg"], p["lb"], p["wc"], p["bst"], n("sgu_fwd"))
    qr, kr, vr = _rope_fwd(proj, cos, sin, n("rope_fwd"))
    out_b = _swa_fwd(qr, kr, vr, p["sinks"], n("swa_fwd"))
    q, k, v, bg = _dn_pre_fwd(proj, p["conv"], p["alog"], p["dtb"], n("dn_pre_fwd"))
    o, tm, uw, vn, st = _dn_core_fwd(q, k, v, bg, n("dn_core_fwd"))
    out_c = _dn_post_fwd(o, proj, p["ng"], n("dn_post_fwd"))
    outs = (out_a, out_b, out_c)
    bds = [_matmul(outs[j], p["wb"][j], name=n(f"mm_branch{j}")) for j in range(3)]
    merged = _merge_fwd(proj, bds, n("merge_fwd"))
    x1 = _matmul(merged, p["wout"], add=x, name=n("mm_out"))
    h2 = _rms_fwd(x1, p["ffn_norm"], n("rms2"))
    gu = _matmul(h2, p["wgu"], name=n("mm_gu"))
    act = _swiglu_fwd(gu, n("swiglu_fwd"))
    x2 = _matmul(act, p["wdown"], add=x1, name=n("mm_down"))
    saved = dict(x=x, h=h, proj=proj, outs=outs, qr=qr, kr=kr, vr=vr, q=q, k=k, v=v, bg=bg, o=o, tm=tm, uw=uw, vn=vn,
                 st=st, bds=bds, merged=merged, x1=x1, h2=h2, gu=gu, act=act)
    return x2, saved


def _layer_bwd(dx2, s, p, cos, sin, l):
    n = lambda t: f"l{l}_{t}"
    proj = s["proj"]
    g = {}
    g["w_down"] = _matmul(s["act"], dx2, ta=True, out_dtype=BF16, name=n("wg_down"))
    dact = _matmul(dx2, p["wdown"], tb=True, name=n("dg_down"))
    dgu = _swiglu_bwd(s["gu"], dact, n("swiglu_bwd"))
    g["w_gate_up"] = _matmul(s["h2"], dgu, ta=True, out_dtype=BF16, name=n("wg_gu"))
    dh2 = _matmul(dgu, p["wgu"], tb=True, name=n("dg_gu"))
    dx1, g["ffn_norm"] = _rms_bwd_add(s["x1"], p["ffn_norm"], dh2, dx2, n("rms2_bwd"))
    g["w_out"] = _matmul(s["merged"], dx1, ta=True, out_dtype=BF16, name=n("wg_out"))
    dm = _matmul(dx1, p["wout"], tb=True, name=n("dg_out"))
    dbd0, dbd1, dbd2, dgp = _merge_bwd(proj, s["bds"], dm, n("merge_bwd"))
    dbds = (dbd0, dbd1, dbd2)
    g["w_branch"] = jnp.stack([_matmul(s["outs"][j], dbds[j], ta=True, out_dtype=BF16, name=n(f"wg_branch{j}"))
                               for j in range(3)])
    douts = [_matmul(dbds[j], p["wb"][j], tb=True, name=n(f"dg_branch{j}")) for j in range(3)]
    dua, dva, g["sgu_ln_g"], g["sgu_ln_b"], dwc, dbs = _sgu_bwd(proj, p["lg"], p["lb"], p["wc"], p["bst"], douts[0],
                                                                n("sgu_bwd"))
    g["sgu_w"] = dwc.reshape(SGU_G, SGU_T, SGU_T)
    g["sgu_b"] = dbs.T
    dqr, dkr, dvr, dsink = _swa_bwd(s["qr"], s["kr"], s["vr"], p["sinks"], douts[1], n("swa_bwd"))
    g["attn_sinks"] = dsink[0, :SWA_H]
    dqb, dkb, dvb = _rope_bwd(dqr, dkr, dvr, cos, sin, n("rope_bwd"))
    do, dz, dng = _dn_post_bwd(s["o"], proj, p["ng"], douts[2], n("dn_post_bwd"))
    g["dn_norm"] = dng[0]
    dq, dk, dv, dbg = _dn_core_bwd(s["q"], s["k"], s["v"], s["bg"], s["tm"], s["uw"], s["vn"], s["st"], do,
                                   n("dn_core_bwd"))
    dpre, dsm, g["dn_conv_w"], dal, ddb = _dn_pre_bwd1(proj, p["conv"], p["alog"], p["dtb"], dq, dk, dv, dbg,
                                                       n("dn_pre_bwd1"))
    g["dn_a_log"] = dal[0, DN_H:2 * DN_H]
    g["dn_dt_bias"] = ddb[0, DN_H:2 * DN_H]
    dqkv = _dn_pre_bwd2(dpre, p["conv"], n("dn_pre_bwd2"))
    dproj = jnp.concatenate([dgp, dqkv, dua, dva, dqb, dz, dkb, dvb, dsm], axis=1)
    g["w_in"] = _matmul(s["h"], dproj, ta=True, out_dtype=BF16, name=n("wg_in"))
    dh = _matmul(dproj, p["win"], tb=True, name=n("dg_in"))
    dx, g["attn_norm"] = _rms_bwd_add(s["x"], p["attn_norm"], dh, dx1, n("rms1_bwd"))
    g["attn_norm"], g["ffn_norm"] = g["attn_norm"][0], g["ffn_norm"][0]
    g["sgu_ln_g"], g["sgu_ln_b"] = g["sgu_ln_g"][0], g["sgu_ln_b"][0]
    return dx, g


def _local_step(x, positions, target, w):
    cos, sin = _rope_tables(positions)
    params = [_layer_params(w, l) for l in range(DEPTH)]
    saves, xs = [], x
    for l in range(DEPTH):
        xs, sv = _layer_fwd(xs, params[l], cos, sin, l)
        saves.append(sv)
    dx, loss_row, dgf = _final_loss(xs, w["final_norm"].reshape(1, -1), target)
    grads = [None] * DEPTH
    for l in reversed(range(DEPTH)):
        dx, grads[l] = _layer_bwd(dx, saves[l], params[l], cos, sin, l)
    stacked = {k: jnp.stack([grads[l][k] for l in range(DEPTH)]) for k in grads[0]}
    stacked["final_norm"] = dgf[0]
    return loss_row[0, 0], dx, stacked


MESH = pl.DeviceIdType.MESH
HBM_SPEC = pl.BlockSpec(memory_space=pltpu.HBM)
VMEM_SPEC = pl.BlockSpec(memory_space=pltpu.VMEM)
N_CHIPS = 4
FLIPS = tuple((fx, fy, fc) for fx in (0, 1) for fy in (0, 1) for fc in (0, 1))[1:]
BIG = ("w_in", "w_branch", "w_out", "w_gate_up", "w_down")
BIG_SPEC = {
    "w_in": dict(rows=1024, cols=1792, axis=1, keep=1730, down=8, up=4),
    "w_branch": dict(rows=1536, cols=256, axis=1, keep=256, down=2, up=1),
    "w_out": dict(rows=256, cols=1024, axis=0, keep=1024, down=1, up=1),
    "w_gate_up": dict(rows=1024, cols=1408, axis=1, keep=1408, down=8, up=4),
    "w_down": dict(rows=704, cols=1024, axis=0, keep=1024, down=4, up=2),
}
CONV_ROWS, CONV_COLS = DEPTH * DN_CONV, 3 * MIX // N_CHIPS


def _full_shape(k):
    sp = BIG_SPEC[k]
    return (sp["rows"], N_CHIPS * sp["cols"]) if sp["axis"] == 1 else (N_CHIPS * sp["rows"], sp["cols"])


def _chip_block(ref, k, s, layer=None):
    sp = BIG_SPEC[k]
    if sp["axis"] == 1:
        idx = (slice(None), pl.ds(pl.multiple_of(s * sp["cols"], 128), sp["cols"]))
    else:
        idx = (pl.ds(pl.multiple_of(s * sp["rows"], 16), sp["rows"]), slice(None))
    return ref.at[idx] if layer is None else ref.at[(layer,) + idx]


def _me():
    return lax.axis_index("x"), lax.axis_index("y"), lax.axis_index("c")


def _peer(x, y, c, flip):
    fx, fy, fc = flip
    return (1 - x if fx else x, 1 - y if fy else y, 1 - c if fc else c)


class _Copies:
    def __init__(self, send_sems, recv_sems):
        self.send_sems, self.recv_sems, self.k, self.sent, self.landing = send_sems, recv_sems, 0, [], []

    def _copy(self, k, src, dst, to):
        return pltpu.make_async_remote_copy(src_ref=src, dst_ref=dst, send_sem=self.send_sems.at[k],
                                            recv_sem=self.recv_sems.at[k], device_id=to, device_id_type=MESH)

    def send(self, src, dst, to, lands):
        k = self.k
        self.k += 1
        cp = self._copy(k, src, dst, to)
        cp.start()
        self.sent.append(cp)
        self.landing.append(self._copy(k, lands, lands, to))
        return k

    def wait_landed(self, k):
        self.landing[k].wait_recv()

    def finish(self, landed=()):
        for k, cp in enumerate(self.landing):
            if k not in landed:
                cp.wait_recv()
        for cp in self.sent:
            cp.wait_send()


def _allgather_weights(shards, conv):
    n = len(BIG)
    n_ici = 3 * n + 3

    def body(*refs):
        src = dict(zip(BIG, refs[:n]))
        conv_ref = refs[n]
        out = dict(zip(BIG, refs[n + 1:2 * n + 1]))
        conv_out, send_sems, recv_sems, local_sems = refs[2 * n + 1:]
        x, y, c = _me()
        me = 2 * x + y
        chips = [(1 - x, y), (x, 1 - y), (1 - x, 1 - y)]
        net = _Copies(send_sems, recv_sems)

        def conv_block(s):
            return conv_out.at[:, pl.ds(pl.multiple_of(s * CONV_COLS, 128), CONV_COLS)]

        local = [pltpu.make_async_copy(src[k].at[l], _chip_block(out[k], k, me, l), local_sems.at[2 * i + l])
                 for i, k in enumerate(BIG) for l in range(DEPTH)]
        local.append(pltpu.make_async_copy(conv_ref, conv_block(me), local_sems.at[2 * n]))
        for cp in local:
            cp.start()
        first = {}
        for k in BIG:
            for j, (px, py) in enumerate(chips):
                first[k, j] = net.send(src[k].at[c], _chip_block(out[k], k, me, c), (px, py, c),
                                       _chip_block(out[k], k, 2 * px + py, c))
        for px, py in chips:
            net.send(conv_ref, conv_block(me), (px, py, c), conv_block(2 * px + py))
        for k in BIG:
            for j, (px, py) in enumerate(chips):
                net.wait_landed(first[k, j])
                net.send(_chip_block(out[k], k, 2 * px + py, c), _chip_block(out[k], k, 2 * px + py, c), (x, y, 1 - c),
                         _chip_block(out[k], k, 2 * px + py, 1 - c))
        net.finish(landed=set(first.values()))
        for cp in local:
            cp.wait()

    out_shape = [jax.ShapeDtypeStruct((DEPTH,) + _full_shape(k), BF16) for k in BIG]
    out_shape.append(jax.ShapeDtypeStruct((CONV_ROWS, N_CHIPS * CONV_COLS), F32))
    outs = pl.pallas_call(
        body, name="allgather_weights", out_shape=out_shape, in_specs=[HBM_SPEC] * (n + 1), out_specs=[HBM_SPEC] * (n + 1),
        scratch_shapes=[pltpu.SemaphoreType.DMA((n_ici + 3 * n,)), pltpu.SemaphoreType.DMA((n_ici + 3 * n,)),
                        pltpu.SemaphoreType.DMA((2 * n + 1,))],
    )(*[shards[k] for k in BIG], conv)
    return dict(zip(BIG, outs[:n])), outs[n]


def _row_chunks(ref, rows, n, layer=None):
    step = rows // n
    sl = [pl.ds(i * step, step) for i in range(n)]
    return [ref.at[s, :] if layer is None else ref.at[layer, s, :] for s in sl]


def _grads_to_sibling(grads):
    n = len(BIG)
    n_sem = sum(BIG_SPEC[k]["down"] for k in BIG)

    def body(*refs):
        g = dict(zip(BIG, refs[:n]))
        out = dict(zip(BIG, refs[n:2 * n]))
        send_sems, recv_sems = refs[2 * n:]
        x, y, c = _me()
        net = _Copies(send_sems, recv_sems)
        for k in BIG:
            rows, nch = _full_shape(k)[0], BIG_SPEC[k]["down"]
            for s, d in zip(_row_chunks(g[k], rows, nch, 1 - c), _row_chunks(out[k], rows, nch)):
                net.send(s, d, (x, y, 1 - c), d)
        net.finish()

    outs = pl.pallas_call(
        body, name="grads_to_sibling", out_shape=[jax.ShapeDtypeStruct(_full_shape(k), BF16) for k in BIG],
        in_specs=[HBM_SPEC] * n, out_specs=[HBM_SPEC] * n,
        scratch_shapes=[pltpu.SemaphoreType.DMA((n_sem,)), pltpu.SemaphoreType.DMA((n_sem,))],
    )(*[grads[k] for k in BIG])
    return dict(zip(BIG, outs))


def _add_layer(g2, other, layer, name):
    _, rows, cols = g2.shape
    tr = _pick(rows, (256, 128))

    def kern(l_ref, a_ref, b_ref, o_ref):
        o_ref[...] = (a_ref[0].astype(F32) + b_ref[...].astype(F32)).astype(BF16)

    return pl.pallas_call(
        kern, name=name, out_shape=jax.ShapeDtypeStruct((rows, cols), BF16),
        grid_spec=pltpu.PrefetchScalarGridSpec(
            num_scalar_prefetch=1, grid=(rows // tr,),
            in_specs=[pl.BlockSpec((1, tr, cols), lambda i, l: (l[0], i, 0)), pl.BlockSpec((tr, cols), lambda i, l: (i, 0))],
            out_specs=pl.BlockSpec((tr, cols), lambda i, l: (i, 0))),
        compiler_params=_cparams(("parallel",)),
    )(layer, g2, other)


def _scatter_chip_sums(sums):
    n = len(BIG)

    def body(*refs):
        src = dict(zip(BIG, refs[:n]))
        out = dict(zip(BIG, refs[n:2 * n]))
        send_sems, recv_sems, local_sems = refs[2 * n:]
        x, y, c = _me()
        me = 2 * x + y
        net = _Copies(send_sems, recv_sems)
        local = [pltpu.make_async_copy(_chip_block(src[k], k, me), out[k].at[me], local_sems.at[i])
                 for i, k in enumerate(BIG)]
        for cp in local:
            cp.start()
        for k in BIG:
            for px, py in [(1 - x, y), (x, 1 - y), (1 - x, 1 - y)]:
                net.send(_chip_block(src[k], k, 2 * px + py), out[k].at[me], (px, py, c), out[k].at[2 * px + py])
        net.finish()
        for cp in local:
            cp.wait()

    outs = pl.pallas_call(
        body, name="scatter_chip_sums",
        out_shape=[jax.ShapeDtypeStruct((N_CHIPS, BIG_SPEC[k]["rows"], BIG_SPEC[k]["cols"]), BF16) for k in BIG],
        in_specs=[HBM_SPEC] * n, out_specs=[HBM_SPEC] * n,
        scratch_shapes=[pltpu.SemaphoreType.DMA((3 * n,)), pltpu.SemaphoreType.DMA((3 * n,)),
                        pltpu.SemaphoreType.DMA((n,))],
    )(*[sums[k] for k in BIG])
    return dict(zip(BIG, outs))


def _sum_chips(parts, keep, name):
    _, rows, cols = parts.shape
    tr = _pick(rows, (256, 64))

    def kern(p_ref, o_ref):
        tot = p_ref[0].astype(F32)
        for s in range(1, N_CHIPS):
            tot = tot + p_ref[s].astype(F32)
        o_ref[...] = tot[:, :keep]

    return pl.pallas_call(
        kern, name=name, grid=(rows // tr,), in_specs=[pl.BlockSpec((N_CHIPS, tr, cols), lambda i: (0, i, 0))],
        out_specs=pl.BlockSpec((tr, keep), lambda i: (i, 0)), out_shape=jax.ShapeDtypeStruct((rows, keep), F32),
        compiler_params=_cparams(("parallel",)),
    )(parts)


def _exchange_layers(red):
    n = len(BIG)
    n_sem = sum(BIG_SPEC[k]["up"] for k in BIG)

    def body(*refs):
        src = dict(zip(BIG, refs[:n]))
        out = dict(zip(BIG, refs[n:2 * n]))
        send_sems, recv_sems, local_sems = refs[2 * n:]
        x, y, c = _me()
        net = _Copies(send_sems, recv_sems)
        local = [pltpu.make_async_copy(src[k], out[k].at[c], local_sems.at[i]) for i, k in enumerate(BIG)]
        for cp in local:
            cp.start()
        for k in BIG:
            rows, nch = BIG_SPEC[k]["rows"], BIG_SPEC[k]["up"]
            for s, d, l in zip(_row_chunks(src[k], rows, nch), _row_chunks(out[k], rows, nch, c),
                               _row_chunks(out[k], rows, nch, 1 - c)):
                net.send(s, d, (x, y, 1 - c), l)
        net.finish()
        for cp in local:
            cp.wait()

    outs = pl.pallas_call(
        body, name="exchange_layers",
        out_shape=[jax.ShapeDtypeStruct((DEPTH, BIG_SPEC[k]["rows"], BIG_SPEC[k]["keep"]), F32) for k in BIG],
        in_specs=[HBM_SPEC] * n, out_specs=[HBM_SPEC] * n,
        scratch_shapes=[pltpu.SemaphoreType.DMA((n_sem,)), pltpu.SemaphoreType.DMA((n_sem,)),
                        pltpu.SemaphoreType.DMA((n,))],
    )(*[red[k] for k in BIG])
    return dict(zip(BIG, outs))


def _adam_vals(g, w, m, v):
    m2 = ADAM_B1 * m + (1.0 - ADAM_B1) * g
    v2 = ADAM_B2 * v + (1.0 - ADAM_B2) * (g * g)
    m_hat = m2 / (1.0 - ADAM_B1 ** ADAM_STEP)
    v_hat = v2 / (1.0 - ADAM_B2 ** ADAM_STEP)
    return -ADAM_LR * (m_hat / (jnp.sqrt(v_hat) + ADAM_EPS) + ADAM_WD * w), m2, v2


def _allreduce_small_adam(gp, wp, mp, vp):
    rows = gp.shape[0]

    def body(g_ref, w_ref, m_ref, v_ref, gs_ref, d_ref, nm_ref, nv_ref, buf, send_sems, recv_sems):
        x, y, c = _me()
        buf[4 * x + 2 * y + c] = g_ref[...]

        def copy(k, to, src):
            sx, sy, sc = src
            return pltpu.make_async_remote_copy(
                src_ref=g_ref, dst_ref=buf.at[4 * sx + 2 * sy + sc], send_sem=send_sems.at[k],
                recv_sem=recv_sems.at[k], device_id=to, device_id_type=MESH)

        sends = [copy(k, _peer(x, y, c, f), (x, y, c)) for k, f in enumerate(FLIPS)]
        for cp in sends:
            cp.start()
        for k, f in enumerate(FLIPS):
            copy(k, (x, y, c), _peer(x, y, c, f)).wait_recv()
        for cp in sends:
            cp.wait_send()
        tot = buf[0]
        for d in range(1, 8):
            tot = tot + buf[d]
        gs_ref[...] = tot
        d_ref[...], nm_ref[...], nv_ref[...] = _adam_vals(tot, w_ref[...], m_ref[...], v_ref[...])

    return pl.pallas_call(
        body, name="allreduce_small", out_shape=[jax.ShapeDtypeStruct((rows, 128), F32)] * 4,
        in_specs=[VMEM_SPEC] * 4, out_specs=[VMEM_SPEC] * 4,
        scratch_shapes=[pltpu.VMEM((8, rows, 128), F32), pltpu.SemaphoreType.DMA((7,)), pltpu.SemaphoreType.DMA((7,))],
        compiler_params=pltpu.CompilerParams(vmem_limit_bytes=VMEM_LIMIT),
    )(gp, wp, mp, vp)


def _adam(g, w, m, v, name):
    shape = w.shape
    cols = shape[-1]
    rows = w.size // cols
    tr = _pick(rows, (256, 8))
    spec = pl.BlockSpec((tr, cols), lambda i: (i, 0))

    def kern(g_ref, w_ref, m_ref, v_ref, d_ref, nm_ref, nv_ref):
        d_ref[...], nm_ref[...], nv_ref[...] = _adam_vals(g_ref[...], w_ref[...], m_ref[...], v_ref[...])

    outs = pl.pallas_call(
        kern, name=name, grid=(rows // tr,), in_specs=[spec] * 4, out_specs=[spec] * 3,
        out_shape=[jax.ShapeDtypeStruct((rows, cols), F32)] * 3, compiler_params=_cparams(("parallel",)),
    )(*[t.reshape(rows, cols) for t in (g, w, m, v)])
    return [o.reshape(shape) for o in outs]


SMALL = ("attn_norm", "sgu_ln_g", "sgu_ln_b", "sgu_w", "sgu_b", "attn_sinks", "dn_a_log", "dn_dt_bias", "dn_norm",
         "ffn_norm", "final_norm")
SMALL_ROWS = 1200


def _pack_small(vals, extra=()):
    flat = [vals[k].astype(F32).reshape(-1) for k in SMALL] + [e.astype(F32).reshape(-1) for e in extra]
    n = sum(f.shape[0] for f in flat)
    flat.append(jnp.zeros((SMALL_ROWS * 128 - n,), F32))
    return jnp.concatenate(flat).reshape(SMALL_ROWS, 128)


def _unpack_small(slab, shapes):
    flat = slab.reshape(-1)
    out, o = {}, 0
    for k in SMALL:
        n = math.prod(shapes[k])
        out[k] = flat[o:o + n].reshape(shapes[k])
        o += n
    return out, flat[o:]


def _permute_in_cols(w_in, dtype):
    parts = [w_in[..., a:a + n] for a, n in IN_PIECES]
    parts.append(jnp.zeros(w_in.shape[:-1] + (IN_PAD,), w_in.dtype))
    return jnp.concatenate(parts, axis=-1).astype(dtype)


def _unpermute_in_cols(g):
    offs, o = {}, 0
    for a, n in IN_PIECES:
        offs[a] = (o, n)
        o += n
    return jnp.concatenate([g[..., offs[a][0]:offs[a][0] + offs[a][1]] for a in sorted(offs)], axis=-1)


WEIGHTS = ("attn_norm", "w_in", "sgu_ln_g", "sgu_ln_b", "sgu_w", "sgu_b", "attn_sinks", "dn_conv_w", "dn_a_log",
           "dn_dt_bias", "dn_norm", "w_branch", "w_out", "ffn_norm", "w_gate_up", "w_down", "final_norm")
IN_SHARD = IN_COLS // N_CHIPS


def _drop_chip_pad(t):
    lead = t.shape[:-1]
    return t.reshape(lead + (N_CHIPS, BIG_SPEC["w_in"]["cols"]))[..., :IN_SHARD].reshape(lead + (IN_COLS,))


def _add_chip_pad(t):
    lead = t.shape[:-1]
    t = jnp.pad(t.reshape(lead + (N_CHIPS, IN_SHARD)), [(0, 0)] * len(lead) + [(0, 0), (0, BIG_SPEC["w_in"]["cols"] - IN_SHARD)])
    return t.reshape(lead + (N_CHIPS * BIG_SPEC["w_in"]["cols"],))


def kernel(x, positions, attn_norm, w_in, sgu_ln_g, sgu_ln_b, sgu_w, sgu_b, attn_sinks, dn_conv_w, dn_a_log, dn_dt_bias, dn_norm, w_branch, w_out, ffn_norm, w_gate_up, w_down, final_norm, loss_target, m_attn_norm, m_w_in, m_sgu_ln_g, m_sgu_ln_b, m_sgu_w, m_sgu_b, m_attn_sinks, m_dn_conv_w, m_dn_a_log, m_dn_dt_bias, m_dn_norm, m_w_branch, m_w_out, m_ffn_norm, m_w_gate_up, m_w_down, m_final_norm, v_attn_norm, v_w_in, v_sgu_ln_g, v_sgu_ln_b, v_sgu_w, v_sgu_b, v_attn_sinks, v_dn_conv_w, v_dn_a_log, v_dn_dt_bias, v_dn_norm, v_w_branch, v_w_out, v_ffn_norm, v_w_gate_up, v_w_down, v_final_norm):
    given = dict(locals())
    W = {k: given[k] for k in WEIGHTS}
    M = {k: given["m_" + k] for k in WEIGHTS}
    V = {k: given["v_" + k] for k in WEIGHTS}
    chip = 2 * lax.axis_index("x") + lax.axis_index("y")
    layer = lax.axis_index("c").astype(jnp.int32).reshape(1)

    in_pad = BIG_SPEC["w_in"]["cols"] - IN_SHARD
    shards = {k: W[k].astype(BF16) for k in BIG}
    shards["w_in"] = jnp.pad(shards["w_in"], ((0, 0), (0, 0), (0, in_pad)))
    shards["w_branch"] = shards["w_branch"].reshape(DEPTH, 3 * MIX, -1)
    full, conv_full = _allgather_weights(shards, dn_conv_w.reshape(CONV_ROWS, CONV_COLS))
    w = {k: W[k] for k in SMALL}
    w.update(full)
    w["w_in"] = _permute_in_cols(_drop_chip_pad(full["w_in"]), BF16)
    w["w_branch"] = full["w_branch"].reshape(DEPTH, 3, MIX, D_MODEL)
    w["dn_conv_w"] = conv_full.reshape(DEPTH, DN_CONV, 3 * MIX)

    loss, dx, g = _local_step(x[0], positions[0], loss_target[0], w)

    gb = {k: g[k] for k in BIG}
    gb["w_in"] = _add_chip_pad(_unpermute_in_cols(g["w_in"]))
    gb["w_branch"] = g["w_branch"].reshape(DEPTH, 3 * MIX, D_MODEL)
    sibling = _grads_to_sibling(gb)
    chip_sums = {k: _add_layer(gb[k], sibling[k], layer, "chip_sum_" + k) for k in BIG}
    parts = _scatter_chip_sums(chip_sums)
    reduced = _exchange_layers({k: _sum_chips(parts[k], BIG_SPEC[k]["keep"], "sum_" + k) for k in BIG})
    grads = {k: reduced[k].reshape(W[k].shape) for k in BIG}

    small_shapes = {k: W[k].shape for k in SMALL}
    gs, ds, nms, nvs = _allreduce_small_adam(_pack_small(g, (g["dn_conv_w"], loss.reshape(1))), _pack_small(W),
                                             _pack_small(M), _pack_small(V))
    gsm, rest = _unpack_small(gs, small_shapes)
    grads.update(gsm)
    n_conv = g["dn_conv_w"].size
    conv_full = rest[:n_conv].reshape(g["dn_conv_w"].shape)
    grads["dn_conv_w"] = lax.dynamic_slice_in_dim(conv_full, chip * dn_conv_w.shape[2], dn_conv_w.shape[2], axis=2)
    loss_total = rest[n_conv]
    delta, new_m, new_v = (_unpack_small(t, small_shapes)[0] for t in (ds, nms, nvs))
    for k in BIG + ("dn_conv_w",):
        delta[k], new_m[k], new_v[k] = _adam(grads[k], W[k], M[k], V[k], "adam_" + k)

    return (loss_total, dx[None], *[grads[k] for k in WEIGHTS], *[delta[k] for k in WEIGHTS],
            *[new_m[k] for k in WEIGHTS], *[new_v[k] for k in WEIGHTS])
```

```python
import functools
import math

import jax
import jax.numpy as jnp
from jax import lax
from jax.experimental import pallas as pl
from jax.experimental.pallas import tpu as pltpu

F32 = jnp.float32
BF16 = jnp.bfloat16
HI = lax.Precision.HIGHEST

D_MODEL = 1024
DEPTH = 2
MIX = 512
EPS = 1e-6
SGU_G, SGU_T = 4, 128
SWA_H, SWA_KV, SWA_HD, WINDOW = 8, 2, 64, 128
ROPE_THETA, ROPE_DIM = 500000.0, 16
DN_H, DN_HD, DN_CONV, DN_C = 4, 128, 4, 64
D_FF = 2816
IN_COLS = 6920
IN_PIECES = ((3848, 3072), (1792, 1536), (0, 512), (512, 512), (1024, 512), (3328, 512), (1536, 128), (1664, 128),
             (3840, 8))
IN_PAD = 120
IN_R = 7040
C_GATE, C_QKV, C_UA, C_VA, C_QB, C_ZC, C_KB, C_VB, C_SM = 0, 3072, 4608, 5120, 5632, 6144, 6656, 6784, 6912

ADAM_LR, ADAM_B1, ADAM_B2, ADAM_EPS, ADAM_WD, ADAM_STEP = 0.001, 0.9, 0.999, 1e-08, 0.01, 10
VMEM_LIMIT = 56 * 1024 * 1024


def _cparams(sem):
    return pltpu.CompilerParams(dimension_semantics=sem, vmem_limit_bytes=VMEM_LIMIT)


def _dg(a, b, ca, cb, prec=None):
    return lax.dot_general(a, b, (((ca,), (cb,)), ((), ())), precision=prec, preferred_element_type=F32)


def _split(x):
    hi = x.astype(BF16)
    return hi, (x - hi.astype(F32)).astype(BF16)


def _dg3(a, b, ca, cb):
    ah, al = _split(a)
    bh, bl = _split(b)
    return _dg(ah, bh, ca, cb) + (_dg(ah, bl, ca, cb) + _dg(al, bh, ca, cb))


def _dg_exact_lhs(a01, b, ca, cb):
    a = a01.astype(BF16)
    b1 = b.astype(BF16)
    r1 = b - b1.astype(F32)
    b2 = r1.astype(BF16)
    b3 = (r1 - b2.astype(F32)).astype(BF16)
    return _dg(a, b1, ca, cb) + (_dg(a, b2, ca, cb) + _dg(a, b3, ca, cb))


def _mm(a, b):
    return _dg(a.astype(BF16), b.astype(BF16), 1, 0)


def _mm_nt(a, b):
    return _dg(a.astype(BF16), b.astype(BF16), 1, 1)


def _mm_tn(a, b):
    return _dg(a.astype(BF16), b.astype(BF16), 0, 0)


def _sigmoid(x):
    return 1.0 / (1.0 + jnp.exp(-x))


def _silu(x):
    return x * _sigmoid(x)


def _dsilu(x):
    s = _sigmoid(x)
    return s * (1.0 + x * (1.0 - s))


_GC = math.sqrt(2.0 / math.pi)


def _gelu(x):
    return 0.5 * x * (1.0 + jnp.tanh(_GC * (x + 0.044715 * x * x * x)))


def _dgelu(x):
    t = jnp.tanh(_GC * (x + 0.044715 * x * x * x))
    return 0.5 * (1.0 + t) + 0.5 * x * (1.0 - t * t) * _GC * (1.0 + 3.0 * 0.044715 * x * x)


def _softplus(x):
    return jnp.maximum(x, 0.0) + jnp.log(1.0 + jnp.exp(-jnp.abs(x)))


def _acc(ref, val, i):
    @pl.when(i == 0)
    def _():
        ref[...] = val

    @pl.when(i > 0)
    def _():
        ref[...] += val


def _tok_call(body, name, S, TB, tok_in, const_in=(), tok_out=(), acc_out=(), prev_in=(), next_in=(), smem_in=()):
    nb = S // TB
    r8 = TB // 8
    in_specs, args = [], []
    for a, w, cb in tok_in:
        in_specs.append(pl.BlockSpec((TB, w), functools.partial(lambda i, cb: (i, cb), cb=cb)))
        args.append(a)
    for a, w, cb in prev_in:
        in_specs.append(pl.BlockSpec((8, w), functools.partial(lambda i, cb: (jnp.maximum(i * r8 - 1, 0), cb), cb=cb)))
        args.append(a)
    for a, w, cb in next_in:
        in_specs.append(pl.BlockSpec((8, w), functools.partial(
            lambda i, cb: (jnp.minimum((i + 1) * r8, S // 8 - 1), cb), cb=cb)))
        args.append(a)
    for a in const_in:
        in_specs.append(pl.BlockSpec(a.shape, lambda i: (0, 0)))
        args.append(a)
    for a in smem_in:
        in_specs.append(pl.BlockSpec(memory_space=pltpu.SMEM))
        args.append(a)
    out_specs, out_shape = [], []
    for w, dt in tok_out:
        out_specs.append(pl.BlockSpec((TB, w), lambda i: (i, 0)))
        out_shape.append(jax.ShapeDtypeStruct((S, w), dt))
    for shp, dt in acc_out:
        out_specs.append(pl.BlockSpec(shp, lambda i: (0, 0)))
        out_shape.append(jax.ShapeDtypeStruct(shp, dt))

    def kern(*refs):
        body(pl.program_id(0), *refs)

    return pl.pallas_call(
        kern, name=name, grid=(nb,), in_specs=in_specs, out_specs=out_specs, out_shape=out_shape,
        compiler_params=_cparams(("arbitrary",)),
    )(*args)


MM_BLOCKS = (1024, 1408, 640, 512, 256, 128)


def _pick(n, cands):
    for c in cands:
        if n % c == 0:
            return c
    return n


def _matmul(a, b, *, ta=False, tb=False, add=None, out_dtype=F32, name):
    M, K = (a.shape[1], a.shape[0]) if ta else a.shape
    N = b.shape[0] if tb else b.shape[1]
    bm, bn, bk = _pick(M, MM_BLOCKS), _pick(N, MM_BLOCKS), _pick(K, MM_BLOCKS)
    nk = K // bk
    a_spec = pl.BlockSpec((bk, bm), lambda i, j, k: (k, i)) if ta else pl.BlockSpec((bm, bk), lambda i, j, k: (i, k))
    b_spec = pl.BlockSpec((bn, bk), lambda i, j, k: (j, k)) if tb else pl.BlockSpec((bk, bn), lambda i, j, k: (k, j))
    o_spec = pl.BlockSpec((bm, bn), lambda i, j, k: (i, j))
    ca, cb = (0 if ta else 1), (1 if tb else 0)

    def kern(*refs):
        if add is None:
            a_ref, b_ref, o_ref, acc_ref = refs
        else:
            a_ref, b_ref, add_ref, o_ref, acc_ref = refs
        k = pl.program_id(2)
        p = _dg(a_ref[...].astype(BF16), b_ref[...].astype(BF16), ca, cb)

        @pl.when(k == 0)
        def _():
            acc_ref[...] = p

        @pl.when(k > 0)
        def _():
            acc_ref[...] += p

        @pl.when(k == nk - 1)
        def _():
            r = acc_ref[...]
            if add is not None:
                r = r + add_ref[...].astype(F32)
            o_ref[...] = r.astype(out_dtype)

    in_specs = [a_spec, b_spec] + ([o_spec] if add is not None else [])
    args = (a, b) + ((add,) if add is not None else ())
    return pl.pallas_call(
        kern, name=name, grid=(M // bm, N // bn, nk), in_specs=in_specs, out_specs=o_spec,
        out_shape=jax.ShapeDtypeStruct((M, N), out_dtype), scratch_shapes=[pltpu.VMEM((bm, bn), F32)],
        compiler_params=_cparams(("parallel", "parallel", "arbitrary")),
    )(*args)


def _rms_fwd(x, g, name):
    S = x.shape[0]

    def body(i, x_ref, g_ref, h_ref):
        xv = x_ref[...]
        r = lax.rsqrt(jnp.mean(xv * xv, axis=-1, keepdims=True) + EPS)
        h_ref[...] = (xv * r * g_ref[...]).astype(BF16)

    return _tok_call(body, name, S, min(S, 512), [(x, D_MODEL, 0)], [g], [(D_MODEL, BF16)])[0]


def _rms_bwd_vals(xv, g, dh):
    r = lax.rsqrt(jnp.mean(xv * xv, axis=-1, keepdims=True) + EPS)
    u = dh * g
    dx = r * u - xv * (r * r * r) * jnp.mean(u * xv, axis=-1, keepdims=True)
    dg = jnp.sum(dh * xv * r, axis=0, keepdims=True)
    return dx, dg


def _rms_bwd_add(x, g, dh, dres, name):
    S = x.shape[0]

    def body(i, x_ref, dh_ref, dr_ref, g_ref, dx_ref, dg_ref):
        dx, dg = _rms_bwd_vals(x_ref[...], g_ref[...], dh_ref[...].astype(F32))
        dx_ref[...] = dr_ref[...] + dx
        _acc(dg_ref, dg, i)

    return _tok_call(body, name, S, min(S, 512), [(x, D_MODEL, 0), (dh, D_MODEL, 0), (dres, D_MODEL, 0)], [g],
                     [(D_MODEL, F32)], [((1, D_MODEL), F32)])


def _final_loss(x, g, target):
    S = x.shape[0]

    def body(i, x_ref, t_ref, g_ref, dx_ref, loss_ref, dg_ref):
        xv, gv = x_ref[...], g_ref[...]
        r = lax.rsqrt(jnp.mean(xv * xv, axis=-1, keepdims=True) + EPS)
        e = xv * r * gv - t_ref[...]
        part = 0.5 * jnp.sum(jnp.mean(e * e, axis=-1, keepdims=True), axis=0, keepdims=True)
        dx, dg = _rms_bwd_vals(xv, gv, e * (1.0 / D_MODEL))
        dx_ref[...] = dx
        _acc(loss_ref, jnp.broadcast_to(part, (1, 128)), i)
        _acc(dg_ref, dg, i)

    return _tok_call(body, "final_loss", S, min(S, 512), [(x, D_MODEL, 0), (target, D_MODEL, 0)], [g],
                     [(D_MODEL, F32)], [((1, 128), F32), ((1, D_MODEL), F32)])


def _swiglu_fwd(gu, name):
    S = gu.shape[0]

    def body(i, gu_ref, a_ref):
        a_ref[...] = (_silu(gu_ref[:, :D_FF]) * gu_ref[:, D_FF:]).astype(BF16)

    return _tok_call(body, name, S, min(S, 256), [(gu, 2 * D_FF, 0)], [], [(D_FF, BF16)])[0]


def _swiglu_bwd(gu, dact, name):
    S = gu.shape[0]

    def body(i, gu_ref, da_ref, dgu_ref):
        gg, uu, da = gu_ref[:, :D_FF], gu_ref[:, D_FF:], da_ref[...]
        dgu_ref[:, :D_FF] = (da * uu * _dsilu(gg)).astype(BF16)
        dgu_ref[:, D_FF:] = (da * _silu(gg)).astype(BF16)

    return _tok_call(body, name, S, min(S, 256), [(gu, 2 * D_FF, 0), (dact, D_FF, 0)], [], [(2 * D_FF, BF16)])[0]


def _merge_fwd(proj, bds, name):
    S = proj.shape[0]

    def body(i, g0, g1, g2, b0, b1, b2, m_ref):
        m = _sigmoid(g0[...]) * b0[...] + _sigmoid(g1[...]) * b1[...] + _sigmoid(g2[...]) * b2[...]
        m_ref[...] = m.astype(BF16)

    tok = [(proj, D_MODEL, n) for n in range(3)] + [(b, D_MODEL, 0) for b in bds]
    return _tok_call(body, name, S, min(S, 512), tok, [], [(D_MODEL, BF16)])[0]


def _merge_bwd(proj, bds, dm, name):
    S = proj.shape[0]

    def body(i, g0, g1, g2, b0, b1, b2, dm_ref, d0, d1, d2, dgp_ref):
        dmv = dm_ref[...]
        for n, (gr, br, dr) in enumerate(((g0, b0, d0), (g1, b1, d1), (g2, b2, d2))):
            s = _sigmoid(gr[...])
            dr[...] = (dmv * s).astype(BF16)
            dgp_ref[:, n * D_MODEL:(n + 1) * D_MODEL] = (dmv * br[...] * s * (1.0 - s)).astype(BF16)

    tok = [(proj, D_MODEL, n) for n in range(3)] + [(b, D_MODEL, 0) for b in bds] + [(dm, D_MODEL, 0)]
    return _tok_call(body, name, S, min(S, 512), tok, [],
                     [(D_MODEL, BF16)] * 3 + [(3 * D_MODEL, BF16)])


def _sgu_ln(v, lg, lb):
    mu = jnp.mean(v, axis=-1, keepdims=True)
    vc = v - mu
    rstd = lax.rsqrt(jnp.mean(vc * vc, axis=-1, keepdims=True) + EPS)
    vhat = vc * rstd
    return vhat, rstd, vhat * lg + lb


def _sgu_fwd(proj, lg, lb, wc, bst, name):
    S = proj.shape[0]

    def body(i, ua_ref, va_ref, lg_ref, lb_ref, wc_ref, bs_ref, o_ref):
        u = _gelu(ua_ref[...])
        _, _, vn = _sgu_ln(_gelu(va_ref[...]), lg_ref[...], lb_ref[...])
        for g in range(SGU_G):
            sl = slice(g * 128, (g + 1) * 128)
            mixed = _mm(wc_ref[sl, :], vn[:, sl]) + bs_ref[:, g:g + 1]
            o_ref[:, sl] = (u[:, sl] * mixed).astype(BF16)

    return _tok_call(body, name, S, SGU_T, [(proj, MIX, C_UA // MIX), (proj, MIX, C_VA // MIX)], [lg, lb, wc, bst],
                     [(MIX, BF16)])[0]


def _sgu_bwd(proj, lg, lb, wc, bst, dout, name):
    S = proj.shape[0]

    def body(i, ua_ref, va_ref, do_ref, lg_ref, lb_ref, wc_ref, bs_ref, dua_ref, dva_ref, dlg_ref, dlb_ref, dwc_ref,
             dbs_ref):
        ua, va, do = ua_ref[...], va_ref[...], do_ref[...].astype(F32)
        u = _gelu(ua)
        lgv = lg_ref[...]
        vhat, rstd, vn = _sgu_ln(_gelu(va), lgv, lb_ref[...])
        tril = lax.broadcasted_iota(jnp.int32, (128, 128), 0) >= lax.broadcasted_iota(jnp.int32, (128, 128), 1)
        lane4 = lax.broadcasted_iota(jnp.int32, (128, 4), 1)
        dvn_parts, dbs = [], jnp.zeros((128, 4), F32)
        for g in range(SGU_G):
            sl = slice(g * 128, (g + 1) * 128)
            wg = wc_ref[sl, :]
            mixed = _mm(wg, vn[:, sl]) + bs_ref[:, g:g + 1]
            dua_ref[:, sl] = (do[:, sl] * mixed * _dgelu(ua[:, sl])).astype(BF16)
            dmix = do[:, sl] * u[:, sl]
            dbs = dbs + jnp.where(lane4 == g, jnp.sum(dmix, axis=-1, keepdims=True), 0.0)
            dwg = jnp.where(tril, _mm_nt(dmix, vn[:, sl]), 0.0)
            _acc(dwc_ref.at[sl, :], dwg, i)
            dvn_parts.append(_mm_tn(wg, dmix))
        dvn = jnp.concatenate(dvn_parts, axis=1)
        _acc(dbs_ref, dbs, i)
        _acc(dlg_ref, jnp.sum(dvn * vhat, axis=0, keepdims=True), i)
        _acc(dlb_ref, jnp.sum(dvn, axis=0, keepdims=True), i)
        dvh = dvn * lgv
        dv = rstd * (dvh - jnp.mean(dvh, axis=-1, keepdims=True) - vhat * jnp.mean(dvh * vhat, axis=-1, keepdims=True))
        dva_ref[...] = (dv * _dgelu(va)).astype(BF16)

    return _tok_call(body, name, S, SGU_T, [(proj, MIX, C_UA // MIX), (proj, MIX, C_VA // MIX), (dout, MIX, 0)],
                     [lg, lb, wc, bst], [(MIX, BF16), (MIX, BF16)],
                     [((1, MIX), F32), ((1, MIX), F32), ((SGU_G * 128, 128), F32), ((128, 4), F32)])


def _rope_tables(positions):
    S = positions.shape[0]
    inv_freq = ROPE_THETA ** (-jnp.arange(0, ROPE_DIM, 2, dtype=F32) / ROPE_DIM)
    ang = positions.astype(F32)[:, None] * inv_freq
    c, s = jnp.cos(ang), jnp.sin(ang)
    c64 = jnp.concatenate([c, c, jnp.ones((S, SWA_HD - ROPE_DIM), F32)], axis=1)
    s64 = jnp.concatenate([-s, s, jnp.zeros((S, SWA_HD - ROPE_DIM), F32)], axis=1)
    return jnp.tile(c64, (1, 2)), jnp.tile(s64, (1, 2))


def _rope128(x, c, s):
    lane = lax.broadcasted_iota(jnp.int32, x.shape, 1) % SWA_HD
    swapped = jnp.where(lane < ROPE_DIM // 2, pltpu.roll(x, 128 - ROPE_DIM // 2, 1), pltpu.roll(x, ROPE_DIM // 2, 1))
    return x * c + swapped * s


def _rope_t128(y, c, s):
    ys = y * s
    lane = lax.broadcasted_iota(jnp.int32, y.shape, 1) % SWA_HD
    swapped = jnp.where(lane < ROPE_DIM // 2, pltpu.roll(ys, 128 - ROPE_DIM // 2, 1), pltpu.roll(ys, ROPE_DIM // 2, 1))
    return y * c + jnp.where(lane < ROPE_DIM, swapped, 0.0)


def _rope_fwd(proj, cos, sin, name):
    S = proj.shape[0]
    scale = SWA_HD ** -0.5

    def body(i, q_ref, k_ref, v_ref, c_ref, s_ref, qo_ref, ko_ref, vo_ref):
        c, s = c_ref[...], s_ref[...]
        for j in range(4):
            sl = slice(j * 128, (j + 1) * 128)
            qo_ref[:, sl] = (_rope128(q_ref[:, sl], c, s) * scale).astype(BF16)
        ko_ref[...] = _rope128(k_ref[...], c, s).astype(BF16)
        vo_ref[...] = v_ref[...].astype(BF16)

    return _tok_call(body, name, S, min(S, 512),
                     [(proj, MIX, C_QB // MIX), (proj, 128, C_KB // 128), (proj, 128, C_VB // 128), (cos, 128, 0),
                      (sin, 128, 0)], [], [(MIX, BF16), (128, BF16), (128, BF16)])


def _rope_bwd(dq, dk, dv, cos, sin, name):
    S = dq.shape[0]
    scale = SWA_HD ** -0.5

    def body(i, dq_ref, dk_ref, dv_ref, c_ref, s_ref, qo_ref, ko_ref, vo_ref):
        c, s = c_ref[...], s_ref[...]
        for j in range(4):
            sl = slice(j * 128, (j + 1) * 128)
            qo_ref[:, sl] = _rope_t128(dq_ref[:, sl] * scale, c, s).astype(BF16)
        ko_ref[...] = _rope_t128(dk_ref[...], c, s).astype(BF16)
        vo_ref[...] = dv_ref[...].astype(BF16)

    return _tok_call(body, name, S, min(S, 512),
                     [(dq, MIX, 0), (dk, 128, 0), (dv, 128, 0), (cos, 128, 0), (sin, 128, 0)], [],
                     [(MIX, BF16), (128, BF16), (128, BF16)])


def _swa_band(i, k_ref, v_ref):
    pstart = pl.multiple_of(jnp.maximum(i - 1, 0) * WINDOW, WINDOW)
    cstart = pl.multiple_of(i * WINDOW, WINDOW)
    kb = jnp.concatenate([k_ref[pl.ds(pstart, WINDOW), :], k_ref[pl.ds(cstart, WINDOW), :]], axis=0)
    vb = jnp.concatenate([v_ref[pl.ds(pstart, WINDOW), :], v_ref[pl.ds(cstart, WINDOW), :]], axis=0)
    qi = lax.broadcasted_iota(jnp.int32, (WINDOW, 2 * WINDOW), 0)
    sj = lax.broadcasted_iota(jnp.int32, (WINDOW, 2 * WINDOW), 1)
    mask = (sj > qi) & (sj <= qi + WINDOW) & ((i > 0) | (sj >= WINDOW))
    return kb, vb, mask, pstart, cstart


def _swa_probs(qh, kh, mask, sink):
    logits = jnp.where(mask, _dg(qh, kh, 1, 1), -1e30)
    m = jnp.maximum(jnp.max(logits, axis=-1, keepdims=True), sink)
    p = jnp.exp(logits - m)
    ps = jnp.exp(sink - m)
    inv = 1.0 / (jnp.sum(p, axis=-1, keepdims=True) + ps)
    return p * inv, ps * inv


def _swa_fwd(q, k, v, sinks, name):
    S = q.shape[0]

    def body(i, q_ref, k_ref, v_ref, s_ref, o_ref):
        kb, vb, mask, _, _ = _swa_band(i, k_ref, v_ref)
        qv = q_ref[...]
        for h in range(SWA_H):
            kv = h // (SWA_H // SWA_KV)
            ksl = slice(kv * SWA_HD, (kv + 1) * SWA_HD)
            hsl = slice(h * SWA_HD, (h + 1) * SWA_HD)
            pn, _ = _swa_probs(qv[:, hsl], kb[:, ksl], mask, s_ref[0, h])
            o_ref[:, hsl] = _dg(pn.astype(BF16), vb[:, ksl], 1, 0).astype(BF16)

    return _tok_call(body, name, S, WINDOW, [(q, MIX, 0)], [k, v], [(MIX, BF16)], smem_in=[sinks])[0]


def _swa_bwd(q, k, v, sinks, dout, name):
    S = q.shape[0]

    def body(i, q_ref, do_ref, k_ref, v_ref, s_ref, dq_ref, dk_ref, dv_ref, ds_ref):
        kb, vb, mask, pstart, cstart = _swa_band(i, k_ref, v_ref)
        qv, dov = q_ref[...], do_ref[...]
        lane = lax.broadcasted_iota(jnp.int32, (1, 128), 1)
        dsink = jnp.zeros((1, 128), F32)
        dkb, dvb = [], []
        for kv in range(SWA_KV):
            ksl = slice(kv * SWA_HD, (kv + 1) * SWA_HD)
            dk_kv = jnp.zeros((2 * WINDOW, SWA_HD), F32)
            dv_kv = jnp.zeros((2 * WINDOW, SWA_HD), F32)
            for gq in range(SWA_H // SWA_KV):
                h = kv * (SWA_H // SWA_KV) + gq
                hsl = slice(h * SWA_HD, (h + 1) * SWA_HD)
                qh, doh = qv[:, hsl], dov[:, hsl].astype(BF16)
                pn, psn = _swa_probs(qh, kb[:, ksl], mask, s_ref[0, h])
                dp = _dg(doh, vb[:, ksl], 1, 1)
                delta = jnp.sum(dp * pn, axis=-1, keepdims=True)
                dsc = (pn * (dp - delta)).astype(BF16)
                dq_ref[:, hsl] = _dg(dsc, kb[:, ksl], 1, 0)
                dk_kv = dk_kv + _dg(dsc, qh, 0, 0)
                dv_kv = dv_kv + _dg(pn.astype(BF16), doh, 0, 0)
                dsink = dsink + jnp.where(lane == h, -jnp.sum(psn * delta, axis=0, keepdims=True), 0.0)
            dkb.append(dk_kv)
            dvb.append(dv_kv)
        dkb = jnp.concatenate(dkb, axis=1)
        dvb = jnp.concatenate(dvb, axis=1)

        @pl.when(i == 0)
        def _():
            dk_ref[...] = jnp.zeros_like(dk_ref)
            dv_ref[...] = jnp.zeros_like(dv_ref)

        dk_ref[pl.ds(pstart, WINDOW), :] += dkb[:WINDOW]
        dv_ref[pl.ds(pstart, WINDOW), :] += dvb[:WINDOW]
        dk_ref[pl.ds(cstart, WINDOW), :] += dkb[WINDOW:]
        dv_ref[pl.ds(cstart, WINDOW), :] += dvb[WINDOW:]
        _acc(ds_ref, dsink, i)

    return _tok_call(body, name, S, WINDOW, [(q, MIX, 0), (dout, MIX, 0)], [k, v], [(MIX, F32)],
                     [((S, 128), F32), ((S, 128), F32), ((1, 128), F32)], smem_in=[sinks])


def _shift_rows(xs, k):
    return xs if k == 0 else pltpu.roll(xs, k, 0)


def _dn_conv(x_ref, p_ref, w_ref, i):
    halo = jnp.where(i > 0, p_ref[...], 0.0)
    xs = jnp.concatenate([halo, x_ref[...]], axis=0)
    sh = [_shift_rows(xs, DN_CONV - 1 - t)[8:] for t in range(DN_CONV)]
    pre = sh[0] * w_ref[0:1, :]
    for t in range(1, DN_CONV):
        pre = pre + sh[t] * w_ref[t:t + 1, :]
    return pre, sh


def _dn_gates(sm, alog, dtb):
    lane = lax.broadcasted_iota(jnp.int32, sm.shape, 1)
    return jnp.where(lane < DN_H, _sigmoid(sm), -jnp.exp(alog) * _softplus(sm + dtb))


def _dn_pre_fwd(proj, conv_w, alog_l, dtb_l, name):
    S = proj.shape[0]
    scale = DN_HD ** -0.5

    def body(i, x_ref, sm_ref, p_ref, w_ref, al_ref, db_ref, q_ref, k_ref, v_ref, bg_ref):
        pre, _ = _dn_conv(x_ref, p_ref, w_ref, i)
        a = _silu(pre)
        for h in range(DN_H):
            sl = slice(h * DN_HD, (h + 1) * DN_HD)
            qh, kh = a[:, sl], a[:, MIX + h * DN_HD:MIX + (h + 1) * DN_HD]
            q_ref[:, sl] = qh * (lax.rsqrt(jnp.sum(qh * qh, axis=-1, keepdims=True) + EPS) * scale)
            k_ref[:, sl] = kh * lax.rsqrt(jnp.sum(kh * kh, axis=-1, keepdims=True) + EPS)
        v_ref[...] = a[:, 2 * MIX:]
        bg_ref[...] = _dn_gates(sm_ref[...], al_ref[...], db_ref[...])

    TB = min(S, 256)
    return _tok_call(body, name, S, TB, [(proj, 3 * MIX, C_QKV // (3 * MIX)), (proj, 128, C_SM // 128)],
                     [conv_w, alog_l, dtb_l], [(MIX, F32), (MIX, F32), (MIX, F32), (128, F32)],
                     prev_in=[(proj, 3 * MIX, C_QKV // (3 * MIX))])


def _dn_pre_bwd1(proj, conv_w, alog_l, dtb_l, dq, dk, dv, dbg, name):
    S = proj.shape[0]
    scale = DN_HD ** -0.5

    def body(i, x_ref, sm_ref, dq_ref, dk_ref, dv_ref, dbg_ref, p_ref, w_ref, al_ref, db_ref, dpre_ref, dsm_ref,
             dw_ref, dal_ref, ddb_ref):
        pre, sh = _dn_conv(x_ref, p_ref, w_ref, i)
        a = _silu(pre)
        da_parts = []
        for part, (g_ref, sc) in enumerate(((dq_ref, scale), (dk_ref, 1.0))):
            for h in range(DN_H):
                xh = a[:, part * MIX + h * DN_HD:part * MIX + (h + 1) * DN_HD]
                rs = lax.rsqrt(jnp.sum(xh * xh, axis=-1, keepdims=True) + EPS)
                y = xh * rs
                dy = g_ref[:, h * DN_HD:(h + 1) * DN_HD] * sc
                da_parts.append(rs * (dy - y * jnp.sum(dy * y, axis=-1, keepdims=True)))
        da_parts.append(dv_ref[...])
        dpre = jnp.concatenate(da_parts, axis=1) * _dsilu(pre)
        dpre_ref[...] = dpre
        dw = jnp.concatenate([jnp.sum(dpre * sh[t], axis=0, keepdims=True) for t in range(DN_CONV)], axis=0)
        _acc(dw_ref, dw, i)
        sm, al, db, dbg_v = sm_ref[...], al_ref[...], db_ref[...], dbg_ref[...]
        lane = lax.broadcasted_iota(jnp.int32, sm.shape, 1)
        sg = _sigmoid(sm)
        gneg = -jnp.exp(al)
        is_g = (lane >= DN_H) & (lane < 2 * DN_H)
        d_al = jnp.where(is_g, dbg_v * gneg * _sigmoid(sm + db), 0.0)
        dsm_ref[...] = jnp.where(lane < DN_H, dbg_v * sg * (1.0 - sg), d_al).astype(BF16)
        _acc(ddb_ref, jnp.sum(d_al, axis=0, keepdims=True), i)
        _acc(dal_ref, jnp.sum(jnp.where(is_g, dbg_v * gneg * _softplus(sm + db), 0.0), axis=0, keepdims=True), i)

    TB = min(S, 256)
    return _tok_call(body, name, S, TB,
                     [(proj, 3 * MIX, C_QKV // (3 * MIX)), (proj, 128, C_SM // 128), (dq, MIX, 0), (dk, MIX, 0),
                      (dv, MIX, 0), (dbg, 128, 0)], [conv_w, alog_l, dtb_l],
                     [(3 * MIX, F32), (128, BF16)], [((DN_CONV, 3 * MIX), F32), ((1, 128), F32), ((1, 128), F32)],
                     prev_in=[(proj, 3 * MIX, C_QKV // (3 * MIX))])


def _dn_pre_bwd2(dpre, conv_w, name):
    S = dpre.shape[0]
    TB = min(S, 256)
    nb = S // TB

    def body(i, d_ref, n_ref, w_ref, o_ref):
        halo = jnp.where(i < nb - 1, n_ref[...], 0.0)
        ds = jnp.concatenate([d_ref[...], halo], axis=0)
        out = ds[:TB] * w_ref[DN_CONV - 1:DN_CONV, :]
        for t in range(DN_CONV - 1):
            k = DN_CONV - 1 - t
            out = out + pltpu.roll(ds, TB + 8 - k, 0)[:TB] * w_ref[t:t + 1, :]
        o_ref[...] = out.astype(BF16)

    return _tok_call(body, name, S, TB, [(dpre, 3 * MIX, 0)], [conv_w], [(3 * MIX, BF16)],
                     next_in=[(dpre, 3 * MIX, 0)])[0]


def _dn_chunk_terms(bg, h):
    C = DN_C
    beta = bg[:, h:h + 1]
    gb = jnp.broadcast_to(bg[:, DN_H + h:DN_H + h + 1], (C, C))
    ri = lax.broadcasted_iota(jnp.int32, (C, C), 0)
    ci = lax.broadcasted_iota(jnp.int32, (C, C), 1)
    tril, eye = ri >= ci, ri == ci
    gc_col = _dg_exact_lhs(tril, gb, 1, 0)
    gc_row = jnp.sum(jnp.where(eye, gc_col, 0.0), axis=0, keepdims=True)
    decay = jnp.exp(jnp.where(tril, gc_col - gc_row, -1e30))
    gci = gc_col[:, 0:1]
    gl = gc_col[C - 1:C, 0:1]
    return beta, decay, jnp.exp(gci), jnp.exp(gl - gci), jnp.exp(gl), tril, eye, ri, ci


def _dn_core_fwd(q, k, v, bg, name):
    S = q.shape[0]
    C = DN_C
    nt = S // C

    def kern(q_ref, k_ref, v_ref, bg_ref, o_ref, t_ref, uw_ref, vn_ref, st_ref, state):
        i = pl.program_id(0)

        @pl.when(i == 0)
        def _():
            state[...] = jnp.zeros_like(state)

        bg_v = bg_ref[...]
        for h in range(DN_H):
            sl = slice(h * DN_HD, (h + 1) * DN_HD)
            qh, kh, vh = q_ref[:, sl], k_ref[:, sl], v_ref[:, sl]
            beta, decay, e_gc, e_kd, cdec, tril, eye, ri, ci = _dn_chunk_terms(bg_v, h)
            kb = kh * beta
            lower = jnp.where(ri > ci, _mm_nt(kb, kh) * decay, 0.0)
            x = -lower
            tm = jnp.where(eye, 1.0, 0.0) + x
            p = x
            for _ in range(5):
                p = _dg3(p, p, 1, 0)
                tm = tm + _dg3(tm, p, 1, 0)
            rhs = jnp.concatenate([vh * beta, kb * e_gc], axis=1)
            sol = _dg3(tm, rhs, 1, 0)
            u, w = sol[:, :DN_HD], sol[:, DN_HD:]
            attn = _mm_nt(qh, kh) * decay
            s_in = state[sl, :]
            vnew = u - _mm(w, s_in)
            o_ref[:, sl] = _mm(qh * e_gc, s_in) + _mm(attn, vnew)
            state[sl, :] = s_in * cdec + _mm_tn(kh * e_kd, vnew)
            st_ref[sl, :] = s_in
            t_ref[:, h * C:(h + 1) * C] = tm
            uw_ref[:, sl] = u
            uw_ref[:, MIX + h * DN_HD:MIX + (h + 1) * DN_HD] = w
            vn_ref[:, sl] = vnew

    tok = lambda w: pl.BlockSpec((C, w), lambda i: (i, 0))
    return pl.pallas_call(
        kern, name=name, grid=(nt,), in_specs=[tok(MIX), tok(MIX), tok(MIX), tok(128)],
        out_specs=[tok(MIX), tok(DN_H * C), tok(2 * MIX), tok(MIX), pl.BlockSpec((DN_H * DN_HD, DN_HD), lambda i: (i, 0))],
        out_shape=[jax.ShapeDtypeStruct((S, MIX), F32), jax.ShapeDtypeStruct((S, DN_H * C), F32),
                   jax.ShapeDtypeStruct((S, 2 * MIX), F32), jax.ShapeDtypeStruct((S, MIX), F32),
                   jax.ShapeDtypeStruct((nt * DN_H * DN_HD, DN_HD), F32)],
        scratch_shapes=[pltpu.VMEM((DN_H * DN_HD, DN_HD), F32)],
        compiler_params=_cparams(("arbitrary",)),
    )(q, k, v, bg)


def _dn_core_bwd(q, k, v, bg, tm, uw, vn, st, do, name):
    S = q.shape[0]
    C = DN_C
    nt = S // C

    def kern(q_ref, k_ref, v_ref, bg_ref, t_ref, uw_ref, vn_ref, st_ref, do_ref, dq_ref, dk_ref, dv_ref, dbg_ref,
             dstate):
        i = pl.program_id(0)

        @pl.when(i == 0)
        def _():
            dstate[...] = jnp.zeros_like(dstate)

        bg_v = bg_ref[...]
        lane = lax.broadcasted_iota(jnp.int32, (C, 128), 1)
        dbg = jnp.zeros((C, 128), F32)
        for h in range(DN_H):
            sl = slice(h * DN_HD, (h + 1) * DN_HD)
            qh, kh, vh, doh = q_ref[:, sl], k_ref[:, sl], v_ref[:, sl], do_ref[:, sl]
            beta, decay, e_gc, e_kd, cdec, tril, eye, ri, ci = _dn_chunk_terms(bg_v, h)
            th = t_ref[:, h * C:(h + 1) * C]
            u, w = uw_ref[:, sl], uw_ref[:, MIX + h * DN_HD:MIX + (h + 1) * DN_HD]
            vnew, s_in, ds_o = vn_ref[:, sl], st_ref[sl, :], dstate[sl, :]
            kb = kh * beta
            kk = _mm_nt(kb, kh)
            attn = _mm_nt(qh, kh) * decay
            qd, kd = qh * e_gc, kh * e_kd
            d_vnew = _mm_tn(attn, doh) + _mm(kd, ds_o)
            d_qd = _mm_nt(doh, s_in)
            d_attn = _mm_nt(doh, vnew)
            d_kd = _mm_nt(vnew, ds_o)
            d_c = jnp.sum(jnp.sum(ds_o * s_in, axis=1, keepdims=True), axis=0, keepdims=True)
            d_w = -_mm_nt(d_vnew, s_in)
            dstate[sl, :] = ds_o * cdec + _mm_tn(qd, doh) - _mm_tn(w, d_vnew)
            d_rhs = _dg3(th, jnp.concatenate([d_vnew, d_w], axis=1), 0, 0)
            d_a = -_dg3(d_rhs, jnp.concatenate([u, w], axis=1), 1, 1)
            d_lower = jnp.where(ri > ci, d_a, 0.0)
            d_vb, dz = d_rhs[:, :DN_HD], d_rhs[:, DN_HD:]
            dv_ref[:, sl] = d_vb * beta
            d_beta = jnp.sum(d_vb * vh, axis=-1, keepdims=True)
            d_kb = dz * e_gc
            d_gc = jnp.sum(dz * kb, axis=-1, keepdims=True) * e_gc
            d_kk = d_lower * decay
            d_qk = d_attn * decay
            dm = d_kk * kk + d_attn * attn
            d_kb = d_kb + _mm(d_kk, kh)
            d_k = _mm_tn(d_kk, kb) + _mm_tn(d_qk, qh)
            d_q = _mm(d_qk, kh) + d_qd * e_gc
            colsum = jnp.sum(dm, axis=0, keepdims=True)
            d_gc = d_gc + jnp.sum(dm, axis=-1, keepdims=True) - jnp.sum(jnp.where(eye, colsum, 0.0), axis=-1, keepdims=True)
            d_gc = d_gc + jnp.sum(d_qd * qd, axis=-1, keepdims=True)
            t_kd = jnp.sum(d_kd * kd, axis=-1, keepdims=True)
            d_gl = jnp.sum(t_kd, axis=0, keepdims=True) + d_c * cdec
            d_gc = d_gc - t_kd + jnp.where(ri[:, 0:1] == C - 1, d_gl, 0.0)
            d_k = d_k + d_kd * e_kd + d_kb * beta
            d_beta = d_beta + jnp.sum(d_kb * kh, axis=-1, keepdims=True)
            d_g = _dg_exact_lhs(ri <= ci, jnp.broadcast_to(d_gc, (C, 128)), 1, 0)
            dq_ref[:, sl] = d_q
            dk_ref[:, sl] = d_k
            dbg = dbg + jnp.where(lane == h, d_beta, 0.0) + jnp.where(lane == DN_H + h, d_g, 0.0)
        dbg_ref[...] = dbg

    tok = lambda w: pl.BlockSpec((C, w), lambda i: (nt - 1 - i, 0))
    return pl.pallas_call(
        kern, name=name, grid=(nt,),
        in_specs=[tok(MIX), tok(MIX), tok(MIX), tok(128), tok(DN_H * C), tok(2 * MIX), tok(MIX),
                  pl.BlockSpec((DN_H * DN_HD, DN_HD), lambda i: (nt - 1 - i, 0)), tok(MIX)],
        out_specs=[tok(MIX), tok(MIX), tok(MIX), tok(128)],
        out_shape=[jax.ShapeDtypeStruct((S, MIX), F32)] * 3 + [jax.ShapeDtypeStruct((S, 128), F32)],
        scratch_shapes=[pltpu.VMEM((DN_H * DN_HD, DN_HD), F32)],
        compiler_params=_cparams(("arbitrary",)),
    )(q, k, v, bg, tm, uw, vn, st, do)


def _dn_post_fwd(o, proj, ng, name):
    S = o.shape[0]

    def body(i, o_ref, z_ref, g_ref, out_ref):
        gv = g_ref[...]
        for h in range(DN_H):
            sl = slice(h * DN_HD, (h + 1) * DN_HD)
            oh = o_ref[:, sl]
            r = lax.rsqrt(jnp.mean(oh * oh, axis=-1, keepdims=True) + EPS)
            out_ref[:, sl] = (oh * r * gv * _silu(z_ref[:, sl])).astype(BF16)

    return _tok_call(body, name, S, min(S, 512), [(o, MIX, 0), (proj, MIX, C_ZC // MIX)], [ng], [(MIX, BF16)])[0]


def _dn_post_bwd(o, proj, ng, dout, name):
    S = o.shape[0]

    def body(i, o_ref, z_ref, do_ref, g_ref, dov_ref, dz_ref, dg_ref):
        gv = g_ref[...]
        dg = jnp.zeros((1, DN_HD), F32)
        for h in range(DN_H):
            sl = slice(h * DN_HD, (h + 1) * DN_HD)
            oh, zh, dh = o_ref[:, sl], z_ref[:, sl], do_ref[:, sl].astype(F32)
            r = lax.rsqrt(jnp.mean(oh * oh, axis=-1, keepdims=True) + EPS)
            dz_ref[:, sl] = (dh * oh * r * gv * _dsilu(zh)).astype(BF16)
            dx, dgh = _rms_bwd_vals(oh, gv, dh * _silu(zh))
            dov_ref[:, sl] = dx
            dg = dg + dgh
        _acc(dg_ref, dg, i)

    return _tok_call(body, name, S, min(S, 512), [(o, MIX, 0), (proj, MIX, C_ZC // MIX), (dout, MIX, 0)], [ng],
                     [(MIX, F32), (MIX, BF16)], [((1, DN_HD), F32)])


def _layer_params(w, l):
    lane = jnp.arange(128)
    is_g = (lane >= DN_H) & (lane < 2 * DN_H)
    spread = lambda t: jnp.where(is_g, jnp.tile(t, 128 // DN_H), 0.0).reshape(1, 128)
    tril = jnp.tril(jnp.ones((SGU_T, SGU_T), bool))
    return dict(
        win=w["w_in"][l], wb=w["w_branch"][l], wout=w["w_out"][l], wgu=w["w_gate_up"][l], wdown=w["w_down"][l],
        conv=w["dn_conv_w"][l], attn_norm=w["attn_norm"][l].reshape(1, -1), ffn_norm=w["ffn_norm"][l].reshape(1, -1),
        lg=w["sgu_ln_g"][l].reshape(1, -1), lb=w["sgu_ln_b"][l].reshape(1, -1),
        wc=jnp.where(tril, w["sgu_w"][l], 0.0).reshape(SGU_G * SGU_T, SGU_T), bst=w["sgu_b"][l].T,
        sinks=w["attn_sinks"][l].reshape(1, -1), alog=spread(w["dn_a_log"][l]), dtb=spread(w["dn_dt_bias"][l]),
        ng=w["dn_norm"][l].reshape(1, -1))


def _layer_fwd(x, p, cos, sin, l):
    n = lambda s: f"l{l}_{s}"
    h = _rms_fwd(x, p["attn_norm"], n("rms1"))
    proj = _matmul(h, p["win"], name=n("mm_in"))
    out_a = _sgu_fwd(proj, p["lg"], p["lb"], p["wc"], p["bst"], n("sgu_fwd"))
    qr, kr, vr = _rope_fwd(proj, cos, sin, n("rope_fwd"))
    out_b = _swa_fwd(qr, kr, vr, p["sinks"], n("swa_fwd"))
    q, k, v, bg = _dn_pre_fwd(proj, p["conv"], p["alog"], p["dtb"], n("dn_pre_fwd"))
    o, tm, uw, vn, st = _dn_core_fwd(q, k, v, bg, n("dn_core_fwd"))
    out_c = _dn_post_fwd(o, proj, p["ng"], n("dn_post_fwd"))
    outs = (out_a, out_b, out_c)
    bds = [_matmul(outs[j], p["wb"][j], name=n(f"mm_branch{j}")) for j in range(3)]
    merged = _merge_fwd(proj, bds, n("merge_fwd"))
    x1 = _matmul(merged, p["wout"], add=x, name=n("mm_out"))
    h2 = _rms_fwd(x1, p["ffn_norm"], n("rms2"))
    gu = _matmul(h2, p["wgu"], name=n("mm_gu"))
    act = _swiglu_fwd(gu, n("swiglu_fwd"))
    x2 = _matmul(act, p["wdown"], add=x1, name=n("mm_down"))
    saved = dict(x=x, h=h, proj=proj, outs=outs, qr=qr, kr=kr, vr=vr, q=q, k=k, v=v, bg=bg, o=o, tm=tm, uw=uw, vn=vn,
                 st=st, bds=bds, merged=merged, x1=x1, h2=h2, gu=gu, act=act)
    return x2, saved


def _layer_bwd(dx2, s, p, cos, sin, l):
    n = lambda t: f"l{l}_{t}"
    proj = s["proj"]
    g = {}
    g["w_down"] = _matmul(s["act"], dx2, ta=True, out_dtype=BF16, name=n("wg_down"))
    dact = _matmul(dx2, p["wdown"], tb=True, name=n("dg_down"))
    dgu = _swiglu_bwd(s["gu"], dact, n("swiglu_bwd"))
    g["w_gate_up"] = _matmul(s["h2"], dgu, ta=True, out_dtype=BF16, name=n("wg_gu"))
    dh2 = _matmul(dgu, p["wgu"], tb=True, name=n("dg_gu"))
    dx1, g["ffn_norm"] = _rms_bwd_add(s["x1"], p["ffn_norm"], dh2, dx2, n("rms2_bwd"))
    g["w_out"] = _matmul(s["merged"], dx1, ta=True, out_dtype=BF16, name=n("wg_out"))
    dm = _matmul(dx1, p["wout"], tb=True, name=n("dg_out"))
    dbd0, dbd1, dbd2, dgp = _merge_bwd(proj, s["bds"], dm, n("merge_bwd"))
    dbds = (dbd0, dbd1, dbd2)
    g["w_branch"] = jnp.stack([_matmul(s["outs"][j], dbds[j], ta=True, out_dtype=BF16, name=n(f"wg_branch{j}"))
                               for j in range(3)])
    douts = [_matmul(dbds[j], p["wb"][j], tb=True, name=n(f"dg_branch{j}")) for j in range(3)]
    dua, dva, g["sgu_ln_g"], g["sgu_ln_b"], dwc, dbs = _sgu_bwd(proj, p["lg"], p["lb"], p["wc"], p["bst"], douts[0],
                                                                n("sgu_bwd"))
    g["sgu_w"] = dwc.reshape(SGU_G, SGU_T, SGU_T)
    g["sgu_b"] = dbs.T
    dqr, dkr, dvr, dsink = _swa_bwd(s["qr"], s["kr"], s["vr"], p["sinks"], douts[1], n("swa_bwd"))
    g["attn_sinks"] = dsink[0, :SWA_H]
    dqb, dkb, dvb = _rope_bwd(dqr, dkr, dvr, cos, sin, n("rope_bwd"))
    do, dz, dng = _dn_post_bwd(s["o"], proj, p["ng"], douts[2], n("dn_post_bwd"))
    g["dn_norm"] = dng[0]
    dq, dk, dv, dbg = _dn_core_bwd(s["q"], s["k"], s["v"], s["bg"], s["tm"], s["uw"], s["vn"], s["st"], do,
                                   n("dn_core_bwd"))
    dpre, dsm, g["dn_conv_w"], dal, ddb = _dn_pre_bwd1(proj, p["conv"], p["alog"], p["dtb"], dq, dk, dv, dbg,
                                                       n("dn_pre_bwd1"))
    g["dn_a_log"] = dal[0, DN_H:2 * DN_H]
    g["dn_dt_bias"] = ddb[0, DN_H:2 * DN_H]
    dqkv = _dn_pre_bwd2(dpre, p["conv"], n("dn_pre_bwd2"))
    dproj = jnp.concatenate([dgp, dqkv, dua, dva, dqb, dz, dkb, dvb, dsm], axis=1)
    g["w_in"] = _matmul(s["h"], dproj, ta=True, out_dtype=BF16, name=n("wg_in"))
    dh = _matmul(dproj, p["win"], tb=True, name=n("dg_in"))
    dx, g["attn_norm"] = _rms_bwd_add(s["x"], p["attn_norm"], dh, dx1, n("rms1_bwd"))
    g["attn_norm"], g["ffn_norm"] = g["attn_norm"][0], g["ffn_norm"][0]
    g["sgu_ln_g"], g["sgu_ln_b"] = g["sgu_ln_g"][0], g["sgu_ln_b"][0]
    return dx, g


def _local_step(x, positions, target, w):
    cos, sin = _rope_tables(positions)
    params = [_layer_params(w, l) for l in range(DEPTH)]
    saves, xs = [], x
    for l in range(DEPTH):
        xs, sv = _layer_fwd(xs, params[l], cos, sin, l)
        saves.append(sv)
    dx, loss_row, dgf = _final_loss(xs, w["final_norm"].reshape(1, -1), target)
    grads = [None] * DEPTH
    for l in reversed(range(DEPTH)):
        dx, grads[l] = _layer_bwd(dx, saves[l], params[l], cos, sin, l)
    stacked = {k: jnp.stack([grads[l][k] for l in range(DEPTH)]) for k in grads[0]}
    stacked["final_norm"] = dgf[0]
    return loss_row[0, 0], dx, stacked


MESH = pl.DeviceIdType.MESH
HBM_SPEC = pl.BlockSpec(memory_space=pltpu.HBM)
VMEM_SPEC = pl.BlockSpec(memory_space=pltpu.VMEM)
N_CHIPS = 4
FLIPS = tuple((fx, fy, fc) for fx in (0, 1) for fy in (0, 1) for fc in (0, 1))[1:]
BIG = ("w_in", "w_branch", "w_out", "w_gate_up", "w_down")
BIG_SPEC = {
    "w_in": dict(rows=1024, cols=1792, axis=1, keep=1730, down=8, up=4),
    "w_branch": dict(rows=1536, cols=256, axis=1, keep=256, down=2, up=1),
    "w_out": dict(rows=256, cols=1024, axis=0, keep=1024, down=1, up=1),
    "w_gate_up": dict(rows=1024, cols=1408, axis=1, keep=1408, down=8, up=4),
    "w_down": dict(rows=704, cols=1024, axis=0, keep=1024, down=4, up=2),
}
CONV_ROWS, CONV_COLS = DEPTH * DN_CONV, 3 * MIX // N_CHIPS


def _full_shape(k):
    sp = BIG_SPEC[k]
    return (sp["rows"], N_CHIPS * sp["cols"]) if sp["axis"] == 1 else (N_CHIPS * sp["rows"], sp["cols"])


def _chip_block(ref, k, s, layer=None):
    sp = BIG_SPEC[k]
    if sp["axis"] == 1:
        idx = (slice(None), pl.ds(pl.multiple_of(s * sp["cols"], 128), sp["cols"]))
    else:
        idx = (pl.ds(pl.multiple_of(s * sp["rows"], 16), sp["rows"]), slice(None))
    return ref.at[idx] if layer is None else ref.at[(layer,) + idx]


def _me():
    return lax.axis_index("x"), lax.axis_index("y"), lax.axis_index("c")


def _peer(x, y, c, flip):
    fx, fy, fc = flip
    return (1 - x if fx else x, 1 - y if fy else y, 1 - c if fc else c)


class _Copies:
    def __init__(self, send_sems, recv_sems):
        self.send_sems, self.recv_sems, self.k, self.sent, self.landing = send_sems, recv_sems, 0, [], []

    def _copy(self, k, src, dst, to):
        return pltpu.make_async_remote_copy(src_ref=src, dst_ref=dst, send_sem=self.send_sems.at[k],
                                            recv_sem=self.recv_sems.at[k], device_id=to, device_id_type=MESH)

    def send(self, src, dst, to, lands):
        k = self.k
        self.k += 1
        cp = self._copy(k, src, dst, to)
        cp.start()
        self.sent.append(cp)
        self.landing.append(self._copy(k, lands, lands, to))
        return k

    def wait_landed(self, k):
        self.landing[k].wait_recv()

    def finish(self, landed=()):
        for k, cp in enumerate(self.landing):
            if k not in landed:
                cp.wait_recv()
        for cp in self.sent:
            cp.wait_send()


def _place_shard(shard, k, chip, name):
    sp = BIG_SPEC[k]
    rows, cols, keep = sp["rows"], sp["cols"], sp["keep"]
    tr = _pick(rows, (256, 64))
    nb = rows // tr
    if sp["axis"] == 1:
        out_spec = pl.BlockSpec((1, tr, cols), lambda l, i, ch: (l, i, ch[0]))
    else:
        out_spec = pl.BlockSpec((1, tr, cols), lambda l, i, ch: (l, ch[0] * nb + i, 0))

    def kern(ch_ref, x_ref, o_ref):
        v = x_ref[0].astype(BF16)
        if keep == cols:
            o_ref[0] = v
        else:
            o_ref[0, :, :keep] = v
            o_ref[0, :, keep:] = jnp.zeros((tr, cols - keep), BF16)

    return pl.pallas_call(
        kern, name=name, out_shape=jax.ShapeDtypeStruct((DEPTH,) + _full_shape(k), BF16),
        grid_spec=pltpu.PrefetchScalarGridSpec(
            num_scalar_prefetch=1, grid=(DEPTH, nb),
            in_specs=[pl.BlockSpec((1, tr, keep), lambda l, i, ch: (l, i, 0))], out_specs=out_spec),
        compiler_params=_cparams(("parallel", "parallel")),
    )(chip, shard)


def _allgather_weights(placed, conv):
    n = len(BIG)
    n_sem = 6 * n + 3

    def body(*refs):
        conv_ref = refs[n]
        out = dict(zip(BIG, refs[n + 1:2 * n + 1]))
        conv_out, send_sems, recv_sems, local_sem = refs[2 * n + 1:]
        x, y, c = _me()
        me = 2 * x + y
        chips = [(1 - x, y), (x, 1 - y), (1 - x, 1 - y)]
        net = _Copies(send_sems, recv_sems)

        def conv_block(s):
            return conv_out.at[:, pl.ds(pl.multiple_of(s * CONV_COLS, 128), CONV_COLS)]

        local = pltpu.make_async_copy(conv_ref, conv_block(me), local_sem)
        local.start()
        first = {}
        for k in BIG:
            for j, (px, py) in enumerate(chips):
                first[k, j] = net.send(_chip_block(out[k], k, me, c), _chip_block(out[k], k, me, c), (px, py, c),
                                       _chip_block(out[k], k, 2 * px + py, c))
        for px, py in chips:
            net.send(conv_ref, conv_block(me), (px, py, c), conv_block(2 * px + py))
        for k in BIG:
            for j, (px, py) in enumerate(chips):
                net.wait_landed(first[k, j])
                net.send(_chip_block(out[k], k, 2 * px + py, c), _chip_block(out[k], k, 2 * px + py, c), (x, y, 1 - c),
                         _chip_block(out[k], k, 2 * px + py, 1 - c))
        net.finish(landed=set(first.values()))
        local.wait()

    out_shape = [jax.ShapeDtypeStruct((DEPTH,) + _full_shape(k), BF16) for k in BIG]
    out_shape.append(jax.ShapeDtypeStruct((CONV_ROWS, N_CHIPS * CONV_COLS), F32))
    outs = pl.pallas_call(
        body, name="allgather_weights", out_shape=out_shape, in_specs=[HBM_SPEC] * (n + 1), out_specs=[HBM_SPEC] * (n + 1),
        input_output_aliases={i: i for i in range(n)},
        scratch_shapes=[pltpu.SemaphoreType.DMA((n_sem,)), pltpu.SemaphoreType.DMA((n_sem,)), pltpu.SemaphoreType.DMA],
    )(*[placed[k] for k in BIG], conv)
    return dict(zip(BIG, outs[:n])), outs[n]


def _row_chunks(ref, rows, n, layer=None):
    step = rows // n
    sl = [pl.ds(i * step, step) for i in range(n)]
    return [ref.at[s, :] if layer is None else ref.at[layer, s, :] for s in sl]


def _grads_to_sibling(grads):
    n = len(BIG)
    n_sem = sum(BIG_SPEC[k]["down"] for k in BIG)

    def body(*refs):
        g = dict(zip(BIG, refs[:n]))
        out = dict(zip(BIG, refs[n:2 * n]))
        send_sems, recv_sems = refs[2 * n:]
        x, y, c = _me()
        net = _Copies(send_sems, recv_sems)
        for k in BIG:
            rows, nch = _full_shape(k)[0], BIG_SPEC[k]["down"]
            for s, d in zip(_row_chunks(g[k], rows, nch, 1 - c), _row_chunks(out[k], rows, nch)):
                net.send(s, d, (x, y, 1 - c), d)
        net.finish()

    outs = pl.pallas_call(
        body, name="grads_to_sibling", out_shape=[jax.ShapeDtypeStruct(_full_shape(k), BF16) for k in BIG],
        in_specs=[HBM_SPEC] * n, out_specs=[HBM_SPEC] * n,
        scratch_shapes=[pltpu.SemaphoreType.DMA((n_sem,)), pltpu.SemaphoreType.DMA((n_sem,))],
    )(*[grads[k] for k in BIG])
    return dict(zip(BIG, outs))


def _add_layer(g2, other, layer, name):
    _, rows, cols = g2.shape
    tr = _pick(rows, (256, 128))

    def kern(l_ref, a_ref, b_ref, o_ref):
        o_ref[...] = (a_ref[0].astype(F32) + b_ref[...].astype(F32)).astype(BF16)

    return pl.pallas_call(
        kern, name=name, out_shape=jax.ShapeDtypeStruct((rows, cols), BF16),
        grid_spec=pltpu.PrefetchScalarGridSpec(
            num_scalar_prefetch=1, grid=(rows // tr,),
            in_specs=[pl.BlockSpec((1, tr, cols), lambda i, l: (l[0], i, 0)), pl.BlockSpec((tr, cols), lambda i, l: (i, 0))],
            out_specs=pl.BlockSpec((tr, cols), lambda i, l: (i, 0))),
        compiler_params=_cparams(("parallel",)),
    )(layer, g2, other)


def _scatter_chip_sums(sums):
    n = len(BIG)

    def body(*refs):
        src = dict(zip(BIG, refs[:n]))
        out = dict(zip(BIG, refs[n:2 * n]))
        send_sems, recv_sems = refs[2 * n:]
        x, y, c = _me()
        net = _Copies(send_sems, recv_sems)
        for k in BIG:
            for j, (px, py) in enumerate([(1 - x, y), (x, 1 - y), (1 - x, 1 - y)]):
                net.send(_chip_block(src[k], k, 2 * px + py), out[k].at[j], (px, py, c), out[k].at[j])
        net.finish()

    outs = pl.pallas_call(
        body, name="scatter_chip_sums",
        out_shape=[jax.ShapeDtypeStruct((N_CHIPS - 1, BIG_SPEC[k]["rows"], BIG_SPEC[k]["cols"]), BF16) for k in BIG],
        in_specs=[HBM_SPEC] * n, out_specs=[HBM_SPEC] * n,
        scratch_shapes=[pltpu.SemaphoreType.DMA((3 * n,)), pltpu.SemaphoreType.DMA((3 * n,))],
    )(*[sums[k] for k in BIG])
    return dict(zip(BIG, outs))


def _sum_chips(parts, own, k, where, name):
    sp = BIG_SPEC[k]
    rows, cols, keep = sp["rows"], sp["cols"], sp["keep"]
    tr = _pick(rows, (256, 64))
    nb = rows // tr
    if sp["axis"] == 1:
        own_spec = pl.BlockSpec((tr, cols), lambda i, w: (i, w[0]))
    else:
        own_spec = pl.BlockSpec((tr, cols), lambda i, w: (w[0] * nb + i, 0))

    def kern(w_ref, p_ref, own_ref, o_ref):
        tot = own_ref[...].astype(F32)
        for j in range(N_CHIPS - 1):
            tot = tot + p_ref[j].astype(F32)
        o_ref[0] = tot[:, :keep]

    return pl.pallas_call(
        kern, name=name, out_shape=jax.ShapeDtypeStruct((DEPTH, rows, keep), F32),
        grid_spec=pltpu.PrefetchScalarGridSpec(
            num_scalar_prefetch=1, grid=(nb,),
            in_specs=[pl.BlockSpec((N_CHIPS - 1, tr, cols), lambda i, w: (0, i, 0)), own_spec],
            out_specs=pl.BlockSpec((1, tr, keep), lambda i, w: (w[1], i, 0))),
        compiler_params=_cparams(("parallel",)),
    )(where, parts, own)


def _exchange_layers(red):
    n = len(BIG)
    n_sem = sum(BIG_SPEC[k]["up"] for k in BIG)

    def body(*refs):
        out = dict(zip(BIG, refs[n:2 * n]))
        send_sems, recv_sems = refs[2 * n:]
        x, y, c = _me()
        net = _Copies(send_sems, recv_sems)
        for k in BIG:
            rows, nch = BIG_SPEC[k]["rows"], BIG_SPEC[k]["up"]
            for mine, theirs in zip(_row_chunks(out[k], rows, nch, c), _row_chunks(out[k], rows, nch, 1 - c)):
                net.send(mine, mine, (x, y, 1 - c), theirs)
        net.finish()

    outs = pl.pallas_call(
        body, name="exchange_layers",
        out_shape=[jax.ShapeDtypeStruct((DEPTH, BIG_SPEC[k]["rows"], BIG_SPEC[k]["keep"]), F32) for k in BIG],
        in_specs=[HBM_SPEC] * n, out_specs=[HBM_SPEC] * n, input_output_aliases={i: i for i in range(n)},
        scratch_shapes=[pltpu.SemaphoreType.DMA((n_sem,)), pltpu.SemaphoreType.DMA((n_sem,))],
    )(*[red[k] for k in BIG])
    return dict(zip(BIG, outs))


def _adam_vals(g, w, m, v):
    m2 = ADAM_B1 * m + (1.0 - ADAM_B1) * g
    v2 = ADAM_B2 * v + (1.0 - ADAM_B2) * (g * g)
    m_hat = m2 / (1.0 - ADAM_B1 ** ADAM_STEP)
    v_hat = v2 / (1.0 - ADAM_B2 ** ADAM_STEP)
    return -ADAM_LR * (m_hat / (jnp.sqrt(v_hat) + ADAM_EPS) + ADAM_WD * w), m2, v2


def _allreduce_small_adam(gp, wp, mp, vp):
    rows = gp.shape[0]

    def body(g_ref, w_ref, m_ref, v_ref, gs_ref, d_ref, nm_ref, nv_ref, buf, send_sems, recv_sems):
        x, y, c = _me()
        buf[4 * x + 2 * y + c] = g_ref[...]

        def copy(k, to, src):
            sx, sy, sc = src
            return pltpu.make_async_remote_copy(
                src_ref=g_ref, dst_ref=buf.at[4 * sx + 2 * sy + sc], send_sem=send_sems.at[k],
                recv_sem=recv_sems.at[k], device_id=to, device_id_type=MESH)

        sends = [copy(k, _peer(x, y, c, f), (x, y, c)) for k, f in enumerate(FLIPS)]
        for cp in sends:
            cp.start()
        for k, f in enumerate(FLIPS):
            copy(k, (x, y, c), _peer(x, y, c, f)).wait_recv()
        for cp in sends:
            cp.wait_send()
        tot = buf[0]
        for d in range(1, 8):
            tot = tot + buf[d]
        gs_ref[...] = tot
        d_ref[...], nm_ref[...], nv_ref[...] = _adam_vals(tot, w_ref[...], m_ref[...], v_ref[...])

    return pl.pallas_call(
        body, name="allreduce_small", out_shape=[jax.ShapeDtypeStruct((rows, 128), F32)] * 4,
        in_specs=[VMEM_SPEC] * 4, out_specs=[VMEM_SPEC] * 4,
        scratch_shapes=[pltpu.VMEM((8, rows, 128), F32), pltpu.SemaphoreType.DMA((7,)), pltpu.SemaphoreType.DMA((7,))],
        compiler_params=pltpu.CompilerParams(vmem_limit_bytes=VMEM_LIMIT),
    )(gp, wp, mp, vp)


def _adam(g, w, m, v, name):
    shape = w.shape
    cols = shape[-1]
    rows = w.size // cols
    tr = _pick(rows, (256, 8))
    spec = pl.BlockSpec((tr, cols), lambda i: (i, 0))

    def kern(g_ref, w_ref, m_ref, v_ref, d_ref, nm_ref, nv_ref):
        d_ref[...], nm_ref[...], nv_ref[...] = _adam_vals(g_ref[...], w_ref[...], m_ref[...], v_ref[...])

    outs = pl.pallas_call(
        kern, name=name, grid=(rows // tr,), in_specs=[spec] * 4, out_specs=[spec] * 3,
        out_shape=[jax.ShapeDtypeStruct((rows, cols), F32)] * 3, compiler_params=_cparams(("parallel",)),
    )(*[t.reshape(rows, cols) for t in (g, w, m, v)])
    return [o.reshape(shape) for o in outs]


SMALL = ("attn_norm", "sgu_ln_g", "sgu_ln_b", "sgu_w", "sgu_b", "attn_sinks", "dn_a_log", "dn_dt_bias", "dn_norm",
         "ffn_norm", "final_norm")
SMALL_ROWS = 1200


def _pack_small(vals, extra=()):
    flat = [vals[k].astype(F32).reshape(-1) for k in SMALL] + [e.astype(F32).reshape(-1) for e in extra]
    n = sum(f.shape[0] for f in flat)
    flat.append(jnp.zeros((SMALL_ROWS * 128 - n,), F32))
    return jnp.concatenate(flat).reshape(SMALL_ROWS, 128)


def _unpack_small(slab, shapes):
    flat = slab.reshape(-1)
    out, o = {}, 0
    for k in SMALL:
        n = math.prod(shapes[k])
        out[k] = flat[o:o + n].reshape(shapes[k])
        o += n
    return out, flat[o:]


def _permute_in_cols(w_in, dtype):
    parts = [w_in[..., a:a + n] for a, n in IN_PIECES]
    parts.append(jnp.zeros(w_in.shape[:-1] + (IN_PAD,), w_in.dtype))
    return jnp.concatenate(parts, axis=-1).astype(dtype)


def _unpermute_in_cols(g):
    offs, o = {}, 0
    for a, n in IN_PIECES:
        offs[a] = (o, n)
        o += n
    return jnp.concatenate([g[..., offs[a][0]:offs[a][0] + offs[a][1]] for a in sorted(offs)], axis=-1)


WEIGHTS = ("attn_norm", "w_in", "sgu_ln_g", "sgu_ln_b", "sgu_w", "sgu_b", "attn_sinks", "dn_conv_w", "dn_a_log",
           "dn_dt_bias", "dn_norm", "w_branch", "w_out", "ffn_norm", "w_gate_up", "w_down", "final_norm")
IN_SHARD = IN_COLS // N_CHIPS


def _drop_chip_pad(t):
    lead = t.shape[:-1]
    return t.reshape(lead + (N_CHIPS, BIG_SPEC["w_in"]["cols"]))[..., :IN_SHARD].reshape(lead + (IN_COLS,))


def _add_chip_pad(t):
    lead = t.shape[:-1]
    t = jnp.pad(t.reshape(lead + (N_CHIPS, IN_SHARD)), [(0, 0)] * len(lead) + [(0, 0), (0, BIG_SPEC["w_in"]["cols"] - IN_SHARD)])
    return t.reshape(lead + (N_CHIPS * BIG_SPEC["w_in"]["cols"],))


def kernel(x, positions, attn_norm, w_in, sgu_ln_g, sgu_ln_b, sgu_w, sgu_b, attn_sinks, dn_conv_w, dn_a_log, dn_dt_bias, dn_norm, w_branch, w_out, ffn_norm, w_gate_up, w_down, final_norm, loss_target, m_attn_norm, m_w_in, m_sgu_ln_g, m_sgu_ln_b, m_sgu_w, m_sgu_b, m_attn_sinks, m_dn_conv_w, m_dn_a_log, m_dn_dt_bias, m_dn_norm, m_w_branch, m_w_out, m_ffn_norm, m_w_gate_up, m_w_down, m_final_norm, v_attn_norm, v_w_in, v_sgu_ln_g, v_sgu_ln_b, v_sgu_w, v_sgu_b, v_attn_sinks, v_dn_conv_w, v_dn_a_log, v_dn_dt_bias, v_dn_norm, v_w_branch, v_w_out, v_ffn_norm, v_w_gate_up, v_w_down, v_final_norm):
    given = dict(locals())
    W = {k: given[k] for k in WEIGHTS}
    M = {k: given["m_" + k] for k in WEIGHTS}
    V = {k: given["v_" + k] for k in WEIGHTS}
    chip = 2 * lax.axis_index("x") + lax.axis_index("y")
    core = lax.axis_index("c")
    layer = core.astype(jnp.int32).reshape(1)
    chip1 = chip.astype(jnp.int32).reshape(1)
    where = jnp.stack([chip, core]).astype(jnp.int32)

    placed = {k: _place_shard(W[k].reshape(DEPTH, BIG_SPEC[k]["rows"], BIG_SPEC[k]["keep"]), k, chip1, "place_" + k)
              for k in BIG}
    full, conv_full = _allgather_weights(placed, dn_conv_w.reshape(CONV_ROWS, CONV_COLS))
    w = {k: W[k] for k in SMALL}
    w.update(full)
    w["w_in"] = _permute_in_cols(_drop_chip_pad(full["w_in"]), BF16)
    w["w_branch"] = full["w_branch"].reshape(DEPTH, 3, MIX, D_MODEL)
    w["dn_conv_w"] = conv_full.reshape(DEPTH, DN_CONV, 3 * MIX)

    loss, dx, g = _local_step(x[0], positions[0], loss_target[0], w)

    gb = {k: g[k] for k in BIG}
    gb["w_in"] = _add_chip_pad(_unpermute_in_cols(g["w_in"]))
    gb["w_branch"] = g["w_branch"].reshape(DEPTH, 3 * MIX, D_MODEL)
    sibling = _grads_to_sibling(gb)
    chip_sums = {k: _add_layer(gb[k], sibling[k], layer, "chip_sum_" + k) for k in BIG}
    parts = _scatter_chip_sums(chip_sums)
    reduced = _exchange_layers({k: _sum_chips(parts[k], chip_sums[k], k, where, "sum_" + k) for k in BIG})
    grads = {k: reduced[k].reshape(W[k].shape) for k in BIG}

    small_shapes = {k: W[k].shape for k in SMALL}
    gs, ds, nms, nvs = _allreduce_small_adam(_pack_small(g, (g["dn_conv_w"], loss.reshape(1))), _pack_small(W),
                                             _pack_small(M), _pack_small(V))
    gsm, rest = _unpack_small(gs, small_shapes)
    grads.update(gsm)
    n_conv = g["dn_conv_w"].size
    conv_full = rest[:n_conv].reshape(g["dn_conv_w"].shape)
    grads["dn_conv_w"] = lax.dynamic_slice_in_dim(conv_full, chip * dn_conv_w.shape[2], dn_conv_w.shape[2], axis=2)
    loss_total = rest[n_conv]
    delta, new_m, new_v = (_unpack_small(t, small_shapes)[0] for t in (ds, nms, nvs))
    for k in BIG + ("dn_conv_w",):
        delta[k], new_m[k], new_v[k] = _adam(grads[k], W[k], M[k], V[k], "adam_" + k)

    return (loss_total, dx[None], *[grads[k] for k in WEIGHTS], *[delta[k] for k in WEIGHTS],
            *[new_m[k] for k in WEIGHTS], *[new_v[k] for k in WEIGHTS])
```

```python
import functools
import math

import jax
import jax.numpy as jnp
from jax import lax
from jax.experimental import pallas as pl
from jax.experimental.pallas import tpu as pltpu

F32 = jnp.float32
BF16 = jnp.bfloat16
HI = lax.Precision.HIGHEST

D_MODEL = 1024
DEPTH = 2
MIX = 512
EPS = 1e-6
SGU_G, SGU_T = 4, 128
SWA_H, SWA_KV, SWA_HD, WINDOW = 8, 2, 64, 128
ROPE_THETA, ROPE_DIM = 500000.0, 16
DN_H, DN_HD, DN_CONV, DN_C = 4, 128, 4, 64
D_FF = 2816
IN_COLS = 6920
IN_PIECES = ((3848, 3072), (1792, 1536), (0, 512), (512, 512), (1024, 512), (3328, 512), (1536, 128), (1664, 128),
             (3840, 8))
IN_PAD = 120
IN_R = 7040
C_GATE, C_QKV, C_UA, C_VA, C_QB, C_ZC, C_KB, C_VB, C_SM = 0, 3072, 4608, 5120, 5632, 6144, 6656, 6784, 6912

ADAM_LR, ADAM_B1, ADAM_B2, ADAM_EPS, ADAM_WD, ADAM_STEP = 0.001, 0.9, 0.999, 1e-08, 0.01, 10
VMEM_LIMIT = 56 * 1024 * 1024


def _cparams(sem):
    return pltpu.CompilerParams(dimension_semantics=sem, vmem_limit_bytes=VMEM_LIMIT)


def _dg(a, b, ca, cb, prec=None):
    return lax.dot_general(a, b, (((ca,), (cb,)), ((), ())), precision=prec, preferred_element_type=F32)


def _split(x):
    hi = x.astype(BF16)
    return hi, (x - hi.astype(F32)).astype(BF16)


def _dg3_many(as_, bs, ca, cb):
    sa = [_split(a) for a in as_]
    sb = [_split(b) for b in bs]
    hh = [_dg(a[0], b[0], ca, cb) for a, b in zip(sa, sb)]
    hl = [_dg(a[0], b[1], ca, cb) for a, b in zip(sa, sb)]
    lh = [_dg(a[1], b[0], ca, cb) for a, b in zip(sa, sb)]
    return [x + (y + z) for x, y, z in zip(hh, hl, lh)]


def _dg_exact_lhs_many(a01, bs, ca, cb):
    a = a01.astype(BF16)
    b1 = [b.astype(BF16) for b in bs]
    r1 = [b - t.astype(F32) for b, t in zip(bs, b1)]
    b2 = [r.astype(BF16) for r in r1]
    b3 = [(r - t.astype(F32)).astype(BF16) for r, t in zip(r1, b2)]
    d1 = [_dg(a, t, ca, cb) for t in b1]
    d2 = [_dg(a, t, ca, cb) for t in b2]
    d3 = [_dg(a, t, ca, cb) for t in b3]
    return [x + (y + z) for x, y, z in zip(d1, d2, d3)]


def _mm(a, b):
    return _dg(a.astype(BF16), b.astype(BF16), 1, 0)


def _mm_nt(a, b):
    return _dg(a.astype(BF16), b.astype(BF16), 1, 1)


def _mm_tn(a, b):
    return _dg(a.astype(BF16), b.astype(BF16), 0, 0)


def _sigmoid(x):
    return 1.0 / (1.0 + jnp.exp(-x))


def _silu(x):
    return x * _sigmoid(x)


def _dsilu(x):
    s = _sigmoid(x)
    return s * (1.0 + x * (1.0 - s))


_GC = math.sqrt(2.0 / math.pi)


def _gelu(x):
    return 0.5 * x * (1.0 + jnp.tanh(_GC * (x + 0.044715 * x * x * x)))


def _dgelu(x):
    t = jnp.tanh(_GC * (x + 0.044715 * x * x * x))
    return 0.5 * (1.0 + t) + 0.5 * x * (1.0 - t * t) * _GC * (1.0 + 3.0 * 0.044715 * x * x)


def _softplus(x):
    return jnp.maximum(x, 0.0) + jnp.log(1.0 + jnp.exp(-jnp.abs(x)))


def _acc(ref, val, i):
    @pl.when(i == 0)
    def _():
        ref[...] = val

    @pl.when(i > 0)
    def _():
        ref[...] += val


def _tok_call(body, name, S, TB, tok_in, const_in=(), tok_out=(), acc_out=(), prev_in=(), next_in=(), smem_in=()):
    nb = S // TB
    r8 = TB // 8
    in_specs, args = [], []
    for a, w, cb in tok_in:
        in_specs.append(pl.BlockSpec((TB, w), functools.partial(lambda i, cb: (i, cb), cb=cb)))
        args.append(a)
    for a, w, cb in prev_in:
        in_specs.append(pl.BlockSpec((8, w), functools.partial(lambda i, cb: (jnp.maximum(i * r8 - 1, 0), cb), cb=cb)))
        args.append(a)
    for a, w, cb in next_in:
        in_specs.append(pl.BlockSpec((8, w), functools.partial(
            lambda i, cb: (jnp.minimum((i + 1) * r8, S // 8 - 1), cb), cb=cb)))
        args.append(a)
    for a in const_in:
        in_specs.append(pl.BlockSpec(a.shape, lambda i: (0, 0)))
        args.append(a)
    for a in smem_in:
        in_specs.append(pl.BlockSpec(memory_space=pltpu.SMEM))
        args.append(a)
    out_specs, out_shape = [], []
    for w, dt in tok_out:
        out_specs.append(pl.BlockSpec((TB, w), lambda i: (i, 0)))
        out_shape.append(jax.ShapeDtypeStruct((S, w), dt))
    for shp, dt in acc_out:
        out_specs.append(pl.BlockSpec(shp, lambda i: (0, 0)))
        out_shape.append(jax.ShapeDtypeStruct(shp, dt))

    def kern(*refs):
        body(pl.program_id(0), *refs)

    return pl.pallas_call(
        kern, name=name, grid=(nb,), in_specs=in_specs, out_specs=out_specs, out_shape=out_shape,
        compiler_params=_cparams(("arbitrary",)),
    )(*args)


MM_BLOCKS = (1024, 1408, 640, 512, 256, 128)


def _pick(n, cands):
    for c in cands:
        if n % c == 0:
            return c
    return n


def _matmul(a, b, *, ta=False, tb=False, add=None, out_dtype=F32, name):
    M, K = (a.shape[1], a.shape[0]) if ta else a.shape
    N = b.shape[0] if tb else b.shape[1]
    bm, bn, bk = _pick(M, MM_BLOCKS), _pick(N, MM_BLOCKS), _pick(K, MM_BLOCKS)
    nk = K // bk
    a_spec = pl.BlockSpec((bk, bm), lambda i, j, k: (k, i)) if ta else pl.BlockSpec((bm, bk), lambda i, j, k: (i, k))
    b_spec = pl.BlockSpec((bn, bk), lambda i, j, k: (j, k)) if tb else pl.BlockSpec((bk, bn), lambda i, j, k: (k, j))
    o_spec = pl.BlockSpec((bm, bn), lambda i, j, k: (i, j))
    ca, cb = (0 if ta else 1), (1 if tb else 0)

    def kern(*refs):
        if add is None:
            a_ref, b_ref, o_ref, acc_ref = refs
        else:
            a_ref, b_ref, add_ref, o_ref, acc_ref = refs
        k = pl.program_id(2)
        p = _dg(a_ref[...].astype(BF16), b_ref[...].astype(BF16), ca, cb)

        @pl.when(k == 0)
        def _():
            acc_ref[...] = p

        @pl.when(k > 0)
        def _():
            acc_ref[...] += p

        @pl.when(k == nk - 1)
        def _():
            r = acc_ref[...]
            if add is not None:
                r = r + add_ref[...].astype(F32)
            o_ref[...] = r.astype(out_dtype)

    in_specs = [a_spec, b_spec] + ([o_spec] if add is not None else [])
    args = (a, b) + ((add,) if add is not None else ())
    return pl.pallas_call(
        kern, name=name, grid=(M // bm, N // bn, nk), in_specs=in_specs, out_specs=o_spec,
        out_shape=jax.ShapeDtypeStruct((M, N), out_dtype), scratch_shapes=[pltpu.VMEM((bm, bn), F32)],
        compiler_params=_cparams(("parallel", "parallel", "arbitrary")),
    )(*args)


def _rms_fwd(x, g, name):
    S = x.shape[0]

    def body(i, x_ref, g_ref, h_ref):
        xv = x_ref[...]
        r = lax.rsqrt(jnp.mean(xv * xv, axis=-1, keepdims=True) + EPS)
        h_ref[...] = (xv * r * g_ref[...]).astype(BF16)

    return _tok_call(body, name, S, min(S, 512), [(x, D_MODEL, 0)], [g], [(D_MODEL, BF16)])[0]


def _rms_bwd_vals(xv, g, dh):
    r = lax.rsqrt(jnp.mean(xv * xv, axis=-1, keepdims=True) + EPS)
    u = dh * g
    dx = r * u - xv * (r * r * r) * jnp.mean(u * xv, axis=-1, keepdims=True)
    dg = jnp.sum(dh * xv * r, axis=0, keepdims=True)
    return dx, dg


def _rms_bwd_add(x, g, dh, dres, name):
    S = x.shape[0]

    def body(i, x_ref, dh_ref, dr_ref, g_ref, dx_ref, dg_ref):
        dx, dg = _rms_bwd_vals(x_ref[...], g_ref[...], dh_ref[...].astype(F32))
        dx_ref[...] = dr_ref[...] + dx
        _acc(dg_ref, dg, i)

    return _tok_call(body, name, S, min(S, 512), [(x, D_MODEL, 0), (dh, D_MODEL, 0), (dres, D_MODEL, 0)], [g],
                     [(D_MODEL, F32)], [((1, D_MODEL), F32)])


def _final_loss(x, g, target):
    S = x.shape[0]

    def body(i, x_ref, t_ref, g_ref, dx_ref, loss_ref, dg_ref):
        xv, gv = x_ref[...], g_ref[...]
        r = lax.rsqrt(jnp.mean(xv * xv, axis=-1, keepdims=True) + EPS)
        e = xv * r * gv - t_ref[...]
        part = 0.5 * jnp.sum(jnp.mean(e * e, axis=-1, keepdims=True), axis=0, keepdims=True)
        dx, dg = _rms_bwd_vals(xv, gv, e * (1.0 / D_MODEL))
        dx_ref[...] = dx
        _acc(loss_ref, jnp.broadcast_to(part, (1, 128)), i)
        _acc(dg_ref, dg, i)

    return _tok_call(body, "final_loss", S, min(S, 512), [(x, D_MODEL, 0), (target, D_MODEL, 0)], [g],
                     [(D_MODEL, F32)], [((1, 128), F32), ((1, D_MODEL), F32)])


def _swiglu_fwd(gu, name):
    S = gu.shape[0]

    def body(i, gu_ref, a_ref):
        a_ref[...] = (_silu(gu_ref[:, :D_FF]) * gu_ref[:, D_FF:]).astype(BF16)

    return _tok_call(body, name, S, min(S, 256), [(gu, 2 * D_FF, 0)], [], [(D_FF, BF16)])[0]


def _swiglu_bwd(gu, dact, name):
    S = gu.shape[0]

    def body(i, gu_ref, da_ref, dgu_ref):
        gg, uu, da = gu_ref[:, :D_FF], gu_ref[:, D_FF:], da_ref[...]
        dgu_ref[:, :D_FF] = (da * uu * _dsilu(gg)).astype(BF16)
        dgu_ref[:, D_FF:] = (da * _silu(gg)).astype(BF16)

    return _tok_call(body, name, S, min(S, 256), [(gu, 2 * D_FF, 0), (dact, D_FF, 0)], [], [(2 * D_FF, BF16)])[0]


def _merge_fwd(proj, bds, name):
    S = proj.shape[0]

    def body(i, g0, g1, g2, b0, b1, b2, m_ref):
        m = _sigmoid(g0[...]) * b0[...] + _sigmoid(g1[...]) * b1[...] + _sigmoid(g2[...]) * b2[...]
        m_ref[...] = m.astype(BF16)

    tok = [(proj, D_MODEL, n) for n in range(3)] + [(b, D_MODEL, 0) for b in bds]
    return _tok_call(body, name, S, min(S, 512), tok, [], [(D_MODEL, BF16)])[0]


def _merge_bwd(proj, bds, dm, name):
    S = proj.shape[0]

    def body(i, g0, g1, g2, b0, b1, b2, dm_ref, d0, d1, d2, dgp_ref):
        dmv = dm_ref[...]
        for n, (gr, br, dr) in enumerate(((g0, b0, d0), (g1, b1, d1), (g2, b2, d2))):
            s = _sigmoid(gr[...])
            dr[...] = (dmv * s).astype(BF16)
            dgp_ref[:, n * D_MODEL:(n + 1) * D_MODEL] = (dmv * br[...] * s * (1.0 - s)).astype(BF16)

    tok = [(proj, D_MODEL, n) for n in range(3)] + [(b, D_MODEL, 0) for b in bds] + [(dm, D_MODEL, 0)]
    return _tok_call(body, name, S, min(S, 512), tok, [],
                     [(D_MODEL, BF16)] * 3 + [(3 * D_MODEL, BF16)])


def _sgu_ln(v, lg, lb):
    mu = jnp.mean(v, axis=-1, keepdims=True)
    vc = v - mu
    rstd = lax.rsqrt(jnp.mean(vc * vc, axis=-1, keepdims=True) + EPS)
    vhat = vc * rstd
    return vhat, rstd, vhat * lg + lb


def _sgu_fwd(proj, lg, lb, wc, bst, name):
    S = proj.shape[0]

    def body(i, ua_ref, va_ref, lg_ref, lb_ref, wc_ref, bs_ref, o_ref):
        u = _gelu(ua_ref[...])
        _, _, vn = _sgu_ln(_gelu(va_ref[...]), lg_ref[...], lb_ref[...])
        for g in range(SGU_G):
            sl = slice(g * 128, (g + 1) * 128)
            mixed = _mm(wc_ref[sl, :], vn[:, sl]) + bs_ref[:, g:g + 1]
            o_ref[:, sl] = (u[:, sl] * mixed).astype(BF16)

    return _tok_call(body, name, S, SGU_T, [(proj, MIX, C_UA // MIX), (proj, MIX, C_VA // MIX)], [lg, lb, wc, bst],
                     [(MIX, BF16)])[0]


def _sgu_bwd(proj, lg, lb, wc, bst, dout, name):
    S = proj.shape[0]

    def body(i, ua_ref, va_ref, do_ref, lg_ref, lb_ref, wc_ref, bs_ref, dua_ref, dva_ref, dlg_ref, dlb_ref, dwc_ref,
             dbs_ref):
        ua, va, do = ua_ref[...], va_ref[...], do_ref[...].astype(F32)
        u = _gelu(ua)
        lgv = lg_ref[...]
        vhat, rstd, vn = _sgu_ln(_gelu(va), lgv, lb_ref[...])
        tril = lax.broadcasted_iota(jnp.int32, (128, 128), 0) >= lax.broadcasted_iota(jnp.int32, (128, 128), 1)
        lane4 = lax.broadcasted_iota(jnp.int32, (128, 4), 1)
        dvn_parts, dbs = [], jnp.zeros((128, 4), F32)
        for g in range(SGU_G):
            sl = slice(g * 128, (g + 1) * 128)
            wg = wc_ref[sl, :]
            mixed = _mm(wg, vn[:, sl]) + bs_ref[:, g:g + 1]
            dua_ref[:, sl] = (do[:, sl] * mixed * _dgelu(ua[:, sl])).astype(BF16)
            dmix = do[:, sl] * u[:, sl]
            dbs = dbs + jnp.where(lane4 == g, jnp.sum(dmix, axis=-1, keepdims=True), 0.0)
            dwg = jnp.where(tril, _mm_nt(dmix, vn[:, sl]), 0.0)
            _acc(dwc_ref.at[sl, :], dwg, i)
            dvn_parts.append(_mm_tn(wg, dmix))
        dvn = jnp.concatenate(dvn_parts, axis=1)
        _acc(dbs_ref, dbs, i)
        _acc(dlg_ref, jnp.sum(dvn * vhat, axis=0, keepdims=True), i)
        _acc(dlb_ref, jnp.sum(dvn, axis=0, keepdims=True), i)
        dvh = dvn * lgv
        dv = rstd * (dvh - jnp.mean(dvh, axis=-1, keepdims=True) - vhat * jnp.mean(dvh * vhat, axis=-1, keepdims=True))
        dva_ref[...] = (dv * _dgelu(va)).astype(BF16)

    return _tok_call(body, name, S, SGU_T, [(proj, MIX, C_UA // MIX), (proj, MIX, C_VA // MIX), (dout, MIX, 0)],
                     [lg, lb, wc, bst], [(MIX, BF16), (MIX, BF16)],
                     [((1, MIX), F32), ((1, MIX), F32), ((SGU_G * 128, 128), F32), ((128, 4), F32)])


def _rope_tables(positions):
    S = positions.shape[0]
    inv_freq = ROPE_THETA ** (-jnp.arange(0, ROPE_DIM, 2, dtype=F32) / ROPE_DIM)
    ang = positions.astype(F32)[:, None] * inv_freq
    c, s = jnp.cos(ang), jnp.sin(ang)
    c64 = jnp.concatenate([c, c, jnp.ones((S, SWA_HD - ROPE_DIM), F32)], axis=1)
    s64 = jnp.concatenate([-s, s, jnp.zeros((S, SWA_HD - ROPE_DIM), F32)], axis=1)
    return jnp.tile(c64, (1, 2)), jnp.tile(s64, (1, 2))


def _rope128(x, c, s):
    lane = lax.broadcasted_iota(jnp.int32, x.shape, 1) % SWA_HD
    swapped = jnp.where(lane < ROPE_DIM // 2, pltpu.roll(x, 128 - ROPE_DIM // 2, 1), pltpu.roll(x, ROPE_DIM // 2, 1))
    return x * c + swapped * s


def _rope_t128(y, c, s):
    ys = y * s
    lane = lax.broadcasted_iota(jnp.int32, y.shape, 1) % SWA_HD
    swapped = jnp.where(lane < ROPE_DIM // 2, pltpu.roll(ys, 128 - ROPE_DIM // 2, 1), pltpu.roll(ys, ROPE_DIM // 2, 1))
    return y * c + jnp.where(lane < ROPE_DIM, swapped, 0.0)


def _rope_fwd(proj, cos, sin, name):
    S = proj.shape[0]
    scale = SWA_HD ** -0.5

    def body(i, q_ref, k_ref, v_ref, c_ref, s_ref, qo_ref, ko_ref, vo_ref):
        c, s = c_ref[...], s_ref[...]
        for j in range(4):
            sl = slice(j * 128, (j + 1) * 128)
            qo_ref[:, sl] = (_rope128(q_ref[:, sl], c, s) * scale).astype(BF16)
        ko_ref[...] = _rope128(k_ref[...], c, s).astype(BF16)
        vo_ref[...] = v_ref[...].astype(BF16)

    return _tok_call(body, name, S, min(S, 512),
                     [(proj, MIX, C_QB // MIX), (proj, 128, C_KB // 128), (proj, 128, C_VB // 128), (cos, 128, 0),
                      (sin, 128, 0)], [], [(MIX, BF16), (128, BF16), (128, BF16)])


def _rope_bwd(dq, dk, dv, cos, sin, name):
    S = dq.shape[0]
    scale = SWA_HD ** -0.5

    def body(i, dq_ref, dk_ref, dv_ref, c_ref, s_ref, qo_ref, ko_ref, vo_ref):
        c, s = c_ref[...], s_ref[...]
        for j in range(4):
            sl = slice(j * 128, (j + 1) * 128)
            qo_ref[:, sl] = _rope_t128(dq_ref[:, sl] * scale, c, s).astype(BF16)
        ko_ref[...] = _rope_t128(dk_ref[...], c, s).astype(BF16)
        vo_ref[...] = dv_ref[...].astype(BF16)

    return _tok_call(body, name, S, min(S, 512),
                     [(dq, MIX, 0), (dk, 128, 0), (dv, 128, 0), (cos, 128, 0), (sin, 128, 0)], [],
                     [(MIX, BF16), (128, BF16), (128, BF16)])


def _swa_band(i, k_ref, v_ref):
    pstart = pl.multiple_of(jnp.maximum(i - 1, 0) * WINDOW, WINDOW)
    cstart = pl.multiple_of(i * WINDOW, WINDOW)
    kb = jnp.concatenate([k_ref[pl.ds(pstart, WINDOW), :], k_ref[pl.ds(cstart, WINDOW), :]], axis=0)
    vb = jnp.concatenate([v_ref[pl.ds(pstart, WINDOW), :], v_ref[pl.ds(cstart, WINDOW), :]], axis=0)
    qi = lax.broadcasted_iota(jnp.int32, (WINDOW, 2 * WINDOW), 0)
    sj = lax.broadcasted_iota(jnp.int32, (WINDOW, 2 * WINDOW), 1)
    mask = (sj > qi) & (sj <= qi + WINDOW) & ((i > 0) | (sj >= WINDOW))
    return kb, vb, mask, pstart, cstart


def _swa_probs(qh, kh, mask, sink):
    logits = jnp.where(mask, _dg(qh, kh, 1, 1), -1e30)
    m = jnp.maximum(jnp.max(logits, axis=-1, keepdims=True), sink)
    p = jnp.exp(logits - m)
    ps = jnp.exp(sink - m)
    inv = 1.0 / (jnp.sum(p, axis=-1, keepdims=True) + ps)
    return p * inv, ps * inv


def _swa_fwd(q, k, v, sinks, name):
    S = q.shape[0]

    def body(i, q_ref, k_ref, v_ref, s_ref, o_ref):
        kb, vb, mask, _, _ = _swa_band(i, k_ref, v_ref)
        qv = q_ref[...]
        for h in range(SWA_H):
            kv = h // (SWA_H // SWA_KV)
            ksl = slice(kv * SWA_HD, (kv + 1) * SWA_HD)
            hsl = slice(h * SWA_HD, (h + 1) * SWA_HD)
            pn, _ = _swa_probs(qv[:, hsl], kb[:, ksl], mask, s_ref[0, h])
            o_ref[:, hsl] = _dg(pn.astype(BF16), vb[:, ksl], 1, 0).astype(BF16)

    return _tok_call(body, name, S, WINDOW, [(q, MIX, 0)], [k, v], [(MIX, BF16)], smem_in=[sinks])[0]


def _swa_bwd(q, k, v, sinks, dout, name):
    S = q.shape[0]

    def body(i, q_ref, do_ref, k_ref, v_ref, s_ref, dq_ref, dk_ref, dv_ref, ds_ref):
        kb, vb, mask, pstart, cstart = _swa_band(i, k_ref, v_ref)
        qv, dov = q_ref[...], do_ref[...]
        lane = lax.broadcasted_iota(jnp.int32, (1, 128), 1)
        dsink = jnp.zeros((1, 128), F32)
        dkb, dvb = [], []
        for kv in range(SWA_KV):
            ksl = slice(kv * SWA_HD, (kv + 1) * SWA_HD)
            dk_kv = jnp.zeros((2 * WINDOW, SWA_HD), F32)
            dv_kv = jnp.zeros((2 * WINDOW, SWA_HD), F32)
            for gq in range(SWA_H // SWA_KV):
                h = kv * (SWA_H // SWA_KV) + gq
                hsl = slice(h * SWA_HD, (h + 1) * SWA_HD)
                qh, doh = qv[:, hsl], dov[:, hsl].astype(BF16)
                pn, psn = _swa_probs(qh, kb[:, ksl], mask, s_ref[0, h])
                dp = _dg(doh, vb[:, ksl], 1, 1)
                delta = jnp.sum(dp * pn, axis=-1, keepdims=True)
                dsc = (pn * (dp - delta)).astype(BF16)
                dq_ref[:, hsl] = _dg(dsc, kb[:, ksl], 1, 0)
                dk_kv = dk_kv + _dg(dsc, qh, 0, 0)
                dv_kv = dv_kv + _dg(pn.astype(BF16), doh, 0, 0)
                dsink = dsink + jnp.where(lane == h, -jnp.sum(psn * delta, axis=0, keepdims=True), 0.0)
            dkb.append(dk_kv)
            dvb.append(dv_kv)
        dkb = jnp.concatenate(dkb, axis=1)
        dvb = jnp.concatenate(dvb, axis=1)

        @pl.when(i == 0)
        def _():
            dk_ref[...] = jnp.zeros_like(dk_ref)
            dv_ref[...] = jnp.zeros_like(dv_ref)

        dk_ref[pl.ds(pstart, WINDOW), :] += dkb[:WINDOW]
        dv_ref[pl.ds(pstart, WINDOW), :] += dvb[:WINDOW]
        dk_ref[pl.ds(cstart, WINDOW), :] += dkb[WINDOW:]
        dv_ref[pl.ds(cstart, WINDOW), :] += dvb[WINDOW:]
        _acc(ds_ref, dsink, i)

    return _tok_call(body, name, S, WINDOW, [(q, MIX, 0), (dout, MIX, 0)], [k, v], [(MIX, F32)],
                     [((S, 128), F32), ((S, 128), F32), ((1, 128), F32)], smem_in=[sinks])


def _shift_rows(xs, k):
    return xs if k == 0 else pltpu.roll(xs, k, 0)


def _dn_conv(x_ref, p_ref, w_ref, i):
    halo = jnp.where(i > 0, p_ref[...], 0.0)
    xs = jnp.concatenate([halo, x_ref[...]], axis=0)
    sh = [_shift_rows(xs, DN_CONV - 1 - t)[8:] for t in range(DN_CONV)]
    pre = sh[0] * w_ref[0:1, :]
    for t in range(1, DN_CONV):
        pre = pre + sh[t] * w_ref[t:t + 1, :]
    return pre, sh


def _dn_gates(sm, alog, dtb):
    lane = lax.broadcasted_iota(jnp.int32, sm.shape, 1)
    return jnp.where(lane < DN_H, _sigmoid(sm), -jnp.exp(alog) * _softplus(sm + dtb))


def _dn_pre_fwd(proj, conv_w, alog_l, dtb_l, name):
    S = proj.shape[0]
    scale = DN_HD ** -0.5

    def body(i, x_ref, sm_ref, p_ref, w_ref, al_ref, db_ref, q_ref, k_ref, v_ref, bg_ref):
        pre, _ = _dn_conv(x_ref, p_ref, w_ref, i)
        a = _silu(pre)
        for h in range(DN_H):
            sl = slice(h * DN_HD, (h + 1) * DN_HD)
            qh, kh = a[:, sl], a[:, MIX + h * DN_HD:MIX + (h + 1) * DN_HD]
            q_ref[:, sl] = qh * (lax.rsqrt(jnp.sum(qh * qh, axis=-1, keepdims=True) + EPS) * scale)
            k_ref[:, sl] = kh * lax.rsqrt(jnp.sum(kh * kh, axis=-1, keepdims=True) + EPS)
        v_ref[...] = a[:, 2 * MIX:]
        bg_ref[...] = _dn_gates(sm_ref[...], al_ref[...], db_ref[...])

    TB = min(S, 256)
    return _tok_call(body, name, S, TB, [(proj, 3 * MIX, C_QKV // (3 * MIX)), (proj, 128, C_SM // 128)],
                     [conv_w, alog_l, dtb_l], [(MIX, F32), (MIX, F32), (MIX, F32), (128, F32)],
                     prev_in=[(proj, 3 * MIX, C_QKV // (3 * MIX))])


def _dn_pre_bwd1(proj, conv_w, alog_l, dtb_l, dq, dk, dv, dbg, name):
    S = proj.shape[0]
    scale = DN_HD ** -0.5

    def body(i, x_ref, sm_ref, dq_ref, dk_ref, dv_ref, dbg_ref, p_ref, w_ref, al_ref, db_ref, dpre_ref, dsm_ref,
             dw_ref, dal_ref, ddb_ref):
        pre, sh = _dn_conv(x_ref, p_ref, w_ref, i)
        a = _silu(pre)
        da_parts = []
        for part, (g_ref, sc) in enumerate(((dq_ref, scale), (dk_ref, 1.0))):
            for h in range(DN_H):
                xh = a[:, part * MIX + h * DN_HD:part * MIX + (h + 1) * DN_HD]
                rs = lax.rsqrt(jnp.sum(xh * xh, axis=-1, keepdims=True) + EPS)
                y = xh * rs
                dy = g_ref[:, h * DN_HD:(h + 1) * DN_HD] * sc
                da_parts.append(rs * (dy - y * jnp.sum(dy * y, axis=-1, keepdims=True)))
        da_parts.append(dv_ref[...])
        dpre = jnp.concatenate(da_parts, axis=1) * _dsilu(pre)
        dpre_ref[...] = dpre
        dw = jnp.concatenate([jnp.sum(dpre * sh[t], axis=0, keepdims=True) for t in range(DN_CONV)], axis=0)
        _acc(dw_ref, dw, i)
        sm, al, db, dbg_v = sm_ref[...], al_ref[...], db_ref[...], dbg_ref[...]
        lane = lax.broadcasted_iota(jnp.int32, sm.shape, 1)
        sg = _sigmoid(sm)
        gneg = -jnp.exp(al)
        is_g = (lane >= DN_H) & (lane < 2 * DN_H)
        d_al = jnp.where(is_g, dbg_v * gneg * _sigmoid(sm + db), 0.0)
        dsm_ref[...] = jnp.where(lane < DN_H, dbg_v * sg * (1.0 - sg), d_al).astype(BF16)
        _acc(ddb_ref, jnp.sum(d_al, axis=0, keepdims=True), i)
        _acc(dal_ref, jnp.sum(jnp.where(is_g, dbg_v * gneg * _softplus(sm + db), 0.0), axis=0, keepdims=True), i)

    TB = min(S, 256)
    return _tok_call(body, name, S, TB,
                     [(proj, 3 * MIX, C_QKV // (3 * MIX)), (proj, 128, C_SM // 128), (dq, MIX, 0), (dk, MIX, 0),
                      (dv, MIX, 0), (dbg, 128, 0)], [conv_w, alog_l, dtb_l],
                     [(3 * MIX, F32), (128, BF16)], [((DN_CONV, 3 * MIX), F32), ((1, 128), F32), ((1, 128), F32)],
                     prev_in=[(proj, 3 * MIX, C_QKV // (3 * MIX))])


def _dn_pre_bwd2(dpre, conv_w, name):
    S = dpre.shape[0]
    TB = min(S, 256)
    nb = S // TB

    def body(i, d_ref, n_ref, w_ref, o_ref):
        halo = jnp.where(i < nb - 1, n_ref[...], 0.0)
        ds = jnp.concatenate([d_ref[...], halo], axis=0)
        out = ds[:TB] * w_ref[DN_CONV - 1:DN_CONV, :]
        for t in range(DN_CONV - 1):
            k = DN_CONV - 1 - t
            out = out + pltpu.roll(ds, TB + 8 - k, 0)[:TB] * w_ref[t:t + 1, :]
        o_ref[...] = out.astype(BF16)

    return _tok_call(body, name, S, TB, [(dpre, 3 * MIX, 0)], [conv_w], [(3 * MIX, BF16)],
                     next_in=[(dpre, 3 * MIX, 0)])[0]


def _dn_decay_terms(bgs, heads):
    C = DN_C
    ri = lax.broadcasted_iota(jnp.int32, (C, C), 0)
    ci = lax.broadcasted_iota(jnp.int32, (C, C), 1)
    tril, eye = ri >= ci, ri == ci
    beta = [b[:, h:h + 1] for b, h in zip(bgs, heads)]
    gcol = _dg_exact_lhs_many(tril, [jnp.broadcast_to(b[:, DN_H + h:DN_H + h + 1], (C, C))
                                     for b, h in zip(bgs, heads)], 1, 0)
    grow = [jnp.sum(jnp.where(eye, g, 0.0), axis=0, keepdims=True) for g in gcol]
    decay = [jnp.exp(jnp.where(tril, g - r, -1e30)) for g, r in zip(gcol, grow)]
    e_gc = [jnp.exp(g[:, 0:1]) for g in gcol]
    e_kd = [jnp.exp(g[C - 1:C, 0:1] - g[:, 0:1]) for g in gcol]
    cdec = [jnp.exp(g[C - 1:C, 0:1]) for g in gcol]
    return beta, decay, e_gc, e_kd, cdec


def _dn_nb(S):
    return 4 if S % (4 * DN_C) == 0 else 1


def _dn_prep_fwd(q, k, v, bg, name):
    S = q.shape[0]
    C, NB = DN_C, _dn_nb(S)
    TB = NB * C

    def kern(q_ref, k_ref, v_ref, bg_ref, t_ref, uw_ref, at_ref, qd_ref, kd_ref, dec_ref):
        lane = lax.broadcasted_iota(jnp.int32, (C, 128), 1)
        ri = lax.broadcasted_iota(jnp.int32, (C, C), 0)
        ci = lax.broadcasted_iota(jnp.int32, (C, C), 1)
        tril, eye = ri >= ci, ri == ci
        chains = [(cb, h) for cb in range(NB) for h in range(DN_H)]
        rows = lambda cb: slice(cb * C, (cb + 1) * C)
        head = lambda h: slice(h * DN_HD, (h + 1) * DN_HD)
        beta, decay, e_gc, e_kd, cdec = _dn_decay_terms([bg_ref[rows(cb), :] for cb, _ in chains],
                                                        [h for _, h in chains])
        qs = [q_ref[rows(cb), head(h)] for cb, h in chains]
        ks = [k_ref[rows(cb), head(h)] for cb, h in chains]
        kb = [kh * b for kh, b in zip(ks, beta)]
        x = [-jnp.where(ri > ci, _mm_nt(a, kh) * d, 0.0) for a, kh, d in zip(kb, ks, decay)]
        tm = [jnp.where(eye, 1.0, 0.0) + xi for xi in x]
        p = x
        for _ in range(5):
            p = _dg3_many(p, p, 1, 0)
            tm = [t + tp for t, tp in zip(tm, _dg3_many(tm, p, 1, 0))]
        rhs = [jnp.concatenate([v_ref[rows(cb), head(h)] * b, a * e], axis=1)
               for (cb, h), b, a, e in zip(chains, beta, kb, e_gc)]
        sol = _dg3_many(tm, rhs, 1, 0)
        attn = [_mm_nt(qh, kh) * d for qh, kh, d in zip(qs, ks, decay)]
        for n_, (cb, h) in enumerate(chains):
            rs, sl, hc = rows(cb), head(h), slice(h * C, (h + 1) * C)
            t_ref[rs, hc] = tm[n_]
            uw_ref[rs, sl] = sol[n_][:, :DN_HD]
            uw_ref[rs, MIX + h * DN_HD:MIX + (h + 1) * DN_HD] = sol[n_][:, DN_HD:]
            at_ref[rs, hc] = attn[n_]
            qd_ref[rs, sl] = (qs[n_] * e_gc[n_]).astype(BF16)
            kd_ref[rs, sl] = (ks[n_] * e_kd[n_]).astype(BF16)
        for cb in range(NB):
            dec = jnp.zeros((C, 128), F32)
            for h in range(DN_H):
                dec = dec + jnp.where(lane == h, cdec[cb * DN_H + h], 0.0)
            dec_ref[rows(cb), :] = dec

    tok = lambda w: pl.BlockSpec((TB, w), lambda i: (i, 0))
    return pl.pallas_call(
        kern, name=name, grid=(S // TB,), in_specs=[tok(MIX), tok(MIX), tok(MIX), tok(128)],
        out_specs=[tok(DN_H * C), tok(2 * MIX), tok(DN_H * C), tok(MIX), tok(MIX), tok(128)],
        out_shape=[jax.ShapeDtypeStruct((S, DN_H * C), F32), jax.ShapeDtypeStruct((S, 2 * MIX), F32),
                   jax.ShapeDtypeStruct((S, DN_H * C), F32), jax.ShapeDtypeStruct((S, MIX), BF16),
                   jax.ShapeDtypeStruct((S, MIX), BF16), jax.ShapeDtypeStruct((S, 128), F32)],
        compiler_params=_cparams(("parallel",)),
    )(q, k, v, bg)


def _dn_scan_fwd(uw, at, qd, kd, dec, name):
    S = uw.shape[0]
    C, NB = DN_C, _dn_nb(S)
    TB = NB * C
    SR = DN_H * DN_HD

    def kern(uw_ref, at_ref, qd_ref, kd_ref, dec_ref, o_ref, vn_ref, st_ref, state):
        @pl.when(pl.program_id(0) == 0)
        def _():
            state[...] = jnp.zeros_like(state)

        for cb in range(NB):
            rs = slice(cb * C, (cb + 1) * C)
            hs = range(DN_H)
            sls = [slice(h * DN_HD, (h + 1) * DN_HD) for h in hs]
            s_in = [state[sl, :] for sl in sls]
            ws = [_mm(uw_ref[rs, MIX + h * DN_HD:MIX + (h + 1) * DN_HD], s_in[h]) for h in hs]
            os_ = [_mm(qd_ref[rs, sls[h]], s_in[h]) for h in hs]
            vnew = [uw_ref[rs, sls[h]] - ws[h] for h in hs]
            oa = [_mm(at_ref[rs, h * C:(h + 1) * C], vnew[h]) for h in hs]
            kv = [_mm_tn(kd_ref[rs, sls[h]], vnew[h]) for h in hs]
            for h in hs:
                o_ref[rs, sls[h]] = os_[h] + oa[h]
                state[sls[h], :] = s_in[h] * dec_ref[cb * C:cb * C + 1, h:h + 1] + kv[h]
                st_ref[cb * SR + h * DN_HD:cb * SR + (h + 1) * DN_HD, :] = s_in[h]
                vn_ref[rs, sls[h]] = vnew[h]

    tok = lambda w: pl.BlockSpec((TB, w), lambda i: (i, 0))
    return pl.pallas_call(
        kern, name=name, grid=(S // TB,), in_specs=[tok(2 * MIX), tok(DN_H * C), tok(MIX), tok(MIX), tok(128)],
        out_specs=[tok(MIX), tok(MIX), pl.BlockSpec((NB * SR, DN_HD), lambda i: (i, 0))],
        out_shape=[jax.ShapeDtypeStruct((S, MIX), F32), jax.ShapeDtypeStruct((S, MIX), F32),
                   jax.ShapeDtypeStruct((S // C * SR, DN_HD), F32)],
        scratch_shapes=[pltpu.VMEM((SR, DN_HD), F32)],
        compiler_params=_cparams(("arbitrary",)),
    )(uw, at, qd, kd, dec)


def _dn_core_fwd(q, k, v, bg, name):
    tm, uw, at, qd, kd, dec = _dn_prep_fwd(q, k, v, bg, name + "_prep")
    o, vn, st = _dn_scan_fwd(uw, at, qd, kd, dec, name + "_scan")
    return o, dict(tm=tm, uw=uw, at=at, qd=qd, kd=kd, dec=dec, vn=vn, st=st)


def _dn_scan_bwd(sv, do, name):
    S = do.shape[0]
    C, NB = DN_C, _dn_nb(S)
    TB = NB * C
    SR = DN_H * DN_HD
    nb = S // TB

    def kern(do_ref, uw_ref, at_ref, qd_ref, kd_ref, dec_ref, vn_ref, st_ref, dvn_ref, dw_ref, dkd_ref, dc_ref, dstate):
        @pl.when(pl.program_id(0) == 0)
        def _():
            dstate[...] = jnp.zeros_like(dstate)

        lane = lax.broadcasted_iota(jnp.int32, (C, 128), 1)
        for cb in reversed(range(NB)):
            rs = slice(cb * C, (cb + 1) * C)
            dcrow = jnp.zeros((C, 128), F32)
            for h in range(DN_H):
                sl = slice(h * DN_HD, (h + 1) * DN_HD)
                doh, ds_o = do_ref[rs, sl], dstate[sl, :]
                s_in = st_ref[cb * SR + h * DN_HD:cb * SR + (h + 1) * DN_HD, :]
                d_vnew = _mm_tn(at_ref[rs, h * C:(h + 1) * C], doh) + _mm(kd_ref[rs, sl], ds_o)
                dvn_ref[rs, sl] = d_vnew
                dw_ref[rs, sl] = -_mm_nt(d_vnew, s_in)
                dkd_ref[rs, sl] = _mm_nt(vn_ref[rs, sl], ds_o)
                d_c = jnp.sum(jnp.sum(ds_o * s_in, axis=1, keepdims=True), axis=0, keepdims=True)
                dcrow = dcrow + jnp.where(lane == h, d_c, 0.0)
                dstate[sl, :] = (ds_o * dec_ref[cb * C:cb * C + 1, h:h + 1] + _mm_tn(qd_ref[rs, sl], doh)
                                 - _mm_tn(uw_ref[rs, MIX + h * DN_HD:MIX + (h + 1) * DN_HD], d_vnew))
            dc_ref[rs, :] = dcrow

    tok = lambda w: pl.BlockSpec((TB, w), lambda i: (nb - 1 - i, 0))
    return pl.pallas_call(
        kern, name=name, grid=(nb,),
        in_specs=[tok(MIX), tok(2 * MIX), tok(DN_H * C), tok(MIX), tok(MIX), tok(128), tok(MIX),
                  pl.BlockSpec((NB * SR, DN_HD), lambda i: (nb - 1 - i, 0))],
        out_specs=[tok(MIX), tok(MIX), tok(MIX), tok(128)],
        out_shape=[jax.ShapeDtypeStruct((S, MIX), F32)] * 3 + [jax.ShapeDtypeStruct((S, 128), F32)],
        scratch_shapes=[pltpu.VMEM((SR, DN_HD), F32)],
        compiler_params=_cparams(("arbitrary",)),
    )(do, sv["uw"], sv["at"], sv["qd"], sv["kd"], sv["dec"], sv["vn"], sv["st"])


def _dn_chunk_bwd(q, k, v, bg, sv, do, dvn, dw, dkd, dc, name):
    S = q.shape[0]
    C, NB = DN_C, _dn_nb(S)
    TB = NB * C
    SR = DN_H * DN_HD

    def kern(q_ref, k_ref, v_ref, bg_ref, t_ref, uw_ref, vn_ref, st_ref, do_ref, dvn_ref, dw_ref, dkd_ref, dc_ref,
             dq_ref, dk_ref, dv_ref, dbg_ref):
        lane = lax.broadcasted_iota(jnp.int32, (C, 128), 1)
        ri = lax.broadcasted_iota(jnp.int32, (C, C), 0)
        ci = lax.broadcasted_iota(jnp.int32, (C, C), 1)
        tril, eye, last = ri >= ci, ri == ci, ri[:, 0:1] == C - 1
        chains = [(cb, h) for cb in range(NB) for h in range(DN_H)]
        each = lambda f, *ls: [f(*a) for a in zip(*ls)]
        rsum = lambda t: jnp.sum(t, axis=-1, keepdims=True)
        rows = lambda cb: slice(cb * C, (cb + 1) * C)
        head = lambda h: slice(h * DN_HD, (h + 1) * DN_HD)
        tok = lambda ref: [ref[rows(cb), head(h)] for cb, h in chains]
        beta, decay, e_gc, e_kd, cdec = _dn_decay_terms([bg_ref[rows(cb), :] for cb, _ in chains],
                                                        [h for _, h in chains])
        qs, ks, vs, dos, vnew, d_kd = tok(q_ref), tok(k_ref), tok(v_ref), tok(do_ref), tok(vn_ref), tok(dkd_ref)
        s_in = [st_ref[cb * SR + h * DN_HD:cb * SR + (h + 1) * DN_HD, :] for cb, h in chains]
        d_c = [dc_ref[cb * C:cb * C + 1, h:h + 1] for cb, h in chains]
        kb = each(lambda a, b: a * b, ks, beta)
        kk = each(_mm_nt, kb, ks)
        attn = each(lambda a, b, d: _mm_nt(a, b) * d, qs, ks, decay)
        d_qd = each(_mm_nt, dos, s_in)
        d_attn = each(_mm_nt, dos, vnew)
        d_sol = [jnp.concatenate([dvn_ref[rows(cb), head(h)], dw_ref[rows(cb), head(h)]], axis=1) for cb, h in chains]
        sol = [jnp.concatenate([uw_ref[rows(cb), head(h)], uw_ref[rows(cb), MIX + h * DN_HD:MIX + (h + 1) * DN_HD]],
                               axis=1) for cb, h in chains]
        d_rhs = _dg3_many([t_ref[rows(cb), h * C:(h + 1) * C] for cb, h in chains], d_sol, 0, 0)
        d_a = _dg3_many(d_rhs, sol, 1, 1)
        d_kk = each(lambda a, d: jnp.where(ri > ci, -a, 0.0) * d, d_a, decay)
        d_qk = each(lambda a, d: a * d, d_attn, decay)
        dm = each(lambda a, b, c_, d: a * b + c_ * d, d_kk, kk, d_attn, attn)
        d_vb = [t[:, :DN_HD] for t in d_rhs]
        dz = [t[:, DN_HD:] for t in d_rhs]
        d_kb = each(lambda z, e, a, kh: z * e + _mm(a, kh), dz, e_gc, d_kk, ks)
        d_k = each(lambda a, b, c_, q: _mm_tn(a, b) + _mm_tn(c_, q), d_kk, kb, d_qk, qs)
        d_q = each(lambda a, kh, b, e: _mm(a, kh) + b * e, d_qk, ks, d_qd, e_gc)
        t_kd = each(lambda a, kh, e: rsum(a * kh * e), d_kd, ks, e_kd)
        d_gl = each(lambda t, c_, cd: jnp.sum(t, axis=0, keepdims=True) + c_ * cd, t_kd, d_c, cdec)
        d_gc = each(lambda z, a, e, m, b, q, t, gl:
                    rsum(z * a) * e + rsum(m) - rsum(jnp.where(eye, jnp.sum(m, axis=0, keepdims=True), 0.0))
                    + rsum(b * q) * e - t + jnp.where(last, gl, 0.0),
                    dz, kb, e_gc, dm, d_qd, qs, t_kd, d_gl)
        d_g = _dg_exact_lhs_many(ri <= ci, [jnp.broadcast_to(t, (C, 128)) for t in d_gc], 1, 0)
        d_beta = each(lambda a, v_, b, kh: rsum(a * v_) + rsum(b * kh), d_vb, vs, d_kb, ks)
        for n_, (cb, h) in enumerate(chains):
            dq_ref[rows(cb), head(h)] = d_q[n_]
            dk_ref[rows(cb), head(h)] = d_k[n_] + d_kd[n_] * e_kd[n_] + d_kb[n_] * beta[n_]
            dv_ref[rows(cb), head(h)] = d_vb[n_] * beta[n_]
        for cb in range(NB):
            dbg = jnp.zeros((C, 128), F32)
            for h in range(DN_H):
                n_ = cb * DN_H + h
                dbg = dbg + jnp.where(lane == h, d_beta[n_], 0.0) + jnp.where(lane == DN_H + h, d_g[n_], 0.0)
            dbg_ref[rows(cb), :] = dbg

    tok = lambda w: pl.BlockSpec((TB, w), lambda i: (i, 0))
    return pl.pallas_call(
        kern, name=name, grid=(S // TB,),
        in_specs=[tok(MIX), tok(MIX), tok(MIX), tok(128), tok(DN_H * C), tok(2 * MIX), tok(MIX),
                  pl.BlockSpec((NB * SR, DN_HD), lambda i: (i, 0)), tok(MIX), tok(MIX), tok(MIX), tok(MIX), tok(128)],
        out_specs=[tok(MIX), tok(MIX), tok(MIX), tok(128)],
        out_shape=[jax.ShapeDtypeStruct((S, MIX), F32)] * 3 + [jax.ShapeDtypeStruct((S, 128), F32)],
        compiler_params=_cparams(("parallel",)),
    )(q, k, v, bg, sv["tm"], sv["uw"], sv["vn"], sv["st"], do, dvn, dw, dkd, dc)


def _dn_core_bwd(q, k, v, bg, sv, do, name):
    dvn, dw, dkd, dc = _dn_scan_bwd(sv, do, name + "_scan")
    return _dn_chunk_bwd(q, k, v, bg, sv, do, dvn, dw, dkd, dc, name + "_chunk")


def _dn_post_fwd(o, proj, ng, name):
    S = o.shape[0]

    def body(i, o_ref, z_ref, g_ref, out_ref):
        gv = g_ref[...]
        for h in range(DN_H):
            sl = slice(h * DN_HD, (h + 1) * DN_HD)
            oh = o_ref[:, sl]
            r = lax.rsqrt(jnp.mean(oh * oh, axis=-1, keepdims=True) + EPS)
            out_ref[:, sl] = (oh * r * gv * _silu(z_ref[:, sl])).astype(BF16)

    return _tok_call(body, name, S, min(S, 512), [(o, MIX, 0), (proj, MIX, C_ZC // MIX)], [ng], [(MIX, BF16)])[0]


def _dn_post_bwd(o, proj, ng, dout, name):
    S = o.shape[0]

    def body(i, o_ref, z_ref, do_ref, g_ref, dov_ref, dz_ref, dg_ref):
        gv = g_ref[...]
        dg = jnp.zeros((1, DN_HD), F32)
        for h in range(DN_H):
            sl = slice(h * DN_HD, (h + 1) * DN_HD)
            oh, zh, dh = o_ref[:, sl], z_ref[:, sl], do_ref[:, sl].astype(F32)
            r = lax.rsqrt(jnp.mean(oh * oh, axis=-1, keepdims=True) + EPS)
            dz_ref[:, sl] = (dh * oh * r * gv * _dsilu(zh)).astype(BF16)
            dx, dgh = _rms_bwd_vals(oh, gv, dh * _silu(zh))
            dov_ref[:, sl] = dx
            dg = dg + dgh
        _acc(dg_ref, dg, i)

    return _tok_call(body, name, S, min(S, 512), [(o, MIX, 0), (proj, MIX, C_ZC // MIX), (dout, MIX, 0)], [ng],
                     [(MIX, F32), (MIX, BF16)], [((1, DN_HD), F32)])


def _layer_params(w, l):
    lane = jnp.arange(128)
    is_g = (lane >= DN_H) & (lane < 2 * DN_H)
    spread = lambda t: jnp.where(is_g, jnp.tile(t, 128 // DN_H), 0.0).reshape(1, 128)
    tril = jnp.tril(jnp.ones((SGU_T, SGU_T), bool))
    return dict(
        win=w["w_in"][l], wb=w["w_branch"][l], wout=w["w_out"][l], wgu=w["w_gate_up"][l], wdown=w["w_down"][l],
        conv=w["dn_conv_w"][l], attn_norm=w["attn_norm"][l].reshape(1, -1), ffn_norm=w["ffn_norm"][l].reshape(1, -1),
        lg=w["sgu_ln_g"][l].reshape(1, -1), lb=w["sgu_ln_b"][l].reshape(1, -1),
        wc=jnp.where(tril, w["sgu_w"][l], 0.0).reshape(SGU_G * SGU_T, SGU_T), bst=w["sgu_b"][l].T,
        sinks=w["attn_sinks"][l].reshape(1, -1), alog=spread(w["dn_a_log"][l]), dtb=spread(w["dn_dt_bias"][l]),
        ng=w["dn_norm"][l].reshape(1, -1))


def _layer_fwd(x, p, cos, sin, l):
    n = lambda s: f"l{l}_{s}"
    h = _rms_fwd(x, p["attn_norm"], n("rms1"))
    proj = _matmul(h, p["win"], name=n("mm_in"))
    out_a = _sgu_fwd(proj, p["lg"], p["lb"], p["wc"], p["bst"], n("sgu_fwd"))
    qr, kr, vr = _rope_fwd(proj, cos, sin, n("rope_fwd"))
    out_b = _swa_fwd(qr, kr, vr, p["sinks"], n("swa_fwd"))
    q, k, v, bg = _dn_pre_fwd(proj, p["conv"], p["alog"], p["dtb"], n("dn_pre_fwd"))
    o, dn = _dn_core_fwd(q, k, v, bg, n("dn_core_fwd"))
    out_c = _dn_post_fwd(o, proj, p["ng"], n("dn_post_fwd"))
    outs = (out_a, out_b, out_c)
    bds = [_matmul(outs[j], p["wb"][j], name=n(f"mm_branch{j}")) for j in range(3)]
    merged = _merge_fwd(proj, bds, n("merge_fwd"))
    x1 = _matmul(merged, p["wout"], add=x, name=n("mm_out"))
    h2 = _rms_fwd(x1, p["ffn_norm"], n("rms2"))
    gu = _matmul(h2, p["wgu"], name=n("mm_gu"))
    act = _swiglu_fwd(gu, n("swiglu_fwd"))
    x2 = _matmul(act, p["wdown"], add=x1, name=n("mm_down"))
    saved = dict(x=x, h=h, proj=proj, outs=outs, qr=qr, kr=kr, vr=vr, q=q, k=k, v=v, bg=bg, o=o, dn=dn, bds=bds,
                 merged=merged, x1=x1, h2=h2, gu=gu, act=act)
    return x2, saved


def _layer_bwd(dx2, s, p, cos, sin, l):
    n = lambda t: f"l{l}_{t}"
    proj = s["proj"]
    g = {}
    g["w_down"] = _matmul(s["act"], dx2, ta=True, out_dtype=BF16, name=n("wg_down"))
    dact = _matmul(dx2, p["wdown"], tb=True, name=n("dg_down"))
    dgu = _swiglu_bwd(s["gu"], dact, n("swiglu_bwd"))
    g["w_gate_up"] = _matmul(s["h2"], dgu, ta=True, out_dtype=BF16, name=n("wg_gu"))
    dh2 = _matmul(dgu, p["wgu"], tb=True, name=n("dg_gu"))
    dx1, g["ffn_norm"] = _rms_bwd_add(s["x1"], p["ffn_norm"], dh2, dx2, n("rms2_bwd"))
    g["w_out"] = _matmul(s["merged"], dx1, ta=True, out_dtype=BF16, name=n("wg_out"))
    dm = _matmul(dx1, p["wout"], tb=True, name=n("dg_out"))
    dbd0, dbd1, dbd2, dgp = _merge_bwd(proj, s["bds"], dm, n("merge_bwd"))
    dbds = (dbd0, dbd1, dbd2)
    g["w_branch"] = jnp.stack([_matmul(s["outs"][j], dbds[j], ta=True, out_dtype=BF16, name=n(f"wg_branch{j}"))
                               for j in range(3)])
    douts = [_matmul(dbds[j], p["wb"][j], tb=True, name=n(f"dg_branch{j}")) for j in range(3)]
    dua, dva, g["sgu_ln_g"], g["sgu_ln_b"], dwc, dbs = _sgu_bwd(proj, p["lg"], p["lb"], p["wc"], p["bst"], douts[0],
                                                                n("sgu_bwd"))
    g["sgu_w"] = dwc.reshape(SGU_G, SGU_T, SGU_T)
    g["sgu_b"] = dbs.T
    dqr, dkr, dvr, dsink = _swa_bwd(s["qr"], s["kr"], s["vr"], p["sinks"], douts[1], n("swa_bwd"))
    g["attn_sinks"] = dsink[0, :SWA_H]
    dqb, dkb, dvb = _rope_bwd(dqr, dkr, dvr, cos, sin, n("rope_bwd"))
    do, dz, dng = _dn_post_bwd(s["o"], proj, p["ng"], douts[2], n("dn_post_bwd"))
    g["dn_norm"] = dng[0]
    dq, dk, dv, dbg = _dn_core_bwd(s["q"], s["k"], s["v"], s["bg"], s["dn"], do, n("dn_core_bwd"))
    dpre, dsm, g["dn_conv_w"], dal, ddb = _dn_pre_bwd1(proj, p["conv"], p["alog"], p["dtb"], dq, dk, dv, dbg,
                                                       n("dn_pre_bwd1"))
    g["dn_a_log"] = dal[0, DN_H:2 * DN_H]
    g["dn_dt_bias"] = ddb[0, DN_H:2 * DN_H]
    dqkv = _dn_pre_bwd2(dpre, p["conv"], n("dn_pre_bwd2"))
    dproj = jnp.concatenate([dgp, dqkv, dua, dva, dqb, dz, dkb, dvb, dsm], axis=1)
    g["w_in"] = _matmul(s["h"], dproj, ta=True, out_dtype=BF16, name=n("wg_in"))
    dh = _matmul(dproj, p["win"], tb=True, name=n("dg_in"))
    dx, g["attn_norm"] = _rms_bwd_add(s["x"], p["attn_norm"], dh, dx1, n("rms1_bwd"))
    g["attn_norm"], g["ffn_norm"] = g["attn_norm"][0], g["ffn_norm"][0]
    g["sgu_ln_g"], g["sgu_ln_b"] = g["sgu_ln_g"][0], g["sgu_ln_b"][0]
    return dx, g


def _local_step(x, positions, target, w):
    cos, sin = _rope_tables(positions)
    params = [_layer_params(w, l) for l in range(DEPTH)]
    saves, xs = [], x
    for l in range(DEPTH):
        xs, sv = _layer_fwd(xs, params[l], cos, sin, l)
        saves.append(sv)
    dx, loss_row, dgf = _final_loss(xs, w["final_norm"].reshape(1, -1), target)
    grads = [None] * DEPTH
    for l in reversed(range(DEPTH)):
        dx, grads[l] = _layer_bwd(dx, saves[l], params[l], cos, sin, l)
    stacked = {k: jnp.stack([grads[l][k] for l in range(DEPTH)]) for k in grads[0]}
    stacked["final_norm"] = dgf[0]
    return loss_row[0, 0], dx, stacked


MESH = pl.DeviceIdType.MESH
HBM_SPEC = pl.BlockSpec(memory_space=pltpu.HBM)
VMEM_SPEC = pl.BlockSpec(memory_space=pltpu.VMEM)
N_CHIPS = 4
FLIPS = tuple((fx, fy, fc) for fx in (0, 1) for fy in (0, 1) for fc in (0, 1))[1:]
BIG = ("w_in", "w_branch", "w_out", "w_gate_up", "w_down")
BIG_SPEC = {
    "w_in": dict(rows=1024, cols=1792, axis=1, keep=1730, down=8, up=4),
    "w_branch": dict(rows=1536, cols=256, axis=1, keep=256, down=2, up=1),
    "w_out": dict(rows=256, cols=1024, axis=0, keep=1024, down=1, up=1),
    "w_gate_up": dict(rows=1024, cols=1408, axis=1, keep=1408, down=8, up=4),
    "w_down": dict(rows=704, cols=1024, axis=0, keep=1024, down=4, up=2),
}
CONV_ROWS, CONV_COLS = DEPTH * DN_CONV, 3 * MIX // N_CHIPS


def _full_shape(k):
    sp = BIG_SPEC[k]
    return (sp["rows"], N_CHIPS * sp["cols"]) if sp["axis"] == 1 else (N_CHIPS * sp["rows"], sp["cols"])


def _chip_block(ref, k, s, layer=None):
    sp = BIG_SPEC[k]
    if sp["axis"] == 1:
        idx = (slice(None), pl.ds(pl.multiple_of(s * sp["cols"], 128), sp["cols"]))
    else:
        idx = (pl.ds(pl.multiple_of(s * sp["rows"], 16), sp["rows"]), slice(None))
    return ref.at[idx] if layer is None else ref.at[(layer,) + idx]


def _me():
    return lax.axis_index("x"), lax.axis_index("y"), lax.axis_index("c")


def _peer(x, y, c, flip):
    fx, fy, fc = flip
    return (1 - x if fx else x, 1 - y if fy else y, 1 - c if fc else c)


class _Copies:
    def __init__(self, send_sems, recv_sems):
        self.send_sems, self.recv_sems, self.k, self.sent, self.landing = send_sems, recv_sems, 0, [], []

    def _copy(self, k, src, dst, to):
        return pltpu.make_async_remote_copy(src_ref=src, dst_ref=dst, send_sem=self.send_sems.at[k],
                                            recv_sem=self.recv_sems.at[k], device_id=to, device_id_type=MESH)

    def send(self, src, dst, to, lands):
        k = self.k
        self.k += 1
        cp = self._copy(k, src, dst, to)
        cp.start()
        self.sent.append(cp)
        self.landing.append(self._copy(k, lands, lands, to))
        return k

    def wait_landed(self, k):
        self.landing[k].wait_recv()

    def finish(self, landed=()):
        for k, cp in enumerate(self.landing):
            if k not in landed:
                cp.wait_recv()
        for cp in self.sent:
            cp.wait_send()


def _place_shard(shard, k, chip, name):
    sp = BIG_SPEC[k]
    rows, cols, keep = sp["rows"], sp["cols"], sp["keep"]
    tr = _pick(rows, (256, 64))
    nb = rows // tr
    if sp["axis"] == 1:
        out_spec = pl.BlockSpec((1, tr, cols), lambda l, i, ch: (l, i, ch[0]))
    else:
        out_spec = pl.BlockSpec((1, tr, cols), lambda l, i, ch: (l, ch[0] * nb + i, 0))

    def kern(ch_ref, x_ref, o_ref):
        v = x_ref[0].astype(BF16)
        if keep == cols:
            o_ref[0] = v
        else:
            o_ref[0, :, :keep] = v
            o_ref[0, :, keep:] = jnp.zeros((tr, cols - keep), BF16)

    return pl.pallas_call(
        kern, name=name, out_shape=jax.ShapeDtypeStruct((DEPTH,) + _full_shape(k), BF16),
        grid_spec=pltpu.PrefetchScalarGridSpec(
            num_scalar_prefetch=1, grid=(DEPTH, nb),
            in_specs=[pl.BlockSpec((1, tr, keep), lambda l, i, ch: (l, i, 0))], out_specs=out_spec),
        compiler_params=_cparams(("parallel", "parallel")),
    )(chip, shard)


def _allgather_weights(placed, conv):
    n = len(BIG)
    n_sem = 6 * n + 3

    def body(*refs):
        conv_ref = refs[n]
        out = dict(zip(BIG, refs[n + 1:2 * n + 1]))
        conv_out, send_sems, recv_sems, local_sem = refs[2 * n + 1:]
        x, y, c = _me()
        me = 2 * x + y
        chips = [(1 - x, y), (x, 1 - y), (1 - x, 1 - y)]
        net = _Copies(send_sems, recv_sems)

        def conv_block(s):
            return conv_out.at[:, pl.ds(pl.multiple_of(s * CONV_COLS, 128), CONV_COLS)]

        local = pltpu.make_async_copy(conv_ref, conv_block(me), local_sem)
        local.start()
        first = {}
        for k in BIG:
            for j, (px, py) in enumerate(chips):
                first[k, j] = net.send(_chip_block(out[k], k, me, c), _chip_block(out[k], k, me, c), (px, py, c),
                                       _chip_block(out[k], k, 2 * px + py, c))
        for px, py in chips:
            net.send(conv_ref, conv_block(me), (px, py, c), conv_block(2 * px + py))
        for k in BIG:
            for j, (px, py) in enumerate(chips):
                net.wait_landed(first[k, j])
                net.send(_chip_block(out[k], k, 2 * px + py, c), _chip_block(out[k], k, 2 * px + py, c), (x, y, 1 - c),
                         _chip_block(out[k], k, 2 * px + py, 1 - c))
        net.finish(landed=set(first.values()))
        local.wait()

    out_shape = [jax.ShapeDtypeStruct((DEPTH,) + _full_shape(k), BF16) for k in BIG]
    out_shape.append(jax.ShapeDtypeStruct((CONV_ROWS, N_CHIPS * CONV_COLS), F32))
    outs = pl.pallas_call(
        body, name="allgather_weights", out_shape=out_shape, in_specs=[HBM_SPEC] * (n + 1), out_specs=[HBM_SPEC] * (n + 1),
        input_output_aliases={i: i for i in range(n)},
        scratch_shapes=[pltpu.SemaphoreType.DMA((n_sem,)), pltpu.SemaphoreType.DMA((n_sem,)), pltpu.SemaphoreType.DMA],
    )(*[placed[k] for k in BIG], conv)
    return dict(zip(BIG, outs[:n])), outs[n]


def _row_chunks(ref, rows, n, layer=None):
    step = rows // n
    sl = [pl.ds(i * step, step) for i in range(n)]
    return [ref.at[s, :] if layer is None else ref.at[layer, s, :] for s in sl]


def _grads_to_sibling(grads):
    n = len(BIG)
    n_sem = sum(BIG_SPEC[k]["down"] for k in BIG)

    def body(*refs):
        g = dict(zip(BIG, refs[:n]))
        out = dict(zip(BIG, refs[n:2 * n]))
        send_sems, recv_sems = refs[2 * n:]
        x, y, c = _me()
        net = _Copies(send_sems, recv_sems)
        for k in BIG:
            rows, nch = _full_shape(k)[0], BIG_SPEC[k]["down"]
            for s, d in zip(_row_chunks(g[k], rows, nch, 1 - c), _row_chunks(out[k], rows, nch)):
                net.send(s, d, (x, y, 1 - c), d)
        net.finish()

    outs = pl.pallas_call(
        body, name="grads_to_sibling", out_shape=[jax.ShapeDtypeStruct(_full_shape(k), BF16) for k in BIG],
        in_specs=[HBM_SPEC] * n, out_specs=[HBM_SPEC] * n,
        scratch_shapes=[pltpu.SemaphoreType.DMA((n_sem,)), pltpu.SemaphoreType.DMA((n_sem,))],
    )(*[grads[k] for k in BIG])
    return dict(zip(BIG, outs))


def _add_layer(g2, other, layer, name):
    _, rows, cols = g2.shape
    tr = _pick(rows, (256, 128))

    def kern(l_ref, a_ref, b_ref, o_ref):
        o_ref[...] = (a_ref[0].astype(F32) + b_ref[...].astype(F32)).astype(BF16)

    return pl.pallas_call(
        kern, name=name, out_shape=jax.ShapeDtypeStruct((rows, cols), BF16),
        grid_spec=pltpu.PrefetchScalarGridSpec(
            num_scalar_prefetch=1, grid=(rows // tr,),
            in_specs=[pl.BlockSpec((1, tr, cols), lambda i, l: (l[0], i, 0)), pl.BlockSpec((tr, cols), lambda i, l: (i, 0))],
            out_specs=pl.BlockSpec((tr, cols), lambda i, l: (i, 0))),
        compiler_params=_cparams(("parallel",)),
    )(layer, g2, other)


def _scatter_chip_sums(sums):
    n = len(BIG)

    def body(*refs):
        src = dict(zip(BIG, refs[:n]))
        out = dict(zip(BIG, refs[n:2 * n]))
        send_sems, recv_sems = refs[2 * n:]
        x, y, c = _me()
        net = _Copies(send_sems, recv_sems)
        for k in BIG:
            for j, (px, py) in enumerate([(1 - x, y), (x, 1 - y), (1 - x, 1 - y)]):
                net.send(_chip_block(src[k], k, 2 * px + py), out[k].at[j], (px, py, c), out[k].at[j])
        net.finish()

    outs = pl.pallas_call(
        body, name="scatter_chip_sums",
        out_shape=[jax.ShapeDtypeStruct((N_CHIPS - 1, BIG_SPEC[k]["rows"], BIG_SPEC[k]["cols"]), BF16) for k in BIG],
        in_specs=[HBM_SPEC] * n, out_specs=[HBM_SPEC] * n,
        scratch_shapes=[pltpu.SemaphoreType.DMA((3 * n,)), pltpu.SemaphoreType.DMA((3 * n,))],
    )(*[sums[k] for k in BIG])
    return dict(zip(BIG, outs))


def _sum_chips(parts, own, k, where, name):
    sp = BIG_SPEC[k]
    rows, cols, keep = sp["rows"], sp["cols"], sp["keep"]
    tr = _pick(rows, (256, 64))
    nb = rows // tr
    if sp["axis"] == 1:
        own_spec = pl.BlockSpec((tr, cols), lambda i, w: (i, w[0]))
    else:
        own_spec = pl.BlockSpec((tr, cols), lambda i, w: (w[0] * nb + i, 0))

    def kern(w_ref, p_ref, own_ref, o_ref):
        tot = own_ref[...].astype(F32)
        for j in range(N_CHIPS - 1):
            tot = tot + p_ref[j].astype(F32)
        o_ref[0] = tot[:, :keep]

    return pl.pallas_call(
        kern, name=name, out_shape=jax.ShapeDtypeStruct((DEPTH, rows, keep), F32),
        grid_spec=pltpu.PrefetchScalarGridSpec(
            num_scalar_prefetch=1, grid=(nb,),
            in_specs=[pl.BlockSpec((N_CHIPS - 1, tr, cols), lambda i, w: (0, i, 0)), own_spec],
            out_specs=pl.BlockSpec((1, tr, keep), lambda i, w: (w[1], i, 0))),
        compiler_params=_cparams(("parallel",)),
    )(where, parts, own)


def _exchange_layers(red):
    n = len(BIG)
    n_sem = sum(BIG_SPEC[k]["up"] for k in BIG)

    def body(*refs):
        out = dict(zip(BIG, refs[n:2 * n]))
        send_sems, recv_sems = refs[2 * n:]
        x, y, c = _me()
        net = _Copies(send_sems, recv_sems)
        for k in BIG:
            rows, nch = BIG_SPEC[k]["rows"], BIG_SPEC[k]["up"]
            for mine, theirs in zip(_row_chunks(out[k], rows, nch, c), _row_chunks(out[k], rows, nch, 1 - c)):
                net.send(mine, mine, (x, y, 1 - c), theirs)
        net.finish()

    outs = pl.pallas_call(
        body, name="exchange_layers",
        out_shape=[jax.ShapeDtypeStruct((DEPTH, BIG_SPEC[k]["rows"], BIG_SPEC[k]["keep"]), F32) for k in BIG],
        in_specs=[HBM_SPEC] * n, out_specs=[HBM_SPEC] * n, input_output_aliases={i: i for i in range(n)},
        scratch_shapes=[pltpu.SemaphoreType.DMA((n_sem,)), pltpu.SemaphoreType.DMA((n_sem,))],
    )(*[red[k] for k in BIG])
    return dict(zip(BIG, outs))


def _adam_vals(g, w, m, v):
    m2 = ADAM_B1 * m + (1.0 - ADAM_B1) * g
    v2 = ADAM_B2 * v + (1.0 - ADAM_B2) * (g * g)
    m_hat = m2 / (1.0 - ADAM_B1 ** ADAM_STEP)
    v_hat = v2 / (1.0 - ADAM_B2 ** ADAM_STEP)
    return -ADAM_LR * (m_hat / (jnp.sqrt(v_hat) + ADAM_EPS) + ADAM_WD * w), m2, v2


def _allreduce_small_adam(gp, wp, mp, vp):
    rows = gp.shape[0]

    def body(g_ref, w_ref, m_ref, v_ref, gs_ref, d_ref, nm_ref, nv_ref, buf, send_sems, recv_sems):
        x, y, c = _me()
        buf[4 * x + 2 * y + c] = g_ref[...]

        def copy(k, to, src):
            sx, sy, sc = src
            return pltpu.make_async_remote_copy(
                src_ref=g_ref, dst_ref=buf.at[4 * sx + 2 * sy + sc], send_sem=send_sems.at[k],
                recv_sem=recv_sems.at[k], device_id=to, device_id_type=MESH)

        sends = [copy(k, _peer(x, y, c, f), (x, y, c)) for k, f in enumerate(FLIPS)]
        for cp in sends:
            cp.start()
        for k, f in enumerate(FLIPS):
            copy(k, (x, y, c), _peer(x, y, c, f)).wait_recv()
        for cp in sends:
            cp.wait_send()
        tot = buf[0]
        for d in range(1, 8):
            tot = tot + buf[d]
        gs_ref[...] = tot
        d_ref[...], nm_ref[...], nv_ref[...] = _adam_vals(tot, w_ref[...], m_ref[...], v_ref[...])

    return pl.pallas_call(
        body, name="allreduce_small", out_shape=[jax.ShapeDtypeStruct((rows, 128), F32)] * 4,
        in_specs=[VMEM_SPEC] * 4, out_specs=[VMEM_SPEC] * 4,
        scratch_shapes=[pltpu.VMEM((8, rows, 128), F32), pltpu.SemaphoreType.DMA((7,)), pltpu.SemaphoreType.DMA((7,))],
        compiler_params=pltpu.CompilerParams(vmem_limit_bytes=VMEM_LIMIT),
    )(gp, wp, mp, vp)


def _adam(g, w, m, v, name):
    shape = w.shape
    cols = shape[-1]
    rows = w.size // cols
    tr = _pick(rows, (256, 8))
    spec = pl.BlockSpec((tr, cols), lambda i: (i, 0))

    def kern(g_ref, w_ref, m_ref, v_ref, d_ref, nm_ref, nv_ref):
        d_ref[...], nm_ref[...], nv_ref[...] = _adam_vals(g_ref[...], w_ref[...], m_ref[...], v_ref[...])

    outs = pl.pallas_call(
        kern, name=name, grid=(rows // tr,), in_specs=[spec] * 4, out_specs=[spec] * 3,
        out_shape=[jax.ShapeDtypeStruct((rows, cols), F32)] * 3, compiler_params=_cparams(("parallel",)),
    )(*[t.reshape(rows, cols) for t in (g, w, m, v)])
    return [o.reshape(shape) for o in outs]


SMALL = ("attn_norm", "sgu_ln_g", "sgu_ln_b", "sgu_w", "sgu_b", "attn_sinks", "dn_a_log", "dn_dt_bias", "dn_norm",
         "ffn_norm", "final_norm")
SMALL_ROWS = 1200


def _pack_small(vals, extra=()):
    flat = [vals[k].astype(F32).reshape(-1) for k in SMALL] + [e.astype(F32).reshape(-1) for e in extra]
    n = sum(f.shape[0] for f in flat)
    flat.append(jnp.zeros((SMALL_ROWS * 128 - n,), F32))
    return jnp.concatenate(flat).reshape(SMALL_ROWS, 128)


def _unpack_small(slab, shapes):
    flat = slab.reshape(-1)
    out, o = {}, 0
    for k in SMALL:
        n = math.prod(shapes[k])
        out[k] = flat[o:o + n].reshape(shapes[k])
        o += n
    return out, flat[o:]


def _permute_in_cols(w_in, dtype):
    parts = [w_in[..., a:a + n] for a, n in IN_PIECES]
    parts.append(jnp.zeros(w_in.shape[:-1] + (IN_PAD,), w_in.dtype))
    return jnp.concatenate(parts, axis=-1).astype(dtype)


def _unpermute_in_cols(g):
    offs, o = {}, 0
    for a, n in IN_PIECES:
        offs[a] = (o, n)
        o += n
    return jnp.concatenate([g[..., offs[a][0]:offs[a][0] + offs[a][1]] for a in sorted(offs)], axis=-1)


WEIGHTS = ("attn_norm", "w_in", "sgu_ln_g", "sgu_ln_b", "sgu_w", "sgu_b", "attn_sinks", "dn_conv_w", "dn_a_log",
           "dn_dt_bias", "dn_norm", "w_branch", "w_out", "ffn_norm", "w_gate_up", "w_down", "final_norm")
IN_SHARD = IN_COLS // N_CHIPS


def _drop_chip_pad(t):
    lead = t.shape[:-1]
    return t.reshape(lead + (N_CHIPS, BIG_SPEC["w_in"]["cols"]))[..., :IN_SHARD].reshape(lead + (IN_COLS,))


def _add_chip_pad(t):
    lead = t.shape[:-1]
    t = jnp.pad(t.reshape(lead + (N_CHIPS, IN_SHARD)), [(0, 0)] * len(lead) + [(0, 0), (0, BIG_SPEC["w_in"]["cols"] - IN_SHARD)])
    return t.reshape(lead + (N_CHIPS * BIG_SPEC["w_in"]["cols"],))


def kernel(x, positions, attn_norm, w_in, sgu_ln_g, sgu_ln_b, sgu_w, sgu_b, attn_sinks, dn_conv_w, dn_a_log, dn_dt_bias, dn_norm, w_branch, w_out, ffn_norm, w_gate_up, w_down, final_norm, loss_target, m_attn_norm, m_w_in, m_sgu_ln_g, m_sgu_ln_b, m_sgu_w, m_sgu_b, m_attn_sinks, m_dn_conv_w, m_dn_a_log, m_dn_dt_bias, m_dn_norm, m_w_branch, m_w_out, m_ffn_norm, m_w_gate_up, m_w_down, m_final_norm, v_attn_norm, v_w_in, v_sgu_ln_g, v_sgu_ln_b, v_sgu_w, v_sgu_b, v_attn_sinks, v_dn_conv_w, v_dn_a_log, v_dn_dt_bias, v_dn_norm, v_w_branch, v_w_out, v_ffn_norm, v_w_gate_up, v_w_down, v_final_norm):
    given = dict(locals())
    W = {k: given[k] for k in WEIGHTS}
    M = {k: given["m_" + k] for k in WEIGHTS}
    V = {k: given["v_" + k] for k in WEIGHTS}
    chip = 2 * lax.axis_index("x") + lax.axis_index("y")
    core = lax.axis_index("c")
    layer = core.astype(jnp.int32).reshape(1)
    chip1 = chip.astype(jnp.int32).reshape(1)
    where = jnp.stack([chip, core]).astype(jnp.int32)

    placed = {k: _place_shard(W[k].reshape(DEPTH, BIG_SPEC[k]["rows"], BIG_SPEC[k]["keep"]), k, chip1, "place_" + k)
              for k in BIG}
    full, conv_full = _allgather_weights(placed, dn_conv_w.reshape(CONV_ROWS, CONV_COLS))
    w = {k: W[k] for k in SMALL}
    w.update(full)
    w["w_in"] = _permute_in_cols(_drop_chip_pad(full["w_in"]), BF16)
    w["w_branch"] = full["w_branch"].reshape(DEPTH, 3, MIX, D_MODEL)
    w["dn_conv_w"] = conv_full.reshape(DEPTH, DN_CONV, 3 * MIX)

    loss, dx, g = _local_step(x[0], positions[0], loss_target[0], w)

    gb = {k: g[k] for k in BIG}
    gb["w_in"] = _add_chip_pad(_unpermute_in_cols(g["w_in"]))
    gb["w_branch"] = g["w_branch"].reshape(DEPTH, 3 * MIX, D_MODEL)
    sibling = _grads_to_sibling(gb)
    chip_sums = {k: _add_layer(gb[k], sibling[k], layer, "chip_sum_" + k) for k in BIG}
    parts = _scatter_chip_sums(chip_sums)
    reduced = _exchange_layers({k: _sum_chips(parts[k], chip_sums[k], k, where, "sum_" + k) for k in BIG})
    grads = {k: reduced[k].reshape(W[k].shape) for k in BIG}

    small_shapes = {k: W[k].shape for k in SMALL}
    gs, ds, nms, nvs = _allreduce_small_adam(_pack_small(g, (g["dn_conv_w"], loss.reshape(1))), _pack_small(W),
                                             _pack_small(M), _pack_small(V))
    gsm, rest = _unpack_small(gs, small_shapes)
    grads.update(gsm)
    n_conv = g["dn_conv_w"].size
    conv_full = rest[:n_conv].reshape(g["dn_conv_w"].shape)
    grads["dn_conv_w"] = lax.dynamic_slice_in_dim(conv_full, chip * dn_conv_w.shape[2], dn_conv_w.shape[2], axis=2)
    loss_total = rest[n_conv]
    delta, new_m, new_v = (_unpack_small(t, small_shapes)[0] for t in (ds, nms, nvs))
    for k in BIG + ("dn_conv_w",):
        delta[k], new_m[k], new_v[k] = _adam(grads[k], W[k], M[k], V[k], "adam_" + k)

    return (loss_total, dx[None], *[grads[k] for k in WEIGHTS], *[delta[k] for k in WEIGHTS],
            *[new_m[k] for k in WEIGHTS], *[new_v[k] for k in WEIGHTS])
```

```python
import functools
import math

import jax
import jax.numpy as jnp
from jax import lax
from jax.experimental import pallas as pl
from jax.experimental.pallas import tpu as pltpu

F32 = jnp.float32
BF16 = jnp.bfloat16
HI = lax.Precision.HIGHEST

D_MODEL = 1024
DEPTH = 2
MIX = 512
EPS = 1e-6
SGU_G, SGU_T = 4, 128
SWA_H, SWA_KV, SWA_HD, WINDOW = 8, 2, 64, 128
ROPE_THETA, ROPE_DIM = 500000.0, 16
DN_H, DN_HD, DN_CONV, DN_C = 4, 128, 4, 64
D_FF = 2816
IN_COLS = 6920
IN_PIECES = ((3848, 3072), (1792, 1536), (0, 512), (512, 512), (1024, 512), (3328, 512), (1536, 128), (1664, 128),
             (3840, 8))
IN_PAD = 120
IN_R = 7040
C_GATE, C_QKV, C_UA, C_VA, C_QB, C_ZC, C_KB, C_VB, C_SM = 0, 3072, 4608, 5120, 5632, 6144, 6656, 6784, 6912

ADAM_LR, ADAM_B1, ADAM_B2, ADAM_EPS, ADAM_WD, ADAM_STEP = 0.001, 0.9, 0.999, 1e-08, 0.01, 10
VMEM_LIMIT = 56 * 1024 * 1024


def _cparams(sem):
    return pltpu.CompilerParams(dimension_semantics=sem, vmem_limit_bytes=VMEM_LIMIT)


def _dg(a, b, ca, cb, prec=None):
    return lax.dot_general(a, b, (((ca,), (cb,)), ((), ())), precision=prec, preferred_element_type=F32)


def _split(x):
    hi = x.astype(BF16)
    return hi, (x - hi.astype(F32)).astype(BF16)


def _dg3_many(as_, bs, ca, cb):
    sa = [_split(a) for a in as_]
    sb = [_split(b) for b in bs]
    hh = [_dg(a[0], b[0], ca, cb) for a, b in zip(sa, sb)]
    hl = [_dg(a[0], b[1], ca, cb) for a, b in zip(sa, sb)]
    lh = [_dg(a[1], b[0], ca, cb) for a, b in zip(sa, sb)]
    return [x + (y + z) for x, y, z in zip(hh, hl, lh)]


def _dg_exact_lhs_many(a01, bs, ca, cb):
    a = a01.astype(BF16)
    b1 = [b.astype(BF16) for b in bs]
    r1 = [b - t.astype(F32) for b, t in zip(bs, b1)]
    b2 = [r.astype(BF16) for r in r1]
    b3 = [(r - t.astype(F32)).astype(BF16) for r, t in zip(r1, b2)]
    d1 = [_dg(a, t, ca, cb) for t in b1]
    d2 = [_dg(a, t, ca, cb) for t in b2]
    d3 = [_dg(a, t, ca, cb) for t in b3]
    return [x + (y + z) for x, y, z in zip(d1, d2, d3)]


def _mm(a, b):
    return _dg(a.astype(BF16), b.astype(BF16), 1, 0)


def _mm_nt(a, b):
    return _dg(a.astype(BF16), b.astype(BF16), 1, 1)


def _mm_tn(a, b):
    return _dg(a.astype(BF16), b.astype(BF16), 0, 0)


def _sigmoid(x):
    return 1.0 / (1.0 + jnp.exp(-x))


def _silu(x):
    return x * _sigmoid(x)


def _dsilu(x):
    s = _sigmoid(x)
    return s * (1.0 + x * (1.0 - s))


_GC = math.sqrt(2.0 / math.pi)


def _gelu(x):
    return 0.5 * x * (1.0 + jnp.tanh(_GC * (x + 0.044715 * x * x * x)))


def _dgelu(x):
    t = jnp.tanh(_GC * (x + 0.044715 * x * x * x))
    return 0.5 * (1.0 + t) + 0.5 * x * (1.0 - t * t) * _GC * (1.0 + 3.0 * 0.044715 * x * x)


def _softplus(x):
    return jnp.maximum(x, 0.0) + jnp.log(1.0 + jnp.exp(-jnp.abs(x)))


def _acc(ref, val, i):
    @pl.when(i == 0)
    def _():
        ref[...] = val

    @pl.when(i > 0)
    def _():
        ref[...] += val


def _halo_rows(dtype):
    return 8 * 4 // jnp.dtype(dtype).itemsize


def _tok_call(body, name, S, TB, tok_in, const_in=(), tok_out=(), acc_out=(), prev_in=(), next_in=(), smem_in=()):
    nb = S // TB
    in_specs, args = [], []
    for a, w, cb in tok_in:
        in_specs.append(pl.BlockSpec((TB, w), functools.partial(lambda i, cb: (i, cb), cb=cb)))
        args.append(a)
    for a, w, cb in prev_in:
        hr = _halo_rows(a.dtype)
        in_specs.append(pl.BlockSpec((hr, w), functools.partial(
            lambda i, cb, r: (jnp.maximum(i * r - 1, 0), cb), cb=cb, r=TB // hr)))
        args.append(a)
    for a, w, cb in next_in:
        hr = _halo_rows(a.dtype)
        in_specs.append(pl.BlockSpec((hr, w), functools.partial(
            lambda i, cb, r, last: (jnp.minimum((i + 1) * r, last), cb), cb=cb, r=TB // hr, last=S // hr - 1)))
        args.append(a)
    for a in const_in:
        in_specs.append(pl.BlockSpec(a.shape, lambda i: (0, 0)))
        args.append(a)
    for a in smem_in:
        in_specs.append(pl.BlockSpec(memory_space=pltpu.SMEM))
        args.append(a)
    out_specs, out_shape = [], []
    for w, dt in tok_out:
        out_specs.append(pl.BlockSpec((TB, w), lambda i: (i, 0)))
        out_shape.append(jax.ShapeDtypeStruct((S, w), dt))
    for shp, dt in acc_out:
        out_specs.append(pl.BlockSpec(shp, lambda i: (0, 0)))
        out_shape.append(jax.ShapeDtypeStruct(shp, dt))

    def kern(*refs):
        body(pl.program_id(0), *refs)

    return pl.pallas_call(
        kern, name=name, grid=(nb,), in_specs=in_specs, out_specs=out_specs, out_shape=out_shape,
        compiler_params=_cparams(("arbitrary",)),
    )(*args)


MM_BLOCKS = (1024, 1408, 640, 512, 256, 128)


def _pick(n, cands):
    for c in cands:
        if n % c == 0:
            return c
    return n


def _matmul(a, b, *, ta=False, tb=False, add=None, out_dtype=F32, name):
    M, K = (a.shape[1], a.shape[0]) if ta else a.shape
    N = b.shape[0] if tb else b.shape[1]
    bm, bn, bk = _pick(M, MM_BLOCKS), _pick(N, MM_BLOCKS), _pick(K, MM_BLOCKS)
    nk = K // bk
    a_spec = pl.BlockSpec((bk, bm), lambda i, j, k: (k, i)) if ta else pl.BlockSpec((bm, bk), lambda i, j, k: (i, k))
    b_spec = pl.BlockSpec((bn, bk), lambda i, j, k: (j, k)) if tb else pl.BlockSpec((bk, bn), lambda i, j, k: (k, j))
    o_spec = pl.BlockSpec((bm, bn), lambda i, j, k: (i, j))
    ca, cb = (0 if ta else 1), (1 if tb else 0)

    def kern(*refs):
        if add is None:
            a_ref, b_ref, o_ref, acc_ref = refs
        else:
            a_ref, b_ref, add_ref, o_ref, acc_ref = refs
        k = pl.program_id(2)
        p = _dg(a_ref[...].astype(BF16), b_ref[...].astype(BF16), ca, cb)

        @pl.when(k == 0)
        def _():
            acc_ref[...] = p

        @pl.when(k > 0)
        def _():
            acc_ref[...] += p

        @pl.when(k == nk - 1)
        def _():
            r = acc_ref[...]
            if add is not None:
                r = r + add_ref[...].astype(F32)
            o_ref[...] = r.astype(out_dtype)

    in_specs = [a_spec, b_spec] + ([o_spec] if add is not None else [])
    args = (a, b) + ((add,) if add is not None else ())
    return pl.pallas_call(
        kern, name=name, grid=(M // bm, N // bn, nk), in_specs=in_specs, out_specs=o_spec,
        out_shape=jax.ShapeDtypeStruct((M, N), out_dtype), scratch_shapes=[pltpu.VMEM((bm, bn), F32)],
        compiler_params=_cparams(("parallel", "parallel", "arbitrary")),
    )(*args)


def _rms_fwd(x, g, name):
    S = x.shape[0]

    def body(i, x_ref, g_ref, h_ref):
        xv = x_ref[...]
        r = lax.rsqrt(jnp.mean(xv * xv, axis=-1, keepdims=True) + EPS)
        h_ref[...] = (xv * r * g_ref[...]).astype(BF16)

    return _tok_call(body, name, S, min(S, 512), [(x, D_MODEL, 0)], [g], [(D_MODEL, BF16)])[0]


def _rms_bwd_vals(xv, g, dh):
    r = lax.rsqrt(jnp.mean(xv * xv, axis=-1, keepdims=True) + EPS)
    u = dh * g
    dx = r * u - xv * (r * r * r) * jnp.mean(u * xv, axis=-1, keepdims=True)
    dg = jnp.sum(dh * xv * r, axis=0, keepdims=True)
    return dx, dg


def _rms_bwd_add(x, g, dh, dres, name):
    S = x.shape[0]

    def body(i, x_ref, dh_ref, dr_ref, g_ref, dx_ref, dg_ref):
        dx, dg = _rms_bwd_vals(x_ref[...], g_ref[...], dh_ref[...].astype(F32))
        dx_ref[...] = dr_ref[...] + dx
        _acc(dg_ref, dg, i)

    return _tok_call(body, name, S, min(S, 512), [(x, D_MODEL, 0), (dh, D_MODEL, 0), (dres, D_MODEL, 0)], [g],
                     [(D_MODEL, F32)], [((1, D_MODEL), F32)])


def _final_loss(x, g, target):
    S = x.shape[0]

    def body(i, x_ref, t_ref, g_ref, dx_ref, loss_ref, dg_ref):
        xv, gv = x_ref[...], g_ref[...]
        r = lax.rsqrt(jnp.mean(xv * xv, axis=-1, keepdims=True) + EPS)
        e = xv * r * gv - t_ref[...]
        part = 0.5 * jnp.sum(jnp.mean(e * e, axis=-1, keepdims=True), axis=0, keepdims=True)
        dx, dg = _rms_bwd_vals(xv, gv, e * (1.0 / D_MODEL))
        dx_ref[...] = dx
        _acc(loss_ref, jnp.broadcast_to(part, (1, 128)), i)
        _acc(dg_ref, dg, i)

    return _tok_call(body, "final_loss", S, min(S, 512), [(x, D_MODEL, 0), (target, D_MODEL, 0)], [g],
                     [(D_MODEL, F32)], [((1, 128), F32), ((1, D_MODEL), F32)])


def _swiglu_fwd(gu, name):
    S = gu.shape[0]

    def body(i, gu_ref, a_ref):
        a_ref[...] = (_silu(gu_ref[:, :D_FF].astype(F32)) * gu_ref[:, D_FF:].astype(F32)).astype(BF16)

    return _tok_call(body, name, S, min(S, 256), [(gu, 2 * D_FF, 0)], [], [(D_FF, BF16)])[0]


def _swiglu_bwd(gu, dact, name):
    S = gu.shape[0]

    def body(i, gu_ref, da_ref, dgu_ref):
        gg, uu, da = gu_ref[:, :D_FF].astype(F32), gu_ref[:, D_FF:].astype(F32), da_ref[...].astype(F32)
        dgu_ref[:, :D_FF] = (da * uu * _dsilu(gg)).astype(BF16)
        dgu_ref[:, D_FF:] = (da * _silu(gg)).astype(BF16)

    return _tok_call(body, name, S, min(S, 256), [(gu, 2 * D_FF, 0), (dact, D_FF, 0)], [], [(2 * D_FF, BF16)])[0]


def _merge_fwd(proj, bds, name):
    S = proj.shape[0]

    def body(i, g0, g1, g2, b0, b1, b2, m_ref):
        m = jnp.zeros(m_ref.shape, F32)
        for gr, br in ((g0, b0), (g1, b1), (g2, b2)):
            m = m + _sigmoid(gr[...].astype(F32)) * br[...].astype(F32)
        m_ref[...] = m.astype(BF16)

    tok = [(proj, D_MODEL, n) for n in range(3)] + [(b, D_MODEL, 0) for b in bds]
    return _tok_call(body, name, S, min(S, 512), tok, [], [(D_MODEL, BF16)])[0]


def _merge_bwd(proj, bds, dm, name):
    S = proj.shape[0]

    def body(i, g0, g1, g2, b0, b1, b2, dm_ref, d0, d1, d2, dgp_ref):
        dmv = dm_ref[...]
        for n, (gr, br, dr) in enumerate(((g0, b0, d0), (g1, b1, d1), (g2, b2, d2))):
            s = _sigmoid(gr[...].astype(F32))
            dr[...] = (dmv * s).astype(BF16)
            dgp_ref[:, n * D_MODEL:(n + 1) * D_MODEL] = (dmv * br[...].astype(F32) * s * (1.0 - s)).astype(BF16)

    tok = [(proj, D_MODEL, n) for n in range(3)] + [(b, D_MODEL, 0) for b in bds] + [(dm, D_MODEL, 0)]
    return _tok_call(body, name, S, min(S, 512), tok, [],
                     [(D_MODEL, BF16)] * 3 + [(3 * D_MODEL, BF16)])


def _sgu_ln(v, lg, lb):
    mu = jnp.mean(v, axis=-1, keepdims=True)
    vc = v - mu
    rstd = lax.rsqrt(jnp.mean(vc * vc, axis=-1, keepdims=True) + EPS)
    vhat = vc * rstd
    return vhat, rstd, vhat * lg + lb


def _sgu_fwd(proj, lg, lb, wc, bst, name):
    S = proj.shape[0]

    def body(i, ua_ref, va_ref, lg_ref, lb_ref, wc_ref, bs_ref, o_ref):
        u = _gelu(ua_ref[...].astype(F32))
        _, _, vn = _sgu_ln(_gelu(va_ref[...].astype(F32)), lg_ref[...], lb_ref[...])
        for g in range(SGU_G):
            sl = slice(g * 128, (g + 1) * 128)
            mixed = _mm(wc_ref[sl, :], vn[:, sl]) + bs_ref[:, g:g + 1]
            o_ref[:, sl] = (u[:, sl] * mixed).astype(BF16)

    return _tok_call(body, name, S, SGU_T, [(proj, MIX, C_UA // MIX), (proj, MIX, C_VA // MIX)], [lg, lb, wc, bst],
                     [(MIX, BF16)])[0]


def _sgu_bwd(proj, lg, lb, wc, bst, dout, name):
    S = proj.shape[0]

    def body(i, ua_ref, va_ref, do_ref, lg_ref, lb_ref, wc_ref, bs_ref, dua_ref, dva_ref, dlg_ref, dlb_ref, dwc_ref,
             dbs_ref):
        ua, va, do = ua_ref[...].astype(F32), va_ref[...].astype(F32), do_ref[...].astype(F32)
        u = _gelu(ua)
        lgv = lg_ref[...]
        vhat, rstd, vn = _sgu_ln(_gelu(va), lgv, lb_ref[...])
        tril = lax.broadcasted_iota(jnp.int32, (128, 128), 0) >= lax.broadcasted_iota(jnp.int32, (128, 128), 1)
        lane4 = lax.broadcasted_iota(jnp.int32, (128, 4), 1)
        dvn_parts, dbs = [], jnp.zeros((128, 4), F32)
        for g in range(SGU_G):
            sl = slice(g * 128, (g + 1) * 128)
            wg = wc_ref[sl, :]
            mixed = _mm(wg, vn[:, sl]) + bs_ref[:, g:g + 1]
            dua_ref[:, sl] = (do[:, sl] * mixed * _dgelu(ua[:, sl])).astype(BF16)
            dmix = do[:, sl] * u[:, sl]
            dbs = dbs + jnp.where(lane4 == g, jnp.sum(dmix, axis=-1, keepdims=True), 0.0)
            dwg = jnp.where(tril, _mm_nt(dmix, vn[:, sl]), 0.0)
            _acc(dwc_ref.at[sl, :], dwg, i)
            dvn_parts.append(_mm_tn(wg, dmix))
        dvn = jnp.concatenate(dvn_parts, axis=1)
        _acc(dbs_ref, dbs, i)
        _acc(dlg_ref, jnp.sum(dvn * vhat, axis=0, keepdims=True), i)
        _acc(dlb_ref, jnp.sum(dvn, axis=0, keepdims=True), i)
        dvh = dvn * lgv
        dv = rstd * (dvh - jnp.mean(dvh, axis=-1, keepdims=True) - vhat * jnp.mean(dvh * vhat, axis=-1, keepdims=True))
        dva_ref[...] = (dv * _dgelu(va)).astype(BF16)

    return _tok_call(body, name, S, SGU_T, [(proj, MIX, C_UA // MIX), (proj, MIX, C_VA // MIX), (dout, MIX, 0)],
                     [lg, lb, wc, bst], [(MIX, BF16), (MIX, BF16)],
                     [((1, MIX), F32), ((1, MIX), F32), ((SGU_G * 128, 128), F32), ((128, 4), F32)])


def _rope_tables(positions):
    S = positions.shape[0]
    inv_freq = ROPE_THETA ** (-jnp.arange(0, ROPE_DIM, 2, dtype=F32) / ROPE_DIM)
    ang = positions.astype(F32)[:, None] * inv_freq
    c, s = jnp.cos(ang), jnp.sin(ang)
    c64 = jnp.concatenate([c, c, jnp.ones((S, SWA_HD - ROPE_DIM), F32)], axis=1)
    s64 = jnp.concatenate([-s, s, jnp.zeros((S, SWA_HD - ROPE_DIM), F32)], axis=1)
    return jnp.tile(c64, (1, 2)), jnp.tile(s64, (1, 2))


def _rope128(x, c, s):
    lane = lax.broadcasted_iota(jnp.int32, x.shape, 1) % SWA_HD
    swapped = jnp.where(lane < ROPE_DIM // 2, pltpu.roll(x, 128 - ROPE_DIM // 2, 1), pltpu.roll(x, ROPE_DIM // 2, 1))
    return x * c + swapped * s


def _rope_t128(y, c, s):
    ys = y * s
    lane = lax.broadcasted_iota(jnp.int32, y.shape, 1) % SWA_HD
    swapped = jnp.where(lane < ROPE_DIM // 2, pltpu.roll(ys, 128 - ROPE_DIM // 2, 1), pltpu.roll(ys, ROPE_DIM // 2, 1))
    return y * c + jnp.where(lane < ROPE_DIM, swapped, 0.0)


def _rope_fwd(proj, cos, sin, name):
    S = proj.shape[0]
    scale = SWA_HD ** -0.5

    def body(i, q_ref, k_ref, v_ref, c_ref, s_ref, qo_ref, ko_ref, vo_ref):
        c, s = c_ref[...], s_ref[...]
        for j in range(4):
            sl = slice(j * 128, (j + 1) * 128)
            qo_ref[:, sl] = (_rope128(q_ref[:, sl].astype(F32), c, s) * scale).astype(BF16)
        ko_ref[...] = _rope128(k_ref[...].astype(F32), c, s).astype(BF16)
        vo_ref[...] = v_ref[...].astype(BF16)

    return _tok_call(body, name, S, min(S, 512),
                     [(proj, MIX, C_QB // MIX), (proj, 128, C_KB // 128), (proj, 128, C_VB // 128), (cos, 128, 0),
                      (sin, 128, 0)], [], [(MIX, BF16), (128, BF16), (128, BF16)])


def _rope_bwd(dq, dk, dv, cos, sin, name):
    S = dq.shape[0]
    scale = SWA_HD ** -0.5

    def body(i, dq_ref, dk_ref, dv_ref, c_ref, s_ref, qo_ref, ko_ref, vo_ref):
        c, s = c_ref[...], s_ref[...]
        for j in range(4):
            sl = slice(j * 128, (j + 1) * 128)
            qo_ref[:, sl] = _rope_t128(dq_ref[:, sl] * scale, c, s).astype(BF16)
        ko_ref[...] = _rope_t128(dk_ref[...], c, s).astype(BF16)
        vo_ref[...] = dv_ref[...].astype(BF16)

    return _tok_call(body, name, S, min(S, 512),
                     [(dq, MIX, 0), (dk, 128, 0), (dv, 128, 0), (cos, 128, 0), (sin, 128, 0)], [],
                     [(MIX, BF16), (128, BF16), (128, BF16)])


def _swa_band(i, k_ref, v_ref):
    pstart = pl.multiple_of(jnp.maximum(i - 1, 0) * WINDOW, WINDOW)
    cstart = pl.multiple_of(i * WINDOW, WINDOW)
    kb = jnp.concatenate([k_ref[pl.ds(pstart, WINDOW), :], k_ref[pl.ds(cstart, WINDOW), :]], axis=0)
    vb = jnp.concatenate([v_ref[pl.ds(pstart, WINDOW), :], v_ref[pl.ds(cstart, WINDOW), :]], axis=0)
    qi = lax.broadcasted_iota(jnp.int32, (WINDOW, 2 * WINDOW), 0)
    sj = lax.broadcasted_iota(jnp.int32, (WINDOW, 2 * WINDOW), 1)
    mask = (sj > qi) & (sj <= qi + WINDOW) & ((i > 0) | (sj >= WINDOW))
    return kb, vb, mask, pstart, cstart


def _swa_probs(qs, kh, mask, sinks):
    logits = [jnp.where(mask, _dg(qh, kh, 1, 1), -1e30) for qh in qs]
    m = [jnp.maximum(jnp.max(l, axis=-1, keepdims=True), s) for l, s in zip(logits, sinks)]
    p = [jnp.exp(l - mm) for l, mm in zip(logits, m)]
    ps = [jnp.exp(s - mm) for s, mm in zip(sinks, m)]
    inv = [1.0 / (jnp.sum(pp, axis=-1, keepdims=True) + s) for pp, s in zip(p, ps)]
    return [pp * iv for pp, iv in zip(p, inv)], [s * iv for s, iv in zip(ps, inv)]


def _swa_fwd(q, k, v, sinks, name):
    S = q.shape[0]
    G = SWA_H // SWA_KV

    def body(i, q_ref, k_ref, v_ref, s_ref, o_ref):
        kb, vb, mask, _, _ = _swa_band(i, k_ref, v_ref)
        qv = q_ref[...]
        for kv in range(SWA_KV):
            ksl = slice(kv * SWA_HD, (kv + 1) * SWA_HD)
            heads = range(kv * G, (kv + 1) * G)
            pn, _ = _swa_probs([qv[:, h * SWA_HD:(h + 1) * SWA_HD] for h in heads], kb[:, ksl], mask,
                               [s_ref[0, h] for h in heads])
            outs = [_dg(p.astype(BF16), vb[:, ksl], 1, 0) for p in pn]
            for h, o in zip(heads, outs):
                o_ref[:, h * SWA_HD:(h + 1) * SWA_HD] = o.astype(BF16)

    return _tok_call(body, name, S, WINDOW, [(q, MIX, 0)], [k, v], [(MIX, BF16)], smem_in=[sinks])[0]


def _swa_bwd(q, k, v, sinks, dout, name):
    S = q.shape[0]

    def body(i, q_ref, do_ref, k_ref, v_ref, s_ref, dq_ref, dk_ref, dv_ref, ds_ref):
        kb, vb, mask, pstart, cstart = _swa_band(i, k_ref, v_ref)
        qv, dov = q_ref[...], do_ref[...]
        lane = lax.broadcasted_iota(jnp.int32, (1, 128), 1)
        dsink = jnp.zeros((1, 128), F32)
        dkb, dvb = [], []
        G = SWA_H // SWA_KV
        for kv in range(SWA_KV):
            ksl = slice(kv * SWA_HD, (kv + 1) * SWA_HD)
            heads = range(kv * G, (kv + 1) * G)
            qs = [qv[:, h * SWA_HD:(h + 1) * SWA_HD] for h in heads]
            dos = [dov[:, h * SWA_HD:(h + 1) * SWA_HD].astype(BF16) for h in heads]
            pn, psn = _swa_probs(qs, kb[:, ksl], mask, [s_ref[0, h] for h in heads])
            dp = [_dg(d, vb[:, ksl], 1, 1) for d in dos]
            delta = [jnp.sum(a * b, axis=-1, keepdims=True) for a, b in zip(dp, pn)]
            dsc = [(p * (a - d)).astype(BF16) for p, a, d in zip(pn, dp, delta)]
            dqs = [_dg(s, kb[:, ksl], 1, 0) for s in dsc]
            dks = [_dg(s, qh, 0, 0) for s, qh in zip(dsc, qs)]
            dvs = [_dg(p.astype(BF16), d, 0, 0) for p, d in zip(pn, dos)]
            for n_, h in enumerate(heads):
                dq_ref[:, h * SWA_HD:(h + 1) * SWA_HD] = dqs[n_]
                dsink = dsink + jnp.where(lane == h, -jnp.sum(psn[n_] * delta[n_], axis=0, keepdims=True), 0.0)
            dkb.append((dks[0] + dks[1]) + (dks[2] + dks[3]))
            dvb.append((dvs[0] + dvs[1]) + (dvs[2] + dvs[3]))
        dkb = jnp.concatenate(dkb, axis=1)
        dvb = jnp.concatenate(dvb, axis=1)

        @pl.when(i == 0)
        def _():
            dk_ref[...] = jnp.zeros_like(dk_ref)
            dv_ref[...] = jnp.zeros_like(dv_ref)

        dk_ref[pl.ds(pstart, WINDOW), :] += dkb[:WINDOW]
        dv_ref[pl.ds(pstart, WINDOW), :] += dvb[:WINDOW]
        dk_ref[pl.ds(cstart, WINDOW), :] += dkb[WINDOW:]
        dv_ref[pl.ds(cstart, WINDOW), :] += dvb[WINDOW:]
        _acc(ds_ref, dsink, i)

    return _tok_call(body, name, S, WINDOW, [(q, MIX, 0), (dout, MIX, 0)], [k, v], [(MIX, F32)],
                     [((S, 128), F32), ((S, 128), F32), ((1, 128), F32)], smem_in=[sinks])


def _shift_rows(xs, k):
    return xs if k == 0 else pltpu.roll(xs, k, 0)


def _dn_conv(x_ref, p_ref, w_ref, i):
    hr = p_ref.shape[0]
    halo = jnp.where(i > 0, p_ref[...].astype(F32), 0.0)
    xs = jnp.concatenate([halo, x_ref[...].astype(F32)], axis=0)
    sh = [_shift_rows(xs, DN_CONV - 1 - t)[hr:] for t in range(DN_CONV)]
    pre = sh[0] * w_ref[0:1, :]
    for t in range(1, DN_CONV):
        pre = pre + sh[t] * w_ref[t:t + 1, :]
    return pre, sh


def _dn_gates(sm, alog, dtb):
    lane = lax.broadcasted_iota(jnp.int32, sm.shape, 1)
    return jnp.where(lane < DN_H, _sigmoid(sm), -jnp.exp(alog) * _softplus(sm + dtb))


def _dn_pre_fwd(proj, conv_w, alog_l, dtb_l, name):
    S = proj.shape[0]
    scale = DN_HD ** -0.5

    def body(i, x_ref, sm_ref, p_ref, w_ref, al_ref, db_ref, q_ref, k_ref, v_ref, bg_ref):
        pre, _ = _dn_conv(x_ref, p_ref, w_ref, i)
        a = _silu(pre)
        for h in range(DN_H):
            sl = slice(h * DN_HD, (h + 1) * DN_HD)
            qh, kh = a[:, sl], a[:, MIX + h * DN_HD:MIX + (h + 1) * DN_HD]
            q_ref[:, sl] = qh * (lax.rsqrt(jnp.sum(qh * qh, axis=-1, keepdims=True) + EPS) * scale)
            k_ref[:, sl] = kh * lax.rsqrt(jnp.sum(kh * kh, axis=-1, keepdims=True) + EPS)
        v_ref[...] = a[:, 2 * MIX:]
        bg_ref[...] = _dn_gates(sm_ref[...].astype(F32), al_ref[...], db_ref[...])

    TB = min(S, 256)
    return _tok_call(body, name, S, TB, [(proj, 3 * MIX, C_QKV // (3 * MIX)), (proj, 128, C_SM // 128)],
                     [conv_w, alog_l, dtb_l], [(MIX, F32), (MIX, F32), (MIX, F32), (128, F32)],
                     prev_in=[(proj, 3 * MIX, C_QKV // (3 * MIX))])


def _dn_pre_bwd1(proj, conv_w, alog_l, dtb_l, dq, dk, dv, dbg, name):
    S = proj.shape[0]
    scale = DN_HD ** -0.5

    def body(i, x_ref, sm_ref, dq_ref, dk_ref, dv_ref, dbg_ref, p_ref, w_ref, al_ref, db_ref, dpre_ref, dsm_ref,
             dw_ref, dal_ref, ddb_ref):
        pre, sh = _dn_conv(x_ref, p_ref, w_ref, i)
        a = _silu(pre)
        da_parts = []
        for part, (g_ref, sc) in enumerate(((dq_ref, scale), (dk_ref, 1.0))):
            for h in range(DN_H):
                xh = a[:, part * MIX + h * DN_HD:part * MIX + (h + 1) * DN_HD]
                rs = lax.rsqrt(jnp.sum(xh * xh, axis=-1, keepdims=True) + EPS)
                y = xh * rs
                dy = g_ref[:, h * DN_HD:(h + 1) * DN_HD] * sc
                da_parts.append(rs * (dy - y * jnp.sum(dy * y, axis=-1, keepdims=True)))
        da_parts.append(dv_ref[...])
        dpre = jnp.concatenate(da_parts, axis=1) * _dsilu(pre)
        dpre_ref[...] = dpre
        dw = jnp.concatenate([jnp.sum(dpre * sh[t], axis=0, keepdims=True) for t in range(DN_CONV)], axis=0)
        _acc(dw_ref, dw, i)
        sm, al, db, dbg_v = sm_ref[...].astype(F32), al_ref[...], db_ref[...], dbg_ref[...]
        lane = lax.broadcasted_iota(jnp.int32, sm.shape, 1)
        sg = _sigmoid(sm)
        gneg = -jnp.exp(al)
        is_g = (lane >= DN_H) & (lane < 2 * DN_H)
        d_al = jnp.where(is_g, dbg_v * gneg * _sigmoid(sm + db), 0.0)
        dsm_ref[...] = jnp.where(lane < DN_H, dbg_v * sg * (1.0 - sg), d_al).astype(BF16)
        _acc(ddb_ref, jnp.sum(d_al, axis=0, keepdims=True), i)
        _acc(dal_ref, jnp.sum(jnp.where(is_g, dbg_v * gneg * _softplus(sm + db), 0.0), axis=0, keepdims=True), i)

    TB = min(S, 256)
    return _tok_call(body, name, S, TB,
                     [(proj, 3 * MIX, C_QKV // (3 * MIX)), (proj, 128, C_SM // 128), (dq, MIX, 0), (dk, MIX, 0),
                      (dv, MIX, 0), (dbg, 128, 0)], [conv_w, alog_l, dtb_l],
                     [(3 * MIX, F32), (128, BF16)], [((DN_CONV, 3 * MIX), F32), ((1, 128), F32), ((1, 128), F32)],
                     prev_in=[(proj, 3 * MIX, C_QKV // (3 * MIX))])


def _dn_pre_bwd2(dpre, conv_w, name):
    S = dpre.shape[0]
    TB = min(S, 256)
    nb = S // TB

    def body(i, d_ref, n_ref, w_ref, o_ref):
        halo = jnp.where(i < nb - 1, n_ref[...], 0.0)
        ds = jnp.concatenate([d_ref[...], halo], axis=0)
        out = ds[:TB] * w_ref[DN_CONV - 1:DN_CONV, :]
        for t in range(DN_CONV - 1):
            k = DN_CONV - 1 - t
            out = out + pltpu.roll(ds, TB + 8 - k, 0)[:TB] * w_ref[t:t + 1, :]
        o_ref[...] = out.astype(BF16)

    return _tok_call(body, name, S, TB, [(dpre, 3 * MIX, 0)], [conv_w], [(3 * MIX, BF16)],
                     next_in=[(dpre, 3 * MIX, 0)])[0]


def _dn_decay_terms(bgs, heads):
    C = DN_C
    ri = lax.broadcasted_iota(jnp.int32, (C, C), 0)
    ci = lax.broadcasted_iota(jnp.int32, (C, C), 1)
    tril, eye = ri >= ci, ri == ci
    beta = [b[:, h:h + 1] for b, h in zip(bgs, heads)]
    gcol = _dg_exact_lhs_many(tril, [jnp.broadcast_to(b[:, DN_H + h:DN_H + h + 1], (C, C))
                                     for b, h in zip(bgs, heads)], 1, 0)
    grow = [jnp.sum(jnp.where(eye, g, 0.0), axis=0, keepdims=True) for g in gcol]
    decay = [jnp.exp(jnp.where(tril, g - r, -1e30)) for g, r in zip(gcol, grow)]
    e_gc = [jnp.exp(g[:, 0:1]) for g in gcol]
    e_kd = [jnp.exp(g[C - 1:C, 0:1] - g[:, 0:1]) for g in gcol]
    cdec = [jnp.exp(g[C - 1:C, 0:1]) for g in gcol]
    return beta, decay, e_gc, e_kd, cdec


def _dn_nb(S):
    return 4 if S % (4 * DN_C) == 0 else 1


def _dn_prep_fwd(q, k, v, bg, name):
    S = q.shape[0]
    C, NB = DN_C, _dn_nb(S)
    TB = NB * C

    def kern(q_ref, k_ref, v_ref, bg_ref, t_ref, uw_ref, at_ref, qd_ref, kd_ref, dec_ref):
        lane = lax.broadcasted_iota(jnp.int32, (C, 128), 1)
        ri = lax.broadcasted_iota(jnp.int32, (C, C), 0)
        ci = lax.broadcasted_iota(jnp.int32, (C, C), 1)
        tril, eye = ri >= ci, ri == ci
        chains = [(cb, h) for cb in range(NB) for h in range(DN_H)]
        rows = lambda cb: slice(cb * C, (cb + 1) * C)
        head = lambda h: slice(h * DN_HD, (h + 1) * DN_HD)
        beta, decay, e_gc, e_kd, cdec = _dn_decay_terms([bg_ref[rows(cb), :] for cb, _ in chains],
                                                        [h for _, h in chains])
        qs = [q_ref[rows(cb), head(h)] for cb, h in chains]
        ks = [k_ref[rows(cb), head(h)] for cb, h in chains]
        kb = [kh * b for kh, b in zip(ks, beta)]
        x = [-jnp.where(ri > ci, _mm_nt(a, kh) * d, 0.0) for a, kh, d in zip(kb, ks, decay)]
        tm = [jnp.where(eye, 1.0, 0.0) + xi for xi in x]
        p = x
        for _ in range(5):
            p = _dg3_many(p, p, 1, 0)
            tm = [t + tp for t, tp in zip(tm, _dg3_many(tm, p, 1, 0))]
        rhs = [jnp.concatenate([v_ref[rows(cb), head(h)] * b, a * e], axis=1)
               for (cb, h), b, a, e in zip(chains, beta, kb, e_gc)]
        sol = _dg3_many(tm, rhs, 1, 0)
        attn = [_mm_nt(qh, kh) * d for qh, kh, d in zip(qs, ks, decay)]
        for n_, (cb, h) in enumerate(chains):
            rs, sl, hc = rows(cb), head(h), slice(h * C, (h + 1) * C)
            t_ref[rs, hc] = tm[n_]
            uw_ref[rs, sl] = sol[n_][:, :DN_HD]
            uw_ref[rs, MIX + h * DN_HD:MIX + (h + 1) * DN_HD] = sol[n_][:, DN_HD:]
            at_ref[rs, hc] = attn[n_]
            qd_ref[rs, sl] = (qs[n_] * e_gc[n_]).astype(BF16)
            kd_ref[rs, sl] = (ks[n_] * e_kd[n_]).astype(BF16)
        for cb in range(NB):
            dec = jnp.zeros((C, 128), F32)
            for h in range(DN_H):
                dec = dec + jnp.where(lane == h, cdec[cb * DN_H + h], 0.0)
            dec_ref[rows(cb), :] = dec

    tok = lambda w: pl.BlockSpec((TB, w), lambda i: (i, 0))
    return pl.pallas_call(
        kern, name=name, grid=(S // TB,), in_specs=[tok(MIX), tok(MIX), tok(MIX), tok(128)],
        out_specs=[tok(DN_H * C), tok(2 * MIX), tok(DN_H * C), tok(MIX), tok(MIX), tok(128)],
        out_shape=[jax.ShapeDtypeStruct((S, DN_H * C), F32), jax.ShapeDtypeStruct((S, 2 * MIX), F32),
                   jax.ShapeDtypeStruct((S, DN_H * C), F32), jax.ShapeDtypeStruct((S, MIX), BF16),
                   jax.ShapeDtypeStruct((S, MIX), BF16), jax.ShapeDtypeStruct((S, 128), F32)],
        compiler_params=_cparams(("parallel",)),
    )(q, k, v, bg)


def _dn_scan_fwd(uw, at, qd, kd, dec, name):
    S = uw.shape[0]
    C, NB = DN_C, _dn_nb(S)
    TB = NB * C
    SR = DN_H * DN_HD

    def kern(uw_ref, at_ref, qd_ref, kd_ref, dec_ref, o_ref, vn_ref, st_ref, state):
        @pl.when(pl.program_id(0) == 0)
        def _():
            state[...] = jnp.zeros_like(state)

        for cb in range(NB):
            rs = slice(cb * C, (cb + 1) * C)
            hs = range(DN_H)
            sls = [slice(h * DN_HD, (h + 1) * DN_HD) for h in hs]
            s_in = [state[sl, :] for sl in sls]
            ws = [_mm(uw_ref[rs, MIX + h * DN_HD:MIX + (h + 1) * DN_HD], s_in[h]) for h in hs]
            os_ = [_mm(qd_ref[rs, sls[h]], s_in[h]) for h in hs]
            vnew = [uw_ref[rs, sls[h]] - ws[h] for h in hs]
            oa = [_mm(at_ref[rs, h * C:(h + 1) * C], vnew[h]) for h in hs]
            kv = [_mm_tn(kd_ref[rs, sls[h]], vnew[h]) for h in hs]
            for h in hs:
                o_ref[rs, sls[h]] = os_[h] + oa[h]
                state[sls[h], :] = s_in[h] * dec_ref[cb * C:cb * C + 1, h:h + 1] + kv[h]
                st_ref[cb * SR + h * DN_HD:cb * SR + (h + 1) * DN_HD, :] = s_in[h]
                vn_ref[rs, sls[h]] = vnew[h]

    tok = lambda w: pl.BlockSpec((TB, w), lambda i: (i, 0))
    return pl.pallas_call(
        kern, name=name, grid=(S // TB,), in_specs=[tok(2 * MIX), tok(DN_H * C), tok(MIX), tok(MIX), tok(128)],
        out_specs=[tok(MIX), tok(MIX), pl.BlockSpec((NB * SR, DN_HD), lambda i: (i, 0))],
        out_shape=[jax.ShapeDtypeStruct((S, MIX), F32), jax.ShapeDtypeStruct((S, MIX), F32),
                   jax.ShapeDtypeStruct((S // C * SR, DN_HD), F32)],
        scratch_shapes=[pltpu.VMEM((SR, DN_HD), F32)],
        compiler_params=_cparams(("arbitrary",)),
    )(uw, at, qd, kd, dec)


def _dn_core_fwd(q, k, v, bg, name):
    tm, uw, at, qd, kd, dec = _dn_prep_fwd(q, k, v, bg, name + "_prep")
    o, vn, st = _dn_scan_fwd(uw, at, qd, kd, dec, name + "_scan")
    return o, dict(tm=tm, uw=uw, at=at, qd=qd, kd=kd, dec=dec, vn=vn, st=st)


def _dn_scan_bwd(sv, do, name):
    S = do.shape[0]
    C, NB = DN_C, _dn_nb(S)
    TB = NB * C
    SR = DN_H * DN_HD
    nb = S // TB

    def kern(do_ref, uw_ref, at_ref, qd_ref, kd_ref, dec_ref, vn_ref, st_ref, dvn_ref, dw_ref, dkd_ref, dc_ref, dstate):
        @pl.when(pl.program_id(0) == 0)
        def _():
            dstate[...] = jnp.zeros_like(dstate)

        lane = lax.broadcasted_iota(jnp.int32, (C, 128), 1)
        for cb in reversed(range(NB)):
            rs = slice(cb * C, (cb + 1) * C)
            dcrow = jnp.zeros((C, 128), F32)
            for h in range(DN_H):
                sl = slice(h * DN_HD, (h + 1) * DN_HD)
                doh, ds_o = do_ref[rs, sl], dstate[sl, :]
                s_in = st_ref[cb * SR + h * DN_HD:cb * SR + (h + 1) * DN_HD, :]
                d_vnew = _mm_tn(at_ref[rs, h * C:(h + 1) * C], doh) + _mm(kd_ref[rs, sl], ds_o)
                dvn_ref[rs, sl] = d_vnew
                dw_ref[rs, sl] = -_mm_nt(d_vnew, s_in)
                dkd_ref[rs, sl] = _mm_nt(vn_ref[rs, sl], ds_o)
                d_c = jnp.sum(jnp.sum(ds_o * s_in, axis=1, keepdims=True), axis=0, keepdims=True)
                dcrow = dcrow + jnp.where(lane == h, d_c, 0.0)
                dstate[sl, :] = (ds_o * dec_ref[cb * C:cb * C + 1, h:h + 1] + _mm_tn(qd_ref[rs, sl], doh)
                                 - _mm_tn(uw_ref[rs, MIX + h * DN_HD:MIX + (h + 1) * DN_HD], d_vnew))
            dc_ref[rs, :] = dcrow

    tok = lambda w: pl.BlockSpec((TB, w), lambda i: (nb - 1 - i, 0))
    return pl.pallas_call(
        kern, name=name, grid=(nb,),
        in_specs=[tok(MIX), tok(2 * MIX), tok(DN_H * C), tok(MIX), tok(MIX), tok(128), tok(MIX),
                  pl.BlockSpec((NB * SR, DN_HD), lambda i: (nb - 1 - i, 0))],
        out_specs=[tok(MIX), tok(MIX), tok(MIX), tok(128)],
        out_shape=[jax.ShapeDtypeStruct((S, MIX), F32)] * 3 + [jax.ShapeDtypeStruct((S, 128), F32)],
        scratch_shapes=[pltpu.VMEM((SR, DN_HD), F32)],
        compiler_params=_cparams(("arbitrary",)),
    )(do, sv["uw"], sv["at"], sv["qd"], sv["kd"], sv["dec"], sv["vn"], sv["st"])


def _dn_chunk_bwd(q, k, v, bg, sv, do, dvn, dw, dkd, dc, name):
    S = q.shape[0]
    C, NB = DN_C, _dn_nb(S)
    TB = NB * C
    SR = DN_H * DN_HD

    def kern(q_ref, k_ref, v_ref, bg_ref, t_ref, uw_ref, vn_ref, st_ref, do_ref, dvn_ref, dw_ref, dkd_ref, dc_ref,
             dq_ref, dk_ref, dv_ref, dbg_ref):
        lane = lax.broadcasted_iota(jnp.int32, (C, 128), 1)
        ri = lax.broadcasted_iota(jnp.int32, (C, C), 0)
        ci = lax.broadcasted_iota(jnp.int32, (C, C), 1)
        tril, eye, last = ri >= ci, ri == ci, ri[:, 0:1] == C - 1
        chains = [(cb, h) for cb in range(NB) for h in range(DN_H)]
        each = lambda f, *ls: [f(*a) for a in zip(*ls)]
        rsum = lambda t: jnp.sum(t, axis=-1, keepdims=True)
        rows = lambda cb: slice(cb * C, (cb + 1) * C)
        head = lambda h: slice(h * DN_HD, (h + 1) * DN_HD)
        tok = lambda ref: [ref[rows(cb), head(h)] for cb, h in chains]
        beta, decay, e_gc, e_kd, cdec = _dn_decay_terms([bg_ref[rows(cb), :] for cb, _ in chains],
                                                        [h for _, h in chains])
        qs, ks, vs, dos, vnew, d_kd = tok(q_ref), tok(k_ref), tok(v_ref), tok(do_ref), tok(vn_ref), tok(dkd_ref)
        s_in = [st_ref[cb * SR + h * DN_HD:cb * SR + (h + 1) * DN_HD, :] for cb, h in chains]
        d_c = [dc_ref[cb * C:cb * C + 1, h:h + 1] for cb, h in chains]
        kb = each(lambda a, b: a * b, ks, beta)
        kk = each(_mm_nt, kb, ks)
        attn = each(lambda a, b, d: _mm_nt(a, b) * d, qs, ks, decay)
        d_qd = each(_mm_nt, dos, s_in)
        d_attn = each(_mm_nt, dos, vnew)
        d_sol = [jnp.concatenate([dvn_ref[rows(cb), head(h)], dw_ref[rows(cb), head(h)]], axis=1) for cb, h in chains]
        sol = [jnp.concatenate([uw_ref[rows(cb), head(h)], uw_ref[rows(cb), MIX + h * DN_HD:MIX + (h + 1) * DN_HD]],
                               axis=1) for cb, h in chains]
        d_rhs = _dg3_many([t_ref[rows(cb), h * C:(h + 1) * C] for cb, h in chains], d_sol, 0, 0)
        d_a = _dg3_many(d_rhs, sol, 1, 1)
        d_kk = each(lambda a, d: jnp.where(ri > ci, -a, 0.0) * d, d_a, decay)
        d_qk = each(lambda a, d: a * d, d_attn, decay)
        dm = each(lambda a, b, c_, d: a * b + c_ * d, d_kk, kk, d_attn, attn)
        d_vb = [t[:, :DN_HD] for t in d_rhs]
        dz = [t[:, DN_HD:] for t in d_rhs]
        d_kb = each(lambda z, e, a, kh: z * e + _mm(a, kh), dz, e_gc, d_kk, ks)
        d_k = each(lambda a, b, c_, q: _mm_tn(a, b) + _mm_tn(c_, q), d_kk, kb, d_qk, qs)
        d_q = each(lambda a, kh, b, e: _mm(a, kh) + b * e, d_qk, ks, d_qd, e_gc)
        t_kd = each(lambda a, kh, e: rsum(a * kh * e), d_kd, ks, e_kd)
        d_gl = each(lambda t, c_, cd: jnp.sum(t, axis=0, keepdims=True) + c_ * cd, t_kd, d_c, cdec)
        d_gc = each(lambda z, a, e, m, b, q, t, gl:
                    rsum(z * a) * e + rsum(m) - rsum(jnp.where(eye, jnp.sum(m, axis=0, keepdims=True), 0.0))
                    + rsum(b * q) * e - t + jnp.where(last, gl, 0.0),
                    dz, kb, e_gc, dm, d_qd, qs, t_kd, d_gl)
        d_g = _dg_exact_lhs_many(ri <= ci, [jnp.broadcast_to(t, (C, 128)) for t in d_gc], 1, 0)
        d_beta = each(lambda a, v_, b, kh: rsum(a * v_) + rsum(b * kh), d_vb, vs, d_kb, ks)
        for n_, (cb, h) in enumerate(chains):
            dq_ref[rows(cb), head(h)] = d_q[n_]
            dk_ref[rows(cb), head(h)] = d_k[n_] + d_kd[n_] * e_kd[n_] + d_kb[n_] * beta[n_]
            dv_ref[rows(cb), head(h)] = d_vb[n_] * beta[n_]
        for cb in range(NB):
            dbg = jnp.zeros((C, 128), F32)
            for h in range(DN_H):
                n_ = cb * DN_H + h
                dbg = dbg + jnp.where(lane == h, d_beta[n_], 0.0) + jnp.where(lane == DN_H + h, d_g[n_], 0.0)
            dbg_ref[rows(cb), :] = dbg

    tok = lambda w: pl.BlockSpec((TB, w), lambda i: (i, 0))
    return pl.pallas_call(
        kern, name=name, grid=(S // TB,),
        in_specs=[tok(MIX), tok(MIX), tok(MIX), tok(128), tok(DN_H * C), tok(2 * MIX), tok(MIX),
                  pl.BlockSpec((NB * SR, DN_HD), lambda i: (i, 0)), tok(MIX), tok(MIX), tok(MIX), tok(MIX), tok(128)],
        out_specs=[tok(MIX), tok(MIX), tok(MIX), tok(128)],
        out_shape=[jax.ShapeDtypeStruct((S, MIX), F32)] * 3 + [jax.ShapeDtypeStruct((S, 128), F32)],
        compiler_params=_cparams(("parallel",)),
    )(q, k, v, bg, sv["tm"], sv["uw"], sv["vn"], sv["st"], do, dvn, dw, dkd, dc)


def _dn_core_bwd(q, k, v, bg, sv, do, name):
    dvn, dw, dkd, dc = _dn_scan_bwd(sv, do, name + "_scan")
    return _dn_chunk_bwd(q, k, v, bg, sv, do, dvn, dw, dkd, dc, name + "_chunk")


def _dn_post_fwd(o, proj, ng, name):
    S = o.shape[0]

    def body(i, o_ref, z_ref, g_ref, out_ref):
        gv = g_ref[...]
        for h in range(DN_H):
            sl = slice(h * DN_HD, (h + 1) * DN_HD)
            oh = o_ref[:, sl]
            r = lax.rsqrt(jnp.mean(oh * oh, axis=-1, keepdims=True) + EPS)
            out_ref[:, sl] = (oh * r * gv * _silu(z_ref[:, sl].astype(F32))).astype(BF16)

    return _tok_call(body, name, S, min(S, 512), [(o, MIX, 0), (proj, MIX, C_ZC // MIX)], [ng], [(MIX, BF16)])[0]


def _dn_post_bwd(o, proj, ng, dout, name):
    S = o.shape[0]

    def body(i, o_ref, z_ref, do_ref, g_ref, dov_ref, dz_ref, dg_ref):
        gv = g_ref[...]
        dg = jnp.zeros((1, DN_HD), F32)
        for h in range(DN_H):
            sl = slice(h * DN_HD, (h + 1) * DN_HD)
            oh, zh, dh = o_ref[:, sl], z_ref[:, sl].astype(F32), do_ref[:, sl].astype(F32)
            r = lax.rsqrt(jnp.mean(oh * oh, axis=-1, keepdims=True) + EPS)
            dz_ref[:, sl] = (dh * oh * r * gv * _dsilu(zh)).astype(BF16)
            dx, dgh = _rms_bwd_vals(oh, gv, dh * _silu(zh))
            dov_ref[:, sl] = dx
            dg = dg + dgh
        _acc(dg_ref, dg, i)

    return _tok_call(body, name, S, min(S, 512), [(o, MIX, 0), (proj, MIX, C_ZC // MIX), (dout, MIX, 0)], [ng],
                     [(MIX, F32), (MIX, BF16)], [((1, DN_HD), F32)])


def _layer_params(w, l):
    lane = jnp.arange(128)
    is_g = (lane >= DN_H) & (lane < 2 * DN_H)
    spread = lambda t: jnp.where(is_g, jnp.tile(t, 128 // DN_H), 0.0).reshape(1, 128)
    tril = jnp.tril(jnp.ones((SGU_T, SGU_T), bool))
    return dict(
        win=w["w_in"][l], wb=w["w_branch"][l], wout=w["w_out"][l], wgu=w["w_gate_up"][l], wdown=w["w_down"][l],
        conv=w["dn_conv_w"][l], attn_norm=w["attn_norm"][l].reshape(1, -1), ffn_norm=w["ffn_norm"][l].reshape(1, -1),
        lg=w["sgu_ln_g"][l].reshape(1, -1), lb=w["sgu_ln_b"][l].reshape(1, -1),
        wc=jnp.where(tril, w["sgu_w"][l], 0.0).reshape(SGU_G * SGU_T, SGU_T), bst=w["sgu_b"][l].T,
        sinks=w["attn_sinks"][l].reshape(1, -1), alog=spread(w["dn_a_log"][l]), dtb=spread(w["dn_dt_bias"][l]),
        ng=w["dn_norm"][l].reshape(1, -1))


def _layer_fwd(x, p, cos, sin, l):
    n = lambda s: f"l{l}_{s}"
    h = _rms_fwd(x, p["attn_norm"], n("rms1"))
    proj = _matmul(h, p["win"], out_dtype=BF16, name=n("mm_in"))
    out_a = _sgu_fwd(proj, p["lg"], p["lb"], p["wc"], p["bst"], n("sgu_fwd"))
    qr, kr, vr = _rope_fwd(proj, cos, sin, n("rope_fwd"))
    out_b = _swa_fwd(qr, kr, vr, p["sinks"], n("swa_fwd"))
    q, k, v, bg = _dn_pre_fwd(proj, p["conv"], p["alog"], p["dtb"], n("dn_pre_fwd"))
    o, dn = _dn_core_fwd(q, k, v, bg, n("dn_core_fwd"))
    out_c = _dn_post_fwd(o, proj, p["ng"], n("dn_post_fwd"))
    outs = (out_a, out_b, out_c)
    bds = [_matmul(outs[j], p["wb"][j], out_dtype=BF16, name=n(f"mm_branch{j}")) for j in range(3)]
    merged = _merge_fwd(proj, bds, n("merge_fwd"))
    x1 = _matmul(merged, p["wout"], add=x, name=n("mm_out"))
    h2 = _rms_fwd(x1, p["ffn_norm"], n("rms2"))
    gu = _matmul(h2, p["wgu"], out_dtype=BF16, name=n("mm_gu"))
    act = _swiglu_fwd(gu, n("swiglu_fwd"))
    x2 = _matmul(act, p["wdown"], add=x1, name=n("mm_down"))
    saved = dict(x=x, h=h, proj=proj, outs=outs, qr=qr, kr=kr, vr=vr, q=q, k=k, v=v, bg=bg, o=o, dn=dn, bds=bds,
                 merged=merged, x1=x1, h2=h2, gu=gu, act=act)
    return x2, saved


def _layer_bwd(dx2, s, p, cos, sin, l):
    n = lambda t: f"l{l}_{t}"
    proj = s["proj"]
    g = {}
    g["w_down"] = _matmul(s["act"], dx2, ta=True, out_dtype=BF16, name=n("wg_down"))
    dact = _matmul(dx2, p["wdown"], tb=True, out_dtype=BF16, name=n("dg_down"))
    dgu = _swiglu_bwd(s["gu"], dact, n("swiglu_bwd"))
    g["w_gate_up"] = _matmul(s["h2"], dgu, ta=True, out_dtype=BF16, name=n("wg_gu"))
    dh2 = _matmul(dgu, p["wgu"], tb=True, name=n("dg_gu"))
    dx1, g["ffn_norm"] = _rms_bwd_add(s["x1"], p["ffn_norm"], dh2, dx2, n("rms2_bwd"))
    g["w_out"] = _matmul(s["merged"], dx1, ta=True, out_dtype=BF16, name=n("wg_out"))
    dm = _matmul(dx1, p["wout"], tb=True, name=n("dg_out"))
    dbd0, dbd1, dbd2, dgp = _merge_bwd(proj, s["bds"], dm, n("merge_bwd"))
    dbds = (dbd0, dbd1, dbd2)
    g["w_branch"] = jnp.stack([_matmul(s["outs"][j], dbds[j], ta=True, out_dtype=BF16, name=n(f"wg_branch{j}"))
                               for j in range(3)])
    douts = [_matmul(dbds[j], p["wb"][j], tb=True, name=n(f"dg_branch{j}")) for j in range(3)]
    dua, dva, g["sgu_ln_g"], g["sgu_ln_b"], dwc, dbs = _sgu_bwd(proj, p["lg"], p["lb"], p["wc"], p["bst"], douts[0],
                                                                n("sgu_bwd"))
    g["sgu_w"] = dwc.reshape(SGU_G, SGU_T, SGU_T)
    g["sgu_b"] = dbs.T
    dqr, dkr, dvr, dsink = _swa_bwd(s["qr"], s["kr"], s["vr"], p["sinks"], douts[1], n("swa_bwd"))
    g["attn_sinks"] = dsink[0, :SWA_H]
    dqb, dkb, dvb = _rope_bwd(dqr, dkr, dvr, cos, sin, n("rope_bwd"))
    do, dz, dng = _dn_post_bwd(s["o"], proj, p["ng"], douts[2], n("dn_post_bwd"))
    g["dn_norm"] = dng[0]
    dq, dk, dv, dbg = _dn_core_bwd(s["q"], s["k"], s["v"], s["bg"], s["dn"], do, n("dn_core_bwd"))
    dpre, dsm, g["dn_conv_w"], dal, ddb = _dn_pre_bwd1(proj, p["conv"], p["alog"], p["dtb"], dq, dk, dv, dbg,
                                                       n("dn_pre_bwd1"))
    g["dn_a_log"] = dal[0, DN_H:2 * DN_H]
    g["dn_dt_bias"] = ddb[0, DN_H:2 * DN_H]
    dqkv = _dn_pre_bwd2(dpre, p["conv"], n("dn_pre_bwd2"))
    dproj = jnp.concatenate([dgp, dqkv, dua, dva, dqb, dz, dkb, dvb, dsm], axis=1)
    g["w_in"] = _matmul(s["h"], dproj, ta=True, out_dtype=BF16, name=n("wg_in"))
    dh = _matmul(dproj, p["win"], tb=True, name=n("dg_in"))
    dx, g["attn_norm"] = _rms_bwd_add(s["x"], p["attn_norm"], dh, dx1, n("rms1_bwd"))
    g["attn_norm"], g["ffn_norm"] = g["attn_norm"][0], g["ffn_norm"][0]
    g["sgu_ln_g"], g["sgu_ln_b"] = g["sgu_ln_g"][0], g["sgu_ln_b"][0]
    return dx, g


def _local_step(x, positions, target, w):
    cos, sin = _rope_tables(positions)
    params = [_layer_params(w, l) for l in range(DEPTH)]
    saves, xs = [], x
    for l in range(DEPTH):
        xs, sv = _layer_fwd(xs, params[l], cos, sin, l)
        saves.append(sv)
    dx, loss_row, dgf = _final_loss(xs, w["final_norm"].reshape(1, -1), target)
    grads = [None] * DEPTH
    for l in reversed(range(DEPTH)):
        dx, grads[l] = _layer_bwd(dx, saves[l], params[l], cos, sin, l)
    stacked = {k: jnp.stack([grads[l][k] for l in range(DEPTH)]) for k in grads[0]}
    stacked["final_norm"] = dgf[0]
    return loss_row[0, 0], dx, stacked


MESH = pl.DeviceIdType.MESH
HBM_SPEC = pl.BlockSpec(memory_space=pltpu.HBM)
VMEM_SPEC = pl.BlockSpec(memory_space=pltpu.VMEM)
N_CHIPS = 4
FLIPS = tuple((fx, fy, fc) for fx in (0, 1) for fy in (0, 1) for fc in (0, 1))[1:]
BIG = ("w_in", "w_branch", "w_out", "w_gate_up", "w_down")
BIG_SPEC = {
    "w_in": dict(rows=1024, cols=1792, axis=1, keep=1730, down=8, up=4),
    "w_branch": dict(rows=1536, cols=256, axis=1, keep=256, down=2, up=1),
    "w_out": dict(rows=256, cols=1024, axis=0, keep=1024, down=1, up=1),
    "w_gate_up": dict(rows=1024, cols=1408, axis=1, keep=1408, down=8, up=4),
    "w_down": dict(rows=704, cols=1024, axis=0, keep=1024, down=4, up=2),
}
CONV_ROWS, CONV_COLS = DEPTH * DN_CONV, 3 * MIX // N_CHIPS


def _full_shape(k):
    sp = BIG_SPEC[k]
    return (sp["rows"], N_CHIPS * sp["cols"]) if sp["axis"] == 1 else (N_CHIPS * sp["rows"], sp["cols"])


def _chip_block(ref, k, s, layer=None):
    sp = BIG_SPEC[k]
    if sp["axis"] == 1:
        idx = (slice(None), pl.ds(pl.multiple_of(s * sp["cols"], 128), sp["cols"]))
    else:
        idx = (pl.ds(pl.multiple_of(s * sp["rows"], 16), sp["rows"]), slice(None))
    return ref.at[idx] if layer is None else ref.at[(layer,) + idx]


def _me():
    return lax.axis_index("x"), lax.axis_index("y"), lax.axis_index("c")


def _peer(x, y, c, flip):
    fx, fy, fc = flip
    return (1 - x if fx else x, 1 - y if fy else y, 1 - c if fc else c)


class _Copies:
    def __init__(self, send_sems, recv_sems):
        self.send_sems, self.recv_sems, self.k, self.sent, self.landing = send_sems, recv_sems, 0, [], []

    def _copy(self, k, src, dst, to):
        return pltpu.make_async_remote_copy(src_ref=src, dst_ref=dst, send_sem=self.send_sems.at[k],
                                            recv_sem=self.recv_sems.at[k], device_id=to, device_id_type=MESH)

    def send(self, src, dst, to, lands):
        k = self.k
        self.k += 1
        cp = self._copy(k, src, dst, to)
        cp.start()
        self.sent.append(cp)
        self.landing.append(self._copy(k, lands, lands, to))
        return k

    def wait_landed(self, k):
        self.landing[k].wait_recv()

    def finish(self, landed=()):
        for k, cp in enumerate(self.landing):
            if k not in landed:
                cp.wait_recv()
        for cp in self.sent:
            cp.wait_send()


def _place_shard(shard, k, chip, name):
    sp = BIG_SPEC[k]
    rows, cols, keep = sp["rows"], sp["cols"], sp["keep"]
    tr = _pick(rows, (256, 64))
    nb = rows // tr
    if sp["axis"] == 1:
        out_spec = pl.BlockSpec((1, tr, cols), lambda l, i, ch: (l, i, ch[0]))
    else:
        out_spec = pl.BlockSpec((1, tr, cols), lambda l, i, ch: (l, ch[0] * nb + i, 0))

    def kern(ch_ref, x_ref, o_ref):
        v = x_ref[0].astype(BF16)
        if keep == cols:
            o_ref[0] = v
        else:
            o_ref[0, :, :keep] = v
            o_ref[0, :, keep:] = jnp.zeros((tr, cols - keep), BF16)

    return pl.pallas_call(
        kern, name=name, out_shape=jax.ShapeDtypeStruct((DEPTH,) + _full_shape(k), BF16),
        grid_spec=pltpu.PrefetchScalarGridSpec(
            num_scalar_prefetch=1, grid=(DEPTH, nb),
            in_specs=[pl.BlockSpec((1, tr, keep), lambda l, i, ch: (l, i, 0))], out_specs=out_spec),
        compiler_params=_cparams(("parallel", "parallel")),
    )(chip, shard)


def _allgather_weights(placed, conv):
    n = len(BIG)
    n_sem = 6 * n + 3

    def body(*refs):
        conv_ref = refs[n]
        out = dict(zip(BIG, refs[n + 1:2 * n + 1]))
        conv_out, send_sems, recv_sems, local_sem = refs[2 * n + 1:]
        x, y, c = _me()
        me = 2 * x + y
        chips = [(1 - x, y), (x, 1 - y), (1 - x, 1 - y)]
        net = _Copies(send_sems, recv_sems)

        def conv_block(s):
            return conv_out.at[:, pl.ds(pl.multiple_of(s * CONV_COLS, 128), CONV_COLS)]

        local = pltpu.make_async_copy(conv_ref, conv_block(me), local_sem)
        local.start()
        first = {}
        for k in BIG:
            for j, (px, py) in enumerate(chips):
                first[k, j] = net.send(_chip_block(out[k], k, me, c), _chip_block(out[k], k, me, c), (px, py, c),
                                       _chip_block(out[k], k, 2 * px + py, c))
        for px, py in chips:
            net.send(conv_ref, conv_block(me), (px, py, c), conv_block(2 * px + py))
        for k in BIG:
            for j, (px, py) in enumerate(chips):
                net.wait_landed(first[k, j])
                net.send(_chip_block(out[k], k, 2 * px + py, c), _chip_block(out[k], k, 2 * px + py, c), (x, y, 1 - c),
                         _chip_block(out[k], k, 2 * px + py, 1 - c))
        net.finish(landed=set(first.values()))
        local.wait()

    out_shape = [jax.ShapeDtypeStruct((DEPTH,) + _full_shape(k), BF16) for k in BIG]
    out_shape.append(jax.ShapeDtypeStruct((CONV_ROWS, N_CHIPS * CONV_COLS), F32))
    outs = pl.pallas_call(
        body, name="allgather_weights", out_shape=out_shape, in_specs=[HBM_SPEC] * (n + 1), out_specs=[HBM_SPEC] * (n + 1),
        input_output_aliases={i: i for i in range(n)},
        scratch_shapes=[pltpu.SemaphoreType.DMA((n_sem,)), pltpu.SemaphoreType.DMA((n_sem,)), pltpu.SemaphoreType.DMA],
    )(*[placed[k] for k in BIG], conv)
    return dict(zip(BIG, outs[:n])), outs[n]


def _row_chunks(ref, rows, n, layer=None):
    step = rows // n
    sl = [pl.ds(i * step, step) for i in range(n)]
    return [ref.at[s, :] if layer is None else ref.at[layer, s, :] for s in sl]


def _grads_to_sibling(grads):
    n = len(BIG)
    n_sem = sum(BIG_SPEC[k]["down"] for k in BIG)

    def body(*refs):
        g = dict(zip(BIG, refs[:n]))
        out = dict(zip(BIG, refs[n:2 * n]))
        send_sems, recv_sems = refs[2 * n:]
        x, y, c = _me()
        net = _Copies(send_sems, recv_sems)
        for k in BIG:
            rows, nch = _full_shape(k)[0], BIG_SPEC[k]["down"]
            for s, d in zip(_row_chunks(g[k], rows, nch, 1 - c), _row_chunks(out[k], rows, nch)):
                net.send(s, d, (x, y, 1 - c), d)
        net.finish()

    outs = pl.pallas_call(
        body, name="grads_to_sibling", out_shape=[jax.ShapeDtypeStruct(_full_shape(k), BF16) for k in BIG],
        in_specs=[HBM_SPEC] * n, out_specs=[HBM_SPEC] * n,
        scratch_shapes=[pltpu.SemaphoreType.DMA((n_sem,)), pltpu.SemaphoreType.DMA((n_sem,))],
    )(*[grads[k] for k in BIG])
    return dict(zip(BIG, outs))


def _add_layer(g2, other, layer, name):
    _, rows, cols = g2.shape
    tr = _pick(rows, (256, 128))

    def kern(l_ref, a_ref, b_ref, o_ref):
        o_ref[...] = (a_ref[0].astype(F32) + b_ref[...].astype(F32)).astype(BF16)

    return pl.pallas_call(
        kern, name=name, out_shape=jax.ShapeDtypeStruct((rows, cols), BF16),
        grid_spec=pltpu.PrefetchScalarGridSpec(
            num_scalar_prefetch=1, grid=(rows // tr,),
            in_specs=[pl.BlockSpec((1, tr, cols), lambda i, l: (l[0], i, 0)), pl.BlockSpec((tr, cols), lambda i, l: (i, 0))],
            out_specs=pl.BlockSpec((tr, cols), lambda i, l: (i, 0))),
        compiler_params=_cparams(("parallel",)),
    )(layer, g2, other)


def _scatter_chip_sums(sums):
    n = len(BIG)

    def body(*refs):
        src = dict(zip(BIG, refs[:n]))
        out = dict(zip(BIG, refs[n:2 * n]))
        send_sems, recv_sems = refs[2 * n:]
        x, y, c = _me()
        net = _Copies(send_sems, recv_sems)
        for k in BIG:
            for j, (px, py) in enumerate([(1 - x, y), (x, 1 - y), (1 - x, 1 - y)]):
                net.send(_chip_block(src[k], k, 2 * px + py), out[k].at[j], (px, py, c), out[k].at[j])
        net.finish()

    outs = pl.pallas_call(
        body, name="scatter_chip_sums",
        out_shape=[jax.ShapeDtypeStruct((N_CHIPS - 1, BIG_SPEC[k]["rows"], BIG_SPEC[k]["cols"]), BF16) for k in BIG],
        in_specs=[HBM_SPEC] * n, out_specs=[HBM_SPEC] * n,
        scratch_shapes=[pltpu.SemaphoreType.DMA((3 * n,)), pltpu.SemaphoreType.DMA((3 * n,))],
    )(*[sums[k] for k in BIG])
    return dict(zip(BIG, outs))


def _sum_chips(parts, own, k, where, name):
    sp = BIG_SPEC[k]
    rows, cols, keep = sp["rows"], sp["cols"], sp["keep"]
    tr = _pick(rows, (256, 64))
    nb = rows // tr
    if sp["axis"] == 1:
        own_spec = pl.BlockSpec((tr, cols), lambda i, w: (i, w[0]))
    else:
        own_spec = pl.BlockSpec((tr, cols), lambda i, w: (w[0] * nb + i, 0))

    def kern(w_ref, p_ref, own_ref, o_ref):
        tot = own_ref[...].astype(F32)
        for j in range(N_CHIPS - 1):
            tot = tot + p_ref[j].astype(F32)
        o_ref[0] = tot[:, :keep]

    return pl.pallas_call(
        kern, name=name, out_shape=jax.ShapeDtypeStruct((DEPTH, rows, keep), F32),
        grid_spec=pltpu.PrefetchScalarGridSpec(
            num_scalar_prefetch=1, grid=(nb,),
            in_specs=[pl.BlockSpec((N_CHIPS - 1, tr, cols), lambda i, w: (0, i, 0)), own_spec],
            out_specs=pl.BlockSpec((1, tr, keep), lambda i, w: (w[1], i, 0))),
        compiler_params=_cparams(("parallel",)),
    )(where, parts, own)


def _exchange_layers(red):
    n = len(BIG)
    n_sem = sum(BIG_SPEC[k]["up"] for k in BIG)

    def body(*refs):
        out = dict(zip(BIG, refs[n:2 * n]))
        send_sems, recv_sems = refs[2 * n:]
        x, y, c = _me()
        net = _Copies(send_sems, recv_sems)
        for k in BIG:
            rows, nch = BIG_SPEC[k]["rows"], BIG_SPEC[k]["up"]
            for mine, theirs in zip(_row_chunks(out[k], rows, nch, c), _row_chunks(out[k], rows, nch, 1 - c)):
                net.send(mine, mine, (x, y, 1 - c), theirs)
        net.finish()

    outs = pl.pallas_call(
        body, name="exchange_layers",
        out_shape=[jax.ShapeDtypeStruct((DEPTH, BIG_SPEC[k]["rows"], BIG_SPEC[k]["keep"]), F32) for k in BIG],
        in_specs=[HBM_SPEC] * n, out_specs=[HBM_SPEC] * n, input_output_aliases={i: i for i in range(n)},
        scratch_shapes=[pltpu.SemaphoreType.DMA((n_sem,)), pltpu.SemaphoreType.DMA((n_sem,))],
    )(*[red[k] for k in BIG])
    return dict(zip(BIG, outs))


def _adam_vals(g, w, m, v):
    m2 = ADAM_B1 * m + (1.0 - ADAM_B1) * g
    v2 = ADAM_B2 * v + (1.0 - ADAM_B2) * (g * g)
    m_hat = m2 / (1.0 - ADAM_B1 ** ADAM_STEP)
    v_hat = v2 / (1.0 - ADAM_B2 ** ADAM_STEP)
    return -ADAM_LR * (m_hat / (jnp.sqrt(v_hat) + ADAM_EPS) + ADAM_WD * w), m2, v2


def _allreduce_small_adam(gp, wp, mp, vp):
    rows = gp.shape[0]

    def body(g_ref, w_ref, m_ref, v_ref, gs_ref, d_ref, nm_ref, nv_ref, buf, send_sems, recv_sems):
        x, y, c = _me()
        buf[4 * x + 2 * y + c] = g_ref[...]

        def copy(k, to, src):
            sx, sy, sc = src
            return pltpu.make_async_remote_copy(
                src_ref=g_ref, dst_ref=buf.at[4 * sx + 2 * sy + sc], send_sem=send_sems.at[k],
                recv_sem=recv_sems.at[k], device_id=to, device_id_type=MESH)

        sends = [copy(k, _peer(x, y, c, f), (x, y, c)) for k, f in enumerate(FLIPS)]
        for cp in sends:
            cp.start()
        for k, f in enumerate(FLIPS):
            copy(k, (x, y, c), _peer(x, y, c, f)).wait_recv()
        for cp in sends:
            cp.wait_send()
        tot = buf[0]
        for d in range(1, 8):
            tot = tot + buf[d]
        gs_ref[...] = tot
        d_ref[...], nm_ref[...], nv_ref[...] = _adam_vals(tot, w_ref[...], m_ref[...], v_ref[...])

    return pl.pallas_call(
        body, name="allreduce_small", out_shape=[jax.ShapeDtypeStruct((rows, 128), F32)] * 4,
        in_specs=[VMEM_SPEC] * 4, out_specs=[VMEM_SPEC] * 4,
        scratch_shapes=[pltpu.VMEM((8, rows, 128), F32), pltpu.SemaphoreType.DMA((7,)), pltpu.SemaphoreType.DMA((7,))],
        compiler_params=pltpu.CompilerParams(vmem_limit_bytes=VMEM_LIMIT),
    )(gp, wp, mp, vp)


def _adam(g, w, m, v, name):
    shape = w.shape
    lead, rows, cols = math.prod(shape[:-2]), shape[-2], shape[-1]
    tr = _pick(rows, (256, 8, rows))
    spec = pl.BlockSpec((1, tr, cols), lambda l, i: (l, i, 0))

    def kern(g_ref, w_ref, m_ref, v_ref, d_ref, nm_ref, nv_ref):
        d_ref[...], nm_ref[...], nv_ref[...] = _adam_vals(g_ref[...], w_ref[...], m_ref[...], v_ref[...])

    outs = pl.pallas_call(
        kern, name=name, grid=(lead, rows // tr), in_specs=[spec] * 4, out_specs=[spec] * 3,
        out_shape=[jax.ShapeDtypeStruct((lead, rows, cols), F32)] * 3, compiler_params=_cparams(("parallel", "parallel")),
    )(*[t.reshape(lead, rows, cols) for t in (g, w, m, v)])
    return [o.reshape(shape) for o in outs]


SMALL = ("attn_norm", "sgu_ln_g", "sgu_ln_b", "sgu_w", "sgu_b", "attn_sinks", "dn_a_log", "dn_dt_bias", "dn_norm",
         "ffn_norm", "final_norm")


def _tile_rows(n):
    return -(-n // 1024) * 8


def _pack_small(vals, extra=()):
    tiles = []
    for t in [vals[k] for k in SMALL] + list(extra):
        flat = t.astype(F32).reshape(-1)
        rows = _tile_rows(flat.shape[0])
        tiles.append(jnp.pad(flat, (0, rows * 128 - flat.shape[0])).reshape(rows, 128))
    return jnp.concatenate(tiles, axis=0)


def _unpack_small(slab, shapes, extra_shapes=()):
    out, extras, o = {}, [], 0
    for k, shp in [(k, shapes[k]) for k in SMALL] + [(None, s) for s in extra_shapes]:
        n = math.prod(shp)
        rows = _tile_rows(n)
        t = slab[o:o + rows].reshape(-1)[:n].reshape(shp)
        o += rows
        if k is None:
            extras.append(t)
        else:
            out[k] = t
    return out, extras


def _in_col_segments():
    shard, padded = IN_COLS // N_CHIPS, BIG_SPEC["w_in"]["cols"]
    segs, mine = [], 0
    for a, n in IN_PIECES:
        o = a
        while o < a + n:
            end = min(a + n, (o // shard + 1) * shard)
            segs.append(((o // shard) * padded + o % shard, mine + o - a, end - o))
            o = end
        mine += n
    return segs


def _move_cols(x, segs, out_cols, name):
    layers, rows, cols = x.shape
    tr = _pick(rows, (256, rows))
    gaps, at = [], 0
    for d, w in sorted((d, w) for _, d, w in segs):
        if d > at:
            gaps.append((at, d - at))
        at = d + w
    if at < out_cols:
        gaps.append((at, out_cols - at))

    def kern(x_ref, o_ref):
        for s, d, w in segs:
            o_ref[0, :, d:d + w] = x_ref[0, :, s:s + w]
        for d, w in gaps:
            o_ref[0, :, d:d + w] = jnp.zeros((tr, w), x.dtype)

    return pl.pallas_call(
        kern, name=name, grid=(layers, rows // tr), in_specs=[pl.BlockSpec((1, tr, cols), lambda l, i: (l, i, 0))],
        out_specs=pl.BlockSpec((1, tr, out_cols), lambda l, i: (l, i, 0)),
        out_shape=jax.ShapeDtypeStruct((layers, rows, out_cols), x.dtype), compiler_params=_cparams(("parallel", "parallel")),
    )(x)


WEIGHTS = ("attn_norm", "w_in", "sgu_ln_g", "sgu_ln_b", "sgu_w", "sgu_b", "attn_sinks", "dn_conv_w", "dn_a_log",
           "dn_dt_bias", "dn_norm", "w_branch", "w_out", "ffn_norm", "w_gate_up", "w_down", "final_norm")


def kernel(x, positions, attn_norm, w_in, sgu_ln_g, sgu_ln_b, sgu_w, sgu_b, attn_sinks, dn_conv_w, dn_a_log, dn_dt_bias, dn_norm, w_branch, w_out, ffn_norm, w_gate_up, w_down, final_norm, loss_target, m_attn_norm, m_w_in, m_sgu_ln_g, m_sgu_ln_b, m_sgu_w, m_sgu_b, m_attn_sinks, m_dn_conv_w, m_dn_a_log, m_dn_dt_bias, m_dn_norm, m_w_branch, m_w_out, m_ffn_norm, m_w_gate_up, m_w_down, m_final_norm, v_attn_norm, v_w_in, v_sgu_ln_g, v_sgu_ln_b, v_sgu_w, v_sgu_b, v_attn_sinks, v_dn_conv_w, v_dn_a_log, v_dn_dt_bias, v_dn_norm, v_w_branch, v_w_out, v_ffn_norm, v_w_gate_up, v_w_down, v_final_norm):
    given = dict(locals())
    W = {k: given[k] for k in WEIGHTS}
    M = {k: given["m_" + k] for k in WEIGHTS}
    V = {k: given["v_" + k] for k in WEIGHTS}
    chip = 2 * lax.axis_index("x") + lax.axis_index("y")
    core = lax.axis_index("c")
    layer = core.astype(jnp.int32).reshape(1)
    chip1 = chip.astype(jnp.int32).reshape(1)
    where = jnp.stack([chip, core]).astype(jnp.int32)

    placed = {k: _place_shard(W[k].reshape(DEPTH, BIG_SPEC[k]["rows"], BIG_SPEC[k]["keep"]), k, chip1, "place_" + k)
              for k in BIG}
    full, conv_full = _allgather_weights(placed, dn_conv_w.reshape(CONV_ROWS, CONV_COLS))
    w = {k: W[k] for k in SMALL}
    w.update(full)
    segs = _in_col_segments()
    w["w_in"] = _move_cols(full["w_in"], segs, IN_R, "w_in_cols")
    w["w_branch"] = full["w_branch"].reshape(DEPTH, 3, MIX, D_MODEL)
    w["dn_conv_w"] = conv_full.reshape(DEPTH, DN_CONV, 3 * MIX)

    loss, dx, g = _local_step(x[0], positions[0], loss_target[0], w)

    gb = {k: g[k] for k in BIG}
    gb["w_in"] = _move_cols(g["w_in"], [(d, s, n) for s, d, n in segs], _full_shape("w_in")[1], "g_in_cols")
    gb["w_branch"] = g["w_branch"].reshape(DEPTH, 3 * MIX, D_MODEL)
    sibling = _grads_to_sibling(gb)
    chip_sums = {k: _add_layer(gb[k], sibling[k], layer, "chip_sum_" + k) for k in BIG}
    parts = _scatter_chip_sums(chip_sums)
    reduced = _exchange_layers({k: _sum_chips(parts[k], chip_sums[k], k, where, "sum_" + k) for k in BIG})
    grads = {k: reduced[k].reshape(W[k].shape) for k in BIG}

    small_shapes = {k: W[k].shape for k in SMALL}
    no_extra = (jnp.zeros(g["dn_conv_w"].shape, F32), jnp.zeros((1,), F32))
    gs, ds, nms, nvs = _allreduce_small_adam(_pack_small(g, (g["dn_conv_w"], loss.reshape(1))),
                                             _pack_small(W, no_extra), _pack_small(M, no_extra), _pack_small(V, no_extra))
    gsm, (conv_sum, loss_sum) = _unpack_small(gs, small_shapes, (g["dn_conv_w"].shape, (1,)))
    grads.update(gsm)
    grads["dn_conv_w"] = lax.dynamic_slice_in_dim(conv_sum, chip * dn_conv_w.shape[2], dn_conv_w.shape[2], axis=2)
    loss_total = loss_sum[0]
    delta, new_m, new_v = (_unpack_small(t, small_shapes)[0] for t in (ds, nms, nvs))
    for k in BIG + ("dn_conv_w",):
        delta[k], new_m[k], new_v[k] = _adam(grads[k], W[k], M[k], V[k], "adam_" + k)

    return (loss_total, dx[None], *[grads[k] for k in WEIGHTS], *[delta[k] for k in WEIGHTS],
            *[new_m[k] for k in WEIGHTS], *[new_v[k] for k in WEIGHTS])
```

```python
import functools
import math

import jax
import jax.numpy as jnp
from jax import lax
from jax.experimental import pallas as pl
from jax.experimental.pallas import tpu as pltpu

F32 = jnp.float32
BF16 = jnp.bfloat16
HI = lax.Precision.HIGHEST

D_MODEL = 1024
DEPTH = 2
MIX = 512
EPS = 1e-6
SGU_G, SGU_T = 4, 128
SWA_H, SWA_KV, SWA_HD, WINDOW = 8, 2, 64, 128
ROPE_THETA, ROPE_DIM = 500000.0, 16
DN_H, DN_HD, DN_CONV, DN_C = 4, 128, 4, 64
D_FF = 2816
IN_COLS = 6920
IN_PIECES = ((3848, 3072), (1792, 1536), (0, 512), (512, 512), (1024, 512), (3328, 512), (1536, 128), (1664, 128),
             (3840, 8))
IN_PAD = 120
IN_R = 7040
C_GATE, C_QKV, C_UA, C_VA, C_QB, C_ZC, C_KB, C_VB, C_SM = 0, 3072, 4608, 5120, 5632, 6144, 6656, 6784, 6912

ADAM_LR, ADAM_B1, ADAM_B2, ADAM_EPS, ADAM_WD, ADAM_STEP = 0.001, 0.9, 0.999, 1e-08, 0.01, 10
VMEM_LIMIT = 56 * 1024 * 1024


def _cparams(sem):
    return pltpu.CompilerParams(dimension_semantics=sem, vmem_limit_bytes=VMEM_LIMIT)


def _dg(a, b, ca, cb, prec=None):
    return lax.dot_general(a, b, (((ca,), (cb,)), ((), ())), precision=prec, preferred_element_type=F32)


def _split(x):
    hi = x.astype(BF16)
    return hi, (x - hi.astype(F32)).astype(BF16)


def _dg3_many(as_, bs, ca, cb):
    sa = [_split(a) for a in as_]
    sb = [_split(b) for b in bs]
    hh = [_dg(a[0], b[0], ca, cb) for a, b in zip(sa, sb)]
    hl = [_dg(a[0], b[1], ca, cb) for a, b in zip(sa, sb)]
    lh = [_dg(a[1], b[0], ca, cb) for a, b in zip(sa, sb)]
    return [x + (y + z) for x, y, z in zip(hh, hl, lh)]


def _dg_exact_lhs_many(a01, bs, ca, cb):
    a = a01.astype(BF16)
    b1 = [b.astype(BF16) for b in bs]
    r1 = [b - t.astype(F32) for b, t in zip(bs, b1)]
    b2 = [r.astype(BF16) for r in r1]
    b3 = [(r - t.astype(F32)).astype(BF16) for r, t in zip(r1, b2)]
    d1 = [_dg(a, t, ca, cb) for t in b1]
    d2 = [_dg(a, t, ca, cb) for t in b2]
    d3 = [_dg(a, t, ca, cb) for t in b3]
    return [x + (y + z) for x, y, z in zip(d1, d2, d3)]


def _mm(a, b):
    return _dg(a.astype(BF16), b.astype(BF16), 1, 0)


def _mm_nt(a, b):
    return _dg(a.astype(BF16), b.astype(BF16), 1, 1)


def _mm_tn(a, b):
    return _dg(a.astype(BF16), b.astype(BF16), 0, 0)


def _sigmoid(x):
    return 1.0 / (1.0 + jnp.exp(-x))


def _silu(x):
    return x * _sigmoid(x)


def _dsilu(x):
    s = _sigmoid(x)
    return s * (1.0 + x * (1.0 - s))


_GC = math.sqrt(2.0 / math.pi)


def _gelu(x):
    return 0.5 * x * (1.0 + jnp.tanh(_GC * (x + 0.044715 * x * x * x)))


def _dgelu(x):
    t = jnp.tanh(_GC * (x + 0.044715 * x * x * x))
    return 0.5 * (1.0 + t) + 0.5 * x * (1.0 - t * t) * _GC * (1.0 + 3.0 * 0.044715 * x * x)


def _softplus(x):
    return jnp.maximum(x, 0.0) + jnp.log(1.0 + jnp.exp(-jnp.abs(x)))


def _acc(ref, val, i):
    @pl.when(i == 0)
    def _():
        ref[...] = val

    @pl.when(i > 0)
    def _():
        ref[...] += val


def _halo_rows(dtype):
    return 8 * 4 // jnp.dtype(dtype).itemsize


def _tok_call(body, name, S, TB, tok_in, const_in=(), tok_out=(), acc_out=(), prev_in=(), next_in=(), smem_in=()):
    nb = S // TB
    in_specs, args = [], []
    for a, w, cb in tok_in:
        in_specs.append(pl.BlockSpec((TB, w), functools.partial(lambda i, cb: (i, cb), cb=cb)))
        args.append(a)
    for a, w, cb in prev_in:
        hr = _halo_rows(a.dtype)
        in_specs.append(pl.BlockSpec((hr, w), functools.partial(
            lambda i, cb, r: (jnp.maximum(i * r - 1, 0), cb), cb=cb, r=TB // hr)))
        args.append(a)
    for a, w, cb in next_in:
        hr = _halo_rows(a.dtype)
        in_specs.append(pl.BlockSpec((hr, w), functools.partial(
            lambda i, cb, r, last: (jnp.minimum((i + 1) * r, last), cb), cb=cb, r=TB // hr, last=S // hr - 1)))
        args.append(a)
    for a in const_in:
        in_specs.append(pl.BlockSpec(a.shape, lambda i: (0, 0)))
        args.append(a)
    for a in smem_in:
        in_specs.append(pl.BlockSpec(memory_space=pltpu.SMEM))
        args.append(a)
    out_specs, out_shape = [], []
    for w, dt in tok_out:
        out_specs.append(pl.BlockSpec((TB, w), lambda i: (i, 0)))
        out_shape.append(jax.ShapeDtypeStruct((S, w), dt))
    for shp, dt in acc_out:
        out_specs.append(pl.BlockSpec(shp, lambda i: (0, 0)))
        out_shape.append(jax.ShapeDtypeStruct(shp, dt))

    def kern(*refs):
        body(pl.program_id(0), *refs)

    return pl.pallas_call(
        kern, name=name, grid=(nb,), in_specs=in_specs, out_specs=out_specs, out_shape=out_shape,
        compiler_params=_cparams(("arbitrary",)),
    )(*args)


MM_BLOCKS = (1024, 1408, 640, 512, 256, 128)


def _pick(n, cands):
    for c in cands:
        if n % c == 0:
            return c
    return n


MM_VMEM_BUDGET = 44 * 1024 * 1024


def _mm_blocks(M, N, K, a_bytes, b_bytes, o_bytes, add_bytes):
    bn = _pick(N, MM_BLOCKS)
    fits = None
    for bk in [K] + [c for c in (2816, 2048) + MM_BLOCKS if c < K and K % c == 0]:
        for bm in [c for c in MM_BLOCKS if M % c == 0 and c >= min(M, 512)]:
            b_bufs = 1 if (bk == K and bn == N) else 2
            need = 2 * bm * bk * a_bytes + b_bufs * bk * bn * b_bytes + 2 * bm * bn * (o_bytes + add_bytes)
            need += bm * bn * 4 if bk < K else 0
            if need <= MM_VMEM_BUDGET:
                fits = fits or (bm, bn, bk)
                if (M // bm) * (N // bn) * (K // bk) >= 4:
                    return bm, bn, bk
    if fits is None:
        raise ValueError(f"no matmul blocks for {(M, N, K)}")
    return fits


def _matmul(a, b, *, ta=False, tb=False, add=None, out_dtype=F32, name):
    M, K = (a.shape[1], a.shape[0]) if ta else a.shape
    N = b.shape[0] if tb else b.shape[1]
    bm, bn, bk = _mm_blocks(M, N, K, a.dtype.itemsize, b.dtype.itemsize, jnp.dtype(out_dtype).itemsize,
                            0 if add is None else add.dtype.itemsize)
    nk = K // bk
    b_mode = dict(pipeline_mode=pl.Buffered(1)) if (bk == K and bn == N) else {}
    a_spec = pl.BlockSpec((bk, bm), lambda i, j, k: (k, i)) if ta else pl.BlockSpec((bm, bk), lambda i, j, k: (i, k))
    b_spec = (pl.BlockSpec((bn, bk), lambda i, j, k: (j, k), **b_mode) if tb
              else pl.BlockSpec((bk, bn), lambda i, j, k: (k, j), **b_mode))
    o_spec = pl.BlockSpec((bm, bn), lambda i, j, k: (i, j))
    ca, cb = (0 if ta else 1), (1 if tb else 0)

    def kern(*refs):
        a_ref, b_ref = refs[:2]
        add_ref = refs[2] if add is not None else None
        o_ref = refs[3] if add is not None else refs[2]
        p = _dg(a_ref[...].astype(BF16), b_ref[...].astype(BF16), ca, cb)

        def finish(r):
            if add is not None:
                r = r + add_ref[...].astype(F32)
            o_ref[...] = r.astype(out_dtype)

        if nk == 1:
            finish(p)
            return
        acc_ref = refs[-1]
        k = pl.program_id(2)

        @pl.when(k == 0)
        def _():
            acc_ref[...] = p

        @pl.when((k > 0) & (k < nk - 1))
        def _():
            acc_ref[...] += p

        @pl.when(k == nk - 1)
        def _():
            finish(acc_ref[...] + p)

    in_specs = [a_spec, b_spec] + ([o_spec] if add is not None else [])
    args = (a, b) + ((add,) if add is not None else ())
    return pl.pallas_call(
        kern, name=name, grid=(M // bm, N // bn, nk), in_specs=in_specs, out_specs=o_spec,
        out_shape=jax.ShapeDtypeStruct((M, N), out_dtype),
        scratch_shapes=[pltpu.VMEM((bm, bn), F32)] if nk > 1 else [],
        compiler_params=_cparams(("parallel", "parallel", "arbitrary")),
    )(*args)


def _rms_fwd(x, g, name):
    S = x.shape[0]

    def body(i, x_ref, g_ref, h_ref):
        xv = x_ref[...]
        r = lax.rsqrt(jnp.mean(xv * xv, axis=-1, keepdims=True) + EPS)
        h_ref[...] = (xv * r * g_ref[...]).astype(BF16)

    return _tok_call(body, name, S, min(S, 512), [(x, D_MODEL, 0)], [g], [(D_MODEL, BF16)])[0]


def _rms_bwd_vals(xv, g, dh):
    r = lax.rsqrt(jnp.mean(xv * xv, axis=-1, keepdims=True) + EPS)
    u = dh * g
    dx = r * u - xv * (r * r * r) * jnp.mean(u * xv, axis=-1, keepdims=True)
    dg = jnp.sum(dh * xv * r, axis=0, keepdims=True)
    return dx, dg


def _rms_bwd_add(x, g, dh, dres, name):
    S = x.shape[0]

    def body(i, x_ref, dh_ref, dr_ref, g_ref, dx_ref, dg_ref):
        dx, dg = _rms_bwd_vals(x_ref[...], g_ref[...], dh_ref[...].astype(F32))
        dx_ref[...] = dr_ref[...] + dx
        _acc(dg_ref, dg, i)

    return _tok_call(body, name, S, min(S, 512), [(x, D_MODEL, 0), (dh, D_MODEL, 0), (dres, D_MODEL, 0)], [g],
                     [(D_MODEL, F32)], [((1, D_MODEL), F32)])


def _final_loss(x, g, target):
    S = x.shape[0]

    def body(i, x_ref, t_ref, g_ref, dx_ref, loss_ref, dg_ref):
        xv, gv = x_ref[...], g_ref[...]
        r = lax.rsqrt(jnp.mean(xv * xv, axis=-1, keepdims=True) + EPS)
        e = xv * r * gv - t_ref[...]
        part = 0.5 * jnp.sum(jnp.mean(e * e, axis=-1, keepdims=True), axis=0, keepdims=True)
        dx, dg = _rms_bwd_vals(xv, gv, e * (1.0 / D_MODEL))
        dx_ref[...] = dx
        _acc(loss_ref, jnp.broadcast_to(part, (1, 128)), i)
        _acc(dg_ref, dg, i)

    return _tok_call(body, "final_loss", S, min(S, 512), [(x, D_MODEL, 0), (target, D_MODEL, 0)], [g],
                     [(D_MODEL, F32)], [((1, 128), F32), ((1, D_MODEL), F32)])


def _swiglu_fwd(gu, name):
    S = gu.shape[0]

    def body(i, gu_ref, a_ref):
        a_ref[...] = (_silu(gu_ref[:, :D_FF].astype(F32)) * gu_ref[:, D_FF:].astype(F32)).astype(BF16)

    return _tok_call(body, name, S, min(S, 256), [(gu, 2 * D_FF, 0)], [], [(D_FF, BF16)])[0]


def _swiglu_bwd(gu, dact, name):
    S = gu.shape[0]

    def body(i, gu_ref, da_ref, dgu_ref):
        gg, uu, da = gu_ref[:, :D_FF].astype(F32), gu_ref[:, D_FF:].astype(F32), da_ref[...].astype(F32)
        dgu_ref[:, :D_FF] = (da * uu * _dsilu(gg)).astype(BF16)
        dgu_ref[:, D_FF:] = (da * _silu(gg)).astype(BF16)

    return _tok_call(body, name, S, min(S, 256), [(gu, 2 * D_FF, 0), (dact, D_FF, 0)], [], [(2 * D_FF, BF16)])[0]


def _merge_fwd(proj, bds, name):
    S = proj.shape[0]

    def body(i, g0, g1, g2, b0, b1, b2, m_ref):
        m = jnp.zeros(m_ref.shape, F32)
        for gr, br in ((g0, b0), (g1, b1), (g2, b2)):
            m = m + _sigmoid(gr[...].astype(F32)) * br[...].astype(F32)
        m_ref[...] = m.astype(BF16)

    tok = [(proj, D_MODEL, n) for n in range(3)] + [(b, D_MODEL, 0) for b in bds]
    return _tok_call(body, name, S, min(S, 512), tok, [], [(D_MODEL, BF16)])[0]


def _merge_bwd(proj, bds, dm, name):
    S = proj.shape[0]

    def body(i, g0, g1, g2, b0, b1, b2, dm_ref, d0, d1, d2, dgp_ref):
        dmv = dm_ref[...]
        for n, (gr, br, dr) in enumerate(((g0, b0, d0), (g1, b1, d1), (g2, b2, d2))):
            s = _sigmoid(gr[...].astype(F32))
            dr[...] = (dmv * s).astype(BF16)
            dgp_ref[:, n * D_MODEL:(n + 1) * D_MODEL] = (dmv * br[...].astype(F32) * s * (1.0 - s)).astype(BF16)

    tok = [(proj, D_MODEL, n) for n in range(3)] + [(b, D_MODEL, 0) for b in bds] + [(dm, D_MODEL, 0)]
    return _tok_call(body, name, S, min(S, 512), tok, [],
                     [(D_MODEL, BF16)] * 3 + [(3 * D_MODEL, BF16)])


def _sgu_ln(v, lg, lb):
    mu = jnp.mean(v, axis=-1, keepdims=True)
    vc = v - mu
    rstd = lax.rsqrt(jnp.mean(vc * vc, axis=-1, keepdims=True) + EPS)
    vhat = vc * rstd
    return vhat, rstd, vhat * lg + lb


def _sgu_fwd(proj, lg, lb, wc, bst, name):
    S = proj.shape[0]

    def body(i, ua_ref, va_ref, lg_ref, lb_ref, wc_ref, bs_ref, o_ref):
        u = _gelu(ua_ref[...].astype(F32))
        _, _, vn = _sgu_ln(_gelu(va_ref[...].astype(F32)), lg_ref[...], lb_ref[...])
        for g in range(SGU_G):
            sl = slice(g * 128, (g + 1) * 128)
            mixed = _mm(wc_ref[sl, :], vn[:, sl]) + bs_ref[:, g:g + 1]
            o_ref[:, sl] = (u[:, sl] * mixed).astype(BF16)

    return _tok_call(body, name, S, SGU_T, [(proj, MIX, C_UA // MIX), (proj, MIX, C_VA // MIX)], [lg, lb, wc, bst],
                     [(MIX, BF16)])[0]


def _sgu_bwd(proj, lg, lb, wc, bst, dout, name):
    S = proj.shape[0]

    def body(i, ua_ref, va_ref, do_ref, lg_ref, lb_ref, wc_ref, bs_ref, dua_ref, dva_ref, dlg_ref, dlb_ref, dwc_ref,
             dbs_ref):
        ua, va, do = ua_ref[...].astype(F32), va_ref[...].astype(F32), do_ref[...].astype(F32)
        u = _gelu(ua)
        lgv = lg_ref[...]
        vhat, rstd, vn = _sgu_ln(_gelu(va), lgv, lb_ref[...])
        tril = lax.broadcasted_iota(jnp.int32, (128, 128), 0) >= lax.broadcasted_iota(jnp.int32, (128, 128), 1)
        lane4 = lax.broadcasted_iota(jnp.int32, (128, 4), 1)
        dvn_parts, dbs = [], jnp.zeros((128, 4), F32)
        for g in range(SGU_G):
            sl = slice(g * 128, (g + 1) * 128)
            wg = wc_ref[sl, :]
            mixed = _mm(wg, vn[:, sl]) + bs_ref[:, g:g + 1]
            dua_ref[:, sl] = (do[:, sl] * mixed * _dgelu(ua[:, sl])).astype(BF16)
            dmix = do[:, sl] * u[:, sl]
            dbs = dbs + jnp.where(lane4 == g, jnp.sum(dmix, axis=-1, keepdims=True), 0.0)
            dwg = jnp.where(tril, _mm_nt(dmix, vn[:, sl]), 0.0)
            _acc(dwc_ref.at[sl, :], dwg, i)
            dvn_parts.append(_mm_tn(wg, dmix))
        dvn = jnp.concatenate(dvn_parts, axis=1)
        _acc(dbs_ref, dbs, i)
        _acc(dlg_ref, jnp.sum(dvn * vhat, axis=0, keepdims=True), i)
        _acc(dlb_ref, jnp.sum(dvn, axis=0, keepdims=True), i)
        dvh = dvn * lgv
        dv = rstd * (dvh - jnp.mean(dvh, axis=-1, keepdims=True) - vhat * jnp.mean(dvh * vhat, axis=-1, keepdims=True))
        dva_ref[...] = (dv * _dgelu(va)).astype(BF16)

    return _tok_call(body, name, S, SGU_T, [(proj, MIX, C_UA // MIX), (proj, MIX, C_VA // MIX), (dout, MIX, 0)],
                     [lg, lb, wc, bst], [(MIX, BF16), (MIX, BF16)],
                     [((1, MIX), F32), ((1, MIX), F32), ((SGU_G * 128, 128), F32), ((128, 4), F32)])


def _rope_tables(positions):
    S = positions.shape[0]
    inv_freq = ROPE_THETA ** (-jnp.arange(0, ROPE_DIM, 2, dtype=F32) / ROPE_DIM)
    ang = positions.astype(F32)[:, None] * inv_freq
    c, s = jnp.cos(ang), jnp.sin(ang)
    c64 = jnp.concatenate([c, c, jnp.ones((S, SWA_HD - ROPE_DIM), F32)], axis=1)
    s64 = jnp.concatenate([-s, s, jnp.zeros((S, SWA_HD - ROPE_DIM), F32)], axis=1)
    return jnp.tile(c64, (1, 2)), jnp.tile(s64, (1, 2))


def _rope128(x, c, s):
    lane = lax.broadcasted_iota(jnp.int32, x.shape, 1) % SWA_HD
    swapped = jnp.where(lane < ROPE_DIM // 2, pltpu.roll(x, 128 - ROPE_DIM // 2, 1), pltpu.roll(x, ROPE_DIM // 2, 1))
    return x * c + swapped * s


def _rope_t128(y, c, s):
    ys = y * s
    lane = lax.broadcasted_iota(jnp.int32, y.shape, 1) % SWA_HD
    swapped = jnp.where(lane < ROPE_DIM // 2, pltpu.roll(ys, 128 - ROPE_DIM // 2, 1), pltpu.roll(ys, ROPE_DIM // 2, 1))
    return y * c + jnp.where(lane < ROPE_DIM, swapped, 0.0)


def _rope_fwd(proj, cos, sin, name):
    S = proj.shape[0]
    scale = SWA_HD ** -0.5

    def body(i, q_ref, k_ref, v_ref, c_ref, s_ref, qo_ref, ko_ref, vo_ref):
        c, s = c_ref[...], s_ref[...]
        for j in range(4):
            sl = slice(j * 128, (j + 1) * 128)
            qo_ref[:, sl] = (_rope128(q_ref[:, sl].astype(F32), c, s) * scale).astype(BF16)
        ko_ref[...] = _rope128(k_ref[...].astype(F32), c, s).astype(BF16)
        vo_ref[...] = v_ref[...].astype(BF16)

    return _tok_call(body, name, S, min(S, 512),
                     [(proj, MIX, C_QB // MIX), (proj, 128, C_KB // 128), (proj, 128, C_VB // 128), (cos, 128, 0),
                      (sin, 128, 0)], [], [(MIX, BF16), (128, BF16), (128, BF16)])


def _rope_bwd(dq, dk, dv, cos, sin, name):
    S = dq.shape[0]
    scale = SWA_HD ** -0.5

    def body(i, dq_ref, dk_ref, dv_ref, c_ref, s_ref, qo_ref, ko_ref, vo_ref):
        c, s = c_ref[...], s_ref[...]
        for j in range(4):
            sl = slice(j * 128, (j + 1) * 128)
            qo_ref[:, sl] = _rope_t128(dq_ref[:, sl] * scale, c, s).astype(BF16)
        ko_ref[...] = _rope_t128(dk_ref[...], c, s).astype(BF16)
        vo_ref[...] = dv_ref[...].astype(BF16)

    return _tok_call(body, name, S, min(S, 512),
                     [(dq, MIX, 0), (dk, 128, 0), (dv, 128, 0), (cos, 128, 0), (sin, 128, 0)], [],
                     [(MIX, BF16), (128, BF16), (128, BF16)])


def _swa_band(i, k_ref, v_ref):
    pstart = pl.multiple_of(jnp.maximum(i - 1, 0) * WINDOW, WINDOW)
    cstart = pl.multiple_of(i * WINDOW, WINDOW)
    kb = jnp.concatenate([k_ref[pl.ds(pstart, WINDOW), :], k_ref[pl.ds(cstart, WINDOW), :]], axis=0)
    vb = jnp.concatenate([v_ref[pl.ds(pstart, WINDOW), :], v_ref[pl.ds(cstart, WINDOW), :]], axis=0)
    qi = lax.broadcasted_iota(jnp.int32, (WINDOW, 2 * WINDOW), 0)
    sj = lax.broadcasted_iota(jnp.int32, (WINDOW, 2 * WINDOW), 1)
    mask = (sj > qi) & (sj <= qi + WINDOW) & ((i > 0) | (sj >= WINDOW))
    return kb, vb, mask, pstart, cstart


def _swa_probs(qs, kh, mask, sinks):
    logits = [jnp.where(mask, _dg(qh, kh, 1, 1), -1e30) for qh in qs]
    m = [jnp.maximum(jnp.max(l, axis=-1, keepdims=True), s) for l, s in zip(logits, sinks)]
    p = [jnp.exp(l - mm) for l, mm in zip(logits, m)]
    ps = [jnp.exp(s - mm) for s, mm in zip(sinks, m)]
    inv = [1.0 / (jnp.sum(pp, axis=-1, keepdims=True) + s) for pp, s in zip(p, ps)]
    return [pp * iv for pp, iv in zip(p, inv)], [s * iv for s, iv in zip(ps, inv)]


def _swa_fwd(q, k, v, sinks, name):
    S = q.shape[0]
    G = SWA_H // SWA_KV

    def body(i, q_ref, k_ref, v_ref, s_ref, o_ref):
        kb, vb, mask, _, _ = _swa_band(i, k_ref, v_ref)
        qv = q_ref[...]
        for kv in range(SWA_KV):
            ksl = slice(kv * SWA_HD, (kv + 1) * SWA_HD)
            heads = range(kv * G, (kv + 1) * G)
            pn, _ = _swa_probs([qv[:, h * SWA_HD:(h + 1) * SWA_HD] for h in heads], kb[:, ksl], mask,
                               [s_ref[0, h] for h in heads])
            outs = [_dg(p.astype(BF16), vb[:, ksl], 1, 0) for p in pn]
            for h, o in zip(heads, outs):
                o_ref[:, h * SWA_HD:(h + 1) * SWA_HD] = o.astype(BF16)

    return _tok_call(body, name, S, WINDOW, [(q, MIX, 0)], [k, v], [(MIX, BF16)], smem_in=[sinks])[0]


def _swa_bwd(q, k, v, sinks, dout, name):
    S = q.shape[0]

    def body(i, q_ref, do_ref, k_ref, v_ref, s_ref, dq_ref, dk_ref, dv_ref, ds_ref):
        kb, vb, mask, pstart, cstart = _swa_band(i, k_ref, v_ref)
        qv, dov = q_ref[...], do_ref[...]
        lane = lax.broadcasted_iota(jnp.int32, (1, 128), 1)
        dsink = jnp.zeros((1, 128), F32)
        dkb, dvb = [], []
        G = SWA_H // SWA_KV
        for kv in range(SWA_KV):
            ksl = slice(kv * SWA_HD, (kv + 1) * SWA_HD)
            heads = range(kv * G, (kv + 1) * G)
            qs = [qv[:, h * SWA_HD:(h + 1) * SWA_HD] for h in heads]
            dos = [dov[:, h * SWA_HD:(h + 1) * SWA_HD].astype(BF16) for h in heads]
            pn, psn = _swa_probs(qs, kb[:, ksl], mask, [s_ref[0, h] for h in heads])
            dp = [_dg(d, vb[:, ksl], 1, 1) for d in dos]
            delta = [jnp.sum(a * b, axis=-1, keepdims=True) for a, b in zip(dp, pn)]
            dsc = [(p * (a - d)).astype(BF16) for p, a, d in zip(pn, dp, delta)]
            dqs = [_dg(s, kb[:, ksl], 1, 0) for s in dsc]
            dks = [_dg(s, qh, 0, 0) for s, qh in zip(dsc, qs)]
            dvs = [_dg(p.astype(BF16), d, 0, 0) for p, d in zip(pn, dos)]
            for n_, h in enumerate(heads):
                dq_ref[:, h * SWA_HD:(h + 1) * SWA_HD] = dqs[n_]
                dsink = dsink + jnp.where(lane == h, -jnp.sum(psn[n_] * delta[n_], axis=0, keepdims=True), 0.0)
            dkb.append((dks[0] + dks[1]) + (dks[2] + dks[3]))
            dvb.append((dvs[0] + dvs[1]) + (dvs[2] + dvs[3]))
        dkb = jnp.concatenate(dkb, axis=1)
        dvb = jnp.concatenate(dvb, axis=1)

        @pl.when(i == 0)
        def _():
            dk_ref[...] = jnp.zeros_like(dk_ref)
            dv_ref[...] = jnp.zeros_like(dv_ref)

        dk_ref[pl.ds(pstart, WINDOW), :] += dkb[:WINDOW]
        dv_ref[pl.ds(pstart, WINDOW), :] += dvb[:WINDOW]
        dk_ref[pl.ds(cstart, WINDOW), :] += dkb[WINDOW:]
        dv_ref[pl.ds(cstart, WINDOW), :] += dvb[WINDOW:]
        _acc(ds_ref, dsink, i)

    return _tok_call(body, name, S, WINDOW, [(q, MIX, 0), (dout, MIX, 0)], [k, v], [(MIX, F32)],
                     [((S, 128), F32), ((S, 128), F32), ((1, 128), F32)], smem_in=[sinks])


def _shift_rows(xs, k):
    return xs if k == 0 else pltpu.roll(xs, k, 0)


def _dn_conv(x_ref, p_ref, w_ref, i):
    hr = p_ref.shape[0]
    halo = jnp.where(i > 0, p_ref[...].astype(F32), 0.0)
    xs = jnp.concatenate([halo, x_ref[...].astype(F32)], axis=0)
    sh = [_shift_rows(xs, DN_CONV - 1 - t)[hr:] for t in range(DN_CONV)]
    pre = sh[0] * w_ref[0:1, :]
    for t in range(1, DN_CONV):
        pre = pre + sh[t] * w_ref[t:t + 1, :]
    return pre, sh


def _dn_gates(sm, alog, dtb):
    lane = lax.broadcasted_iota(jnp.int32, sm.shape, 1)
    return jnp.where(lane < DN_H, _sigmoid(sm), -jnp.exp(alog) * _softplus(sm + dtb))


def _dn_pre_fwd(proj, conv_w, alog_l, dtb_l, name):
    S = proj.shape[0]
    scale = DN_HD ** -0.5

    def body(i, x_ref, sm_ref, p_ref, w_ref, al_ref, db_ref, q_ref, k_ref, v_ref, bg_ref):
        pre, _ = _dn_conv(x_ref, p_ref, w_ref, i)
        a = _silu(pre)
        for h in range(DN_H):
            sl = slice(h * DN_HD, (h + 1) * DN_HD)
            qh, kh = a[:, sl], a[:, MIX + h * DN_HD:MIX + (h + 1) * DN_HD]
            q_ref[:, sl] = qh * (lax.rsqrt(jnp.sum(qh * qh, axis=-1, keepdims=True) + EPS) * scale)
            k_ref[:, sl] = kh * lax.rsqrt(jnp.sum(kh * kh, axis=-1, keepdims=True) + EPS)
        v_ref[...] = a[:, 2 * MIX:]
        bg_ref[...] = _dn_gates(sm_ref[...].astype(F32), al_ref[...], db_ref[...])

    TB = min(S, 256)
    return _tok_call(body, name, S, TB, [(proj, 3 * MIX, C_QKV // (3 * MIX)), (proj, 128, C_SM // 128)],
                     [conv_w, alog_l, dtb_l], [(MIX, F32), (MIX, F32), (MIX, F32), (128, F32)],
                     prev_in=[(proj, 3 * MIX, C_QKV // (3 * MIX))])


def _dn_pre_bwd1(proj, conv_w, alog_l, dtb_l, dq, dk, dv, dbg, name):
    S = proj.shape[0]
    scale = DN_HD ** -0.5

    def body(i, x_ref, sm_ref, dq_ref, dk_ref, dv_ref, dbg_ref, p_ref, w_ref, al_ref, db_ref, dpre_ref, dsm_ref,
             dw_ref, dal_ref, ddb_ref):
        pre, sh = _dn_conv(x_ref, p_ref, w_ref, i)
        a = _silu(pre)
        da_parts = []
        for part, (g_ref, sc) in enumerate(((dq_ref, scale), (dk_ref, 1.0))):
            for h in range(DN_H):
                xh = a[:, part * MIX + h * DN_HD:part * MIX + (h + 1) * DN_HD]
                rs = lax.rsqrt(jnp.sum(xh * xh, axis=-1, keepdims=True) + EPS)
                y = xh * rs
                dy = g_ref[:, h * DN_HD:(h + 1) * DN_HD] * sc
                da_parts.append(rs * (dy - y * jnp.sum(dy * y, axis=-1, keepdims=True)))
        da_parts.append(dv_ref[...])
        dpre = jnp.concatenate(da_parts, axis=1) * _dsilu(pre)
        dpre_ref[...] = dpre
        dw = jnp.concatenate([jnp.sum(dpre * sh[t], axis=0, keepdims=True) for t in range(DN_CONV)], axis=0)
        _acc(dw_ref, dw, i)
        sm, al, db, dbg_v = sm_ref[...].astype(F32), al_ref[...], db_ref[...], dbg_ref[...]
        lane = lax.broadcasted_iota(jnp.int32, sm.shape, 1)
        sg = _sigmoid(sm)
        gneg = -jnp.exp(al)
        is_g = (lane >= DN_H) & (lane < 2 * DN_H)
        d_al = jnp.where(is_g, dbg_v * gneg * _sigmoid(sm + db), 0.0)
        dsm_ref[...] = jnp.where(lane < DN_H, dbg_v * sg * (1.0 - sg), d_al).astype(BF16)
        _acc(ddb_ref, jnp.sum(d_al, axis=0, keepdims=True), i)
        _acc(dal_ref, jnp.sum(jnp.where(is_g, dbg_v * gneg * _softplus(sm + db), 0.0), axis=0, keepdims=True), i)

    TB = min(S, 256)
    return _tok_call(body, name, S, TB,
                     [(proj, 3 * MIX, C_QKV // (3 * MIX)), (proj, 128, C_SM // 128), (dq, MIX, 0), (dk, MIX, 0),
                      (dv, MIX, 0), (dbg, 128, 0)], [conv_w, alog_l, dtb_l],
                     [(3 * MIX, F32), (128, BF16)], [((DN_CONV, 3 * MIX), F32), ((1, 128), F32), ((1, 128), F32)],
                     prev_in=[(proj, 3 * MIX, C_QKV // (3 * MIX))])


def _dn_pre_bwd2(dpre, conv_w, name):
    S = dpre.shape[0]
    TB = min(S, 256)
    nb = S // TB

    def body(i, d_ref, n_ref, w_ref, o_ref):
        halo = jnp.where(i < nb - 1, n_ref[...], 0.0)
        ds = jnp.concatenate([d_ref[...], halo], axis=0)
        out = ds[:TB] * w_ref[DN_CONV - 1:DN_CONV, :]
        for t in range(DN_CONV - 1):
            k = DN_CONV - 1 - t
            out = out + pltpu.roll(ds, TB + 8 - k, 0)[:TB] * w_ref[t:t + 1, :]
        o_ref[...] = out.astype(BF16)

    return _tok_call(body, name, S, TB, [(dpre, 3 * MIX, 0)], [conv_w], [(3 * MIX, BF16)],
                     next_in=[(dpre, 3 * MIX, 0)])[0]


def _dn_decay_terms(bgs, heads):
    C = DN_C
    ri = lax.broadcasted_iota(jnp.int32, (C, C), 0)
    ci = lax.broadcasted_iota(jnp.int32, (C, C), 1)
    tril, eye = ri >= ci, ri == ci
    beta = [b[:, h:h + 1] for b, h in zip(bgs, heads)]
    gcol = _dg_exact_lhs_many(tril, [jnp.broadcast_to(b[:, DN_H + h:DN_H + h + 1], (C, C))
                                     for b, h in zip(bgs, heads)], 1, 0)
    grow = [jnp.sum(jnp.where(eye, g, 0.0), axis=0, keepdims=True) for g in gcol]
    decay = [jnp.exp(jnp.where(tril, g - r, -1e30)) for g, r in zip(gcol, grow)]
    e_gc = [jnp.exp(g[:, 0:1]) for g in gcol]
    e_kd = [jnp.exp(g[C - 1:C, 0:1] - g[:, 0:1]) for g in gcol]
    cdec = [jnp.exp(g[C - 1:C, 0:1]) for g in gcol]
    return beta, decay, e_gc, e_kd, cdec


def _dn_nb(S):
    return 4 if S % (4 * DN_C) == 0 else 1


def _dn_prep_fwd(q, k, v, bg, name):
    S = q.shape[0]
    C, NB = DN_C, _dn_nb(S)
    TB = NB * C

    def kern(q_ref, k_ref, v_ref, bg_ref, t_ref, uw_ref, at_ref, qd_ref, kd_ref, dec_ref):
        lane = lax.broadcasted_iota(jnp.int32, (C, 128), 1)
        ri = lax.broadcasted_iota(jnp.int32, (C, C), 0)
        ci = lax.broadcasted_iota(jnp.int32, (C, C), 1)
        tril, eye = ri >= ci, ri == ci
        chains = [(cb, h) for cb in range(NB) for h in range(DN_H)]
        rows = lambda cb: slice(cb * C, (cb + 1) * C)
        head = lambda h: slice(h * DN_HD, (h + 1) * DN_HD)
        beta, decay, e_gc, e_kd, cdec = _dn_decay_terms([bg_ref[rows(cb), :] for cb, _ in chains],
                                                        [h for _, h in chains])
        qs = [q_ref[rows(cb), head(h)] for cb, h in chains]
        ks = [k_ref[rows(cb), head(h)] for cb, h in chains]
        kb = [kh * b for kh, b in zip(ks, beta)]
        x = [-jnp.where(ri > ci, _mm_nt(a, kh) * d, 0.0) for a, kh, d in zip(kb, ks, decay)]
        tm = [jnp.where(eye, 1.0, 0.0) + xi for xi in x]
        p = x
        for _ in range(5):
            p = _dg3_many(p, p, 1, 0)
            tm = [t + tp for t, tp in zip(tm, _dg3_many(tm, p, 1, 0))]
        rhs = [jnp.concatenate([v_ref[rows(cb), head(h)] * b, a * e], axis=1)
               for (cb, h), b, a, e in zip(chains, beta, kb, e_gc)]
        sol = _dg3_many(tm, rhs, 1, 0)
        attn = [_mm_nt(qh, kh) * d for qh, kh, d in zip(qs, ks, decay)]
        for n_, (cb, h) in enumerate(chains):
            rs, sl, hc = rows(cb), head(h), slice(h * C, (h + 1) * C)
            t_ref[rs, hc] = tm[n_]
            uw_ref[rs, sl] = sol[n_][:, :DN_HD]
            uw_ref[rs, MIX + h * DN_HD:MIX + (h + 1) * DN_HD] = sol[n_][:, DN_HD:]
            at_ref[rs, hc] = attn[n_]
            qd_ref[rs, sl] = (qs[n_] * e_gc[n_]).astype(BF16)
            kd_ref[rs, sl] = (ks[n_] * e_kd[n_]).astype(BF16)
        for cb in range(NB):
            dec = jnp.zeros((C, 128), F32)
            for h in range(DN_H):
                dec = dec + jnp.where(lane == h, cdec[cb * DN_H + h], 0.0)
            dec_ref[rows(cb), :] = dec

    tok = lambda w: pl.BlockSpec((TB, w), lambda i: (i, 0))
    return pl.pallas_call(
        kern, name=name, grid=(S // TB,), in_specs=[tok(MIX), tok(MIX), tok(MIX), tok(128)],
        out_specs=[tok(DN_H * C), tok(2 * MIX), tok(DN_H * C), tok(MIX), tok(MIX), tok(128)],
        out_shape=[jax.ShapeDtypeStruct((S, DN_H * C), F32), jax.ShapeDtypeStruct((S, 2 * MIX), F32),
                   jax.ShapeDtypeStruct((S, DN_H * C), F32), jax.ShapeDtypeStruct((S, MIX), BF16),
                   jax.ShapeDtypeStruct((S, MIX), BF16), jax.ShapeDtypeStruct((S, 128), F32)],
        compiler_params=_cparams(("parallel",)),
    )(q, k, v, bg)


def _dn_scan_fwd(uw, at, qd, kd, dec, name):
    S = uw.shape[0]
    C, NB = DN_C, _dn_nb(S)
    TB = NB * C
    SR = DN_H * DN_HD

    def kern(uw_ref, at_ref, qd_ref, kd_ref, dec_ref, o_ref, vn_ref, st_ref, state):
        @pl.when(pl.program_id(0) == 0)
        def _():
            state[...] = jnp.zeros_like(state)

        for cb in range(NB):
            rs = slice(cb * C, (cb + 1) * C)
            hs = range(DN_H)
            sls = [slice(h * DN_HD, (h + 1) * DN_HD) for h in hs]
            s_in = [state[sl, :] for sl in sls]
            ws = [_mm(uw_ref[rs, MIX + h * DN_HD:MIX + (h + 1) * DN_HD], s_in[h]) for h in hs]
            os_ = [_mm(qd_ref[rs, sls[h]], s_in[h]) for h in hs]
            vnew = [uw_ref[rs, sls[h]] - ws[h] for h in hs]
            oa = [_mm(at_ref[rs, h * C:(h + 1) * C], vnew[h]) for h in hs]
            kv = [_mm_tn(kd_ref[rs, sls[h]], vnew[h]) for h in hs]
            for h in hs:
                o_ref[rs, sls[h]] = os_[h] + oa[h]
                state[sls[h], :] = s_in[h] * dec_ref[cb * C:cb * C + 1, h:h + 1] + kv[h]
                st_ref[cb * SR + h * DN_HD:cb * SR + (h + 1) * DN_HD, :] = s_in[h]
                vn_ref[rs, sls[h]] = vnew[h]

    tok = lambda w: pl.BlockSpec((TB, w), lambda i: (i, 0))
    return pl.pallas_call(
        kern, name=name, grid=(S // TB,), in_specs=[tok(2 * MIX), tok(DN_H * C), tok(MIX), tok(MIX), tok(128)],
        out_specs=[tok(MIX), tok(MIX), pl.BlockSpec((NB * SR, DN_HD), lambda i: (i, 0))],
        out_shape=[jax.ShapeDtypeStruct((S, MIX), F32), jax.ShapeDtypeStruct((S, MIX), F32),
                   jax.ShapeDtypeStruct((S // C * SR, DN_HD), F32)],
        scratch_shapes=[pltpu.VMEM((SR, DN_HD), F32)],
        compiler_params=_cparams(("arbitrary",)),
    )(uw, at, qd, kd, dec)


def _dn_core_fwd(q, k, v, bg, name):
    tm, uw, at, qd, kd, dec = _dn_prep_fwd(q, k, v, bg, name + "_prep")
    o, vn, st = _dn_scan_fwd(uw, at, qd, kd, dec, name + "_scan")
    return o, dict(tm=tm, uw=uw, at=at, qd=qd, kd=kd, dec=dec, vn=vn, st=st)


def _dn_scan_bwd(sv, do, name):
    S = do.shape[0]
    C, NB = DN_C, _dn_nb(S)
    TB = NB * C
    SR = DN_H * DN_HD
    nb = S // TB

    def kern(do_ref, uw_ref, at_ref, qd_ref, kd_ref, dec_ref, vn_ref, st_ref, dvn_ref, dw_ref, dkd_ref, dc_ref, dstate):
        @pl.when(pl.program_id(0) == 0)
        def _():
            dstate[...] = jnp.zeros_like(dstate)

        lane = lax.broadcasted_iota(jnp.int32, (C, 128), 1)
        for cb in reversed(range(NB)):
            rs = slice(cb * C, (cb + 1) * C)
            dcrow = jnp.zeros((C, 128), F32)
            for h in range(DN_H):
                sl = slice(h * DN_HD, (h + 1) * DN_HD)
                doh, ds_o = do_ref[rs, sl], dstate[sl, :]
                s_in = st_ref[cb * SR + h * DN_HD:cb * SR + (h + 1) * DN_HD, :]
                d_vnew = _mm_tn(at_ref[rs, h * C:(h + 1) * C], doh) + _mm(kd_ref[rs, sl], ds_o)
                dvn_ref[rs, sl] = d_vnew
                dw_ref[rs, sl] = -_mm_nt(d_vnew, s_in)
                dkd_ref[rs, sl] = _mm_nt(vn_ref[rs, sl], ds_o)
                d_c = jnp.sum(jnp.sum(ds_o * s_in, axis=1, keepdims=True), axis=0, keepdims=True)
                dcrow = dcrow + jnp.where(lane == h, d_c, 0.0)
                dstate[sl, :] = (ds_o * dec_ref[cb * C:cb * C + 1, h:h + 1] + _mm_tn(qd_ref[rs, sl], doh)
                                 - _mm_tn(uw_ref[rs, MIX + h * DN_HD:MIX + (h + 1) * DN_HD], d_vnew))
            dc_ref[rs, :] = dcrow

    tok = lambda w: pl.BlockSpec((TB, w), lambda i: (nb - 1 - i, 0))
    return pl.pallas_call(
        kern, name=name, grid=(nb,),
        in_specs=[tok(MIX), tok(2 * MIX), tok(DN_H * C), tok(MIX), tok(MIX), tok(128), tok(MIX),
                  pl.BlockSpec((NB * SR, DN_HD), lambda i: (nb - 1 - i, 0))],
        out_specs=[tok(MIX), tok(MIX), tok(MIX), tok(128)],
        out_shape=[jax.ShapeDtypeStruct((S, MIX), F32)] * 3 + [jax.ShapeDtypeStruct((S, 128), F32)],
        scratch_shapes=[pltpu.VMEM((SR, DN_HD), F32)],
        compiler_params=_cparams(("arbitrary",)),
    )(do, sv["uw"], sv["at"], sv["qd"], sv["kd"], sv["dec"], sv["vn"], sv["st"])


def _dn_chunk_bwd(q, k, v, bg, sv, do, dvn, dw, dkd, dc, name):
    S = q.shape[0]
    C, NB = DN_C, _dn_nb(S)
    TB = NB * C
    SR = DN_H * DN_HD

    def kern(q_ref, k_ref, v_ref, bg_ref, t_ref, uw_ref, vn_ref, st_ref, do_ref, dvn_ref, dw_ref, dkd_ref, dc_ref,
             dq_ref, dk_ref, dv_ref, dbg_ref):
        lane = lax.broadcasted_iota(jnp.int32, (C, 128), 1)
        ri = lax.broadcasted_iota(jnp.int32, (C, C), 0)
        ci = lax.broadcasted_iota(jnp.int32, (C, C), 1)
        tril, eye, last = ri >= ci, ri == ci, ri[:, 0:1] == C - 1
        chains = [(cb, h) for cb in range(NB) for h in range(DN_H)]
        each = lambda f, *ls: [f(*a) for a in zip(*ls)]
        rsum = lambda t: jnp.sum(t, axis=-1, keepdims=True)
        rows = lambda cb: slice(cb * C, (cb + 1) * C)
        head = lambda h: slice(h * DN_HD, (h + 1) * DN_HD)
        tok = lambda ref: [ref[rows(cb), head(h)] for cb, h in chains]
        beta, decay, e_gc, e_kd, cdec = _dn_decay_terms([bg_ref[rows(cb), :] for cb, _ in chains],
                                                        [h for _, h in chains])
        qs, ks, vs, dos, vnew, d_kd = tok(q_ref), tok(k_ref), tok(v_ref), tok(do_ref), tok(vn_ref), tok(dkd_ref)
        s_in = [st_ref[cb * SR + h * DN_HD:cb * SR + (h + 1) * DN_HD, :] for cb, h in chains]
        d_c = [dc_ref[cb * C:cb * C + 1, h:h + 1] for cb, h in chains]
        kb = each(lambda a, b: a * b, ks, beta)
        kk = each(_mm_nt, kb, ks)
        attn = each(lambda a, b, d: _mm_nt(a, b) * d, qs, ks, decay)
        d_qd = each(_mm_nt, dos, s_in)
        d_attn = each(_mm_nt, dos, vnew)
        d_sol = [jnp.concatenate([dvn_ref[rows(cb), head(h)], dw_ref[rows(cb), head(h)]], axis=1) for cb, h in chains]
        sol = [jnp.concatenate([uw_ref[rows(cb), head(h)], uw_ref[rows(cb), MIX + h * DN_HD:MIX + (h + 1) * DN_HD]],
                               axis=1) for cb, h in chains]
        d_rhs = _dg3_many([t_ref[rows(cb), h * C:(h + 1) * C] for cb, h in chains], d_sol, 0, 0)
        d_a = _dg3_many(d_rhs, sol, 1, 1)
        d_kk = each(lambda a, d: jnp.where(ri > ci, -a, 0.0) * d, d_a, decay)
        d_qk = each(lambda a, d: a * d, d_attn, decay)
        dm = each(lambda a, b, c_, d: a * b + c_ * d, d_kk, kk, d_attn, attn)
        d_vb = [t[:, :DN_HD] for t in d_rhs]
        dz = [t[:, DN_HD:] for t in d_rhs]
        d_kb = each(lambda z, e, a, kh: z * e + _mm(a, kh), dz, e_gc, d_kk, ks)
        d_k = each(lambda a, b, c_, q: _mm_tn(a, b) + _mm_tn(c_, q), d_kk, kb, d_qk, qs)
        d_q = each(lambda a, kh, b, e: _mm(a, kh) + b * e, d_qk, ks, d_qd, e_gc)
        t_kd = each(lambda a, kh, e: rsum(a * kh * e), d_kd, ks, e_kd)
        d_gl = each(lambda t, c_, cd: jnp.sum(t, axis=0, keepdims=True) + c_ * cd, t_kd, d_c, cdec)
        d_gc = each(lambda z, a, e, m, b, q, t, gl:
                    rsum(z * a) * e + rsum(m) - rsum(jnp.where(eye, jnp.sum(m, axis=0, keepdims=True), 0.0))
                    + rsum(b * q) * e - t + jnp.where(last, gl, 0.0),
                    dz, kb, e_gc, dm, d_qd, qs, t_kd, d_gl)
        d_g = _dg_exact_lhs_many(ri <= ci, [jnp.broadcast_to(t, (C, 128)) for t in d_gc], 1, 0)
        d_beta = each(lambda a, v_, b, kh: rsum(a * v_) + rsum(b * kh), d_vb, vs, d_kb, ks)
        for n_, (cb, h) in enumerate(chains):
            dq_ref[rows(cb), head(h)] = d_q[n_]
            dk_ref[rows(cb), head(h)] = d_k[n_] + d_kd[n_] * e_kd[n_] + d_kb[n_] * beta[n_]
            dv_ref[rows(cb), head(h)] = d_vb[n_] * beta[n_]
        for cb in range(NB):
            dbg = jnp.zeros((C, 128), F32)
            for h in range(DN_H):
                n_ = cb * DN_H + h
                dbg = dbg + jnp.where(lane == h, d_beta[n_], 0.0) + jnp.where(lane == DN_H + h, d_g[n_], 0.0)
            dbg_ref[rows(cb), :] = dbg

    tok = lambda w: pl.BlockSpec((TB, w), lambda i: (i, 0))
    return pl.pallas_call(
        kern, name=name, grid=(S // TB,),
        in_specs=[tok(MIX), tok(MIX), tok(MIX), tok(128), tok(DN_H * C), tok(2 * MIX), tok(MIX),
                  pl.BlockSpec((NB * SR, DN_HD), lambda i: (i, 0)), tok(MIX), tok(MIX), tok(MIX), tok(MIX), tok(128)],
        out_specs=[tok(MIX), tok(MIX), tok(MIX), tok(128)],
        out_shape=[jax.ShapeDtypeStruct((S, MIX), F32)] * 3 + [jax.ShapeDtypeStruct((S, 128), F32)],
        compiler_params=_cparams(("parallel",)),
    )(q, k, v, bg, sv["tm"], sv["uw"], sv["vn"], sv["st"], do, dvn, dw, dkd, dc)


def _dn_core_bwd(q, k, v, bg, sv, do, name):
    dvn, dw, dkd, dc = _dn_scan_bwd(sv, do, name + "_scan")
    return _dn_chunk_bwd(q, k, v, bg, sv, do, dvn, dw, dkd, dc, name + "_chunk")


def _dn_post_fwd(o, proj, ng, name):
    S = o.shape[0]

    def body(i, o_ref, z_ref, g_ref, out_ref):
        gv = g_ref[...]
        for h in range(DN_H):
            sl = slice(h * DN_HD, (h + 1) * DN_HD)
            oh = o_ref[:, sl]
            r = lax.rsqrt(jnp.mean(oh * oh, axis=-1, keepdims=True) + EPS)
            out_ref[:, sl] = (oh * r * gv * _silu(z_ref[:, sl].astype(F32))).astype(BF16)

    return _tok_call(body, name, S, min(S, 512), [(o, MIX, 0), (proj, MIX, C_ZC // MIX)], [ng], [(MIX, BF16)])[0]


def _dn_post_bwd(o, proj, ng, dout, name):
    S = o.shape[0]

    def body(i, o_ref, z_ref, do_ref, g_ref, dov_ref, dz_ref, dg_ref):
        gv = g_ref[...]
        dg = jnp.zeros((1, DN_HD), F32)
        for h in range(DN_H):
            sl = slice(h * DN_HD, (h + 1) * DN_HD)
            oh, zh, dh = o_ref[:, sl], z_ref[:, sl].astype(F32), do_ref[:, sl].astype(F32)
            r = lax.rsqrt(jnp.mean(oh * oh, axis=-1, keepdims=True) + EPS)
            dz_ref[:, sl] = (dh * oh * r * gv * _dsilu(zh)).astype(BF16)
            dx, dgh = _rms_bwd_vals(oh, gv, dh * _silu(zh))
            dov_ref[:, sl] = dx
            dg = dg + dgh
        _acc(dg_ref, dg, i)

    return _tok_call(body, name, S, min(S, 512), [(o, MIX, 0), (proj, MIX, C_ZC // MIX), (dout, MIX, 0)], [ng],
                     [(MIX, F32), (MIX, BF16)], [((1, DN_HD), F32)])


def _layer_params(w, l):
    lane = jnp.arange(128)
    is_g = (lane >= DN_H) & (lane < 2 * DN_H)
    spread = lambda t: jnp.where(is_g, jnp.tile(t, 128 // DN_H), 0.0).reshape(1, 128)
    tril = jnp.tril(jnp.ones((SGU_T, SGU_T), bool))
    return dict(
        win=w["w_in"][l], wb=w["w_branch"][l], wout=w["w_out"][l], wgu=w["w_gate_up"][l], wdown=w["w_down"][l],
        conv=w["dn_conv_w"][l], attn_norm=w["attn_norm"][l].reshape(1, -1), ffn_norm=w["ffn_norm"][l].reshape(1, -1),
        lg=w["sgu_ln_g"][l].reshape(1, -1), lb=w["sgu_ln_b"][l].reshape(1, -1),
        wc=jnp.where(tril, w["sgu_w"][l], 0.0).reshape(SGU_G * SGU_T, SGU_T), bst=w["sgu_b"][l].T,
        sinks=w["attn_sinks"][l].reshape(1, -1), alog=spread(w["dn_a_log"][l]), dtb=spread(w["dn_dt_bias"][l]),
        ng=w["dn_norm"][l].reshape(1, -1))


def _layer_fwd(x, p, cos, sin, l):
    n = lambda s: f"l{l}_{s}"
    h = _rms_fwd(x, p["attn_norm"], n("rms1"))
    proj = _matmul(h, p["win"], out_dtype=BF16, name=n("mm_in"))
    out_a = _sgu_fwd(proj, p["lg"], p["lb"], p["wc"], p["bst"], n("sgu_fwd"))
    qr, kr, vr = _rope_fwd(proj, cos, sin, n("rope_fwd"))
    out_b = _swa_fwd(qr, kr, vr, p["sinks"], n("swa_fwd"))
    q, k, v, bg = _dn_pre_fwd(proj, p["conv"], p["alog"], p["dtb"], n("dn_pre_fwd"))
    o, dn = _dn_core_fwd(q, k, v, bg, n("dn_core_fwd"))
    out_c = _dn_post_fwd(o, proj, p["ng"], n("dn_post_fwd"))
    outs = (out_a, out_b, out_c)
    bds = [_matmul(outs[j], p["wb"][j], out_dtype=BF16, name=n(f"mm_branch{j}")) for j in range(3)]
    merged = _merge_fwd(proj, bds, n("merge_fwd"))
    x1 = _matmul(merged, p["wout"], add=x, name=n("mm_out"))
    h2 = _rms_fwd(x1, p["ffn_norm"], n("rms2"))
    gu = _matmul(h2, p["wgu"], out_dtype=BF16, name=n("mm_gu"))
    act = _swiglu_fwd(gu, n("swiglu_fwd"))
    x2 = _matmul(act, p["wdown"], add=x1, name=n("mm_down"))
    saved = dict(x=x, h=h, proj=proj, outs=outs, qr=qr, kr=kr, vr=vr, q=q, k=k, v=v, bg=bg, o=o, dn=dn, bds=bds,
                 merged=merged, x1=x1, h2=h2, gu=gu, act=act)
    return x2, saved


def _layer_bwd(dx2, s, p, cos, sin, l):
    n = lambda t: f"l{l}_{t}"
    proj = s["proj"]
    g = {}
    g["w_down"] = _matmul(s["act"], dx2, ta=True, out_dtype=BF16, name=n("wg_down"))
    dact = _matmul(dx2, p["wdown"], tb=True, out_dtype=BF16, name=n("dg_down"))
    dgu = _swiglu_bwd(s["gu"], dact, n("swiglu_bwd"))
    g["w_gate_up"] = _matmul(s["h2"], dgu, ta=True, out_dtype=BF16, name=n("wg_gu"))
    dh2 = _matmul(dgu, p["wgu"], tb=True, name=n("dg_gu"))
    dx1, g["ffn_norm"] = _rms_bwd_add(s["x1"], p["ffn_norm"], dh2, dx2, n("rms2_bwd"))
    g["w_out"] = _matmul(s["merged"], dx1, ta=True, out_dtype=BF16, name=n("wg_out"))
    dm = _matmul(dx1, p["wout"], tb=True, name=n("dg_out"))
    dbd0, dbd1, dbd2, dgp = _merge_bwd(proj, s["bds"], dm, n("merge_bwd"))
    dbds = (dbd0, dbd1, dbd2)
    g["w_branch"] = jnp.stack([_matmul(s["outs"][j], dbds[j], ta=True, out_dtype=BF16, name=n(f"wg_branch{j}"))
                               for j in range(3)])
    douts = [_matmul(dbds[j], p["wb"][j], tb=True, name=n(f"dg_branch{j}")) for j in range(3)]
    dua, dva, g["sgu_ln_g"], g["sgu_ln_b"], dwc, dbs = _sgu_bwd(proj, p["lg"], p["lb"], p["wc"], p["bst"], douts[0],
                                                                n("sgu_bwd"))
    g["sgu_w"] = dwc.reshape(SGU_G, SGU_T, SGU_T)
    g["sgu_b"] = dbs.T
    dqr, dkr, dvr, dsink = _swa_bwd(s["qr"], s["kr"], s["vr"], p["sinks"], douts[1], n("swa_bwd"))
    g["attn_sinks"] = dsink[0, :SWA_H]
    dqb, dkb, dvb = _rope_bwd(dqr, dkr, dvr, cos, sin, n("rope_bwd"))
    do, dz, dng = _dn_post_bwd(s["o"], proj, p["ng"], douts[2], n("dn_post_bwd"))
    g["dn_norm"] = dng[0]
    dq, dk, dv, dbg = _dn_core_bwd(s["q"], s["k"], s["v"], s["bg"], s["dn"], do, n("dn_core_bwd"))
    dpre, dsm, g["dn_conv_w"], dal, ddb = _dn_pre_bwd1(proj, p["conv"], p["alog"], p["dtb"], dq, dk, dv, dbg,
                                                       n("dn_pre_bwd1"))
    g["dn_a_log"] = dal[0, DN_H:2 * DN_H]
    g["dn_dt_bias"] = ddb[0, DN_H:2 * DN_H]
    dqkv = _dn_pre_bwd2(dpre, p["conv"], n("dn_pre_bwd2"))
    dproj = jnp.concatenate([dgp, dqkv, dua, dva, dqb, dz, dkb, dvb, dsm], axis=1)
    g["w_in"] = _matmul(s["h"], dproj, ta=True, out_dtype=BF16, name=n("wg_in"))
    dh = _matmul(dproj, p["win"], tb=True, name=n("dg_in"))
    dx, g["attn_norm"] = _rms_bwd_add(s["x"], p["attn_norm"], dh, dx1, n("rms1_bwd"))
    g["attn_norm"], g["ffn_norm"] = g["attn_norm"][0], g["ffn_norm"][0]
    g["sgu_ln_g"], g["sgu_ln_b"] = g["sgu_ln_g"][0], g["sgu_ln_b"][0]
    return dx, g


def _local_step(x, positions, target, w):
    cos, sin = _rope_tables(positions)
    params = [_layer_params(w, l) for l in range(DEPTH)]
    saves, xs = [], x
    for l in range(DEPTH):
        xs, sv = _layer_fwd(xs, params[l], cos, sin, l)
        saves.append(sv)
    dx, loss_row, dgf = _final_loss(xs, w["final_norm"].reshape(1, -1), target)
    grads = [None] * DEPTH
    for l in reversed(range(DEPTH)):
        dx, grads[l] = _layer_bwd(dx, saves[l], params[l], cos, sin, l)
    stacked = {k: jnp.stack([grads[l][k] for l in range(DEPTH)]) for k in grads[0]}
    stacked["final_norm"] = dgf[0]
    return loss_row[0, 0], dx, stacked


MESH = pl.DeviceIdType.MESH
HBM_SPEC = pl.BlockSpec(memory_space=pltpu.HBM)
VMEM_SPEC = pl.BlockSpec(memory_space=pltpu.VMEM)
N_CHIPS = 4
FLIPS = tuple((fx, fy, fc) for fx in (0, 1) for fy in (0, 1) for fc in (0, 1))[1:]
BIG = ("w_in", "w_branch", "w_out", "w_gate_up", "w_down")
BIG_SPEC = {
    "w_in": dict(rows=1024, cols=1792, axis=1, keep=1730, down=8, up=4),
    "w_branch": dict(rows=1536, cols=256, axis=1, keep=256, down=2, up=1),
    "w_out": dict(rows=256, cols=1024, axis=0, keep=1024, down=1, up=1),
    "w_gate_up": dict(rows=1024, cols=1408, axis=1, keep=1408, down=8, up=4),
    "w_down": dict(rows=704, cols=1024, axis=0, keep=1024, down=4, up=2),
}
CONV_ROWS, CONV_COLS = DEPTH * DN_CONV, 3 * MIX // N_CHIPS


def _full_shape(k):
    sp = BIG_SPEC[k]
    return (sp["rows"], N_CHIPS * sp["cols"]) if sp["axis"] == 1 else (N_CHIPS * sp["rows"], sp["cols"])


def _chip_block(ref, k, s, layer=None):
    sp = BIG_SPEC[k]
    if sp["axis"] == 1:
        idx = (slice(None), pl.ds(pl.multiple_of(s * sp["cols"], 128), sp["cols"]))
    else:
        idx = (pl.ds(pl.multiple_of(s * sp["rows"], 16), sp["rows"]), slice(None))
    return ref.at[idx] if layer is None else ref.at[(layer,) + idx]


def _me():
    return lax.axis_index("x"), lax.axis_index("y"), lax.axis_index("c")


def _peer(x, y, c, flip):
    fx, fy, fc = flip
    return (1 - x if fx else x, 1 - y if fy else y, 1 - c if fc else c)


class _Copies:
    def __init__(self, send_sems, recv_sems):
        self.send_sems, self.recv_sems, self.k, self.sent, self.landing = send_sems, recv_sems, 0, [], []

    def _copy(self, k, src, dst, to):
        return pltpu.make_async_remote_copy(src_ref=src, dst_ref=dst, send_sem=self.send_sems.at[k],
                                            recv_sem=self.recv_sems.at[k], device_id=to, device_id_type=MESH)

    def send(self, src, dst, to, lands):
        k = self.k
        self.k += 1
        cp = self._copy(k, src, dst, to)
        cp.start()
        self.sent.append(cp)
        self.landing.append(self._copy(k, lands, lands, to))
        return k

    def wait_landed(self, k):
        self.landing[k].wait_recv()

    def finish(self, landed=()):
        for k, cp in enumerate(self.landing):
            if k not in landed:
                cp.wait_recv()
        for cp in self.sent:
            cp.wait_send()


def _place_shard(shard, k, chip, name):
    sp = BIG_SPEC[k]
    rows, cols, keep = sp["rows"], sp["cols"], sp["keep"]
    tr = _pick(rows, (256, 64))
    nb = rows // tr
    if sp["axis"] == 1:
        out_spec = pl.BlockSpec((1, tr, cols), lambda l, i, ch: (l, i, ch[0]))
    else:
        out_spec = pl.BlockSpec((1, tr, cols), lambda l, i, ch: (l, ch[0] * nb + i, 0))

    def kern(ch_ref, x_ref, o_ref):
        v = x_ref[0].astype(BF16)
        if keep == cols:
            o_ref[0] = v
        else:
            o_ref[0, :, :keep] = v
            o_ref[0, :, keep:] = jnp.zeros((tr, cols - keep), BF16)

    return pl.pallas_call(
        kern, name=name, out_shape=jax.ShapeDtypeStruct((DEPTH,) + _full_shape(k), BF16),
        grid_spec=pltpu.PrefetchScalarGridSpec(
            num_scalar_prefetch=1, grid=(DEPTH, nb),
            in_specs=[pl.BlockSpec((1, tr, keep), lambda l, i, ch: (l, i, 0))], out_specs=out_spec),
        compiler_params=_cparams(("parallel", "parallel")),
    )(chip, shard)


def _allgather_weights(placed, conv):
    n = len(BIG)
    n_sem = 6 * n + 3

    def body(*refs):
        conv_ref = refs[n]
        out = dict(zip(BIG, refs[n + 1:2 * n + 1]))
        conv_out, send_sems, recv_sems, local_sem = refs[2 * n + 1:]
        x, y, c = _me()
        me = 2 * x + y
        chips = [(1 - x, y), (x, 1 - y), (1 - x, 1 - y)]
        net = _Copies(send_sems, recv_sems)

        def conv_block(s):
            return conv_out.at[:, pl.ds(pl.multiple_of(s * CONV_COLS, 128), CONV_COLS)]

        local = pltpu.make_async_copy(conv_ref, conv_block(me), local_sem)
        local.start()
        first = {}
        for k in BIG:
            for j, (px, py) in enumerate(chips):
                first[k, j] = net.send(_chip_block(out[k], k, me, c), _chip_block(out[k], k, me, c), (px, py, c),
                                       _chip_block(out[k], k, 2 * px + py, c))
        for px, py in chips:
            net.send(conv_ref, conv_block(me), (px, py, c), conv_block(2 * px + py))
        for k in BIG:
            for j, (px, py) in enumerate(chips):
                net.wait_landed(first[k, j])
                net.send(_chip_block(out[k], k, 2 * px + py, c), _chip_block(out[k], k, 2 * px + py, c), (x, y, 1 - c),
                         _chip_block(out[k], k, 2 * px + py, 1 - c))
        net.finish(landed=set(first.values()))
        local.wait()

    out_shape = [jax.ShapeDtypeStruct((DEPTH,) + _full_shape(k), BF16) for k in BIG]
    out_shape.append(jax.ShapeDtypeStruct((CONV_ROWS, N_CHIPS * CONV_COLS), F32))
    outs = pl.pallas_call(
        body, name="allgather_weights", out_shape=out_shape, in_specs=[HBM_SPEC] * (n + 1), out_specs=[HBM_SPEC] * (n + 1),
        input_output_aliases={i: i for i in range(n)},
        scratch_shapes=[pltpu.SemaphoreType.DMA((n_sem,)), pltpu.SemaphoreType.DMA((n_sem,)), pltpu.SemaphoreType.DMA],
    )(*[placed[k] for k in BIG], conv)
    return dict(zip(BIG, outs[:n])), outs[n]


def _row_chunks(ref, rows, n, layer=None):
    step = rows // n
    sl = [pl.ds(i * step, step) for i in range(n)]
    return [ref.at[s, :] if layer is None else ref.at[layer, s, :] for s in sl]


def _grads_to_sibling(grads):
    n = len(BIG)
    n_sem = sum(BIG_SPEC[k]["down"] for k in BIG)

    def body(*refs):
        g = dict(zip(BIG, refs[:n]))
        out = dict(zip(BIG, refs[n:2 * n]))
        send_sems, recv_sems = refs[2 * n:]
        x, y, c = _me()
        net = _Copies(send_sems, recv_sems)
        for k in BIG:
            rows, nch = _full_shape(k)[0], BIG_SPEC[k]["down"]
            for s, d in zip(_row_chunks(g[k], rows, nch, 1 - c), _row_chunks(out[k], rows, nch)):
                net.send(s, d, (x, y, 1 - c), d)
        net.finish()

    outs = pl.pallas_call(
        body, name="grads_to_sibling", out_shape=[jax.ShapeDtypeStruct(_full_shape(k), BF16) for k in BIG],
        in_specs=[HBM_SPEC] * n, out_specs=[HBM_SPEC] * n,
        scratch_shapes=[pltpu.SemaphoreType.DMA((n_sem,)), pltpu.SemaphoreType.DMA((n_sem,))],
    )(*[grads[k] for k in BIG])
    return dict(zip(BIG, outs))


def _add_layer(g2, other, layer, name):
    _, rows, cols = g2.shape
    tr = _pick(rows, (256, 128))

    def kern(l_ref, a_ref, b_ref, o_ref):
        o_ref[...] = (a_ref[0].astype(F32) + b_ref[...].astype(F32)).astype(BF16)

    return pl.pallas_call(
        kern, name=name, out_shape=jax.ShapeDtypeStruct((rows, cols), BF16),
        grid_spec=pltpu.PrefetchScalarGridSpec(
            num_scalar_prefetch=1, grid=(rows // tr,),
            in_specs=[pl.BlockSpec((1, tr, cols), lambda i, l: (l[0], i, 0)), pl.BlockSpec((tr, cols), lambda i, l: (i, 0))],
            out_specs=pl.BlockSpec((tr, cols), lambda i, l: (i, 0))),
        compiler_params=_cparams(("parallel",)),
    )(layer, g2, other)


def _scatter_chip_sums(sums):
    n = len(BIG)

    def body(*refs):
        src = dict(zip(BIG, refs[:n]))
        out = dict(zip(BIG, refs[n:2 * n]))
        send_sems, recv_sems = refs[2 * n:]
        x, y, c = _me()
        net = _Copies(send_sems, recv_sems)
        for k in BIG:
            for j, (px, py) in enumerate([(1 - x, y), (x, 1 - y), (1 - x, 1 - y)]):
                net.send(_chip_block(src[k], k, 2 * px + py), out[k].at[j], (px, py, c), out[k].at[j])
        net.finish()

    outs = pl.pallas_call(
        body, name="scatter_chip_sums",
        out_shape=[jax.ShapeDtypeStruct((N_CHIPS - 1, BIG_SPEC[k]["rows"], BIG_SPEC[k]["cols"]), BF16) for k in BIG],
        in_specs=[HBM_SPEC] * n, out_specs=[HBM_SPEC] * n,
        scratch_shapes=[pltpu.SemaphoreType.DMA((3 * n,)), pltpu.SemaphoreType.DMA((3 * n,))],
    )(*[sums[k] for k in BIG])
    return dict(zip(BIG, outs))


def _sum_chips(parts, own, k, where, name):
    sp = BIG_SPEC[k]
    rows, cols, keep = sp["rows"], sp["cols"], sp["keep"]
    tr = _pick(rows, (256, 64))
    nb = rows // tr
    if sp["axis"] == 1:
        own_spec = pl.BlockSpec((tr, cols), lambda i, w: (i, w[0]))
    else:
        own_spec = pl.BlockSpec((tr, cols), lambda i, w: (w[0] * nb + i, 0))

    def kern(w_ref, p_ref, own_ref, o_ref):
        tot = own_ref[...].astype(F32)
        for j in range(N_CHIPS - 1):
            tot = tot + p_ref[j].astype(F32)
        o_ref[0] = tot[:, :keep]

    return pl.pallas_call(
        kern, name=name, out_shape=jax.ShapeDtypeStruct((DEPTH, rows, keep), F32),
        grid_spec=pltpu.PrefetchScalarGridSpec(
            num_scalar_prefetch=1, grid=(nb,),
            in_specs=[pl.BlockSpec((N_CHIPS - 1, tr, cols), lambda i, w: (0, i, 0)), own_spec],
            out_specs=pl.BlockSpec((1, tr, keep), lambda i, w: (w[1], i, 0))),
        compiler_params=_cparams(("parallel",)),
    )(where, parts, own)


def _exchange_layers(red):
    n = len(BIG)
    n_sem = sum(BIG_SPEC[k]["up"] for k in BIG)

    def body(*refs):
        out = dict(zip(BIG, refs[n:2 * n]))
        send_sems, recv_sems = refs[2 * n:]
        x, y, c = _me()
        net = _Copies(send_sems, recv_sems)
        for k in BIG:
            rows, nch = BIG_SPEC[k]["rows"], BIG_SPEC[k]["up"]
            for mine, theirs in zip(_row_chunks(out[k], rows, nch, c), _row_chunks(out[k], rows, nch, 1 - c)):
                net.send(mine, mine, (x, y, 1 - c), theirs)
        net.finish()

    outs = pl.pallas_call(
        body, name="exchange_layers",
        out_shape=[jax.ShapeDtypeStruct((DEPTH, BIG_SPEC[k]["rows"], BIG_SPEC[k]["keep"]), F32) for k in BIG],
        in_specs=[HBM_SPEC] * n, out_specs=[HBM_SPEC] * n, input_output_aliases={i: i for i in range(n)},
        scratch_shapes=[pltpu.SemaphoreType.DMA((n_sem,)), pltpu.SemaphoreType.DMA((n_sem,))],
    )(*[red[k] for k in BIG])
    return dict(zip(BIG, outs))


def _adam_vals(g, w, m, v):
    m2 = ADAM_B1 * m + (1.0 - ADAM_B1) * g
    v2 = ADAM_B2 * v + (1.0 - ADAM_B2) * (g * g)
    m_hat = m2 / (1.0 - ADAM_B1 ** ADAM_STEP)
    v_hat = v2 / (1.0 - ADAM_B2 ** ADAM_STEP)
    return -ADAM_LR * (m_hat / (jnp.sqrt(v_hat) + ADAM_EPS) + ADAM_WD * w), m2, v2


def _allreduce_small_adam(groups):
    ng = len(groups)

    def body(*refs):
        ins = [refs[4 * i:4 * i + 4] for i in range(ng)]
        outs = [refs[4 * ng + 4 * i:4 * ng + 4 * i + 4] for i in range(ng)]
        bufs = refs[8 * ng:9 * ng]
        send_sems, recv_sems = refs[9 * ng:]
        x, y, c = _me()
        me = 4 * x + 2 * y + c
        net = _Copies(send_sems, recv_sems)
        for (g_ref, _, _, _), buf in zip(ins, bufs):
            buf[me] = g_ref[...]
            for f in FLIPS:
                px, py, pc = _peer(x, y, c, f)
                net.send(g_ref, buf.at[me], (px, py, pc), buf.at[4 * px + 2 * py + pc])
        net.finish()
        for (_, w_ref, m_ref, v_ref), (gs_ref, d_ref, nm_ref, nv_ref), buf in zip(ins, outs, bufs):
            tot = buf[0]
            for d in range(1, 8):
                tot = tot + buf[d]
            gs_ref[...] = tot
            d_ref[...], nm_ref[...], nv_ref[...] = _adam_vals(tot, w_ref[...], m_ref[...], v_ref[...])

    shapes = [jax.ShapeDtypeStruct(g[0].shape, F32) for g in groups for _ in range(4)]
    outs = pl.pallas_call(
        body, name="allreduce_small", out_shape=shapes, in_specs=[VMEM_SPEC] * (4 * ng), out_specs=[VMEM_SPEC] * (4 * ng),
        scratch_shapes=[pltpu.VMEM((8,) + g[0].shape, F32) for g in groups]
        + [pltpu.SemaphoreType.DMA((7 * ng,)), pltpu.SemaphoreType.DMA((7 * ng,))],
        compiler_params=pltpu.CompilerParams(vmem_limit_bytes=VMEM_LIMIT),
    )(*[t for g in groups for t in g])
    return [outs[4 * i:4 * i + 4] for i in range(ng)]


def _adam(g, w, m, v, name):
    shape = w.shape
    lead, rows, cols = math.prod(shape[:-2]), shape[-2], shape[-1]
    tr = _pick(rows, (256, 8, rows))
    spec = pl.BlockSpec((1, tr, cols), lambda l, i: (l, i, 0))

    def kern(g_ref, w_ref, m_ref, v_ref, d_ref, nm_ref, nv_ref):
        d_ref[...], nm_ref[...], nv_ref[...] = _adam_vals(g_ref[...], w_ref[...], m_ref[...], v_ref[...])

    outs = pl.pallas_call(
        kern, name=name, grid=(lead, rows // tr), in_specs=[spec] * 4, out_specs=[spec] * 3,
        out_shape=[jax.ShapeDtypeStruct((lead, rows, cols), F32)] * 3, compiler_params=_cparams(("parallel", "parallel")),
    )(*[t.reshape(lead, rows, cols) for t in (g, w, m, v)])
    return [o.reshape(shape) for o in outs]


SMALL = ("attn_norm", "sgu_ln_g", "sgu_ln_b", "sgu_w", "sgu_b", "attn_sinks", "dn_a_log", "dn_dt_bias", "dn_norm",
         "ffn_norm", "final_norm")


def _tile_rows(n):
    return -(-n // 1024) * 8


SLAB = tuple(k for k in SMALL if k != "sgu_w")


def _pack_small(vals, extra=()):
    tiles = []
    for t in [vals[k] for k in SLAB] + list(extra):
        flat = t.astype(F32).reshape(-1)
        rows = _tile_rows(flat.shape[0])
        tiles.append(jnp.pad(flat, (0, rows * 128 - flat.shape[0])).reshape(rows, 128))
    return jnp.concatenate(tiles, axis=0)


def _unpack_small(slab, shapes, extra_shapes=()):
    out, extras, o = {}, [], 0
    for k, shp in [(k, shapes[k]) for k in SLAB] + [(None, s) for s in extra_shapes]:
        n = math.prod(shp)
        rows = _tile_rows(n)
        t = slab[o:o + rows].reshape(-1)[:n].reshape(shp)
        o += rows
        if k is None:
            extras.append(t)
        else:
            out[k] = t
    return out, extras


def _in_col_segments():
    shard, padded = IN_COLS // N_CHIPS, BIG_SPEC["w_in"]["cols"]
    segs, mine = [], 0
    for a, n in IN_PIECES:
        o = a
        while o < a + n:
            end = min(a + n, (o // shard + 1) * shard)
            segs.append(((o // shard) * padded + o % shard, mine + o - a, end - o))
            o = end
        mine += n
    return segs


def _move_cols(x, segs, out_cols, name):
    layers, rows, cols = x.shape
    tr = _pick(rows, (256, rows))
    gaps, at = [], 0
    for d, w in sorted((d, w) for _, d, w in segs):
        if d > at:
            gaps.append((at, d - at))
        at = d + w
    if at < out_cols:
        gaps.append((at, out_cols - at))

    def kern(x_ref, o_ref):
        for s, d, w in segs:
            o_ref[0, :, d:d + w] = x_ref[0, :, s:s + w]
        for d, w in gaps:
            o_ref[0, :, d:d + w] = jnp.zeros((tr, w), x.dtype)

    return pl.pallas_call(
        kern, name=name, grid=(layers, rows // tr), in_specs=[pl.BlockSpec((1, tr, cols), lambda l, i: (l, i, 0))],
        out_specs=pl.BlockSpec((1, tr, out_cols), lambda l, i: (l, i, 0)),
        out_shape=jax.ShapeDtypeStruct((layers, rows, out_cols), x.dtype), compiler_params=_cparams(("parallel", "parallel")),
    )(x)


WEIGHTS = ("attn_norm", "w_in", "sgu_ln_g", "sgu_ln_b", "sgu_w", "sgu_b", "attn_sinks", "dn_conv_w", "dn_a_log",
           "dn_dt_bias", "dn_norm", "w_branch", "w_out", "ffn_norm", "w_gate_up", "w_down", "final_norm")


def kernel(x, positions, attn_norm, w_in, sgu_ln_g, sgu_ln_b, sgu_w, sgu_b, attn_sinks, dn_conv_w, dn_a_log, dn_dt_bias, dn_norm, w_branch, w_out, ffn_norm, w_gate_up, w_down, final_norm, loss_target, m_attn_norm, m_w_in, m_sgu_ln_g, m_sgu_ln_b, m_sgu_w, m_sgu_b, m_attn_sinks, m_dn_conv_w, m_dn_a_log, m_dn_dt_bias, m_dn_norm, m_w_branch, m_w_out, m_ffn_norm, m_w_gate_up, m_w_down, m_final_norm, v_attn_norm, v_w_in, v_sgu_ln_g, v_sgu_ln_b, v_sgu_w, v_sgu_b, v_attn_sinks, v_dn_conv_w, v_dn_a_log, v_dn_dt_bias, v_dn_norm, v_w_branch, v_w_out, v_ffn_norm, v_w_gate_up, v_w_down, v_final_norm):
    given = dict(locals())
    W = {k: given[k] for k in WEIGHTS}
    M = {k: given["m_" + k] for k in WEIGHTS}
    V = {k: given["v_" + k] for k in WEIGHTS}
    chip = 2 * lax.axis_index("x") + lax.axis_index("y")
    core = lax.axis_index("c")
    layer = core.astype(jnp.int32).reshape(1)
    chip1 = chip.astype(jnp.int32).reshape(1)
    where = jnp.stack([chip, core]).astype(jnp.int32)

    placed = {k: _place_shard(W[k].reshape(DEPTH, BIG_SPEC[k]["rows"], BIG_SPEC[k]["keep"]), k, chip1, "place_" + k)
              for k in BIG}
    full, conv_full = _allgather_weights(placed, dn_conv_w.reshape(CONV_ROWS, CONV_COLS))
    w = {k: W[k] for k in SMALL}
    w.update(full)
    segs = _in_col_segments()
    w["w_in"] = _move_cols(full["w_in"], segs, IN_R, "w_in_cols")
    w["w_branch"] = full["w_branch"].reshape(DEPTH, 3, MIX, D_MODEL)
    w["dn_conv_w"] = conv_full.reshape(DEPTH, DN_CONV, 3 * MIX)

    loss, dx, g = _local_step(x[0], positions[0], loss_target[0], w)

    gb = {k: g[k] for k in BIG}
    gb["w_in"] = _move_cols(g["w_in"], [(d, s, n) for s, d, n in segs], _full_shape("w_in")[1], "g_in_cols")
    gb["w_branch"] = g["w_branch"].reshape(DEPTH, 3 * MIX, D_MODEL)
    sibling = _grads_to_sibling(gb)
    chip_sums = {k: _add_layer(gb[k], sibling[k], layer, "chip_sum_" + k) for k in BIG}
    parts = _scatter_chip_sums(chip_sums)
    reduced = _exchange_layers({k: _sum_chips(parts[k], chip_sums[k], k, where, "sum_" + k) for k in BIG})
    grads = {k: reduced[k].reshape(W[k].shape) for k in BIG}

    small_shapes = {k: W[k].shape for k in SMALL}
    no_extra = (jnp.zeros(g["dn_conv_w"].shape, F32), jnp.zeros((1,), F32))
    rows128 = lambda t: t.reshape(-1, 128)
    sgu, (gs, ds, nms, nvs) = _allreduce_small_adam([
        tuple(rows128(d["sgu_w"]) for d in (g, W, M, V)),
        (_pack_small(g, (g["dn_conv_w"], loss.reshape(1))), _pack_small(W, no_extra), _pack_small(M, no_extra),
         _pack_small(V, no_extra))])
    gsm, (conv_sum, loss_sum) = _unpack_small(gs, small_shapes, (g["dn_conv_w"].shape, (1,)))
    grads.update(gsm)
    grads["dn_conv_w"] = lax.dynamic_slice_in_dim(conv_sum, chip * dn_conv_w.shape[2], dn_conv_w.shape[2], axis=2)
    loss_total = loss_sum[0]
    delta, new_m, new_v = (_unpack_small(t, small_shapes)[0] for t in (ds, nms, nvs))
    for d, t in zip((grads, delta, new_m, new_v), sgu):
        d["sgu_w"] = t.reshape(sgu_w.shape)
    for k in BIG + ("dn_conv_w",):
        delta[k], new_m[k], new_v[k] = _adam(grads[k], W[k], M[k], V[k], "adam_" + k)

    return (loss_total, dx[None], *[grads[k] for k in WEIGHTS], *[delta[k] for k in WEIGHTS],
            *[new_m[k] for k in WEIGHTS], *[new_v[k] for k in WEIGHTS])
```

```python
import functools
import math

import jax
import jax.numpy as jnp
from jax import lax
from jax.experimental import pallas as pl
from jax.experimental.pallas import tpu as pltpu

F32 = jnp.float32
BF16 = jnp.bfloat16
HI = lax.Precision.HIGHEST

D_MODEL = 1024
DEPTH = 2
MIX = 512
EPS = 1e-6
SGU_G, SGU_T = 4, 128
SWA_H, SWA_KV, SWA_HD, WINDOW = 8, 2, 64, 128
ROPE_THETA, ROPE_DIM = 500000.0, 16
DN_H, DN_HD, DN_CONV, DN_C = 4, 128, 4, 64
D_FF = 2816
IN_COLS = 6920
IN_PIECES = ((3848, 3072), (1792, 1536), (0, 512), (512, 512), (1024, 512), (3328, 512), (1536, 128), (1664, 128),
             (3840, 8))
IN_PAD = 120
IN_R = 7040
C_GATE, C_QKV, C_UA, C_VA, C_QB, C_ZC, C_KB, C_VB, C_SM = 0, 3072, 4608, 5120, 5632, 6144, 6656, 6784, 6912

ADAM_LR, ADAM_B1, ADAM_B2, ADAM_EPS, ADAM_WD, ADAM_STEP = 0.001, 0.9, 0.999, 1e-08, 0.01, 10
VMEM_LIMIT = 56 * 1024 * 1024


def _cparams(sem):
    return pltpu.CompilerParams(dimension_semantics=sem, vmem_limit_bytes=VMEM_LIMIT)


def _dg(a, b, ca, cb, prec=None):
    return lax.dot_general(a, b, (((ca,), (cb,)), ((), ())), precision=prec, preferred_element_type=F32)


def _split(x):
    hi = x.astype(BF16)
    return hi, (x - hi.astype(F32)).astype(BF16)


def _dg3_many(as_, bs, ca, cb):
    sa = [_split(a) for a in as_]
    sb = [_split(b) for b in bs]
    hh = [_dg(a[0], b[0], ca, cb) for a, b in zip(sa, sb)]
    hl = [_dg(a[0], b[1], ca, cb) for a, b in zip(sa, sb)]
    lh = [_dg(a[1], b[0], ca, cb) for a, b in zip(sa, sb)]
    return [x + (y + z) for x, y, z in zip(hh, hl, lh)]


def _dg_exact_lhs_many(a01, bs, ca, cb):
    a = a01.astype(BF16)
    b1 = [b.astype(BF16) for b in bs]
    r1 = [b - t.astype(F32) for b, t in zip(bs, b1)]
    b2 = [r.astype(BF16) for r in r1]
    b3 = [(r - t.astype(F32)).astype(BF16) for r, t in zip(r1, b2)]
    d1 = [_dg(a, t, ca, cb) for t in b1]
    d2 = [_dg(a, t, ca, cb) for t in b2]
    d3 = [_dg(a, t, ca, cb) for t in b3]
    return [x + (y + z) for x, y, z in zip(d1, d2, d3)]


def _mm(a, b):
    return _dg(a.astype(BF16), b.astype(BF16), 1, 0)


def _mm_nt(a, b):
    return _dg(a.astype(BF16), b.astype(BF16), 1, 1)


def _mm_tn(a, b):
    return _dg(a.astype(BF16), b.astype(BF16), 0, 0)


def _sigmoid(x):
    return 0.5 * jnp.tanh(0.5 * x) + 0.5


def _silu(x):
    return x * _sigmoid(x)


def _dsilu(x):
    s = _sigmoid(x)
    return s * (1.0 + x * (1.0 - s))


_GC = math.sqrt(2.0 / math.pi)


def _gelu(x):
    return 0.5 * x * (1.0 + jnp.tanh(_GC * (x + 0.044715 * x * x * x)))


def _dgelu(x):
    t = jnp.tanh(_GC * (x + 0.044715 * x * x * x))
    return 0.5 * (1.0 + t) + 0.5 * x * (1.0 - t * t) * _GC * (1.0 + 3.0 * 0.044715 * x * x)


def _softplus(x):
    return jnp.maximum(x, 0.0) + jnp.log(1.0 + jnp.exp(-jnp.abs(x)))


def _acc(ref, val, i):
    @pl.when(i == 0)
    def _():
        ref[...] = val

    @pl.when(i > 0)
    def _():
        ref[...] += val


def _halo_rows(dtype):
    return 8 * 4 // jnp.dtype(dtype).itemsize


def _tok_call(body, name, S, TB, tok_in, const_in=(), tok_out=(), acc_out=(), prev_in=(), next_in=(), smem_in=()):
    nb = S // TB
    in_specs, args = [], []
    for a, w, cb in tok_in:
        in_specs.append(pl.BlockSpec((TB, w), functools.partial(lambda i, cb: (i, cb), cb=cb)))
        args.append(a)
    for a, w, cb in prev_in:
        hr = _halo_rows(a.dtype)
        in_specs.append(pl.BlockSpec((hr, w), functools.partial(
            lambda i, cb, r: (jnp.maximum(i * r - 1, 0), cb), cb=cb, r=TB // hr)))
        args.append(a)
    for a, w, cb in next_in:
        hr = _halo_rows(a.dtype)
        in_specs.append(pl.BlockSpec((hr, w), functools.partial(
            lambda i, cb, r, last: (jnp.minimum((i + 1) * r, last), cb), cb=cb, r=TB // hr, last=S // hr - 1)))
        args.append(a)
    for a in const_in:
        in_specs.append(pl.BlockSpec(a.shape, lambda i: (0, 0)))
        args.append(a)
    for a in smem_in:
        in_specs.append(pl.BlockSpec(memory_space=pltpu.SMEM))
        args.append(a)
    out_specs, out_shape = [], []
    for w, dt in tok_out:
        out_specs.append(pl.BlockSpec((TB, w), lambda i: (i, 0)))
        out_shape.append(jax.ShapeDtypeStruct((S, w), dt))
    for shp, dt in acc_out:
        out_specs.append(pl.BlockSpec(shp, lambda i: (0, 0)))
        out_shape.append(jax.ShapeDtypeStruct(shp, dt))

    def kern(*refs):
        body(pl.program_id(0), *refs)

    return pl.pallas_call(
        kern, name=name, grid=(nb,), in_specs=in_specs, out_specs=out_specs, out_shape=out_shape,
        compiler_params=_cparams(("arbitrary",)),
    )(*args)


MM_BLOCKS = (1024, 1408, 640, 512, 256, 128)


def _pick(n, cands):
    for c in cands:
        if n % c == 0:
            return c
    return n


MM_VMEM_BUDGET = 44 * 1024 * 1024


def _mm_blocks(M, N, K, a_bytes, b_bytes, o_bytes, add_bytes):
    bn = _pick(N, MM_BLOCKS)
    fits = None
    for bk in [K] + [c for c in (2816, 2048) + MM_BLOCKS if c < K and K % c == 0]:
        for bm in [c for c in MM_BLOCKS if M % c == 0 and c >= min(M, 512)]:
            b_bufs = 1 if (bk == K and bn == N) else 2
            need = 2 * bm * bk * a_bytes + b_bufs * bk * bn * b_bytes + 2 * bm * bn * (o_bytes + add_bytes)
            need += bm * bn * 4 if bk < K else 0
            if need <= MM_VMEM_BUDGET:
                fits = fits or (bm, bn, bk)
                if (M // bm) * (N // bn) * (K // bk) >= 4:
                    return bm, bn, bk
    if fits is None:
        raise ValueError(f"no matmul blocks for {(M, N, K)}")
    return fits


def _matmul(a, b, *, ta=False, tb=False, add=None, out_dtype=F32, name):
    M, K = (a.shape[1], a.shape[0]) if ta else a.shape
    N = b.shape[0] if tb else b.shape[1]
    bm, bn, bk = _mm_blocks(M, N, K, a.dtype.itemsize, b.dtype.itemsize, jnp.dtype(out_dtype).itemsize,
                            0 if add is None else add.dtype.itemsize)
    nk = K // bk
    b_mode = dict(pipeline_mode=pl.Buffered(1)) if (bk == K and bn == N) else {}
    a_spec = pl.BlockSpec((bk, bm), lambda i, j, k: (k, i)) if ta else pl.BlockSpec((bm, bk), lambda i, j, k: (i, k))
    b_spec = (pl.BlockSpec((bn, bk), lambda i, j, k: (j, k), **b_mode) if tb
              else pl.BlockSpec((bk, bn), lambda i, j, k: (k, j), **b_mode))
    o_spec = pl.BlockSpec((bm, bn), lambda i, j, k: (i, j))
    ca, cb = (0 if ta else 1), (1 if tb else 0)

    def kern(*refs):
        a_ref, b_ref = refs[:2]
        add_ref = refs[2] if add is not None else None
        o_ref = refs[3] if add is not None else refs[2]
        p = _dg(a_ref[...].astype(BF16), b_ref[...].astype(BF16), ca, cb)

        def finish(r):
            if add is not None:
                r = r + add_ref[...].astype(F32)
            o_ref[...] = r.astype(out_dtype)

        if nk == 1:
            finish(p)
            return
        acc_ref = refs[-1]
        k = pl.program_id(2)

        @pl.when(k == 0)
        def _():
            acc_ref[...] = p

        @pl.when((k > 0) & (k < nk - 1))
        def _():
            acc_ref[...] += p

        @pl.when(k == nk - 1)
        def _():
            finish(acc_ref[...] + p)

    in_specs = [a_spec, b_spec] + ([o_spec] if add is not None else [])
    args = (a, b) + ((add,) if add is not None else ())
    return pl.pallas_call(
        kern, name=name, grid=(M // bm, N // bn, nk), in_specs=in_specs, out_specs=o_spec,
        out_shape=jax.ShapeDtypeStruct((M, N), out_dtype),
        scratch_shapes=[pltpu.VMEM((bm, bn), F32)] if nk > 1 else [],
        compiler_params=_cparams(("parallel", "parallel", "arbitrary")),
    )(*args)


def _rms_fwd(x, g, name):
    S = x.shape[0]

    def body(i, x_ref, g_ref, h_ref):
        xv = x_ref[...]
        r = lax.rsqrt(jnp.mean(xv * xv, axis=-1, keepdims=True) + EPS)
        h_ref[...] = (xv * r * g_ref[...]).astype(BF16)

    return _tok_call(body, name, S, min(S, 512), [(x, D_MODEL, 0)], [g], [(D_MODEL, BF16)])[0]


def _rms_bwd_vals(xv, g, dh):
    r = lax.rsqrt(jnp.mean(xv * xv, axis=-1, keepdims=True) + EPS)
    u = dh * g
    dx = r * u - xv * (r * r * r) * jnp.mean(u * xv, axis=-1, keepdims=True)
    dg = jnp.sum(dh * xv * r, axis=0, keepdims=True)
    return dx, dg


def _rms_bwd_add(x, g, dh, dres, name):
    S = x.shape[0]

    def body(i, x_ref, dh_ref, dr_ref, g_ref, dx_ref, dg_ref):
        dx, dg = _rms_bwd_vals(x_ref[...], g_ref[...], dh_ref[...].astype(F32))
        dx_ref[...] = dr_ref[...] + dx
        _acc(dg_ref, dg, i)

    return _tok_call(body, name, S, min(S, 512), [(x, D_MODEL, 0), (dh, D_MODEL, 0), (dres, D_MODEL, 0)], [g],
                     [(D_MODEL, F32)], [((1, D_MODEL), F32)])


def _final_loss(x, g, target):
    S = x.shape[0]

    def body(i, x_ref, t_ref, g_ref, dx_ref, loss_ref, dg_ref):
        xv, gv = x_ref[...], g_ref[...]
        r = lax.rsqrt(jnp.mean(xv * xv, axis=-1, keepdims=True) + EPS)
        e = xv * r * gv - t_ref[...]
        part = 0.5 * jnp.sum(jnp.mean(e * e, axis=-1, keepdims=True), axis=0, keepdims=True)
        dx, dg = _rms_bwd_vals(xv, gv, e * (1.0 / D_MODEL))
        dx_ref[...] = dx
        _acc(loss_ref, jnp.broadcast_to(part, (1, 128)), i)
        _acc(dg_ref, dg, i)

    return _tok_call(body, "final_loss", S, min(S, 512), [(x, D_MODEL, 0), (target, D_MODEL, 0)], [g],
                     [(D_MODEL, F32)], [((1, 128), F32), ((1, D_MODEL), F32)])


def _swiglu_fwd(gu, name):
    S = gu.shape[0]

    def body(i, gu_ref, a_ref):
        a_ref[...] = (_silu(gu_ref[:, :D_FF].astype(F32)) * gu_ref[:, D_FF:].astype(F32)).astype(BF16)

    return _tok_call(body, name, S, min(S, 256), [(gu, 2 * D_FF, 0)], [], [(D_FF, BF16)])[0]


def _swiglu_bwd(gu, dact, name):
    S = gu.shape[0]

    def body(i, gu_ref, da_ref, dgu_ref):
        gg, uu, da = gu_ref[:, :D_FF].astype(F32), gu_ref[:, D_FF:].astype(F32), da_ref[...].astype(F32)
        dgu_ref[:, :D_FF] = (da * uu * _dsilu(gg)).astype(BF16)
        dgu_ref[:, D_FF:] = (da * _silu(gg)).astype(BF16)

    return _tok_call(body, name, S, min(S, 256), [(gu, 2 * D_FF, 0), (dact, D_FF, 0)], [], [(2 * D_FF, BF16)])[0]


def _merge_fwd(proj, bds, name):
    S = proj.shape[0]

    def body(i, g0, g1, g2, b0, b1, b2, m_ref):
        m = jnp.zeros(m_ref.shape, F32)
        for gr, br in ((g0, b0), (g1, b1), (g2, b2)):
            m = m + _sigmoid(gr[...].astype(F32)) * br[...].astype(F32)
        m_ref[...] = m.astype(BF16)

    tok = [(proj, D_MODEL, n) for n in range(3)] + [(b, D_MODEL, 0) for b in bds]
    return _tok_call(body, name, S, min(S, 512), tok, [], [(D_MODEL, BF16)])[0]


def _merge_bwd(proj, bds, dm, name):
    S = proj.shape[0]

    def body(i, g0, g1, g2, b0, b1, b2, dm_ref, d0, d1, d2, dgp_ref):
        dmv = dm_ref[...]
        for n, (gr, br, dr) in enumerate(((g0, b0, d0), (g1, b1, d1), (g2, b2, d2))):
            s = _sigmoid(gr[...].astype(F32))
            dr[...] = (dmv * s).astype(BF16)
            dgp_ref[:, n * D_MODEL:(n + 1) * D_MODEL] = (dmv * br[...].astype(F32) * s * (1.0 - s)).astype(BF16)

    tok = [(proj, D_MODEL, n) for n in range(3)] + [(b, D_MODEL, 0) for b in bds] + [(dm, D_MODEL, 0)]
    return _tok_call(body, name, S, min(S, 512), tok, [],
                     [(D_MODEL, BF16)] * 3 + [(3 * D_MODEL, BF16)])


def _sgu_ln(v, lg, lb):
    mu = jnp.mean(v, axis=-1, keepdims=True)
    vc = v - mu
    rstd = lax.rsqrt(jnp.mean(vc * vc, axis=-1, keepdims=True) + EPS)
    vhat = vc * rstd
    return vhat, rstd, vhat * lg + lb


def _sgu_fwd(proj, lg, lb, wc, bst, name):
    S = proj.shape[0]

    def body(i, ua_ref, va_ref, lg_ref, lb_ref, wc_ref, bs_ref, o_ref):
        u = _gelu(ua_ref[...].astype(F32))
        _, _, vn = _sgu_ln(_gelu(va_ref[...].astype(F32)), lg_ref[...], lb_ref[...])
        for g in range(SGU_G):
            sl = slice(g * 128, (g + 1) * 128)
            mixed = _mm(wc_ref[sl, :], vn[:, sl]) + bs_ref[:, g:g + 1]
            o_ref[:, sl] = (u[:, sl] * mixed).astype(BF16)

    return _tok_call(body, name, S, SGU_T, [(proj, MIX, C_UA // MIX), (proj, MIX, C_VA // MIX)], [lg, lb, wc, bst],
                     [(MIX, BF16)])[0]


def _sgu_bwd(proj, lg, lb, wc, bst, dout, name):
    S = proj.shape[0]

    def body(i, ua_ref, va_ref, do_ref, lg_ref, lb_ref, wc_ref, bs_ref, dua_ref, dva_ref, dlg_ref, dlb_ref, dwc_ref,
             dbs_ref):
        ua, va, do = ua_ref[...].astype(F32), va_ref[...].astype(F32), do_ref[...].astype(F32)
        u = _gelu(ua)
        lgv = lg_ref[...]
        vhat, rstd, vn = _sgu_ln(_gelu(va), lgv, lb_ref[...])
        tril = lax.broadcasted_iota(jnp.int32, (128, 128), 0) >= lax.broadcasted_iota(jnp.int32, (128, 128), 1)
        lane4 = lax.broadcasted_iota(jnp.int32, (128, 4), 1)
        dvn_parts, dbs = [], jnp.zeros((128, 4), F32)
        for g in range(SGU_G):
            sl = slice(g * 128, (g + 1) * 128)
            wg = wc_ref[sl, :]
            mixed = _mm(wg, vn[:, sl]) + bs_ref[:, g:g + 1]
            dua_ref[:, sl] = (do[:, sl] * mixed * _dgelu(ua[:, sl])).astype(BF16)
            dmix = do[:, sl] * u[:, sl]
            dbs = dbs + jnp.where(lane4 == g, jnp.sum(dmix, axis=-1, keepdims=True), 0.0)
            dwg = jnp.where(tril, _mm_nt(dmix, vn[:, sl]), 0.0)
            _acc(dwc_ref.at[sl, :], dwg, i)
            dvn_parts.append(_mm_tn(wg, dmix))
        dvn = jnp.concatenate(dvn_parts, axis=1)
        _acc(dbs_ref, dbs, i)
        _acc(dlg_ref, jnp.sum(dvn * vhat, axis=0, keepdims=True), i)
        _acc(dlb_ref, jnp.sum(dvn, axis=0, keepdims=True), i)
        dvh = dvn * lgv
        dv = rstd * (dvh - jnp.mean(dvh, axis=-1, keepdims=True) - vhat * jnp.mean(dvh * vhat, axis=-1, keepdims=True))
        dva_ref[...] = (dv * _dgelu(va)).astype(BF16)

    return _tok_call(body, name, S, SGU_T, [(proj, MIX, C_UA // MIX), (proj, MIX, C_VA // MIX), (dout, MIX, 0)],
                     [lg, lb, wc, bst], [(MIX, BF16), (MIX, BF16)],
                     [((1, MIX), F32), ((1, MIX), F32), ((SGU_G * 128, 128), F32), ((128, 4), F32)])


def _rope_tables(positions):
    S = positions.shape[0]
    inv_freq = ROPE_THETA ** (-jnp.arange(0, ROPE_DIM, 2, dtype=F32) / ROPE_DIM)
    ang = positions.astype(F32)[:, None] * inv_freq
    c, s = jnp.cos(ang), jnp.sin(ang)
    c64 = jnp.concatenate([c, c, jnp.ones((S, SWA_HD - ROPE_DIM), F32)], axis=1)
    s64 = jnp.concatenate([-s, s, jnp.zeros((S, SWA_HD - ROPE_DIM), F32)], axis=1)
    return jnp.tile(c64, (1, 2)), jnp.tile(s64, (1, 2))


def _rope128(x, c, s):
    lane = lax.broadcasted_iota(jnp.int32, x.shape, 1) % SWA_HD
    swapped = jnp.where(lane < ROPE_DIM // 2, pltpu.roll(x, 128 - ROPE_DIM // 2, 1), pltpu.roll(x, ROPE_DIM // 2, 1))
    return x * c + swapped * s


def _rope_t128(y, c, s):
    ys = y * s
    lane = lax.broadcasted_iota(jnp.int32, y.shape, 1) % SWA_HD
    swapped = jnp.where(lane < ROPE_DIM // 2, pltpu.roll(ys, 128 - ROPE_DIM // 2, 1), pltpu.roll(ys, ROPE_DIM // 2, 1))
    return y * c + jnp.where(lane < ROPE_DIM, swapped, 0.0)


def _rope_fwd(proj, cos, sin, name):
    S = proj.shape[0]
    scale = SWA_HD ** -0.5

    def body(i, q_ref, k_ref, v_ref, c_ref, s_ref, qo_ref, ko_ref, vo_ref):
        c, s = c_ref[...], s_ref[...]
        for j in range(4):
            sl = slice(j * 128, (j + 1) * 128)
            qo_ref[:, sl] = (_rope128(q_ref[:, sl].astype(F32), c, s) * scale).astype(BF16)
        ko_ref[...] = _rope128(k_ref[...].astype(F32), c, s).astype(BF16)
        vo_ref[...] = v_ref[...].astype(BF16)

    return _tok_call(body, name, S, min(S, 512),
                     [(proj, MIX, C_QB // MIX), (proj, 128, C_KB // 128), (proj, 128, C_VB // 128), (cos, 128, 0),
                      (sin, 128, 0)], [], [(MIX, BF16), (128, BF16), (128, BF16)])


def _rope_bwd(dq, dk, dv, cos, sin, name):
    S = dq.shape[0]
    scale = SWA_HD ** -0.5

    def body(i, dq_ref, dk_ref, dv_ref, c_ref, s_ref, qo_ref, ko_ref, vo_ref):
        c, s = c_ref[...], s_ref[...]
        for j in range(4):
            sl = slice(j * 128, (j + 1) * 128)
            qo_ref[:, sl] = _rope_t128(dq_ref[:, sl] * scale, c, s).astype(BF16)
        ko_ref[...] = _rope_t128(dk_ref[...], c, s).astype(BF16)
        vo_ref[...] = dv_ref[...].astype(BF16)

    return _tok_call(body, name, S, min(S, 512),
                     [(dq, MIX, 0), (dk, 128, 0), (dv, 128, 0), (cos, 128, 0), (sin, 128, 0)], [],
                     [(MIX, BF16), (128, BF16), (128, BF16)])


def _swa_band(i, k_ref, v_ref):
    pstart = pl.multiple_of(jnp.maximum(i - 1, 0) * WINDOW, WINDOW)
    cstart = pl.multiple_of(i * WINDOW, WINDOW)
    kb = jnp.concatenate([k_ref[pl.ds(pstart, WINDOW), :], k_ref[pl.ds(cstart, WINDOW), :]], axis=0)
    vb = jnp.concatenate([v_ref[pl.ds(pstart, WINDOW), :], v_ref[pl.ds(cstart, WINDOW), :]], axis=0)
    qi = lax.broadcasted_iota(jnp.int32, (WINDOW, 2 * WINDOW), 0)
    sj = lax.broadcasted_iota(jnp.int32, (WINDOW, 2 * WINDOW), 1)
    mask = (sj > qi) & (sj <= qi + WINDOW) & ((i > 0) | (sj >= WINDOW))
    return kb, vb, mask, pstart, cstart


def _swa_probs(qs, kh, mask, sinks):
    logits = [jnp.where(mask, _dg(qh, kh, 1, 1), -1e30) for qh in qs]
    m = [jnp.maximum(jnp.max(l, axis=-1, keepdims=True), s) for l, s in zip(logits, sinks)]
    p = [jnp.exp(l - mm) for l, mm in zip(logits, m)]
    ps = [jnp.exp(s - mm) for s, mm in zip(sinks, m)]
    inv = [1.0 / (jnp.sum(pp, axis=-1, keepdims=True) + s) for pp, s in zip(p, ps)]
    return [pp * iv for pp, iv in zip(p, inv)], [s * iv for s, iv in zip(ps, inv)]


def _swa_fwd(q, k, v, sinks, name):
    S = q.shape[0]
    G = SWA_H // SWA_KV

    def body(i, q_ref, k_ref, v_ref, s_ref, o_ref):
        kb, vb, mask, _, _ = _swa_band(i, k_ref, v_ref)
        qv = q_ref[...]
        for kv in range(SWA_KV):
            ksl = slice(kv * SWA_HD, (kv + 1) * SWA_HD)
            heads = range(kv * G, (kv + 1) * G)
            pn, _ = _swa_probs([qv[:, h * SWA_HD:(h + 1) * SWA_HD] for h in heads], kb[:, ksl], mask,
                               [s_ref[0, h] for h in heads])
            outs = [_dg(p.astype(BF16), vb[:, ksl], 1, 0) for p in pn]
            for h, o in zip(heads, outs):
                o_ref[:, h * SWA_HD:(h + 1) * SWA_HD] = o.astype(BF16)

    return _tok_call(body, name, S, WINDOW, [(q, MIX, 0)], [k, v], [(MIX, BF16)], smem_in=[sinks])[0]


def _swa_bwd(q, k, v, sinks, dout, name):
    S = q.shape[0]

    def body(i, q_ref, do_ref, k_ref, v_ref, s_ref, dq_ref, dk_ref, dv_ref, ds_ref):
        kb, vb, mask, pstart, cstart = _swa_band(i, k_ref, v_ref)
        qv, dov = q_ref[...], do_ref[...]
        lane = lax.broadcasted_iota(jnp.int32, (1, 128), 1)
        dsink = jnp.zeros((1, 128), F32)
        dkb, dvb = [], []
        G = SWA_H // SWA_KV
        for kv in range(SWA_KV):
            ksl = slice(kv * SWA_HD, (kv + 1) * SWA_HD)
            heads = range(kv * G, (kv + 1) * G)
            qs = [qv[:, h * SWA_HD:(h + 1) * SWA_HD] for h in heads]
            dos = [dov[:, h * SWA_HD:(h + 1) * SWA_HD].astype(BF16) for h in heads]
            pn, psn = _swa_probs(qs, kb[:, ksl], mask, [s_ref[0, h] for h in heads])
            dp = [_dg(d, vb[:, ksl], 1, 1) for d in dos]
            delta = [jnp.sum(a * b, axis=-1, keepdims=True) for a, b in zip(dp, pn)]
            dsc = [(p * (a - d)).astype(BF16) for p, a, d in zip(pn, dp, delta)]
            dqs = [_dg(s, kb[:, ksl], 1, 0) for s in dsc]
            dks = [_dg(s, qh, 0, 0) for s, qh in zip(dsc, qs)]
            dvs = [_dg(p.astype(BF16), d, 0, 0) for p, d in zip(pn, dos)]
            for n_, h in enumerate(heads):
                dq_ref[:, h * SWA_HD:(h + 1) * SWA_HD] = dqs[n_]
                dsink = dsink + jnp.where(lane == h, -jnp.sum(psn[n_] * delta[n_], axis=0, keepdims=True), 0.0)
            dkb.append((dks[0] + dks[1]) + (dks[2] + dks[3]))
            dvb.append((dvs[0] + dvs[1]) + (dvs[2] + dvs[3]))
        dkb = jnp.concatenate(dkb, axis=1)
        dvb = jnp.concatenate(dvb, axis=1)

        @pl.when(i == 0)
        def _():
            dk_ref[...] = jnp.zeros_like(dk_ref)
            dv_ref[...] = jnp.zeros_like(dv_ref)

        dk_ref[pl.ds(pstart, WINDOW), :] += dkb[:WINDOW]
        dv_ref[pl.ds(pstart, WINDOW), :] += dvb[:WINDOW]
        dk_ref[pl.ds(cstart, WINDOW), :] += dkb[WINDOW:]
        dv_ref[pl.ds(cstart, WINDOW), :] += dvb[WINDOW:]
        _acc(ds_ref, dsink, i)

    return _tok_call(body, name, S, WINDOW, [(q, MIX, 0), (dout, MIX, 0)], [k, v], [(MIX, F32)],
                     [((S, 128), F32), ((S, 128), F32), ((1, 128), F32)], smem_in=[sinks])


def _shift_rows(xs, k):
    return xs if k == 0 else pltpu.roll(xs, k, 0)


def _dn_conv(x_ref, p_ref, w_ref, i):
    hr = p_ref.shape[0]
    halo = jnp.where(i > 0, p_ref[...].astype(F32), 0.0)
    xs = jnp.concatenate([halo, x_ref[...].astype(F32)], axis=0)
    sh = [_shift_rows(xs, DN_CONV - 1 - t)[hr:] for t in range(DN_CONV)]
    pre = sh[0] * w_ref[0:1, :]
    for t in range(1, DN_CONV):
        pre = pre + sh[t] * w_ref[t:t + 1, :]
    return pre, sh


def _dn_gates(sm, alog, dtb):
    lane = lax.broadcasted_iota(jnp.int32, sm.shape, 1)
    return jnp.where(lane < DN_H, _sigmoid(sm), -jnp.exp(alog) * _softplus(sm + dtb))


def _dn_pre_fwd(proj, conv_w, alog_l, dtb_l, name):
    S = proj.shape[0]
    scale = DN_HD ** -0.5

    def body(i, x_ref, sm_ref, p_ref, w_ref, al_ref, db_ref, q_ref, k_ref, v_ref, bg_ref):
        pre, _ = _dn_conv(x_ref, p_ref, w_ref, i)
        a = _silu(pre)
        for h in range(DN_H):
            sl = slice(h * DN_HD, (h + 1) * DN_HD)
            qh, kh = a[:, sl], a[:, MIX + h * DN_HD:MIX + (h + 1) * DN_HD]
            q_ref[:, sl] = qh * (lax.rsqrt(jnp.sum(qh * qh, axis=-1, keepdims=True) + EPS) * scale)
            k_ref[:, sl] = kh * lax.rsqrt(jnp.sum(kh * kh, axis=-1, keepdims=True) + EPS)
        v_ref[...] = a[:, 2 * MIX:]
        bg_ref[...] = _dn_gates(sm_ref[...].astype(F32), al_ref[...], db_ref[...])

    TB = min(S, 256)
    return _tok_call(body, name, S, TB, [(proj, 3 * MIX, C_QKV // (3 * MIX)), (proj, 128, C_SM // 128)],
                     [conv_w, alog_l, dtb_l], [(MIX, F32), (MIX, F32), (MIX, F32), (128, F32)],
                     prev_in=[(proj, 3 * MIX, C_QKV // (3 * MIX))])


def _dn_pre_bwd1(proj, conv_w, alog_l, dtb_l, dq, dk, dv, dbg, name):
    S = proj.shape[0]
    scale = DN_HD ** -0.5

    def body(i, x_ref, sm_ref, dq_ref, dk_ref, dv_ref, dbg_ref, p_ref, w_ref, al_ref, db_ref, dpre_ref, dsm_ref,
             dw_ref, dal_ref, ddb_ref):
        pre, sh = _dn_conv(x_ref, p_ref, w_ref, i)
        a = _silu(pre)
        da_parts = []
        for part, (g_ref, sc) in enumerate(((dq_ref, scale), (dk_ref, 1.0))):
            for h in range(DN_H):
                xh = a[:, part * MIX + h * DN_HD:part * MIX + (h + 1) * DN_HD]
                rs = lax.rsqrt(jnp.sum(xh * xh, axis=-1, keepdims=True) + EPS)
                y = xh * rs
                dy = g_ref[:, h * DN_HD:(h + 1) * DN_HD] * sc
                da_parts.append(rs * (dy - y * jnp.sum(dy * y, axis=-1, keepdims=True)))
        da_parts.append(dv_ref[...])
        dpre = jnp.concatenate(da_parts, axis=1) * _dsilu(pre)
        dpre_ref[...] = dpre
        dw = jnp.concatenate([jnp.sum(dpre * sh[t], axis=0, keepdims=True) for t in range(DN_CONV)], axis=0)
        _acc(dw_ref, dw, i)
        sm, al, db, dbg_v = sm_ref[...].astype(F32), al_ref[...], db_ref[...], dbg_ref[...]
        lane = lax.broadcasted_iota(jnp.int32, sm.shape, 1)
        sg = _sigmoid(sm)
        gneg = -jnp.exp(al)
        is_g = (lane >= DN_H) & (lane < 2 * DN_H)
        d_al = jnp.where(is_g, dbg_v * gneg * _sigmoid(sm + db), 0.0)
        dsm_ref[...] = jnp.where(lane < DN_H, dbg_v * sg * (1.0 - sg), d_al).astype(BF16)
        _acc(ddb_ref, jnp.sum(d_al, axis=0, keepdims=True), i)
        _acc(dal_ref, jnp.sum(jnp.where(is_g, dbg_v * gneg * _softplus(sm + db), 0.0), axis=0, keepdims=True), i)

    TB = min(S, 256)
    return _tok_call(body, name, S, TB,
                     [(proj, 3 * MIX, C_QKV // (3 * MIX)), (proj, 128, C_SM // 128), (dq, MIX, 0), (dk, MIX, 0),
                      (dv, MIX, 0), (dbg, 128, 0)], [conv_w, alog_l, dtb_l],
                     [(3 * MIX, F32), (128, BF16)], [((DN_CONV, 3 * MIX), F32), ((1, 128), F32), ((1, 128), F32)],
                     prev_in=[(proj, 3 * MIX, C_QKV // (3 * MIX))])


def _dn_pre_bwd2(dpre, conv_w, name):
    S = dpre.shape[0]
    TB = min(S, 256)
    nb = S // TB

    def body(i, d_ref, n_ref, w_ref, o_ref):
        halo = jnp.where(i < nb - 1, n_ref[...], 0.0)
        ds = jnp.concatenate([d_ref[...], halo], axis=0)
        out = ds[:TB] * w_ref[DN_CONV - 1:DN_CONV, :]
        for t in range(DN_CONV - 1):
            k = DN_CONV - 1 - t
            out = out + pltpu.roll(ds, TB + 8 - k, 0)[:TB] * w_ref[t:t + 1, :]
        o_ref[...] = out.astype(BF16)

    return _tok_call(body, name, S, TB, [(dpre, 3 * MIX, 0)], [conv_w], [(3 * MIX, BF16)],
                     next_in=[(dpre, 3 * MIX, 0)])[0]


def _dn_decay_terms(bgs, heads):
    C = DN_C
    ri = lax.broadcasted_iota(jnp.int32, (C, C), 0)
    ci = lax.broadcasted_iota(jnp.int32, (C, C), 1)
    tril, eye = ri >= ci, ri == ci
    beta = [b[:, h:h + 1] for b, h in zip(bgs, heads)]
    gcol = _dg_exact_lhs_many(tril, [jnp.broadcast_to(b[:, DN_H + h:DN_H + h + 1], (C, C))
                                     for b, h in zip(bgs, heads)], 1, 0)
    grow = [jnp.sum(jnp.where(eye, g, 0.0), axis=0, keepdims=True) for g in gcol]
    decay = [jnp.exp(jnp.where(tril, g - r, -1e30)) for g, r in zip(gcol, grow)]
    e_gc = [jnp.exp(g[:, 0:1]) for g in gcol]
    e_kd = [jnp.exp(g[C - 1:C, 0:1] - g[:, 0:1]) for g in gcol]
    cdec = [jnp.exp(g[C - 1:C, 0:1]) for g in gcol]
    return beta, decay, e_gc, e_kd, cdec


def _dn_nb(S):
    return 4 if S % (4 * DN_C) == 0 else 1


def _dn_prep_fwd(q, k, v, bg, name):
    S = q.shape[0]
    C, NB = DN_C, _dn_nb(S)
    TB = NB * C

    def kern(q_ref, k_ref, v_ref, bg_ref, t_ref, uw_ref, at_ref, qd_ref, kd_ref, dec_ref):
        lane = lax.broadcasted_iota(jnp.int32, (C, 128), 1)
        ri = lax.broadcasted_iota(jnp.int32, (C, C), 0)
        ci = lax.broadcasted_iota(jnp.int32, (C, C), 1)
        tril, eye = ri >= ci, ri == ci
        chains = [(cb, h) for cb in range(NB) for h in range(DN_H)]
        rows = lambda cb: slice(cb * C, (cb + 1) * C)
        head = lambda h: slice(h * DN_HD, (h + 1) * DN_HD)
        beta, decay, e_gc, e_kd, cdec = _dn_decay_terms([bg_ref[rows(cb), :] for cb, _ in chains],
                                                        [h for _, h in chains])
        qs = [q_ref[rows(cb), head(h)] for cb, h in chains]
        ks = [k_ref[rows(cb), head(h)] for cb, h in chains]
        kb = [kh * b for kh, b in zip(ks, beta)]
        x = [-jnp.where(ri > ci, _mm_nt(a, kh) * d, 0.0) for a, kh, d in zip(kb, ks, decay)]
        tm = [jnp.where(eye, 1.0, 0.0) + xi for xi in x]
        p = x
        for _ in range(5):
            p = _dg3_many(p, p, 1, 0)
            tm = [t + tp for t, tp in zip(tm, _dg3_many(tm, p, 1, 0))]
        rhs = [jnp.concatenate([v_ref[rows(cb), head(h)] * b, a * e], axis=1)
               for (cb, h), b, a, e in zip(chains, beta, kb, e_gc)]
        sol = _dg3_many(tm, rhs, 1, 0)
        attn = [_mm_nt(qh, kh) * d for qh, kh, d in zip(qs, ks, decay)]
        for n_, (cb, h) in enumerate(chains):
            rs, sl, hc = rows(cb), head(h), slice(h * C, (h + 1) * C)
            t_ref[rs, hc] = tm[n_]
            uw_ref[rs, sl] = sol[n_][:, :DN_HD]
            uw_ref[rs, MIX + h * DN_HD:MIX + (h + 1) * DN_HD] = sol[n_][:, DN_HD:]
            at_ref[rs, hc] = attn[n_]
            qd_ref[rs, sl] = (qs[n_] * e_gc[n_]).astype(BF16)
            kd_ref[rs, sl] = (ks[n_] * e_kd[n_]).astype(BF16)
        for cb in range(NB):
            dec = jnp.zeros((C, 128), F32)
            for h in range(DN_H):
                dec = dec + jnp.where(lane == h, cdec[cb * DN_H + h], 0.0)
            dec_ref[rows(cb), :] = dec

    tok = lambda w: pl.BlockSpec((TB, w), lambda i: (i, 0))
    return pl.pallas_call(
        kern, name=name, grid=(S // TB,), in_specs=[tok(MIX), tok(MIX), tok(MIX), tok(128)],
        out_specs=[tok(DN_H * C), tok(2 * MIX), tok(DN_H * C), tok(MIX), tok(MIX), tok(128)],
        out_shape=[jax.ShapeDtypeStruct((S, DN_H * C), F32), jax.ShapeDtypeStruct((S, 2 * MIX), F32),
                   jax.ShapeDtypeStruct((S, DN_H * C), F32), jax.ShapeDtypeStruct((S, MIX), BF16),
                   jax.ShapeDtypeStruct((S, MIX), BF16), jax.ShapeDtypeStruct((S, 128), F32)],
        compiler_params=_cparams(("parallel",)),
    )(q, k, v, bg)


def _dn_scan_fwd(uw, at, qd, kd, dec, name):
    S = uw.shape[0]
    C, NB = DN_C, _dn_nb(S)
    TB = NB * C
    SR = DN_H * DN_HD

    def kern(uw_ref, at_ref, qd_ref, kd_ref, dec_ref, o_ref, vn_ref, st_ref, state):
        @pl.when(pl.program_id(0) == 0)
        def _():
            state[...] = jnp.zeros_like(state)

        for cb in range(NB):
            rs = slice(cb * C, (cb + 1) * C)
            hs = range(DN_H)
            sls = [slice(h * DN_HD, (h + 1) * DN_HD) for h in hs]
            s_in = [state[sl, :] for sl in sls]
            ws = [_mm(uw_ref[rs, MIX + h * DN_HD:MIX + (h + 1) * DN_HD], s_in[h]) for h in hs]
            os_ = [_mm(qd_ref[rs, sls[h]], s_in[h]) for h in hs]
            vnew = [uw_ref[rs, sls[h]] - ws[h] for h in hs]
            oa = [_mm(at_ref[rs, h * C:(h + 1) * C], vnew[h]) for h in hs]
            kv = [_mm_tn(kd_ref[rs, sls[h]], vnew[h]) for h in hs]
            for h in hs:
                o_ref[rs, sls[h]] = os_[h] + oa[h]
                state[sls[h], :] = s_in[h] * dec_ref[cb * C:cb * C + 1, h:h + 1] + kv[h]
                st_ref[cb * SR + h * DN_HD:cb * SR + (h + 1) * DN_HD, :] = s_in[h]
                vn_ref[rs, sls[h]] = vnew[h]

    tok = lambda w: pl.BlockSpec((TB, w), lambda i: (i, 0))
    return pl.pallas_call(
        kern, name=name, grid=(S // TB,), in_specs=[tok(2 * MIX), tok(DN_H * C), tok(MIX), tok(MIX), tok(128)],
        out_specs=[tok(MIX), tok(MIX), pl.BlockSpec((NB * SR, DN_HD), lambda i: (i, 0))],
        out_shape=[jax.ShapeDtypeStruct((S, MIX), F32), jax.ShapeDtypeStruct((S, MIX), F32),
                   jax.ShapeDtypeStruct((S // C * SR, DN_HD), F32)],
        scratch_shapes=[pltpu.VMEM((SR, DN_HD), F32)],
        compiler_params=_cparams(("arbitrary",)),
    )(uw, at, qd, kd, dec)


def _dn_core_fwd(q, k, v, bg, name):
    tm, uw, at, qd, kd, dec = _dn_prep_fwd(q, k, v, bg, name + "_prep")
    o, vn, st = _dn_scan_fwd(uw, at, qd, kd, dec, name + "_scan")
    return o, dict(tm=tm, uw=uw, at=at, qd=qd, kd=kd, dec=dec, vn=vn, st=st)


def _dn_scan_bwd(sv, do, name):
    S = do.shape[0]
    C, NB = DN_C, _dn_nb(S)
    TB = NB * C
    SR = DN_H * DN_HD
    nb = S // TB

    def kern(do_ref, uw_ref, at_ref, qd_ref, kd_ref, dec_ref, vn_ref, st_ref, dvn_ref, dw_ref, dkd_ref, dc_ref, dstate):
        @pl.when(pl.program_id(0) == 0)
        def _():
            dstate[...] = jnp.zeros_like(dstate)

        lane = lax.broadcasted_iota(jnp.int32, (C, 128), 1)
        for cb in reversed(range(NB)):
            rs = slice(cb * C, (cb + 1) * C)
            dcrow = jnp.zeros((C, 128), F32)
            for h in range(DN_H):
                sl = slice(h * DN_HD, (h + 1) * DN_HD)
                doh, ds_o = do_ref[rs, sl], dstate[sl, :]
                s_in = st_ref[cb * SR + h * DN_HD:cb * SR + (h + 1) * DN_HD, :]
                d_vnew = _mm_tn(at_ref[rs, h * C:(h + 1) * C], doh) + _mm(kd_ref[rs, sl], ds_o)
                dvn_ref[rs, sl] = d_vnew
                dw_ref[rs, sl] = -_mm_nt(d_vnew, s_in)
                dkd_ref[rs, sl] = _mm_nt(vn_ref[rs, sl], ds_o)
                d_c = jnp.sum(jnp.sum(ds_o * s_in, axis=1, keepdims=True), axis=0, keepdims=True)
                dcrow = dcrow + jnp.where(lane == h, d_c, 0.0)
                dstate[sl, :] = (ds_o * dec_ref[cb * C:cb * C + 1, h:h + 1] + _mm_tn(qd_ref[rs, sl], doh)
                                 - _mm_tn(uw_ref[rs, MIX + h * DN_HD:MIX + (h + 1) * DN_HD], d_vnew))
            dc_ref[rs, :] = dcrow

    tok = lambda w: pl.BlockSpec((TB, w), lambda i: (nb - 1 - i, 0))
    return pl.pallas_call(
        kern, name=name, grid=(nb,),
        in_specs=[tok(MIX), tok(2 * MIX), tok(DN_H * C), tok(MIX), tok(MIX), tok(128), tok(MIX),
                  pl.BlockSpec((NB * SR, DN_HD), lambda i: (nb - 1 - i, 0))],
        out_specs=[tok(MIX), tok(MIX), tok(MIX), tok(128)],
        out_shape=[jax.ShapeDtypeStruct((S, MIX), F32)] * 3 + [jax.ShapeDtypeStruct((S, 128), F32)],
        scratch_shapes=[pltpu.VMEM((SR, DN_HD), F32)],
        compiler_params=_cparams(("arbitrary",)),
    )(do, sv["uw"], sv["at"], sv["qd"], sv["kd"], sv["dec"], sv["vn"], sv["st"])


def _dn_chunk_bwd(q, k, v, bg, sv, do, dvn, dw, dkd, dc, name):
    S = q.shape[0]
    C, NB = DN_C, _dn_nb(S)
    TB = NB * C
    SR = DN_H * DN_HD

    def kern(q_ref, k_ref, v_ref, bg_ref, t_ref, uw_ref, vn_ref, st_ref, do_ref, dvn_ref, dw_ref, dkd_ref, dc_ref,
             dq_ref, dk_ref, dv_ref, dbg_ref):
        lane = lax.broadcasted_iota(jnp.int32, (C, 128), 1)
        ri = lax.broadcasted_iota(jnp.int32, (C, C), 0)
        ci = lax.broadcasted_iota(jnp.int32, (C, C), 1)
        tril, eye, last = ri >= ci, ri == ci, ri[:, 0:1] == C - 1
        chains = [(cb, h) for cb in range(NB) for h in range(DN_H)]
        each = lambda f, *ls: [f(*a) for a in zip(*ls)]
        rsum = lambda t: jnp.sum(t, axis=-1, keepdims=True)
        rows = lambda cb: slice(cb * C, (cb + 1) * C)
        head = lambda h: slice(h * DN_HD, (h + 1) * DN_HD)
        tok = lambda ref: [ref[rows(cb), head(h)] for cb, h in chains]
        beta, decay, e_gc, e_kd, cdec = _dn_decay_terms([bg_ref[rows(cb), :] for cb, _ in chains],
                                                        [h for _, h in chains])
        qs, ks, vs, dos, vnew, d_kd = tok(q_ref), tok(k_ref), tok(v_ref), tok(do_ref), tok(vn_ref), tok(dkd_ref)
        s_in = [st_ref[cb * SR + h * DN_HD:cb * SR + (h + 1) * DN_HD, :] for cb, h in chains]
        d_c = [dc_ref[cb * C:cb * C + 1, h:h + 1] for cb, h in chains]
        kb = each(lambda a, b: a * b, ks, beta)
        kk = each(_mm_nt, kb, ks)
        attn = each(lambda a, b, d: _mm_nt(a, b) * d, qs, ks, decay)
        d_qd = each(_mm_nt, dos, s_in)
        d_attn = each(_mm_nt, dos, vnew)
        d_sol = [jnp.concatenate([dvn_ref[rows(cb), head(h)], dw_ref[rows(cb), head(h)]], axis=1) for cb, h in chains]
        sol = [jnp.concatenate([uw_ref[rows(cb), head(h)], uw_ref[rows(cb), MIX + h * DN_HD:MIX + (h + 1) * DN_HD]],
                               axis=1) for cb, h in chains]
        d_rhs = _dg3_many([t_ref[rows(cb), h * C:(h + 1) * C] for cb, h in chains], d_sol, 0, 0)
        d_a = _dg3_many(d_rhs, sol, 1, 1)
        d_kk = each(lambda a, d: jnp.where(ri > ci, -a, 0.0) * d, d_a, decay)
        d_qk = each(lambda a, d: a * d, d_attn, decay)
        dm = each(lambda a, b, c_, d: a * b + c_ * d, d_kk, kk, d_attn, attn)
        d_vb = [t[:, :DN_HD] for t in d_rhs]
        dz = [t[:, DN_HD:] for t in d_rhs]
        d_kb = each(lambda z, e, a, kh: z * e + _mm(a, kh), dz, e_gc, d_kk, ks)
        d_k = each(lambda a, b, c_, q: _mm_tn(a, b) + _mm_tn(c_, q), d_kk, kb, d_qk, qs)
        d_q = each(lambda a, kh, b, e: _mm(a, kh) + b * e, d_qk, ks, d_qd, e_gc)
        t_kd = each(lambda a, kh, e: rsum(a * kh * e), d_kd, ks, e_kd)
        d_gl = each(lambda t, c_, cd: jnp.sum(t, axis=0, keepdims=True) + c_ * cd, t_kd, d_c, cdec)
        d_gc = each(lambda z, a, e, m, b, q, t, gl:
                    rsum(z * a) * e + rsum(m) - rsum(jnp.where(eye, jnp.sum(m, axis=0, keepdims=True), 0.0))
                    + rsum(b * q) * e - t + jnp.where(last, gl, 0.0),
                    dz, kb, e_gc, dm, d_qd, qs, t_kd, d_gl)
        d_g = _dg_exact_lhs_many(ri <= ci, [jnp.broadcast_to(t, (C, 128)) for t in d_gc], 1, 0)
        d_beta = each(lambda a, v_, b, kh: rsum(a * v_) + rsum(b * kh), d_vb, vs, d_kb, ks)
        for n_, (cb, h) in enumerate(chains):
            dq_ref[rows(cb), head(h)] = d_q[n_]
            dk_ref[rows(cb), head(h)] = d_k[n_] + d_kd[n_] * e_kd[n_] + d_kb[n_] * beta[n_]
            dv_ref[rows(cb), head(h)] = d_vb[n_] * beta[n_]
        for cb in range(NB):
            dbg = jnp.zeros((C, 128), F32)
            for h in range(DN_H):
                n_ = cb * DN_H + h
                dbg = dbg + jnp.where(lane == h, d_beta[n_], 0.0) + jnp.where(lane == DN_H + h, d_g[n_], 0.0)
            dbg_ref[rows(cb), :] = dbg

    tok = lambda w: pl.BlockSpec((TB, w), lambda i: (i, 0))
    return pl.pallas_call(
        kern, name=name, grid=(S // TB,),
        in_specs=[tok(MIX), tok(MIX), tok(MIX), tok(128), tok(DN_H * C), tok(2 * MIX), tok(MIX),
                  pl.BlockSpec((NB * SR, DN_HD), lambda i: (i, 0)), tok(MIX), tok(MIX), tok(MIX), tok(MIX), tok(128)],
        out_specs=[tok(MIX), tok(MIX), tok(MIX), tok(128)],
        out_shape=[jax.ShapeDtypeStruct((S, MIX), F32)] * 3 + [jax.ShapeDtypeStruct((S, 128), F32)],
        compiler_params=_cparams(("parallel",)),
    )(q, k, v, bg, sv["tm"], sv["uw"], sv["vn"], sv["st"], do, dvn, dw, dkd, dc)


def _dn_core_bwd(q, k, v, bg, sv, do, name):
    dvn, dw, dkd, dc = _dn_scan_bwd(sv, do, name + "_scan")
    return _dn_chunk_bwd(q, k, v, bg, sv, do, dvn, dw, dkd, dc, name + "_chunk")


def _dn_post_fwd(o, proj, ng, name):
    S = o.shape[0]

    def body(i, o_ref, z_ref, g_ref, out_ref):
        gv = g_ref[...]
        for h in range(DN_H):
            sl = slice(h * DN_HD, (h + 1) * DN_HD)
            oh = o_ref[:, sl]
            r = lax.rsqrt(jnp.mean(oh * oh, axis=-1, keepdims=True) + EPS)
            out_ref[:, sl] = (oh * r * gv * _silu(z_ref[:, sl].astype(F32))).astype(BF16)

    return _tok_call(body, name, S, min(S, 512), [(o, MIX, 0), (proj, MIX, C_ZC // MIX)], [ng], [(MIX, BF16)])[0]


def _dn_post_bwd(o, proj, ng, dout, name):
    S = o.shape[0]

    def body(i, o_ref, z_ref, do_ref, g_ref, dov_ref, dz_ref, dg_ref):
        gv = g_ref[...]
        dg = jnp.zeros((1, DN_HD), F32)
        for h in range(DN_H):
            sl = slice(h * DN_HD, (h + 1) * DN_HD)
            oh, zh, dh = o_ref[:, sl], z_ref[:, sl].astype(F32), do_ref[:, sl].astype(F32)
            r = lax.rsqrt(jnp.mean(oh * oh, axis=-1, keepdims=True) + EPS)
            dz_ref[:, sl] = (dh * oh * r * gv * _dsilu(zh)).astype(BF16)
            dx, dgh = _rms_bwd_vals(oh, gv, dh * _silu(zh))
            dov_ref[:, sl] = dx
            dg = dg + dgh
        _acc(dg_ref, dg, i)

    return _tok_call(body, name, S, min(S, 512), [(o, MIX, 0), (proj, MIX, C_ZC // MIX), (dout, MIX, 0)], [ng],
                     [(MIX, F32), (MIX, BF16)], [((1, DN_HD), F32)])


def _layer_params(w, big, l):
    lane = jnp.arange(128)
    is_g = (lane >= DN_H) & (lane < 2 * DN_H)
    spread = lambda t: jnp.where(is_g, jnp.tile(t, 128 // DN_H), 0.0).reshape(1, 128)
    tril = jnp.tril(jnp.ones((SGU_T, SGU_T), bool))
    return dict(
        win=big["w_in"], wb=big["w_branch"], wout=big["w_out"], wgu=big["w_gate_up"], wdown=big["w_down"],
        conv=w["dn_conv_w"][l], attn_norm=w["attn_norm"][l].reshape(1, -1), ffn_norm=w["ffn_norm"][l].reshape(1, -1),
        lg=w["sgu_ln_g"][l].reshape(1, -1), lb=w["sgu_ln_b"][l].reshape(1, -1),
        wc=jnp.where(tril, w["sgu_w"][l], 0.0).reshape(SGU_G * SGU_T, SGU_T), bst=w["sgu_b"][l].T,
        sinks=w["attn_sinks"][l].reshape(1, -1), alog=spread(w["dn_a_log"][l]), dtb=spread(w["dn_dt_bias"][l]),
        ng=w["dn_norm"][l].reshape(1, -1))


def _layer_fwd(x, p, cos, sin, l):
    n = lambda s: f"l{l}_{s}"
    h = _rms_fwd(x, p["attn_norm"], n("rms1"))
    proj = _matmul(h, p["win"], out_dtype=BF16, name=n("mm_in"))
    out_a = _sgu_fwd(proj, p["lg"], p["lb"], p["wc"], p["bst"], n("sgu_fwd"))
    qr, kr, vr = _rope_fwd(proj, cos, sin, n("rope_fwd"))
    out_b = _swa_fwd(qr, kr, vr, p["sinks"], n("swa_fwd"))
    q, k, v, bg = _dn_pre_fwd(proj, p["conv"], p["alog"], p["dtb"], n("dn_pre_fwd"))
    o, dn = _dn_core_fwd(q, k, v, bg, n("dn_core_fwd"))
    out_c = _dn_post_fwd(o, proj, p["ng"], n("dn_post_fwd"))
    outs = (out_a, out_b, out_c)
    bds = [_matmul(outs[j], p["wb"][j], out_dtype=BF16, name=n(f"mm_branch{j}")) for j in range(3)]
    merged = _merge_fwd(proj, bds, n("merge_fwd"))
    x1 = _matmul(merged, p["wout"], add=x, name=n("mm_out"))
    h2 = _rms_fwd(x1, p["ffn_norm"], n("rms2"))
    gu = _matmul(h2, p["wgu"], out_dtype=BF16, name=n("mm_gu"))
    act = _swiglu_fwd(gu, n("swiglu_fwd"))
    x2 = _matmul(act, p["wdown"], add=x1, name=n("mm_down"))
    saved = dict(x=x, h=h, proj=proj, outs=outs, qr=qr, kr=kr, vr=vr, q=q, k=k, v=v, bg=bg, o=o, dn=dn, bds=bds,
                 merged=merged, x1=x1, h2=h2, gu=gu, act=act)
    return x2, saved


def _layer_bwd(dx2, s, p, cos, sin, l):
    n = lambda t: f"l{l}_{t}"
    proj = s["proj"]
    g = {}
    g["w_down"] = _matmul(s["act"], dx2, ta=True, out_dtype=BF16, name=n("wg_down"))
    dact = _matmul(dx2, p["wdown"], tb=True, out_dtype=BF16, name=n("dg_down"))
    dgu = _swiglu_bwd(s["gu"], dact, n("swiglu_bwd"))
    g["w_gate_up"] = _matmul(s["h2"], dgu, ta=True, out_dtype=BF16, name=n("wg_gu"))
    dh2 = _matmul(dgu, p["wgu"], tb=True, name=n("dg_gu"))
    dx1, g["ffn_norm"] = _rms_bwd_add(s["x1"], p["ffn_norm"], dh2, dx2, n("rms2_bwd"))
    g["w_out"] = _matmul(s["merged"], dx1, ta=True, out_dtype=BF16, name=n("wg_out"))
    dm = _matmul(dx1, p["wout"], tb=True, name=n("dg_out"))
    dbd0, dbd1, dbd2, dgp = _merge_bwd(proj, s["bds"], dm, n("merge_bwd"))
    dbds = (dbd0, dbd1, dbd2)
    g["w_branch"] = jnp.stack([_matmul(s["outs"][j], dbds[j], ta=True, out_dtype=BF16, name=n(f"wg_branch{j}"))
                               for j in range(3)])
    douts = [_matmul(dbds[j], p["wb"][j], tb=True, name=n(f"dg_branch{j}")) for j in range(3)]
    dua, dva, g["sgu_ln_g"], g["sgu_ln_b"], dwc, dbs = _sgu_bwd(proj, p["lg"], p["lb"], p["wc"], p["bst"], douts[0],
                                                                n("sgu_bwd"))
    g["sgu_w"] = dwc.reshape(SGU_G, SGU_T, SGU_T)
    g["sgu_b"] = dbs.T
    dqr, dkr, dvr, dsink = _swa_bwd(s["qr"], s["kr"], s["vr"], p["sinks"], douts[1], n("swa_bwd"))
    g["attn_sinks"] = dsink[0, :SWA_H]
    dqb, dkb, dvb = _rope_bwd(dqr, dkr, dvr, cos, sin, n("rope_bwd"))
    do, dz, dng = _dn_post_bwd(s["o"], proj, p["ng"], douts[2], n("dn_post_bwd"))
    g["dn_norm"] = dng[0]
    dq, dk, dv, dbg = _dn_core_bwd(s["q"], s["k"], s["v"], s["bg"], s["dn"], do, n("dn_core_bwd"))
    dpre, dsm, g["dn_conv_w"], dal, ddb = _dn_pre_bwd1(proj, p["conv"], p["alog"], p["dtb"], dq, dk, dv, dbg,
                                                       n("dn_pre_bwd1"))
    g["dn_a_log"] = dal[0, DN_H:2 * DN_H]
    g["dn_dt_bias"] = ddb[0, DN_H:2 * DN_H]
    dqkv = _dn_pre_bwd2(dpre, p["conv"], n("dn_pre_bwd2"))
    dproj = jnp.concatenate([dgp, dqkv, dua, dva, dqb, dz, dkb, dvb, dsm], axis=1)
    g["w_in"] = _matmul(s["h"], dproj, ta=True, out_dtype=BF16, name=n("wg_in"))
    dh = _matmul(dproj, p["win"], tb=True, name=n("dg_in"))
    dx, g["attn_norm"] = _rms_bwd_add(s["x"], p["attn_norm"], dh, dx1, n("rms1_bwd"))
    g["attn_norm"], g["ffn_norm"] = g["attn_norm"][0], g["ffn_norm"][0]
    g["sgu_ln_g"], g["sgu_ln_b"] = g["sgu_ln_g"][0], g["sgu_ln_b"][0]
    return dx, g


def _local_step(x, positions, target, w, big_of_layer):
    cos, sin = _rope_tables(positions)
    params, saves, xs = [], [], x
    for l in range(DEPTH):
        params.append(_layer_params(w, big_of_layer(l, xs), l))
        xs, sv = _layer_fwd(xs, params[l], cos, sin, l)
        saves.append(sv)
    dx, loss_row, dgf = _final_loss(xs, w["final_norm"].reshape(1, -1), target)
    grads = [None] * DEPTH
    for l in reversed(range(DEPTH)):
        dx, grads[l] = _layer_bwd(dx, saves[l], params[l], cos, sin, l)
    stacked = {k: jnp.stack([grads[l][k] for l in range(DEPTH)]) for k in grads[0]}
    stacked["final_norm"] = dgf[0]
    return loss_row[0, 0], dx, stacked


MESH = pl.DeviceIdType.MESH
HBM_SPEC = pl.BlockSpec(memory_space=pltpu.HBM)
VMEM_SPEC = pl.BlockSpec(memory_space=pltpu.VMEM)
N_CHIPS = 4
FLIPS = tuple((fx, fy, fc) for fx in (0, 1) for fy in (0, 1) for fc in (0, 1))[1:]
BIG = ("w_in", "w_branch", "w_out", "w_gate_up", "w_down")
BIG_SPEC = {
    "w_in": dict(rows=1024, cols=1792, axis=1, keep=1730, down=8, up=4),
    "w_branch": dict(rows=1536, cols=256, axis=1, keep=256, down=2, up=1),
    "w_out": dict(rows=256, cols=1024, axis=0, keep=1024, down=1, up=1),
    "w_gate_up": dict(rows=1024, cols=1408, axis=1, keep=1408, down=8, up=4),
    "w_down": dict(rows=704, cols=1024, axis=0, keep=1024, down=4, up=2),
}
CONV_ROWS, CONV_COLS = DEPTH * DN_CONV, 3 * MIX // N_CHIPS


def _full_shape(k):
    sp = BIG_SPEC[k]
    return (sp["rows"], N_CHIPS * sp["cols"]) if sp["axis"] == 1 else (N_CHIPS * sp["rows"], sp["cols"])


def _chip_block(ref, k, s, layer=None):
    sp = BIG_SPEC[k]
    if sp["axis"] == 1:
        idx = (slice(None), pl.ds(pl.multiple_of(s * sp["cols"], 128), sp["cols"]))
    else:
        idx = (pl.ds(pl.multiple_of(s * sp["rows"], 16), sp["rows"]), slice(None))
    return ref.at[idx] if layer is None else ref.at[(layer,) + idx]


def _me():
    return lax.axis_index("x"), lax.axis_index("y"), lax.axis_index("c")


def _peer(x, y, c, flip):
    fx, fy, fc = flip
    return (1 - x if fx else x, 1 - y if fy else y, 1 - c if fc else c)


class _Copies:
    def __init__(self, send_sems, recv_sems):
        self.send_sems, self.recv_sems, self.k, self.sent, self.landing = send_sems, recv_sems, 0, [], []

    def _copy(self, k, src, dst, to):
        return pltpu.make_async_remote_copy(src_ref=src, dst_ref=dst, send_sem=self.send_sems.at[k],
                                            recv_sem=self.recv_sems.at[k], device_id=to, device_id_type=MESH)

    def send(self, src, dst, to, lands):
        k = self.k
        self.k += 1
        cp = self._copy(k, src, dst, to)
        cp.start()
        self.sent.append(cp)
        self.landing.append(self._copy(k, lands, lands, to))
        return k

    def wait_landed(self, k):
        self.landing[k].wait_recv()

    def finish(self, landed=()):
        for k, cp in enumerate(self.landing):
            if k not in landed:
                cp.wait_recv()
        for cp in self.sent:
            cp.wait_send()


def _place_shard(shard, k, chip, layer, name):
    sp = BIG_SPEC[k]
    rows, cols, keep = sp["rows"], sp["cols"], sp["keep"]
    tr = _pick(rows, (256, 64))
    nb = rows // tr
    if sp["axis"] == 1:
        out_spec = pl.BlockSpec((tr, cols), lambda i, ch: (i, ch[0]))
    else:
        out_spec = pl.BlockSpec((tr, cols), lambda i, ch: (ch[0] * nb + i, 0))

    def kern(ch_ref, x_ref, o_ref):
        v = x_ref[0].astype(BF16)
        if keep == cols:
            o_ref[...] = v
        else:
            o_ref[:, :keep] = v
            o_ref[:, keep:] = jnp.zeros((tr, cols - keep), BF16)

    return pl.pallas_call(
        kern, name=name, out_shape=jax.ShapeDtypeStruct(_full_shape(k), BF16),
        grid_spec=pltpu.PrefetchScalarGridSpec(
            num_scalar_prefetch=1, grid=(nb,),
            in_specs=[pl.BlockSpec((1, tr, keep), lambda i, ch: (layer, i, 0))], out_specs=out_spec),
        compiler_params=_cparams(("parallel",)),
    )(chip, shard)


def _half_block(ref, k, s, half):
    sp = BIG_SPEC[k]
    hr = sp["rows"] // 2
    if sp["axis"] == 1:
        return ref.at[pl.ds(pl.multiple_of(half * hr, 16), hr), pl.ds(pl.multiple_of(s * sp["cols"], 128), sp["cols"])]
    return ref.at[pl.ds(pl.multiple_of(s * sp["rows"] + half * hr, 16), hr), :]


def _other_chips(x, y):
    return [(1 - x, y), (x, 1 - y), (1 - x, 1 - y)]


def _gather_layer(placed, conv):
    n = len(BIG)
    n_sem = 6 * n + 3

    def body(*refs):
        conv_ref = refs[n]
        out = dict(zip(BIG, refs[n + 1:2 * n + 1]))
        conv_out, send_sems, recv_sems, local_sem = refs[2 * n + 1:]
        x, y, c = _me()
        me = 2 * x + y
        chips = _other_chips(x, y)
        net = _Copies(send_sems, recv_sems)

        def conv_block(s):
            return conv_out.at[:, pl.ds(pl.multiple_of(s * CONV_COLS, 128), CONV_COLS)]

        local = pltpu.make_async_copy(conv_ref, conv_block(me), local_sem)
        local.start()
        first = {}
        for k in BIG:
            for j, (px, py) in enumerate(chips):
                first[k, j] = net.send(_half_block(out[k], k, me, c), _half_block(out[k], k, me, c), (px, py, c),
                                       _half_block(out[k], k, 2 * px + py, c))
        for px, py in chips:
            net.send(conv_ref, conv_block(me), (px, py, c), conv_block(2 * px + py))
        for k in BIG:
            for j, (px, py) in enumerate(chips):
                net.wait_landed(first[k, j])
                net.send(_half_block(out[k], k, 2 * px + py, c), _half_block(out[k], k, 2 * px + py, c), (x, y, 1 - c),
                         _half_block(out[k], k, 2 * px + py, 1 - c))
        net.finish(landed=set(first.values()))
        local.wait()

    out_shape = [jax.ShapeDtypeStruct(_full_shape(k), BF16) for k in BIG]
    out_shape.append(jax.ShapeDtypeStruct((CONV_ROWS, N_CHIPS * CONV_COLS), F32))
    outs = pl.pallas_call(
        body, name="gather_layer", out_shape=out_shape, in_specs=[HBM_SPEC] * (n + 1), out_specs=[HBM_SPEC] * (n + 1),
        input_output_aliases={i: i for i in range(n)},
        scratch_shapes=[pltpu.SemaphoreType.DMA((n_sem,)), pltpu.SemaphoreType.DMA((n_sem,)), pltpu.SemaphoreType.DMA],
    )(*[placed[k] for k in BIG], conv)
    return dict(zip(BIG, outs[:n])), outs[n]


SEM_SPEC = pl.BlockSpec(memory_space=pltpu.SEMAPHORE)
N_BEHIND = 3 * len(BIG)


def _behind_copies(arrs, send_sems, recv_sems):
    x, y, c = _me()
    copies = []
    for i, k in enumerate(BIG):
        for j, (px, py) in enumerate(_other_chips(x, y)):
            copies.append(pltpu.make_async_remote_copy(
                src_ref=_half_block(arrs[k], k, 2 * x + y, c), dst_ref=_half_block(arrs[k], k, 2 * x + y, c),
                send_sem=send_sems.at[3 * i + j], recv_sem=recv_sems.at[3 * i + j], device_id=(px, py, c),
                device_id_type=MESH))
    return copies


def _gather_start(placed):
    n = len(BIG)

    def body(*refs):
        arrs = dict(zip(BIG, refs[n + 2:2 * n + 2]))
        send_sems, recv_sems = refs[n], refs[n + 1]
        for cp in _behind_copies(arrs, send_sems, recv_sems):
            cp.start()
        refs[2 * n + 2][...] = jnp.zeros((8, 128), F32)

    outs = pl.pallas_call(
        body, name="gather_start",
        out_shape=(pltpu.SemaphoreType.DMA((N_BEHIND,)), pltpu.SemaphoreType.DMA((N_BEHIND,)),
                   *[pltpu.HBM(_full_shape(k), BF16) for k in BIG], jax.ShapeDtypeStruct((8, 128), F32)),
        in_specs=[HBM_SPEC] * n, out_specs=(SEM_SPEC, SEM_SPEC, *[HBM_SPEC] * n, VMEM_SPEC),
        input_output_aliases={i: i + 2 for i in range(n)},
        compiler_params=pltpu.CompilerParams(has_side_effects=pltpu.SideEffectType.DATAFLOW_SIDE_EFFECTING),
    )(*[pltpu.with_memory_space_constraint(placed[k], pltpu.HBM) for k in BIG])
    return outs[0], outs[1], dict(zip(BIG, outs[2:n + 2])), outs[n + 2]


def _gather_wait(send_sems, recv_sems, inflight, after):
    n = len(BIG)

    def body(*refs):
        arrs = dict(zip(BIG, refs[:n]))
        for cp in _behind_copies(arrs, refs[n], refs[n + 1]):
            cp.wait_send()
            cp.wait_recv()

    outs = pl.pallas_call(
        body, name="gather_wait", out_shape=tuple(pltpu.HBM(_full_shape(k), BF16) for k in BIG),
        in_specs=[HBM_SPEC] * n + [SEM_SPEC, SEM_SPEC, pl.BlockSpec(memory_space=pl.ANY)], out_specs=(HBM_SPEC,) * n,
        input_output_aliases={i: i for i in range(n)},
        compiler_params=pltpu.CompilerParams(has_side_effects=pltpu.SideEffectType.DATAFLOW_SIDE_EFFECTING),
    )(*[inflight[k] for k in BIG], send_sems, recv_sems, after)
    return dict(zip(BIG, outs))


def _gather_finish(arrs):
    n = len(BIG)

    def body(*refs):
        out = dict(zip(BIG, refs[n:2 * n]))
        send_sems, recv_sems = refs[2 * n:]
        x, y, c = _me()
        net = _Copies(send_sems, recv_sems)
        for k in BIG:
            for px, py in _other_chips(x, y):
                net.send(_half_block(out[k], k, 2 * px + py, c), _half_block(out[k], k, 2 * px + py, c), (x, y, 1 - c),
                         _half_block(out[k], k, 2 * px + py, 1 - c))
        net.finish()

    outs = pl.pallas_call(
        body, name="gather_finish", out_shape=[jax.ShapeDtypeStruct(_full_shape(k), BF16) for k in BIG],
        in_specs=[HBM_SPEC] * n, out_specs=[HBM_SPEC] * n, input_output_aliases={i: i for i in range(n)},
        scratch_shapes=[pltpu.SemaphoreType.DMA((N_BEHIND,)), pltpu.SemaphoreType.DMA((N_BEHIND,))],
    )(*[arrs[k] for k in BIG])
    return dict(zip(BIG, outs))


def _row_chunks(ref, rows, n, layer=None):
    step = rows // n
    sl = [pl.ds(i * step, step) for i in range(n)]
    return [ref.at[s, :] if layer is None else ref.at[layer, s, :] for s in sl]


def _grads_to_sibling(grads):
    n = len(BIG)
    n_sem = sum(BIG_SPEC[k]["down"] for k in BIG)

    def body(*refs):
        g = dict(zip(BIG, refs[:n]))
        out = dict(zip(BIG, refs[n:2 * n]))
        send_sems, recv_sems = refs[2 * n:]
        x, y, c = _me()
        net = _Copies(send_sems, recv_sems)
        for k in BIG:
            rows, nch = _full_shape(k)[0], BIG_SPEC[k]["down"]
            for s, d in zip(_row_chunks(g[k], rows, nch, 1 - c), _row_chunks(out[k], rows, nch)):
                net.send(s, d, (x, y, 1 - c), d)
        net.finish()

    outs = pl.pallas_call(
        body, name="grads_to_sibling", out_shape=[jax.ShapeDtypeStruct(_full_shape(k), BF16) for k in BIG],
        in_specs=[HBM_SPEC] * n, out_specs=[HBM_SPEC] * n,
        scratch_shapes=[pltpu.SemaphoreType.DMA((n_sem,)), pltpu.SemaphoreType.DMA((n_sem,))],
    )(*[grads[k] for k in BIG])
    return dict(zip(BIG, outs))


def _add_layer(g2, other, layer, name):
    _, rows, cols = g2.shape
    tr = _pick(rows, (256, 128))

    def kern(l_ref, a_ref, b_ref, o_ref):
        o_ref[...] = (a_ref[0].astype(F32) + b_ref[...].astype(F32)).astype(BF16)

    return pl.pallas_call(
        kern, name=name, out_shape=jax.ShapeDtypeStruct((rows, cols), BF16),
        grid_spec=pltpu.PrefetchScalarGridSpec(
            num_scalar_prefetch=1, grid=(rows // tr,),
            in_specs=[pl.BlockSpec((1, tr, cols), lambda i, l: (l[0], i, 0)), pl.BlockSpec((tr, cols), lambda i, l: (i, 0))],
            out_specs=pl.BlockSpec((tr, cols), lambda i, l: (i, 0))),
        compiler_params=_cparams(("parallel",)),
    )(layer, g2, other)


def _scatter_chip_sums(sums):
    n = len(BIG)

    def body(*refs):
        src = dict(zip(BIG, refs[:n]))
        out = dict(zip(BIG, refs[n:2 * n]))
        send_sems, recv_sems = refs[2 * n:]
        x, y, c = _me()
        net = _Copies(send_sems, recv_sems)
        for k in BIG:
            for j, (px, py) in enumerate([(1 - x, y), (x, 1 - y), (1 - x, 1 - y)]):
                net.send(_chip_block(src[k], k, 2 * px + py), out[k].at[j], (px, py, c), out[k].at[j])
        net.finish()

    outs = pl.pallas_call(
        body, name="scatter_chip_sums",
        out_shape=[jax.ShapeDtypeStruct((N_CHIPS - 1, BIG_SPEC[k]["rows"], BIG_SPEC[k]["cols"]), BF16) for k in BIG],
        in_specs=[HBM_SPEC] * n, out_specs=[HBM_SPEC] * n,
        scratch_shapes=[pltpu.SemaphoreType.DMA((3 * n,)), pltpu.SemaphoreType.DMA((3 * n,))],
    )(*[sums[k] for k in BIG])
    return dict(zip(BIG, outs))


def _sum_chips(parts, own, k, where, name):
    sp = BIG_SPEC[k]
    rows, cols, keep = sp["rows"], sp["cols"], sp["keep"]
    tr = _pick(rows, (256, 64))
    nb = rows // tr
    if sp["axis"] == 1:
        own_spec = pl.BlockSpec((tr, cols), lambda i, w: (i, w[0]))
    else:
        own_spec = pl.BlockSpec((tr, cols), lambda i, w: (w[0] * nb + i, 0))

    def kern(w_ref, p_ref, own_ref, o_ref):
        tot = own_ref[...].astype(F32)
        for j in range(N_CHIPS - 1):
            tot = tot + p_ref[j].astype(F32)
        o_ref[0] = tot[:, :keep]

    return pl.pallas_call(
        kern, name=name, out_shape=jax.ShapeDtypeStruct((DEPTH, rows, keep), F32),
        grid_spec=pltpu.PrefetchScalarGridSpec(
            num_scalar_prefetch=1, grid=(nb,),
            in_specs=[pl.BlockSpec((N_CHIPS - 1, tr, cols), lambda i, w: (0, i, 0)), own_spec],
            out_specs=pl.BlockSpec((1, tr, keep), lambda i, w: (w[1], i, 0))),
        compiler_params=_cparams(("parallel",)),
    )(where, parts, own)


def _exchange_layers(red):
    n = len(BIG)
    n_sem = sum(BIG_SPEC[k]["up"] for k in BIG)

    def body(*refs):
        out = dict(zip(BIG, refs[n:2 * n]))
        send_sems, recv_sems = refs[2 * n:]
        x, y, c = _me()
        net = _Copies(send_sems, recv_sems)
        for k in BIG:
            rows, nch = BIG_SPEC[k]["rows"], BIG_SPEC[k]["up"]
            for mine, theirs in zip(_row_chunks(out[k], rows, nch, c), _row_chunks(out[k], rows, nch, 1 - c)):
                net.send(mine, mine, (x, y, 1 - c), theirs)
        net.finish()

    outs = pl.pallas_call(
        body, name="exchange_layers",
        out_shape=[jax.ShapeDtypeStruct((DEPTH, BIG_SPEC[k]["rows"], BIG_SPEC[k]["keep"]), F32) for k in BIG],
        in_specs=[HBM_SPEC] * n, out_specs=[HBM_SPEC] * n, input_output_aliases={i: i for i in range(n)},
        scratch_shapes=[pltpu.SemaphoreType.DMA((n_sem,)), pltpu.SemaphoreType.DMA((n_sem,))],
    )(*[red[k] for k in BIG])
    return dict(zip(BIG, outs))


def _adam_vals(g, w, m, v):
    m2 = ADAM_B1 * m + (1.0 - ADAM_B1) * g
    v2 = ADAM_B2 * v + (1.0 - ADAM_B2) * (g * g)
    m_hat = m2 / (1.0 - ADAM_B1 ** ADAM_STEP)
    v_hat = v2 / (1.0 - ADAM_B2 ** ADAM_STEP)
    return -ADAM_LR * (m_hat / (jnp.sqrt(v_hat) + ADAM_EPS) + ADAM_WD * w), m2, v2


def _allreduce_small_adam(groups):
    ng = len(groups)

    def body(*refs):
        ins = [refs[4 * i:4 * i + 4] for i in range(ng)]
        outs = [refs[4 * ng + 4 * i:4 * ng + 4 * i + 4] for i in range(ng)]
        bufs = refs[8 * ng:9 * ng]
        send_sems, recv_sems = refs[9 * ng:]
        x, y, c = _me()
        me = 4 * x + 2 * y + c
        net = _Copies(send_sems, recv_sems)
        for (g_ref, _, _, _), buf in zip(ins, bufs):
            buf[me] = g_ref[...]
            for f in FLIPS:
                px, py, pc = _peer(x, y, c, f)
                net.send(g_ref, buf.at[me], (px, py, pc), buf.at[4 * px + 2 * py + pc])
        net.finish()
        for (_, w_ref, m_ref, v_ref), (gs_ref, d_ref, nm_ref, nv_ref), buf in zip(ins, outs, bufs):
            tot = buf[0]
            for d in range(1, 8):
                tot = tot + buf[d]
            gs_ref[...] = tot
            d_ref[...], nm_ref[...], nv_ref[...] = _adam_vals(tot, w_ref[...], m_ref[...], v_ref[...])

    shapes = [jax.ShapeDtypeStruct(g[0].shape, F32) for g in groups for _ in range(4)]
    outs = pl.pallas_call(
        body, name="allreduce_small", out_shape=shapes, in_specs=[VMEM_SPEC] * (4 * ng), out_specs=[VMEM_SPEC] * (4 * ng),
        scratch_shapes=[pltpu.VMEM((8,) + g[0].shape, F32) for g in groups]
        + [pltpu.SemaphoreType.DMA((7 * ng,)), pltpu.SemaphoreType.DMA((7 * ng,))],
        compiler_params=pltpu.CompilerParams(vmem_limit_bytes=VMEM_LIMIT),
    )(*[t for g in groups for t in g])
    return [outs[4 * i:4 * i + 4] for i in range(ng)]


def _adam(g, w, m, v, name):
    shape = w.shape
    lead, rows, cols = math.prod(shape[:-2]), shape[-2], shape[-1]
    tr = _pick(rows, (256, 352, 64, 8, rows))
    spec = pl.BlockSpec((1, tr, cols), lambda l, i: (l, i, 0))

    def kern(g_ref, w_ref, m_ref, v_ref, d_ref, nm_ref, nv_ref):
        d_ref[...], nm_ref[...], nv_ref[...] = _adam_vals(g_ref[...], w_ref[...], m_ref[...], v_ref[...])

    outs = pl.pallas_call(
        kern, name=name, grid=(lead, rows // tr), in_specs=[spec] * 4, out_specs=[spec] * 3,
        out_shape=[jax.ShapeDtypeStruct((lead, rows, cols), F32)] * 3, compiler_params=_cparams(("parallel", "parallel")),
    )(*[t.reshape(lead, rows, cols) for t in (g, w, m, v)])
    return [o.reshape(shape) for o in outs]


SMALL = ("attn_norm", "sgu_ln_g", "sgu_ln_b", "sgu_w", "sgu_b", "attn_sinks", "dn_a_log", "dn_dt_bias", "dn_norm",
         "ffn_norm", "final_norm")


def _tile_rows(n):
    return -(-n // 1024) * 8


SLAB = tuple(k for k in SMALL if k != "sgu_w")


def _pack_small(vals, extra=()):
    tiles = []
    for t in [vals[k] for k in SLAB] + list(extra):
        flat = t.astype(F32).reshape(-1)
        rows = _tile_rows(flat.shape[0])
        tiles.append(jnp.pad(flat, (0, rows * 128 - flat.shape[0])).reshape(rows, 128))
    return jnp.concatenate(tiles, axis=0)


def _unpack_small(slab, shapes, extra_shapes=()):
    out, extras, o = {}, [], 0
    for k, shp in [(k, shapes[k]) for k in SLAB] + [(None, s) for s in extra_shapes]:
        n = math.prod(shp)
        rows = _tile_rows(n)
        t = slab[o:o + rows].reshape(-1)[:n].reshape(shp)
        o += rows
        if k is None:
            extras.append(t)
        else:
            out[k] = t
    return out, extras


def _in_col_segments():
    shard, padded = IN_COLS // N_CHIPS, BIG_SPEC["w_in"]["cols"]
    segs, mine = [], 0
    for a, n in IN_PIECES:
        o = a
        while o < a + n:
            end = min(a + n, (o // shard + 1) * shard)
            segs.append(((o // shard) * padded + o % shard, mine + o - a, end - o))
            o = end
        mine += n
    return segs


def _move_cols(x, segs, out_cols, name):
    layers, rows, cols = x.shape
    tr = _pick(rows, (256, rows))
    gaps, at = [], 0
    for d, w in sorted((d, w) for _, d, w in segs):
        if d > at:
            gaps.append((at, d - at))
        at = d + w
    if at < out_cols:
        gaps.append((at, out_cols - at))

    def kern(x_ref, o_ref):
        for s, d, w in segs:
            o_ref[0, :, d:d + w] = x_ref[0, :, s:s + w]
        for d, w in gaps:
            o_ref[0, :, d:d + w] = jnp.zeros((tr, w), x.dtype)

    return pl.pallas_call(
        kern, name=name, grid=(layers, rows // tr), in_specs=[pl.BlockSpec((1, tr, cols), lambda l, i: (l, i, 0))],
        out_specs=pl.BlockSpec((1, tr, out_cols), lambda l, i: (l, i, 0)),
        out_shape=jax.ShapeDtypeStruct((layers, rows, out_cols), x.dtype), compiler_params=_cparams(("parallel", "parallel")),
    )(x)


WEIGHTS = ("attn_norm", "w_in", "sgu_ln_g", "sgu_ln_b", "sgu_w", "sgu_b", "attn_sinks", "dn_conv_w", "dn_a_log",
           "dn_dt_bias", "dn_norm", "w_branch", "w_out", "ffn_norm", "w_gate_up", "w_down", "final_norm")


def kernel(x, positions, attn_norm, w_in, sgu_ln_g, sgu_ln_b, sgu_w, sgu_b, attn_sinks, dn_conv_w, dn_a_log, dn_dt_bias, dn_norm, w_branch, w_out, ffn_norm, w_gate_up, w_down, final_norm, loss_target, m_attn_norm, m_w_in, m_sgu_ln_g, m_sgu_ln_b, m_sgu_w, m_sgu_b, m_attn_sinks, m_dn_conv_w, m_dn_a_log, m_dn_dt_bias, m_dn_norm, m_w_branch, m_w_out, m_ffn_norm, m_w_gate_up, m_w_down, m_final_norm, v_attn_norm, v_w_in, v_sgu_ln_g, v_sgu_ln_b, v_sgu_w, v_sgu_b, v_attn_sinks, v_dn_conv_w, v_dn_a_log, v_dn_dt_bias, v_dn_norm, v_w_branch, v_w_out, v_ffn_norm, v_w_gate_up, v_w_down, v_final_norm):
    given = dict(locals())
    W = {k: given[k] for k in WEIGHTS}
    M = {k: given["m_" + k] for k in WEIGHTS}
    V = {k: given["v_" + k] for k in WEIGHTS}
    chip = 2 * lax.axis_index("x") + lax.axis_index("y")
    core = lax.axis_index("c")
    layer = core.astype(jnp.int32).reshape(1)
    chip1 = chip.astype(jnp.int32).reshape(1)
    where = jnp.stack([chip, core]).astype(jnp.int32)

    placed = [{k: _place_shard(W[k].reshape(DEPTH, BIG_SPEC[k]["rows"], BIG_SPEC[k]["keep"]), k, chip1, l,
                               f"place{l}_{k}") for k in BIG} for l in range(DEPTH)]
    full0, conv_full = _gather_layer(placed[0], dn_conv_w.reshape(CONV_ROWS, CONV_COLS))
    send_sems, recv_sems, inflight, token = _gather_start(placed[1])
    segs = _in_col_segments()

    def layer_matrices(full, l):
        big = dict(full)
        big["w_in"] = _move_cols(full["w_in"][None], segs, IN_R, f"w_in_cols{l}")[0]
        big["w_branch"] = full["w_branch"].reshape(3, MIX, D_MODEL)
        return big

    def big_of_layer(l, x_l):
        if l == 0:
            return layer_matrices(full0, 0)
        return layer_matrices(_gather_finish(_gather_wait(send_sems, recv_sems, inflight, x_l)), l)

    w = {k: W[k] for k in SMALL}
    w["attn_norm"] = attn_norm + token[0, 0]
    w["dn_conv_w"] = conv_full.reshape(DEPTH, DN_CONV, 3 * MIX)

    loss, dx, g = _local_step(x[0], positions[0], loss_target[0], w, big_of_layer)

    gb = {k: g[k] for k in BIG}
    gb["w_in"] = _move_cols(g["w_in"], [(d, s, n) for s, d, n in segs], _full_shape("w_in")[1], "g_in_cols")
    gb["w_branch"] = g["w_branch"].reshape(DEPTH, 3 * MIX, D_MODEL)
    sibling = _grads_to_sibling(gb)
    chip_sums = {k: _add_layer(gb[k], sibling[k], layer, "chip_sum_" + k) for k in BIG}
    parts = _scatter_chip_sums(chip_sums)
    reduced = _exchange_layers({k: _sum_chips(parts[k], chip_sums[k], k, where, "sum_" + k) for k in BIG})
    grads = {k: reduced[k].reshape(W[k].shape) for k in BIG}

    small_shapes = {k: W[k].shape for k in SMALL}
    no_extra = (jnp.zeros(g["dn_conv_w"].shape, F32), jnp.zeros((1,), F32))
    rows128 = lambda t: t.reshape(-1, 128)
    sgu, (gs, ds, nms, nvs) = _allreduce_small_adam([
        tuple(rows128(d["sgu_w"]) for d in (g, W, M, V)),
        (_pack_small(g, (g["dn_conv_w"], loss.reshape(1))), _pack_small(W, no_extra), _pack_small(M, no_extra),
         _pack_small(V, no_extra))])
    gsm, (conv_sum, loss_sum) = _unpack_small(gs, small_shapes, (g["dn_conv_w"].shape, (1,)))
    grads.update(gsm)
    grads["dn_conv_w"] = lax.dynamic_slice_in_dim(conv_sum, chip * dn_conv_w.shape[2], dn_conv_w.shape[2], axis=2)
    loss_total = loss_sum[0]
    delta, new_m, new_v = (_unpack_small(t, small_shapes)[0] for t in (ds, nms, nvs))
    for d, t in zip((grads, delta, new_m, new_v), sgu):
        d["sgu_w"] = t.reshape(sgu_w.shape)
    for k in BIG + ("dn_conv_w",):
        delta[k], new_m[k], new_v[k] = _adam(grads[k], W[k], M[k], V[k], "adam_" + k)

    return (loss_total, dx[None], *[grads[k] for k in WEIGHTS], *[delta[k] for k in WEIGHTS],
            *[new_m[k] for k in WEIGHTS], *[new_v[k] for k in WEIGHTS])
```

```python
import functools
import math

import jax
import jax.numpy as jnp
from jax import lax
from jax.experimental import pallas as pl
from jax.experimental.pallas import tpu as pltpu

F32 = jnp.float32
BF16 = jnp.bfloat16
HI = lax.Precision.HIGHEST

D_MODEL = 1024
DEPTH = 2
MIX = 512
EPS = 1e-6
SGU_G, SGU_T = 4, 128
SWA_H, SWA_KV, SWA_HD, WINDOW = 8, 2, 64, 128
ROPE_THETA, ROPE_DIM = 500000.0, 16
DN_H, DN_HD, DN_CONV, DN_C = 4, 128, 4, 64
D_FF = 2816
IN_COLS = 6920
IN_PIECES = ((3848, 3072), (1792, 1536), (0, 512), (512, 512), (1024, 512), (3328, 512), (1536, 128), (1664, 128),
             (3840, 8))
IN_PAD = 120
IN_R = 7040
C_GATE, C_QKV, C_UA, C_VA, C_QB, C_ZC, C_KB, C_VB, C_SM = 0, 3072, 4608, 5120, 5632, 6144, 6656, 6784, 6912

ADAM_LR, ADAM_B1, ADAM_B2, ADAM_EPS, ADAM_WD, ADAM_STEP = 0.001, 0.9, 0.999, 1e-08, 0.01, 10
VMEM_LIMIT = 56 * 1024 * 1024


def _cparams(sem):
    return pltpu.CompilerParams(dimension_semantics=sem, vmem_limit_bytes=VMEM_LIMIT)


def _dg(a, b, ca, cb, prec=None):
    return lax.dot_general(a, b, (((ca,), (cb,)), ((), ())), precision=prec, preferred_element_type=F32)


def _split(x):
    hi = x.astype(BF16)
    return hi, (x - hi.astype(F32)).astype(BF16)


def _dg3_many(as_, bs, ca, cb):
    sa = [_split(a) for a in as_]
    sb = [_split(b) for b in bs]
    hh = [_dg(a[0], b[0], ca, cb) for a, b in zip(sa, sb)]
    hl = [_dg(a[0], b[1], ca, cb) for a, b in zip(sa, sb)]
    lh = [_dg(a[1], b[0], ca, cb) for a, b in zip(sa, sb)]
    return [x + (y + z) for x, y, z in zip(hh, hl, lh)]


def _dg_exact_lhs_many(a01, bs, ca, cb):
    a = a01.astype(BF16)
    b1 = [b.astype(BF16) for b in bs]
    r1 = [b - t.astype(F32) for b, t in zip(bs, b1)]
    b2 = [r.astype(BF16) for r in r1]
    b3 = [(r - t.astype(F32)).astype(BF16) for r, t in zip(r1, b2)]
    d1 = [_dg(a, t, ca, cb) for t in b1]
    d2 = [_dg(a, t, ca, cb) for t in b2]
    d3 = [_dg(a, t, ca, cb) for t in b3]
    return [x + (y + z) for x, y, z in zip(d1, d2, d3)]


def _mm(a, b):
    return _dg(a.astype(BF16), b.astype(BF16), 1, 0)


def _mm_nt(a, b):
    return _dg(a.astype(BF16), b.astype(BF16), 1, 1)


def _mm_tn(a, b):
    return _dg(a.astype(BF16), b.astype(BF16), 0, 0)


def _sigmoid(x):
    return 0.5 * jnp.tanh(0.5 * x) + 0.5


def _silu(x):
    return x * _sigmoid(x)


def _dsilu(x):
    s = _sigmoid(x)
    return s * (1.0 + x * (1.0 - s))


_GC = math.sqrt(2.0 / math.pi)


def _gelu(x):
    return 0.5 * x * (1.0 + jnp.tanh(_GC * (x + 0.044715 * x * x * x)))


def _dgelu(x):
    t = jnp.tanh(_GC * (x + 0.044715 * x * x * x))
    return 0.5 * (1.0 + t) + 0.5 * x * (1.0 - t * t) * _GC * (1.0 + 3.0 * 0.044715 * x * x)


def _softplus(x):
    return jnp.maximum(x, 0.0) + jnp.log(1.0 + jnp.exp(-jnp.abs(x)))


def _acc(ref, val, i):
    @pl.when(i == 0)
    def _():
        ref[...] = val

    @pl.when(i > 0)
    def _():
        ref[...] += val


def _halo_rows(dtype):
    return 8 * 4 // jnp.dtype(dtype).itemsize


def _tok_call(body, name, S, TB, tok_in, const_in=(), tok_out=(), acc_out=(), prev_in=(), next_in=(), smem_in=()):
    nb = S // TB
    in_specs, args = [], []
    for a, w, cb in tok_in:
        in_specs.append(pl.BlockSpec((TB, w), functools.partial(lambda i, cb: (i, cb), cb=cb)))
        args.append(a)
    for a, w, cb in prev_in:
        hr = _halo_rows(a.dtype)
        in_specs.append(pl.BlockSpec((hr, w), functools.partial(
            lambda i, cb, r: (jnp.maximum(i * r - 1, 0), cb), cb=cb, r=TB // hr)))
        args.append(a)
    for a, w, cb in next_in:
        hr = _halo_rows(a.dtype)
        in_specs.append(pl.BlockSpec((hr, w), functools.partial(
            lambda i, cb, r, last: (jnp.minimum((i + 1) * r, last), cb), cb=cb, r=TB // hr, last=S // hr - 1)))
        args.append(a)
    for a in const_in:
        in_specs.append(pl.BlockSpec(a.shape, lambda i: (0, 0)))
        args.append(a)
    for a in smem_in:
        in_specs.append(pl.BlockSpec(memory_space=pltpu.SMEM))
        args.append(a)
    out_specs, out_shape = [], []
    for w, dt in tok_out:
        out_specs.append(pl.BlockSpec((TB, w), lambda i: (i, 0)))
        out_shape.append(jax.ShapeDtypeStruct((S, w), dt))
    for shp, dt in acc_out:
        out_specs.append(pl.BlockSpec(shp, lambda i: (0, 0)))
        out_shape.append(jax.ShapeDtypeStruct(shp, dt))

    def kern(*refs):
        body(pl.program_id(0), *refs)

    return pl.pallas_call(
        kern, name=name, grid=(nb,), in_specs=in_specs, out_specs=out_specs, out_shape=out_shape,
        compiler_params=_cparams(("arbitrary",)),
    )(*args)


MM_BLOCKS = (1024, 1408, 640, 512, 256, 128)


def _pick(n, cands):
    for c in cands:
        if n % c == 0:
            return c
    return n


MM_VMEM_BUDGET = 44 * 1024 * 1024


def _mm_blocks(M, N, K, a_bytes, b_bytes, o_bytes, add_bytes):
    bn = _pick(N, MM_BLOCKS)
    fits = None
    for bk in [K] + [c for c in (2816, 2048) + MM_BLOCKS if c < K and K % c == 0]:
        for bm in [c for c in MM_BLOCKS if M % c == 0 and c >= min(M, 512)]:
            b_bufs = 1 if (bk == K and bn == N) else 2
            need = 2 * bm * bk * a_bytes + b_bufs * bk * bn * b_bytes + 2 * bm * bn * (o_bytes + add_bytes)
            need += bm * bn * 4 if bk < K else 0
            if need <= MM_VMEM_BUDGET:
                fits = fits or (bm, bn, bk)
                if (M // bm) * (N // bn) * (K // bk) >= 4:
                    return bm, bn, bk
    if fits is None:
        raise ValueError(f"no matmul blocks for {(M, N, K)}")
    return fits


def _matmul(a, b, *, ta=False, tb=False, add=None, out_dtype=F32, name):
    M, K = (a.shape[1], a.shape[0]) if ta else a.shape
    N = b.shape[0] if tb else b.shape[1]
    bm, bn, bk = _mm_blocks(M, N, K, a.dtype.itemsize, b.dtype.itemsize, jnp.dtype(out_dtype).itemsize,
                            0 if add is None else add.dtype.itemsize)
    nk = K // bk
    b_mode = dict(pipeline_mode=pl.Buffered(1)) if (bk == K and bn == N) else {}
    a_spec = pl.BlockSpec((bk, bm), lambda i, j, k: (k, i)) if ta else pl.BlockSpec((bm, bk), lambda i, j, k: (i, k))
    b_spec = (pl.BlockSpec((bn, bk), lambda i, j, k: (j, k), **b_mode) if tb
              else pl.BlockSpec((bk, bn), lambda i, j, k: (k, j), **b_mode))
    o_spec = pl.BlockSpec((bm, bn), lambda i, j, k: (i, j))
    ca, cb = (0 if ta else 1), (1 if tb else 0)

    def kern(*refs):
        a_ref, b_ref = refs[:2]
        add_ref = refs[2] if add is not None else None
        o_ref = refs[3] if add is not None else refs[2]
        p = _dg(a_ref[...].astype(BF16), b_ref[...].astype(BF16), ca, cb)

        def finish(r):
            if add is not None:
                r = r + add_ref[...].astype(F32)
            o_ref[...] = r.astype(out_dtype)

        if nk == 1:
            finish(p)
            return
        acc_ref = refs[-1]
        k = pl.program_id(2)

        @pl.when(k == 0)
        def _():
            acc_ref[...] = p

        @pl.when((k > 0) & (k < nk - 1))
        def _():
            acc_ref[...] += p

        @pl.when(k == nk - 1)
        def _():
            finish(acc_ref[...] + p)

    in_specs = [a_spec, b_spec] + ([o_spec] if add is not None else [])
    args = (a, b) + ((add,) if add is not None else ())
    return pl.pallas_call(
        kern, name=name, grid=(M // bm, N // bn, nk), in_specs=in_specs, out_specs=o_spec,
        out_shape=jax.ShapeDtypeStruct((M, N), out_dtype),
        scratch_shapes=[pltpu.VMEM((bm, bn), F32)] if nk > 1 else [],
        compiler_params=_cparams(("parallel", "parallel", "arbitrary")),
    )(*args)


def _rms_fwd(x, g, name):
    S = x.shape[0]

    def body(i, x_ref, g_ref, h_ref):
        xv = x_ref[...]
        r = lax.rsqrt(jnp.mean(xv * xv, axis=-1, keepdims=True) + EPS)
        h_ref[...] = (xv * r * g_ref[...]).astype(BF16)

    return _tok_call(body, name, S, min(S, 512), [(x, D_MODEL, 0)], [g], [(D_MODEL, BF16)])[0]


def _rms_bwd_vals(xv, g, dh):
    r = lax.rsqrt(jnp.mean(xv * xv, axis=-1, keepdims=True) + EPS)
    u = dh * g
    dx = r * u - xv * (r * r * r) * jnp.mean(u * xv, axis=-1, keepdims=True)
    dg = jnp.sum(dh * xv * r, axis=0, keepdims=True)
    return dx, dg


def _rms_bwd_add(x, g, dh, dres, name):
    S = x.shape[0]

    def body(i, x_ref, dh_ref, dr_ref, g_ref, dx_ref, dg_ref):
        dx, dg = _rms_bwd_vals(x_ref[...], g_ref[...], dh_ref[...].astype(F32))
        dx_ref[...] = dr_ref[...] + dx
        _acc(dg_ref, dg, i)

    return _tok_call(body, name, S, min(S, 512), [(x, D_MODEL, 0), (dh, D_MODEL, 0), (dres, D_MODEL, 0)], [g],
                     [(D_MODEL, F32)], [((1, D_MODEL), F32)])


def _final_loss(x, g, target):
    S = x.shape[0]

    def body(i, x_ref, t_ref, g_ref, dx_ref, loss_ref, dg_ref):
        xv, gv = x_ref[...], g_ref[...]
        r = lax.rsqrt(jnp.mean(xv * xv, axis=-1, keepdims=True) + EPS)
        e = xv * r * gv - t_ref[...]
        part = 0.5 * jnp.sum(jnp.mean(e * e, axis=-1, keepdims=True), axis=0, keepdims=True)
        dx, dg = _rms_bwd_vals(xv, gv, e * (1.0 / D_MODEL))
        dx_ref[...] = dx
        _acc(loss_ref, jnp.broadcast_to(part, (1, 128)), i)
        _acc(dg_ref, dg, i)

    return _tok_call(body, "final_loss", S, min(S, 512), [(x, D_MODEL, 0), (target, D_MODEL, 0)], [g],
                     [(D_MODEL, F32)], [((1, 128), F32), ((1, D_MODEL), F32)])


def _swiglu_fwd(gu, name):
    S = gu.shape[0]

    def body(i, gu_ref, a_ref):
        a_ref[...] = (_silu(gu_ref[:, :D_FF].astype(F32)) * gu_ref[:, D_FF:].astype(F32)).astype(BF16)

    return _tok_call(body, name, S, min(S, 256), [(gu, 2 * D_FF, 0)], [], [(D_FF, BF16)])[0]


def _swiglu_bwd(gu, dact, name):
    S = gu.shape[0]

    def body(i, gu_ref, da_ref, dgu_ref):
        gg, uu, da = gu_ref[:, :D_FF].astype(F32), gu_ref[:, D_FF:].astype(F32), da_ref[...].astype(F32)
        dgu_ref[:, :D_FF] = (da * uu * _dsilu(gg)).astype(BF16)
        dgu_ref[:, D_FF:] = (da * _silu(gg)).astype(BF16)

    return _tok_call(body, name, S, min(S, 256), [(gu, 2 * D_FF, 0), (dact, D_FF, 0)], [], [(2 * D_FF, BF16)])[0]


def _merge_fwd(proj, bds, name):
    S = proj.shape[0]

    def body(i, g0, g1, g2, b0, b1, b2, m_ref):
        m = jnp.zeros(m_ref.shape, F32)
        for gr, br in ((g0, b0), (g1, b1), (g2, b2)):
            m = m + _sigmoid(gr[...].astype(F32)) * br[...].astype(F32)
        m_ref[...] = m.astype(BF16)

    tok = [(proj, D_MODEL, n) for n in range(3)] + [(b, D_MODEL, 0) for b in bds]
    return _tok_call(body, name, S, min(S, 512), tok, [], [(D_MODEL, BF16)])[0]


def _merge_bwd(proj, bds, dm, name):
    S = proj.shape[0]

    def body(i, g0, g1, g2, b0, b1, b2, dm_ref, d0, d1, d2, dgp_ref):
        dmv = dm_ref[...]
        for n, (gr, br, dr) in enumerate(((g0, b0, d0), (g1, b1, d1), (g2, b2, d2))):
            s = _sigmoid(gr[...].astype(F32))
            dr[...] = (dmv * s).astype(BF16)
            dgp_ref[:, n * D_MODEL:(n + 1) * D_MODEL] = (dmv * br[...].astype(F32) * s * (1.0 - s)).astype(BF16)

    tok = [(proj, D_MODEL, n) for n in range(3)] + [(b, D_MODEL, 0) for b in bds] + [(dm, D_MODEL, 0)]
    return _tok_call(body, name, S, min(S, 512), tok, [],
                     [(D_MODEL, BF16)] * 3 + [(3 * D_MODEL, BF16)])


def _sgu_ln(v, lg, lb):
    mu = jnp.mean(v, axis=-1, keepdims=True)
    vc = v - mu
    rstd = lax.rsqrt(jnp.mean(vc * vc, axis=-1, keepdims=True) + EPS)
    vhat = vc * rstd
    return vhat, rstd, vhat * lg + lb


def _sgu_fwd(proj, lg, lb, wc, bst, name):
    S = proj.shape[0]

    def body(i, ua_ref, va_ref, lg_ref, lb_ref, wc_ref, bs_ref, o_ref):
        u = _gelu(ua_ref[...].astype(F32))
        _, _, vn = _sgu_ln(_gelu(va_ref[...].astype(F32)), lg_ref[...], lb_ref[...])
        for g in range(SGU_G):
            sl = slice(g * 128, (g + 1) * 128)
            mixed = _mm(wc_ref[sl, :], vn[:, sl]) + bs_ref[:, g:g + 1]
            o_ref[:, sl] = (u[:, sl] * mixed).astype(BF16)

    return _tok_call(body, name, S, SGU_T, [(proj, MIX, C_UA // MIX), (proj, MIX, C_VA // MIX)], [lg, lb, wc, bst],
                     [(MIX, BF16)])[0]


def _sgu_bwd(proj, lg, lb, wc, bst, dout, name):
    S = proj.shape[0]

    def body(i, ua_ref, va_ref, do_ref, lg_ref, lb_ref, wc_ref, bs_ref, dua_ref, dva_ref, dlg_ref, dlb_ref, dwc_ref,
             dbs_ref):
        ua, va, do = ua_ref[...].astype(F32), va_ref[...].astype(F32), do_ref[...].astype(F32)
        u = _gelu(ua)
        lgv = lg_ref[...]
        vhat, rstd, vn = _sgu_ln(_gelu(va), lgv, lb_ref[...])
        tril = lax.broadcasted_iota(jnp.int32, (128, 128), 0) >= lax.broadcasted_iota(jnp.int32, (128, 128), 1)
        lane4 = lax.broadcasted_iota(jnp.int32, (128, 4), 1)
        dvn_parts, dbs = [], jnp.zeros((128, 4), F32)
        for g in range(SGU_G):
            sl = slice(g * 128, (g + 1) * 128)
            wg = wc_ref[sl, :]
            mixed = _mm(wg, vn[:, sl]) + bs_ref[:, g:g + 1]
            dua_ref[:, sl] = (do[:, sl] * mixed * _dgelu(ua[:, sl])).astype(BF16)
            dmix = do[:, sl] * u[:, sl]
            dbs = dbs + jnp.where(lane4 == g, jnp.sum(dmix, axis=-1, keepdims=True), 0.0)
            dwg = jnp.where(tril, _mm_nt(dmix, vn[:, sl]), 0.0)
            _acc(dwc_ref.at[sl, :], dwg, i)
            dvn_parts.append(_mm_tn(wg, dmix))
        dvn = jnp.concatenate(dvn_parts, axis=1)
        _acc(dbs_ref, dbs, i)
        _acc(dlg_ref, jnp.sum(dvn * vhat, axis=0, keepdims=True), i)
        _acc(dlb_ref, jnp.sum(dvn, axis=0, keepdims=True), i)
        dvh = dvn * lgv
        dv = rstd * (dvh - jnp.mean(dvh, axis=-1, keepdims=True) - vhat * jnp.mean(dvh * vhat, axis=-1, keepdims=True))
        dva_ref[...] = (dv * _dgelu(va)).astype(BF16)

    return _tok_call(body, name, S, SGU_T, [(proj, MIX, C_UA // MIX), (proj, MIX, C_VA // MIX), (dout, MIX, 0)],
                     [lg, lb, wc, bst], [(MIX, BF16), (MIX, BF16)],
                     [((1, MIX), F32), ((1, MIX), F32), ((SGU_G * 128, 128), F32), ((128, 4), F32)])


def _rope_tables(positions):
    S = positions.shape[0]
    inv_freq = ROPE_THETA ** (-jnp.arange(0, ROPE_DIM, 2, dtype=F32) / ROPE_DIM)
    ang = positions.astype(F32)[:, None] * inv_freq
    c, s = jnp.cos(ang), jnp.sin(ang)
    c64 = jnp.concatenate([c, c, jnp.ones((S, SWA_HD - ROPE_DIM), F32)], axis=1)
    s64 = jnp.concatenate([-s, s, jnp.zeros((S, SWA_HD - ROPE_DIM), F32)], axis=1)
    return jnp.tile(c64, (1, 2)), jnp.tile(s64, (1, 2))


def _rope128(x, c, s):
    lane = lax.broadcasted_iota(jnp.int32, x.shape, 1) % SWA_HD
    swapped = jnp.where(lane < ROPE_DIM // 2, pltpu.roll(x, 128 - ROPE_DIM // 2, 1), pltpu.roll(x, ROPE_DIM // 2, 1))
    return x * c + swapped * s


def _rope_t128(y, c, s):
    ys = y * s
    lane = lax.broadcasted_iota(jnp.int32, y.shape, 1) % SWA_HD
    swapped = jnp.where(lane < ROPE_DIM // 2, pltpu.roll(ys, 128 - ROPE_DIM // 2, 1), pltpu.roll(ys, ROPE_DIM // 2, 1))
    return y * c + jnp.where(lane < ROPE_DIM, swapped, 0.0)


def _rope_fwd(proj, cos, sin, name):
    S = proj.shape[0]
    scale = SWA_HD ** -0.5

    def body(i, q_ref, k_ref, v_ref, c_ref, s_ref, qo_ref, ko_ref, vo_ref):
        c, s = c_ref[...], s_ref[...]
        for j in range(4):
            sl = slice(j * 128, (j + 1) * 128)
            qo_ref[:, sl] = (_rope128(q_ref[:, sl].astype(F32), c, s) * scale).astype(BF16)
        ko_ref[...] = _rope128(k_ref[...].astype(F32), c, s).astype(BF16)
        vo_ref[...] = v_ref[...].astype(BF16)

    return _tok_call(body, name, S, min(S, 512),
                     [(proj, MIX, C_QB // MIX), (proj, 128, C_KB // 128), (proj, 128, C_VB // 128), (cos, 128, 0),
                      (sin, 128, 0)], [], [(MIX, BF16), (128, BF16), (128, BF16)])


def _rope_bwd(dq, dk, dv, cos, sin, name):
    S = dq.shape[0]
    scale = SWA_HD ** -0.5

    def body(i, dq_ref, dk_ref, dv_ref, c_ref, s_ref, qo_ref, ko_ref, vo_ref):
        c, s = c_ref[...], s_ref[...]
        for j in range(4):
            sl = slice(j * 128, (j + 1) * 128)
            qo_ref[:, sl] = _rope_t128(dq_ref[:, sl] * scale, c, s).astype(BF16)
        ko_ref[...] = _rope_t128(dk_ref[...], c, s).astype(BF16)
        vo_ref[...] = dv_ref[...].astype(BF16)

    return _tok_call(body, name, S, min(S, 512),
                     [(dq, MIX, 0), (dk, 128, 0), (dv, 128, 0), (cos, 128, 0), (sin, 128, 0)], [],
                     [(MIX, BF16), (128, BF16), (128, BF16)])


def _swa_band(i, k_ref, v_ref):
    pstart = pl.multiple_of(jnp.maximum(i - 1, 0) * WINDOW, WINDOW)
    cstart = pl.multiple_of(i * WINDOW, WINDOW)
    kb = jnp.concatenate([k_ref[pl.ds(pstart, WINDOW), :], k_ref[pl.ds(cstart, WINDOW), :]], axis=0)
    vb = jnp.concatenate([v_ref[pl.ds(pstart, WINDOW), :], v_ref[pl.ds(cstart, WINDOW), :]], axis=0)
    qi = lax.broadcasted_iota(jnp.int32, (WINDOW, 2 * WINDOW), 0)
    sj = lax.broadcasted_iota(jnp.int32, (WINDOW, 2 * WINDOW), 1)
    mask = (sj > qi) & (sj <= qi + WINDOW) & ((i > 0) | (sj >= WINDOW))
    return kb, vb, mask, pstart, cstart


def _swa_probs(qs, kh, mask, sinks):
    logits = [jnp.where(mask, _dg(qh, kh, 1, 1), -1e30) for qh in qs]
    m = [jnp.maximum(jnp.max(l, axis=-1, keepdims=True), s) for l, s in zip(logits, sinks)]
    p = [jnp.exp(l - mm) for l, mm in zip(logits, m)]
    ps = [jnp.exp(s - mm) for s, mm in zip(sinks, m)]
    inv = [1.0 / (jnp.sum(pp, axis=-1, keepdims=True) + s) for pp, s in zip(p, ps)]
    return [pp * iv for pp, iv in zip(p, inv)], [s * iv for s, iv in zip(ps, inv)]


def _swa_fwd(q, k, v, sinks, name):
    S = q.shape[0]
    G = SWA_H // SWA_KV

    def body(i, q_ref, k_ref, v_ref, s_ref, o_ref):
        kb, vb, mask, _, _ = _swa_band(i, k_ref, v_ref)
        qv = q_ref[...]
        for kv in range(SWA_KV):
            ksl = slice(kv * SWA_HD, (kv + 1) * SWA_HD)
            heads = range(kv * G, (kv + 1) * G)
            pn, _ = _swa_probs([qv[:, h * SWA_HD:(h + 1) * SWA_HD] for h in heads], kb[:, ksl], mask,
                               [s_ref[0, h] for h in heads])
            outs = [_dg(p.astype(BF16), vb[:, ksl], 1, 0) for p in pn]
            for h, o in zip(heads, outs):
                o_ref[:, h * SWA_HD:(h + 1) * SWA_HD] = o.astype(BF16)

    return _tok_call(body, name, S, WINDOW, [(q, MIX, 0)], [k, v], [(MIX, BF16)], smem_in=[sinks])[0]


def _swa_bwd(q, k, v, sinks, dout, name):
    S = q.shape[0]

    def body(i, q_ref, do_ref, k_ref, v_ref, s_ref, dq_ref, dk_ref, dv_ref, ds_ref):
        kb, vb, mask, pstart, cstart = _swa_band(i, k_ref, v_ref)
        qv, dov = q_ref[...], do_ref[...]
        lane = lax.broadcasted_iota(jnp.int32, (1, 128), 1)
        dsink = jnp.zeros((1, 128), F32)
        dkb, dvb = [], []
        G = SWA_H // SWA_KV
        for kv in range(SWA_KV):
            ksl = slice(kv * SWA_HD, (kv + 1) * SWA_HD)
            heads = range(kv * G, (kv + 1) * G)
            qs = [qv[:, h * SWA_HD:(h + 1) * SWA_HD] for h in heads]
            dos = [dov[:, h * SWA_HD:(h + 1) * SWA_HD].astype(BF16) for h in heads]
            pn, psn = _swa_probs(qs, kb[:, ksl], mask, [s_ref[0, h] for h in heads])
            dp = [_dg(d, vb[:, ksl], 1, 1) for d in dos]
            delta = [jnp.sum(a * b, axis=-1, keepdims=True) for a, b in zip(dp, pn)]
            dsc = [(p * (a - d)).astype(BF16) for p, a, d in zip(pn, dp, delta)]
            dqs = [_dg(s, kb[:, ksl], 1, 0) for s in dsc]
            dks = [_dg(s, qh, 0, 0) for s, qh in zip(dsc, qs)]
            dvs = [_dg(p.astype(BF16), d, 0, 0) for p, d in zip(pn, dos)]
            for n_, h in enumerate(heads):
                dq_ref[:, h * SWA_HD:(h + 1) * SWA_HD] = dqs[n_]
                dsink = dsink + jnp.where(lane == h, -jnp.sum(psn[n_] * delta[n_], axis=0, keepdims=True), 0.0)
            dkb.append((dks[0] + dks[1]) + (dks[2] + dks[3]))
            dvb.append((dvs[0] + dvs[1]) + (dvs[2] + dvs[3]))
        dkb = jnp.concatenate(dkb, axis=1)
        dvb = jnp.concatenate(dvb, axis=1)

        @pl.when(i == 0)
        def _():
            dk_ref[...] = jnp.zeros_like(dk_ref)
            dv_ref[...] = jnp.zeros_like(dv_ref)

        dk_ref[pl.ds(pstart, WINDOW), :] += dkb[:WINDOW]
        dv_ref[pl.ds(pstart, WINDOW), :] += dvb[:WINDOW]
        dk_ref[pl.ds(cstart, WINDOW), :] += dkb[WINDOW:]
        dv_ref[pl.ds(cstart, WINDOW), :] += dvb[WINDOW:]
        _acc(ds_ref, dsink, i)

    return _tok_call(body, name, S, WINDOW, [(q, MIX, 0), (dout, MIX, 0)], [k, v], [(MIX, F32)],
                     [((S, 128), F32), ((S, 128), F32), ((1, 128), F32)], smem_in=[sinks])


def _shift_rows(xs, k):
    return xs if k == 0 else pltpu.roll(xs, k, 0)


def _dn_conv(x_ref, p_ref, w_ref, i):
    hr = p_ref.shape[0]
    halo = jnp.where(i > 0, p_ref[...].astype(F32), 0.0)
    xs = jnp.concatenate([halo, x_ref[...].astype(F32)], axis=0)
    sh = [_shift_rows(xs, DN_CONV - 1 - t)[hr:] for t in range(DN_CONV)]
    pre = sh[0] * w_ref[0:1, :]
    for t in range(1, DN_CONV):
        pre = pre + sh[t] * w_ref[t:t + 1, :]
    return pre, sh


def _dn_gates(sm, alog, dtb):
    lane = lax.broadcasted_iota(jnp.int32, sm.shape, 1)
    return jnp.where(lane < DN_H, _sigmoid(sm), -jnp.exp(alog) * _softplus(sm + dtb))


def _dn_pre_fwd(proj, conv_w, alog_l, dtb_l, name):
    S = proj.shape[0]
    scale = DN_HD ** -0.5

    def body(i, x_ref, sm_ref, p_ref, w_ref, al_ref, db_ref, q_ref, k_ref, v_ref, bg_ref):
        pre, _ = _dn_conv(x_ref, p_ref, w_ref, i)
        a = _silu(pre)
        for h in range(DN_H):
            sl = slice(h * DN_HD, (h + 1) * DN_HD)
            qh, kh = a[:, sl], a[:, MIX + h * DN_HD:MIX + (h + 1) * DN_HD]
            q_ref[:, sl] = qh * (lax.rsqrt(jnp.sum(qh * qh, axis=-1, keepdims=True) + EPS) * scale)
            k_ref[:, sl] = kh * lax.rsqrt(jnp.sum(kh * kh, axis=-1, keepdims=True) + EPS)
        v_ref[...] = a[:, 2 * MIX:]
        bg_ref[...] = _dn_gates(sm_ref[...].astype(F32), al_ref[...], db_ref[...])

    TB = min(S, 256)
    return _tok_call(body, name, S, TB, [(proj, 3 * MIX, C_QKV // (3 * MIX)), (proj, 128, C_SM // 128)],
                     [conv_w, alog_l, dtb_l], [(MIX, F32), (MIX, F32), (MIX, F32), (128, F32)],
                     prev_in=[(proj, 3 * MIX, C_QKV // (3 * MIX))])


def _dn_pre_bwd1(proj, conv_w, alog_l, dtb_l, dq, dk, dv, dbg, name):
    S = proj.shape[0]
    scale = DN_HD ** -0.5

    def body(i, x_ref, sm_ref, dq_ref, dk_ref, dv_ref, dbg_ref, p_ref, w_ref, al_ref, db_ref, dpre_ref, dsm_ref,
             dw_ref, dal_ref, ddb_ref):
        pre, sh = _dn_conv(x_ref, p_ref, w_ref, i)
        a = _silu(pre)
        da_parts = []
        for part, (g_ref, sc) in enumerate(((dq_ref, scale), (dk_ref, 1.0))):
            for h in range(DN_H):
                xh = a[:, part * MIX + h * DN_HD:part * MIX + (h + 1) * DN_HD]
                rs = lax.rsqrt(jnp.sum(xh * xh, axis=-1, keepdims=True) + EPS)
                y = xh * rs
                dy = g_ref[:, h * DN_HD:(h + 1) * DN_HD] * sc
                da_parts.append(rs * (dy - y * jnp.sum(dy * y, axis=-1, keepdims=True)))
        da_parts.append(dv_ref[...])
        dpre = jnp.concatenate(da_parts, axis=1) * _dsilu(pre)
        dpre_ref[...] = dpre
        dw = jnp.concatenate([jnp.sum(dpre * sh[t], axis=0, keepdims=True) for t in range(DN_CONV)], axis=0)
        _acc(dw_ref, dw, i)
        sm, al, db, dbg_v = sm_ref[...].astype(F32), al_ref[...], db_ref[...], dbg_ref[...]
        lane = lax.broadcasted_iota(jnp.int32, sm.shape, 1)
        sg = _sigmoid(sm)
        gneg = -jnp.exp(al)
        is_g = (lane >= DN_H) & (lane < 2 * DN_H)
        d_al = jnp.where(is_g, dbg_v * gneg * _sigmoid(sm + db), 0.0)
        dsm_ref[...] = jnp.where(lane < DN_H, dbg_v * sg * (1.0 - sg), d_al).astype(BF16)
        _acc(ddb_ref, jnp.sum(d_al, axis=0, keepdims=True), i)
        _acc(dal_ref, jnp.sum(jnp.where(is_g, dbg_v * gneg * _softplus(sm + db), 0.0), axis=0, keepdims=True), i)

    TB = min(S, 256)
    return _tok_call(body, name, S, TB,
                     [(proj, 3 * MIX, C_QKV // (3 * MIX)), (proj, 128, C_SM // 128), (dq, MIX, 0), (dk, MIX, 0),
                      (dv, MIX, 0), (dbg, 128, 0)], [conv_w, alog_l, dtb_l],
                     [(3 * MIX, F32), (128, BF16)], [((DN_CONV, 3 * MIX), F32), ((1, 128), F32), ((1, 128), F32)],
                     prev_in=[(proj, 3 * MIX, C_QKV // (3 * MIX))])


def _dn_pre_bwd2(dpre, conv_w, name):
    S = dpre.shape[0]
    TB = min(S, 256)
    nb = S // TB

    def body(i, d_ref, n_ref, w_ref, o_ref):
        halo = jnp.where(i < nb - 1, n_ref[...], 0.0)
        ds = jnp.concatenate([d_ref[...], halo], axis=0)
        out = ds[:TB] * w_ref[DN_CONV - 1:DN_CONV, :]
        for t in range(DN_CONV - 1):
            k = DN_CONV - 1 - t
            out = out + pltpu.roll(ds, TB + 8 - k, 0)[:TB] * w_ref[t:t + 1, :]
        o_ref[...] = out.astype(BF16)

    return _tok_call(body, name, S, TB, [(dpre, 3 * MIX, 0)], [conv_w], [(3 * MIX, BF16)],
                     next_in=[(dpre, 3 * MIX, 0)])[0]


def _dn_decay_terms(bgs, heads):
    C = DN_C
    ri = lax.broadcasted_iota(jnp.int32, (C, C), 0)
    ci = lax.broadcasted_iota(jnp.int32, (C, C), 1)
    tril, eye = ri >= ci, ri == ci
    beta = [b[:, h:h + 1] for b, h in zip(bgs, heads)]
    gcol = _dg_exact_lhs_many(tril, [jnp.broadcast_to(b[:, DN_H + h:DN_H + h + 1], (C, C))
                                     for b, h in zip(bgs, heads)], 1, 0)
    grow = [jnp.sum(jnp.where(eye, g, 0.0), axis=0, keepdims=True) for g in gcol]
    decay = [jnp.exp(jnp.where(tril, g - r, -1e30)) for g, r in zip(gcol, grow)]
    e_gc = [jnp.exp(g[:, 0:1]) for g in gcol]
    e_kd = [jnp.exp(g[C - 1:C, 0:1] - g[:, 0:1]) for g in gcol]
    cdec = [jnp.exp(g[C - 1:C, 0:1]) for g in gcol]
    return beta, decay, e_gc, e_kd, cdec


def _dn_nb(S):
    return 4 if S % (4 * DN_C) == 0 else 1


def _dn_prep_fwd(q, k, v, bg, name):
    S = q.shape[0]
    C, NB = DN_C, _dn_nb(S)
    TB = NB * C

    def kern(q_ref, k_ref, v_ref, bg_ref, t_ref, uw_ref, at_ref, qd_ref, kd_ref, dec_ref):
        lane = lax.broadcasted_iota(jnp.int32, (C, 128), 1)
        ri = lax.broadcasted_iota(jnp.int32, (C, C), 0)
        ci = lax.broadcasted_iota(jnp.int32, (C, C), 1)
        tril, eye = ri >= ci, ri == ci
        chains = [(cb, h) for cb in range(NB) for h in range(DN_H)]
        rows = lambda cb: slice(cb * C, (cb + 1) * C)
        head = lambda h: slice(h * DN_HD, (h + 1) * DN_HD)
        beta, decay, e_gc, e_kd, cdec = _dn_decay_terms([bg_ref[rows(cb), :] for cb, _ in chains],
                                                        [h for _, h in chains])
        qs = [q_ref[rows(cb), head(h)] for cb, h in chains]
        ks = [k_ref[rows(cb), head(h)] for cb, h in chains]
        kb = [kh * b for kh, b in zip(ks, beta)]
        x = [-jnp.where(ri > ci, _mm_nt(a, kh) * d, 0.0) for a, kh, d in zip(kb, ks, decay)]
        tm = [jnp.where(eye, 1.0, 0.0) + xi for xi in x]
        p = x
        for _ in range(5):
            p = _dg3_many(p, p, 1, 0)
            tm = [t + tp for t, tp in zip(tm, _dg3_many(tm, p, 1, 0))]
        rhs = [jnp.concatenate([v_ref[rows(cb), head(h)] * b, a * e], axis=1)
               for (cb, h), b, a, e in zip(chains, beta, kb, e_gc)]
        sol = _dg3_many(tm, rhs, 1, 0)
        attn = [_mm_nt(qh, kh) * d for qh, kh, d in zip(qs, ks, decay)]
        for n_, (cb, h) in enumerate(chains):
            rs, sl, hc = rows(cb), head(h), slice(h * C, (h + 1) * C)
            t_ref[rs, hc] = tm[n_]
            uw_ref[rs, sl] = sol[n_][:, :DN_HD]
            uw_ref[rs, MIX + h * DN_HD:MIX + (h + 1) * DN_HD] = sol[n_][:, DN_HD:]
            at_ref[rs, hc] = attn[n_]
            qd_ref[rs, sl] = (qs[n_] * e_gc[n_]).astype(BF16)
            kd_ref[rs, sl] = (ks[n_] * e_kd[n_]).astype(BF16)
        for cb in range(NB):
            dec = jnp.zeros((C, 128), F32)
            for h in range(DN_H):
                dec = dec + jnp.where(lane == h, cdec[cb * DN_H + h], 0.0)
            dec_ref[rows(cb), :] = dec

    tok = lambda w: pl.BlockSpec((TB, w), lambda i: (i, 0))
    return pl.pallas_call(
        kern, name=name, grid=(S // TB,), in_specs=[tok(MIX), tok(MIX), tok(MIX), tok(128)],
        out_specs=[tok(DN_H * C), tok(2 * MIX), tok(DN_H * C), tok(MIX), tok(MIX), tok(128)],
        out_shape=[jax.ShapeDtypeStruct((S, DN_H * C), F32), jax.ShapeDtypeStruct((S, 2 * MIX), F32),
                   jax.ShapeDtypeStruct((S, DN_H * C), F32), jax.ShapeDtypeStruct((S, MIX), BF16),
                   jax.ShapeDtypeStruct((S, MIX), BF16), jax.ShapeDtypeStruct((S, 128), F32)],
        compiler_params=_cparams(("parallel",)),
    )(q, k, v, bg)


def _dn_scan_fwd(uw, at, qd, kd, dec, name):
    S = uw.shape[0]
    C, NB = DN_C, _dn_nb(S)
    TB = NB * C
    SR = DN_H * DN_HD

    def kern(uw_ref, at_ref, qd_ref, kd_ref, dec_ref, o_ref, vn_ref, st_ref, state):
        @pl.when(pl.program_id(0) == 0)
        def _():
            state[...] = jnp.zeros_like(state)

        for cb in range(NB):
            rs = slice(cb * C, (cb + 1) * C)
            hs = range(DN_H)
            sls = [slice(h * DN_HD, (h + 1) * DN_HD) for h in hs]
            s_in = [state[sl, :] for sl in sls]
            ws = [_mm(uw_ref[rs, MIX + h * DN_HD:MIX + (h + 1) * DN_HD], s_in[h]) for h in hs]
            os_ = [_mm(qd_ref[rs, sls[h]], s_in[h]) for h in hs]
            vnew = [uw_ref[rs, sls[h]] - ws[h] for h in hs]
            oa = [_mm(at_ref[rs, h * C:(h + 1) * C], vnew[h]) for h in hs]
            kv = [_mm_tn(kd_ref[rs, sls[h]], vnew[h]) for h in hs]
            for h in hs:
                o_ref[rs, sls[h]] = os_[h] + oa[h]
                state[sls[h], :] = s_in[h] * dec_ref[cb * C:cb * C + 1, h:h + 1] + kv[h]
                st_ref[cb * SR + h * DN_HD:cb * SR + (h + 1) * DN_HD, :] = s_in[h]
                vn_ref[rs, sls[h]] = vnew[h]

    tok = lambda w: pl.BlockSpec((TB, w), lambda i: (i, 0))
    return pl.pallas_call(
        kern, name=name, grid=(S // TB,), in_specs=[tok(2 * MIX), tok(DN_H * C), tok(MIX), tok(MIX), tok(128)],
        out_specs=[tok(MIX), tok(MIX), pl.BlockSpec((NB * SR, DN_HD), lambda i: (i, 0))],
        out_shape=[jax.ShapeDtypeStruct((S, MIX), F32), jax.ShapeDtypeStruct((S, MIX), F32),
                   jax.ShapeDtypeStruct((S // C * SR, DN_HD), F32)],
        scratch_shapes=[pltpu.VMEM((SR, DN_HD), F32)],
        compiler_params=_cparams(("arbitrary",)),
    )(uw, at, qd, kd, dec)


def _dn_core_fwd(q, k, v, bg, name):
    tm, uw, at, qd, kd, dec = _dn_prep_fwd(q, k, v, bg, name + "_prep")
    o, vn, st = _dn_scan_fwd(uw, at, qd, kd, dec, name + "_scan")
    return o, dict(tm=tm, uw=uw, at=at, qd=qd, kd=kd, dec=dec, vn=vn, st=st)


def _dn_scan_bwd(sv, do, name):
    S = do.shape[0]
    C, NB = DN_C, _dn_nb(S)
    TB = NB * C
    SR = DN_H * DN_HD
    nb = S // TB

    def kern(do_ref, uw_ref, at_ref, qd_ref, kd_ref, dec_ref, vn_ref, st_ref, dvn_ref, dw_ref, dkd_ref, dc_ref, dstate):
        @pl.when(pl.program_id(0) == 0)
        def _():
            dstate[...] = jnp.zeros_like(dstate)

        lane = lax.broadcasted_iota(jnp.int32, (C, 128), 1)
        for cb in reversed(range(NB)):
            rs = slice(cb * C, (cb + 1) * C)
            dcrow = jnp.zeros((C, 128), F32)
            for h in range(DN_H):
                sl = slice(h * DN_HD, (h + 1) * DN_HD)
                doh, ds_o = do_ref[rs, sl], dstate[sl, :]
                s_in = st_ref[cb * SR + h * DN_HD:cb * SR + (h + 1) * DN_HD, :]
                d_vnew = _mm_tn(at_ref[rs, h * C:(h + 1) * C], doh) + _mm(kd_ref[rs, sl], ds_o)
                dvn_ref[rs, sl] = d_vnew
                dw_ref[rs, sl] = -_mm_nt(d_vnew, s_in)
                dkd_ref[rs, sl] = _mm_nt(vn_ref[rs, sl], ds_o)
                d_c = jnp.sum(jnp.sum(ds_o * s_in, axis=1, keepdims=True), axis=0, keepdims=True)
                dcrow = dcrow + jnp.where(lane == h, d_c, 0.0)
                dstate[sl, :] = (ds_o * dec_ref[cb * C:cb * C + 1, h:h + 1] + _mm_tn(qd_ref[rs, sl], doh)
                                 - _mm_tn(uw_ref[rs, MIX + h * DN_HD:MIX + (h + 1) * DN_HD], d_vnew))
            dc_ref[rs, :] = dcrow

    tok = lambda w: pl.BlockSpec((TB, w), lambda i: (nb - 1 - i, 0))
    return pl.pallas_call(
        kern, name=name, grid=(nb,),
        in_specs=[tok(MIX), tok(2 * MIX), tok(DN_H * C), tok(MIX), tok(MIX), tok(128), tok(MIX),
                  pl.BlockSpec((NB * SR, DN_HD), lambda i: (nb - 1 - i, 0))],
        out_specs=[tok(MIX), tok(MIX), tok(MIX), tok(128)],
        out_shape=[jax.ShapeDtypeStruct((S, MIX), F32)] * 3 + [jax.ShapeDtypeStruct((S, 128), F32)],
        scratch_shapes=[pltpu.VMEM((SR, DN_HD), F32)],
        compiler_params=_cparams(("arbitrary",)),
    )(do, sv["uw"], sv["at"], sv["qd"], sv["kd"], sv["dec"], sv["vn"], sv["st"])


def _dn_chunk_bwd(q, k, v, bg, sv, do, dvn, dw, dkd, dc, name):
    S = q.shape[0]
    C, NB = DN_C, _dn_nb(S)
    TB = NB * C
    SR = DN_H * DN_HD

    def kern(q_ref, k_ref, v_ref, bg_ref, t_ref, uw_ref, vn_ref, st_ref, do_ref, dvn_ref, dw_ref, dkd_ref, dc_ref,
             dq_ref, dk_ref, dv_ref, dbg_ref):
        lane = lax.broadcasted_iota(jnp.int32, (C, 128), 1)
        ri = lax.broadcasted_iota(jnp.int32, (C, C), 0)
        ci = lax.broadcasted_iota(jnp.int32, (C, C), 1)
        tril, eye, last = ri >= ci, ri == ci, ri[:, 0:1] == C - 1
        chains = [(cb, h) for cb in range(NB) for h in range(DN_H)]
        each = lambda f, *ls: [f(*a) for a in zip(*ls)]
        rsum = lambda t: jnp.sum(t, axis=-1, keepdims=True)
        rows = lambda cb: slice(cb * C, (cb + 1) * C)
        head = lambda h: slice(h * DN_HD, (h + 1) * DN_HD)
        tok = lambda ref: [ref[rows(cb), head(h)] for cb, h in chains]
        beta, decay, e_gc, e_kd, cdec = _dn_decay_terms([bg_ref[rows(cb), :] for cb, _ in chains],
                                                        [h for _, h in chains])
        qs, ks, vs, dos, vnew, d_kd = tok(q_ref), tok(k_ref), tok(v_ref), tok(do_ref), tok(vn_ref), tok(dkd_ref)
        s_in = [st_ref[cb * SR + h * DN_HD:cb * SR + (h + 1) * DN_HD, :] for cb, h in chains]
        d_c = [dc_ref[cb * C:cb * C + 1, h:h + 1] for cb, h in chains]
        kb = each(lambda a, b: a * b, ks, beta)
        kk = each(_mm_nt, kb, ks)
        attn = each(lambda a, b, d: _mm_nt(a, b) * d, qs, ks, decay)
        d_qd = each(_mm_nt, dos, s_in)
        d_attn = each(_mm_nt, dos, vnew)
        d_sol = [jnp.concatenate([dvn_ref[rows(cb), head(h)], dw_ref[rows(cb), head(h)]], axis=1) for cb, h in chains]
        sol = [jnp.concatenate([uw_ref[rows(cb), head(h)], uw_ref[rows(cb), MIX + h * DN_HD:MIX + (h + 1) * DN_HD]],
                               axis=1) for cb, h in chains]
        d_rhs = _dg3_many([t_ref[rows(cb), h * C:(h + 1) * C] for cb, h in chains], d_sol, 0, 0)
        d_a = _dg3_many(d_rhs, sol, 1, 1)
        d_kk = each(lambda a, d: jnp.where(ri > ci, -a, 0.0) * d, d_a, decay)
        d_qk = each(lambda a, d: a * d, d_attn, decay)
        dm = each(lambda a, b, c_, d: a * b + c_ * d, d_kk, kk, d_attn, attn)
        d_vb = [t[:, :DN_HD] for t in d_rhs]
        dz = [t[:, DN_HD:] for t in d_rhs]
        d_kb = each(lambda z, e, a, kh: z * e + _mm(a, kh), dz, e_gc, d_kk, ks)
        d_k = each(lambda a, b, c_, q: _mm_tn(a, b) + _mm_tn(c_, q), d_kk, kb, d_qk, qs)
        d_q = each(lambda a, kh, b, e: _mm(a, kh) + b * e, d_qk, ks, d_qd, e_gc)
        t_kd = each(lambda a, kh, e: rsum(a * kh * e), d_kd, ks, e_kd)
        d_gl = each(lambda t, c_, cd: jnp.sum(t, axis=0, keepdims=True) + c_ * cd, t_kd, d_c, cdec)
        d_gc = each(lambda z, a, e, m, b, q, t, gl:
                    rsum(z * a) * e + rsum(m) - rsum(jnp.where(eye, jnp.sum(m, axis=0, keepdims=True), 0.0))
                    + rsum(b * q) * e - t + jnp.where(last, gl, 0.0),
                    dz, kb, e_gc, dm, d_qd, qs, t_kd, d_gl)
        d_g = _dg_exact_lhs_many(ri <= ci, [jnp.broadcast_to(t, (C, 128)) for t in d_gc], 1, 0)
        d_beta = each(lambda a, v_, b, kh: rsum(a * v_) + rsum(b * kh), d_vb, vs, d_kb, ks)
        for n_, (cb, h) in enumerate(chains):
            dq_ref[rows(cb), head(h)] = d_q[n_]
            dk_ref[rows(cb), head(h)] = d_k[n_] + d_kd[n_] * e_kd[n_] + d_kb[n_] * beta[n_]
            dv_ref[rows(cb), head(h)] = d_vb[n_] * beta[n_]
        for cb in range(NB):
            dbg = jnp.zeros((C, 128), F32)
            for h in range(DN_H):
                n_ = cb * DN_H + h
                dbg = dbg + jnp.where(lane == h, d_beta[n_], 0.0) + jnp.where(lane == DN_H + h, d_g[n_], 0.0)
            dbg_ref[rows(cb), :] = dbg

    tok = lambda w: pl.BlockSpec((TB, w), lambda i: (i, 0))
    return pl.pallas_call(
        kern, name=name, grid=(S // TB,),
        in_specs=[tok(MIX), tok(MIX), tok(MIX), tok(128), tok(DN_H * C), tok(2 * MIX), tok(MIX),
                  pl.BlockSpec((NB * SR, DN_HD), lambda i: (i, 0)), tok(MIX), tok(MIX), tok(MIX), tok(MIX), tok(128)],
        out_specs=[tok(MIX), tok(MIX), tok(MIX), tok(128)],
        out_shape=[jax.ShapeDtypeStruct((S, MIX), F32)] * 3 + [jax.ShapeDtypeStruct((S, 128), F32)],
        compiler_params=_cparams(("parallel",)),
    )(q, k, v, bg, sv["tm"], sv["uw"], sv["vn"], sv["st"], do, dvn, dw, dkd, dc)


def _dn_core_bwd(q, k, v, bg, sv, do, name):
    dvn, dw, dkd, dc = _dn_scan_bwd(sv, do, name + "_scan")
    return _dn_chunk_bwd(q, k, v, bg, sv, do, dvn, dw, dkd, dc, name + "_chunk")


def _dn_post_fwd(o, proj, ng, name):
    S = o.shape[0]

    def body(i, o_ref, z_ref, g_ref, out_ref):
        gv = g_ref[...]
        for h in range(DN_H):
            sl = slice(h * DN_HD, (h + 1) * DN_HD)
            oh = o_ref[:, sl]
            r = lax.rsqrt(jnp.mean(oh * oh, axis=-1, keepdims=True) + EPS)
            out_ref[:, sl] = (oh * r * gv * _silu(z_ref[:, sl].astype(F32))).astype(BF16)

    return _tok_call(body, name, S, min(S, 512), [(o, MIX, 0), (proj, MIX, C_ZC // MIX)], [ng], [(MIX, BF16)])[0]


def _dn_post_bwd(o, proj, ng, dout, name):
    S = o.shape[0]

    def body(i, o_ref, z_ref, do_ref, g_ref, dov_ref, dz_ref, dg_ref):
        gv = g_ref[...]
        dg = jnp.zeros((1, DN_HD), F32)
        for h in range(DN_H):
            sl = slice(h * DN_HD, (h + 1) * DN_HD)
            oh, zh, dh = o_ref[:, sl], z_ref[:, sl].astype(F32), do_ref[:, sl].astype(F32)
            r = lax.rsqrt(jnp.mean(oh * oh, axis=-1, keepdims=True) + EPS)
            dz_ref[:, sl] = (dh * oh * r * gv * _dsilu(zh)).astype(BF16)
            dx, dgh = _rms_bwd_vals(oh, gv, dh * _silu(zh))
            dov_ref[:, sl] = dx
            dg = dg + dgh
        _acc(dg_ref, dg, i)

    return _tok_call(body, name, S, min(S, 512), [(o, MIX, 0), (proj, MIX, C_ZC // MIX), (dout, MIX, 0)], [ng],
                     [(MIX, F32), (MIX, BF16)], [((1, DN_HD), F32)])


def _layer_params(w, big, l):
    lane = jnp.arange(128)
    is_g = (lane >= DN_H) & (lane < 2 * DN_H)
    spread = lambda t: jnp.where(is_g, jnp.tile(t, 128 // DN_H), 0.0).reshape(1, 128)
    tril = jnp.tril(jnp.ones((SGU_T, SGU_T), bool))
    return dict(
        win=big["w_in"], wb=big["w_branch"], wout=big["w_out"], wgu=big["w_gate_up"], wdown=big["w_down"],
        conv=w["dn_conv_w"][l], attn_norm=w["attn_norm"][l].reshape(1, -1), ffn_norm=w["ffn_norm"][l].reshape(1, -1),
        lg=w["sgu_ln_g"][l].reshape(1, -1), lb=w["sgu_ln_b"][l].reshape(1, -1),
        wc=jnp.where(tril, w["sgu_w"][l], 0.0).reshape(SGU_G * SGU_T, SGU_T), bst=w["sgu_b"][l].T,
        sinks=w["attn_sinks"][l].reshape(1, -1), alog=spread(w["dn_a_log"][l]), dtb=spread(w["dn_dt_bias"][l]),
        ng=w["dn_norm"][l].reshape(1, -1))


def _layer_fwd(x, p, cos, sin, l):
    n = lambda s: f"l{l}_{s}"
    h = _rms_fwd(x, p["attn_norm"], n("rms1"))
    proj = _matmul(h, p["win"], out_dtype=BF16, name=n("mm_in"))
    out_a = _sgu_fwd(proj, p["lg"], p["lb"], p["wc"], p["bst"], n("sgu_fwd"))
    qr, kr, vr = _rope_fwd(proj, cos, sin, n("rope_fwd"))
    out_b = _swa_fwd(qr, kr, vr, p["sinks"], n("swa_fwd"))
    q, k, v, bg = _dn_pre_fwd(proj, p["conv"], p["alog"], p["dtb"], n("dn_pre_fwd"))
    o, dn = _dn_core_fwd(q, k, v, bg, n("dn_core_fwd"))
    out_c = _dn_post_fwd(o, proj, p["ng"], n("dn_post_fwd"))
    outs = (out_a, out_b, out_c)
    bds = [_matmul(outs[j], p["wb"][j], out_dtype=BF16, name=n(f"mm_branch{j}")) for j in range(3)]
    merged = _merge_fwd(proj, bds, n("merge_fwd"))
    x1 = _matmul(merged, p["wout"], add=x, name=n("mm_out"))
    h2 = _rms_fwd(x1, p["ffn_norm"], n("rms2"))
    gu = _matmul(h2, p["wgu"], out_dtype=BF16, name=n("mm_gu"))
    act = _swiglu_fwd(gu, n("swiglu_fwd"))
    x2 = _matmul(act, p["wdown"], add=x1, name=n("mm_down"))
    saved = dict(x=x, h=h, proj=proj, outs=outs, qr=qr, kr=kr, vr=vr, q=q, k=k, v=v, bg=bg, o=o, dn=dn, bds=bds,
                 merged=merged, x1=x1, h2=h2, gu=gu, act=act)
    return x2, saved


def _layer_bwd(dx2, s, p, cos, sin, l):
    n = lambda t: f"l{l}_{t}"
    proj = s["proj"]
    g = {}
    g["w_down"] = _matmul(s["act"], dx2, ta=True, out_dtype=BF16, name=n("wg_down"))
    dact = _matmul(dx2, p["wdown"], tb=True, out_dtype=BF16, name=n("dg_down"))
    dgu = _swiglu_bwd(s["gu"], dact, n("swiglu_bwd"))
    g["w_gate_up"] = _matmul(s["h2"], dgu, ta=True, out_dtype=BF16, name=n("wg_gu"))
    dh2 = _matmul(dgu, p["wgu"], tb=True, name=n("dg_gu"))
    dx1, g["ffn_norm"] = _rms_bwd_add(s["x1"], p["ffn_norm"], dh2, dx2, n("rms2_bwd"))
    g["w_out"] = _matmul(s["merged"], dx1, ta=True, out_dtype=BF16, name=n("wg_out"))
    dm = _matmul(dx1, p["wout"], tb=True, name=n("dg_out"))
    dbd0, dbd1, dbd2, dgp = _merge_bwd(proj, s["bds"], dm, n("merge_bwd"))
    dbds = (dbd0, dbd1, dbd2)
    g["w_branch"] = jnp.stack([_matmul(s["outs"][j], dbds[j], ta=True, out_dtype=BF16, name=n(f"wg_branch{j}"))
                               for j in range(3)])
    douts = [_matmul(dbds[j], p["wb"][j], tb=True, name=n(f"dg_branch{j}")) for j in range(3)]
    dua, dva, g["sgu_ln_g"], g["sgu_ln_b"], dwc, dbs = _sgu_bwd(proj, p["lg"], p["lb"], p["wc"], p["bst"], douts[0],
                                                                n("sgu_bwd"))
    g["sgu_w"] = dwc.reshape(SGU_G, SGU_T, SGU_T)
    g["sgu_b"] = dbs.T
    dqr, dkr, dvr, dsink = _swa_bwd(s["qr"], s["kr"], s["vr"], p["sinks"], douts[1], n("swa_bwd"))
    g["attn_sinks"] = dsink[0, :SWA_H]
    dqb, dkb, dvb = _rope_bwd(dqr, dkr, dvr, cos, sin, n("rope_bwd"))
    do, dz, dng = _dn_post_bwd(s["o"], proj, p["ng"], douts[2], n("dn_post_bwd"))
    g["dn_norm"] = dng[0]
    dq, dk, dv, dbg = _dn_core_bwd(s["q"], s["k"], s["v"], s["bg"], s["dn"], do, n("dn_core_bwd"))
    dpre, dsm, g["dn_conv_w"], dal, ddb = _dn_pre_bwd1(proj, p["conv"], p["alog"], p["dtb"], dq, dk, dv, dbg,
                                                       n("dn_pre_bwd1"))
    g["dn_a_log"] = dal[0, DN_H:2 * DN_H]
    g["dn_dt_bias"] = ddb[0, DN_H:2 * DN_H]
    dqkv = _dn_pre_bwd2(dpre, p["conv"], n("dn_pre_bwd2"))
    dproj = jnp.concatenate([dgp, dqkv, dua, dva, dqb, dz, dkb, dvb, dsm], axis=1)
    g["w_in"] = _matmul(s["h"], dproj, ta=True, out_dtype=BF16, name=n("wg_in"))
    dh = _matmul(dproj, p["win"], tb=True, name=n("dg_in"))
    dx, g["attn_norm"] = _rms_bwd_add(s["x"], p["attn_norm"], dh, dx1, n("rms1_bwd"))
    g["attn_norm"], g["ffn_norm"] = g["attn_norm"][0], g["ffn_norm"][0]
    g["sgu_ln_g"], g["sgu_ln_b"] = g["sgu_ln_g"][0], g["sgu_ln_b"][0]
    return dx, g


def _local_step(x, positions, target, w, big_of_layer, on_grads):
    cos, sin = _rope_tables(positions)
    params, saves, xs = [], [], x
    for l in range(DEPTH):
        params.append(_layer_params(w, big_of_layer(l, xs), l))
        xs, sv = _layer_fwd(xs, params[l], cos, sin, l)
        saves.append(sv)
    dx, loss_row, dgf = _final_loss(xs, w["final_norm"].reshape(1, -1), target)
    grads = [None] * DEPTH
    for l in reversed(range(DEPTH)):
        dx, grads[l] = _layer_bwd(dx, saves[l], params[l], cos, sin, l)
        token = on_grads(l, {k: grads[l].pop(k) for k in BIG})
        if token is not None and l > 0:
            params[l - 1] = dict(params[l - 1], ffn_norm=params[l - 1]["ffn_norm"] + token[0, 0])
    stacked = {k: jnp.stack([grads[l][k] for l in range(DEPTH)]) for k in grads[0]}
    stacked["final_norm"] = dgf[0]
    return loss_row[0, 0], dx, stacked


MESH = pl.DeviceIdType.MESH
HBM_SPEC = pl.BlockSpec(memory_space=pltpu.HBM)
VMEM_SPEC = pl.BlockSpec(memory_space=pltpu.VMEM)
N_CHIPS = 4
FLIPS = tuple((fx, fy, fc) for fx in (0, 1) for fy in (0, 1) for fc in (0, 1))[1:]
BIG = ("w_in", "w_branch", "w_out", "w_gate_up", "w_down")
BIG_SPEC = {
    "w_in": dict(rows=1024, cols=1792, axis=1, keep=1730, down=8),
    "w_branch": dict(rows=1536, cols=256, axis=1, keep=256, down=2),
    "w_out": dict(rows=256, cols=1024, axis=0, keep=1024, down=1),
    "w_gate_up": dict(rows=1024, cols=1408, axis=1, keep=1408, down=8),
    "w_down": dict(rows=704, cols=1024, axis=0, keep=1024, down=4),
}
CONV_ROWS, CONV_COLS = DEPTH * DN_CONV, 3 * MIX // N_CHIPS


def _full_shape(k):
    sp = BIG_SPEC[k]
    return (sp["rows"], N_CHIPS * sp["cols"]) if sp["axis"] == 1 else (N_CHIPS * sp["rows"], sp["cols"])


def _me():
    return lax.axis_index("x"), lax.axis_index("y"), lax.axis_index("c")


def _peer(x, y, c, flip):
    fx, fy, fc = flip
    return (1 - x if fx else x, 1 - y if fy else y, 1 - c if fc else c)


class _Copies:
    def __init__(self, send_sems, recv_sems):
        self.send_sems, self.recv_sems, self.k, self.sent, self.landing = send_sems, recv_sems, 0, [], []

    def _copy(self, k, src, dst, to):
        return pltpu.make_async_remote_copy(src_ref=src, dst_ref=dst, send_sem=self.send_sems.at[k],
                                            recv_sem=self.recv_sems.at[k], device_id=to, device_id_type=MESH)

    def send(self, src, dst, to, lands):
        k = self.k
        self.k += 1
        cp = self._copy(k, src, dst, to)
        cp.start()
        self.sent.append(cp)
        self.landing.append(self._copy(k, lands, lands, to))
        return k

    def wait_landed(self, k):
        self.landing[k].wait_recv()

    def finish(self, landed=()):
        for k, cp in enumerate(self.landing):
            if k not in landed:
                cp.wait_recv()
        for cp in self.sent:
            cp.wait_send()


def _place_shard(shard, k, chip, layer, name):
    sp = BIG_SPEC[k]
    rows, cols, keep = sp["rows"], sp["cols"], sp["keep"]
    tr = _pick(rows, (256, 64))
    nb = rows // tr
    if sp["axis"] == 1:
        out_spec = pl.BlockSpec((tr, cols), lambda i, ch: (i, ch[0]))
    else:
        out_spec = pl.BlockSpec((tr, cols), lambda i, ch: (ch[0] * nb + i, 0))

    def kern(ch_ref, x_ref, o_ref):
        v = x_ref[0].astype(BF16)
        if keep == cols:
            o_ref[...] = v
        else:
            o_ref[:, :keep] = v
            o_ref[:, keep:] = jnp.zeros((tr, cols - keep), BF16)

    return pl.pallas_call(
        kern, name=name, out_shape=jax.ShapeDtypeStruct(_full_shape(k), BF16),
        grid_spec=pltpu.PrefetchScalarGridSpec(
            num_scalar_prefetch=1, grid=(nb,),
            in_specs=[pl.BlockSpec((1, tr, keep), lambda i, ch: (layer, i, 0))], out_specs=out_spec),
        compiler_params=_cparams(("parallel",)),
    )(chip, shard)


def _half_block(ref, k, s, half):
    sp = BIG_SPEC[k]
    hr = sp["rows"] // 2
    if sp["axis"] == 1:
        return ref.at[pl.ds(pl.multiple_of(half * hr, 16), hr), pl.ds(pl.multiple_of(s * sp["cols"], 128), sp["cols"])]
    return ref.at[pl.ds(pl.multiple_of(s * sp["rows"] + half * hr, 16), hr), :]


def _other_chips(x, y):
    return [(1 - x, y), (x, 1 - y), (1 - x, 1 - y)]


def _gather_layer(placed, conv):
    n = len(BIG)
    n_sem = 6 * n + 3

    def body(*refs):
        conv_ref = refs[n]
        out = dict(zip(BIG, refs[n + 1:2 * n + 1]))
        conv_out, send_sems, recv_sems, local_sem = refs[2 * n + 1:]
        x, y, c = _me()
        me = 2 * x + y
        chips = _other_chips(x, y)
        net = _Copies(send_sems, recv_sems)

        def conv_block(s):
            return conv_out.at[:, pl.ds(pl.multiple_of(s * CONV_COLS, 128), CONV_COLS)]

        local = pltpu.make_async_copy(conv_ref, conv_block(me), local_sem)
        local.start()
        first = {}
        for k in BIG:
            for j, (px, py) in enumerate(chips):
                first[k, j] = net.send(_half_block(out[k], k, me, c), _half_block(out[k], k, me, c), (px, py, c),
                                       _half_block(out[k], k, 2 * px + py, c))
        for px, py in chips:
            net.send(conv_ref, conv_block(me), (px, py, c), conv_block(2 * px + py))
        for k in BIG:
            for j, (px, py) in enumerate(chips):
                net.wait_landed(first[k, j])
                net.send(_half_block(out[k], k, 2 * px + py, c), _half_block(out[k], k, 2 * px + py, c), (x, y, 1 - c),
                         _half_block(out[k], k, 2 * px + py, 1 - c))
        net.finish(landed=set(first.values()))
        local.wait()

    out_shape = [jax.ShapeDtypeStruct(_full_shape(k), BF16) for k in BIG]
    out_shape.append(jax.ShapeDtypeStruct((CONV_ROWS, N_CHIPS * CONV_COLS), F32))
    outs = pl.pallas_call(
        body, name="gather_layer", out_shape=out_shape, in_specs=[HBM_SPEC] * (n + 1), out_specs=[HBM_SPEC] * (n + 1),
        input_output_aliases={i: i for i in range(n)},
        scratch_shapes=[pltpu.SemaphoreType.DMA((n_sem,)), pltpu.SemaphoreType.DMA((n_sem,)), pltpu.SemaphoreType.DMA],
    )(*[placed[k] for k in BIG], conv)
    return dict(zip(BIG, outs[:n])), outs[n]


SEM_SPEC = pl.BlockSpec(memory_space=pltpu.SEMAPHORE)
N_BEHIND = 3 * len(BIG)


def _behind_copies(arrs, send_sems, recv_sems):
    x, y, c = _me()
    copies = []
    for i, k in enumerate(BIG):
        for j, (px, py) in enumerate(_other_chips(x, y)):
            copies.append(pltpu.make_async_remote_copy(
                src_ref=_half_block(arrs[k], k, 2 * x + y, c), dst_ref=_half_block(arrs[k], k, 2 * x + y, c),
                send_sem=send_sems.at[3 * i + j], recv_sem=recv_sems.at[3 * i + j], device_id=(px, py, c),
                device_id_type=MESH))
    return copies


def _gather_start(placed):
    n = len(BIG)

    def body(*refs):
        arrs = dict(zip(BIG, refs[n + 2:2 * n + 2]))
        send_sems, recv_sems = refs[n], refs[n + 1]
        for cp in _behind_copies(arrs, send_sems, recv_sems):
            cp.start()
        refs[2 * n + 2][...] = jnp.zeros((8, 128), F32)

    outs = pl.pallas_call(
        body, name="gather_start",
        out_shape=(pltpu.SemaphoreType.DMA((N_BEHIND,)), pltpu.SemaphoreType.DMA((N_BEHIND,)),
                   *[pltpu.HBM(_full_shape(k), BF16) for k in BIG], jax.ShapeDtypeStruct((8, 128), F32)),
        in_specs=[HBM_SPEC] * n, out_specs=(SEM_SPEC, SEM_SPEC, *[HBM_SPEC] * n, VMEM_SPEC),
        input_output_aliases={i: i + 2 for i in range(n)},
        compiler_params=pltpu.CompilerParams(has_side_effects=pltpu.SideEffectType.DATAFLOW_SIDE_EFFECTING),
    )(*[pltpu.with_memory_space_constraint(placed[k], pltpu.HBM) for k in BIG])
    return outs[0], outs[1], dict(zip(BIG, outs[2:n + 2])), outs[n + 2]


def _gather_wait(send_sems, recv_sems, inflight, after):
    n = len(BIG)

    def body(*refs):
        arrs = dict(zip(BIG, refs[:n]))
        for cp in _behind_copies(arrs, refs[n], refs[n + 1]):
            cp.wait_send()
            cp.wait_recv()

    outs = pl.pallas_call(
        body, name="gather_wait", out_shape=tuple(pltpu.HBM(_full_shape(k), BF16) for k in BIG),
        in_specs=[HBM_SPEC] * n + [SEM_SPEC, SEM_SPEC, pl.BlockSpec(memory_space=pl.ANY)], out_specs=(HBM_SPEC,) * n,
        input_output_aliases={i: i for i in range(n)},
        compiler_params=pltpu.CompilerParams(has_side_effects=pltpu.SideEffectType.DATAFLOW_SIDE_EFFECTING),
    )(*[inflight[k] for k in BIG], send_sems, recv_sems, after)
    return dict(zip(BIG, outs))


def _gather_finish(arrs):
    n = len(BIG)

    def body(*refs):
        out = dict(zip(BIG, refs[n:2 * n]))
        send_sems, recv_sems = refs[2 * n:]
        x, y, c = _me()
        net = _Copies(send_sems, recv_sems)
        for k in BIG:
            for px, py in _other_chips(x, y):
                net.send(_half_block(out[k], k, 2 * px + py, c), _half_block(out[k], k, 2 * px + py, c), (x, y, 1 - c),
                         _half_block(out[k], k, 2 * px + py, 1 - c))
        net.finish()

    outs = pl.pallas_call(
        body, name="gather_finish", out_shape=[jax.ShapeDtypeStruct(_full_shape(k), BF16) for k in BIG],
        in_specs=[HBM_SPEC] * n, out_specs=[HBM_SPEC] * n, input_output_aliases={i: i for i in range(n)},
        scratch_shapes=[pltpu.SemaphoreType.DMA((N_BEHIND,)), pltpu.SemaphoreType.DMA((N_BEHIND,))],
    )(*[arrs[k] for k in BIG])
    return dict(zip(BIG, outs))


def _row_chunks(ref, rows, n):
    step = rows // n
    return [ref.at[pl.ds(i * step, step), :] for i in range(n)]


def _half_pieces(ref, k, half):
    sp = BIG_SPEC[k]
    hr = sp["rows"] // 2
    if sp["axis"] == 1:
        return [ref.at[pl.ds(pl.multiple_of(half * hr, 16), hr), :]]
    return [ref.at[pl.ds(pl.multiple_of(s * sp["rows"] + half * hr, 16), hr), :] for s in range(N_CHIPS)]


def _half_shape(k):
    rows, cols = _full_shape(k)
    return rows // 2, cols


def _stacked_pieces(ref, k):
    sp = BIG_SPEC[k]
    hr = sp["rows"] // 2
    return [ref] if sp["axis"] == 1 else [ref.at[pl.ds(s * hr, hr), :] for s in range(N_CHIPS)]


def _chip_part(ref, k, s):
    sp = BIG_SPEC[k]
    hr = sp["rows"] // 2
    if sp["axis"] == 1:
        return ref.at[:, pl.ds(pl.multiple_of(s * sp["cols"], 128), sp["cols"])]
    return ref.at[pl.ds(pl.multiple_of(s * hr, 16), hr), :]


def _halves_to_sibling(grads, name):
    n = len(BIG)
    chunks = {k: max(BIG_SPEC[k]["down"] // 2, 1) if BIG_SPEC[k]["axis"] == 1 else 1 for k in BIG}
    n_sem = sum(chunks[k] if BIG_SPEC[k]["axis"] == 1 else N_CHIPS for k in BIG)

    def body(*refs):
        g = dict(zip(BIG, refs[:n]))
        out = dict(zip(BIG, refs[n:2 * n]))
        send_sems, recv_sems = refs[2 * n:]
        x, y, c = _me()
        net = _Copies(send_sems, recv_sems)
        for k in BIG:
            hr = BIG_SPEC[k]["rows"] // 2
            for src, dst in zip(_half_pieces(g[k], k, 1 - c), _stacked_pieces(out[k], k)):
                for s, d in zip(_row_chunks(src, hr, chunks[k]), _row_chunks(dst, hr, chunks[k])):
                    net.send(s, d, (x, y, 1 - c), d)
        net.finish()

    outs = pl.pallas_call(
        body, name=name, out_shape=[jax.ShapeDtypeStruct(_half_shape(k), BF16) for k in BIG],
        in_specs=[HBM_SPEC] * n, out_specs=[HBM_SPEC] * n,
        scratch_shapes=[pltpu.SemaphoreType.DMA((n_sem,)), pltpu.SemaphoreType.DMA((n_sem,))],
    )(*[grads[k] for k in BIG])
    return dict(zip(BIG, outs))


def _add_half(g, other, k, core, name):
    sp = BIG_SPEC[k]
    hr, cols = sp["rows"] // 2, _full_shape(k)[1]
    tr = _pick(hr, (256, 352, 128))
    nb = hr // tr
    if sp["axis"] == 1:
        grid = (nb,)
        g_spec = pl.BlockSpec((tr, cols), lambda i, c: (c[0] * nb + i, 0))
        h_spec = pl.BlockSpec((tr, cols), lambda i, c: (i, 0))
    else:
        grid = (N_CHIPS, nb)
        g_spec = pl.BlockSpec((tr, cols), lambda s, i, c: ((2 * s + c[0]) * nb + i, 0))
        h_spec = pl.BlockSpec((tr, cols), lambda s, i, c: (s * nb + i, 0))

    def kern(c_ref, a_ref, b_ref, o_ref):
        o_ref[...] = (a_ref[...].astype(F32) + b_ref[...].astype(F32)).astype(BF16)

    return pl.pallas_call(
        kern, name=name, out_shape=jax.ShapeDtypeStruct(_half_shape(k), BF16),
        grid_spec=pltpu.PrefetchScalarGridSpec(num_scalar_prefetch=1, grid=grid, in_specs=[g_spec, h_spec],
                                               out_specs=h_spec),
        compiler_params=_cparams(("parallel",) * len(grid)),
    )(core, g, other)


def _part_shape(k):
    return N_CHIPS - 1, BIG_SPEC[k]["rows"] // 2, BIG_SPEC[k]["cols"]


def _scatter_chip_sums(sums, name):
    n = len(BIG)

    def body(*refs):
        src = dict(zip(BIG, refs[:n]))
        out = dict(zip(BIG, refs[n:2 * n]))
        send_sems, recv_sems = refs[2 * n:]
        x, y, c = _me()
        net = _Copies(send_sems, recv_sems)
        for k in BIG:
            for j, (px, py) in enumerate(_other_chips(x, y)):
                net.send(_chip_part(src[k], k, 2 * px + py), out[k].at[j], (px, py, c), out[k].at[j])
        net.finish()

    outs = pl.pallas_call(
        body, name=name, out_shape=[jax.ShapeDtypeStruct(_part_shape(k), BF16) for k in BIG],
        in_specs=[HBM_SPEC] * n, out_specs=[HBM_SPEC] * n,
        scratch_shapes=[pltpu.SemaphoreType.DMA((N_BEHIND,)), pltpu.SemaphoreType.DMA((N_BEHIND,))],
    )(*[sums[k] for k in BIG])
    return dict(zip(BIG, outs))


def _scatter_copies(sums, parts, send_sems, recv_sems):
    x, y, c = _me()
    copies = []
    for i, k in enumerate(BIG):
        for j, (px, py) in enumerate(_other_chips(x, y)):
            copies.append(pltpu.make_async_remote_copy(
                src_ref=_chip_part(sums[k], k, 2 * px + py), dst_ref=parts[k].at[j], send_sem=send_sems.at[3 * i + j],
                recv_sem=recv_sems.at[3 * i + j], device_id=(px, py, c), device_id_type=MESH))
    return copies


def _scatter_start(sums):
    n = len(BIG)
    lands = [pltpu.with_memory_space_constraint(lax.empty(_part_shape(k), BF16), pltpu.HBM) for k in BIG]

    def body(*refs):
        outs = refs[2 * n + 2:4 * n + 2]
        for cp in _scatter_copies(dict(zip(BIG, outs[:n])), dict(zip(BIG, outs[n:])), refs[2 * n], refs[2 * n + 1]):
            cp.start()
        refs[4 * n + 2][...] = jnp.zeros((8, 128), F32)

    outs = pl.pallas_call(
        body, name="scatter_start",
        out_shape=(pltpu.SemaphoreType.DMA((N_BEHIND,)), pltpu.SemaphoreType.DMA((N_BEHIND,)),
                   *[pltpu.HBM(_half_shape(k), BF16) for k in BIG], *[pltpu.HBM(_part_shape(k), BF16) for k in BIG],
                   jax.ShapeDtypeStruct((8, 128), F32)),
        in_specs=[HBM_SPEC] * (2 * n), out_specs=(SEM_SPEC, SEM_SPEC, *[HBM_SPEC] * (2 * n), VMEM_SPEC),
        input_output_aliases={i: i + 2 for i in range(2 * n)},
        compiler_params=pltpu.CompilerParams(has_side_effects=pltpu.SideEffectType.DATAFLOW_SIDE_EFFECTING),
    )(*[pltpu.with_memory_space_constraint(sums[k], pltpu.HBM) for k in BIG], *lands)
    return outs[0], outs[1], outs[2:2 * n + 2], outs[2 * n + 2]


def _scatter_wait(send_sems, recv_sems, inflight, after):
    n = len(BIG)

    def body(*refs):
        for cp in _scatter_copies(dict(zip(BIG, refs[:n])), dict(zip(BIG, refs[n:2 * n])), refs[2 * n], refs[2 * n + 1]):
            cp.wait_send()
            cp.wait_recv()

    outs = pl.pallas_call(
        body, name="scatter_wait",
        out_shape=(*[pltpu.HBM(_half_shape(k), BF16) for k in BIG], *[pltpu.HBM(_part_shape(k), BF16) for k in BIG]),
        in_specs=[HBM_SPEC] * (2 * n) + [SEM_SPEC, SEM_SPEC, pl.BlockSpec(memory_space=pl.ANY)],
        out_specs=(HBM_SPEC,) * (2 * n), input_output_aliases={i: i for i in range(2 * n)},
        compiler_params=pltpu.CompilerParams(has_side_effects=pltpu.SideEffectType.DATAFLOW_SIDE_EFFECTING),
    )(*inflight, send_sems, recv_sems, after)
    return dict(zip(BIG, outs[:n])), dict(zip(BIG, outs[n:]))


def _sum_half(parts, own, k, where, layer, into, name):
    sp = BIG_SPEC[k]
    rows, cols, keep = sp["rows"], sp["cols"], sp["keep"]
    hr = rows // 2
    tr = _pick(hr, (256, 352, 128))
    nb = hr // tr
    if sp["axis"] == 1:
        own_spec = pl.BlockSpec((tr, cols), lambda i, w: (i, w[0]))
    else:
        own_spec = pl.BlockSpec((tr, cols), lambda i, w: (w[0] * nb + i, 0))

    def kern(w_ref, p_ref, own_ref, *rest):
        tot = own_ref[...].astype(F32)
        for j in range(N_CHIPS - 1):
            tot = tot + p_ref[j].astype(F32)
        rest[-1][0] = tot[:, :keep]

    in_specs = [pl.BlockSpec((N_CHIPS - 1, tr, cols), lambda i, w: (0, i, 0)), own_spec]
    args = [where, parts, own]
    if into is not None:
        in_specs.append(pl.BlockSpec(memory_space=pl.ANY))
        args.append(into)
    return pl.pallas_call(
        kern, name=name, out_shape=jax.ShapeDtypeStruct((DEPTH, rows, keep), F32),
        grid_spec=pltpu.PrefetchScalarGridSpec(
            num_scalar_prefetch=1, grid=(nb,), in_specs=in_specs,
            out_specs=pl.BlockSpec((1, tr, keep), lambda i, w: (layer, w[1] * nb + i, 0))),
        input_output_aliases={} if into is None else {3: 0},
        compiler_params=_cparams(("parallel",)),
    )(*args)


def _exchange_halves(red):
    n = len(BIG)

    def body(*refs):
        out = dict(zip(BIG, refs[n:2 * n]))
        send_sems, recv_sems = refs[2 * n:]
        x, y, c = _me()
        net = _Copies(send_sems, recv_sems)
        for k in BIG:
            hr = BIG_SPEC[k]["rows"] // 2
            for l in range(DEPTH):
                mine = out[k].at[l, pl.ds(pl.multiple_of(c * hr, 8), hr), :]
                theirs = out[k].at[l, pl.ds(pl.multiple_of((1 - c) * hr, 8), hr), :]
                net.send(mine, mine, (x, y, 1 - c), theirs)
        net.finish()

    outs = pl.pallas_call(
        body, name="exchange_halves",
        out_shape=[jax.ShapeDtypeStruct((DEPTH, BIG_SPEC[k]["rows"], BIG_SPEC[k]["keep"]), F32) for k in BIG],
        in_specs=[HBM_SPEC] * n, out_specs=[HBM_SPEC] * n, input_output_aliases={i: i for i in range(n)},
        scratch_shapes=[pltpu.SemaphoreType.DMA((DEPTH * n,)), pltpu.SemaphoreType.DMA((DEPTH * n,))],
    )(*[red[k] for k in BIG])
    return dict(zip(BIG, outs))


def _adam_vals(g, w, m, v):
    m2 = ADAM_B1 * m + (1.0 - ADAM_B1) * g
    v2 = ADAM_B2 * v + (1.0 - ADAM_B2) * (g * g)
    m_hat = m2 / (1.0 - ADAM_B1 ** ADAM_STEP)
    v_hat = v2 / (1.0 - ADAM_B2 ** ADAM_STEP)
    return -ADAM_LR * (m_hat / (jnp.sqrt(v_hat) + ADAM_EPS) + ADAM_WD * w), m2, v2


def _allreduce_small_adam(groups):
    ng = len(groups)

    def body(*refs):
        ins = [refs[4 * i:4 * i + 4] for i in range(ng)]
        outs = [refs[4 * ng + 4 * i:4 * ng + 4 * i + 4] for i in range(ng)]
        bufs = refs[8 * ng:9 * ng]
        send_sems, recv_sems = refs[9 * ng:]
        x, y, c = _me()
        me = 4 * x + 2 * y + c
        net = _Copies(send_sems, recv_sems)
        for (g_ref, _, _, _), buf in zip(ins, bufs):
            buf[me] = g_ref[...]
            for f in FLIPS:
                px, py, pc = _peer(x, y, c, f)
                net.send(g_ref, buf.at[me], (px, py, pc), buf.at[4 * px + 2 * py + pc])
        net.finish()
        for (_, w_ref, m_ref, v_ref), (gs_ref, d_ref, nm_ref, nv_ref), buf in zip(ins, outs, bufs):
            tot = buf[0]
            for d in range(1, 8):
                tot = tot + buf[d]
            gs_ref[...] = tot
            d_ref[...], nm_ref[...], nv_ref[...] = _adam_vals(tot, w_ref[...], m_ref[...], v_ref[...])

    shapes = [jax.ShapeDtypeStruct(g[0].shape, F32) for g in groups for _ in range(4)]
    outs = pl.pallas_call(
        body, name="allreduce_small", out_shape=shapes, in_specs=[VMEM_SPEC] * (4 * ng), out_specs=[VMEM_SPEC] * (4 * ng),
        scratch_shapes=[pltpu.VMEM((8,) + g[0].shape, F32) for g in groups]
        + [pltpu.SemaphoreType.DMA((7 * ng,)), pltpu.SemaphoreType.DMA((7 * ng,))],
        compiler_params=pltpu.CompilerParams(vmem_limit_bytes=VMEM_LIMIT),
    )(*[t for g in groups for t in g])
    return [outs[4 * i:4 * i + 4] for i in range(ng)]


def _adam(g, w, m, v, name):
    shape = w.shape
    lead, rows, cols = math.prod(shape[:-2]), shape[-2], shape[-1]
    tr = _pick(rows, (256, 352, 64, 8, rows))
    spec = pl.BlockSpec((1, tr, cols), lambda l, i: (l, i, 0))

    def kern(g_ref, w_ref, m_ref, v_ref, d_ref, nm_ref, nv_ref):
        d_ref[...], nm_ref[...], nv_ref[...] = _adam_vals(g_ref[...], w_ref[...], m_ref[...], v_ref[...])

    outs = pl.pallas_call(
        kern, name=name, grid=(lead, rows // tr), in_specs=[spec] * 4, out_specs=[spec] * 3,
        out_shape=[jax.ShapeDtypeStruct((lead, rows, cols), F32)] * 3, compiler_params=_cparams(("parallel", "parallel")),
    )(*[t.reshape(lead, rows, cols) for t in (g, w, m, v)])
    return [o.reshape(shape) for o in outs]


SMALL = ("attn_norm", "sgu_ln_g", "sgu_ln_b", "sgu_w", "sgu_b", "attn_sinks", "dn_a_log", "dn_dt_bias", "dn_norm",
         "ffn_norm", "final_norm")


def _tile_rows(n):
    return -(-n // 1024) * 8


SLAB = tuple(k for k in SMALL if k != "sgu_w")


def _pack_small(vals, extra=()):
    tiles = []
    for t in [vals[k] for k in SLAB] + list(extra):
        flat = t.astype(F32).reshape(-1)
        rows = _tile_rows(flat.shape[0])
        tiles.append(jnp.pad(flat, (0, rows * 128 - flat.shape[0])).reshape(rows, 128))
    return jnp.concatenate(tiles, axis=0)


def _unpack_small(slab, shapes, extra_shapes=()):
    out, extras, o = {}, [], 0
    for k, shp in [(k, shapes[k]) for k in SLAB] + [(None, s) for s in extra_shapes]:
        n = math.prod(shp)
        rows = _tile_rows(n)
        t = slab[o:o + rows].reshape(-1)[:n].reshape(shp)
        o += rows
        if k is None:
            extras.append(t)
        else:
            out[k] = t
    return out, extras


def _in_col_segments():
    shard, padded = IN_COLS // N_CHIPS, BIG_SPEC["w_in"]["cols"]
    segs, mine = [], 0
    for a, n in IN_PIECES:
        o = a
        while o < a + n:
            end = min(a + n, (o // shard + 1) * shard)
            segs.append(((o // shard) * padded + o % shard, mine + o - a, end - o))
            o = end
        mine += n
    return segs


def _move_cols(x, segs, out_cols, name):
    layers, rows, cols = x.shape
    tr = _pick(rows, (256, rows))
    gaps, at = [], 0
    for d, w in sorted((d, w) for _, d, w in segs):
        if d > at:
            gaps.append((at, d - at))
        at = d + w
    if at < out_cols:
        gaps.append((at, out_cols - at))

    def kern(x_ref, o_ref):
        for s, d, w in segs:
            o_ref[0, :, d:d + w] = x_ref[0, :, s:s + w]
        for d, w in gaps:
            o_ref[0, :, d:d + w] = jnp.zeros((tr, w), x.dtype)

    return pl.pallas_call(
        kern, name=name, grid=(layers, rows // tr), in_specs=[pl.BlockSpec((1, tr, cols), lambda l, i: (l, i, 0))],
        out_specs=pl.BlockSpec((1, tr, out_cols), lambda l, i: (l, i, 0)),
        out_shape=jax.ShapeDtypeStruct((layers, rows, out_cols), x.dtype), compiler_params=_cparams(("parallel", "parallel")),
    )(x)


WEIGHTS = ("attn_norm", "w_in", "sgu_ln_g", "sgu_ln_b", "sgu_w", "sgu_b", "attn_sinks", "dn_conv_w", "dn_a_log",
           "dn_dt_bias", "dn_norm", "w_branch", "w_out", "ffn_norm", "w_gate_up", "w_down", "final_norm")


def kernel(x, positions, attn_norm, w_in, sgu_ln_g, sgu_ln_b, sgu_w, sgu_b, attn_sinks, dn_conv_w, dn_a_log, dn_dt_bias, dn_norm, w_branch, w_out, ffn_norm, w_gate_up, w_down, final_norm, loss_target, m_attn_norm, m_w_in, m_sgu_ln_g, m_sgu_ln_b, m_sgu_w, m_sgu_b, m_attn_sinks, m_dn_conv_w, m_dn_a_log, m_dn_dt_bias, m_dn_norm, m_w_branch, m_w_out, m_ffn_norm, m_w_gate_up, m_w_down, m_final_norm, v_attn_norm, v_w_in, v_sgu_ln_g, v_sgu_ln_b, v_sgu_w, v_sgu_b, v_attn_sinks, v_dn_conv_w, v_dn_a_log, v_dn_dt_bias, v_dn_norm, v_w_branch, v_w_out, v_ffn_norm, v_w_gate_up, v_w_down, v_final_norm):
    given = dict(locals())
    W = {k: given[k] for k in WEIGHTS}
    M = {k: given["m_" + k] for k in WEIGHTS}
    V = {k: given["v_" + k] for k in WEIGHTS}
    chip = 2 * lax.axis_index("x") + lax.axis_index("y")
    core = lax.axis_index("c")
    chip1 = chip.astype(jnp.int32).reshape(1)
    where = jnp.stack([chip, core]).astype(jnp.int32)

    placed = [{k: _place_shard(W[k].reshape(DEPTH, BIG_SPEC[k]["rows"], BIG_SPEC[k]["keep"]), k, chip1, l,
                               f"place{l}_{k}") for k in BIG} for l in range(DEPTH)]
    full0, conv_full = _gather_layer(placed[0], dn_conv_w.reshape(CONV_ROWS, CONV_COLS))
    send_sems, recv_sems, inflight, token = _gather_start(placed[1])
    segs = _in_col_segments()

    def layer_matrices(full, l):
        big = dict(full)
        big["w_in"] = _move_cols(full["w_in"][None], segs, IN_R, f"w_in_cols{l}")[0]
        big["w_branch"] = full["w_branch"].reshape(3, MIX, D_MODEL)
        return big

    def big_of_layer(l, x_l):
        if l == 0:
            return layer_matrices(full0, 0)
        return layer_matrices(_gather_finish(_gather_wait(send_sems, recv_sems, inflight, x_l)), l)

    w = {k: W[k] for k in SMALL}
    w["attn_norm"] = attn_norm + token[0, 0]
    w["dn_conv_w"] = conv_full.reshape(DEPTH, DN_CONV, 3 * MIX)

    core1 = core.astype(jnp.int32).reshape(1)
    back_segs = [(d, s, n) for s, d, n in segs]
    pending = {}

    def on_grads(l, gl):
        gl = dict(gl)
        gl["w_in"] = _move_cols(gl["w_in"][None], back_segs, _full_shape("w_in")[1], f"g_in_cols{l}")[0]
        gl["w_branch"] = gl["w_branch"].reshape(3 * MIX, D_MODEL)
        sibling = _halves_to_sibling(gl, f"halves_to_sibling{l}")
        sums = {k: _add_half(gl[k], sibling[k], k, core1, f"chip_sum{l}_{k}") for k in BIG}
        if l == 0:
            pending[l] = (sums, _scatter_chip_sums(sums, "scatter_chip_sums"))
            return None
        sems_s, sems_r, inflight, tok = _scatter_start(sums)
        pending[l] = (sems_s, sems_r, inflight)
        return tok

    loss, dx, g = _local_step(x[0], positions[0], loss_target[0], w, big_of_layer, on_grads)
    sums1, parts1 = _scatter_wait(*pending[1], dx)
    sums0, parts0 = pending[0]
    red = {k: _sum_half(parts1[k], sums1[k], k, where, 1, None, f"sum1_{k}") for k in BIG}
    red = {k: _sum_half(parts0[k], sums0[k], k, where, 0, red[k], f"sum0_{k}") for k in BIG}
    reduced = _exchange_halves(red)
    grads = {k: reduced[k].reshape(W[k].shape) for k in BIG}

    small_shapes = {k: W[k].shape for k in SMALL}
    no_extra = (jnp.zeros(g["dn_conv_w"].shape, F32), jnp.zeros((1,), F32))
    rows128 = lambda t: t.reshape(-1, 128)
    sgu, (gs, ds, nms, nvs) = _allreduce_small_adam([
        tuple(rows128(d["sgu_w"]) for d in (g, W, M, V)),
        (_pack_small(g, (g["dn_conv_w"], loss.reshape(1))), _pack_small(W, no_extra), _pack_small(M, no_extra),
         _pack_small(V, no_extra))])
    gsm, (conv_sum, loss_sum) = _unpack_small(gs, small_shapes, (g["dn_conv_w"].shape, (1,)))
    grads.update(gsm)
    grads["dn_conv_w"] = lax.dynamic_slice_in_dim(conv_sum, chip * dn_conv_w.shape[2], dn_conv_w.shape[2], axis=2)
    loss_total = loss_sum[0]
    delta, new_m, new_v = (_unpack_small(t, small_shapes)[0] for t in (ds, nms, nvs))
    for d, t in zip((grads, delta, new_m, new_v), sgu):
        d["sgu_w"] = t.reshape(sgu_w.shape)
    for k in BIG + ("dn_conv_w",):
        delta[k], new_m[k], new_v[k] = _adam(grads[k], W[k], M[k], V[k], "adam_" + k)

    return (loss_total, dx[None], *[grads[k] for k in WEIGHTS], *[delta[k] for k in WEIGHTS],
            *[new_m[k] for k in WEIGHTS], *[new_v[k] for k in WEIGHTS])
```

```python
import functools
import math

import jax
import jax.numpy as jnp
from jax import lax
from jax.experimental import pallas as pl
from jax.experimental.pallas import tpu as pltpu

F32 = jnp.float32
BF16 = jnp.bfloat16
HI = lax.Precision.HIGHEST

D_MODEL = 1024
DEPTH = 2
MIX = 512
EPS = 1e-6
SGU_G, SGU_T = 4, 128
SWA_H, SWA_KV, SWA_HD, WINDOW = 8, 2, 64, 128
ROPE_THETA, ROPE_DIM = 500000.0, 16
DN_H, DN_HD, DN_CONV, DN_C = 4, 128, 4, 64
D_FF = 2816
IN_COLS = 6920
IN_PIECES = ((3848, 3072), (1792, 1536), (0, 512), (512, 512), (1024, 512), (3328, 512), (1536, 128), (1664, 128),
             (3840, 8))
IN_PAD = 120
IN_R = 7040
C_GATE, C_QKV, C_UA, C_VA, C_QB, C_ZC, C_KB, C_VB, C_SM = 0, 3072, 4608, 5120, 5632, 6144, 6656, 6784, 6912

ADAM_LR, ADAM_B1, ADAM_B2, ADAM_EPS, ADAM_WD, ADAM_STEP = 0.001, 0.9, 0.999, 1e-08, 0.01, 10
VMEM_LIMIT = 56 * 1024 * 1024


def _cparams(sem):
    return pltpu.CompilerParams(dimension_semantics=sem, vmem_limit_bytes=VMEM_LIMIT)


def _dg(a, b, ca, cb, prec=None):
    return lax.dot_general(a, b, (((ca,), (cb,)), ((), ())), precision=prec, preferred_element_type=F32)


def _split(x):
    hi = x.astype(BF16)
    return hi, (x - hi.astype(F32)).astype(BF16)


def _dg3_many(as_, bs, ca, cb):
    sa = [_split(a) for a in as_]
    sb = [_split(b) for b in bs]
    hh = [_dg(a[0], b[0], ca, cb) for a, b in zip(sa, sb)]
    hl = [_dg(a[0], b[1], ca, cb) for a, b in zip(sa, sb)]
    lh = [_dg(a[1], b[0], ca, cb) for a, b in zip(sa, sb)]
    return [x + (y + z) for x, y, z in zip(hh, hl, lh)]


def _dg_exact_lhs_many(a01, bs, ca, cb):
    a = a01.astype(BF16)
    b1 = [b.astype(BF16) for b in bs]
    r1 = [b - t.astype(F32) for b, t in zip(bs, b1)]
    b2 = [r.astype(BF16) for r in r1]
    b3 = [(r - t.astype(F32)).astype(BF16) for r, t in zip(r1, b2)]
    d1 = [_dg(a, t, ca, cb) for t in b1]
    d2 = [_dg(a, t, ca, cb) for t in b2]
    d3 = [_dg(a, t, ca, cb) for t in b3]
    return [x + (y + z) for x, y, z in zip(d1, d2, d3)]


def _mm(a, b):
    return _dg(a.astype(BF16), b.astype(BF16), 1, 0)


def _mm_nt(a, b):
    return _dg(a.astype(BF16), b.astype(BF16), 1, 1)


def _mm_tn(a, b):
    return _dg(a.astype(BF16), b.astype(BF16), 0, 0)


def _sigmoid(x):
    return 0.5 * jnp.tanh(0.5 * x) + 0.5


def _silu(x):
    return x * _sigmoid(x)


def _dsilu(x):
    s = _sigmoid(x)
    return s * (1.0 + x * (1.0 - s))


_GC = math.sqrt(2.0 / math.pi)


def _gelu(x):
    return 0.5 * x * (1.0 + jnp.tanh(_GC * (x + 0.044715 * x * x * x)))


def _dgelu(x):
    t = jnp.tanh(_GC * (x + 0.044715 * x * x * x))
    return 0.5 * (1.0 + t) + 0.5 * x * (1.0 - t * t) * _GC * (1.0 + 3.0 * 0.044715 * x * x)


def _softplus(x):
    return jnp.maximum(x, 0.0) + jnp.log(1.0 + jnp.exp(-jnp.abs(x)))


def _acc(ref, val, i):
    @pl.when(i == 0)
    def _():
        ref[...] = val

    @pl.when(i > 0)
    def _():
        ref[...] += val


def _halo_rows(dtype):
    return 8 * 4 // jnp.dtype(dtype).itemsize


def _tok_call(body, name, S, TB, tok_in, const_in=(), tok_out=(), acc_out=(), prev_in=(), next_in=(), smem_in=()):
    nb = S // TB
    in_specs, args = [], []
    for a, w, cb in tok_in:
        in_specs.append(pl.BlockSpec((TB, w), functools.partial(lambda i, cb: (i, cb), cb=cb)))
        args.append(a)
    for a, w, cb in prev_in:
        hr = _halo_rows(a.dtype)
        in_specs.append(pl.BlockSpec((hr, w), functools.partial(
            lambda i, cb, r: (jnp.maximum(i * r - 1, 0), cb), cb=cb, r=TB // hr)))
        args.append(a)
    for a, w, cb in next_in:
        hr = _halo_rows(a.dtype)
        in_specs.append(pl.BlockSpec((hr, w), functools.partial(
            lambda i, cb, r, last: (jnp.minimum((i + 1) * r, last), cb), cb=cb, r=TB // hr, last=S // hr - 1)))
        args.append(a)
    for a in const_in:
        in_specs.append(pl.BlockSpec(a.shape, lambda i: (0, 0)))
        args.append(a)
    for a in smem_in:
        in_specs.append(pl.BlockSpec(memory_space=pltpu.SMEM))
        args.append(a)
    out_specs, out_shape = [], []
    for w, dt in tok_out:
        out_specs.append(pl.BlockSpec((TB, w), lambda i: (i, 0)))
        out_shape.append(jax.ShapeDtypeStruct((S, w), dt))
    for shp, dt in acc_out:
        out_specs.append(pl.BlockSpec(shp, lambda i: (0, 0)))
        out_shape.append(jax.ShapeDtypeStruct(shp, dt))

    def kern(*refs):
        body(pl.program_id(0), *refs)

    return pl.pallas_call(
        kern, name=name, grid=(nb,), in_specs=in_specs, out_specs=out_specs, out_shape=out_shape,
        compiler_params=_cparams(("arbitrary",)),
    )(*args)


MM_BLOCKS = (1024, 1408, 640, 512, 256, 128)


def _pick(n, cands):
    for c in cands:
        if n % c == 0:
            return c
    return n


MM_VMEM_BUDGET = 44 * 1024 * 1024


def _mm_blocks(M, N, K, a_bytes, b_bytes, o_bytes, add_bytes):
    bn = _pick(N, MM_BLOCKS)
    fits = None
    for bk in [K] + [c for c in (2816, 2048) + MM_BLOCKS if c < K and K % c == 0]:
        for bm in [c for c in MM_BLOCKS if M % c == 0 and c >= min(M, 512)]:
            b_bufs = 1 if (bk == K and bn == N) else 2
            need = 2 * bm * bk * a_bytes + b_bufs * bk * bn * b_bytes + 2 * bm * bn * (o_bytes + add_bytes)
            need += bm * bn * 4 if bk < K else 0
            if need <= MM_VMEM_BUDGET:
                fits = fits or (bm, bn, bk)
                if (M // bm) * (N // bn) * (K // bk) >= 4:
                    return bm, bn, bk
    if fits is None:
        raise ValueError(f"no matmul blocks for {(M, N, K)}")
    return fits


def _matmul(a, b, *, ta=False, tb=False, add=None, out_dtype=F32, name):
    M, K = (a.shape[1], a.shape[0]) if ta else a.shape
    N = b.shape[0] if tb else b.shape[1]
    bm, bn, bk = _mm_blocks(M, N, K, a.dtype.itemsize, b.dtype.itemsize, jnp.dtype(out_dtype).itemsize,
                            0 if add is None else add.dtype.itemsize)
    nk = K // bk
    b_mode = dict(pipeline_mode=pl.Buffered(1)) if (bk == K and bn == N) else {}
    a_spec = pl.BlockSpec((bk, bm), lambda i, j, k: (k, i)) if ta else pl.BlockSpec((bm, bk), lambda i, j, k: (i, k))
    b_spec = (pl.BlockSpec((bn, bk), lambda i, j, k: (j, k), **b_mode) if tb
              else pl.BlockSpec((bk, bn), lambda i, j, k: (k, j), **b_mode))
    o_spec = pl.BlockSpec((bm, bn), lambda i, j, k: (i, j))
    ca, cb = (0 if ta else 1), (1 if tb else 0)

    def kern(*refs):
        a_ref, b_ref = refs[:2]
        add_ref = refs[2] if add is not None else None
        o_ref = refs[3] if add is not None else refs[2]
        p = _dg(a_ref[...].astype(BF16), b_ref[...].astype(BF16), ca, cb)

        def finish(r):
            if add is not None:
                r = r + add_ref[...].astype(F32)
            o_ref[...] = r.astype(out_dtype)

        if nk == 1:
            finish(p)
            return
        acc_ref = refs[-1]
        k = pl.program_id(2)

        @pl.when(k == 0)
        def _():
            acc_ref[...] = p

        @pl.when((k > 0) & (k < nk - 1))
        def _():
            acc_ref[...] += p

        @pl.when(k == nk - 1)
        def _():
            finish(acc_ref[...] + p)

    in_specs = [a_spec, b_spec] + ([o_spec] if add is not None else [])
    args = (a, b) + ((add,) if add is not None else ())
    return pl.pallas_call(
        kern, name=name, grid=(M // bm, N // bn, nk), in_specs=in_specs, out_specs=o_spec,
        out_shape=jax.ShapeDtypeStruct((M, N), out_dtype),
        scratch_shapes=[pltpu.VMEM((bm, bn), F32)] if nk > 1 else [],
        compiler_params=_cparams(("parallel", "parallel", "arbitrary")),
    )(*args)


def _rms_fwd(x, g, name):
    S = x.shape[0]

    def body(i, x_ref, g_ref, h_ref):
        xv = x_ref[...]
        r = lax.rsqrt(jnp.mean(xv * xv, axis=-1, keepdims=True) + EPS)
        h_ref[...] = (xv * r * g_ref[...]).astype(BF16)

    return _tok_call(body, name, S, min(S, 512), [(x, D_MODEL, 0)], [g], [(D_MODEL, BF16)])[0]


def _rms_bwd_vals(xv, g, dh):
    r = lax.rsqrt(jnp.mean(xv * xv, axis=-1, keepdims=True) + EPS)
    u = dh * g
    dx = r * u - xv * (r * r * r) * jnp.mean(u * xv, axis=-1, keepdims=True)
    dg = jnp.sum(dh * xv * r, axis=0, keepdims=True)
    return dx, dg


def _rms_bwd_add(x, g, dh, dres, name):
    S = x.shape[0]

    def body(i, x_ref, dh_ref, dr_ref, g_ref, dx_ref, dg_ref):
        dx, dg = _rms_bwd_vals(x_ref[...], g_ref[...], dh_ref[...].astype(F32))
        dx_ref[...] = dr_ref[...] + dx
        _acc(dg_ref, dg, i)

    return _tok_call(body, name, S, min(S, 512), [(x, D_MODEL, 0), (dh, D_MODEL, 0), (dres, D_MODEL, 0)], [g],
                     [(D_MODEL, F32)], [((1, D_MODEL), F32)])


def _final_loss(x, g, target):
    S = x.shape[0]

    def body(i, x_ref, t_ref, g_ref, dx_ref, loss_ref, dg_ref):
        xv, gv = x_ref[...], g_ref[...]
        r = lax.rsqrt(jnp.mean(xv * xv, axis=-1, keepdims=True) + EPS)
        e = xv * r * gv - t_ref[...]
        part = 0.5 * jnp.sum(jnp.mean(e * e, axis=-1, keepdims=True), axis=0, keepdims=True)
        dx, dg = _rms_bwd_vals(xv, gv, e * (1.0 / D_MODEL))
        dx_ref[...] = dx
        _acc(loss_ref, jnp.broadcast_to(part, (1, 128)), i)
        _acc(dg_ref, dg, i)

    return _tok_call(body, "final_loss", S, min(S, 512), [(x, D_MODEL, 0), (target, D_MODEL, 0)], [g],
                     [(D_MODEL, F32)], [((1, 128), F32), ((1, D_MODEL), F32)])


def _swiglu_fwd(gu, name):
    S = gu.shape[0]

    def body(i, gu_ref, a_ref):
        a_ref[...] = (_silu(gu_ref[:, :D_FF].astype(F32)) * gu_ref[:, D_FF:].astype(F32)).astype(BF16)

    return _tok_call(body, name, S, min(S, 256), [(gu, 2 * D_FF, 0)], [], [(D_FF, BF16)])[0]


def _swiglu_bwd(gu, dact, name):
    S = gu.shape[0]

    def body(i, gu_ref, da_ref, dgu_ref):
        gg, uu, da = gu_ref[:, :D_FF].astype(F32), gu_ref[:, D_FF:].astype(F32), da_ref[...].astype(F32)
        dgu_ref[:, :D_FF] = (da * uu * _dsilu(gg)).astype(BF16)
        dgu_ref[:, D_FF:] = (da * _silu(gg)).astype(BF16)

    return _tok_call(body, name, S, min(S, 256), [(gu, 2 * D_FF, 0), (dact, D_FF, 0)], [], [(2 * D_FF, BF16)])[0]


def _merge_fwd(proj, bds, name):
    S = proj.shape[0]

    def body(i, g0, g1, g2, b0, b1, b2, m_ref):
        m = jnp.zeros(m_ref.shape, F32)
        for gr, br in ((g0, b0), (g1, b1), (g2, b2)):
            m = m + _sigmoid(gr[...].astype(F32)) * br[...].astype(F32)
        m_ref[...] = m.astype(BF16)

    tok = [(proj, D_MODEL, n) for n in range(3)] + [(b, D_MODEL, 0) for b in bds]
    return _tok_call(body, name, S, min(S, 512), tok, [], [(D_MODEL, BF16)])[0]


def _merge_bwd(proj, bds, dm, name):
    S = proj.shape[0]

    def body(i, g0, g1, g2, b0, b1, b2, dm_ref, d0, d1, d2, dgp_ref):
        dmv = dm_ref[...]
        for n, (gr, br, dr) in enumerate(((g0, b0, d0), (g1, b1, d1), (g2, b2, d2))):
            s = _sigmoid(gr[...].astype(F32))
            dr[...] = (dmv * s).astype(BF16)
            dgp_ref[:, n * D_MODEL:(n + 1) * D_MODEL] = (dmv * br[...].astype(F32) * s * (1.0 - s)).astype(BF16)

    tok = [(proj, D_MODEL, n) for n in range(3)] + [(b, D_MODEL, 0) for b in bds] + [(dm, D_MODEL, 0)]
    return _tok_call(body, name, S, min(S, 512), tok, [],
                     [(D_MODEL, BF16)] * 3 + [(3 * D_MODEL, BF16)])


def _sgu_ln(v, lg, lb):
    mu = jnp.mean(v, axis=-1, keepdims=True)
    vc = v - mu
    rstd = lax.rsqrt(jnp.mean(vc * vc, axis=-1, keepdims=True) + EPS)
    vhat = vc * rstd
    return vhat, rstd, vhat * lg + lb


def _sgu_fwd(proj, lg, lb, wc, bst, name):
    S = proj.shape[0]

    def body(i, ua_ref, va_ref, lg_ref, lb_ref, wc_ref, bs_ref, o_ref):
        u = _gelu(ua_ref[...].astype(F32))
        _, _, vn = _sgu_ln(_gelu(va_ref[...].astype(F32)), lg_ref[...], lb_ref[...])
        for g in range(SGU_G):
            sl = slice(g * 128, (g + 1) * 128)
            mixed = _mm(wc_ref[sl, :], vn[:, sl]) + bs_ref[:, g:g + 1]
            o_ref[:, sl] = (u[:, sl] * mixed).astype(BF16)

    return _tok_call(body, name, S, SGU_T, [(proj, MIX, C_UA // MIX), (proj, MIX, C_VA // MIX)], [lg, lb, wc, bst],
                     [(MIX, BF16)])[0]


def _sgu_bwd(proj, lg, lb, wc, bst, dout, name):
    S = proj.shape[0]

    def body(i, ua_ref, va_ref, do_ref, lg_ref, lb_ref, wc_ref, bs_ref, dua_ref, dva_ref, dlg_ref, dlb_ref, dwc_ref,
             dbs_ref):
        ua, va, do = ua_ref[...].astype(F32), va_ref[...].astype(F32), do_ref[...].astype(F32)
        u = _gelu(ua)
        lgv = lg_ref[...]
        vhat, rstd, vn = _sgu_ln(_gelu(va), lgv, lb_ref[...])
        tril = lax.broadcasted_iota(jnp.int32, (128, 128), 0) >= lax.broadcasted_iota(jnp.int32, (128, 128), 1)
        lane4 = lax.broadcasted_iota(jnp.int32, (128, 4), 1)
        gs = range(SGU_G)
        sls = [slice(g * 128, (g + 1) * 128) for g in gs]
        wgs = [wc_ref[sl, :] for sl in sls]
        mixed = [_mm(wgs[g], vn[:, sls[g]]) for g in gs]
        dmix = [do[:, sl] * u[:, sl] for sl in sls]
        dwg = [_mm_nt(dmix[g], vn[:, sls[g]]) for g in gs]
        dvn = jnp.concatenate([_mm_tn(wgs[g], dmix[g]) for g in gs], axis=1)
        dbs = jnp.zeros((128, 4), F32)
        for g in gs:
            dua_ref[:, sls[g]] = (do[:, sls[g]] * (mixed[g] + bs_ref[:, g:g + 1]) * _dgelu(ua[:, sls[g]])).astype(BF16)
            dbs = dbs + jnp.where(lane4 == g, jnp.sum(dmix[g], axis=-1, keepdims=True), 0.0)
            _acc(dwc_ref.at[sls[g], :], jnp.where(tril, dwg[g], 0.0), i)
        _acc(dbs_ref, dbs, i)
        _acc(dlg_ref, jnp.sum(dvn * vhat, axis=0, keepdims=True), i)
        _acc(dlb_ref, jnp.sum(dvn, axis=0, keepdims=True), i)
        dvh = dvn * lgv
        dv = rstd * (dvh - jnp.mean(dvh, axis=-1, keepdims=True) - vhat * jnp.mean(dvh * vhat, axis=-1, keepdims=True))
        dva_ref[...] = (dv * _dgelu(va)).astype(BF16)

    return _tok_call(body, name, S, SGU_T, [(proj, MIX, C_UA // MIX), (proj, MIX, C_VA // MIX), (dout, MIX, 0)],
                     [lg, lb, wc, bst], [(MIX, BF16), (MIX, BF16)],
                     [((1, MIX), F32), ((1, MIX), F32), ((SGU_G * 128, 128), F32), ((128, 4), F32)])


def _rope_tables(positions):
    S = positions.shape[0]
    inv_freq = ROPE_THETA ** (-jnp.arange(0, ROPE_DIM, 2, dtype=F32) / ROPE_DIM)
    ang = positions.astype(F32)[:, None] * inv_freq
    c, s = jnp.cos(ang), jnp.sin(ang)
    c64 = jnp.concatenate([c, c, jnp.ones((S, SWA_HD - ROPE_DIM), F32)], axis=1)
    s64 = jnp.concatenate([-s, s, jnp.zeros((S, SWA_HD - ROPE_DIM), F32)], axis=1)
    return jnp.tile(c64, (1, 2)), jnp.tile(s64, (1, 2))


def _rope128(x, c, s):
    lane = lax.broadcasted_iota(jnp.int32, x.shape, 1) % SWA_HD
    swapped = jnp.where(lane < ROPE_DIM // 2, pltpu.roll(x, 128 - ROPE_DIM // 2, 1), pltpu.roll(x, ROPE_DIM // 2, 1))
    return x * c + swapped * s


def _rope_t128(y, c, s):
    ys = y * s
    lane = lax.broadcasted_iota(jnp.int32, y.shape, 1) % SWA_HD
    swapped = jnp.where(lane < ROPE_DIM // 2, pltpu.roll(ys, 128 - ROPE_DIM // 2, 1), pltpu.roll(ys, ROPE_DIM // 2, 1))
    return y * c + jnp.where(lane < ROPE_DIM, swapped, 0.0)


def _rope_fwd(proj, cos, sin, name):
    S = proj.shape[0]
    scale = SWA_HD ** -0.5

    def body(i, q_ref, k_ref, v_ref, c_ref, s_ref, qo_ref, ko_ref, vo_ref):
        c, s = c_ref[...], s_ref[...]
        for j in range(4):
            sl = slice(j * 128, (j + 1) * 128)
            qo_ref[:, sl] = (_rope128(q_ref[:, sl].astype(F32), c, s) * scale).astype(BF16)
        ko_ref[...] = _rope128(k_ref[...].astype(F32), c, s).astype(BF16)
        vo_ref[...] = v_ref[...].astype(BF16)

    return _tok_call(body, name, S, min(S, 512),
                     [(proj, MIX, C_QB // MIX), (proj, 128, C_KB // 128), (proj, 128, C_VB // 128), (cos, 128, 0),
                      (sin, 128, 0)], [], [(MIX, BF16), (128, BF16), (128, BF16)])


def _rope_bwd(dq, dk, dv, cos, sin, name):
    S = dq.shape[0]
    scale = SWA_HD ** -0.5

    def body(i, dq_ref, dk_ref, dv_ref, c_ref, s_ref, qo_ref, ko_ref, vo_ref):
        c, s = c_ref[...], s_ref[...]
        for j in range(4):
            sl = slice(j * 128, (j + 1) * 128)
            qo_ref[:, sl] = _rope_t128(dq_ref[:, sl] * scale, c, s).astype(BF16)
        ko_ref[...] = _rope_t128(dk_ref[...], c, s).astype(BF16)
        vo_ref[...] = dv_ref[...].astype(BF16)

    return _tok_call(body, name, S, min(S, 512),
                     [(dq, MIX, 0), (dk, 128, 0), (dv, 128, 0), (cos, 128, 0), (sin, 128, 0)], [],
                     [(MIX, BF16), (128, BF16), (128, BF16)])


def _swa_band(i, k_ref, v_ref):
    pstart = pl.multiple_of(jnp.maximum(i - 1, 0) * WINDOW, WINDOW)
    cstart = pl.multiple_of(i * WINDOW, WINDOW)
    kb = jnp.concatenate([k_ref[pl.ds(pstart, WINDOW), :], k_ref[pl.ds(cstart, WINDOW), :]], axis=0)
    vb = jnp.concatenate([v_ref[pl.ds(pstart, WINDOW), :], v_ref[pl.ds(cstart, WINDOW), :]], axis=0)
    qi = lax.broadcasted_iota(jnp.int32, (WINDOW, 2 * WINDOW), 0)
    sj = lax.broadcasted_iota(jnp.int32, (WINDOW, 2 * WINDOW), 1)
    mask = (sj > qi) & (sj <= qi + WINDOW) & ((i > 0) | (sj >= WINDOW))
    return kb, vb, mask, pstart, cstart


def _swa_probs(qs, kh, mask, sinks):
    logits = [jnp.where(mask, _dg(qh, kh, 1, 1), -1e30) for qh in qs]
    m = [jnp.maximum(jnp.max(l, axis=-1, keepdims=True), s) for l, s in zip(logits, sinks)]
    p = [jnp.exp(l - mm) for l, mm in zip(logits, m)]
    ps = [jnp.exp(s - mm) for s, mm in zip(sinks, m)]
    inv = [1.0 / (jnp.sum(pp, axis=-1, keepdims=True) + s) for pp, s in zip(p, ps)]
    return [pp * iv for pp, iv in zip(p, inv)], [s * iv for s, iv in zip(ps, inv)]


def _swa_fwd(q, k, v, sinks, name):
    S = q.shape[0]
    G = SWA_H // SWA_KV

    def body(i, q_ref, k_ref, v_ref, s_ref, o_ref):
        kb, vb, mask, _, _ = _swa_band(i, k_ref, v_ref)
        qv = q_ref[...]
        for kv in range(SWA_KV):
            ksl = slice(kv * SWA_HD, (kv + 1) * SWA_HD)
            heads = range(kv * G, (kv + 1) * G)
            pn, _ = _swa_probs([qv[:, h * SWA_HD:(h + 1) * SWA_HD] for h in heads], kb[:, ksl], mask,
                               [s_ref[0, h] for h in heads])
            outs = [_dg(p.astype(BF16), vb[:, ksl], 1, 0) for p in pn]
            for h, o in zip(heads, outs):
                o_ref[:, h * SWA_HD:(h + 1) * SWA_HD] = o.astype(BF16)

    return _tok_call(body, name, S, WINDOW, [(q, MIX, 0)], [k, v], [(MIX, BF16)], smem_in=[sinks])[0]


def _swa_bwd(q, k, v, sinks, dout, name):
    S = q.shape[0]

    def body(i, q_ref, do_ref, k_ref, v_ref, s_ref, dq_ref, dk_ref, dv_ref, ds_ref):
        kb, vb, mask, pstart, cstart = _swa_band(i, k_ref, v_ref)
        qv, dov = q_ref[...], do_ref[...]
        lane = lax.broadcasted_iota(jnp.int32, (1, 128), 1)
        dsink = jnp.zeros((1, 128), F32)
        dkb, dvb = [], []
        G = SWA_H // SWA_KV
        for kv in range(SWA_KV):
            ksl = slice(kv * SWA_HD, (kv + 1) * SWA_HD)
            heads = range(kv * G, (kv + 1) * G)
            qs = [qv[:, h * SWA_HD:(h + 1) * SWA_HD] for h in heads]
            dos = [dov[:, h * SWA_HD:(h + 1) * SWA_HD].astype(BF16) for h in heads]
            pn, psn = _swa_probs(qs, kb[:, ksl], mask, [s_ref[0, h] for h in heads])
            dp = [_dg(d, vb[:, ksl], 1, 1) for d in dos]
            delta = [jnp.sum(a * b, axis=-1, keepdims=True) for a, b in zip(dp, pn)]
            dsc = [(p * (a - d)).astype(BF16) for p, a, d in zip(pn, dp, delta)]
            dqs = [_dg(s, kb[:, ksl], 1, 0) for s in dsc]
            dks = [_dg(s, qh, 0, 0) for s, qh in zip(dsc, qs)]
            dvs = [_dg(p.astype(BF16), d, 0, 0) for p, d in zip(pn, dos)]
            for n_, h in enumerate(heads):
                dq_ref[:, h * SWA_HD:(h + 1) * SWA_HD] = dqs[n_]
                dsink = dsink + jnp.where(lane == h, -jnp.sum(psn[n_] * delta[n_], axis=0, keepdims=True), 0.0)
            dkb.append((dks[0] + dks[1]) + (dks[2] + dks[3]))
            dvb.append((dvs[0] + dvs[1]) + (dvs[2] + dvs[3]))
        dkb = jnp.concatenate(dkb, axis=1)
        dvb = jnp.concatenate(dvb, axis=1)

        @pl.when(i == 0)
        def _():
            dk_ref[...] = jnp.zeros_like(dk_ref)
            dv_ref[...] = jnp.zeros_like(dv_ref)

        dk_ref[pl.ds(pstart, WINDOW), :] += dkb[:WINDOW]
        dv_ref[pl.ds(pstart, WINDOW), :] += dvb[:WINDOW]
        dk_ref[pl.ds(cstart, WINDOW), :] += dkb[WINDOW:]
        dv_ref[pl.ds(cstart, WINDOW), :] += dvb[WINDOW:]
        _acc(ds_ref, dsink, i)

    return _tok_call(body, name, S, WINDOW, [(q, MIX, 0), (dout, MIX, 0)], [k, v], [(MIX, F32)],
                     [((S, 128), F32), ((S, 128), F32), ((1, 128), F32)], smem_in=[sinks])


def _shift_rows(xs, k):
    return xs if k == 0 else pltpu.roll(xs, k, 0)


def _dn_conv(x_ref, p_ref, w_ref, i):
    hr = p_ref.shape[0]
    halo = jnp.where(i > 0, p_ref[...].astype(F32), 0.0)
    xs = jnp.concatenate([halo, x_ref[...].astype(F32)], axis=0)
    sh = [_shift_rows(xs, DN_CONV - 1 - t)[hr:] for t in range(DN_CONV)]
    pre = sh[0] * w_ref[0:1, :]
    for t in range(1, DN_CONV):
        pre = pre + sh[t] * w_ref[t:t + 1, :]
    return pre, sh


def _dn_gates(sm, alog, dtb):
    lane = lax.broadcasted_iota(jnp.int32, sm.shape, 1)
    return jnp.where(lane < DN_H, _sigmoid(sm), -jnp.exp(alog) * _softplus(sm + dtb))


def _dn_pre_fwd(proj, conv_w, alog_l, dtb_l, name):
    S = proj.shape[0]
    scale = DN_HD ** -0.5

    def body(i, x_ref, sm_ref, p_ref, w_ref, al_ref, db_ref, q_ref, k_ref, v_ref, bg_ref):
        pre, _ = _dn_conv(x_ref, p_ref, w_ref, i)
        a = _silu(pre)
        for h in range(DN_H):
            sl = slice(h * DN_HD, (h + 1) * DN_HD)
            qh, kh = a[:, sl], a[:, MIX + h * DN_HD:MIX + (h + 1) * DN_HD]
            q_ref[:, sl] = qh * (lax.rsqrt(jnp.sum(qh * qh, axis=-1, keepdims=True) + EPS) * scale)
            k_ref[:, sl] = kh * lax.rsqrt(jnp.sum(kh * kh, axis=-1, keepdims=True) + EPS)
        v_ref[...] = a[:, 2 * MIX:]
        bg_ref[...] = _dn_gates(sm_ref[...].astype(F32), al_ref[...], db_ref[...])

    TB = min(S, 256)
    return _tok_call(body, name, S, TB, [(proj, 3 * MIX, C_QKV // (3 * MIX)), (proj, 128, C_SM // 128)],
                     [conv_w, alog_l, dtb_l], [(MIX, F32), (MIX, F32), (MIX, F32), (128, F32)],
                     prev_in=[(proj, 3 * MIX, C_QKV // (3 * MIX))])


def _dn_pre_bwd1(proj, conv_w, alog_l, dtb_l, dq, dk, dv, dbg, name):
    S = proj.shape[0]
    scale = DN_HD ** -0.5

    def body(i, x_ref, sm_ref, dq_ref, dk_ref, dv_ref, dbg_ref, p_ref, w_ref, al_ref, db_ref, dpre_ref, dsm_ref,
             dw_ref, dal_ref, ddb_ref):
        pre, sh = _dn_conv(x_ref, p_ref, w_ref, i)
        a = _silu(pre)
        da_parts = []
        for part, (g_ref, sc) in enumerate(((dq_ref, scale), (dk_ref, 1.0))):
            for h in range(DN_H):
                xh = a[:, part * MIX + h * DN_HD:part * MIX + (h + 1) * DN_HD]
                rs = lax.rsqrt(jnp.sum(xh * xh, axis=-1, keepdims=True) + EPS)
                y = xh * rs
                dy = g_ref[:, h * DN_HD:(h + 1) * DN_HD] * sc
                da_parts.append(rs * (dy - y * jnp.sum(dy * y, axis=-1, keepdims=True)))
        da_parts.append(dv_ref[...])
        dpre = jnp.concatenate(da_parts, axis=1) * _dsilu(pre)
        dpre_ref[...] = dpre
        dw = jnp.concatenate([jnp.sum(dpre * sh[t], axis=0, keepdims=True) for t in range(DN_CONV)], axis=0)
        _acc(dw_ref, dw, i)
        sm, al, db, dbg_v = sm_ref[...].astype(F32), al_ref[...], db_ref[...], dbg_ref[...]
        lane = lax.broadcasted_iota(jnp.int32, sm.shape, 1)
        sg = _sigmoid(sm)
        gneg = -jnp.exp(al)
        is_g = (lane >= DN_H) & (lane < 2 * DN_H)
        d_al = jnp.where(is_g, dbg_v * gneg * _sigmoid(sm + db), 0.0)
        dsm_ref[...] = jnp.where(lane < DN_H, dbg_v * sg * (1.0 - sg), d_al).astype(BF16)
        _acc(ddb_ref, jnp.sum(d_al, axis=0, keepdims=True), i)
        _acc(dal_ref, jnp.sum(jnp.where(is_g, dbg_v * gneg * _softplus(sm + db), 0.0), axis=0, keepdims=True), i)

    TB = min(S, 256)
    return _tok_call(body, name, S, TB,
                     [(proj, 3 * MIX, C_QKV // (3 * MIX)), (proj, 128, C_SM // 128), (dq, MIX, 0), (dk, MIX, 0),
                      (dv, MIX, 0), (dbg, 128, 0)], [conv_w, alog_l, dtb_l],
                     [(3 * MIX, F32), (128, BF16)], [((DN_CONV, 3 * MIX), F32), ((1, 128), F32), ((1, 128), F32)],
                     prev_in=[(proj, 3 * MIX, C_QKV // (3 * MIX))])


def _dn_pre_bwd2(dpre, conv_w, name):
    S = dpre.shape[0]
    TB = min(S, 256)
    nb = S // TB

    def body(i, d_ref, n_ref, w_ref, o_ref):
        halo = jnp.where(i < nb - 1, n_ref[...], 0.0)
        ds = jnp.concatenate([d_ref[...], halo], axis=0)
        out = ds[:TB] * w_ref[DN_CONV - 1:DN_CONV, :]
        for t in range(DN_CONV - 1):
            k = DN_CONV - 1 - t
            out = out + pltpu.roll(ds, TB + 8 - k, 0)[:TB] * w_ref[t:t + 1, :]
        o_ref[...] = out.astype(BF16)

    return _tok_call(body, name, S, TB, [(dpre, 3 * MIX, 0)], [conv_w], [(3 * MIX, BF16)],
                     next_in=[(dpre, 3 * MIX, 0)])[0]


def _dn_decay_terms(bgs, heads):
    C = DN_C
    ri = lax.broadcasted_iota(jnp.int32, (C, C), 0)
    ci = lax.broadcasted_iota(jnp.int32, (C, C), 1)
    tril, eye = ri >= ci, ri == ci
    beta = [b[:, h:h + 1] for b, h in zip(bgs, heads)]
    gcol = _dg_exact_lhs_many(tril, [jnp.broadcast_to(b[:, DN_H + h:DN_H + h + 1], (C, C))
                                     for b, h in zip(bgs, heads)], 1, 0)
    grow = [jnp.sum(jnp.where(eye, g, 0.0), axis=0, keepdims=True) for g in gcol]
    decay = [jnp.exp(jnp.where(tril, g - r, -1e30)) for g, r in zip(gcol, grow)]
    e_gc = [jnp.exp(g[:, 0:1]) for g in gcol]
    e_kd = [jnp.exp(g[C - 1:C, 0:1] - g[:, 0:1]) for g in gcol]
    cdec = [jnp.exp(g[C - 1:C, 0:1]) for g in gcol]
    return beta, decay, e_gc, e_kd, cdec


def _dn_nb(S):
    return 4 if S % (4 * DN_C) == 0 else 1


def _dn_prep_fwd(q, k, v, bg, name):
    S = q.shape[0]
    C, NB = DN_C, _dn_nb(S)
    TB = NB * C

    def kern(q_ref, k_ref, v_ref, bg_ref, t_ref, uw_ref, at_ref, qd_ref, kd_ref, dec_ref):
        lane = lax.broadcasted_iota(jnp.int32, (C, 128), 1)
        ri = lax.broadcasted_iota(jnp.int32, (C, C), 0)
        ci = lax.broadcasted_iota(jnp.int32, (C, C), 1)
        tril, eye = ri >= ci, ri == ci
        chains = [(cb, h) for cb in range(NB) for h in range(DN_H)]
        rows = lambda cb: slice(cb * C, (cb + 1) * C)
        head = lambda h: slice(h * DN_HD, (h + 1) * DN_HD)
        beta, decay, e_gc, e_kd, cdec = _dn_decay_terms([bg_ref[rows(cb), :] for cb, _ in chains],
                                                        [h for _, h in chains])
        qs = [q_ref[rows(cb), head(h)] for cb, h in chains]
        ks = [k_ref[rows(cb), head(h)] for cb, h in chains]
        kb = [kh * b for kh, b in zip(ks, beta)]
        x = [-jnp.where(ri > ci, _mm_nt(a, kh) * d, 0.0) for a, kh, d in zip(kb, ks, decay)]
        tm = [jnp.where(eye, 1.0, 0.0) + xi for xi in x]
        p = x
        p = _dg3_many(p, p, 1, 0)
        for it in range(5):
            if it == 4:
                tm = [t + tp for t, tp in zip(tm, _dg3_many(tm, p, 1, 0))]
                break
            both = _dg3_many([jnp.concatenate([t, pp], axis=0) for t, pp in zip(tm, p)], p, 1, 0)
            tm = [t + b[:C] for t, b in zip(tm, both)]
            p = [b[C:] for b in both]
        rhs = [jnp.concatenate([v_ref[rows(cb), head(h)] * b, a * e], axis=1)
               for (cb, h), b, a, e in zip(chains, beta, kb, e_gc)]
        sol = _dg3_many(tm, rhs, 1, 0)
        attn = [_mm_nt(qh, kh) * d for qh, kh, d in zip(qs, ks, decay)]
        for n_, (cb, h) in enumerate(chains):
            rs, sl, hc = rows(cb), head(h), slice(h * C, (h + 1) * C)
            t_ref[rs, hc] = tm[n_]
            uw_ref[rs, sl] = sol[n_][:, :DN_HD]
            uw_ref[rs, MIX + h * DN_HD:MIX + (h + 1) * DN_HD] = sol[n_][:, DN_HD:]
            at_ref[rs, hc] = attn[n_]
            qd_ref[rs, sl] = (qs[n_] * e_gc[n_]).astype(BF16)
            kd_ref[rs, sl] = (ks[n_] * e_kd[n_]).astype(BF16)
        for cb in range(NB):
            dec = jnp.zeros((C, 128), F32)
            for h in range(DN_H):
                dec = dec + jnp.where(lane == h, cdec[cb * DN_H + h], 0.0)
            dec_ref[rows(cb), :] = dec

    tok = lambda w: pl.BlockSpec((TB, w), lambda i: (i, 0))
    return pl.pallas_call(
        kern, name=name, grid=(S // TB,), in_specs=[tok(MIX), tok(MIX), tok(MIX), tok(128)],
        out_specs=[tok(DN_H * C), tok(2 * MIX), tok(DN_H * C), tok(MIX), tok(MIX), tok(128)],
        out_shape=[jax.ShapeDtypeStruct((S, DN_H * C), F32), jax.ShapeDtypeStruct((S, 2 * MIX), F32),
                   jax.ShapeDtypeStruct((S, DN_H * C), F32), jax.ShapeDtypeStruct((S, MIX), BF16),
                   jax.ShapeDtypeStruct((S, MIX), BF16), jax.ShapeDtypeStruct((S, 128), F32)],
        compiler_params=_cparams(("parallel",)),
    )(q, k, v, bg)


def _dn_scan_fwd(uw, at, qd, kd, dec, name):
    S = uw.shape[0]
    C, NB = DN_C, _dn_nb(S)
    TB = NB * C
    SR = DN_H * DN_HD

    def kern(uw_ref, at_ref, qd_ref, kd_ref, dec_ref, o_ref, vn_ref, st_ref, state):
        @pl.when(pl.program_id(0) == 0)
        def _():
            state[...] = jnp.zeros_like(state)

        for cb in range(NB):
            rs = slice(cb * C, (cb + 1) * C)
            hs = range(DN_H)
            sls = [slice(h * DN_HD, (h + 1) * DN_HD) for h in hs]
            s_in = [state[sl, :] for sl in sls]
            ws = [_mm(uw_ref[rs, MIX + h * DN_HD:MIX + (h + 1) * DN_HD], s_in[h]) for h in hs]
            os_ = [_mm(qd_ref[rs, sls[h]], s_in[h]) for h in hs]
            vnew = [uw_ref[rs, sls[h]] - ws[h] for h in hs]
            oa = [_mm(at_ref[rs, h * C:(h + 1) * C], vnew[h]) for h in hs]
            kv = [_mm_tn(kd_ref[rs, sls[h]], vnew[h]) for h in hs]
            for h in hs:
                o_ref[rs, sls[h]] = os_[h] + oa[h]
                state[sls[h], :] = s_in[h] * dec_ref[cb * C:cb * C + 1, h:h + 1] + kv[h]
                st_ref[cb * SR + h * DN_HD:cb * SR + (h + 1) * DN_HD, :] = s_in[h]
                vn_ref[rs, sls[h]] = vnew[h]

    tok = lambda w: pl.BlockSpec((TB, w), lambda i: (i, 0))
    return pl.pallas_call(
        kern, name=name, grid=(S // TB,), in_specs=[tok(2 * MIX), tok(DN_H * C), tok(MIX), tok(MIX), tok(128)],
        out_specs=[tok(MIX), tok(MIX), pl.BlockSpec((NB * SR, DN_HD), lambda i: (i, 0))],
        out_shape=[jax.ShapeDtypeStruct((S, MIX), F32), jax.ShapeDtypeStruct((S, MIX), F32),
                   jax.ShapeDtypeStruct((S // C * SR, DN_HD), F32)],
        scratch_shapes=[pltpu.VMEM((SR, DN_HD), F32)],
        compiler_params=_cparams(("arbitrary",)),
    )(uw, at, qd, kd, dec)


def _dn_core_fwd(q, k, v, bg, name):
    tm, uw, at, qd, kd, dec = _dn_prep_fwd(q, k, v, bg, name + "_prep")
    o, vn, st = _dn_scan_fwd(uw, at, qd, kd, dec, name + "_scan")
    return o, dict(tm=tm, uw=uw, at=at, qd=qd, kd=kd, dec=dec, vn=vn, st=st)


def _dn_scan_bwd(sv, do, name):
    S = do.shape[0]
    C, NB = DN_C, _dn_nb(S)
    TB = NB * C
    SR = DN_H * DN_HD
    nb = S // TB

    def kern(do_ref, uw_ref, at_ref, qd_ref, kd_ref, dec_ref, vn_ref, st_ref, dvn_ref, dw_ref, dkd_ref, dc_ref, dstate):
        @pl.when(pl.program_id(0) == 0)
        def _():
            dstate[...] = jnp.zeros_like(dstate)

        lane = lax.broadcasted_iota(jnp.int32, (C, 128), 1)
        for cb in reversed(range(NB)):
            rs = slice(cb * C, (cb + 1) * C)
            dcrow = jnp.zeros((C, 128), F32)
            for h in range(DN_H):
                sl = slice(h * DN_HD, (h + 1) * DN_HD)
                doh, ds_o = do_ref[rs, sl], dstate[sl, :]
                s_in = st_ref[cb * SR + h * DN_HD:cb * SR + (h + 1) * DN_HD, :]
                d_vnew = _mm_tn(at_ref[rs, h * C:(h + 1) * C], doh) + _mm(kd_ref[rs, sl], ds_o)
                dvn_ref[rs, sl] = d_vnew
                dw_ref[rs, sl] = -_mm_nt(d_vnew, s_in)
                dkd_ref[rs, sl] = _mm_nt(vn_ref[rs, sl], ds_o)
                d_c = jnp.sum(jnp.sum(ds_o * s_in, axis=1, keepdims=True), axis=0, keepdims=True)
                dcrow = dcrow + jnp.where(lane == h, d_c, 0.0)
                dstate[sl, :] = (ds_o * dec_ref[cb * C:cb * C + 1, h:h + 1] + _mm_tn(qd_ref[rs, sl], doh)
                                 - _mm_tn(uw_ref[rs, MIX + h * DN_HD:MIX + (h + 1) * DN_HD], d_vnew))
            dc_ref[rs, :] = dcrow

    tok = lambda w: pl.BlockSpec((TB, w), lambda i: (nb - 1 - i, 0))
    return pl.pallas_call(
        kern, name=name, grid=(nb,),
        in_specs=[tok(MIX), tok(2 * MIX), tok(DN_H * C), tok(MIX), tok(MIX), tok(128), tok(MIX),
                  pl.BlockSpec((NB * SR, DN_HD), lambda i: (nb - 1 - i, 0))],
        out_specs=[tok(MIX), tok(MIX), tok(MIX), tok(128)],
        out_shape=[jax.ShapeDtypeStruct((S, MIX), F32)] * 3 + [jax.ShapeDtypeStruct((S, 128), F32)],
        scratch_shapes=[pltpu.VMEM((SR, DN_HD), F32)],
        compiler_params=_cparams(("arbitrary",)),
    )(do, sv["uw"], sv["at"], sv["qd"], sv["kd"], sv["dec"], sv["vn"], sv["st"])


def _dn_chunk_bwd(q, k, v, bg, sv, do, dvn, dw, dkd, dc, name):
    S = q.shape[0]
    C, NB = DN_C, _dn_nb(S)
    TB = NB * C
    SR = DN_H * DN_HD

    def kern(q_ref, k_ref, v_ref, bg_ref, t_ref, uw_ref, vn_ref, st_ref, do_ref, dvn_ref, dw_ref, dkd_ref, dc_ref,
             dq_ref, dk_ref, dv_ref, dbg_ref):
        lane = lax.broadcasted_iota(jnp.int32, (C, 128), 1)
        ri = lax.broadcasted_iota(jnp.int32, (C, C), 0)
        ci = lax.broadcasted_iota(jnp.int32, (C, C), 1)
        tril, eye, last = ri >= ci, ri == ci, ri[:, 0:1] == C - 1
        chains = [(cb, h) for cb in range(NB) for h in range(DN_H)]
        each = lambda f, *ls: [f(*a) for a in zip(*ls)]
        rsum = lambda t: jnp.sum(t, axis=-1, keepdims=True)
        rows = lambda cb: slice(cb * C, (cb + 1) * C)
        head = lambda h: slice(h * DN_HD, (h + 1) * DN_HD)
        tok = lambda ref: [ref[rows(cb), head(h)] for cb, h in chains]
        beta, decay, e_gc, e_kd, cdec = _dn_decay_terms([bg_ref[rows(cb), :] for cb, _ in chains],
                                                        [h for _, h in chains])
        qs, ks, vs, dos, vnew, d_kd = tok(q_ref), tok(k_ref), tok(v_ref), tok(do_ref), tok(vn_ref), tok(dkd_ref)
        s_in = [st_ref[cb * SR + h * DN_HD:cb * SR + (h + 1) * DN_HD, :] for cb, h in chains]
        d_c = [dc_ref[cb * C:cb * C + 1, h:h + 1] for cb, h in chains]
        kb = each(lambda a, b: a * b, ks, beta)
        kk = each(_mm_nt, kb, ks)
        attn = each(lambda a, b, d: _mm_nt(a, b) * d, qs, ks, decay)
        d_qd = each(_mm_nt, dos, s_in)
        d_attn = each(_mm_nt, dos, vnew)
        d_sol = [jnp.concatenate([dvn_ref[rows(cb), head(h)], dw_ref[rows(cb), head(h)]], axis=1) for cb, h in chains]
        sol = [jnp.concatenate([uw_ref[rows(cb), head(h)], uw_ref[rows(cb), MIX + h * DN_HD:MIX + (h + 1) * DN_HD]],
                               axis=1) for cb, h in chains]
        d_rhs = _dg3_many([t_ref[rows(cb), h * C:(h + 1) * C] for cb, h in chains], d_sol, 0, 0)
        d_a = _dg3_many(d_rhs, sol, 1, 1)
        d_kk = each(lambda a, d: jnp.where(ri > ci, -a, 0.0) * d, d_a, decay)
        d_qk = each(lambda a, d: a * d, d_attn, decay)
        dm = each(lambda a, b, c_, d: a * b + c_ * d, d_kk, kk, d_attn, attn)
        d_vb = [t[:, :DN_HD] for t in d_rhs]
        dz = [t[:, DN_HD:] for t in d_rhs]
        d_kb = each(lambda z, e, a, kh: z * e + _mm(a, kh), dz, e_gc, d_kk, ks)
        d_k = each(lambda a, b, c_, q: _mm_tn(a, b) + _mm_tn(c_, q), d_kk, kb, d_qk, qs)
        d_q = each(lambda a, kh, b, e: _mm(a, kh) + b * e, d_qk, ks, d_qd, e_gc)
        t_kd = each(lambda a, kh, e: rsum(a * kh * e), d_kd, ks, e_kd)
        d_gl = each(lambda t, c_, cd: jnp.sum(t, axis=0, keepdims=True) + c_ * cd, t_kd, d_c, cdec)
        d_gc = each(lambda z, a, e, m, b, q, t, gl:
                    rsum(z * a) * e + rsum(m) - rsum(jnp.where(eye, jnp.sum(m, axis=0, keepdims=True), 0.0))
                    + rsum(b * q) * e - t + jnp.where(last, gl, 0.0),
                    dz, kb, e_gc, dm, d_qd, qs, t_kd, d_gl)
        d_g = _dg_exact_lhs_many(ri <= ci, [jnp.broadcast_to(t, (C, 128)) for t in d_gc], 1, 0)
        d_beta = each(lambda a, v_, b, kh: rsum(a * v_) + rsum(b * kh), d_vb, vs, d_kb, ks)
        for n_, (cb, h) in enumerate(chains):
            dq_ref[rows(cb), head(h)] = d_q[n_]
            dk_ref[rows(cb), head(h)] = d_k[n_] + d_kd[n_] * e_kd[n_] + d_kb[n_] * beta[n_]
            dv_ref[rows(cb), head(h)] = d_vb[n_] * beta[n_]
        for cb in range(NB):
            dbg = jnp.zeros((C, 128), F32)
            for h in range(DN_H):
                n_ = cb * DN_H + h
                dbg = dbg + jnp.where(lane == h, d_beta[n_], 0.0) + jnp.where(lane == DN_H + h, d_g[n_], 0.0)
            dbg_ref[rows(cb), :] = dbg

    tok = lambda w: pl.BlockSpec((TB, w), lambda i: (i, 0))
    return pl.pallas_call(
        kern, name=name, grid=(S // TB,),
        in_specs=[tok(MIX), tok(MIX), tok(MIX), tok(128), tok(DN_H * C), tok(2 * MIX), tok(MIX),
                  pl.BlockSpec((NB * SR, DN_HD), lambda i: (i, 0)), tok(MIX), tok(MIX), tok(MIX), tok(MIX), tok(128)],
        out_specs=[tok(MIX), tok(MIX), tok(MIX), tok(128)],
        out_shape=[jax.ShapeDtypeStruct((S, MIX), F32)] * 3 + [jax.ShapeDtypeStruct((S, 128), F32)],
        compiler_params=_cparams(("parallel",)),
    )(q, k, v, bg, sv["tm"], sv["uw"], sv["vn"], sv["st"], do, dvn, dw, dkd, dc)


def _dn_core_bwd(q, k, v, bg, sv, do, name):
    dvn, dw, dkd, dc = _dn_scan_bwd(sv, do, name + "_scan")
    return _dn_chunk_bwd(q, k, v, bg, sv, do, dvn, dw, dkd, dc, name + "_chunk")


def _dn_post_fwd(o, proj, ng, name):
    S = o.shape[0]

    def body(i, o_ref, z_ref, g_ref, out_ref):
        gv = g_ref[...]
        for h in range(DN_H):
            sl = slice(h * DN_HD, (h + 1) * DN_HD)
            oh = o_ref[:, sl]
            r = lax.rsqrt(jnp.mean(oh * oh, axis=-1, keepdims=True) + EPS)
            out_ref[:, sl] = (oh * r * gv * _silu(z_ref[:, sl].astype(F32))).astype(BF16)

    return _tok_call(body, name, S, min(S, 512), [(o, MIX, 0), (proj, MIX, C_ZC // MIX)], [ng], [(MIX, BF16)])[0]


def _dn_post_bwd(o, proj, ng, dout, name):
    S = o.shape[0]

    def body(i, o_ref, z_ref, do_ref, g_ref, dov_ref, dz_ref, dg_ref):
        gv = g_ref[...]
        dg = jnp.zeros((1, DN_HD), F32)
        for h in range(DN_H):
            sl = slice(h * DN_HD, (h + 1) * DN_HD)
            oh, zh, dh = o_ref[:, sl], z_ref[:, sl].astype(F32), do_ref[:, sl].astype(F32)
            r = lax.rsqrt(jnp.mean(oh * oh, axis=-1, keepdims=True) + EPS)
            dz_ref[:, sl] = (dh * oh * r * gv * _dsilu(zh)).astype(BF16)
            dx, dgh = _rms_bwd_vals(oh, gv, dh * _silu(zh))
            dov_ref[:, sl] = dx
            dg = dg + dgh
        _acc(dg_ref, dg, i)

    return _tok_call(body, name, S, min(S, 512), [(o, MIX, 0), (proj, MIX, C_ZC // MIX), (dout, MIX, 0)], [ng],
                     [(MIX, F32), (MIX, BF16)], [((1, DN_HD), F32)])


def _layer_params(w, big, l):
    lane = jnp.arange(128)
    is_g = (lane >= DN_H) & (lane < 2 * DN_H)
    spread = lambda t: jnp.where(is_g, jnp.tile(t, 128 // DN_H), 0.0).reshape(1, 128)
    tril = jnp.tril(jnp.ones((SGU_T, SGU_T), bool))
    return dict(
        win=big["w_in"], wb=big["w_branch"], wout=big["w_out"], wgu=big["w_gate_up"], wdown=big["w_down"],
        conv=w["dn_conv_w"][l], attn_norm=w["attn_norm"][l].reshape(1, -1), ffn_norm=w["ffn_norm"][l].reshape(1, -1),
        lg=w["sgu_ln_g"][l].reshape(1, -1), lb=w["sgu_ln_b"][l].reshape(1, -1),
        wc=jnp.where(tril, w["sgu_w"][l], 0.0).reshape(SGU_G * SGU_T, SGU_T), bst=w["sgu_b"][l].T,
        sinks=w["attn_sinks"][l].reshape(1, -1), alog=spread(w["dn_a_log"][l]), dtb=spread(w["dn_dt_bias"][l]),
        ng=w["dn_norm"][l].reshape(1, -1))


def _layer_fwd(x, p, cos, sin, l):
    n = lambda s: f"l{l}_{s}"
    h = _rms_fwd(x, p["attn_norm"], n("rms1"))
    proj = _matmul(h, p["win"], out_dtype=BF16, name=n("mm_in"))
    out_a = _sgu_fwd(proj, p["lg"], p["lb"], p["wc"], p["bst"], n("sgu_fwd"))
    qr, kr, vr = _rope_fwd(proj, cos, sin, n("rope_fwd"))
    out_b = _swa_fwd(qr, kr, vr, p["sinks"], n("swa_fwd"))
    q, k, v, bg = _dn_pre_fwd(proj, p["conv"], p["alog"], p["dtb"], n("dn_pre_fwd"))
    o, dn = _dn_core_fwd(q, k, v, bg, n("dn_core_fwd"))
    out_c = _dn_post_fwd(o, proj, p["ng"], n("dn_post_fwd"))
    outs = (out_a, out_b, out_c)
    bds = [_matmul(outs[j], p["wb"][j], out_dtype=BF16, name=n(f"mm_branch{j}")) for j in range(3)]
    merged = _merge_fwd(proj, bds, n("merge_fwd"))
    x1 = _matmul(merged, p["wout"], add=x, name=n("mm_out"))
    h2 = _rms_fwd(x1, p["ffn_norm"], n("rms2"))
    gu = _matmul(h2, p["wgu"], out_dtype=BF16, name=n("mm_gu"))
    act = _swiglu_fwd(gu, n("swiglu_fwd"))
    x2 = _matmul(act, p["wdown"], add=x1, name=n("mm_down"))
    saved = dict(x=x, h=h, proj=proj, outs=outs, qr=qr, kr=kr, vr=vr, q=q, k=k, v=v, bg=bg, o=o, dn=dn, bds=bds,
                 merged=merged, x1=x1, h2=h2, gu=gu, act=act)
    return x2, saved


def _layer_bwd(dx2, s, p, cos, sin, l):
    n = lambda t: f"l{l}_{t}"
    proj = s["proj"]
    g = {}
    g["w_down"] = _matmul(s["act"], dx2, ta=True, out_dtype=BF16, name=n("wg_down"))
    dact = _matmul(dx2, p["wdown"], tb=True, out_dtype=BF16, name=n("dg_down"))
    dgu = _swiglu_bwd(s["gu"], dact, n("swiglu_bwd"))
    g["w_gate_up"] = _matmul(s["h2"], dgu, ta=True, out_dtype=BF16, name=n("wg_gu"))
    dh2 = _matmul(dgu, p["wgu"], tb=True, name=n("dg_gu"))
    dx1, g["ffn_norm"] = _rms_bwd_add(s["x1"], p["ffn_norm"], dh2, dx2, n("rms2_bwd"))
    g["w_out"] = _matmul(s["merged"], dx1, ta=True, out_dtype=BF16, name=n("wg_out"))
    dm = _matmul(dx1, p["wout"], tb=True, name=n("dg_out"))
    dbd0, dbd1, dbd2, dgp = _merge_bwd(proj, s["bds"], dm, n("merge_bwd"))
    dbds = (dbd0, dbd1, dbd2)
    g["w_branch"] = jnp.stack([_matmul(s["outs"][j], dbds[j], ta=True, out_dtype=BF16, name=n(f"wg_branch{j}"))
                               for j in range(3)])
    douts = [_matmul(dbds[j], p["wb"][j], tb=True, name=n(f"dg_branch{j}")) for j in range(3)]
    dua, dva, g["sgu_ln_g"], g["sgu_ln_b"], dwc, dbs = _sgu_bwd(proj, p["lg"], p["lb"], p["wc"], p["bst"], douts[0],
                                                                n("sgu_bwd"))
    g["sgu_w"] = dwc.reshape(SGU_G, SGU_T, SGU_T)
    g["sgu_b"] = dbs.T
    dqr, dkr, dvr, dsink = _swa_bwd(s["qr"], s["kr"], s["vr"], p["sinks"], douts[1], n("swa_bwd"))
    g["attn_sinks"] = dsink[0, :SWA_H]
    dqb, dkb, dvb = _rope_bwd(dqr, dkr, dvr, cos, sin, n("rope_bwd"))
    do, dz, dng = _dn_post_bwd(s["o"], proj, p["ng"], douts[2], n("dn_post_bwd"))
    g["dn_norm"] = dng[0]
    dq, dk, dv, dbg = _dn_core_bwd(s["q"], s["k"], s["v"], s["bg"], s["dn"], do, n("dn_core_bwd"))
    dpre, dsm, g["dn_conv_w"], dal, ddb = _dn_pre_bwd1(proj, p["conv"], p["alog"], p["dtb"], dq, dk, dv, dbg,
                                                       n("dn_pre_bwd1"))
    g["dn_a_log"] = dal[0, DN_H:2 * DN_H]
    g["dn_dt_bias"] = ddb[0, DN_H:2 * DN_H]
    dqkv = _dn_pre_bwd2(dpre, p["conv"], n("dn_pre_bwd2"))
    dproj = jnp.concatenate([dgp, dqkv, dua, dva, dqb, dz, dkb, dvb, dsm], axis=1)
    g["w_in"] = _matmul(s["h"], dproj, ta=True, out_dtype=BF16, name=n("wg_in"))
    dh = _matmul(dproj, p["win"], tb=True, name=n("dg_in"))
    dx, g["attn_norm"] = _rms_bwd_add(s["x"], p["attn_norm"], dh, dx1, n("rms1_bwd"))
    g["attn_norm"], g["ffn_norm"] = g["attn_norm"][0], g["ffn_norm"][0]
    g["sgu_ln_g"], g["sgu_ln_b"] = g["sgu_ln_g"][0], g["sgu_ln_b"][0]
    return dx, g


def _local_step(x, positions, target, w, big_of_layer, on_grads):
    cos, sin = _rope_tables(positions)
    params, saves, xs = [], [], x
    for l in range(DEPTH):
        params.append(_layer_params(w, big_of_layer(l, xs), l))
        xs, sv = _layer_fwd(xs, params[l], cos, sin, l)
        saves.append(sv)
    dx, loss_row, dgf = _final_loss(xs, w["final_norm"].reshape(1, -1), target)
    grads = [None] * DEPTH
    for l in reversed(range(DEPTH)):
        dx, grads[l] = _layer_bwd(dx, saves[l], params[l], cos, sin, l)
        token = on_grads(l, {k: grads[l].pop(k) for k in BIG})
        if token is not None and l > 0:
            params[l - 1] = dict(params[l - 1], ffn_norm=params[l - 1]["ffn_norm"] + token[0, 0])
    stacked = {k: jnp.stack([grads[l][k] for l in range(DEPTH)]) for k in grads[0]}
    stacked["final_norm"] = dgf[0]
    return loss_row[0, 0], dx, stacked


MESH = pl.DeviceIdType.MESH
HBM_SPEC = pl.BlockSpec(memory_space=pltpu.HBM)
VMEM_SPEC = pl.BlockSpec(memory_space=pltpu.VMEM)
N_CHIPS = 4
FLIPS = tuple((fx, fy, fc) for fx in (0, 1) for fy in (0, 1) for fc in (0, 1))[1:]
BIG = ("w_in", "w_branch", "w_out", "w_gate_up", "w_down")
BIG_SPEC = {
    "w_in": dict(rows=1024, cols=1792, axis=1, keep=1730, down=8),
    "w_branch": dict(rows=1536, cols=256, axis=1, keep=256, down=2),
    "w_out": dict(rows=256, cols=1024, axis=0, keep=1024, down=1),
    "w_gate_up": dict(rows=1024, cols=1408, axis=1, keep=1408, down=8),
    "w_down": dict(rows=704, cols=1024, axis=0, keep=1024, down=4),
}
CONV_ROWS, CONV_COLS = DEPTH * DN_CONV, 3 * MIX // N_CHIPS


def _full_shape(k):
    sp = BIG_SPEC[k]
    return (sp["rows"], N_CHIPS * sp["cols"]) if sp["axis"] == 1 else (N_CHIPS * sp["rows"], sp["cols"])


def _me():
    return lax.axis_index("x"), lax.axis_index("y"), lax.axis_index("c")


def _peer(x, y, c, flip):
    fx, fy, fc = flip
    return (1 - x if fx else x, 1 - y if fy else y, 1 - c if fc else c)


class _Copies:
    def __init__(self, send_sems, recv_sems):
        self.send_sems, self.recv_sems, self.k, self.sent, self.landing = send_sems, recv_sems, 0, [], []

    def _copy(self, k, src, dst, to):
        return pltpu.make_async_remote_copy(src_ref=src, dst_ref=dst, send_sem=self.send_sems.at[k],
                                            recv_sem=self.recv_sems.at[k], device_id=to, device_id_type=MESH)

    def send(self, src, dst, to, lands):
        k = self.k
        self.k += 1
        cp = self._copy(k, src, dst, to)
        cp.start()
        self.sent.append(cp)
        self.landing.append(self._copy(k, lands, lands, to))
        return k

    def wait_landed(self, k):
        self.landing[k].wait_recv()

    def finish(self, landed=()):
        for k, cp in enumerate(self.landing):
            if k not in landed:
                cp.wait_recv()
        for cp in self.sent:
            cp.wait_send()


def _place_shard(shard, k, chip, layer, name):
    sp = BIG_SPEC[k]
    rows, cols, keep = sp["rows"], sp["cols"], sp["keep"]
    tr = _pick(rows, (256, 64))
    nb = rows // tr
    if sp["axis"] == 1:
        out_spec = pl.BlockSpec((tr, cols), lambda i, ch: (i, ch[0]))
    else:
        out_spec = pl.BlockSpec((tr, cols), lambda i, ch: (ch[0] * nb + i, 0))

    def kern(ch_ref, x_ref, o_ref):
        v = x_ref[0].astype(BF16)
        if keep == cols:
            o_ref[...] = v
        else:
            o_ref[:, :keep] = v
            o_ref[:, keep:] = jnp.zeros((tr, cols - keep), BF16)

    return pl.pallas_call(
        kern, name=name, out_shape=jax.ShapeDtypeStruct(_full_shape(k), BF16),
        grid_spec=pltpu.PrefetchScalarGridSpec(
            num_scalar_prefetch=1, grid=(nb,),
            in_specs=[pl.BlockSpec((1, tr, keep), lambda i, ch: (layer, i, 0))], out_specs=out_spec),
        compiler_params=_cparams(("parallel",)),
    )(chip, shard)


def _half_block(ref, k, s, half):
    sp = BIG_SPEC[k]
    hr = sp["rows"] // 2
    if sp["axis"] == 1:
        return ref.at[pl.ds(pl.multiple_of(half * hr, 16), hr), pl.ds(pl.multiple_of(s * sp["cols"], 128), sp["cols"])]
    return ref.at[pl.ds(pl.multiple_of(s * sp["rows"] + half * hr, 16), hr), :]


def _other_chips(x, y):
    return [(1 - x, y), (x, 1 - y), (1 - x, 1 - y)]


def _gather_layer(placed, conv):
    n = len(BIG)
    n_sem = 6 * n + 3

    def body(*refs):
        conv_ref = refs[n]
        out = dict(zip(BIG, refs[n + 1:2 * n + 1]))
        conv_out, send_sems, recv_sems, local_sem = refs[2 * n + 1:]
        x, y, c = _me()
        me = 2 * x + y
        chips = _other_chips(x, y)
        net = _Copies(send_sems, recv_sems)

        def conv_block(s):
            return conv_out.at[:, pl.ds(pl.multiple_of(s * CONV_COLS, 128), CONV_COLS)]

        local = pltpu.make_async_copy(conv_ref, conv_block(me), local_sem)
        local.start()
        first = {}
        for k in BIG:
            for j, (px, py) in enumerate(chips):
                first[k, j] = net.send(_half_block(out[k], k, me, c), _half_block(out[k], k, me, c), (px, py, c),
                                       _half_block(out[k], k, 2 * px + py, c))
        for px, py in chips:
            net.send(conv_ref, conv_block(me), (px, py, c), conv_block(2 * px + py))
        for k in BIG:
            for j, (px, py) in enumerate(chips):
                net.wait_landed(first[k, j])
                net.send(_half_block(out[k], k, 2 * px + py, c), _half_block(out[k], k, 2 * px + py, c), (x, y, 1 - c),
                         _half_block(out[k], k, 2 * px + py, 1 - c))
        net.finish(landed=set(first.values()))
        local.wait()

    out_shape = [jax.ShapeDtypeStruct(_full_shape(k), BF16) for k in BIG]
    out_shape.append(jax.ShapeDtypeStruct((CONV_ROWS, N_CHIPS * CONV_COLS), F32))
    outs = pl.pallas_call(
        body, name="gather_layer", out_shape=out_shape, in_specs=[HBM_SPEC] * (n + 1), out_specs=[HBM_SPEC] * (n + 1),
        input_output_aliases={i: i for i in range(n)},
        scratch_shapes=[pltpu.SemaphoreType.DMA((n_sem,)), pltpu.SemaphoreType.DMA((n_sem,)), pltpu.SemaphoreType.DMA],
    )(*[placed[k] for k in BIG], conv)
    return dict(zip(BIG, outs[:n])), outs[n]


SEM_SPEC = pl.BlockSpec(memory_space=pltpu.SEMAPHORE)
N_BEHIND = 3 * len(BIG)


def _behind_copies(arrs, send_sems, recv_sems):
    x, y, c = _me()
    copies = []
    for i, k in enumerate(BIG):
        for j, (px, py) in enumerate(_other_chips(x, y)):
            copies.append(pltpu.make_async_remote_copy(
                src_ref=_half_block(arrs[k], k, 2 * x + y, c), dst_ref=_half_block(arrs[k], k, 2 * x + y, c),
                send_sem=send_sems.at[3 * i + j], recv_sem=recv_sems.at[3 * i + j], device_id=(px, py, c),
                device_id_type=MESH))
    return copies


def _gather_start(placed, after):
    n = len(BIG)

    def body(*refs):
        arrs = dict(zip(BIG, refs[n + 3:2 * n + 3]))
        send_sems, recv_sems = refs[n + 1], refs[n + 2]
        for cp in _behind_copies(arrs, send_sems, recv_sems):
            cp.start()
        refs[2 * n + 3][...] = jnp.zeros((8, 128), F32)

    outs = pl.pallas_call(
        body, name="gather_start",
        out_shape=(pltpu.SemaphoreType.DMA((N_BEHIND,)), pltpu.SemaphoreType.DMA((N_BEHIND,)),
                   *[pltpu.HBM(_full_shape(k), BF16) for k in BIG], jax.ShapeDtypeStruct((8, 128), F32)),
        in_specs=[HBM_SPEC] * n + [pl.BlockSpec(memory_space=pl.ANY)],
        out_specs=(SEM_SPEC, SEM_SPEC, *[HBM_SPEC] * n, VMEM_SPEC),
        input_output_aliases={i: i + 2 for i in range(n)},
        compiler_params=pltpu.CompilerParams(has_side_effects=pltpu.SideEffectType.DATAFLOW_SIDE_EFFECTING),
    )(*[pltpu.with_memory_space_constraint(placed[k], pltpu.HBM) for k in BIG], after)
    return outs[0], outs[1], dict(zip(BIG, outs[2:n + 2])), outs[n + 2]


def _gather_wait(send_sems, recv_sems, inflight, after):
    n = len(BIG)

    def body(*refs):
        arrs = dict(zip(BIG, refs[:n]))
        for cp in _behind_copies(arrs, refs[n], refs[n + 1]):
            cp.wait_send()
            cp.wait_recv()

    outs = pl.pallas_call(
        body, name="gather_wait", out_shape=tuple(pltpu.HBM(_full_shape(k), BF16) for k in BIG),
        in_specs=[HBM_SPEC] * n + [SEM_SPEC, SEM_SPEC, pl.BlockSpec(memory_space=pl.ANY)], out_specs=(HBM_SPEC,) * n,
        input_output_aliases={i: i for i in range(n)},
        compiler_params=pltpu.CompilerParams(has_side_effects=pltpu.SideEffectType.DATAFLOW_SIDE_EFFECTING),
    )(*[inflight[k] for k in BIG], send_sems, recv_sems, after)
    return dict(zip(BIG, outs))


def _gather_finish(arrs):
    n = len(BIG)

    def body(*refs):
        out = dict(zip(BIG, refs[n:2 * n]))
        send_sems, recv_sems = refs[2 * n:]
        x, y, c = _me()
        net = _Copies(send_sems, recv_sems)
        for k in BIG:
            for px, py in _other_chips(x, y):
                net.send(_half_block(out[k], k, 2 * px + py, c), _half_block(out[k], k, 2 * px + py, c), (x, y, 1 - c),
                         _half_block(out[k], k, 2 * px + py, 1 - c))
        net.finish()

    outs = pl.pallas_call(
        body, name="gather_finish", out_shape=[jax.ShapeDtypeStruct(_full_shape(k), BF16) for k in BIG],
        in_specs=[HBM_SPEC] * n, out_specs=[HBM_SPEC] * n, input_output_aliases={i: i for i in range(n)},
        scratch_shapes=[pltpu.SemaphoreType.DMA((N_BEHIND,)), pltpu.SemaphoreType.DMA((N_BEHIND,))],
    )(*[arrs[k] for k in BIG])
    return dict(zip(BIG, outs))


def _row_chunks(ref, rows, n):
    step = rows // n
    return [ref.at[pl.ds(i * step, step), :] for i in range(n)]


def _half_pieces(ref, k, half):
    sp = BIG_SPEC[k]
    hr = sp["rows"] // 2
    if sp["axis"] == 1:
        return [ref.at[pl.ds(pl.multiple_of(half * hr, 16), hr), :]]
    return [ref.at[pl.ds(pl.multiple_of(s * sp["rows"] + half * hr, 16), hr), :] for s in range(N_CHIPS)]


def _half_shape(k):
    rows, cols = _full_shape(k)
    return rows // 2, cols


def _stacked_pieces(ref, k):
    sp = BIG_SPEC[k]
    hr = sp["rows"] // 2
    return [ref] if sp["axis"] == 1 else [ref.at[pl.ds(s * hr, hr), :] for s in range(N_CHIPS)]


def _chip_part(ref, k, s):
    sp = BIG_SPEC[k]
    hr = sp["rows"] // 2
    if sp["axis"] == 1:
        return ref.at[:, pl.ds(pl.multiple_of(s * sp["cols"], 128), sp["cols"])]
    return ref.at[pl.ds(pl.multiple_of(s * hr, 16), hr), :]


def _halves_to_sibling(grads, name):
    n = len(BIG)
    chunks = {k: max(BIG_SPEC[k]["down"] // 2, 1) if BIG_SPEC[k]["axis"] == 1 else 1 for k in BIG}
    n_sem = sum(chunks[k] if BIG_SPEC[k]["axis"] == 1 else N_CHIPS for k in BIG)

    def body(*refs):
        g = dict(zip(BIG, refs[:n]))
        out = dict(zip(BIG, refs[n:2 * n]))
        send_sems, recv_sems = refs[2 * n:]
        x, y, c = _me()
        net = _Copies(send_sems, recv_sems)
        for k in BIG:
            hr = BIG_SPEC[k]["rows"] // 2
            for src, dst in zip(_half_pieces(g[k], k, 1 - c), _stacked_pieces(out[k], k)):
                for s, d in zip(_row_chunks(src, hr, chunks[k]), _row_chunks(dst, hr, chunks[k])):
                    net.send(s, d, (x, y, 1 - c), d)
        net.finish()

    outs = pl.pallas_call(
        body, name=name, out_shape=[jax.ShapeDtypeStruct(_half_shape(k), BF16) for k in BIG],
        in_specs=[HBM_SPEC] * n, out_specs=[HBM_SPEC] * n,
        scratch_shapes=[pltpu.SemaphoreType.DMA((n_sem,)), pltpu.SemaphoreType.DMA((n_sem,))],
    )(*[grads[k] for k in BIG])
    return dict(zip(BIG, outs))


def _add_half(g, other, k, core, name):
    sp = BIG_SPEC[k]
    hr, cols = sp["rows"] // 2, _full_shape(k)[1]
    tr = _pick(hr, (256, 352, 128))
    nb = hr // tr
    if sp["axis"] == 1:
        grid = (nb,)
        g_spec = pl.BlockSpec((tr, cols), lambda i, c: (c[0] * nb + i, 0))
        h_spec = pl.BlockSpec((tr, cols), lambda i, c: (i, 0))
    else:
        grid = (N_CHIPS, nb)
        g_spec = pl.BlockSpec((tr, cols), lambda s, i, c: ((2 * s + c[0]) * nb + i, 0))
        h_spec = pl.BlockSpec((tr, cols), lambda s, i, c: (s * nb + i, 0))

    def kern(c_ref, a_ref, b_ref, o_ref):
        o_ref[...] = (a_ref[...].astype(F32) + b_ref[...].astype(F32)).astype(BF16)

    return pl.pallas_call(
        kern, name=name, out_shape=jax.ShapeDtypeStruct(_half_shape(k), BF16),
        grid_spec=pltpu.PrefetchScalarGridSpec(num_scalar_prefetch=1, grid=grid, in_specs=[g_spec, h_spec],
                                               out_specs=h_spec),
        compiler_params=_cparams(("parallel",) * len(grid)),
    )(core, g, other)


def _part_shape(k):
    return N_CHIPS - 1, BIG_SPEC[k]["rows"] // 2, BIG_SPEC[k]["cols"]


def _scatter_chip_sums(sums, name):
    n = len(BIG)

    def body(*refs):
        src = dict(zip(BIG, refs[:n]))
        out = dict(zip(BIG, refs[n:2 * n]))
        send_sems, recv_sems = refs[2 * n:]
        x, y, c = _me()
        net = _Copies(send_sems, recv_sems)
        for k in BIG:
            for j, (px, py) in enumerate(_other_chips(x, y)):
                net.send(_chip_part(src[k], k, 2 * px + py), out[k].at[j], (px, py, c), out[k].at[j])
        net.finish()

    outs = pl.pallas_call(
        body, name=name, out_shape=[jax.ShapeDtypeStruct(_part_shape(k), BF16) for k in BIG],
        in_specs=[HBM_SPEC] * n, out_specs=[HBM_SPEC] * n,
        scratch_shapes=[pltpu.SemaphoreType.DMA((N_BEHIND,)), pltpu.SemaphoreType.DMA((N_BEHIND,))],
    )(*[sums[k] for k in BIG])
    return dict(zip(BIG, outs))


def _scatter_copies(sums, parts, send_sems, recv_sems):
    x, y, c = _me()
    copies = []
    for i, k in enumerate(BIG):
        for j, (px, py) in enumerate(_other_chips(x, y)):
            copies.append(pltpu.make_async_remote_copy(
                src_ref=_chip_part(sums[k], k, 2 * px + py), dst_ref=parts[k].at[j], send_sem=send_sems.at[3 * i + j],
                recv_sem=recv_sems.at[3 * i + j], device_id=(px, py, c), device_id_type=MESH))
    return copies


def _scatter_start(sums):
    n = len(BIG)
    lands = [pltpu.with_memory_space_constraint(lax.empty(_part_shape(k), BF16), pltpu.HBM) for k in BIG]

    def body(*refs):
        outs = refs[2 * n + 2:4 * n + 2]
        for cp in _scatter_copies(dict(zip(BIG, outs[:n])), dict(zip(BIG, outs[n:])), refs[2 * n], refs[2 * n + 1]):
            cp.start()
        refs[4 * n + 2][...] = jnp.zeros((8, 128), F32)

    outs = pl.pallas_call(
        body, name="scatter_start",
        out_shape=(pltpu.SemaphoreType.DMA((N_BEHIND,)), pltpu.SemaphoreType.DMA((N_BEHIND,)),
                   *[pltpu.HBM(_half_shape(k), BF16) for k in BIG], *[pltpu.HBM(_part_shape(k), BF16) for k in BIG],
                   jax.ShapeDtypeStruct((8, 128), F32)),
        in_specs=[HBM_SPEC] * (2 * n), out_specs=(SEM_SPEC, SEM_SPEC, *[HBM_SPEC] * (2 * n), VMEM_SPEC),
        input_output_aliases={i: i + 2 for i in range(2 * n)},
        compiler_params=pltpu.CompilerParams(has_side_effects=pltpu.SideEffectType.DATAFLOW_SIDE_EFFECTING),
    )(*[pltpu.with_memory_space_constraint(sums[k], pltpu.HBM) for k in BIG], *lands)
    return outs[0], outs[1], outs[2:2 * n + 2], outs[2 * n + 2]


def _scatter_wait(send_sems, recv_sems, inflight, after):
    n = len(BIG)

    def body(*refs):
        for cp in _scatter_copies(dict(zip(BIG, refs[:n])), dict(zip(BIG, refs[n:2 * n])), refs[2 * n], refs[2 * n + 1]):
            cp.wait_send()
            cp.wait_recv()

    outs = pl.pallas_call(
        body, name="scatter_wait",
        out_shape=(*[pltpu.HBM(_half_shape(k), BF16) for k in BIG], *[pltpu.HBM(_part_shape(k), BF16) for k in BIG]),
        in_specs=[HBM_SPEC] * (2 * n) + [SEM_SPEC, SEM_SPEC, pl.BlockSpec(memory_space=pl.ANY)],
        out_specs=(HBM_SPEC,) * (2 * n), input_output_aliases={i: i for i in range(2 * n)},
        compiler_params=pltpu.CompilerParams(has_side_effects=pltpu.SideEffectType.DATAFLOW_SIDE_EFFECTING),
    )(*inflight, send_sems, recv_sems, after)
    return dict(zip(BIG, outs[:n])), dict(zip(BIG, outs[n:]))


def _sum_half(parts, own, k, where, layer, into, name):
    sp = BIG_SPEC[k]
    rows, cols, keep = sp["rows"], sp["cols"], sp["keep"]
    hr = rows // 2
    tr = _pick(hr, (256, 352, 128))
    nb = hr // tr
    if sp["axis"] == 1:
        own_spec = pl.BlockSpec((tr, cols), lambda i, w: (i, w[0]))
    else:
        own_spec = pl.BlockSpec((tr, cols), lambda i, w: (w[0] * nb + i, 0))

    def kern(w_ref, p_ref, own_ref, *rest):
        tot = own_ref[...].astype(F32)
        for j in range(N_CHIPS - 1):
            tot = tot + p_ref[j].astype(F32)
        rest[-1][0] = tot[:, :keep]

    in_specs = [pl.BlockSpec((N_CHIPS - 1, tr, cols), lambda i, w: (0, i, 0)), own_spec]
    args = [where, parts, own]
    if into is not None:
        in_specs.append(pl.BlockSpec(memory_space=pl.ANY))
        args.append(into)
    return pl.pallas_call(
        kern, name=name, out_shape=jax.ShapeDtypeStruct((DEPTH, rows, keep), F32),
        grid_spec=pltpu.PrefetchScalarGridSpec(
            num_scalar_prefetch=1, grid=(nb,), in_specs=in_specs,
            out_specs=pl.BlockSpec((1, tr, keep), lambda i, w: (layer, w[1] * nb + i, 0))),
        input_output_aliases={} if into is None else {3: 0},
        compiler_params=_cparams(("parallel",)),
    )(*args)


def _exchange_halves(red):
    n = len(BIG)

    def body(*refs):
        out = dict(zip(BIG, refs[n:2 * n]))
        send_sems, recv_sems = refs[2 * n:]
        x, y, c = _me()
        net = _Copies(send_sems, recv_sems)
        for k in BIG:
            hr = BIG_SPEC[k]["rows"] // 2
            for l in range(DEPTH):
                mine = out[k].at[l, pl.ds(pl.multiple_of(c * hr, 8), hr), :]
                theirs = out[k].at[l, pl.ds(pl.multiple_of((1 - c) * hr, 8), hr), :]
                net.send(mine, mine, (x, y, 1 - c), theirs)
        net.finish()

    outs = pl.pallas_call(
        body, name="exchange_halves",
        out_shape=[jax.ShapeDtypeStruct((DEPTH, BIG_SPEC[k]["rows"], BIG_SPEC[k]["keep"]), F32) for k in BIG],
        in_specs=[HBM_SPEC] * n, out_specs=[HBM_SPEC] * n, input_output_aliases={i: i for i in range(n)},
        scratch_shapes=[pltpu.SemaphoreType.DMA((DEPTH * n,)), pltpu.SemaphoreType.DMA((DEPTH * n,))],
    )(*[red[k] for k in BIG])
    return dict(zip(BIG, outs))


def _adam_vals(g, w, m, v):
    m2 = ADAM_B1 * m + (1.0 - ADAM_B1) * g
    v2 = ADAM_B2 * v + (1.0 - ADAM_B2) * (g * g)
    m_hat = m2 / (1.0 - ADAM_B1 ** ADAM_STEP)
    v_hat = v2 / (1.0 - ADAM_B2 ** ADAM_STEP)
    return -ADAM_LR * (m_hat / (jnp.sqrt(v_hat) + ADAM_EPS) + ADAM_WD * w), m2, v2


def _allreduce_small_adam(groups):
    ng = len(groups)

    def body(*refs):
        ins = [refs[4 * i:4 * i + 4] for i in range(ng)]
        outs = [refs[4 * ng + 4 * i:4 * ng + 4 * i + 4] for i in range(ng)]
        bufs = refs[8 * ng:9 * ng]
        send_sems, recv_sems = refs[9 * ng:]
        x, y, c = _me()
        me = 4 * x + 2 * y + c
        net = _Copies(send_sems, recv_sems)
        for (g_ref, _, _, _), buf in zip(ins, bufs):
            buf[me] = g_ref[...]
            for f in FLIPS:
                px, py, pc = _peer(x, y, c, f)
                net.send(g_ref, buf.at[me], (px, py, pc), buf.at[4 * px + 2 * py + pc])
        net.finish()
        for (_, w_ref, m_ref, v_ref), (gs_ref, d_ref, nm_ref, nv_ref), buf in zip(ins, outs, bufs):
            tot = buf[0]
            for d in range(1, 8):
                tot = tot + buf[d]
            gs_ref[...] = tot
            d_ref[...], nm_ref[...], nv_ref[...] = _adam_vals(tot, w_ref[...], m_ref[...], v_ref[...])

    shapes = [jax.ShapeDtypeStruct(g[0].shape, F32) for g in groups for _ in range(4)]
    outs = pl.pallas_call(
        body, name="allreduce_small", out_shape=shapes, in_specs=[VMEM_SPEC] * (4 * ng), out_specs=[VMEM_SPEC] * (4 * ng),
        scratch_shapes=[pltpu.VMEM((8,) + g[0].shape, F32) for g in groups]
        + [pltpu.SemaphoreType.DMA((7 * ng,)), pltpu.SemaphoreType.DMA((7 * ng,))],
        compiler_params=pltpu.CompilerParams(vmem_limit_bytes=VMEM_LIMIT),
    )(*[t for g in groups for t in g])
    return [outs[4 * i:4 * i + 4] for i in range(ng)]


def _adam(g, w, m, v, name):
    shape = w.shape
    lead, rows, cols = math.prod(shape[:-2]), shape[-2], shape[-1]
    tr = _pick(rows, (256, 352, 64, 8, rows))
    spec = pl.BlockSpec((1, tr, cols), lambda l, i: (l, i, 0))

    def kern(g_ref, w_ref, m_ref, v_ref, d_ref, nm_ref, nv_ref):
        d_ref[...], nm_ref[...], nv_ref[...] = _adam_vals(g_ref[...], w_ref[...], m_ref[...], v_ref[...])

    outs = pl.pallas_call(
        kern, name=name, grid=(lead, rows // tr), in_specs=[spec] * 4, out_specs=[spec] * 3,
        out_shape=[jax.ShapeDtypeStruct((lead, rows, cols), F32)] * 3, compiler_params=_cparams(("parallel", "parallel")),
    )(*[t.reshape(lead, rows, cols) for t in (g, w, m, v)])
    return [o.reshape(shape) for o in outs]


SMALL = ("attn_norm", "sgu_ln_g", "sgu_ln_b", "sgu_w", "sgu_b", "attn_sinks", "dn_a_log", "dn_dt_bias", "dn_norm",
         "ffn_norm", "final_norm")
SMALL_2D = {"attn_norm": (DEPTH, D_MODEL), "ffn_norm": (DEPTH, D_MODEL), "final_norm": (1, D_MODEL),
            "sgu_ln_g": (DEPTH, MIX), "sgu_ln_b": (DEPTH, MIX), "sgu_w": (DEPTH * SGU_G * SGU_T, SGU_T),
            "sgu_b": (DEPTH * SGU_G, SGU_T), "dn_norm": (DEPTH, DN_HD)}
TINY = ("attn_sinks", "dn_a_log", "dn_dt_bias")


def _pack_tiny(vals, extra=None):
    flat = [vals[k].astype(F32).reshape(-1) for k in TINY] + ([] if extra is None else [extra.astype(F32).reshape(-1)])
    n = sum(f.shape[0] for f in flat)
    return jnp.concatenate(flat + [jnp.zeros((8 * 128 - n,), F32)]).reshape(8, 128)


def _unpack_tiny(tile, shapes):
    flat, out, o = tile.reshape(-1), {}, 0
    for k in TINY:
        n = math.prod(shapes[k])
        out[k] = flat[o:o + n].reshape(shapes[k])
        o += n
    return out, flat[o]


def _in_col_segments():
    shard, padded = IN_COLS // N_CHIPS, BIG_SPEC["w_in"]["cols"]
    segs, mine = [], 0
    for a, n in IN_PIECES:
        o = a
        while o < a + n:
            end = min(a + n, (o // shard + 1) * shard)
            segs.append(((o // shard) * padded + o % shard, mine + o - a, end - o))
            o = end
        mine += n
    return segs


def _move_cols(x, segs, out_cols, name):
    layers, rows, cols = x.shape
    tr = _pick(rows, (256, rows))
    gaps, at = [], 0
    for d, w in sorted((d, w) for _, d, w in segs):
        if d > at:
            gaps.append((at, d - at))
        at = d + w
    if at < out_cols:
        gaps.append((at, out_cols - at))

    def kern(x_ref, o_ref):
        for s, d, w in segs:
            o_ref[0, :, d:d + w] = x_ref[0, :, s:s + w]
        for d, w in gaps:
            o_ref[0, :, d:d + w] = jnp.zeros((tr, w), x.dtype)

    return pl.pallas_call(
        kern, name=name, grid=(layers, rows // tr), in_specs=[pl.BlockSpec((1, tr, cols), lambda l, i: (l, i, 0))],
        out_specs=pl.BlockSpec((1, tr, out_cols), lambda l, i: (l, i, 0)),
        out_shape=jax.ShapeDtypeStruct((layers, rows, out_cols), x.dtype), compiler_params=_cparams(("parallel", "parallel")),
    )(x)


WEIGHTS = ("attn_norm", "w_in", "sgu_ln_g", "sgu_ln_b", "sgu_w", "sgu_b", "attn_sinks", "dn_conv_w", "dn_a_log",
           "dn_dt_bias", "dn_norm", "w_branch", "w_out", "ffn_norm", "w_gate_up", "w_down", "final_norm")


def kernel(x, positions, attn_norm, w_in, sgu_ln_g, sgu_ln_b, sgu_w, sgu_b, attn_sinks, dn_conv_w, dn_a_log, dn_dt_bias, dn_norm, w_branch, w_out, ffn_norm, w_gate_up, w_down, final_norm, loss_target, m_attn_norm, m_w_in, m_sgu_ln_g, m_sgu_ln_b, m_sgu_w, m_sgu_b, m_attn_sinks, m_dn_conv_w, m_dn_a_log, m_dn_dt_bias, m_dn_norm, m_w_branch, m_w_out, m_ffn_norm, m_w_gate_up, m_w_down, m_final_norm, v_attn_norm, v_w_in, v_sgu_ln_g, v_sgu_ln_b, v_sgu_w, v_sgu_b, v_attn_sinks, v_dn_conv_w, v_dn_a_log, v_dn_dt_bias, v_dn_norm, v_w_branch, v_w_out, v_ffn_norm, v_w_gate_up, v_w_down, v_final_norm):
    given = dict(locals())
    W = {k: given[k] for k in WEIGHTS}
    M = {k: given["m_" + k] for k in WEIGHTS}
    V = {k: given["v_" + k] for k in WEIGHTS}
    chip = 2 * lax.axis_index("x") + lax.axis_index("y")
    core = lax.axis_index("c")
    chip1 = chip.astype(jnp.int32).reshape(1)
    where = jnp.stack([chip, core]).astype(jnp.int32)

    placed = [{k: _place_shard(W[k].reshape(DEPTH, BIG_SPEC[k]["rows"], BIG_SPEC[k]["keep"]), k, chip1, l,
                               f"place{l}_{k}") for k in BIG} for l in range(DEPTH)]
    full0, conv_full = _gather_layer(placed[0], dn_conv_w.reshape(CONV_ROWS, CONV_COLS))
    send_sems, recv_sems, inflight, token = _gather_start(placed[1], conv_full)
    segs = _in_col_segments()

    def layer_matrices(full, l):
        big = dict(full)
        big["w_in"] = _move_cols(full["w_in"][None], segs, IN_R, f"w_in_cols{l}")[0]
        big["w_branch"] = full["w_branch"].reshape(3, MIX, D_MODEL)
        return big

    def big_of_layer(l, x_l):
        if l == 0:
            return layer_matrices(full0, 0)
        return layer_matrices(_gather_finish(_gather_wait(send_sems, recv_sems, inflight, x_l)), l)

    w = {k: W[k] for k in SMALL}
    w["attn_norm"] = attn_norm + token[0, 0]
    w["dn_conv_w"] = conv_full.reshape(DEPTH, DN_CONV, 3 * MIX)

    core1 = core.astype(jnp.int32).reshape(1)
    back_segs = [(d, s, n) for s, d, n in segs]
    pending = {}

    def on_grads(l, gl):
        gl = dict(gl)
        gl["w_in"] = _move_cols(gl["w_in"][None], back_segs, _full_shape("w_in")[1], f"g_in_cols{l}")[0]
        gl["w_branch"] = gl["w_branch"].reshape(3 * MIX, D_MODEL)
        sibling = _halves_to_sibling(gl, f"halves_to_sibling{l}")
        sums = {k: _add_half(gl[k], sibling[k], k, core1, f"chip_sum{l}_{k}") for k in BIG}
        if l == 0:
            pending[l] = (sums, _scatter_chip_sums(sums, "scatter_chip_sums"))
            return None
        sems_s, sems_r, inflight, tok = _scatter_start(sums)
        pending[l] = (sems_s, sems_r, inflight)
        return tok

    loss, dx, g = _local_step(x[0], positions[0], loss_target[0], w, big_of_layer, on_grads)
    sums1, parts1 = _scatter_wait(*pending[1], dx)
    sums0, parts0 = pending[0]
    red = {k: _sum_half(parts1[k], sums1[k], k, where, 1, None, f"sum1_{k}") for k in BIG}
    red = {k: _sum_half(parts0[k], sums0[k], k, where, 0, red[k], f"sum0_{k}") for k in BIG}
    reduced = _exchange_halves(red)
    grads = {k: reduced[k].reshape(W[k].shape) for k in BIG}

    conv_2d = (CONV_ROWS, N_CHIPS * CONV_COLS)
    conv_zero = jnp.zeros(conv_2d, F32)
    groups = [tuple(d[k].reshape(SMALL_2D[k]) for d in (g, W, M, V)) for k in SMALL_2D]
    groups.append((g["dn_conv_w"].reshape(conv_2d), conv_zero, conv_zero, conv_zero))
    groups.append((_pack_tiny(g, loss), _pack_tiny(W), _pack_tiny(M), _pack_tiny(V)))
    summed = _allreduce_small_adam(groups)
    delta, new_m, new_v = {}, {}, {}
    for k, outs in zip(SMALL_2D, summed):
        for d, t in zip((grads, delta, new_m, new_v), outs):
            d[k] = t.reshape(W[k].shape)
    conv_sum = summed[len(SMALL_2D)][0].reshape(g["dn_conv_w"].shape)
    grads["dn_conv_w"] = lax.dynamic_slice_in_dim(conv_sum, chip * dn_conv_w.shape[2], dn_conv_w.shape[2], axis=2)
    tiny_shapes = {k: W[k].shape for k in TINY}
    tiny, loss_total = _unpack_tiny(summed[-1][0], tiny_shapes)
    grads.update(tiny)
    for d, t in zip((delta, new_m, new_v), summed[-1][1:]):
        d.update(_unpack_tiny(t, tiny_shapes)[0])
    for k in BIG + ("dn_conv_w",):
        delta[k], new_m[k], new_v[k] = _adam(grads[k], W[k], M[k], V[k], "adam_" + k)

    return (loss_total, dx[None], *[grads[k] for k in WEIGHTS], *[delta[k] for k in WEIGHTS],
            *[new_m[k] for k in WEIGHTS], *[new_v[k] for k in WEIGHTS])
```

```python
import functools
import math

import jax
import jax.numpy as jnp
from jax import lax
from jax.experimental import pallas as pl
from jax.experimental.pallas import tpu as pltpu

F32 = jnp.float32
BF16 = jnp.bfloat16
HI = lax.Precision.HIGHEST

D_MODEL = 1024
DEPTH = 2
MIX = 512
EPS = 1e-6
SGU_G, SGU_T = 4, 128
SWA_H, SWA_KV, SWA_HD, WINDOW = 8, 2, 64, 128
ROPE_THETA, ROPE_DIM = 500000.0, 16
DN_H, DN_HD, DN_CONV, DN_C = 4, 128, 4, 64
D_FF = 2816
IN_COLS = 6920
IN_PIECES = ((3848, 3072), (1792, 1536), (3328, 512), (0, 512), (512, 512), (1024, 512), (1536, 128), (1664, 128),
             (3840, 8))
IN_PAD = 120
IN_R = 7040
C_GATE, C_QKV, C_ZC, C_UA, C_VA, C_QB, C_KB, C_VB, C_SM = 0, 3072, 4608, 5120, 5632, 6144, 6656, 6784, 6912

ADAM_LR, ADAM_B1, ADAM_B2, ADAM_EPS, ADAM_WD, ADAM_STEP = 0.001, 0.9, 0.999, 1e-08, 0.01, 10
VMEM_LIMIT = 56 * 1024 * 1024


def _cparams(sem):
    return pltpu.CompilerParams(dimension_semantics=sem, vmem_limit_bytes=VMEM_LIMIT)


def _dg(a, b, ca, cb, prec=None):
    return lax.dot_general(a, b, (((ca,), (cb,)), ((), ())), precision=prec, preferred_element_type=F32)


def _split(x):
    hi = x.astype(BF16)
    return hi, (x - hi.astype(F32)).astype(BF16)


def _dg3_many(as_, bs, ca, cb):
    sa = [_split(a) for a in as_]
    sb = [_split(b) for b in bs]
    hh = [_dg(a[0], b[0], ca, cb) for a, b in zip(sa, sb)]
    hl = [_dg(a[0], b[1], ca, cb) for a, b in zip(sa, sb)]
    lh = [_dg(a[1], b[0], ca, cb) for a, b in zip(sa, sb)]
    return [x + (y + z) for x, y, z in zip(hh, hl, lh)]


def _dg_exact_lhs_many(a01, bs, ca, cb):
    a = a01.astype(BF16)
    b1 = [b.astype(BF16) for b in bs]
    r1 = [b - t.astype(F32) for b, t in zip(bs, b1)]
    b2 = [r.astype(BF16) for r in r1]
    b3 = [(r - t.astype(F32)).astype(BF16) for r, t in zip(r1, b2)]
    d1 = [_dg(a, t, ca, cb) for t in b1]
    d2 = [_dg(a, t, ca, cb) for t in b2]
    d3 = [_dg(a, t, ca, cb) for t in b3]
    return [x + (y + z) for x, y, z in zip(d1, d2, d3)]


def _mm(a, b):
    return _dg(a.astype(BF16), b.astype(BF16), 1, 0)


def _mm_nt(a, b):
    return _dg(a.astype(BF16), b.astype(BF16), 1, 1)


def _mm_tn(a, b):
    return _dg(a.astype(BF16), b.astype(BF16), 0, 0)


def _sigmoid(x):
    return 0.5 * jnp.tanh(0.5 * x) + 0.5


def _silu(x):
    return x * _sigmoid(x)


def _dsilu(x):
    s = _sigmoid(x)
    return s * (1.0 + x * (1.0 - s))


_GC = math.sqrt(2.0 / math.pi)


def _gelu(x):
    return 0.5 * x * (1.0 + jnp.tanh(_GC * (x + 0.044715 * x * x * x)))


def _dgelu(x):
    t = jnp.tanh(_GC * (x + 0.044715 * x * x * x))
    return 0.5 * (1.0 + t) + 0.5 * x * (1.0 - t * t) * _GC * (1.0 + 3.0 * 0.044715 * x * x)


def _softplus(x):
    return jnp.maximum(x, 0.0) + jnp.log(1.0 + jnp.exp(-jnp.abs(x)))


def _acc(ref, val, i):
    @pl.when(i == 0)
    def _():
        ref[...] = val

    @pl.when(i > 0)
    def _():
        ref[...] += val


def _halo_rows(dtype):
    return 8 * 4 // jnp.dtype(dtype).itemsize


def _tok_call(body, name, S, TB, tok_in, const_in=(), tok_out=(), acc_out=(), prev_in=(), next_in=(), smem_in=()):
    nb = S // TB
    in_specs, args = [], []
    for a, w, cb in tok_in:
        in_specs.append(pl.BlockSpec((TB, w), functools.partial(lambda i, cb: (i, cb), cb=cb)))
        args.append(a)
    for a, w, cb in prev_in:
        hr = _halo_rows(a.dtype)
        in_specs.append(pl.BlockSpec((hr, w), functools.partial(
            lambda i, cb, r: (jnp.maximum(i * r - 1, 0), cb), cb=cb, r=TB // hr)))
        args.append(a)
    for a, w, cb in next_in:
        hr = _halo_rows(a.dtype)
        in_specs.append(pl.BlockSpec((hr, w), functools.partial(
            lambda i, cb, r, last: (jnp.minimum((i + 1) * r, last), cb), cb=cb, r=TB // hr, last=S // hr - 1)))
        args.append(a)
    for a in const_in:
        in_specs.append(pl.BlockSpec(a.shape, lambda i: (0, 0)))
        args.append(a)
    for a in smem_in:
        in_specs.append(pl.BlockSpec(memory_space=pltpu.SMEM))
        args.append(a)
    out_specs, out_shape, aliases, shared = [], [], {}, {}
    for o, (w, dt, *dest) in enumerate(tok_out):
        if not dest:
            out_specs.append(pl.BlockSpec((TB, w), lambda i: (i, 0)))
            out_shape.append(jax.ShapeDtypeStruct((S, w), dt))
            continue
        cb, wide = dest
        out_specs.append(pl.BlockSpec((TB, w), functools.partial(lambda i, cb: (i, cb), cb=cb)))
        out_shape.append(jax.ShapeDtypeStruct((S, wide if isinstance(wide, int) else wide.shape[1]), dt))
        if not isinstance(wide, int):
            if id(wide) not in shared:
                shared[id(wide)] = len(args)
                in_specs.append(pl.BlockSpec(memory_space=pl.ANY))
                args.append(wide)
            aliases[shared[id(wide)]] = o
    for shp, dt in acc_out:
        out_specs.append(pl.BlockSpec(shp, lambda i: (0, 0)))
        out_shape.append(jax.ShapeDtypeStruct(shp, dt))
    n_extra = len(shared)

    def kern(*refs):
        n_in = len(in_specs) - n_extra
        body(pl.program_id(0), *refs[:n_in], *refs[n_in + n_extra:])

    return pl.pallas_call(
        kern, name=name, grid=(nb,), in_specs=in_specs, out_specs=out_specs, out_shape=out_shape,
        input_output_aliases=aliases, compiler_params=_cparams(("arbitrary",)),
    )(*args)


MM_BLOCKS = (1024, 1408, 640, 512, 256, 128)


def _pick(n, cands):
    for c in cands:
        if n % c == 0:
            return c
    return n


MM_VMEM_BUDGET = 44 * 1024 * 1024


def _mm_blocks(M, N, K, a_bytes, b_bytes, o_bytes, add_bytes):
    bn = _pick(N, MM_BLOCKS)
    fits = None
    for bk in [K] + [c for c in (2816, 2048) + MM_BLOCKS if c < K and K % c == 0]:
        for bm in [c for c in MM_BLOCKS if M % c == 0 and c >= min(M, 512)]:
            b_bufs = 1 if (bk == K and bn == N) else 2
            need = 2 * bm * bk * a_bytes + b_bufs * bk * bn * b_bytes + 2 * bm * bn * (o_bytes + add_bytes)
            need += bm * bn * 4 if bk < K else 0
            if need <= MM_VMEM_BUDGET:
                fits = fits or (bm, bn, bk)
                if (M // bm) * (N // bn) * (K // bk) >= 4:
                    return bm, bn, bk
    if fits is None:
        raise ValueError(f"no matmul blocks for {(M, N, K)}")
    return fits


def _matmul(a, b, *, ta=False, tb=False, add=None, out_dtype=F32, name):
    M, K = (a.shape[1], a.shape[0]) if ta else a.shape
    N = b.shape[0] if tb else b.shape[1]
    bm, bn, bk = _mm_blocks(M, N, K, a.dtype.itemsize, b.dtype.itemsize, jnp.dtype(out_dtype).itemsize,
                            0 if add is None else add.dtype.itemsize)
    nk = K // bk
    b_mode = dict(pipeline_mode=pl.Buffered(1)) if (bk == K and bn == N) else {}
    a_spec = pl.BlockSpec((bk, bm), lambda i, j, k: (k, i)) if ta else pl.BlockSpec((bm, bk), lambda i, j, k: (i, k))
    b_spec = (pl.BlockSpec((bn, bk), lambda i, j, k: (j, k), **b_mode) if tb
              else pl.BlockSpec((bk, bn), lambda i, j, k: (k, j), **b_mode))
    o_spec = pl.BlockSpec((bm, bn), lambda i, j, k: (i, j))
    ca, cb = (0 if ta else 1), (1 if tb else 0)

    def kern(*refs):
        a_ref, b_ref = refs[:2]
        add_ref = refs[2] if add is not None else None
        o_ref = refs[3] if add is not None else refs[2]
        p = _dg(a_ref[...].astype(BF16), b_ref[...].astype(BF16), ca, cb)

        def finish(r):
            if add is not None:
                r = r + add_ref[...].astype(F32)
            o_ref[...] = r.astype(out_dtype)

        if nk == 1:
            finish(p)
            return
        acc_ref = refs[-1]
        k = pl.program_id(2)

        @pl.when(k == 0)
        def _():
            acc_ref[...] = p

        @pl.when((k > 0) & (k < nk - 1))
        def _():
            acc_ref[...] += p

        @pl.when(k == nk - 1)
        def _():
            finish(acc_ref[...] + p)

    in_specs = [a_spec, b_spec] + ([o_spec] if add is not None else [])
    args = (a, b) + ((add,) if add is not None else ())
    return pl.pallas_call(
        kern, name=name, grid=(M // bm, N // bn, nk), in_specs=in_specs, out_specs=o_spec,
        out_shape=jax.ShapeDtypeStruct((M, N), out_dtype),
        scratch_shapes=[pltpu.VMEM((bm, bn), F32)] if nk > 1 else [],
        compiler_params=_cparams(("parallel", "parallel", "arbitrary")),
    )(*args)


def _rms_fwd(x, g, name):
    S = x.shape[0]

    def body(i, x_ref, g_ref, h_ref):
        xv = x_ref[...]
        r = lax.rsqrt(jnp.mean(xv * xv, axis=-1, keepdims=True) + EPS)
        h_ref[...] = (xv * r * g_ref[...]).astype(BF16)

    return _tok_call(body, name, S, min(S, 512), [(x, D_MODEL, 0)], [g], [(D_MODEL, BF16)])[0]


def _rms_bwd_vals(xv, g, dh):
    r = lax.rsqrt(jnp.mean(xv * xv, axis=-1, keepdims=True) + EPS)
    u = dh * g
    dx = r * u - xv * (r * r * r) * jnp.mean(u * xv, axis=-1, keepdims=True)
    dg = jnp.sum(dh * xv * r, axis=0, keepdims=True)
    return dx, dg


def _rms_bwd_add(x, g, dh, dres, name):
    S = x.shape[0]

    def body(i, x_ref, dh_ref, dr_ref, g_ref, dx_ref, dg_ref):
        dx, dg = _rms_bwd_vals(x_ref[...], g_ref[...], dh_ref[...].astype(F32))
        dx_ref[...] = dr_ref[...] + dx
        _acc(dg_ref, dg, i)

    return _tok_call(body, name, S, min(S, 512), [(x, D_MODEL, 0), (dh, D_MODEL, 0), (dres, D_MODEL, 0)], [g],
                     [(D_MODEL, F32)], [((1, D_MODEL), F32)])


def _final_loss(x, g, target):
    S = x.shape[0]

    def body(i, x_ref, t_ref, g_ref, dx_ref, loss_ref, dg_ref):
        xv, gv = x_ref[...], g_ref[...]
        r = lax.rsqrt(jnp.mean(xv * xv, axis=-1, keepdims=True) + EPS)
        e = xv * r * gv - t_ref[...]
        part = 0.5 * jnp.sum(jnp.mean(e * e, axis=-1, keepdims=True), axis=0, keepdims=True)
        dx, dg = _rms_bwd_vals(xv, gv, e * (1.0 / D_MODEL))
        dx_ref[...] = dx
        _acc(loss_ref, jnp.broadcast_to(part, (1, 128)), i)
        _acc(dg_ref, dg, i)

    return _tok_call(body, "final_loss", S, min(S, 512), [(x, D_MODEL, 0), (target, D_MODEL, 0)], [g],
                     [(D_MODEL, F32)], [((1, 128), F32), ((1, D_MODEL), F32)])


def _swiglu_fwd(gu, name):
    S = gu.shape[0]

    def body(i, gu_ref, a_ref):
        a_ref[...] = (_silu(gu_ref[:, :D_FF].astype(F32)) * gu_ref[:, D_FF:].astype(F32)).astype(BF16)

    return _tok_call(body, name, S, min(S, 256), [(gu, 2 * D_FF, 0)], [], [(D_FF, BF16)])[0]


def _swiglu_bwd(gu, dact, name):
    S = gu.shape[0]

    def body(i, gu_ref, da_ref, dgu_ref):
        gg, uu, da = gu_ref[:, :D_FF].astype(F32), gu_ref[:, D_FF:].astype(F32), da_ref[...].astype(F32)
        dgu_ref[:, :D_FF] = (da * uu * _dsilu(gg)).astype(BF16)
        dgu_ref[:, D_FF:] = (da * _silu(gg)).astype(BF16)

    return _tok_call(body, name, S, min(S, 256), [(gu, 2 * D_FF, 0), (dact, D_FF, 0)], [], [(2 * D_FF, BF16)])[0]


def _merge_fwd(proj, bds, name):
    S = proj.shape[0]

    def body(i, g0, g1, g2, b0, b1, b2, m_ref):
        m = jnp.zeros(m_ref.shape, F32)
        for gr, br in ((g0, b0), (g1, b1), (g2, b2)):
            m = m + _sigmoid(gr[...].astype(F32)) * br[...].astype(F32)
        m_ref[...] = m.astype(BF16)

    tok = [(proj, D_MODEL, n) for n in range(3)] + [(b, D_MODEL, 0) for b in bds]
    return _tok_call(body, name, S, min(S, 512), tok, [], [(D_MODEL, BF16)])[0]


def _merge_bwd(proj, bds, dm, name):
    S = proj.shape[0]

    def body(i, g0, g1, g2, b0, b1, b2, dm_ref, d0, d1, d2, dgp_ref):
        dmv = dm_ref[...]
        for n, (gr, br, dr) in enumerate(((g0, b0, d0), (g1, b1, d1), (g2, b2, d2))):
            s = _sigmoid(gr[...].astype(F32))
            dr[...] = (dmv * s).astype(BF16)
            dgp_ref[:, n * D_MODEL:(n + 1) * D_MODEL] = (dmv * br[...].astype(F32) * s * (1.0 - s)).astype(BF16)

    tok = [(proj, D_MODEL, n) for n in range(3)] + [(b, D_MODEL, 0) for b in bds] + [(dm, D_MODEL, 0)]
    return _tok_call(body, name, S, min(S, 512), tok, [],
                     [(D_MODEL, BF16)] * 3 + [(3 * D_MODEL, BF16, C_GATE // (3 * D_MODEL), IN_R)])


def _sgu_ln(v, lg, lb):
    mu = jnp.mean(v, axis=-1, keepdims=True)
    vc = v - mu
    rstd = lax.rsqrt(jnp.mean(vc * vc, axis=-1, keepdims=True) + EPS)
    vhat = vc * rstd
    return vhat, rstd, vhat * lg + lb


def _sgu_fwd(proj, lg, lb, wc, bst, name):
    S = proj.shape[0]

    def body(i, ua_ref, va_ref, lg_ref, lb_ref, wc_ref, bs_ref, o_ref):
        u = _gelu(ua_ref[...].astype(F32))
        _, _, vn = _sgu_ln(_gelu(va_ref[...].astype(F32)), lg_ref[...], lb_ref[...])
        for g in range(SGU_G):
            sl = slice(g * 128, (g + 1) * 128)
            mixed = _mm(wc_ref[sl, :], vn[:, sl]) + bs_ref[:, g:g + 1]
            o_ref[:, sl] = (u[:, sl] * mixed).astype(BF16)

    return _tok_call(body, name, S, SGU_T, [(proj, MIX, C_UA // MIX), (proj, MIX, C_VA // MIX)], [lg, lb, wc, bst],
                     [(MIX, BF16)])[0]


def _sgu_bwd(proj, lg, lb, wc, bst, dout, dproj, name):
    S = proj.shape[0]

    def body(i, ua_ref, va_ref, do_ref, lg_ref, lb_ref, wc_ref, bs_ref, duv_ref, dlg_ref, dlb_ref, dwc_ref,
             dbs_ref):
        ua, va, do = ua_ref[...].astype(F32), va_ref[...].astype(F32), do_ref[...].astype(F32)
        u = _gelu(ua)
        lgv = lg_ref[...]
        vhat, rstd, vn = _sgu_ln(_gelu(va), lgv, lb_ref[...])
        tril = lax.broadcasted_iota(jnp.int32, (128, 128), 0) >= lax.broadcasted_iota(jnp.int32, (128, 128), 1)
        lane4 = lax.broadcasted_iota(jnp.int32, (128, 4), 1)
        gs = range(SGU_G)
        sls = [slice(g * 128, (g + 1) * 128) for g in gs]
        wgs = [wc_ref[sl, :] for sl in sls]
        mixed = [_mm(wgs[g], vn[:, sls[g]]) for g in gs]
        dmix = [do[:, sl] * u[:, sl] for sl in sls]
        dwg = [_mm_nt(dmix[g], vn[:, sls[g]]) for g in gs]
        dvn = jnp.concatenate([_mm_tn(wgs[g], dmix[g]) for g in gs], axis=1)
        dbs = jnp.zeros((128, 4), F32)
        for g in gs:
            duv_ref[:, sls[g]] = (do[:, sls[g]] * (mixed[g] + bs_ref[:, g:g + 1]) * _dgelu(ua[:, sls[g]])).astype(BF16)
            dbs = dbs + jnp.where(lane4 == g, jnp.sum(dmix[g], axis=-1, keepdims=True), 0.0)
            _acc(dwc_ref.at[sls[g], :], jnp.where(tril, dwg[g], 0.0), i)
        _acc(dbs_ref, dbs, i)
        _acc(dlg_ref, jnp.sum(dvn * vhat, axis=0, keepdims=True), i)
        _acc(dlb_ref, jnp.sum(dvn, axis=0, keepdims=True), i)
        dvh = dvn * lgv
        dv = rstd * (dvh - jnp.mean(dvh, axis=-1, keepdims=True) - vhat * jnp.mean(dvh * vhat, axis=-1, keepdims=True))
        duv_ref[:, MIX:] = (dv * _dgelu(va)).astype(BF16)

    return _tok_call(body, name, S, SGU_T, [(proj, MIX, C_UA // MIX), (proj, MIX, C_VA // MIX), (dout, MIX, 0)],
                     [lg, lb, wc, bst], [(2 * MIX, BF16, C_UA // (2 * MIX), dproj)],
                     [((1, MIX), F32), ((1, MIX), F32), ((SGU_G * 128, 128), F32), ((128, 4), F32)])


def _rope_tables(positions):
    S = positions.shape[0]
    inv_freq = ROPE_THETA ** (-jnp.arange(0, ROPE_DIM, 2, dtype=F32) / ROPE_DIM)
    ang = positions.astype(F32)[:, None] * inv_freq
    c, s = jnp.cos(ang), jnp.sin(ang)
    c64 = jnp.concatenate([c, c, jnp.ones((S, SWA_HD - ROPE_DIM), F32)], axis=1)
    s64 = jnp.concatenate([-s, s, jnp.zeros((S, SWA_HD - ROPE_DIM), F32)], axis=1)
    return jnp.tile(c64, (1, 2)), jnp.tile(s64, (1, 2))


def _rope128(x, c, s):
    lane = lax.broadcasted_iota(jnp.int32, x.shape, 1) % SWA_HD
    swapped = jnp.where(lane < ROPE_DIM // 2, pltpu.roll(x, 128 - ROPE_DIM // 2, 1), pltpu.roll(x, ROPE_DIM // 2, 1))
    return x * c + swapped * s


def _rope_t128(y, c, s):
    ys = y * s
    lane = lax.broadcasted_iota(jnp.int32, y.shape, 1) % SWA_HD
    swapped = jnp.where(lane < ROPE_DIM // 2, pltpu.roll(ys, 128 - ROPE_DIM // 2, 1), pltpu.roll(ys, ROPE_DIM // 2, 1))
    return y * c + jnp.where(lane < ROPE_DIM, swapped, 0.0)


def _rope_fwd(proj, cos, sin, name):
    S = proj.shape[0]
    scale = SWA_HD ** -0.5

    def body(i, q_ref, k_ref, v_ref, c_ref, s_ref, qo_ref, ko_ref, vo_ref):
        c, s = c_ref[...], s_ref[...]
        for j in range(4):
            sl = slice(j * 128, (j + 1) * 128)
            qo_ref[:, sl] = (_rope128(q_ref[:, sl].astype(F32), c, s) * scale).astype(BF16)
        ko_ref[...] = _rope128(k_ref[...].astype(F32), c, s).astype(BF16)
        vo_ref[...] = v_ref[...].astype(BF16)

    return _tok_call(body, name, S, min(S, 512),
                     [(proj, MIX, C_QB // MIX), (proj, 128, C_KB // 128), (proj, 128, C_VB // 128), (cos, 128, 0),
                      (sin, 128, 0)], [], [(MIX, BF16), (128, BF16), (128, BF16)])


def _rope_bwd(dq, dk, dv, cos, sin, dproj, name):
    S = dq.shape[0]
    scale = SWA_HD ** -0.5
    width = C_SM - C_QB

    def body(i, dq_ref, dk_ref, dv_ref, c_ref, s_ref, o_ref):
        c, s = c_ref[...], s_ref[...]
        for j in range(4):
            sl = slice(j * 128, (j + 1) * 128)
            o_ref[:, sl] = _rope_t128(dq_ref[:, sl] * scale, c, s).astype(BF16)
        o_ref[:, C_KB - C_QB:C_VB - C_QB] = _rope_t128(dk_ref[...], c, s).astype(BF16)
        o_ref[:, C_VB - C_QB:] = dv_ref[...].astype(BF16)

    return _tok_call(body, name, S, min(S, 512),
                     [(dq, MIX, 0), (dk, 128, 0), (dv, 128, 0), (cos, 128, 0), (sin, 128, 0)], [],
                     [(width, BF16, C_QB // width, dproj)])[0]


def _swa_band(i, k_ref, v_ref):
    pstart = pl.multiple_of(jnp.maximum(i - 1, 0) * WINDOW, WINDOW)
    cstart = pl.multiple_of(i * WINDOW, WINDOW)
    kb = jnp.concatenate([k_ref[pl.ds(pstart, WINDOW), :], k_ref[pl.ds(cstart, WINDOW), :]], axis=0)
    vb = jnp.concatenate([v_ref[pl.ds(pstart, WINDOW), :], v_ref[pl.ds(cstart, WINDOW), :]], axis=0)
    qi = lax.broadcasted_iota(jnp.int32, (WINDOW, 2 * WINDOW), 0)
    sj = lax.broadcasted_iota(jnp.int32, (WINDOW, 2 * WINDOW), 1)
    mask = (sj > qi) & (sj <= qi + WINDOW) & ((i > 0) | (sj >= WINDOW))
    return kb, vb, mask, pstart, cstart


def _swa_probs(qs, kh, mask, sinks):
    logits = [jnp.where(mask, _dg(qh, kh, 1, 1), -1e30) for qh in qs]
    m = [jnp.maximum(jnp.max(l, axis=-1, keepdims=True), s) for l, s in zip(logits, sinks)]
    p = [jnp.exp(l - mm) for l, mm in zip(logits, m)]
    ps = [jnp.exp(s - mm) for s, mm in zip(sinks, m)]
    inv = [1.0 / (jnp.sum(pp, axis=-1, keepdims=True) + s) for pp, s in zip(p, ps)]
    return [pp * iv for pp, iv in zip(p, inv)], [s * iv for s, iv in zip(ps, inv)]


def _swa_fwd(q, k, v, sinks, name):
    S = q.shape[0]
    G = SWA_H // SWA_KV

    def body(i, q_ref, k_ref, v_ref, s_ref, o_ref):
        kb, vb, mask, _, _ = _swa_band(i, k_ref, v_ref)
        qv = q_ref[...]
        for kv in range(SWA_KV):
            ksl = slice(kv * SWA_HD, (kv + 1) * SWA_HD)
            heads = range(kv * G, (kv + 1) * G)
            pn, _ = _swa_probs([qv[:, h * SWA_HD:(h + 1) * SWA_HD] for h in heads], kb[:, ksl], mask,
                               [s_ref[0, h] for h in heads])
            outs = [_dg(p.astype(BF16), vb[:, ksl], 1, 0) for p in pn]
            for h, o in zip(heads, outs):
                o_ref[:, h * SWA_HD:(h + 1) * SWA_HD] = o.astype(BF16)

    return _tok_call(body, name, S, WINDOW, [(q, MIX, 0)], [k, v], [(MIX, BF16)], smem_in=[sinks])[0]


def _swa_bwd(q, k, v, sinks, dout, name):
    S = q.shape[0]

    def body(i, q_ref, do_ref, k_ref, v_ref, s_ref, dq_ref, dk_ref, dv_ref, ds_ref):
        kb, vb, mask, pstart, cstart = _swa_band(i, k_ref, v_ref)
        qv, dov = q_ref[...], do_ref[...]
        lane = lax.broadcasted_iota(jnp.int32, (1, 128), 1)
        dsink = jnp.zeros((1, 128), F32)
        dkb, dvb = [], []
        G = SWA_H // SWA_KV
        for kv in range(SWA_KV):
            ksl = slice(kv * SWA_HD, (kv + 1) * SWA_HD)
            heads = range(kv * G, (kv + 1) * G)
            qs = [qv[:, h * SWA_HD:(h + 1) * SWA_HD] for h in heads]
            dos = [dov[:, h * SWA_HD:(h + 1) * SWA_HD].astype(BF16) for h in heads]
            pn, psn = _swa_probs(qs, kb[:, ksl], mask, [s_ref[0, h] for h in heads])
            dp = [_dg(d, vb[:, ksl], 1, 1) for d in dos]
            delta = [jnp.sum(a * b, axis=-1, keepdims=True) for a, b in zip(dp, pn)]
            dsc = [(p * (a - d)).astype(BF16) for p, a, d in zip(pn, dp, delta)]
            dqs = [_dg(s, kb[:, ksl], 1, 0) for s in dsc]
            dks = [_dg(s, qh, 0, 0) for s, qh in zip(dsc, qs)]
            dvs = [_dg(p.astype(BF16), d, 0, 0) for p, d in zip(pn, dos)]
            for n_, h in enumerate(heads):
                dq_ref[:, h * SWA_HD:(h + 1) * SWA_HD] = dqs[n_]
                dsink = dsink + jnp.where(lane == h, -jnp.sum(psn[n_] * delta[n_], axis=0, keepdims=True), 0.0)
            dkb.append((dks[0] + dks[1]) + (dks[2] + dks[3]))
            dvb.append((dvs[0] + dvs[1]) + (dvs[2] + dvs[3]))
        dkb = jnp.concatenate(dkb, axis=1)
        dvb = jnp.concatenate(dvb, axis=1)

        @pl.when(i == 0)
        def _():
            dk_ref[...] = jnp.zeros_like(dk_ref)
            dv_ref[...] = jnp.zeros_like(dv_ref)

        dk_ref[pl.ds(pstart, WINDOW), :] += dkb[:WINDOW]
        dv_ref[pl.ds(pstart, WINDOW), :] += dvb[:WINDOW]
        dk_ref[pl.ds(cstart, WINDOW), :] += dkb[WINDOW:]
        dv_ref[pl.ds(cstart, WINDOW), :] += dvb[WINDOW:]
        _acc(ds_ref, dsink, i)

    return _tok_call(body, name, S, WINDOW, [(q, MIX, 0), (dout, MIX, 0)], [k, v], [(MIX, F32)],
                     [((S, 128), F32), ((S, 128), F32), ((1, 128), F32)], smem_in=[sinks])


def _shift_rows(xs, k):
    return xs if k == 0 else pltpu.roll(xs, k, 0)


def _dn_conv(x_ref, p_ref, w_ref, i):
    hr = p_ref.shape[0]
    halo = jnp.where(i > 0, p_ref[...].astype(F32), 0.0)
    xs = jnp.concatenate([halo, x_ref[...].astype(F32)], axis=0)
    sh = [_shift_rows(xs, DN_CONV - 1 - t)[hr:] for t in range(DN_CONV)]
    pre = sh[0] * w_ref[0:1, :]
    for t in range(1, DN_CONV):
        pre = pre + sh[t] * w_ref[t:t + 1, :]
    return pre, sh


def _dn_gates(sm, alog, dtb):
    lane = lax.broadcasted_iota(jnp.int32, sm.shape, 1)
    return jnp.where(lane < DN_H, _sigmoid(sm), -jnp.exp(alog) * _softplus(sm + dtb))


def _dn_pre_fwd(proj, conv_w, alog_l, dtb_l, name):
    S = proj.shape[0]
    scale = DN_HD ** -0.5

    def body(i, x_ref, sm_ref, p_ref, w_ref, al_ref, db_ref, q_ref, k_ref, v_ref, bg_ref):
        pre, _ = _dn_conv(x_ref, p_ref, w_ref, i)
        a = _silu(pre)
        for h in range(DN_H):
            sl = slice(h * DN_HD, (h + 1) * DN_HD)
            qh, kh = a[:, sl], a[:, MIX + h * DN_HD:MIX + (h + 1) * DN_HD]
            q_ref[:, sl] = qh * (lax.rsqrt(jnp.sum(qh * qh, axis=-1, keepdims=True) + EPS) * scale)
            k_ref[:, sl] = kh * lax.rsqrt(jnp.sum(kh * kh, axis=-1, keepdims=True) + EPS)
        v_ref[...] = a[:, 2 * MIX:]
        bg_ref[...] = _dn_gates(sm_ref[...].astype(F32), al_ref[...], db_ref[...])

    TB = min(S, 256)
    return _tok_call(body, name, S, TB, [(proj, 3 * MIX, C_QKV // (3 * MIX)), (proj, 128, C_SM // 128)],
                     [conv_w, alog_l, dtb_l], [(MIX, F32), (MIX, F32), (MIX, F32), (128, F32)],
                     prev_in=[(proj, 3 * MIX, C_QKV // (3 * MIX))])


def _dn_pre_bwd1(proj, conv_w, alog_l, dtb_l, dq, dk, dv, dbg, dproj, name):
    S = proj.shape[0]
    scale = DN_HD ** -0.5

    def body(i, x_ref, sm_ref, dq_ref, dk_ref, dv_ref, dbg_ref, p_ref, w_ref, al_ref, db_ref, dpre_ref, dsm_ref,
             dw_ref, dal_ref, ddb_ref):
        pre, sh = _dn_conv(x_ref, p_ref, w_ref, i)
        a = _silu(pre)
        da_parts = []
        for part, (g_ref, sc) in enumerate(((dq_ref, scale), (dk_ref, 1.0))):
            for h in range(DN_H):
                xh = a[:, part * MIX + h * DN_HD:part * MIX + (h + 1) * DN_HD]
                rs = lax.rsqrt(jnp.sum(xh * xh, axis=-1, keepdims=True) + EPS)
                y = xh * rs
                dy = g_ref[:, h * DN_HD:(h + 1) * DN_HD] * sc
                da_parts.append(rs * (dy - y * jnp.sum(dy * y, axis=-1, keepdims=True)))
        da_parts.append(dv_ref[...])
        dpre = jnp.concatenate(da_parts, axis=1) * _dsilu(pre)
        dpre_ref[...] = dpre
        dw = jnp.concatenate([jnp.sum(dpre * sh[t], axis=0, keepdims=True) for t in range(DN_CONV)], axis=0)
        _acc(dw_ref, dw, i)
        sm, al, db, dbg_v = sm_ref[...].astype(F32), al_ref[...], db_ref[...], dbg_ref[...]
        lane = lax.broadcasted_iota(jnp.int32, sm.shape, 1)
        sg = _sigmoid(sm)
        gneg = -jnp.exp(al)
        is_g = (lane >= DN_H) & (lane < 2 * DN_H)
        d_al = jnp.where(is_g, dbg_v * gneg * _sigmoid(sm + db), 0.0)
        dsm_ref[...] = jnp.where(lane < DN_H, dbg_v * sg * (1.0 - sg), d_al).astype(BF16)
        _acc(ddb_ref, jnp.sum(d_al, axis=0, keepdims=True), i)
        _acc(dal_ref, jnp.sum(jnp.where(is_g, dbg_v * gneg * _softplus(sm + db), 0.0), axis=0, keepdims=True), i)

    TB = min(S, 256)
    return _tok_call(body, name, S, TB,
                     [(proj, 3 * MIX, C_QKV // (3 * MIX)), (proj, 128, C_SM // 128), (dq, MIX, 0), (dk, MIX, 0),
                      (dv, MIX, 0), (dbg, 128, 0)], [conv_w, alog_l, dtb_l],
                     [(3 * MIX, F32), (128, BF16, C_SM // 128, dproj)],
                     [((DN_CONV, 3 * MIX), F32), ((1, 128), F32), ((1, 128), F32)],
                     prev_in=[(proj, 3 * MIX, C_QKV // (3 * MIX))])


def _dn_pre_bwd2(dpre, conv_w, dproj, name):
    S = dpre.shape[0]
    TB = min(S, 256)
    nb = S // TB

    def body(i, d_ref, n_ref, w_ref, o_ref):
        halo = jnp.where(i < nb - 1, n_ref[...], 0.0)
        ds = jnp.concatenate([d_ref[...], halo], axis=0)
        out = ds[:TB] * w_ref[DN_CONV - 1:DN_CONV, :]
        for t in range(DN_CONV - 1):
            k = DN_CONV - 1 - t
            out = out + pltpu.roll(ds, TB + 8 - k, 0)[:TB] * w_ref[t:t + 1, :]
        o_ref[...] = out.astype(BF16)

    return _tok_call(body, name, S, TB, [(dpre, 3 * MIX, 0)], [conv_w],
                     [(3 * MIX, BF16, C_QKV // (3 * MIX), dproj)], next_in=[(dpre, 3 * MIX, 0)])[0]


def _dn_decay_terms(bgs, heads):
    C = DN_C
    ri = lax.broadcasted_iota(jnp.int32, (C, C), 0)
    ci = lax.broadcasted_iota(jnp.int32, (C, C), 1)
    tril, eye = ri >= ci, ri == ci
    beta = [b[:, h:h + 1] for b, h in zip(bgs, heads)]
    gcol = _dg_exact_lhs_many(tril, [jnp.broadcast_to(b[:, DN_H + h:DN_H + h + 1], (C, C))
                                     for b, h in zip(bgs, heads)], 1, 0)
    grow = [jnp.sum(jnp.where(eye, g, 0.0), axis=0, keepdims=True) for g in gcol]
    decay = [jnp.exp(jnp.where(tril, g - r, -1e30)) for g, r in zip(gcol, grow)]
    e_gc = [jnp.exp(g[:, 0:1]) for g in gcol]
    e_kd = [jnp.exp(g[C - 1:C, 0:1] - g[:, 0:1]) for g in gcol]
    cdec = [jnp.exp(g[C - 1:C, 0:1]) for g in gcol]
    return beta, decay, e_gc, e_kd, cdec


def _dn_nb(S):
    return 4 if S % (4 * DN_C) == 0 else 1


def _dn_prep_fwd(q, k, v, bg, name):
    S = q.shape[0]
    C, NB = DN_C, _dn_nb(S)
    TB = NB * C

    def kern(q_ref, k_ref, v_ref, bg_ref, t_ref, uw_ref, at_ref, qd_ref, kd_ref, dec_ref):
        lane = lax.broadcasted_iota(jnp.int32, (C, 128), 1)
        ri = lax.broadcasted_iota(jnp.int32, (C, C), 0)
        ci = lax.broadcasted_iota(jnp.int32, (C, C), 1)
        tril, eye = ri >= ci, ri == ci
        chains = [(cb, h) for cb in range(NB) for h in range(DN_H)]
        rows = lambda cb: slice(cb * C, (cb + 1) * C)
        head = lambda h: slice(h * DN_HD, (h + 1) * DN_HD)
        beta, decay, e_gc, e_kd, cdec = _dn_decay_terms([bg_ref[rows(cb), :] for cb, _ in chains],
                                                        [h for _, h in chains])
        qs = [q_ref[rows(cb), head(h)] for cb, h in chains]
        ks = [k_ref[rows(cb), head(h)] for cb, h in chains]
        kb = [kh * b for kh, b in zip(ks, beta)]
        x = [-jnp.where(ri > ci, _mm_nt(a, kh) * d, 0.0) for a, kh, d in zip(kb, ks, decay)]
        tm = [jnp.where(eye, 1.0, 0.0) + xi for xi in x]
        p = x
        p = _dg3_many(p, p, 1, 0)
        for it in range(5):
            if it == 4:
                tm = [t + tp for t, tp in zip(tm, _dg3_many(tm, p, 1, 0))]
                break
            both = _dg3_many([jnp.concatenate([t, pp], axis=0) for t, pp in zip(tm, p)], p, 1, 0)
            tm = [t + b[:C] for t, b in zip(tm, both)]
            p = [b[C:] for b in both]
        rhs = [jnp.concatenate([v_ref[rows(cb), head(h)] * b, a * e], axis=1)
               for (cb, h), b, a, e in zip(chains, beta, kb, e_gc)]
        sol = _dg3_many(tm, rhs, 1, 0)
        attn = [_mm_nt(qh, kh) * d for qh, kh, d in zip(qs, ks, decay)]
        for n_, (cb, h) in enumerate(chains):
            rs, sl, hc = rows(cb), head(h), slice(h * C, (h + 1) * C)
            t_ref[rs, hc] = tm[n_]
            uw_ref[rs, sl] = sol[n_][:, :DN_HD]
            uw_ref[rs, MIX + h * DN_HD:MIX + (h + 1) * DN_HD] = sol[n_][:, DN_HD:]
            at_ref[rs, hc] = attn[n_]
            qd_ref[rs, sl] = (qs[n_] * e_gc[n_]).astype(BF16)
            kd_ref[rs, sl] = (ks[n_] * e_kd[n_]).astype(BF16)
        for cb in range(NB):
            dec = jnp.zeros((C, 128), F32)
            for h in range(DN_H):
                dec = dec + jnp.where(lane == h, cdec[cb * DN_H + h], 0.0)
            dec_ref[rows(cb), :] = dec

    tok = lambda w: pl.BlockSpec((TB, w), lambda i: (i, 0))
    return pl.pallas_call(
        kern, name=name, grid=(S // TB,), in_specs=[tok(MIX), tok(MIX), tok(MIX), tok(128)],
        out_specs=[tok(DN_H * C), tok(2 * MIX), tok(DN_H * C), tok(MIX), tok(MIX), tok(128)],
        out_shape=[jax.ShapeDtypeStruct((S, DN_H * C), F32), jax.ShapeDtypeStruct((S, 2 * MIX), F32),
                   jax.ShapeDtypeStruct((S, DN_H * C), F32), jax.ShapeDtypeStruct((S, MIX), BF16),
                   jax.ShapeDtypeStruct((S, MIX), BF16), jax.ShapeDtypeStruct((S, 128), F32)],
        compiler_params=_cparams(("parallel",)),
    )(q, k, v, bg)


def _dn_scan_fwd(uw, at, qd, kd, dec, name):
    S = uw.shape[0]
    C, NB = DN_C, _dn_nb(S)
    TB = NB * C
    SR = DN_H * DN_HD

    def kern(uw_ref, at_ref, qd_ref, kd_ref, dec_ref, o_ref, vn_ref, st_ref, state):
        @pl.when(pl.program_id(0) == 0)
        def _():
            state[...] = jnp.zeros_like(state)

        for cb in range(NB):
            rs = slice(cb * C, (cb + 1) * C)
            hs = range(DN_H)
            sls = [slice(h * DN_HD, (h + 1) * DN_HD) for h in hs]
            s_in = [state[sl, :] for sl in sls]
            ws = [_mm(uw_ref[rs, MIX + h * DN_HD:MIX + (h + 1) * DN_HD], s_in[h]) for h in hs]
            os_ = [_mm(qd_ref[rs, sls[h]], s_in[h]) for h in hs]
            vnew = [uw_ref[rs, sls[h]] - ws[h] for h in hs]
            oa = [_mm(at_ref[rs, h * C:(h + 1) * C], vnew[h]) for h in hs]
            kv = [_mm_tn(kd_ref[rs, sls[h]], vnew[h]) for h in hs]
            for h in hs:
                o_ref[rs, sls[h]] = os_[h] + oa[h]
                state[sls[h], :] = s_in[h] * dec_ref[cb * C:cb * C + 1, h:h + 1] + kv[h]
                st_ref[cb * SR + h * DN_HD:cb * SR + (h + 1) * DN_HD, :] = s_in[h]
                vn_ref[rs, sls[h]] = vnew[h]

    tok = lambda w: pl.BlockSpec((TB, w), lambda i: (i, 0))
    return pl.pallas_call(
        kern, name=name, grid=(S // TB,), in_specs=[tok(2 * MIX), tok(DN_H * C), tok(MIX), tok(MIX), tok(128)],
        out_specs=[tok(MIX), tok(MIX), pl.BlockSpec((NB * SR, DN_HD), lambda i: (i, 0))],
        out_shape=[jax.ShapeDtypeStruct((S, MIX), F32), jax.ShapeDtypeStruct((S, MIX), F32),
                   jax.ShapeDtypeStruct((S // C * SR, DN_HD), F32)],
        scratch_shapes=[pltpu.VMEM((SR, DN_HD), F32)],
        compiler_params=_cparams(("arbitrary",)),
    )(uw, at, qd, kd, dec)


def _dn_core_fwd(q, k, v, bg, name):
    tm, uw, at, qd, kd, dec = _dn_prep_fwd(q, k, v, bg, name + "_prep")
    o, vn, st = _dn_scan_fwd(uw, at, qd, kd, dec, name + "_scan")
    return o, dict(tm=tm, uw=uw, at=at, qd=qd, kd=kd, dec=dec, vn=vn, st=st)


def _dn_scan_bwd(sv, do, name):
    S = do.shape[0]
    C, NB = DN_C, _dn_nb(S)
    TB = NB * C
    SR = DN_H * DN_HD
    nb = S // TB

    def kern(do_ref, uw_ref, at_ref, qd_ref, kd_ref, dec_ref, vn_ref, st_ref, dvn_ref, dw_ref, dkd_ref, dc_ref, dstate):
        @pl.when(pl.program_id(0) == 0)
        def _():
            dstate[...] = jnp.zeros_like(dstate)

        lane = lax.broadcasted_iota(jnp.int32, (C, 128), 1)
        for cb in reversed(range(NB)):
            rs = slice(cb * C, (cb + 1) * C)
            dcrow = jnp.zeros((C, 128), F32)
            for h in range(DN_H):
                sl = slice(h * DN_HD, (h + 1) * DN_HD)
                doh, ds_o = do_ref[rs, sl], dstate[sl, :]
                s_in = st_ref[cb * SR + h * DN_HD:cb * SR + (h + 1) * DN_HD, :]
                d_vnew = _mm_tn(at_ref[rs, h * C:(h + 1) * C], doh) + _mm(kd_ref[rs, sl], ds_o)
                dvn_ref[rs, sl] = d_vnew
                dw_ref[rs, sl] = -_mm_nt(d_vnew, s_in)
                dkd_ref[rs, sl] = _mm_nt(vn_ref[rs, sl], ds_o)
                d_c = jnp.sum(jnp.sum(ds_o * s_in, axis=1, keepdims=True), axis=0, keepdims=True)
                dcrow = dcrow + jnp.where(lane == h, d_c, 0.0)
                dstate[sl, :] = (ds_o * dec_ref[cb * C:cb * C + 1, h:h + 1] + _mm_tn(qd_ref[rs, sl], doh)
                                 - _mm_tn(uw_ref[rs, MIX + h * DN_HD:MIX + (h + 1) * DN_HD], d_vnew))
            dc_ref[rs, :] = dcrow

    tok = lambda w: pl.BlockSpec((TB, w), lambda i: (nb - 1 - i, 0))
    return pl.pallas_call(
        kern, name=name, grid=(nb,),
        in_specs=[tok(MIX), tok(2 * MIX), tok(DN_H * C), tok(MIX), tok(MIX), tok(128), tok(MIX),
                  pl.BlockSpec((NB * SR, DN_HD), lambda i: (nb - 1 - i, 0))],
        out_specs=[tok(MIX), tok(MIX), tok(MIX), tok(128)],
        out_shape=[jax.ShapeDtypeStruct((S, MIX), F32)] * 3 + [jax.ShapeDtypeStruct((S, 128), F32)],
        scratch_shapes=[pltpu.VMEM((SR, DN_HD), F32)],
        compiler_params=_cparams(("arbitrary",)),
    )(do, sv["uw"], sv["at"], sv["qd"], sv["kd"], sv["dec"], sv["vn"], sv["st"])


def _dn_chunk_bwd(q, k, v, bg, sv, do, dvn, dw, dkd, dc, name):
    S = q.shape[0]
    C, NB = DN_C, _dn_nb(S)
    TB = NB * C
    SR = DN_H * DN_HD

    def kern(q_ref, k_ref, v_ref, bg_ref, t_ref, uw_ref, vn_ref, st_ref, do_ref, dvn_ref, dw_ref, dkd_ref, dc_ref,
             dq_ref, dk_ref, dv_ref, dbg_ref):
        lane = lax.broadcasted_iota(jnp.int32, (C, 128), 1)
        ri = lax.broadcasted_iota(jnp.int32, (C, C), 0)
        ci = lax.broadcasted_iota(jnp.int32, (C, C), 1)
        tril, eye, last = ri >= ci, ri == ci, ri[:, 0:1] == C - 1
        chains = [(cb, h) for cb in range(NB) for h in range(DN_H)]
        each = lambda f, *ls: [f(*a) for a in zip(*ls)]
        rsum = lambda t: jnp.sum(t, axis=-1, keepdims=True)
        rows = lambda cb: slice(cb * C, (cb + 1) * C)
        head = lambda h: slice(h * DN_HD, (h + 1) * DN_HD)
        tok = lambda ref: [ref[rows(cb), head(h)] for cb, h in chains]
        beta, decay, e_gc, e_kd, cdec = _dn_decay_terms([bg_ref[rows(cb), :] for cb, _ in chains],
                                                        [h for _, h in chains])
        qs, ks, vs, dos, vnew, d_kd = tok(q_ref), tok(k_ref), tok(v_ref), tok(do_ref), tok(vn_ref), tok(dkd_ref)
        s_in = [st_ref[cb * SR + h * DN_HD:cb * SR + (h + 1) * DN_HD, :] for cb, h in chains]
        d_c = [dc_ref[cb * C:cb * C + 1, h:h + 1] for cb, h in chains]
        kb = each(lambda a, b: a * b, ks, beta)
        kk = each(_mm_nt, kb, ks)
        attn = each(lambda a, b, d: _mm_nt(a, b) * d, qs, ks, decay)
        d_qd = each(_mm_nt, dos, s_in)
        d_attn = each(_mm_nt, dos, vnew)
        d_sol = [jnp.concatenate([dvn_ref[rows(cb), head(h)], dw_ref[rows(cb), head(h)]], axis=1) for cb, h in chains]
        sol = [jnp.concatenate([uw_ref[rows(cb), head(h)], uw_ref[rows(cb), MIX + h * DN_HD:MIX + (h + 1) * DN_HD]],
                               axis=1) for cb, h in chains]
        d_rhs = _dg3_many([t_ref[rows(cb), h * C:(h + 1) * C] for cb, h in chains], d_sol, 0, 0)
        d_a = _dg3_many(d_rhs, sol, 1, 1)
        d_kk = each(lambda a, d: jnp.where(ri > ci, -a, 0.0) * d, d_a, decay)
        d_qk = each(lambda a, d: a * d, d_attn, decay)
        dm = each(lambda a, b, c_, d: a * b + c_ * d, d_kk, kk, d_attn, attn)
        d_vb = [t[:, :DN_HD] for t in d_rhs]
        dz = [t[:, DN_HD:] for t in d_rhs]
        d_kb = each(lambda z, e, a, kh: z * e + _mm(a, kh), dz, e_gc, d_kk, ks)
        d_k = each(lambda a, b, c_, q: _mm_tn(a, b) + _mm_tn(c_, q), d_kk, kb, d_qk, qs)
        d_q = each(lambda a, kh, b, e: _mm(a, kh) + b * e, d_qk, ks, d_qd, e_gc)
        t_kd = each(lambda a, kh, e: rsum(a * kh * e), d_kd, ks, e_kd)
        d_gl = each(lambda t, c_, cd: jnp.sum(t, axis=0, keepdims=True) + c_ * cd, t_kd, d_c, cdec)
        d_gc = each(lambda z, a, e, m, b, q, t, gl:
                    rsum(z * a) * e + rsum(m) - rsum(jnp.where(eye, jnp.sum(m, axis=0, keepdims=True), 0.0))
                    + rsum(b * q) * e - t + jnp.where(last, gl, 0.0),
                    dz, kb, e_gc, dm, d_qd, qs, t_kd, d_gl)
        d_g = _dg_exact_lhs_many(ri <= ci, [jnp.broadcast_to(t, (C, 128)) for t in d_gc], 1, 0)
        d_beta = each(lambda a, v_, b, kh: rsum(a * v_) + rsum(b * kh), d_vb, vs, d_kb, ks)
        for n_, (cb, h) in enumerate(chains):
            dq_ref[rows(cb), head(h)] = d_q[n_]
            dk_ref[rows(cb), head(h)] = d_k[n_] + d_kd[n_] * e_kd[n_] + d_kb[n_] * beta[n_]
            dv_ref[rows(cb), head(h)] = d_vb[n_] * beta[n_]
        for cb in range(NB):
            dbg = jnp.zeros((C, 128), F32)
            for h in range(DN_H):
                n_ = cb * DN_H + h
                dbg = dbg + jnp.where(lane == h, d_beta[n_], 0.0) + jnp.where(lane == DN_H + h, d_g[n_], 0.0)
            dbg_ref[rows(cb), :] = dbg

    tok = lambda w: pl.BlockSpec((TB, w), lambda i: (i, 0))
    return pl.pallas_call(
        kern, name=name, grid=(S // TB,),
        in_specs=[tok(MIX), tok(MIX), tok(MIX), tok(128), tok(DN_H * C), tok(2 * MIX), tok(MIX),
                  pl.BlockSpec((NB * SR, DN_HD), lambda i: (i, 0)), tok(MIX), tok(MIX), tok(MIX), tok(MIX), tok(128)],
        out_specs=[tok(MIX), tok(MIX), tok(MIX), tok(128)],
        out_shape=[jax.ShapeDtypeStruct((S, MIX), F32)] * 3 + [jax.ShapeDtypeStruct((S, 128), F32)],
        compiler_params=_cparams(("parallel",)),
    )(q, k, v, bg, sv["tm"], sv["uw"], sv["vn"], sv["st"], do, dvn, dw, dkd, dc)


def _dn_core_bwd(q, k, v, bg, sv, do, name):
    dvn, dw, dkd, dc = _dn_scan_bwd(sv, do, name + "_scan")
    return _dn_chunk_bwd(q, k, v, bg, sv, do, dvn, dw, dkd, dc, name + "_chunk")


def _dn_post_fwd(o, proj, ng, name):
    S = o.shape[0]

    def body(i, o_ref, z_ref, g_ref, out_ref):
        gv = g_ref[...]
        for h in range(DN_H):
            sl = slice(h * DN_HD, (h + 1) * DN_HD)
            oh = o_ref[:, sl]
            r = lax.rsqrt(jnp.mean(oh * oh, axis=-1, keepdims=True) + EPS)
            out_ref[:, sl] = (oh * r * gv * _silu(z_ref[:, sl].astype(F32))).astype(BF16)

    return _tok_call(body, name, S, min(S, 512), [(o, MIX, 0), (proj, MIX, C_ZC // MIX)], [ng], [(MIX, BF16)])[0]


def _dn_post_bwd(o, proj, ng, dout, dproj, name):
    S = o.shape[0]

    def body(i, o_ref, z_ref, do_ref, g_ref, dov_ref, dz_ref, dg_ref):
        gv = g_ref[...]
        dg = jnp.zeros((1, DN_HD), F32)
        for h in range(DN_H):
            sl = slice(h * DN_HD, (h + 1) * DN_HD)
            oh, zh, dh = o_ref[:, sl], z_ref[:, sl].astype(F32), do_ref[:, sl].astype(F32)
            r = lax.rsqrt(jnp.mean(oh * oh, axis=-1, keepdims=True) + EPS)
            dz_ref[:, sl] = (dh * oh * r * gv * _dsilu(zh)).astype(BF16)
            dx, dgh = _rms_bwd_vals(oh, gv, dh * _silu(zh))
            dov_ref[:, sl] = dx
            dg = dg + dgh
        _acc(dg_ref, dg, i)

    return _tok_call(body, name, S, min(S, 512), [(o, MIX, 0), (proj, MIX, C_ZC // MIX), (dout, MIX, 0)], [ng],
                     [(MIX, F32), (MIX, BF16, C_ZC // MIX, dproj)], [((1, DN_HD), F32)])


def _layer_params(w, big, l):
    lane = jnp.arange(128)
    is_g = (lane >= DN_H) & (lane < 2 * DN_H)
    spread = lambda t: jnp.where(is_g, jnp.tile(t, 128 // DN_H), 0.0).reshape(1, 128)
    tril = jnp.tril(jnp.ones((SGU_T, SGU_T), bool))
    return dict(
        win=big["w_in"], wb=big["w_branch"], wout=big["w_out"], wgu=big["w_gate_up"], wdown=big["w_down"],
        conv=w["dn_conv_w"][l], attn_norm=w["attn_norm"][l].reshape(1, -1), ffn_norm=w["ffn_norm"][l].reshape(1, -1),
        lg=w["sgu_ln_g"][l].reshape(1, -1), lb=w["sgu_ln_b"][l].reshape(1, -1),
        wc=jnp.where(tril, w["sgu_w"][l], 0.0).reshape(SGU_G * SGU_T, SGU_T), bst=w["sgu_b"][l].T,
        sinks=w["attn_sinks"][l].reshape(1, -1), alog=spread(w["dn_a_log"][l]), dtb=spread(w["dn_dt_bias"][l]),
        ng=w["dn_norm"][l].reshape(1, -1))


def _layer_fwd(x, p, cos, sin, l):
    n = lambda s: f"l{l}_{s}"
    h = _rms_fwd(x, p["attn_norm"], n("rms1"))
    proj = _matmul(h, p["win"], out_dtype=BF16, name=n("mm_in"))
    out_a = _sgu_fwd(proj, p["lg"], p["lb"], p["wc"], p["bst"], n("sgu_fwd"))
    qr, kr, vr = _rope_fwd(proj, cos, sin, n("rope_fwd"))
    out_b = _swa_fwd(qr, kr, vr, p["sinks"], n("swa_fwd"))
    q, k, v, bg = _dn_pre_fwd(proj, p["conv"], p["alog"], p["dtb"], n("dn_pre_fwd"))
    o, dn = _dn_core_fwd(q, k, v, bg, n("dn_core_fwd"))
    out_c = _dn_post_fwd(o, proj, p["ng"], n("dn_post_fwd"))
    outs = (out_a, out_b, out_c)
    bds = [_matmul(outs[j], p["wb"][j], out_dtype=BF16, name=n(f"mm_branch{j}")) for j in range(3)]
    merged = _merge_fwd(proj, bds, n("merge_fwd"))
    x1 = _matmul(merged, p["wout"], add=x, name=n("mm_out"))
    h2 = _rms_fwd(x1, p["ffn_norm"], n("rms2"))
    gu = _matmul(h2, p["wgu"], out_dtype=BF16, name=n("mm_gu"))
    act = _swiglu_fwd(gu, n("swiglu_fwd"))
    x2 = _matmul(act, p["wdown"], add=x1, name=n("mm_down"))
    saved = dict(x=x, h=h, proj=proj, outs=outs, qr=qr, kr=kr, vr=vr, q=q, k=k, v=v, bg=bg, o=o, dn=dn, bds=bds,
                 merged=merged, x1=x1, h2=h2, gu=gu, act=act)
    return x2, saved


def _layer_bwd(dx2, s, p, cos, sin, l):
    n = lambda t: f"l{l}_{t}"
    proj = s["proj"]
    g = {}
    g["w_down"] = _matmul(s["act"], dx2, ta=True, out_dtype=BF16, name=n("wg_down"))
    dact = _matmul(dx2, p["wdown"], tb=True, out_dtype=BF16, name=n("dg_down"))
    dgu = _swiglu_bwd(s["gu"], dact, n("swiglu_bwd"))
    g["w_gate_up"] = _matmul(s["h2"], dgu, ta=True, out_dtype=BF16, name=n("wg_gu"))
    dh2 = _matmul(dgu, p["wgu"], tb=True, name=n("dg_gu"))
    dx1, g["ffn_norm"] = _rms_bwd_add(s["x1"], p["ffn_norm"], dh2, dx2, n("rms2_bwd"))
    g["w_out"] = _matmul(s["merged"], dx1, ta=True, out_dtype=BF16, name=n("wg_out"))
    dm = _matmul(dx1, p["wout"], tb=True, name=n("dg_out"))
    dbd0, dbd1, dbd2, dproj = _merge_bwd(proj, s["bds"], dm, n("merge_bwd"))
    dbds = (dbd0, dbd1, dbd2)
    g["w_branch"] = jnp.stack([_matmul(s["outs"][j], dbds[j], ta=True, out_dtype=BF16, name=n(f"wg_branch{j}"))
                               for j in range(3)])
    douts = [_matmul(dbds[j], p["wb"][j], tb=True, name=n(f"dg_branch{j}")) for j in range(3)]
    dproj, g["sgu_ln_g"], g["sgu_ln_b"], dwc, dbs = _sgu_bwd(proj, p["lg"], p["lb"], p["wc"], p["bst"], douts[0], dproj,
                                                             n("sgu_bwd"))
    g["sgu_w"] = dwc.reshape(SGU_G, SGU_T, SGU_T)
    g["sgu_b"] = dbs.T
    dqr, dkr, dvr, dsink = _swa_bwd(s["qr"], s["kr"], s["vr"], p["sinks"], douts[1], n("swa_bwd"))
    g["attn_sinks"] = dsink[0, :SWA_H]
    dproj = _rope_bwd(dqr, dkr, dvr, cos, sin, dproj, n("rope_bwd"))
    do, dproj, dng = _dn_post_bwd(s["o"], proj, p["ng"], douts[2], dproj, n("dn_post_bwd"))
    g["dn_norm"] = dng[0]
    dq, dk, dv, dbg = _dn_core_bwd(s["q"], s["k"], s["v"], s["bg"], s["dn"], do, n("dn_core_bwd"))
    dpre, dproj, g["dn_conv_w"], dal, ddb = _dn_pre_bwd1(proj, p["conv"], p["alog"], p["dtb"], dq, dk, dv, dbg, dproj,
                                                         n("dn_pre_bwd1"))
    g["dn_a_log"] = dal[0, DN_H:2 * DN_H]
    g["dn_dt_bias"] = ddb[0, DN_H:2 * DN_H]
    dproj = _dn_pre_bwd2(dpre, p["conv"], dproj, n("dn_pre_bwd2"))
    g["w_in"] = _matmul(s["h"], dproj, ta=True, out_dtype=BF16, name=n("wg_in"))
    dh = _matmul(dproj, p["win"], tb=True, name=n("dg_in"))
    dx, g["attn_norm"] = _rms_bwd_add(s["x"], p["attn_norm"], dh, dx1, n("rms1_bwd"))
    g["attn_norm"], g["ffn_norm"] = g["attn_norm"][0], g["ffn_norm"][0]
    g["sgu_ln_g"], g["sgu_ln_b"] = g["sgu_ln_g"][0], g["sgu_ln_b"][0]
    return dx, g


def _local_step(x, positions, target, w, big_of_layer, on_grads):
    cos, sin = _rope_tables(positions)
    params, saves, xs = [], [], x
    for l in range(DEPTH):
        params.append(_layer_params(w, big_of_layer(l, xs), l))
        xs, sv = _layer_fwd(xs, params[l], cos, sin, l)
        saves.append(sv)
    dx, loss_row, dgf = _final_loss(xs, w["final_norm"].reshape(1, -1), target)
    grads = [None] * DEPTH
    for l in reversed(range(DEPTH)):
        dx, grads[l] = _layer_bwd(dx, saves[l], params[l], cos, sin, l)
        token = on_grads(l, {k: grads[l].pop(k) for k in BIG})
        if token is not None and l > 0:
            params[l - 1] = dict(params[l - 1], ffn_norm=params[l - 1]["ffn_norm"] + token[0, 0])
    stacked = {k: jnp.stack([grads[l][k] for l in range(DEPTH)]) for k in grads[0]}
    stacked["final_norm"] = dgf[0]
    return loss_row[0, 0], dx, stacked


MESH = pl.DeviceIdType.MESH
HBM_SPEC = pl.BlockSpec(memory_space=pltpu.HBM)
VMEM_SPEC = pl.BlockSpec(memory_space=pltpu.VMEM)
N_CHIPS = 4
FLIPS = tuple((fx, fy, fc) for fx in (0, 1) for fy in (0, 1) for fc in (0, 1))[1:]
BIG = ("w_in", "w_branch", "w_out", "w_gate_up", "w_down")
BIG_SPEC = {
    "w_in": dict(rows=1024, cols=1792, axis=1, keep=1730, down=8),
    "w_branch": dict(rows=1536, cols=256, axis=1, keep=256, down=2),
    "w_out": dict(rows=256, cols=1024, axis=0, keep=1024, down=1),
    "w_gate_up": dict(rows=1024, cols=1408, axis=1, keep=1408, down=8),
    "w_down": dict(rows=704, cols=1024, axis=0, keep=1024, down=4),
}
CONV_ROWS, CONV_COLS = DEPTH * DN_CONV, 3 * MIX // N_CHIPS


def _full_shape(k):
    sp = BIG_SPEC[k]
    return (sp["rows"], N_CHIPS * sp["cols"]) if sp["axis"] == 1 else (N_CHIPS * sp["rows"], sp["cols"])


def _me():
    return lax.axis_index("x"), lax.axis_index("y"), lax.axis_index("c")


def _peer(x, y, c, flip):
    fx, fy, fc = flip
    return (1 - x if fx else x, 1 - y if fy else y, 1 - c if fc else c)


class _Copies:
    def __init__(self, send_sems, recv_sems):
        self.send_sems, self.recv_sems, self.k, self.sent, self.landing = send_sems, recv_sems, 0, [], []

    def _copy(self, k, src, dst, to):
        return pltpu.make_async_remote_copy(src_ref=src, dst_ref=dst, send_sem=self.send_sems.at[k],
                                            recv_sem=self.recv_sems.at[k], device_id=to, device_id_type=MESH)

    def send(self, src, dst, to, lands):
        k = self.k
        self.k += 1
        cp = self._copy(k, src, dst, to)
        cp.start()
        self.sent.append(cp)
        self.landing.append(self._copy(k, lands, lands, to))
        return k

    def wait_landed(self, k):
        self.landing[k].wait_recv()

    def finish(self, landed=()):
        for k, cp in enumerate(self.landing):
            if k not in landed:
                cp.wait_recv()
        for cp in self.sent:
            cp.wait_send()


def _place_shard(shard, k, chip, layer, name):
    sp = BIG_SPEC[k]
    rows, cols, keep = sp["rows"], sp["cols"], sp["keep"]
    tr = _pick(rows, (256, 64))
    nb = rows // tr
    if sp["axis"] == 1:
        out_spec = pl.BlockSpec((tr, cols), lambda i, ch: (i, ch[0]))
    else:
        out_spec = pl.BlockSpec((tr, cols), lambda i, ch: (ch[0] * nb + i, 0))

    def kern(ch_ref, x_ref, o_ref):
        v = x_ref[0].astype(BF16)
        if keep == cols:
            o_ref[...] = v
        else:
            o_ref[:, :keep] = v
            o_ref[:, keep:] = jnp.zeros((tr, cols - keep), BF16)

    return pl.pallas_call(
        kern, name=name, out_shape=jax.ShapeDtypeStruct(_full_shape(k), BF16),
        grid_spec=pltpu.PrefetchScalarGridSpec(
            num_scalar_prefetch=1, grid=(nb,),
            in_specs=[pl.BlockSpec((1, tr, keep), lambda i, ch: (layer, i, 0))], out_specs=out_spec),
        compiler_params=_cparams(("parallel",)),
    )(chip, shard)


def _half_block(ref, k, s, half):
    sp = BIG_SPEC[k]
    hr = sp["rows"] // 2
    if sp["axis"] == 1:
        return ref.at[pl.ds(pl.multiple_of(half * hr, 16), hr), pl.ds(pl.multiple_of(s * sp["cols"], 128), sp["cols"])]
    return ref.at[pl.ds(pl.multiple_of(s * sp["rows"] + half * hr, 16), hr), :]


def _other_chips(x, y):
    return [(1 - x, y), (x, 1 - y), (1 - x, 1 - y)]


def _gather_layer(placed, conv):
    n = len(BIG)
    n_sem = 6 * n + 3

    def body(*refs):
        conv_ref = refs[n]
        out = dict(zip(BIG, refs[n + 1:2 * n + 1]))
        conv_out, send_sems, recv_sems, local_sem = refs[2 * n + 1:]
        x, y, c = _me()
        me = 2 * x + y
        chips = _other_chips(x, y)
        net = _Copies(send_sems, recv_sems)

        def conv_block(s):
            return conv_out.at[:, pl.ds(pl.multiple_of(s * CONV_COLS, 128), CONV_COLS)]

        local = pltpu.make_async_copy(conv_ref, conv_block(me), local_sem)
        local.start()
        first = {}
        for k in BIG:
            for j, (px, py) in enumerate(chips):
                first[k, j] = net.send(_half_block(out[k], k, me, c), _half_block(out[k], k, me, c), (px, py, c),
                                       _half_block(out[k], k, 2 * px + py, c))
        for px, py in chips:
            net.send(conv_ref, conv_block(me), (px, py, c), conv_block(2 * px + py))
        for k in BIG:
            for j, (px, py) in enumerate(chips):
                net.wait_landed(first[k, j])
                net.send(_half_block(out[k], k, 2 * px + py, c), _half_block(out[k], k, 2 * px + py, c), (x, y, 1 - c),
                         _half_block(out[k], k, 2 * px + py, 1 - c))
        net.finish(landed=set(first.values()))
        local.wait()

    out_shape = [jax.ShapeDtypeStruct(_full_shape(k), BF16) for k in BIG]
    out_shape.append(jax.ShapeDtypeStruct((CONV_ROWS, N_CHIPS * CONV_COLS), F32))
    outs = pl.pallas_call(
        body, name="gather_layer", out_shape=out_shape, in_specs=[HBM_SPEC] * (n + 1), out_specs=[HBM_SPEC] * (n + 1),
        input_output_aliases={i: i for i in range(n)},
        scratch_shapes=[pltpu.SemaphoreType.DMA((n_sem,)), pltpu.SemaphoreType.DMA((n_sem,)), pltpu.SemaphoreType.DMA],
    )(*[placed[k] for k in BIG], conv)
    return dict(zip(BIG, outs[:n])), outs[n]


SEM_SPEC = pl.BlockSpec(memory_space=pltpu.SEMAPHORE)
N_BEHIND = 3 * len(BIG)


def _behind_copies(arrs, send_sems, recv_sems):
    x, y, c = _me()
    copies = []
    for i, k in enumerate(BIG):
        for j, (px, py) in enumerate(_other_chips(x, y)):
            copies.append(pltpu.make_async_remote_copy(
                src_ref=_half_block(arrs[k], k, 2 * x + y, c), dst_ref=_half_block(arrs[k], k, 2 * x + y, c),
                send_sem=send_sems.at[3 * i + j], recv_sem=recv_sems.at[3 * i + j], device_id=(px, py, c),
                device_id_type=MESH))
    return copies


def _gather_start(placed, after):
    n = len(BIG)

    def body(*refs):
        arrs = dict(zip(BIG, refs[n + 3:2 * n + 3]))
        send_sems, recv_sems = refs[n + 1], refs[n + 2]
        for cp in _behind_copies(arrs, send_sems, recv_sems):
            cp.start()
        refs[2 * n + 3][...] = jnp.zeros((8, 128), F32)

    outs = pl.pallas_call(
        body, name="gather_start",
        out_shape=(pltpu.SemaphoreType.DMA((N_BEHIND,)), pltpu.SemaphoreType.DMA((N_BEHIND,)),
                   *[pltpu.HBM(_full_shape(k), BF16) for k in BIG], jax.ShapeDtypeStruct((8, 128), F32)),
        in_specs=[HBM_SPEC] * n + [pl.BlockSpec(memory_space=pl.ANY)],
        out_specs=(SEM_SPEC, SEM_SPEC, *[HBM_SPEC] * n, VMEM_SPEC),
        input_output_aliases={i: i + 2 for i in range(n)},
        compiler_params=pltpu.CompilerParams(has_side_effects=pltpu.SideEffectType.DATAFLOW_SIDE_EFFECTING),
    )(*[pltpu.with_memory_space_constraint(placed[k], pltpu.HBM) for k in BIG], after)
    return outs[0], outs[1], dict(zip(BIG, outs[2:n + 2])), outs[n + 2]


def _gather_wait(send_sems, recv_sems, inflight, after):
    n = len(BIG)

    def body(*refs):
        arrs = dict(zip(BIG, refs[:n]))
        for cp in _behind_copies(arrs, refs[n], refs[n + 1]):
            cp.wait_send()
            cp.wait_recv()

    outs = pl.pallas_call(
        body, name="gather_wait", out_shape=tuple(pltpu.HBM(_full_shape(k), BF16) for k in BIG),
        in_specs=[HBM_SPEC] * n + [SEM_SPEC, SEM_SPEC, pl.BlockSpec(memory_space=pl.ANY)], out_specs=(HBM_SPEC,) * n,
        input_output_aliases={i: i for i in range(n)},
        compiler_params=pltpu.CompilerParams(has_side_effects=pltpu.SideEffectType.DATAFLOW_SIDE_EFFECTING),
    )(*[inflight[k] for k in BIG], send_sems, recv_sems, after)
    return dict(zip(BIG, outs))


def _gather_finish(arrs):
    n = len(BIG)

    def body(*refs):
        out = dict(zip(BIG, refs[n:2 * n]))
        send_sems, recv_sems = refs[2 * n:]
        x, y, c = _me()
        net = _Copies(send_sems, recv_sems)
        for k in BIG:
            for px, py in _other_chips(x, y):
                net.send(_half_block(out[k], k, 2 * px + py, c), _half_block(out[k], k, 2 * px + py, c), (x, y, 1 - c),
                         _half_block(out[k], k, 2 * px + py, 1 - c))
        net.finish()

    outs = pl.pallas_call(
        body, name="gather_finish", out_shape=[jax.ShapeDtypeStruct(_full_shape(k), BF16) for k in BIG],
        in_specs=[HBM_SPEC] * n, out_specs=[HBM_SPEC] * n, input_output_aliases={i: i for i in range(n)},
        scratch_shapes=[pltpu.SemaphoreType.DMA((N_BEHIND,)), pltpu.SemaphoreType.DMA((N_BEHIND,))],
    )(*[arrs[k] for k in BIG])
    return dict(zip(BIG, outs))


def _row_chunks(ref, rows, n):
    step = rows // n
    return [ref.at[pl.ds(i * step, step), :] for i in range(n)]


def _half_pieces(ref, k, half):
    sp = BIG_SPEC[k]
    hr = sp["rows"] // 2
    if sp["axis"] == 1:
        return [ref.at[pl.ds(pl.multiple_of(half * hr, 16), hr), :]]
    return [ref.at[pl.ds(pl.multiple_of(s * sp["rows"] + half * hr, 16), hr), :] for s in range(N_CHIPS)]


def _half_shape(k):
    rows, cols = _full_shape(k)
    return rows // 2, cols


def _stacked_pieces(ref, k):
    sp = BIG_SPEC[k]
    hr = sp["rows"] // 2
    return [ref] if sp["axis"] == 1 else [ref.at[pl.ds(s * hr, hr), :] for s in range(N_CHIPS)]


def _chip_part(ref, k, s):
    sp = BIG_SPEC[k]
    hr = sp["rows"] // 2
    if sp["axis"] == 1:
        return ref.at[:, pl.ds(pl.multiple_of(s * sp["cols"], 128), sp["cols"])]
    return ref.at[pl.ds(pl.multiple_of(s * hr, 16), hr), :]


def _halves_to_sibling(grads, name):
    n = len(BIG)
    chunks = {k: max(BIG_SPEC[k]["down"] // 2, 1) if BIG_SPEC[k]["axis"] == 1 else 1 for k in BIG}
    n_sem = sum(chunks[k] if BIG_SPEC[k]["axis"] == 1 else N_CHIPS for k in BIG)

    def body(*refs):
        g = dict(zip(BIG, refs[:n]))
        out = dict(zip(BIG, refs[n:2 * n]))
        send_sems, recv_sems = refs[2 * n:]
        x, y, c = _me()
        net = _Copies(send_sems, recv_sems)
        for k in BIG:
            hr = BIG_SPEC[k]["rows"] // 2
            for src, dst in zip(_half_pieces(g[k], k, 1 - c), _stacked_pieces(out[k], k)):
                for s, d in zip(_row_chunks(src, hr, chunks[k]), _row_chunks(dst, hr, chunks[k])):
                    net.send(s, d, (x, y, 1 - c), d)
        net.finish()

    outs = pl.pallas_call(
        body, name=name, out_shape=[jax.ShapeDtypeStruct(_half_shape(k), BF16) for k in BIG],
        in_specs=[HBM_SPEC] * n, out_specs=[HBM_SPEC] * n,
        scratch_shapes=[pltpu.SemaphoreType.DMA((n_sem,)), pltpu.SemaphoreType.DMA((n_sem,))],
    )(*[grads[k] for k in BIG])
    return dict(zip(BIG, outs))


def _add_half(g, other, k, core, name):
    sp = BIG_SPEC[k]
    hr, cols = sp["rows"] // 2, _full_shape(k)[1]
    tr = _pick(hr, (256, 352, 128))
    nb = hr // tr
    if sp["axis"] == 1:
        grid = (nb,)
        g_spec = pl.BlockSpec((tr, cols), lambda i, c: (c[0] * nb + i, 0))
        h_spec = pl.BlockSpec((tr, cols), lambda i, c: (i, 0))
    else:
        grid = (N_CHIPS, nb)
        g_spec = pl.BlockSpec((tr, cols), lambda s, i, c: ((2 * s + c[0]) * nb + i, 0))
        h_spec = pl.BlockSpec((tr, cols), lambda s, i, c: (s * nb + i, 0))

    def kern(c_ref, a_ref, b_ref, o_ref):
        o_ref[...] = (a_ref[...].astype(F32) + b_ref[...].astype(F32)).astype(BF16)

    return pl.pallas_call(
        kern, name=name, out_shape=jax.ShapeDtypeStruct(_half_shape(k), BF16),
        grid_spec=pltpu.PrefetchScalarGridSpec(num_scalar_prefetch=1, grid=grid, in_specs=[g_spec, h_spec],
                                               out_specs=h_spec),
        compiler_params=_cparams(("parallel",) * len(grid)),
    )(core, g, other)


def _part_shape(k):
    return N_CHIPS - 1, BIG_SPEC[k]["rows"] // 2, BIG_SPEC[k]["cols"]


def _scatter_chip_sums(sums, name):
    n = len(BIG)

    def body(*refs):
        src = dict(zip(BIG, refs[:n]))
        out = dict(zip(BIG, refs[n:2 * n]))
        send_sems, recv_sems = refs[2 * n:]
        x, y, c = _me()
        net = _Copies(send_sems, recv_sems)
        for k in BIG:
            for j, (px, py) in enumerate(_other_chips(x, y)):
                net.send(_chip_part(src[k], k, 2 * px + py), out[k].at[j], (px, py, c), out[k].at[j])
        net.finish()

    outs = pl.pallas_call(
        body, name=name, out_shape=[jax.ShapeDtypeStruct(_part_shape(k), BF16) for k in BIG],
        in_specs=[HBM_SPEC] * n, out_specs=[HBM_SPEC] * n,
        scratch_shapes=[pltpu.SemaphoreType.DMA((N_BEHIND,)), pltpu.SemaphoreType.DMA((N_BEHIND,))],
    )(*[sums[k] for k in BIG])
    return dict(zip(BIG, outs))


def _scatter_copies(sums, parts, send_sems, recv_sems):
    x, y, c = _me()
    copies = []
    for i, k in enumerate(BIG):
        for j, (px, py) in enumerate(_other_chips(x, y)):
            copies.append(pltpu.make_async_remote_copy(
                src_ref=_chip_part(sums[k], k, 2 * px + py), dst_ref=parts[k].at[j], send_sem=send_sems.at[3 * i + j],
                recv_sem=recv_sems.at[3 * i + j], device_id=(px, py, c), device_id_type=MESH))
    return copies


def _scatter_start(sums):
    n = len(BIG)
    lands = [pltpu.with_memory_space_constraint(lax.empty(_part_shape(k), BF16), pltpu.HBM) for k in BIG]

    def body(*refs):
        outs = refs[2 * n + 2:4 * n + 2]
        for cp in _scatter_copies(dict(zip(BIG, outs[:n])), dict(zip(BIG, outs[n:])), refs[2 * n], refs[2 * n + 1]):
            cp.start()
        refs[4 * n + 2][...] = jnp.zeros((8, 128), F32)

    outs = pl.pallas_call(
        body, name="scatter_start",
        out_shape=(pltpu.SemaphoreType.DMA((N_BEHIND,)), pltpu.SemaphoreType.DMA((N_BEHIND,)),
                   *[pltpu.HBM(_half_shape(k), BF16) for k in BIG], *[pltpu.HBM(_part_shape(k), BF16) for k in BIG],
                   jax.ShapeDtypeStruct((8, 128), F32)),
        in_specs=[HBM_SPEC] * (2 * n), out_specs=(SEM_SPEC, SEM_SPEC, *[HBM_SPEC] * (2 * n), VMEM_SPEC),
        input_output_aliases={i: i + 2 for i in range(2 * n)},
        compiler_params=pltpu.CompilerParams(has_side_effects=pltpu.SideEffectType.DATAFLOW_SIDE_EFFECTING),
    )(*[pltpu.with_memory_space_constraint(sums[k], pltpu.HBM) for k in BIG], *lands)
    return outs[0], outs[1], outs[2:2 * n + 2], outs[2 * n + 2]


def _scatter_wait(send_sems, recv_sems, inflight, after):
    n = len(BIG)

    def body(*refs):
        for cp in _scatter_copies(dict(zip(BIG, refs[:n])), dict(zip(BIG, refs[n:2 * n])), refs[2 * n], refs[2 * n + 1]):
            cp.wait_send()
            cp.wait_recv()

    outs = pl.pallas_call(
        body, name="scatter_wait",
        out_shape=(*[pltpu.HBM(_half_shape(k), BF16) for k in BIG], *[pltpu.HBM(_part_shape(k), BF16) for k in BIG]),
        in_specs=[HBM_SPEC] * (2 * n) + [SEM_SPEC, SEM_SPEC, pl.BlockSpec(memory_space=pl.ANY)],
        out_specs=(HBM_SPEC,) * (2 * n), input_output_aliases={i: i for i in range(2 * n)},
        compiler_params=pltpu.CompilerParams(has_side_effects=pltpu.SideEffectType.DATAFLOW_SIDE_EFFECTING),
    )(*inflight, send_sems, recv_sems, after)
    return dict(zip(BIG, outs[:n])), dict(zip(BIG, outs[n:]))


def _sum_half(parts, own, k, where, layer, into, name):
    sp = BIG_SPEC[k]
    rows, cols, keep = sp["rows"], sp["cols"], sp["keep"]
    hr = rows // 2
    tr = _pick(hr, (256, 352, 128))
    nb = hr // tr
    if sp["axis"] == 1:
        own_spec = pl.BlockSpec((tr, cols), lambda i, w: (i, w[0]))
    else:
        own_spec = pl.BlockSpec((tr, cols), lambda i, w: (w[0] * nb + i, 0))

    def kern(w_ref, p_ref, own_ref, *rest):
        tot = own_ref[...].astype(F32)
        for j in range(N_CHIPS - 1):
            tot = tot + p_ref[j].astype(F32)
        rest[-1][0] = tot[:, :keep]

    in_specs = [pl.BlockSpec((N_CHIPS - 1, tr, cols), lambda i, w: (0, i, 0)), own_spec]
    args = [where, parts, own]
    if into is not None:
        in_specs.append(pl.BlockSpec(memory_space=pl.ANY))
        args.append(into)
    return pl.pallas_call(
        kern, name=name, out_shape=jax.ShapeDtypeStruct((DEPTH, rows, keep), F32),
        grid_spec=pltpu.PrefetchScalarGridSpec(
            num_scalar_prefetch=1, grid=(nb,), in_specs=in_specs,
            out_specs=pl.BlockSpec((1, tr, keep), lambda i, w: (layer, w[1] * nb + i, 0))),
        input_output_aliases={} if into is None else {3: 0},
        compiler_params=_cparams(("parallel",)),
    )(*args)


def _exchange_halves(red):
    n = len(BIG)

    def body(*refs):
        out = dict(zip(BIG, refs[n:2 * n]))
        send_sems, recv_sems = refs[2 * n:]
        x, y, c = _me()
        net = _Copies(send_sems, recv_sems)
        for k in BIG:
            hr = BIG_SPEC[k]["rows"] // 2
            for l in range(DEPTH):
                mine = out[k].at[l, pl.ds(pl.multiple_of(c * hr, 8), hr), :]
                theirs = out[k].at[l, pl.ds(pl.multiple_of((1 - c) * hr, 8), hr), :]
                net.send(mine, mine, (x, y, 1 - c), theirs)
        net.finish()

    outs = pl.pallas_call(
        body, name="exchange_halves",
        out_shape=[jax.ShapeDtypeStruct((DEPTH, BIG_SPEC[k]["rows"], BIG_SPEC[k]["keep"]), F32) for k in BIG],
        in_specs=[HBM_SPEC] * n, out_specs=[HBM_SPEC] * n, input_output_aliases={i: i for i in range(n)},
        scratch_shapes=[pltpu.SemaphoreType.DMA((DEPTH * n,)), pltpu.SemaphoreType.DMA((DEPTH * n,))],
    )(*[red[k] for k in BIG])
    return dict(zip(BIG, outs))


def _adam_vals(g, w, m, v):
    m2 = ADAM_B1 * m + (1.0 - ADAM_B1) * g
    v2 = ADAM_B2 * v + (1.0 - ADAM_B2) * (g * g)
    m_hat = m2 / (1.0 - ADAM_B1 ** ADAM_STEP)
    v_hat = v2 / (1.0 - ADAM_B2 ** ADAM_STEP)
    return -ADAM_LR * (m_hat / (jnp.sqrt(v_hat) + ADAM_EPS) + ADAM_WD * w), m2, v2


def _allreduce_small_adam(groups):
    ng = len(groups)

    def body(*refs):
        ins = [refs[4 * i:4 * i + 4] for i in range(ng)]
        outs = [refs[4 * ng + 4 * i:4 * ng + 4 * i + 4] for i in range(ng)]
        bufs = refs[8 * ng:9 * ng]
        send_sems, recv_sems = refs[9 * ng:]
        x, y, c = _me()
        me = 4 * x + 2 * y + c
        net = _Copies(send_sems, recv_sems)
        for (g_ref, _, _, _), buf in zip(ins, bufs):
            buf[me] = g_ref[...]
            for f in FLIPS:
                px, py, pc = _peer(x, y, c, f)
                net.send(g_ref, buf.at[me], (px, py, pc), buf.at[4 * px + 2 * py + pc])
        net.finish()
        for (_, w_ref, m_ref, v_ref), (gs_ref, d_ref, nm_ref, nv_ref), buf in zip(ins, outs, bufs):
            tot = buf[0]
            for d in range(1, 8):
                tot = tot + buf[d]
            gs_ref[...] = tot
            d_ref[...], nm_ref[...], nv_ref[...] = _adam_vals(tot, w_ref[...], m_ref[...], v_ref[...])

    shapes = [jax.ShapeDtypeStruct(g[0].shape, F32) for g in groups for _ in range(4)]
    outs = pl.pallas_call(
        body, name="allreduce_small", out_shape=shapes, in_specs=[VMEM_SPEC] * (4 * ng), out_specs=[VMEM_SPEC] * (4 * ng),
        scratch_shapes=[pltpu.VMEM((8,) + g[0].shape, F32) for g in groups]
        + [pltpu.SemaphoreType.DMA((7 * ng,)), pltpu.SemaphoreType.DMA((7 * ng,))],
        compiler_params=pltpu.CompilerParams(vmem_limit_bytes=VMEM_LIMIT),
    )(*[t for g in groups for t in g])
    return [outs[4 * i:4 * i + 4] for i in range(ng)]


def _adam(g, w, m, v, name, lead_block=1):
    shape = w.shape
    lead, rows, cols = math.prod(shape[:-2]), shape[-2], shape[-1]
    tr = _pick(rows, (256, 352, 64, 8, rows))
    spec = pl.BlockSpec((lead_block, tr, cols), lambda l, i: (l, i, 0))

    def kern(g_ref, w_ref, m_ref, v_ref, d_ref, nm_ref, nv_ref):
        d_ref[...], nm_ref[...], nv_ref[...] = _adam_vals(g_ref[...], w_ref[...], m_ref[...], v_ref[...])

    outs = pl.pallas_call(
        kern, name=name, grid=(lead // lead_block, rows // tr), in_specs=[spec] * 4, out_specs=[spec] * 3,
        out_shape=[jax.ShapeDtypeStruct((lead, rows, cols), F32)] * 3, compiler_params=_cparams(("parallel", "parallel")),
    )(*[t.reshape(lead, rows, cols) for t in (g, w, m, v)])
    return [o.reshape(shape) for o in outs]


SMALL = ("attn_norm", "sgu_ln_g", "sgu_ln_b", "sgu_w", "sgu_b", "attn_sinks", "dn_a_log", "dn_dt_bias", "dn_norm",
         "ffn_norm", "final_norm")
SMALL_2D = {"attn_norm": (DEPTH, D_MODEL), "ffn_norm": (DEPTH, D_MODEL), "final_norm": (1, D_MODEL),
            "sgu_ln_g": (DEPTH, MIX), "sgu_ln_b": (DEPTH, MIX), "sgu_w": (DEPTH * SGU_G * SGU_T, SGU_T),
            "sgu_b": (DEPTH * SGU_G, SGU_T), "dn_norm": (DEPTH, DN_HD)}
TINY = ("attn_sinks", "dn_a_log", "dn_dt_bias")


def _pack_tiny(vals, extra=None):
    flat = [vals[k].astype(F32).reshape(-1) for k in TINY] + ([] if extra is None else [extra.astype(F32).reshape(-1)])
    n = sum(f.shape[0] for f in flat)
    return jnp.concatenate(flat + [jnp.zeros((8 * 128 - n,), F32)]).reshape(8, 128)


def _unpack_tiny(tile, shapes):
    flat, out, o = tile.reshape(-1), {}, 0
    for k in TINY:
        n = math.prod(shapes[k])
        out[k] = flat[o:o + n].reshape(shapes[k])
        o += n
    return out, flat[o]


def _in_col_segments():
    shard, padded = IN_COLS // N_CHIPS, BIG_SPEC["w_in"]["cols"]
    segs, mine = [], 0
    for a, n in IN_PIECES:
        o = a
        while o < a + n:
            end = min(a + n, (o // shard + 1) * shard)
            segs.append(((o // shard) * padded + o % shard, mine + o - a, end - o))
            o = end
        mine += n
    return segs


def _move_cols(x, segs, out_cols, name):
    layers, rows, cols = x.shape
    tr = _pick(rows, (256, rows))
    gaps, at = [], 0
    for d, w in sorted((d, w) for _, d, w in segs):
        if d > at:
            gaps.append((at, d - at))
        at = d + w
    if at < out_cols:
        gaps.append((at, out_cols - at))

    def kern(x_ref, o_ref):
        for s, d, w in segs:
            o_ref[0, :, d:d + w] = x_ref[0, :, s:s + w]
        for d, w in gaps:
            o_ref[0, :, d:d + w] = jnp.zeros((tr, w), x.dtype)

    return pl.pallas_call(
        kern, name=name, grid=(layers, rows // tr), in_specs=[pl.BlockSpec((1, tr, cols), lambda l, i: (l, i, 0))],
        out_specs=pl.BlockSpec((1, tr, out_cols), lambda l, i: (l, i, 0)),
        out_shape=jax.ShapeDtypeStruct((layers, rows, out_cols), x.dtype), compiler_params=_cparams(("parallel", "parallel")),
    )(x)


WEIGHTS = ("attn_norm", "w_in", "sgu_ln_g", "sgu_ln_b", "sgu_w", "sgu_b", "attn_sinks", "dn_conv_w", "dn_a_log",
           "dn_dt_bias", "dn_norm", "w_branch", "w_out", "ffn_norm", "w_gate_up", "w_down", "final_norm")


def kernel(x, positions, attn_norm, w_in, sgu_ln_g, sgu_ln_b, sgu_w, sgu_b, attn_sinks, dn_conv_w, dn_a_log, dn_dt_bias, dn_norm, w_branch, w_out, ffn_norm, w_gate_up, w_down, final_norm, loss_target, m_attn_norm, m_w_in, m_sgu_ln_g, m_sgu_ln_b, m_sgu_w, m_sgu_b, m_attn_sinks, m_dn_conv_w, m_dn_a_log, m_dn_dt_bias, m_dn_norm, m_w_branch, m_w_out, m_ffn_norm, m_w_gate_up, m_w_down, m_final_norm, v_attn_norm, v_w_in, v_sgu_ln_g, v_sgu_ln_b, v_sgu_w, v_sgu_b, v_attn_sinks, v_dn_conv_w, v_dn_a_log, v_dn_dt_bias, v_dn_norm, v_w_branch, v_w_out, v_ffn_norm, v_w_gate_up, v_w_down, v_final_norm):
    given = dict(locals())
    W = {k: given[k] for k in WEIGHTS}
    M = {k: given["m_" + k] for k in WEIGHTS}
    V = {k: given["v_" + k] for k in WEIGHTS}
    chip = 2 * lax.axis_index("x") + lax.axis_index("y")
    core = lax.axis_index("c")
    chip1 = chip.astype(jnp.int32).reshape(1)
    where = jnp.stack([chip, core]).astype(jnp.int32)

    placed = [{k: _place_shard(W[k].reshape(DEPTH, BIG_SPEC[k]["rows"], BIG_SPEC[k]["keep"]), k, chip1, l,
                               f"place{l}_{k}") for k in BIG} for l in range(DEPTH)]
    full0, conv_full = _gather_layer(placed[0], dn_conv_w.reshape(CONV_ROWS, CONV_COLS))
    send_sems, recv_sems, inflight, token = _gather_start(placed[1], conv_full)
    segs = _in_col_segments()

    def layer_matrices(full, l):
        big = dict(full)
        big["w_in"] = _move_cols(full["w_in"][None], segs, IN_R, f"w_in_cols{l}")[0]
        big["w_branch"] = full["w_branch"].reshape(3, MIX, D_MODEL)
        return big

    def big_of_layer(l, x_l):
        if l == 0:
            return layer_matrices(full0, 0)
        return layer_matrices(_gather_finish(_gather_wait(send_sems, recv_sems, inflight, x_l)), l)

    w = {k: W[k] for k in SMALL}
    w["attn_norm"] = attn_norm + token[0, 0]
    w["dn_conv_w"] = conv_full.reshape(DEPTH, DN_CONV, 3 * MIX)

    core1 = core.astype(jnp.int32).reshape(1)
    back_segs = [(d, s, n) for s, d, n in segs]
    pending = {}

    def on_grads(l, gl):
        gl = dict(gl)
        gl["w_in"] = _move_cols(gl["w_in"][None], back_segs, _full_shape("w_in")[1], f"g_in_cols{l}")[0]
        gl["w_branch"] = gl["w_branch"].reshape(3 * MIX, D_MODEL)
        sibling = _halves_to_sibling(gl, f"halves_to_sibling{l}")
        sums = {k: _add_half(gl[k], sibling[k], k, core1, f"chip_sum{l}_{k}") for k in BIG}
        if l == 0:
            pending[l] = (sums, _scatter_chip_sums(sums, "scatter_chip_sums"))
            return None
        sems_s, sems_r, inflight, tok = _scatter_start(sums)
        pending[l] = (sems_s, sems_r, inflight)
        return tok

    loss, dx, g = _local_step(x[0], positions[0], loss_target[0], w, big_of_layer, on_grads)
    sums1, parts1 = _scatter_wait(*pending[1], dx)
    sums0, parts0 = pending[0]
    red = {k: _sum_half(parts1[k], sums1[k], k, where, 1, None, f"sum1_{k}") for k in BIG}
    red = {k: _sum_half(parts0[k], sums0[k], k, where, 0, red[k], f"sum0_{k}") for k in BIG}
    reduced = _exchange_halves(red)
    grads = {k: reduced[k].reshape(W[k].shape) for k in BIG}

    conv_2d = (CONV_ROWS, N_CHIPS * CONV_COLS)
    conv_zero = jnp.zeros(conv_2d, F32)
    groups = [tuple(d[k].reshape(SMALL_2D[k]) for d in (g, W, M, V)) for k in SMALL_2D]
    groups.append((g["dn_conv_w"].reshape(conv_2d), conv_zero, conv_zero, conv_zero))
    groups.append((_pack_tiny(g, loss), _pack_tiny(W), _pack_tiny(M), _pack_tiny(V)))
    summed = _allreduce_small_adam(groups)
    delta, new_m, new_v = {}, {}, {}
    for k, outs in zip(SMALL_2D, summed):
        for d, t in zip((grads, delta, new_m, new_v), outs):
            d[k] = t.reshape(W[k].shape)
    conv_sum = summed[len(SMALL_2D)][0].reshape(g["dn_conv_w"].shape)
    grads["dn_conv_w"] = lax.dynamic_slice_in_dim(conv_sum, chip * dn_conv_w.shape[2], dn_conv_w.shape[2], axis=2)
    tiny_shapes = {k: W[k].shape for k in TINY}
    tiny, loss_total = _unpack_tiny(summed[-1][0], tiny_shapes)
    grads.update(tiny)
    for d, t in zip((delta, new_m, new_v), summed[-1][1:]):
        d.update(_unpack_tiny(t, tiny_shapes)[0])
    for k in ("w_branch", "w_out", "w_gate_up", "w_down", "dn_conv_w"):
        delta[k], new_m[k], new_v[k] = _adam(grads[k], W[k], M[k], V[k], "adam_" + k)
    lead_first = lambda t: jnp.transpose(t, (2, 0, 1))
    outs = _adam(*[lead_first(d["w_in"]) for d in (grads, W, M, V)], "adam_w_in", lead_block=IN_COLS // N_CHIPS // 10)
    delta["w_in"], new_m["w_in"], new_v["w_in"] = (jnp.transpose(o, (1, 2, 0)) for o in outs)

    return (loss_total, dx[None], *[grads[k] for k in WEIGHTS], *[delta[k] for k in WEIGHTS],
            *[new_m[k] for k in WEIGHTS], *[new_v[k] for k in WEIGHTS])
```

```python
import functools
import math

import jax
import jax.numpy as jnp
from jax import lax
from jax.experimental import pallas as pl
from jax.experimental.pallas import tpu as pltpu

F32 = jnp.float32
BF16 = jnp.bfloat16
HI = lax.Precision.HIGHEST

D_MODEL = 1024
DEPTH = 2
MIX = 512
EPS = 1e-6
SGU_G, SGU_T = 4, 128
SWA_H, SWA_KV, SWA_HD, WINDOW = 8, 2, 64, 128
ROPE_THETA, ROPE_DIM = 500000.0, 16
DN_H, DN_HD, DN_CONV, DN_C = 4, 128, 4, 64
D_FF = 2816
IN_COLS = 6920
IN_PIECES = ((3848, 3072), (1792, 1536), (3328, 512), (0, 512), (512, 512), (1024, 512), (1536, 128), (1664, 128),
             (3840, 8))
IN_PAD = 120
IN_R = 7040
C_GATE, C_QKV, C_ZC, C_UA, C_VA, C_QB, C_KB, C_VB, C_SM = 0, 3072, 4608, 5120, 5632, 6144, 6656, 6784, 6912

ADAM_LR, ADAM_B1, ADAM_B2, ADAM_EPS, ADAM_WD, ADAM_STEP = 0.001, 0.9, 0.999, 1e-08, 0.01, 10
VMEM_LIMIT = 56 * 1024 * 1024


def _cparams(sem):
    return pltpu.CompilerParams(dimension_semantics=sem, vmem_limit_bytes=VMEM_LIMIT)


def _dg(a, b, ca, cb, prec=None):
    return lax.dot_general(a, b, (((ca,), (cb,)), ((), ())), precision=prec, preferred_element_type=F32)


def _split(x):
    hi = x.astype(BF16)
    return hi, (x - hi.astype(F32)).astype(BF16)


def _dg3_many(as_, bs, ca, cb):
    sa = [_split(a) for a in as_]
    sb = [_split(b) for b in bs]
    hh = [_dg(a[0], b[0], ca, cb) for a, b in zip(sa, sb)]
    hl = [_dg(a[0], b[1], ca, cb) for a, b in zip(sa, sb)]
    lh = [_dg(a[1], b[0], ca, cb) for a, b in zip(sa, sb)]
    return [x + (y + z) for x, y, z in zip(hh, hl, lh)]


def _dg_exact_lhs_many(a01, bs, ca, cb):
    a = a01.astype(BF16)
    b1 = [b.astype(BF16) for b in bs]
    r1 = [b - t.astype(F32) for b, t in zip(bs, b1)]
    b2 = [r.astype(BF16) for r in r1]
    b3 = [(r - t.astype(F32)).astype(BF16) for r, t in zip(r1, b2)]
    d1 = [_dg(a, t, ca, cb) for t in b1]
    d2 = [_dg(a, t, ca, cb) for t in b2]
    d3 = [_dg(a, t, ca, cb) for t in b3]
    return [x + (y + z) for x, y, z in zip(d1, d2, d3)]


def _mm(a, b):
    return _dg(a.astype(BF16), b.astype(BF16), 1, 0)


def _mm_nt(a, b):
    return _dg(a.astype(BF16), b.astype(BF16), 1, 1)


def _mm_tn(a, b):
    return _dg(a.astype(BF16), b.astype(BF16), 0, 0)


def _sigmoid(x):
    return 0.5 * jnp.tanh(0.5 * x) + 0.5


def _silu(x):
    return x * _sigmoid(x)


def _dsilu(x):
    s = _sigmoid(x)
    return s * (1.0 + x * (1.0 - s))


_GC = math.sqrt(2.0 / math.pi)


def _gelu(x):
    return 0.5 * x * (1.0 + jnp.tanh(_GC * (x + 0.044715 * x * x * x)))


def _dgelu(x):
    t = jnp.tanh(_GC * (x + 0.044715 * x * x * x))
    return 0.5 * (1.0 + t) + 0.5 * x * (1.0 - t * t) * _GC * (1.0 + 3.0 * 0.044715 * x * x)


def _softplus(x):
    return jnp.maximum(x, 0.0) + jnp.log(1.0 + jnp.exp(-jnp.abs(x)))


def _acc(ref, val, i):
    @pl.when(i == 0)
    def _():
        ref[...] = val

    @pl.when(i > 0)
    def _():
        ref[...] += val


def _halo_rows(dtype):
    return 8 * 4 // jnp.dtype(dtype).itemsize


def _tok_call(body, name, S, TB, tok_in, const_in=(), tok_out=(), acc_out=(), prev_in=(), next_in=(), smem_in=()):
    nb = S // TB
    in_specs, args = [], []
    for a, w, cb in tok_in:
        in_specs.append(pl.BlockSpec((TB, w), functools.partial(lambda i, cb: (i, cb), cb=cb)))
        args.append(a)
    for a, w, cb in prev_in:
        hr = _halo_rows(a.dtype)
        in_specs.append(pl.BlockSpec((hr, w), functools.partial(
            lambda i, cb, r: (jnp.maximum(i * r - 1, 0), cb), cb=cb, r=TB // hr)))
        args.append(a)
    for a, w, cb in next_in:
        hr = _halo_rows(a.dtype)
        in_specs.append(pl.BlockSpec((hr, w), functools.partial(
            lambda i, cb, r, last: (jnp.minimum((i + 1) * r, last), cb), cb=cb, r=TB // hr, last=S // hr - 1)))
        args.append(a)
    for a in const_in:
        in_specs.append(pl.BlockSpec(a.shape, lambda i: (0, 0)))
        args.append(a)
    for a in smem_in:
        in_specs.append(pl.BlockSpec(memory_space=pltpu.SMEM))
        args.append(a)
    out_specs, out_shape, aliases, shared = [], [], {}, {}
    for o, (w, dt, *dest) in enumerate(tok_out):
        if not dest:
            out_specs.append(pl.BlockSpec((TB, w), lambda i: (i, 0)))
            out_shape.append(jax.ShapeDtypeStruct((S, w), dt))
            continue
        cb, wide = dest
        out_specs.append(pl.BlockSpec((TB, w), functools.partial(lambda i, cb: (i, cb), cb=cb)))
        out_shape.append(jax.ShapeDtypeStruct((S, wide if isinstance(wide, int) else wide.shape[1]), dt))
        if not isinstance(wide, int):
            if id(wide) not in shared:
                shared[id(wide)] = len(args)
                in_specs.append(pl.BlockSpec(memory_space=pl.ANY))
                args.append(wide)
            aliases[shared[id(wide)]] = o
    for shp, dt in acc_out:
        out_specs.append(pl.BlockSpec(shp, lambda i: (0, 0)))
        out_shape.append(jax.ShapeDtypeStruct(shp, dt))
    n_extra = len(shared)

    def kern(*refs):
        n_in = len(in_specs) - n_extra
        body(pl.program_id(0), *refs[:n_in], *refs[n_in + n_extra:])

    return pl.pallas_call(
        kern, name=name, grid=(nb,), in_specs=in_specs, out_specs=out_specs, out_shape=out_shape,
        input_output_aliases=aliases, compiler_params=_cparams(("arbitrary",)),
    )(*args)


MM_BLOCKS = (1024, 1408, 640, 512, 256, 128)


def _pick(n, cands):
    for c in cands:
        if n % c == 0:
            return c
    return n


MM_VMEM_BUDGET = 44 * 1024 * 1024


def _mm_blocks(M, N, K, a_bytes, b_bytes, o_bytes, add_bytes):
    bn = _pick(N, MM_BLOCKS)
    fits = None
    for bk in [K] + [c for c in (2816, 2048) + MM_BLOCKS if c < K and K % c == 0]:
        for bm in [c for c in MM_BLOCKS if M % c == 0 and c >= min(M, 512)]:
            b_bufs = 1 if (bk == K and bn == N) else 2
            need = 2 * bm * bk * a_bytes + b_bufs * bk * bn * b_bytes + 2 * bm * bn * (o_bytes + add_bytes)
            need += bm * bn * 4 if bk < K else 0
            if need <= MM_VMEM_BUDGET:
                fits = fits or (bm, bn, bk)
                if (M // bm) * (N // bn) * (K // bk) >= 4:
                    return bm, bn, bk
    if fits is None:
        raise ValueError(f"no matmul blocks for {(M, N, K)}")
    return fits


def _matmul(a, b, *, ta=False, tb=False, add=None, out_dtype=F32, name):
    M, K = (a.shape[1], a.shape[0]) if ta else a.shape
    N = b.shape[0] if tb else b.shape[1]
    bm, bn, bk = _mm_blocks(M, N, K, a.dtype.itemsize, b.dtype.itemsize, jnp.dtype(out_dtype).itemsize,
                            0 if add is None else add.dtype.itemsize)
    nk = K // bk
    b_mode = dict(pipeline_mode=pl.Buffered(1)) if (bk == K and bn == N) else {}
    a_spec = pl.BlockSpec((bk, bm), lambda i, j, k: (k, i)) if ta else pl.BlockSpec((bm, bk), lambda i, j, k: (i, k))
    b_spec = (pl.BlockSpec((bn, bk), lambda i, j, k: (j, k), **b_mode) if tb
              else pl.BlockSpec((bk, bn), lambda i, j, k: (k, j), **b_mode))
    o_spec = pl.BlockSpec((bm, bn), lambda i, j, k: (i, j))
    ca, cb = (0 if ta else 1), (1 if tb else 0)

    def kern(*refs):
        a_ref, b_ref = refs[:2]
        add_ref = refs[2] if add is not None else None
        o_ref = refs[3] if add is not None else refs[2]
        p = _dg(a_ref[...].astype(BF16), b_ref[...].astype(BF16), ca, cb)

        def finish(r):
            if add is not None:
                r = r + add_ref[...].astype(F32)
            o_ref[...] = r.astype(out_dtype)

        if nk == 1:
            finish(p)
            return
        acc_ref = refs[-1]
        k = pl.program_id(2)

        @pl.when(k == 0)
        def _():
            acc_ref[...] = p

        @pl.when((k > 0) & (k < nk - 1))
        def _():
            acc_ref[...] += p

        @pl.when(k == nk - 1)
        def _():
            finish(acc_ref[...] + p)

    in_specs = [a_spec, b_spec] + ([o_spec] if add is not None else [])
    args = (a, b) + ((add,) if add is not None else ())
    return pl.pallas_call(
        kern, name=name, grid=(M // bm, N // bn, nk), in_specs=in_specs, out_specs=o_spec,
        out_shape=jax.ShapeDtypeStruct((M, N), out_dtype),
        scratch_shapes=[pltpu.VMEM((bm, bn), F32)] if nk > 1 else [],
        compiler_params=_cparams(("parallel", "parallel", "arbitrary")),
    )(*args)


def _rms_fwd(x, g, name):
    S = x.shape[0]

    def body(i, x_ref, g_ref, h_ref):
        xv = x_ref[...]
        r = lax.rsqrt(jnp.mean(xv * xv, axis=-1, keepdims=True) + EPS)
        h_ref[...] = (xv * r * g_ref[...]).astype(BF16)

    return _tok_call(body, name, S, min(S, 512), [(x, D_MODEL, 0)], [g], [(D_MODEL, BF16)])[0]


def _rms_bwd_vals(xv, g, dh):
    r = lax.rsqrt(jnp.mean(xv * xv, axis=-1, keepdims=True) + EPS)
    u = dh * g
    dx = r * u - xv * (r * r * r) * jnp.mean(u * xv, axis=-1, keepdims=True)
    dg = jnp.sum(dh * xv * r, axis=0, keepdims=True)
    return dx, dg


def _rms_bwd_add(x, g, dh, dres, name):
    S = x.shape[0]

    def body(i, x_ref, dh_ref, dr_ref, g_ref, dx_ref, dg_ref):
        dx, dg = _rms_bwd_vals(x_ref[...], g_ref[...], dh_ref[...].astype(F32))
        dx_ref[...] = dr_ref[...] + dx
        _acc(dg_ref, dg, i)

    return _tok_call(body, name, S, min(S, 512), [(x, D_MODEL, 0), (dh, D_MODEL, 0), (dres, D_MODEL, 0)], [g],
                     [(D_MODEL, F32)], [((1, D_MODEL), F32)])


def _final_loss(x, g, target):
    S = x.shape[0]

    def body(i, x_ref, t_ref, g_ref, dx_ref, loss_ref, dg_ref):
        xv, gv = x_ref[...], g_ref[...]
        r = lax.rsqrt(jnp.mean(xv * xv, axis=-1, keepdims=True) + EPS)
        e = xv * r * gv - t_ref[...]
        part = 0.5 * jnp.sum(jnp.mean(e * e, axis=-1, keepdims=True), axis=0, keepdims=True)
        dx, dg = _rms_bwd_vals(xv, gv, e * (1.0 / D_MODEL))
        dx_ref[...] = dx
        _acc(loss_ref, jnp.broadcast_to(part, (1, 128)), i)
        _acc(dg_ref, dg, i)

    return _tok_call(body, "final_loss", S, min(S, 512), [(x, D_MODEL, 0), (target, D_MODEL, 0)], [g],
                     [(D_MODEL, F32)], [((1, 128), F32), ((1, D_MODEL), F32)])


def _swiglu_fwd(gu, name):
    S = gu.shape[0]

    def body(i, gu_ref, a_ref):
        a_ref[...] = (_silu(gu_ref[:, :D_FF].astype(F32)) * gu_ref[:, D_FF:].astype(F32)).astype(BF16)

    return _tok_call(body, name, S, min(S, 256), [(gu, 2 * D_FF, 0)], [], [(D_FF, BF16)])[0]


def _swiglu_bwd(gu, dact, name):
    S = gu.shape[0]

    def body(i, gu_ref, da_ref, dgu_ref):
        gg, uu, da = gu_ref[:, :D_FF].astype(F32), gu_ref[:, D_FF:].astype(F32), da_ref[...].astype(F32)
        dgu_ref[:, :D_FF] = (da * uu * _dsilu(gg)).astype(BF16)
        dgu_ref[:, D_FF:] = (da * _silu(gg)).astype(BF16)

    return _tok_call(body, name, S, min(S, 256), [(gu, 2 * D_FF, 0), (dact, D_FF, 0)], [], [(2 * D_FF, BF16)])[0]


def _merge_fwd(proj, bds, name):
    S = proj.shape[0]

    def body(i, g0, g1, g2, b0, b1, b2, m_ref):
        m = jnp.zeros(m_ref.shape, F32)
        for gr, br in ((g0, b0), (g1, b1), (g2, b2)):
            m = m + _sigmoid(gr[...].astype(F32)) * br[...].astype(F32)
        m_ref[...] = m.astype(BF16)

    tok = [(proj, D_MODEL, n) for n in range(3)] + [(b, D_MODEL, 0) for b in bds]
    return _tok_call(body, name, S, min(S, 512), tok, [], [(D_MODEL, BF16)])[0]


def _merge_bwd(proj, bds, dm, name):
    S = proj.shape[0]

    def body(i, g0, g1, g2, b0, b1, b2, dm_ref, d0, d1, d2, dgp_ref):
        dmv = dm_ref[...]
        for n, (gr, br, dr) in enumerate(((g0, b0, d0), (g1, b1, d1), (g2, b2, d2))):
            s = _sigmoid(gr[...].astype(F32))
            dr[...] = (dmv * s).astype(BF16)
            dgp_ref[:, n * D_MODEL:(n + 1) * D_MODEL] = (dmv * br[...].astype(F32) * s * (1.0 - s)).astype(BF16)

    tok = [(proj, D_MODEL, n) for n in range(3)] + [(b, D_MODEL, 0) for b in bds] + [(dm, D_MODEL, 0)]
    return _tok_call(body, name, S, min(S, 512), tok, [],
                     [(D_MODEL, BF16)] * 3 + [(3 * D_MODEL, BF16, C_GATE // (3 * D_MODEL), IN_R)])


def _sgu_ln(v, lg, lb):
    mu = jnp.mean(v, axis=-1, keepdims=True)
    vc = v - mu
    rstd = lax.rsqrt(jnp.mean(vc * vc, axis=-1, keepdims=True) + EPS)
    vhat = vc * rstd
    return vhat, rstd, vhat * lg + lb


def _sgu_fwd(proj, lg, lb, wc, bst, name):
    S = proj.shape[0]

    def body(i, ua_ref, va_ref, lg_ref, lb_ref, wc_ref, bs_ref, o_ref):
        u = _gelu(ua_ref[...].astype(F32))
        _, _, vn = _sgu_ln(_gelu(va_ref[...].astype(F32)), lg_ref[...], lb_ref[...])
        for g in range(SGU_G):
            sl = slice(g * 128, (g + 1) * 128)
            mixed = _mm(wc_ref[sl, :], vn[:, sl]) + bs_ref[:, g:g + 1]
            o_ref[:, sl] = (u[:, sl] * mixed).astype(BF16)

    return _tok_call(body, name, S, SGU_T, [(proj, MIX, C_UA // MIX), (proj, MIX, C_VA // MIX)], [lg, lb, wc, bst],
                     [(MIX, BF16)])[0]


def _sgu_bwd(proj, lg, lb, wc, bst, dout, dproj, name):
    S = proj.shape[0]

    def body(i, ua_ref, va_ref, do_ref, lg_ref, lb_ref, wc_ref, bs_ref, duv_ref, dlg_ref, dlb_ref, dwc_ref,
             dbs_ref):
        ua, va, do = ua_ref[...].astype(F32), va_ref[...].astype(F32), do_ref[...].astype(F32)
        u = _gelu(ua)
        lgv = lg_ref[...]
        vhat, rstd, vn = _sgu_ln(_gelu(va), lgv, lb_ref[...])
        tril = lax.broadcasted_iota(jnp.int32, (128, 128), 0) >= lax.broadcasted_iota(jnp.int32, (128, 128), 1)
        lane4 = lax.broadcasted_iota(jnp.int32, (128, 4), 1)
        gs = range(SGU_G)
        sls = [slice(g * 128, (g + 1) * 128) for g in gs]
        wgs = [wc_ref[sl, :] for sl in sls]
        mixed = [_mm(wgs[g], vn[:, sls[g]]) for g in gs]
        dmix = [do[:, sl] * u[:, sl] for sl in sls]
        dwg = [_mm_nt(dmix[g], vn[:, sls[g]]) for g in gs]
        dvn = jnp.concatenate([_mm_tn(wgs[g], dmix[g]) for g in gs], axis=1)
        dbs = jnp.zeros((128, 4), F32)
        for g in gs:
            duv_ref[:, sls[g]] = (do[:, sls[g]] * (mixed[g] + bs_ref[:, g:g + 1]) * _dgelu(ua[:, sls[g]])).astype(BF16)
            dbs = dbs + jnp.where(lane4 == g, jnp.sum(dmix[g], axis=-1, keepdims=True), 0.0)
            _acc(dwc_ref.at[sls[g], :], jnp.where(tril, dwg[g], 0.0), i)
        _acc(dbs_ref, dbs, i)
        _acc(dlg_ref, jnp.sum(dvn * vhat, axis=0, keepdims=True), i)
        _acc(dlb_ref, jnp.sum(dvn, axis=0, keepdims=True), i)
        dvh = dvn * lgv
        dv = rstd * (dvh - jnp.mean(dvh, axis=-1, keepdims=True) - vhat * jnp.mean(dvh * vhat, axis=-1, keepdims=True))
        duv_ref[:, MIX:] = (dv * _dgelu(va)).astype(BF16)

    return _tok_call(body, name, S, SGU_T, [(proj, MIX, C_UA // MIX), (proj, MIX, C_VA // MIX), (dout, MIX, 0)],
                     [lg, lb, wc, bst], [(2 * MIX, BF16, C_UA // (2 * MIX), dproj)],
                     [((1, MIX), F32), ((1, MIX), F32), ((SGU_G * 128, 128), F32), ((128, 4), F32)])


def _rope_tables(positions):
    S = positions.shape[0]
    inv_freq = ROPE_THETA ** (-jnp.arange(0, ROPE_DIM, 2, dtype=F32) / ROPE_DIM)
    ang = positions.astype(F32)[:, None] * inv_freq
    c, s = jnp.cos(ang), jnp.sin(ang)
    c64 = jnp.concatenate([c, c, jnp.ones((S, SWA_HD - ROPE_DIM), F32)], axis=1)
    s64 = jnp.concatenate([-s, s, jnp.zeros((S, SWA_HD - ROPE_DIM), F32)], axis=1)
    return jnp.tile(c64, (1, 2)), jnp.tile(s64, (1, 2))


def _rope128(x, c, s):
    lane = lax.broadcasted_iota(jnp.int32, x.shape, 1) % SWA_HD
    swapped = jnp.where(lane < ROPE_DIM // 2, pltpu.roll(x, 128 - ROPE_DIM // 2, 1), pltpu.roll(x, ROPE_DIM // 2, 1))
    return x * c + swapped * s


def _rope_t128(y, c, s):
    ys = y * s
    lane = lax.broadcasted_iota(jnp.int32, y.shape, 1) % SWA_HD
    swapped = jnp.where(lane < ROPE_DIM // 2, pltpu.roll(ys, 128 - ROPE_DIM // 2, 1), pltpu.roll(ys, ROPE_DIM // 2, 1))
    return y * c + jnp.where(lane < ROPE_DIM, swapped, 0.0)


def _rope_fwd(proj, cos, sin, name):
    S = proj.shape[0]
    scale = SWA_HD ** -0.5

    def body(i, q_ref, k_ref, v_ref, c_ref, s_ref, qo_ref, ko_ref, vo_ref):
        c, s = c_ref[...], s_ref[...]
        for j in range(4):
            sl = slice(j * 128, (j + 1) * 128)
            qo_ref[:, sl] = (_rope128(q_ref[:, sl].astype(F32), c, s) * scale).astype(BF16)
        ko_ref[...] = _rope128(k_ref[...].astype(F32), c, s).astype(BF16)
        vo_ref[...] = v_ref[...].astype(BF16)

    return _tok_call(body, name, S, min(S, 512),
                     [(proj, MIX, C_QB // MIX), (proj, 128, C_KB // 128), (proj, 128, C_VB // 128), (cos, 128, 0),
                      (sin, 128, 0)], [], [(MIX, BF16), (128, BF16), (128, BF16)])


def _rope_bwd(dq, dk, dv, cos, sin, dproj, name):
    S = dq.shape[0]
    scale = SWA_HD ** -0.5
    width = C_SM - C_QB

    def body(i, dq_ref, dk_ref, dv_ref, c_ref, s_ref, o_ref):
        c, s = c_ref[...], s_ref[...]
        for j in range(4):
            sl = slice(j * 128, (j + 1) * 128)
            o_ref[:, sl] = _rope_t128(dq_ref[:, sl] * scale, c, s).astype(BF16)
        o_ref[:, C_KB - C_QB:C_VB - C_QB] = _rope_t128(dk_ref[...], c, s).astype(BF16)
        o_ref[:, C_VB - C_QB:] = dv_ref[...].astype(BF16)

    return _tok_call(body, name, S, min(S, 512),
                     [(dq, MIX, 0), (dk, 128, 0), (dv, 128, 0), (cos, 128, 0), (sin, 128, 0)], [],
                     [(width, BF16, C_QB // width, dproj)])[0]


def _swa_band(i, k_ref, v_ref):
    pstart = pl.multiple_of(jnp.maximum(i - 1, 0) * WINDOW, WINDOW)
    cstart = pl.multiple_of(i * WINDOW, WINDOW)
    kb = jnp.concatenate([k_ref[pl.ds(pstart, WINDOW), :], k_ref[pl.ds(cstart, WINDOW), :]], axis=0)
    vb = jnp.concatenate([v_ref[pl.ds(pstart, WINDOW), :], v_ref[pl.ds(cstart, WINDOW), :]], axis=0)
    qi = lax.broadcasted_iota(jnp.int32, (WINDOW, 2 * WINDOW), 0)
    sj = lax.broadcasted_iota(jnp.int32, (WINDOW, 2 * WINDOW), 1)
    mask = (sj > qi) & (sj <= qi + WINDOW) & ((i > 0) | (sj >= WINDOW))
    return kb, vb, mask, pstart, cstart


def _swa_probs(qs, kh, mask, sinks):
    logits = [jnp.where(mask, _dg(qh, kh, 1, 1), -1e30) for qh in qs]
    m = [jnp.maximum(jnp.max(l, axis=-1, keepdims=True), s) for l, s in zip(logits, sinks)]
    p = [jnp.exp(l - mm) for l, mm in zip(logits, m)]
    ps = [jnp.exp(s - mm) for s, mm in zip(sinks, m)]
    inv = [1.0 / (jnp.sum(pp, axis=-1, keepdims=True) + s) for pp, s in zip(p, ps)]
    return [pp * iv for pp, iv in zip(p, inv)], [s * iv for s, iv in zip(ps, inv)]


def _swa_fwd(q, k, v, sinks, name):
    S = q.shape[0]
    G = SWA_H // SWA_KV

    def body(i, q_ref, k_ref, v_ref, s_ref, o_ref):
        kb, vb, mask, _, _ = _swa_band(i, k_ref, v_ref)
        qv = q_ref[...]
        for kv in range(SWA_KV):
            ksl = slice(kv * SWA_HD, (kv + 1) * SWA_HD)
            heads = range(kv * G, (kv + 1) * G)
            pn, _ = _swa_probs([qv[:, h * SWA_HD:(h + 1) * SWA_HD] for h in heads], kb[:, ksl], mask,
                               [s_ref[0, h] for h in heads])
            outs = [_dg(p.astype(BF16), vb[:, ksl], 1, 0) for p in pn]
            for h, o in zip(heads, outs):
                o_ref[:, h * SWA_HD:(h + 1) * SWA_HD] = o.astype(BF16)

    return _tok_call(body, name, S, WINDOW, [(q, MIX, 0)], [k, v], [(MIX, BF16)], smem_in=[sinks])[0]


def _swa_bwd(q, k, v, sinks, dout, name):
    S = q.shape[0]

    def body(i, q_ref, do_ref, k_ref, v_ref, s_ref, dq_ref, dk_ref, dv_ref, ds_ref):
        kb, vb, mask, pstart, cstart = _swa_band(i, k_ref, v_ref)
        qv, dov = q_ref[...], do_ref[...]
        lane = lax.broadcasted_iota(jnp.int32, (1, 128), 1)
        dsink = jnp.zeros((1, 128), F32)
        dkb, dvb = [], []
        G = SWA_H // SWA_KV
        for kv in range(SWA_KV):
            ksl = slice(kv * SWA_HD, (kv + 1) * SWA_HD)
            heads = range(kv * G, (kv + 1) * G)
            qs = [qv[:, h * SWA_HD:(h + 1) * SWA_HD] for h in heads]
            dos = [dov[:, h * SWA_HD:(h + 1) * SWA_HD].astype(BF16) for h in heads]
            pn, psn = _swa_probs(qs, kb[:, ksl], mask, [s_ref[0, h] for h in heads])
            dp = [_dg(d, vb[:, ksl], 1, 1) for d in dos]
            delta = [jnp.sum(a * b, axis=-1, keepdims=True) for a, b in zip(dp, pn)]
            dsc = [(p * (a - d)).astype(BF16) for p, a, d in zip(pn, dp, delta)]
            dqs = [_dg(s, kb[:, ksl], 1, 0) for s in dsc]
            dks = [_dg(s, qh, 0, 0) for s, qh in zip(dsc, qs)]
            dvs = [_dg(p.astype(BF16), d, 0, 0) for p, d in zip(pn, dos)]
            for n_, h in enumerate(heads):
                dq_ref[:, h * SWA_HD:(h + 1) * SWA_HD] = dqs[n_]
                dsink = dsink + jnp.where(lane == h, -jnp.sum(psn[n_] * delta[n_], axis=0, keepdims=True), 0.0)
            dkb.append((dks[0] + dks[1]) + (dks[2] + dks[3]))
            dvb.append((dvs[0] + dvs[1]) + (dvs[2] + dvs[3]))
        dkb = jnp.concatenate(dkb, axis=1)
        dvb = jnp.concatenate(dvb, axis=1)

        @pl.when(i == 0)
        def _():
            dk_ref[...] = jnp.zeros_like(dk_ref)
            dv_ref[...] = jnp.zeros_like(dv_ref)

        dk_ref[pl.ds(pstart, WINDOW), :] += dkb[:WINDOW]
        dv_ref[pl.ds(pstart, WINDOW), :] += dvb[:WINDOW]
        dk_ref[pl.ds(cstart, WINDOW), :] += dkb[WINDOW:]
        dv_ref[pl.ds(cstart, WINDOW), :] += dvb[WINDOW:]
        _acc(ds_ref, dsink, i)

    return _tok_call(body, name, S, WINDOW, [(q, MIX, 0), (dout, MIX, 0)], [k, v], [(MIX, F32)],
                     [((S, 128), F32), ((S, 128), F32), ((1, 128), F32)], smem_in=[sinks])


def _shift_rows(xs, k):
    return xs if k == 0 else pltpu.roll(xs, k, 0)


def _dn_conv(x_ref, p_ref, w_ref, i):
    hr = p_ref.shape[0]
    halo = jnp.where(i > 0, p_ref[...].astype(F32), 0.0)
    xs = jnp.concatenate([halo, x_ref[...].astype(F32)], axis=0)
    sh = [_shift_rows(xs, DN_CONV - 1 - t)[hr:] for t in range(DN_CONV)]
    pre = sh[0] * w_ref[0:1, :]
    for t in range(1, DN_CONV):
        pre = pre + sh[t] * w_ref[t:t + 1, :]
    return pre, sh


def _dn_gates(sm, alog, dtb):
    lane = lax.broadcasted_iota(jnp.int32, sm.shape, 1)
    return jnp.where(lane < DN_H, _sigmoid(sm), -jnp.exp(alog) * _softplus(sm + dtb))


def _dn_pre_fwd(proj, conv_w, alog_l, dtb_l, name):
    S = proj.shape[0]
    scale = DN_HD ** -0.5

    def body(i, x_ref, sm_ref, p_ref, w_ref, al_ref, db_ref, q_ref, k_ref, v_ref, bg_ref):
        pre, _ = _dn_conv(x_ref, p_ref, w_ref, i)
        a = _silu(pre)
        for h in range(DN_H):
            sl = slice(h * DN_HD, (h + 1) * DN_HD)
            qh, kh = a[:, sl], a[:, MIX + h * DN_HD:MIX + (h + 1) * DN_HD]
            q_ref[:, sl] = qh * (lax.rsqrt(jnp.sum(qh * qh, axis=-1, keepdims=True) + EPS) * scale)
            k_ref[:, sl] = kh * lax.rsqrt(jnp.sum(kh * kh, axis=-1, keepdims=True) + EPS)
        v_ref[...] = a[:, 2 * MIX:]
        bg_ref[...] = _dn_gates(sm_ref[...].astype(F32), al_ref[...], db_ref[...])

    TB = min(S, 256)
    return _tok_call(body, name, S, TB, [(proj, 3 * MIX, C_QKV // (3 * MIX)), (proj, 128, C_SM // 128)],
                     [conv_w, alog_l, dtb_l], [(MIX, F32), (MIX, F32), (MIX, F32), (128, F32)],
                     prev_in=[(proj, 3 * MIX, C_QKV // (3 * MIX))])


def _dn_pre_bwd1(proj, conv_w, alog_l, dtb_l, dq, dk, dv, dbg, dproj, name):
    S = proj.shape[0]
    scale = DN_HD ** -0.5

    def body(i, x_ref, sm_ref, dq_ref, dk_ref, dv_ref, dbg_ref, p_ref, w_ref, al_ref, db_ref, dpre_ref, dsm_ref,
             dw_ref, dal_ref, ddb_ref):
        pre, sh = _dn_conv(x_ref, p_ref, w_ref, i)
        a = _silu(pre)
        da_parts = []
        for part, (g_ref, sc) in enumerate(((dq_ref, scale), (dk_ref, 1.0))):
            for h in range(DN_H):
                xh = a[:, part * MIX + h * DN_HD:part * MIX + (h + 1) * DN_HD]
                rs = lax.rsqrt(jnp.sum(xh * xh, axis=-1, keepdims=True) + EPS)
                y = xh * rs
                dy = g_ref[:, h * DN_HD:(h + 1) * DN_HD] * sc
                da_parts.append(rs * (dy - y * jnp.sum(dy * y, axis=-1, keepdims=True)))
        da_parts.append(dv_ref[...])
        dpre = jnp.concatenate(da_parts, axis=1) * _dsilu(pre)
        dpre_ref[...] = dpre
        dw = jnp.concatenate([jnp.sum(dpre * sh[t], axis=0, keepdims=True) for t in range(DN_CONV)], axis=0)
        _acc(dw_ref, dw, i)
        sm, al, db, dbg_v = sm_ref[...].astype(F32), al_ref[...], db_ref[...], dbg_ref[...]
        lane = lax.broadcasted_iota(jnp.int32, sm.shape, 1)
        sg = _sigmoid(sm)
        gneg = -jnp.exp(al)
        is_g = (lane >= DN_H) & (lane < 2 * DN_H)
        d_al = jnp.where(is_g, dbg_v * gneg * _sigmoid(sm + db), 0.0)
        dsm_ref[...] = jnp.where(lane < DN_H, dbg_v * sg * (1.0 - sg), d_al).astype(BF16)
        _acc(ddb_ref, jnp.sum(d_al, axis=0, keepdims=True), i)
        _acc(dal_ref, jnp.sum(jnp.where(is_g, dbg_v * gneg * _softplus(sm + db), 0.0), axis=0, keepdims=True), i)

    TB = min(S, 256)
    return _tok_call(body, name, S, TB,
                     [(proj, 3 * MIX, C_QKV // (3 * MIX)), (proj, 128, C_SM // 128), (dq, MIX, 0), (dk, MIX, 0),
                      (dv, MIX, 0), (dbg, 128, 0)], [conv_w, alog_l, dtb_l],
                     [(3 * MIX, F32), (128, BF16, C_SM // 128, dproj)],
                     [((DN_CONV, 3 * MIX), F32), ((1, 128), F32), ((1, 128), F32)],
                     prev_in=[(proj, 3 * MIX, C_QKV // (3 * MIX))])


def _dn_pre_bwd2(dpre, conv_w, dproj, name):
    S = dpre.shape[0]
    TB = min(S, 256)
    nb = S // TB

    def body(i, d_ref, n_ref, w_ref, o_ref):
        halo = jnp.where(i < nb - 1, n_ref[...], 0.0)
        ds = jnp.concatenate([d_ref[...], halo], axis=0)
        out = ds[:TB] * w_ref[DN_CONV - 1:DN_CONV, :]
        for t in range(DN_CONV - 1):
            k = DN_CONV - 1 - t
            out = out + pltpu.roll(ds, TB + 8 - k, 0)[:TB] * w_ref[t:t + 1, :]
        o_ref[...] = out.astype(BF16)

    return _tok_call(body, name, S, TB, [(dpre, 3 * MIX, 0)], [conv_w],
                     [(3 * MIX, BF16, C_QKV // (3 * MIX), dproj)], next_in=[(dpre, 3 * MIX, 0)])[0]


def _dn_decay_terms(bgs, heads):
    C = DN_C
    ri = lax.broadcasted_iota(jnp.int32, (C, C), 0)
    ci = lax.broadcasted_iota(jnp.int32, (C, C), 1)
    tril, eye = ri >= ci, ri == ci
    beta = [b[:, h:h + 1] for b, h in zip(bgs, heads)]
    gcol = _dg_exact_lhs_many(tril, [jnp.broadcast_to(b[:, DN_H + h:DN_H + h + 1], (C, C))
                                     for b, h in zip(bgs, heads)], 1, 0)
    grow = [jnp.sum(jnp.where(eye, g, 0.0), axis=0, keepdims=True) for g in gcol]
    decay = [jnp.exp(jnp.where(tril, g - r, -1e30)) for g, r in zip(gcol, grow)]
    e_gc = [jnp.exp(g[:, 0:1]) for g in gcol]
    e_kd = [jnp.exp(g[C - 1:C, 0:1] - g[:, 0:1]) for g in gcol]
    cdec = [jnp.exp(g[C - 1:C, 0:1]) for g in gcol]
    return beta, decay, e_gc, e_kd, cdec


def _dn_nb(S):
    return 4 if S % (4 * DN_C) == 0 else 1


def _dn_prep_fwd(q, k, v, bg, name):
    S = q.shape[0]
    C, NB = DN_C, _dn_nb(S)
    TB = NB * C

    def kern(q_ref, k_ref, v_ref, bg_ref, t_ref, uw_ref, at_ref, qd_ref, kd_ref, dec_ref):
        lane = lax.broadcasted_iota(jnp.int32, (C, 128), 1)
        ri = lax.broadcasted_iota(jnp.int32, (C, C), 0)
        ci = lax.broadcasted_iota(jnp.int32, (C, C), 1)
        tril, eye = ri >= ci, ri == ci
        chains = [(cb, h) for cb in range(NB) for h in range(DN_H)]
        rows = lambda cb: slice(cb * C, (cb + 1) * C)
        head = lambda h: slice(h * DN_HD, (h + 1) * DN_HD)
        beta, decay, e_gc, e_kd, cdec = _dn_decay_terms([bg_ref[rows(cb), :] for cb, _ in chains],
                                                        [h for _, h in chains])
        qs = [q_ref[rows(cb), head(h)] for cb, h in chains]
        ks = [k_ref[rows(cb), head(h)] for cb, h in chains]
        kb = [kh * b for kh, b in zip(ks, beta)]
        x = [-jnp.where(ri > ci, _mm_nt(a, kh) * d, 0.0) for a, kh, d in zip(kb, ks, decay)]
        tm = [jnp.where(eye, 1.0, 0.0) + xi for xi in x]
        p = x
        p = _dg3_many(p, p, 1, 0)
        for it in range(5):
            if it == 4:
                tm = [t + tp for t, tp in zip(tm, _dg3_many(tm, p, 1, 0))]
                break
            both = _dg3_many([jnp.concatenate([t, pp], axis=0) for t, pp in zip(tm, p)], p, 1, 0)
            tm = [t + b[:C] for t, b in zip(tm, both)]
            p = [b[C:] for b in both]
        rhs = [jnp.concatenate([v_ref[rows(cb), head(h)] * b, a * e], axis=1)
               for (cb, h), b, a, e in zip(chains, beta, kb, e_gc)]
        sol = _dg3_many(tm, rhs, 1, 0)
        attn = [_mm_nt(qh, kh) * d for qh, kh, d in zip(qs, ks, decay)]
        for n_, (cb, h) in enumerate(chains):
            rs, sl, hc = rows(cb), head(h), slice(h * C, (h + 1) * C)
            t_ref[rs, hc] = tm[n_]
            uw_ref[rs, sl] = sol[n_][:, :DN_HD]
            uw_ref[rs, MIX + h * DN_HD:MIX + (h + 1) * DN_HD] = sol[n_][:, DN_HD:]
            at_ref[rs, hc] = attn[n_]
            qd_ref[rs, sl] = (qs[n_] * e_gc[n_]).astype(BF16)
            kd_ref[rs, sl] = (ks[n_] * e_kd[n_]).astype(BF16)
        for cb in range(NB):
            dec = jnp.zeros((C, 128), F32)
            for h in range(DN_H):
                dec = dec + jnp.where(lane == h, cdec[cb * DN_H + h], 0.0)
            dec_ref[rows(cb), :] = dec

    tok = lambda w: pl.BlockSpec((TB, w), lambda i: (i, 0))
    return pl.pallas_call(
        kern, name=name, grid=(S // TB,), in_specs=[tok(MIX), tok(MIX), tok(MIX), tok(128)],
        out_specs=[tok(DN_H * C), tok(2 * MIX), tok(DN_H * C), tok(MIX), tok(MIX), tok(128)],
        out_shape=[jax.ShapeDtypeStruct((S, DN_H * C), F32), jax.ShapeDtypeStruct((S, 2 * MIX), F32),
                   jax.ShapeDtypeStruct((S, DN_H * C), F32), jax.ShapeDtypeStruct((S, MIX), BF16),
                   jax.ShapeDtypeStruct((S, MIX), BF16), jax.ShapeDtypeStruct((S, 128), F32)],
        compiler_params=_cparams(("parallel",)),
    )(q, k, v, bg)


def _dn_scan_fwd(uw, at, qd, kd, dec, name):
    S = uw.shape[0]
    C, NB = DN_C, _dn_nb(S)
    TB = NB * C
    SR = DN_H * DN_HD

    def kern(uw_ref, at_ref, qd_ref, kd_ref, dec_ref, o_ref, vn_ref, st_ref, state):
        @pl.when(pl.program_id(0) == 0)
        def _():
            state[...] = jnp.zeros_like(state)

        for cb in range(NB):
            rs = slice(cb * C, (cb + 1) * C)
            hs = range(DN_H)
            sls = [slice(h * DN_HD, (h + 1) * DN_HD) for h in hs]
            s_in = [state[sl, :] for sl in sls]
            ws = [_mm(uw_ref[rs, MIX + h * DN_HD:MIX + (h + 1) * DN_HD], s_in[h]) for h in hs]
            os_ = [_mm(qd_ref[rs, sls[h]], s_in[h]) for h in hs]
            vnew = [uw_ref[rs, sls[h]] - ws[h] for h in hs]
            oa = [_mm(at_ref[rs, h * C:(h + 1) * C], vnew[h]) for h in hs]
            kv = [_mm_tn(kd_ref[rs, sls[h]], vnew[h]) for h in hs]
            for h in hs:
                o_ref[rs, sls[h]] = os_[h] + oa[h]
                state[sls[h], :] = s_in[h] * dec_ref[cb * C:cb * C + 1, h:h + 1] + kv[h]
                st_ref[cb * SR + h * DN_HD:cb * SR + (h + 1) * DN_HD, :] = s_in[h]
                vn_ref[rs, sls[h]] = vnew[h]

    tok = lambda w: pl.BlockSpec((TB, w), lambda i: (i, 0))
    return pl.pallas_call(
        kern, name=name, grid=(S // TB,), in_specs=[tok(2 * MIX), tok(DN_H * C), tok(MIX), tok(MIX), tok(128)],
        out_specs=[tok(MIX), tok(MIX), pl.BlockSpec((NB * SR, DN_HD), lambda i: (i, 0))],
        out_shape=[jax.ShapeDtypeStruct((S, MIX), F32), jax.ShapeDtypeStruct((S, MIX), F32),
                   jax.ShapeDtypeStruct((S // C * SR, DN_HD), F32)],
        scratch_shapes=[pltpu.VMEM((SR, DN_HD), F32)],
        compiler_params=_cparams(("arbitrary",)),
    )(uw, at, qd, kd, dec)


def _dn_core_fwd(q, k, v, bg, name):
    tm, uw, at, qd, kd, dec = _dn_prep_fwd(q, k, v, bg, name + "_prep")
    o, vn, st = _dn_scan_fwd(uw, at, qd, kd, dec, name + "_scan")
    return o, dict(tm=tm, uw=uw, at=at, qd=qd, kd=kd, dec=dec, vn=vn, st=st)


def _dn_scan_bwd(sv, do, name):
    S = do.shape[0]
    C, NB = DN_C, _dn_nb(S)
    TB = NB * C
    SR = DN_H * DN_HD
    nb = S // TB

    def kern(do_ref, uw_ref, at_ref, qd_ref, kd_ref, dec_ref, vn_ref, st_ref, dvn_ref, dw_ref, dkd_ref, dc_ref, dstate):
        @pl.when(pl.program_id(0) == 0)
        def _():
            dstate[...] = jnp.zeros_like(dstate)

        lane = lax.broadcasted_iota(jnp.int32, (C, 128), 1)
        for cb in reversed(range(NB)):
            rs = slice(cb * C, (cb + 1) * C)
            dcrow = jnp.zeros((C, 128), F32)
            for h in range(DN_H):
                sl = slice(h * DN_HD, (h + 1) * DN_HD)
                doh, ds_o = do_ref[rs, sl], dstate[sl, :]
                s_in = st_ref[cb * SR + h * DN_HD:cb * SR + (h + 1) * DN_HD, :]
                d_vnew = _mm_tn(at_ref[rs, h * C:(h + 1) * C], doh) + _mm(kd_ref[rs, sl], ds_o)
                dvn_ref[rs, sl] = d_vnew
                dw_ref[rs, sl] = -_mm_nt(d_vnew, s_in)
                dkd_ref[rs, sl] = _mm_nt(vn_ref[rs, sl], ds_o)
                d_c = jnp.sum(jnp.sum(ds_o * s_in, axis=1, keepdims=True), axis=0, keepdims=True)
                dcrow = dcrow + jnp.where(lane == h, d_c, 0.0)
                dstate[sl, :] = (ds_o * dec_ref[cb * C:cb * C + 1, h:h + 1] + _mm_tn(qd_ref[rs, sl], doh)
                                 - _mm_tn(uw_ref[rs, MIX + h * DN_HD:MIX + (h + 1) * DN_HD], d_vnew))
            dc_ref[rs, :] = dcrow

    tok = lambda w: pl.BlockSpec((TB, w), lambda i: (nb - 1 - i, 0))
    return pl.pallas_call(
        kern, name=name, grid=(nb,),
        in_specs=[tok(MIX), tok(2 * MIX), tok(DN_H * C), tok(MIX), tok(MIX), tok(128), tok(MIX),
                  pl.BlockSpec((NB * SR, DN_HD), lambda i: (nb - 1 - i, 0))],
        out_specs=[tok(MIX), tok(MIX), tok(MIX), tok(128)],
        out_shape=[jax.ShapeDtypeStruct((S, MIX), F32)] * 3 + [jax.ShapeDtypeStruct((S, 128), F32)],
        scratch_shapes=[pltpu.VMEM((SR, DN_HD), F32)],
        compiler_params=_cparams(("arbitrary",)),
    )(do, sv["uw"], sv["at"], sv["qd"], sv["kd"], sv["dec"], sv["vn"], sv["st"])


def _dn_chunk_bwd(q, k, v, bg, sv, do, dvn, dw, dkd, dc, name):
    S = q.shape[0]
    C, NB = DN_C, _dn_nb(S)
    TB = NB * C
    SR = DN_H * DN_HD

    def kern(q_ref, k_ref, v_ref, bg_ref, t_ref, uw_ref, vn_ref, st_ref, do_ref, dvn_ref, dw_ref, dkd_ref, dc_ref,
             dq_ref, dk_ref, dv_ref, dbg_ref):
        lane = lax.broadcasted_iota(jnp.int32, (C, 128), 1)
        ri = lax.broadcasted_iota(jnp.int32, (C, C), 0)
        ci = lax.broadcasted_iota(jnp.int32, (C, C), 1)
        tril, eye, last = ri >= ci, ri == ci, ri[:, 0:1] == C - 1
        chains = [(cb, h) for cb in range(NB) for h in range(DN_H)]
        each = lambda f, *ls: [f(*a) for a in zip(*ls)]
        rsum = lambda t: jnp.sum(t, axis=-1, keepdims=True)
        rows = lambda cb: slice(cb * C, (cb + 1) * C)
        head = lambda h: slice(h * DN_HD, (h + 1) * DN_HD)
        tok = lambda ref: [ref[rows(cb), head(h)] for cb, h in chains]
        beta, decay, e_gc, e_kd, cdec = _dn_decay_terms([bg_ref[rows(cb), :] for cb, _ in chains],
                                                        [h for _, h in chains])
        qs, ks, vs, dos, vnew, d_kd = tok(q_ref), tok(k_ref), tok(v_ref), tok(do_ref), tok(vn_ref), tok(dkd_ref)
        s_in = [st_ref[cb * SR + h * DN_HD:cb * SR + (h + 1) * DN_HD, :] for cb, h in chains]
        d_c = [dc_ref[cb * C:cb * C + 1, h:h + 1] for cb, h in chains]
        kb = each(lambda a, b: a * b, ks, beta)
        kk = each(_mm_nt, kb, ks)
        attn = each(lambda a, b, d: _mm_nt(a, b) * d, qs, ks, decay)
        d_qd = each(_mm_nt, dos, s_in)
        d_attn = each(_mm_nt, dos, vnew)
        d_sol = [jnp.concatenate([dvn_ref[rows(cb), head(h)], dw_ref[rows(cb), head(h)]], axis=1) for cb, h in chains]
        sol = [jnp.concatenate([uw_ref[rows(cb), head(h)], uw_ref[rows(cb), MIX + h * DN_HD:MIX + (h + 1) * DN_HD]],
                               axis=1) for cb, h in chains]
        d_rhs = _dg3_many([t_ref[rows(cb), h * C:(h + 1) * C] for cb, h in chains], d_sol, 0, 0)
        d_a = _dg3_many(d_rhs, sol, 1, 1)
        d_kk = each(lambda a, d: jnp.where(ri > ci, -a, 0.0) * d, d_a, decay)
        d_qk = each(lambda a, d: a * d, d_attn, decay)
        dm = each(lambda a, b, c_, d: a * b + c_ * d, d_kk, kk, d_attn, attn)
        d_vb = [t[:, :DN_HD] for t in d_rhs]
        dz = [t[:, DN_HD:] for t in d_rhs]
        d_kb = each(lambda z, e, a, kh: z * e + _mm(a, kh), dz, e_gc, d_kk, ks)
        d_k = each(lambda a, b, c_, q: _mm_tn(a, b) + _mm_tn(c_, q), d_kk, kb, d_qk, qs)
        d_q = each(lambda a, kh, b, e: _mm(a, kh) + b * e, d_qk, ks, d_qd, e_gc)
        t_kd = each(lambda a, kh, e: rsum(a * kh * e), d_kd, ks, e_kd)
        d_gl = each(lambda t, c_, cd: jnp.sum(t, axis=0, keepdims=True) + c_ * cd, t_kd, d_c, cdec)
        d_gc = each(lambda z, a, e, m, b, q, t, gl:
                    rsum(z * a) * e + rsum(m) - rsum(jnp.where(eye, jnp.sum(m, axis=0, keepdims=True), 0.0))
                    + rsum(b * q) * e - t + jnp.where(last, gl, 0.0),
                    dz, kb, e_gc, dm, d_qd, qs, t_kd, d_gl)
        d_g = _dg_exact_lhs_many(ri <= ci, [jnp.broadcast_to(t, (C, 128)) for t in d_gc], 1, 0)
        d_beta = each(lambda a, v_, b, kh: rsum(a * v_) + rsum(b * kh), d_vb, vs, d_kb, ks)
        for n_, (cb, h) in enumerate(chains):
            dq_ref[rows(cb), head(h)] = d_q[n_]
            dk_ref[rows(cb), head(h)] = d_k[n_] + d_kd[n_] * e_kd[n_] + d_kb[n_] * beta[n_]
            dv_ref[rows(cb), head(h)] = d_vb[n_] * beta[n_]
        for cb in range(NB):
            dbg = jnp.zeros((C, 128), F32)
            for h in range(DN_H):
                n_ = cb * DN_H + h
                dbg = dbg + jnp.where(lane == h, d_beta[n_], 0.0) + jnp.where(lane == DN_H + h, d_g[n_], 0.0)
            dbg_ref[rows(cb), :] = dbg

    tok = lambda w: pl.BlockSpec((TB, w), lambda i: (i, 0))
    return pl.pallas_call(
        kern, name=name, grid=(S // TB,),
        in_specs=[tok(MIX), tok(MIX), tok(MIX), tok(128), tok(DN_H * C), tok(2 * MIX), tok(MIX),
                  pl.BlockSpec((NB * SR, DN_HD), lambda i: (i, 0)), tok(MIX), tok(MIX), tok(MIX), tok(MIX), tok(128)],
        out_specs=[tok(MIX), tok(MIX), tok(MIX), tok(128)],
        out_shape=[jax.ShapeDtypeStruct((S, MIX), F32)] * 3 + [jax.ShapeDtypeStruct((S, 128), F32)],
        compiler_params=_cparams(("parallel",)),
    )(q, k, v, bg, sv["tm"], sv["uw"], sv["vn"], sv["st"], do, dvn, dw, dkd, dc)


def _dn_core_bwd(q, k, v, bg, sv, do, name):
    dvn, dw, dkd, dc = _dn_scan_bwd(sv, do, name + "_scan")
    return _dn_chunk_bwd(q, k, v, bg, sv, do, dvn, dw, dkd, dc, name + "_chunk")


def _dn_post_fwd(o, proj, ng, name):
    S = o.shape[0]

    def body(i, o_ref, z_ref, g_ref, out_ref):
        gv = g_ref[...]
        for h in range(DN_H):
            sl = slice(h * DN_HD, (h + 1) * DN_HD)
            oh = o_ref[:, sl]
            r = lax.rsqrt(jnp.mean(oh * oh, axis=-1, keepdims=True) + EPS)
            out_ref[:, sl] = (oh * r * gv * _silu(z_ref[:, sl].astype(F32))).astype(BF16)

    return _tok_call(body, name, S, min(S, 512), [(o, MIX, 0), (proj, MIX, C_ZC // MIX)], [ng], [(MIX, BF16)])[0]


def _dn_post_bwd(o, proj, ng, dout, dproj, name):
    S = o.shape[0]

    def body(i, o_ref, z_ref, do_ref, g_ref, dov_ref, dz_ref, dg_ref):
        gv = g_ref[...]
        dg = jnp.zeros((1, DN_HD), F32)
        for h in range(DN_H):
            sl = slice(h * DN_HD, (h + 1) * DN_HD)
            oh, zh, dh = o_ref[:, sl], z_ref[:, sl].astype(F32), do_ref[:, sl].astype(F32)
            r = lax.rsqrt(jnp.mean(oh * oh, axis=-1, keepdims=True) + EPS)
            dz_ref[:, sl] = (dh * oh * r * gv * _dsilu(zh)).astype(BF16)
            dx, dgh = _rms_bwd_vals(oh, gv, dh * _silu(zh))
            dov_ref[:, sl] = dx
            dg = dg + dgh
        _acc(dg_ref, dg, i)

    return _tok_call(body, name, S, min(S, 512), [(o, MIX, 0), (proj, MIX, C_ZC // MIX), (dout, MIX, 0)], [ng],
                     [(MIX, F32), (MIX, BF16, C_ZC // MIX, dproj)], [((1, DN_HD), F32)])


def _layer_params(w, big, l):
    lane = jnp.arange(128)
    is_g = (lane >= DN_H) & (lane < 2 * DN_H)
    spread = lambda t: jnp.where(is_g, jnp.tile(t, 128 // DN_H), 0.0).reshape(1, 128)
    tril = jnp.tril(jnp.ones((SGU_T, SGU_T), bool))
    return dict(
        win=big["w_in"], rest=big["rest"], conv=w["dn_conv_w"][l], attn_norm=w["attn_norm"][l].reshape(1, -1), ffn_norm=w["ffn_norm"][l].reshape(1, -1),
        lg=w["sgu_ln_g"][l].reshape(1, -1), lb=w["sgu_ln_b"][l].reshape(1, -1),
        wc=jnp.where(tril, w["sgu_w"][l], 0.0).reshape(SGU_G * SGU_T, SGU_T), bst=w["sgu_b"][l].T,
        sinks=w["attn_sinks"][l].reshape(1, -1), alog=spread(w["dn_a_log"][l]), dtb=spread(w["dn_dt_bias"][l]),
        ng=w["dn_norm"][l].reshape(1, -1))


def _layer_fwd(x, p, cos, sin, l):
    n = lambda s: f"l{l}_{s}"
    h = _rms_fwd(x, p["attn_norm"], n("rms1"))
    proj = _matmul(h, p["win"], out_dtype=BF16, name=n("mm_in"))
    out_a = _sgu_fwd(proj, p["lg"], p["lb"], p["wc"], p["bst"], n("sgu_fwd"))
    qr, kr, vr = _rope_fwd(proj, cos, sin, n("rope_fwd"))
    out_b = _swa_fwd(qr, kr, vr, p["sinks"], n("swa_fwd"))
    q, k, v, bg = _dn_pre_fwd(proj, p["conv"], p["alog"], p["dtb"], n("dn_pre_fwd"))
    o, dn = _dn_core_fwd(q, k, v, bg, n("dn_core_fwd"))
    out_c = _dn_post_fwd(o, proj, p["ng"], n("dn_post_fwd"))
    outs = (out_a, out_b, out_c)
    rest = p.pop("rest")(out_c)
    p.update(wb=rest["w_branch"], wout=rest["w_out"], wgu=rest["w_gate_up"], wdown=rest["w_down"])
    bds = [_matmul(outs[j], p["wb"][j], out_dtype=BF16, name=n(f"mm_branch{j}")) for j in range(3)]
    merged = _merge_fwd(proj, bds, n("merge_fwd"))
    x1 = _matmul(merged, p["wout"], add=x, name=n("mm_out"))
    h2 = _rms_fwd(x1, p["ffn_norm"], n("rms2"))
    gu = _matmul(h2, p["wgu"], out_dtype=BF16, name=n("mm_gu"))
    act = _swiglu_fwd(gu, n("swiglu_fwd"))
    x2 = _matmul(act, p["wdown"], add=x1, name=n("mm_down"))
    saved = dict(x=x, h=h, proj=proj, outs=outs, qr=qr, kr=kr, vr=vr, q=q, k=k, v=v, bg=bg, o=o, dn=dn, bds=bds,
                 merged=merged, x1=x1, h2=h2, gu=gu, act=act)
    return x2, saved


def _layer_bwd(dx2, s, p, cos, sin, l, early=None):
    n = lambda t: f"l{l}_{t}"
    proj = s["proj"]
    g = {}
    g["w_down"] = _matmul(s["act"], dx2, ta=True, out_dtype=BF16, name=n("wg_down"))
    dact = _matmul(dx2, p["wdown"], tb=True, out_dtype=BF16, name=n("dg_down"))
    dgu = _swiglu_bwd(s["gu"], dact, n("swiglu_bwd"))
    g["w_gate_up"] = _matmul(s["h2"], dgu, ta=True, out_dtype=BF16, name=n("wg_gu"))
    dh2 = _matmul(dgu, p["wgu"], tb=True, name=n("dg_gu"))
    dx1, g["ffn_norm"] = _rms_bwd_add(s["x1"], p["ffn_norm"], dh2, dx2, n("rms2_bwd"))
    g["w_out"] = _matmul(s["merged"], dx1, ta=True, out_dtype=BF16, name=n("wg_out"))
    dm = _matmul(dx1, p["wout"], tb=True, name=n("dg_out"))
    dbd0, dbd1, dbd2, dproj = _merge_bwd(proj, s["bds"], dm, n("merge_bwd"))
    dbds = (dbd0, dbd1, dbd2)
    g["w_branch"] = jnp.stack([_matmul(s["outs"][j], dbds[j], ta=True, out_dtype=BF16, name=n(f"wg_branch{j}"))
                               for j in range(3)])
    douts = [_matmul(dbds[j], p["wb"][j], tb=True, name=n(f"dg_branch{j}")) for j in range(3)]
    lg = p["lg"]
    if early is not None:
        token = early({k: g.pop(k) for k in ("w_down", "w_gate_up", "w_out", "w_branch")})
        lg = lg if token is None else lg + token[0, 0]
    dproj, g["sgu_ln_g"], g["sgu_ln_b"], dwc, dbs = _sgu_bwd(proj, lg, p["lb"], p["wc"], p["bst"], douts[0], dproj,
                                                             n("sgu_bwd"))
    g["sgu_w"] = dwc.reshape(SGU_G, SGU_T, SGU_T)
    g["sgu_b"] = dbs.T
    dqr, dkr, dvr, dsink = _swa_bwd(s["qr"], s["kr"], s["vr"], p["sinks"], douts[1], n("swa_bwd"))
    g["attn_sinks"] = dsink[0, :SWA_H]
    dproj = _rope_bwd(dqr, dkr, dvr, cos, sin, dproj, n("rope_bwd"))
    do, dproj, dng = _dn_post_bwd(s["o"], proj, p["ng"], douts[2], dproj, n("dn_post_bwd"))
    g["dn_norm"] = dng[0]
    dq, dk, dv, dbg = _dn_core_bwd(s["q"], s["k"], s["v"], s["bg"], s["dn"], do, n("dn_core_bwd"))
    dpre, dproj, g["dn_conv_w"], dal, ddb = _dn_pre_bwd1(proj, p["conv"], p["alog"], p["dtb"], dq, dk, dv, dbg, dproj,
                                                         n("dn_pre_bwd1"))
    g["dn_a_log"] = dal[0, DN_H:2 * DN_H]
    g["dn_dt_bias"] = ddb[0, DN_H:2 * DN_H]
    dproj = _dn_pre_bwd2(dpre, p["conv"], dproj, n("dn_pre_bwd2"))
    g["w_in"] = _matmul(s["h"], dproj, ta=True, out_dtype=BF16, name=n("wg_in"))
    dh = _matmul(dproj, p["win"], tb=True, name=n("dg_in"))
    dx, g["attn_norm"] = _rms_bwd_add(s["x"], p["attn_norm"], dh, dx1, n("rms1_bwd"))
    g["attn_norm"], g["ffn_norm"] = g["attn_norm"][0], g["ffn_norm"][0]
    g["sgu_ln_g"], g["sgu_ln_b"] = g["sgu_ln_g"][0], g["sgu_ln_b"][0]
    return dx, g


def _local_step(x, positions, target, w, big_of_layer, on_grads):
    cos, sin = _rope_tables(positions)
    params, saves, xs = [], [], x
    for l in range(DEPTH):
        params.append(_layer_params(w, big_of_layer(l, xs), l))
        xs, sv = _layer_fwd(xs, params[l], cos, sin, l)
        saves.append(sv)
    dx, loss_row, dgf = _final_loss(xs, w["final_norm"].reshape(1, -1), target)
    grads = [None] * DEPTH
    for l in reversed(range(DEPTH)):
        early = functools.partial(on_grads, l) if l == 0 else None
        dx, grads[l] = _layer_bwd(dx, saves[l], params[l], cos, sin, l, early)
        token = on_grads(l, {k: grads[l].pop(k) for k in BIG if k in grads[l]})
        if token is not None and l > 0:
            params[l - 1] = dict(params[l - 1], ffn_norm=params[l - 1]["ffn_norm"] + token[0, 0])
    stacked = {k: jnp.stack([grads[l][k] for l in range(DEPTH)]) for k in grads[0]}
    stacked["final_norm"] = dgf[0]
    return loss_row[0, 0], dx, stacked


MESH = pl.DeviceIdType.MESH
HBM_SPEC = pl.BlockSpec(memory_space=pltpu.HBM)
VMEM_SPEC = pl.BlockSpec(memory_space=pltpu.VMEM)
N_CHIPS = 4
FLIPS = tuple((fx, fy, fc) for fx in (0, 1) for fy in (0, 1) for fc in (0, 1))[1:]
BIG = ("w_in", "w_branch", "w_out", "w_gate_up", "w_down")
BIG_SPEC = {
    "w_in": dict(rows=1024, cols=1792, axis=1, keep=1730, down=8),
    "w_branch": dict(rows=1536, cols=256, axis=1, keep=256, down=2),
    "w_out": dict(rows=256, cols=1024, axis=0, keep=1024, down=1),
    "w_gate_up": dict(rows=1024, cols=1408, axis=1, keep=1408, down=8),
    "w_down": dict(rows=704, cols=1024, axis=0, keep=1024, down=4),
}
CONV_ROWS, CONV_COLS = DEPTH * DN_CONV, 3 * MIX // N_CHIPS


def _full_shape(k):
    sp = BIG_SPEC[k]
    return (sp["rows"], N_CHIPS * sp["cols"]) if sp["axis"] == 1 else (N_CHIPS * sp["rows"], sp["cols"])


def _me():
    return lax.axis_index("x"), lax.axis_index("y"), lax.axis_index("c")


def _peer(x, y, c, flip):
    fx, fy, fc = flip
    return (1 - x if fx else x, 1 - y if fy else y, 1 - c if fc else c)


class _Copies:
    def __init__(self, send_sems, recv_sems):
        self.send_sems, self.recv_sems, self.k, self.sent, self.landing = send_sems, recv_sems, 0, [], []

    def _copy(self, k, src, dst, to):
        return pltpu.make_async_remote_copy(src_ref=src, dst_ref=dst, send_sem=self.send_sems.at[k],
                                            recv_sem=self.recv_sems.at[k], device_id=to, device_id_type=MESH)

    def send(self, src, dst, to, lands):
        k = self.k
        self.k += 1
        cp = self._copy(k, src, dst, to)
        cp.start()
        self.sent.append(cp)
        self.landing.append(self._copy(k, lands, lands, to))
        return k

    def wait_landed(self, k):
        self.landing[k].wait_recv()

    def finish(self, landed=()):
        for k, cp in enumerate(self.landing):
            if k not in landed:
                cp.wait_recv()
        for cp in self.sent:
            cp.wait_send()


def _place_shard(shard, k, chip, layer, name):
    sp = BIG_SPEC[k]
    rows, cols, keep = sp["rows"], sp["cols"], sp["keep"]
    tr = _pick(rows, (256, 64))
    nb = rows // tr
    if sp["axis"] == 1:
        out_spec = pl.BlockSpec((tr, cols), lambda i, ch: (i, ch[0]))
    else:
        out_spec = pl.BlockSpec((tr, cols), lambda i, ch: (ch[0] * nb + i, 0))

    def kern(ch_ref, x_ref, o_ref):
        v = x_ref[0].astype(BF16)
        if keep == cols:
            o_ref[...] = v
        else:
            o_ref[:, :keep] = v
            o_ref[:, keep:] = jnp.zeros((tr, cols - keep), BF16)

    return pl.pallas_call(
        kern, name=name, out_shape=jax.ShapeDtypeStruct(_full_shape(k), BF16),
        grid_spec=pltpu.PrefetchScalarGridSpec(
            num_scalar_prefetch=1, grid=(nb,),
            in_specs=[pl.BlockSpec((1, tr, keep), lambda i, ch: (layer, i, 0))], out_specs=out_spec),
        compiler_params=_cparams(("parallel",)),
    )(chip, shard)


def _half_block(ref, k, s, half):
    sp = BIG_SPEC[k]
    hr = sp["rows"] // 2
    if sp["axis"] == 1:
        return ref.at[pl.ds(pl.multiple_of(half * hr, 16), hr), pl.ds(pl.multiple_of(s * sp["cols"], 128), sp["cols"])]
    return ref.at[pl.ds(pl.multiple_of(s * sp["rows"] + half * hr, 16), hr), :]


def _other_chips(x, y):
    return [(1 - x, y), (x, 1 - y), (1 - x, 1 - y)]


ALL_BIG = BIG


def _present(d):
    return tuple(k for k in ALL_BIG if k in d)


def _gather_layer(placed, conv):
    BIG = _present(placed)
    n = len(BIG)
    n_sem = 6 * n + 3

    def body(*refs):
        conv_ref = refs[n]
        out = dict(zip(BIG, refs[n + 1:2 * n + 1]))
        conv_out, send_sems, recv_sems, local_sem = refs[2 * n + 1:]
        x, y, c = _me()
        me = 2 * x + y
        chips = _other_chips(x, y)
        net = _Copies(send_sems, recv_sems)

        def conv_block(s):
            return conv_out.at[:, pl.ds(pl.multiple_of(s * CONV_COLS, 128), CONV_COLS)]

        local = pltpu.make_async_copy(conv_ref, conv_block(me), local_sem)
        local.start()
        first = {}
        for k in BIG:
            for j, (px, py) in enumerate(chips):
                first[k, j] = net.send(_half_block(out[k], k, me, c), _half_block(out[k], k, me, c), (px, py, c),
                                       _half_block(out[k], k, 2 * px + py, c))
        for px, py in chips:
            net.send(conv_ref, conv_block(me), (px, py, c), conv_block(2 * px + py))
        for k in BIG:
            for j, (px, py) in enumerate(chips):
                net.wait_landed(first[k, j])
                net.send(_half_block(out[k], k, 2 * px + py, c), _half_block(out[k], k, 2 * px + py, c), (x, y, 1 - c),
                         _half_block(out[k], k, 2 * px + py, 1 - c))
        net.finish(landed=set(first.values()))
        local.wait()

    out_shape = [jax.ShapeDtypeStruct(_full_shape(k), BF16) for k in BIG]
    out_shape.append(jax.ShapeDtypeStruct((CONV_ROWS, N_CHIPS * CONV_COLS), F32))
    outs = pl.pallas_call(
        body, name="gather_layer", out_shape=out_shape, in_specs=[HBM_SPEC] * (n + 1), out_specs=[HBM_SPEC] * (n + 1),
        input_output_aliases={i: i for i in range(n)},
        scratch_shapes=[pltpu.SemaphoreType.DMA((n_sem,)), pltpu.SemaphoreType.DMA((n_sem,)), pltpu.SemaphoreType.DMA],
    )(*[placed[k] for k in BIG], conv)
    return dict(zip(BIG, outs[:n])), outs[n]


SEM_SPEC = pl.BlockSpec(memory_space=pltpu.SEMAPHORE)


def _behind_copies(arrs, send_sems, recv_sems):
    x, y, c = _me()
    copies = []
    for i, k in enumerate(_present(arrs)):
        for j, (px, py) in enumerate(_other_chips(x, y)):
            copies.append(pltpu.make_async_remote_copy(
                src_ref=_half_block(arrs[k], k, 2 * x + y, c), dst_ref=_half_block(arrs[k], k, 2 * x + y, c),
                send_sem=send_sems.at[3 * i + j], recv_sem=recv_sems.at[3 * i + j], device_id=(px, py, c),
                device_id_type=MESH))
    return copies


def _gather_start(placed, after, tag):
    BIG = _present(placed)
    n = len(BIG)
    N_BEHIND = 3 * n

    def body(*refs):
        arrs = dict(zip(BIG, refs[n + 3:2 * n + 3]))
        send_sems, recv_sems = refs[n + 1], refs[n + 2]
        for cp in _behind_copies(arrs, send_sems, recv_sems):
            cp.start()
        refs[2 * n + 3][...] = jnp.zeros((8, 128), F32)

    outs = pl.pallas_call(
        body, name="gather_start" + tag,
        out_shape=(pltpu.SemaphoreType.DMA((N_BEHIND,)), pltpu.SemaphoreType.DMA((N_BEHIND,)),
                   *[pltpu.HBM(_full_shape(k), BF16) for k in BIG], jax.ShapeDtypeStruct((8, 128), F32)),
        in_specs=[HBM_SPEC] * n + [pl.BlockSpec(memory_space=pl.ANY)],
        out_specs=(SEM_SPEC, SEM_SPEC, *[HBM_SPEC] * n, VMEM_SPEC),
        input_output_aliases={i: i + 2 for i in range(n)},
        compiler_params=pltpu.CompilerParams(has_side_effects=pltpu.SideEffectType.DATAFLOW_SIDE_EFFECTING),
    )(*[pltpu.with_memory_space_constraint(placed[k], pltpu.HBM) for k in BIG], after)
    return outs[0], outs[1], dict(zip(BIG, outs[2:n + 2])), outs[n + 2]


def _gather_wait(send_sems, recv_sems, inflight, after, tag):
    BIG = _present(inflight)
    n = len(BIG)

    def body(*refs):
        arrs = dict(zip(BIG, refs[:n]))
        for cp in _behind_copies(arrs, refs[n], refs[n + 1]):
            cp.wait_send()
            cp.wait_recv()

    outs = pl.pallas_call(
        body, name="gather_wait" + tag, out_shape=tuple(pltpu.HBM(_full_shape(k), BF16) for k in BIG),
        in_specs=[HBM_SPEC] * n + [SEM_SPEC, SEM_SPEC, pl.BlockSpec(memory_space=pl.ANY)], out_specs=(HBM_SPEC,) * n,
        input_output_aliases={i: i for i in range(n)},
        compiler_params=pltpu.CompilerParams(has_side_effects=pltpu.SideEffectType.DATAFLOW_SIDE_EFFECTING),
    )(*[inflight[k] for k in BIG], send_sems, recv_sems, after)
    return dict(zip(BIG, outs))


def _gather_finish(arrs, tag):
    BIG = _present(arrs)
    n = len(BIG)
    N_BEHIND = 3 * n

    def body(*refs):
        out = dict(zip(BIG, refs[n:2 * n]))
        send_sems, recv_sems = refs[2 * n:]
        x, y, c = _me()
        net = _Copies(send_sems, recv_sems)
        for k in BIG:
            for px, py in _other_chips(x, y):
                net.send(_half_block(out[k], k, 2 * px + py, c), _half_block(out[k], k, 2 * px + py, c), (x, y, 1 - c),
                         _half_block(out[k], k, 2 * px + py, 1 - c))
        net.finish()

    outs = pl.pallas_call(
        body, name="gather_finish" + tag, out_shape=[jax.ShapeDtypeStruct(_full_shape(k), BF16) for k in BIG],
        in_specs=[HBM_SPEC] * n, out_specs=[HBM_SPEC] * n, input_output_aliases={i: i for i in range(n)},
        scratch_shapes=[pltpu.SemaphoreType.DMA((N_BEHIND,)), pltpu.SemaphoreType.DMA((N_BEHIND,))],
    )(*[arrs[k] for k in BIG])
    return dict(zip(BIG, outs))


def _row_chunks(ref, rows, n):
    step = rows // n
    return [ref.at[pl.ds(i * step, step), :] for i in range(n)]


def _half_pieces(ref, k, half):
    sp = BIG_SPEC[k]
    hr = sp["rows"] // 2
    if sp["axis"] == 1:
        return [ref.at[pl.ds(pl.multiple_of(half * hr, 16), hr), :]]
    return [ref.at[pl.ds(pl.multiple_of(s * sp["rows"] + half * hr, 16), hr), :] for s in range(N_CHIPS)]


def _half_shape(k):
    rows, cols = _full_shape(k)
    return rows // 2, cols


def _stacked_pieces(ref, k):
    sp = BIG_SPEC[k]
    hr = sp["rows"] // 2
    return [ref] if sp["axis"] == 1 else [ref.at[pl.ds(s * hr, hr), :] for s in range(N_CHIPS)]


def _chip_part(ref, k, s):
    sp = BIG_SPEC[k]
    hr = sp["rows"] // 2
    if sp["axis"] == 1:
        return ref.at[:, pl.ds(pl.multiple_of(s * sp["cols"], 128), sp["cols"])]
    return ref.at[pl.ds(pl.multiple_of(s * hr, 16), hr), :]


def _halves_to_sibling(grads, name):
    BIG = _present(grads)
    n = len(BIG)
    chunks = {k: max(BIG_SPEC[k]["down"] // 2, 1) if BIG_SPEC[k]["axis"] == 1 else 1 for k in BIG}
    n_sem = sum(chunks[k] if BIG_SPEC[k]["axis"] == 1 else N_CHIPS for k in BIG)

    def body(*refs):
        g = dict(zip(BIG, refs[:n]))
        out = dict(zip(BIG, refs[n:2 * n]))
        send_sems, recv_sems = refs[2 * n:]
        x, y, c = _me()
        net = _Copies(send_sems, recv_sems)
        for k in BIG:
            hr = BIG_SPEC[k]["rows"] // 2
            for src, dst in zip(_half_pieces(g[k], k, 1 - c), _stacked_pieces(out[k], k)):
                for s, d in zip(_row_chunks(src, hr, chunks[k]), _row_chunks(dst, hr, chunks[k])):
                    net.send(s, d, (x, y, 1 - c), d)
        net.finish()

    outs = pl.pallas_call(
        body, name=name, out_shape=[jax.ShapeDtypeStruct(_half_shape(k), BF16) for k in BIG],
        in_specs=[HBM_SPEC] * n, out_specs=[HBM_SPEC] * n,
        scratch_shapes=[pltpu.SemaphoreType.DMA((n_sem,)), pltpu.SemaphoreType.DMA((n_sem,))],
    )(*[grads[k] for k in BIG])
    return dict(zip(BIG, outs))


def _add_half(g, other, k, core, name):
    sp = BIG_SPEC[k]
    hr, cols = sp["rows"] // 2, _full_shape(k)[1]
    tr = _pick(hr, (256, 352, 128))
    nb = hr // tr
    if sp["axis"] == 1:
        grid = (nb,)
        g_spec = pl.BlockSpec((tr, cols), lambda i, c: (c[0] * nb + i, 0))
        h_spec = pl.BlockSpec((tr, cols), lambda i, c: (i, 0))
    else:
        grid = (N_CHIPS, nb)
        g_spec = pl.BlockSpec((tr, cols), lambda s, i, c: ((2 * s + c[0]) * nb + i, 0))
        h_spec = pl.BlockSpec((tr, cols), lambda s, i, c: (s * nb + i, 0))

    def kern(c_ref, a_ref, b_ref, o_ref):
        o_ref[...] = (a_ref[...].astype(F32) + b_ref[...].astype(F32)).astype(BF16)

    return pl.pallas_call(
        kern, name=name, out_shape=jax.ShapeDtypeStruct(_half_shape(k), BF16),
        grid_spec=pltpu.PrefetchScalarGridSpec(num_scalar_prefetch=1, grid=grid, in_specs=[g_spec, h_spec],
                                               out_specs=h_spec),
        compiler_params=_cparams(("parallel",) * len(grid)),
    )(core, g, other)


def _part_shape(k):
    return N_CHIPS - 1, BIG_SPEC[k]["rows"] // 2, BIG_SPEC[k]["cols"]


def _scatter_chip_sums(sums, name):
    BIG = _present(sums)
    n = len(BIG)
    N_BEHIND = 3 * n

    def body(*refs):
        src = dict(zip(BIG, refs[:n]))
        out = dict(zip(BIG, refs[n:2 * n]))
        send_sems, recv_sems = refs[2 * n:]
        x, y, c = _me()
        net = _Copies(send_sems, recv_sems)
        for k in BIG:
            for j, (px, py) in enumerate(_other_chips(x, y)):
                net.send(_chip_part(src[k], k, 2 * px + py), out[k].at[j], (px, py, c), out[k].at[j])
        net.finish()

    outs = pl.pallas_call(
        body, name=name, out_shape=[jax.ShapeDtypeStruct(_part_shape(k), BF16) for k in BIG],
        in_specs=[HBM_SPEC] * n, out_specs=[HBM_SPEC] * n,
        scratch_shapes=[pltpu.SemaphoreType.DMA((N_BEHIND,)), pltpu.SemaphoreType.DMA((N_BEHIND,))],
    )(*[sums[k] for k in BIG])
    return dict(zip(BIG, outs))


def _scatter_copies(sums, parts, send_sems, recv_sems):
    x, y, c = _me()
    copies = []
    for i, k in enumerate(_present(sums)):
        for j, (px, py) in enumerate(_other_chips(x, y)):
            copies.append(pltpu.make_async_remote_copy(
                src_ref=_chip_part(sums[k], k, 2 * px + py), dst_ref=parts[k].at[j], send_sem=send_sems.at[3 * i + j],
                recv_sem=recv_sems.at[3 * i + j], device_id=(px, py, c), device_id_type=MESH))
    return copies


def _scatter_start(sums, tag):
    BIG = _present(sums)
    n = len(BIG)
    N_BEHIND = 3 * n
    lands = [pltpu.with_memory_space_constraint(lax.empty(_part_shape(k), BF16), pltpu.HBM) for k in BIG]

    def body(*refs):
        outs = refs[2 * n + 2:4 * n + 2]
        for cp in _scatter_copies(dict(zip(BIG, outs[:n])), dict(zip(BIG, outs[n:])), refs[2 * n], refs[2 * n + 1]):
            cp.start()
        refs[4 * n + 2][...] = jnp.zeros((8, 128), F32)

    outs = pl.pallas_call(
        body, name="scatter_start" + tag,
        out_shape=(pltpu.SemaphoreType.DMA((N_BEHIND,)), pltpu.SemaphoreType.DMA((N_BEHIND,)),
                   *[pltpu.HBM(_half_shape(k), BF16) for k in BIG], *[pltpu.HBM(_part_shape(k), BF16) for k in BIG],
                   jax.ShapeDtypeStruct((8, 128), F32)),
        in_specs=[HBM_SPEC] * (2 * n), out_specs=(SEM_SPEC, SEM_SPEC, *[HBM_SPEC] * (2 * n), VMEM_SPEC),
        input_output_aliases={i: i + 2 for i in range(2 * n)},
        compiler_params=pltpu.CompilerParams(has_side_effects=pltpu.SideEffectType.DATAFLOW_SIDE_EFFECTING),
    )(*[pltpu.with_memory_space_constraint(sums[k], pltpu.HBM) for k in BIG], *lands)
    return outs[0], outs[1], outs[2:2 * n + 2], outs[2 * n + 2]


def _scatter_wait(send_sems, recv_sems, inflight, keys, after, tag):
    BIG = keys
    n = len(BIG)

    def body(*refs):
        for cp in _scatter_copies(dict(zip(BIG, refs[:n])), dict(zip(BIG, refs[n:2 * n])), refs[2 * n], refs[2 * n + 1]):
            cp.wait_send()
            cp.wait_recv()

    outs = pl.pallas_call(
        body, name="scatter_wait" + tag,
        out_shape=(*[pltpu.HBM(_half_shape(k), BF16) for k in BIG], *[pltpu.HBM(_part_shape(k), BF16) for k in BIG]),
        in_specs=[HBM_SPEC] * (2 * n) + [SEM_SPEC, SEM_SPEC, pl.BlockSpec(memory_space=pl.ANY)],
        out_specs=(HBM_SPEC,) * (2 * n), input_output_aliases={i: i for i in range(2 * n)},
        compiler_params=pltpu.CompilerParams(has_side_effects=pltpu.SideEffectType.DATAFLOW_SIDE_EFFECTING),
    )(*inflight, send_sems, recv_sems, after)
    return dict(zip(BIG, outs[:n])), dict(zip(BIG, outs[n:]))


def _sum_half(parts, own, k, where, layer, into, name):
    sp = BIG_SPEC[k]
    rows, cols, keep = sp["rows"], sp["cols"], sp["keep"]
    hr = rows // 2
    tr = _pick(hr, (256, 352, 128))
    nb = hr // tr
    if sp["axis"] == 1:
        own_spec = pl.BlockSpec((tr, cols), lambda i, w: (i, w[0]))
    else:
        own_spec = pl.BlockSpec((tr, cols), lambda i, w: (w[0] * nb + i, 0))

    def kern(w_ref, p_ref, own_ref, *rest):
        tot = own_ref[...].astype(F32)
        for j in range(N_CHIPS - 1):
            tot = tot + p_ref[j].astype(F32)
        rest[-1][0] = tot[:, :keep]

    in_specs = [pl.BlockSpec((N_CHIPS - 1, tr, cols), lambda i, w: (0, i, 0)), own_spec]
    args = [where, parts, own]
    if into is not None:
        in_specs.append(pl.BlockSpec(memory_space=pl.ANY))
        args.append(into)
    return pl.pallas_call(
        kern, name=name, out_shape=jax.ShapeDtypeStruct((DEPTH, rows, keep), F32),
        grid_spec=pltpu.PrefetchScalarGridSpec(
            num_scalar_prefetch=1, grid=(nb,), in_specs=in_specs,
            out_specs=pl.BlockSpec((1, tr, keep), lambda i, w: (layer, w[1] * nb + i, 0))),
        input_output_aliases={} if into is None else {3: 0},
        compiler_params=_cparams(("parallel",)),
    )(*args)


def _exchange_halves(red):
    n = len(BIG)

    def body(*refs):
        out = dict(zip(BIG, refs[n:2 * n]))
        send_sems, recv_sems = refs[2 * n:]
        x, y, c = _me()
        net = _Copies(send_sems, recv_sems)
        for k in BIG:
            hr = BIG_SPEC[k]["rows"] // 2
            for l in range(DEPTH):
                mine = out[k].at[l, pl.ds(pl.multiple_of(c * hr, 8), hr), :]
                theirs = out[k].at[l, pl.ds(pl.multiple_of((1 - c) * hr, 8), hr), :]
                net.send(mine, mine, (x, y, 1 - c), theirs)
        net.finish()

    outs = pl.pallas_call(
        body, name="exchange_halves",
        out_shape=[jax.ShapeDtypeStruct((DEPTH, BIG_SPEC[k]["rows"], BIG_SPEC[k]["keep"]), F32) for k in BIG],
        in_specs=[HBM_SPEC] * n, out_specs=[HBM_SPEC] * n, input_output_aliases={i: i for i in range(n)},
        scratch_shapes=[pltpu.SemaphoreType.DMA((DEPTH * n,)), pltpu.SemaphoreType.DMA((DEPTH * n,))],
    )(*[red[k] for k in BIG])
    return dict(zip(BIG, outs))


def _adam_vals(g, w, m, v):
    m2 = ADAM_B1 * m + (1.0 - ADAM_B1) * g
    v2 = ADAM_B2 * v + (1.0 - ADAM_B2) * (g * g)
    m_hat = m2 / (1.0 - ADAM_B1 ** ADAM_STEP)
    v_hat = v2 / (1.0 - ADAM_B2 ** ADAM_STEP)
    return -ADAM_LR * (m_hat / (jnp.sqrt(v_hat) + ADAM_EPS) + ADAM_WD * w), m2, v2


def _allreduce_small_adam(groups):
    ng = len(groups)

    def body(*refs):
        ins = [refs[4 * i:4 * i + 4] for i in range(ng)]
        outs = [refs[4 * ng + 4 * i:4 * ng + 4 * i + 4] for i in range(ng)]
        bufs = refs[8 * ng:9 * ng]
        send_sems, recv_sems = refs[9 * ng:]
        x, y, c = _me()
        me = 4 * x + 2 * y + c
        net = _Copies(send_sems, recv_sems)
        for (g_ref, _, _, _), buf in zip(ins, bufs):
            buf[me] = g_ref[...]
            for f in FLIPS:
                px, py, pc = _peer(x, y, c, f)
                net.send(g_ref, buf.at[me], (px, py, pc), buf.at[4 * px + 2 * py + pc])
        net.finish()
        for (_, w_ref, m_ref, v_ref), (gs_ref, d_ref, nm_ref, nv_ref), buf in zip(ins, outs, bufs):
            tot = buf[0]
            for d in range(1, 8):
                tot = tot + buf[d]
            gs_ref[...] = tot
            d_ref[...], nm_ref[...], nv_ref[...] = _adam_vals(tot, w_ref[...], m_ref[...], v_ref[...])

    shapes = [jax.ShapeDtypeStruct(g[0].shape, F32) for g in groups for _ in range(4)]
    outs = pl.pallas_call(
        body, name="allreduce_small", out_shape=shapes, in_specs=[VMEM_SPEC] * (4 * ng), out_specs=[VMEM_SPEC] * (4 * ng),
        scratch_shapes=[pltpu.VMEM((8,) + g[0].shape, F32) for g in groups]
        + [pltpu.SemaphoreType.DMA((7 * ng,)), pltpu.SemaphoreType.DMA((7 * ng,))],
        compiler_params=pltpu.CompilerParams(vmem_limit_bytes=VMEM_LIMIT),
    )(*[t for g in groups for t in g])
    return [outs[4 * i:4 * i + 4] for i in range(ng)]


def _adam(g, w, m, v, name, lead_block=1):
    shape = w.shape
    lead, rows, cols = math.prod(shape[:-2]), shape[-2], shape[-1]
    tr = _pick(rows, (256, 352, 64, 8, rows))
    spec = pl.BlockSpec((lead_block, tr, cols), lambda l, i: (l, i, 0))

    def kern(g_ref, w_ref, m_ref, v_ref, d_ref, nm_ref, nv_ref):
        d_ref[...], nm_ref[...], nv_ref[...] = _adam_vals(g_ref[...], w_ref[...], m_ref[...], v_ref[...])

    outs = pl.pallas_call(
        kern, name=name, grid=(lead // lead_block, rows // tr), in_specs=[spec] * 4, out_specs=[spec] * 3,
        out_shape=[jax.ShapeDtypeStruct((lead, rows, cols), F32)] * 3, compiler_params=_cparams(("parallel", "parallel")),
    )(*[t.reshape(lead, rows, cols) for t in (g, w, m, v)])
    return [o.reshape(shape) for o in outs]


SMALL = ("attn_norm", "sgu_ln_g", "sgu_ln_b", "sgu_w", "sgu_b", "attn_sinks", "dn_a_log", "dn_dt_bias", "dn_norm",
         "ffn_norm", "final_norm")
SMALL_2D = {"attn_norm": (DEPTH, D_MODEL), "ffn_norm": (DEPTH, D_MODEL), "final_norm": (1, D_MODEL),
            "sgu_ln_g": (DEPTH, MIX), "sgu_ln_b": (DEPTH, MIX), "sgu_w": (DEPTH * SGU_G * SGU_T, SGU_T),
            "sgu_b": (DEPTH * SGU_G, SGU_T), "dn_norm": (DEPTH, DN_HD)}
TINY = ("attn_sinks", "dn_a_log", "dn_dt_bias")


def _pack_tiny(vals, extra=None):
    flat = [vals[k].astype(F32).reshape(-1) for k in TINY] + ([] if extra is None else [extra.astype(F32).reshape(-1)])
    n = sum(f.shape[0] for f in flat)
    return jnp.concatenate(flat + [jnp.zeros((8 * 128 - n,), F32)]).reshape(8, 128)


def _unpack_tiny(tile, shapes):
    flat, out, o = tile.reshape(-1), {}, 0
    for k in TINY:
        n = math.prod(shapes[k])
        out[k] = flat[o:o + n].reshape(shapes[k])
        o += n
    return out, flat[o]


def _in_col_segments():
    shard, padded = IN_COLS // N_CHIPS, BIG_SPEC["w_in"]["cols"]
    segs, mine = [], 0
    for a, n in IN_PIECES:
        o = a
        while o < a + n:
            end = min(a + n, (o // shard + 1) * shard)
            segs.append(((o // shard) * padded + o % shard, mine + o - a, end - o))
            o = end
        mine += n
    return segs


def _move_cols(x, segs, out_cols, name):
    layers, rows, cols = x.shape
    tr = _pick(rows, (256, rows))
    gaps, at = [], 0
    for d, w in sorted((d, w) for _, d, w in segs):
        if d > at:
            gaps.append((at, d - at))
        at = d + w
    if at < out_cols:
        gaps.append((at, out_cols - at))

    def kern(x_ref, o_ref):
        for s, d, w in segs:
            o_ref[0, :, d:d + w] = x_ref[0, :, s:s + w]
        for d, w in gaps:
            o_ref[0, :, d:d + w] = jnp.zeros((tr, w), x.dtype)

    return pl.pallas_call(
        kern, name=name, grid=(layers, rows // tr), in_specs=[pl.BlockSpec((1, tr, cols), lambda l, i: (l, i, 0))],
        out_specs=pl.BlockSpec((1, tr, out_cols), lambda l, i: (l, i, 0)),
        out_shape=jax.ShapeDtypeStruct((layers, rows, out_cols), x.dtype), compiler_params=_cparams(("parallel", "parallel")),
    )(x)


WEIGHTS = ("attn_norm", "w_in", "sgu_ln_g", "sgu_ln_b", "sgu_w", "sgu_b", "attn_sinks", "dn_conv_w", "dn_a_log",
           "dn_dt_bias", "dn_norm", "w_branch", "w_out", "ffn_norm", "w_gate_up", "w_down", "final_norm")


def kernel(x, positions, attn_norm, w_in, sgu_ln_g, sgu_ln_b, sgu_w, sgu_b, attn_sinks, dn_conv_w, dn_a_log, dn_dt_bias, dn_norm, w_branch, w_out, ffn_norm, w_gate_up, w_down, final_norm, loss_target, m_attn_norm, m_w_in, m_sgu_ln_g, m_sgu_ln_b, m_sgu_w, m_sgu_b, m_attn_sinks, m_dn_conv_w, m_dn_a_log, m_dn_dt_bias, m_dn_norm, m_w_branch, m_w_out, m_ffn_norm, m_w_gate_up, m_w_down, m_final_norm, v_attn_norm, v_w_in, v_sgu_ln_g, v_sgu_ln_b, v_sgu_w, v_sgu_b, v_attn_sinks, v_dn_conv_w, v_dn_a_log, v_dn_dt_bias, v_dn_norm, v_w_branch, v_w_out, v_ffn_norm, v_w_gate_up, v_w_down, v_final_norm):
    given = dict(locals())
    W = {k: given[k] for k in WEIGHTS}
    M = {k: given["m_" + k] for k in WEIGHTS}
    V = {k: given["v_" + k] for k in WEIGHTS}
    chip = 2 * lax.axis_index("x") + lax.axis_index("y")
    core = lax.axis_index("c")
    chip1 = chip.astype(jnp.int32).reshape(1)
    where = jnp.stack([chip, core]).astype(jnp.int32)

    placed = [{k: _place_shard(W[k].reshape(DEPTH, BIG_SPEC[k]["rows"], BIG_SPEC[k]["keep"]), k, chip1, l,
                               f"place{l}_{k}") for k in BIG} for l in range(DEPTH)]
    first, conv_full = _gather_layer({"w_in": placed[0]["w_in"]}, dn_conv_w.reshape(CONV_ROWS, CONV_COLS))
    behind = [_gather_start({k: placed[0][k] for k in BIG if k != "w_in"}, conv_full, "0")]
    behind.append(_gather_start(placed[1], behind[0][3], "1"))
    segs = _in_col_segments()

    def arrived(l, after):
        send_sems, recv_sems, inflight, _ = behind[l]
        return _gather_finish(_gather_wait(send_sems, recv_sems, inflight, after, str(l)), str(l))

    def big_of_layer(l, x_l):
        got = {} if l == 0 else arrived(1, x_l)
        w_in = first["w_in"] if l == 0 else got["w_in"]

        def rest(after):
            full = got or arrived(0, after)
            return dict(full, w_branch=full["w_branch"].reshape(3, MIX, D_MODEL))

        return dict(w_in=_move_cols(w_in[None], segs, IN_R, f"w_in_cols{l}")[0], rest=rest)

    w = {k: W[k] for k in SMALL}
    w["attn_norm"] = attn_norm + behind[1][3][0, 0]
    w["dn_conv_w"] = conv_full.reshape(DEPTH, DN_CONV, 3 * MIX)

    core1 = core.astype(jnp.int32).reshape(1)
    back_segs = [(d, s, n) for s, d, n in segs]
    travelling, sums, parts = [], [{}, {}], [{}, {}]

    def on_grads(l, gl):
        gl, tag = dict(gl), f"{l}_{len(gl)}"
        if "w_in" in gl:
            gl["w_in"] = _move_cols(gl["w_in"][None], back_segs, _full_shape("w_in")[1], f"g_in_cols{l}")[0]
        if "w_branch" in gl:
            gl["w_branch"] = gl["w_branch"].reshape(3 * MIX, D_MODEL)
        sibling = _halves_to_sibling(gl, "halves_to_sibling" + tag)
        chip_sums = {k: _add_half(gl[k], sibling[k], k, core1, f"chip_sum{l}_{k}") for k in gl}
        if l == 0 and "w_in" in gl:
            sums[l].update(chip_sums)
            parts[l].update(_scatter_chip_sums(chip_sums, "scatter_chip_sums"))
            return None
        send_sems, recv_sems, inflight, token = _scatter_start(chip_sums, tag)
        travelling.append((l, send_sems, recv_sems, inflight, _present(gl), tag))
        return token

    loss, dx, g = _local_step(x[0], positions[0], loss_target[0], w, big_of_layer, on_grads)
    for l, send_sems, recv_sems, inflight, keys, tag in travelling:
        landed = _scatter_wait(send_sems, recv_sems, inflight, keys, dx, tag)
        sums[l].update(landed[0])
        parts[l].update(landed[1])
    red = {k: _sum_half(parts[1][k], sums[1][k], k, where, 1, None, f"sum1_{k}") for k in BIG}
    red = {k: _sum_half(parts[0][k], sums[0][k], k, where, 0, red[k], f"sum0_{k}") for k in BIG}
    reduced = _exchange_halves(red)
    grads = {k: reduced[k].reshape(W[k].shape) for k in BIG}

    conv_2d = (CONV_ROWS, N_CHIPS * CONV_COLS)
    conv_zero = jnp.zeros(conv_2d, F32)
    groups = [tuple(d[k].reshape(SMALL_2D[k]) for d in (g, W, M, V)) for k in SMALL_2D]
    groups.append((g["dn_conv_w"].reshape(conv_2d), conv_zero, conv_zero, conv_zero))
    groups.append((_pack_tiny(g, loss), _pack_tiny(W), _pack_tiny(M), _pack_tiny(V)))
    summed = _allreduce_small_adam(groups)
    delta, new_m, new_v = {}, {}, {}
    for k, outs in zip(SMALL_2D, summed):
        for d, t in zip((grads, delta, new_m, new_v), outs):
            d[k] = t.reshape(W[k].shape)
    conv_sum = summed[len(SMALL_2D)][0].reshape(g["dn_conv_w"].shape)
    grads["dn_conv_w"] = lax.dynamic_slice_in_dim(conv_sum, chip * dn_conv_w.shape[2], dn_conv_w.shape[2], axis=2)
    tiny_shapes = {k: W[k].shape for k in TINY}
    tiny, loss_total = _unpack_tiny(summed[-1][0], tiny_shapes)
    grads.update(tiny)
    for d, t in zip((delta, new_m, new_v), summed[-1][1:]):
        d.update(_unpack_tiny(t, tiny_shapes)[0])
    for k in ("w_branch", "w_out", "w_gate_up", "w_down", "dn_conv_w"):
        delta[k], new_m[k], new_v[k] = _adam(grads[k], W[k], M[k], V[k], "adam_" + k)
    lead_first = lambda t: jnp.transpose(t, (2, 0, 1))
    outs = _adam(*[lead_first(d["w_in"]) for d in (grads, W, M, V)], "adam_w_in", lead_block=IN_COLS // N_CHIPS // 10)
    delta["w_in"], new_m["w_in"], new_v["w_in"] = (jnp.transpose(o, (1, 2, 0)) for o in outs)

    return (loss_total, dx[None], *[grads[k] for k in WEIGHTS], *[delta[k] for k in WEIGHTS],
            *[new_m[k] for k in WEIGHTS], *[new_v[k] for k in WEIGHTS])
```

```python
import functools
import math

import jax
import jax.numpy as jnp
from jax import lax
from jax.experimental import pallas as pl
from jax.experimental.pallas import tpu as pltpu

F32 = jnp.float32
BF16 = jnp.bfloat16
HI = lax.Precision.HIGHEST

D_MODEL = 1024
DEPTH = 2
MIX = 512
EPS = 1e-6
SGU_G, SGU_T = 4, 128
SWA_H, SWA_KV, SWA_HD, WINDOW = 8, 2, 64, 128
ROPE_THETA, ROPE_DIM = 500000.0, 16
DN_H, DN_HD, DN_CONV, DN_C = 4, 128, 4, 64
D_FF = 2816
IN_COLS = 6920
IN_PIECES = ((3848, 3072), (1792, 1536), (3328, 512), (0, 512), (512, 512), (1024, 512), (1536, 128), (1664, 128),
             (3840, 8))
IN_PAD = 120
IN_R = 7040
C_GATE, C_QKV, C_ZC, C_UA, C_VA, C_QB, C_KB, C_VB, C_SM = 0, 3072, 4608, 5120, 5632, 6144, 6656, 6784, 6912

ADAM_LR, ADAM_B1, ADAM_B2, ADAM_EPS, ADAM_WD, ADAM_STEP = 0.001, 0.9, 0.999, 1e-08, 0.01, 10
VMEM_LIMIT = 56 * 1024 * 1024


def _cparams(sem):
    return pltpu.CompilerParams(dimension_semantics=sem, vmem_limit_bytes=VMEM_LIMIT)


def _dg(a, b, ca, cb, prec=None):
    return lax.dot_general(a, b, (((ca,), (cb,)), ((), ())), precision=prec, preferred_element_type=F32)


def _split(x):
    hi = x.astype(BF16)
    return hi, (x - hi.astype(F32)).astype(BF16)


def _dg3_many(as_, bs, ca, cb):
    sa = [_split(a) for a in as_]
    sb = [_split(b) for b in bs]
    hh = [_dg(a[0], b[0], ca, cb) for a, b in zip(sa, sb)]
    hl = [_dg(a[0], b[1], ca, cb) for a, b in zip(sa, sb)]
    lh = [_dg(a[1], b[0], ca, cb) for a, b in zip(sa, sb)]
    return [x + (y + z) for x, y, z in zip(hh, hl, lh)]


def _dg_exact_lhs_many(a01, bs, ca, cb):
    a = a01.astype(BF16)
    b1 = [b.astype(BF16) for b in bs]
    r1 = [b - t.astype(F32) for b, t in zip(bs, b1)]
    b2 = [r.astype(BF16) for r in r1]
    b3 = [(r - t.astype(F32)).astype(BF16) for r, t in zip(r1, b2)]
    d1 = [_dg(a, t, ca, cb) for t in b1]
    d2 = [_dg(a, t, ca, cb) for t in b2]
    d3 = [_dg(a, t, ca, cb) for t in b3]
    return [x + (y + z) for x, y, z in zip(d1, d2, d3)]


def _mm(a, b):
    return _dg(a.astype(BF16), b.astype(BF16), 1, 0)


def _mm_nt(a, b):
    return _dg(a.astype(BF16), b.astype(BF16), 1, 1)


def _mm_tn(a, b):
    return _dg(a.astype(BF16), b.astype(BF16), 0, 0)


def _sigmoid(x):
    return 0.5 * jnp.tanh(0.5 * x) + 0.5


def _silu(x):
    return x * _sigmoid(x)


def _dsilu(x):
    s = _sigmoid(x)
    return s * (1.0 + x * (1.0 - s))


_GC = math.sqrt(2.0 / math.pi)


def _gelu(x):
    return 0.5 * x * (1.0 + jnp.tanh(_GC * (x + 0.044715 * x * x * x)))


def _dgelu(x):
    t = jnp.tanh(_GC * (x + 0.044715 * x * x * x))
    return 0.5 * (1.0 + t) + 0.5 * x * (1.0 - t * t) * _GC * (1.0 + 3.0 * 0.044715 * x * x)


def _softplus(x):
    return jnp.maximum(x, 0.0) + jnp.log(1.0 + jnp.exp(-jnp.abs(x)))


def _acc(ref, val, i):
    @pl.when(i == 0)
    def _():
        ref[...] = val

    @pl.when(i > 0)
    def _():
        ref[...] += val


def _halo_rows(dtype):
    return 8 * 4 // jnp.dtype(dtype).itemsize


def _tok_call(body, name, S, TB, tok_in, const_in=(), tok_out=(), acc_out=(), prev_in=(), next_in=(), smem_in=()):
    nb = S // TB
    in_specs, args = [], []
    for a, w, cb in tok_in:
        in_specs.append(pl.BlockSpec((TB, w), functools.partial(lambda i, cb: (i, cb), cb=cb)))
        args.append(a)
    for a, w, cb in prev_in:
        hr = _halo_rows(a.dtype)
        in_specs.append(pl.BlockSpec((hr, w), functools.partial(
            lambda i, cb, r: (jnp.maximum(i * r - 1, 0), cb), cb=cb, r=TB // hr)))
        args.append(a)
    for a, w, cb in next_in:
        hr = _halo_rows(a.dtype)
        in_specs.append(pl.BlockSpec((hr, w), functools.partial(
            lambda i, cb, r, last: (jnp.minimum((i + 1) * r, last), cb), cb=cb, r=TB // hr, last=S // hr - 1)))
        args.append(a)
    for a in const_in:
        in_specs.append(pl.BlockSpec(a.shape, lambda i: (0, 0)))
        args.append(a)
    for a in smem_in:
        in_specs.append(pl.BlockSpec(memory_space=pltpu.SMEM))
        args.append(a)
    out_specs, out_shape, aliases, shared = [], [], {}, {}
    for o, (w, dt, *dest) in enumerate(tok_out):
        if not dest:
            out_specs.append(pl.BlockSpec((TB, w), lambda i: (i, 0)))
            out_shape.append(jax.ShapeDtypeStruct((S, w), dt))
            continue
        cb, wide = dest
        out_specs.append(pl.BlockSpec((TB, w), functools.partial(lambda i, cb: (i, cb), cb=cb)))
        out_shape.append(jax.ShapeDtypeStruct((S, wide if isinstance(wide, int) else wide.shape[1]), dt))
        if not isinstance(wide, int):
            if id(wide) not in shared:
                shared[id(wide)] = len(args)
                in_specs.append(pl.BlockSpec(memory_space=pl.ANY))
                args.append(wide)
            aliases[shared[id(wide)]] = o
    for shp, dt in acc_out:
        out_specs.append(pl.BlockSpec(shp, lambda i: (0, 0)))
        out_shape.append(jax.ShapeDtypeStruct(shp, dt))
    n_extra = len(shared)

    def kern(*refs):
        n_in = len(in_specs) - n_extra
        body(pl.program_id(0), *refs[:n_in], *refs[n_in + n_extra:])

    return pl.pallas_call(
        kern, name=name, grid=(nb,), in_specs=in_specs, out_specs=out_specs, out_shape=out_shape,
        input_output_aliases=aliases, compiler_params=_cparams(("arbitrary",)),
    )(*args)


MM_BLOCKS = (1024, 1408, 640, 512, 256, 128)


def _pick(n, cands):
    for c in cands:
        if n % c == 0:
            return c
    return n


MM_VMEM_BUDGET = 44 * 1024 * 1024


def _mm_blocks(M, N, K, a_bytes, b_bytes, o_bytes, add_bytes):
    bn = _pick(N, MM_BLOCKS)
    fits = None
    for bk in [K] + [c for c in (2816, 2048) + MM_BLOCKS if c < K and K % c == 0]:
        for bm in [c for c in (2048,) + MM_BLOCKS if M % c == 0 and c >= min(M, 512)]:
            b_bufs = 1 if (bk == K and bn == N) else 2
            need = 2 * bm * bk * a_bytes + b_bufs * bk * bn * b_bytes + 2 * bm * bn * (o_bytes + add_bytes)
            need += bm * bn * 4 if bk < K else 0
            if need <= MM_VMEM_BUDGET:
                fits = fits or (bm, bn, bk)
                if (M // bm) * (N // bn) * (K // bk) >= 4:
                    return bm, bn, bk
    if fits is None:
        raise ValueError(f"no matmul blocks for {(M, N, K)}")
    return fits


def _matmul(a, b, *, ta=False, tb=False, add=None, out_dtype=F32, name):
    M, K = (a.shape[1], a.shape[0]) if ta else a.shape
    N = b.shape[0] if tb else b.shape[1]
    bm, bn, bk = _mm_blocks(M, N, K, a.dtype.itemsize, b.dtype.itemsize, jnp.dtype(out_dtype).itemsize,
                            0 if add is None else add.dtype.itemsize)
    nk = K // bk
    b_mode = dict(pipeline_mode=pl.Buffered(1)) if (bk == K and bn == N) else {}
    a_spec = pl.BlockSpec((bk, bm), lambda i, j, k: (k, i)) if ta else pl.BlockSpec((bm, bk), lambda i, j, k: (i, k))
    b_spec = (pl.BlockSpec((bn, bk), lambda i, j, k: (j, k), **b_mode) if tb
              else pl.BlockSpec((bk, bn), lambda i, j, k: (k, j), **b_mode))
    o_spec = pl.BlockSpec((bm, bn), lambda i, j, k: (i, j))
    ca, cb = (0 if ta else 1), (1 if tb else 0)

    def kern(*refs):
        a_ref, b_ref = refs[:2]
        add_ref = refs[2] if add is not None else None
        o_ref = refs[3] if add is not None else refs[2]
        p = _dg(a_ref[...].astype(BF16), b_ref[...].astype(BF16), ca, cb)

        def finish(r):
            if add is not None:
                r = r + add_ref[...].astype(F32)
            o_ref[...] = r.astype(out_dtype)

        if nk == 1:
            finish(p)
            return
        acc_ref = refs[-1]
        k = pl.program_id(2)

        @pl.when(k == 0)
        def _():
            acc_ref[...] = p

        @pl.when((k > 0) & (k < nk - 1))
        def _():
            acc_ref[...] += p

        @pl.when(k == nk - 1)
        def _():
            finish(acc_ref[...] + p)

    in_specs = [a_spec, b_spec] + ([o_spec] if add is not None else [])
    args = (a, b) + ((add,) if add is not None else ())
    return pl.pallas_call(
        kern, name=name, grid=(M // bm, N // bn, nk), in_specs=in_specs, out_specs=o_spec,
        out_shape=jax.ShapeDtypeStruct((M, N), out_dtype),
        scratch_shapes=[pltpu.VMEM((bm, bn), F32)] if nk > 1 else [],
        compiler_params=_cparams(("parallel", "parallel", "arbitrary")),
    )(*args)


def _rms_fwd(x, g, name):
    S = x.shape[0]

    def body(i, x_ref, g_ref, h_ref):
        xv = x_ref[...]
        r = lax.rsqrt(jnp.mean(xv * xv, axis=-1, keepdims=True) + EPS)
        h_ref[...] = (xv * r * g_ref[...]).astype(BF16)

    return _tok_call(body, name, S, min(S, 512), [(x, D_MODEL, 0)], [g], [(D_MODEL, BF16)])[0]


def _rms_bwd_vals(xv, g, dh):
    r = lax.rsqrt(jnp.mean(xv * xv, axis=-1, keepdims=True) + EPS)
    u = dh * g
    dx = r * u - xv * (r * r * r) * jnp.mean(u * xv, axis=-1, keepdims=True)
    dg = jnp.sum(dh * xv * r, axis=0, keepdims=True)
    return dx, dg


def _rms_bwd_add(x, g, dh, dres, name):
    S = x.shape[0]

    def body(i, x_ref, dh_ref, dr_ref, g_ref, dx_ref, dg_ref):
        dx, dg = _rms_bwd_vals(x_ref[...], g_ref[...], dh_ref[...].astype(F32))
        dx_ref[...] = dr_ref[...] + dx
        _acc(dg_ref, dg, i)

    return _tok_call(body, name, S, min(S, 512), [(x, D_MODEL, 0), (dh, D_MODEL, 0), (dres, D_MODEL, 0)], [g],
                     [(D_MODEL, F32)], [((1, D_MODEL), F32)])


def _final_loss(x, g, target):
    S = x.shape[0]

    def body(i, x_ref, t_ref, g_ref, dx_ref, loss_ref, dg_ref):
        xv, gv = x_ref[...], g_ref[...]
        r = lax.rsqrt(jnp.mean(xv * xv, axis=-1, keepdims=True) + EPS)
        e = xv * r * gv - t_ref[...]
        part = 0.5 * jnp.sum(jnp.mean(e * e, axis=-1, keepdims=True), axis=0, keepdims=True)
        dx, dg = _rms_bwd_vals(xv, gv, e * (1.0 / D_MODEL))
        dx_ref[...] = dx
        _acc(loss_ref, jnp.broadcast_to(part, (1, 128)), i)
        _acc(dg_ref, dg, i)

    return _tok_call(body, "final_loss", S, min(S, 512), [(x, D_MODEL, 0), (target, D_MODEL, 0)], [g],
                     [(D_MODEL, F32)], [((1, 128), F32), ((1, D_MODEL), F32)])


def _swiglu_fwd(gu, name):
    S = gu.shape[0]

    def body(i, gu_ref, a_ref):
        a_ref[...] = (_silu(gu_ref[:, :D_FF].astype(F32)) * gu_ref[:, D_FF:].astype(F32)).astype(BF16)

    return _tok_call(body, name, S, min(S, 256), [(gu, 2 * D_FF, 0)], [], [(D_FF, BF16)])[0]


def _swiglu_bwd(gu, dact, name):
    S = gu.shape[0]

    def body(i, gu_ref, da_ref, dgu_ref):
        gg, uu, da = gu_ref[:, :D_FF].astype(F32), gu_ref[:, D_FF:].astype(F32), da_ref[...].astype(F32)
        dgu_ref[:, :D_FF] = (da * uu * _dsilu(gg)).astype(BF16)
        dgu_ref[:, D_FF:] = (da * _silu(gg)).astype(BF16)

    return _tok_call(body, name, S, min(S, 256), [(gu, 2 * D_FF, 0), (dact, D_FF, 0)], [], [(2 * D_FF, BF16)])[0]


def _merge_fwd(proj, bds, name):
    S = proj.shape[0]

    def body(i, g0, g1, g2, b0, b1, b2, m_ref):
        m = jnp.zeros(m_ref.shape, F32)
        for gr, br in ((g0, b0), (g1, b1), (g2, b2)):
            m = m + _sigmoid(gr[...].astype(F32)) * br[...].astype(F32)
        m_ref[...] = m.astype(BF16)

    tok = [(proj, D_MODEL, n) for n in range(3)] + [(b, D_MODEL, 0) for b in bds]
    return _tok_call(body, name, S, min(S, 512), tok, [], [(D_MODEL, BF16)])[0]


def _merge_bwd(proj, bds, dm, name):
    S = proj.shape[0]

    def body(i, g0, g1, g2, b0, b1, b2, dm_ref, d0, d1, d2, dgp_ref):
        dmv = dm_ref[...]
        for n, (gr, br, dr) in enumerate(((g0, b0, d0), (g1, b1, d1), (g2, b2, d2))):
            s = _sigmoid(gr[...].astype(F32))
            dr[...] = (dmv * s).astype(BF16)
            dgp_ref[:, n * D_MODEL:(n + 1) * D_MODEL] = (dmv * br[...].astype(F32) * s * (1.0 - s)).astype(BF16)

    tok = [(proj, D_MODEL, n) for n in range(3)] + [(b, D_MODEL, 0) for b in bds] + [(dm, D_MODEL, 0)]
    return _tok_call(body, name, S, min(S, 512), tok, [],
                     [(D_MODEL, BF16)] * 3 + [(3 * D_MODEL, BF16, C_GATE // (3 * D_MODEL), IN_R)])


def _sgu_ln(v, lg, lb):
    mu = jnp.mean(v, axis=-1, keepdims=True)
    vc = v - mu
    rstd = lax.rsqrt(jnp.mean(vc * vc, axis=-1, keepdims=True) + EPS)
    vhat = vc * rstd
    return vhat, rstd, vhat * lg + lb


def _sgu_fwd(proj, lg, lb, wc, bst, name):
    S = proj.shape[0]

    def body(i, ua_ref, va_ref, lg_ref, lb_ref, wc_ref, bs_ref, o_ref):
        u = _gelu(ua_ref[...].astype(F32))
        _, _, vn = _sgu_ln(_gelu(va_ref[...].astype(F32)), lg_ref[...], lb_ref[...])
        for g in range(SGU_G):
            sl = slice(g * 128, (g + 1) * 128)
            mixed = _mm(wc_ref[sl, :], vn[:, sl]) + bs_ref[:, g:g + 1]
            o_ref[:, sl] = (u[:, sl] * mixed).astype(BF16)

    return _tok_call(body, name, S, SGU_T, [(proj, MIX, C_UA // MIX), (proj, MIX, C_VA // MIX)], [lg, lb, wc, bst],
                     [(MIX, BF16)])[0]


def _sgu_bwd(proj, lg, lb, wc, bst, dout, dproj, name):
    S = proj.shape[0]

    def body(i, ua_ref, va_ref, do_ref, lg_ref, lb_ref, wc_ref, bs_ref, duv_ref, dlg_ref, dlb_ref, dwc_ref,
             dbs_ref):
        ua, va, do = ua_ref[...].astype(F32), va_ref[...].astype(F32), do_ref[...].astype(F32)
        u = _gelu(ua)
        lgv = lg_ref[...]
        vhat, rstd, vn = _sgu_ln(_gelu(va), lgv, lb_ref[...])
        tril = lax.broadcasted_iota(jnp.int32, (128, 128), 0) >= lax.broadcasted_iota(jnp.int32, (128, 128), 1)
        lane4 = lax.broadcasted_iota(jnp.int32, (128, 4), 1)
        gs = range(SGU_G)
        sls = [slice(g * 128, (g + 1) * 128) for g in gs]
        wgs = [wc_ref[sl, :] for sl in sls]
        mixed = [_mm(wgs[g], vn[:, sls[g]]) for g in gs]
        dmix = [do[:, sl] * u[:, sl] for sl in sls]
        dwg = [_mm_nt(dmix[g], vn[:, sls[g]]) for g in gs]
        dvn = jnp.concatenate([_mm_tn(wgs[g], dmix[g]) for g in gs], axis=1)
        dbs = jnp.zeros((128, 4), F32)
        for g in gs:
            duv_ref[:, sls[g]] = (do[:, sls[g]] * (mixed[g] + bs_ref[:, g:g + 1]) * _dgelu(ua[:, sls[g]])).astype(BF16)
            dbs = dbs + jnp.where(lane4 == g, jnp.sum(dmix[g], axis=-1, keepdims=True), 0.0)
            _acc(dwc_ref.at[sls[g], :], jnp.where(tril, dwg[g], 0.0), i)
        _acc(dbs_ref, dbs, i)
        _acc(dlg_ref, jnp.sum(dvn * vhat, axis=0, keepdims=True), i)
        _acc(dlb_ref, jnp.sum(dvn, axis=0, keepdims=True), i)
        dvh = dvn * lgv
        dv = rstd * (dvh - jnp.mean(dvh, axis=-1, keepdims=True) - vhat * jnp.mean(dvh * vhat, axis=-1, keepdims=True))
        duv_ref[:, MIX:] = (dv * _dgelu(va)).astype(BF16)

    return _tok_call(body, name, S, SGU_T, [(proj, MIX, C_UA // MIX), (proj, MIX, C_VA // MIX), (dout, MIX, 0)],
                     [lg, lb, wc, bst], [(2 * MIX, BF16, C_UA // (2 * MIX), dproj)],
                     [((1, MIX), F32), ((1, MIX), F32), ((SGU_G * 128, 128), F32), ((128, 4), F32)])


def _rope_tables(positions):
    S = positions.shape[0]
    inv_freq = ROPE_THETA ** (-jnp.arange(0, ROPE_DIM, 2, dtype=F32) / ROPE_DIM)
    ang = positions.astype(F32)[:, None] * inv_freq
    c, s = jnp.cos(ang), jnp.sin(ang)
    c64 = jnp.concatenate([c, c, jnp.ones((S, SWA_HD - ROPE_DIM), F32)], axis=1)
    s64 = jnp.concatenate([-s, s, jnp.zeros((S, SWA_HD - ROPE_DIM), F32)], axis=1)
    return jnp.tile(c64, (1, 2)), jnp.tile(s64, (1, 2))


def _rope128(x, c, s):
    lane = lax.broadcasted_iota(jnp.int32, x.shape, 1) % SWA_HD
    swapped = jnp.where(lane < ROPE_DIM // 2, pltpu.roll(x, 128 - ROPE_DIM // 2, 1), pltpu.roll(x, ROPE_DIM // 2, 1))
    return x * c + swapped * s


def _rope_t128(y, c, s):
    ys = y * s
    lane = lax.broadcasted_iota(jnp.int32, y.shape, 1) % SWA_HD
    swapped = jnp.where(lane < ROPE_DIM // 2, pltpu.roll(ys, 128 - ROPE_DIM // 2, 1), pltpu.roll(ys, ROPE_DIM // 2, 1))
    return y * c + jnp.where(lane < ROPE_DIM, swapped, 0.0)


def _rope_fwd(proj, cos, sin, name):
    S = proj.shape[0]
    scale = SWA_HD ** -0.5

    def body(i, q_ref, k_ref, v_ref, c_ref, s_ref, qo_ref, ko_ref, vo_ref):
        c, s = c_ref[...], s_ref[...]
        for j in range(4):
            sl = slice(j * 128, (j + 1) * 128)
            qo_ref[:, sl] = (_rope128(q_ref[:, sl].astype(F32), c, s) * scale).astype(BF16)
        ko_ref[...] = _rope128(k_ref[...].astype(F32), c, s).astype(BF16)
        vo_ref[...] = v_ref[...].astype(BF16)

    return _tok_call(body, name, S, min(S, 512),
                     [(proj, MIX, C_QB // MIX), (proj, 128, C_KB // 128), (proj, 128, C_VB // 128), (cos, 128, 0),
                      (sin, 128, 0)], [], [(MIX, BF16), (128, BF16), (128, BF16)])


def _rope_bwd(dq, dk, dv, cos, sin, dproj, name):
    S = dq.shape[0]
    scale = SWA_HD ** -0.5
    width = C_SM - C_QB

    def body(i, dq_ref, dk_ref, dv_ref, c_ref, s_ref, o_ref):
        c, s = c_ref[...], s_ref[...]
        for j in range(4):
            sl = slice(j * 128, (j + 1) * 128)
            o_ref[:, sl] = _rope_t128(dq_ref[:, sl] * scale, c, s).astype(BF16)
        o_ref[:, C_KB - C_QB:C_VB - C_QB] = _rope_t128(dk_ref[...], c, s).astype(BF16)
        o_ref[:, C_VB - C_QB:] = dv_ref[...].astype(BF16)

    return _tok_call(body, name, S, min(S, 512),
                     [(dq, MIX, 0), (dk, 128, 0), (dv, 128, 0), (cos, 128, 0), (sin, 128, 0)], [],
                     [(width, BF16, C_QB // width, dproj)])[0]


def _swa_band(i, k_ref, v_ref):
    pstart = pl.multiple_of(jnp.maximum(i - 1, 0) * WINDOW, WINDOW)
    cstart = pl.multiple_of(i * WINDOW, WINDOW)
    kb = jnp.concatenate([k_ref[pl.ds(pstart, WINDOW), :], k_ref[pl.ds(cstart, WINDOW), :]], axis=0)
    vb = jnp.concatenate([v_ref[pl.ds(pstart, WINDOW), :], v_ref[pl.ds(cstart, WINDOW), :]], axis=0)
    qi = lax.broadcasted_iota(jnp.int32, (WINDOW, 2 * WINDOW), 0)
    sj = lax.broadcasted_iota(jnp.int32, (WINDOW, 2 * WINDOW), 1)
    mask = (sj > qi) & (sj <= qi + WINDOW) & ((i > 0) | (sj >= WINDOW))
    return kb, vb, mask, pstart, cstart


def _swa_probs(qs, kh, mask, sinks):
    logits = [jnp.where(mask, _dg(qh, kh, 1, 1), -1e30) for qh in qs]
    m = [jnp.maximum(jnp.max(l, axis=-1, keepdims=True), s) for l, s in zip(logits, sinks)]
    p = [jnp.exp(l - mm) for l, mm in zip(logits, m)]
    ps = [jnp.exp(s - mm) for s, mm in zip(sinks, m)]
    inv = [1.0 / (jnp.sum(pp, axis=-1, keepdims=True) + s) for pp, s in zip(p, ps)]
    return [pp * iv for pp, iv in zip(p, inv)], [s * iv for s, iv in zip(ps, inv)]


def _swa_fwd(q, k, v, sinks, name):
    S = q.shape[0]
    G = SWA_H // SWA_KV

    def body(i, q_ref, k_ref, v_ref, s_ref, o_ref):
        kb, vb, mask, _, _ = _swa_band(i, k_ref, v_ref)
        qv = q_ref[...]
        for kv in range(SWA_KV):
            ksl = slice(kv * SWA_HD, (kv + 1) * SWA_HD)
            heads = range(kv * G, (kv + 1) * G)
            pn, _ = _swa_probs([qv[:, h * SWA_HD:(h + 1) * SWA_HD] for h in heads], kb[:, ksl], mask,
                               [s_ref[0, h] for h in heads])
            outs = [_dg(p.astype(BF16), vb[:, ksl], 1, 0) for p in pn]
            for h, o in zip(heads, outs):
                o_ref[:, h * SWA_HD:(h + 1) * SWA_HD] = o.astype(BF16)

    return _tok_call(body, name, S, WINDOW, [(q, MIX, 0)], [k, v], [(MIX, BF16)], smem_in=[sinks])[0]


def _swa_bwd(q, k, v, sinks, dout, name):
    S = q.shape[0]

    def body(i, q_ref, do_ref, k_ref, v_ref, s_ref, dq_ref, dk_ref, dv_ref, ds_ref):
        kb, vb, mask, pstart, cstart = _swa_band(i, k_ref, v_ref)
        qv, dov = q_ref[...], do_ref[...]
        lane = lax.broadcasted_iota(jnp.int32, (1, 128), 1)
        dsink = jnp.zeros((1, 128), F32)
        dkb, dvb = [], []
        G = SWA_H // SWA_KV
        for kv in range(SWA_KV):
            ksl = slice(kv * SWA_HD, (kv + 1) * SWA_HD)
            heads = range(kv * G, (kv + 1) * G)
            qs = [qv[:, h * SWA_HD:(h + 1) * SWA_HD] for h in heads]
            dos = [dov[:, h * SWA_HD:(h + 1) * SWA_HD].astype(BF16) for h in heads]
            pn, psn = _swa_probs(qs, kb[:, ksl], mask, [s_ref[0, h] for h in heads])
            dp = [_dg(d, vb[:, ksl], 1, 1) for d in dos]
            delta = [jnp.sum(a * b, axis=-1, keepdims=True) for a, b in zip(dp, pn)]
            dsc = [(p * (a - d)).astype(BF16) for p, a, d in zip(pn, dp, delta)]
            dqs = [_dg(s, kb[:, ksl], 1, 0) for s in dsc]
            dks = [_dg(s, qh, 0, 0) for s, qh in zip(dsc, qs)]
            dvs = [_dg(p.astype(BF16), d, 0, 0) for p, d in zip(pn, dos)]
            for n_, h in enumerate(heads):
                dq_ref[:, h * SWA_HD:(h + 1) * SWA_HD] = dqs[n_]
                dsink = dsink + jnp.where(lane == h, -jnp.sum(psn[n_] * delta[n_], axis=0, keepdims=True), 0.0)
            dkb.append((dks[0] + dks[1]) + (dks[2] + dks[3]))
            dvb.append((dvs[0] + dvs[1]) + (dvs[2] + dvs[3]))
        dkb = jnp.concatenate(dkb, axis=1)
        dvb = jnp.concatenate(dvb, axis=1)

        @pl.when(i == 0)
        def _():
            dk_ref[...] = jnp.zeros_like(dk_ref)
            dv_ref[...] = jnp.zeros_like(dv_ref)

        dk_ref[pl.ds(pstart, WINDOW), :] += dkb[:WINDOW]
        dv_ref[pl.ds(pstart, WINDOW), :] += dvb[:WINDOW]
        dk_ref[pl.ds(cstart, WINDOW), :] += dkb[WINDOW:]
        dv_ref[pl.ds(cstart, WINDOW), :] += dvb[WINDOW:]
        _acc(ds_ref, dsink, i)

    return _tok_call(body, name, S, WINDOW, [(q, MIX, 0), (dout, MIX, 0)], [k, v], [(MIX, F32)],
                     [((S, 128), F32), ((S, 128), F32), ((1, 128), F32)], smem_in=[sinks])


def _shift_rows(xs, k):
    return xs if k == 0 else pltpu.roll(xs, k, 0)


def _dn_conv(x_ref, p_ref, w_ref, i):
    hr = p_ref.shape[0]
    halo = jnp.where(i > 0, p_ref[...].astype(F32), 0.0)
    xs = jnp.concatenate([halo, x_ref[...].astype(F32)], axis=0)
    sh = [_shift_rows(xs, DN_CONV - 1 - t)[hr:] for t in range(DN_CONV)]
    pre = sh[0] * w_ref[0:1, :]
    for t in range(1, DN_CONV):
        pre = pre + sh[t] * w_ref[t:t + 1, :]
    return pre, sh


def _dn_gates(sm, alog, dtb):
    lane = lax.broadcasted_iota(jnp.int32, sm.shape, 1)
    return jnp.where(lane < DN_H, _sigmoid(sm), -jnp.exp(alog) * _softplus(sm + dtb))


def _dn_pre_fwd(proj, conv_w, alog_l, dtb_l, name):
    S = proj.shape[0]
    scale = DN_HD ** -0.5

    def body(i, x_ref, sm_ref, p_ref, w_ref, al_ref, db_ref, q_ref, k_ref, v_ref, bg_ref):
        pre, _ = _dn_conv(x_ref, p_ref, w_ref, i)
        a = _silu(pre)
        for h in range(DN_H):
            sl = slice(h * DN_HD, (h + 1) * DN_HD)
            qh, kh = a[:, sl], a[:, MIX + h * DN_HD:MIX + (h + 1) * DN_HD]
            q_ref[:, sl] = qh * (lax.rsqrt(jnp.sum(qh * qh, axis=-1, keepdims=True) + EPS) * scale)
            k_ref[:, sl] = kh * lax.rsqrt(jnp.sum(kh * kh, axis=-1, keepdims=True) + EPS)
        v_ref[...] = a[:, 2 * MIX:]
        bg_ref[...] = _dn_gates(sm_ref[...].astype(F32), al_ref[...], db_ref[...])

    TB = min(S, 256)
    return _tok_call(body, name, S, TB, [(proj, 3 * MIX, C_QKV // (3 * MIX)), (proj, 128, C_SM // 128)],
                     [conv_w, alog_l, dtb_l], [(MIX, F32), (MIX, F32), (MIX, F32), (128, F32)],
                     prev_in=[(proj, 3 * MIX, C_QKV // (3 * MIX))])


def _dn_pre_bwd1(proj, conv_w, alog_l, dtb_l, dq, dk, dv, dbg, dproj, name):
    S = proj.shape[0]
    scale = DN_HD ** -0.5

    def body(i, x_ref, sm_ref, dq_ref, dk_ref, dv_ref, dbg_ref, p_ref, w_ref, al_ref, db_ref, dpre_ref, dsm_ref,
             dw_ref, dal_ref, ddb_ref):
        pre, sh = _dn_conv(x_ref, p_ref, w_ref, i)
        a = _silu(pre)
        da_parts = []
        for part, (g_ref, sc) in enumerate(((dq_ref, scale), (dk_ref, 1.0))):
            for h in range(DN_H):
                xh = a[:, part * MIX + h * DN_HD:part * MIX + (h + 1) * DN_HD]
                rs = lax.rsqrt(jnp.sum(xh * xh, axis=-1, keepdims=True) + EPS)
                y = xh * rs
                dy = g_ref[:, h * DN_HD:(h + 1) * DN_HD] * sc
                da_parts.append(rs * (dy - y * jnp.sum(dy * y, axis=-1, keepdims=True)))
        da_parts.append(dv_ref[...])
        dpre = jnp.concatenate(da_parts, axis=1) * _dsilu(pre)
        dpre_ref[...] = dpre
        dw = jnp.concatenate([jnp.sum(dpre * sh[t], axis=0, keepdims=True) for t in range(DN_CONV)], axis=0)
        _acc(dw_ref, dw, i)
        sm, al, db, dbg_v = sm_ref[...].astype(F32), al_ref[...], db_ref[...], dbg_ref[...]
        lane = lax.broadcasted_iota(jnp.int32, sm.shape, 1)
        sg = _sigmoid(sm)
        gneg = -jnp.exp(al)
        is_g = (lane >= DN_H) & (lane < 2 * DN_H)
        d_al = jnp.where(is_g, dbg_v * gneg * _sigmoid(sm + db), 0.0)
        dsm_ref[...] = jnp.where(lane < DN_H, dbg_v * sg * (1.0 - sg), d_al).astype(BF16)
        _acc(ddb_ref, jnp.sum(d_al, axis=0, keepdims=True), i)
        _acc(dal_ref, jnp.sum(jnp.where(is_g, dbg_v * gneg * _softplus(sm + db), 0.0), axis=0, keepdims=True), i)

    TB = min(S, 256)
    return _tok_call(body, name, S, TB,
                     [(proj, 3 * MIX, C_QKV // (3 * MIX)), (proj, 128, C_SM // 128), (dq, MIX, 0), (dk, MIX, 0),
                      (dv, MIX, 0), (dbg, 128, 0)], [conv_w, alog_l, dtb_l],
                     [(3 * MIX, F32), (128, BF16, C_SM // 128, dproj)],
                     [((DN_CONV, 3 * MIX), F32), ((1, 128), F32), ((1, 128), F32)],
                     prev_in=[(proj, 3 * MIX, C_QKV // (3 * MIX))])


def _dn_pre_bwd2(dpre, conv_w, dproj, name):
    S = dpre.shape[0]
    TB = min(S, 256)
    nb = S // TB

    def body(i, d_ref, n_ref, w_ref, o_ref):
        halo = jnp.where(i < nb - 1, n_ref[...], 0.0)
        ds = jnp.concatenate([d_ref[...], halo], axis=0)
        out = ds[:TB] * w_ref[DN_CONV - 1:DN_CONV, :]
        for t in range(DN_CONV - 1):
            k = DN_CONV - 1 - t
            out = out + pltpu.roll(ds, TB + 8 - k, 0)[:TB] * w_ref[t:t + 1, :]
        o_ref[...] = out.astype(BF16)

    return _tok_call(body, name, S, TB, [(dpre, 3 * MIX, 0)], [conv_w],
                     [(3 * MIX, BF16, C_QKV // (3 * MIX), dproj)], next_in=[(dpre, 3 * MIX, 0)])[0]


def _dn_decay_terms(bgs, heads):
    C = DN_C
    ri = lax.broadcasted_iota(jnp.int32, (C, C), 0)
    ci = lax.broadcasted_iota(jnp.int32, (C, C), 1)
    tril, eye = ri >= ci, ri == ci
    beta = [b[:, h:h + 1] for b, h in zip(bgs, heads)]
    gcol = _dg_exact_lhs_many(tril, [jnp.broadcast_to(b[:, DN_H + h:DN_H + h + 1], (C, C))
                                     for b, h in zip(bgs, heads)], 1, 0)
    grow = [jnp.sum(jnp.where(eye, g, 0.0), axis=0, keepdims=True) for g in gcol]
    decay = [jnp.exp(jnp.where(tril, g - r, -1e30)) for g, r in zip(gcol, grow)]
    e_gc = [jnp.exp(g[:, 0:1]) for g in gcol]
    e_kd = [jnp.exp(g[C - 1:C, 0:1] - g[:, 0:1]) for g in gcol]
    cdec = [jnp.exp(g[C - 1:C, 0:1]) for g in gcol]
    return beta, decay, e_gc, e_kd, cdec


def _dn_nb(S):
    return 4 if S % (4 * DN_C) == 0 else 1


def _dn_prep_fwd(q, k, v, bg, name):
    S = q.shape[0]
    C, NB = DN_C, _dn_nb(S)
    TB = NB * C

    def kern(q_ref, k_ref, v_ref, bg_ref, t_ref, uw_ref, at_ref, qd_ref, kd_ref, dec_ref):
        lane = lax.broadcasted_iota(jnp.int32, (C, 128), 1)
        ri = lax.broadcasted_iota(jnp.int32, (C, C), 0)
        ci = lax.broadcasted_iota(jnp.int32, (C, C), 1)
        tril, eye = ri >= ci, ri == ci
        chains = [(cb, h) for cb in range(NB) for h in range(DN_H)]
        rows = lambda cb: slice(cb * C, (cb + 1) * C)
        head = lambda h: slice(h * DN_HD, (h + 1) * DN_HD)
        beta, decay, e_gc, e_kd, cdec = _dn_decay_terms([bg_ref[rows(cb), :] for cb, _ in chains],
                                                        [h for _, h in chains])
        qs = [q_ref[rows(cb), head(h)] for cb, h in chains]
        ks = [k_ref[rows(cb), head(h)] for cb, h in chains]
        kb = [kh * b for kh, b in zip(ks, beta)]
        x = [-jnp.where(ri > ci, _mm_nt(a, kh) * d, 0.0) for a, kh, d in zip(kb, ks, decay)]
        tm = [jnp.where(eye, 1.0, 0.0) + xi for xi in x]
        p = x
        p = _dg3_many(p, p, 1, 0)
        for it in range(5):
            if it == 4:
                tm = [t + tp for t, tp in zip(tm, _dg3_many(tm, p, 1, 0))]
                break
            both = _dg3_many([jnp.concatenate([t, pp], axis=0) for t, pp in zip(tm, p)], p, 1, 0)
            tm = [t + b[:C] for t, b in zip(tm, both)]
            p = [b[C:] for b in both]
        rhs = [jnp.concatenate([v_ref[rows(cb), head(h)] * b, a * e], axis=1)
               for (cb, h), b, a, e in zip(chains, beta, kb, e_gc)]
        sol = _dg3_many(tm, rhs, 1, 0)
        attn = [_mm_nt(qh, kh) * d for qh, kh, d in zip(qs, ks, decay)]
        for n_, (cb, h) in enumerate(chains):
            rs, sl, hc = rows(cb), head(h), slice(h * C, (h + 1) * C)
            t_ref[rs, hc] = tm[n_]
            uw_ref[rs, sl] = sol[n_][:, :DN_HD]
            uw_ref[rs, MIX + h * DN_HD:MIX + (h + 1) * DN_HD] = sol[n_][:, DN_HD:]
            at_ref[rs, hc] = attn[n_]
            qd_ref[rs, sl] = (qs[n_] * e_gc[n_]).astype(BF16)
            kd_ref[rs, sl] = (ks[n_] * e_kd[n_]).astype(BF16)
        for cb in range(NB):
            dec = jnp.zeros((C, 128), F32)
            for h in range(DN_H):
                dec = dec + jnp.where(lane == h, cdec[cb * DN_H + h], 0.0)
            dec_ref[rows(cb), :] = dec

    tok = lambda w: pl.BlockSpec((TB, w), lambda i: (i, 0))
    return pl.pallas_call(
        kern, name=name, grid=(S // TB,), in_specs=[tok(MIX), tok(MIX), tok(MIX), tok(128)],
        out_specs=[tok(DN_H * C), tok(2 * MIX), tok(DN_H * C), tok(MIX), tok(MIX), tok(128)],
        out_shape=[jax.ShapeDtypeStruct((S, DN_H * C), F32), jax.ShapeDtypeStruct((S, 2 * MIX), F32),
                   jax.ShapeDtypeStruct((S, DN_H * C), F32), jax.ShapeDtypeStruct((S, MIX), BF16),
                   jax.ShapeDtypeStruct((S, MIX), BF16), jax.ShapeDtypeStruct((S, 128), F32)],
        compiler_params=_cparams(("parallel",)),
    )(q, k, v, bg)


def _dn_scan_fwd(uw, at, qd, kd, dec, name):
    S = uw.shape[0]
    C, NB = DN_C, _dn_nb(S)
    TB = NB * C
    SR = DN_H * DN_HD

    def kern(uw_ref, at_ref, qd_ref, kd_ref, dec_ref, o_ref, vn_ref, st_ref, state):
        @pl.when(pl.program_id(0) == 0)
        def _():
            state[...] = jnp.zeros_like(state)

        for cb in range(NB):
            rs = slice(cb * C, (cb + 1) * C)
            hs = range(DN_H)
            sls = [slice(h * DN_HD, (h + 1) * DN_HD) for h in hs]
            s_in = [state[sl, :] for sl in sls]
            ws = [_mm(uw_ref[rs, MIX + h * DN_HD:MIX + (h + 1) * DN_HD], s_in[h]) for h in hs]
            os_ = [_mm(qd_ref[rs, sls[h]], s_in[h]) for h in hs]
            vnew = [uw_ref[rs, sls[h]] - ws[h] for h in hs]
            oa = [_mm(at_ref[rs, h * C:(h + 1) * C], vnew[h]) for h in hs]
            kv = [_mm_tn(kd_ref[rs, sls[h]], vnew[h]) for h in hs]
            for h in hs:
                o_ref[rs, sls[h]] = os_[h] + oa[h]
                state[sls[h], :] = s_in[h] * dec_ref[cb * C:cb * C + 1, h:h + 1] + kv[h]
                st_ref[cb * SR + h * DN_HD:cb * SR + (h + 1) * DN_HD, :] = s_in[h]
                vn_ref[rs, sls[h]] = vnew[h]

    tok = lambda w: pl.BlockSpec((TB, w), lambda i: (i, 0))
    return pl.pallas_call(
        kern, name=name, grid=(S // TB,), in_specs=[tok(2 * MIX), tok(DN_H * C), tok(MIX), tok(MIX), tok(128)],
        out_specs=[tok(MIX), tok(MIX), pl.BlockSpec((NB * SR, DN_HD), lambda i: (i, 0))],
        out_shape=[jax.ShapeDtypeStruct((S, MIX), F32), jax.ShapeDtypeStruct((S, MIX), F32),
                   jax.ShapeDtypeStruct((S // C * SR, DN_HD), F32)],
        scratch_shapes=[pltpu.VMEM((SR, DN_HD), F32)],
        compiler_params=_cparams(("arbitrary",)),
    )(uw, at, qd, kd, dec)


def _dn_core_fwd(q, k, v, bg, name):
    tm, uw, at, qd, kd, dec = _dn_prep_fwd(q, k, v, bg, name + "_prep")
    o, vn, st = _dn_scan_fwd(uw, at, qd, kd, dec, name + "_scan")
    return o, dict(tm=tm, uw=uw, at=at, qd=qd, kd=kd, dec=dec, vn=vn, st=st)


def _dn_scan_bwd(sv, do, name):
    S = do.shape[0]
    C, NB = DN_C, _dn_nb(S)
    TB = NB * C
    SR = DN_H * DN_HD
    nb = S // TB

    def kern(do_ref, uw_ref, at_ref, qd_ref, kd_ref, dec_ref, vn_ref, st_ref, dvn_ref, dw_ref, dkd_ref, dc_ref, dstate):
        @pl.when(pl.program_id(0) == 0)
        def _():
            dstate[...] = jnp.zeros_like(dstate)

        lane = lax.broadcasted_iota(jnp.int32, (C, 128), 1)
        for cb in reversed(range(NB)):
            rs = slice(cb * C, (cb + 1) * C)
            dcrow = jnp.zeros((C, 128), F32)
            for h in range(DN_H):
                sl = slice(h * DN_HD, (h + 1) * DN_HD)
                doh, ds_o = do_ref[rs, sl], dstate[sl, :]
                s_in = st_ref[cb * SR + h * DN_HD:cb * SR + (h + 1) * DN_HD, :]
                d_vnew = _mm_tn(at_ref[rs, h * C:(h + 1) * C], doh) + _mm(kd_ref[rs, sl], ds_o)
                dvn_ref[rs, sl] = d_vnew
                dw_ref[rs, sl] = -_mm_nt(d_vnew, s_in)
                dkd_ref[rs, sl] = _mm_nt(vn_ref[rs, sl], ds_o)
                d_c = jnp.sum(jnp.sum(ds_o * s_in, axis=1, keepdims=True), axis=0, keepdims=True)
                dcrow = dcrow + jnp.where(lane == h, d_c, 0.0)
                dstate[sl, :] = (ds_o * dec_ref[cb * C:cb * C + 1, h:h + 1] + _mm_tn(qd_ref[rs, sl], doh)
                                 - _mm_tn(uw_ref[rs, MIX + h * DN_HD:MIX + (h + 1) * DN_HD], d_vnew))
            dc_ref[rs, :] = dcrow

    tok = lambda w: pl.BlockSpec((TB, w), lambda i: (nb - 1 - i, 0))
    return pl.pallas_call(
        kern, name=name, grid=(nb,),
        in_specs=[tok(MIX), tok(2 * MIX), tok(DN_H * C), tok(MIX), tok(MIX), tok(128), tok(MIX),
                  pl.BlockSpec((NB * SR, DN_HD), lambda i: (nb - 1 - i, 0))],
        out_specs=[tok(MIX), tok(MIX), tok(MIX), tok(128)],
        out_shape=[jax.ShapeDtypeStruct((S, MIX), F32)] * 3 + [jax.ShapeDtypeStruct((S, 128), F32)],
        scratch_shapes=[pltpu.VMEM((SR, DN_HD), F32)],
        compiler_params=_cparams(("arbitrary",)),
    )(do, sv["uw"], sv["at"], sv["qd"], sv["kd"], sv["dec"], sv["vn"], sv["st"])


def _dn_chunk_bwd(q, k, v, bg, sv, do, dvn, dw, dkd, dc, name):
    S = q.shape[0]
    C, NB = DN_C, _dn_nb(S)
    TB = NB * C
    SR = DN_H * DN_HD

    def kern(q_ref, k_ref, v_ref, bg_ref, t_ref, uw_ref, vn_ref, st_ref, do_ref, dvn_ref, dw_ref, dkd_ref, dc_ref,
             dq_ref, dk_ref, dv_ref, dbg_ref):
        lane = lax.broadcasted_iota(jnp.int32, (C, 128), 1)
        ri = lax.broadcasted_iota(jnp.int32, (C, C), 0)
        ci = lax.broadcasted_iota(jnp.int32, (C, C), 1)
        tril, eye, last = ri >= ci, ri == ci, ri[:, 0:1] == C - 1
        chains = [(cb, h) for cb in range(NB) for h in range(DN_H)]
        each = lambda f, *ls: [f(*a) for a in zip(*ls)]
        rsum = lambda t: jnp.sum(t, axis=-1, keepdims=True)
        rows = lambda cb: slice(cb * C, (cb + 1) * C)
        head = lambda h: slice(h * DN_HD, (h + 1) * DN_HD)
        tok = lambda ref: [ref[rows(cb), head(h)] for cb, h in chains]
        beta, decay, e_gc, e_kd, cdec = _dn_decay_terms([bg_ref[rows(cb), :] for cb, _ in chains],
                                                        [h for _, h in chains])
        qs, ks, vs, dos, vnew, d_kd = tok(q_ref), tok(k_ref), tok(v_ref), tok(do_ref), tok(vn_ref), tok(dkd_ref)
        s_in = [st_ref[cb * SR + h * DN_HD:cb * SR + (h + 1) * DN_HD, :] for cb, h in chains]
        d_c = [dc_ref[cb * C:cb * C + 1, h:h + 1] for cb, h in chains]
        kb = each(lambda a, b: a * b, ks, beta)
        kk = each(_mm_nt, kb, ks)
        attn = each(lambda a, b, d: _mm_nt(a, b) * d, qs, ks, decay)
        d_qd = each(_mm_nt, dos, s_in)
        d_attn = each(_mm_nt, dos, vnew)
        d_sol = [jnp.concatenate([dvn_ref[rows(cb), head(h)], dw_ref[rows(cb), head(h)]], axis=1) for cb, h in chains]
        sol = [jnp.concatenate([uw_ref[rows(cb), head(h)], uw_ref[rows(cb), MIX + h * DN_HD:MIX + (h + 1) * DN_HD]],
                               axis=1) for cb, h in chains]
        d_rhs = _dg3_many([t_ref[rows(cb), h * C:(h + 1) * C] for cb, h in chains], d_sol, 0, 0)
        d_a = _dg3_many(d_rhs, sol, 1, 1)
        d_kk = each(lambda a, d: jnp.where(ri > ci, -a, 0.0) * d, d_a, decay)
        d_qk = each(lambda a, d: a * d, d_attn, decay)
        dm = each(lambda a, b, c_, d: a * b + c_ * d, d_kk, kk, d_attn, attn)
        d_vb = [t[:, :DN_HD] for t in d_rhs]
        dz = [t[:, DN_HD:] for t in d_rhs]
        d_kb = each(lambda z, e, a, kh: z * e + _mm(a, kh), dz, e_gc, d_kk, ks)
        d_k = each(lambda a, b, c_, q: _mm_tn(a, b) + _mm_tn(c_, q), d_kk, kb, d_qk, qs)
        d_q = each(lambda a, kh, b, e: _mm(a, kh) + b * e, d_qk, ks, d_qd, e_gc)
        t_kd = each(lambda a, kh, e: rsum(a * kh * e), d_kd, ks, e_kd)
        d_gl = each(lambda t, c_, cd: jnp.sum(t, axis=0, keepdims=True) + c_ * cd, t_kd, d_c, cdec)
        d_gc = each(lambda z, a, e, m, b, q, t, gl:
                    rsum(z * a) * e + rsum(m) - rsum(jnp.where(eye, jnp.sum(m, axis=0, keepdims=True), 0.0))
                    + rsum(b * q) * e - t + jnp.where(last, gl, 0.0),
                    dz, kb, e_gc, dm, d_qd, qs, t_kd, d_gl)
        d_g = _dg_exact_lhs_many(ri <= ci, [jnp.broadcast_to(t, (C, 128)) for t in d_gc], 1, 0)
        d_beta = each(lambda a, v_, b, kh: rsum(a * v_) + rsum(b * kh), d_vb, vs, d_kb, ks)
        for n_, (cb, h) in enumerate(chains):
            dq_ref[rows(cb), head(h)] = d_q[n_]
            dk_ref[rows(cb), head(h)] = d_k[n_] + d_kd[n_] * e_kd[n_] + d_kb[n_] * beta[n_]
            dv_ref[rows(cb), head(h)] = d_vb[n_] * beta[n_]
        for cb in range(NB):
            dbg = jnp.zeros((C, 128), F32)
            for h in range(DN_H):
                n_ = cb * DN_H + h
                dbg = dbg + jnp.where(lane == h, d_beta[n_], 0.0) + jnp.where(lane == DN_H + h, d_g[n_], 0.0)
            dbg_ref[rows(cb), :] = dbg

    tok = lambda w: pl.BlockSpec((TB, w), lambda i: (i, 0))
    return pl.pallas_call(
        kern, name=name, grid=(S // TB,),
        in_specs=[tok(MIX), tok(MIX), tok(MIX), tok(128), tok(DN_H * C), tok(2 * MIX), tok(MIX),
                  pl.BlockSpec((NB * SR, DN_HD), lambda i: (i, 0)), tok(MIX), tok(MIX), tok(MIX), tok(MIX), tok(128)],
        out_specs=[tok(MIX), tok(MIX), tok(MIX), tok(128)],
        out_shape=[jax.ShapeDtypeStruct((S, MIX), F32)] * 3 + [jax.ShapeDtypeStruct((S, 128), F32)],
        compiler_params=_cparams(("parallel",)),
    )(q, k, v, bg, sv["tm"], sv["uw"], sv["vn"], sv["st"], do, dvn, dw, dkd, dc)


def _dn_core_bwd(q, k, v, bg, sv, do, name):
    dvn, dw, dkd, dc = _dn_scan_bwd(sv, do, name + "_scan")
    return _dn_chunk_bwd(q, k, v, bg, sv, do, dvn, dw, dkd, dc, name + "_chunk")


def _dn_post_fwd(o, proj, ng, name):
    S = o.shape[0]

    def body(i, o_ref, z_ref, g_ref, out_ref):
        gv = g_ref[...]
        for h in range(DN_H):
            sl = slice(h * DN_HD, (h + 1) * DN_HD)
            oh = o_ref[:, sl]
            r = lax.rsqrt(jnp.mean(oh * oh, axis=-1, keepdims=True) + EPS)
            out_ref[:, sl] = (oh * r * gv * _silu(z_ref[:, sl].astype(F32))).astype(BF16)

    return _tok_call(body, name, S, min(S, 512), [(o, MIX, 0), (proj, MIX, C_ZC // MIX)], [ng], [(MIX, BF16)])[0]


def _dn_post_bwd(o, proj, ng, dout, dproj, name):
    S = o.shape[0]

    def body(i, o_ref, z_ref, do_ref, g_ref, dov_ref, dz_ref, dg_ref):
        gv = g_ref[...]
        dg = jnp.zeros((1, DN_HD), F32)
        for h in range(DN_H):
            sl = slice(h * DN_HD, (h + 1) * DN_HD)
            oh, zh, dh = o_ref[:, sl], z_ref[:, sl].astype(F32), do_ref[:, sl].astype(F32)
            r = lax.rsqrt(jnp.mean(oh * oh, axis=-1, keepdims=True) + EPS)
            dz_ref[:, sl] = (dh * oh * r * gv * _dsilu(zh)).astype(BF16)
            dx, dgh = _rms_bwd_vals(oh, gv, dh * _silu(zh))
            dov_ref[:, sl] = dx
            dg = dg + dgh
        _acc(dg_ref, dg, i)

    return _tok_call(body, name, S, min(S, 512), [(o, MIX, 0), (proj, MIX, C_ZC // MIX), (dout, MIX, 0)], [ng],
                     [(MIX, F32), (MIX, BF16, C_ZC // MIX, dproj)], [((1, DN_HD), F32)])


def _layer_params(w, big, l):
    lane = jnp.arange(128)
    is_g = (lane >= DN_H) & (lane < 2 * DN_H)
    spread = lambda t: jnp.where(is_g, jnp.tile(t, 128 // DN_H), 0.0).reshape(1, 128)
    tril = jnp.tril(jnp.ones((SGU_T, SGU_T), bool))
    return dict(
        win=big["w_in"], rest=big["rest"], conv=w["dn_conv_w"][l], attn_norm=w["attn_norm"][l].reshape(1, -1), ffn_norm=w["ffn_norm"][l].reshape(1, -1),
        lg=w["sgu_ln_g"][l].reshape(1, -1), lb=w["sgu_ln_b"][l].reshape(1, -1),
        wc=jnp.where(tril, w["sgu_w"][l], 0.0).reshape(SGU_G * SGU_T, SGU_T), bst=w["sgu_b"][l].T,
        sinks=w["attn_sinks"][l].reshape(1, -1), alog=spread(w["dn_a_log"][l]), dtb=spread(w["dn_dt_bias"][l]),
        ng=w["dn_norm"][l].reshape(1, -1))


def _layer_fwd(x, p, cos, sin, l):
    n = lambda s: f"l{l}_{s}"
    h = _rms_fwd(x, p["attn_norm"], n("rms1"))
    proj = _matmul(h, p["win"], out_dtype=BF16, name=n("mm_in"))
    out_a = _sgu_fwd(proj, p["lg"], p["lb"], p["wc"], p["bst"], n("sgu_fwd"))
    qr, kr, vr = _rope_fwd(proj, cos, sin, n("rope_fwd"))
    out_b = _swa_fwd(qr, kr, vr, p["sinks"], n("swa_fwd"))
    q, k, v, bg = _dn_pre_fwd(proj, p["conv"], p["alog"], p["dtb"], n("dn_pre_fwd"))
    o, dn = _dn_core_fwd(q, k, v, bg, n("dn_core_fwd"))
    out_c = _dn_post_fwd(o, proj, p["ng"], n("dn_post_fwd"))
    outs = (out_a, out_b, out_c)
    rest = p.pop("rest")(out_c)
    p.update(wb=rest["w_branch"], wout=rest["w_out"], wgu=rest["w_gate_up"], wdown=rest["w_down"])
    bds = [_matmul(outs[j], p["wb"][j], out_dtype=BF16, name=n(f"mm_branch{j}")) for j in range(3)]
    merged = _merge_fwd(proj, bds, n("merge_fwd"))
    x1 = _matmul(merged, p["wout"], add=x, name=n("mm_out"))
    h2 = _rms_fwd(x1, p["ffn_norm"], n("rms2"))
    gu = _matmul(h2, p["wgu"], out_dtype=BF16, name=n("mm_gu"))
    act = _swiglu_fwd(gu, n("swiglu_fwd"))
    x2 = _matmul(act, p["wdown"], add=x1, name=n("mm_down"))
    saved = dict(x=x, h=h, proj=proj, outs=outs, qr=qr, kr=kr, vr=vr, q=q, k=k, v=v, bg=bg, o=o, dn=dn, bds=bds,
                 merged=merged, x1=x1, h2=h2, gu=gu, act=act)
    return x2, saved


def _layer_bwd(dx2, s, p, cos, sin, l, early=None):
    n = lambda t: f"l{l}_{t}"
    proj = s["proj"]
    g = {}
    g["w_down"] = _matmul(s["act"], dx2, ta=True, out_dtype=BF16, name=n("wg_down"))
    dact = _matmul(dx2, p["wdown"], tb=True, out_dtype=BF16, name=n("dg_down"))
    dgu = _swiglu_bwd(s["gu"], dact, n("swiglu_bwd"))
    g["w_gate_up"] = _matmul(s["h2"], dgu, ta=True, out_dtype=BF16, name=n("wg_gu"))
    dh2 = _matmul(dgu, p["wgu"], tb=True, name=n("dg_gu"))
    dx1, g["ffn_norm"] = _rms_bwd_add(s["x1"], p["ffn_norm"], dh2, dx2, n("rms2_bwd"))
    g["w_out"] = _matmul(s["merged"], dx1, ta=True, out_dtype=BF16, name=n("wg_out"))
    dm = _matmul(dx1, p["wout"], tb=True, name=n("dg_out"))
    dbd0, dbd1, dbd2, dproj = _merge_bwd(proj, s["bds"], dm, n("merge_bwd"))
    dbds = (dbd0, dbd1, dbd2)
    g["w_branch"] = jnp.stack([_matmul(s["outs"][j], dbds[j], ta=True, out_dtype=BF16, name=n(f"wg_branch{j}"))
                               for j in range(3)])
    douts = [_matmul(dbds[j], p["wb"][j], tb=True, name=n(f"dg_branch{j}")) for j in range(3)]
    lg = p["lg"]
    if early is not None:
        token = early({k: g.pop(k) for k in ("w_down", "w_gate_up", "w_out", "w_branch")})
        lg = lg if token is None else lg + token[0, 0]
    dproj, g["sgu_ln_g"], g["sgu_ln_b"], dwc, dbs = _sgu_bwd(proj, lg, p["lb"], p["wc"], p["bst"], douts[0], dproj,
                                                             n("sgu_bwd"))
    g["sgu_w"] = dwc.reshape(SGU_G, SGU_T, SGU_T)
    g["sgu_b"] = dbs.T
    dqr, dkr, dvr, dsink = _swa_bwd(s["qr"], s["kr"], s["vr"], p["sinks"], douts[1], n("swa_bwd"))
    g["attn_sinks"] = dsink[0, :SWA_H]
    dproj = _rope_bwd(dqr, dkr, dvr, cos, sin, dproj, n("rope_bwd"))
    do, dproj, dng = _dn_post_bwd(s["o"], proj, p["ng"], douts[2], dproj, n("dn_post_bwd"))
    g["dn_norm"] = dng[0]
    dq, dk, dv, dbg = _dn_core_bwd(s["q"], s["k"], s["v"], s["bg"], s["dn"], do, n("dn_core_bwd"))
    dpre, dproj, g["dn_conv_w"], dal, ddb = _dn_pre_bwd1(proj, p["conv"], p["alog"], p["dtb"], dq, dk, dv, dbg, dproj,
                                                         n("dn_pre_bwd1"))
    g["dn_a_log"] = dal[0, DN_H:2 * DN_H]
    g["dn_dt_bias"] = ddb[0, DN_H:2 * DN_H]
    dproj = _dn_pre_bwd2(dpre, p["conv"], dproj, n("dn_pre_bwd2"))
    g["w_in"] = _matmul(s["h"], dproj, ta=True, out_dtype=BF16, name=n("wg_in"))
    dh = _matmul(dproj, p["win"], tb=True, name=n("dg_in"))
    dx, g["attn_norm"] = _rms_bwd_add(s["x"], p["attn_norm"], dh, dx1, n("rms1_bwd"))
    g["attn_norm"], g["ffn_norm"] = g["attn_norm"][0], g["ffn_norm"][0]
    g["sgu_ln_g"], g["sgu_ln_b"] = g["sgu_ln_g"][0], g["sgu_ln_b"][0]
    return dx, g


def _local_step(x, positions, target, w, big_of_layer, on_grads):
    cos, sin = _rope_tables(positions)
    params, saves, xs = [], [], x
    for l in range(DEPTH):
        params.append(_layer_params(w, big_of_layer(l, xs), l))
        xs, sv = _layer_fwd(xs, params[l], cos, sin, l)
        saves.append(sv)
    dx, loss_row, dgf = _final_loss(xs, w["final_norm"].reshape(1, -1), target)
    grads = [None] * DEPTH
    for l in reversed(range(DEPTH)):
        early = functools.partial(on_grads, l) if l == 0 else None
        dx, grads[l] = _layer_bwd(dx, saves[l], params[l], cos, sin, l, early)
        token = on_grads(l, {k: grads[l].pop(k) for k in BIG if k in grads[l]})
        if token is not None and l > 0:
            params[l - 1] = dict(params[l - 1], ffn_norm=params[l - 1]["ffn_norm"] + token[0, 0])
    stacked = {k: jnp.stack([grads[l][k] for l in range(DEPTH)]) for k in grads[0]}
    stacked["final_norm"] = dgf[0]
    return loss_row[0, 0], dx, stacked


MESH = pl.DeviceIdType.MESH
HBM_SPEC = pl.BlockSpec(memory_space=pltpu.HBM)
VMEM_SPEC = pl.BlockSpec(memory_space=pltpu.VMEM)
N_CHIPS = 4
FLIPS = tuple((fx, fy, fc) for fx in (0, 1) for fy in (0, 1) for fc in (0, 1))[1:]
BIG = ("w_in", "w_branch", "w_out", "w_gate_up", "w_down")
BIG_SPEC = {
    "w_in": dict(rows=1024, cols=1792, axis=1, keep=1730, down=8),
    "w_branch": dict(rows=1536, cols=256, axis=1, keep=256, down=2),
    "w_out": dict(rows=256, cols=1024, axis=0, keep=1024, down=1),
    "w_gate_up": dict(rows=1024, cols=1408, axis=1, keep=1408, down=8),
    "w_down": dict(rows=704, cols=1024, axis=0, keep=1024, down=4),
}
CONV_ROWS, CONV_COLS = DEPTH * DN_CONV, 3 * MIX // N_CHIPS


def _full_shape(k):
    sp = BIG_SPEC[k]
    return (sp["rows"], N_CHIPS * sp["cols"]) if sp["axis"] == 1 else (N_CHIPS * sp["rows"], sp["cols"])


def _me():
    return lax.axis_index("x"), lax.axis_index("y"), lax.axis_index("c")


def _peer(x, y, c, flip):
    fx, fy, fc = flip
    return (1 - x if fx else x, 1 - y if fy else y, 1 - c if fc else c)


class _Copies:
    def __init__(self, send_sems, recv_sems):
        self.send_sems, self.recv_sems, self.k, self.sent, self.landing = send_sems, recv_sems, 0, [], []

    def _copy(self, k, src, dst, to):
        return pltpu.make_async_remote_copy(src_ref=src, dst_ref=dst, send_sem=self.send_sems.at[k],
                                            recv_sem=self.recv_sems.at[k], device_id=to, device_id_type=MESH)

    def send(self, src, dst, to, lands):
        k = self.k
        self.k += 1
        cp = self._copy(k, src, dst, to)
        cp.start()
        self.sent.append(cp)
        self.landing.append(self._copy(k, lands, lands, to))
        return k

    def wait_landed(self, k):
        self.landing[k].wait_recv()

    def finish(self, landed=()):
        for k, cp in enumerate(self.landing):
            if k not in landed:
                cp.wait_recv()
        for cp in self.sent:
            cp.wait_send()


def _place_shard(shard, k, chip, layer, name):
    sp = BIG_SPEC[k]
    rows, cols, keep = sp["rows"], sp["cols"], sp["keep"]
    tr = _pick(rows, (256, 64))
    nb = rows // tr
    if sp["axis"] == 1:
        out_spec = pl.BlockSpec((tr, cols), lambda i, ch: (i, ch[0]))
    else:
        out_spec = pl.BlockSpec((tr, cols), lambda i, ch: (ch[0] * nb + i, 0))

    def kern(ch_ref, x_ref, o_ref):
        v = x_ref[0].astype(BF16)
        if keep == cols:
            o_ref[...] = v
        else:
            o_ref[:, :keep] = v
            o_ref[:, keep:] = jnp.zeros((tr, cols - keep), BF16)

    return pl.pallas_call(
        kern, name=name, out_shape=jax.ShapeDtypeStruct(_full_shape(k), BF16),
        grid_spec=pltpu.PrefetchScalarGridSpec(
            num_scalar_prefetch=1, grid=(nb,),
            in_specs=[pl.BlockSpec((1, tr, keep), lambda i, ch: (layer, i, 0))], out_specs=out_spec),
        compiler_params=_cparams(("parallel",)),
    )(chip, shard)


def _half_block(ref, k, s, half):
    sp = BIG_SPEC[k]
    hr = sp["rows"] // 2
    if sp["axis"] == 1:
        return ref.at[pl.ds(pl.multiple_of(half * hr, 16), hr), pl.ds(pl.multiple_of(s * sp["cols"], 128), sp["cols"])]
    return ref.at[pl.ds(pl.multiple_of(s * sp["rows"] + half * hr, 16), hr), :]


def _other_chips(x, y):
    return [(1 - x, y), (x, 1 - y), (1 - x, 1 - y)]


ALL_BIG = BIG


def _present(d):
    return tuple(k for k in ALL_BIG if k in d)


def _gather_layer(placed, conv):
    BIG = _present(placed)
    n = len(BIG)
    n_sem = 6 * n + 3

    def body(*refs):
        conv_ref = refs[n]
        out = dict(zip(BIG, refs[n + 1:2 * n + 1]))
        conv_out, send_sems, recv_sems, local_sem = refs[2 * n + 1:]
        x, y, c = _me()
        me = 2 * x + y
        chips = _other_chips(x, y)
        net = _Copies(send_sems, recv_sems)

        def conv_block(s):
            return conv_out.at[:, pl.ds(pl.multiple_of(s * CONV_COLS, 128), CONV_COLS)]

        local = pltpu.make_async_copy(conv_ref, conv_block(me), local_sem)
        local.start()
        first = {}
        for k in BIG:
            for j, (px, py) in enumerate(chips):
                first[k, j] = net.send(_half_block(out[k], k, me, c), _half_block(out[k], k, me, c), (px, py, c),
                                       _half_block(out[k], k, 2 * px + py, c))
        for px, py in chips:
            net.send(conv_ref, conv_block(me), (px, py, c), conv_block(2 * px + py))
        for k in BIG:
            for j, (px, py) in enumerate(chips):
                net.wait_landed(first[k, j])
                net.send(_half_block(out[k], k, 2 * px + py, c), _half_block(out[k], k, 2 * px + py, c), (x, y, 1 - c),
                         _half_block(out[k], k, 2 * px + py, 1 - c))
        net.finish(landed=set(first.values()))
        local.wait()

    out_shape = [jax.ShapeDtypeStruct(_full_shape(k), BF16) for k in BIG]
    out_shape.append(jax.ShapeDtypeStruct((CONV_ROWS, N_CHIPS * CONV_COLS), F32))
    outs = pl.pallas_call(
        body, name="gather_layer", out_shape=out_shape, in_specs=[HBM_SPEC] * (n + 1), out_specs=[HBM_SPEC] * (n + 1),
        input_output_aliases={i: i for i in range(n)},
        scratch_shapes=[pltpu.SemaphoreType.DMA((n_sem,)), pltpu.SemaphoreType.DMA((n_sem,)), pltpu.SemaphoreType.DMA],
    )(*[placed[k] for k in BIG], conv)
    return dict(zip(BIG, outs[:n])), outs[n]


SEM_SPEC = pl.BlockSpec(memory_space=pltpu.SEMAPHORE)


def _behind_copies(arrs, send_sems, recv_sems):
    x, y, c = _me()
    copies = []
    for i, k in enumerate(_present(arrs)):
        for j, (px, py) in enumerate(_other_chips(x, y)):
            copies.append(pltpu.make_async_remote_copy(
                src_ref=_half_block(arrs[k], k, 2 * x + y, c), dst_ref=_half_block(arrs[k], k, 2 * x + y, c),
                send_sem=send_sems.at[3 * i + j], recv_sem=recv_sems.at[3 * i + j], device_id=(px, py, c),
                device_id_type=MESH))
    return copies


def _gather_start(placed, after, tag):
    BIG = _present(placed)
    n = len(BIG)
    N_BEHIND = 3 * n

    def body(*refs):
        arrs = dict(zip(BIG, refs[n + 3:2 * n + 3]))
        send_sems, recv_sems = refs[n + 1], refs[n + 2]
        for cp in _behind_copies(arrs, send_sems, recv_sems):
            cp.start()
        refs[2 * n + 3][...] = jnp.zeros((8, 128), F32)

    outs = pl.pallas_call(
        body, name="gather_start" + tag,
        out_shape=(pltpu.SemaphoreType.DMA((N_BEHIND,)), pltpu.SemaphoreType.DMA((N_BEHIND,)),
                   *[pltpu.HBM(_full_shape(k), BF16) for k in BIG], jax.ShapeDtypeStruct((8, 128), F32)),
        in_specs=[HBM_SPEC] * n + [pl.BlockSpec(memory_space=pl.ANY)],
        out_specs=(SEM_SPEC, SEM_SPEC, *[HBM_SPEC] * n, VMEM_SPEC),
        input_output_aliases={i: i + 2 for i in range(n)},
        compiler_params=pltpu.CompilerParams(has_side_effects=pltpu.SideEffectType.DATAFLOW_SIDE_EFFECTING),
    )(*[pltpu.with_memory_space_constraint(placed[k], pltpu.HBM) for k in BIG], after)
    return outs[0], outs[1], dict(zip(BIG, outs[2:n + 2])), outs[n + 2]


def _gather_wait(send_sems, recv_sems, inflight, after, tag):
    BIG = _present(inflight)
    n = len(BIG)

    def body(*refs):
        arrs = dict(zip(BIG, refs[:n]))
        for cp in _behind_copies(arrs, refs[n], refs[n + 1]):
            cp.wait_send()
            cp.wait_recv()

    outs = pl.pallas_call(
        body, name="gather_wait" + tag, out_shape=tuple(pltpu.HBM(_full_shape(k), BF16) for k in BIG),
        in_specs=[HBM_SPEC] * n + [SEM_SPEC, SEM_SPEC, pl.BlockSpec(memory_space=pl.ANY)], out_specs=(HBM_SPEC,) * n,
        input_output_aliases={i: i for i in range(n)},
        compiler_params=pltpu.CompilerParams(has_side_effects=pltpu.SideEffectType.DATAFLOW_SIDE_EFFECTING),
    )(*[inflight[k] for k in BIG], send_sems, recv_sems, after)
    return dict(zip(BIG, outs))


def _gather_finish(arrs, tag):
    BIG = _present(arrs)
    n = len(BIG)
    N_BEHIND = 3 * n

    def body(*refs):
        out = dict(zip(BIG, refs[n:2 * n]))
        send_sems, recv_sems = refs[2 * n:]
        x, y, c = _me()
        net = _Copies(send_sems, recv_sems)
        for k in BIG:
            for px, py in _other_chips(x, y):
                net.send(_half_block(out[k], k, 2 * px + py, c), _half_block(out[k], k, 2 * px + py, c), (x, y, 1 - c),
                         _half_block(out[k], k, 2 * px + py, 1 - c))
        net.finish()

    outs = pl.pallas_call(
        body, name="gather_finish" + tag, out_shape=[jax.ShapeDtypeStruct(_full_shape(k), BF16) for k in BIG],
        in_specs=[HBM_SPEC] * n, out_specs=[HBM_SPEC] * n, input_output_aliases={i: i for i in range(n)},
        scratch_shapes=[pltpu.SemaphoreType.DMA((N_BEHIND,)), pltpu.SemaphoreType.DMA((N_BEHIND,))],
    )(*[arrs[k] for k in BIG])
    return dict(zip(BIG, outs))


def _row_chunks(ref, rows, n):
    step = rows // n
    return [ref.at[pl.ds(i * step, step), :] for i in range(n)]


def _half_pieces(ref, k, half):
    sp = BIG_SPEC[k]
    hr = sp["rows"] // 2
    if sp["axis"] == 1:
        return [ref.at[pl.ds(pl.multiple_of(half * hr, 16), hr), :]]
    return [ref.at[pl.ds(pl.multiple_of(s * sp["rows"] + half * hr, 16), hr), :] for s in range(N_CHIPS)]


def _half_shape(k):
    rows, cols = _full_shape(k)
    return rows // 2, cols


def _stacked_pieces(ref, k):
    sp = BIG_SPEC[k]
    hr = sp["rows"] // 2
    return [ref] if sp["axis"] == 1 else [ref.at[pl.ds(s * hr, hr), :] for s in range(N_CHIPS)]


def _chip_part(ref, k, s):
    sp = BIG_SPEC[k]
    hr = sp["rows"] // 2
    if sp["axis"] == 1:
        return ref.at[:, pl.ds(pl.multiple_of(s * sp["cols"], 128), sp["cols"])]
    return ref.at[pl.ds(pl.multiple_of(s * hr, 16), hr), :]


def _halves_to_sibling(grads, name):
    BIG = _present(grads)
    n = len(BIG)
    chunks = {k: max(BIG_SPEC[k]["down"] // 2, 1) if BIG_SPEC[k]["axis"] == 1 else 1 for k in BIG}
    n_sem = sum(chunks[k] if BIG_SPEC[k]["axis"] == 1 else N_CHIPS for k in BIG)

    def body(*refs):
        g = dict(zip(BIG, refs[:n]))
        out = dict(zip(BIG, refs[n:2 * n]))
        send_sems, recv_sems = refs[2 * n:]
        x, y, c = _me()
        net = _Copies(send_sems, recv_sems)
        for k in BIG:
            hr = BIG_SPEC[k]["rows"] // 2
            for src, dst in zip(_half_pieces(g[k], k, 1 - c), _stacked_pieces(out[k], k)):
                for s, d in zip(_row_chunks(src, hr, chunks[k]), _row_chunks(dst, hr, chunks[k])):
                    net.send(s, d, (x, y, 1 - c), d)
        net.finish()

    outs = pl.pallas_call(
        body, name=name, out_shape=[jax.ShapeDtypeStruct(_half_shape(k), BF16) for k in BIG],
        in_specs=[HBM_SPEC] * n, out_specs=[HBM_SPEC] * n,
        scratch_shapes=[pltpu.SemaphoreType.DMA((n_sem,)), pltpu.SemaphoreType.DMA((n_sem,))],
    )(*[grads[k] for k in BIG])
    return dict(zip(BIG, outs))


def _add_half(g, other, k, core, name):
    sp = BIG_SPEC[k]
    hr, cols = sp["rows"] // 2, _full_shape(k)[1]
    tr = _pick(hr, (256, 352, 128))
    nb = hr // tr
    if sp["axis"] == 1:
        grid = (nb,)
        g_spec = pl.BlockSpec((tr, cols), lambda i, c: (c[0] * nb + i, 0))
        h_spec = pl.BlockSpec((tr, cols), lambda i, c: (i, 0))
    else:
        grid = (N_CHIPS, nb)
        g_spec = pl.BlockSpec((tr, cols), lambda s, i, c: ((2 * s + c[0]) * nb + i, 0))
        h_spec = pl.BlockSpec((tr, cols), lambda s, i, c: (s * nb + i, 0))

    def kern(c_ref, a_ref, b_ref, o_ref):
        o_ref[...] = (a_ref[...].astype(F32) + b_ref[...].astype(F32)).astype(BF16)

    return pl.pallas_call(
        kern, name=name, out_shape=jax.ShapeDtypeStruct(_half_shape(k), BF16),
        grid_spec=pltpu.PrefetchScalarGridSpec(num_scalar_prefetch=1, grid=grid, in_specs=[g_spec, h_spec],
                                               out_specs=h_spec),
        compiler_params=_cparams(("parallel",) * len(grid)),
    )(core, g, other)


def _part_shape(k):
    return N_CHIPS - 1, BIG_SPEC[k]["rows"] // 2, BIG_SPEC[k]["cols"]


def _scatter_chip_sums(sums, name):
    BIG = _present(sums)
    n = len(BIG)
    N_BEHIND = 3 * n

    def body(*refs):
        src = dict(zip(BIG, refs[:n]))
        out = dict(zip(BIG, refs[n:2 * n]))
        send_sems, recv_sems = refs[2 * n:]
        x, y, c = _me()
        net = _Copies(send_sems, recv_sems)
        for k in BIG:
            for j, (px, py) in enumerate(_other_chips(x, y)):
                net.send(_chip_part(src[k], k, 2 * px + py), out[k].at[j], (px, py, c), out[k].at[j])
        net.finish()

    outs = pl.pallas_call(
        body, name=name, out_shape=[jax.ShapeDtypeStruct(_part_shape(k), BF16) for k in BIG],
        in_specs=[HBM_SPEC] * n, out_specs=[HBM_SPEC] * n,
        scratch_shapes=[pltpu.SemaphoreType.DMA((N_BEHIND,)), pltpu.SemaphoreType.DMA((N_BEHIND,))],
    )(*[sums[k] for k in BIG])
    return dict(zip(BIG, outs))


def _scatter_copies(sums, parts, send_sems, recv_sems):
    x, y, c = _me()
    copies = []
    for i, k in enumerate(_present(sums)):
        for j, (px, py) in enumerate(_other_chips(x, y)):
            copies.append(pltpu.make_async_remote_copy(
                src_ref=_chip_part(sums[k], k, 2 * px + py), dst_ref=parts[k].at[j], send_sem=send_sems.at[3 * i + j],
                recv_sem=recv_sems.at[3 * i + j], device_id=(px, py, c), device_id_type=MESH))
    return copies


def _scatter_start(sums, tag):
    BIG = _present(sums)
    n = len(BIG)
    N_BEHIND = 3 * n
    lands = [pltpu.with_memory_space_constraint(lax.empty(_part_shape(k), BF16), pltpu.HBM) for k in BIG]

    def body(*refs):
        outs = refs[2 * n + 2:4 * n + 2]
        for cp in _scatter_copies(dict(zip(BIG, outs[:n])), dict(zip(BIG, outs[n:])), refs[2 * n], refs[2 * n + 1]):
            cp.start()
        refs[4 * n + 2][...] = jnp.zeros((8, 128), F32)

    outs = pl.pallas_call(
        body, name="scatter_start" + tag,
        out_shape=(pltpu.SemaphoreType.DMA((N_BEHIND,)), pltpu.SemaphoreType.DMA((N_BEHIND,)),
                   *[pltpu.HBM(_half_shape(k), BF16) for k in BIG], *[pltpu.HBM(_part_shape(k), BF16) for k in BIG],
                   jax.ShapeDtypeStruct((8, 128), F32)),
        in_specs=[HBM_SPEC] * (2 * n), out_specs=(SEM_SPEC, SEM_SPEC, *[HBM_SPEC] * (2 * n), VMEM_SPEC),
        input_output_aliases={i: i + 2 for i in range(2 * n)},
        compiler_params=pltpu.CompilerParams(has_side_effects=pltpu.SideEffectType.DATAFLOW_SIDE_EFFECTING),
    )(*[pltpu.with_memory_space_constraint(sums[k], pltpu.HBM) for k in BIG], *lands)
    return outs[0], outs[1], outs[2:2 * n + 2], outs[2 * n + 2]


def _scatter_wait(send_sems, recv_sems, inflight, keys, after, tag):
    BIG = keys
    n = len(BIG)

    def body(*refs):
        for cp in _scatter_copies(dict(zip(BIG, refs[:n])), dict(zip(BIG, refs[n:2 * n])), refs[2 * n], refs[2 * n + 1]):
            cp.wait_send()
            cp.wait_recv()

    outs = pl.pallas_call(
        body, name="scatter_wait" + tag,
        out_shape=(*[pltpu.HBM(_half_shape(k), BF16) for k in BIG], *[pltpu.HBM(_part_shape(k), BF16) for k in BIG]),
        in_specs=[HBM_SPEC] * (2 * n) + [SEM_SPEC, SEM_SPEC, pl.BlockSpec(memory_space=pl.ANY)],
        out_specs=(HBM_SPEC,) * (2 * n), input_output_aliases={i: i for i in range(2 * n)},
        compiler_params=pltpu.CompilerParams(has_side_effects=pltpu.SideEffectType.DATAFLOW_SIDE_EFFECTING),
    )(*inflight, send_sems, recv_sems, after)
    return dict(zip(BIG, outs[:n])), dict(zip(BIG, outs[n:]))


def _sum_half(parts, own, k, where, layer, into, name):
    sp = BIG_SPEC[k]
    rows, cols, keep = sp["rows"], sp["cols"], sp["keep"]
    hr = rows // 2
    tr = _pick(hr, (256, 352, 128))
    nb = hr // tr
    if sp["axis"] == 1:
        own_spec = pl.BlockSpec((tr, cols), lambda i, w: (i, w[0]))
    else:
        own_spec = pl.BlockSpec((tr, cols), lambda i, w: (w[0] * nb + i, 0))

    def kern(w_ref, p_ref, own_ref, *rest):
        tot = own_ref[...].astype(F32)
        for j in range(N_CHIPS - 1):
            tot = tot + p_ref[j].astype(F32)
        rest[-1][0] = tot[:, :keep]

    in_specs = [pl.BlockSpec((N_CHIPS - 1, tr, cols), lambda i, w: (0, i, 0)), own_spec]
    args = [where, parts, own]
    if into is not None:
        in_specs.append(pl.BlockSpec(memory_space=pl.ANY))
        args.append(into)
    return pl.pallas_call(
        kern, name=name, out_shape=jax.ShapeDtypeStruct((DEPTH, rows, keep), F32),
        grid_spec=pltpu.PrefetchScalarGridSpec(
            num_scalar_prefetch=1, grid=(nb,), in_specs=in_specs,
            out_specs=pl.BlockSpec((1, tr, keep), lambda i, w: (layer, w[1] * nb + i, 0))),
        input_output_aliases={} if into is None else {3: 0},
        compiler_params=_cparams(("parallel",)),
    )(*args)


def _exchange_halves(red):
    n = len(BIG)

    def body(*refs):
        out = dict(zip(BIG, refs[n:2 * n]))
        send_sems, recv_sems = refs[2 * n:]
        x, y, c = _me()
        net = _Copies(send_sems, recv_sems)
        for k in BIG:
            hr = BIG_SPEC[k]["rows"] // 2
            for l in range(DEPTH):
                mine = out[k].at[l, pl.ds(pl.multiple_of(c * hr, 8), hr), :]
                theirs = out[k].at[l, pl.ds(pl.multiple_of((1 - c) * hr, 8), hr), :]
                net.send(mine, mine, (x, y, 1 - c), theirs)
        net.finish()

    outs = pl.pallas_call(
        body, name="exchange_halves",
        out_shape=[jax.ShapeDtypeStruct((DEPTH, BIG_SPEC[k]["rows"], BIG_SPEC[k]["keep"]), F32) for k in BIG],
        in_specs=[HBM_SPEC] * n, out_specs=[HBM_SPEC] * n, input_output_aliases={i: i for i in range(n)},
        scratch_shapes=[pltpu.SemaphoreType.DMA((DEPTH * n,)), pltpu.SemaphoreType.DMA((DEPTH * n,))],
    )(*[red[k] for k in BIG])
    return dict(zip(BIG, outs))


def _adam_vals(g, w, m, v):
    m2 = ADAM_B1 * m + (1.0 - ADAM_B1) * g
    v2 = ADAM_B2 * v + (1.0 - ADAM_B2) * (g * g)
    m_hat = m2 / (1.0 - ADAM_B1 ** ADAM_STEP)
    v_hat = v2 / (1.0 - ADAM_B2 ** ADAM_STEP)
    return -ADAM_LR * (m_hat / (jnp.sqrt(v_hat) + ADAM_EPS) + ADAM_WD * w), m2, v2


def _allreduce_small_adam(groups):
    ng = len(groups)

    def body(*refs):
        ins = [refs[4 * i:4 * i + 4] for i in range(ng)]
        outs = [refs[4 * ng + 4 * i:4 * ng + 4 * i + 4] for i in range(ng)]
        bufs = refs[8 * ng:9 * ng]
        send_sems, recv_sems = refs[9 * ng:]
        x, y, c = _me()
        me = 4 * x + 2 * y + c
        net = _Copies(send_sems, recv_sems)
        for (g_ref, _, _, _), buf in zip(ins, bufs):
            buf[me] = g_ref[...]
            for f in FLIPS:
                px, py, pc = _peer(x, y, c, f)
                net.send(g_ref, buf.at[me], (px, py, pc), buf.at[4 * px + 2 * py + pc])
        net.finish()
        for (_, w_ref, m_ref, v_ref), (gs_ref, d_ref, nm_ref, nv_ref), buf in zip(ins, outs, bufs):
            tot = buf[0]
            for d in range(1, 8):
                tot = tot + buf[d]
            gs_ref[...] = tot
            d_ref[...], nm_ref[...], nv_ref[...] = _adam_vals(tot, w_ref[...], m_ref[...], v_ref[...])

    shapes = [jax.ShapeDtypeStruct(g[0].shape, F32) for g in groups for _ in range(4)]
    outs = pl.pallas_call(
        body, name="allreduce_small", out_shape=shapes, in_specs=[VMEM_SPEC] * (4 * ng), out_specs=[VMEM_SPEC] * (4 * ng),
        scratch_shapes=[pltpu.VMEM((8,) + g[0].shape, F32) for g in groups]
        + [pltpu.SemaphoreType.DMA((7 * ng,)), pltpu.SemaphoreType.DMA((7 * ng,))],
        compiler_params=pltpu.CompilerParams(vmem_limit_bytes=VMEM_LIMIT),
    )(*[t for g in groups for t in g])
    return [outs[4 * i:4 * i + 4] for i in range(ng)]


def _adam(g, w, m, v, name, lead_block=1):
    shape = w.shape
    lead, rows, cols = math.prod(shape[:-2]), shape[-2], shape[-1]
    tr = _pick(rows, (256, 352, 64, 8, rows))
    spec = pl.BlockSpec((lead_block, tr, cols), lambda l, i: (l, i, 0))

    def kern(g_ref, w_ref, m_ref, v_ref, d_ref, nm_ref, nv_ref):
        d_ref[...], nm_ref[...], nv_ref[...] = _adam_vals(g_ref[...], w_ref[...], m_ref[...], v_ref[...])

    outs = pl.pallas_call(
        kern, name=name, grid=(lead // lead_block, rows // tr), in_specs=[spec] * 4, out_specs=[spec] * 3,
        out_shape=[jax.ShapeDtypeStruct((lead, rows, cols), F32)] * 3, compiler_params=_cparams(("parallel", "parallel")),
    )(*[t.reshape(lead, rows, cols) for t in (g, w, m, v)])
    return [o.reshape(shape) for o in outs]


SMALL = ("attn_norm", "sgu_ln_g", "sgu_ln_b", "sgu_w", "sgu_b", "attn_sinks", "dn_a_log", "dn_dt_bias", "dn_norm",
         "ffn_norm", "final_norm")
SMALL_2D = {"attn_norm": (DEPTH, D_MODEL), "ffn_norm": (DEPTH, D_MODEL), "final_norm": (1, D_MODEL),
            "sgu_ln_g": (DEPTH, MIX), "sgu_ln_b": (DEPTH, MIX), "sgu_w": (DEPTH * SGU_G * SGU_T, SGU_T),
            "sgu_b": (DEPTH * SGU_G, SGU_T), "dn_norm": (DEPTH, DN_HD)}
TINY = ("attn_sinks", "dn_a_log", "dn_dt_bias")


def _pack_tiny(vals, extra=None):
    flat = [vals[k].astype(F32).reshape(-1) for k in TINY] + ([] if extra is None else [extra.astype(F32).reshape(-1)])
    n = sum(f.shape[0] for f in flat)
    return jnp.concatenate(flat + [jnp.zeros((8 * 128 - n,), F32)]).reshape(8, 128)


def _unpack_tiny(tile, shapes):
    flat, out, o = tile.reshape(-1), {}, 0
    for k in TINY:
        n = math.prod(shapes[k])
        out[k] = flat[o:o + n].reshape(shapes[k])
        o += n
    return out, flat[o]


def _in_col_segments():
    shard, padded = IN_COLS // N_CHIPS, BIG_SPEC["w_in"]["cols"]
    segs, mine = [], 0
    for a, n in IN_PIECES:
        o = a
        while o < a + n:
            end = min(a + n, (o // shard + 1) * shard)
            segs.append(((o // shard) * padded + o % shard, mine + o - a, end - o))
            o = end
        mine += n
    return segs


def _move_cols(x, segs, out_cols, name):
    layers, rows, cols = x.shape
    tr = _pick(rows, (256, rows))
    gaps, at = [], 0
    for d, w in sorted((d, w) for _, d, w in segs):
        if d > at:
            gaps.append((at, d - at))
        at = d + w
    if at < out_cols:
        gaps.append((at, out_cols - at))

    def kern(x_ref, o_ref):
        for s, d, w in segs:
            o_ref[0, :, d:d + w] = x_ref[0, :, s:s + w]
        for d, w in gaps:
            o_ref[0, :, d:d + w] = jnp.zeros((tr, w), x.dtype)

    return pl.pallas_call(
        kern, name=name, grid=(layers, rows // tr), in_specs=[pl.BlockSpec((1, tr, cols), lambda l, i: (l, i, 0))],
        out_specs=pl.BlockSpec((1, tr, out_cols), lambda l, i: (l, i, 0)),
        out_shape=jax.ShapeDtypeStruct((layers, rows, out_cols), x.dtype), compiler_params=_cparams(("parallel", "parallel")),
    )(x)


WEIGHTS = ("attn_norm", "w_in", "sgu_ln_g", "sgu_ln_b", "sgu_w", "sgu_b", "attn_sinks", "dn_conv_w", "dn_a_log",
           "dn_dt_bias", "dn_norm", "w_branch", "w_out", "ffn_norm", "w_gate_up", "w_down", "final_norm")


def kernel(x, positions, attn_norm, w_in, sgu_ln_g, sgu_ln_b, sgu_w, sgu_b, attn_sinks, dn_conv_w, dn_a_log, dn_dt_bias, dn_norm, w_branch, w_out, ffn_norm, w_gate_up, w_down, final_norm, loss_target, m_attn_norm, m_w_in, m_sgu_ln_g, m_sgu_ln_b, m_sgu_w, m_sgu_b, m_attn_sinks, m_dn_conv_w, m_dn_a_log, m_dn_dt_bias, m_dn_norm, m_w_branch, m_w_out, m_ffn_norm, m_w_gate_up, m_w_down, m_final_norm, v_attn_norm, v_w_in, v_sgu_ln_g, v_sgu_ln_b, v_sgu_w, v_sgu_b, v_attn_sinks, v_dn_conv_w, v_dn_a_log, v_dn_dt_bias, v_dn_norm, v_w_branch, v_w_out, v_ffn_norm, v_w_gate_up, v_w_down, v_final_norm):
    given = dict(locals())
    W = {k: given[k] for k in WEIGHTS}
    M = {k: given["m_" + k] for k in WEIGHTS}
    V = {k: given["v_" + k] for k in WEIGHTS}
    chip = 2 * lax.axis_index("x") + lax.axis_index("y")
    core = lax.axis_index("c")
    chip1 = chip.astype(jnp.int32).reshape(1)
    where = jnp.stack([chip, core]).astype(jnp.int32)

    placed = [{k: _place_shard(W[k].reshape(DEPTH, BIG_SPEC[k]["rows"], BIG_SPEC[k]["keep"]), k, chip1, l,
                               f"place{l}_{k}") for k in BIG} for l in range(DEPTH)]
    first, conv_full = _gather_layer({"w_in": placed[0]["w_in"]}, dn_conv_w.reshape(CONV_ROWS, CONV_COLS))
    behind = [_gather_start({k: placed[0][k] for k in BIG if k != "w_in"}, conv_full, "0")]
    behind.append(_gather_start(placed[1], behind[0][3], "1"))
    segs = _in_col_segments()

    def arrived(l, after):
        send_sems, recv_sems, inflight, _ = behind[l]
        return _gather_finish(_gather_wait(send_sems, recv_sems, inflight, after, str(l)), str(l))

    def big_of_layer(l, x_l):
        got = {} if l == 0 else arrived(1, x_l)
        w_in = first["w_in"] if l == 0 else got["w_in"]

        def rest(after):
            full = got or arrived(0, after)
            return dict(full, w_branch=full["w_branch"].reshape(3, MIX, D_MODEL))

        return dict(w_in=_move_cols(w_in[None], segs, IN_R, f"w_in_cols{l}")[0], rest=rest)

    w = {k: W[k] for k in SMALL}
    w["attn_norm"] = attn_norm + behind[1][3][0, 0]
    w["dn_conv_w"] = conv_full.reshape(DEPTH, DN_CONV, 3 * MIX)

    core1 = core.astype(jnp.int32).reshape(1)
    back_segs = [(d, s, n) for s, d, n in segs]
    travelling, sums, parts = [], [{}, {}], [{}, {}]

    def on_grads(l, gl):
        gl, tag = dict(gl), f"{l}_{len(gl)}"
        if "w_in" in gl:
            gl["w_in"] = _move_cols(gl["w_in"][None], back_segs, _full_shape("w_in")[1], f"g_in_cols{l}")[0]
        if "w_branch" in gl:
            gl["w_branch"] = gl["w_branch"].reshape(3 * MIX, D_MODEL)
        sibling = _halves_to_sibling(gl, "halves_to_sibling" + tag)
        chip_sums = {k: _add_half(gl[k], sibling[k], k, core1, f"chip_sum{l}_{k}") for k in gl}
        if l == 0 and "w_in" in gl:
            sums[l].update(chip_sums)
            parts[l].update(_scatter_chip_sums(chip_sums, "scatter_chip_sums"))
            return None
        send_sems, recv_sems, inflight, token = _scatter_start(chip_sums, tag)
        travelling.append((l, send_sems, recv_sems, inflight, _present(gl), tag))
        return token

    loss, dx, g = _local_step(x[0], positions[0], loss_target[0], w, big_of_layer, on_grads)
    for l, send_sems, recv_sems, inflight, keys, tag in travelling:
        landed = _scatter_wait(send_sems, recv_sems, inflight, keys, dx, tag)
        sums[l].update(landed[0])
        parts[l].update(landed[1])
    red = {k: _sum_half(parts[1][k], sums[1][k], k, where, 1, None, f"sum1_{k}") for k in BIG}
    red = {k: _sum_half(parts[0][k], sums[0][k], k, where, 0, red[k], f"sum0_{k}") for k in BIG}
    reduced = _exchange_halves(red)
    grads = {k: reduced[k].reshape(W[k].shape) for k in BIG}

    conv_2d = (CONV_ROWS, N_CHIPS * CONV_COLS)
    conv_zero = jnp.zeros(conv_2d, F32)
    groups = [tuple(d[k].reshape(SMALL_2D[k]) for d in (g, W, M, V)) for k in SMALL_2D]
    groups.append((g["dn_conv_w"].reshape(conv_2d), conv_zero, conv_zero, conv_zero))
    groups.append((_pack_tiny(g, loss), _pack_tiny(W), _pack_tiny(M), _pack_tiny(V)))
    summed = _allreduce_small_adam(groups)
    delta, new_m, new_v = {}, {}, {}
    for k, outs in zip(SMALL_2D, summed):
        for d, t in zip((grads, delta, new_m, new_v), outs):
            d[k] = t.reshape(W[k].shape)
    conv_sum = summed[len(SMALL_2D)][0].reshape(g["dn_conv_w"].shape)
    grads["dn_conv_w"] = lax.dynamic_slice_in_dim(conv_sum, chip * dn_conv_w.shape[2], dn_conv_w.shape[2], axis=2)
    tiny_shapes = {k: W[k].shape for k in TINY}
    tiny, loss_total = _unpack_tiny(summed[-1][0], tiny_shapes)
    grads.update(tiny)
    for d, t in zip((delta, new_m, new_v), summed[-1][1:]):
        d.update(_unpack_tiny(t, tiny_shapes)[0])
    for k in ("w_branch", "w_out", "w_gate_up", "w_down", "dn_conv_w"):
        delta[k], new_m[k], new_v[k] = _adam(grads[k], W[k], M[k], V[k], "adam_" + k)
    lead_first = lambda t: jnp.transpose(t, (2, 0, 1))
    outs = _adam(*[lead_first(d["w_in"]) for d in (grads, W, M, V)], "adam_w_in", lead_block=IN_COLS // N_CHIPS // 10)
    delta["w_in"], new_m["w_in"], new_v["w_in"] = (jnp.transpose(o, (1, 2, 0)) for o in outs)

    return (loss_total, dx[None], *[grads[k] for k in WEIGHTS], *[delta[k] for k in WEIGHTS],
            *[new_m[k] for k in WEIGHTS], *[new_v[k] for k in WEIGHTS])
```

```python
import functools
import math

import jax
import jax.numpy as jnp
from jax import lax
from jax.experimental import pallas as pl
from jax.experimental.pallas import tpu as pltpu

F32 = jnp.float32
BF16 = jnp.bfloat16
HI = lax.Precision.HIGHEST

D_MODEL = 1024
DEPTH = 2
MIX = 512
EPS = 1e-6
SGU_G, SGU_T = 4, 128
SWA_H, SWA_KV, SWA_HD, WINDOW = 8, 2, 64, 128
ROPE_THETA, ROPE_DIM = 500000.0, 16
DN_H, DN_HD, DN_CONV, DN_C = 4, 128, 4, 64
D_FF = 2816
IN_COLS = 6920
IN_PIECES = ((3848, 3072), (1792, 1536), (3328, 512), (0, 512), (512, 512), (1024, 512), (1536, 128), (1664, 128),
             (3840, 8))
IN_PAD = 120
IN_R = 7040
C_GATE, C_QKV, C_ZC, C_UA, C_VA, C_QB, C_KB, C_VB, C_SM = 0, 3072, 4608, 5120, 5632, 6144, 6656, 6784, 6912

ADAM_LR, ADAM_B1, ADAM_B2, ADAM_EPS, ADAM_WD, ADAM_STEP = 0.001, 0.9, 0.999, 1e-08, 0.01, 10
VMEM_LIMIT = 56 * 1024 * 1024


def _cparams(sem):
    return pltpu.CompilerParams(dimension_semantics=sem, vmem_limit_bytes=VMEM_LIMIT)


def _dg(a, b, ca, cb, prec=None):
    return lax.dot_general(a, b, (((ca,), (cb,)), ((), ())), precision=prec, preferred_element_type=F32)


def _split(x):
    hi = x.astype(BF16)
    return hi, (x - hi.astype(F32)).astype(BF16)


def _dg3_many(as_, bs, ca, cb):
    sa = [_split(a) for a in as_]
    sb = [_split(b) for b in bs]
    hh = [_dg(a[0], b[0], ca, cb) for a, b in zip(sa, sb)]
    hl = [_dg(a[0], b[1], ca, cb) for a, b in zip(sa, sb)]
    lh = [_dg(a[1], b[0], ca, cb) for a, b in zip(sa, sb)]
    return [x + (y + z) for x, y, z in zip(hh, hl, lh)]


def _dg_exact_lhs_many(a01, bs, ca, cb):
    a = a01.astype(BF16)
    b1 = [b.astype(BF16) for b in bs]
    r1 = [b - t.astype(F32) for b, t in zip(bs, b1)]
    b2 = [r.astype(BF16) for r in r1]
    b3 = [(r - t.astype(F32)).astype(BF16) for r, t in zip(r1, b2)]
    d1 = [_dg(a, t, ca, cb) for t in b1]
    d2 = [_dg(a, t, ca, cb) for t in b2]
    d3 = [_dg(a, t, ca, cb) for t in b3]
    return [x + (y + z) for x, y, z in zip(d1, d2, d3)]


def _mm(a, b):
    return _dg(a.astype(BF16), b.astype(BF16), 1, 0)


def _mm_nt(a, b):
    return _dg(a.astype(BF16), b.astype(BF16), 1, 1)


def _mm_tn(a, b):
    return _dg(a.astype(BF16), b.astype(BF16), 0, 0)


def _sigmoid(x):
    return 0.5 * jnp.tanh(0.5 * x) + 0.5


def _silu(x):
    return x * _sigmoid(x)


def _dsilu(x):
    s = _sigmoid(x)
    return s * (1.0 + x * (1.0 - s))


_GC = math.sqrt(2.0 / math.pi)


def _gelu(x):
    return 0.5 * x * (1.0 + jnp.tanh(_GC * (x + 0.044715 * x * x * x)))


def _dgelu(x):
    t = jnp.tanh(_GC * (x + 0.044715 * x * x * x))
    return 0.5 * (1.0 + t) + 0.5 * x * (1.0 - t * t) * _GC * (1.0 + 3.0 * 0.044715 * x * x)


def _softplus(x):
    return jnp.maximum(x, 0.0) + jnp.log(1.0 + jnp.exp(-jnp.abs(x)))


def _acc(ref, val, i):
    @pl.when(i == 0)
    def _():
        ref[...] = val

    @pl.when(i > 0)
    def _():
        ref[...] += val


def _halo_rows(dtype):
    return 8 * 4 // jnp.dtype(dtype).itemsize


def _tok_call(body, name, S, TB, tok_in, const_in=(), tok_out=(), acc_out=(), prev_in=(), next_in=(), smem_in=()):
    nb = S // TB
    in_specs, args = [], []
    for a, w, cb in tok_in:
        in_specs.append(pl.BlockSpec((TB, w), functools.partial(lambda i, cb: (i, cb), cb=cb)))
        args.append(a)
    for a, w, cb in prev_in:
        hr = _halo_rows(a.dtype)
        in_specs.append(pl.BlockSpec((hr, w), functools.partial(
            lambda i, cb, r: (jnp.maximum(i * r - 1, 0), cb), cb=cb, r=TB // hr)))
        args.append(a)
    for a, w, cb in next_in:
        hr = _halo_rows(a.dtype)
        in_specs.append(pl.BlockSpec((hr, w), functools.partial(
            lambda i, cb, r, last: (jnp.minimum((i + 1) * r, last), cb), cb=cb, r=TB // hr, last=S // hr - 1)))
        args.append(a)
    for a in const_in:
        in_specs.append(pl.BlockSpec(a.shape, lambda i: (0, 0)))
        args.append(a)
    for a in smem_in:
        in_specs.append(pl.BlockSpec(memory_space=pltpu.SMEM))
        args.append(a)
    out_specs, out_shape, aliases, shared = [], [], {}, {}
    for o, (w, dt, *dest) in enumerate(tok_out):
        if not dest:
            out_specs.append(pl.BlockSpec((TB, w), lambda i: (i, 0)))
            out_shape.append(jax.ShapeDtypeStruct((S, w), dt))
            continue
        cb, wide = dest
        out_specs.append(pl.BlockSpec((TB, w), functools.partial(lambda i, cb: (i, cb), cb=cb)))
        out_shape.append(jax.ShapeDtypeStruct((S, wide if isinstance(wide, int) else wide.shape[1]), dt))
        if not isinstance(wide, int):
            if id(wide) not in shared:
                shared[id(wide)] = len(args)
                in_specs.append(pl.BlockSpec(memory_space=pl.ANY))
                args.append(wide)
            aliases[shared[id(wide)]] = o
    for shp, dt in acc_out:
        out_specs.append(pl.BlockSpec(shp, lambda i: (0, 0)))
        out_shape.append(jax.ShapeDtypeStruct(shp, dt))
    n_extra = len(shared)

    def kern(*refs):
        n_in = len(in_specs) - n_extra
        body(pl.program_id(0), *refs[:n_in], *refs[n_in + n_extra:])

    return pl.pallas_call(
        kern, name=name, grid=(nb,), in_specs=in_specs, out_specs=out_specs, out_shape=out_shape,
        input_output_aliases=aliases, compiler_params=_cparams(("arbitrary",)),
    )(*args)


MM_BLOCKS = (1024, 1408, 640, 512, 256, 128)


def _pick(n, cands):
    for c in cands:
        if n % c == 0:
            return c
    return n


MM_VMEM_BUDGET = 44 * 1024 * 1024


def _mm_blocks(M, N, K, a_bytes, b_bytes, o_bytes, add_bytes):
    bn = _pick(N, MM_BLOCKS)
    fits = None
    for bk in [K] + [c for c in (2816, 2048) + MM_BLOCKS if c < K and K % c == 0]:
        for bm in [c for c in (2048,) + MM_BLOCKS if M % c == 0 and c >= min(M, 512)]:
            b_bufs = 1 if (bk == K and bn == N) else 2
            need = 2 * bm * bk * a_bytes + b_bufs * bk * bn * b_bytes + 2 * bm * bn * (o_bytes + add_bytes)
            need += bm * bn * 4 if bk < K else 0
            if need <= MM_VMEM_BUDGET:
                fits = fits or (bm, bn, bk)
                if (M // bm) * (N // bn) * (K // bk) >= 4:
                    return bm, bn, bk
    if fits is None:
        raise ValueError(f"no matmul blocks for {(M, N, K)}")
    return fits


def _matmul(a, b, *, ta=False, tb=False, add=None, out_dtype=F32, name):
    M, K = (a.shape[1], a.shape[0]) if ta else a.shape
    N = b.shape[0] if tb else b.shape[1]
    bm, bn, bk = _mm_blocks(M, N, K, a.dtype.itemsize, b.dtype.itemsize, jnp.dtype(out_dtype).itemsize,
                            0 if add is None else add.dtype.itemsize)
    nk = K // bk
    b_mode = dict(pipeline_mode=pl.Buffered(1)) if (bk == K and bn == N) else {}
    a_spec = pl.BlockSpec((bk, bm), lambda i, j, k: (k, i)) if ta else pl.BlockSpec((bm, bk), lambda i, j, k: (i, k))
    b_spec = (pl.BlockSpec((bn, bk), lambda i, j, k: (j, k), **b_mode) if tb
              else pl.BlockSpec((bk, bn), lambda i, j, k: (k, j), **b_mode))
    o_spec = pl.BlockSpec((bm, bn), lambda i, j, k: (i, j))
    ca, cb = (0 if ta else 1), (1 if tb else 0)

    def kern(*refs):
        a_ref, b_ref = refs[:2]
        add_ref = refs[2] if add is not None else None
        o_ref = refs[3] if add is not None else refs[2]
        p = _dg(a_ref[...].astype(BF16), b_ref[...].astype(BF16), ca, cb)

        def finish(r):
            if add is not None:
                r = r + add_ref[...].astype(F32)
            o_ref[...] = r.astype(out_dtype)

        if nk == 1:
            finish(p)
            return
        acc_ref = refs[-1]
        k = pl.program_id(2)

        @pl.when(k == 0)
        def _():
            acc_ref[...] = p

        @pl.when((k > 0) & (k < nk - 1))
        def _():
            acc_ref[...] += p

        @pl.when(k == nk - 1)
        def _():
            finish(acc_ref[...] + p)

    in_specs = [a_spec, b_spec] + ([o_spec] if add is not None else [])
    args = (a, b) + ((add,) if add is not None else ())
    return pl.pallas_call(
        kern, name=name, grid=(M // bm, N // bn, nk), in_specs=in_specs, out_specs=o_spec,
        out_shape=jax.ShapeDtypeStruct((M, N), out_dtype),
        scratch_shapes=[pltpu.VMEM((bm, bn), F32)] if nk > 1 else [],
        compiler_params=_cparams(("parallel", "parallel", "arbitrary")),
    )(*args)


def _rms_fwd(x, g, name):
    S = x.shape[0]

    def body(i, x_ref, g_ref, h_ref):
        xv = x_ref[...]
        r = lax.rsqrt(jnp.mean(xv * xv, axis=-1, keepdims=True) + EPS)
        h_ref[...] = (xv * r * g_ref[...]).astype(BF16)

    return _tok_call(body, name, S, min(S, 512), [(x, D_MODEL, 0)], [g], [(D_MODEL, BF16)])[0]


def _rms_bwd_vals(xv, g, dh):
    r = lax.rsqrt(jnp.mean(xv * xv, axis=-1, keepdims=True) + EPS)
    u = dh * g
    dx = r * u - xv * (r * r * r) * jnp.mean(u * xv, axis=-1, keepdims=True)
    dg = jnp.sum(dh * xv * r, axis=0, keepdims=True)
    return dx, dg


def _rms_bwd_add(x, g, dh, dres, name):
    S = x.shape[0]

    def body(i, x_ref, dh_ref, dr_ref, g_ref, dx_ref, dg_ref):
        dx, dg = _rms_bwd_vals(x_ref[...], g_ref[...], dh_ref[...].astype(F32))
        dx_ref[...] = dr_ref[...] + dx
        _acc(dg_ref, dg, i)

    return _tok_call(body, name, S, min(S, 512), [(x, D_MODEL, 0), (dh, D_MODEL, 0), (dres, D_MODEL, 0)], [g],
                     [(D_MODEL, F32)], [((1, D_MODEL), F32)])


def _final_loss(x, g, target):
    S = x.shape[0]

    def body(i, x_ref, t_ref, g_ref, dx_ref, loss_ref, dg_ref):
        xv, gv = x_ref[...], g_ref[...]
        r = lax.rsqrt(jnp.mean(xv * xv, axis=-1, keepdims=True) + EPS)
        e = xv * r * gv - t_ref[...]
        part = 0.5 * jnp.sum(jnp.mean(e * e, axis=-1, keepdims=True), axis=0, keepdims=True)
        dx, dg = _rms_bwd_vals(xv, gv, e * (1.0 / D_MODEL))
        dx_ref[...] = dx
        _acc(loss_ref, jnp.broadcast_to(part, (1, 128)), i)
        _acc(dg_ref, dg, i)

    return _tok_call(body, "final_loss", S, min(S, 512), [(x, D_MODEL, 0), (target, D_MODEL, 0)], [g],
                     [(D_MODEL, F32)], [((1, 128), F32), ((1, D_MODEL), F32)])


def _swiglu_fwd(gu, name):
    S = gu.shape[0]

    def body(i, gu_ref, a_ref):
        a_ref[...] = (_silu(gu_ref[:, :D_FF].astype(F32)) * gu_ref[:, D_FF:].astype(F32)).astype(BF16)

    return _tok_call(body, name, S, min(S, 256), [(gu, 2 * D_FF, 0)], [], [(D_FF, BF16)])[0]


def _swiglu_bwd(gu, dact, name):
    S = gu.shape[0]

    def body(i, gu_ref, da_ref, dgu_ref):
        gg, uu, da = gu_ref[:, :D_FF].astype(F32), gu_ref[:, D_FF:].astype(F32), da_ref[...].astype(F32)
        dgu_ref[:, :D_FF] = (da * uu * _dsilu(gg)).astype(BF16)
        dgu_ref[:, D_FF:] = (da * _silu(gg)).astype(BF16)

    return _tok_call(body, name, S, min(S, 256), [(gu, 2 * D_FF, 0), (dact, D_FF, 0)], [], [(2 * D_FF, BF16)])[0]


def _merge_fwd(proj, bds, name):
    S = proj.shape[0]

    def body(i, g0, g1, g2, b0, b1, b2, m_ref):
        m = jnp.zeros(m_ref.shape, F32)
        for gr, br in ((g0, b0), (g1, b1), (g2, b2)):
            m = m + _sigmoid(gr[...].astype(F32)) * br[...].astype(F32)
        m_ref[...] = m.astype(BF16)

    tok = [(proj, D_MODEL, n) for n in range(3)] + [(b, D_MODEL, 0) for b in bds]
    return _tok_call(body, name, S, min(S, 512), tok, [], [(D_MODEL, BF16)])[0]


def _merge_bwd(proj, bds, dm, name):
    S = proj.shape[0]

    def body(i, g0, g1, g2, b0, b1, b2, dm_ref, d0, d1, d2, dgp_ref):
        dmv = dm_ref[...]
        for n, (gr, br, dr) in enumerate(((g0, b0, d0), (g1, b1, d1), (g2, b2, d2))):
            s = _sigmoid(gr[...].astype(F32))
            dr[...] = (dmv * s).astype(BF16)
            dgp_ref[:, n * D_MODEL:(n + 1) * D_MODEL] = (dmv * br[...].astype(F32) * s * (1.0 - s)).astype(BF16)

    tok = [(proj, D_MODEL, n) for n in range(3)] + [(b, D_MODEL, 0) for b in bds] + [(dm, D_MODEL, 0)]
    return _tok_call(body, name, S, min(S, 512), tok, [],
                     [(D_MODEL, BF16)] * 3 + [(3 * D_MODEL, BF16, C_GATE // (3 * D_MODEL), IN_R)])


def _sgu_ln(v, lg, lb):
    mu = jnp.mean(v, axis=-1, keepdims=True)
    vc = v - mu
    rstd = lax.rsqrt(jnp.mean(vc * vc, axis=-1, keepdims=True) + EPS)
    vhat = vc * rstd
    return vhat, rstd, vhat * lg + lb


def _sgu_fwd(proj, lg, lb, wc, bst, name):
    S = proj.shape[0]

    def body(i, ua_ref, va_ref, lg_ref, lb_ref, wc_ref, bs_ref, o_ref):
        u = _gelu(ua_ref[...].astype(F32))
        _, _, vn = _sgu_ln(_gelu(va_ref[...].astype(F32)), lg_ref[...], lb_ref[...])
        for g in range(SGU_G):
            sl = slice(g * 128, (g + 1) * 128)
            mixed = _mm(wc_ref[sl, :], vn[:, sl]) + bs_ref[:, g:g + 1]
            o_ref[:, sl] = (u[:, sl] * mixed).astype(BF16)

    return _tok_call(body, name, S, SGU_T, [(proj, MIX, C_UA // MIX), (proj, MIX, C_VA // MIX)], [lg, lb, wc, bst],
                     [(MIX, BF16)])[0]


def _sgu_bwd(proj, lg, lb, wc, bst, dout, dproj, name):
    S = proj.shape[0]

    def body(i, ua_ref, va_ref, do_ref, lg_ref, lb_ref, wc_ref, bs_ref, duv_ref, dlg_ref, dlb_ref, dwc_ref,
             dbs_ref):
        ua, va, do = ua_ref[...].astype(F32), va_ref[...].astype(F32), do_ref[...].astype(F32)
        u = _gelu(ua)
        lgv = lg_ref[...]
        vhat, rstd, vn = _sgu_ln(_gelu(va), lgv, lb_ref[...])
        tril = lax.broadcasted_iota(jnp.int32, (128, 128), 0) >= lax.broadcasted_iota(jnp.int32, (128, 128), 1)
        lane4 = lax.broadcasted_iota(jnp.int32, (128, 4), 1)
        gs = range(SGU_G)
        sls = [slice(g * 128, (g + 1) * 128) for g in gs]
        wgs = [wc_ref[sl, :] for sl in sls]
        mixed = [_mm(wgs[g], vn[:, sls[g]]) for g in gs]
        dmix = [do[:, sl] * u[:, sl] for sl in sls]
        dwg = [_mm_nt(dmix[g], vn[:, sls[g]]) for g in gs]
        dvn = jnp.concatenate([_mm_tn(wgs[g], dmix[g]) for g in gs], axis=1)
        dbs = jnp.zeros((128, 4), F32)
        for g in gs:
            duv_ref[:, sls[g]] = (do[:, sls[g]] * (mixed[g] + bs_ref[:, g:g + 1]) * _dgelu(ua[:, sls[g]])).astype(BF16)
            dbs = dbs + jnp.where(lane4 == g, jnp.sum(dmix[g], axis=-1, keepdims=True), 0.0)
            _acc(dwc_ref.at[sls[g], :], jnp.where(tril, dwg[g], 0.0), i)
        _acc(dbs_ref, dbs, i)
        _acc(dlg_ref, jnp.sum(dvn * vhat, axis=0, keepdims=True), i)
        _acc(dlb_ref, jnp.sum(dvn, axis=0, keepdims=True), i)
        dvh = dvn * lgv
        dv = rstd * (dvh - jnp.mean(dvh, axis=-1, keepdims=True) - vhat * jnp.mean(dvh * vhat, axis=-1, keepdims=True))
        duv_ref[:, MIX:] = (dv * _dgelu(va)).astype(BF16)

    return _tok_call(body, name, S, SGU_T, [(proj, MIX, C_UA // MIX), (proj, MIX, C_VA // MIX), (dout, MIX, 0)],
                     [lg, lb, wc, bst], [(2 * MIX, BF16, C_UA // (2 * MIX), dproj)],
                     [((1, MIX), F32), ((1, MIX), F32), ((SGU_G * 128, 128), F32), ((128, 4), F32)])


def _rope_tables(positions):
    S = positions.shape[0]
    inv_freq = ROPE_THETA ** (-jnp.arange(0, ROPE_DIM, 2, dtype=F32) / ROPE_DIM)
    ang = positions.astype(F32)[:, None] * inv_freq
    c, s = jnp.cos(ang), jnp.sin(ang)
    c64 = jnp.concatenate([c, c, jnp.ones((S, SWA_HD - ROPE_DIM), F32)], axis=1)
    s64 = jnp.concatenate([-s, s, jnp.zeros((S, SWA_HD - ROPE_DIM), F32)], axis=1)
    return jnp.tile(c64, (1, 2)), jnp.tile(s64, (1, 2))


def _rope128(x, c, s):
    lane = lax.broadcasted_iota(jnp.int32, x.shape, 1) % SWA_HD
    swapped = jnp.where(lane < ROPE_DIM // 2, pltpu.roll(x, 128 - ROPE_DIM // 2, 1), pltpu.roll(x, ROPE_DIM // 2, 1))
    return x * c + swapped * s


def _rope_t128(y, c, s):
    ys = y * s
    lane = lax.broadcasted_iota(jnp.int32, y.shape, 1) % SWA_HD
    swapped = jnp.where(lane < ROPE_DIM // 2, pltpu.roll(ys, 128 - ROPE_DIM // 2, 1), pltpu.roll(ys, ROPE_DIM // 2, 1))
    return y * c + jnp.where(lane < ROPE_DIM, swapped, 0.0)


def _rope_fwd(proj, cos, sin, name):
    S = proj.shape[0]
    scale = SWA_HD ** -0.5

    def body(i, q_ref, k_ref, v_ref, c_ref, s_ref, qo_ref, ko_ref, vo_ref):
        c, s = c_ref[...], s_ref[...]
        for j in range(4):
            sl = slice(j * 128, (j + 1) * 128)
            qo_ref[:, sl] = (_rope128(q_ref[:, sl].astype(F32), c, s) * scale).astype(BF16)
        ko_ref[...] = _rope128(k_ref[...].astype(F32), c, s).astype(BF16)
        vo_ref[...] = v_ref[...].astype(BF16)

    return _tok_call(body, name, S, min(S, 512),
                     [(proj, MIX, C_QB // MIX), (proj, 128, C_KB // 128), (proj, 128, C_VB // 128), (cos, 128, 0),
                      (sin, 128, 0)], [], [(MIX, BF16), (128, BF16), (128, BF16)])


def _rope_bwd(dq, dk, dv, cos, sin, dproj, name):
    S = dq.shape[0]
    scale = SWA_HD ** -0.5
    width = C_SM - C_QB

    def body(i, dq_ref, dk_ref, dv_ref, c_ref, s_ref, o_ref):
        c, s = c_ref[...], s_ref[...]
        for j in range(4):
            sl = slice(j * 128, (j + 1) * 128)
            o_ref[:, sl] = _rope_t128(dq_ref[:, sl] * scale, c, s).astype(BF16)
        o_ref[:, C_KB - C_QB:C_VB - C_QB] = _rope_t128(dk_ref[...], c, s).astype(BF16)
        o_ref[:, C_VB - C_QB:] = dv_ref[...].astype(BF16)

    return _tok_call(body, name, S, min(S, 512),
                     [(dq, MIX, 0), (dk, 128, 0), (dv, 128, 0), (cos, 128, 0), (sin, 128, 0)], [],
                     [(width, BF16, C_QB // width, dproj)])[0]


def _swa_band(i, k_ref, v_ref):
    pstart = pl.multiple_of(jnp.maximum(i - 1, 0) * WINDOW, WINDOW)
    cstart = pl.multiple_of(i * WINDOW, WINDOW)
    kb = jnp.concatenate([k_ref[pl.ds(pstart, WINDOW), :], k_ref[pl.ds(cstart, WINDOW), :]], axis=0)
    vb = jnp.concatenate([v_ref[pl.ds(pstart, WINDOW), :], v_ref[pl.ds(cstart, WINDOW), :]], axis=0)
    qi = lax.broadcasted_iota(jnp.int32, (WINDOW, 2 * WINDOW), 0)
    sj = lax.broadcasted_iota(jnp.int32, (WINDOW, 2 * WINDOW), 1)
    mask = (sj > qi) & (sj <= qi + WINDOW) & ((i > 0) | (sj >= WINDOW))
    return kb, vb, mask, pstart, cstart


def _swa_blocks(S):
    return 2 if S % (2 * WINDOW) == 0 else 1


def _swa_probs(qs, kh, mask, sinks):
    logits = [jnp.where(mask, _dg(qh, kh, 1, 1), -1e30) for qh in qs]
    m = [jnp.maximum(jnp.max(l, axis=-1, keepdims=True), s) for l, s in zip(logits, sinks)]
    p = [jnp.exp(l - mm) for l, mm in zip(logits, m)]
    ps = [jnp.exp(s - mm) for s, mm in zip(sinks, m)]
    inv = [1.0 / (jnp.sum(pp, axis=-1, keepdims=True) + s) for pp, s in zip(p, ps)]
    return [pp * iv for pp, iv in zip(p, inv)], [s * iv for s, iv in zip(ps, inv)]


def _swa_fwd(q, k, v, sinks, name):
    S = q.shape[0]
    G = SWA_H // SWA_KV
    QB = _swa_blocks(S)

    def body(i, q_ref, k_ref, v_ref, s_ref, o_ref):
        for qb in range(QB):
            rows = slice(qb * WINDOW, (qb + 1) * WINDOW)
            kb, vb, mask, _, _ = _swa_band(i * QB + qb, k_ref, v_ref)
            qv = q_ref[rows, :]
            for kv in range(SWA_KV):
                ksl = slice(kv * SWA_HD, (kv + 1) * SWA_HD)
                heads = range(kv * G, (kv + 1) * G)
                pn, _ = _swa_probs([qv[:, h * SWA_HD:(h + 1) * SWA_HD] for h in heads], kb[:, ksl], mask,
                                   [s_ref[0, h] for h in heads])
                outs = [_dg(p.astype(BF16), vb[:, ksl], 1, 0) for p in pn]
                for h, o in zip(heads, outs):
                    o_ref[rows, h * SWA_HD:(h + 1) * SWA_HD] = o.astype(BF16)

    return _tok_call(body, name, S, QB * WINDOW, [(q, MIX, 0)], [k, v], [(MIX, BF16)], smem_in=[sinks])[0]


def _swa_bwd(q, k, v, sinks, dout, name):
    S = q.shape[0]
    QB = _swa_blocks(S)

    def body(i, q_ref, do_ref, k_ref, v_ref, s_ref, dq_ref, dk_ref, dv_ref, ds_ref):
        @pl.when(i == 0)
        def _():
            dk_ref[...] = jnp.zeros_like(dk_ref)
            dv_ref[...] = jnp.zeros_like(dv_ref)

        lane = lax.broadcasted_iota(jnp.int32, (1, 128), 1)
        dsink = jnp.zeros((1, 128), F32)
        G = SWA_H // SWA_KV
        for qb in range(QB):
            rows = slice(qb * WINDOW, (qb + 1) * WINDOW)
            kb, vb, mask, pstart, cstart = _swa_band(i * QB + qb, k_ref, v_ref)
            qv, dov = q_ref[rows, :], do_ref[rows, :]
            dkb, dvb = [], []
            for kv in range(SWA_KV):
                ksl = slice(kv * SWA_HD, (kv + 1) * SWA_HD)
                heads = range(kv * G, (kv + 1) * G)
                qs = [qv[:, h * SWA_HD:(h + 1) * SWA_HD] for h in heads]
                dos = [dov[:, h * SWA_HD:(h + 1) * SWA_HD].astype(BF16) for h in heads]
                pn, psn = _swa_probs(qs, kb[:, ksl], mask, [s_ref[0, h] for h in heads])
                dp = [_dg(d, vb[:, ksl], 1, 1) for d in dos]
                delta = [jnp.sum(a * b, axis=-1, keepdims=True) for a, b in zip(dp, pn)]
                dsc = [(p * (a - d)).astype(BF16) for p, a, d in zip(pn, dp, delta)]
                dqs = [_dg(s, kb[:, ksl], 1, 0) for s in dsc]
                dks = [_dg(s, qh, 0, 0) for s, qh in zip(dsc, qs)]
                dvs = [_dg(p.astype(BF16), d, 0, 0) for p, d in zip(pn, dos)]
                for n_, h in enumerate(heads):
                    dq_ref[rows, h * SWA_HD:(h + 1) * SWA_HD] = dqs[n_]
                    dsink = dsink + jnp.where(lane == h, -jnp.sum(psn[n_] * delta[n_], axis=0, keepdims=True), 0.0)
                dkb.append((dks[0] + dks[1]) + (dks[2] + dks[3]))
                dvb.append((dvs[0] + dvs[1]) + (dvs[2] + dvs[3]))
            dkb = jnp.concatenate(dkb, axis=1)
            dvb = jnp.concatenate(dvb, axis=1)
            dk_ref[pl.ds(pstart, WINDOW), :] += dkb[:WINDOW]
            dv_ref[pl.ds(pstart, WINDOW), :] += dvb[:WINDOW]
            dk_ref[pl.ds(cstart, WINDOW), :] += dkb[WINDOW:]
            dv_ref[pl.ds(cstart, WINDOW), :] += dvb[WINDOW:]
        _acc(ds_ref, dsink, i)

    return _tok_call(body, name, S, QB * WINDOW, [(q, MIX, 0), (dout, MIX, 0)], [k, v], [(MIX, F32)],
                     [((S, 128), F32), ((S, 128), F32), ((1, 128), F32)], smem_in=[sinks])


def _shift_rows(xs, k):
    return xs if k == 0 else pltpu.roll(xs, k, 0)


def _dn_conv(x_ref, p_ref, w_ref, i):
    hr = p_ref.shape[0]
    halo = jnp.where(i > 0, p_ref[...].astype(F32), 0.0)
    xs = jnp.concatenate([halo, x_ref[...].astype(F32)], axis=0)
    sh = [_shift_rows(xs, DN_CONV - 1 - t)[hr:] for t in range(DN_CONV)]
    pre = sh[0] * w_ref[0:1, :]
    for t in range(1, DN_CONV):
        pre = pre + sh[t] * w_ref[t:t + 1, :]
    return pre, sh


def _dn_gates(sm, alog, dtb):
    lane = lax.broadcasted_iota(jnp.int32, sm.shape, 1)
    return jnp.where(lane < DN_H, _sigmoid(sm), -jnp.exp(alog) * _softplus(sm + dtb))


def _dn_pre_fwd(proj, conv_w, alog_l, dtb_l, name):
    S = proj.shape[0]
    scale = DN_HD ** -0.5

    def body(i, x_ref, sm_ref, p_ref, w_ref, al_ref, db_ref, q_ref, k_ref, v_ref, bg_ref):
        pre, _ = _dn_conv(x_ref, p_ref, w_ref, i)
        a = _silu(pre)
        for h in range(DN_H):
            sl = slice(h * DN_HD, (h + 1) * DN_HD)
            qh, kh = a[:, sl], a[:, MIX + h * DN_HD:MIX + (h + 1) * DN_HD]
            q_ref[:, sl] = qh * (lax.rsqrt(jnp.sum(qh * qh, axis=-1, keepdims=True) + EPS) * scale)
            k_ref[:, sl] = kh * lax.rsqrt(jnp.sum(kh * kh, axis=-1, keepdims=True) + EPS)
        v_ref[...] = a[:, 2 * MIX:]
        bg_ref[...] = _dn_gates(sm_ref[...].astype(F32), al_ref[...], db_ref[...])

    TB = min(S, 256)
    return _tok_call(body, name, S, TB, [(proj, 3 * MIX, C_QKV // (3 * MIX)), (proj, 128, C_SM // 128)],
                     [conv_w, alog_l, dtb_l], [(MIX, F32), (MIX, F32), (MIX, F32), (128, F32)],
                     prev_in=[(proj, 3 * MIX, C_QKV // (3 * MIX))])


def _dn_pre_bwd1(proj, conv_w, alog_l, dtb_l, dq, dk, dv, dbg, dproj, name):
    S = proj.shape[0]
    scale = DN_HD ** -0.5

    def body(i, x_ref, sm_ref, dq_ref, dk_ref, dv_ref, dbg_ref, p_ref, w_ref, al_ref, db_ref, dpre_ref, dsm_ref,
             dw_ref, dal_ref, ddb_ref):
        pre, sh = _dn_conv(x_ref, p_ref, w_ref, i)
        a = _silu(pre)
        da_parts = []
        for part, (g_ref, sc) in enumerate(((dq_ref, scale), (dk_ref, 1.0))):
            for h in range(DN_H):
                xh = a[:, part * MIX + h * DN_HD:part * MIX + (h + 1) * DN_HD]
                rs = lax.rsqrt(jnp.sum(xh * xh, axis=-1, keepdims=True) + EPS)
                y = xh * rs
                dy = g_ref[:, h * DN_HD:(h + 1) * DN_HD] * sc
                da_parts.append(rs * (dy - y * jnp.sum(dy * y, axis=-1, keepdims=True)))
        da_parts.append(dv_ref[...])
        dpre = jnp.concatenate(da_parts, axis=1) * _dsilu(pre)
        dpre_ref[...] = dpre
        dw = jnp.concatenate([jnp.sum(dpre * sh[t], axis=0, keepdims=True) for t in range(DN_CONV)], axis=0)
        _acc(dw_ref, dw, i)
        sm, al, db, dbg_v = sm_ref[...].astype(F32), al_ref[...], db_ref[...], dbg_ref[...]
        lane = lax.broadcasted_iota(jnp.int32, sm.shape, 1)
        sg = _sigmoid(sm)
        gneg = -jnp.exp(al)
        is_g = (lane >= DN_H) & (lane < 2 * DN_H)
        d_al = jnp.where(is_g, dbg_v * gneg * _sigmoid(sm + db), 0.0)
        dsm_ref[...] = jnp.where(lane < DN_H, dbg_v * sg * (1.0 - sg), d_al).astype(BF16)
        _acc(ddb_ref, jnp.sum(d_al, axis=0, keepdims=True), i)
        _acc(dal_ref, jnp.sum(jnp.where(is_g, dbg_v * gneg * _softplus(sm + db), 0.0), axis=0, keepdims=True), i)

    TB = min(S, 256)
    return _tok_call(body, name, S, TB,
                     [(proj, 3 * MIX, C_QKV // (3 * MIX)), (proj, 128, C_SM // 128), (dq, MIX, 0), (dk, MIX, 0),
                      (dv, MIX, 0), (dbg, 128, 0)], [conv_w, alog_l, dtb_l],
                     [(3 * MIX, F32), (128, BF16, C_SM // 128, dproj)],
                     [((DN_CONV, 3 * MIX), F32), ((1, 128), F32), ((1, 128), F32)],
                     prev_in=[(proj, 3 * MIX, C_QKV // (3 * MIX))])


def _dn_pre_bwd2(dpre, conv_w, dproj, name):
    S = dpre.shape[0]
    TB = min(S, 256)
    nb = S // TB

    def body(i, d_ref, n_ref, w_ref, o_ref):
        halo = jnp.where(i < nb - 1, n_ref[...], 0.0)
        ds = jnp.concatenate([d_ref[...], halo], axis=0)
        out = ds[:TB] * w_ref[DN_CONV - 1:DN_CONV, :]
        for t in range(DN_CONV - 1):
            k = DN_CONV - 1 - t
            out = out + pltpu.roll(ds, TB + 8 - k, 0)[:TB] * w_ref[t:t + 1, :]
        o_ref[...] = out.astype(BF16)

    return _tok_call(body, name, S, TB, [(dpre, 3 * MIX, 0)], [conv_w],
                     [(3 * MIX, BF16, C_QKV // (3 * MIX), dproj)], next_in=[(dpre, 3 * MIX, 0)])[0]


def _dn_decay_terms(bgs, heads):
    C = DN_C
    ri = lax.broadcasted_iota(jnp.int32, (C, C), 0)
    ci = lax.broadcasted_iota(jnp.int32, (C, C), 1)
    tril, eye = ri >= ci, ri == ci
    beta = [b[:, h:h + 1] for b, h in zip(bgs, heads)]
    gcol = _dg_exact_lhs_many(tril, [jnp.broadcast_to(b[:, DN_H + h:DN_H + h + 1], (C, C))
                                     for b, h in zip(bgs, heads)], 1, 0)
    grow = [jnp.sum(jnp.where(eye, g, 0.0), axis=0, keepdims=True) for g in gcol]
    decay = [jnp.exp(jnp.where(tril, g - r, -1e30)) for g, r in zip(gcol, grow)]
    e_gc = [jnp.exp(g[:, 0:1]) for g in gcol]
    e_kd = [jnp.exp(g[C - 1:C, 0:1] - g[:, 0:1]) for g in gcol]
    cdec = [jnp.exp(g[C - 1:C, 0:1]) for g in gcol]
    return beta, decay, e_gc, e_kd, cdec


def _dn_nb(S):
    return 4 if S % (4 * DN_C) == 0 else 1


def _dn_prep_fwd(q, k, v, bg, name):
    S = q.shape[0]
    C, NB = DN_C, _dn_nb(S)
    TB = NB * C

    def kern(q_ref, k_ref, v_ref, bg_ref, t_ref, uw_ref, at_ref, qd_ref, kd_ref, dec_ref):
        lane = lax.broadcasted_iota(jnp.int32, (C, 128), 1)
        ri = lax.broadcasted_iota(jnp.int32, (C, C), 0)
        ci = lax.broadcasted_iota(jnp.int32, (C, C), 1)
        tril, eye = ri >= ci, ri == ci
        chains = [(cb, h) for cb in range(NB) for h in range(DN_H)]
        rows = lambda cb: slice(cb * C, (cb + 1) * C)
        head = lambda h: slice(h * DN_HD, (h + 1) * DN_HD)
        beta, decay, e_gc, e_kd, cdec = _dn_decay_terms([bg_ref[rows(cb), :] for cb, _ in chains],
                                                        [h for _, h in chains])
        qs = [q_ref[rows(cb), head(h)] for cb, h in chains]
        ks = [k_ref[rows(cb), head(h)] for cb, h in chains]
        kb = [kh * b for kh, b in zip(ks, beta)]
        x = [-jnp.where(ri > ci, _mm_nt(a, kh) * d, 0.0) for a, kh, d in zip(kb, ks, decay)]
        tm = [jnp.where(eye, 1.0, 0.0) + xi for xi in x]
        p = x
        p = _dg3_many(p, p, 1, 0)
        for it in range(5):
            if it == 4:
                tm = [t + tp for t, tp in zip(tm, _dg3_many(tm, p, 1, 0))]
                break
            both = _dg3_many([jnp.concatenate([t, pp], axis=0) for t, pp in zip(tm, p)], p, 1, 0)
            tm = [t + b[:C] for t, b in zip(tm, both)]
            p = [b[C:] for b in both]
        rhs = [jnp.concatenate([v_ref[rows(cb), head(h)] * b, a * e], axis=1)
               for (cb, h), b, a, e in zip(chains, beta, kb, e_gc)]
        sol = _dg3_many(tm, rhs, 1, 0)
        attn = [_mm_nt(qh, kh) * d for qh, kh, d in zip(qs, ks, decay)]
        for n_, (cb, h) in enumerate(chains):
            rs, sl, hc = rows(cb), head(h), slice(h * C, (h + 1) * C)
            t_ref[rs, hc] = tm[n_]
            uw_ref[rs, sl] = sol[n_][:, :DN_HD]
            uw_ref[rs, MIX + h * DN_HD:MIX + (h + 1) * DN_HD] = sol[n_][:, DN_HD:]
            at_ref[rs, hc] = attn[n_]
            qd_ref[rs, sl] = (qs[n_] * e_gc[n_]).astype(BF16)
            kd_ref[rs, sl] = (ks[n_] * e_kd[n_]).astype(BF16)
        for cb in range(NB):
            dec = jnp.zeros((C, 128), F32)
            for h in range(DN_H):
                dec = dec + jnp.where(lane == h, cdec[cb * DN_H + h], 0.0)
            dec_ref[rows(cb), :] = dec

    tok = lambda w: pl.BlockSpec((TB, w), lambda i: (i, 0))
    return pl.pallas_call(
        kern, name=name, grid=(S // TB,), in_specs=[tok(MIX), tok(MIX), tok(MIX), tok(128)],
        out_specs=[tok(DN_H * C), tok(2 * MIX), tok(DN_H * C), tok(MIX), tok(MIX), tok(128)],
        out_shape=[jax.ShapeDtypeStruct((S, DN_H * C), F32), jax.ShapeDtypeStruct((S, 2 * MIX), F32),
                   jax.ShapeDtypeStruct((S, DN_H * C), F32), jax.ShapeDtypeStruct((S, MIX), BF16),
                   jax.ShapeDtypeStruct((S, MIX), BF16), jax.ShapeDtypeStruct((S, 128), F32)],
        compiler_params=_cparams(("parallel",)),
    )(q, k, v, bg)


def _dn_scan_fwd(uw, at, qd, kd, dec, name):
    S = uw.shape[0]
    C, NB = DN_C, _dn_nb(S)
    TB = NB * C
    SR = DN_H * DN_HD

    def kern(uw_ref, at_ref, qd_ref, kd_ref, dec_ref, o_ref, vn_ref, st_ref, state):
        @pl.when(pl.program_id(0) == 0)
        def _():
            state[...] = jnp.zeros_like(state)

        for cb in range(NB):
            rs = slice(cb * C, (cb + 1) * C)
            hs = range(DN_H)
            sls = [slice(h * DN_HD, (h + 1) * DN_HD) for h in hs]
            s_in = [state[sl, :] for sl in sls]
            ws = [_mm(uw_ref[rs, MIX + h * DN_HD:MIX + (h + 1) * DN_HD], s_in[h]) for h in hs]
            os_ = [_mm(qd_ref[rs, sls[h]], s_in[h]) for h in hs]
            vnew = [uw_ref[rs, sls[h]] - ws[h] for h in hs]
            oa = [_mm(at_ref[rs, h * C:(h + 1) * C], vnew[h]) for h in hs]
            kv = [_mm_tn(kd_ref[rs, sls[h]], vnew[h]) for h in hs]
            for h in hs:
                o_ref[rs, sls[h]] = os_[h] + oa[h]
                state[sls[h], :] = s_in[h] * dec_ref[cb * C:cb * C + 1, h:h + 1] + kv[h]
                st_ref[cb * SR + h * DN_HD:cb * SR + (h + 1) * DN_HD, :] = s_in[h]
                vn_ref[rs, sls[h]] = vnew[h]

    tok = lambda w: pl.BlockSpec((TB, w), lambda i: (i, 0))
    return pl.pallas_call(
        kern, name=name, grid=(S // TB,), in_specs=[tok(2 * MIX), tok(DN_H * C), tok(MIX), tok(MIX), tok(128)],
        out_specs=[tok(MIX), tok(MIX), pl.BlockSpec((NB * SR, DN_HD), lambda i: (i, 0))],
        out_shape=[jax.ShapeDtypeStruct((S, MIX), F32), jax.ShapeDtypeStruct((S, MIX), F32),
                   jax.ShapeDtypeStruct((S // C * SR, DN_HD), F32)],
        scratch_shapes=[pltpu.VMEM((SR, DN_HD), F32)],
        compiler_params=_cparams(("arbitrary",)),
    )(uw, at, qd, kd, dec)


def _dn_core_fwd(q, k, v, bg, name):
    tm, uw, at, qd, kd, dec = _dn_prep_fwd(q, k, v, bg, name + "_prep")
    o, vn, st = _dn_scan_fwd(uw, at, qd, kd, dec, name + "_scan")
    return o, dict(tm=tm, uw=uw, at=at, qd=qd, kd=kd, dec=dec, vn=vn, st=st)


def _dn_scan_bwd(sv, do, name):
    S = do.shape[0]
    C, NB = DN_C, _dn_nb(S)
    TB = NB * C
    SR = DN_H * DN_HD
    nb = S // TB

    def kern(do_ref, uw_ref, at_ref, qd_ref, kd_ref, dec_ref, vn_ref, st_ref, dvn_ref, dw_ref, dkd_ref, dc_ref, dstate):
        @pl.when(pl.program_id(0) == 0)
        def _():
            dstate[...] = jnp.zeros_like(dstate)

        lane = lax.broadcasted_iota(jnp.int32, (C, 128), 1)
        for cb in reversed(range(NB)):
            rs = slice(cb * C, (cb + 1) * C)
            dcrow = jnp.zeros((C, 128), F32)
            for h in range(DN_H):
                sl = slice(h * DN_HD, (h + 1) * DN_HD)
                doh, ds_o = do_ref[rs, sl], dstate[sl, :]
                s_in = st_ref[cb * SR + h * DN_HD:cb * SR + (h + 1) * DN_HD, :]
                d_vnew = _mm_tn(at_ref[rs, h * C:(h + 1) * C], doh) + _mm(kd_ref[rs, sl], ds_o)
                dvn_ref[rs, sl] = d_vnew
                dw_ref[rs, sl] = -_mm_nt(d_vnew, s_in)
                dkd_ref[rs, sl] = _mm_nt(vn_ref[rs, sl], ds_o)
                d_c = jnp.sum(jnp.sum(ds_o * s_in, axis=1, keepdims=True), axis=0, keepdims=True)
                dcrow = dcrow + jnp.where(lane == h, d_c, 0.0)
                dstate[sl, :] = (ds_o * dec_ref[cb * C:cb * C + 1, h:h + 1] + _mm_tn(qd_ref[rs, sl], doh)
                                 - _mm_tn(uw_ref[rs, MIX + h * DN_HD:MIX + (h + 1) * DN_HD], d_vnew))
            dc_ref[rs, :] = dcrow

    tok = lambda w: pl.BlockSpec((TB, w), lambda i: (nb - 1 - i, 0))
    return pl.pallas_call(
        kern, name=name, grid=(nb,),
        in_specs=[tok(MIX), tok(2 * MIX), tok(DN_H * C), tok(MIX), tok(MIX), tok(128), tok(MIX),
                  pl.BlockSpec((NB * SR, DN_HD), lambda i: (nb - 1 - i, 0))],
        out_specs=[tok(MIX), tok(MIX), tok(MIX), tok(128)],
        out_shape=[jax.ShapeDtypeStruct((S, MIX), F32)] * 3 + [jax.ShapeDtypeStruct((S, 128), F32)],
        scratch_shapes=[pltpu.VMEM((SR, DN_HD), F32)],
        compiler_params=_cparams(("arbitrary",)),
    )(do, sv["uw"], sv["at"], sv["qd"], sv["kd"], sv["dec"], sv["vn"], sv["st"])


def _dn_chunk_bwd(q, k, v, bg, sv, do, dvn, dw, dkd, dc, name):
    S = q.shape[0]
    C, NB = DN_C, _dn_nb(S)
    TB = NB * C
    SR = DN_H * DN_HD

    def kern(q_ref, k_ref, v_ref, bg_ref, t_ref, uw_ref, vn_ref, st_ref, do_ref, dvn_ref, dw_ref, dkd_ref, dc_ref,
             dq_ref, dk_ref, dv_ref, dbg_ref):
        lane = lax.broadcasted_iota(jnp.int32, (C, 128), 1)
        ri = lax.broadcasted_iota(jnp.int32, (C, C), 0)
        ci = lax.broadcasted_iota(jnp.int32, (C, C), 1)
        tril, eye, last = ri >= ci, ri == ci, ri[:, 0:1] == C - 1
        chains = [(cb, h) for cb in range(NB) for h in range(DN_H)]
        each = lambda f, *ls: [f(*a) for a in zip(*ls)]
        rsum = lambda t: jnp.sum(t, axis=-1, keepdims=True)
        rows = lambda cb: slice(cb * C, (cb + 1) * C)
        head = lambda h: slice(h * DN_HD, (h + 1) * DN_HD)
        tok = lambda ref: [ref[rows(cb), head(h)] for cb, h in chains]
        beta, decay, e_gc, e_kd, cdec = _dn_decay_terms([bg_ref[rows(cb), :] for cb, _ in chains],
                                                        [h for _, h in chains])
        qs, ks, vs, dos, vnew, d_kd = tok(q_ref), tok(k_ref), tok(v_ref), tok(do_ref), tok(vn_ref), tok(dkd_ref)
        s_in = [st_ref[cb * SR + h * DN_HD:cb * SR + (h + 1) * DN_HD, :] for cb, h in chains]
        d_c = [dc_ref[cb * C:cb * C + 1, h:h + 1] for cb, h in chains]
        kb = each(lambda a, b: a * b, ks, beta)
        kk = each(_mm_nt, kb, ks)
        attn = each(lambda a, b, d: _mm_nt(a, b) * d, qs, ks, decay)
        d_qd = each(_mm_nt, dos, s_in)
        d_attn = each(_mm_nt, dos, vnew)
        d_sol = [jnp.concatenate([dvn_ref[rows(cb), head(h)], dw_ref[rows(cb), head(h)]], axis=1) for cb, h in chains]
        sol = [jnp.concatenate([uw_ref[rows(cb), head(h)], uw_ref[rows(cb), MIX + h * DN_HD:MIX + (h + 1) * DN_HD]],
                               axis=1) for cb, h in chains]
        d_rhs = _dg3_many([t_ref[rows(cb), h * C:(h + 1) * C] for cb, h in chains], d_sol, 0, 0)
        d_a = _dg3_many(d_rhs, sol, 1, 1)
        d_kk = each(lambda a, d: jnp.where(ri > ci, -a, 0.0) * d, d_a, decay)
        d_qk = each(lambda a, d: a * d, d_attn, decay)
        dm = each(lambda a, b, c_, d: a * b + c_ * d, d_kk, kk, d_attn, attn)
        d_vb = [t[:, :DN_HD] for t in d_rhs]
        dz = [t[:, DN_HD:] for t in d_rhs]
        d_kb = each(lambda z, e, a, kh: z * e + _mm(a, kh), dz, e_gc, d_kk, ks)
        d_k = each(lambda a, b, c_, q: _mm_tn(a, b) + _mm_tn(c_, q), d_kk, kb, d_qk, qs)
        d_q = each(lambda a, kh, b, e: _mm(a, kh) + b * e, d_qk, ks, d_qd, e_gc)
        t_kd = each(lambda a, kh, e: rsum(a * kh * e), d_kd, ks, e_kd)
        d_gl = each(lambda t, c_, cd: jnp.sum(t, axis=0, keepdims=True) + c_ * cd, t_kd, d_c, cdec)
        d_gc = each(lambda z, a, e, m, b, q, t, gl:
                    rsum(z * a) * e + rsum(m) - rsum(jnp.where(eye, jnp.sum(m, axis=0, keepdims=True), 0.0))
                    + rsum(b * q) * e - t + jnp.where(last, gl, 0.0),
                    dz, kb, e_gc, dm, d_qd, qs, t_kd, d_gl)
        d_g = _dg_exact_lhs_many(ri <= ci, [jnp.broadcast_to(t, (C, 128)) for t in d_gc], 1, 0)
        d_beta = each(lambda a, v_, b, kh: rsum(a * v_) + rsum(b * kh), d_vb, vs, d_kb, ks)
        for n_, (cb, h) in enumerate(chains):
            dq_ref[rows(cb), head(h)] = d_q[n_]
            dk_ref[rows(cb), head(h)] = d_k[n_] + d_kd[n_] * e_kd[n_] + d_kb[n_] * beta[n_]
            dv_ref[rows(cb), head(h)] = d_vb[n_] * beta[n_]
        for cb in range(NB):
            dbg = jnp.zeros((C, 128), F32)
            for h in range(DN_H):
                n_ = cb * DN_H + h
                dbg = dbg + jnp.where(lane == h, d_beta[n_], 0.0) + jnp.where(lane == DN_H + h, d_g[n_], 0.0)
            dbg_ref[rows(cb), :] = dbg

    tok = lambda w: pl.BlockSpec((TB, w), lambda i: (i, 0))
    return pl.pallas_call(
        kern, name=name, grid=(S // TB,),
        in_specs=[tok(MIX), tok(MIX), tok(MIX), tok(128), tok(DN_H * C), tok(2 * MIX), tok(MIX),
                  pl.BlockSpec((NB * SR, DN_HD), lambda i: (i, 0)), tok(MIX), tok(MIX), tok(MIX), tok(MIX), tok(128)],
        out_specs=[tok(MIX), tok(MIX), tok(MIX), tok(128)],
        out_shape=[jax.ShapeDtypeStruct((S, MIX), F32)] * 3 + [jax.ShapeDtypeStruct((S, 128), F32)],
        compiler_params=_cparams(("parallel",)),
    )(q, k, v, bg, sv["tm"], sv["uw"], sv["vn"], sv["st"], do, dvn, dw, dkd, dc)


def _dn_core_bwd(q, k, v, bg, sv, do, name):
    dvn, dw, dkd, dc = _dn_scan_bwd(sv, do, name + "_scan")
    return _dn_chunk_bwd(q, k, v, bg, sv, do, dvn, dw, dkd, dc, name + "_chunk")


def _dn_post_fwd(o, proj, ng, name):
    S = o.shape[0]

    def body(i, o_ref, z_ref, g_ref, out_ref):
        gv = g_ref[...]
        for h in range(DN_H):
            sl = slice(h * DN_HD, (h + 1) * DN_HD)
            oh = o_ref[:, sl]
            r = lax.rsqrt(jnp.mean(oh * oh, axis=-1, keepdims=True) + EPS)
            out_ref[:, sl] = (oh * r * gv * _silu(z_ref[:, sl].astype(F32))).astype(BF16)

    return _tok_call(body, name, S, min(S, 512), [(o, MIX, 0), (proj, MIX, C_ZC // MIX)], [ng], [(MIX, BF16)])[0]


def _dn_post_bwd(o, proj, ng, dout, dproj, name):
    S = o.shape[0]

    def body(i, o_ref, z_ref, do_ref, g_ref, dov_ref, dz_ref, dg_ref):
        gv = g_ref[...]
        dg = jnp.zeros((1, DN_HD), F32)
        for h in range(DN_H):
            sl = slice(h * DN_HD, (h + 1) * DN_HD)
            oh, zh, dh = o_ref[:, sl], z_ref[:, sl].astype(F32), do_ref[:, sl].astype(F32)
            r = lax.rsqrt(jnp.mean(oh * oh, axis=-1, keepdims=True) + EPS)
            dz_ref[:, sl] = (dh * oh * r * gv * _dsilu(zh)).astype(BF16)
            dx, dgh = _rms_bwd_vals(oh, gv, dh * _silu(zh))
            dov_ref[:, sl] = dx
            dg = dg + dgh
        _acc(dg_ref, dg, i)

    return _tok_call(body, name, S, min(S, 512), [(o, MIX, 0), (proj, MIX, C_ZC // MIX), (dout, MIX, 0)], [ng],
                     [(MIX, F32), (MIX, BF16, C_ZC // MIX, dproj)], [((1, DN_HD), F32)])


def _layer_params(w, big, l):
    lane = jnp.arange(128)
    is_g = (lane >= DN_H) & (lane < 2 * DN_H)
    spread = lambda t: jnp.where(is_g, jnp.tile(t, 128 // DN_H), 0.0).reshape(1, 128)
    tril = jnp.tril(jnp.ones((SGU_T, SGU_T), bool))
    return dict(
        win=big["w_in"], rest=big["rest"], conv=w["dn_conv_w"][l], attn_norm=w["attn_norm"][l].reshape(1, -1), ffn_norm=w["ffn_norm"][l].reshape(1, -1),
        lg=w["sgu_ln_g"][l].reshape(1, -1), lb=w["sgu_ln_b"][l].reshape(1, -1),
        wc=jnp.where(tril, w["sgu_w"][l], 0.0).reshape(SGU_G * SGU_T, SGU_T), bst=w["sgu_b"][l].T,
        sinks=w["attn_sinks"][l].reshape(1, -1), alog=spread(w["dn_a_log"][l]), dtb=spread(w["dn_dt_bias"][l]),
        ng=w["dn_norm"][l].reshape(1, -1))


def _layer_fwd(x, p, cos, sin, l):
    n = lambda s: f"l{l}_{s}"
    h = _rms_fwd(x, p["attn_norm"], n("rms1"))
    proj = _matmul(h, p["win"], out_dtype=BF16, name=n("mm_in"))
    out_a = _sgu_fwd(proj, p["lg"], p["lb"], p["wc"], p["bst"], n("sgu_fwd"))
    qr, kr, vr = _rope_fwd(proj, cos, sin, n("rope_fwd"))
    out_b = _swa_fwd(qr, kr, vr, p["sinks"], n("swa_fwd"))
    q, k, v, bg = _dn_pre_fwd(proj, p["conv"], p["alog"], p["dtb"], n("dn_pre_fwd"))
    o, dn = _dn_core_fwd(q, k, v, bg, n("dn_core_fwd"))
    out_c = _dn_post_fwd(o, proj, p["ng"], n("dn_post_fwd"))
    outs = (out_a, out_b, out_c)
    rest = p.pop("rest")(out_c)
    p.update(wb=rest["w_branch"], wout=rest["w_out"], wgu=rest["w_gate_up"], wdown=rest["w_down"])
    bds = [_matmul(outs[j], p["wb"][j], out_dtype=BF16, name=n(f"mm_branch{j}")) for j in range(3)]
    merged = _merge_fwd(proj, bds, n("merge_fwd"))
    x1 = _matmul(merged, p["wout"], add=x, name=n("mm_out"))
    h2 = _rms_fwd(x1, p["ffn_norm"], n("rms2"))
    gu = _matmul(h2, p["wgu"], out_dtype=BF16, name=n("mm_gu"))
    act = _swiglu_fwd(gu, n("swiglu_fwd"))
    x2 = _matmul(act, p["wdown"], add=x1, name=n("mm_down"))
    saved = dict(x=x, h=h, proj=proj, outs=outs, qr=qr, kr=kr, vr=vr, q=q, k=k, v=v, bg=bg, o=o, dn=dn, bds=bds,
                 merged=merged, x1=x1, h2=h2, gu=gu, act=act)
    return x2, saved


def _layer_bwd(dx2, s, p, cos, sin, l, early=None):
    n = lambda t: f"l{l}_{t}"
    proj = s["proj"]
    g = {}
    g["w_down"] = _matmul(s["act"], dx2, ta=True, out_dtype=BF16, name=n("wg_down"))
    dact = _matmul(dx2, p["wdown"], tb=True, out_dtype=BF16, name=n("dg_down"))
    dgu = _swiglu_bwd(s["gu"], dact, n("swiglu_bwd"))
    g["w_gate_up"] = _matmul(s["h2"], dgu, ta=True, out_dtype=BF16, name=n("wg_gu"))
    dh2 = _matmul(dgu, p["wgu"], tb=True, name=n("dg_gu"))
    dx1, g["ffn_norm"] = _rms_bwd_add(s["x1"], p["ffn_norm"], dh2, dx2, n("rms2_bwd"))
    g["w_out"] = _matmul(s["merged"], dx1, ta=True, out_dtype=BF16, name=n("wg_out"))
    dm = _matmul(dx1, p["wout"], tb=True, name=n("dg_out"))
    dbd0, dbd1, dbd2, dproj = _merge_bwd(proj, s["bds"], dm, n("merge_bwd"))
    dbds = (dbd0, dbd1, dbd2)
    g["w_branch"] = jnp.stack([_matmul(s["outs"][j], dbds[j], ta=True, out_dtype=BF16, name=n(f"wg_branch{j}"))
                               for j in range(3)])
    douts = [_matmul(dbds[j], p["wb"][j], tb=True, name=n(f"dg_branch{j}")) for j in range(3)]
    lg = p["lg"]
    if early is not None:
        token = early({k: g.pop(k) for k in ("w_down", "w_gate_up", "w_out", "w_branch")})
        lg = lg if token is None else lg + token[0, 0]
    dproj, g["sgu_ln_g"], g["sgu_ln_b"], dwc, dbs = _sgu_bwd(proj, lg, p["lb"], p["wc"], p["bst"], douts[0], dproj,
                                                             n("sgu_bwd"))
    g["sgu_w"] = dwc.reshape(SGU_G, SGU_T, SGU_T)
    g["sgu_b"] = dbs.T
    dqr, dkr, dvr, dsink = _swa_bwd(s["qr"], s["kr"], s["vr"], p["sinks"], douts[1], n("swa_bwd"))
    g["attn_sinks"] = dsink[0, :SWA_H]
    dproj = _rope_bwd(dqr, dkr, dvr, cos, sin, dproj, n("rope_bwd"))
    do, dproj, dng = _dn_post_bwd(s["o"], proj, p["ng"], douts[2], dproj, n("dn_post_bwd"))
    g["dn_norm"] = dng[0]
    dq, dk, dv, dbg = _dn_core_bwd(s["q"], s["k"], s["v"], s["bg"], s["dn"], do, n("dn_core_bwd"))
    dpre, dproj, g["dn_conv_w"], dal, ddb = _dn_pre_bwd1(proj, p["conv"], p["alog"], p["dtb"], dq, dk, dv, dbg, dproj,
                                                         n("dn_pre_bwd1"))
    g["dn_a_log"] = dal[0, DN_H:2 * DN_H]
    g["dn_dt_bias"] = ddb[0, DN_H:2 * DN_H]
    dproj = _dn_pre_bwd2(dpre, p["conv"], dproj, n("dn_pre_bwd2"))
    g["w_in"] = _matmul(s["h"], dproj, ta=True, out_dtype=BF16, name=n("wg_in"))
    dh = _matmul(dproj, p["win"], tb=True, name=n("dg_in"))
    dx, g["attn_norm"] = _rms_bwd_add(s["x"], p["attn_norm"], dh, dx1, n("rms1_bwd"))
    g["attn_norm"], g["ffn_norm"] = g["attn_norm"][0], g["ffn_norm"][0]
    g["sgu_ln_g"], g["sgu_ln_b"] = g["sgu_ln_g"][0], g["sgu_ln_b"][0]
    return dx, g


def _local_step(x, positions, target, w, big_of_layer, on_grads):
    cos, sin = _rope_tables(positions)
    params, saves, xs = [], [], x
    for l in range(DEPTH):
        params.append(_layer_params(w, big_of_layer(l, xs), l))
        xs, sv = _layer_fwd(xs, params[l], cos, sin, l)
        saves.append(sv)
    dx, loss_row, dgf = _final_loss(xs, w["final_norm"].reshape(1, -1), target)
    grads = [None] * DEPTH
    for l in reversed(range(DEPTH)):
        early = functools.partial(on_grads, l) if l == 0 else None
        dx, grads[l] = _layer_bwd(dx, saves[l], params[l], cos, sin, l, early)
        token = on_grads(l, {k: grads[l].pop(k) for k in BIG if k in grads[l]})
        if token is not None and l > 0:
            params[l - 1] = dict(params[l - 1], ffn_norm=params[l - 1]["ffn_norm"] + token[0, 0])
    stacked = {k: jnp.stack([grads[l][k] for l in range(DEPTH)]) for k in grads[0]}
    stacked["final_norm"] = dgf[0]
    return loss_row[0, 0], dx, stacked


MESH = pl.DeviceIdType.MESH
HBM_SPEC = pl.BlockSpec(memory_space=pltpu.HBM)
VMEM_SPEC = pl.BlockSpec(memory_space=pltpu.VMEM)
N_CHIPS = 4
FLIPS = tuple((fx, fy, fc) for fx in (0, 1) for fy in (0, 1) for fc in (0, 1))[1:]
BIG = ("w_in", "w_branch", "w_out", "w_gate_up", "w_down")
BIG_SPEC = {
    "w_in": dict(rows=1024, cols=1792, axis=1, keep=1730, down=8),
    "w_branch": dict(rows=1536, cols=256, axis=1, keep=256, down=2),
    "w_out": dict(rows=256, cols=1024, axis=0, keep=1024, down=1),
    "w_gate_up": dict(rows=1024, cols=1408, axis=1, keep=1408, down=8),
    "w_down": dict(rows=704, cols=1024, axis=0, keep=1024, down=4),
}
CONV_ROWS, CONV_COLS = DEPTH * DN_CONV, 3 * MIX // N_CHIPS


def _full_shape(k):
    sp = BIG_SPEC[k]
    return (sp["rows"], N_CHIPS * sp["cols"]) if sp["axis"] == 1 else (N_CHIPS * sp["rows"], sp["cols"])


def _me():
    return lax.axis_index("x"), lax.axis_index("y"), lax.axis_index("c")


def _peer(x, y, c, flip):
    fx, fy, fc = flip
    return (1 - x if fx else x, 1 - y if fy else y, 1 - c if fc else c)


class _Copies:
    def __init__(self, send_sems, recv_sems):
        self.send_sems, self.recv_sems, self.k, self.sent, self.landing = send_sems, recv_sems, 0, [], []

    def _copy(self, k, src, dst, to):
        return pltpu.make_async_remote_copy(src_ref=src, dst_ref=dst, send_sem=self.send_sems.at[k],
                                            recv_sem=self.recv_sems.at[k], device_id=to, device_id_type=MESH)

    def send(self, src, dst, to, lands):
        k = self.k
        self.k += 1
        cp = self._copy(k, src, dst, to)
        cp.start()
        self.sent.append(cp)
        self.landing.append(self._copy(k, lands, lands, to))
        return k

    def wait_landed(self, k):
        self.landing[k].wait_recv()

    def finish(self, landed=()):
        for k, cp in enumerate(self.landing):
            if k not in landed:
                cp.wait_recv()
        for cp in self.sent:
            cp.wait_send()


def _place_shard(shard, k, chip, layer, name):
    sp = BIG_SPEC[k]
    rows, cols, keep = sp["rows"], sp["cols"], sp["keep"]
    tr = _pick(rows, (256, 64))
    nb = rows // tr
    if sp["axis"] == 1:
        out_spec = pl.BlockSpec((tr, cols), lambda i, ch: (i, ch[0]))
    else:
        out_spec = pl.BlockSpec((tr, cols), lambda i, ch: (ch[0] * nb + i, 0))

    def kern(ch_ref, x_ref, o_ref):
        v = x_ref[0].astype(BF16)
        if keep == cols:
            o_ref[...] = v
        else:
            o_ref[:, :keep] = v
            o_ref[:, keep:] = jnp.zeros((tr, cols - keep), BF16)

    return pl.pallas_call(
        kern, name=name, out_shape=jax.ShapeDtypeStruct(_full_shape(k), BF16),
        grid_spec=pltpu.PrefetchScalarGridSpec(
            num_scalar_prefetch=1, grid=(nb,),
            in_specs=[pl.BlockSpec((1, tr, keep), lambda i, ch: (layer, i, 0))], out_specs=out_spec),
        compiler_params=_cparams(("parallel",)),
    )(chip, shard)


def _half_block(ref, k, s, half):
    sp = BIG_SPEC[k]
    hr = sp["rows"] // 2
    if sp["axis"] == 1:
        return ref.at[pl.ds(pl.multiple_of(half * hr, 16), hr), pl.ds(pl.multiple_of(s * sp["cols"], 128), sp["cols"])]
    return ref.at[pl.ds(pl.multiple_of(s * sp["rows"] + half * hr, 16), hr), :]


def _other_chips(x, y):
    return [(1 - x, y), (x, 1 - y), (1 - x, 1 - y)]


ALL_BIG = BIG


def _present(d):
    return tuple(k for k in ALL_BIG if k in d)


def _gather_layer(placed, conv):
    BIG = _present(placed)
    n = len(BIG)
    n_sem = 6 * n + 3

    def body(*refs):
        conv_ref = refs[n]
        out = dict(zip(BIG, refs[n + 1:2 * n + 1]))
        conv_out, send_sems, recv_sems, local_sem = refs[2 * n + 1:]
        x, y, c = _me()
        me = 2 * x + y
        chips = _other_chips(x, y)
        net = _Copies(send_sems, recv_sems)

        def conv_block(s):
            return conv_out.at[:, pl.ds(pl.multiple_of(s * CONV_COLS, 128), CONV_COLS)]

        local = pltpu.make_async_copy(conv_ref, conv_block(me), local_sem)
        local.start()
        first = {}
        for k in BIG:
            for j, (px, py) in enumerate(chips):
                first[k, j] = net.send(_half_block(out[k], k, me, c), _half_block(out[k], k, me, c), (px, py, c),
                                       _half_block(out[k], k, 2 * px + py, c))
        for px, py in chips:
            net.send(conv_ref, conv_block(me), (px, py, c), conv_block(2 * px + py))
        for k in BIG:
            for j, (px, py) in enumerate(chips):
                net.wait_landed(first[k, j])
                net.send(_half_block(out[k], k, 2 * px + py, c), _half_block(out[k], k, 2 * px + py, c), (x, y, 1 - c),
                         _half_block(out[k], k, 2 * px + py, 1 - c))
        net.finish(landed=set(first.values()))
        local.wait()

    out_shape = [jax.ShapeDtypeStruct(_full_shape(k), BF16) for k in BIG]
    out_shape.append(jax.ShapeDtypeStruct((CONV_ROWS, N_CHIPS * CONV_COLS), F32))
    outs = pl.pallas_call(
        body, name="gather_layer", out_shape=out_shape, in_specs=[HBM_SPEC] * (n + 1), out_specs=[HBM_SPEC] * (n + 1),
        input_output_aliases={i: i for i in range(n)},
        scratch_shapes=[pltpu.SemaphoreType.DMA((n_sem,)), pltpu.SemaphoreType.DMA((n_sem,)), pltpu.SemaphoreType.DMA],
    )(*[placed[k] for k in BIG], conv)
    return dict(zip(BIG, outs[:n])), outs[n]


SEM_SPEC = pl.BlockSpec(memory_space=pltpu.SEMAPHORE)


def _behind_copies(arrs, send_sems, recv_sems):
    x, y, c = _me()
    copies = []
    for i, k in enumerate(_present(arrs)):
        for j, (px, py) in enumerate(_other_chips(x, y)):
            copies.append(pltpu.make_async_remote_copy(
                src_ref=_half_block(arrs[k], k, 2 * x + y, c), dst_ref=_half_block(arrs[k], k, 2 * x + y, c),
                send_sem=send_sems.at[3 * i + j], recv_sem=recv_sems.at[3 * i + j], device_id=(px, py, c),
                device_id_type=MESH))
    return copies


def _gather_start(placed, after, tag):
    BIG = _present(placed)
    n = len(BIG)
    N_BEHIND = 3 * n

    def body(*refs):
        arrs = dict(zip(BIG, refs[n + 3:2 * n + 3]))
        send_sems, recv_sems = refs[n + 1], refs[n + 2]
        for cp in _behind_copies(arrs, send_sems, recv_sems):
            cp.start()
        refs[2 * n + 3][...] = jnp.zeros((8, 128), F32)

    outs = pl.pallas_call(
        body, name="gather_start" + tag,
        out_shape=(pltpu.SemaphoreType.DMA((N_BEHIND,)), pltpu.SemaphoreType.DMA((N_BEHIND,)),
                   *[pltpu.HBM(_full_shape(k), BF16) for k in BIG], jax.ShapeDtypeStruct((8, 128), F32)),
        in_specs=[HBM_SPEC] * n + [pl.BlockSpec(memory_space=pl.ANY)],
        out_specs=(SEM_SPEC, SEM_SPEC, *[HBM_SPEC] * n, VMEM_SPEC),
        input_output_aliases={i: i + 2 for i in range(n)},
        compiler_params=pltpu.CompilerParams(has_side_effects=pltpu.SideEffectType.DATAFLOW_SIDE_EFFECTING),
    )(*[pltpu.with_memory_space_constraint(placed[k], pltpu.HBM) for k in BIG], after)
    return outs[0], outs[1], dict(zip(BIG, outs[2:n + 2])), outs[n + 2]


def _gather_wait(send_sems, recv_sems, inflight, after, tag):
    BIG = _present(inflight)
    n = len(BIG)

    def body(*refs):
        arrs = dict(zip(BIG, refs[:n]))
        for cp in _behind_copies(arrs, refs[n], refs[n + 1]):
            cp.wait_send()
            cp.wait_recv()

    outs = pl.pallas_call(
        body, name="gather_wait" + tag, out_shape=tuple(pltpu.HBM(_full_shape(k), BF16) for k in BIG),
        in_specs=[HBM_SPEC] * n + [SEM_SPEC, SEM_SPEC, pl.BlockSpec(memory_space=pl.ANY)], out_specs=(HBM_SPEC,) * n,
        input_output_aliases={i: i for i in range(n)},
        compiler_params=pltpu.CompilerParams(has_side_effects=pltpu.SideEffectType.DATAFLOW_SIDE_EFFECTING),
    )(*[inflight[k] for k in BIG], send_sems, recv_sems, after)
    return dict(zip(BIG, outs))


def _gather_finish(arrs, tag):
    BIG = _present(arrs)
    n = len(BIG)
    N_BEHIND = 3 * n

    def body(*refs):
        out = dict(zip(BIG, refs[n:2 * n]))
        send_sems, recv_sems = refs[2 * n:]
        x, y, c = _me()
        net = _Copies(send_sems, recv_sems)
        for k in BIG:
            for px, py in _other_chips(x, y):
                net.send(_half_block(out[k], k, 2 * px + py, c), _half_block(out[k], k, 2 * px + py, c), (x, y, 1 - c),
                         _half_block(out[k], k, 2 * px + py, 1 - c))
        net.finish()

    outs = pl.pallas_call(
        body, name="gather_finish" + tag, out_shape=[jax.ShapeDtypeStruct(_full_shape(k), BF16) for k in BIG],
        in_specs=[HBM_SPEC] * n, out_specs=[HBM_SPEC] * n, input_output_aliases={i: i for i in range(n)},
        scratch_shapes=[pltpu.SemaphoreType.DMA((N_BEHIND,)), pltpu.SemaphoreType.DMA((N_BEHIND,))],
    )(*[arrs[k] for k in BIG])
    return dict(zip(BIG, outs))


def _row_chunks(ref, rows, n):
    step = rows // n
    return [ref.at[pl.ds(i * step, step), :] for i in range(n)]


def _half_pieces(ref, k, half):
    sp = BIG_SPEC[k]
    hr = sp["rows"] // 2
    if sp["axis"] == 1:
        return [ref.at[pl.ds(pl.multiple_of(half * hr, 16), hr), :]]
    return [ref.at[pl.ds(pl.multiple_of(s * sp["rows"] + half * hr, 16), hr), :] for s in range(N_CHIPS)]


def _half_shape(k):
    rows, cols = _full_shape(k)
    return rows // 2, cols


def _stacked_pieces(ref, k):
    sp = BIG_SPEC[k]
    hr = sp["rows"] // 2
    return [ref] if sp["axis"] == 1 else [ref.at[pl.ds(s * hr, hr), :] for s in range(N_CHIPS)]


def _chip_part(ref, k, s):
    sp = BIG_SPEC[k]
    hr = sp["rows"] // 2
    if sp["axis"] == 1:
        return ref.at[:, pl.ds(pl.multiple_of(s * sp["cols"], 128), sp["cols"])]
    return ref.at[pl.ds(pl.multiple_of(s * hr, 16), hr), :]


def _halves_to_sibling(grads, name):
    BIG = _present(grads)
    n = len(BIG)
    chunks = {k: max(BIG_SPEC[k]["down"] // 2, 1) if BIG_SPEC[k]["axis"] == 1 else 1 for k in BIG}
    n_sem = sum(chunks[k] if BIG_SPEC[k]["axis"] == 1 else N_CHIPS for k in BIG)

    def body(*refs):
        g = dict(zip(BIG, refs[:n]))
        out = dict(zip(BIG, refs[n:2 * n]))
        send_sems, recv_sems = refs[2 * n:]
        x, y, c = _me()
        net = _Copies(send_sems, recv_sems)
        for k in BIG:
            hr = BIG_SPEC[k]["rows"] // 2
            for src, dst in zip(_half_pieces(g[k], k, 1 - c), _stacked_pieces(out[k], k)):
                for s, d in zip(_row_chunks(src, hr, chunks[k]), _row_chunks(dst, hr, chunks[k])):
                    net.send(s, d, (x, y, 1 - c), d)
        net.finish()

    outs = pl.pallas_call(
        body, name=name, out_shape=[jax.ShapeDtypeStruct(_half_shape(k), BF16) for k in BIG],
        in_specs=[HBM_SPEC] * n, out_specs=[HBM_SPEC] * n,
        scratch_shapes=[pltpu.SemaphoreType.DMA((n_sem,)), pltpu.SemaphoreType.DMA((n_sem,))],
    )(*[grads[k] for k in BIG])
    return dict(zip(BIG, outs))


def _add_half(g, other, k, core, name):
    sp = BIG_SPEC[k]
    hr, cols = sp["rows"] // 2, _full_shape(k)[1]
    tr = _pick(hr, (256, 352, 128))
    nb = hr // tr
    if sp["axis"] == 1:
        grid = (nb,)
        g_spec = pl.BlockSpec((tr, cols), lambda i, c: (c[0] * nb + i, 0))
        h_spec = pl.BlockSpec((tr, cols), lambda i, c: (i, 0))
    else:
        grid = (N_CHIPS, nb)
        g_spec = pl.BlockSpec((tr, cols), lambda s, i, c: ((2 * s + c[0]) * nb + i, 0))
        h_spec = pl.BlockSpec((tr, cols), lambda s, i, c: (s * nb + i, 0))

    def kern(c_ref, a_ref, b_ref, o_ref):
        o_ref[...] = (a_ref[...].astype(F32) + b_ref[...].astype(F32)).astype(BF16)

    return pl.pallas_call(
        kern, name=name, out_shape=jax.ShapeDtypeStruct(_half_shape(k), BF16),
        grid_spec=pltpu.PrefetchScalarGridSpec(num_scalar_prefetch=1, grid=grid, in_specs=[g_spec, h_spec],
                                               out_specs=h_spec),
        compiler_params=_cparams(("parallel",) * len(grid)),
    )(core, g, other)


def _part_shape(k):
    return N_CHIPS - 1, BIG_SPEC[k]["rows"] // 2, BIG_SPEC[k]["cols"]


def _scatter_chip_sums(sums, name):
    BIG = _present(sums)
    n = len(BIG)
    N_BEHIND = 3 * n

    def body(*refs):
        src = dict(zip(BIG, refs[:n]))
        out = dict(zip(BIG, refs[n:2 * n]))
        send_sems, recv_sems = refs[2 * n:]
        x, y, c = _me()
        net = _Copies(send_sems, recv_sems)
        for k in BIG:
            for j, (px, py) in enumerate(_other_chips(x, y)):
                net.send(_chip_part(src[k], k, 2 * px + py), out[k].at[j], (px, py, c), out[k].at[j])
        net.finish()

    outs = pl.pallas_call(
        body, name=name, out_shape=[jax.ShapeDtypeStruct(_part_shape(k), BF16) for k in BIG],
        in_specs=[HBM_SPEC] * n, out_specs=[HBM_SPEC] * n,
        scratch_shapes=[pltpu.SemaphoreType.DMA((N_BEHIND,)), pltpu.SemaphoreType.DMA((N_BEHIND,))],
    )(*[sums[k] for k in BIG])
    return dict(zip(BIG, outs))


def _scatter_copies(sums, parts, send_sems, recv_sems):
    x, y, c = _me()
    copies = []
    for i, k in enumerate(_present(sums)):
        for j, (px, py) in enumerate(_other_chips(x, y)):
            copies.append(pltpu.make_async_remote_copy(
                src_ref=_chip_part(sums[k], k, 2 * px + py), dst_ref=parts[k].at[j], send_sem=send_sems.at[3 * i + j],
                recv_sem=recv_sems.at[3 * i + j], device_id=(px, py, c), device_id_type=MESH))
    return copies


def _scatter_start(sums, tag):
    BIG = _present(sums)
    n = len(BIG)
    N_BEHIND = 3 * n
    lands = [pltpu.with_memory_space_constraint(lax.empty(_part_shape(k), BF16), pltpu.HBM) for k in BIG]

    def body(*refs):
        outs = refs[2 * n + 2:4 * n + 2]
        for cp in _scatter_copies(dict(zip(BIG, outs[:n])), dict(zip(BIG, outs[n:])), refs[2 * n], refs[2 * n + 1]):
            cp.start()
        refs[4 * n + 2][...] = jnp.zeros((8, 128), F32)

    outs = pl.pallas_call(
        body, name="scatter_start" + tag,
        out_shape=(pltpu.SemaphoreType.DMA((N_BEHIND,)), pltpu.SemaphoreType.DMA((N_BEHIND,)),
                   *[pltpu.HBM(_half_shape(k), BF16) for k in BIG], *[pltpu.HBM(_part_shape(k), BF16) for k in BIG],
                   jax.ShapeDtypeStruct((8, 128), F32)),
        in_specs=[HBM_SPEC] * (2 * n), out_specs=(SEM_SPEC, SEM_SPEC, *[HBM_SPEC] * (2 * n), VMEM_SPEC),
        input_output_aliases={i: i + 2 for i in range(2 * n)},
        compiler_params=pltpu.CompilerParams(has_side_effects=pltpu.SideEffectType.DATAFLOW_SIDE_EFFECTING),
    )(*[pltpu.with_memory_space_constraint(sums[k], pltpu.HBM) for k in BIG], *lands)
    return outs[0], outs[1], outs[2:2 * n + 2], outs[2 * n + 2]


def _scatter_wait(send_sems, recv_sems, inflight, keys, after, tag):
    BIG = keys
    n = len(BIG)

    def body(*refs):
        for cp in _scatter_copies(dict(zip(BIG, refs[:n])), dict(zip(BIG, refs[n:2 * n])), refs[2 * n], refs[2 * n + 1]):
            cp.wait_send()
            cp.wait_recv()

    outs = pl.pallas_call(
        body, name="scatter_wait" + tag,
        out_shape=(*[pltpu.HBM(_half_shape(k), BF16) for k in BIG], *[pltpu.HBM(_part_shape(k), BF16) for k in BIG]),
        in_specs=[HBM_SPEC] * (2 * n) + [SEM_SPEC, SEM_SPEC, pl.BlockSpec(memory_space=pl.ANY)],
        out_specs=(HBM_SPEC,) * (2 * n), input_output_aliases={i: i for i in range(2 * n)},
        compiler_params=pltpu.CompilerParams(has_side_effects=pltpu.SideEffectType.DATAFLOW_SIDE_EFFECTING),
    )(*inflight, send_sems, recv_sems, after)
    return dict(zip(BIG, outs[:n])), dict(zip(BIG, outs[n:]))


def _sum_half(parts, own, k, where, layer, into, name):
    sp = BIG_SPEC[k]
    rows, cols, keep = sp["rows"], sp["cols"], sp["keep"]
    hr = rows // 2
    tr = _pick(hr, (256, 352, 128))
    nb = hr // tr
    if sp["axis"] == 1:
        own_spec = pl.BlockSpec((tr, cols), lambda i, w: (i, w[0]))
    else:
        own_spec = pl.BlockSpec((tr, cols), lambda i, w: (w[0] * nb + i, 0))

    def kern(w_ref, p_ref, own_ref, *rest):
        tot = own_ref[...].astype(F32)
        for j in range(N_CHIPS - 1):
            tot = tot + p_ref[j].astype(F32)
        rest[-1][0] = tot[:, :keep]

    in_specs = [pl.BlockSpec((N_CHIPS - 1, tr, cols), lambda i, w: (0, i, 0)), own_spec]
    args = [where, parts, own]
    if into is not None:
        in_specs.append(pl.BlockSpec(memory_space=pl.ANY))
        args.append(into)
    return pl.pallas_call(
        kern, name=name, out_shape=jax.ShapeDtypeStruct((DEPTH, rows, keep), F32),
        grid_spec=pltpu.PrefetchScalarGridSpec(
            num_scalar_prefetch=1, grid=(nb,), in_specs=in_specs,
            out_specs=pl.BlockSpec((1, tr, keep), lambda i, w: (layer, w[1] * nb + i, 0))),
        input_output_aliases={} if into is None else {3: 0},
        compiler_params=_cparams(("parallel",)),
    )(*args)


def _exchange_halves(red):
    n = len(BIG)

    def body(*refs):
        out = dict(zip(BIG, refs[n:2 * n]))
        send_sems, recv_sems = refs[2 * n:]
        x, y, c = _me()
        net = _Copies(send_sems, recv_sems)
        for k in BIG:
            hr = BIG_SPEC[k]["rows"] // 2
            for l in range(DEPTH):
                mine = out[k].at[l, pl.ds(pl.multiple_of(c * hr, 8), hr), :]
                theirs = out[k].at[l, pl.ds(pl.multiple_of((1 - c) * hr, 8), hr), :]
                net.send(mine, mine, (x, y, 1 - c), theirs)
        net.finish()

    outs = pl.pallas_call(
        body, name="exchange_halves",
        out_shape=[jax.ShapeDtypeStruct((DEPTH, BIG_SPEC[k]["rows"], BIG_SPEC[k]["keep"]), F32) for k in BIG],
        in_specs=[HBM_SPEC] * n, out_specs=[HBM_SPEC] * n, input_output_aliases={i: i for i in range(n)},
        scratch_shapes=[pltpu.SemaphoreType.DMA((DEPTH * n,)), pltpu.SemaphoreType.DMA((DEPTH * n,))],
    )(*[red[k] for k in BIG])
    return dict(zip(BIG, outs))


def _adam_vals(g, w, m, v):
    m2 = ADAM_B1 * m + (1.0 - ADAM_B1) * g
    v2 = ADAM_B2 * v + (1.0 - ADAM_B2) * (g * g)
    m_hat = m2 / (1.0 - ADAM_B1 ** ADAM_STEP)
    v_hat = v2 / (1.0 - ADAM_B2 ** ADAM_STEP)
    return -ADAM_LR * (m_hat / (jnp.sqrt(v_hat) + ADAM_EPS) + ADAM_WD * w), m2, v2


def _allreduce_small_adam(groups):
    ng = len(groups)

    def body(*refs):
        ins = [refs[4 * i:4 * i + 4] for i in range(ng)]
        outs = [refs[4 * ng + 4 * i:4 * ng + 4 * i + 4] for i in range(ng)]
        bufs = refs[8 * ng:9 * ng]
        send_sems, recv_sems = refs[9 * ng:]
        x, y, c = _me()
        me = 4 * x + 2 * y + c
        net = _Copies(send_sems, recv_sems)
        for (g_ref, _, _, _), buf in zip(ins, bufs):
            buf[me] = g_ref[...]
            for f in FLIPS:
                px, py, pc = _peer(x, y, c, f)
                net.send(g_ref, buf.at[me], (px, py, pc), buf.at[4 * px + 2 * py + pc])
        net.finish()
        for (_, w_ref, m_ref, v_ref), (gs_ref, d_ref, nm_ref, nv_ref), buf in zip(ins, outs, bufs):
            tot = buf[0]
            for d in range(1, 8):
                tot = tot + buf[d]
            gs_ref[...] = tot
            d_ref[...], nm_ref[...], nv_ref[...] = _adam_vals(tot, w_ref[...], m_ref[...], v_ref[...])

    shapes = [jax.ShapeDtypeStruct(g[0].shape, F32) for g in groups for _ in range(4)]
    outs = pl.pallas_call(
        body, name="allreduce_small", out_shape=shapes, in_specs=[VMEM_SPEC] * (4 * ng), out_specs=[VMEM_SPEC] * (4 * ng),
        scratch_shapes=[pltpu.VMEM((8,) + g[0].shape, F32) for g in groups]
        + [pltpu.SemaphoreType.DMA((7 * ng,)), pltpu.SemaphoreType.DMA((7 * ng,))],
        compiler_params=pltpu.CompilerParams(vmem_limit_bytes=VMEM_LIMIT),
    )(*[t for g in groups for t in g])
    return [outs[4 * i:4 * i + 4] for i in range(ng)]


def _adam(g, w, m, v, name, lead_block=1):
    shape = w.shape
    lead, rows, cols = math.prod(shape[:-2]), shape[-2], shape[-1]
    tr = _pick(rows, (256, 352, 64, 8, rows))
    spec = pl.BlockSpec((lead_block, tr, cols), lambda l, i: (l, i, 0))

    def kern(g_ref, w_ref, m_ref, v_ref, d_ref, nm_ref, nv_ref):
        d_ref[...], nm_ref[...], nv_ref[...] = _adam_vals(g_ref[...], w_ref[...], m_ref[...], v_ref[...])

    outs = pl.pallas_call(
        kern, name=name, grid=(lead // lead_block, rows // tr), in_specs=[spec] * 4, out_specs=[spec] * 3,
        out_shape=[jax.ShapeDtypeStruct((lead, rows, cols), F32)] * 3, compiler_params=_cparams(("parallel", "parallel")),
    )(*[t.reshape(lead, rows, cols) for t in (g, w, m, v)])
    return [o.reshape(shape) for o in outs]


SMALL = ("attn_norm", "sgu_ln_g", "sgu_ln_b", "sgu_w", "sgu_b", "attn_sinks", "dn_a_log", "dn_dt_bias", "dn_norm",
         "ffn_norm", "final_norm")
SMALL_2D = {"attn_norm": (DEPTH, D_MODEL), "ffn_norm": (DEPTH, D_MODEL), "final_norm": (1, D_MODEL),
            "sgu_ln_g": (DEPTH, MIX), "sgu_ln_b": (DEPTH, MIX), "sgu_w": (DEPTH * SGU_G * SGU_T, SGU_T),
            "sgu_b": (DEPTH * SGU_G, SGU_T), "dn_norm": (DEPTH, DN_HD)}
TINY = ("attn_sinks", "dn_a_log", "dn_dt_bias")


def _pack_tiny(vals, extra=None):
    flat = [vals[k].astype(F32).reshape(-1) for k in TINY] + ([] if extra is None else [extra.astype(F32).reshape(-1)])
    n = sum(f.shape[0] for f in flat)
    return jnp.concatenate(flat + [jnp.zeros((8 * 128 - n,), F32)]).reshape(8, 128)


def _unpack_tiny(tile, shapes):
    flat, out, o = tile.reshape(-1), {}, 0
    for k in TINY:
        n = math.prod(shapes[k])
        out[k] = flat[o:o + n].reshape(shapes[k])
        o += n
    return out, flat[o]


def _in_col_segments():
    shard, padded = IN_COLS // N_CHIPS, BIG_SPEC["w_in"]["cols"]
    segs, mine = [], 0
    for a, n in IN_PIECES:
        o = a
        while o < a + n:
            end = min(a + n, (o // shard + 1) * shard)
            segs.append(((o // shard) * padded + o % shard, mine + o - a, end - o))
            o = end
        mine += n
    return segs


def _move_cols(x, segs, out_cols, name):
    layers, rows, cols = x.shape
    tr = _pick(rows, (256, rows))
    gaps, at = [], 0
    for d, w in sorted((d, w) for _, d, w in segs):
        if d > at:
            gaps.append((at, d - at))
        at = d + w
    if at < out_cols:
        gaps.append((at, out_cols - at))

    def kern(x_ref, o_ref):
        for s, d, w in segs:
            o_ref[0, :, d:d + w] = x_ref[0, :, s:s + w]
        for d, w in gaps:
            o_ref[0, :, d:d + w] = jnp.zeros((tr, w), x.dtype)

    return pl.pallas_call(
        kern, name=name, grid=(layers, rows // tr), in_specs=[pl.BlockSpec((1, tr, cols), lambda l, i: (l, i, 0))],
        out_specs=pl.BlockSpec((1, tr, out_cols), lambda l, i: (l, i, 0)),
        out_shape=jax.ShapeDtypeStruct((layers, rows, out_cols), x.dtype), compiler_params=_cparams(("parallel", "parallel")),
    )(x)


WEIGHTS = ("attn_norm", "w_in", "sgu_ln_g", "sgu_ln_b", "sgu_w", "sgu_b", "attn_sinks", "dn_conv_w", "dn_a_log",
           "dn_dt_bias", "dn_norm", "w_branch", "w_out", "ffn_norm", "w_gate_up", "w_down", "final_norm")


def kernel(x, positions, attn_norm, w_in, sgu_ln_g, sgu_ln_b, sgu_w, sgu_b, attn_sinks, dn_conv_w, dn_a_log, dn_dt_bias, dn_norm, w_branch, w_out, ffn_norm, w_gate_up, w_down, final_norm, loss_target, m_attn_norm, m_w_in, m_sgu_ln_g, m_sgu_ln_b, m_sgu_w, m_sgu_b, m_attn_sinks, m_dn_conv_w, m_dn_a_log, m_dn_dt_bias, m_dn_norm, m_w_branch, m_w_out, m_ffn_norm, m_w_gate_up, m_w_down, m_final_norm, v_attn_norm, v_w_in, v_sgu_ln_g, v_sgu_ln_b, v_sgu_w, v_sgu_b, v_attn_sinks, v_dn_conv_w, v_dn_a_log, v_dn_dt_bias, v_dn_norm, v_w_branch, v_w_out, v_ffn_norm, v_w_gate_up, v_w_down, v_final_norm):
    given = dict(locals())
    W = {k: given[k] for k in WEIGHTS}
    M = {k: given["m_" + k] for k in WEIGHTS}
    V = {k: given["v_" + k] for k in WEIGHTS}
    chip = 2 * lax.axis_index("x") + lax.axis_index("y")
    core = lax.axis_index("c")
    chip1 = chip.astype(jnp.int32).reshape(1)
    where = jnp.stack([chip, core]).astype(jnp.int32)

    placed = [{k: _place_shard(W[k].reshape(DEPTH, BIG_SPEC[k]["rows"], BIG_SPEC[k]["keep"]), k, chip1, l,
                               f"place{l}_{k}") for k in BIG} for l in range(DEPTH)]
    first, conv_full = _gather_layer({"w_in": placed[0]["w_in"]}, dn_conv_w.reshape(CONV_ROWS, CONV_COLS))
    behind = [_gather_start({k: placed[0][k] for k in BIG if k != "w_in"}, conv_full, "0")]
    behind.append(_gather_start(placed[1], behind[0][3], "1"))
    segs = _in_col_segments()

    def arrived(l, after):
        send_sems, recv_sems, inflight, _ = behind[l]
        return _gather_finish(_gather_wait(send_sems, recv_sems, inflight, after, str(l)), str(l))

    def big_of_layer(l, x_l):
        got = {} if l == 0 else arrived(1, x_l)
        w_in = first["w_in"] if l == 0 else got["w_in"]

        def rest(after):
            full = got or arrived(0, after)
            return dict(full, w_branch=full["w_branch"].reshape(3, MIX, D_MODEL))

        return dict(w_in=_move_cols(w_in[None], segs, IN_R, f"w_in_cols{l}")[0], rest=rest)

    w = {k: W[k] for k in SMALL}
    w["attn_norm"] = attn_norm + behind[1][3][0, 0]
    w["dn_conv_w"] = conv_full.reshape(DEPTH, DN_CONV, 3 * MIX)

    core1 = core.astype(jnp.int32).reshape(1)
    back_segs = [(d, s, n) for s, d, n in segs]
    travelling, sums, parts = [], [{}, {}], [{}, {}]

    def on_grads(l, gl):
        gl, tag = dict(gl), f"{l}_{len(gl)}"
        if "w_in" in gl:
            gl["w_in"] = _move_cols(gl["w_in"][None], back_segs, _full_shape("w_in")[1], f"g_in_cols{l}")[0]
        if "w_branch" in gl:
            gl["w_branch"] = gl["w_branch"].reshape(3 * MIX, D_MODEL)
        sibling = _halves_to_sibling(gl, "halves_to_sibling" + tag)
        chip_sums = {k: _add_half(gl[k], sibling[k], k, core1, f"chip_sum{l}_{k}") for k in gl}
        if l == 0 and "w_in" in gl:
            sums[l].update(chip_sums)
            parts[l].update(_scatter_chip_sums(chip_sums, "scatter_chip_sums"))
            return None
        send_sems, recv_sems, inflight, token = _scatter_start(chip_sums, tag)
        travelling.append((l, send_sems, recv_sems, inflight, _present(gl), tag))
        return token

    loss, dx, g = _local_step(x[0], positions[0], loss_target[0], w, big_of_layer, on_grads)
    for l, send_sems, recv_sems, inflight, keys, tag in travelling:
        landed = _scatter_wait(send_sems, recv_sems, inflight, keys, dx, tag)
        sums[l].update(landed[0])
        parts[l].update(landed[1])
    red = {k: _sum_half(parts[1][k], sums[1][k], k, where, 1, None, f"sum1_{k}") for k in BIG}
    red = {k: _sum_half(parts[0][k], sums[0][k], k, where, 0, red[k], f"sum0_{k}") for k in BIG}
    reduced = _exchange_halves(red)
    grads = {k: reduced[k].reshape(W[k].shape) for k in BIG}

    conv_2d = (CONV_ROWS, N_CHIPS * CONV_COLS)
    conv_zero = jnp.zeros(conv_2d, F32)
    groups = [tuple(d[k].reshape(SMALL_2D[k]) for d in (g, W, M, V)) for k in SMALL_2D]
    groups.append((g["dn_conv_w"].reshape(conv_2d), conv_zero, conv_zero, conv_zero))
    groups.append((_pack_tiny(g, loss), _pack_tiny(W), _pack_tiny(M), _pack_tiny(V)))
    summed = _allreduce_small_adam(groups)
    delta, new_m, new_v = {}, {}, {}
    for k, outs in zip(SMALL_2D, summed):
        for d, t in zip((grads, delta, new_m, new_v), outs):
            d[k] = t.reshape(W[k].shape)
    conv_sum = summed[len(SMALL_2D)][0].reshape(g["dn_conv_w"].shape)
    grads["dn_conv_w"] = lax.dynamic_slice_in_dim(conv_sum, chip * dn_conv_w.shape[2], dn_conv_w.shape[2], axis=2)
    tiny_shapes = {k: W[k].shape for k in TINY}
    tiny, loss_total = _unpack_tiny(summed[-1][0], tiny_shapes)
    grads.update(tiny)
    for d, t in zip((delta, new_m, new_v), summed[-1][1:]):
        d.update(_unpack_tiny(t, tiny_shapes)[0])
    for k in ("w_branch", "w_out", "w_gate_up", "w_down", "dn_conv_w"):
        delta[k], new_m[k], new_v[k] = _adam(grads[k], W[k], M[k], V[k], "adam_" + k)
    lead_first = lambda t: jnp.transpose(t, (2, 0, 1))
    outs = _adam(*[lead_first(d["w_in"]) for d in (grads, W, M, V)], "adam_w_in", lead_block=IN_COLS // N_CHIPS // 10)
    delta["w_in"], new_m["w_in"], new_v["w_in"] = (jnp.transpose(o, (1, 2, 0)) for o in outs)

    return (loss_total, dx[None], *[grads[k] for k in WEIGHTS], *[delta[k] for k in WEIGHTS],
            *[new_m[k] for k in WEIGHTS], *[new_v[k] for k in WEIGHTS])
```

```python
import functools
import math

import jax
import jax.numpy as jnp
from jax import lax
from jax.experimental import pallas as pl
from jax.experimental.pallas import tpu as pltpu

F32 = jnp.float32
BF16 = jnp.bfloat16
HI = lax.Precision.HIGHEST

D_MODEL = 1024
DEPTH = 2
MIX = 512
EPS = 1e-6
SGU_G, SGU_T = 4, 128
SWA_H, SWA_KV, SWA_HD, WINDOW = 8, 2, 64, 128
ROPE_THETA, ROPE_DIM = 500000.0, 16
DN_H, DN_HD, DN_CONV, DN_C = 4, 128, 4, 64
D_FF = 2816
IN_COLS = 6920
IN_PIECES = ((3848, 3072), (1792, 1536), (3328, 512), (0, 512), (512, 512), (1024, 512), (1536, 128), (1664, 128),
             (3840, 8))
IN_PAD = 120
IN_R = 7040
C_GATE, C_QKV, C_ZC, C_UA, C_VA, C_QB, C_KB, C_VB, C_SM = 0, 3072, 4608, 5120, 5632, 6144, 6656, 6784, 6912

ADAM_LR, ADAM_B1, ADAM_B2, ADAM_EPS, ADAM_WD, ADAM_STEP = 0.001, 0.9, 0.999, 1e-08, 0.01, 10
VMEM_LIMIT = 56 * 1024 * 1024


def _cparams(sem):
    return pltpu.CompilerParams(dimension_semantics=sem, vmem_limit_bytes=VMEM_LIMIT)


def _dg(a, b, ca, cb, prec=None):
    return lax.dot_general(a, b, (((ca,), (cb,)), ((), ())), precision=prec, preferred_element_type=F32)


def _split(x):
    hi = x.astype(BF16)
    return hi, (x - hi.astype(F32)).astype(BF16)


def _dg3_many(as_, bs, ca, cb):
    sa = [_split(a) for a in as_]
    sb = [_split(b) for b in bs]
    hh = [_dg(a[0], b[0], ca, cb) for a, b in zip(sa, sb)]
    hl = [_dg(a[0], b[1], ca, cb) for a, b in zip(sa, sb)]
    lh = [_dg(a[1], b[0], ca, cb) for a, b in zip(sa, sb)]
    return [x + (y + z) for x, y, z in zip(hh, hl, lh)]


def _dg_exact_lhs_many(a01, bs, ca, cb):
    a = a01.astype(BF16)
    b1 = [b.astype(BF16) for b in bs]
    r1 = [b - t.astype(F32) for b, t in zip(bs, b1)]
    b2 = [r.astype(BF16) for r in r1]
    b3 = [(r - t.astype(F32)).astype(BF16) for r, t in zip(r1, b2)]
    d1 = [_dg(a, t, ca, cb) for t in b1]
    d2 = [_dg(a, t, ca, cb) for t in b2]
    d3 = [_dg(a, t, ca, cb) for t in b3]
    return [x + (y + z) for x, y, z in zip(d1, d2, d3)]


def _mm(a, b):
    return _dg(a.astype(BF16), b.astype(BF16), 1, 0)


def _mm_nt(a, b):
    return _dg(a.astype(BF16), b.astype(BF16), 1, 1)


def _mm_tn(a, b):
    return _dg(a.astype(BF16), b.astype(BF16), 0, 0)


def _sigmoid(x):
    return 0.5 * jnp.tanh(0.5 * x) + 0.5


def _silu(x):
    return x * _sigmoid(x)


def _dsilu(x):
    s = _sigmoid(x)
    return s * (1.0 + x * (1.0 - s))


_GC = math.sqrt(2.0 / math.pi)


def _gelu(x):
    return 0.5 * x * (1.0 + jnp.tanh(_GC * (x + 0.044715 * x * x * x)))


def _dgelu(x):
    t = jnp.tanh(_GC * (x + 0.044715 * x * x * x))
    return 0.5 * (1.0 + t) + 0.5 * x * (1.0 - t * t) * _GC * (1.0 + 3.0 * 0.044715 * x * x)


def _softplus(x):
    return jnp.maximum(x, 0.0) + jnp.log(1.0 + jnp.exp(-jnp.abs(x)))


def _acc(ref, val, i):
    @pl.when(i == 0)
    def _():
        ref[...] = val

    @pl.when(i > 0)
    def _():
        ref[...] += val


def _halo_rows(dtype):
    return 8 * 4 // jnp.dtype(dtype).itemsize


def _tok_call(body, name, S, TB, tok_in, const_in=(), tok_out=(), acc_out=(), prev_in=(), next_in=(), smem_in=()):
    nb = S // TB
    in_specs, args = [], []
    for a, w, cb in tok_in:
        in_specs.append(pl.BlockSpec((TB, w), functools.partial(lambda i, cb: (i, cb), cb=cb)))
        args.append(a)
    for a, w, cb in prev_in:
        hr = _halo_rows(a.dtype)
        in_specs.append(pl.BlockSpec((hr, w), functools.partial(
            lambda i, cb, r: (jnp.maximum(i * r - 1, 0), cb), cb=cb, r=TB // hr)))
        args.append(a)
    for a, w, cb in next_in:
        hr = _halo_rows(a.dtype)
        in_specs.append(pl.BlockSpec((hr, w), functools.partial(
            lambda i, cb, r, last: (jnp.minimum((i + 1) * r, last), cb), cb=cb, r=TB // hr, last=S // hr - 1)))
        args.append(a)
    for a in const_in:
        in_specs.append(pl.BlockSpec(a.shape, lambda i: (0, 0)))
        args.append(a)
    for a in smem_in:
        in_specs.append(pl.BlockSpec(memory_space=pltpu.SMEM))
        args.append(a)
    out_specs, out_shape, aliases, shared = [], [], {}, {}
    for o, (w, dt, *dest) in enumerate(tok_out):
        if not dest:
            out_specs.append(pl.BlockSpec((TB, w), lambda i: (i, 0)))
            out_shape.append(jax.ShapeDtypeStruct((S, w), dt))
            continue
        cb, wide = dest
        out_specs.append(pl.BlockSpec((TB, w), functools.partial(lambda i, cb: (i, cb), cb=cb)))
        out_shape.append(jax.ShapeDtypeStruct((S, wide if isinstance(wide, int) else wide.shape[1]), dt))
        if not isinstance(wide, int):
            if id(wide) not in shared:
                shared[id(wide)] = len(args)
                in_specs.append(pl.BlockSpec(memory_space=pl.ANY))
                args.append(wide)
            aliases[shared[id(wide)]] = o
    for shp, dt in acc_out:
        out_specs.append(pl.BlockSpec(shp, lambda i: (0, 0)))
        out_shape.append(jax.ShapeDtypeStruct(shp, dt))
    n_extra = len(shared)

    def kern(*refs):
        n_in = len(in_specs) - n_extra
        body(pl.program_id(0), *refs[:n_in], *refs[n_in + n_extra:])

    return pl.pallas_call(
        kern, name=name, grid=(nb,), in_specs=in_specs, out_specs=out_specs, out_shape=out_shape,
        input_output_aliases=aliases, compiler_params=_cparams(("arbitrary",)),
    )(*args)


MM_BLOCKS = (1024, 1408, 640, 512, 256, 128)


def _pick(n, cands):
    for c in cands:
        if n % c == 0:
            return c
    return n


MM_VMEM_BUDGET = 44 * 1024 * 1024


def _mm_blocks(M, N, K, a_bytes, b_bytes, o_bytes, add_bytes):
    bn = _pick(N, MM_BLOCKS)
    fits = None
    for bk in [K] + [c for c in (2816, 2048) + MM_BLOCKS if c < K and K % c == 0]:
        for bm in [c for c in (2048,) + MM_BLOCKS if M % c == 0 and c >= min(M, 512)]:
            b_bufs = 1 if (bk == K and bn == N) else 2
            need = 2 * bm * bk * a_bytes + b_bufs * bk * bn * b_bytes + 2 * bm * bn * (o_bytes + add_bytes)
            need += bm * bn * 4 if bk < K else 0
            if need <= MM_VMEM_BUDGET:
                fits = fits or (bm, bn, bk)
                if (M // bm) * (N // bn) * (K // bk) >= 4:
                    return bm, bn, bk
    if fits is None:
        raise ValueError(f"no matmul blocks for {(M, N, K)}")
    return fits


def _matmul(a, b, *, ta=False, tb=False, add=None, out_dtype=F32, name):
    M, K = (a.shape[1], a.shape[0]) if ta else a.shape
    N = b.shape[0] if tb else b.shape[1]
    bm, bn, bk = _mm_blocks(M, N, K, a.dtype.itemsize, b.dtype.itemsize, jnp.dtype(out_dtype).itemsize,
                            0 if add is None else add.dtype.itemsize)
    nk = K // bk
    b_mode = dict(pipeline_mode=pl.Buffered(1)) if (bk == K and bn == N) else {}
    a_spec = pl.BlockSpec((bk, bm), lambda i, j, k: (k, i)) if ta else pl.BlockSpec((bm, bk), lambda i, j, k: (i, k))
    b_spec = (pl.BlockSpec((bn, bk), lambda i, j, k: (j, k), **b_mode) if tb
              else pl.BlockSpec((bk, bn), lambda i, j, k: (k, j), **b_mode))
    o_spec = pl.BlockSpec((bm, bn), lambda i, j, k: (i, j))
    ca, cb = (0 if ta else 1), (1 if tb else 0)

    def kern(*refs):
        a_ref, b_ref = refs[:2]
        add_ref = refs[2] if add is not None else None
        o_ref = refs[3] if add is not None else refs[2]
        p = _dg(a_ref[...].astype(BF16), b_ref[...].astype(BF16), ca, cb)

        def finish(r):
            if add is not None:
                r = r + add_ref[...].astype(F32)
            o_ref[...] = r.astype(out_dtype)

        if nk == 1:
            finish(p)
            return
        acc_ref = refs[-1]
        k = pl.program_id(2)

        @pl.when(k == 0)
        def _():
            acc_ref[...] = p

        @pl.when((k > 0) & (k < nk - 1))
        def _():
            acc_ref[...] += p

        @pl.when(k == nk - 1)
        def _():
            finish(acc_ref[...] + p)

    in_specs = [a_spec, b_spec] + ([o_spec] if add is not None else [])
    args = (a, b) + ((add,) if add is not None else ())
    return pl.pallas_call(
        kern, name=name, grid=(M // bm, N // bn, nk), in_specs=in_specs, out_specs=o_spec,
        out_shape=jax.ShapeDtypeStruct((M, N), out_dtype),
        scratch_shapes=[pltpu.VMEM((bm, bn), F32)] if nk > 1 else [],
        compiler_params=_cparams(("parallel", "parallel", "arbitrary")),
    )(*args)


def _rms_fwd(x, g, name):
    S = x.shape[0]

    def body(i, x_ref, g_ref, h_ref):
        xv = x_ref[...]
        r = lax.rsqrt(jnp.mean(xv * xv, axis=-1, keepdims=True) + EPS)
        h_ref[...] = (xv * r * g_ref[...]).astype(BF16)

    return _tok_call(body, name, S, min(S, 512), [(x, D_MODEL, 0)], [g], [(D_MODEL, BF16)])[0]


def _rms_bwd_vals(xv, g, dh):
    r = lax.rsqrt(jnp.mean(xv * xv, axis=-1, keepdims=True) + EPS)
    u = dh * g
    dx = r * u - xv * (r * r * r) * jnp.mean(u * xv, axis=-1, keepdims=True)
    dg = jnp.sum(dh * xv * r, axis=0, keepdims=True)
    return dx, dg


def _rms_bwd_add(x, g, dh, dres, name):
    S = x.shape[0]

    def body(i, x_ref, dh_ref, dr_ref, g_ref, dx_ref, dg_ref):
        dx, dg = _rms_bwd_vals(x_ref[...], g_ref[...], dh_ref[...].astype(F32))
        dx_ref[...] = dr_ref[...] + dx
        _acc(dg_ref, dg, i)

    return _tok_call(body, name, S, min(S, 512), [(x, D_MODEL, 0), (dh, D_MODEL, 0), (dres, D_MODEL, 0)], [g],
                     [(D_MODEL, F32)], [((1, D_MODEL), F32)])


def _final_loss(x, g, target):
    S = x.shape[0]

    def body(i, x_ref, t_ref, g_ref, dx_ref, loss_ref, dg_ref):
        xv, gv = x_ref[...], g_ref[...]
        r = lax.rsqrt(jnp.mean(xv * xv, axis=-1, keepdims=True) + EPS)
        e = xv * r * gv - t_ref[...]
        part = 0.5 * jnp.sum(jnp.mean(e * e, axis=-1, keepdims=True), axis=0, keepdims=True)
        dx, dg = _rms_bwd_vals(xv, gv, e * (1.0 / D_MODEL))
        dx_ref[...] = dx
        _acc(loss_ref, jnp.broadcast_to(part, (1, 128)), i)
        _acc(dg_ref, dg, i)

    return _tok_call(body, "final_loss", S, min(S, 512), [(x, D_MODEL, 0), (target, D_MODEL, 0)], [g],
                     [(D_MODEL, F32)], [((1, 128), F32), ((1, D_MODEL), F32)])


def _swiglu_fwd(gu, name):
    S = gu.shape[0]

    def body(i, gu_ref, a_ref):
        a_ref[...] = (_silu(gu_ref[:, :D_FF].astype(F32)) * gu_ref[:, D_FF:].astype(F32)).astype(BF16)

    return _tok_call(body, name, S, min(S, 256), [(gu, 2 * D_FF, 0)], [], [(D_FF, BF16)])[0]


def _swiglu_bwd(gu, dact, name):
    S = gu.shape[0]

    def body(i, gu_ref, da_ref, dgu_ref):
        gg, uu, da = gu_ref[:, :D_FF].astype(F32), gu_ref[:, D_FF:].astype(F32), da_ref[...].astype(F32)
        dgu_ref[:, :D_FF] = (da * uu * _dsilu(gg)).astype(BF16)
        dgu_ref[:, D_FF:] = (da * _silu(gg)).astype(BF16)

    return _tok_call(body, name, S, min(S, 256), [(gu, 2 * D_FF, 0), (dact, D_FF, 0)], [], [(2 * D_FF, BF16)])[0]


def _merge_fwd(proj, bds, name):
    S = proj.shape[0]

    def body(i, g0, g1, g2, b0, b1, b2, m_ref):
        m = jnp.zeros(m_ref.shape, F32)
        for gr, br in ((g0, b0), (g1, b1), (g2, b2)):
            m = m + _sigmoid(gr[...].astype(F32)) * br[...].astype(F32)
        m_ref[...] = m.astype(BF16)

    tok = [(proj, D_MODEL, n) for n in range(3)] + [(b, D_MODEL, 0) for b in bds]
    return _tok_call(body, name, S, min(S, 512), tok, [], [(D_MODEL, BF16)])[0]


def _merge_bwd(proj, bds, dm, name):
    S = proj.shape[0]

    def body(i, g0, g1, g2, b0, b1, b2, dm_ref, d0, d1, d2, dgp_ref):
        dmv = dm_ref[...]
        for n, (gr, br, dr) in enumerate(((g0, b0, d0), (g1, b1, d1), (g2, b2, d2))):
            s = _sigmoid(gr[...].astype(F32))
            dr[...] = (dmv * s).astype(BF16)
            dgp_ref[:, n * D_MODEL:(n + 1) * D_MODEL] = (dmv * br[...].astype(F32) * s * (1.0 - s)).astype(BF16)

    tok = [(proj, D_MODEL, n) for n in range(3)] + [(b, D_MODEL, 0) for b in bds] + [(dm, D_MODEL, 0)]
    return _tok_call(body, name, S, min(S, 512), tok, [],
                     [(D_MODEL, BF16)] * 3 + [(3 * D_MODEL, BF16, C_GATE // (3 * D_MODEL), IN_R)])


def _sgu_ln(v, lg, lb):
    mu = jnp.mean(v, axis=-1, keepdims=True)
    vc = v - mu
    rstd = lax.rsqrt(jnp.mean(vc * vc, axis=-1, keepdims=True) + EPS)
    vhat = vc * rstd
    return vhat, rstd, vhat * lg + lb


def _sgu_fwd(proj, lg, lb, wc, bst, name):
    S = proj.shape[0]

    def body(i, ua_ref, va_ref, lg_ref, lb_ref, wc_ref, bs_ref, o_ref):
        u = _gelu(ua_ref[...].astype(F32))
        _, _, vn = _sgu_ln(_gelu(va_ref[...].astype(F32)), lg_ref[...], lb_ref[...])
        for g in range(SGU_G):
            sl = slice(g * 128, (g + 1) * 128)
            mixed = _mm(wc_ref[sl, :], vn[:, sl]) + bs_ref[:, g:g + 1]
            o_ref[:, sl] = (u[:, sl] * mixed).astype(BF16)

    return _tok_call(body, name, S, SGU_T, [(proj, MIX, C_UA // MIX), (proj, MIX, C_VA // MIX)], [lg, lb, wc, bst],
                     [(MIX, BF16)])[0]


def _sgu_bwd(proj, lg, lb, wc, bst, dout, dproj, name):
    S = proj.shape[0]

    def body(i, ua_ref, va_ref, do_ref, lg_ref, lb_ref, wc_ref, bs_ref, duv_ref, dlg_ref, dlb_ref, dwc_ref,
             dbs_ref):
        ua, va, do = ua_ref[...].astype(F32), va_ref[...].astype(F32), do_ref[...].astype(F32)
        u = _gelu(ua)
        lgv = lg_ref[...]
        vhat, rstd, vn = _sgu_ln(_gelu(va), lgv, lb_ref[...])
        tril = lax.broadcasted_iota(jnp.int32, (128, 128), 0) >= lax.broadcasted_iota(jnp.int32, (128, 128), 1)
        lane4 = lax.broadcasted_iota(jnp.int32, (128, 4), 1)
        gs = range(SGU_G)
        sls = [slice(g * 128, (g + 1) * 128) for g in gs]
        wgs = [wc_ref[sl, :] for sl in sls]
        mixed = [_mm(wgs[g], vn[:, sls[g]]) for g in gs]
        dmix = [do[:, sl] * u[:, sl] for sl in sls]
        dwg = [_mm_nt(dmix[g], vn[:, sls[g]]) for g in gs]
        dvn = jnp.concatenate([_mm_tn(wgs[g], dmix[g]) for g in gs], axis=1)
        dbs = jnp.zeros((128, 4), F32)
        for g in gs:
            duv_ref[:, sls[g]] = (do[:, sls[g]] * (mixed[g] + bs_ref[:, g:g + 1]) * _dgelu(ua[:, sls[g]])).astype(BF16)
            dbs = dbs + jnp.where(lane4 == g, jnp.sum(dmix[g], axis=-1, keepdims=True), 0.0)
            _acc(dwc_ref.at[sls[g], :], jnp.where(tril, dwg[g], 0.0), i)
        _acc(dbs_ref, dbs, i)
        _acc(dlg_ref, jnp.sum(dvn * vhat, axis=0, keepdims=True), i)
        _acc(dlb_ref, jnp.sum(dvn, axis=0, keepdims=True), i)
        dvh = dvn * lgv
        dv = rstd * (dvh - jnp.mean(dvh, axis=-1, keepdims=True) - vhat * jnp.mean(dvh * vhat, axis=-1, keepdims=True))
        duv_ref[:, MIX:] = (dv * _dgelu(va)).astype(BF16)

    return _tok_call(body, name, S, SGU_T, [(proj, MIX, C_UA // MIX), (proj, MIX, C_VA // MIX), (dout, MIX, 0)],
                     [lg, lb, wc, bst], [(2 * MIX, BF16, C_UA // (2 * MIX), dproj)],
                     [((1, MIX), F32), ((1, MIX), F32), ((SGU_G * 128, 128), F32), ((128, 4), F32)])


def _rope_tables(positions):
    S = positions.shape[0]
    inv_freq = ROPE_THETA ** (-jnp.arange(0, ROPE_DIM, 2, dtype=F32) / ROPE_DIM)
    ang = positions.astype(F32)[:, None] * inv_freq
    c, s = jnp.cos(ang), jnp.sin(ang)
    c64 = jnp.concatenate([c, c, jnp.ones((S, SWA_HD - ROPE_DIM), F32)], axis=1)
    s64 = jnp.concatenate([-s, s, jnp.zeros((S, SWA_HD - ROPE_DIM), F32)], axis=1)
    return jnp.tile(c64, (1, 2)), jnp.tile(s64, (1, 2))


def _rope128(x, c, s):
    lane = lax.broadcasted_iota(jnp.int32, x.shape, 1) % SWA_HD
    swapped = jnp.where(lane < ROPE_DIM // 2, pltpu.roll(x, 128 - ROPE_DIM // 2, 1), pltpu.roll(x, ROPE_DIM // 2, 1))
    return x * c + swapped * s


def _rope_t128(y, c, s):
    ys = y * s
    lane = lax.broadcasted_iota(jnp.int32, y.shape, 1) % SWA_HD
    swapped = jnp.where(lane < ROPE_DIM // 2, pltpu.roll(ys, 128 - ROPE_DIM // 2, 1), pltpu.roll(ys, ROPE_DIM // 2, 1))
    return y * c + jnp.where(lane < ROPE_DIM, swapped, 0.0)


def _rope_fwd(proj, cos, sin, name):
    S = proj.shape[0]
    scale = SWA_HD ** -0.5

    def body(i, q_ref, k_ref, v_ref, c_ref, s_ref, qo_ref, ko_ref, vo_ref):
        c, s = c_ref[...], s_ref[...]
        for j in range(4):
            sl = slice(j * 128, (j + 1) * 128)
            qo_ref[:, sl] = (_rope128(q_ref[:, sl].astype(F32), c, s) * scale).astype(BF16)
        ko_ref[...] = _rope128(k_ref[...].astype(F32), c, s).astype(BF16)
        vo_ref[...] = v_ref[...].astype(BF16)

    return _tok_call(body, name, S, min(S, 512),
                     [(proj, MIX, C_QB // MIX), (proj, 128, C_KB // 128), (proj, 128, C_VB // 128), (cos, 128, 0),
                      (sin, 128, 0)], [], [(MIX, BF16), (128, BF16), (128, BF16)])


def _rope_bwd(dq, dk, dv, cos, sin, dproj, name):
    S = dq.shape[0]
    scale = SWA_HD ** -0.5
    width = C_SM - C_QB

    def body(i, dq_ref, dk_ref, dv_ref, c_ref, s_ref, o_ref):
        c, s = c_ref[...], s_ref[...]
        for j in range(4):
            sl = slice(j * 128, (j + 1) * 128)
            o_ref[:, sl] = _rope_t128(dq_ref[:, sl] * scale, c, s).astype(BF16)
        o_ref[:, C_KB - C_QB:C_VB - C_QB] = _rope_t128(dk_ref[...], c, s).astype(BF16)
        o_ref[:, C_VB - C_QB:] = dv_ref[...].astype(BF16)

    return _tok_call(body, name, S, min(S, 512),
                     [(dq, MIX, 0), (dk, 128, 0), (dv, 128, 0), (cos, 128, 0), (sin, 128, 0)], [],
                     [(width, BF16, C_QB // width, dproj)])[0]


def _swa_band(i, k_ref, v_ref):
    pstart = pl.multiple_of(jnp.maximum(i - 1, 0) * WINDOW, WINDOW)
    cstart = pl.multiple_of(i * WINDOW, WINDOW)
    kb = jnp.concatenate([k_ref[pl.ds(pstart, WINDOW), :], k_ref[pl.ds(cstart, WINDOW), :]], axis=0)
    vb = jnp.concatenate([v_ref[pl.ds(pstart, WINDOW), :], v_ref[pl.ds(cstart, WINDOW), :]], axis=0)
    qi = lax.broadcasted_iota(jnp.int32, (WINDOW, 2 * WINDOW), 0)
    sj = lax.broadcasted_iota(jnp.int32, (WINDOW, 2 * WINDOW), 1)
    mask = (sj > qi) & (sj <= qi + WINDOW) & ((i > 0) | (sj >= WINDOW))
    return kb, vb, mask, pstart, cstart


def _swa_probs(qs, kh, mask, sinks):
    logits = [jnp.where(mask, _dg(qh, kh, 1, 1), -1e30) for qh in qs]
    m = [jnp.maximum(jnp.max(l, axis=-1, keepdims=True), s) for l, s in zip(logits, sinks)]
    p = [jnp.exp(l - mm) for l, mm in zip(logits, m)]
    ps = [jnp.exp(s - mm) for s, mm in zip(sinks, m)]
    inv = [1.0 / (jnp.sum(pp, axis=-1, keepdims=True) + s) for pp, s in zip(p, ps)]
    return [pp * iv for pp, iv in zip(p, inv)], [s * iv for s, iv in zip(ps, inv)]


def _swa_fwd(q, k, v, sinks, name):
    S = q.shape[0]
    G = SWA_H // SWA_KV

    def body(i, q_ref, k_ref, v_ref, s_ref, o_ref):
        kb, vb, mask, _, _ = _swa_band(i, k_ref, v_ref)
        qv = q_ref[...]
        for kv in range(SWA_KV):
            ksl = slice(kv * SWA_HD, (kv + 1) * SWA_HD)
            heads = range(kv * G, (kv + 1) * G)
            pn, _ = _swa_probs([qv[:, h * SWA_HD:(h + 1) * SWA_HD] for h in heads], kb[:, ksl], mask,
                               [s_ref[0, h] for h in heads])
            outs = [_dg(p.astype(BF16), vb[:, ksl], 1, 0) for p in pn]
            for h, o in zip(heads, outs):
                o_ref[:, h * SWA_HD:(h + 1) * SWA_HD] = o.astype(BF16)

    return _tok_call(body, name, S, WINDOW, [(q, MIX, 0)], [k, v], [(MIX, BF16)], smem_in=[sinks])[0]


def _swa_bwd(q, k, v, sinks, dout, name):
    S = q.shape[0]

    def body(i, q_ref, do_ref, k_ref, v_ref, s_ref, dq_ref, dk_ref, dv_ref, ds_ref):
        kb, vb, mask, pstart, cstart = _swa_band(i, k_ref, v_ref)
        qv, dov = q_ref[...], do_ref[...]
        lane = lax.broadcasted_iota(jnp.int32, (1, 128), 1)
        dsink = jnp.zeros((1, 128), F32)
        dkb, dvb = [], []
        G = SWA_H // SWA_KV
        for kv in range(SWA_KV):
            ksl = slice(kv * SWA_HD, (kv + 1) * SWA_HD)
            heads = range(kv * G, (kv + 1) * G)
            qs = [qv[:, h * SWA_HD:(h + 1) * SWA_HD] for h in heads]
            dos = [dov[:, h * SWA_HD:(h + 1) * SWA_HD].astype(BF16) for h in heads]
            pn, psn = _swa_probs(qs, kb[:, ksl], mask, [s_ref[0, h] for h in heads])
            dp = [_dg(d, vb[:, ksl], 1, 1) for d in dos]
            delta = [jnp.sum(a * b, axis=-1, keepdims=True) for a, b in zip(dp, pn)]
            dsc = [(p * (a - d)).astype(BF16) for p, a, d in zip(pn, dp, delta)]
            dqs = [_dg(s, kb[:, ksl], 1, 0) for s in dsc]
            dks = [_dg(s, qh, 0, 0) for s, qh in zip(dsc, qs)]
            dvs = [_dg(p.astype(BF16), d, 0, 0) for p, d in zip(pn, dos)]
            for n_, h in enumerate(heads):
                dq_ref[:, h * SWA_HD:(h + 1) * SWA_HD] = dqs[n_]
                dsink = dsink + jnp.where(lane == h, -jnp.sum(psn[n_] * delta[n_], axis=0, keepdims=True), 0.0)
            dkb.append((dks[0] + dks[1]) + (dks[2] + dks[3]))
            dvb.append((dvs[0] + dvs[1]) + (dvs[2] + dvs[3]))
        dkb = jnp.concatenate(dkb, axis=1)
        dvb = jnp.concatenate(dvb, axis=1)

        @pl.when(i == 0)
        def _():
            dk_ref[...] = jnp.zeros_like(dk_ref)
            dv_ref[...] = jnp.zeros_like(dv_ref)

        dk_ref[pl.ds(pstart, WINDOW), :] += dkb[:WINDOW]
        dv_ref[pl.ds(pstart, WINDOW), :] += dvb[:WINDOW]
        dk_ref[pl.ds(cstart, WINDOW), :] += dkb[WINDOW:]
        dv_ref[pl.ds(cstart, WINDOW), :] += dvb[WINDOW:]
        _acc(ds_ref, dsink, i)

    return _tok_call(body, name, S, WINDOW, [(q, MIX, 0), (dout, MIX, 0)], [k, v], [(MIX, F32)],
                     [((S, 128), F32), ((S, 128), F32), ((1, 128), F32)], smem_in=[sinks])


def _shift_rows(xs, k):
    return xs if k == 0 else pltpu.roll(xs, k, 0)


def _dn_conv(x_ref, p_ref, w_ref, i):
    hr = p_ref.shape[0]
    halo = jnp.where(i > 0, p_ref[...].astype(F32), 0.0)
    xs = jnp.concatenate([halo, x_ref[...].astype(F32)], axis=0)
    sh = [_shift_rows(xs, DN_CONV - 1 - t)[hr:] for t in range(DN_CONV)]
    pre = sh[0] * w_ref[0:1, :]
    for t in range(1, DN_CONV):
        pre = pre + sh[t] * w_ref[t:t + 1, :]
    return pre, sh


def _dn_gates(sm, alog, dtb):
    lane = lax.broadcasted_iota(jnp.int32, sm.shape, 1)
    return jnp.where(lane < DN_H, _sigmoid(sm), -jnp.exp(alog) * _softplus(sm + dtb))


def _dn_pre_fwd(proj, conv_w, alog_l, dtb_l, name):
    S = proj.shape[0]
    scale = DN_HD ** -0.5

    def body(i, x_ref, sm_ref, p_ref, w_ref, al_ref, db_ref, q_ref, k_ref, v_ref, bg_ref):
        pre, _ = _dn_conv(x_ref, p_ref, w_ref, i)
        a = _silu(pre)
        for h in range(DN_H):
            sl = slice(h * DN_HD, (h + 1) * DN_HD)
            qh, kh = a[:, sl], a[:, MIX + h * DN_HD:MIX + (h + 1) * DN_HD]
            q_ref[:, sl] = qh * (lax.rsqrt(jnp.sum(qh * qh, axis=-1, keepdims=True) + EPS) * scale)
            k_ref[:, sl] = kh * lax.rsqrt(jnp.sum(kh * kh, axis=-1, keepdims=True) + EPS)
        v_ref[...] = a[:, 2 * MIX:]
        bg_ref[...] = _dn_gates(sm_ref[...].astype(F32), al_ref[...], db_ref[...])

    TB = min(S, 256)
    return _tok_call(body, name, S, TB, [(proj, 3 * MIX, C_QKV // (3 * MIX)), (proj, 128, C_SM // 128)],
                     [conv_w, alog_l, dtb_l], [(MIX, F32), (MIX, F32), (MIX, F32), (128, F32)],
                     prev_in=[(proj, 3 * MIX, C_QKV // (3 * MIX))])


def _dn_pre_bwd1(proj, conv_w, alog_l, dtb_l, dq, dk, dv, dbg, dproj, name):
    S = proj.shape[0]
    scale = DN_HD ** -0.5

    def body(i, x_ref, sm_ref, dq_ref, dk_ref, dv_ref, dbg_ref, p_ref, w_ref, al_ref, db_ref, dpre_ref, dsm_ref,
             dw_ref, dal_ref, ddb_ref):
        pre, sh = _dn_conv(x_ref, p_ref, w_ref, i)
        a = _silu(pre)
        da_parts = []
        for part, (g_ref, sc) in enumerate(((dq_ref, scale), (dk_ref, 1.0))):
            for h in range(DN_H):
                xh = a[:, part * MIX + h * DN_HD:part * MIX + (h + 1) * DN_HD]
                rs = lax.rsqrt(jnp.sum(xh * xh, axis=-1, keepdims=True) + EPS)
                y = xh * rs
                dy = g_ref[:, h * DN_HD:(h + 1) * DN_HD] * sc
                da_parts.append(rs * (dy - y * jnp.sum(dy * y, axis=-1, keepdims=True)))
        da_parts.append(dv_ref[...])
        dpre = jnp.concatenate(da_parts, axis=1) * _dsilu(pre)
        dpre_ref[...] = dpre
        dw = jnp.concatenate([jnp.sum(dpre * sh[t], axis=0, keepdims=True) for t in range(DN_CONV)], axis=0)
        _acc(dw_ref, dw, i)
        sm, al, db, dbg_v = sm_ref[...].astype(F32), al_ref[...], db_ref[...], dbg_ref[...]
        lane = lax.broadcasted_iota(jnp.int32, sm.shape, 1)
        sg = _sigmoid(sm)
        gneg = -jnp.exp(al)
        is_g = (lane >= DN_H) & (lane < 2 * DN_H)
        d_al = jnp.where(is_g, dbg_v * gneg * _sigmoid(sm + db), 0.0)
        dsm_ref[...] = jnp.where(lane < DN_H, dbg_v * sg * (1.0 - sg), d_al).astype(BF16)
        _acc(ddb_ref, jnp.sum(d_al, axis=0, keepdims=True), i)
        _acc(dal_ref, jnp.sum(jnp.where(is_g, dbg_v * gneg * _softplus(sm + db), 0.0), axis=0, keepdims=True), i)

    TB = min(S, 256)
    return _tok_call(body, name, S, TB,
                     [(proj, 3 * MIX, C_QKV // (3 * MIX)), (proj, 128, C_SM // 128), (dq, MIX, 0), (dk, MIX, 0),
                      (dv, MIX, 0), (dbg, 128, 0)], [conv_w, alog_l, dtb_l],
                     [(3 * MIX, F32), (128, BF16, C_SM // 128, dproj)],
                     [((DN_CONV, 3 * MIX), F32), ((1, 128), F32), ((1, 128), F32)],
                     prev_in=[(proj, 3 * MIX, C_QKV // (3 * MIX))])


def _dn_pre_bwd2(dpre, conv_w, dproj, name):
    S = dpre.shape[0]
    TB = min(S, 256)
    nb = S // TB

    def body(i, d_ref, n_ref, w_ref, o_ref):
        halo = jnp.where(i < nb - 1, n_ref[...], 0.0)
        ds = jnp.concatenate([d_ref[...], halo], axis=0)
        out = ds[:TB] * w_ref[DN_CONV - 1:DN_CONV, :]
        for t in range(DN_CONV - 1):
            k = DN_CONV - 1 - t
            out = out + pltpu.roll(ds, TB + 8 - k, 0)[:TB] * w_ref[t:t + 1, :]
        o_ref[...] = out.astype(BF16)

    return _tok_call(body, name, S, TB, [(dpre, 3 * MIX, 0)], [conv_w],
                     [(3 * MIX, BF16, C_QKV // (3 * MIX), dproj)], next_in=[(dpre, 3 * MIX, 0)])[0]


def _dn_decay_terms(bgs, heads):
    C = DN_C
    ri = lax.broadcasted_iota(jnp.int32, (C, C), 0)
    ci = lax.broadcasted_iota(jnp.int32, (C, C), 1)
    tril, eye = ri >= ci, ri == ci
    beta = [b[:, h:h + 1] for b, h in zip(bgs, heads)]
    gcol = _dg_exact_lhs_many(tril, [jnp.broadcast_to(b[:, DN_H + h:DN_H + h + 1], (C, C))
                                     for b, h in zip(bgs, heads)], 1, 0)
    grow = [jnp.sum(jnp.where(eye, g, 0.0), axis=0, keepdims=True) for g in gcol]
    decay = [jnp.exp(jnp.where(tril, g - r, -1e30)) for g, r in zip(gcol, grow)]
    e_gc = [jnp.exp(g[:, 0:1]) for g in gcol]
    e_kd = [jnp.exp(g[C - 1:C, 0:1] - g[:, 0:1]) for g in gcol]
    cdec = [jnp.exp(g[C - 1:C, 0:1]) for g in gcol]
    return beta, decay, e_gc, e_kd, cdec


def _dn_nb(S):
    return 4 if S % (4 * DN_C) == 0 else 1


def _dn_prep_fwd(q, k, v, bg, name):
    S = q.shape[0]
    C, NB = DN_C, _dn_nb(S)
    TB = NB * C

    def kern(q_ref, k_ref, v_ref, bg_ref, t_ref, uw_ref, at_ref, qd_ref, kd_ref, dec_ref):
        lane = lax.broadcasted_iota(jnp.int32, (C, 128), 1)
        ri = lax.broadcasted_iota(jnp.int32, (C, C), 0)
        ci = lax.broadcasted_iota(jnp.int32, (C, C), 1)
        tril, eye = ri >= ci, ri == ci
        chains = [(cb, h) for cb in range(NB) for h in range(DN_H)]
        rows = lambda cb: slice(cb * C, (cb + 1) * C)
        head = lambda h: slice(h * DN_HD, (h + 1) * DN_HD)
        beta, decay, e_gc, e_kd, cdec = _dn_decay_terms([bg_ref[rows(cb), :] for cb, _ in chains],
                                                        [h for _, h in chains])
        qs = [q_ref[rows(cb), head(h)] for cb, h in chains]
        ks = [k_ref[rows(cb), head(h)] for cb, h in chains]
        kb = [kh * b for kh, b in zip(ks, beta)]
        x = [-jnp.where(ri > ci, _mm_nt(a, kh) * d, 0.0) for a, kh, d in zip(kb, ks, decay)]
        tm = [jnp.where(eye, 1.0, 0.0) + xi for xi in x]
        p = x
        p = _dg3_many(p, p, 1, 0)
        for it in range(5):
            if it == 4:
                tm = [t + tp for t, tp in zip(tm, _dg3_many(tm, p, 1, 0))]
                break
            both = _dg3_many([jnp.concatenate([t, pp], axis=0) for t, pp in zip(tm, p)], p, 1, 0)
            tm = [t + b[:C] for t, b in zip(tm, both)]
            p = [b[C:] for b in both]
        rhs = [jnp.concatenate([v_ref[rows(cb), head(h)] * b, a * e], axis=1)
               for (cb, h), b, a, e in zip(chains, beta, kb, e_gc)]
        sol = _dg3_many(tm, rhs, 1, 0)
        attn = [_mm_nt(qh, kh) * d for qh, kh, d in zip(qs, ks, decay)]
        for n_, (cb, h) in enumerate(chains):
            rs, sl, hc = rows(cb), head(h), slice(h * C, (h + 1) * C)
            t_ref[rs, hc] = tm[n_]
            uw_ref[rs, sl] = sol[n_][:, :DN_HD]
            uw_ref[rs, MIX + h * DN_HD:MIX + (h + 1) * DN_HD] = sol[n_][:, DN_HD:]
            at_ref[rs, hc] = attn[n_]
            qd_ref[rs, sl] = (qs[n_] * e_gc[n_]).astype(BF16)
            kd_ref[rs, sl] = (ks[n_] * e_kd[n_]).astype(BF16)
        for cb in range(NB):
            dec = jnp.zeros((C, 128), F32)
            for h in range(DN_H):
                dec = dec + jnp.where(lane == h, cdec[cb * DN_H + h], 0.0)
            dec_ref[rows(cb), :] = dec

    tok = lambda w: pl.BlockSpec((TB, w), lambda i: (i, 0))
    return pl.pallas_call(
        kern, name=name, grid=(S // TB,), in_specs=[tok(MIX), tok(MIX), tok(MIX), tok(128)],
        out_specs=[tok(DN_H * C), tok(2 * MIX), tok(DN_H * C), tok(MIX), tok(MIX), tok(128)],
        out_shape=[jax.ShapeDtypeStruct((S, DN_H * C), F32), jax.ShapeDtypeStruct((S, 2 * MIX), F32),
                   jax.ShapeDtypeStruct((S, DN_H * C), F32), jax.ShapeDtypeStruct((S, MIX), BF16),
                   jax.ShapeDtypeStruct((S, MIX), BF16), jax.ShapeDtypeStruct((S, 128), F32)],
        compiler_params=_cparams(("parallel",)),
    )(q, k, v, bg)


def _dn_scan_fwd(uw, at, qd, kd, dec, name):
    S = uw.shape[0]
    C, NB = DN_C, _dn_nb(S)
    TB = NB * C
    SR = DN_H * DN_HD

    def kern(uw_ref, at_ref, qd_ref, kd_ref, dec_ref, o_ref, vn_ref, st_ref, state):
        @pl.when(pl.program_id(0) == 0)
        def _():
            state[...] = jnp.zeros_like(state)

        for cb in range(NB):
            rs = slice(cb * C, (cb + 1) * C)
            hs = range(DN_H)
            sls = [slice(h * DN_HD, (h + 1) * DN_HD) for h in hs]
            s_in = [state[sl, :] for sl in sls]
            ws = [_mm(uw_ref[rs, MIX + h * DN_HD:MIX + (h + 1) * DN_HD], s_in[h]) for h in hs]
            os_ = [_mm(qd_ref[rs, sls[h]], s_in[h]) for h in hs]
            vnew = [uw_ref[rs, sls[h]] - ws[h] for h in hs]
            oa = [_mm(at_ref[rs, h * C:(h + 1) * C], vnew[h]) for h in hs]
            kv = [_mm_tn(kd_ref[rs, sls[h]], vnew[h]) for h in hs]
            for h in hs:
                o_ref[rs, sls[h]] = os_[h] + oa[h]
                state[sls[h], :] = s_in[h] * dec_ref[cb * C:cb * C + 1, h:h + 1] + kv[h]
                st_ref[cb * SR + h * DN_HD:cb * SR + (h + 1) * DN_HD, :] = s_in[h]
                vn_ref[rs, sls[h]] = vnew[h]

    tok = lambda w: pl.BlockSpec((TB, w), lambda i: (i, 0))
    return pl.pallas_call(
        kern, name=name, grid=(S // TB,), in_specs=[tok(2 * MIX), tok(DN_H * C), tok(MIX), tok(MIX), tok(128)],
        out_specs=[tok(MIX), tok(MIX), pl.BlockSpec((NB * SR, DN_HD), lambda i: (i, 0))],
        out_shape=[jax.ShapeDtypeStruct((S, MIX), F32), jax.ShapeDtypeStruct((S, MIX), F32),
                   jax.ShapeDtypeStruct((S // C * SR, DN_HD), F32)],
        scratch_shapes=[pltpu.VMEM((SR, DN_HD), F32)],
        compiler_params=_cparams(("arbitrary",)),
    )(uw, at, qd, kd, dec)


def _dn_core_fwd(q, k, v, bg, name):
    tm, uw, at, qd, kd, dec = _dn_prep_fwd(q, k, v, bg, name + "_prep")
    o, vn, st = _dn_scan_fwd(uw, at, qd, kd, dec, name + "_scan")
    return o, dict(tm=tm, uw=uw, at=at, qd=qd, kd=kd, dec=dec, vn=vn, st=st)


def _dn_scan_bwd(sv, do, name):
    S = do.shape[0]
    C, NB = DN_C, _dn_nb(S)
    TB = NB * C
    SR = DN_H * DN_HD
    nb = S // TB

    def kern(do_ref, uw_ref, at_ref, qd_ref, kd_ref, dec_ref, vn_ref, st_ref, dvn_ref, dw_ref, dkd_ref, dc_ref, dstate):
        @pl.when(pl.program_id(0) == 0)
        def _():
            dstate[...] = jnp.zeros_like(dstate)

        lane = lax.broadcasted_iota(jnp.int32, (C, 128), 1)
        for cb in reversed(range(NB)):
            rs = slice(cb * C, (cb + 1) * C)
            dcrow = jnp.zeros((C, 128), F32)
            for h in range(DN_H):
                sl = slice(h * DN_HD, (h + 1) * DN_HD)
                doh, ds_o = do_ref[rs, sl], dstate[sl, :]
                s_in = st_ref[cb * SR + h * DN_HD:cb * SR + (h + 1) * DN_HD, :]
                d_vnew = _mm_tn(at_ref[rs, h * C:(h + 1) * C], doh) + _mm(kd_ref[rs, sl], ds_o)
                dvn_ref[rs, sl] = d_vnew
                dw_ref[rs, sl] = -_mm_nt(d_vnew, s_in)
                dkd_ref[rs, sl] = _mm_nt(vn_ref[rs, sl], ds_o)
                d_c = jnp.sum(jnp.sum(ds_o * s_in, axis=1, keepdims=True), axis=0, keepdims=True)
                dcrow = dcrow + jnp.where(lane == h, d_c, 0.0)
                dstate[sl, :] = (ds_o * dec_ref[cb * C:cb * C + 1, h:h + 1] + _mm_tn(qd_ref[rs, sl], doh)
                                 - _mm_tn(uw_ref[rs, MIX + h * DN_HD:MIX + (h + 1) * DN_HD], d_vnew))
            dc_ref[rs, :] = dcrow

    tok = lambda w: pl.BlockSpec((TB, w), lambda i: (nb - 1 - i, 0))
    return pl.pallas_call(
        kern, name=name, grid=(nb,),
        in_specs=[tok(MIX), tok(2 * MIX), tok(DN_H * C), tok(MIX), tok(MIX), tok(128), tok(MIX),
                  pl.BlockSpec((NB * SR, DN_HD), lambda i: (nb - 1 - i, 0))],
        out_specs=[tok(MIX), tok(MIX), tok(MIX), tok(128)],
        out_shape=[jax.ShapeDtypeStruct((S, MIX), F32)] * 3 + [jax.ShapeDtypeStruct((S, 128), F32)],
        scratch_shapes=[pltpu.VMEM((SR, DN_HD), F32)],
        compiler_params=_cparams(("arbitrary",)),
    )(do, sv["uw"], sv["at"], sv["qd"], sv["kd"], sv["dec"], sv["vn"], sv["st"])


def _dn_chunk_bwd(q, k, v, bg, sv, do, dvn, dw, dkd, dc, name):
    S = q.shape[0]
    C, NB = DN_C, _dn_nb(S)
    TB = NB * C
    SR = DN_H * DN_HD

    def kern(q_ref, k_ref, v_ref, bg_ref, t_ref, uw_ref, vn_ref, st_ref, do_ref, dvn_ref, dw_ref, dkd_ref, dc_ref,
             dq_ref, dk_ref, dv_ref, dbg_ref):
        lane = lax.broadcasted_iota(jnp.int32, (C, 128), 1)
        ri = lax.broadcasted_iota(jnp.int32, (C, C), 0)
        ci = lax.broadcasted_iota(jnp.int32, (C, C), 1)
        tril, eye, last = ri >= ci, ri == ci, ri[:, 0:1] == C - 1
        chains = [(cb, h) for cb in range(NB) for h in range(DN_H)]
        each = lambda f, *ls: [f(*a) for a in zip(*ls)]
        rsum = lambda t: jnp.sum(t, axis=-1, keepdims=True)
        rows = lambda cb: slice(cb * C, (cb + 1) * C)
        head = lambda h: slice(h * DN_HD, (h + 1) * DN_HD)
        tok = lambda ref: [ref[rows(cb), head(h)] for cb, h in chains]
        beta, decay, e_gc, e_kd, cdec = _dn_decay_terms([bg_ref[rows(cb), :] for cb, _ in chains],
                                                        [h for _, h in chains])
        qs, ks, vs, dos, vnew, d_kd = tok(q_ref), tok(k_ref), tok(v_ref), tok(do_ref), tok(vn_ref), tok(dkd_ref)
        s_in = [st_ref[cb * SR + h * DN_HD:cb * SR + (h + 1) * DN_HD, :] for cb, h in chains]
        d_c = [dc_ref[cb * C:cb * C + 1, h:h + 1] for cb, h in chains]
        kb = each(lambda a, b: a * b, ks, beta)
        kk = each(_mm_nt, kb, ks)
        attn = each(lambda a, b, d: _mm_nt(a, b) * d, qs, ks, decay)
        d_qd = each(_mm_nt, dos, s_in)
        d_attn = each(_mm_nt, dos, vnew)
        d_sol = [jnp.concatenate([dvn_ref[rows(cb), head(h)], dw_ref[rows(cb), head(h)]], axis=1) for cb, h in chains]
        sol = [jnp.concatenate([uw_ref[rows(cb), head(h)], uw_ref[rows(cb), MIX + h * DN_HD:MIX + (h + 1) * DN_HD]],
                               axis=1) for cb, h in chains]
        d_rhs = _dg3_many([t_ref[rows(cb), h * C:(h + 1) * C] for cb, h in chains], d_sol, 0, 0)
        d_a = _dg3_many(d_rhs, sol, 1, 1)
        d_kk = each(lambda a, d: jnp.where(ri > ci, -a, 0.0) * d, d_a, decay)
        d_qk = each(lambda a, d: a * d, d_attn, decay)
        dm = each(lambda a, b, c_, d: a * b + c_ * d, d_kk, kk, d_attn, attn)
        d_vb = [t[:, :DN_HD] for t in d_rhs]
        dz = [t[:, DN_HD:] for t in d_rhs]
        d_kb = each(lambda z, e, a, kh: z * e + _mm(a, kh), dz, e_gc, d_kk, ks)
        d_k = each(lambda a, b, c_, q: _mm_tn(a, b) + _mm_tn(c_, q), d_kk, kb, d_qk, qs)
        d_q = each(lambda a, kh, b, e: _mm(a, kh) + b * e, d_qk, ks, d_qd, e_gc)
        t_kd = each(lambda a, kh, e: rsum(a * kh * e), d_kd, ks, e_kd)
        d_gl = each(lambda t, c_, cd: jnp.sum(t, axis=0, keepdims=True) + c_ * cd, t_kd, d_c, cdec)
        d_gc = each(lambda z, a, e, m, b, q, t, gl:
                    rsum(z * a) * e + rsum(m) - rsum(jnp.where(eye, jnp.sum(m, axis=0, keepdims=True), 0.0))
                    + rsum(b * q) * e - t + jnp.where(last, gl, 0.0),
                    dz, kb, e_gc, dm, d_qd, qs, t_kd, d_gl)
        d_g = _dg_exact_lhs_many(ri <= ci, [jnp.broadcast_to(t, (C, 128)) for t in d_gc], 1, 0)
        d_beta = each(lambda a, v_, b, kh: rsum(a * v_) + rsum(b * kh), d_vb, vs, d_kb, ks)
        for n_, (cb, h) in enumerate(chains):
            dq_ref[rows(cb), head(h)] = d_q[n_]
            dk_ref[rows(cb), head(h)] = d_k[n_] + d_kd[n_] * e_kd[n_] + d_kb[n_] * beta[n_]
            dv_ref[rows(cb), head(h)] = d_vb[n_] * beta[n_]
        for cb in range(NB):
            dbg = jnp.zeros((C, 128), F32)
            for h in range(DN_H):
                n_ = cb * DN_H + h
                dbg = dbg + jnp.where(lane == h, d_beta[n_], 0.0) + jnp.where(lane == DN_H + h, d_g[n_], 0.0)
            dbg_ref[rows(cb), :] = dbg

    tok = lambda w: pl.BlockSpec((TB, w), lambda i: (i, 0))
    return pl.pallas_call(
        kern, name=name, grid=(S // TB,),
        in_specs=[tok(MIX), tok(MIX), tok(MIX), tok(128), tok(DN_H * C), tok(2 * MIX), tok(MIX),
                  pl.BlockSpec((NB * SR, DN_HD), lambda i: (i, 0)), tok(MIX), tok(MIX), tok(MIX), tok(MIX), tok(128)],
        out_specs=[tok(MIX), tok(MIX), tok(MIX), tok(128)],
        out_shape=[jax.ShapeDtypeStruct((S, MIX), F32)] * 3 + [jax.ShapeDtypeStruct((S, 128), F32)],
        compiler_params=_cparams(("parallel",)),
    )(q, k, v, bg, sv["tm"], sv["uw"], sv["vn"], sv["st"], do, dvn, dw, dkd, dc)


def _dn_core_bwd(q, k, v, bg, sv, do, name):
    dvn, dw, dkd, dc = _dn_scan_bwd(sv, do, name + "_scan")
    return _dn_chunk_bwd(q, k, v, bg, sv, do, dvn, dw, dkd, dc, name + "_chunk")


def _dn_post_fwd(o, proj, ng, name):
    S = o.shape[0]

    def body(i, o_ref, z_ref, g_ref, out_ref):
        gv = g_ref[...]
        for h in range(DN_H):
            sl = slice(h * DN_HD, (h + 1) * DN_HD)
            oh = o_ref[:, sl]
            r = lax.rsqrt(jnp.mean(oh * oh, axis=-1, keepdims=True) + EPS)
            out_ref[:, sl] = (oh * r * gv * _silu(z_ref[:, sl].astype(F32))).astype(BF16)

    return _tok_call(body, name, S, min(S, 512), [(o, MIX, 0), (proj, MIX, C_ZC // MIX)], [ng], [(MIX, BF16)])[0]


def _dn_post_bwd(o, proj, ng, dout, dproj, name):
    S = o.shape[0]

    def body(i, o_ref, z_ref, do_ref, g_ref, dov_ref, dz_ref, dg_ref):
        gv = g_ref[...]
        dg = jnp.zeros((1, DN_HD), F32)
        for h in range(DN_H):
            sl = slice(h * DN_HD, (h + 1) * DN_HD)
            oh, zh, dh = o_ref[:, sl], z_ref[:, sl].astype(F32), do_ref[:, sl].astype(F32)
            r = lax.rsqrt(jnp.mean(oh * oh, axis=-1, keepdims=True) + EPS)
            dz_ref[:, sl] = (dh * oh * r * gv * _dsilu(zh)).astype(BF16)
            dx, dgh = _rms_bwd_vals(oh, gv, dh * _silu(zh))
            dov_ref[:, sl] = dx
            dg = dg + dgh
        _acc(dg_ref, dg, i)

    return _tok_call(body, name, S, min(S, 512), [(o, MIX, 0), (proj, MIX, C_ZC // MIX), (dout, MIX, 0)], [ng],
                     [(MIX, F32), (MIX, BF16, C_ZC // MIX, dproj)], [((1, DN_HD), F32)])


def _layer_params(w, big, l):
    lane = jnp.arange(128)
    is_g = (lane >= DN_H) & (lane < 2 * DN_H)
    spread = lambda t: jnp.where(is_g, jnp.tile(t, 128 // DN_H), 0.0).reshape(1, 128)
    tril = jnp.tril(jnp.ones((SGU_T, SGU_T), bool))
    return dict(
        win=big["w_in"], rest=big["rest"], conv=w["dn_conv_w"][l], attn_norm=w["attn_norm"][l].reshape(1, -1), ffn_norm=w["ffn_norm"][l].reshape(1, -1),
        lg=w["sgu_ln_g"][l].reshape(1, -1), lb=w["sgu_ln_b"][l].reshape(1, -1),
        wc=jnp.where(tril, w["sgu_w"][l], 0.0).reshape(SGU_G * SGU_T, SGU_T), bst=w["sgu_b"][l].T,
        sinks=w["attn_sinks"][l].reshape(1, -1), alog=spread(w["dn_a_log"][l]), dtb=spread(w["dn_dt_bias"][l]),
        ng=w["dn_norm"][l].reshape(1, -1))


def _layer_fwd(x, p, cos, sin, l):
    n = lambda s: f"l{l}_{s}"
    h = _rms_fwd(x, p["attn_norm"], n("rms1"))
    proj = _matmul(h, p["win"], out_dtype=BF16, name=n("mm_in"))
    out_a = _sgu_fwd(proj, p["lg"], p["lb"], p["wc"], p["bst"], n("sgu_fwd"))
    qr, kr, vr = _rope_fwd(proj, cos, sin, n("rope_fwd"))
    out_b = _swa_fwd(qr, kr, vr, p["sinks"], n("swa_fwd"))
    q, k, v, bg = _dn_pre_fwd(proj, p["conv"], p["alog"], p["dtb"], n("dn_pre_fwd"))
    o, dn = _dn_core_fwd(q, k, v, bg, n("dn_core_fwd"))
    out_c = _dn_post_fwd(o, proj, p["ng"], n("dn_post_fwd"))
    outs = (out_a, out_b, out_c)
    rest = p.pop("rest")(out_c)
    p.update(wb=rest["w_branch"], wout=rest["w_out"], wgu=rest["w_gate_up"], wdown=rest["w_down"])
    bds = [_matmul(outs[j], p["wb"][j], out_dtype=BF16, name=n(f"mm_branch{j}")) for j in range(3)]
    merged = _merge_fwd(proj, bds, n("merge_fwd"))
    x1 = _matmul(merged, p["wout"], add=x, name=n("mm_out"))
    h2 = _rms_fwd(x1, p["ffn_norm"], n("rms2"))
    gu = _matmul(h2, p["wgu"], out_dtype=BF16, name=n("mm_gu"))
    act = _swiglu_fwd(gu, n("swiglu_fwd"))
    x2 = _matmul(act, p["wdown"], add=x1, name=n("mm_down"))
    saved = dict(x=x, h=h, proj=proj, outs=outs, qr=qr, kr=kr, vr=vr, q=q, k=k, v=v, bg=bg, o=o, dn=dn, bds=bds,
                 merged=merged, x1=x1, h2=h2, gu=gu, act=act)
    return x2, saved


def _layer_bwd(dx2, s, p, cos, sin, l, early=None):
    n = lambda t: f"l{l}_{t}"
    proj = s["proj"]
    g = {}
    g["w_down"] = _matmul(s["act"], dx2, ta=True, out_dtype=BF16, name=n("wg_down"))
    dact = _matmul(dx2, p["wdown"], tb=True, out_dtype=BF16, name=n("dg_down"))
    dgu = _swiglu_bwd(s["gu"], dact, n("swiglu_bwd"))
    g["w_gate_up"] = _matmul(s["h2"], dgu, ta=True, out_dtype=BF16, name=n("wg_gu"))
    dh2 = _matmul(dgu, p["wgu"], tb=True, name=n("dg_gu"))
    dx1, g["ffn_norm"] = _rms_bwd_add(s["x1"], p["ffn_norm"], dh2, dx2, n("rms2_bwd"))
    g["w_out"] = _matmul(s["merged"], dx1, ta=True, out_dtype=BF16, name=n("wg_out"))
    dm = _matmul(dx1, p["wout"], tb=True, name=n("dg_out"))
    dbd0, dbd1, dbd2, dproj = _merge_bwd(proj, s["bds"], dm, n("merge_bwd"))
    dbds = (dbd0, dbd1, dbd2)
    g["w_branch"] = jnp.stack([_matmul(s["outs"][j], dbds[j], ta=True, out_dtype=BF16, name=n(f"wg_branch{j}"))
                               for j in range(3)])
    douts = [_matmul(dbds[j], p["wb"][j], tb=True, name=n(f"dg_branch{j}")) for j in range(3)]
    lg = p["lg"]
    if early is not None:
        token = early({k: g.pop(k) for k in ("w_down", "w_gate_up", "w_out", "w_branch")})
        lg = lg if token is None else lg + token[0, 0]
    dproj, g["sgu_ln_g"], g["sgu_ln_b"], dwc, dbs = _sgu_bwd(proj, lg, p["lb"], p["wc"], p["bst"], douts[0], dproj,
                                                             n("sgu_bwd"))
    g["sgu_w"] = dwc.reshape(SGU_G, SGU_T, SGU_T)
    g["sgu_b"] = dbs.T
    dqr, dkr, dvr, dsink = _swa_bwd(s["qr"], s["kr"], s["vr"], p["sinks"], douts[1], n("swa_bwd"))
    g["attn_sinks"] = dsink[0, :SWA_H]
    dproj = _rope_bwd(dqr, dkr, dvr, cos, sin, dproj, n("rope_bwd"))
    do, dproj, dng = _dn_post_bwd(s["o"], proj, p["ng"], douts[2], dproj, n("dn_post_bwd"))
    g["dn_norm"] = dng[0]
    dq, dk, dv, dbg = _dn_core_bwd(s["q"], s["k"], s["v"], s["bg"], s["dn"], do, n("dn_core_bwd"))
    dpre, dproj, g["dn_conv_w"], dal, ddb = _dn_pre_bwd1(proj, p["conv"], p["alog"], p["dtb"], dq, dk, dv, dbg, dproj,
                                                         n("dn_pre_bwd1"))
    g["dn_a_log"] = dal[0, DN_H:2 * DN_H]
    g["dn_dt_bias"] = ddb[0, DN_H:2 * DN_H]
    dproj = _dn_pre_bwd2(dpre, p["conv"], dproj, n("dn_pre_bwd2"))
    g["w_in"] = _matmul(s["h"], dproj, ta=True, out_dtype=BF16, name=n("wg_in"))
    dh = _matmul(dproj, p["win"], tb=True, name=n("dg_in"))
    dx, g["attn_norm"] = _rms_bwd_add(s["x"], p["attn_norm"], dh, dx1, n("rms1_bwd"))
    g["attn_norm"], g["ffn_norm"] = g["attn_norm"][0], g["ffn_norm"][0]
    g["sgu_ln_g"], g["sgu_ln_b"] = g["sgu_ln_g"][0], g["sgu_ln_b"][0]
    return dx, g


def _local_step(x, positions, target, w, big_of_layer, on_grads):
    cos, sin = _rope_tables(positions)
    params, saves, xs = [], [], x
    for l in range(DEPTH):
        params.append(_layer_params(w, big_of_layer(l, xs), l))
        xs, sv = _layer_fwd(xs, params[l], cos, sin, l)
        saves.append(sv)
    dx, loss_row, dgf = _final_loss(xs, w["final_norm"].reshape(1, -1), target)
    grads = [None] * DEPTH
    for l in reversed(range(DEPTH)):
        early = functools.partial(on_grads, l) if l == 0 else None
        dx, grads[l] = _layer_bwd(dx, saves[l], params[l], cos, sin, l, early)
        token = on_grads(l, {k: grads[l].pop(k) for k in BIG if k in grads[l]})
        if token is not None and l > 0:
            params[l - 1] = dict(params[l - 1], ffn_norm=params[l - 1]["ffn_norm"] + token[0, 0])
    stacked = {k: jnp.stack([grads[l][k] for l in range(DEPTH)]) for k in grads[0]}
    stacked["final_norm"] = dgf[0]
    return loss_row[0, 0], dx, stacked


MESH = pl.DeviceIdType.MESH
HBM_SPEC = pl.BlockSpec(memory_space=pltpu.HBM)
VMEM_SPEC = pl.BlockSpec(memory_space=pltpu.VMEM)
N_CHIPS = 4
FLIPS = tuple((fx, fy, fc) for fx in (0, 1) for fy in (0, 1) for fc in (0, 1))[1:]
BIG = ("w_in", "w_branch", "w_out", "w_gate_up", "w_down")
BIG_SPEC = {
    "w_in": dict(rows=1024, cols=1792, axis=1, keep=1730, down=8),
    "w_branch": dict(rows=1536, cols=256, axis=1, keep=256, down=2),
    "w_out": dict(rows=256, cols=1024, axis=0, keep=1024, down=1),
    "w_gate_up": dict(rows=1024, cols=1408, axis=1, keep=1408, down=8),
    "w_down": dict(rows=704, cols=1024, axis=0, keep=1024, down=4),
}
CONV_ROWS, CONV_COLS = DEPTH * DN_CONV, 3 * MIX // N_CHIPS


def _full_shape(k):
    sp = BIG_SPEC[k]
    return (sp["rows"], N_CHIPS * sp["cols"]) if sp["axis"] == 1 else (N_CHIPS * sp["rows"], sp["cols"])


def _me():
    return lax.axis_index("x"), lax.axis_index("y"), lax.axis_index("c")


def _peer(x, y, c, flip):
    fx, fy, fc = flip
    return (1 - x if fx else x, 1 - y if fy else y, 1 - c if fc else c)


class _Copies:
    def __init__(self, send_sems, recv_sems):
        self.send_sems, self.recv_sems, self.k, self.sent, self.landing = send_sems, recv_sems, 0, [], []

    def _copy(self, k, src, dst, to):
        return pltpu.make_async_remote_copy(src_ref=src, dst_ref=dst, send_sem=self.send_sems.at[k],
                                            recv_sem=self.recv_sems.at[k], device_id=to, device_id_type=MESH)

    def send(self, src, dst, to, lands):
        k = self.k
        self.k += 1
        cp = self._copy(k, src, dst, to)
        cp.start()
        self.sent.append(cp)
        self.landing.append(self._copy(k, lands, lands, to))
        return k

    def wait_landed(self, k):
        self.landing[k].wait_recv()

    def finish(self, landed=()):
        for k, cp in enumerate(self.landing):
            if k not in landed:
                cp.wait_recv()
        for cp in self.sent:
            cp.wait_send()


def _place_shard(shard, k, chip, layer, name):
    sp = BIG_SPEC[k]
    rows, cols, keep = sp["rows"], sp["cols"], sp["keep"]
    tr = _pick(rows, (256, 64))
    nb = rows // tr
    if sp["axis"] == 1:
        out_spec = pl.BlockSpec((tr, cols), lambda i, ch: (i, ch[0]))
    else:
        out_spec = pl.BlockSpec((tr, cols), lambda i, ch: (ch[0] * nb + i, 0))

    def kern(ch_ref, x_ref, o_ref):
        v = x_ref[0].astype(BF16)
        if keep == cols:
            o_ref[...] = v
        else:
            o_ref[:, :keep] = v
            o_ref[:, keep:] = jnp.zeros((tr, cols - keep), BF16)

    return pl.pallas_call(
        kern, name=name, out_shape=jax.ShapeDtypeStruct(_full_shape(k), BF16),
        grid_spec=pltpu.PrefetchScalarGridSpec(
            num_scalar_prefetch=1, grid=(nb,),
            in_specs=[pl.BlockSpec((1, tr, keep), lambda i, ch: (layer, i, 0))], out_specs=out_spec),
        compiler_params=_cparams(("parallel",)),
    )(chip, shard)


def _half_block(ref, k, s, half):
    sp = BIG_SPEC[k]
    hr = sp["rows"] // 2
    if sp["axis"] == 1:
        return ref.at[pl.ds(pl.multiple_of(half * hr, 16), hr), pl.ds(pl.multiple_of(s * sp["cols"], 128), sp["cols"])]
    return ref.at[pl.ds(pl.multiple_of(s * sp["rows"] + half * hr, 16), hr), :]


def _other_chips(x, y):
    return [(1 - x, y), (x, 1 - y), (1 - x, 1 - y)]


ALL_BIG = BIG


def _present(d):
    return tuple(k for k in ALL_BIG if k in d)


def _gather_layer(placed, conv):
    BIG = _present(placed)
    n = len(BIG)
    n_sem = 6 * n + 3

    def body(*refs):
        conv_ref = refs[n]
        out = dict(zip(BIG, refs[n + 1:2 * n + 1]))
        conv_out, send_sems, recv_sems, local_sem = refs[2 * n + 1:]
        x, y, c = _me()
        me = 2 * x + y
        chips = _other_chips(x, y)
        net = _Copies(send_sems, recv_sems)

        def conv_block(s):
            return conv_out.at[:, pl.ds(pl.multiple_of(s * CONV_COLS, 128), CONV_COLS)]

        local = pltpu.make_async_copy(conv_ref, conv_block(me), local_sem)
        local.start()
        first = {}
        for k in BIG:
            for j, (px, py) in enumerate(chips):
                first[k, j] = net.send(_half_block(out[k], k, me, c), _half_block(out[k], k, me, c), (px, py, c),
                                       _half_block(out[k], k, 2 * px + py, c))
        for px, py in chips:
            net.send(conv_ref, conv_block(me), (px, py, c), conv_block(2 * px + py))
        for k in BIG:
            for j, (px, py) in enumerate(chips):
                net.wait_landed(first[k, j])
                net.send(_half_block(out[k], k, 2 * px + py, c), _half_block(out[k], k, 2 * px + py, c), (x, y, 1 - c),
                         _half_block(out[k], k, 2 * px + py, 1 - c))
        net.finish(landed=set(first.values()))
        local.wait()

    out_shape = [jax.ShapeDtypeStruct(_full_shape(k), BF16) for k in BIG]
    out_shape.append(jax.ShapeDtypeStruct((CONV_ROWS, N_CHIPS * CONV_COLS), F32))
    outs = pl.pallas_call(
        body, name="gather_layer", out_shape=out_shape, in_specs=[HBM_SPEC] * (n + 1), out_specs=[HBM_SPEC] * (n + 1),
        input_output_aliases={i: i for i in range(n)},
        scratch_shapes=[pltpu.SemaphoreType.DMA((n_sem,)), pltpu.SemaphoreType.DMA((n_sem,)), pltpu.SemaphoreType.DMA],
    )(*[placed[k] for k in BIG], conv)
    return dict(zip(BIG, outs[:n])), outs[n]


SEM_SPEC = pl.BlockSpec(memory_space=pltpu.SEMAPHORE)


def _behind_copies(arrs, send_sems, recv_sems):
    x, y, c = _me()
    copies = []
    for i, k in enumerate(_present(arrs)):
        for j, (px, py) in enumerate(_other_chips(x, y)):
            copies.append(pltpu.make_async_remote_copy(
                src_ref=_half_block(arrs[k], k, 2 * x + y, c), dst_ref=_half_block(arrs[k], k, 2 * x + y, c),
                send_sem=send_sems.at[3 * i + j], recv_sem=recv_sems.at[3 * i + j], device_id=(px, py, c),
                device_id_type=MESH))
    return copies


def _gather_start(placed, after, tag):
    BIG = _present(placed)
    n = len(BIG)
    N_BEHIND = 3 * n

    def body(*refs):
        arrs = dict(zip(BIG, refs[n + 3:2 * n + 3]))
        send_sems, recv_sems = refs[n + 1], refs[n + 2]
        for cp in _behind_copies(arrs, send_sems, recv_sems):
            cp.start()
        refs[2 * n + 3][...] = jnp.zeros((8, 128), F32)

    outs = pl.pallas_call(
        body, name="gather_start" + tag,
        out_shape=(pltpu.SemaphoreType.DMA((N_BEHIND,)), pltpu.SemaphoreType.DMA((N_BEHIND,)),
                   *[pltpu.HBM(_full_shape(k), BF16) for k in BIG], jax.ShapeDtypeStruct((8, 128), F32)),
        in_specs=[HBM_SPEC] * n + [pl.BlockSpec(memory_space=pl.ANY)],
        out_specs=(SEM_SPEC, SEM_SPEC, *[HBM_SPEC] * n, VMEM_SPEC),
        input_output_aliases={i: i + 2 for i in range(n)},
        compiler_params=pltpu.CompilerParams(has_side_effects=pltpu.SideEffectType.DATAFLOW_SIDE_EFFECTING),
    )(*[pltpu.with_memory_space_constraint(placed[k], pltpu.HBM) for k in BIG], after)
    return outs[0], outs[1], dict(zip(BIG, outs[2:n + 2])), outs[n + 2]


def _gather_wait(send_sems, recv_sems, inflight, after, tag):
    BIG = _present(inflight)
    n = len(BIG)

    def body(*refs):
        arrs = dict(zip(BIG, refs[:n]))
        for cp in _behind_copies(arrs, refs[n], refs[n + 1]):
            cp.wait_send()
            cp.wait_recv()

    outs = pl.pallas_call(
        body, name="gather_wait" + tag, out_shape=tuple(pltpu.HBM(_full_shape(k), BF16) for k in BIG),
        in_specs=[HBM_SPEC] * n + [SEM_SPEC, SEM_SPEC, pl.BlockSpec(memory_space=pl.ANY)], out_specs=(HBM_SPEC,) * n,
        input_output_aliases={i: i for i in range(n)},
        compiler_params=pltpu.CompilerParams(has_side_effects=pltpu.SideEffectType.DATAFLOW_SIDE_EFFECTING),
    )(*[inflight[k] for k in BIG], send_sems, recv_sems, after)
    return dict(zip(BIG, outs))


def _gather_finish(arrs, tag):
    BIG = _present(arrs)
    n = len(BIG)
    N_BEHIND = 3 * n

    def body(*refs):
        out = dict(zip(BIG, refs[n:2 * n]))
        send_sems, recv_sems = refs[2 * n:]
        x, y, c = _me()
        net = _Copies(send_sems, recv_sems)
        for k in BIG:
            for px, py in _other_chips(x, y):
                net.send(_half_block(out[k], k, 2 * px + py, c), _half_block(out[k], k, 2 * px + py, c), (x, y, 1 - c),
                         _half_block(out[k], k, 2 * px + py, 1 - c))
        net.finish()

    outs = pl.pallas_call(
        body, name="gather_finish" + tag, out_shape=[jax.ShapeDtypeStruct(_full_shape(k), BF16) for k in BIG],
        in_specs=[HBM_SPEC] * n, out_specs=[HBM_SPEC] * n, input_output_aliases={i: i for i in range(n)},
        scratch_shapes=[pltpu.SemaphoreType.DMA((N_BEHIND,)), pltpu.SemaphoreType.DMA((N_BEHIND,))],
    )(*[arrs[k] for k in BIG])
    return dict(zip(BIG, outs))


def _row_chunks(ref, rows, n):
    step = rows // n
    return [ref.at[pl.ds(i * step, step), :] for i in range(n)]


def _half_pieces(ref, k, half):
    sp = BIG_SPEC[k]
    hr = sp["rows"] // 2
    if sp["axis"] == 1:
        return [ref.at[pl.ds(pl.multiple_of(half * hr, 16), hr), :]]
    return [ref.at[pl.ds(pl.multiple_of(s * sp["rows"] + half * hr, 16), hr), :] for s in range(N_CHIPS)]


def _half_shape(k):
    rows, cols = _full_shape(k)
    return rows // 2, cols


def _stacked_pieces(ref, k):
    sp = BIG_SPEC[k]
    hr = sp["rows"] // 2
    return [ref] if sp["axis"] == 1 else [ref.at[pl.ds(s * hr, hr), :] for s in range(N_CHIPS)]


def _chip_part(ref, k, s):
    sp = BIG_SPEC[k]
    hr = sp["rows"] // 2
    if sp["axis"] == 1:
        return ref.at[:, pl.ds(pl.multiple_of(s * sp["cols"], 128), sp["cols"])]
    return ref.at[pl.ds(pl.multiple_of(s * hr, 16), hr), :]


def _halves_to_sibling(grads, name):
    BIG = _present(grads)
    n = len(BIG)
    chunks = {k: max(BIG_SPEC[k]["down"] // 2, 1) if BIG_SPEC[k]["axis"] == 1 else 1 for k in BIG}
    n_sem = sum(chunks[k] if BIG_SPEC[k]["axis"] == 1 else N_CHIPS for k in BIG)

    def body(*refs):
        g = dict(zip(BIG, refs[:n]))
        out = dict(zip(BIG, refs[n:2 * n]))
        send_sems, recv_sems = refs[2 * n:]
        x, y, c = _me()
        net = _Copies(send_sems, recv_sems)
        for k in BIG:
            hr = BIG_SPEC[k]["rows"] // 2
            for src, dst in zip(_half_pieces(g[k], k, 1 - c), _stacked_pieces(out[k], k)):
                for s, d in zip(_row_chunks(src, hr, chunks[k]), _row_chunks(dst, hr, chunks[k])):
                    net.send(s, d, (x, y, 1 - c), d)
        net.finish()

    outs = pl.pallas_call(
        body, name=name, out_shape=[jax.ShapeDtypeStruct(_half_shape(k), BF16) for k in BIG],
        in_specs=[HBM_SPEC] * n, out_specs=[HBM_SPEC] * n,
        scratch_shapes=[pltpu.SemaphoreType.DMA((n_sem,)), pltpu.SemaphoreType.DMA((n_sem,))],
    )(*[grads[k] for k in BIG])
    return dict(zip(BIG, outs))


def _add_half(g, other, k, core, name):
    sp = BIG_SPEC[k]
    hr, cols = sp["rows"] // 2, _full_shape(k)[1]
    tr = _pick(hr, (256, 352, 128))
    nb = hr // tr
    if sp["axis"] == 1:
        grid = (nb,)
        g_spec = pl.BlockSpec((tr, cols), lambda i, c: (c[0] * nb + i, 0))
        h_spec = pl.BlockSpec((tr, cols), lambda i, c: (i, 0))
    else:
        grid = (N_CHIPS, nb)
        g_spec = pl.BlockSpec((tr, cols), lambda s, i, c: ((2 * s + c[0]) * nb + i, 0))
        h_spec = pl.BlockSpec((tr, cols), lambda s, i, c: (s * nb + i, 0))

    def kern(c_ref, a_ref, b_ref, o_ref):
        o_ref[...] = (a_ref[...].astype(F32) + b_ref[...].astype(F32)).astype(BF16)

    return pl.pallas_call(
        kern, name=name, out_shape=jax.ShapeDtypeStruct(_half_shape(k), BF16),
        grid_spec=pltpu.PrefetchScalarGridSpec(num_scalar_prefetch=1, grid=grid, in_specs=[g_spec, h_spec],
                                               out_specs=h_spec),
        compiler_params=_cparams(("parallel",) * len(grid)),
    )(core, g, other)


def _part_shape(k):
    return N_CHIPS - 1, BIG_SPEC[k]["rows"] // 2, BIG_SPEC[k]["cols"]


def _scatter_copies(sums, parts, send_sems, recv_sems):
    x, y, c = _me()
    copies = []
    for i, k in enumerate(_present(sums)):
        for j, (px, py) in enumerate(_other_chips(x, y)):
            copies.append(pltpu.make_async_remote_copy(
                src_ref=_chip_part(sums[k], k, 2 * px + py), dst_ref=parts[k].at[j], send_sem=send_sems.at[3 * i + j],
                recv_sem=recv_sems.at[3 * i + j], device_id=(px, py, c), device_id_type=MESH))
    return copies


def _scatter_start(sums, tag):
    BIG = _present(sums)
    n = len(BIG)
    N_BEHIND = 3 * n
    lands = [pltpu.with_memory_space_constraint(lax.empty(_part_shape(k), BF16), pltpu.HBM) for k in BIG]

    def body(*refs):
        outs = refs[2 * n + 2:4 * n + 2]
        for cp in _scatter_copies(dict(zip(BIG, outs[:n])), dict(zip(BIG, outs[n:])), refs[2 * n], refs[2 * n + 1]):
            cp.start()
        refs[4 * n + 2][...] = jnp.zeros((8, 128), F32)

    outs = pl.pallas_call(
        body, name="scatter_start" + tag,
        out_shape=(pltpu.SemaphoreType.DMA((N_BEHIND,)), pltpu.SemaphoreType.DMA((N_BEHIND,)),
                   *[pltpu.HBM(_half_shape(k), BF16) for k in BIG], *[pltpu.HBM(_part_shape(k), BF16) for k in BIG],
                   jax.ShapeDtypeStruct((8, 128), F32)),
        in_specs=[HBM_SPEC] * (2 * n), out_specs=(SEM_SPEC, SEM_SPEC, *[HBM_SPEC] * (2 * n), VMEM_SPEC),
        input_output_aliases={i: i + 2 for i in range(2 * n)},
        compiler_params=pltpu.CompilerParams(has_side_effects=pltpu.SideEffectType.DATAFLOW_SIDE_EFFECTING),
    )(*[pltpu.with_memory_space_constraint(sums[k], pltpu.HBM) for k in BIG], *lands)
    return outs[0], outs[1], outs[2:2 * n + 2], outs[2 * n + 2]


def _scatter_wait(send_sems, recv_sems, inflight, keys, after, tag):
    BIG = keys
    n = len(BIG)

    def body(*refs):
        for cp in _scatter_copies(dict(zip(BIG, refs[:n])), dict(zip(BIG, refs[n:2 * n])), refs[2 * n], refs[2 * n + 1]):
            cp.wait_send()
            cp.wait_recv()

    outs = pl.pallas_call(
        body, name="scatter_wait" + tag,
        out_shape=(*[pltpu.HBM(_half_shape(k), BF16) for k in BIG], *[pltpu.HBM(_part_shape(k), BF16) for k in BIG]),
        in_specs=[HBM_SPEC] * (2 * n) + [SEM_SPEC, SEM_SPEC, pl.BlockSpec(memory_space=pl.ANY)],
        out_specs=(HBM_SPEC,) * (2 * n), input_output_aliases={i: i for i in range(2 * n)},
        compiler_params=pltpu.CompilerParams(has_side_effects=pltpu.SideEffectType.DATAFLOW_SIDE_EFFECTING),
    )(*inflight, send_sems, recv_sems, after)
    return dict(zip(BIG, outs[:n])), dict(zip(BIG, outs[n:]))


def _sum_half(parts, own, k, where, layer, into, name):
    sp = BIG_SPEC[k]
    rows, cols, keep = sp["rows"], sp["cols"], sp["keep"]
    hr = rows // 2
    tr = _pick(hr, (256, 352, 128))
    nb = hr // tr
    if sp["axis"] == 1:
        own_spec = pl.BlockSpec((tr, cols), lambda i, w: (i, w[0]))
    else:
        own_spec = pl.BlockSpec((tr, cols), lambda i, w: (w[0] * nb + i, 0))

    def kern(w_ref, p_ref, own_ref, *rest):
        tot = own_ref[...].astype(F32)
        for j in range(N_CHIPS - 1):
            tot = tot + p_ref[j].astype(F32)
        rest[-1][0] = tot[:, :keep]

    in_specs = [pl.BlockSpec((N_CHIPS - 1, tr, cols), lambda i, w: (0, i, 0)), own_spec]
    args = [where, parts, own]
    if into is not None:
        in_specs.append(pl.BlockSpec(memory_space=pl.ANY))
        args.append(into)
    return pl.pallas_call(
        kern, name=name, out_shape=jax.ShapeDtypeStruct((DEPTH, rows, keep), F32),
        grid_spec=pltpu.PrefetchScalarGridSpec(
            num_scalar_prefetch=1, grid=(nb,), in_specs=in_specs,
            out_specs=pl.BlockSpec((1, tr, keep), lambda i, w: (layer, w[1] * nb + i, 0))),
        input_output_aliases={} if into is None else {3: 0},
        compiler_params=_cparams(("parallel",)),
    )(*args)


def _exchange_halves(red):
    n = len(BIG)

    def body(*refs):
        out = dict(zip(BIG, refs[n:2 * n]))
        send_sems, recv_sems = refs[2 * n:]
        x, y, c = _me()
        net = _Copies(send_sems, recv_sems)
        for k in BIG:
            hr = BIG_SPEC[k]["rows"] // 2
            for l in range(DEPTH):
                mine = out[k].at[l, pl.ds(pl.multiple_of(c * hr, 8), hr), :]
                theirs = out[k].at[l, pl.ds(pl.multiple_of((1 - c) * hr, 8), hr), :]
                net.send(mine, mine, (x, y, 1 - c), theirs)
        net.finish()

    outs = pl.pallas_call(
        body, name="exchange_halves",
        out_shape=[jax.ShapeDtypeStruct((DEPTH, BIG_SPEC[k]["rows"], BIG_SPEC[k]["keep"]), F32) for k in BIG],
        in_specs=[HBM_SPEC] * n, out_specs=[HBM_SPEC] * n, input_output_aliases={i: i for i in range(n)},
        scratch_shapes=[pltpu.SemaphoreType.DMA((DEPTH * n,)), pltpu.SemaphoreType.DMA((DEPTH * n,))],
    )(*[red[k] for k in BIG])
    return dict(zip(BIG, outs))


def _adam_vals(g, w, m, v):
    m2 = ADAM_B1 * m + (1.0 - ADAM_B1) * g
    v2 = ADAM_B2 * v + (1.0 - ADAM_B2) * (g * g)
    m_hat = m2 / (1.0 - ADAM_B1 ** ADAM_STEP)
    v_hat = v2 / (1.0 - ADAM_B2 ** ADAM_STEP)
    return -ADAM_LR * (m_hat / (jnp.sqrt(v_hat) + ADAM_EPS) + ADAM_WD * w), m2, v2


def _allreduce_small_adam(groups):
    ng = len(groups)

    def body(*refs):
        ins = [refs[4 * i:4 * i + 4] for i in range(ng)]
        outs = [refs[4 * ng + 4 * i:4 * ng + 4 * i + 4] for i in range(ng)]
        bufs = refs[8 * ng:9 * ng]
        send_sems, recv_sems = refs[9 * ng:]
        x, y, c = _me()
        me = 4 * x + 2 * y + c
        net = _Copies(send_sems, recv_sems)
        for (g_ref, _, _, _), buf in zip(ins, bufs):
            buf[me] = g_ref[...]
            for f in FLIPS:
                px, py, pc = _peer(x, y, c, f)
                net.send(g_ref, buf.at[me], (px, py, pc), buf.at[4 * px + 2 * py + pc])
        net.finish()
        for (_, w_ref, m_ref, v_ref), (gs_ref, d_ref, nm_ref, nv_ref), buf in zip(ins, outs, bufs):
            tot = buf[0]
            for d in range(1, 8):
                tot = tot + buf[d]
            gs_ref[...] = tot
            d_ref[...], nm_ref[...], nv_ref[...] = _adam_vals(tot, w_ref[...], m_ref[...], v_ref[...])

    shapes = [jax.ShapeDtypeStruct(g[0].shape, F32) for g in groups for _ in range(4)]
    outs = pl.pallas_call(
        body, name="allreduce_small", out_shape=shapes, in_specs=[VMEM_SPEC] * (4 * ng), out_specs=[VMEM_SPEC] * (4 * ng),
        scratch_shapes=[pltpu.VMEM((8,) + g[0].shape, F32) for g in groups]
        + [pltpu.SemaphoreType.DMA((7 * ng,)), pltpu.SemaphoreType.DMA((7 * ng,))],
        compiler_params=pltpu.CompilerParams(vmem_limit_bytes=VMEM_LIMIT),
    )(*[t for g in groups for t in g])
    return [outs[4 * i:4 * i + 4] for i in range(ng)]


def _adam(g, w, m, v, name, lead_block=1):
    shape = w.shape
    lead, rows, cols = math.prod(shape[:-2]), shape[-2], shape[-1]
    tr = _pick(rows, (256, 352, 64, 8, rows))
    spec = pl.BlockSpec((lead_block, tr, cols), lambda l, i: (l, i, 0))

    def kern(g_ref, w_ref, m_ref, v_ref, d_ref, nm_ref, nv_ref):
        d_ref[...], nm_ref[...], nv_ref[...] = _adam_vals(g_ref[...], w_ref[...], m_ref[...], v_ref[...])

    outs = pl.pallas_call(
        kern, name=name, grid=(lead // lead_block, rows // tr), in_specs=[spec] * 4, out_specs=[spec] * 3,
        out_shape=[jax.ShapeDtypeStruct((lead, rows, cols), F32)] * 3, compiler_params=_cparams(("parallel", "parallel")),
    )(*[t.reshape(lead, rows, cols) for t in (g, w, m, v)])
    return [o.reshape(shape) for o in outs]


SMALL = ("attn_norm", "sgu_ln_g", "sgu_ln_b", "sgu_w", "sgu_b", "attn_sinks", "dn_a_log", "dn_dt_bias", "dn_norm",
         "ffn_norm", "final_norm")
SMALL_2D = {"attn_norm": (DEPTH, D_MODEL), "ffn_norm": (DEPTH, D_MODEL), "final_norm": (1, D_MODEL),
            "sgu_ln_g": (DEPTH, MIX), "sgu_ln_b": (DEPTH, MIX), "sgu_w": (DEPTH * SGU_G * SGU_T, SGU_T),
            "sgu_b": (DEPTH * SGU_G, SGU_T), "dn_norm": (DEPTH, DN_HD)}
TINY = ("attn_sinks", "dn_a_log", "dn_dt_bias")


def _pack_tiny(vals, extra=None):
    flat = [vals[k].astype(F32).reshape(-1) for k in TINY] + ([] if extra is None else [extra.astype(F32).reshape(-1)])
    n = sum(f.shape[0] for f in flat)
    return jnp.concatenate(flat + [jnp.zeros((8 * 128 - n,), F32)]).reshape(8, 128)


def _unpack_tiny(tile, shapes):
    flat, out, o = tile.reshape(-1), {}, 0
    for k in TINY:
        n = math.prod(shapes[k])
        out[k] = flat[o:o + n].reshape(shapes[k])
        o += n
    return out, flat[o]


def _in_col_segments():
    shard, padded = IN_COLS // N_CHIPS, BIG_SPEC["w_in"]["cols"]
    segs, mine = [], 0
    for a, n in IN_PIECES:
        o = a
        while o < a + n:
            end = min(a + n, (o // shard + 1) * shard)
            segs.append(((o // shard) * padded + o % shard, mine + o - a, end - o))
            o = end
        mine += n
    return segs


def _move_cols(x, segs, out_cols, name):
    layers, rows, cols = x.shape
    tr = _pick(rows, (256, rows))
    gaps, at = [], 0
    for d, w in sorted((d, w) for _, d, w in segs):
        if d > at:
            gaps.append((at, d - at))
        at = d + w
    if at < out_cols:
        gaps.append((at, out_cols - at))

    def kern(x_ref, o_ref):
        for s, d, w in segs:
            o_ref[0, :, d:d + w] = x_ref[0, :, s:s + w]
        for d, w in gaps:
            o_ref[0, :, d:d + w] = jnp.zeros((tr, w), x.dtype)

    return pl.pallas_call(
        kern, name=name, grid=(layers, rows // tr), in_specs=[pl.BlockSpec((1, tr, cols), lambda l, i: (l, i, 0))],
        out_specs=pl.BlockSpec((1, tr, out_cols), lambda l, i: (l, i, 0)),
        out_shape=jax.ShapeDtypeStruct((layers, rows, out_cols), x.dtype), compiler_params=_cparams(("parallel", "parallel")),
    )(x)


WEIGHTS = ("attn_norm", "w_in", "sgu_ln_g", "sgu_ln_b", "sgu_w", "sgu_b", "attn_sinks", "dn_conv_w", "dn_a_log",
           "dn_dt_bias", "dn_norm", "w_branch", "w_out", "ffn_norm", "w_gate_up", "w_down", "final_norm")


def kernel(x, positions, attn_norm, w_in, sgu_ln_g, sgu_ln_b, sgu_w, sgu_b, attn_sinks, dn_conv_w, dn_a_log, dn_dt_bias, dn_norm, w_branch, w_out, ffn_norm, w_gate_up, w_down, final_norm, loss_target, m_attn_norm, m_w_in, m_sgu_ln_g, m_sgu_ln_b, m_sgu_w, m_sgu_b, m_attn_sinks, m_dn_conv_w, m_dn_a_log, m_dn_dt_bias, m_dn_norm, m_w_branch, m_w_out, m_ffn_norm, m_w_gate_up, m_w_down, m_final_norm, v_attn_norm, v_w_in, v_sgu_ln_g, v_sgu_ln_b, v_sgu_w, v_sgu_b, v_attn_sinks, v_dn_conv_w, v_dn_a_log, v_dn_dt_bias, v_dn_norm, v_w_branch, v_w_out, v_ffn_norm, v_w_gate_up, v_w_down, v_final_norm):
    given = dict(locals())
    W = {k: given[k] for k in WEIGHTS}
    M = {k: given["m_" + k] for k in WEIGHTS}
    V = {k: given["v_" + k] for k in WEIGHTS}
    chip = 2 * lax.axis_index("x") + lax.axis_index("y")
    core = lax.axis_index("c")
    chip1 = chip.astype(jnp.int32).reshape(1)
    where = jnp.stack([chip, core]).astype(jnp.int32)

    placed = [{k: _place_shard(W[k].reshape(DEPTH, BIG_SPEC[k]["rows"], BIG_SPEC[k]["keep"]), k, chip1, l,
                               f"place{l}_{k}") for k in BIG} for l in range(DEPTH)]
    first, conv_full = _gather_layer({"w_in": placed[0]["w_in"]}, dn_conv_w.reshape(CONV_ROWS, CONV_COLS))
    behind = [_gather_start({k: placed[0][k] for k in BIG if k != "w_in"}, conv_full, "0")]
    behind.append(_gather_start(placed[1], behind[0][3], "1"))
    segs = _in_col_segments()

    def arrived(l, after):
        send_sems, recv_sems, inflight, _ = behind[l]
        return _gather_finish(_gather_wait(send_sems, recv_sems, inflight, after, str(l)), str(l))

    def big_of_layer(l, x_l):
        got = {} if l == 0 else arrived(1, x_l)
        w_in = first["w_in"] if l == 0 else got["w_in"]

        def rest(after):
            full = got or arrived(0, after)
            return dict(full, w_branch=full["w_branch"].reshape(3, MIX, D_MODEL))

        return dict(w_in=_move_cols(w_in[None], segs, IN_R, f"w_in_cols{l}")[0], rest=rest)

    w = {k: W[k] for k in SMALL}
    w["attn_norm"] = attn_norm + behind[1][3][0, 0]
    w["dn_conv_w"] = conv_full.reshape(DEPTH, DN_CONV, 3 * MIX)

    core1 = core.astype(jnp.int32).reshape(1)
    back_segs = [(d, s, n) for s, d, n in segs]
    travelling, sums, parts = [], [{}, {}], [{}, {}]

    def on_grads(l, gl):
        gl, tag = dict(gl), f"{l}_{len(gl)}"
        if "w_in" in gl:
            gl["w_in"] = _move_cols(gl["w_in"][None], back_segs, _full_shape("w_in")[1], f"g_in_cols{l}")[0]
        if "w_branch" in gl:
            gl["w_branch"] = gl["w_branch"].reshape(3 * MIX, D_MODEL)
        sibling = _halves_to_sibling(gl, "halves_to_sibling" + tag)
        chip_sums = {k: _add_half(gl[k], sibling[k], k, core1, f"chip_sum{l}_{k}") for k in gl}
        send_sems, recv_sems, inflight, token = _scatter_start(chip_sums, tag)
        travelling.append((l, send_sems, recv_sems, inflight, _present(gl), tag))
        return token

    loss, dx, g = _local_step(x[0], positions[0], loss_target[0], w, big_of_layer, on_grads)

    grads = {}
    conv_2d = (CONV_ROWS, N_CHIPS * CONV_COLS)
    conv_zero = jnp.zeros(conv_2d, F32)
    groups = [tuple(d[k].reshape(SMALL_2D[k]) for d in (g, W, M, V)) for k in SMALL_2D]
    groups.append((g["dn_conv_w"].reshape(conv_2d), conv_zero, conv_zero, conv_zero))
    groups.append((_pack_tiny(g, loss), _pack_tiny(W), _pack_tiny(M), _pack_tiny(V)))
    summed = _allreduce_small_adam(groups)
    delta, new_m, new_v = {}, {}, {}
    for k, outs in zip(SMALL_2D, summed):
        for d, t in zip((grads, delta, new_m, new_v), outs):
            d[k] = t.reshape(W[k].shape)
    conv_sum = summed[len(SMALL_2D)][0].reshape(g["dn_conv_w"].shape)
    grads["dn_conv_w"] = lax.dynamic_slice_in_dim(conv_sum, chip * dn_conv_w.shape[2], dn_conv_w.shape[2], axis=2)
    tiny_shapes = {k: W[k].shape for k in TINY}
    tiny, loss_total = _unpack_tiny(summed[-1][0], tiny_shapes)
    grads.update(tiny)
    for d, t in zip((delta, new_m, new_v), summed[-1][1:]):
        d.update(_unpack_tiny(t, tiny_shapes)[0])

    for l, send_sems, recv_sems, inflight, keys, tag in travelling:
        landed = _scatter_wait(send_sems, recv_sems, inflight, keys, summed[0][0], tag)
        sums[l].update(landed[0])
        parts[l].update(landed[1])
    red = {k: _sum_half(parts[1][k], sums[1][k], k, where, 1, None, f"sum1_{k}") for k in BIG}
    red = {k: _sum_half(parts[0][k], sums[0][k], k, where, 0, red[k], f"sum0_{k}") for k in BIG}
    reduced = _exchange_halves(red)
    grads.update({k: reduced[k].reshape(W[k].shape) for k in BIG})
    for k in ("w_branch", "w_out", "w_gate_up", "w_down", "dn_conv_w"):
        delta[k], new_m[k], new_v[k] = _adam(grads[k], W[k], M[k], V[k], "adam_" + k)
    lead_first = lambda t: jnp.transpose(t, (2, 0, 1))
    outs = _adam(*[lead_first(d["w_in"]) for d in (grads, W, M, V)], "adam_w_in", lead_block=IN_COLS // N_CHIPS // 10)
    delta["w_in"], new_m["w_in"], new_v["w_in"] = (jnp.transpose(o, (1, 2, 0)) for o in outs)

    return (loss_total, dx[None], *[grads[k] for k in WEIGHTS], *[delta[k] for k in WEIGHTS],
            *[new_m[k] for k in WEIGHTS], *[new_v[k] for k in WEIGHTS])
```

```python
import functools
import math

import jax
import jax.numpy as jnp
from jax import lax
from jax.experimental import pallas as pl
from jax.experimental.pallas import tpu as pltpu

F32 = jnp.float32
BF16 = jnp.bfloat16
HI = lax.Precision.HIGHEST

D_MODEL = 1024
DEPTH = 2
MIX = 512
EPS = 1e-6
SGU_G, SGU_T = 4, 128
SWA_H, SWA_KV, SWA_HD, WINDOW = 8, 2, 64, 128
ROPE_THETA, ROPE_DIM = 500000.0, 16
DN_H, DN_HD, DN_CONV, DN_C = 4, 128, 4, 64
D_FF = 2816
IN_COLS = 6920
IN_PIECES = ((3848, 3072), (1792, 1536), (3328, 512), (0, 512), (512, 512), (1024, 512), (1536, 128), (1664, 128),
             (3840, 8))
IN_PAD = 120
IN_R = 7040
C_GATE, C_QKV, C_ZC, C_UA, C_VA, C_QB, C_KB, C_VB, C_SM = 0, 3072, 4608, 5120, 5632, 6144, 6656, 6784, 6912

ADAM_LR, ADAM_B1, ADAM_B2, ADAM_EPS, ADAM_WD, ADAM_STEP = 0.001, 0.9, 0.999, 1e-08, 0.01, 10
VMEM_LIMIT = 56 * 1024 * 1024


def _cparams(sem):
    return pltpu.CompilerParams(dimension_semantics=sem, vmem_limit_bytes=VMEM_LIMIT)


def _dg(a, b, ca, cb, prec=None):
    return lax.dot_general(a, b, (((ca,), (cb,)), ((), ())), precision=prec, preferred_element_type=F32)


def _split(x):
    hi = x.astype(BF16)
    return hi, (x - hi.astype(F32)).astype(BF16)


def _dg3_many(as_, bs, ca, cb):
    sa = [_split(a) for a in as_]
    sb = [_split(b) for b in bs]
    hh = [_dg(a[0], b[0], ca, cb) for a, b in zip(sa, sb)]
    hl = [_dg(a[0], b[1], ca, cb) for a, b in zip(sa, sb)]
    lh = [_dg(a[1], b[0], ca, cb) for a, b in zip(sa, sb)]
    return [x + (y + z) for x, y, z in zip(hh, hl, lh)]


def _dg_exact_lhs_many(a01, bs, ca, cb):
    a = a01.astype(BF16)
    b1 = [b.astype(BF16) for b in bs]
    r1 = [b - t.astype(F32) for b, t in zip(bs, b1)]
    b2 = [r.astype(BF16) for r in r1]
    b3 = [(r - t.astype(F32)).astype(BF16) for r, t in zip(r1, b2)]
    d1 = [_dg(a, t, ca, cb) for t in b1]
    d2 = [_dg(a, t, ca, cb) for t in b2]
    d3 = [_dg(a, t, ca, cb) for t in b3]
    return [x + (y + z) for x, y, z in zip(d1, d2, d3)]


def _mm(a, b):
    return _dg(a.astype(BF16), b.astype(BF16), 1, 0)


def _mm_nt(a, b):
    return _dg(a.astype(BF16), b.astype(BF16), 1, 1)


def _mm_tn(a, b):
    return _dg(a.astype(BF16), b.astype(BF16), 0, 0)


def _sigmoid(x):
    return 0.5 * jnp.tanh(0.5 * x) + 0.5


def _silu(x):
    return x * _sigmoid(x)


def _dsilu(x):
    s = _sigmoid(x)
    return s * (1.0 + x * (1.0 - s))


_GC = math.sqrt(2.0 / math.pi)


def _gelu(x):
    return 0.5 * x * (1.0 + jnp.tanh(_GC * (x + 0.044715 * x * x * x)))


def _dgelu(x):
    t = jnp.tanh(_GC * (x + 0.044715 * x * x * x))
    return 0.5 * (1.0 + t) + 0.5 * x * (1.0 - t * t) * _GC * (1.0 + 3.0 * 0.044715 * x * x)


def _softplus(x):
    return jnp.maximum(x, 0.0) + jnp.log(1.0 + jnp.exp(-jnp.abs(x)))


def _acc(ref, val, i):
    @pl.when(i == 0)
    def _():
        ref[...] = val

    @pl.when(i > 0)
    def _():
        ref[...] += val


def _halo_rows(dtype):
    return 8 * 4 // jnp.dtype(dtype).itemsize


def _tok_call(body, name, S, TB, tok_in, const_in=(), tok_out=(), acc_out=(), prev_in=(), next_in=(), smem_in=()):
    nb = S // TB
    in_specs, args = [], []
    for a, w, cb in tok_in:
        in_specs.append(pl.BlockSpec((TB, w), functools.partial(lambda i, cb: (i, cb), cb=cb)))
        args.append(a)
    for a, w, cb in prev_in:
        hr = _halo_rows(a.dtype)
        in_specs.append(pl.BlockSpec((hr, w), functools.partial(
            lambda i, cb, r: (jnp.maximum(i * r - 1, 0), cb), cb=cb, r=TB // hr)))
        args.append(a)
    for a, w, cb in next_in:
        hr = _halo_rows(a.dtype)
        in_specs.append(pl.BlockSpec((hr, w), functools.partial(
            lambda i, cb, r, last: (jnp.minimum((i + 1) * r, last), cb), cb=cb, r=TB // hr, last=S // hr - 1)))
        args.append(a)
    for a in const_in:
        in_specs.append(pl.BlockSpec(a.shape, lambda i: (0, 0)))
        args.append(a)
    for a in smem_in:
        in_specs.append(pl.BlockSpec(memory_space=pltpu.SMEM))
        args.append(a)
    out_specs, out_shape, aliases, shared = [], [], {}, {}
    for o, (w, dt, *dest) in enumerate(tok_out):
        if not dest:
            out_specs.append(pl.BlockSpec((TB, w), lambda i: (i, 0)))
            out_shape.append(jax.ShapeDtypeStruct((S, w), dt))
            continue
        cb, wide = dest
        out_specs.append(pl.BlockSpec((TB, w), functools.partial(lambda i, cb: (i, cb), cb=cb)))
        out_shape.append(jax.ShapeDtypeStruct((S, wide if isinstance(wide, int) else wide.shape[1]), dt))
        if not isinstance(wide, int):
            if id(wide) not in shared:
                shared[id(wide)] = len(args)
                in_specs.append(pl.BlockSpec(memory_space=pl.ANY))
                args.append(wide)
            aliases[shared[id(wide)]] = o
    for shp, dt in acc_out:
        out_specs.append(pl.BlockSpec(shp, lambda i: (0, 0)))
        out_shape.append(jax.ShapeDtypeStruct(shp, dt))
    n_extra = len(shared)

    def kern(*refs):
        n_in = len(in_specs) - n_extra
        body(pl.program_id(0), *refs[:n_in], *refs[n_in + n_extra:])

    return pl.pallas_call(
        kern, name=name, grid=(nb,), in_specs=in_specs, out_specs=out_specs, out_shape=out_shape,
        input_output_aliases=aliases, compiler_params=_cparams(("arbitrary",)),
    )(*args)


MM_BLOCKS = (1024, 1408, 640, 512, 256, 128)


def _pick(n, cands):
    for c in cands:
        if n % c == 0:
            return c
    return n


MM_VMEM_BUDGET = 44 * 1024 * 1024


def _mm_blocks(M, N, K, a_bytes, b_bytes, o_bytes, add_bytes):
    bn = _pick(N, MM_BLOCKS)
    fits = None
    for bk in [K] + [c for c in (2816, 2048) + MM_BLOCKS if c < K and K % c == 0]:
        for bm in [c for c in (2048,) + MM_BLOCKS if M % c == 0 and c >= min(M, 512)]:
            b_bufs = 1 if (bk == K and bn == N) else 2
            need = 2 * bm * bk * a_bytes + b_bufs * bk * bn * b_bytes + 2 * bm * bn * (o_bytes + add_bytes)
            need += bm * bn * 4 if bk < K else 0
            if need <= MM_VMEM_BUDGET:
                fits = fits or (bm, bn, bk)
                if (M // bm) * (N // bn) * (K // bk) >= 4:
                    return bm, bn, bk
    if fits is None:
        raise ValueError(f"no matmul blocks for {(M, N, K)}")
    return fits


def _matmul(a, b, *, ta=False, tb=False, add=None, out_dtype=F32, name):
    M, K = (a.shape[1], a.shape[0]) if ta else a.shape
    N = b.shape[0] if tb else b.shape[1]
    bm, bn, bk = _mm_blocks(M, N, K, a.dtype.itemsize, b.dtype.itemsize, jnp.dtype(out_dtype).itemsize,
                            0 if add is None else add.dtype.itemsize)
    nk = K // bk
    b_mode = dict(pipeline_mode=pl.Buffered(1)) if (bk == K and bn == N) else {}
    a_spec = pl.BlockSpec((bk, bm), lambda i, j, k: (k, i)) if ta else pl.BlockSpec((bm, bk), lambda i, j, k: (i, k))
    b_spec = (pl.BlockSpec((bn, bk), lambda i, j, k: (j, k), **b_mode) if tb
              else pl.BlockSpec((bk, bn), lambda i, j, k: (k, j), **b_mode))
    o_spec = pl.BlockSpec((bm, bn), lambda i, j, k: (i, j))
    ca, cb = (0 if ta else 1), (1 if tb else 0)

    def kern(*refs):
        a_ref, b_ref = refs[:2]
        add_ref = refs[2] if add is not None else None
        o_ref = refs[3] if add is not None else refs[2]
        p = _dg(a_ref[...].astype(BF16), b_ref[...].astype(BF16), ca, cb)

        def finish(r):
            if add is not None:
                r = r + add_ref[...].astype(F32)
            o_ref[...] = r.astype(out_dtype)

        if nk == 1:
            finish(p)
            return
        acc_ref = refs[-1]
        k = pl.program_id(2)

        @pl.when(k == 0)
        def _():
            acc_ref[...] = p

        @pl.when((k > 0) & (k < nk - 1))
        def _():
            acc_ref[...] += p

        @pl.when(k == nk - 1)
        def _():
            finish(acc_ref[...] + p)

    in_specs = [a_spec, b_spec] + ([o_spec] if add is not None else [])
    args = (a, b) + ((add,) if add is not None else ())
    return pl.pallas_call(
        kern, name=name, grid=(M // bm, N // bn, nk), in_specs=in_specs, out_specs=o_spec,
        out_shape=jax.ShapeDtypeStruct((M, N), out_dtype),
        scratch_shapes=[pltpu.VMEM((bm, bn), F32)] if nk > 1 else [],
        compiler_params=_cparams(("parallel", "parallel", "arbitrary")),
    )(*args)


def _rms_fwd(x, g, name):
    S = x.shape[0]

    def body(i, x_ref, g_ref, h_ref):
        xv = x_ref[...]
        r = lax.rsqrt(jnp.mean(xv * xv, axis=-1, keepdims=True) + EPS)
        h_ref[...] = (xv * r * g_ref[...]).astype(BF16)

    return _tok_call(body, name, S, min(S, 512), [(x, D_MODEL, 0)], [g], [(D_MODEL, BF16)])[0]


def _rms_bwd_vals(xv, g, dh):
    r = lax.rsqrt(jnp.mean(xv * xv, axis=-1, keepdims=True) + EPS)
    u = dh * g
    dx = r * u - xv * (r * r * r) * jnp.mean(u * xv, axis=-1, keepdims=True)
    dg = jnp.sum(dh * xv * r, axis=0, keepdims=True)
    return dx, dg


def _rms_bwd_add(x, g, dh, dres, name):
    S = x.shape[0]

    def body(i, x_ref, dh_ref, dr_ref, g_ref, dx_ref, dg_ref):
        dx, dg = _rms_bwd_vals(x_ref[...], g_ref[...], dh_ref[...].astype(F32))
        dx_ref[...] = dr_ref[...] + dx
        _acc(dg_ref, dg, i)

    return _tok_call(body, name, S, min(S, 512), [(x, D_MODEL, 0), (dh, D_MODEL, 0), (dres, D_MODEL, 0)], [g],
                     [(D_MODEL, F32)], [((1, D_MODEL), F32)])


def _final_loss(x, g, target):
    S = x.shape[0]

    def body(i, x_ref, t_ref, g_ref, dx_ref, loss_ref, dg_ref):
        xv, gv = x_ref[...], g_ref[...]
        r = lax.rsqrt(jnp.mean(xv * xv, axis=-1, keepdims=True) + EPS)
        e = xv * r * gv - t_ref[...]
        part = 0.5 * jnp.sum(jnp.mean(e * e, axis=-1, keepdims=True), axis=0, keepdims=True)
        dx, dg = _rms_bwd_vals(xv, gv, e * (1.0 / D_MODEL))
        dx_ref[...] = dx
        _acc(loss_ref, jnp.broadcast_to(part, (1, 128)), i)
        _acc(dg_ref, dg, i)

    return _tok_call(body, "final_loss", S, min(S, 512), [(x, D_MODEL, 0), (target, D_MODEL, 0)], [g],
                     [(D_MODEL, F32)], [((1, 128), F32), ((1, D_MODEL), F32)])


def _swiglu_fwd(gu, name):
    S = gu.shape[0]

    def body(i, gu_ref, a_ref):
        a_ref[...] = (_silu(gu_ref[:, :D_FF].astype(F32)) * gu_ref[:, D_FF:].astype(F32)).astype(BF16)

    return _tok_call(body, name, S, min(S, 256), [(gu, 2 * D_FF, 0)], [], [(D_FF, BF16)])[0]


def _swiglu_bwd(gu, dact, name):
    S = gu.shape[0]

    def body(i, gu_ref, da_ref, dgu_ref):
        gg, uu, da = gu_ref[:, :D_FF].astype(F32), gu_ref[:, D_FF:].astype(F32), da_ref[...].astype(F32)
        dgu_ref[:, :D_FF] = (da * uu * _dsilu(gg)).astype(BF16)
        dgu_ref[:, D_FF:] = (da * _silu(gg)).astype(BF16)

    return _tok_call(body, name, S, min(S, 256), [(gu, 2 * D_FF, 0), (dact, D_FF, 0)], [], [(2 * D_FF, BF16)])[0]


def _merge_fwd(proj, bds, name):
    S = proj.shape[0]

    def body(i, g0, g1, g2, b0, b1, b2, m_ref):
        m = jnp.zeros(m_ref.shape, F32)
        for gr, br in ((g0, b0), (g1, b1), (g2, b2)):
            m = m + _sigmoid(gr[...].astype(F32)) * br[...].astype(F32)
        m_ref[...] = m.astype(BF16)

    tok = [(proj, D_MODEL, n) for n in range(3)] + [(b, D_MODEL, 0) for b in bds]
    return _tok_call(body, name, S, min(S, 512), tok, [], [(D_MODEL, BF16)])[0]


def _merge_bwd(proj, bds, dm, name):
    S = proj.shape[0]

    def body(i, g0, g1, g2, b0, b1, b2, dm_ref, d0, d1, d2, dgp_ref):
        dmv = dm_ref[...]
        for n, (gr, br, dr) in enumerate(((g0, b0, d0), (g1, b1, d1), (g2, b2, d2))):
            s = _sigmoid(gr[...].astype(F32))
            dr[...] = (dmv * s).astype(BF16)
            dgp_ref[:, n * D_MODEL:(n + 1) * D_MODEL] = (dmv * br[...].astype(F32) * s * (1.0 - s)).astype(BF16)

    tok = [(proj, D_MODEL, n) for n in range(3)] + [(b, D_MODEL, 0) for b in bds] + [(dm, D_MODEL, 0)]
    return _tok_call(body, name, S, min(S, 512), tok, [],
                     [(D_MODEL, BF16)] * 3 + [(3 * D_MODEL, BF16, C_GATE // (3 * D_MODEL), IN_R)])


def _sgu_ln(v, lg, lb):
    mu = jnp.mean(v, axis=-1, keepdims=True)
    vc = v - mu
    rstd = lax.rsqrt(jnp.mean(vc * vc, axis=-1, keepdims=True) + EPS)
    vhat = vc * rstd
    return vhat, rstd, vhat * lg + lb


def _sgu_fwd(proj, lg, lb, wc, bst, name):
    S = proj.shape[0]

    def body(i, ua_ref, va_ref, lg_ref, lb_ref, wc_ref, bs_ref, o_ref):
        u = _gelu(ua_ref[...].astype(F32))
        _, _, vn = _sgu_ln(_gelu(va_ref[...].astype(F32)), lg_ref[...], lb_ref[...])
        for g in range(SGU_G):
            sl = slice(g * 128, (g + 1) * 128)
            mixed = _mm(wc_ref[sl, :], vn[:, sl]) + bs_ref[:, g:g + 1]
            o_ref[:, sl] = (u[:, sl] * mixed).astype(BF16)

    return _tok_call(body, name, S, SGU_T, [(proj, MIX, C_UA // MIX), (proj, MIX, C_VA // MIX)], [lg, lb, wc, bst],
                     [(MIX, BF16)])[0]


def _sgu_bwd(proj, lg, lb, wc, bst, dout, dproj, name):
    S = proj.shape[0]

    def body(i, ua_ref, va_ref, do_ref, lg_ref, lb_ref, wc_ref, bs_ref, duv_ref, dlg_ref, dlb_ref, dwc_ref,
             dbs_ref):
        ua, va, do = ua_ref[...].astype(F32), va_ref[...].astype(F32), do_ref[...].astype(F32)
        u = _gelu(ua)
        lgv = lg_ref[...]
        vhat, rstd, vn = _sgu_ln(_gelu(va), lgv, lb_ref[...])
        tril = lax.broadcasted_iota(jnp.int32, (128, 128), 0) >= lax.broadcasted_iota(jnp.int32, (128, 128), 1)
        lane4 = lax.broadcasted_iota(jnp.int32, (128, 4), 1)
        gs = range(SGU_G)
        sls = [slice(g * 128, (g + 1) * 128) for g in gs]
        wgs = [wc_ref[sl, :] for sl in sls]
        mixed = [_mm(wgs[g], vn[:, sls[g]]) for g in gs]
        dmix = [do[:, sl] * u[:, sl] for sl in sls]
        dwg = [_mm_nt(dmix[g], vn[:, sls[g]]) for g in gs]
        dvn = jnp.concatenate([_mm_tn(wgs[g], dmix[g]) for g in gs], axis=1)
        dbs = jnp.zeros((128, 4), F32)
        for g in gs:
            duv_ref[:, sls[g]] = (do[:, sls[g]] * (mixed[g] + bs_ref[:, g:g + 1]) * _dgelu(ua[:, sls[g]])).astype(BF16)
            dbs = dbs + jnp.where(lane4 == g, jnp.sum(dmix[g], axis=-1, keepdims=True), 0.0)
            _acc(dwc_ref.at[sls[g], :], jnp.where(tril, dwg[g], 0.0), i)
        _acc(dbs_ref, dbs, i)
        _acc(dlg_ref, jnp.sum(dvn * vhat, axis=0, keepdims=True), i)
        _acc(dlb_ref, jnp.sum(dvn, axis=0, keepdims=True), i)
        dvh = dvn * lgv
        dv = rstd * (dvh - jnp.mean(dvh, axis=-1, keepdims=True) - vhat * jnp.mean(dvh * vhat, axis=-1, keepdims=True))
        duv_ref[:, MIX:] = (dv * _dgelu(va)).astype(BF16)

    return _tok_call(body, name, S, SGU_T, [(proj, MIX, C_UA // MIX), (proj, MIX, C_VA // MIX), (dout, MIX, 0)],
                     [lg, lb, wc, bst], [(2 * MIX, BF16, C_UA // (2 * MIX), dproj)],
                     [((1, MIX), F32), ((1, MIX), F32), ((SGU_G * 128, 128), F32), ((128, 4), F32)])


def _rope_tables(positions):
    S = positions.shape[0]
    inv_freq = ROPE_THETA ** (-jnp.arange(0, ROPE_DIM, 2, dtype=F32) / ROPE_DIM)
    ang = positions.astype(F32)[:, None] * inv_freq
    c, s = jnp.cos(ang), jnp.sin(ang)
    c64 = jnp.concatenate([c, c, jnp.ones((S, SWA_HD - ROPE_DIM), F32)], axis=1)
    s64 = jnp.concatenate([-s, s, jnp.zeros((S, SWA_HD - ROPE_DIM), F32)], axis=1)
    return jnp.tile(c64, (1, 2)), jnp.tile(s64, (1, 2))


def _rope128(x, c, s):
    lane = lax.broadcasted_iota(jnp.int32, x.shape, 1) % SWA_HD
    swapped = jnp.where(lane < ROPE_DIM // 2, pltpu.roll(x, 128 - ROPE_DIM // 2, 1), pltpu.roll(x, ROPE_DIM // 2, 1))
    return x * c + swapped * s


def _rope_t128(y, c, s):
    ys = y * s
    lane = lax.broadcasted_iota(jnp.int32, y.shape, 1) % SWA_HD
    swapped = jnp.where(lane < ROPE_DIM // 2, pltpu.roll(ys, 128 - ROPE_DIM // 2, 1), pltpu.roll(ys, ROPE_DIM // 2, 1))
    return y * c + jnp.where(lane < ROPE_DIM, swapped, 0.0)


def _rope_fwd(proj, cos, sin, name):
    S = proj.shape[0]
    scale = SWA_HD ** -0.5

    def body(i, q_ref, k_ref, v_ref, c_ref, s_ref, qo_ref, ko_ref, vo_ref):
        c, s = c_ref[...], s_ref[...]
        for j in range(4):
            sl = slice(j * 128, (j + 1) * 128)
            qo_ref[:, sl] = (_rope128(q_ref[:, sl].astype(F32), c, s) * scale).astype(BF16)
        ko_ref[...] = _rope128(k_ref[...].astype(F32), c, s).astype(BF16)
        vo_ref[...] = v_ref[...].astype(BF16)

    return _tok_call(body, name, S, min(S, 512),
                     [(proj, MIX, C_QB // MIX), (proj, 128, C_KB // 128), (proj, 128, C_VB // 128), (cos, 128, 0),
                      (sin, 128, 0)], [], [(MIX, BF16), (128, BF16), (128, BF16)])


def _rope_bwd(dq, dk, dv, cos, sin, dproj, name):
    S = dq.shape[0]
    scale = SWA_HD ** -0.5
    width = C_SM - C_QB

    def body(i, dq_ref, dk_ref, dv_ref, c_ref, s_ref, o_ref):
        c, s = c_ref[...], s_ref[...]
        for j in range(4):
            sl = slice(j * 128, (j + 1) * 128)
            o_ref[:, sl] = _rope_t128(dq_ref[:, sl] * scale, c, s).astype(BF16)
        o_ref[:, C_KB - C_QB:C_VB - C_QB] = _rope_t128(dk_ref[...], c, s).astype(BF16)
        o_ref[:, C_VB - C_QB:] = dv_ref[...].astype(BF16)

    return _tok_call(body, name, S, min(S, 512),
                     [(dq, MIX, 0), (dk, 128, 0), (dv, 128, 0), (cos, 128, 0), (sin, 128, 0)], [],
                     [(width, BF16, C_QB // width, dproj)])[0]


def _swa_band(i, k_ref, v_ref):
    pstart = pl.multiple_of(jnp.maximum(i - 1, 0) * WINDOW, WINDOW)
    cstart = pl.multiple_of(i * WINDOW, WINDOW)
    kb = jnp.concatenate([k_ref[pl.ds(pstart, WINDOW), :], k_ref[pl.ds(cstart, WINDOW), :]], axis=0)
    vb = jnp.concatenate([v_ref[pl.ds(pstart, WINDOW), :], v_ref[pl.ds(cstart, WINDOW), :]], axis=0)
    qi = lax.broadcasted_iota(jnp.int32, (WINDOW, 2 * WINDOW), 0)
    sj = lax.broadcasted_iota(jnp.int32, (WINDOW, 2 * WINDOW), 1)
    mask = (sj > qi) & (sj <= qi + WINDOW) & ((i > 0) | (sj >= WINDOW))
    return kb, vb, mask, pstart, cstart


def _swa_probs(qs, kh, mask, sinks):
    logits = [jnp.where(mask, _dg(qh, kh, 1, 1), -1e30) for qh in qs]
    m = [jnp.maximum(jnp.max(l, axis=-1, keepdims=True), s) for l, s in zip(logits, sinks)]
    p = [jnp.exp(l - mm) for l, mm in zip(logits, m)]
    ps = [jnp.exp(s - mm) for s, mm in zip(sinks, m)]
    inv = [1.0 / (jnp.sum(pp, axis=-1, keepdims=True) + s) for pp, s in zip(p, ps)]
    return [pp * iv for pp, iv in zip(p, inv)], [s * iv for s, iv in zip(ps, inv)]


def _swa_fwd(q, k, v, sinks, name):
    S = q.shape[0]
    G = SWA_H // SWA_KV

    def body(i, q_ref, k_ref, v_ref, s_ref, o_ref):
        kb, vb, mask, _, _ = _swa_band(i, k_ref, v_ref)
        qv = q_ref[...]
        for kv in range(SWA_KV):
            ksl = slice(kv * SWA_HD, (kv + 1) * SWA_HD)
            heads = range(kv * G, (kv + 1) * G)
            pn, _ = _swa_probs([qv[:, h * SWA_HD:(h + 1) * SWA_HD] for h in heads], kb[:, ksl], mask,
                               [s_ref[0, h] for h in heads])
            outs = [_dg(p.astype(BF16), vb[:, ksl], 1, 0) for p in pn]
            for h, o in zip(heads, outs):
                o_ref[:, h * SWA_HD:(h + 1) * SWA_HD] = o.astype(BF16)

    return _tok_call(body, name, S, WINDOW, [(q, MIX, 0)], [k, v], [(MIX, BF16)], smem_in=[sinks])[0]


def _swa_bwd(q, k, v, sinks, dout, name):
    S = q.shape[0]

    def body(i, q_ref, do_ref, k_ref, v_ref, s_ref, dq_ref, dk_ref, dv_ref, ds_ref):
        kb, vb, mask, pstart, cstart = _swa_band(i, k_ref, v_ref)
        qv, dov = q_ref[...], do_ref[...]
        lane = lax.broadcasted_iota(jnp.int32, (1, 128), 1)
        dsink = jnp.zeros((1, 128), F32)
        dkb, dvb = [], []
        G = SWA_H // SWA_KV
        for kv in range(SWA_KV):
            ksl = slice(kv * SWA_HD, (kv + 1) * SWA_HD)
            heads = range(kv * G, (kv + 1) * G)
            qs = [qv[:, h * SWA_HD:(h + 1) * SWA_HD] for h in heads]
            dos = [dov[:, h * SWA_HD:(h + 1) * SWA_HD].astype(BF16) for h in heads]
            pn, psn = _swa_probs(qs, kb[:, ksl], mask, [s_ref[0, h] for h in heads])
            dp = [_dg(d, vb[:, ksl], 1, 1) for d in dos]
            delta = [jnp.sum(a * b, axis=-1, keepdims=True) for a, b in zip(dp, pn)]
            dsc = [(p * (a - d)).astype(BF16) for p, a, d in zip(pn, dp, delta)]
            dqs = [_dg(s, kb[:, ksl], 1, 0) for s in dsc]
            dks = [_dg(s, qh, 0, 0) for s, qh in zip(dsc, qs)]
            dvs = [_dg(p.astype(BF16), d, 0, 0) for p, d in zip(pn, dos)]
            for n_, h in enumerate(heads):
                dq_ref[:, h * SWA_HD:(h + 1) * SWA_HD] = dqs[n_]
                dsink = dsink + jnp.where(lane == h, -jnp.sum(psn[n_] * delta[n_], axis=0, keepdims=True), 0.0)
            dkb.append((dks[0] + dks[1]) + (dks[2] + dks[3]))
            dvb.append((dvs[0] + dvs[1]) + (dvs[2] + dvs[3]))
        dkb = jnp.concatenate(dkb, axis=1)
        dvb = jnp.concatenate(dvb, axis=1)

        @pl.when(i == 0)
        def _():
            dk_ref[...] = jnp.zeros_like(dk_ref)
            dv_ref[...] = jnp.zeros_like(dv_ref)

        dk_ref[pl.ds(pstart, WINDOW), :] += dkb[:WINDOW]
        dv_ref[pl.ds(pstart, WINDOW), :] += dvb[:WINDOW]
        dk_ref[pl.ds(cstart, WINDOW), :] += dkb[WINDOW:]
        dv_ref[pl.ds(cstart, WINDOW), :] += dvb[WINDOW:]
        _acc(ds_ref, dsink, i)

    return _tok_call(body, name, S, WINDOW, [(q, MIX, 0), (dout, MIX, 0)], [k, v], [(MIX, F32)],
                     [((S, 128), F32), ((S, 128), F32), ((1, 128), F32)], smem_in=[sinks])


def _shift_rows(xs, k):
    return xs if k == 0 else pltpu.roll(xs, k, 0)


def _dn_conv(x_ref, p_ref, w_ref, i):
    hr = p_ref.shape[0]
    halo = jnp.where(i > 0, p_ref[...].astype(F32), 0.0)
    xs = jnp.concatenate([halo, x_ref[...].astype(F32)], axis=0)
    sh = [_shift_rows(xs, DN_CONV - 1 - t)[hr:] for t in range(DN_CONV)]
    pre = sh[0] * w_ref[0:1, :]
    for t in range(1, DN_CONV):
        pre = pre + sh[t] * w_ref[t:t + 1, :]
    return pre, sh


def _dn_gates(sm, alog, dtb):
    lane = lax.broadcasted_iota(jnp.int32, sm.shape, 1)
    return jnp.where(lane < DN_H, _sigmoid(sm), -jnp.exp(alog) * _softplus(sm + dtb))


def _dn_pre_fwd(proj, conv_w, alog_l, dtb_l, name):
    S = proj.shape[0]
    scale = DN_HD ** -0.5

    def body(i, x_ref, sm_ref, p_ref, w_ref, al_ref, db_ref, q_ref, k_ref, v_ref, bg_ref):
        pre, _ = _dn_conv(x_ref, p_ref, w_ref, i)
        a = _silu(pre)
        for h in range(DN_H):
            sl = slice(h * DN_HD, (h + 1) * DN_HD)
            qh, kh = a[:, sl], a[:, MIX + h * DN_HD:MIX + (h + 1) * DN_HD]
            q_ref[:, sl] = qh * (lax.rsqrt(jnp.sum(qh * qh, axis=-1, keepdims=True) + EPS) * scale)
            k_ref[:, sl] = kh * lax.rsqrt(jnp.sum(kh * kh, axis=-1, keepdims=True) + EPS)
        v_ref[...] = a[:, 2 * MIX:]
        bg_ref[...] = _dn_gates(sm_ref[...].astype(F32), al_ref[...], db_ref[...])

    TB = min(S, 256)
    return _tok_call(body, name, S, TB, [(proj, 3 * MIX, C_QKV // (3 * MIX)), (proj, 128, C_SM // 128)],
                     [conv_w, alog_l, dtb_l], [(MIX, F32), (MIX, F32), (MIX, F32), (128, F32)],
                     prev_in=[(proj, 3 * MIX, C_QKV // (3 * MIX))])


def _dn_pre_bwd1(proj, conv_w, alog_l, dtb_l, dq, dk, dv, dbg, dproj, name):
    S = proj.shape[0]
    scale = DN_HD ** -0.5

    def body(i, x_ref, sm_ref, dq_ref, dk_ref, dv_ref, dbg_ref, p_ref, w_ref, al_ref, db_ref, dpre_ref, dsm_ref,
             dw_ref, dal_ref, ddb_ref):
        pre, sh = _dn_conv(x_ref, p_ref, w_ref, i)
        a = _silu(pre)
        da_parts = []
        for part, (g_ref, sc) in enumerate(((dq_ref, scale), (dk_ref, 1.0))):
            for h in range(DN_H):
                xh = a[:, part * MIX + h * DN_HD:part * MIX + (h + 1) * DN_HD]
                rs = lax.rsqrt(jnp.sum(xh * xh, axis=-1, keepdims=True) + EPS)
                y = xh * rs
                dy = g_ref[:, h * DN_HD:(h + 1) * DN_HD] * sc
                da_parts.append(rs * (dy - y * jnp.sum(dy * y, axis=-1, keepdims=True)))
        da_parts.append(dv_ref[...])
        dpre = jnp.concatenate(da_parts, axis=1) * _dsilu(pre)
        dpre_ref[...] = dpre
        dw = jnp.concatenate([jnp.sum(dpre * sh[t], axis=0, keepdims=True) for t in range(DN_CONV)], axis=0)
        _acc(dw_ref, dw, i)
        sm, al, db, dbg_v = sm_ref[...].astype(F32), al_ref[...], db_ref[...], dbg_ref[...]
        lane = lax.broadcasted_iota(jnp.int32, sm.shape, 1)
        sg = _sigmoid(sm)
        gneg = -jnp.exp(al)
        is_g = (lane >= DN_H) & (lane < 2 * DN_H)
        d_al = jnp.where(is_g, dbg_v * gneg * _sigmoid(sm + db), 0.0)
        dsm_ref[...] = jnp.where(lane < DN_H, dbg_v * sg * (1.0 - sg), d_al).astype(BF16)
        _acc(ddb_ref, jnp.sum(d_al, axis=0, keepdims=True), i)
        _acc(dal_ref, jnp.sum(jnp.where(is_g, dbg_v * gneg * _softplus(sm + db), 0.0), axis=0, keepdims=True), i)

    TB = min(S, 256)
    return _tok_call(body, name, S, TB,
                     [(proj, 3 * MIX, C_QKV // (3 * MIX)), (proj, 128, C_SM // 128), (dq, MIX, 0), (dk, MIX, 0),
                      (dv, MIX, 0), (dbg, 128, 0)], [conv_w, alog_l, dtb_l],
                     [(3 * MIX, F32), (128, BF16, C_SM // 128, dproj)],
                     [((DN_CONV, 3 * MIX), F32), ((1, 128), F32), ((1, 128), F32)],
                     prev_in=[(proj, 3 * MIX, C_QKV // (3 * MIX))])


def _dn_pre_bwd2(dpre, conv_w, dproj, name):
    S = dpre.shape[0]
    TB = min(S, 256)
    nb = S // TB

    def body(i, d_ref, n_ref, w_ref, o_ref):
        halo = jnp.where(i < nb - 1, n_ref[...], 0.0)
        ds = jnp.concatenate([d_ref[...], halo], axis=0)
        out = ds[:TB] * w_ref[DN_CONV - 1:DN_CONV, :]
        for t in range(DN_CONV - 1):
            k = DN_CONV - 1 - t
            out = out + pltpu.roll(ds, TB + 8 - k, 0)[:TB] * w_ref[t:t + 1, :]
        o_ref[...] = out.astype(BF16)

    return _tok_call(body, name, S, TB, [(dpre, 3 * MIX, 0)], [conv_w],
                     [(3 * MIX, BF16, C_QKV // (3 * MIX), dproj)], next_in=[(dpre, 3 * MIX, 0)])[0]


def _dn_decay_terms(bgs, heads):
    C = DN_C
    ri = lax.broadcasted_iota(jnp.int32, (C, C), 0)
    ci = lax.broadcasted_iota(jnp.int32, (C, C), 1)
    tril, eye = ri >= ci, ri == ci
    beta = [b[:, h:h + 1] for b, h in zip(bgs, heads)]
    gcol = _dg_exact_lhs_many(tril, [jnp.broadcast_to(b[:, DN_H + h:DN_H + h + 1], (C, C))
                                     for b, h in zip(bgs, heads)], 1, 0)
    grow = [jnp.sum(jnp.where(eye, g, 0.0), axis=0, keepdims=True) for g in gcol]
    decay = [jnp.exp(jnp.where(tril, g - r, -1e30)) for g, r in zip(gcol, grow)]
    e_gc = [jnp.exp(g[:, 0:1]) for g in gcol]
    e_kd = [jnp.exp(g[C - 1:C, 0:1] - g[:, 0:1]) for g in gcol]
    cdec = [jnp.exp(g[C - 1:C, 0:1]) for g in gcol]
    return beta, decay, e_gc, e_kd, cdec


def _dn_nb(S):
    return 4 if S % (4 * DN_C) == 0 else 1


def _dn_prep_fwd(q, k, v, bg, name):
    S = q.shape[0]
    C, NB = DN_C, _dn_nb(S)
    TB = NB * C

    def kern(q_ref, k_ref, v_ref, bg_ref, t_ref, uw_ref, at_ref, qd_ref, kd_ref, dec_ref):
        lane = lax.broadcasted_iota(jnp.int32, (C, 128), 1)
        ri = lax.broadcasted_iota(jnp.int32, (C, C), 0)
        ci = lax.broadcasted_iota(jnp.int32, (C, C), 1)
        tril, eye = ri >= ci, ri == ci
        chains = [(cb, h) for cb in range(NB) for h in range(DN_H)]
        rows = lambda cb: slice(cb * C, (cb + 1) * C)
        head = lambda h: slice(h * DN_HD, (h + 1) * DN_HD)
        beta, decay, e_gc, e_kd, cdec = _dn_decay_terms([bg_ref[rows(cb), :] for cb, _ in chains],
                                                        [h for _, h in chains])
        qs = [q_ref[rows(cb), head(h)] for cb, h in chains]
        ks = [k_ref[rows(cb), head(h)] for cb, h in chains]
        kb = [kh * b for kh, b in zip(ks, beta)]
        x = [-jnp.where(ri > ci, _mm_nt(a, kh) * d, 0.0) for a, kh, d in zip(kb, ks, decay)]
        tm = [jnp.where(eye, 1.0, 0.0) + xi for xi in x]
        p = x
        p = _dg3_many(p, p, 1, 0)
        for it in range(5):
            if it == 4:
                tm = [t + tp for t, tp in zip(tm, _dg3_many(tm, p, 1, 0))]
                break
            both = _dg3_many([jnp.concatenate([t, pp], axis=0) for t, pp in zip(tm, p)], p, 1, 0)
            tm = [t + b[:C] for t, b in zip(tm, both)]
            p = [b[C:] for b in both]
        rhs = [jnp.concatenate([v_ref[rows(cb), head(h)] * b, a * e], axis=1)
               for (cb, h), b, a, e in zip(chains, beta, kb, e_gc)]
        sol = _dg3_many(tm, rhs, 1, 0)
        attn = [_mm_nt(qh, kh) * d for qh, kh, d in zip(qs, ks, decay)]
        for n_, (cb, h) in enumerate(chains):
            rs, sl, hc = rows(cb), head(h), slice(h * C, (h + 1) * C)
            t_ref[rs, hc] = tm[n_]
            uw_ref[rs, sl] = sol[n_][:, :DN_HD]
            uw_ref[rs, MIX + h * DN_HD:MIX + (h + 1) * DN_HD] = sol[n_][:, DN_HD:]
            at_ref[rs, hc] = attn[n_]
            qd_ref[rs, sl] = (qs[n_] * e_gc[n_]).astype(BF16)
            kd_ref[rs, sl] = (ks[n_] * e_kd[n_]).astype(BF16)
        for cb in range(NB):
            dec = jnp.zeros((C, 128), F32)
            for h in range(DN_H):
                dec = dec + jnp.where(lane == h, cdec[cb * DN_H + h], 0.0)
            dec_ref[rows(cb), :] = dec

    tok = lambda w: pl.BlockSpec((TB, w), lambda i: (i, 0))
    return pl.pallas_call(
        kern, name=name, grid=(S // TB,), in_specs=[tok(MIX), tok(MIX), tok(MIX), tok(128)],
        out_specs=[tok(DN_H * C), tok(2 * MIX), tok(DN_H * C), tok(MIX), tok(MIX), tok(128)],
        out_shape=[jax.ShapeDtypeStruct((S, DN_H * C), F32), jax.ShapeDtypeStruct((S, 2 * MIX), F32),
                   jax.ShapeDtypeStruct((S, DN_H * C), F32), jax.ShapeDtypeStruct((S, MIX), BF16),
                   jax.ShapeDtypeStruct((S, MIX), BF16), jax.ShapeDtypeStruct((S, 128), F32)],
        compiler_params=_cparams(("parallel",)),
    )(q, k, v, bg)


def _dn_scan_fwd(uw, at, qd, kd, dec, name):
    S = uw.shape[0]
    C, NB = DN_C, _dn_nb(S)
    TB = NB * C
    SR = DN_H * DN_HD

    def kern(uw_ref, at_ref, qd_ref, kd_ref, dec_ref, o_ref, vn_ref, st_ref, state):
        @pl.when(pl.program_id(0) == 0)
        def _():
            state[...] = jnp.zeros_like(state)

        for cb in range(NB):
            rs = slice(cb * C, (cb + 1) * C)
            hs = range(DN_H)
            sls = [slice(h * DN_HD, (h + 1) * DN_HD) for h in hs]
            s_in = [state[sl, :] for sl in sls]
            ws = [_mm(uw_ref[rs, MIX + h * DN_HD:MIX + (h + 1) * DN_HD], s_in[h]) for h in hs]
            os_ = [_mm(qd_ref[rs, sls[h]], s_in[h]) for h in hs]
            vnew = [uw_ref[rs, sls[h]] - ws[h] for h in hs]
            oa = [_mm(at_ref[rs, h * C:(h + 1) * C], vnew[h]) for h in hs]
            kv = [_mm_tn(kd_ref[rs, sls[h]], vnew[h]) for h in hs]
            for h in hs:
                o_ref[rs, sls[h]] = os_[h] + oa[h]
                state[sls[h], :] = s_in[h] * dec_ref[cb * C:cb * C + 1, h:h + 1] + kv[h]
                st_ref[cb * SR + h * DN_HD:cb * SR + (h + 1) * DN_HD, :] = s_in[h]
                vn_ref[rs, sls[h]] = vnew[h]

    tok = lambda w: pl.BlockSpec((TB, w), lambda i: (i, 0))
    return pl.pallas_call(
        kern, name=name, grid=(S // TB,), in_specs=[tok(2 * MIX), tok(DN_H * C), tok(MIX), tok(MIX), tok(128)],
        out_specs=[tok(MIX), tok(MIX), pl.BlockSpec((NB * SR, DN_HD), lambda i: (i, 0))],
        out_shape=[jax.ShapeDtypeStruct((S, MIX), F32), jax.ShapeDtypeStruct((S, MIX), F32),
                   jax.ShapeDtypeStruct((S // C * SR, DN_HD), F32)],
        scratch_shapes=[pltpu.VMEM((SR, DN_HD), F32)],
        compiler_params=_cparams(("arbitrary",)),
    )(uw, at, qd, kd, dec)


def _dn_core_fwd(q, k, v, bg, name):
    tm, uw, at, qd, kd, dec = _dn_prep_fwd(q, k, v, bg, name + "_prep")
    o, vn, st = _dn_scan_fwd(uw, at, qd, kd, dec, name + "_scan")
    return o, dict(tm=tm, uw=uw, at=at, qd=qd, kd=kd, dec=dec, vn=vn, st=st)


def _dn_scan_bwd(sv, do, name):
    S = do.shape[0]
    C, NB = DN_C, _dn_nb(S)
    TB = NB * C
    SR = DN_H * DN_HD
    nb = S // TB

    def kern(do_ref, uw_ref, at_ref, qd_ref, kd_ref, dec_ref, vn_ref, st_ref, dvn_ref, dw_ref, dkd_ref, dc_ref, dstate):
        @pl.when(pl.program_id(0) == 0)
        def _():
            dstate[...] = jnp.zeros_like(dstate)

        lane = lax.broadcasted_iota(jnp.int32, (C, 128), 1)
        for cb in reversed(range(NB)):
            rs = slice(cb * C, (cb + 1) * C)
            dcrow = jnp.zeros((C, 128), F32)
            for h in range(DN_H):
                sl = slice(h * DN_HD, (h + 1) * DN_HD)
                doh, ds_o = do_ref[rs, sl], dstate[sl, :]
                s_in = st_ref[cb * SR + h * DN_HD:cb * SR + (h + 1) * DN_HD, :]
                d_vnew = _mm_tn(at_ref[rs, h * C:(h + 1) * C], doh) + _mm(kd_ref[rs, sl], ds_o)
                dvn_ref[rs, sl] = d_vnew
                dw_ref[rs, sl] = -_mm_nt(d_vnew, s_in)
                dkd_ref[rs, sl] = _mm_nt(vn_ref[rs, sl], ds_o)
                d_c = jnp.sum(jnp.sum(ds_o * s_in, axis=1, keepdims=True), axis=0, keepdims=True)
                dcrow = dcrow + jnp.where(lane == h, d_c, 0.0)
                dstate[sl, :] = (ds_o * dec_ref[cb * C:cb * C + 1, h:h + 1] + _mm_tn(qd_ref[rs, sl], doh)
                                 - _mm_tn(uw_ref[rs, MIX + h * DN_HD:MIX + (h + 1) * DN_HD], d_vnew))
            dc_ref[rs, :] = dcrow

    tok = lambda w: pl.BlockSpec((TB, w), lambda i: (nb - 1 - i, 0))
    return pl.pallas_call(
        kern, name=name, grid=(nb,),
        in_specs=[tok(MIX), tok(2 * MIX), tok(DN_H * C), tok(MIX), tok(MIX), tok(128), tok(MIX),
                  pl.BlockSpec((NB * SR, DN_HD), lambda i: (nb - 1 - i, 0))],
        out_specs=[tok(MIX), tok(MIX), tok(MIX), tok(128)],
        out_shape=[jax.ShapeDtypeStruct((S, MIX), F32)] * 3 + [jax.ShapeDtypeStruct((S, 128), F32)],
        scratch_shapes=[pltpu.VMEM((SR, DN_HD), F32)],
        compiler_params=_cparams(("arbitrary",)),
    )(do, sv["uw"], sv["at"], sv["qd"], sv["kd"], sv["dec"], sv["vn"], sv["st"])


def _dn_chunk_bwd(q, k, v, bg, sv, do, dvn, dw, dkd, dc, name):
    S = q.shape[0]
    C, NB = DN_C, _dn_nb(S)
    TB = NB * C
    SR = DN_H * DN_HD

    def kern(q_ref, k_ref, v_ref, bg_ref, t_ref, uw_ref, vn_ref, st_ref, do_ref, dvn_ref, dw_ref, dkd_ref, dc_ref,
             dq_ref, dk_ref, dv_ref, dbg_ref):
        lane = lax.broadcasted_iota(jnp.int32, (C, 128), 1)
        ri = lax.broadcasted_iota(jnp.int32, (C, C), 0)
        ci = lax.broadcasted_iota(jnp.int32, (C, C), 1)
        tril, eye, last = ri >= ci, ri == ci, ri[:, 0:1] == C - 1
        chains = [(cb, h) for cb in range(NB) for h in range(DN_H)]
        each = lambda f, *ls: [f(*a) for a in zip(*ls)]
        rsum = lambda t: jnp.sum(t, axis=-1, keepdims=True)
        rows = lambda cb: slice(cb * C, (cb + 1) * C)
        head = lambda h: slice(h * DN_HD, (h + 1) * DN_HD)
        tok = lambda ref: [ref[rows(cb), head(h)] for cb, h in chains]
        beta, decay, e_gc, e_kd, cdec = _dn_decay_terms([bg_ref[rows(cb), :] for cb, _ in chains],
                                                        [h for _, h in chains])
        qs, ks, vs, dos, vnew, d_kd = tok(q_ref), tok(k_ref), tok(v_ref), tok(do_ref), tok(vn_ref), tok(dkd_ref)
        s_in = [st_ref[cb * SR + h * DN_HD:cb * SR + (h + 1) * DN_HD, :] for cb, h in chains]
        d_c = [dc_ref[cb * C:cb * C + 1, h:h + 1] for cb, h in chains]
        kb = each(lambda a, b: a * b, ks, beta)
        kk = each(_mm_nt, kb, ks)
        attn = each(lambda a, b, d: _mm_nt(a, b) * d, qs, ks, decay)
        d_qd = each(_mm_nt, dos, s_in)
        d_attn = each(_mm_nt, dos, vnew)
        d_sol = [jnp.concatenate([dvn_ref[rows(cb), head(h)], dw_ref[rows(cb), head(h)]], axis=1) for cb, h in chains]
        sol = [jnp.concatenate([uw_ref[rows(cb), head(h)], uw_ref[rows(cb), MIX + h * DN_HD:MIX + (h + 1) * DN_HD]],
                               axis=1) for cb, h in chains]
        d_rhs = _dg3_many([t_ref[rows(cb), h * C:(h + 1) * C] for cb, h in chains], d_sol, 0, 0)
        d_a = _dg3_many(d_rhs, sol, 1, 1)
        d_kk = each(lambda a, d: jnp.where(ri > ci, -a, 0.0) * d, d_a, decay)
        d_qk = each(lambda a, d: a * d, d_attn, decay)
        dm = each(lambda a, b, c_, d: a * b + c_ * d, d_kk, kk, d_attn, attn)
        d_vb = [t[:, :DN_HD] for t in d_rhs]
        dz = [t[:, DN_HD:] for t in d_rhs]
        d_kb = each(lambda z, e, a, kh: z * e + _mm(a, kh), dz, e_gc, d_kk, ks)
        d_k = each(lambda a, b, c_, q: _mm_tn(a, b) + _mm_tn(c_, q), d_kk, kb, d_qk, qs)
        d_q = each(lambda a, kh, b, e: _mm(a, kh) + b * e, d_qk, ks, d_qd, e_gc)
        t_kd = each(lambda a, kh, e: rsum(a * kh * e), d_kd, ks, e_kd)
        d_gl = each(lambda t, c_, cd: jnp.sum(t, axis=0, keepdims=True) + c_ * cd, t_kd, d_c, cdec)
        d_gc = each(lambda z, a, e, m, b, q, t, gl:
                    rsum(z * a) * e + rsum(m) - rsum(jnp.where(eye, jnp.sum(m, axis=0, keepdims=True), 0.0))
                    + rsum(b * q) * e - t + jnp.where(last, gl, 0.0),
                    dz, kb, e_gc, dm, d_qd, qs, t_kd, d_gl)
        d_g = _dg_exact_lhs_many(ri <= ci, [jnp.broadcast_to(t, (C, 128)) for t in d_gc], 1, 0)
        d_beta = each(lambda a, v_, b, kh: rsum(a * v_) + rsum(b * kh), d_vb, vs, d_kb, ks)
        for n_, (cb, h) in enumerate(chains):
            dq_ref[rows(cb), head(h)] = d_q[n_]
            dk_ref[rows(cb), head(h)] = d_k[n_] + d_kd[n_] * e_kd[n_] + d_kb[n_] * beta[n_]
            dv_ref[rows(cb), head(h)] = d_vb[n_] * beta[n_]
        for cb in range(NB):
            dbg = jnp.zeros((C, 128), F32)
            for h in range(DN_H):
                n_ = cb * DN_H + h
                dbg = dbg + jnp.where(lane == h, d_beta[n_], 0.0) + jnp.where(lane == DN_H + h, d_g[n_], 0.0)
            dbg_ref[rows(cb), :] = dbg

    tok = lambda w: pl.BlockSpec((TB, w), lambda i: (i, 0))
    return pl.pallas_call(
        kern, name=name, grid=(S // TB,),
        in_specs=[tok(MIX), tok(MIX), tok(MIX), tok(128), tok(DN_H * C), tok(2 * MIX), tok(MIX),
                  pl.BlockSpec((NB * SR, DN_HD), lambda i: (i, 0)), tok(MIX), tok(MIX), tok(MIX), tok(MIX), tok(128)],
        out_specs=[tok(MIX), tok(MIX), tok(MIX), tok(128)],
        out_shape=[jax.ShapeDtypeStruct((S, MIX), F32)] * 3 + [jax.ShapeDtypeStruct((S, 128), F32)],
        compiler_params=_cparams(("parallel",)),
    )(q, k, v, bg, sv["tm"], sv["uw"], sv["vn"], sv["st"], do, dvn, dw, dkd, dc)


def _dn_core_bwd(q, k, v, bg, sv, do, name):
    dvn, dw, dkd, dc = _dn_scan_bwd(sv, do, name + "_scan")
    return _dn_chunk_bwd(q, k, v, bg, sv, do, dvn, dw, dkd, dc, name + "_chunk")


def _dn_post_fwd(o, proj, ng, name):
    S = o.shape[0]

    def body(i, o_ref, z_ref, g_ref, out_ref):
        gv = g_ref[...]
        for h in range(DN_H):
            sl = slice(h * DN_HD, (h + 1) * DN_HD)
            oh = o_ref[:, sl]
            r = lax.rsqrt(jnp.mean(oh * oh, axis=-1, keepdims=True) + EPS)
            out_ref[:, sl] = (oh * r * gv * _silu(z_ref[:, sl].astype(F32))).astype(BF16)

    return _tok_call(body, name, S, min(S, 512), [(o, MIX, 0), (proj, MIX, C_ZC // MIX)], [ng], [(MIX, BF16)])[0]


def _dn_post_bwd(o, proj, ng, dout, dproj, name):
    S = o.shape[0]

    def body(i, o_ref, z_ref, do_ref, g_ref, dov_ref, dz_ref, dg_ref):
        gv = g_ref[...]
        dg = jnp.zeros((1, DN_HD), F32)
        for h in range(DN_H):
            sl = slice(h * DN_HD, (h + 1) * DN_HD)
            oh, zh, dh = o_ref[:, sl], z_ref[:, sl].astype(F32), do_ref[:, sl].astype(F32)
            r = lax.rsqrt(jnp.mean(oh * oh, axis=-1, keepdims=True) + EPS)
            dz_ref[:, sl] = (dh * oh * r * gv * _dsilu(zh)).astype(BF16)
            dx, dgh = _rms_bwd_vals(oh, gv, dh * _silu(zh))
            dov_ref[:, sl] = dx
            dg = dg + dgh
        _acc(dg_ref, dg, i)

    return _tok_call(body, name, S, min(S, 512), [(o, MIX, 0), (proj, MIX, C_ZC // MIX), (dout, MIX, 0)], [ng],
                     [(MIX, F32), (MIX, BF16, C_ZC // MIX, dproj)], [((1, DN_HD), F32)])


def _layer_params(w, big, l):
    lane = jnp.arange(128)
    is_g = (lane >= DN_H) & (lane < 2 * DN_H)
    spread = lambda t: jnp.where(is_g, jnp.tile(t, 128 // DN_H), 0.0).reshape(1, 128)
    tril = jnp.tril(jnp.ones((SGU_T, SGU_T), bool))
    return dict(
        win=big["w_in"], rest=big["rest"], conv=w["dn_conv_w"][l], attn_norm=w["attn_norm"][l].reshape(1, -1), ffn_norm=w["ffn_norm"][l].reshape(1, -1),
        lg=w["sgu_ln_g"][l].reshape(1, -1), lb=w["sgu_ln_b"][l].reshape(1, -1),
        wc=jnp.where(tril, w["sgu_w"][l], 0.0).reshape(SGU_G * SGU_T, SGU_T), bst=w["sgu_b"][l].T,
        sinks=w["attn_sinks"][l].reshape(1, -1), alog=spread(w["dn_a_log"][l]), dtb=spread(w["dn_dt_bias"][l]),
        ng=w["dn_norm"][l].reshape(1, -1))


def _layer_fwd(x, p, cos, sin, l):
    n = lambda s: f"l{l}_{s}"
    h = _rms_fwd(x, p["attn_norm"], n("rms1"))
    proj = _matmul(h, p["win"], out_dtype=BF16, name=n("mm_in"))
    out_a = _sgu_fwd(proj, p["lg"], p["lb"], p["wc"], p["bst"], n("sgu_fwd"))
    qr, kr, vr = _rope_fwd(proj, cos, sin, n("rope_fwd"))
    out_b = _swa_fwd(qr, kr, vr, p["sinks"], n("swa_fwd"))
    q, k, v, bg = _dn_pre_fwd(proj, p["conv"], p["alog"], p["dtb"], n("dn_pre_fwd"))
    o, dn = _dn_core_fwd(q, k, v, bg, n("dn_core_fwd"))
    out_c = _dn_post_fwd(o, proj, p["ng"], n("dn_post_fwd"))
    outs = (out_a, out_b, out_c)
    rest = p.pop("rest")(out_c)
    p.update(wb=rest["w_branch"], wout=rest["w_out"], wgu=rest["w_gate_up"], wdown=rest["w_down"])
    bds = [_matmul(outs[j], p["wb"][j], out_dtype=BF16, name=n(f"mm_branch{j}")) for j in range(3)]
    merged = _merge_fwd(proj, bds, n("merge_fwd"))
    x1 = _matmul(merged, p["wout"], add=x, name=n("mm_out"))
    h2 = _rms_fwd(x1, p["ffn_norm"], n("rms2"))
    gu = _matmul(h2, p["wgu"], out_dtype=BF16, name=n("mm_gu"))
    act = _swiglu_fwd(gu, n("swiglu_fwd"))
    x2 = _matmul(act, p["wdown"], add=x1, name=n("mm_down"))
    saved = dict(x=x, h=h, proj=proj, outs=outs, qr=qr, kr=kr, vr=vr, q=q, k=k, v=v, bg=bg, o=o, dn=dn, bds=bds,
                 merged=merged, x1=x1, h2=h2, gu=gu, act=act)
    return x2, saved


def _layer_bwd(dx2, s, p, cos, sin, l, early=None):
    n = lambda t: f"l{l}_{t}"
    proj = s["proj"]
    g = {}
    g["w_down"] = _matmul(s["act"], dx2, ta=True, out_dtype=BF16, name=n("wg_down"))
    dact = _matmul(dx2, p["wdown"], tb=True, out_dtype=BF16, name=n("dg_down"))
    dgu = _swiglu_bwd(s["gu"], dact, n("swiglu_bwd"))
    g["w_gate_up"] = _matmul(s["h2"], dgu, ta=True, out_dtype=BF16, name=n("wg_gu"))
    dh2 = _matmul(dgu, p["wgu"], tb=True, name=n("dg_gu"))
    dx1, g["ffn_norm"] = _rms_bwd_add(s["x1"], p["ffn_norm"], dh2, dx2, n("rms2_bwd"))
    g["w_out"] = _matmul(s["merged"], dx1, ta=True, out_dtype=BF16, name=n("wg_out"))
    dm = _matmul(dx1, p["wout"], tb=True, name=n("dg_out"))
    dbd0, dbd1, dbd2, dproj = _merge_bwd(proj, s["bds"], dm, n("merge_bwd"))
    dbds = (dbd0, dbd1, dbd2)
    g["w_branch"] = jnp.stack([_matmul(s["outs"][j], dbds[j], ta=True, out_dtype=BF16, name=n(f"wg_branch{j}"))
                               for j in range(3)])
    douts = [_matmul(dbds[j], p["wb"][j], tb=True, name=n(f"dg_branch{j}")) for j in range(3)]
    lg = p["lg"]
    if early is not None:
        token = early({k: g.pop(k) for k in ("w_down", "w_gate_up", "w_out", "w_branch")})
        lg = lg if token is None else lg + token[0, 0]
    dproj, g["sgu_ln_g"], g["sgu_ln_b"], dwc, dbs = _sgu_bwd(proj, lg, p["lb"], p["wc"], p["bst"], douts[0], dproj,
                                                             n("sgu_bwd"))
    g["sgu_w"] = dwc.reshape(SGU_G, SGU_T, SGU_T)
    g["sgu_b"] = dbs.T
    dqr, dkr, dvr, dsink = _swa_bwd(s["qr"], s["kr"], s["vr"], p["sinks"], douts[1], n("swa_bwd"))
    g["attn_sinks"] = dsink[0, :SWA_H]
    dproj = _rope_bwd(dqr, dkr, dvr, cos, sin, dproj, n("rope_bwd"))
    do, dproj, dng = _dn_post_bwd(s["o"], proj, p["ng"], douts[2], dproj, n("dn_post_bwd"))
    g["dn_norm"] = dng[0]
    dq, dk, dv, dbg = _dn_core_bwd(s["q"], s["k"], s["v"], s["bg"], s["dn"], do, n("dn_core_bwd"))
    dpre, dproj, g["dn_conv_w"], dal, ddb = _dn_pre_bwd1(proj, p["conv"], p["alog"], p["dtb"], dq, dk, dv, dbg, dproj,
                                                         n("dn_pre_bwd1"))
    g["dn_a_log"] = dal[0, DN_H:2 * DN_H]
    g["dn_dt_bias"] = ddb[0, DN_H:2 * DN_H]
    dproj = _dn_pre_bwd2(dpre, p["conv"], dproj, n("dn_pre_bwd2"))
    g["w_in"] = _matmul(s["h"], dproj, ta=True, out_dtype=BF16, name=n("wg_in"))
    dh = _matmul(dproj, p["win"], tb=True, name=n("dg_in"))
    dx, g["attn_norm"] = _rms_bwd_add(s["x"], p["attn_norm"], dh, dx1, n("rms1_bwd"))
    g["attn_norm"], g["ffn_norm"] = g["attn_norm"][0], g["ffn_norm"][0]
    g["sgu_ln_g"], g["sgu_ln_b"] = g["sgu_ln_g"][0], g["sgu_ln_b"][0]
    return dx, g


def _local_step(x, positions, target, w, big_of_layer, on_grads):
    cos, sin = _rope_tables(positions)
    params, saves, xs = [], [], x
    for l in range(DEPTH):
        params.append(_layer_params(w, big_of_layer(l, xs), l))
        xs, sv = _layer_fwd(xs, params[l], cos, sin, l)
        saves.append(sv)
    dx, loss_row, dgf = _final_loss(xs, w["final_norm"].reshape(1, -1), target)
    grads = [None] * DEPTH
    for l in reversed(range(DEPTH)):
        early = functools.partial(on_grads, l) if l == 0 else None
        dx, grads[l] = _layer_bwd(dx, saves[l], params[l], cos, sin, l, early)
        token = on_grads(l, {k: grads[l].pop(k) for k in BIG if k in grads[l]})
        if token is not None and l > 0:
            params[l - 1] = dict(params[l - 1], ffn_norm=params[l - 1]["ffn_norm"] + token[0, 0])
    stacked = {k: jnp.stack([grads[l][k] for l in range(DEPTH)]) for k in grads[0]}
    stacked["final_norm"] = dgf[0]
    return loss_row[0, 0], dx, stacked


MESH = pl.DeviceIdType.MESH
HBM_SPEC = pl.BlockSpec(memory_space=pltpu.HBM)
VMEM_SPEC = pl.BlockSpec(memory_space=pltpu.VMEM)
N_CHIPS = 4
FLIPS = tuple((fx, fy, fc) for fx in (0, 1) for fy in (0, 1) for fc in (0, 1))[1:]
BIG = ("w_in", "w_branch", "w_out", "w_gate_up", "w_down")
BIG_SPEC = {
    "w_in": dict(rows=1024, cols=1792, axis=1, keep=1730, down=8),
    "w_branch": dict(rows=1536, cols=256, axis=1, keep=256, down=2),
    "w_out": dict(rows=256, cols=1024, axis=0, keep=1024, down=1),
    "w_gate_up": dict(rows=1024, cols=1408, axis=1, keep=1408, down=8),
    "w_down": dict(rows=704, cols=1024, axis=0, keep=1024, down=4),
}
CONV_ROWS, CONV_COLS = DEPTH * DN_CONV, 3 * MIX // N_CHIPS


def _full_shape(k):
    sp = BIG_SPEC[k]
    return (sp["rows"], N_CHIPS * sp["cols"]) if sp["axis"] == 1 else (N_CHIPS * sp["rows"], sp["cols"])


def _me():
    return lax.axis_index("x"), lax.axis_index("y"), lax.axis_index("c")


def _peer(x, y, c, flip):
    fx, fy, fc = flip
    return (1 - x if fx else x, 1 - y if fy else y, 1 - c if fc else c)


class _Copies:
    def __init__(self, send_sems, recv_sems):
        self.send_sems, self.recv_sems, self.k, self.sent, self.landing = send_sems, recv_sems, 0, [], []

    def _copy(self, k, src, dst, to):
        return pltpu.make_async_remote_copy(src_ref=src, dst_ref=dst, send_sem=self.send_sems.at[k],
                                            recv_sem=self.recv_sems.at[k], device_id=to, device_id_type=MESH)

    def send(self, src, dst, to, lands):
        k = self.k
        self.k += 1
        cp = self._copy(k, src, dst, to)
        cp.start()
        self.sent.append(cp)
        self.landing.append(self._copy(k, lands, lands, to))
        return k

    def wait_landed(self, k):
        self.landing[k].wait_recv()

    def finish(self, landed=()):
        for k, cp in enumerate(self.landing):
            if k not in landed:
                cp.wait_recv()
        for cp in self.sent:
            cp.wait_send()


def _place_shard(shard, k, chip, layer, name):
    sp = BIG_SPEC[k]
    rows, cols, keep = sp["rows"], sp["cols"], sp["keep"]
    tr = _pick(rows, (256, 64))
    nb = rows // tr
    if sp["axis"] == 1:
        out_spec = pl.BlockSpec((tr, cols), lambda i, ch: (i, ch[0]))
    else:
        out_spec = pl.BlockSpec((tr, cols), lambda i, ch: (ch[0] * nb + i, 0))

    def kern(ch_ref, x_ref, o_ref):
        v = x_ref[0].astype(BF16)
        if keep == cols:
            o_ref[...] = v
        else:
            o_ref[:, :keep] = v
            o_ref[:, keep:] = jnp.zeros((tr, cols - keep), BF16)

    return pl.pallas_call(
        kern, name=name, out_shape=jax.ShapeDtypeStruct(_full_shape(k), BF16),
        grid_spec=pltpu.PrefetchScalarGridSpec(
            num_scalar_prefetch=1, grid=(nb,),
            in_specs=[pl.BlockSpec((1, tr, keep), lambda i, ch: (layer, i, 0))], out_specs=out_spec),
        compiler_params=_cparams(("parallel",)),
    )(chip, shard)


def _half_block(ref, k, s, half):
    sp = BIG_SPEC[k]
    hr = sp["rows"] // 2
    if sp["axis"] == 1:
        return ref.at[pl.ds(pl.multiple_of(half * hr, 16), hr), pl.ds(pl.multiple_of(s * sp["cols"], 128), sp["cols"])]
    return ref.at[pl.ds(pl.multiple_of(s * sp["rows"] + half * hr, 16), hr), :]


def _other_chips(x, y):
    return [(1 - x, y), (x, 1 - y), (1 - x, 1 - y)]


ALL_BIG = BIG


def _present(d):
    return tuple(k for k in ALL_BIG if k in d)


def _gather_layer(placed, conv):
    BIG = _present(placed)
    n = len(BIG)
    n_sem = 6 * n + 3

    def body(*refs):
        conv_ref = refs[n]
        out = dict(zip(BIG, refs[n + 1:2 * n + 1]))
        conv_out, send_sems, recv_sems, local_sem = refs[2 * n + 1:]
        x, y, c = _me()
        me = 2 * x + y
        chips = _other_chips(x, y)
        net = _Copies(send_sems, recv_sems)

        def conv_block(s):
            return conv_out.at[:, pl.ds(pl.multiple_of(s * CONV_COLS, 128), CONV_COLS)]

        local = pltpu.make_async_copy(conv_ref, conv_block(me), local_sem)
        local.start()
        first = {}
        for k in BIG:
            for j, (px, py) in enumerate(chips):
                first[k, j] = net.send(_half_block(out[k], k, me, c), _half_block(out[k], k, me, c), (px, py, c),
                                       _half_block(out[k], k, 2 * px + py, c))
        for px, py in chips:
            net.send(conv_ref, conv_block(me), (px, py, c), conv_block(2 * px + py))
        for k in BIG:
            for j, (px, py) in enumerate(chips):
                net.wait_landed(first[k, j])
                net.send(_half_block(out[k], k, 2 * px + py, c), _half_block(out[k], k, 2 * px + py, c), (x, y, 1 - c),
                         _half_block(out[k], k, 2 * px + py, 1 - c))
        net.finish(landed=set(first.values()))
        local.wait()

    out_shape = [jax.ShapeDtypeStruct(_full_shape(k), BF16) for k in BIG]
    out_shape.append(jax.ShapeDtypeStruct((CONV_ROWS, N_CHIPS * CONV_COLS), F32))
    outs = pl.pallas_call(
        body, name="gather_layer", out_shape=out_shape, in_specs=[HBM_SPEC] * (n + 1), out_specs=[HBM_SPEC] * (n + 1),
        input_output_aliases={i: i for i in range(n)},
        scratch_shapes=[pltpu.SemaphoreType.DMA((n_sem,)), pltpu.SemaphoreType.DMA((n_sem,)), pltpu.SemaphoreType.DMA],
    )(*[placed[k] for k in BIG], conv)
    return dict(zip(BIG, outs[:n])), outs[n]


SEM_SPEC = pl.BlockSpec(memory_space=pltpu.SEMAPHORE)


def _behind_copies(arrs, send_sems, recv_sems):
    x, y, c = _me()
    copies = []
    for i, k in enumerate(_present(arrs)):
        for j, (px, py) in enumerate(_other_chips(x, y)):
            copies.append(pltpu.make_async_remote_copy(
                src_ref=_half_block(arrs[k], k, 2 * x + y, c), dst_ref=_half_block(arrs[k], k, 2 * x + y, c),
                send_sem=send_sems.at[3 * i + j], recv_sem=recv_sems.at[3 * i + j], device_id=(px, py, c),
                device_id_type=MESH))
    return copies


def _gather_start(placed, after, tag):
    BIG = _present(placed)
    n = len(BIG)
    N_BEHIND = 3 * n

    def body(*refs):
        arrs = dict(zip(BIG, refs[n + 3:2 * n + 3]))
        send_sems, recv_sems = refs[n + 1], refs[n + 2]
        for cp in _behind_copies(arrs, send_sems, recv_sems):
            cp.start()
        refs[2 * n + 3][...] = jnp.zeros((8, 128), F32)

    outs = pl.pallas_call(
        body, name="gather_start" + tag,
        out_shape=(pltpu.SemaphoreType.DMA((N_BEHIND,)), pltpu.SemaphoreType.DMA((N_BEHIND,)),
                   *[pltpu.HBM(_full_shape(k), BF16) for k in BIG], jax.ShapeDtypeStruct((8, 128), F32)),
        in_specs=[HBM_SPEC] * n + [pl.BlockSpec(memory_space=pl.ANY)],
        out_specs=(SEM_SPEC, SEM_SPEC, *[HBM_SPEC] * n, VMEM_SPEC),
        input_output_aliases={i: i + 2 for i in range(n)},
        compiler_params=pltpu.CompilerParams(has_side_effects=pltpu.SideEffectType.DATAFLOW_SIDE_EFFECTING),
    )(*[pltpu.with_memory_space_constraint(placed[k], pltpu.HBM) for k in BIG], after)
    return outs[0], outs[1], dict(zip(BIG, outs[2:n + 2])), outs[n + 2]


def _gather_wait(send_sems, recv_sems, inflight, after, tag):
    BIG = _present(inflight)
    n = len(BIG)

    def body(*refs):
        arrs = dict(zip(BIG, refs[:n]))
        for cp in _behind_copies(arrs, refs[n], refs[n + 1]):
            cp.wait_send()
            cp.wait_recv()

    outs = pl.pallas_call(
        body, name="gather_wait" + tag, out_shape=tuple(pltpu.HBM(_full_shape(k), BF16) for k in BIG),
        in_specs=[HBM_SPEC] * n + [SEM_SPEC, SEM_SPEC, pl.BlockSpec(memory_space=pl.ANY)], out_specs=(HBM_SPEC,) * n,
        input_output_aliases={i: i for i in range(n)},
        compiler_params=pltpu.CompilerParams(has_side_effects=pltpu.SideEffectType.DATAFLOW_SIDE_EFFECTING),
    )(*[inflight[k] for k in BIG], send_sems, recv_sems, after)
    return dict(zip(BIG, outs))


def _gather_finish(arrs, tag):
    BIG = _present(arrs)
    n = len(BIG)
    N_BEHIND = 3 * n

    def body(*refs):
        out = dict(zip(BIG, refs[n:2 * n]))
        send_sems, recv_sems = refs[2 * n:]
        x, y, c = _me()
        net = _Copies(send_sems, recv_sems)
        for k in BIG:
            for px, py in _other_chips(x, y):
                net.send(_half_block(out[k], k, 2 * px + py, c), _half_block(out[k], k, 2 * px + py, c), (x, y, 1 - c),
                         _half_block(out[k], k, 2 * px + py, 1 - c))
        net.finish()

    outs = pl.pallas_call(
        body, name="gather_finish" + tag, out_shape=[jax.ShapeDtypeStruct(_full_shape(k), BF16) for k in BIG],
        in_specs=[HBM_SPEC] * n, out_specs=[HBM_SPEC] * n, input_output_aliases={i: i for i in range(n)},
        scratch_shapes=[pltpu.SemaphoreType.DMA((N_BEHIND,)), pltpu.SemaphoreType.DMA((N_BEHIND,))],
    )(*[arrs[k] for k in BIG])
    return dict(zip(BIG, outs))


def _row_chunks(ref, rows, n):
    step = rows // n
    return [ref.at[pl.ds(i * step, step), :] for i in range(n)]


def _half_pieces(ref, k, half):
    sp = BIG_SPEC[k]
    hr = sp["rows"] // 2
    if sp["axis"] == 1:
        return [ref.at[pl.ds(pl.multiple_of(half * hr, 16), hr), :]]
    return [ref.at[pl.ds(pl.multiple_of(s * sp["rows"] + half * hr, 16), hr), :] for s in range(N_CHIPS)]


def _half_shape(k):
    rows, cols = _full_shape(k)
    return rows // 2, cols


def _stacked_pieces(ref, k):
    sp = BIG_SPEC[k]
    hr = sp["rows"] // 2
    return [ref] if sp["axis"] == 1 else [ref.at[pl.ds(s * hr, hr), :] for s in range(N_CHIPS)]


def _chip_part(ref, k, s):
    sp = BIG_SPEC[k]
    hr = sp["rows"] // 2
    if sp["axis"] == 1:
        return ref.at[:, pl.ds(pl.multiple_of(s * sp["cols"], 128), sp["cols"])]
    return ref.at[pl.ds(pl.multiple_of(s * hr, 16), hr), :]


def _halves_to_sibling(grads, name):
    BIG = _present(grads)
    n = len(BIG)
    chunks = {k: max(BIG_SPEC[k]["down"] // 2, 1) if BIG_SPEC[k]["axis"] == 1 else 1 for k in BIG}
    n_sem = sum(chunks[k] if BIG_SPEC[k]["axis"] == 1 else N_CHIPS for k in BIG)

    def body(*refs):
        g = dict(zip(BIG, refs[:n]))
        out = dict(zip(BIG, refs[n:2 * n]))
        send_sems, recv_sems = refs[2 * n:]
        x, y, c = _me()
        net = _Copies(send_sems, recv_sems)
        for k in BIG:
            hr = BIG_SPEC[k]["rows"] // 2
            for src, dst in zip(_half_pieces(g[k], k, 1 - c), _stacked_pieces(out[k], k)):
                for s, d in zip(_row_chunks(src, hr, chunks[k]), _row_chunks(dst, hr, chunks[k])):
                    net.send(s, d, (x, y, 1 - c), d)
        net.finish()

    outs = pl.pallas_call(
        body, name=name, out_shape=[jax.ShapeDtypeStruct(_half_shape(k), BF16) for k in BIG],
        in_specs=[HBM_SPEC] * n, out_specs=[HBM_SPEC] * n,
        scratch_shapes=[pltpu.SemaphoreType.DMA((n_sem,)), pltpu.SemaphoreType.DMA((n_sem,))],
    )(*[grads[k] for k in BIG])
    return dict(zip(BIG, outs))


def _add_half(g, other, k, core, name):
    sp = BIG_SPEC[k]
    hr, cols = sp["rows"] // 2, _full_shape(k)[1]
    tr = _pick(hr, (256, 352, 128))
    nb = hr // tr
    if sp["axis"] == 1:
        grid = (nb,)
        g_spec = pl.BlockSpec((tr, cols), lambda i, c: (c[0] * nb + i, 0))
        h_spec = pl.BlockSpec((tr, cols), lambda i, c: (i, 0))
    else:
        grid = (N_CHIPS, nb)
        g_spec = pl.BlockSpec((tr, cols), lambda s, i, c: ((2 * s + c[0]) * nb + i, 0))
        h_spec = pl.BlockSpec((tr, cols), lambda s, i, c: (s * nb + i, 0))

    def kern(c_ref, a_ref, b_ref, o_ref):
        o_ref[...] = (a_ref[...].astype(F32) + b_ref[...].astype(F32)).astype(BF16)

    return pl.pallas_call(
        kern, name=name, out_shape=jax.ShapeDtypeStruct(_half_shape(k), BF16),
        grid_spec=pltpu.PrefetchScalarGridSpec(num_scalar_prefetch=1, grid=grid, in_specs=[g_spec, h_spec],
                                               out_specs=h_spec),
        compiler_params=_cparams(("parallel",) * len(grid)),
    )(core, g, other)


def _part_shape(k):
    return N_CHIPS - 1, BIG_SPEC[k]["rows"] // 2, BIG_SPEC[k]["cols"]


def _scatter_copies(sums, parts, send_sems, recv_sems):
    x, y, c = _me()
    copies = []
    for i, k in enumerate(_present(sums)):
        for j, (px, py) in enumerate(_other_chips(x, y)):
            copies.append(pltpu.make_async_remote_copy(
                src_ref=_chip_part(sums[k], k, 2 * px + py), dst_ref=parts[k].at[j], send_sem=send_sems.at[3 * i + j],
                recv_sem=recv_sems.at[3 * i + j], device_id=(px, py, c), device_id_type=MESH))
    return copies


def _scatter_start(sums, tag):
    BIG = _present(sums)
    n = len(BIG)
    N_BEHIND = 3 * n
    lands = [pltpu.with_memory_space_constraint(lax.empty(_part_shape(k), BF16), pltpu.HBM) for k in BIG]

    def body(*refs):
        outs = refs[2 * n + 2:4 * n + 2]
        for cp in _scatter_copies(dict(zip(BIG, outs[:n])), dict(zip(BIG, outs[n:])), refs[2 * n], refs[2 * n + 1]):
            cp.start()
        refs[4 * n + 2][...] = jnp.zeros((8, 128), F32)

    outs = pl.pallas_call(
        body, name="scatter_start" + tag,
        out_shape=(pltpu.SemaphoreType.DMA((N_BEHIND,)), pltpu.SemaphoreType.DMA((N_BEHIND,)),
                   *[pltpu.HBM(_half_shape(k), BF16) for k in BIG], *[pltpu.HBM(_part_shape(k), BF16) for k in BIG],
                   jax.ShapeDtypeStruct((8, 128), F32)),
        in_specs=[HBM_SPEC] * (2 * n), out_specs=(SEM_SPEC, SEM_SPEC, *[HBM_SPEC] * (2 * n), VMEM_SPEC),
        input_output_aliases={i: i + 2 for i in range(2 * n)},
        compiler_params=pltpu.CompilerParams(has_side_effects=pltpu.SideEffectType.DATAFLOW_SIDE_EFFECTING),
    )(*[pltpu.with_memory_space_constraint(sums[k], pltpu.HBM) for k in BIG], *lands)
    return outs[0], outs[1], outs[2:2 * n + 2], outs[2 * n + 2]


def _scatter_wait(send_sems, recv_sems, inflight, keys, after, tag):
    BIG = keys
    n = len(BIG)

    def body(*refs):
        for cp in _scatter_copies(dict(zip(BIG, refs[:n])), dict(zip(BIG, refs[n:2 * n])), refs[2 * n], refs[2 * n + 1]):
            cp.wait_send()
            cp.wait_recv()

    outs = pl.pallas_call(
        body, name="scatter_wait" + tag,
        out_shape=(*[pltpu.HBM(_half_shape(k), BF16) for k in BIG], *[pltpu.HBM(_part_shape(k), BF16) for k in BIG]),
        in_specs=[HBM_SPEC] * (2 * n) + [SEM_SPEC, SEM_SPEC, pl.BlockSpec(memory_space=pl.ANY)],
        out_specs=(HBM_SPEC,) * (2 * n), input_output_aliases={i: i for i in range(2 * n)},
        compiler_params=pltpu.CompilerParams(has_side_effects=pltpu.SideEffectType.DATAFLOW_SIDE_EFFECTING),
    )(*inflight, send_sems, recv_sems, after)
    return dict(zip(BIG, outs[:n])), dict(zip(BIG, outs[n:]))


def _sum_half(parts, own, k, where, layer, into, name):
    sp = BIG_SPEC[k]
    rows, cols, keep = sp["rows"], sp["cols"], sp["keep"]
    hr = rows // 2
    tr = _pick(hr, (256, 352, 128))
    nb = hr // tr
    if sp["axis"] == 1:
        own_spec = pl.BlockSpec((tr, cols), lambda i, w: (i, w[0]))
    else:
        own_spec = pl.BlockSpec((tr, cols), lambda i, w: (w[0] * nb + i, 0))

    def kern(w_ref, p_ref, own_ref, *rest):
        tot = own_ref[...].astype(F32)
        for j in range(N_CHIPS - 1):
            tot = tot + p_ref[j].astype(F32)
        rest[-1][0] = tot[:, :keep]

    in_specs = [pl.BlockSpec((N_CHIPS - 1, tr, cols), lambda i, w: (0, i, 0)), own_spec]
    args = [where, parts, own]
    if into is not None:
        in_specs.append(pl.BlockSpec(memory_space=pl.ANY))
        args.append(into)
    return pl.pallas_call(
        kern, name=name, out_shape=jax.ShapeDtypeStruct((DEPTH, rows, keep), F32),
        grid_spec=pltpu.PrefetchScalarGridSpec(
            num_scalar_prefetch=1, grid=(nb,), in_specs=in_specs,
            out_specs=pl.BlockSpec((1, tr, keep), lambda i, w: (layer, w[1] * nb + i, 0))),
        input_output_aliases={} if into is None else {3: 0},
        compiler_params=_cparams(("parallel",)),
    )(*args)


def _exchange_halves(red):
    n = len(BIG)

    def body(*refs):
        out = dict(zip(BIG, refs[n:2 * n]))
        send_sems, recv_sems = refs[2 * n:]
        x, y, c = _me()
        net = _Copies(send_sems, recv_sems)
        for k in BIG:
            hr = BIG_SPEC[k]["rows"] // 2
            for l in range(DEPTH):
                mine = out[k].at[l, pl.ds(pl.multiple_of(c * hr, 8), hr), :]
                theirs = out[k].at[l, pl.ds(pl.multiple_of((1 - c) * hr, 8), hr), :]
                net.send(mine, mine, (x, y, 1 - c), theirs)
        net.finish()

    outs = pl.pallas_call(
        body, name="exchange_halves",
        out_shape=[jax.ShapeDtypeStruct((DEPTH, BIG_SPEC[k]["rows"], BIG_SPEC[k]["keep"]), F32) for k in BIG],
        in_specs=[HBM_SPEC] * n, out_specs=[HBM_SPEC] * n, input_output_aliases={i: i for i in range(n)},
        scratch_shapes=[pltpu.SemaphoreType.DMA((DEPTH * n,)), pltpu.SemaphoreType.DMA((DEPTH * n,))],
    )(*[red[k] for k in BIG])
    return dict(zip(BIG, outs))


def _adam_vals(g, w, m, v):
    m2 = ADAM_B1 * m + (1.0 - ADAM_B1) * g
    v2 = ADAM_B2 * v + (1.0 - ADAM_B2) * (g * g)
    m_hat = m2 / (1.0 - ADAM_B1 ** ADAM_STEP)
    v_hat = v2 / (1.0 - ADAM_B2 ** ADAM_STEP)
    return -ADAM_LR * (m_hat / (jnp.sqrt(v_hat) + ADAM_EPS) + ADAM_WD * w), m2, v2


def _allreduce_small_adam(groups):
    ng = len(groups)

    def body(*refs):
        ins = [refs[4 * i:4 * i + 4] for i in range(ng)]
        outs = [refs[4 * ng + 4 * i:4 * ng + 4 * i + 4] for i in range(ng)]
        bufs = refs[8 * ng:9 * ng]
        send_sems, recv_sems = refs[9 * ng:]
        x, y, c = _me()
        me = 4 * x + 2 * y + c
        net = _Copies(send_sems, recv_sems)
        for (g_ref, _, _, _), buf in zip(ins, bufs):
            buf[me] = g_ref[...]
            for f in FLIPS:
                px, py, pc = _peer(x, y, c, f)
                net.send(g_ref, buf.at[me], (px, py, pc), buf.at[4 * px + 2 * py + pc])
        net.finish()
        for (_, w_ref, m_ref, v_ref), (gs_ref, d_ref, nm_ref, nv_ref), buf in zip(ins, outs, bufs):
            tot = buf[0]
            for d in range(1, 8):
                tot = tot + buf[d]
            gs_ref[...] = tot
            d_ref[...], nm_ref[...], nv_ref[...] = _adam_vals(tot, w_ref[...], m_ref[...], v_ref[...])

    shapes = [jax.ShapeDtypeStruct(g[0].shape, F32) for g in groups for _ in range(4)]
    outs = pl.pallas_call(
        body, name="allreduce_small", out_shape=shapes, in_specs=[VMEM_SPEC] * (4 * ng), out_specs=[VMEM_SPEC] * (4 * ng),
        scratch_shapes=[pltpu.VMEM((8,) + g[0].shape, F32) for g in groups]
        + [pltpu.SemaphoreType.DMA((7 * ng,)), pltpu.SemaphoreType.DMA((7 * ng,))],
        compiler_params=pltpu.CompilerParams(vmem_limit_bytes=VMEM_LIMIT),
    )(*[t for g in groups for t in g])
    return [outs[4 * i:4 * i + 4] for i in range(ng)]


def _adam(g, w, m, v, name, lead_block=1):
    shape = w.shape
    lead, rows, cols = math.prod(shape[:-2]), shape[-2], shape[-1]
    tr = _pick(rows, (256, 352, 64, 8, rows))
    spec = pl.BlockSpec((lead_block, tr, cols), lambda l, i: (l, i, 0))

    def kern(g_ref, w_ref, m_ref, v_ref, d_ref, nm_ref, nv_ref):
        d_ref[...], nm_ref[...], nv_ref[...] = _adam_vals(g_ref[...], w_ref[...], m_ref[...], v_ref[...])

    outs = pl.pallas_call(
        kern, name=name, grid=(lead // lead_block, rows // tr), in_specs=[spec] * 4, out_specs=[spec] * 3,
        out_shape=[jax.ShapeDtypeStruct((lead, rows, cols), F32)] * 3, compiler_params=_cparams(("parallel", "parallel")),
    )(*[t.reshape(lead, rows, cols) for t in (g, w, m, v)])
    return [o.reshape(shape) for o in outs]


SMALL = ("attn_norm", "sgu_ln_g", "sgu_ln_b", "sgu_w", "sgu_b", "attn_sinks", "dn_a_log", "dn_dt_bias", "dn_norm",
         "ffn_norm", "final_norm")
SMALL_2D = {"attn_norm": (DEPTH, D_MODEL), "ffn_norm": (DEPTH, D_MODEL), "final_norm": (1, D_MODEL),
            "sgu_ln_g": (DEPTH, MIX), "sgu_ln_b": (DEPTH, MIX), "sgu_w": (DEPTH * SGU_G * SGU_T, SGU_T),
            "sgu_b": (DEPTH * SGU_G, SGU_T), "dn_norm": (DEPTH, DN_HD)}
TINY = ("attn_sinks", "dn_a_log", "dn_dt_bias")


def _pack_tiny(vals, extra=None):
    flat = [vals[k].astype(F32).reshape(-1) for k in TINY] + ([] if extra is None else [extra.astype(F32).reshape(-1)])
    n = sum(f.shape[0] for f in flat)
    return jnp.concatenate(flat + [jnp.zeros((8 * 128 - n,), F32)]).reshape(8, 128)


def _unpack_tiny(tile, shapes):
    flat, out, o = tile.reshape(-1), {}, 0
    for k in TINY:
        n = math.prod(shapes[k])
        out[k] = flat[o:o + n].reshape(shapes[k])
        o += n
    return out, flat[o]


def _in_col_segments():
    shard, padded = IN_COLS // N_CHIPS, BIG_SPEC["w_in"]["cols"]
    segs, mine = [], 0
    for a, n in IN_PIECES:
        o = a
        while o < a + n:
            end = min(a + n, (o // shard + 1) * shard)
            segs.append(((o // shard) * padded + o % shard, mine + o - a, end - o))
            o = end
        mine += n
    return segs


def _move_cols(x, segs, out_cols, name):
    layers, rows, cols = x.shape
    tr = _pick(rows, (256, rows))
    gaps, at = [], 0
    for d, w in sorted((d, w) for _, d, w in segs):
        if d > at:
            gaps.append((at, d - at))
        at = d + w
    if at < out_cols:
        gaps.append((at, out_cols - at))

    def kern(x_ref, o_ref):
        for s, d, w in segs:
            o_ref[0, :, d:d + w] = x_ref[0, :, s:s + w]
        for d, w in gaps:
            o_ref[0, :, d:d + w] = jnp.zeros((tr, w), x.dtype)

    return pl.pallas_call(
        kern, name=name, grid=(layers, rows // tr), in_specs=[pl.BlockSpec((1, tr, cols), lambda l, i: (l, i, 0))],
        out_specs=pl.BlockSpec((1, tr, out_cols), lambda l, i: (l, i, 0)),
        out_shape=jax.ShapeDtypeStruct((layers, rows, out_cols), x.dtype), compiler_params=_cparams(("parallel", "parallel")),
    )(x)


WEIGHTS = ("attn_norm", "w_in", "sgu_ln_g", "sgu_ln_b", "sgu_w", "sgu_b", "attn_sinks", "dn_conv_w", "dn_a_log",
           "dn_dt_bias", "dn_norm", "w_branch", "w_out", "ffn_norm", "w_gate_up", "w_down", "final_norm")


def kernel(x, positions, attn_norm, w_in, sgu_ln_g, sgu_ln_b, sgu_w, sgu_b, attn_sinks, dn_conv_w, dn_a_log, dn_dt_bias, dn_norm, w_branch, w_out, ffn_norm, w_gate_up, w_down, final_norm, loss_target, m_attn_norm, m_w_in, m_sgu_ln_g, m_sgu_ln_b, m_sgu_w, m_sgu_b, m_attn_sinks, m_dn_conv_w, m_dn_a_log, m_dn_dt_bias, m_dn_norm, m_w_branch, m_w_out, m_ffn_norm, m_w_gate_up, m_w_down, m_final_norm, v_attn_norm, v_w_in, v_sgu_ln_g, v_sgu_ln_b, v_sgu_w, v_sgu_b, v_attn_sinks, v_dn_conv_w, v_dn_a_log, v_dn_dt_bias, v_dn_norm, v_w_branch, v_w_out, v_ffn_norm, v_w_gate_up, v_w_down, v_final_norm):
    given = dict(locals())
    W = {k: given[k] for k in WEIGHTS}
    M = {k: given["m_" + k] for k in WEIGHTS}
    V = {k: given["v_" + k] for k in WEIGHTS}
    chip = 2 * lax.axis_index("x") + lax.axis_index("y")
    core = lax.axis_index("c")
    chip1 = chip.astype(jnp.int32).reshape(1)
    where = jnp.stack([chip, core]).astype(jnp.int32)

    placed = [{k: _place_shard(W[k].reshape(DEPTH, BIG_SPEC[k]["rows"], BIG_SPEC[k]["keep"]), k, chip1, l,
                               f"place{l}_{k}") for k in BIG} for l in range(DEPTH)]
    first, conv_full = _gather_layer({"w_in": placed[0]["w_in"]}, dn_conv_w.reshape(CONV_ROWS, CONV_COLS))
    behind = [_gather_start({k: placed[0][k] for k in BIG if k != "w_in"}, conv_full, "0")]
    behind.append(_gather_start(placed[1], behind[0][3], "1"))
    segs = _in_col_segments()

    def arrived(l, after):
        send_sems, recv_sems, inflight, _ = behind[l]
        return _gather_finish(_gather_wait(send_sems, recv_sems, inflight, after, str(l)), str(l))

    def big_of_layer(l, x_l):
        got = {} if l == 0 else arrived(1, x_l)
        w_in = first["w_in"] if l == 0 else got["w_in"]

        def rest(after):
            full = got or arrived(0, after)
            return dict(full, w_branch=full["w_branch"].reshape(3, MIX, D_MODEL))

        return dict(w_in=_move_cols(w_in[None], segs, IN_R, f"w_in_cols{l}")[0], rest=rest)

    w = {k: W[k] for k in SMALL}
    w["attn_norm"] = attn_norm + behind[1][3][0, 0]
    w["dn_conv_w"] = conv_full.reshape(DEPTH, DN_CONV, 3 * MIX)

    core1 = core.astype(jnp.int32).reshape(1)
    back_segs = [(d, s, n) for s, d, n in segs]
    travelling, started, sums, parts = [], [], [{}, {}], [{}, {}]

    def on_grads(l, gl):
        gl, tag = dict(gl), f"{l}_{len(gl)}"
        if "w_in" in gl:
            gl["w_in"] = _move_cols(gl["w_in"][None], back_segs, _full_shape("w_in")[1], f"g_in_cols{l}")[0]
        if "w_branch" in gl:
            gl["w_branch"] = gl["w_branch"].reshape(3 * MIX, D_MODEL)
        sibling = _halves_to_sibling(gl, "halves_to_sibling" + tag)
        chip_sums = {k: _add_half(gl[k], sibling[k], k, core1, f"chip_sum{l}_{k}") for k in gl}
        send_sems, recv_sems, inflight, token = _scatter_start(chip_sums, tag)
        travelling.append((l, send_sems, recv_sems, inflight, _present(gl), tag))
        started.append(token)
        return token

    loss, dx, g = _local_step(x[0], positions[0], loss_target[0], w, big_of_layer, on_grads)

    grads = {}
    conv_2d = (CONV_ROWS, N_CHIPS * CONV_COLS)
    conv_zero = jnp.zeros(conv_2d, F32)
    groups = [tuple(d[k].reshape(SMALL_2D[k]) for d in (g, W, M, V)) for k in SMALL_2D]
    groups.append((g["dn_conv_w"].reshape(conv_2d), conv_zero, conv_zero, conv_zero))
    loss = loss + started[-1][0, 0]
    groups.append((_pack_tiny(g, loss), _pack_tiny(W), _pack_tiny(M), _pack_tiny(V)))
    summed = _allreduce_small_adam(groups)
    delta, new_m, new_v = {}, {}, {}
    for k, outs in zip(SMALL_2D, summed):
        for d, t in zip((grads, delta, new_m, new_v), outs):
            d[k] = t.reshape(W[k].shape)
    conv_sum = summed[len(SMALL_2D)][0].reshape(g["dn_conv_w"].shape)
    grads["dn_conv_w"] = lax.dynamic_slice_in_dim(conv_sum, chip * dn_conv_w.shape[2], dn_conv_w.shape[2], axis=2)
    tiny_shapes = {k: W[k].shape for k in TINY}
    tiny, loss_total = _unpack_tiny(summed[-1][0], tiny_shapes)
    grads.update(tiny)
    for d, t in zip((delta, new_m, new_v), summed[-1][1:]):
        d.update(_unpack_tiny(t, tiny_shapes)[0])

    for l, send_sems, recv_sems, inflight, keys, tag in travelling:
        landed = _scatter_wait(send_sems, recv_sems, inflight, keys, summed[0][0], tag)
        sums[l].update(landed[0])
        parts[l].update(landed[1])
    red = {k: _sum_half(parts[1][k], sums[1][k], k, where, 1, None, f"sum1_{k}") for k in BIG}
    red = {k: _sum_half(parts[0][k], sums[0][k], k, where, 0, red[k], f"sum0_{k}") for k in BIG}
    reduced = _exchange_halves(red)
    grads.update({k: reduced[k].reshape(W[k].shape) for k in BIG})
    for k in ("w_branch", "w_out", "w_gate_up", "w_down", "dn_conv_w"):
        delta[k], new_m[k], new_v[k] = _adam(grads[k], W[k], M[k], V[k], "adam_" + k)
    lead_first = lambda t: jnp.transpose(t, (2, 0, 1))
    outs = _adam(*[lead_first(d["w_in"]) for d in (grads, W, M, V)], "adam_w_in", lead_block=IN_COLS // N_CHIPS // 10)
    delta["w_in"], new_m["w_in"], new_v["w_in"] = (jnp.transpose(o, (1, 2, 0)) for o in outs)

    return (loss_total, dx[None], *[grads[k] for k in WEIGHTS], *[delta[k] for k in WEIGHTS],
            *[new_m[k] for k in WEIGHTS], *[new_v[k] for k in WEIGHTS])
```

```python
import functools
import math

import jax
import jax.numpy as jnp
from jax import lax
from jax.experimental import pallas as pl
from jax.experimental.pallas import tpu as pltpu

F32 = jnp.float32
BF16 = jnp.bfloat16
HI = lax.Precision.HIGHEST

D_MODEL = 1024
DEPTH = 2
MIX = 512
EPS = 1e-6
SGU_G, SGU_T = 4, 128
SWA_H, SWA_KV, SWA_HD, WINDOW = 8, 2, 64, 128
ROPE_THETA, ROPE_DIM = 500000.0, 16
DN_H, DN_HD, DN_CONV, DN_C = 4, 128, 4, 64
D_FF = 2816
IN_COLS = 6920
IN_PIECES = ((3848, 3072), (1792, 1536), (3328, 512), (0, 512), (512, 512), (1024, 512), (1536, 128), (1664, 128),
             (3840, 8))
IN_PAD = 120
IN_R = 7040
C_GATE, C_QKV, C_ZC, C_UA, C_VA, C_QB, C_KB, C_VB, C_SM = 0, 3072, 4608, 5120, 5632, 6144, 6656, 6784, 6912

ADAM_LR, ADAM_B1, ADAM_B2, ADAM_EPS, ADAM_WD, ADAM_STEP = 0.001, 0.9, 0.999, 1e-08, 0.01, 10
VMEM_LIMIT = 56 * 1024 * 1024


def _cparams(sem):
    return pltpu.CompilerParams(dimension_semantics=sem, vmem_limit_bytes=VMEM_LIMIT)


def _dg(a, b, ca, cb, prec=None):
    return lax.dot_general(a, b, (((ca,), (cb,)), ((), ())), precision=prec, preferred_element_type=F32)


def _split(x):
    hi = x.astype(BF16)
    return hi, (x - hi.astype(F32)).astype(BF16)


def _dg3_many(as_, bs, ca, cb):
    sa = [_split(a) for a in as_]
    sb = [_split(b) for b in bs]
    hh = [_dg(a[0], b[0], ca, cb) for a, b in zip(sa, sb)]
    hl = [_dg(a[0], b[1], ca, cb) for a, b in zip(sa, sb)]
    lh = [_dg(a[1], b[0], ca, cb) for a, b in zip(sa, sb)]
    return [x + (y + z) for x, y, z in zip(hh, hl, lh)]


def _dg_exact_lhs_many(a01, bs, ca, cb):
    a = a01.astype(BF16)
    b1 = [b.astype(BF16) for b in bs]
    r1 = [b - t.astype(F32) for b, t in zip(bs, b1)]
    b2 = [r.astype(BF16) for r in r1]
    b3 = [(r - t.astype(F32)).astype(BF16) for r, t in zip(r1, b2)]
    d1 = [_dg(a, t, ca, cb) for t in b1]
    d2 = [_dg(a, t, ca, cb) for t in b2]
    d3 = [_dg(a, t, ca, cb) for t in b3]
    return [x + (y + z) for x, y, z in zip(d1, d2, d3)]


def _mm(a, b):
    return _dg(a.astype(BF16), b.astype(BF16), 1, 0)


def _mm_nt(a, b):
    return _dg(a.astype(BF16), b.astype(BF16), 1, 1)


def _mm_tn(a, b):
    return _dg(a.astype(BF16), b.astype(BF16), 0, 0)


def _sigmoid(x):
    return 0.5 * jnp.tanh(0.5 * x) + 0.5


def _silu(x):
    return x * _sigmoid(x)


def _dsilu(x):
    s = _sigmoid(x)
    return s * (1.0 + x * (1.0 - s))


_GC = math.sqrt(2.0 / math.pi)


def _gelu(x):
    return 0.5 * x * (1.0 + jnp.tanh(_GC * (x + 0.044715 * x * x * x)))


def _dgelu(x):
    t = jnp.tanh(_GC * (x + 0.044715 * x * x * x))
    return 0.5 * (1.0 + t) + 0.5 * x * (1.0 - t * t) * _GC * (1.0 + 3.0 * 0.044715 * x * x)


def _softplus(x):
    return jnp.maximum(x, 0.0) + jnp.log(1.0 + jnp.exp(-jnp.abs(x)))


def _acc(ref, val, i):
    @pl.when(i == 0)
    def _():
        ref[...] = val

    @pl.when(i > 0)
    def _():
        ref[...] += val


def _halo_rows(dtype):
    return 8 * 4 // jnp.dtype(dtype).itemsize


def _tok_call(body, name, S, TB, tok_in, const_in=(), tok_out=(), acc_out=(), prev_in=(), next_in=(), smem_in=()):
    nb = S // TB
    in_specs, args = [], []
    for a, w, cb in tok_in:
        in_specs.append(pl.BlockSpec((TB, w), functools.partial(lambda i, cb: (i, cb), cb=cb)))
        args.append(a)
    for a, w, cb in prev_in:
        hr = _halo_rows(a.dtype)
        in_specs.append(pl.BlockSpec((hr, w), functools.partial(
            lambda i, cb, r: (jnp.maximum(i * r - 1, 0), cb), cb=cb, r=TB // hr)))
        args.append(a)
    for a, w, cb in next_in:
        hr = _halo_rows(a.dtype)
        in_specs.append(pl.BlockSpec((hr, w), functools.partial(
            lambda i, cb, r, last: (jnp.minimum((i + 1) * r, last), cb), cb=cb, r=TB // hr, last=S // hr - 1)))
        args.append(a)
    for a in const_in:
        in_specs.append(pl.BlockSpec(a.shape, lambda i: (0, 0)))
        args.append(a)
    for a in smem_in:
        in_specs.append(pl.BlockSpec(memory_space=pltpu.SMEM))
        args.append(a)
    out_specs, out_shape, aliases, shared = [], [], {}, {}
    for o, (w, dt, *dest) in enumerate(tok_out):
        if not dest:
            out_specs.append(pl.BlockSpec((TB, w), lambda i: (i, 0)))
            out_shape.append(jax.ShapeDtypeStruct((S, w), dt))
            continue
        cb, wide = dest
        out_specs.append(pl.BlockSpec((TB, w), functools.partial(lambda i, cb: (i, cb), cb=cb)))
        out_shape.append(jax.ShapeDtypeStruct((S, wide if isinstance(wide, int) else wide.shape[1]), dt))
        if not isinstance(wide, int):
            if id(wide) not in shared:
                shared[id(wide)] = len(args)
                in_specs.append(pl.BlockSpec(memory_space=pl.ANY))
                args.append(wide)
            aliases[shared[id(wide)]] = o
    for shp, dt in acc_out:
        out_specs.append(pl.BlockSpec(shp, lambda i: (0, 0)))
        out_shape.append(jax.ShapeDtypeStruct(shp, dt))
    n_extra = len(shared)

    def kern(*refs):
        n_in = len(in_specs) - n_extra
        body(pl.program_id(0), *refs[:n_in], *refs[n_in + n_extra:])

    return pl.pallas_call(
        kern, name=name, grid=(nb,), in_specs=in_specs, out_specs=out_specs, out_shape=out_shape,
        input_output_aliases=aliases, compiler_params=_cparams(("arbitrary",)),
    )(*args)


MM_BLOCKS = (1024, 1408, 640, 512, 256, 128)


def _pick(n, cands):
    for c in cands:
        if n % c == 0:
            return c
    return n


MM_VMEM_BUDGET = 44 * 1024 * 1024


def _mm_blocks(M, N, K, a_bytes, b_bytes, o_bytes, add_bytes):
    bn = _pick(N, MM_BLOCKS)
    fits = None
    for bk in [K] + [c for c in (2816, 2048) + MM_BLOCKS if c < K and K % c == 0]:
        for bm in [c for c in (2048,) + MM_BLOCKS if M % c == 0 and c >= min(M, 512)]:
            b_bufs = 1 if (bk == K and bn == N) else 2
            need = 2 * bm * bk * a_bytes + b_bufs * bk * bn * b_bytes + 2 * bm * bn * (o_bytes + add_bytes)
            need += bm * bn * 4 if bk < K else 0
            if need <= MM_VMEM_BUDGET:
                fits = fits or (bm, bn, bk)
                if (M // bm) * (N // bn) * (K // bk) >= 4:
                    return bm, bn, bk
    if fits is None:
        raise ValueError(f"no matmul blocks for {(M, N, K)}")
    return fits


def _matmul(a, b, *, ta=False, tb=False, add=None, out_dtype=F32, name):
    M, K = (a.shape[1], a.shape[0]) if ta else a.shape
    N = b.shape[0] if tb else b.shape[1]
    bm, bn, bk = _mm_blocks(M, N, K, a.dtype.itemsize, b.dtype.itemsize, jnp.dtype(out_dtype).itemsize,
                            0 if add is None else add.dtype.itemsize)
    nk = K // bk
    b_mode = dict(pipeline_mode=pl.Buffered(1)) if (bk == K and bn == N) else {}
    a_spec = pl.BlockSpec((bk, bm), lambda i, j, k: (k, i)) if ta else pl.BlockSpec((bm, bk), lambda i, j, k: (i, k))
    b_spec = (pl.BlockSpec((bn, bk), lambda i, j, k: (j, k), **b_mode) if tb
              else pl.BlockSpec((bk, bn), lambda i, j, k: (k, j), **b_mode))
    o_spec = pl.BlockSpec((bm, bn), lambda i, j, k: (i, j))
    ca, cb = (0 if ta else 1), (1 if tb else 0)

    def kern(*refs):
        a_ref, b_ref = refs[:2]
        add_ref = refs[2] if add is not None else None
        o_ref = refs[3] if add is not None else refs[2]
        p = _dg(a_ref[...].astype(BF16), b_ref[...].astype(BF16), ca, cb)

        def finish(r):
            if add is not None:
                r = r + add_ref[...].astype(F32)
            o_ref[...] = r.astype(out_dtype)

        if nk == 1:
            finish(p)
            return
        acc_ref = refs[-1]
        k = pl.program_id(2)

        @pl.when(k == 0)
        def _():
            acc_ref[...] = p

        @pl.when((k > 0) & (k < nk - 1))
        def _():
            acc_ref[...] += p

        @pl.when(k == nk - 1)
        def _():
            finish(acc_ref[...] + p)

    in_specs = [a_spec, b_spec] + ([o_spec] if add is not None else [])
    args = (a, b) + ((add,) if add is not None else ())
    return pl.pallas_call(
        kern, name=name, grid=(M // bm, N // bn, nk), in_specs=in_specs, out_specs=o_spec,
        out_shape=jax.ShapeDtypeStruct((M, N), out_dtype),
        scratch_shapes=[pltpu.VMEM((bm, bn), F32)] if nk > 1 else [],
        compiler_params=_cparams(("parallel", "parallel", "arbitrary")),
    )(*args)


def _rms_fwd(x, g, name):
    S = x.shape[0]

    def body(i, x_ref, g_ref, h_ref):
        xv = x_ref[...]
        r = lax.rsqrt(jnp.mean(xv * xv, axis=-1, keepdims=True) + EPS)
        h_ref[...] = (xv * r * g_ref[...]).astype(BF16)

    return _tok_call(body, name, S, min(S, 512), [(x, D_MODEL, 0)], [g], [(D_MODEL, BF16)])[0]


def _rms_bwd_vals(xv, g, dh):
    r = lax.rsqrt(jnp.mean(xv * xv, axis=-1, keepdims=True) + EPS)
    u = dh * g
    dx = r * u - xv * (r * r * r) * jnp.mean(u * xv, axis=-1, keepdims=True)
    dg = jnp.sum(dh * xv * r, axis=0, keepdims=True)
    return dx, dg


def _rms_bwd_add(x, g, dh, dres, name):
    S = x.shape[0]

    def body(i, x_ref, dh_ref, dr_ref, g_ref, dx_ref, dg_ref):
        dx, dg = _rms_bwd_vals(x_ref[...], g_ref[...], dh_ref[...].astype(F32))
        dx_ref[...] = dr_ref[...] + dx
        _acc(dg_ref, dg, i)

    return _tok_call(body, name, S, min(S, 512), [(x, D_MODEL, 0), (dh, D_MODEL, 0), (dres, D_MODEL, 0)], [g],
                     [(D_MODEL, F32)], [((1, D_MODEL), F32)])


def _final_loss(x, g, target):
    S = x.shape[0]

    def body(i, x_ref, t_ref, g_ref, dx_ref, loss_ref, dg_ref):
        xv, gv = x_ref[...], g_ref[...]
        r = lax.rsqrt(jnp.mean(xv * xv, axis=-1, keepdims=True) + EPS)
        e = xv * r * gv - t_ref[...]
        part = 0.5 * jnp.sum(jnp.mean(e * e, axis=-1, keepdims=True), axis=0, keepdims=True)
        dx, dg = _rms_bwd_vals(xv, gv, e * (1.0 / D_MODEL))
        dx_ref[...] = dx
        _acc(loss_ref, jnp.broadcast_to(part, (1, 128)), i)
        _acc(dg_ref, dg, i)

    return _tok_call(body, "final_loss", S, min(S, 512), [(x, D_MODEL, 0), (target, D_MODEL, 0)], [g],
                     [(D_MODEL, F32)], [((1, 128), F32), ((1, D_MODEL), F32)])


def _swiglu_fwd(gu, name):
    S = gu.shape[0]

    def body(i, gu_ref, a_ref):
        a_ref[...] = (_silu(gu_ref[:, :D_FF].astype(F32)) * gu_ref[:, D_FF:].astype(F32)).astype(BF16)

    return _tok_call(body, name, S, min(S, 256), [(gu, 2 * D_FF, 0)], [], [(D_FF, BF16)])[0]


def _swiglu_bwd(gu, dact, name):
    S = gu.shape[0]

    def body(i, gu_ref, da_ref, dgu_ref):
        gg, uu, da = gu_ref[:, :D_FF].astype(F32), gu_ref[:, D_FF:].astype(F32), da_ref[...].astype(F32)
        dgu_ref[:, :D_FF] = (da * uu * _dsilu(gg)).astype(BF16)
        dgu_ref[:, D_FF:] = (da * _silu(gg)).astype(BF16)

    return _tok_call(body, name, S, min(S, 256), [(gu, 2 * D_FF, 0), (dact, D_FF, 0)], [], [(2 * D_FF, BF16)])[0]


def _merge_fwd(proj, bds, name):
    S = proj.shape[0]

    def body(i, g0, g1, g2, b0, b1, b2, m_ref):
        m = jnp.zeros(m_ref.shape, F32)
        for gr, br in ((g0, b0), (g1, b1), (g2, b2)):
            m = m + _sigmoid(gr[...].astype(F32)) * br[...].astype(F32)
        m_ref[...] = m.astype(BF16)

    tok = [(proj, D_MODEL, n) for n in range(3)] + [(b, D_MODEL, 0) for b in bds]
    return _tok_call(body, name, S, min(S, 512), tok, [], [(D_MODEL, BF16)])[0]


def _merge_bwd(proj, bds, dm, name):
    S = proj.shape[0]

    def body(i, g0, g1, g2, b0, b1, b2, dm_ref, d0, d1, d2, dgp_ref):
        dmv = dm_ref[...]
        for n, (gr, br, dr) in enumerate(((g0, b0, d0), (g1, b1, d1), (g2, b2, d2))):
            s = _sigmoid(gr[...].astype(F32))
            dr[...] = (dmv * s).astype(BF16)
            dgp_ref[:, n * D_MODEL:(n + 1) * D_MODEL] = (dmv * br[...].astype(F32) * s * (1.0 - s)).astype(BF16)

    tok = [(proj, D_MODEL, n) for n in range(3)] + [(b, D_MODEL, 0) for b in bds] + [(dm, D_MODEL, 0)]
    return _tok_call(body, name, S, min(S, 512), tok, [],
                     [(D_MODEL, BF16)] * 3 + [(3 * D_MODEL, BF16, C_GATE // (3 * D_MODEL), IN_R)])


def _sgu_ln(v, lg, lb):
    mu = jnp.mean(v, axis=-1, keepdims=True)
    vc = v - mu
    rstd = lax.rsqrt(jnp.mean(vc * vc, axis=-1, keepdims=True) + EPS)
    vhat = vc * rstd
    return vhat, rstd, vhat * lg + lb


def _sgu_fwd(proj, lg, lb, wc, bst, name):
    S = proj.shape[0]

    def body(i, ua_ref, va_ref, lg_ref, lb_ref, wc_ref, bs_ref, o_ref):
        u = _gelu(ua_ref[...].astype(F32))
        _, _, vn = _sgu_ln(_gelu(va_ref[...].astype(F32)), lg_ref[...], lb_ref[...])
        for g in range(SGU_G):
            sl = slice(g * 128, (g + 1) * 128)
            mixed = _mm(wc_ref[sl, :], vn[:, sl]) + bs_ref[:, g:g + 1]
            o_ref[:, sl] = (u[:, sl] * mixed).astype(BF16)

    return _tok_call(body, name, S, SGU_T, [(proj, MIX, C_UA // MIX), (proj, MIX, C_VA // MIX)], [lg, lb, wc, bst],
                     [(MIX, BF16)])[0]


def _sgu_bwd(proj, lg, lb, wc, bst, dout, dproj, name):
    S = proj.shape[0]

    def body(i, ua_ref, va_ref, do_ref, lg_ref, lb_ref, wc_ref, bs_ref, duv_ref, dlg_ref, dlb_ref, dwc_ref,
             dbs_ref):
        ua, va, do = ua_ref[...].astype(F32), va_ref[...].astype(F32), do_ref[...].astype(F32)
        u = _gelu(ua)
        lgv = lg_ref[...]
        vhat, rstd, vn = _sgu_ln(_gelu(va), lgv, lb_ref[...])
        tril = lax.broadcasted_iota(jnp.int32, (128, 128), 0) >= lax.broadcasted_iota(jnp.int32, (128, 128), 1)
        lane4 = lax.broadcasted_iota(jnp.int32, (128, 4), 1)
        gs = range(SGU_G)
        sls = [slice(g * 128, (g + 1) * 128) for g in gs]
        wgs = [wc_ref[sl, :] for sl in sls]
        mixed = [_mm(wgs[g], vn[:, sls[g]]) for g in gs]
        dmix = [do[:, sl] * u[:, sl] for sl in sls]
        dwg = [_mm_nt(dmix[g], vn[:, sls[g]]) for g in gs]
        dvn = jnp.concatenate([_mm_tn(wgs[g], dmix[g]) for g in gs], axis=1)
        dbs = jnp.zeros((128, 4), F32)
        for g in gs:
            duv_ref[:, sls[g]] = (do[:, sls[g]] * (mixed[g] + bs_ref[:, g:g + 1]) * _dgelu(ua[:, sls[g]])).astype(BF16)
            dbs = dbs + jnp.where(lane4 == g, jnp.sum(dmix[g], axis=-1, keepdims=True), 0.0)
            _acc(dwc_ref.at[sls[g], :], jnp.where(tril, dwg[g], 0.0), i)
        _acc(dbs_ref, dbs, i)
        _acc(dlg_ref, jnp.sum(dvn * vhat, axis=0, keepdims=True), i)
        _acc(dlb_ref, jnp.sum(dvn, axis=0, keepdims=True), i)
        dvh = dvn * lgv
        dv = rstd * (dvh - jnp.mean(dvh, axis=-1, keepdims=True) - vhat * jnp.mean(dvh * vhat, axis=-1, keepdims=True))
        duv_ref[:, MIX:] = (dv * _dgelu(va)).astype(BF16)

    return _tok_call(body, name, S, SGU_T, [(proj, MIX, C_UA // MIX), (proj, MIX, C_VA // MIX), (dout, MIX, 0)],
                     [lg, lb, wc, bst], [(2 * MIX, BF16, C_UA // (2 * MIX), dproj)],
                     [((1, MIX), F32), ((1, MIX), F32), ((SGU_G * 128, 128), F32), ((128, 4), F32)])


def _rope_tables(positions):
    S = positions.shape[0]
    inv_freq = ROPE_THETA ** (-jnp.arange(0, ROPE_DIM, 2, dtype=F32) / ROPE_DIM)
    ang = positions.astype(F32)[:, None] * inv_freq
    c, s = jnp.cos(ang), jnp.sin(ang)
    c64 = jnp.concatenate([c, c, jnp.ones((S, SWA_HD - ROPE_DIM), F32)], axis=1)
    s64 = jnp.concatenate([-s, s, jnp.zeros((S, SWA_HD - ROPE_DIM), F32)], axis=1)
    return jnp.tile(c64, (1, 2)), jnp.tile(s64, (1, 2))


def _rope128(x, c, s):
    lane = lax.broadcasted_iota(jnp.int32, x.shape, 1) % SWA_HD
    swapped = jnp.where(lane < ROPE_DIM // 2, pltpu.roll(x, 128 - ROPE_DIM // 2, 1), pltpu.roll(x, ROPE_DIM // 2, 1))
    return x * c + swapped * s


def _rope_t128(y, c, s):
    ys = y * s
    lane = lax.broadcasted_iota(jnp.int32, y.shape, 1) % SWA_HD
    swapped = jnp.where(lane < ROPE_DIM // 2, pltpu.roll(ys, 128 - ROPE_DIM // 2, 1), pltpu.roll(ys, ROPE_DIM // 2, 1))
    return y * c + jnp.where(lane < ROPE_DIM, swapped, 0.0)


def _rope_fwd(proj, cos, sin, name):
    S = proj.shape[0]
    scale = SWA_HD ** -0.5

    def body(i, q_ref, k_ref, v_ref, c_ref, s_ref, qo_ref, ko_ref, vo_ref):
        c, s = c_ref[...], s_ref[...]
        for j in range(4):
            sl = slice(j * 128, (j + 1) * 128)
            qo_ref[:, sl] = (_rope128(q_ref[:, sl].astype(F32), c, s) * scale).astype(BF16)
        ko_ref[...] = _rope128(k_ref[...].astype(F32), c, s).astype(BF16)
        vo_ref[...] = v_ref[...].astype(BF16)

    return _tok_call(body, name, S, min(S, 512),
                     [(proj, MIX, C_QB // MIX), (proj, 128, C_KB // 128), (proj, 128, C_VB // 128), (cos, 128, 0),
                      (sin, 128, 0)], [], [(MIX, BF16), (128, BF16), (128, BF16)])


def _rope_bwd(dq, dk, dv, cos, sin, dproj, name):
    S = dq.shape[0]
    scale = SWA_HD ** -0.5
    width = C_SM - C_QB

    def body(i, dq_ref, dk_ref, dv_ref, c_ref, s_ref, o_ref):
        c, s = c_ref[...], s_ref[...]
        for j in range(4):
            sl = slice(j * 128, (j + 1) * 128)
            o_ref[:, sl] = _rope_t128(dq_ref[:, sl] * scale, c, s).astype(BF16)
        o_ref[:, C_KB - C_QB:C_VB - C_QB] = _rope_t128(dk_ref[...], c, s).astype(BF16)
        o_ref[:, C_VB - C_QB:] = dv_ref[...].astype(BF16)

    return _tok_call(body, name, S, min(S, 512),
                     [(dq, MIX, 0), (dk, 128, 0), (dv, 128, 0), (cos, 128, 0), (sin, 128, 0)], [],
                     [(width, BF16, C_QB // width, dproj)])[0]


def _swa_band(i, k_ref, v_ref):
    pstart = pl.multiple_of(jnp.maximum(i - 1, 0) * WINDOW, WINDOW)
    cstart = pl.multiple_of(i * WINDOW, WINDOW)
    kb = jnp.concatenate([k_ref[pl.ds(pstart, WINDOW), :], k_ref[pl.ds(cstart, WINDOW), :]], axis=0)
    vb = jnp.concatenate([v_ref[pl.ds(pstart, WINDOW), :], v_ref[pl.ds(cstart, WINDOW), :]], axis=0)
    qi = lax.broadcasted_iota(jnp.int32, (WINDOW, 2 * WINDOW), 0)
    sj = lax.broadcasted_iota(jnp.int32, (WINDOW, 2 * WINDOW), 1)
    mask = (sj > qi) & (sj <= qi + WINDOW) & ((i > 0) | (sj >= WINDOW))
    return kb, vb, mask, pstart, cstart


def _swa_probs(qs, kh, mask, sinks):
    logits = [jnp.where(mask, _dg(qh, kh, 1, 1), -1e30) for qh in qs]
    m = [jnp.maximum(jnp.max(l, axis=-1, keepdims=True), s) for l, s in zip(logits, sinks)]
    p = [jnp.exp(l - mm) for l, mm in zip(logits, m)]
    ps = [jnp.exp(s - mm) for s, mm in zip(sinks, m)]
    inv = [1.0 / (jnp.sum(pp, axis=-1, keepdims=True) + s) for pp, s in zip(p, ps)]
    return [pp * iv for pp, iv in zip(p, inv)], [s * iv for s, iv in zip(ps, inv)]


def _swa_fwd(q, k, v, sinks, name):
    S = q.shape[0]
    G = SWA_H // SWA_KV

    def body(i, q_ref, k_ref, v_ref, s_ref, o_ref):
        kb, vb, mask, _, _ = _swa_band(i, k_ref, v_ref)
        qv = q_ref[...]
        for kv in range(SWA_KV):
            ksl = slice(kv * SWA_HD, (kv + 1) * SWA_HD)
            heads = range(kv * G, (kv + 1) * G)
            pn, _ = _swa_probs([qv[:, h * SWA_HD:(h + 1) * SWA_HD] for h in heads], kb[:, ksl], mask,
                               [s_ref[0, h] for h in heads])
            outs = [_dg(p.astype(BF16), vb[:, ksl], 1, 0) for p in pn]
            for h, o in zip(heads, outs):
                o_ref[:, h * SWA_HD:(h + 1) * SWA_HD] = o.astype(BF16)

    return _tok_call(body, name, S, WINDOW, [(q, MIX, 0)], [k, v], [(MIX, BF16)], smem_in=[sinks])[0]


def _swa_bwd(q, k, v, sinks, dout, name):
    S = q.shape[0]

    def body(i, q_ref, do_ref, k_ref, v_ref, s_ref, dq_ref, dk_ref, dv_ref, ds_ref):
        kb, vb, mask, pstart, cstart = _swa_band(i, k_ref, v_ref)
        qv, dov = q_ref[...], do_ref[...]
        lane = lax.broadcasted_iota(jnp.int32, (1, 128), 1)
        dsink = jnp.zeros((1, 128), F32)
        dkb, dvb = [], []
        G = SWA_H // SWA_KV
        for kv in range(SWA_KV):
            ksl = slice(kv * SWA_HD, (kv + 1) * SWA_HD)
            heads = range(kv * G, (kv + 1) * G)
            qs = [qv[:, h * SWA_HD:(h + 1) * SWA_HD] for h in heads]
            dos = [dov[:, h * SWA_HD:(h + 1) * SWA_HD].astype(BF16) for h in heads]
            pn, psn = _swa_probs(qs, kb[:, ksl], mask, [s_ref[0, h] for h in heads])
            dp = [_dg(d, vb[:, ksl], 1, 1) for d in dos]
            delta = [jnp.sum(a * b, axis=-1, keepdims=True) for a, b in zip(dp, pn)]
            dsc = [(p * (a - d)).astype(BF16) for p, a, d in zip(pn, dp, delta)]
            dqs = [_dg(s, kb[:, ksl], 1, 0) for s in dsc]
            dks = [_dg(s, qh, 0, 0) for s, qh in zip(dsc, qs)]
            dvs = [_dg(p.astype(BF16), d, 0, 0) for p, d in zip(pn, dos)]
            for n_, h in enumerate(heads):
                dq_ref[:, h * SWA_HD:(h + 1) * SWA_HD] = dqs[n_]
                dsink = dsink + jnp.where(lane == h, -jnp.sum(psn[n_] * delta[n_], axis=0, keepdims=True), 0.0)
            dkb.append((dks[0] + dks[1]) + (dks[2] + dks[3]))
            dvb.append((dvs[0] + dvs[1]) + (dvs[2] + dvs[3]))
        dkb = jnp.concatenate(dkb, axis=1)
        dvb = jnp.concatenate(dvb, axis=1)

        @pl.when(i == 0)
        def _():
            dk_ref[...] = jnp.zeros_like(dk_ref)
            dv_ref[...] = jnp.zeros_like(dv_ref)

        dk_ref[pl.ds(pstart, WINDOW), :] += dkb[:WINDOW]
        dv_ref[pl.ds(pstart, WINDOW), :] += dvb[:WINDOW]
        dk_ref[pl.ds(cstart, WINDOW), :] += dkb[WINDOW:]
        dv_ref[pl.ds(cstart, WINDOW), :] += dvb[WINDOW:]
        _acc(ds_ref, dsink, i)

    return _tok_call(body, name, S, WINDOW, [(q, MIX, 0), (dout, MIX, 0)], [k, v], [(MIX, F32)],
                     [((S, 128), F32), ((S, 128), F32), ((1, 128), F32)], smem_in=[sinks])


def _shift_rows(xs, k):
    return xs if k == 0 else pltpu.roll(xs, k, 0)


def _dn_conv(x_ref, p_ref, w_ref, i):
    hr = p_ref.shape[0]
    halo = jnp.where(i > 0, p_ref[...].astype(F32), 0.0)
    xs = jnp.concatenate([halo, x_ref[...].astype(F32)], axis=0)
    sh = [_shift_rows(xs, DN_CONV - 1 - t)[hr:] for t in range(DN_CONV)]
    pre = sh[0] * w_ref[0:1, :]
    for t in range(1, DN_CONV):
        pre = pre + sh[t] * w_ref[t:t + 1, :]
    return pre, sh


def _dn_gates(sm, alog, dtb):
    lane = lax.broadcasted_iota(jnp.int32, sm.shape, 1)
    return jnp.where(lane < DN_H, _sigmoid(sm), -jnp.exp(alog) * _softplus(sm + dtb))


def _dn_pre_fwd(proj, conv_w, alog_l, dtb_l, name):
    S = proj.shape[0]
    scale = DN_HD ** -0.5

    def body(i, x_ref, sm_ref, p_ref, w_ref, al_ref, db_ref, q_ref, k_ref, v_ref, bg_ref):
        pre, _ = _dn_conv(x_ref, p_ref, w_ref, i)
        a = _silu(pre)
        for h in range(DN_H):
            sl = slice(h * DN_HD, (h + 1) * DN_HD)
            qh, kh = a[:, sl], a[:, MIX + h * DN_HD:MIX + (h + 1) * DN_HD]
            q_ref[:, sl] = qh * (lax.rsqrt(jnp.sum(qh * qh, axis=-1, keepdims=True) + EPS) * scale)
            k_ref[:, sl] = kh * lax.rsqrt(jnp.sum(kh * kh, axis=-1, keepdims=True) + EPS)
        v_ref[...] = a[:, 2 * MIX:]
        bg_ref[...] = _dn_gates(sm_ref[...].astype(F32), al_ref[...], db_ref[...])

    TB = min(S, 256)
    return _tok_call(body, name, S, TB, [(proj, 3 * MIX, C_QKV // (3 * MIX)), (proj, 128, C_SM // 128)],
                     [conv_w, alog_l, dtb_l], [(MIX, F32), (MIX, F32), (MIX, F32), (128, F32)],
                     prev_in=[(proj, 3 * MIX, C_QKV // (3 * MIX))])


def _dn_pre_bwd1(proj, conv_w, alog_l, dtb_l, dq, dk, dv, dbg, dproj, name):
    S = proj.shape[0]
    scale = DN_HD ** -0.5

    def body(i, x_ref, sm_ref, dq_ref, dk_ref, dv_ref, dbg_ref, p_ref, w_ref, al_ref, db_ref, dpre_ref, dsm_ref,
             dw_ref, dal_ref, ddb_ref):
        pre, sh = _dn_conv(x_ref, p_ref, w_ref, i)
        a = _silu(pre)
        da_parts = []
        for part, (g_ref, sc) in enumerate(((dq_ref, scale), (dk_ref, 1.0))):
            for h in range(DN_H):
                xh = a[:, part * MIX + h * DN_HD:part * MIX + (h + 1) * DN_HD]
                rs = lax.rsqrt(jnp.sum(xh * xh, axis=-1, keepdims=True) + EPS)
                y = xh * rs
                dy = g_ref[:, h * DN_HD:(h + 1) * DN_HD] * sc
                da_parts.append(rs * (dy - y * jnp.sum(dy * y, axis=-1, keepdims=True)))
        da_parts.append(dv_ref[...])
        dpre = jnp.concatenate(da_parts, axis=1) * _dsilu(pre)
        dpre_ref[...] = dpre
        dw = jnp.concatenate([jnp.sum(dpre * sh[t], axis=0, keepdims=True) for t in range(DN_CONV)], axis=0)
        _acc(dw_ref, dw, i)
        sm, al, db, dbg_v = sm_ref[...].astype(F32), al_ref[...], db_ref[...], dbg_ref[...]
        lane = lax.broadcasted_iota(jnp.int32, sm.shape, 1)
        sg = _sigmoid(sm)
        gneg = -jnp.exp(al)
        is_g = (lane >= DN_H) & (lane < 2 * DN_H)
        d_al = jnp.where(is_g, dbg_v * gneg * _sigmoid(sm + db), 0.0)
        dsm_ref[...] = jnp.where(lane < DN_H, dbg_v * sg * (1.0 - sg), d_al).astype(BF16)
        _acc(ddb_ref, jnp.sum(d_al, axis=0, keepdims=True), i)
        _acc(dal_ref, jnp.sum(jnp.where(is_g, dbg_v * gneg * _softplus(sm + db), 0.0), axis=0, keepdims=True), i)

    TB = min(S, 256)
    return _tok_call(body, name, S, TB,
                     [(proj, 3 * MIX, C_QKV // (3 * MIX)), (proj, 128, C_SM // 128), (dq, MIX, 0), (dk, MIX, 0),
                      (dv, MIX, 0), (dbg, 128, 0)], [conv_w, alog_l, dtb_l],
                     [(3 * MIX, F32), (128, BF16, C_SM // 128, dproj)],
                     [((DN_CONV, 3 * MIX), F32), ((1, 128), F32), ((1, 128), F32)],
                     prev_in=[(proj, 3 * MIX, C_QKV // (3 * MIX))])


def _dn_pre_bwd2(dpre, conv_w, dproj, name):
    S = dpre.shape[0]
    TB = min(S, 256)
    nb = S // TB

    def body(i, d_ref, n_ref, w_ref, o_ref):
        halo = jnp.where(i < nb - 1, n_ref[...], 0.0)
        ds = jnp.concatenate([d_ref[...], halo], axis=0)
        out = ds[:TB] * w_ref[DN_CONV - 1:DN_CONV, :]
        for t in range(DN_CONV - 1):
            k = DN_CONV - 1 - t
            out = out + pltpu.roll(ds, TB + 8 - k, 0)[:TB] * w_ref[t:t + 1, :]
        o_ref[...] = out.astype(BF16)

    return _tok_call(body, name, S, TB, [(dpre, 3 * MIX, 0)], [conv_w],
                     [(3 * MIX, BF16, C_QKV // (3 * MIX), dproj)], next_in=[(dpre, 3 * MIX, 0)])[0]


def _dn_decay_terms(bgs, heads):
    C = DN_C
    ri = lax.broadcasted_iota(jnp.int32, (C, C), 0)
    ci = lax.broadcasted_iota(jnp.int32, (C, C), 1)
    tril, eye = ri >= ci, ri == ci
    beta = [b[:, h:h + 1] for b, h in zip(bgs, heads)]
    gcol = _dg_exact_lhs_many(tril, [jnp.broadcast_to(b[:, DN_H + h:DN_H + h + 1], (C, C))
                                     for b, h in zip(bgs, heads)], 1, 0)
    grow = [jnp.sum(jnp.where(eye, g, 0.0), axis=0, keepdims=True) for g in gcol]
    decay = [jnp.exp(jnp.where(tril, g - r, -1e30)) for g, r in zip(gcol, grow)]
    e_gc = [jnp.exp(g[:, 0:1]) for g in gcol]
    e_kd = [jnp.exp(g[C - 1:C, 0:1] - g[:, 0:1]) for g in gcol]
    cdec = [jnp.exp(g[C - 1:C, 0:1]) for g in gcol]
    return beta, decay, e_gc, e_kd, cdec


def _dn_nb(S):
    return 4 if S % (4 * DN_C) == 0 else 1


def _dn_prep_fwd(q, k, v, bg, name):
    S = q.shape[0]
    C, NB = DN_C, _dn_nb(S)
    TB = NB * C

    def kern(q_ref, k_ref, v_ref, bg_ref, t_ref, uw_ref, at_ref, qd_ref, kd_ref, dec_ref):
        lane = lax.broadcasted_iota(jnp.int32, (C, 128), 1)
        ri = lax.broadcasted_iota(jnp.int32, (C, C), 0)
        ci = lax.broadcasted_iota(jnp.int32, (C, C), 1)
        tril, eye = ri >= ci, ri == ci
        chains = [(cb, h) for cb in range(NB) for h in range(DN_H)]
        rows = lambda cb: slice(cb * C, (cb + 1) * C)
        head = lambda h: slice(h * DN_HD, (h + 1) * DN_HD)
        beta, decay, e_gc, e_kd, cdec = _dn_decay_terms([bg_ref[rows(cb), :] for cb, _ in chains],
                                                        [h for _, h in chains])
        qs = [q_ref[rows(cb), head(h)] for cb, h in chains]
        ks = [k_ref[rows(cb), head(h)] for cb, h in chains]
        kb = [kh * b for kh, b in zip(ks, beta)]
        x = [-jnp.where(ri > ci, _mm_nt(a, kh) * d, 0.0) for a, kh, d in zip(kb, ks, decay)]
        tm = [jnp.where(eye, 1.0, 0.0) + xi for xi in x]
        p = x
        p = _dg3_many(p, p, 1, 0)
        for it in range(5):
            if it == 4:
                tm = [t + tp for t, tp in zip(tm, _dg3_many(tm, p, 1, 0))]
                break
            both = _dg3_many([jnp.concatenate([t, pp], axis=0) for t, pp in zip(tm, p)], p, 1, 0)
            tm = [t + b[:C] for t, b in zip(tm, both)]
            p = [b[C:] for b in both]
        rhs = [jnp.concatenate([v_ref[rows(cb), head(h)] * b, a * e], axis=1)
               for (cb, h), b, a, e in zip(chains, beta, kb, e_gc)]
        sol = _dg3_many(tm, rhs, 1, 0)
        attn = [_mm_nt(qh, kh) * d for qh, kh, d in zip(qs, ks, decay)]
        for n_, (cb, h) in enumerate(chains):
            rs, sl, hc = rows(cb), head(h), slice(h * C, (h + 1) * C)
            t_ref[rs, hc] = tm[n_]
            uw_ref[rs, sl] = sol[n_][:, :DN_HD]
            uw_ref[rs, MIX + h * DN_HD:MIX + (h + 1) * DN_HD] = sol[n_][:, DN_HD:]
            at_ref[rs, hc] = attn[n_]
            qd_ref[rs, sl] = (qs[n_] * e_gc[n_]).astype(BF16)
            kd_ref[rs, sl] = (ks[n_] * e_kd[n_]).astype(BF16)
        for cb in range(NB):
            dec = jnp.zeros((C, 128), F32)
            for h in range(DN_H):
                dec = dec + jnp.where(lane == h, cdec[cb * DN_H + h], 0.0)
            dec_ref[rows(cb), :] = dec

    tok = lambda w: pl.BlockSpec((TB, w), lambda i: (i, 0))
    return pl.pallas_call(
        kern, name=name, grid=(S // TB,), in_specs=[tok(MIX), tok(MIX), tok(MIX), tok(128)],
        out_specs=[tok(DN_H * C), tok(2 * MIX), tok(DN_H * C), tok(MIX), tok(MIX), tok(128)],
        out_shape=[jax.ShapeDtypeStruct((S, DN_H * C), F32), jax.ShapeDtypeStruct((S, 2 * MIX), F32),
                   jax.ShapeDtypeStruct((S, DN_H * C), F32), jax.ShapeDtypeStruct((S, MIX), BF16),
                   jax.ShapeDtypeStruct((S, MIX), BF16), jax.ShapeDtypeStruct((S, 128), F32)],
        compiler_params=_cparams(("parallel",)),
    )(q, k, v, bg)


def _dn_scan_fwd(uw, at, qd, kd, dec, name):
    S = uw.shape[0]
    C, NB = DN_C, _dn_nb(S)
    TB = NB * C
    SR = DN_H * DN_HD

    def kern(uw_ref, at_ref, qd_ref, kd_ref, dec_ref, o_ref, vn_ref, st_ref, state):
        @pl.when(pl.program_id(0) == 0)
        def _():
            state[...] = jnp.zeros_like(state)

        for cb in range(NB):
            rs = slice(cb * C, (cb + 1) * C)
            hs = range(DN_H)
            sls = [slice(h * DN_HD, (h + 1) * DN_HD) for h in hs]
            s_in = [state[sl, :] for sl in sls]
            ws = [_mm(uw_ref[rs, MIX + h * DN_HD:MIX + (h + 1) * DN_HD], s_in[h]) for h in hs]
            os_ = [_mm(qd_ref[rs, sls[h]], s_in[h]) for h in hs]
            vnew = [uw_ref[rs, sls[h]] - ws[h] for h in hs]
            oa = [_mm(at_ref[rs, h * C:(h + 1) * C], vnew[h]) for h in hs]
            kv = [_mm_tn(kd_ref[rs, sls[h]], vnew[h]) for h in hs]
            for h in hs:
                o_ref[rs, sls[h]] = os_[h] + oa[h]
                state[sls[h], :] = s_in[h] * dec_ref[cb * C:cb * C + 1, h:h + 1] + kv[h]
                st_ref[cb * SR + h * DN_HD:cb * SR + (h + 1) * DN_HD, :] = s_in[h]
                vn_ref[rs, sls[h]] = vnew[h]

    tok = lambda w: pl.BlockSpec((TB, w), lambda i: (i, 0))
    return pl.pallas_call(
        kern, name=name, grid=(S // TB,), in_specs=[tok(2 * MIX), tok(DN_H * C), tok(MIX), tok(MIX), tok(128)],
        out_specs=[tok(MIX), tok(MIX), pl.BlockSpec((NB * SR, DN_HD), lambda i: (i, 0))],
        out_shape=[jax.ShapeDtypeStruct((S, MIX), F32), jax.ShapeDtypeStruct((S, MIX), F32),
                   jax.ShapeDtypeStruct((S // C * SR, DN_HD), F32)],
        scratch_shapes=[pltpu.VMEM((SR, DN_HD), F32)],
        compiler_params=_cparams(("arbitrary",)),
    )(uw, at, qd, kd, dec)


def _dn_core_fwd(q, k, v, bg, name):
    tm, uw, at, qd, kd, dec = _dn_prep_fwd(q, k, v, bg, name + "_prep")
    o, vn, st = _dn_scan_fwd(uw, at, qd, kd, dec, name + "_scan")
    return o, dict(tm=tm, uw=uw, at=at, qd=qd, kd=kd, dec=dec, vn=vn, st=st)


def _dn_scan_bwd(sv, do, name):
    S = do.shape[0]
    C, NB = DN_C, _dn_nb(S)
    TB = NB * C
    SR = DN_H * DN_HD
    nb = S // TB

    def kern(do_ref, uw_ref, at_ref, qd_ref, kd_ref, dec_ref, vn_ref, st_ref, dvn_ref, dw_ref, dkd_ref, dc_ref, dstate):
        @pl.when(pl.program_id(0) == 0)
        def _():
            dstate[...] = jnp.zeros_like(dstate)

        lane = lax.broadcasted_iota(jnp.int32, (C, 128), 1)
        for cb in reversed(range(NB)):
            rs = slice(cb * C, (cb + 1) * C)
            dcrow = jnp.zeros((C, 128), F32)
            for h in range(DN_H):
                sl = slice(h * DN_HD, (h + 1) * DN_HD)
                doh, ds_o = do_ref[rs, sl], dstate[sl, :]
                s_in = st_ref[cb * SR + h * DN_HD:cb * SR + (h + 1) * DN_HD, :]
                d_vnew = _mm_tn(at_ref[rs, h * C:(h + 1) * C], doh) + _mm(kd_ref[rs, sl], ds_o)
                dvn_ref[rs, sl] = d_vnew
                dw_ref[rs, sl] = -_mm_nt(d_vnew, s_in)
                dkd_ref[rs, sl] = _mm_nt(vn_ref[rs, sl], ds_o)
                d_c = jnp.sum(jnp.sum(ds_o * s_in, axis=1, keepdims=True), axis=0, keepdims=True)
                dcrow = dcrow + jnp.where(lane == h, d_c, 0.0)
                dstate[sl, :] = (ds_o * dec_ref[cb * C:cb * C + 1, h:h + 1] + _mm_tn(qd_ref[rs, sl], doh)
                                 - _mm_tn(uw_ref[rs, MIX + h * DN_HD:MIX + (h + 1) * DN_HD], d_vnew))
            dc_ref[rs, :] = dcrow

    tok = lambda w: pl.BlockSpec((TB, w), lambda i: (nb - 1 - i, 0))
    return pl.pallas_call(
        kern, name=name, grid=(nb,),
        in_specs=[tok(MIX), tok(2 * MIX), tok(DN_H * C), tok(MIX), tok(MIX), tok(128), tok(MIX),
                  pl.BlockSpec((NB * SR, DN_HD), lambda i: (nb - 1 - i, 0))],
        out_specs=[tok(MIX), tok(MIX), tok(MIX), tok(128)],
        out_shape=[jax.ShapeDtypeStruct((S, MIX), F32)] * 3 + [jax.ShapeDtypeStruct((S, 128), F32)],
        scratch_shapes=[pltpu.VMEM((SR, DN_HD), F32)],
        compiler_params=_cparams(("arbitrary",)),
    )(do, sv["uw"], sv["at"], sv["qd"], sv["kd"], sv["dec"], sv["vn"], sv["st"])


def _dn_chunk_bwd(q, k, v, bg, sv, do, dvn, dw, dkd, dc, name):
    S = q.shape[0]
    C, NB = DN_C, _dn_nb(S)
    TB = NB * C
    SR = DN_H * DN_HD

    def kern(q_ref, k_ref, v_ref, bg_ref, t_ref, uw_ref, vn_ref, st_ref, do_ref, dvn_ref, dw_ref, dkd_ref, dc_ref,
             dq_ref, dk_ref, dv_ref, dbg_ref):
        lane = lax.broadcasted_iota(jnp.int32, (C, 128), 1)
        ri = lax.broadcasted_iota(jnp.int32, (C, C), 0)
        ci = lax.broadcasted_iota(jnp.int32, (C, C), 1)
        tril, eye, last = ri >= ci, ri == ci, ri[:, 0:1] == C - 1
        chains = [(cb, h) for cb in range(NB) for h in range(DN_H)]
        each = lambda f, *ls: [f(*a) for a in zip(*ls)]
        rsum = lambda t: jnp.sum(t, axis=-1, keepdims=True)
        rows = lambda cb: slice(cb * C, (cb + 1) * C)
        head = lambda h: slice(h * DN_HD, (h + 1) * DN_HD)
        tok = lambda ref: [ref[rows(cb), head(h)] for cb, h in chains]
        beta, decay, e_gc, e_kd, cdec = _dn_decay_terms([bg_ref[rows(cb), :] for cb, _ in chains],
                                                        [h for _, h in chains])
        qs, ks, vs, dos, vnew, d_kd = tok(q_ref), tok(k_ref), tok(v_ref), tok(do_ref), tok(vn_ref), tok(dkd_ref)
        s_in = [st_ref[cb * SR + h * DN_HD:cb * SR + (h + 1) * DN_HD, :] for cb, h in chains]
        d_c = [dc_ref[cb * C:cb * C + 1, h:h + 1] for cb, h in chains]
        kb = each(lambda a, b: a * b, ks, beta)
        kk = each(_mm_nt, kb, ks)
        attn = each(lambda a, b, d: _mm_nt(a, b) * d, qs, ks, decay)
        d_qd = each(_mm_nt, dos, s_in)
        d_attn = each(_mm_nt, dos, vnew)
        d_sol = [jnp.concatenate([dvn_ref[rows(cb), head(h)], dw_ref[rows(cb), head(h)]], axis=1) for cb, h in chains]
        sol = [jnp.concatenate([uw_ref[rows(cb), head(h)], uw_ref[rows(cb), MIX + h * DN_HD:MIX + (h + 1) * DN_HD]],
                               axis=1) for cb, h in chains]
        d_rhs = _dg3_many([t_ref[rows(cb), h * C:(h + 1) * C] for cb, h in chains], d_sol, 0, 0)
        d_a = _dg3_many(d_rhs, sol, 1, 1)
        d_kk = each(lambda a, d: jnp.where(ri > ci, -a, 0.0) * d, d_a, decay)
        d_qk = each(lambda a, d: a * d, d_attn, decay)
        dm = each(lambda a, b, c_, d: a * b + c_ * d, d_kk, kk, d_attn, attn)
        d_vb = [t[:, :DN_HD] for t in d_rhs]
        dz = [t[:, DN_HD:] for t in d_rhs]
        d_kb = each(lambda z, e, a, kh: z * e + _mm(a, kh), dz, e_gc, d_kk, ks)
        d_k = each(lambda a, b, c_, q: _mm_tn(a, b) + _mm_tn(c_, q), d_kk, kb, d_qk, qs)
        d_q = each(lambda a, kh, b, e: _mm(a, kh) + b * e, d_qk, ks, d_qd, e_gc)
        t_kd = each(lambda a, kh, e: rsum(a * kh * e), d_kd, ks, e_kd)
        d_gl = each(lambda t, c_, cd: jnp.sum(t, axis=0, keepdims=True) + c_ * cd, t_kd, d_c, cdec)
        d_gc = each(lambda z, a, e, m, b, q, t, gl:
                    rsum(z * a) * e + rsum(m) - rsum(jnp.where(eye, jnp.sum(m, axis=0, keepdims=True), 0.0))
                    + rsum(b * q) * e - t + jnp.where(last, gl, 0.0),
                    dz, kb, e_gc, dm, d_qd, qs, t_kd, d_gl)
        d_g = _dg_exact_lhs_many(ri <= ci, [jnp.broadcast_to(t, (C, 128)) for t in d_gc], 1, 0)
        d_beta = each(lambda a, v_, b, kh: rsum(a * v_) + rsum(b * kh), d_vb, vs, d_kb, ks)
        for n_, (cb, h) in enumerate(chains):
            dq_ref[rows(cb), head(h)] = d_q[n_]
            dk_ref[rows(cb), head(h)] = d_k[n_] + d_kd[n_] * e_kd[n_] + d_kb[n_] * beta[n_]
            dv_ref[rows(cb), head(h)] = d_vb[n_] * beta[n_]
        for cb in range(NB):
            dbg = jnp.zeros((C, 128), F32)
            for h in range(DN_H):
                n_ = cb * DN_H + h
                dbg = dbg + jnp.where(lane == h, d_beta[n_], 0.0) + jnp.where(lane == DN_H + h, d_g[n_], 0.0)
            dbg_ref[rows(cb), :] = dbg

    tok = lambda w: pl.BlockSpec((TB, w), lambda i: (i, 0))
    return pl.pallas_call(
        kern, name=name, grid=(S // TB,),
        in_specs=[tok(MIX), tok(MIX), tok(MIX), tok(128), tok(DN_H * C), tok(2 * MIX), tok(MIX),
                  pl.BlockSpec((NB * SR, DN_HD), lambda i: (i, 0)), tok(MIX), tok(MIX), tok(MIX), tok(MIX), tok(128)],
        out_specs=[tok(MIX), tok(MIX), tok(MIX), tok(128)],
        out_shape=[jax.ShapeDtypeStruct((S, MIX), F32)] * 3 + [jax.ShapeDtypeStruct((S, 128), F32)],
        compiler_params=_cparams(("parallel",)),
    )(q, k, v, bg, sv["tm"], sv["uw"], sv["vn"], sv["st"], do, dvn, dw, dkd, dc)


def _dn_core_bwd(q, k, v, bg, sv, do, name):
    dvn, dw, dkd, dc = _dn_scan_bwd(sv, do, name + "_scan")
    return _dn_chunk_bwd(q, k, v, bg, sv, do, dvn, dw, dkd, dc, name + "_chunk")


def _dn_post_fwd(o, proj, ng, name):
    S = o.shape[0]

    def body(i, o_ref, z_ref, g_ref, out_ref):
        gv = g_ref[...]
        for h in range(DN_H):
            sl = slice(h * DN_HD, (h + 1) * DN_HD)
            oh = o_ref[:, sl]
            r = lax.rsqrt(jnp.mean(oh * oh, axis=-1, keepdims=True) + EPS)
            out_ref[:, sl] = (oh * r * gv * _silu(z_ref[:, sl].astype(F32))).astype(BF16)

    return _tok_call(body, name, S, min(S, 512), [(o, MIX, 0), (proj, MIX, C_ZC // MIX)], [ng], [(MIX, BF16)])[0]


def _dn_post_bwd(o, proj, ng, dout, dproj, name):
    S = o.shape[0]

    def body(i, o_ref, z_ref, do_ref, g_ref, dov_ref, dz_ref, dg_ref):
        gv = g_ref[...]
        dg = jnp.zeros((1, DN_HD), F32)
        for h in range(DN_H):
            sl = slice(h * DN_HD, (h + 1) * DN_HD)
            oh, zh, dh = o_ref[:, sl], z_ref[:, sl].astype(F32), do_ref[:, sl].astype(F32)
            r = lax.rsqrt(jnp.mean(oh * oh, axis=-1, keepdims=True) + EPS)
            dz_ref[:, sl] = (dh * oh * r * gv * _dsilu(zh)).astype(BF16)
            dx, dgh = _rms_bwd_vals(oh, gv, dh * _silu(zh))
            dov_ref[:, sl] = dx
            dg = dg + dgh
        _acc(dg_ref, dg, i)

    return _tok_call(body, name, S, min(S, 512), [(o, MIX, 0), (proj, MIX, C_ZC // MIX), (dout, MIX, 0)], [ng],
                     [(MIX, F32), (MIX, BF16, C_ZC // MIX, dproj)], [((1, DN_HD), F32)])


def _layer_params(w, big, l):
    lane = jnp.arange(128)
    is_g = (lane >= DN_H) & (lane < 2 * DN_H)
    spread = lambda t: jnp.where(is_g, jnp.tile(t, 128 // DN_H), 0.0).reshape(1, 128)
    tril = jnp.tril(jnp.ones((SGU_T, SGU_T), bool))
    return dict(
        win=big["w_in"], rest=big["rest"], conv=w["dn_conv_w"][l], attn_norm=w["attn_norm"][l].reshape(1, -1), ffn_norm=w["ffn_norm"][l].reshape(1, -1),
        lg=w["sgu_ln_g"][l].reshape(1, -1), lb=w["sgu_ln_b"][l].reshape(1, -1),
        wc=jnp.where(tril, w["sgu_w"][l], 0.0).reshape(SGU_G * SGU_T, SGU_T), bst=w["sgu_b"][l].T,
        sinks=w["attn_sinks"][l].reshape(1, -1), alog=spread(w["dn_a_log"][l]), dtb=spread(w["dn_dt_bias"][l]),
        ng=w["dn_norm"][l].reshape(1, -1))


def _layer_fwd(x, p, cos, sin, l):
    n = lambda s: f"l{l}_{s}"
    h = _rms_fwd(x, p["attn_norm"], n("rms1"))
    proj = _matmul(h, p["win"], out_dtype=BF16, name=n("mm_in"))
    out_a = _sgu_fwd(proj, p["lg"], p["lb"], p["wc"], p["bst"], n("sgu_fwd"))
    qr, kr, vr = _rope_fwd(proj, cos, sin, n("rope_fwd"))
    out_b = _swa_fwd(qr, kr, vr, p["sinks"], n("swa_fwd"))
    q, k, v, bg = _dn_pre_fwd(proj, p["conv"], p["alog"], p["dtb"], n("dn_pre_fwd"))
    o, dn = _dn_core_fwd(q, k, v, bg, n("dn_core_fwd"))
    out_c = _dn_post_fwd(o, proj, p["ng"], n("dn_post_fwd"))
    outs = (out_a, out_b, out_c)
    rest = p.pop("rest")(out_c)
    p.update(wb=rest["w_branch"], wout=rest["w_out"], wgu=rest["w_gate_up"], wdown=rest["w_down"])
    bds = [_matmul(outs[j], p["wb"][j], out_dtype=BF16, name=n(f"mm_branch{j}")) for j in range(3)]
    merged = _merge_fwd(proj, bds, n("merge_fwd"))
    x1 = _matmul(merged, p["wout"], add=x, name=n("mm_out"))
    h2 = _rms_fwd(x1, p["ffn_norm"], n("rms2"))
    gu = _matmul(h2, p["wgu"], out_dtype=BF16, name=n("mm_gu"))
    act = _swiglu_fwd(gu, n("swiglu_fwd"))
    x2 = _matmul(act, p["wdown"], add=x1, name=n("mm_down"))
    saved = dict(x=x, h=h, proj=proj, outs=outs, qr=qr, kr=kr, vr=vr, q=q, k=k, v=v, bg=bg, o=o, dn=dn, bds=bds,
                 merged=merged, x1=x1, h2=h2, gu=gu, act=act)
    return x2, saved


def _layer_bwd(dx2, s, p, cos, sin, l, early=None):
    n = lambda t: f"l{l}_{t}"
    proj = s["proj"]
    g = {}
    g["w_down"] = _matmul(s["act"], dx2, ta=True, out_dtype=BF16, name=n("wg_down"))
    dact = _matmul(dx2, p["wdown"], tb=True, out_dtype=BF16, name=n("dg_down"))
    dgu = _swiglu_bwd(s["gu"], dact, n("swiglu_bwd"))
    g["w_gate_up"] = _matmul(s["h2"], dgu, ta=True, out_dtype=BF16, name=n("wg_gu"))
    dh2 = _matmul(dgu, p["wgu"], tb=True, name=n("dg_gu"))
    dx1, g["ffn_norm"] = _rms_bwd_add(s["x1"], p["ffn_norm"], dh2, dx2, n("rms2_bwd"))
    g["w_out"] = _matmul(s["merged"], dx1, ta=True, out_dtype=BF16, name=n("wg_out"))
    dm = _matmul(dx1, p["wout"], tb=True, name=n("dg_out"))
    dbd0, dbd1, dbd2, dproj = _merge_bwd(proj, s["bds"], dm, n("merge_bwd"))
    dbds = (dbd0, dbd1, dbd2)
    g["w_branch"] = jnp.stack([_matmul(s["outs"][j], dbds[j], ta=True, out_dtype=BF16, name=n(f"wg_branch{j}"))
                               for j in range(3)])
    douts = [_matmul(dbds[j], p["wb"][j], tb=True, name=n(f"dg_branch{j}")) for j in range(3)]
    lg = p["lg"]
    if early is not None:
        token = early({k: g.pop(k) for k in ("w_down", "w_gate_up", "w_out", "w_branch")})
        lg = lg if token is None else lg + token[0, 0]
    dproj, g["sgu_ln_g"], g["sgu_ln_b"], dwc, dbs = _sgu_bwd(proj, lg, p["lb"], p["wc"], p["bst"], douts[0], dproj,
                                                             n("sgu_bwd"))
    g["sgu_w"] = dwc.reshape(SGU_G, SGU_T, SGU_T)
    g["sgu_b"] = dbs.T
    dqr, dkr, dvr, dsink = _swa_bwd(s["qr"], s["kr"], s["vr"], p["sinks"], douts[1], n("swa_bwd"))
    g["attn_sinks"] = dsink[0, :SWA_H]
    dproj = _rope_bwd(dqr, dkr, dvr, cos, sin, dproj, n("rope_bwd"))
    do, dproj, dng = _dn_post_bwd(s["o"], proj, p["ng"], douts[2], dproj, n("dn_post_bwd"))
    g["dn_norm"] = dng[0]
    dq, dk, dv, dbg = _dn_core_bwd(s["q"], s["k"], s["v"], s["bg"], s["dn"], do, n("dn_core_bwd"))
    dpre, dproj, g["dn_conv_w"], dal, ddb = _dn_pre_bwd1(proj, p["conv"], p["alog"], p["dtb"], dq, dk, dv, dbg, dproj,
                                                         n("dn_pre_bwd1"))
    g["dn_a_log"] = dal[0, DN_H:2 * DN_H]
    g["dn_dt_bias"] = ddb[0, DN_H:2 * DN_H]
    dproj = _dn_pre_bwd2(dpre, p["conv"], dproj, n("dn_pre_bwd2"))
    g["w_in"] = _matmul(s["h"], dproj, ta=True, out_dtype=BF16, name=n("wg_in"))
    attn_norm = p["attn_norm"]
    if early is not None:
        token = early({"w_in": g.pop("w_in")})
        attn_norm = attn_norm if token is None else attn_norm + token[0, 0]
    dh = _matmul(dproj, p["win"], tb=True, name=n("dg_in"))
    dx, g["attn_norm"] = _rms_bwd_add(s["x"], attn_norm, dh, dx1, n("rms1_bwd"))
    g["attn_norm"], g["ffn_norm"] = g["attn_norm"][0], g["ffn_norm"][0]
    g["sgu_ln_g"], g["sgu_ln_b"] = g["sgu_ln_g"][0], g["sgu_ln_b"][0]
    return dx, g


def _local_step(x, positions, target, w, big_of_layer, on_grads):
    cos, sin = _rope_tables(positions)
    params, saves, xs = [], [], x
    for l in range(DEPTH):
        params.append(_layer_params(w, big_of_layer(l, xs), l))
        xs, sv = _layer_fwd(xs, params[l], cos, sin, l)
        saves.append(sv)
    dx, loss_row, dgf = _final_loss(xs, w["final_norm"].reshape(1, -1), target)
    grads = [None] * DEPTH
    for l in reversed(range(DEPTH)):
        early = functools.partial(on_grads, l) if l == 0 else None
        dx, grads[l] = _layer_bwd(dx, saves[l], params[l], cos, sin, l, early)
        left = {k: grads[l].pop(k) for k in BIG if k in grads[l]}
        token = on_grads(l, left) if left else None
        if token is not None and l > 0:
            params[l - 1] = dict(params[l - 1], ffn_norm=params[l - 1]["ffn_norm"] + token[0, 0])
    stacked = {k: jnp.stack([grads[l][k] for l in range(DEPTH)]) for k in grads[0]}
    stacked["final_norm"] = dgf[0]
    return loss_row[0, 0], dx, stacked


MESH = pl.DeviceIdType.MESH
HBM_SPEC = pl.BlockSpec(memory_space=pltpu.HBM)
VMEM_SPEC = pl.BlockSpec(memory_space=pltpu.VMEM)
N_CHIPS = 4
FLIPS = tuple((fx, fy, fc) for fx in (0, 1) for fy in (0, 1) for fc in (0, 1))[1:]
BIG = ("w_in", "w_branch", "w_out", "w_gate_up", "w_down")
BIG_SPEC = {
    "w_in": dict(rows=1024, cols=1792, axis=1, keep=1730, down=8),
    "w_branch": dict(rows=1536, cols=256, axis=1, keep=256, down=2),
    "w_out": dict(rows=256, cols=1024, axis=0, keep=1024, down=1),
    "w_gate_up": dict(rows=1024, cols=1408, axis=1, keep=1408, down=8),
    "w_down": dict(rows=704, cols=1024, axis=0, keep=1024, down=4),
}
CONV_ROWS, CONV_COLS = DEPTH * DN_CONV, 3 * MIX // N_CHIPS


def _full_shape(k):
    sp = BIG_SPEC[k]
    return (sp["rows"], N_CHIPS * sp["cols"]) if sp["axis"] == 1 else (N_CHIPS * sp["rows"], sp["cols"])


def _me():
    return lax.axis_index("x"), lax.axis_index("y"), lax.axis_index("c")


def _peer(x, y, c, flip):
    fx, fy, fc = flip
    return (1 - x if fx else x, 1 - y if fy else y, 1 - c if fc else c)


class _Copies:
    def __init__(self, send_sems, recv_sems):
        self.send_sems, self.recv_sems, self.k, self.sent, self.landing = send_sems, recv_sems, 0, [], []

    def _copy(self, k, src, dst, to):
        return pltpu.make_async_remote_copy(src_ref=src, dst_ref=dst, send_sem=self.send_sems.at[k],
                                            recv_sem=self.recv_sems.at[k], device_id=to, device_id_type=MESH)

    def send(self, src, dst, to, lands):
        k = self.k
        self.k += 1
        cp = self._copy(k, src, dst, to)
        cp.start()
        self.sent.append(cp)
        self.landing.append(self._copy(k, lands, lands, to))
        return k

    def wait_landed(self, k):
        self.landing[k].wait_recv()

    def finish(self, landed=()):
        for k, cp in enumerate(self.landing):
            if k not in landed:
                cp.wait_recv()
        for cp in self.sent:
            cp.wait_send()


def _place_shard(shard, k, chip, layer, name):
    sp = BIG_SPEC[k]
    rows, cols, keep = sp["rows"], sp["cols"], sp["keep"]
    tr = _pick(rows, (256, 64))
    nb = rows // tr
    if sp["axis"] == 1:
        out_spec = pl.BlockSpec((tr, cols), lambda i, ch: (i, ch[0]))
    else:
        out_spec = pl.BlockSpec((tr, cols), lambda i, ch: (ch[0] * nb + i, 0))

    def kern(ch_ref, x_ref, o_ref):
        v = x_ref[0].astype(BF16)
        if keep == cols:
            o_ref[...] = v
        else:
            o_ref[:, :keep] = v
            o_ref[:, keep:] = jnp.zeros((tr, cols - keep), BF16)

    return pl.pallas_call(
        kern, name=name, out_shape=jax.ShapeDtypeStruct(_full_shape(k), BF16),
        grid_spec=pltpu.PrefetchScalarGridSpec(
            num_scalar_prefetch=1, grid=(nb,),
            in_specs=[pl.BlockSpec((1, tr, keep), lambda i, ch: (layer, i, 0))], out_specs=out_spec),
        compiler_params=_cparams(("parallel",)),
    )(chip, shard)


def _half_block(ref, k, s, half):
    sp = BIG_SPEC[k]
    hr = sp["rows"] // 2
    if sp["axis"] == 1:
        return ref.at[pl.ds(pl.multiple_of(half * hr, 16), hr), pl.ds(pl.multiple_of(s * sp["cols"], 128), sp["cols"])]
    return ref.at[pl.ds(pl.multiple_of(s * sp["rows"] + half * hr, 16), hr), :]


def _other_chips(x, y):
    return [(1 - x, y), (x, 1 - y), (1 - x, 1 - y)]


ALL_BIG = BIG


def _present(d):
    return tuple(k for k in ALL_BIG if k in d)


def _gather_layer(placed, conv):
    BIG = _present(placed)
    n = len(BIG)
    n_sem = 6 * n + 3

    def body(*refs):
        conv_ref = refs[n]
        out = dict(zip(BIG, refs[n + 1:2 * n + 1]))
        conv_out, send_sems, recv_sems, local_sem = refs[2 * n + 1:]
        x, y, c = _me()
        me = 2 * x + y
        chips = _other_chips(x, y)
        net = _Copies(send_sems, recv_sems)

        def conv_block(s):
            return conv_out.at[:, pl.ds(pl.multiple_of(s * CONV_COLS, 128), CONV_COLS)]

        local = pltpu.make_async_copy(conv_ref, conv_block(me), local_sem)
        local.start()
        first = {}
        for k in BIG:
            for j, (px, py) in enumerate(chips):
                first[k, j] = net.send(_half_block(out[k], k, me, c), _half_block(out[k], k, me, c), (px, py, c),
                                       _half_block(out[k], k, 2 * px + py, c))
        for px, py in chips:
            net.send(conv_ref, conv_block(me), (px, py, c), conv_block(2 * px + py))
        for k in BIG:
            for j, (px, py) in enumerate(chips):
                net.wait_landed(first[k, j])
                net.send(_half_block(out[k], k, 2 * px + py, c), _half_block(out[k], k, 2 * px + py, c), (x, y, 1 - c),
                         _half_block(out[k], k, 2 * px + py, 1 - c))
        net.finish(landed=set(first.values()))
        local.wait()

    out_shape = [jax.ShapeDtypeStruct(_full_shape(k), BF16) for k in BIG]
    out_shape.append(jax.ShapeDtypeStruct((CONV_ROWS, N_CHIPS * CONV_COLS), F32))
    outs = pl.pallas_call(
        body, name="gather_layer", out_shape=out_shape, in_specs=[HBM_SPEC] * (n + 1), out_specs=[HBM_SPEC] * (n + 1),
        input_output_aliases={i: i for i in range(n)},
        scratch_shapes=[pltpu.SemaphoreType.DMA((n_sem,)), pltpu.SemaphoreType.DMA((n_sem,)), pltpu.SemaphoreType.DMA],
    )(*[placed[k] for k in BIG], conv)
    return dict(zip(BIG, outs[:n])), outs[n]


SEM_SPEC = pl.BlockSpec(memory_space=pltpu.SEMAPHORE)


def _behind_copies(arrs, send_sems, recv_sems):
    x, y, c = _me()
    copies = []
    for i, k in enumerate(_present(arrs)):
        for j, (px, py) in enumerate(_other_chips(x, y)):
            copies.append(pltpu.make_async_remote_copy(
                src_ref=_half_block(arrs[k], k, 2 * x + y, c), dst_ref=_half_block(arrs[k], k, 2 * x + y, c),
                send_sem=send_sems.at[3 * i + j], recv_sem=recv_sems.at[3 * i + j], device_id=(px, py, c),
                device_id_type=MESH))
    return copies


def _gather_start(placed, after, tag):
    BIG = _present(placed)
    n = len(BIG)
    N_BEHIND = 3 * n

    def body(*refs):
        arrs = dict(zip(BIG, refs[n + 3:2 * n + 3]))
        send_sems, recv_sems = refs[n + 1], refs[n + 2]
        for cp in _behind_copies(arrs, send_sems, recv_sems):
            cp.start()
        refs[2 * n + 3][...] = jnp.zeros((8, 128), F32)

    outs = pl.pallas_call(
        body, name="gather_start" + tag,
        out_shape=(pltpu.SemaphoreType.DMA((N_BEHIND,)), pltpu.SemaphoreType.DMA((N_BEHIND,)),
                   *[pltpu.HBM(_full_shape(k), BF16) for k in BIG], jax.ShapeDtypeStruct((8, 128), F32)),
        in_specs=[HBM_SPEC] * n + [pl.BlockSpec(memory_space=pl.ANY)],
        out_specs=(SEM_SPEC, SEM_SPEC, *[HBM_SPEC] * n, VMEM_SPEC),
        input_output_aliases={i: i + 2 for i in range(n)},
        compiler_params=pltpu.CompilerParams(has_side_effects=pltpu.SideEffectType.DATAFLOW_SIDE_EFFECTING),
    )(*[pltpu.with_memory_space_constraint(placed[k], pltpu.HBM) for k in BIG], after)
    return outs[0], outs[1], dict(zip(BIG, outs[2:n + 2])), outs[n + 2]


def _gather_wait(send_sems, recv_sems, inflight, after, tag):
    BIG = _present(inflight)
    n = len(BIG)

    def body(*refs):
        arrs = dict(zip(BIG, refs[:n]))
        for cp in _behind_copies(arrs, refs[n], refs[n + 1]):
            cp.wait_send()
            cp.wait_recv()

    outs = pl.pallas_call(
        body, name="gather_wait" + tag, out_shape=tuple(pltpu.HBM(_full_shape(k), BF16) for k in BIG),
        in_specs=[HBM_SPEC] * n + [SEM_SPEC, SEM_SPEC, pl.BlockSpec(memory_space=pl.ANY)], out_specs=(HBM_SPEC,) * n,
        input_output_aliases={i: i for i in range(n)},
        compiler_params=pltpu.CompilerParams(has_side_effects=pltpu.SideEffectType.DATAFLOW_SIDE_EFFECTING),
    )(*[inflight[k] for k in BIG], send_sems, recv_sems, after)
    return dict(zip(BIG, outs))


def _gather_finish(arrs, tag):
    BIG = _present(arrs)
    n = len(BIG)
    N_BEHIND = 3 * n

    def body(*refs):
        out = dict(zip(BIG, refs[n:2 * n]))
        send_sems, recv_sems = refs[2 * n:]
        x, y, c = _me()
        net = _Copies(send_sems, recv_sems)
        for k in BIG:
            for px, py in _other_chips(x, y):
                net.send(_half_block(out[k], k, 2 * px + py, c), _half_block(out[k], k, 2 * px + py, c), (x, y, 1 - c),
                         _half_block(out[k], k, 2 * px + py, 1 - c))
        net.finish()

    outs = pl.pallas_call(
        body, name="gather_finish" + tag, out_shape=[jax.ShapeDtypeStruct(_full_shape(k), BF16) for k in BIG],
        in_specs=[HBM_SPEC] * n, out_specs=[HBM_SPEC] * n, input_output_aliases={i: i for i in range(n)},
        scratch_shapes=[pltpu.SemaphoreType.DMA((N_BEHIND,)), pltpu.SemaphoreType.DMA((N_BEHIND,))],
    )(*[arrs[k] for k in BIG])
    return dict(zip(BIG, outs))


def _row_chunks(ref, rows, n):
    step = rows // n
    return [ref.at[pl.ds(i * step, step), :] for i in range(n)]


def _half_pieces(ref, k, half):
    sp = BIG_SPEC[k]
    hr = sp["rows"] // 2
    if sp["axis"] == 1:
        return [ref.at[pl.ds(pl.multiple_of(half * hr, 16), hr), :]]
    return [ref.at[pl.ds(pl.multiple_of(s * sp["rows"] + half * hr, 16), hr), :] for s in range(N_CHIPS)]


def _half_shape(k):
    rows, cols = _full_shape(k)
    return rows // 2, cols


def _stacked_pieces(ref, k):
    sp = BIG_SPEC[k]
    hr = sp["rows"] // 2
    return [ref] if sp["axis"] == 1 else [ref.at[pl.ds(s * hr, hr), :] for s in range(N_CHIPS)]


def _chip_part(ref, k, s):
    sp = BIG_SPEC[k]
    hr = sp["rows"] // 2
    if sp["axis"] == 1:
        return ref.at[:, pl.ds(pl.multiple_of(s * sp["cols"], 128), sp["cols"])]
    return ref.at[pl.ds(pl.multiple_of(s * hr, 16), hr), :]


def _halves_to_sibling(grads, name):
    BIG = _present(grads)
    n = len(BIG)
    chunks = {k: max(BIG_SPEC[k]["down"] // 2, 1) if BIG_SPEC[k]["axis"] == 1 else 1 for k in BIG}
    n_sem = sum(chunks[k] if BIG_SPEC[k]["axis"] == 1 else N_CHIPS for k in BIG)

    def body(*refs):
        g = dict(zip(BIG, refs[:n]))
        out = dict(zip(BIG, refs[n:2 * n]))
        send_sems, recv_sems = refs[2 * n:]
        x, y, c = _me()
        net = _Copies(send_sems, recv_sems)
        for k in BIG:
            hr = BIG_SPEC[k]["rows"] // 2
            for src, dst in zip(_half_pieces(g[k], k, 1 - c), _stacked_pieces(out[k], k)):
                for s, d in zip(_row_chunks(src, hr, chunks[k]), _row_chunks(dst, hr, chunks[k])):
                    net.send(s, d, (x, y, 1 - c), d)
        net.finish()

    outs = pl.pallas_call(
        body, name=name, out_shape=[jax.ShapeDtypeStruct(_half_shape(k), BF16) for k in BIG],
        in_specs=[HBM_SPEC] * n, out_specs=[HBM_SPEC] * n,
        scratch_shapes=[pltpu.SemaphoreType.DMA((n_sem,)), pltpu.SemaphoreType.DMA((n_sem,))],
    )(*[grads[k] for k in BIG])
    return dict(zip(BIG, outs))


def _add_half(g, other, k, core, name):
    sp = BIG_SPEC[k]
    hr, cols = sp["rows"] // 2, _full_shape(k)[1]
    tr = _pick(hr, (256, 352, 128))
    nb = hr // tr
    if sp["axis"] == 1:
        grid = (nb,)
        g_spec = pl.BlockSpec((tr, cols), lambda i, c: (c[0] * nb + i, 0))
        h_spec = pl.BlockSpec((tr, cols), lambda i, c: (i, 0))
    else:
        grid = (N_CHIPS, nb)
        g_spec = pl.BlockSpec((tr, cols), lambda s, i, c: ((2 * s + c[0]) * nb + i, 0))
        h_spec = pl.BlockSpec((tr, cols), lambda s, i, c: (s * nb + i, 0))

    def kern(c_ref, a_ref, b_ref, o_ref):
        o_ref[...] = (a_ref[...].astype(F32) + b_ref[...].astype(F32)).astype(BF16)

    return pl.pallas_call(
        kern, name=name, out_shape=jax.ShapeDtypeStruct(_half_shape(k), BF16),
        grid_spec=pltpu.PrefetchScalarGridSpec(num_scalar_prefetch=1, grid=grid, in_specs=[g_spec, h_spec],
                                               out_specs=h_spec),
        compiler_params=_cparams(("parallel",) * len(grid)),
    )(core, g, other)


def _part_shape(k):
    return N_CHIPS - 1, BIG_SPEC[k]["rows"] // 2, BIG_SPEC[k]["cols"]


def _scatter_copies(sums, parts, send_sems, recv_sems):
    x, y, c = _me()
    copies = []
    for i, k in enumerate(_present(sums)):
        for j, (px, py) in enumerate(_other_chips(x, y)):
            copies.append(pltpu.make_async_remote_copy(
                src_ref=_chip_part(sums[k], k, 2 * px + py), dst_ref=parts[k].at[j], send_sem=send_sems.at[3 * i + j],
                recv_sem=recv_sems.at[3 * i + j], device_id=(px, py, c), device_id_type=MESH))
    return copies


def _scatter_start(sums, tag):
    BIG = _present(sums)
    n = len(BIG)
    N_BEHIND = 3 * n
    lands = [pltpu.with_memory_space_constraint(lax.empty(_part_shape(k), BF16), pltpu.HBM) for k in BIG]

    def body(*refs):
        outs = refs[2 * n + 2:4 * n + 2]
        for cp in _scatter_copies(dict(zip(BIG, outs[:n])), dict(zip(BIG, outs[n:])), refs[2 * n], refs[2 * n + 1]):
            cp.start()
        refs[4 * n + 2][...] = jnp.zeros((8, 128), F32)

    outs = pl.pallas_call(
        body, name="scatter_start" + tag,
        out_shape=(pltpu.SemaphoreType.DMA((N_BEHIND,)), pltpu.SemaphoreType.DMA((N_BEHIND,)),
                   *[pltpu.HBM(_half_shape(k), BF16) for k in BIG], *[pltpu.HBM(_part_shape(k), BF16) for k in BIG],
                   jax.ShapeDtypeStruct((8, 128), F32)),
        in_specs=[HBM_SPEC] * (2 * n), out_specs=(SEM_SPEC, SEM_SPEC, *[HBM_SPEC] * (2 * n), VMEM_SPEC),
        input_output_aliases={i: i + 2 for i in range(2 * n)},
        compiler_params=pltpu.CompilerParams(has_side_effects=pltpu.SideEffectType.DATAFLOW_SIDE_EFFECTING),
    )(*[pltpu.with_memory_space_constraint(sums[k], pltpu.HBM) for k in BIG], *lands)
    return outs[0], outs[1], outs[2:2 * n + 2], outs[2 * n + 2]


def _scatter_wait(send_sems, recv_sems, inflight, keys, after, tag):
    BIG = keys
    n = len(BIG)

    def body(*refs):
        for cp in _scatter_copies(dict(zip(BIG, refs[:n])), dict(zip(BIG, refs[n:2 * n])), refs[2 * n], refs[2 * n + 1]):
            cp.wait_send()
            cp.wait_recv()

    outs = pl.pallas_call(
        body, name="scatter_wait" + tag,
        out_shape=(*[pltpu.HBM(_half_shape(k), BF16) for k in BIG], *[pltpu.HBM(_part_shape(k), BF16) for k in BIG]),
        in_specs=[HBM_SPEC] * (2 * n) + [SEM_SPEC, SEM_SPEC, pl.BlockSpec(memory_space=pl.ANY)],
        out_specs=(HBM_SPEC,) * (2 * n), input_output_aliases={i: i for i in range(2 * n)},
        compiler_params=pltpu.CompilerParams(has_side_effects=pltpu.SideEffectType.DATAFLOW_SIDE_EFFECTING),
    )(*inflight, send_sems, recv_sems, after)
    return dict(zip(BIG, outs[:n])), dict(zip(BIG, outs[n:]))


def _sum_half(parts, own, k, where, layer, into, name):
    sp = BIG_SPEC[k]
    rows, cols, keep = sp["rows"], sp["cols"], sp["keep"]
    hr = rows // 2
    tr = _pick(hr, (256, 352, 128))
    nb = hr // tr
    if sp["axis"] == 1:
        own_spec = pl.BlockSpec((tr, cols), lambda i, w: (i, w[0]))
    else:
        own_spec = pl.BlockSpec((tr, cols), lambda i, w: (w[0] * nb + i, 0))

    def kern(w_ref, p_ref, own_ref, *rest):
        tot = own_ref[...].astype(F32)
        for j in range(N_CHIPS - 1):
            tot = tot + p_ref[j].astype(F32)
        rest[-1][0] = tot[:, :keep]

    in_specs = [pl.BlockSpec((N_CHIPS - 1, tr, cols), lambda i, w: (0, i, 0)), own_spec]
    args = [where, parts, own]
    if into is not None:
        in_specs.append(pl.BlockSpec(memory_space=pl.ANY))
        args.append(into)
    return pl.pallas_call(
        kern, name=name, out_shape=jax.ShapeDtypeStruct((DEPTH, rows, keep), F32),
        grid_spec=pltpu.PrefetchScalarGridSpec(
            num_scalar_prefetch=1, grid=(nb,), in_specs=in_specs,
            out_specs=pl.BlockSpec((1, tr, keep), lambda i, w: (layer, w[1] * nb + i, 0))),
        input_output_aliases={} if into is None else {3: 0},
        compiler_params=_cparams(("parallel",)),
    )(*args)


def _exchange_halves(red):
    n = len(BIG)

    def body(*refs):
        out = dict(zip(BIG, refs[n:2 * n]))
        send_sems, recv_sems = refs[2 * n:]
        x, y, c = _me()
        net = _Copies(send_sems, recv_sems)
        for k in BIG:
            hr = BIG_SPEC[k]["rows"] // 2
            for l in range(DEPTH):
                mine = out[k].at[l, pl.ds(pl.multiple_of(c * hr, 8), hr), :]
                theirs = out[k].at[l, pl.ds(pl.multiple_of((1 - c) * hr, 8), hr), :]
                net.send(mine, mine, (x, y, 1 - c), theirs)
        net.finish()

    outs = pl.pallas_call(
        body, name="exchange_halves",
        out_shape=[jax.ShapeDtypeStruct((DEPTH, BIG_SPEC[k]["rows"], BIG_SPEC[k]["keep"]), F32) for k in BIG],
        in_specs=[HBM_SPEC] * n, out_specs=[HBM_SPEC] * n, input_output_aliases={i: i for i in range(n)},
        scratch_shapes=[pltpu.SemaphoreType.DMA((DEPTH * n,)), pltpu.SemaphoreType.DMA((DEPTH * n,))],
    )(*[red[k] for k in BIG])
    return dict(zip(BIG, outs))


def _adam_vals(g, w, m, v):
    m2 = ADAM_B1 * m + (1.0 - ADAM_B1) * g
    v2 = ADAM_B2 * v + (1.0 - ADAM_B2) * (g * g)
    m_hat = m2 / (1.0 - ADAM_B1 ** ADAM_STEP)
    v_hat = v2 / (1.0 - ADAM_B2 ** ADAM_STEP)
    return -ADAM_LR * (m_hat / (jnp.sqrt(v_hat) + ADAM_EPS) + ADAM_WD * w), m2, v2


def _allreduce_small_adam(groups):
    ng = len(groups)

    def body(*refs):
        ins = [refs[4 * i:4 * i + 4] for i in range(ng)]
        outs = [refs[4 * ng + 4 * i:4 * ng + 4 * i + 4] for i in range(ng)]
        bufs = refs[8 * ng:9 * ng]
        send_sems, recv_sems = refs[9 * ng:]
        x, y, c = _me()
        me = 4 * x + 2 * y + c
        net = _Copies(send_sems, recv_sems)
        for (g_ref, _, _, _), buf in zip(ins, bufs):
            buf[me] = g_ref[...]
            for f in FLIPS:
                px, py, pc = _peer(x, y, c, f)
                net.send(g_ref, buf.at[me], (px, py, pc), buf.at[4 * px + 2 * py + pc])
        net.finish()
        for (_, w_ref, m_ref, v_ref), (gs_ref, d_ref, nm_ref, nv_ref), buf in zip(ins, outs, bufs):
            tot = buf[0]
            for d in range(1, 8):
                tot = tot + buf[d]
            gs_ref[...] = tot
            d_ref[...], nm_ref[...], nv_ref[...] = _adam_vals(tot, w_ref[...], m_ref[...], v_ref[...])

    shapes = [jax.ShapeDtypeStruct(g[0].shape, F32) for g in groups for _ in range(4)]
    outs = pl.pallas_call(
        body, name="allreduce_small", out_shape=shapes, in_specs=[VMEM_SPEC] * (4 * ng), out_specs=[VMEM_SPEC] * (4 * ng),
        scratch_shapes=[pltpu.VMEM((8,) + g[0].shape, F32) for g in groups]
        + [pltpu.SemaphoreType.DMA((7 * ng,)), pltpu.SemaphoreType.DMA((7 * ng,))],
        compiler_params=pltpu.CompilerParams(vmem_limit_bytes=VMEM_LIMIT),
    )(*[t for g in groups for t in g])
    return [outs[4 * i:4 * i + 4] for i in range(ng)]


def _adam(g, w, m, v, name, lead_block=1):
    shape = w.shape
    lead, rows, cols = math.prod(shape[:-2]), shape[-2], shape[-1]
    tr = _pick(rows, (256, 352, 64, 8, rows))
    spec = pl.BlockSpec((lead_block, tr, cols), lambda l, i: (l, i, 0))

    def kern(g_ref, w_ref, m_ref, v_ref, d_ref, nm_ref, nv_ref):
        d_ref[...], nm_ref[...], nv_ref[...] = _adam_vals(g_ref[...], w_ref[...], m_ref[...], v_ref[...])

    outs = pl.pallas_call(
        kern, name=name, grid=(lead // lead_block, rows // tr), in_specs=[spec] * 4, out_specs=[spec] * 3,
        out_shape=[jax.ShapeDtypeStruct((lead, rows, cols), F32)] * 3, compiler_params=_cparams(("parallel", "parallel")),
    )(*[t.reshape(lead, rows, cols) for t in (g, w, m, v)])
    return [o.reshape(shape) for o in outs]


SMALL = ("attn_norm", "sgu_ln_g", "sgu_ln_b", "sgu_w", "sgu_b", "attn_sinks", "dn_a_log", "dn_dt_bias", "dn_norm",
         "ffn_norm", "final_norm")
SMALL_2D = {"attn_norm": (DEPTH, D_MODEL), "ffn_norm": (DEPTH, D_MODEL), "final_norm": (1, D_MODEL),
            "sgu_ln_g": (DEPTH, MIX), "sgu_ln_b": (DEPTH, MIX), "sgu_w": (DEPTH * SGU_G * SGU_T, SGU_T),
            "sgu_b": (DEPTH * SGU_G, SGU_T), "dn_norm": (DEPTH, DN_HD)}
TINY = ("attn_sinks", "dn_a_log", "dn_dt_bias")


def _pack_tiny(vals, extra=None):
    flat = [vals[k].astype(F32).reshape(-1) for k in TINY] + ([] if extra is None else [extra.astype(F32).reshape(-1)])
    n = sum(f.shape[0] for f in flat)
    return jnp.concatenate(flat + [jnp.zeros((8 * 128 - n,), F32)]).reshape(8, 128)


def _unpack_tiny(tile, shapes):
    flat, out, o = tile.reshape(-1), {}, 0
    for k in TINY:
        n = math.prod(shapes[k])
        out[k] = flat[o:o + n].reshape(shapes[k])
        o += n
    return out, flat[o]


def _in_col_segments():
    shard, padded = IN_COLS // N_CHIPS, BIG_SPEC["w_in"]["cols"]
    segs, mine = [], 0
    for a, n in IN_PIECES:
        o = a
        while o < a + n:
            end = min(a + n, (o // shard + 1) * shard)
            segs.append(((o // shard) * padded + o % shard, mine + o - a, end - o))
            o = end
        mine += n
    return segs


def _move_cols(x, segs, out_cols, name):
    layers, rows, cols = x.shape
    tr = _pick(rows, (256, rows))
    gaps, at = [], 0
    for d, w in sorted((d, w) for _, d, w in segs):
        if d > at:
            gaps.append((at, d - at))
        at = d + w
    if at < out_cols:
        gaps.append((at, out_cols - at))

    def kern(x_ref, o_ref):
        for s, d, w in segs:
            o_ref[0, :, d:d + w] = x_ref[0, :, s:s + w]
        for d, w in gaps:
            o_ref[0, :, d:d + w] = jnp.zeros((tr, w), x.dtype)

    return pl.pallas_call(
        kern, name=name, grid=(layers, rows // tr), in_specs=[pl.BlockSpec((1, tr, cols), lambda l, i: (l, i, 0))],
        out_specs=pl.BlockSpec((1, tr, out_cols), lambda l, i: (l, i, 0)),
        out_shape=jax.ShapeDtypeStruct((layers, rows, out_cols), x.dtype), compiler_params=_cparams(("parallel", "parallel")),
    )(x)


WEIGHTS = ("attn_norm", "w_in", "sgu_ln_g", "sgu_ln_b", "sgu_w", "sgu_b", "attn_sinks", "dn_conv_w", "dn_a_log",
           "dn_dt_bias", "dn_norm", "w_branch", "w_out", "ffn_norm", "w_gate_up", "w_down", "final_norm")


def kernel(x, positions, attn_norm, w_in, sgu_ln_g, sgu_ln_b, sgu_w, sgu_b, attn_sinks, dn_conv_w, dn_a_log, dn_dt_bias, dn_norm, w_branch, w_out, ffn_norm, w_gate_up, w_down, final_norm, loss_target, m_attn_norm, m_w_in, m_sgu_ln_g, m_sgu_ln_b, m_sgu_w, m_sgu_b, m_attn_sinks, m_dn_conv_w, m_dn_a_log, m_dn_dt_bias, m_dn_norm, m_w_branch, m_w_out, m_ffn_norm, m_w_gate_up, m_w_down, m_final_norm, v_attn_norm, v_w_in, v_sgu_ln_g, v_sgu_ln_b, v_sgu_w, v_sgu_b, v_attn_sinks, v_dn_conv_w, v_dn_a_log, v_dn_dt_bias, v_dn_norm, v_w_branch, v_w_out, v_ffn_norm, v_w_gate_up, v_w_down, v_final_norm):
    given = dict(locals())
    W = {k: given[k] for k in WEIGHTS}
    M = {k: given["m_" + k] for k in WEIGHTS}
    V = {k: given["v_" + k] for k in WEIGHTS}
    chip = 2 * lax.axis_index("x") + lax.axis_index("y")
    core = lax.axis_index("c")
    chip1 = chip.astype(jnp.int32).reshape(1)
    where = jnp.stack([chip, core]).astype(jnp.int32)

    placed = [{k: _place_shard(W[k].reshape(DEPTH, BIG_SPEC[k]["rows"], BIG_SPEC[k]["keep"]), k, chip1, l,
                               f"place{l}_{k}") for k in BIG} for l in range(DEPTH)]
    first, conv_full = _gather_layer({"w_in": placed[0]["w_in"]}, dn_conv_w.reshape(CONV_ROWS, CONV_COLS))
    behind = [_gather_start({k: placed[0][k] for k in BIG if k != "w_in"}, conv_full, "0")]
    behind.append(_gather_start(placed[1], behind[0][3], "1"))
    segs = _in_col_segments()

    def arrived(l, after):
        send_sems, recv_sems, inflight, _ = behind[l]
        return _gather_finish(_gather_wait(send_sems, recv_sems, inflight, after, str(l)), str(l))

    def big_of_layer(l, x_l):
        got = {} if l == 0 else arrived(1, x_l)
        w_in = first["w_in"] if l == 0 else got["w_in"]

        def rest(after):
            full = got or arrived(0, after)
            return dict(full, w_branch=full["w_branch"].reshape(3, MIX, D_MODEL))

        return dict(w_in=_move_cols(w_in[None], segs, IN_R, f"w_in_cols{l}")[0], rest=rest)

    w = {k: W[k] for k in SMALL}
    w["attn_norm"] = attn_norm + behind[1][3][0, 0]
    w["dn_conv_w"] = conv_full.reshape(DEPTH, DN_CONV, 3 * MIX)

    core1 = core.astype(jnp.int32).reshape(1)
    back_segs = [(d, s, n) for s, d, n in segs]
    travelling, started, sums, parts = [], [], [{}, {}], [{}, {}]

    def on_grads(l, gl):
        gl, tag = dict(gl), f"{l}_{len(gl)}"
        if "w_in" in gl:
            gl["w_in"] = _move_cols(gl["w_in"][None], back_segs, _full_shape("w_in")[1], f"g_in_cols{l}")[0]
        if "w_branch" in gl:
            gl["w_branch"] = gl["w_branch"].reshape(3 * MIX, D_MODEL)
        sibling = _halves_to_sibling(gl, "halves_to_sibling" + tag)
        chip_sums = {k: _add_half(gl[k], sibling[k], k, core1, f"chip_sum{l}_{k}") for k in gl}
        send_sems, recv_sems, inflight, token = _scatter_start(chip_sums, tag)
        travelling.append((l, send_sems, recv_sems, inflight, _present(gl), tag))
        started.append(token)
        return token

    loss, dx, g = _local_step(x[0], positions[0], loss_target[0], w, big_of_layer, on_grads)

    grads = {}
    conv_2d = (CONV_ROWS, N_CHIPS * CONV_COLS)
    conv_zero = jnp.zeros(conv_2d, F32)
    groups = [tuple(d[k].reshape(SMALL_2D[k]) for d in (g, W, M, V)) for k in SMALL_2D]
    groups.append((g["dn_conv_w"].reshape(conv_2d), conv_zero, conv_zero, conv_zero))
    loss = loss + started[-1][0, 0]
    groups.append((_pack_tiny(g, loss), _pack_tiny(W), _pack_tiny(M), _pack_tiny(V)))
    summed = _allreduce_small_adam(groups)
    delta, new_m, new_v = {}, {}, {}
    for k, outs in zip(SMALL_2D, summed):
        for d, t in zip((grads, delta, new_m, new_v), outs):
            d[k] = t.reshape(W[k].shape)
    conv_sum = summed[len(SMALL_2D)][0].reshape(g["dn_conv_w"].shape)
    grads["dn_conv_w"] = lax.dynamic_slice_in_dim(conv_sum, chip * dn_conv_w.shape[2], dn_conv_w.shape[2], axis=2)
    tiny_shapes = {k: W[k].shape for k in TINY}
    tiny, loss_total = _unpack_tiny(summed[-1][0], tiny_shapes)
    grads.update(tiny)
    for d, t in zip((delta, new_m, new_v), summed[-1][1:]):
        d.update(_unpack_tiny(t, tiny_shapes)[0])

    for l, send_sems, recv_sems, inflight, keys, tag in travelling:
        landed = _scatter_wait(send_sems, recv_sems, inflight, keys, summed[0][0], tag)
        sums[l].update(landed[0])
        parts[l].update(landed[1])
    red = {k: _sum_half(parts[1][k], sums[1][k], k, where, 1, None, f"sum1_{k}") for k in BIG}
    red = {k: _sum_half(parts[0][k], sums[0][k], k, where, 0, red[k], f"sum0_{k}") for k in BIG}
    reduced = _exchange_halves(red)
    grads.update({k: reduced[k].reshape(W[k].shape) for k in BIG})
    for k in ("w_branch", "w_out", "w_gate_up", "w_down", "dn_conv_w"):
        delta[k], new_m[k], new_v[k] = _adam(grads[k], W[k], M[k], V[k], "adam_" + k)
    lead_first = lambda t: jnp.transpose(t, (2, 0, 1))
    outs = _adam(*[lead_first(d["w_in"]) for d in (grads, W, M, V)], "adam_w_in", lead_block=IN_COLS // N_CHIPS // 10)
    delta["w_in"], new_m["w_in"], new_v["w_in"] = (jnp.transpose(o, (1, 2, 0)) for o in outs)

    return (loss_total, dx[None], *[grads[k] for k in WEIGHTS], *[delta[k] for k in WEIGHTS],
            *[new_m[k] for k in WEIGHTS], *[new_v[k] for k in WEIGHTS])
```

```python
import functools
import math

import jax
import jax.numpy as jnp
from jax import lax
from jax.experimental import pallas as pl
from jax.experimental.pallas import tpu as pltpu

F32 = jnp.float32
BF16 = jnp.bfloat16
HI = lax.Precision.HIGHEST

D_MODEL = 1024
DEPTH = 2
MIX = 512
EPS = 1e-6
SGU_G, SGU_T = 4, 128
SWA_H, SWA_KV, SWA_HD, WINDOW = 8, 2, 64, 128
ROPE_THETA, ROPE_DIM = 500000.0, 16
DN_H, DN_HD, DN_CONV, DN_C = 4, 128, 4, 64
D_FF = 2816
IN_COLS = 6920
IN_PIECES = ((3848, 3072), (1792, 1536), (3328, 512), (0, 512), (512, 512), (1024, 512), (1536, 128), (1664, 128),
             (3840, 8))
IN_PAD = 120
IN_R = 7040
C_GATE, C_QKV, C_ZC, C_UA, C_VA, C_QB, C_KB, C_VB, C_SM = 0, 3072, 4608, 5120, 5632, 6144, 6656, 6784, 6912

ADAM_LR, ADAM_B1, ADAM_B2, ADAM_EPS, ADAM_WD, ADAM_STEP = 0.001, 0.9, 0.999, 1e-08, 0.01, 10
VMEM_LIMIT = 56 * 1024 * 1024


def _cparams(sem):
    return pltpu.CompilerParams(dimension_semantics=sem, vmem_limit_bytes=VMEM_LIMIT)


def _dg(a, b, ca, cb, prec=None):
    return lax.dot_general(a, b, (((ca,), (cb,)), ((), ())), precision=prec, preferred_element_type=F32)


def _split(x):
    hi = x.astype(BF16)
    return hi, (x - hi.astype(F32)).astype(BF16)


def _dg3_many(as_, bs, ca, cb):
    sa = [_split(a) for a in as_]
    sb = [_split(b) for b in bs]
    hh = [_dg(a[0], b[0], ca, cb) for a, b in zip(sa, sb)]
    hl = [_dg(a[0], b[1], ca, cb) for a, b in zip(sa, sb)]
    lh = [_dg(a[1], b[0], ca, cb) for a, b in zip(sa, sb)]
    return [x + (y + z) for x, y, z in zip(hh, hl, lh)]


def _dg_exact_lhs_many(a01, bs, ca, cb):
    a = a01.astype(BF16)
    b1 = [b.astype(BF16) for b in bs]
    r1 = [b - t.astype(F32) for b, t in zip(bs, b1)]
    b2 = [r.astype(BF16) for r in r1]
    b3 = [(r - t.astype(F32)).astype(BF16) for r, t in zip(r1, b2)]
    d1 = [_dg(a, t, ca, cb) for t in b1]
    d2 = [_dg(a, t, ca, cb) for t in b2]
    d3 = [_dg(a, t, ca, cb) for t in b3]
    return [x + (y + z) for x, y, z in zip(d1, d2, d3)]


def _mm(a, b):
    return _dg(a.astype(BF16), b.astype(BF16), 1, 0)


def _mm_nt(a, b):
    return _dg(a.astype(BF16), b.astype(BF16), 1, 1)


def _mm_tn(a, b):
    return _dg(a.astype(BF16), b.astype(BF16), 0, 0)


def _sigmoid(x):
    return 0.5 * jnp.tanh(0.5 * x) + 0.5


def _silu(x):
    return x * _sigmoid(x)


def _dsilu(x):
    s = _sigmoid(x)
    return s * (1.0 + x * (1.0 - s))


_GC = math.sqrt(2.0 / math.pi)


def _gelu(x):
    return 0.5 * x * (1.0 + jnp.tanh(_GC * (x + 0.044715 * x * x * x)))


def _dgelu(x):
    t = jnp.tanh(_GC * (x + 0.044715 * x * x * x))
    return 0.5 * (1.0 + t) + 0.5 * x * (1.0 - t * t) * _GC * (1.0 + 3.0 * 0.044715 * x * x)


def _softplus(x):
    return jnp.maximum(x, 0.0) + jnp.log(1.0 + jnp.exp(-jnp.abs(x)))


def _acc(ref, val, i):
    @pl.when(i == 0)
    def _():
        ref[...] = val

    @pl.when(i > 0)
    def _():
        ref[...] += val


def _halo_rows(dtype):
    return 8 * 4 // jnp.dtype(dtype).itemsize


def _tok_call(body, name, S, TB, tok_in, const_in=(), tok_out=(), acc_out=(), prev_in=(), next_in=(), smem_in=()):
    nb = S // TB
    in_specs, args = [], []
    for a, w, cb in tok_in:
        in_specs.append(pl.BlockSpec((TB, w), functools.partial(lambda i, cb: (i, cb), cb=cb)))
        args.append(a)
    for a, w, cb in prev_in:
        hr = _halo_rows(a.dtype)
        in_specs.append(pl.BlockSpec((hr, w), functools.partial(
            lambda i, cb, r: (jnp.maximum(i * r - 1, 0), cb), cb=cb, r=TB // hr)))
        args.append(a)
    for a, w, cb in next_in:
        hr = _halo_rows(a.dtype)
        in_specs.append(pl.BlockSpec((hr, w), functools.partial(
            lambda i, cb, r, last: (jnp.minimum((i + 1) * r, last), cb), cb=cb, r=TB // hr, last=S // hr - 1)))
        args.append(a)
    for a in const_in:
        in_specs.append(pl.BlockSpec(a.shape, lambda i: (0, 0)))
        args.append(a)
    for a in smem_in:
        in_specs.append(pl.BlockSpec(memory_space=pltpu.SMEM))
        args.append(a)
    out_specs, out_shape, aliases, shared = [], [], {}, {}
    for o, (w, dt, *dest) in enumerate(tok_out):
        if not dest:
            out_specs.append(pl.BlockSpec((TB, w), lambda i: (i, 0)))
            out_shape.append(jax.ShapeDtypeStruct((S, w), dt))
            continue
        cb, wide = dest
        out_specs.append(pl.BlockSpec((TB, w), functools.partial(lambda i, cb: (i, cb), cb=cb)))
        out_shape.append(jax.ShapeDtypeStruct((S, wide if isinstance(wide, int) else wide.shape[1]), dt))
        if not isinstance(wide, int):
            if id(wide) not in shared:
                shared[id(wide)] = len(args)
                in_specs.append(pl.BlockSpec(memory_space=pl.ANY))
                args.append(wide)
            aliases[shared[id(wide)]] = o
    for shp, dt in acc_out:
        out_specs.append(pl.BlockSpec(shp, lambda i: (0, 0)))
        out_shape.append(jax.ShapeDtypeStruct(shp, dt))
    n_extra = len(shared)

    def kern(*refs):
        n_in = len(in_specs) - n_extra
        body(pl.program_id(0), *refs[:n_in], *refs[n_in + n_extra:])

    return pl.pallas_call(
        kern, name=name, grid=(nb,), in_specs=in_specs, out_specs=out_specs, out_shape=out_shape,
        input_output_aliases=aliases, compiler_params=_cparams(("arbitrary",)),
    )(*args)


MM_BLOCKS = (1024, 1408, 640, 512, 256, 128)


def _pick(n, cands):
    for c in cands:
        if n % c == 0:
            return c
    return n


MM_VMEM_BUDGET = 44 * 1024 * 1024


def _mm_blocks(M, N, K, a_bytes, b_bytes, o_bytes, add_bytes):
    bn = _pick(N, MM_BLOCKS)
    fits = None
    for bk in [K] + [c for c in (2816, 2048) + MM_BLOCKS if c < K and K % c == 0]:
        for bm in [c for c in (2048,) + MM_BLOCKS if M % c == 0 and c >= min(M, 512)]:
            b_bufs = 1 if (bk == K and bn == N) else 2
            need = 2 * bm * bk * a_bytes + b_bufs * bk * bn * b_bytes + 2 * bm * bn * (o_bytes + add_bytes)
            need += bm * bn * 4 if bk < K else 0
            if need <= MM_VMEM_BUDGET:
                fits = fits or (bm, bn, bk)
                if (M // bm) * (N // bn) * (K // bk) >= 4:
                    return bm, bn, bk
    if fits is None:
        raise ValueError(f"no matmul blocks for {(M, N, K)}")
    return fits


def _matmul(a, b, *, ta=False, tb=False, add=None, out_dtype=F32, name):
    M, K = (a.shape[1], a.shape[0]) if ta else a.shape
    N = b.shape[0] if tb else b.shape[1]
    bm, bn, bk = _mm_blocks(M, N, K, a.dtype.itemsize, b.dtype.itemsize, jnp.dtype(out_dtype).itemsize,
                            0 if add is None else add.dtype.itemsize)
    nk = K // bk
    b_mode = dict(pipeline_mode=pl.Buffered(1)) if (bk == K and bn == N) else {}
    a_spec = pl.BlockSpec((bk, bm), lambda i, j, k: (k, i)) if ta else pl.BlockSpec((bm, bk), lambda i, j, k: (i, k))
    b_spec = (pl.BlockSpec((bn, bk), lambda i, j, k: (j, k), **b_mode) if tb
              else pl.BlockSpec((bk, bn), lambda i, j, k: (k, j), **b_mode))
    o_spec = pl.BlockSpec((bm, bn), lambda i, j, k: (i, j))
    ca, cb = (0 if ta else 1), (1 if tb else 0)

    def kern(*refs):
        a_ref, b_ref = refs[:2]
        add_ref = refs[2] if add is not None else None
        o_ref = refs[3] if add is not None else refs[2]
        p = _dg(a_ref[...].astype(BF16), b_ref[...].astype(BF16), ca, cb)

        def finish(r):
            if add is not None:
                r = r + add_ref[...].astype(F32)
            o_ref[...] = r.astype(out_dtype)

        if nk == 1:
            finish(p)
            return
        acc_ref = refs[-1]
        k = pl.program_id(2)

        @pl.when(k == 0)
        def _():
            acc_ref[...] = p

        @pl.when((k > 0) & (k < nk - 1))
        def _():
            acc_ref[...] += p

        @pl.when(k == nk - 1)
        def _():
            finish(acc_ref[...] + p)

    in_specs = [a_spec, b_spec] + ([o_spec] if add is not None else [])
    args = (a, b) + ((add,) if add is not None else ())
    return pl.pallas_call(
        kern, name=name, grid=(M // bm, N // bn, nk), in_specs=in_specs, out_specs=o_spec,
        out_shape=jax.ShapeDtypeStruct((M, N), out_dtype),
        scratch_shapes=[pltpu.VMEM((bm, bn), F32)] if nk > 1 else [],
        compiler_params=_cparams(("parallel", "parallel", "arbitrary")),
    )(*args)


def _rms_fwd(x, g, name):
    S = x.shape[0]

    def body(i, x_ref, g_ref, h_ref):
        xv = x_ref[...]
        r = lax.rsqrt(jnp.mean(xv * xv, axis=-1, keepdims=True) + EPS)
        h_ref[...] = (xv * r * g_ref[...]).astype(BF16)

    return _tok_call(body, name, S, min(S, 512), [(x, D_MODEL, 0)], [g], [(D_MODEL, BF16)])[0]


def _rms_bwd_vals(xv, g, dh):
    r = lax.rsqrt(jnp.mean(xv * xv, axis=-1, keepdims=True) + EPS)
    u = dh * g
    dx = r * u - xv * (r * r * r) * jnp.mean(u * xv, axis=-1, keepdims=True)
    dg = jnp.sum(dh * xv * r, axis=0, keepdims=True)
    return dx, dg


def _rms_bwd_add(x, g, dh, dres, name):
    S = x.shape[0]

    def body(i, x_ref, dh_ref, dr_ref, g_ref, dx_ref, dg_ref):
        dx, dg = _rms_bwd_vals(x_ref[...], g_ref[...], dh_ref[...].astype(F32))
        dx_ref[...] = dr_ref[...] + dx
        _acc(dg_ref, dg, i)

    return _tok_call(body, name, S, min(S, 512), [(x, D_MODEL, 0), (dh, D_MODEL, 0), (dres, D_MODEL, 0)], [g],
                     [(D_MODEL, F32)], [((1, D_MODEL), F32)])


def _final_loss(x, g, target):
    S = x.shape[0]

    def body(i, x_ref, t_ref, g_ref, dx_ref, loss_ref, dg_ref):
        xv, gv = x_ref[...], g_ref[...]
        r = lax.rsqrt(jnp.mean(xv * xv, axis=-1, keepdims=True) + EPS)
        e = xv * r * gv - t_ref[...]
        part = 0.5 * jnp.sum(jnp.mean(e * e, axis=-1, keepdims=True), axis=0, keepdims=True)
        dx, dg = _rms_bwd_vals(xv, gv, e * (1.0 / D_MODEL))
        dx_ref[...] = dx
        _acc(loss_ref, jnp.broadcast_to(part, (1, 128)), i)
        _acc(dg_ref, dg, i)

    return _tok_call(body, "final_loss", S, min(S, 512), [(x, D_MODEL, 0), (target, D_MODEL, 0)], [g],
                     [(D_MODEL, F32)], [((1, 128), F32), ((1, D_MODEL), F32)])


def _swiglu_fwd(gu, name):
    S = gu.shape[0]

    def body(i, gu_ref, a_ref):
        a_ref[...] = (_silu(gu_ref[:, :D_FF].astype(F32)) * gu_ref[:, D_FF:].astype(F32)).astype(BF16)

    return _tok_call(body, name, S, min(S, 256), [(gu, 2 * D_FF, 0)], [], [(D_FF, BF16)])[0]


def _swiglu_bwd(gu, dact, name):
    S = gu.shape[0]

    def body(i, gu_ref, da_ref, dgu_ref):
        gg, uu, da = gu_ref[:, :D_FF].astype(F32), gu_ref[:, D_FF:].astype(F32), da_ref[...].astype(F32)
        dgu_ref[:, :D_FF] = (da * uu * _dsilu(gg)).astype(BF16)
        dgu_ref[:, D_FF:] = (da * _silu(gg)).astype(BF16)

    return _tok_call(body, name, S, min(S, 256), [(gu, 2 * D_FF, 0), (dact, D_FF, 0)], [], [(2 * D_FF, BF16)])[0]


def _merge_fwd(proj, bds, name):
    S = proj.shape[0]

    def body(i, g0, g1, g2, b0, b1, b2, m_ref):
        m = jnp.zeros(m_ref.shape, F32)
        for gr, br in ((g0, b0), (g1, b1), (g2, b2)):
            m = m + _sigmoid(gr[...].astype(F32)) * br[...].astype(F32)
        m_ref[...] = m.astype(BF16)

    tok = [(proj, D_MODEL, n) for n in range(3)] + [(b, D_MODEL, 0) for b in bds]
    return _tok_call(body, name, S, min(S, 512), tok, [], [(D_MODEL, BF16)])[0]


def _merge_bwd(proj, bds, dm, name):
    S = proj.shape[0]

    def body(i, g0, g1, g2, b0, b1, b2, dm_ref, d0, d1, d2, dgp_ref):
        dmv = dm_ref[...]
        for n, (gr, br, dr) in enumerate(((g0, b0, d0), (g1, b1, d1), (g2, b2, d2))):
            s = _sigmoid(gr[...].astype(F32))
            dr[...] = (dmv * s).astype(BF16)
            dgp_ref[:, n * D_MODEL:(n + 1) * D_MODEL] = (dmv * br[...].astype(F32) * s * (1.0 - s)).astype(BF16)

    tok = [(proj, D_MODEL, n) for n in range(3)] + [(b, D_MODEL, 0) for b in bds] + [(dm, D_MODEL, 0)]
    return _tok_call(body, name, S, min(S, 512), tok, [],
                     [(D_MODEL, BF16)] * 3 + [(3 * D_MODEL, BF16, C_GATE // (3 * D_MODEL), IN_R)])


def _sgu_ln(v, lg, lb):
    mu = jnp.mean(v, axis=-1, keepdims=True)
    vc = v - mu
    rstd = lax.rsqrt(jnp.mean(vc * vc, axis=-1, keepdims=True) + EPS)
    vhat = vc * rstd
    return vhat, rstd, vhat * lg + lb


def _sgu_fwd(proj, lg, lb, wc, bst, name):
    S = proj.shape[0]

    def body(i, ua_ref, va_ref, lg_ref, lb_ref, wc_ref, bs_ref, o_ref):
        u = _gelu(ua_ref[...].astype(F32))
        _, _, vn = _sgu_ln(_gelu(va_ref[...].astype(F32)), lg_ref[...], lb_ref[...])
        for g in range(SGU_G):
            sl = slice(g * 128, (g + 1) * 128)
            mixed = _mm(wc_ref[sl, :], vn[:, sl]) + bs_ref[:, g:g + 1]
            o_ref[:, sl] = (u[:, sl] * mixed).astype(BF16)

    return _tok_call(body, name, S, SGU_T, [(proj, MIX, C_UA // MIX), (proj, MIX, C_VA // MIX)], [lg, lb, wc, bst],
                     [(MIX, BF16)])[0]


def _sgu_bwd(proj, lg, lb, wc, bst, dout, dproj, name):
    S = proj.shape[0]

    def body(i, ua_ref, va_ref, do_ref, lg_ref, lb_ref, wc_ref, bs_ref, duv_ref, dlg_ref, dlb_ref, dwc_ref,
             dbs_ref):
        ua, va, do = ua_ref[...].astype(F32), va_ref[...].astype(F32), do_ref[...].astype(F32)
        u = _gelu(ua)
        lgv = lg_ref[...]
        vhat, rstd, vn = _sgu_ln(_gelu(va), lgv, lb_ref[...])
        tril = lax.broadcasted_iota(jnp.int32, (128, 128), 0) >= lax.broadcasted_iota(jnp.int32, (128, 128), 1)
        lane4 = lax.broadcasted_iota(jnp.int32, (128, 4), 1)
        gs = range(SGU_G)
        sls = [slice(g * 128, (g + 1) * 128) for g in gs]
        wgs = [wc_ref[sl, :] for sl in sls]
        mixed = [_mm(wgs[g], vn[:, sls[g]]) for g in gs]
        dmix = [do[:, sl] * u[:, sl] for sl in sls]
        dwg = [_mm_nt(dmix[g], vn[:, sls[g]]) for g in gs]
        dvn = jnp.concatenate([_mm_tn(wgs[g], dmix[g]) for g in gs], axis=1)
        dbs = jnp.zeros((128, 4), F32)
        for g in gs:
            duv_ref[:, sls[g]] = (do[:, sls[g]] * (mixed[g] + bs_ref[:, g:g + 1]) * _dgelu(ua[:, sls[g]])).astype(BF16)
            dbs = dbs + jnp.where(lane4 == g, jnp.sum(dmix[g], axis=-1, keepdims=True), 0.0)
            _acc(dwc_ref.at[sls[g], :], jnp.where(tril, dwg[g], 0.0), i)
        _acc(dbs_ref, dbs, i)
        _acc(dlg_ref, jnp.sum(dvn * vhat, axis=0, keepdims=True), i)
        _acc(dlb_ref, jnp.sum(dvn, axis=0, keepdims=True), i)
        dvh = dvn * lgv
        dv = rstd * (dvh - jnp.mean(dvh, axis=-1, keepdims=True) - vhat * jnp.mean(dvh * vhat, axis=-1, keepdims=True))
        duv_ref[:, MIX:] = (dv * _dgelu(va)).astype(BF16)

    return _tok_call(body, name, S, SGU_T, [(proj, MIX, C_UA // MIX), (proj, MIX, C_VA // MIX), (dout, MIX, 0)],
                     [lg, lb, wc, bst], [(2 * MIX, BF16, C_UA // (2 * MIX), dproj)],
                     [((1, MIX), F32), ((1, MIX), F32), ((SGU_G * 128, 128), F32), ((128, 4), F32)])


def _rope_tables(positions):
    S = positions.shape[0]
    inv_freq = ROPE_THETA ** (-jnp.arange(0, ROPE_DIM, 2, dtype=F32) / ROPE_DIM)
    ang = positions.astype(F32)[:, None] * inv_freq
    c, s = jnp.cos(ang), jnp.sin(ang)
    c64 = jnp.concatenate([c, c, jnp.ones((S, SWA_HD - ROPE_DIM), F32)], axis=1)
    s64 = jnp.concatenate([-s, s, jnp.zeros((S, SWA_HD - ROPE_DIM), F32)], axis=1)
    return jnp.tile(c64, (1, 2)), jnp.tile(s64, (1, 2))


def _rope128(x, c, s):
    lane = lax.broadcasted_iota(jnp.int32, x.shape, 1) % SWA_HD
    swapped = jnp.where(lane < ROPE_DIM // 2, pltpu.roll(x, 128 - ROPE_DIM // 2, 1), pltpu.roll(x, ROPE_DIM // 2, 1))
    return x * c + swapped * s


def _rope_t128(y, c, s):
    ys = y * s
    lane = lax.broadcasted_iota(jnp.int32, y.shape, 1) % SWA_HD
    swapped = jnp.where(lane < ROPE_DIM // 2, pltpu.roll(ys, 128 - ROPE_DIM // 2, 1), pltpu.roll(ys, ROPE_DIM // 2, 1))
    return y * c + jnp.where(lane < ROPE_DIM, swapped, 0.0)


def _rope_fwd(proj, cos, sin, name):
    S = proj.shape[0]
    scale = SWA_HD ** -0.5

    def body(i, q_ref, k_ref, v_ref, c_ref, s_ref, qo_ref, ko_ref, vo_ref):
        c, s = c_ref[...], s_ref[...]
        for j in range(4):
            sl = slice(j * 128, (j + 1) * 128)
            qo_ref[:, sl] = (_rope128(q_ref[:, sl].astype(F32), c, s) * scale).astype(BF16)
        ko_ref[...] = _rope128(k_ref[...].astype(F32), c, s).astype(BF16)
        vo_ref[...] = v_ref[...].astype(BF16)

    return _tok_call(body, name, S, min(S, 512),
                     [(proj, MIX, C_QB // MIX), (proj, 128, C_KB // 128), (proj, 128, C_VB // 128), (cos, 128, 0),
                      (sin, 128, 0)], [], [(MIX, BF16), (128, BF16), (128, BF16)])


def _rope_bwd(dq, dk, dv, cos, sin, dproj, name):
    S = dq.shape[0]
    scale = SWA_HD ** -0.5
    width = C_SM - C_QB

    def body(i, dq_ref, dk_ref, dv_ref, c_ref, s_ref, o_ref):
        c, s = c_ref[...], s_ref[...]
        for j in range(4):
            sl = slice(j * 128, (j + 1) * 128)
            o_ref[:, sl] = _rope_t128(dq_ref[:, sl] * scale, c, s).astype(BF16)
        o_ref[:, C_KB - C_QB:C_VB - C_QB] = _rope_t128(dk_ref[...], c, s).astype(BF16)
        o_ref[:, C_VB - C_QB:] = dv_ref[...].astype(BF16)

    return _tok_call(body, name, S, min(S, 512),
                     [(dq, MIX, 0), (dk, 128, 0), (dv, 128, 0), (cos, 128, 0), (sin, 128, 0)], [],
                     [(width, BF16, C_QB // width, dproj)])[0]


def _swa_band(i, k_ref, v_ref):
    pstart = pl.multiple_of(jnp.maximum(i - 1, 0) * WINDOW, WINDOW)
    cstart = pl.multiple_of(i * WINDOW, WINDOW)
    kb = jnp.concatenate([k_ref[pl.ds(pstart, WINDOW), :], k_ref[pl.ds(cstart, WINDOW), :]], axis=0)
    vb = jnp.concatenate([v_ref[pl.ds(pstart, WINDOW), :], v_ref[pl.ds(cstart, WINDOW), :]], axis=0)
    qi = lax.broadcasted_iota(jnp.int32, (WINDOW, 2 * WINDOW), 0)
    sj = lax.broadcasted_iota(jnp.int32, (WINDOW, 2 * WINDOW), 1)
    mask = (sj > qi) & (sj <= qi + WINDOW) & ((i > 0) | (sj >= WINDOW))
    return kb, vb, mask, pstart, cstart


def _swa_probs(qs, kh, mask, sinks):
    logits = [jnp.where(mask, _dg(qh, kh, 1, 1), -1e30) for qh in qs]
    m = [jnp.maximum(jnp.max(l, axis=-1, keepdims=True), s) for l, s in zip(logits, sinks)]
    p = [jnp.exp(l - mm) for l, mm in zip(logits, m)]
    ps = [jnp.exp(s - mm) for s, mm in zip(sinks, m)]
    inv = [1.0 / (jnp.sum(pp, axis=-1, keepdims=True) + s) for pp, s in zip(p, ps)]
    return [pp * iv for pp, iv in zip(p, inv)], [s * iv for s, iv in zip(ps, inv)]


def _swa_fwd(q, k, v, sinks, name):
    S = q.shape[0]
    G = SWA_H // SWA_KV

    def body(i, q_ref, k_ref, v_ref, s_ref, o_ref):
        kb, vb, mask, _, _ = _swa_band(i, k_ref, v_ref)
        qv = q_ref[...]
        for kv in range(SWA_KV):
            ksl = slice(kv * SWA_HD, (kv + 1) * SWA_HD)
            heads = range(kv * G, (kv + 1) * G)
            pn, _ = _swa_probs([qv[:, h * SWA_HD:(h + 1) * SWA_HD] for h in heads], kb[:, ksl], mask,
                               [s_ref[0, h] for h in heads])
            outs = [_dg(p.astype(BF16), vb[:, ksl], 1, 0) for p in pn]
            for h, o in zip(heads, outs):
                o_ref[:, h * SWA_HD:(h + 1) * SWA_HD] = o.astype(BF16)

    return _tok_call(body, name, S, WINDOW, [(q, MIX, 0)], [k, v], [(MIX, BF16)], smem_in=[sinks])[0]


def _swa_bwd(q, k, v, sinks, dout, name):
    S = q.shape[0]

    def body(i, q_ref, do_ref, k_ref, v_ref, s_ref, dq_ref, dk_ref, dv_ref, ds_ref):
        kb, vb, mask, pstart, cstart = _swa_band(i, k_ref, v_ref)
        qv, dov = q_ref[...], do_ref[...]
        lane = lax.broadcasted_iota(jnp.int32, (1, 128), 1)
        dsink = jnp.zeros((1, 128), F32)
        dkb, dvb = [], []
        G = SWA_H // SWA_KV
        for kv in range(SWA_KV):
            ksl = slice(kv * SWA_HD, (kv + 1) * SWA_HD)
            heads = range(kv * G, (kv + 1) * G)
            qs = [qv[:, h * SWA_HD:(h + 1) * SWA_HD] for h in heads]
            dos = [dov[:, h * SWA_HD:(h + 1) * SWA_HD].astype(BF16) for h in heads]
            pn, psn = _swa_probs(qs, kb[:, ksl], mask, [s_ref[0, h] for h in heads])
            dp = [_dg(d, vb[:, ksl], 1, 1) for d in dos]
            delta = [jnp.sum(a * b, axis=-1, keepdims=True) for a, b in zip(dp, pn)]
            dsc = [(p * (a - d)).astype(BF16) for p, a, d in zip(pn, dp, delta)]
            dqs = [_dg(s, kb[:, ksl], 1, 0) for s in dsc]
            dks = [_dg(s, qh, 0, 0) for s, qh in zip(dsc, qs)]
            dvs = [_dg(p.astype(BF16), d, 0, 0) for p, d in zip(pn, dos)]
            for n_, h in enumerate(heads):
                dq_ref[:, h * SWA_HD:(h + 1) * SWA_HD] = dqs[n_]
                dsink = dsink + jnp.where(lane == h, -jnp.sum(psn[n_] * delta[n_], axis=0, keepdims=True), 0.0)
            dkb.append((dks[0] + dks[1]) + (dks[2] + dks[3]))
            dvb.append((dvs[0] + dvs[1]) + (dvs[2] + dvs[3]))
        dkb = jnp.concatenate(dkb, axis=1)
        dvb = jnp.concatenate(dvb, axis=1)

        @pl.when(i == 0)
        def _():
            dk_ref[...] = jnp.zeros_like(dk_ref)
            dv_ref[...] = jnp.zeros_like(dv_ref)

        dk_ref[pl.ds(pstart, WINDOW), :] += dkb[:WINDOW]
        dv_ref[pl.ds(pstart, WINDOW), :] += dvb[:WINDOW]
        dk_ref[pl.ds(cstart, WINDOW), :] += dkb[WINDOW:]
        dv_ref[pl.ds(cstart, WINDOW), :] += dvb[WINDOW:]
        _acc(ds_ref, dsink, i)

    return _tok_call(body, name, S, WINDOW, [(q, MIX, 0), (dout, MIX, 0)], [k, v], [(MIX, F32)],
                     [((S, 128), F32), ((S, 128), F32), ((1, 128), F32)], smem_in=[sinks])


def _shift_rows(xs, k):
    return xs if k == 0 else pltpu.roll(xs, k, 0)


def _dn_conv(x_ref, p_ref, w_ref, i):
    hr = p_ref.shape[0]
    halo = jnp.where(i > 0, p_ref[...].astype(F32), 0.0)
    xs = jnp.concatenate([halo, x_ref[...].astype(F32)], axis=0)
    sh = [_shift_rows(xs, DN_CONV - 1 - t)[hr:] for t in range(DN_CONV)]
    pre = sh[0] * w_ref[0:1, :]
    for t in range(1, DN_CONV):
        pre = pre + sh[t] * w_ref[t:t + 1, :]
    return pre, sh


def _dn_gates(sm, alog, dtb):
    lane = lax.broadcasted_iota(jnp.int32, sm.shape, 1)
    return jnp.where(lane < DN_H, _sigmoid(sm), -jnp.exp(alog) * _softplus(sm + dtb))


def _dn_pre_fwd(proj, conv_w, alog_l, dtb_l, name):
    S = proj.shape[0]
    scale = DN_HD ** -0.5

    def body(i, x_ref, sm_ref, p_ref, w_ref, al_ref, db_ref, q_ref, k_ref, v_ref, bg_ref):
        pre, _ = _dn_conv(x_ref, p_ref, w_ref, i)
        a = _silu(pre)
        for h in range(DN_H):
            sl = slice(h * DN_HD, (h + 1) * DN_HD)
            qh, kh = a[:, sl], a[:, MIX + h * DN_HD:MIX + (h + 1) * DN_HD]
            q_ref[:, sl] = qh * (lax.rsqrt(jnp.sum(qh * qh, axis=-1, keepdims=True) + EPS) * scale)
            k_ref[:, sl] = kh * lax.rsqrt(jnp.sum(kh * kh, axis=-1, keepdims=True) + EPS)
        v_ref[...] = a[:, 2 * MIX:]
        bg_ref[...] = _dn_gates(sm_ref[...].astype(F32), al_ref[...], db_ref[...])

    TB = min(S, 256)
    return _tok_call(body, name, S, TB, [(proj, 3 * MIX, C_QKV // (3 * MIX)), (proj, 128, C_SM // 128)],
                     [conv_w, alog_l, dtb_l], [(MIX, F32), (MIX, F32), (MIX, F32), (128, F32)],
                     prev_in=[(proj, 3 * MIX, C_QKV // (3 * MIX))])


def _dn_pre_bwd1(proj, conv_w, alog_l, dtb_l, dq, dk, dv, dbg, dproj, name):
    S = proj.shape[0]
    scale = DN_HD ** -0.5

    def body(i, x_ref, sm_ref, dq_ref, dk_ref, dv_ref, dbg_ref, p_ref, w_ref, al_ref, db_ref, dpre_ref, dsm_ref,
             dw_ref, dal_ref, ddb_ref):
        pre, sh = _dn_conv(x_ref, p_ref, w_ref, i)
        a = _silu(pre)
        da_parts = []
        for part, (g_ref, sc) in enumerate(((dq_ref, scale), (dk_ref, 1.0))):
            for h in range(DN_H):
                xh = a[:, part * MIX + h * DN_HD:part * MIX + (h + 1) * DN_HD]
                rs = lax.rsqrt(jnp.sum(xh * xh, axis=-1, keepdims=True) + EPS)
                y = xh * rs
                dy = g_ref[:, h * DN_HD:(h + 1) * DN_HD] * sc
                da_parts.append(rs * (dy - y * jnp.sum(dy * y, axis=-1, keepdims=True)))
        da_parts.append(dv_ref[...])
        dpre = jnp.concatenate(da_parts, axis=1) * _dsilu(pre)
        dpre_ref[...] = dpre
        dw = jnp.concatenate([jnp.sum(dpre * sh[t], axis=0, keepdims=True) for t in range(DN_CONV)], axis=0)
        _acc(dw_ref, dw, i)
        sm, al, db, dbg_v = sm_ref[...].astype(F32), al_ref[...], db_ref[...], dbg_ref[...]
        lane = lax.broadcasted_iota(jnp.int32, sm.shape, 1)
        sg = _sigmoid(sm)
        gneg = -jnp.exp(al)
        is_g = (lane >= DN_H) & (lane < 2 * DN_H)
        d_al = jnp.where(is_g, dbg_v * gneg * _sigmoid(sm + db), 0.0)
        dsm_ref[...] = jnp.where(lane < DN_H, dbg_v * sg * (1.0 - sg), d_al).astype(BF16)
        _acc(ddb_ref, jnp.sum(d_al, axis=0, keepdims=True), i)
        _acc(dal_ref, jnp.sum(jnp.where(is_g, dbg_v * gneg * _softplus(sm + db), 0.0), axis=0, keepdims=True), i)

    TB = min(S, 256)
    return _tok_call(body, name, S, TB,
                     [(proj, 3 * MIX, C_QKV // (3 * MIX)), (proj, 128, C_SM // 128), (dq, MIX, 0), (dk, MIX, 0),
                      (dv, MIX, 0), (dbg, 128, 0)], [conv_w, alog_l, dtb_l],
                     [(3 * MIX, F32), (128, BF16, C_SM // 128, dproj)],
                     [((DN_CONV, 3 * MIX), F32), ((1, 128), F32), ((1, 128), F32)],
                     prev_in=[(proj, 3 * MIX, C_QKV // (3 * MIX))])


def _dn_pre_bwd2(dpre, conv_w, dproj, name):
    S = dpre.shape[0]
    TB = min(S, 256)
    nb = S // TB

    def body(i, d_ref, n_ref, w_ref, o_ref):
        halo = jnp.where(i < nb - 1, n_ref[...], 0.0)
        ds = jnp.concatenate([d_ref[...], halo], axis=0)
        out = ds[:TB] * w_ref[DN_CONV - 1:DN_CONV, :]
        for t in range(DN_CONV - 1):
            k = DN_CONV - 1 - t
            out = out + pltpu.roll(ds, TB + 8 - k, 0)[:TB] * w_ref[t:t + 1, :]
        o_ref[...] = out.astype(BF16)

    return _tok_call(body, name, S, TB, [(dpre, 3 * MIX, 0)], [conv_w],
                     [(3 * MIX, BF16, C_QKV // (3 * MIX), dproj)], next_in=[(dpre, 3 * MIX, 0)])[0]


def _dn_decay_terms(bgs, heads):
    C = DN_C
    ri = lax.broadcasted_iota(jnp.int32, (C, C), 0)
    ci = lax.broadcasted_iota(jnp.int32, (C, C), 1)
    tril, eye = ri >= ci, ri == ci
    beta = [b[:, h:h + 1] for b, h in zip(bgs, heads)]
    gcol = _dg_exact_lhs_many(tril, [jnp.broadcast_to(b[:, DN_H + h:DN_H + h + 1], (C, C))
                                     for b, h in zip(bgs, heads)], 1, 0)
    grow = [jnp.sum(jnp.where(eye, g, 0.0), axis=0, keepdims=True) for g in gcol]
    decay = [jnp.exp(jnp.where(tril, g - r, -1e30)) for g, r in zip(gcol, grow)]
    e_gc = [jnp.exp(g[:, 0:1]) for g in gcol]
    e_kd = [jnp.exp(g[C - 1:C, 0:1] - g[:, 0:1]) for g in gcol]
    cdec = [jnp.exp(g[C - 1:C, 0:1]) for g in gcol]
    return beta, decay, e_gc, e_kd, cdec


def _dn_nb(S):
    return 4 if S % (4 * DN_C) == 0 else 1


def _dn_prep_fwd(q, k, v, bg, name):
    S = q.shape[0]
    C, NB = DN_C, _dn_nb(S)
    TB = NB * C

    def kern(q_ref, k_ref, v_ref, bg_ref, t_ref, uw_ref, at_ref, qd_ref, kd_ref, dec_ref):
        lane = lax.broadcasted_iota(jnp.int32, (C, 128), 1)
        ri = lax.broadcasted_iota(jnp.int32, (C, C), 0)
        ci = lax.broadcasted_iota(jnp.int32, (C, C), 1)
        tril, eye = ri >= ci, ri == ci
        chains = [(cb, h) for cb in range(NB) for h in range(DN_H)]
        rows = lambda cb: slice(cb * C, (cb + 1) * C)
        head = lambda h: slice(h * DN_HD, (h + 1) * DN_HD)
        beta, decay, e_gc, e_kd, cdec = _dn_decay_terms([bg_ref[rows(cb), :] for cb, _ in chains],
                                                        [h for _, h in chains])
        qs = [q_ref[rows(cb), head(h)] for cb, h in chains]
        ks = [k_ref[rows(cb), head(h)] for cb, h in chains]
        kb = [kh * b for kh, b in zip(ks, beta)]
        x = [-jnp.where(ri > ci, _mm_nt(a, kh) * d, 0.0) for a, kh, d in zip(kb, ks, decay)]
        tm = [jnp.where(eye, 1.0, 0.0) + xi for xi in x]
        p = x
        p = _dg3_many(p, p, 1, 0)
        for it in range(5):
            if it == 4:
                tm = [t + tp for t, tp in zip(tm, _dg3_many(tm, p, 1, 0))]
                break
            both = _dg3_many([jnp.concatenate([t, pp], axis=0) for t, pp in zip(tm, p)], p, 1, 0)
            tm = [t + b[:C] for t, b in zip(tm, both)]
            p = [b[C:] for b in both]
        rhs = [jnp.concatenate([v_ref[rows(cb), head(h)] * b, a * e], axis=1)
               for (cb, h), b, a, e in zip(chains, beta, kb, e_gc)]
        sol = _dg3_many(tm, rhs, 1, 0)
        attn = [_mm_nt(qh, kh) * d for qh, kh, d in zip(qs, ks, decay)]
        for n_, (cb, h) in enumerate(chains):
            rs, sl, hc = rows(cb), head(h), slice(h * C, (h + 1) * C)
            t_ref[rs, hc] = tm[n_]
            uw_ref[rs, sl] = sol[n_][:, :DN_HD]
            uw_ref[rs, MIX + h * DN_HD:MIX + (h + 1) * DN_HD] = sol[n_][:, DN_HD:]
            at_ref[rs, hc] = attn[n_]
            qd_ref[rs, sl] = (qs[n_] * e_gc[n_]).astype(BF16)
            kd_ref[rs, sl] = (ks[n_] * e_kd[n_]).astype(BF16)
        for cb in range(NB):
            dec = jnp.zeros((C, 128), F32)
            for h in range(DN_H):
                dec = dec + jnp.where(lane == h, cdec[cb * DN_H + h], 0.0)
            dec_ref[rows(cb), :] = dec

    tok = lambda w: pl.BlockSpec((TB, w), lambda i: (i, 0))
    return pl.pallas_call(
        kern, name=name, grid=(S // TB,), in_specs=[tok(MIX), tok(MIX), tok(MIX), tok(128)],
        out_specs=[tok(DN_H * C), tok(2 * MIX), tok(DN_H * C), tok(MIX), tok(MIX), tok(128)],
        out_shape=[jax.ShapeDtypeStruct((S, DN_H * C), F32), jax.ShapeDtypeStruct((S, 2 * MIX), F32),
                   jax.ShapeDtypeStruct((S, DN_H * C), F32), jax.ShapeDtypeStruct((S, MIX), BF16),
                   jax.ShapeDtypeStruct((S, MIX), BF16), jax.ShapeDtypeStruct((S, 128), F32)],
        compiler_params=_cparams(("parallel",)),
    )(q, k, v, bg)


def _dn_scan_fwd(uw, at, qd, kd, dec, name):
    S = uw.shape[0]
    C, NB = DN_C, _dn_nb(S)
    TB = NB * C
    SR = DN_H * DN_HD

    def kern(uw_ref, at_ref, qd_ref, kd_ref, dec_ref, o_ref, vn_ref, st_ref, state):
        @pl.when(pl.program_id(0) == 0)
        def _():
            state[...] = jnp.zeros_like(state)

        for cb in range(NB):
            rs = slice(cb * C, (cb + 1) * C)
            hs = range(DN_H)
            sls = [slice(h * DN_HD, (h + 1) * DN_HD) for h in hs]
            s_in = [state[sl, :] for sl in sls]
            ws = [_mm(uw_ref[rs, MIX + h * DN_HD:MIX + (h + 1) * DN_HD], s_in[h]) for h in hs]
            os_ = [_mm(qd_ref[rs, sls[h]], s_in[h]) for h in hs]
            vnew = [uw_ref[rs, sls[h]] - ws[h] for h in hs]
            oa = [_mm(at_ref[rs, h * C:(h + 1) * C], vnew[h]) for h in hs]
            kv = [_mm_tn(kd_ref[rs, sls[h]], vnew[h]) for h in hs]
            for h in hs:
                o_ref[rs, sls[h]] = os_[h] + oa[h]
                state[sls[h], :] = s_in[h] * dec_ref[cb * C:cb * C + 1, h:h + 1] + kv[h]
                st_ref[cb * SR + h * DN_HD:cb * SR + (h + 1) * DN_HD, :] = s_in[h]
                vn_ref[rs, sls[h]] = vnew[h]

    tok = lambda w: pl.BlockSpec((TB, w), lambda i: (i, 0))
    return pl.pallas_call(
        kern, name=name, grid=(S // TB,), in_specs=[tok(2 * MIX), tok(DN_H * C), tok(MIX), tok(MIX), tok(128)],
        out_specs=[tok(MIX), tok(MIX), pl.BlockSpec((NB * SR, DN_HD), lambda i: (i, 0))],
        out_shape=[jax.ShapeDtypeStruct((S, MIX), F32), jax.ShapeDtypeStruct((S, MIX), F32),
                   jax.ShapeDtypeStruct((S // C * SR, DN_HD), F32)],
        scratch_shapes=[pltpu.VMEM((SR, DN_HD), F32)],
        compiler_params=_cparams(("arbitrary",)),
    )(uw, at, qd, kd, dec)


def _dn_core_fwd(q, k, v, bg, name):
    tm, uw, at, qd, kd, dec = _dn_prep_fwd(q, k, v, bg, name + "_prep")
    o, vn, st = _dn_scan_fwd(uw, at, qd, kd, dec, name + "_scan")
    return o, dict(tm=tm, uw=uw, at=at, qd=qd, kd=kd, dec=dec, vn=vn, st=st)


def _dn_scan_bwd(sv, do, name):
    S = do.shape[0]
    C, NB = DN_C, _dn_nb(S)
    TB = NB * C
    SR = DN_H * DN_HD
    nb = S // TB

    def kern(do_ref, uw_ref, at_ref, qd_ref, kd_ref, dec_ref, vn_ref, st_ref, dvn_ref, dw_ref, dkd_ref, dc_ref, dstate):
        @pl.when(pl.program_id(0) == 0)
        def _():
            dstate[...] = jnp.zeros_like(dstate)

        lane = lax.broadcasted_iota(jnp.int32, (C, 128), 1)
        for cb in reversed(range(NB)):
            rs = slice(cb * C, (cb + 1) * C)
            dcrow = jnp.zeros((C, 128), F32)
            for h in range(DN_H):
                sl = slice(h * DN_HD, (h + 1) * DN_HD)
                doh, ds_o = do_ref[rs, sl], dstate[sl, :]
                s_in = st_ref[cb * SR + h * DN_HD:cb * SR + (h + 1) * DN_HD, :]
                d_vnew = _mm_tn(at_ref[rs, h * C:(h + 1) * C], doh) + _mm(kd_ref[rs, sl], ds_o)
                dvn_ref[rs, sl] = d_vnew
                dw_ref[rs, sl] = -_mm_nt(d_vnew, s_in)
                dkd_ref[rs, sl] = _mm_nt(vn_ref[rs, sl], ds_o)
                d_c = jnp.sum(jnp.sum(ds_o * s_in, axis=1, keepdims=True), axis=0, keepdims=True)
                dcrow = dcrow + jnp.where(lane == h, d_c, 0.0)
                dstate[sl, :] = (ds_o * dec_ref[cb * C:cb * C + 1, h:h + 1] + _mm_tn(qd_ref[rs, sl], doh)
                                 - _mm_tn(uw_ref[rs, MIX + h * DN_HD:MIX + (h + 1) * DN_HD], d_vnew))
            dc_ref[rs, :] = dcrow

    tok = lambda w: pl.BlockSpec((TB, w), lambda i: (nb - 1 - i, 0))
    return pl.pallas_call(
        kern, name=name, grid=(nb,),
        in_specs=[tok(MIX), tok(2 * MIX), tok(DN_H * C), tok(MIX), tok(MIX), tok(128), tok(MIX),
                  pl.BlockSpec((NB * SR, DN_HD), lambda i: (nb - 1 - i, 0))],
        out_specs=[tok(MIX), tok(MIX), tok(MIX), tok(128)],
        out_shape=[jax.ShapeDtypeStruct((S, MIX), F32)] * 3 + [jax.ShapeDtypeStruct((S, 128), F32)],
        scratch_shapes=[pltpu.VMEM((SR, DN_HD), F32)],
        compiler_params=_cparams(("arbitrary",)),
    )(do, sv["uw"], sv["at"], sv["qd"], sv["kd"], sv["dec"], sv["vn"], sv["st"])


def _dn_chunk_bwd(q, k, v, bg, sv, do, dvn, dw, dkd, dc, name):
    S = q.shape[0]
    C, NB = DN_C, _dn_nb(S)
    TB = NB * C
    SR = DN_H * DN_HD

    def kern(q_ref, k_ref, v_ref, bg_ref, t_ref, uw_ref, vn_ref, st_ref, do_ref, dvn_ref, dw_ref, dkd_ref, dc_ref,
             dq_ref, dk_ref, dv_ref, dbg_ref):
        lane = lax.broadcasted_iota(jnp.int32, (C, 128), 1)
        ri = lax.broadcasted_iota(jnp.int32, (C, C), 0)
        ci = lax.broadcasted_iota(jnp.int32, (C, C), 1)
        tril, eye, last = ri >= ci, ri == ci, ri[:, 0:1] == C - 1
        chains = [(cb, h) for cb in range(NB) for h in range(DN_H)]
        each = lambda f, *ls: [f(*a) for a in zip(*ls)]
        rsum = lambda t: jnp.sum(t, axis=-1, keepdims=True)
        rows = lambda cb: slice(cb * C, (cb + 1) * C)
        head = lambda h: slice(h * DN_HD, (h + 1) * DN_HD)
        tok = lambda ref: [ref[rows(cb), head(h)] for cb, h in chains]
        beta, decay, e_gc, e_kd, cdec = _dn_decay_terms([bg_ref[rows(cb), :] for cb, _ in chains],
                                                        [h for _, h in chains])
        qs, ks, vs, dos, vnew, d_kd = tok(q_ref), tok(k_ref), tok(v_ref), tok(do_ref), tok(vn_ref), tok(dkd_ref)
        s_in = [st_ref[cb * SR + h * DN_HD:cb * SR + (h + 1) * DN_HD, :] for cb, h in chains]
        d_c = [dc_ref[cb * C:cb * C + 1, h:h + 1] for cb, h in chains]
        kb = each(lambda a, b: a * b, ks, beta)
        kk = each(_mm_nt, kb, ks)
        attn = each(lambda a, b, d: _mm_nt(a, b) * d, qs, ks, decay)
        d_qd = each(_mm_nt, dos, s_in)
        d_attn = each(_mm_nt, dos, vnew)
        d_sol = [jnp.concatenate([dvn_ref[rows(cb), head(h)], dw_ref[rows(cb), head(h)]], axis=1) for cb, h in chains]
        sol = [jnp.concatenate([uw_ref[rows(cb), head(h)], uw_ref[rows(cb), MIX + h * DN_HD:MIX + (h + 1) * DN_HD]],
                               axis=1) for cb, h in chains]
        d_rhs = _dg3_many([t_ref[rows(cb), h * C:(h + 1) * C] for cb, h in chains], d_sol, 0, 0)
        d_a = _dg3_many(d_rhs, sol, 1, 1)
        d_kk = each(lambda a, d: jnp.where(ri > ci, -a, 0.0) * d, d_a, decay)
        d_qk = each(lambda a, d: a * d, d_attn, decay)
        dm = each(lambda a, b, c_, d: a * b + c_ * d, d_kk, kk, d_attn, attn)
        d_vb = [t[:, :DN_HD] for t in d_rhs]
        dz = [t[:, DN_HD:] for t in d_rhs]
        d_kb = each(lambda z, e, a, kh: z * e + _mm(a, kh), dz, e_gc, d_kk, ks)
        d_k = each(lambda a, b, c_, q: _mm_tn(a, b) + _mm_tn(c_, q), d_kk, kb, d_qk, qs)
        d_q = each(lambda a, kh, b, e: _mm(a, kh) + b * e, d_qk, ks, d_qd, e_gc)
        t_kd = each(lambda a, kh, e: rsum(a * kh * e), d_kd, ks, e_kd)
        d_gl = each(lambda t, c_, cd: jnp.sum(t, axis=0, keepdims=True) + c_ * cd, t_kd, d_c, cdec)
        d_gc = each(lambda z, a, e, m, b, q, t, gl:
                    rsum(z * a) * e + rsum(m) - rsum(jnp.where(eye, jnp.sum(m, axis=0, keepdims=True), 0.0))
                    + rsum(b * q) * e - t + jnp.where(last, gl, 0.0),
                    dz, kb, e_gc, dm, d_qd, qs, t_kd, d_gl)
        d_g = _dg_exact_lhs_many(ri <= ci, [jnp.broadcast_to(t, (C, 128)) for t in d_gc], 1, 0)
        d_beta = each(lambda a, v_, b, kh: rsum(a * v_) + rsum(b * kh), d_vb, vs, d_kb, ks)
        for n_, (cb, h) in enumerate(chains):
            dq_ref[rows(cb), head(h)] = d_q[n_]
            dk_ref[rows(cb), head(h)] = d_k[n_] + d_kd[n_] * e_kd[n_] + d_kb[n_] * beta[n_]
            dv_ref[rows(cb), head(h)] = d_vb[n_] * beta[n_]
        for cb in range(NB):
            dbg = jnp.zeros((C, 128), F32)
            for h in range(DN_H):
                n_ = cb * DN_H + h
                dbg = dbg + jnp.where(lane == h, d_beta[n_], 0.0) + jnp.where(lane == DN_H + h, d_g[n_], 0.0)
            dbg_ref[rows(cb), :] = dbg

    tok = lambda w: pl.BlockSpec((TB, w), lambda i: (i, 0))
    return pl.pallas_call(
        kern, name=name, grid=(S // TB,),
        in_specs=[tok(MIX), tok(MIX), tok(MIX), tok(128), tok(DN_H * C), tok(2 * MIX), tok(MIX),
                  pl.BlockSpec((NB * SR, DN_HD), lambda i: (i, 0)), tok(MIX), tok(MIX), tok(MIX), tok(MIX), tok(128)],
        out_specs=[tok(MIX), tok(MIX), tok(MIX), tok(128)],
        out_shape=[jax.ShapeDtypeStruct((S, MIX), F32)] * 3 + [jax.ShapeDtypeStruct((S, 128), F32)],
        compiler_params=_cparams(("parallel",)),
    )(q, k, v, bg, sv["tm"], sv["uw"], sv["vn"], sv["st"], do, dvn, dw, dkd, dc)


def _dn_core_bwd(q, k, v, bg, sv, do, name):
    dvn, dw, dkd, dc = _dn_scan_bwd(sv, do, name + "_scan")
    return _dn_chunk_bwd(q, k, v, bg, sv, do, dvn, dw, dkd, dc, name + "_chunk")


def _dn_post_fwd(o, proj, ng, name):
    S = o.shape[0]

    def body(i, o_ref, z_ref, g_ref, out_ref):
        gv = g_ref[...]
        for h in range(DN_H):
            sl = slice(h * DN_HD, (h + 1) * DN_HD)
            oh = o_ref[:, sl]
            r = lax.rsqrt(jnp.mean(oh * oh, axis=-1, keepdims=True) + EPS)
            out_ref[:, sl] = (oh * r * gv * _silu(z_ref[:, sl].astype(F32))).astype(BF16)

    return _tok_call(body, name, S, min(S, 512), [(o, MIX, 0), (proj, MIX, C_ZC // MIX)], [ng], [(MIX, BF16)])[0]


def _dn_post_bwd(o, proj, ng, dout, dproj, name):
    S = o.shape[0]

    def body(i, o_ref, z_ref, do_ref, g_ref, dov_ref, dz_ref, dg_ref):
        gv = g_ref[...]
        dg = jnp.zeros((1, DN_HD), F32)
        for h in range(DN_H):
            sl = slice(h * DN_HD, (h + 1) * DN_HD)
            oh, zh, dh = o_ref[:, sl], z_ref[:, sl].astype(F32), do_ref[:, sl].astype(F32)
            r = lax.rsqrt(jnp.mean(oh * oh, axis=-1, keepdims=True) + EPS)
            dz_ref[:, sl] = (dh * oh * r * gv * _dsilu(zh)).astype(BF16)
            dx, dgh = _rms_bwd_vals(oh, gv, dh * _silu(zh))
            dov_ref[:, sl] = dx
            dg = dg + dgh
        _acc(dg_ref, dg, i)

    return _tok_call(body, name, S, min(S, 512), [(o, MIX, 0), (proj, MIX, C_ZC // MIX), (dout, MIX, 0)], [ng],
                     [(MIX, F32), (MIX, BF16, C_ZC // MIX, dproj)], [((1, DN_HD), F32)])


def _layer_params(w, big, l):
    lane = jnp.arange(128)
    is_g = (lane >= DN_H) & (lane < 2 * DN_H)
    spread = lambda t: jnp.where(is_g, jnp.tile(t, 128 // DN_H), 0.0).reshape(1, 128)
    tril = jnp.tril(jnp.ones((SGU_T, SGU_T), bool))
    return dict(
        win=big["w_in"], rest=big["rest"], conv=w["dn_conv_w"][l], attn_norm=w["attn_norm"][l].reshape(1, -1), ffn_norm=w["ffn_norm"][l].reshape(1, -1),
        lg=w["sgu_ln_g"][l].reshape(1, -1), lb=w["sgu_ln_b"][l].reshape(1, -1),
        wc=jnp.where(tril, w["sgu_w"][l], 0.0).reshape(SGU_G * SGU_T, SGU_T), bst=w["sgu_b"][l].T,
        sinks=w["attn_sinks"][l].reshape(1, -1), alog=spread(w["dn_a_log"][l]), dtb=spread(w["dn_dt_bias"][l]),
        ng=w["dn_norm"][l].reshape(1, -1))


def _layer_fwd(x, p, cos, sin, l):
    n = lambda s: f"l{l}_{s}"
    h = _rms_fwd(x, p["attn_norm"], n("rms1"))
    if callable(p["win"]):
        p["win"] = p["win"](h)
    proj = _matmul(h, p["win"], out_dtype=BF16, name=n("mm_in"))
    out_a = _sgu_fwd(proj, p["lg"], p["lb"], p["wc"], p["bst"], n("sgu_fwd"))
    qr, kr, vr = _rope_fwd(proj, cos, sin, n("rope_fwd"))
    out_b = _swa_fwd(qr, kr, vr, p["sinks"], n("swa_fwd"))
    q, k, v, bg = _dn_pre_fwd(proj, p["conv"], p["alog"], p["dtb"], n("dn_pre_fwd"))
    o, dn = _dn_core_fwd(q, k, v, bg, n("dn_core_fwd"))
    out_c = _dn_post_fwd(o, proj, p["ng"], n("dn_post_fwd"))
    outs = (out_a, out_b, out_c)
    rest = p.pop("rest")(out_c)
    p.update(wb=rest["w_branch"], wout=rest["w_out"], wgu=rest["w_gate_up"], wdown=rest["w_down"])
    bds = [_matmul(outs[j], p["wb"][j], out_dtype=BF16, name=n(f"mm_branch{j}")) for j in range(3)]
    merged = _merge_fwd(proj, bds, n("merge_fwd"))
    x1 = _matmul(merged, p["wout"], add=x, name=n("mm_out"))
    h2 = _rms_fwd(x1, p["ffn_norm"], n("rms2"))
    gu = _matmul(h2, p["wgu"], out_dtype=BF16, name=n("mm_gu"))
    act = _swiglu_fwd(gu, n("swiglu_fwd"))
    x2 = _matmul(act, p["wdown"], add=x1, name=n("mm_down"))
    saved = dict(x=x, h=h, proj=proj, outs=outs, qr=qr, kr=kr, vr=vr, q=q, k=k, v=v, bg=bg, o=o, dn=dn, bds=bds,
                 merged=merged, x1=x1, h2=h2, gu=gu, act=act)
    return x2, saved


def _layer_bwd(dx2, s, p, cos, sin, l, early=None):
    n = lambda t: f"l{l}_{t}"
    proj = s["proj"]
    g = {}
    g["w_down"] = _matmul(s["act"], dx2, ta=True, out_dtype=BF16, name=n("wg_down"))
    dact = _matmul(dx2, p["wdown"], tb=True, out_dtype=BF16, name=n("dg_down"))
    dgu = _swiglu_bwd(s["gu"], dact, n("swiglu_bwd"))
    g["w_gate_up"] = _matmul(s["h2"], dgu, ta=True, out_dtype=BF16, name=n("wg_gu"))
    dh2 = _matmul(dgu, p["wgu"], tb=True, name=n("dg_gu"))
    dx1, g["ffn_norm"] = _rms_bwd_add(s["x1"], p["ffn_norm"], dh2, dx2, n("rms2_bwd"))
    g["w_out"] = _matmul(s["merged"], dx1, ta=True, out_dtype=BF16, name=n("wg_out"))
    dm = _matmul(dx1, p["wout"], tb=True, name=n("dg_out"))
    dbd0, dbd1, dbd2, dproj = _merge_bwd(proj, s["bds"], dm, n("merge_bwd"))
    dbds = (dbd0, dbd1, dbd2)
    g["w_branch"] = jnp.stack([_matmul(s["outs"][j], dbds[j], ta=True, out_dtype=BF16, name=n(f"wg_branch{j}"))
                               for j in range(3)])
    douts = [_matmul(dbds[j], p["wb"][j], tb=True, name=n(f"dg_branch{j}")) for j in range(3)]
    lg = p["lg"]
    if early is not None:
        token = early({k: g.pop(k) for k in ("w_down", "w_gate_up", "w_out", "w_branch")})
        lg = lg if token is None else lg + token[0, 0]
    dproj, g["sgu_ln_g"], g["sgu_ln_b"], dwc, dbs = _sgu_bwd(proj, lg, p["lb"], p["wc"], p["bst"], douts[0], dproj,
                                                             n("sgu_bwd"))
    g["sgu_w"] = dwc.reshape(SGU_G, SGU_T, SGU_T)
    g["sgu_b"] = dbs.T
    dqr, dkr, dvr, dsink = _swa_bwd(s["qr"], s["kr"], s["vr"], p["sinks"], douts[1], n("swa_bwd"))
    g["attn_sinks"] = dsink[0, :SWA_H]
    dproj = _rope_bwd(dqr, dkr, dvr, cos, sin, dproj, n("rope_bwd"))
    do, dproj, dng = _dn_post_bwd(s["o"], proj, p["ng"], douts[2], dproj, n("dn_post_bwd"))
    g["dn_norm"] = dng[0]
    dq, dk, dv, dbg = _dn_core_bwd(s["q"], s["k"], s["v"], s["bg"], s["dn"], do, n("dn_core_bwd"))
    dpre, dproj, g["dn_conv_w"], dal, ddb = _dn_pre_bwd1(proj, p["conv"], p["alog"], p["dtb"], dq, dk, dv, dbg, dproj,
                                                         n("dn_pre_bwd1"))
    g["dn_a_log"] = dal[0, DN_H:2 * DN_H]
    g["dn_dt_bias"] = ddb[0, DN_H:2 * DN_H]
    dproj = _dn_pre_bwd2(dpre, p["conv"], dproj, n("dn_pre_bwd2"))
    g["w_in"] = _matmul(s["h"], dproj, ta=True, out_dtype=BF16, name=n("wg_in"))
    attn_norm = p["attn_norm"]
    if early is not None:
        token = early({"w_in": g.pop("w_in")})
        attn_norm = attn_norm if token is None else attn_norm + token[0, 0]
    dh = _matmul(dproj, p["win"], tb=True, name=n("dg_in"))
    dx, g["attn_norm"] = _rms_bwd_add(s["x"], attn_norm, dh, dx1, n("rms1_bwd"))
    g["attn_norm"], g["ffn_norm"] = g["attn_norm"][0], g["ffn_norm"][0]
    g["sgu_ln_g"], g["sgu_ln_b"] = g["sgu_ln_g"][0], g["sgu_ln_b"][0]
    return dx, g


def _local_step(x, positions, target, w, big_of_layer, on_grads):
    cos, sin = _rope_tables(positions)
    params, saves, xs = [], [], x
    for l in range(DEPTH):
        params.append(_layer_params(w, big_of_layer(l, xs), l))
        xs, sv = _layer_fwd(xs, params[l], cos, sin, l)
        saves.append(sv)
    dx, loss_row, dgf = _final_loss(xs, w["final_norm"].reshape(1, -1), target)
    grads = [None] * DEPTH
    for l in reversed(range(DEPTH)):
        early = functools.partial(on_grads, l) if l == 0 else None
        dx, grads[l] = _layer_bwd(dx, saves[l], params[l], cos, sin, l, early)
        left = {k: grads[l].pop(k) for k in BIG if k in grads[l]}
        token = on_grads(l, left) if left else None
        if token is not None and l > 0:
            params[l - 1] = dict(params[l - 1], ffn_norm=params[l - 1]["ffn_norm"] + token[0, 0])
    stacked = {k: jnp.stack([grads[l][k] for l in range(DEPTH)]) for k in grads[0]}
    stacked["final_norm"] = dgf[0]
    return loss_row[0, 0], dx, stacked


MESH = pl.DeviceIdType.MESH
HBM_SPEC = pl.BlockSpec(memory_space=pltpu.HBM)
VMEM_SPEC = pl.BlockSpec(memory_space=pltpu.VMEM)
N_CHIPS = 4
FLIPS = tuple((fx, fy, fc) for fx in (0, 1) for fy in (0, 1) for fc in (0, 1))[1:]
BIG = ("w_in", "w_branch", "w_out", "w_gate_up", "w_down")
BIG_SPEC = {
    "w_in": dict(rows=1024, cols=1792, axis=1, keep=1730, down=8),
    "w_branch": dict(rows=1536, cols=256, axis=1, keep=256, down=2),
    "w_out": dict(rows=256, cols=1024, axis=0, keep=1024, down=1),
    "w_gate_up": dict(rows=1024, cols=1408, axis=1, keep=1408, down=8),
    "w_down": dict(rows=704, cols=1024, axis=0, keep=1024, down=4),
}
CONV_ROWS, CONV_COLS = DEPTH * DN_CONV, 3 * MIX // N_CHIPS


def _full_shape(k):
    sp = BIG_SPEC[k]
    return (sp["rows"], N_CHIPS * sp["cols"]) if sp["axis"] == 1 else (N_CHIPS * sp["rows"], sp["cols"])


def _me():
    return lax.axis_index("x"), lax.axis_index("y"), lax.axis_index("c")


def _peer(x, y, c, flip):
    fx, fy, fc = flip
    return (1 - x if fx else x, 1 - y if fy else y, 1 - c if fc else c)


class _Copies:
    def __init__(self, send_sems, recv_sems):
        self.send_sems, self.recv_sems, self.k, self.sent, self.landing = send_sems, recv_sems, 0, [], []

    def _copy(self, k, src, dst, to):
        return pltpu.make_async_remote_copy(src_ref=src, dst_ref=dst, send_sem=self.send_sems.at[k],
                                            recv_sem=self.recv_sems.at[k], device_id=to, device_id_type=MESH)

    def send(self, src, dst, to, lands):
        k = self.k
        self.k += 1
        cp = self._copy(k, src, dst, to)
        cp.start()
        self.sent.append(cp)
        self.landing.append(self._copy(k, lands, lands, to))
        return k

    def wait_landed(self, k):
        self.landing[k].wait_recv()

    def finish(self, landed=()):
        for k, cp in enumerate(self.landing):
            if k not in landed:
                cp.wait_recv()
        for cp in self.sent:
            cp.wait_send()


def _place_shard(shard, k, chip, layer, name):
    sp = BIG_SPEC[k]
    rows, cols, keep = sp["rows"], sp["cols"], sp["keep"]
    tr = _pick(rows, (256, 64))
    nb = rows // tr
    if sp["axis"] == 1:
        out_spec = pl.BlockSpec((tr, cols), lambda i, ch: (i, ch[0]))
    else:
        out_spec = pl.BlockSpec((tr, cols), lambda i, ch: (ch[0] * nb + i, 0))

    def kern(ch_ref, x_ref, o_ref):
        v = x_ref[0].astype(BF16)
        if keep == cols:
            o_ref[...] = v
        else:
            o_ref[:, :keep] = v
            o_ref[:, keep:] = jnp.zeros((tr, cols - keep), BF16)

    return pl.pallas_call(
        kern, name=name, out_shape=jax.ShapeDtypeStruct(_full_shape(k), BF16),
        grid_spec=pltpu.PrefetchScalarGridSpec(
            num_scalar_prefetch=1, grid=(nb,),
            in_specs=[pl.BlockSpec((1, tr, keep), lambda i, ch: (layer, i, 0))], out_specs=out_spec),
        compiler_params=_cparams(("parallel",)),
    )(chip, shard)


def _half_block(ref, k, s, half):
    sp = BIG_SPEC[k]
    hr = sp["rows"] // 2
    if sp["axis"] == 1:
        return ref.at[pl.ds(pl.multiple_of(half * hr, 16), hr), pl.ds(pl.multiple_of(s * sp["cols"], 128), sp["cols"])]
    return ref.at[pl.ds(pl.multiple_of(s * sp["rows"] + half * hr, 16), hr), :]


def _other_chips(x, y):
    return [(1 - x, y), (x, 1 - y), (1 - x, 1 - y)]


ALL_BIG = BIG


def _present(d):
    return tuple(k for k in ALL_BIG if k in d)


def _gather_layer(placed, conv):
    BIG = _present(placed)
    n = len(BIG)
    n_sem = 6 * n + 3

    def body(*refs):
        conv_ref = refs[n]
        out = dict(zip(BIG, refs[n + 1:2 * n + 1]))
        conv_out, send_sems, recv_sems, local_sem = refs[2 * n + 1:]
        x, y, c = _me()
        me = 2 * x + y
        chips = _other_chips(x, y)
        net = _Copies(send_sems, recv_sems)

        def conv_block(s):
            return conv_out.at[:, pl.ds(pl.multiple_of(s * CONV_COLS, 128), CONV_COLS)]

        local = pltpu.make_async_copy(conv_ref, conv_block(me), local_sem)
        local.start()
        first = {}
        for k in BIG:
            for j, (px, py) in enumerate(chips):
                first[k, j] = net.send(_half_block(out[k], k, me, c), _half_block(out[k], k, me, c), (px, py, c),
                                       _half_block(out[k], k, 2 * px + py, c))
        for px, py in chips:
            net.send(conv_ref, conv_block(me), (px, py, c), conv_block(2 * px + py))
        for k in BIG:
            for j, (px, py) in enumerate(chips):
                net.wait_landed(first[k, j])
                net.send(_half_block(out[k], k, 2 * px + py, c), _half_block(out[k], k, 2 * px + py, c), (x, y, 1 - c),
                         _half_block(out[k], k, 2 * px + py, 1 - c))
        net.finish(landed=set(first.values()))
        local.wait()

    out_shape = [jax.ShapeDtypeStruct(_full_shape(k), BF16) for k in BIG]
    out_shape.append(jax.ShapeDtypeStruct((CONV_ROWS, N_CHIPS * CONV_COLS), F32))
    outs = pl.pallas_call(
        body, name="gather_layer", out_shape=out_shape, in_specs=[HBM_SPEC] * (n + 1), out_specs=[HBM_SPEC] * (n + 1),
        input_output_aliases={i: i for i in range(n)},
        scratch_shapes=[pltpu.SemaphoreType.DMA((n_sem,)), pltpu.SemaphoreType.DMA((n_sem,)), pltpu.SemaphoreType.DMA],
    )(*[placed[k] for k in BIG], conv)
    return dict(zip(BIG, outs[:n])), outs[n]


SEM_SPEC = pl.BlockSpec(memory_space=pltpu.SEMAPHORE)


def _behind_copies(arrs, send_sems, recv_sems):
    x, y, c = _me()
    copies = []
    for i, k in enumerate(_present(arrs)):
        for j, (px, py) in enumerate(_other_chips(x, y)):
            copies.append(pltpu.make_async_remote_copy(
                src_ref=_half_block(arrs[k], k, 2 * x + y, c), dst_ref=_half_block(arrs[k], k, 2 * x + y, c),
                send_sem=send_sems.at[3 * i + j], recv_sem=recv_sems.at[3 * i + j], device_id=(px, py, c),
                device_id_type=MESH))
    return copies


def _gather_start(placed, after, tag):
    BIG = _present(placed)
    n = len(BIG)
    N_BEHIND = 3 * n

    def body(*refs):
        arrs = dict(zip(BIG, refs[n + 3:2 * n + 3]))
        send_sems, recv_sems = refs[n + 1], refs[n + 2]
        for cp in _behind_copies(arrs, send_sems, recv_sems):
            cp.start()
        refs[2 * n + 3][...] = jnp.zeros((8, 128), F32)

    outs = pl.pallas_call(
        body, name="gather_start" + tag,
        out_shape=(pltpu.SemaphoreType.DMA((N_BEHIND,)), pltpu.SemaphoreType.DMA((N_BEHIND,)),
                   *[pltpu.HBM(_full_shape(k), BF16) for k in BIG], jax.ShapeDtypeStruct((8, 128), F32)),
        in_specs=[HBM_SPEC] * n + [pl.BlockSpec(memory_space=pl.ANY)],
        out_specs=(SEM_SPEC, SEM_SPEC, *[HBM_SPEC] * n, VMEM_SPEC),
        input_output_aliases={i: i + 2 for i in range(n)},
        compiler_params=pltpu.CompilerParams(has_side_effects=pltpu.SideEffectType.DATAFLOW_SIDE_EFFECTING),
    )(*[pltpu.with_memory_space_constraint(placed[k], pltpu.HBM) for k in BIG], after)
    return outs[0], outs[1], dict(zip(BIG, outs[2:n + 2])), outs[n + 2]


def _gather_wait(send_sems, recv_sems, inflight, after, tag):
    BIG = _present(inflight)
    n = len(BIG)

    def body(*refs):
        arrs = dict(zip(BIG, refs[:n]))
        for cp in _behind_copies(arrs, refs[n], refs[n + 1]):
            cp.wait_send()
            cp.wait_recv()

    outs = pl.pallas_call(
        body, name="gather_wait" + tag, out_shape=tuple(pltpu.HBM(_full_shape(k), BF16) for k in BIG),
        in_specs=[HBM_SPEC] * n + [SEM_SPEC, SEM_SPEC, pl.BlockSpec(memory_space=pl.ANY)], out_specs=(HBM_SPEC,) * n,
        input_output_aliases={i: i for i in range(n)},
        compiler_params=pltpu.CompilerParams(has_side_effects=pltpu.SideEffectType.DATAFLOW_SIDE_EFFECTING),
    )(*[inflight[k] for k in BIG], send_sems, recv_sems, after)
    return dict(zip(BIG, outs))


def _gather_finish(arrs, tag):
    BIG = _present(arrs)
    n = len(BIG)
    N_BEHIND = 3 * n

    def body(*refs):
        out = dict(zip(BIG, refs[n:2 * n]))
        send_sems, recv_sems = refs[2 * n:]
        x, y, c = _me()
        net = _Copies(send_sems, recv_sems)
        for k in BIG:
            for px, py in _other_chips(x, y):
                net.send(_half_block(out[k], k, 2 * px + py, c), _half_block(out[k], k, 2 * px + py, c), (x, y, 1 - c),
                         _half_block(out[k], k, 2 * px + py, 1 - c))
        net.finish()

    outs = pl.pallas_call(
        body, name="gather_finish" + tag, out_shape=[jax.ShapeDtypeStruct(_full_shape(k), BF16) for k in BIG],
        in_specs=[HBM_SPEC] * n, out_specs=[HBM_SPEC] * n, input_output_aliases={i: i for i in range(n)},
        scratch_shapes=[pltpu.SemaphoreType.DMA((N_BEHIND,)), pltpu.SemaphoreType.DMA((N_BEHIND,))],
    )(*[arrs[k] for k in BIG])
    return dict(zip(BIG, outs))


def _row_chunks(ref, rows, n):
    step = rows // n
    return [ref.at[pl.ds(i * step, step), :] for i in range(n)]


def _half_pieces(ref, k, half):
    sp = BIG_SPEC[k]
    hr = sp["rows"] // 2
    if sp["axis"] == 1:
        return [ref.at[pl.ds(pl.multiple_of(half * hr, 16), hr), :]]
    return [ref.at[pl.ds(pl.multiple_of(s * sp["rows"] + half * hr, 16), hr), :] for s in range(N_CHIPS)]


def _half_shape(k):
    rows, cols = _full_shape(k)
    return rows // 2, cols


def _stacked_pieces(ref, k):
    sp = BIG_SPEC[k]
    hr = sp["rows"] // 2
    return [ref] if sp["axis"] == 1 else [ref.at[pl.ds(s * hr, hr), :] for s in range(N_CHIPS)]


def _chip_part(ref, k, s):
    sp = BIG_SPEC[k]
    hr = sp["rows"] // 2
    if sp["axis"] == 1:
        return ref.at[:, pl.ds(pl.multiple_of(s * sp["cols"], 128), sp["cols"])]
    return ref.at[pl.ds(pl.multiple_of(s * hr, 16), hr), :]


def _halves_to_sibling(grads, name):
    BIG = _present(grads)
    n = len(BIG)
    chunks = {k: max(BIG_SPEC[k]["down"] // 2, 1) if BIG_SPEC[k]["axis"] == 1 else 1 for k in BIG}
    n_sem = sum(chunks[k] if BIG_SPEC[k]["axis"] == 1 else N_CHIPS for k in BIG)

    def body(*refs):
        g = dict(zip(BIG, refs[:n]))
        out = dict(zip(BIG, refs[n:2 * n]))
        send_sems, recv_sems = refs[2 * n:]
        x, y, c = _me()
        net = _Copies(send_sems, recv_sems)
        for k in BIG:
            hr = BIG_SPEC[k]["rows"] // 2
            for src, dst in zip(_half_pieces(g[k], k, 1 - c), _stacked_pieces(out[k], k)):
                for s, d in zip(_row_chunks(src, hr, chunks[k]), _row_chunks(dst, hr, chunks[k])):
                    net.send(s, d, (x, y, 1 - c), d)
        net.finish()

    outs = pl.pallas_call(
        body, name=name, out_shape=[jax.ShapeDtypeStruct(_half_shape(k), BF16) for k in BIG],
        in_specs=[HBM_SPEC] * n, out_specs=[HBM_SPEC] * n,
        scratch_shapes=[pltpu.SemaphoreType.DMA((n_sem,)), pltpu.SemaphoreType.DMA((n_sem,))],
    )(*[grads[k] for k in BIG])
    return dict(zip(BIG, outs))


def _add_half(g, other, k, core, name):
    sp = BIG_SPEC[k]
    hr, cols = sp["rows"] // 2, _full_shape(k)[1]
    tr = _pick(hr, (256, 352, 128))
    nb = hr // tr
    if sp["axis"] == 1:
        grid = (nb,)
        g_spec = pl.BlockSpec((tr, cols), lambda i, c: (c[0] * nb + i, 0))
        h_spec = pl.BlockSpec((tr, cols), lambda i, c: (i, 0))
    else:
        grid = (N_CHIPS, nb)
        g_spec = pl.BlockSpec((tr, cols), lambda s, i, c: ((2 * s + c[0]) * nb + i, 0))
        h_spec = pl.BlockSpec((tr, cols), lambda s, i, c: (s * nb + i, 0))

    def kern(c_ref, a_ref, b_ref, o_ref):
        o_ref[...] = (a_ref[...].astype(F32) + b_ref[...].astype(F32)).astype(BF16)

    return pl.pallas_call(
        kern, name=name, out_shape=jax.ShapeDtypeStruct(_half_shape(k), BF16),
        grid_spec=pltpu.PrefetchScalarGridSpec(num_scalar_prefetch=1, grid=grid, in_specs=[g_spec, h_spec],
                                               out_specs=h_spec),
        compiler_params=_cparams(("parallel",) * len(grid)),
    )(core, g, other)


def _part_shape(k):
    return N_CHIPS - 1, BIG_SPEC[k]["rows"] // 2, BIG_SPEC[k]["cols"]


def _scatter_copies(sums, parts, send_sems, recv_sems):
    x, y, c = _me()
    copies = []
    for i, k in enumerate(_present(sums)):
        for j, (px, py) in enumerate(_other_chips(x, y)):
            copies.append(pltpu.make_async_remote_copy(
                src_ref=_chip_part(sums[k], k, 2 * px + py), dst_ref=parts[k].at[j], send_sem=send_sems.at[3 * i + j],
                recv_sem=recv_sems.at[3 * i + j], device_id=(px, py, c), device_id_type=MESH))
    return copies


def _scatter_start(sums, tag):
    BIG = _present(sums)
    n = len(BIG)
    N_BEHIND = 3 * n
    lands = [pltpu.with_memory_space_constraint(lax.empty(_part_shape(k), BF16), pltpu.HBM) for k in BIG]

    def body(*refs):
        outs = refs[2 * n + 2:4 * n + 2]
        for cp in _scatter_copies(dict(zip(BIG, outs[:n])), dict(zip(BIG, outs[n:])), refs[2 * n], refs[2 * n + 1]):
            cp.start()
        refs[4 * n + 2][...] = jnp.zeros((8, 128), F32)

    outs = pl.pallas_call(
        body, name="scatter_start" + tag,
        out_shape=(pltpu.SemaphoreType.DMA((N_BEHIND,)), pltpu.SemaphoreType.DMA((N_BEHIND,)),
                   *[pltpu.HBM(_half_shape(k), BF16) for k in BIG], *[pltpu.HBM(_part_shape(k), BF16) for k in BIG],
                   jax.ShapeDtypeStruct((8, 128), F32)),
        in_specs=[HBM_SPEC] * (2 * n), out_specs=(SEM_SPEC, SEM_SPEC, *[HBM_SPEC] * (2 * n), VMEM_SPEC),
        input_output_aliases={i: i + 2 for i in range(2 * n)},
        compiler_params=pltpu.CompilerParams(has_side_effects=pltpu.SideEffectType.DATAFLOW_SIDE_EFFECTING),
    )(*[pltpu.with_memory_space_constraint(sums[k], pltpu.HBM) for k in BIG], *lands)
    return outs[0], outs[1], outs[2:2 * n + 2], outs[2 * n + 2]


def _scatter_wait(send_sems, recv_sems, inflight, keys, after, tag):
    BIG = keys
    n = len(BIG)

    def body(*refs):
        for cp in _scatter_copies(dict(zip(BIG, refs[:n])), dict(zip(BIG, refs[n:2 * n])), refs[2 * n], refs[2 * n + 1]):
            cp.wait_send()
            cp.wait_recv()

    outs = pl.pallas_call(
        body, name="scatter_wait" + tag,
        out_shape=(*[pltpu.HBM(_half_shape(k), BF16) for k in BIG], *[pltpu.HBM(_part_shape(k), BF16) for k in BIG]),
        in_specs=[HBM_SPEC] * (2 * n) + [SEM_SPEC, SEM_SPEC, pl.BlockSpec(memory_space=pl.ANY)],
        out_specs=(HBM_SPEC,) * (2 * n), input_output_aliases={i: i for i in range(2 * n)},
        compiler_params=pltpu.CompilerParams(has_side_effects=pltpu.SideEffectType.DATAFLOW_SIDE_EFFECTING),
    )(*inflight, send_sems, recv_sems, after)
    return dict(zip(BIG, outs[:n])), dict(zip(BIG, outs[n:]))


def _sum_half(parts, own, k, where, layer, into, name):
    sp = BIG_SPEC[k]
    rows, cols, keep = sp["rows"], sp["cols"], sp["keep"]
    hr = rows // 2
    tr = _pick(hr, (256, 352, 128))
    nb = hr // tr
    if sp["axis"] == 1:
        own_spec = pl.BlockSpec((tr, cols), lambda i, w: (i, w[0]))
    else:
        own_spec = pl.BlockSpec((tr, cols), lambda i, w: (w[0] * nb + i, 0))

    def kern(w_ref, p_ref, own_ref, *rest):
        tot = own_ref[...].astype(F32)
        for j in range(N_CHIPS - 1):
            tot = tot + p_ref[j].astype(F32)
        rest[-1][0] = tot[:, :keep]

    in_specs = [pl.BlockSpec((N_CHIPS - 1, tr, cols), lambda i, w: (0, i, 0)), own_spec]
    args = [where, parts, own]
    if into is not None:
        in_specs.append(pl.BlockSpec(memory_space=pl.ANY))
        args.append(into)
    return pl.pallas_call(
        kern, name=name, out_shape=jax.ShapeDtypeStruct((DEPTH, rows, keep), F32),
        grid_spec=pltpu.PrefetchScalarGridSpec(
            num_scalar_prefetch=1, grid=(nb,), in_specs=in_specs,
            out_specs=pl.BlockSpec((1, tr, keep), lambda i, w: (layer, w[1] * nb + i, 0))),
        input_output_aliases={} if into is None else {3: 0},
        compiler_params=_cparams(("parallel",)),
    )(*args)


def _exchange_halves(red):
    n = len(BIG)

    def body(*refs):
        out = dict(zip(BIG, refs[n:2 * n]))
        send_sems, recv_sems = refs[2 * n:]
        x, y, c = _me()
        net = _Copies(send_sems, recv_sems)
        for k in BIG:
            hr = BIG_SPEC[k]["rows"] // 2
            for l in range(DEPTH):
                mine = out[k].at[l, pl.ds(pl.multiple_of(c * hr, 8), hr), :]
                theirs = out[k].at[l, pl.ds(pl.multiple_of((1 - c) * hr, 8), hr), :]
                net.send(mine, mine, (x, y, 1 - c), theirs)
        net.finish()

    outs = pl.pallas_call(
        body, name="exchange_halves",
        out_shape=[jax.ShapeDtypeStruct((DEPTH, BIG_SPEC[k]["rows"], BIG_SPEC[k]["keep"]), F32) for k in BIG],
        in_specs=[HBM_SPEC] * n, out_specs=[HBM_SPEC] * n, input_output_aliases={i: i for i in range(n)},
        scratch_shapes=[pltpu.SemaphoreType.DMA((DEPTH * n,)), pltpu.SemaphoreType.DMA((DEPTH * n,))],
    )(*[red[k] for k in BIG])
    return dict(zip(BIG, outs))


def _adam_vals(g, w, m, v):
    m2 = ADAM_B1 * m + (1.0 - ADAM_B1) * g
    v2 = ADAM_B2 * v + (1.0 - ADAM_B2) * (g * g)
    m_hat = m2 / (1.0 - ADAM_B1 ** ADAM_STEP)
    v_hat = v2 / (1.0 - ADAM_B2 ** ADAM_STEP)
    return -ADAM_LR * (m_hat / (jnp.sqrt(v_hat) + ADAM_EPS) + ADAM_WD * w), m2, v2


def _allreduce_small_adam(groups):
    ng = len(groups)

    def body(*refs):
        ins = [refs[4 * i:4 * i + 4] for i in range(ng)]
        outs = [refs[4 * ng + 4 * i:4 * ng + 4 * i + 4] for i in range(ng)]
        bufs = refs[8 * ng:9 * ng]
        send_sems, recv_sems = refs[9 * ng:]
        x, y, c = _me()
        me = 4 * x + 2 * y + c
        net = _Copies(send_sems, recv_sems)
        for (g_ref, _, _, _), buf in zip(ins, bufs):
            buf[me] = g_ref[...]
            for f in FLIPS:
                px, py, pc = _peer(x, y, c, f)
                net.send(g_ref, buf.at[me], (px, py, pc), buf.at[4 * px + 2 * py + pc])
        net.finish()
        for (_, w_ref, m_ref, v_ref), (gs_ref, d_ref, nm_ref, nv_ref), buf in zip(ins, outs, bufs):
            tot = buf[0]
            for d in range(1, 8):
                tot = tot + buf[d]
            gs_ref[...] = tot
            d_ref[...], nm_ref[...], nv_ref[...] = _adam_vals(tot, w_ref[...], m_ref[...], v_ref[...])

    shapes = [jax.ShapeDtypeStruct(g[0].shape, F32) for g in groups for _ in range(4)]
    outs = pl.pallas_call(
        body, name="allreduce_small", out_shape=shapes, in_specs=[VMEM_SPEC] * (4 * ng), out_specs=[VMEM_SPEC] * (4 * ng),
        scratch_shapes=[pltpu.VMEM((8,) + g[0].shape, F32) for g in groups]
        + [pltpu.SemaphoreType.DMA((7 * ng,)), pltpu.SemaphoreType.DMA((7 * ng,))],
        compiler_params=pltpu.CompilerParams(vmem_limit_bytes=VMEM_LIMIT),
    )(*[t for g in groups for t in g])
    return [outs[4 * i:4 * i + 4] for i in range(ng)]


def _adam(g, w, m, v, name, lead_block=1):
    shape = w.shape
    lead, rows, cols = math.prod(shape[:-2]), shape[-2], shape[-1]
    tr = _pick(rows, (256, 352, 64, 8, rows))
    spec = pl.BlockSpec((lead_block, tr, cols), lambda l, i: (l, i, 0))

    def kern(g_ref, w_ref, m_ref, v_ref, d_ref, nm_ref, nv_ref):
        d_ref[...], nm_ref[...], nv_ref[...] = _adam_vals(g_ref[...], w_ref[...], m_ref[...], v_ref[...])

    outs = pl.pallas_call(
        kern, name=name, grid=(lead // lead_block, rows // tr), in_specs=[spec] * 4, out_specs=[spec] * 3,
        out_shape=[jax.ShapeDtypeStruct((lead, rows, cols), F32)] * 3, compiler_params=_cparams(("parallel", "parallel")),
    )(*[t.reshape(lead, rows, cols) for t in (g, w, m, v)])
    return [o.reshape(shape) for o in outs]


SMALL = ("attn_norm", "sgu_ln_g", "sgu_ln_b", "sgu_w", "sgu_b", "attn_sinks", "dn_a_log", "dn_dt_bias", "dn_norm",
         "ffn_norm", "final_norm")
SMALL_2D = {"attn_norm": (DEPTH, D_MODEL), "ffn_norm": (DEPTH, D_MODEL), "final_norm": (1, D_MODEL),
            "sgu_ln_g": (DEPTH, MIX), "sgu_ln_b": (DEPTH, MIX), "sgu_w": (DEPTH * SGU_G * SGU_T, SGU_T),
            "sgu_b": (DEPTH * SGU_G, SGU_T), "dn_norm": (DEPTH, DN_HD)}
TINY = ("attn_sinks", "dn_a_log", "dn_dt_bias")


def _pack_tiny(vals, extra=None):
    flat = [vals[k].astype(F32).reshape(-1) for k in TINY] + ([] if extra is None else [extra.astype(F32).reshape(-1)])
    n = sum(f.shape[0] for f in flat)
    return jnp.concatenate(flat + [jnp.zeros((8 * 128 - n,), F32)]).reshape(8, 128)


def _unpack_tiny(tile, shapes):
    flat, out, o = tile.reshape(-1), {}, 0
    for k in TINY:
        n = math.prod(shapes[k])
        out[k] = flat[o:o + n].reshape(shapes[k])
        o += n
    return out, flat[o]


def _in_col_segments():
    shard, padded = IN_COLS // N_CHIPS, BIG_SPEC["w_in"]["cols"]
    segs, mine = [], 0
    for a, n in IN_PIECES:
        o = a
        while o < a + n:
            end = min(a + n, (o // shard + 1) * shard)
            segs.append(((o // shard) * padded + o % shard, mine + o - a, end - o))
            o = end
        mine += n
    return segs


def _move_cols(x, segs, out_cols, name):
    layers, rows, cols = x.shape
    tr = _pick(rows, (256, rows))
    gaps, at = [], 0
    for d, w in sorted((d, w) for _, d, w in segs):
        if d > at:
            gaps.append((at, d - at))
        at = d + w
    if at < out_cols:
        gaps.append((at, out_cols - at))

    def kern(x_ref, o_ref):
        for s, d, w in segs:
            o_ref[0, :, d:d + w] = x_ref[0, :, s:s + w]
        for d, w in gaps:
            o_ref[0, :, d:d + w] = jnp.zeros((tr, w), x.dtype)

    return pl.pallas_call(
        kern, name=name, grid=(layers, rows // tr), in_specs=[pl.BlockSpec((1, tr, cols), lambda l, i: (l, i, 0))],
        out_specs=pl.BlockSpec((1, tr, out_cols), lambda l, i: (l, i, 0)),
        out_shape=jax.ShapeDtypeStruct((layers, rows, out_cols), x.dtype), compiler_params=_cparams(("parallel", "parallel")),
    )(x)


WEIGHTS = ("attn_norm", "w_in", "sgu_ln_g", "sgu_ln_b", "sgu_w", "sgu_b", "attn_sinks", "dn_conv_w", "dn_a_log",
           "dn_dt_bias", "dn_norm", "w_branch", "w_out", "ffn_norm", "w_gate_up", "w_down", "final_norm")


def kernel(x, positions, attn_norm, w_in, sgu_ln_g, sgu_ln_b, sgu_w, sgu_b, attn_sinks, dn_conv_w, dn_a_log, dn_dt_bias, dn_norm, w_branch, w_out, ffn_norm, w_gate_up, w_down, final_norm, loss_target, m_attn_norm, m_w_in, m_sgu_ln_g, m_sgu_ln_b, m_sgu_w, m_sgu_b, m_attn_sinks, m_dn_conv_w, m_dn_a_log, m_dn_dt_bias, m_dn_norm, m_w_branch, m_w_out, m_ffn_norm, m_w_gate_up, m_w_down, m_final_norm, v_attn_norm, v_w_in, v_sgu_ln_g, v_sgu_ln_b, v_sgu_w, v_sgu_b, v_attn_sinks, v_dn_conv_w, v_dn_a_log, v_dn_dt_bias, v_dn_norm, v_w_branch, v_w_out, v_ffn_norm, v_w_gate_up, v_w_down, v_final_norm):
    given = dict(locals())
    W = {k: given[k] for k in WEIGHTS}
    M = {k: given["m_" + k] for k in WEIGHTS}
    V = {k: given["v_" + k] for k in WEIGHTS}
    chip = 2 * lax.axis_index("x") + lax.axis_index("y")
    core = lax.axis_index("c")
    chip1 = chip.astype(jnp.int32).reshape(1)
    where = jnp.stack([chip, core]).astype(jnp.int32)

    placed = [{k: _place_shard(W[k].reshape(DEPTH, BIG_SPEC[k]["rows"], BIG_SPEC[k]["keep"]), k, chip1, l,
                               f"place{l}_{k}") for k in BIG} for l in range(DEPTH)]
    _, conv_full = _gather_layer({}, dn_conv_w.reshape(CONV_ROWS, CONV_COLS))
    behind = {"in": _gather_start({"w_in": placed[0]["w_in"]}, conv_full, "in")}
    behind["0"] = _gather_start({k: placed[0][k] for k in BIG if k != "w_in"}, behind["in"][3], "0")
    behind["1"] = _gather_start(placed[1], behind["0"][3], "1")
    segs = _in_col_segments()

    def arrived(tag, after):
        send_sems, recv_sems, inflight, _ = behind[tag]
        return _gather_finish(_gather_wait(send_sems, recv_sems, inflight, after, tag), tag)

    def big_of_layer(l, x_l):
        got = {} if l == 0 else arrived("1", x_l)
        cols = lambda t: _move_cols(t[None], segs, IN_R, f"w_in_cols{l}")[0]

        def rest(after):
            full = got or arrived("0", after)
            return dict(full, w_branch=full["w_branch"].reshape(3, MIX, D_MODEL))

        return dict(w_in=cols(got["w_in"]) if got else (lambda after: cols(arrived("in", after)["w_in"])), rest=rest)

    w = {k: W[k] for k in SMALL}
    w["attn_norm"] = attn_norm + behind["1"][3][0, 0]
    w["dn_conv_w"] = conv_full.reshape(DEPTH, DN_CONV, 3 * MIX)

    core1 = core.astype(jnp.int32).reshape(1)
    back_segs = [(d, s, n) for s, d, n in segs]
    travelling, started, sums, parts = [], [], [{}, {}], [{}, {}]

    def on_grads(l, gl):
        gl, tag = dict(gl), f"{l}_{len(gl)}"
        if "w_in" in gl:
            gl["w_in"] = _move_cols(gl["w_in"][None], back_segs, _full_shape("w_in")[1], f"g_in_cols{l}")[0]
        if "w_branch" in gl:
            gl["w_branch"] = gl["w_branch"].reshape(3 * MIX, D_MODEL)
        sibling = _halves_to_sibling(gl, "halves_to_sibling" + tag)
        chip_sums = {k: _add_half(gl[k], sibling[k], k, core1, f"chip_sum{l}_{k}") for k in gl}
        send_sems, recv_sems, inflight, token = _scatter_start(chip_sums, tag)
        travelling.append((l, send_sems, recv_sems, inflight, _present(gl), tag))
        started.append(token)
        return token

    loss, dx, g = _local_step(x[0], positions[0], loss_target[0], w, big_of_layer, on_grads)

    grads = {}
    conv_2d = (CONV_ROWS, N_CHIPS * CONV_COLS)
    conv_zero = jnp.zeros(conv_2d, F32)
    groups = [tuple(d[k].reshape(SMALL_2D[k]) for d in (g, W, M, V)) for k in SMALL_2D]
    groups.append((g["dn_conv_w"].reshape(conv_2d), conv_zero, conv_zero, conv_zero))
    loss = loss + started[-1][0, 0]
    groups.append((_pack_tiny(g, loss), _pack_tiny(W), _pack_tiny(M), _pack_tiny(V)))
    summed = _allreduce_small_adam(groups)
    delta, new_m, new_v = {}, {}, {}
    for k, outs in zip(SMALL_2D, summed):
        for d, t in zip((grads, delta, new_m, new_v), outs):
            d[k] = t.reshape(W[k].shape)
    conv_sum = summed[len(SMALL_2D)][0].reshape(g["dn_conv_w"].shape)
    grads["dn_conv_w"] = lax.dynamic_slice_in_dim(conv_sum, chip * dn_conv_w.shape[2], dn_conv_w.shape[2], axis=2)
    tiny_shapes = {k: W[k].shape for k in TINY}
    tiny, loss_total = _unpack_tiny(summed[-1][0], tiny_shapes)
    grads.update(tiny)
    for d, t in zip((delta, new_m, new_v), summed[-1][1:]):
        d.update(_unpack_tiny(t, tiny_shapes)[0])

    for l, send_sems, recv_sems, inflight, keys, tag in travelling:
        landed = _scatter_wait(send_sems, recv_sems, inflight, keys, summed[0][0], tag)
        sums[l].update(landed[0])
        parts[l].update(landed[1])
    red = {k: _sum_half(parts[1][k], sums[1][k], k, where, 1, None, f"sum1_{k}") for k in BIG}
    red = {k: _sum_half(parts[0][k], sums[0][k], k, where, 0, red[k], f"sum0_{k}") for k in BIG}
    reduced = _exchange_halves(red)
    grads.update({k: reduced[k].reshape(W[k].shape) for k in BIG})
    for k in ("w_branch", "w_out", "w_gate_up", "w_down", "dn_conv_w"):
        delta[k], new_m[k], new_v[k] = _adam(grads[k], W[k], M[k], V[k], "adam_" + k)
    lead_first = lambda t: jnp.transpose(t, (2, 0, 1))
    outs = _adam(*[lead_first(d["w_in"]) for d in (grads, W, M, V)], "adam_w_in", lead_block=IN_COLS // N_CHIPS // 10)
    delta["w_in"], new_m["w_in"], new_v["w_in"] = (jnp.transpose(o, (1, 2, 0)) for o in outs)

    return (loss_total, dx[None], *[grads[k] for k in WEIGHTS], *[delta[k] for k in WEIGHTS],
            *[new_m[k] for k in WEIGHTS], *[new_v[k] for k in WEIGHTS])
```

```python
import functools
import math

import jax
import jax.numpy as jnp
from jax import lax
from jax.experimental import pallas as pl
from jax.experimental.pallas import tpu as pltpu

F32 = jnp.float32
BF16 = jnp.bfloat16
HI = lax.Precision.HIGHEST

D_MODEL = 1024
DEPTH = 2
MIX = 512
EPS = 1e-6
SGU_G, SGU_T = 4, 128
SWA_H, SWA_KV, SWA_HD, WINDOW = 8, 2, 64, 128
ROPE_THETA, ROPE_DIM = 500000.0, 16
DN_H, DN_HD, DN_CONV, DN_C = 4, 128, 4, 64
D_FF = 2816
IN_COLS = 6920
IN_PIECES = ((3848, 3072), (1792, 1536), (3328, 512), (0, 512), (512, 512), (1024, 512), (1536, 128), (1664, 128),
             (3840, 8))
IN_PAD = 120
IN_R = 7040
C_GATE, C_QKV, C_ZC, C_UA, C_VA, C_QB, C_KB, C_VB, C_SM = 0, 3072, 4608, 5120, 5632, 6144, 6656, 6784, 6912

ADAM_LR, ADAM_B1, ADAM_B2, ADAM_EPS, ADAM_WD, ADAM_STEP = 0.001, 0.9, 0.999, 1e-08, 0.01, 10
VMEM_LIMIT = 56 * 1024 * 1024


def _cparams(sem):
    return pltpu.CompilerParams(dimension_semantics=sem, vmem_limit_bytes=VMEM_LIMIT)


def _dg(a, b, ca, cb, prec=None):
    return lax.dot_general(a, b, (((ca,), (cb,)), ((), ())), precision=prec, preferred_element_type=F32)


def _split(x):
    hi = x.astype(BF16)
    return hi, (x - hi.astype(F32)).astype(BF16)


def _dg3_many(as_, bs, ca, cb):
    sa = [_split(a) for a in as_]
    sb = [_split(b) for b in bs]
    hh = [_dg(a[0], b[0], ca, cb) for a, b in zip(sa, sb)]
    hl = [_dg(a[0], b[1], ca, cb) for a, b in zip(sa, sb)]
    lh = [_dg(a[1], b[0], ca, cb) for a, b in zip(sa, sb)]
    return [x + (y + z) for x, y, z in zip(hh, hl, lh)]


def _dg_exact_lhs_many(a01, bs, ca, cb):
    a = a01.astype(BF16)
    b1 = [b.astype(BF16) for b in bs]
    r1 = [b - t.astype(F32) for b, t in zip(bs, b1)]
    b2 = [r.astype(BF16) for r in r1]
    b3 = [(r - t.astype(F32)).astype(BF16) for r, t in zip(r1, b2)]
    d1 = [_dg(a, t, ca, cb) for t in b1]
    d2 = [_dg(a, t, ca, cb) for t in b2]
    d3 = [_dg(a, t, ca, cb) for t in b3]
    return [x + (y + z) for x, y, z in zip(d1, d2, d3)]


def _mm(a, b):
    return _dg(a.astype(BF16), b.astype(BF16), 1, 0)


def _mm_nt(a, b):
    return _dg(a.astype(BF16), b.astype(BF16), 1, 1)


def _mm_tn(a, b):
    return _dg(a.astype(BF16), b.astype(BF16), 0, 0)


def _sigmoid(x):
    return 0.5 * jnp.tanh(0.5 * x) + 0.5


def _silu(x):
    return x * _sigmoid(x)


def _dsilu(x):
    s = _sigmoid(x)
    return s * (1.0 + x * (1.0 - s))


_GC = math.sqrt(2.0 / math.pi)


def _gelu(x):
    return 0.5 * x * (1.0 + jnp.tanh(_GC * (x + 0.044715 * x * x * x)))


def _dgelu(x):
    t = jnp.tanh(_GC * (x + 0.044715 * x * x * x))
    return 0.5 * (1.0 + t) + 0.5 * x * (1.0 - t * t) * _GC * (1.0 + 3.0 * 0.044715 * x * x)


def _softplus(x):
    return jnp.maximum(x, 0.0) + jnp.log(1.0 + jnp.exp(-jnp.abs(x)))


def _acc(ref, val, i):
    @pl.when(i == 0)
    def _():
        ref[...] = val

    @pl.when(i > 0)
    def _():
        ref[...] += val


def _halo_rows(dtype):
    return 8 * 4 // jnp.dtype(dtype).itemsize


def _tok_call(body, name, S, TB, tok_in, const_in=(), tok_out=(), acc_out=(), prev_in=(), next_in=(), smem_in=()):
    nb = S // TB
    in_specs, args = [], []
    for a, w, cb in tok_in:
        in_specs.append(pl.BlockSpec((TB, w), functools.partial(lambda i, cb: (i, cb), cb=cb)))
        args.append(a)
    for a, w, cb in prev_in:
        hr = _halo_rows(a.dtype)
        in_specs.append(pl.BlockSpec((hr, w), functools.partial(
            lambda i, cb, r: (jnp.maximum(i * r - 1, 0), cb), cb=cb, r=TB // hr)))
        args.append(a)
    for a, w, cb in next_in:
        hr = _halo_rows(a.dtype)
        in_specs.append(pl.BlockSpec((hr, w), functools.partial(
            lambda i, cb, r, last: (jnp.minimum((i + 1) * r, last), cb), cb=cb, r=TB // hr, last=S // hr - 1)))
        args.append(a)
    for a in const_in:
        in_specs.append(pl.BlockSpec(a.shape, lambda i: (0, 0)))
        args.append(a)
    for a in smem_in:
        in_specs.append(pl.BlockSpec(memory_space=pltpu.SMEM))
        args.append(a)
    out_specs, out_shape, aliases, shared = [], [], {}, {}
    for o, (w, dt, *dest) in enumerate(tok_out):
        if not dest:
            out_specs.append(pl.BlockSpec((TB, w), lambda i: (i, 0)))
            out_shape.append(jax.ShapeDtypeStruct((S, w), dt))
            continue
        cb, wide = dest
        out_specs.append(pl.BlockSpec((TB, w), functools.partial(lambda i, cb: (i, cb), cb=cb)))
        out_shape.append(jax.ShapeDtypeStruct((S, wide if isinstance(wide, int) else wide.shape[1]), dt))
        if not isinstance(wide, int):
            if id(wide) not in shared:
                shared[id(wide)] = len(args)
                in_specs.append(pl.BlockSpec(memory_space=pl.ANY))
                args.append(wide)
            aliases[shared[id(wide)]] = o
    for shp, dt in acc_out:
        out_specs.append(pl.BlockSpec(shp, lambda i: (0, 0)))
        out_shape.append(jax.ShapeDtypeStruct(shp, dt))
    n_extra = len(shared)

    def kern(*refs):
        n_in = len(in_specs) - n_extra
        body(pl.program_id(0), *refs[:n_in], *refs[n_in + n_extra:])

    return pl.pallas_call(
        kern, name=name, grid=(nb,), in_specs=in_specs, out_specs=out_specs, out_shape=out_shape,
        input_output_aliases=aliases, compiler_params=_cparams(("arbitrary",)),
    )(*args)


MM_BLOCKS = (1024, 1408, 640, 512, 256, 128)


def _pick(n, cands):
    for c in cands:
        if n % c == 0:
            return c
    return n


MM_VMEM_BUDGET = 44 * 1024 * 1024


def _mm_blocks(M, N, K, a_bytes, b_bytes, o_bytes, add_bytes):
    bn = _pick(N, MM_BLOCKS)
    fits = None
    for bk in [K] + [c for c in (2816, 2048) + MM_BLOCKS if c < K and K % c == 0]:
        for bm in [c for c in (2048,) + MM_BLOCKS if M % c == 0 and c >= min(M, 512)]:
            b_bufs = 1 if (bk == K and bn == N) else 2
            need = 2 * bm * bk * a_bytes + b_bufs * bk * bn * b_bytes + 2 * bm * bn * (o_bytes + add_bytes)
            need += bm * bn * 4 if bk < K else 0
            if need <= MM_VMEM_BUDGET:
                fits = fits or (bm, bn, bk)
                if (M // bm) * (N // bn) * (K // bk) >= 4:
                    return bm, bn, bk
    if fits is None:
        raise ValueError(f"no matmul blocks for {(M, N, K)}")
    return fits


def _matmul(a, b, *, ta=False, tb=False, add=None, out_dtype=F32, name):
    M, K = (a.shape[1], a.shape[0]) if ta else a.shape
    N = b.shape[0] if tb else b.shape[1]
    bm, bn, bk = _mm_blocks(M, N, K, a.dtype.itemsize, b.dtype.itemsize, jnp.dtype(out_dtype).itemsize,
                            0 if add is None else add.dtype.itemsize)
    nk = K // bk
    b_mode = dict(pipeline_mode=pl.Buffered(1)) if (bk == K and bn == N) else {}
    a_spec = pl.BlockSpec((bk, bm), lambda i, j, k: (k, i)) if ta else pl.BlockSpec((bm, bk), lambda i, j, k: (i, k))
    b_spec = (pl.BlockSpec((bn, bk), lambda i, j, k: (j, k), **b_mode) if tb
              else pl.BlockSpec((bk, bn), lambda i, j, k: (k, j), **b_mode))
    o_spec = pl.BlockSpec((bm, bn), lambda i, j, k: (i, j))
    ca, cb = (0 if ta else 1), (1 if tb else 0)

    def kern(*refs):
        a_ref, b_ref = refs[:2]
        add_ref = refs[2] if add is not None else None
        o_ref = refs[3] if add is not None else refs[2]
        p = _dg(a_ref[...].astype(BF16), b_ref[...].astype(BF16), ca, cb)

        def finish(r):
            if add is not None:
                r = r + add_ref[...].astype(F32)
            o_ref[...] = r.astype(out_dtype)

        if nk == 1:
            finish(p)
            return
        acc_ref = refs[-1]
        k = pl.program_id(2)

        @pl.when(k == 0)
        def _():
            acc_ref[...] = p

        @pl.when((k > 0) & (k < nk - 1))
        def _():
            acc_ref[...] += p

        @pl.when(k == nk - 1)
        def _():
            finish(acc_ref[...] + p)

    in_specs = [a_spec, b_spec] + ([o_spec] if add is not None else [])
    args = (a, b) + ((add,) if add is not None else ())
    return pl.pallas_call(
        kern, name=name, grid=(M // bm, N // bn, nk), in_specs=in_specs, out_specs=o_spec,
        out_shape=jax.ShapeDtypeStruct((M, N), out_dtype),
        scratch_shapes=[pltpu.VMEM((bm, bn), F32)] if nk > 1 else [],
        compiler_params=_cparams(("parallel", "parallel", "arbitrary")),
    )(*args)


def _rms_fwd(x, g, name):
    S = x.shape[0]

    def body(i, x_ref, g_ref, h_ref):
        xv = x_ref[...]
        r = lax.rsqrt(jnp.mean(xv * xv, axis=-1, keepdims=True) + EPS)
        h_ref[...] = (xv * r * g_ref[...]).astype(BF16)

    return _tok_call(body, name, S, min(S, 512), [(x, D_MODEL, 0)], [g], [(D_MODEL, BF16)])[0]


def _rms_bwd_vals(xv, g, dh):
    r = lax.rsqrt(jnp.mean(xv * xv, axis=-1, keepdims=True) + EPS)
    u = dh * g
    dx = r * u - xv * (r * r * r) * jnp.mean(u * xv, axis=-1, keepdims=True)
    dg = jnp.sum(dh * xv * r, axis=0, keepdims=True)
    return dx, dg


def _rms_bwd_add(x, g, dh, dres, name):
    S = x.shape[0]

    def body(i, x_ref, dh_ref, dr_ref, g_ref, dx_ref, dxb_ref, dg_ref):
        dx, dg = _rms_bwd_vals(x_ref[...], g_ref[...], dh_ref[...].astype(F32))
        tot = dr_ref[...] + dx
        dx_ref[...] = tot
        dxb_ref[...] = tot.astype(BF16)
        _acc(dg_ref, dg, i)

    return _tok_call(body, name, S, min(S, 512), [(x, D_MODEL, 0), (dh, D_MODEL, 0), (dres, D_MODEL, 0)], [g],
                     [(D_MODEL, F32), (D_MODEL, BF16)], [((1, D_MODEL), F32)])


def _final_loss(x, g, target):
    S = x.shape[0]

    def body(i, x_ref, t_ref, g_ref, dx_ref, dxb_ref, loss_ref, dg_ref):
        xv, gv = x_ref[...], g_ref[...]
        r = lax.rsqrt(jnp.mean(xv * xv, axis=-1, keepdims=True) + EPS)
        e = xv * r * gv - t_ref[...]
        part = 0.5 * jnp.sum(jnp.mean(e * e, axis=-1, keepdims=True), axis=0, keepdims=True)
        dx, dg = _rms_bwd_vals(xv, gv, e * (1.0 / D_MODEL))
        dx_ref[...] = dx
        dxb_ref[...] = dx.astype(BF16)
        _acc(loss_ref, jnp.broadcast_to(part, (1, 128)), i)
        _acc(dg_ref, dg, i)

    return _tok_call(body, "final_loss", S, min(S, 512), [(x, D_MODEL, 0), (target, D_MODEL, 0)], [g],
                     [(D_MODEL, F32), (D_MODEL, BF16)], [((1, 128), F32), ((1, D_MODEL), F32)])


def _swiglu_fwd(gu, name):
    S = gu.shape[0]

    def body(i, gu_ref, a_ref):
        a_ref[...] = (_silu(gu_ref[:, :D_FF].astype(F32)) * gu_ref[:, D_FF:].astype(F32)).astype(BF16)

    return _tok_call(body, name, S, min(S, 256), [(gu, 2 * D_FF, 0)], [], [(D_FF, BF16)])[0]


def _swiglu_bwd(gu, dact, name):
    S = gu.shape[0]

    def body(i, gu_ref, da_ref, dgu_ref):
        gg, uu, da = gu_ref[:, :D_FF].astype(F32), gu_ref[:, D_FF:].astype(F32), da_ref[...].astype(F32)
        dgu_ref[:, :D_FF] = (da * uu * _dsilu(gg)).astype(BF16)
        dgu_ref[:, D_FF:] = (da * _silu(gg)).astype(BF16)

    return _tok_call(body, name, S, min(S, 256), [(gu, 2 * D_FF, 0), (dact, D_FF, 0)], [], [(2 * D_FF, BF16)])[0]


def _merge_fwd(proj, bds, name):
    S = proj.shape[0]

    def body(i, g0, g1, g2, b0, b1, b2, m_ref):
        m = jnp.zeros(m_ref.shape, F32)
        for gr, br in ((g0, b0), (g1, b1), (g2, b2)):
            m = m + _sigmoid(gr[...].astype(F32)) * br[...].astype(F32)
        m_ref[...] = m.astype(BF16)

    tok = [(proj, D_MODEL, n) for n in range(3)] + [(b, D_MODEL, 0) for b in bds]
    return _tok_call(body, name, S, min(S, 512), tok, [], [(D_MODEL, BF16)])[0]


def _merge_bwd(proj, bds, dm, name):
    S = proj.shape[0]

    def body(i, g0, g1, g2, b0, b1, b2, dm_ref, d0, d1, d2, dgp_ref):
        dmv = dm_ref[...]
        for n, (gr, br, dr) in enumerate(((g0, b0, d0), (g1, b1, d1), (g2, b2, d2))):
            s = _sigmoid(gr[...].astype(F32))
            dr[...] = (dmv * s).astype(BF16)
            dgp_ref[:, n * D_MODEL:(n + 1) * D_MODEL] = (dmv * br[...].astype(F32) * s * (1.0 - s)).astype(BF16)

    tok = [(proj, D_MODEL, n) for n in range(3)] + [(b, D_MODEL, 0) for b in bds] + [(dm, D_MODEL, 0)]
    return _tok_call(body, name, S, min(S, 512), tok, [],
                     [(D_MODEL, BF16)] * 3 + [(3 * D_MODEL, BF16, C_GATE // (3 * D_MODEL), IN_R)])


def _sgu_ln(v, lg, lb):
    mu = jnp.mean(v, axis=-1, keepdims=True)
    vc = v - mu
    rstd = lax.rsqrt(jnp.mean(vc * vc, axis=-1, keepdims=True) + EPS)
    vhat = vc * rstd
    return vhat, rstd, vhat * lg + lb


def _sgu_fwd(proj, lg, lb, wc, bst, name):
    S = proj.shape[0]

    def body(i, ua_ref, va_ref, lg_ref, lb_ref, wc_ref, bs_ref, o_ref):
        u = _gelu(ua_ref[...].astype(F32))
        _, _, vn = _sgu_ln(_gelu(va_ref[...].astype(F32)), lg_ref[...], lb_ref[...])
        for g in range(SGU_G):
            sl = slice(g * 128, (g + 1) * 128)
            mixed = _mm(wc_ref[sl, :], vn[:, sl]) + bs_ref[:, g:g + 1]
            o_ref[:, sl] = (u[:, sl] * mixed).astype(BF16)

    return _tok_call(body, name, S, SGU_T, [(proj, MIX, C_UA // MIX), (proj, MIX, C_VA // MIX)], [lg, lb, wc, bst],
                     [(MIX, BF16)])[0]


def _sgu_bwd(proj, lg, lb, wc, bst, dout, dproj, name):
    S = proj.shape[0]

    def body(i, ua_ref, va_ref, do_ref, lg_ref, lb_ref, wc_ref, bs_ref, duv_ref, dlg_ref, dlb_ref, dwc_ref,
             dbs_ref):
        ua, va, do = ua_ref[...].astype(F32), va_ref[...].astype(F32), do_ref[...].astype(F32)
        u = _gelu(ua)
        lgv = lg_ref[...]
        vhat, rstd, vn = _sgu_ln(_gelu(va), lgv, lb_ref[...])
        tril = lax.broadcasted_iota(jnp.int32, (128, 128), 0) >= lax.broadcasted_iota(jnp.int32, (128, 128), 1)
        lane4 = lax.broadcasted_iota(jnp.int32, (128, 4), 1)
        gs = range(SGU_G)
        sls = [slice(g * 128, (g + 1) * 128) for g in gs]
        wgs = [wc_ref[sl, :] for sl in sls]
        mixed = [_mm(wgs[g], vn[:, sls[g]]) for g in gs]
        dmix = [do[:, sl] * u[:, sl] for sl in sls]
        dwg = [_mm_nt(dmix[g], vn[:, sls[g]]) for g in gs]
        dvn = jnp.concatenate([_mm_tn(wgs[g], dmix[g]) for g in gs], axis=1)
        dbs = jnp.zeros((128, 4), F32)
        for g in gs:
            duv_ref[:, sls[g]] = (do[:, sls[g]] * (mixed[g] + bs_ref[:, g:g + 1]) * _dgelu(ua[:, sls[g]])).astype(BF16)
            dbs = dbs + jnp.where(lane4 == g, jnp.sum(dmix[g], axis=-1, keepdims=True), 0.0)
            _acc(dwc_ref.at[sls[g], :], jnp.where(tril, dwg[g], 0.0), i)
        _acc(dbs_ref, dbs, i)
        _acc(dlg_ref, jnp.sum(dvn * vhat, axis=0, keepdims=True), i)
        _acc(dlb_ref, jnp.sum(dvn, axis=0, keepdims=True), i)
        dvh = dvn * lgv
        dv = rstd * (dvh - jnp.mean(dvh, axis=-1, keepdims=True) - vhat * jnp.mean(dvh * vhat, axis=-1, keepdims=True))
        duv_ref[:, MIX:] = (dv * _dgelu(va)).astype(BF16)

    return _tok_call(body, name, S, SGU_T, [(proj, MIX, C_UA // MIX), (proj, MIX, C_VA // MIX), (dout, MIX, 0)],
                     [lg, lb, wc, bst], [(2 * MIX, BF16, C_UA // (2 * MIX), dproj)],
                     [((1, MIX), F32), ((1, MIX), F32), ((SGU_G * 128, 128), F32), ((128, 4), F32)])


def _rope_tables(positions):
    S = positions.shape[0]
    inv_freq = ROPE_THETA ** (-jnp.arange(0, ROPE_DIM, 2, dtype=F32) / ROPE_DIM)
    ang = positions.astype(F32)[:, None] * inv_freq
    c, s = jnp.cos(ang), jnp.sin(ang)
    c64 = jnp.concatenate([c, c, jnp.ones((S, SWA_HD - ROPE_DIM), F32)], axis=1)
    s64 = jnp.concatenate([-s, s, jnp.zeros((S, SWA_HD - ROPE_DIM), F32)], axis=1)
    return jnp.tile(c64, (1, 2)), jnp.tile(s64, (1, 2))


def _rope128(x, c, s):
    lane = lax.broadcasted_iota(jnp.int32, x.shape, 1) % SWA_HD
    swapped = jnp.where(lane < ROPE_DIM // 2, pltpu.roll(x, 128 - ROPE_DIM // 2, 1), pltpu.roll(x, ROPE_DIM // 2, 1))
    return x * c + swapped * s


def _rope_t128(y, c, s):
    ys = y * s
    lane = lax.broadcasted_iota(jnp.int32, y.shape, 1) % SWA_HD
    swapped = jnp.where(lane < ROPE_DIM // 2, pltpu.roll(ys, 128 - ROPE_DIM // 2, 1), pltpu.roll(ys, ROPE_DIM // 2, 1))
    return y * c + jnp.where(lane < ROPE_DIM, swapped, 0.0)


def _rope_fwd(proj, cos, sin, name):
    S = proj.shape[0]
    scale = SWA_HD ** -0.5

    def body(i, q_ref, k_ref, v_ref, c_ref, s_ref, qo_ref, ko_ref, vo_ref):
        c, s = c_ref[...], s_ref[...]
        for j in range(4):
            sl = slice(j * 128, (j + 1) * 128)
            qo_ref[:, sl] = (_rope128(q_ref[:, sl].astype(F32), c, s) * scale).astype(BF16)
        ko_ref[...] = _rope128(k_ref[...].astype(F32), c, s).astype(BF16)
        vo_ref[...] = v_ref[...].astype(BF16)

    return _tok_call(body, name, S, min(S, 512),
                     [(proj, MIX, C_QB // MIX), (proj, 128, C_KB // 128), (proj, 128, C_VB // 128), (cos, 128, 0),
                      (sin, 128, 0)], [], [(MIX, BF16), (128, BF16), (128, BF16)])


def _rope_bwd(dq, dk, dv, cos, sin, dproj, name):
    S = dq.shape[0]
    scale = SWA_HD ** -0.5
    width = C_SM - C_QB

    def body(i, dq_ref, dk_ref, dv_ref, c_ref, s_ref, o_ref):
        c, s = c_ref[...], s_ref[...]
        for j in range(4):
            sl = slice(j * 128, (j + 1) * 128)
            o_ref[:, sl] = _rope_t128(dq_ref[:, sl] * scale, c, s).astype(BF16)
        o_ref[:, C_KB - C_QB:C_VB - C_QB] = _rope_t128(dk_ref[...], c, s).astype(BF16)
        o_ref[:, C_VB - C_QB:] = dv_ref[...].astype(BF16)

    return _tok_call(body, name, S, min(S, 512),
                     [(dq, MIX, 0), (dk, 128, 0), (dv, 128, 0), (cos, 128, 0), (sin, 128, 0)], [],
                     [(width, BF16, C_QB // width, dproj)])[0]


def _swa_band(i, k_ref, v_ref):
    pstart = pl.multiple_of(jnp.maximum(i - 1, 0) * WINDOW, WINDOW)
    cstart = pl.multiple_of(i * WINDOW, WINDOW)
    kb = jnp.concatenate([k_ref[pl.ds(pstart, WINDOW), :], k_ref[pl.ds(cstart, WINDOW), :]], axis=0)
    vb = jnp.concatenate([v_ref[pl.ds(pstart, WINDOW), :], v_ref[pl.ds(cstart, WINDOW), :]], axis=0)
    qi = lax.broadcasted_iota(jnp.int32, (WINDOW, 2 * WINDOW), 0)
    sj = lax.broadcasted_iota(jnp.int32, (WINDOW, 2 * WINDOW), 1)
    mask = (sj > qi) & (sj <= qi + WINDOW) & ((i > 0) | (sj >= WINDOW))
    return kb, vb, mask, pstart, cstart


def _swa_probs(qs, kh, mask, sinks):
    logits = [jnp.where(mask, _dg(qh, kh, 1, 1), -1e30) for qh in qs]
    m = [jnp.maximum(jnp.max(l, axis=-1, keepdims=True), s) for l, s in zip(logits, sinks)]
    p = [jnp.exp(l - mm) for l, mm in zip(logits, m)]
    ps = [jnp.exp(s - mm) for s, mm in zip(sinks, m)]
    inv = [1.0 / (jnp.sum(pp, axis=-1, keepdims=True) + s) for pp, s in zip(p, ps)]
    return [pp * iv for pp, iv in zip(p, inv)], [s * iv for s, iv in zip(ps, inv)]


def _swa_fwd(q, k, v, sinks, name):
    S = q.shape[0]
    G = SWA_H // SWA_KV

    def body(i, q_ref, k_ref, v_ref, s_ref, o_ref):
        kb, vb, mask, _, _ = _swa_band(i, k_ref, v_ref)
        qv = q_ref[...]
        for kv in range(SWA_KV):
            ksl = slice(kv * SWA_HD, (kv + 1) * SWA_HD)
            heads = range(kv * G, (kv + 1) * G)
            pn, _ = _swa_probs([qv[:, h * SWA_HD:(h + 1) * SWA_HD] for h in heads], kb[:, ksl], mask,
                               [s_ref[0, h] for h in heads])
            outs = [_dg(p.astype(BF16), vb[:, ksl], 1, 0) for p in pn]
            for h, o in zip(heads, outs):
                o_ref[:, h * SWA_HD:(h + 1) * SWA_HD] = o.astype(BF16)

    return _tok_call(body, name, S, WINDOW, [(q, MIX, 0)], [k, v], [(MIX, BF16)], smem_in=[sinks])[0]


def _swa_bwd(q, k, v, sinks, dout, name):
    S = q.shape[0]

    def body(i, q_ref, do_ref, k_ref, v_ref, s_ref, dq_ref, dk_ref, dv_ref, ds_ref):
        kb, vb, mask, pstart, cstart = _swa_band(i, k_ref, v_ref)
        qv, dov = q_ref[...], do_ref[...]
        lane = lax.broadcasted_iota(jnp.int32, (1, 128), 1)
        dsink = jnp.zeros((1, 128), F32)
        dkb, dvb = [], []
        G = SWA_H // SWA_KV
        for kv in range(SWA_KV):
            ksl = slice(kv * SWA_HD, (kv + 1) * SWA_HD)
            heads = range(kv * G, (kv + 1) * G)
            qs = [qv[:, h * SWA_HD:(h + 1) * SWA_HD] for h in heads]
            dos = [dov[:, h * SWA_HD:(h + 1) * SWA_HD].astype(BF16) for h in heads]
            pn, psn = _swa_probs(qs, kb[:, ksl], mask, [s_ref[0, h] for h in heads])
            dp = [_dg(d, vb[:, ksl], 1, 1) for d in dos]
            delta = [jnp.sum(a * b, axis=-1, keepdims=True) for a, b in zip(dp, pn)]
            dsc = [(p * (a - d)).astype(BF16) for p, a, d in zip(pn, dp, delta)]
            dqs = [_dg(s, kb[:, ksl], 1, 0) for s in dsc]
            dks = [_dg(s, qh, 0, 0) for s, qh in zip(dsc, qs)]
            dvs = [_dg(p.astype(BF16), d, 0, 0) for p, d in zip(pn, dos)]
            for n_, h in enumerate(heads):
                dq_ref[:, h * SWA_HD:(h + 1) * SWA_HD] = dqs[n_]
                dsink = dsink + jnp.where(lane == h, -jnp.sum(psn[n_] * delta[n_], axis=0, keepdims=True), 0.0)
            dkb.append((dks[0] + dks[1]) + (dks[2] + dks[3]))
            dvb.append((dvs[0] + dvs[1]) + (dvs[2] + dvs[3]))
        dkb = jnp.concatenate(dkb, axis=1)
        dvb = jnp.concatenate(dvb, axis=1)

        @pl.when(i == 0)
        def _():
            dk_ref[...] = jnp.zeros_like(dk_ref)
            dv_ref[...] = jnp.zeros_like(dv_ref)

        dk_ref[pl.ds(pstart, WINDOW), :] += dkb[:WINDOW]
        dv_ref[pl.ds(pstart, WINDOW), :] += dvb[:WINDOW]
        dk_ref[pl.ds(cstart, WINDOW), :] += dkb[WINDOW:]
        dv_ref[pl.ds(cstart, WINDOW), :] += dvb[WINDOW:]
        _acc(ds_ref, dsink, i)

    return _tok_call(body, name, S, WINDOW, [(q, MIX, 0), (dout, MIX, 0)], [k, v], [(MIX, F32)],
                     [((S, 128), F32), ((S, 128), F32), ((1, 128), F32)], smem_in=[sinks])


def _shift_rows(xs, k):
    return xs if k == 0 else pltpu.roll(xs, k, 0)


def _dn_conv(x_ref, p_ref, w_ref, i):
    hr = p_ref.shape[0]
    halo = jnp.where(i > 0, p_ref[...].astype(F32), 0.0)
    xs = jnp.concatenate([halo, x_ref[...].astype(F32)], axis=0)
    sh = [_shift_rows(xs, DN_CONV - 1 - t)[hr:] for t in range(DN_CONV)]
    pre = sh[0] * w_ref[0:1, :]
    for t in range(1, DN_CONV):
        pre = pre + sh[t] * w_ref[t:t + 1, :]
    return pre, sh


def _dn_gates(sm, alog, dtb):
    lane = lax.broadcasted_iota(jnp.int32, sm.shape, 1)
    return jnp.where(lane < DN_H, _sigmoid(sm), -jnp.exp(alog) * _softplus(sm + dtb))


def _dn_pre_fwd(proj, conv_w, alog_l, dtb_l, name):
    S = proj.shape[0]
    scale = DN_HD ** -0.5

    def body(i, x_ref, sm_ref, p_ref, w_ref, al_ref, db_ref, q_ref, k_ref, v_ref, bg_ref):
        pre, _ = _dn_conv(x_ref, p_ref, w_ref, i)
        a = _silu(pre)
        for h in range(DN_H):
            sl = slice(h * DN_HD, (h + 1) * DN_HD)
            qh, kh = a[:, sl], a[:, MIX + h * DN_HD:MIX + (h + 1) * DN_HD]
            q_ref[:, sl] = qh * (lax.rsqrt(jnp.sum(qh * qh, axis=-1, keepdims=True) + EPS) * scale)
            k_ref[:, sl] = kh * lax.rsqrt(jnp.sum(kh * kh, axis=-1, keepdims=True) + EPS)
        v_ref[...] = a[:, 2 * MIX:]
        bg_ref[...] = _dn_gates(sm_ref[...].astype(F32), al_ref[...], db_ref[...])

    TB = min(S, 256)
    return _tok_call(body, name, S, TB, [(proj, 3 * MIX, C_QKV // (3 * MIX)), (proj, 128, C_SM // 128)],
                     [conv_w, alog_l, dtb_l], [(MIX, F32), (MIX, F32), (MIX, F32), (128, F32)],
                     prev_in=[(proj, 3 * MIX, C_QKV // (3 * MIX))])


def _dn_pre_bwd1(proj, conv_w, alog_l, dtb_l, dq, dk, dv, dbg, dproj, name):
    S = proj.shape[0]
    scale = DN_HD ** -0.5

    def body(i, x_ref, sm_ref, dq_ref, dk_ref, dv_ref, dbg_ref, p_ref, w_ref, al_ref, db_ref, dpre_ref, dsm_ref,
             dw_ref, dal_ref, ddb_ref):
        pre, sh = _dn_conv(x_ref, p_ref, w_ref, i)
        a = _silu(pre)
        da_parts = []
        for part, (g_ref, sc) in enumerate(((dq_ref, scale), (dk_ref, 1.0))):
            for h in range(DN_H):
                xh = a[:, part * MIX + h * DN_HD:part * MIX + (h + 1) * DN_HD]
                rs = lax.rsqrt(jnp.sum(xh * xh, axis=-1, keepdims=True) + EPS)
                y = xh * rs
                dy = g_ref[:, h * DN_HD:(h + 1) * DN_HD] * sc
                da_parts.append(rs * (dy - y * jnp.sum(dy * y, axis=-1, keepdims=True)))
        da_parts.append(dv_ref[...])
        dpre = jnp.concatenate(da_parts, axis=1) * _dsilu(pre)
        dpre_ref[...] = dpre
        dw = jnp.concatenate([jnp.sum(dpre * sh[t], axis=0, keepdims=True) for t in range(DN_CONV)], axis=0)
        _acc(dw_ref, dw, i)
        sm, al, db, dbg_v = sm_ref[...].astype(F32), al_ref[...], db_ref[...], dbg_ref[...]
        lane = lax.broadcasted_iota(jnp.int32, sm.shape, 1)
        sg = _sigmoid(sm)
        gneg = -jnp.exp(al)
        is_g = (lane >= DN_H) & (lane < 2 * DN_H)
        d_al = jnp.where(is_g, dbg_v * gneg * _sigmoid(sm + db), 0.0)
        dsm_ref[...] = jnp.where(lane < DN_H, dbg_v * sg * (1.0 - sg), d_al).astype(BF16)
        _acc(ddb_ref, jnp.sum(d_al, axis=0, keepdims=True), i)
        _acc(dal_ref, jnp.sum(jnp.where(is_g, dbg_v * gneg * _softplus(sm + db), 0.0), axis=0, keepdims=True), i)

    TB = min(S, 256)
    return _tok_call(body, name, S, TB,
                     [(proj, 3 * MIX, C_QKV // (3 * MIX)), (proj, 128, C_SM // 128), (dq, MIX, 0), (dk, MIX, 0),
                      (dv, MIX, 0), (dbg, 128, 0)], [conv_w, alog_l, dtb_l],
                     [(3 * MIX, F32), (128, BF16, C_SM // 128, dproj)],
                     [((DN_CONV, 3 * MIX), F32), ((1, 128), F32), ((1, 128), F32)],
                     prev_in=[(proj, 3 * MIX, C_QKV // (3 * MIX))])


def _dn_pre_bwd2(dpre, conv_w, dproj, name):
    S = dpre.shape[0]
    TB = min(S, 256)
    nb = S // TB

    def body(i, d_ref, n_ref, w_ref, o_ref):
        halo = jnp.where(i < nb - 1, n_ref[...], 0.0)
        ds = jnp.concatenate([d_ref[...], halo], axis=0)
        out = ds[:TB] * w_ref[DN_CONV - 1:DN_CONV, :]
        for t in range(DN_CONV - 1):
            k = DN_CONV - 1 - t
            out = out + pltpu.roll(ds, TB + 8 - k, 0)[:TB] * w_ref[t:t + 1, :]
        o_ref[...] = out.astype(BF16)

    return _tok_call(body, name, S, TB, [(dpre, 3 * MIX, 0)], [conv_w],
                     [(3 * MIX, BF16, C_QKV // (3 * MIX), dproj)], next_in=[(dpre, 3 * MIX, 0)])[0]


def _dn_decay_terms(bgs, heads):
    C = DN_C
    ri = lax.broadcasted_iota(jnp.int32, (C, C), 0)
    ci = lax.broadcasted_iota(jnp.int32, (C, C), 1)
    tril, eye = ri >= ci, ri == ci
    beta = [b[:, h:h + 1] for b, h in zip(bgs, heads)]
    gcol = _dg_exact_lhs_many(tril, [jnp.broadcast_to(b[:, DN_H + h:DN_H + h + 1], (C, C))
                                     for b, h in zip(bgs, heads)], 1, 0)
    grow = [jnp.sum(jnp.where(eye, g, 0.0), axis=0, keepdims=True) for g in gcol]
    decay = [jnp.exp(jnp.where(tril, g - r, -1e30)) for g, r in zip(gcol, grow)]
    e_gc = [jnp.exp(g[:, 0:1]) for g in gcol]
    e_kd = [jnp.exp(g[C - 1:C, 0:1] - g[:, 0:1]) for g in gcol]
    cdec = [jnp.exp(g[C - 1:C, 0:1]) for g in gcol]
    return beta, decay, e_gc, e_kd, cdec


def _dn_nb(S):
    return 4 if S % (4 * DN_C) == 0 else 1


def _dn_prep_fwd(q, k, v, bg, name):
    S = q.shape[0]
    C, NB = DN_C, _dn_nb(S)
    TB = NB * C

    def kern(q_ref, k_ref, v_ref, bg_ref, t_ref, uw_ref, at_ref, qd_ref, kd_ref, dec_ref):
        lane = lax.broadcasted_iota(jnp.int32, (C, 128), 1)
        ri = lax.broadcasted_iota(jnp.int32, (C, C), 0)
        ci = lax.broadcasted_iota(jnp.int32, (C, C), 1)
        tril, eye = ri >= ci, ri == ci
        chains = [(cb, h) for cb in range(NB) for h in range(DN_H)]
        rows = lambda cb: slice(cb * C, (cb + 1) * C)
        head = lambda h: slice(h * DN_HD, (h + 1) * DN_HD)
        beta, decay, e_gc, e_kd, cdec = _dn_decay_terms([bg_ref[rows(cb), :] for cb, _ in chains],
                                                        [h for _, h in chains])
        qs = [q_ref[rows(cb), head(h)] for cb, h in chains]
        ks = [k_ref[rows(cb), head(h)] for cb, h in chains]
        kb = [kh * b for kh, b in zip(ks, beta)]
        x = [-jnp.where(ri > ci, _mm_nt(a, kh) * d, 0.0) for a, kh, d in zip(kb, ks, decay)]
        tm = [jnp.where(eye, 1.0, 0.0) + xi for xi in x]
        p = x
        p = _dg3_many(p, p, 1, 0)
        for it in range(5):
            if it == 4:
                tm = [t + tp for t, tp in zip(tm, _dg3_many(tm, p, 1, 0))]
                break
            both = _dg3_many([jnp.concatenate([t, pp], axis=0) for t, pp in zip(tm, p)], p, 1, 0)
            tm = [t + b[:C] for t, b in zip(tm, both)]
            p = [b[C:] for b in both]
        rhs = [jnp.concatenate([v_ref[rows(cb), head(h)] * b, a * e], axis=1)
               for (cb, h), b, a, e in zip(chains, beta, kb, e_gc)]
        sol = _dg3_many(tm, rhs, 1, 0)
        attn = [_mm_nt(qh, kh) * d for qh, kh, d in zip(qs, ks, decay)]
        for n_, (cb, h) in enumerate(chains):
            rs, sl, hc = rows(cb), head(h), slice(h * C, (h + 1) * C)
            t_ref[rs, hc] = tm[n_]
            uw_ref[rs, sl] = sol[n_][:, :DN_HD]
            uw_ref[rs, MIX + h * DN_HD:MIX + (h + 1) * DN_HD] = sol[n_][:, DN_HD:]
            at_ref[rs, hc] = attn[n_]
            qd_ref[rs, sl] = (qs[n_] * e_gc[n_]).astype(BF16)
            kd_ref[rs, sl] = (ks[n_] * e_kd[n_]).astype(BF16)
        for cb in range(NB):
            dec = jnp.zeros((C, 128), F32)
            for h in range(DN_H):
                dec = dec + jnp.where(lane == h, cdec[cb * DN_H + h], 0.0)
            dec_ref[rows(cb), :] = dec

    tok = lambda w: pl.BlockSpec((TB, w), lambda i: (i, 0))
    return pl.pallas_call(
        kern, name=name, grid=(S // TB,), in_specs=[tok(MIX), tok(MIX), tok(MIX), tok(128)],
        out_specs=[tok(DN_H * C), tok(2 * MIX), tok(DN_H * C), tok(MIX), tok(MIX), tok(128)],
        out_shape=[jax.ShapeDtypeStruct((S, DN_H * C), F32), jax.ShapeDtypeStruct((S, 2 * MIX), F32),
                   jax.ShapeDtypeStruct((S, DN_H * C), F32), jax.ShapeDtypeStruct((S, MIX), BF16),
                   jax.ShapeDtypeStruct((S, MIX), BF16), jax.ShapeDtypeStruct((S, 128), F32)],
        compiler_params=_cparams(("parallel",)),
    )(q, k, v, bg)


def _dn_scan_fwd(uw, at, qd, kd, dec, name):
    S = uw.shape[0]
    C, NB = DN_C, _dn_nb(S)
    TB = NB * C
    SR = DN_H * DN_HD

    def kern(uw_ref, at_ref, qd_ref, kd_ref, dec_ref, o_ref, vn_ref, st_ref, state):
        @pl.when(pl.program_id(0) == 0)
        def _():
            state[...] = jnp.zeros_like(state)

        for cb in range(NB):
            rs = slice(cb * C, (cb + 1) * C)
            hs = range(DN_H)
            sls = [slice(h * DN_HD, (h + 1) * DN_HD) for h in hs]
            s_in = [state[sl, :] for sl in sls]
            ws = [_mm(uw_ref[rs, MIX + h * DN_HD:MIX + (h + 1) * DN_HD], s_in[h]) for h in hs]
            os_ = [_mm(qd_ref[rs, sls[h]], s_in[h]) for h in hs]
            vnew = [uw_ref[rs, sls[h]] - ws[h] for h in hs]
            oa = [_mm(at_ref[rs, h * C:(h + 1) * C], vnew[h]) for h in hs]
            kv = [_mm_tn(kd_ref[rs, sls[h]], vnew[h]) for h in hs]
            for h in hs:
                o_ref[rs, sls[h]] = os_[h] + oa[h]
                state[sls[h], :] = s_in[h] * dec_ref[cb * C:cb * C + 1, h:h + 1] + kv[h]
                st_ref[cb * SR + h * DN_HD:cb * SR + (h + 1) * DN_HD, :] = s_in[h]
                vn_ref[rs, sls[h]] = vnew[h]

    tok = lambda w: pl.BlockSpec((TB, w), lambda i: (i, 0))
    return pl.pallas_call(
        kern, name=name, grid=(S // TB,), in_specs=[tok(2 * MIX), tok(DN_H * C), tok(MIX), tok(MIX), tok(128)],
        out_specs=[tok(MIX), tok(MIX), pl.BlockSpec((NB * SR, DN_HD), lambda i: (i, 0))],
        out_shape=[jax.ShapeDtypeStruct((S, MIX), F32), jax.ShapeDtypeStruct((S, MIX), F32),
                   jax.ShapeDtypeStruct((S // C * SR, DN_HD), F32)],
        scratch_shapes=[pltpu.VMEM((SR, DN_HD), F32)],
        compiler_params=_cparams(("arbitrary",)),
    )(uw, at, qd, kd, dec)


def _dn_core_fwd(q, k, v, bg, name):
    tm, uw, at, qd, kd, dec = _dn_prep_fwd(q, k, v, bg, name + "_prep")
    o, vn, st = _dn_scan_fwd(uw, at, qd, kd, dec, name + "_scan")
    return o, dict(tm=tm, uw=uw, at=at, qd=qd, kd=kd, dec=dec, vn=vn, st=st)


def _dn_scan_bwd(sv, do, name):
    S = do.shape[0]
    C, NB = DN_C, _dn_nb(S)
    TB = NB * C
    SR = DN_H * DN_HD
    nb = S // TB

    def kern(do_ref, uw_ref, at_ref, qd_ref, kd_ref, dec_ref, vn_ref, st_ref, dvn_ref, dw_ref, dkd_ref, dc_ref, dstate):
        @pl.when(pl.program_id(0) == 0)
        def _():
            dstate[...] = jnp.zeros_like(dstate)

        lane = lax.broadcasted_iota(jnp.int32, (C, 128), 1)
        for cb in reversed(range(NB)):
            rs = slice(cb * C, (cb + 1) * C)
            dcrow = jnp.zeros((C, 128), F32)
            for h in range(DN_H):
                sl = slice(h * DN_HD, (h + 1) * DN_HD)
                doh, ds_o = do_ref[rs, sl], dstate[sl, :]
                s_in = st_ref[cb * SR + h * DN_HD:cb * SR + (h + 1) * DN_HD, :]
                d_vnew = _mm_tn(at_ref[rs, h * C:(h + 1) * C], doh) + _mm(kd_ref[rs, sl], ds_o)
                dvn_ref[rs, sl] = d_vnew
                dw_ref[rs, sl] = -_mm_nt(d_vnew, s_in)
                dkd_ref[rs, sl] = _mm_nt(vn_ref[rs, sl], ds_o)
                d_c = jnp.sum(jnp.sum(ds_o * s_in, axis=1, keepdims=True), axis=0, keepdims=True)
                dcrow = dcrow + jnp.where(lane == h, d_c, 0.0)
                dstate[sl, :] = (ds_o * dec_ref[cb * C:cb * C + 1, h:h + 1] + _mm_tn(qd_ref[rs, sl], doh)
                                 - _mm_tn(uw_ref[rs, MIX + h * DN_HD:MIX + (h + 1) * DN_HD], d_vnew))
            dc_ref[rs, :] = dcrow

    tok = lambda w: pl.BlockSpec((TB, w), lambda i: (nb - 1 - i, 0))
    return pl.pallas_call(
        kern, name=name, grid=(nb,),
        in_specs=[tok(MIX), tok(2 * MIX), tok(DN_H * C), tok(MIX), tok(MIX), tok(128), tok(MIX),
                  pl.BlockSpec((NB * SR, DN_HD), lambda i: (nb - 1 - i, 0))],
        out_specs=[tok(MIX), tok(MIX), tok(MIX), tok(128)],
        out_shape=[jax.ShapeDtypeStruct((S, MIX), F32)] * 3 + [jax.ShapeDtypeStruct((S, 128), F32)],
        scratch_shapes=[pltpu.VMEM((SR, DN_HD), F32)],
        compiler_params=_cparams(("arbitrary",)),
    )(do, sv["uw"], sv["at"], sv["qd"], sv["kd"], sv["dec"], sv["vn"], sv["st"])


def _dn_chunk_bwd(q, k, v, bg, sv, do, dvn, dw, dkd, dc, name):
    S = q.shape[0]
    C, NB = DN_C, _dn_nb(S)
    TB = NB * C
    SR = DN_H * DN_HD

    def kern(q_ref, k_ref, v_ref, bg_ref, t_ref, uw_ref, vn_ref, st_ref, do_ref, dvn_ref, dw_ref, dkd_ref, dc_ref,
             dq_ref, dk_ref, dv_ref, dbg_ref):
        lane = lax.broadcasted_iota(jnp.int32, (C, 128), 1)
        ri = lax.broadcasted_iota(jnp.int32, (C, C), 0)
        ci = lax.broadcasted_iota(jnp.int32, (C, C), 1)
        tril, eye, last = ri >= ci, ri == ci, ri[:, 0:1] == C - 1
        chains = [(cb, h) for cb in range(NB) for h in range(DN_H)]
        each = lambda f, *ls: [f(*a) for a in zip(*ls)]
        rsum = lambda t: jnp.sum(t, axis=-1, keepdims=True)
        rows = lambda cb: slice(cb * C, (cb + 1) * C)
        head = lambda h: slice(h * DN_HD, (h + 1) * DN_HD)
        tok = lambda ref: [ref[rows(cb), head(h)] for cb, h in chains]
        beta, decay, e_gc, e_kd, cdec = _dn_decay_terms([bg_ref[rows(cb), :] for cb, _ in chains],
                                                        [h for _, h in chains])
        qs, ks, vs, dos, vnew, d_kd = tok(q_ref), tok(k_ref), tok(v_ref), tok(do_ref), tok(vn_ref), tok(dkd_ref)
        s_in = [st_ref[cb * SR + h * DN_HD:cb * SR + (h + 1) * DN_HD, :] for cb, h in chains]
        d_c = [dc_ref[cb * C:cb * C + 1, h:h + 1] for cb, h in chains]
        kb = each(lambda a, b: a * b, ks, beta)
        kk = each(_mm_nt, kb, ks)
        attn = each(lambda a, b, d: _mm_nt(a, b) * d, qs, ks, decay)
        d_qd = each(_mm_nt, dos, s_in)
        d_attn = each(_mm_nt, dos, vnew)
        d_sol = [jnp.concatenate([dvn_ref[rows(cb), head(h)], dw_ref[rows(cb), head(h)]], axis=1) for cb, h in chains]
        sol = [jnp.concatenate([uw_ref[rows(cb), head(h)], uw_ref[rows(cb), MIX + h * DN_HD:MIX + (h + 1) * DN_HD]],
                               axis=1) for cb, h in chains]
        d_rhs = _dg3_many([t_ref[rows(cb), h * C:(h + 1) * C] for cb, h in chains], d_sol, 0, 0)
        d_a = _dg3_many(d_rhs, sol, 1, 1)
        d_kk = each(lambda a, d: jnp.where(ri > ci, -a, 0.0) * d, d_a, decay)
        d_qk = each(lambda a, d: a * d, d_attn, decay)
        dm = each(lambda a, b, c_, d: a * b + c_ * d, d_kk, kk, d_attn, attn)
        d_vb = [t[:, :DN_HD] for t in d_rhs]
        dz = [t[:, DN_HD:] for t in d_rhs]
        d_kb = each(lambda z, e, a, kh: z * e + _mm(a, kh), dz, e_gc, d_kk, ks)
        d_k = each(lambda a, b, c_, q: _mm_tn(a, b) + _mm_tn(c_, q), d_kk, kb, d_qk, qs)
        d_q = each(lambda a, kh, b, e: _mm(a, kh) + b * e, d_qk, ks, d_qd, e_gc)
        t_kd = each(lambda a, kh, e: rsum(a * kh * e), d_kd, ks, e_kd)
        d_gl = each(lambda t, c_, cd: jnp.sum(t, axis=0, keepdims=True) + c_ * cd, t_kd, d_c, cdec)
        d_gc = each(lambda z, a, e, m, b, q, t, gl:
                    rsum(z * a) * e + rsum(m) - rsum(jnp.where(eye, jnp.sum(m, axis=0, keepdims=True), 0.0))
                    + rsum(b * q) * e - t + jnp.where(last, gl, 0.0),
                    dz, kb, e_gc, dm, d_qd, qs, t_kd, d_gl)
        d_g = _dg_exact_lhs_many(ri <= ci, [jnp.broadcast_to(t, (C, 128)) for t in d_gc], 1, 0)
        d_beta = each(lambda a, v_, b, kh: rsum(a * v_) + rsum(b * kh), d_vb, vs, d_kb, ks)
        for n_, (cb, h) in enumerate(chains):
            dq_ref[rows(cb), head(h)] = d_q[n_]
            dk_ref[rows(cb), head(h)] = d_k[n_] + d_kd[n_] * e_kd[n_] + d_kb[n_] * beta[n_]
            dv_ref[rows(cb), head(h)] = d_vb[n_] * beta[n_]
        for cb in range(NB):
            dbg = jnp.zeros((C, 128), F32)
            for h in range(DN_H):
                n_ = cb * DN_H + h
                dbg = dbg + jnp.where(lane == h, d_beta[n_], 0.0) + jnp.where(lane == DN_H + h, d_g[n_], 0.0)
            dbg_ref[rows(cb), :] = dbg

    tok = lambda w: pl.BlockSpec((TB, w), lambda i: (i, 0))
    return pl.pallas_call(
        kern, name=name, grid=(S // TB,),
        in_specs=[tok(MIX), tok(MIX), tok(MIX), tok(128), tok(DN_H * C), tok(2 * MIX), tok(MIX),
                  pl.BlockSpec((NB * SR, DN_HD), lambda i: (i, 0)), tok(MIX), tok(MIX), tok(MIX), tok(MIX), tok(128)],
        out_specs=[tok(MIX), tok(MIX), tok(MIX), tok(128)],
        out_shape=[jax.ShapeDtypeStruct((S, MIX), F32)] * 3 + [jax.ShapeDtypeStruct((S, 128), F32)],
        compiler_params=_cparams(("parallel",)),
    )(q, k, v, bg, sv["tm"], sv["uw"], sv["vn"], sv["st"], do, dvn, dw, dkd, dc)


def _dn_core_bwd(q, k, v, bg, sv, do, name):
    dvn, dw, dkd, dc = _dn_scan_bwd(sv, do, name + "_scan")
    return _dn_chunk_bwd(q, k, v, bg, sv, do, dvn, dw, dkd, dc, name + "_chunk")


def _dn_post_fwd(o, proj, ng, name):
    S = o.shape[0]

    def body(i, o_ref, z_ref, g_ref, out_ref):
        gv = g_ref[...]
        for h in range(DN_H):
            sl = slice(h * DN_HD, (h + 1) * DN_HD)
            oh = o_ref[:, sl]
            r = lax.rsqrt(jnp.mean(oh * oh, axis=-1, keepdims=True) + EPS)
            out_ref[:, sl] = (oh * r * gv * _silu(z_ref[:, sl].astype(F32))).astype(BF16)

    return _tok_call(body, name, S, min(S, 512), [(o, MIX, 0), (proj, MIX, C_ZC // MIX)], [ng], [(MIX, BF16)])[0]


def _dn_post_bwd(o, proj, ng, dout, dproj, name):
    S = o.shape[0]

    def body(i, o_ref, z_ref, do_ref, g_ref, dov_ref, dz_ref, dg_ref):
        gv = g_ref[...]
        dg = jnp.zeros((1, DN_HD), F32)
        for h in range(DN_H):
            sl = slice(h * DN_HD, (h + 1) * DN_HD)
            oh, zh, dh = o_ref[:, sl], z_ref[:, sl].astype(F32), do_ref[:, sl].astype(F32)
            r = lax.rsqrt(jnp.mean(oh * oh, axis=-1, keepdims=True) + EPS)
            dz_ref[:, sl] = (dh * oh * r * gv * _dsilu(zh)).astype(BF16)
            dx, dgh = _rms_bwd_vals(oh, gv, dh * _silu(zh))
            dov_ref[:, sl] = dx
            dg = dg + dgh
        _acc(dg_ref, dg, i)

    return _tok_call(body, name, S, min(S, 512), [(o, MIX, 0), (proj, MIX, C_ZC // MIX), (dout, MIX, 0)], [ng],
                     [(MIX, F32), (MIX, BF16, C_ZC // MIX, dproj)], [((1, DN_HD), F32)])


def _layer_params(w, big, l):
    lane = jnp.arange(128)
    is_g = (lane >= DN_H) & (lane < 2 * DN_H)
    spread = lambda t: jnp.where(is_g, jnp.tile(t, 128 // DN_H), 0.0).reshape(1, 128)
    tril = jnp.tril(jnp.ones((SGU_T, SGU_T), bool))
    return dict(
        win=big["w_in"], rest=big["rest"], conv=w["dn_conv_w"][l], attn_norm=w["attn_norm"][l].reshape(1, -1), ffn_norm=w["ffn_norm"][l].reshape(1, -1),
        lg=w["sgu_ln_g"][l].reshape(1, -1), lb=w["sgu_ln_b"][l].reshape(1, -1),
        wc=jnp.where(tril, w["sgu_w"][l], 0.0).reshape(SGU_G * SGU_T, SGU_T), bst=w["sgu_b"][l].T,
        sinks=w["attn_sinks"][l].reshape(1, -1), alog=spread(w["dn_a_log"][l]), dtb=spread(w["dn_dt_bias"][l]),
        ng=w["dn_norm"][l].reshape(1, -1))


def _layer_fwd(x, p, cos, sin, l):
    n = lambda s: f"l{l}_{s}"
    h = _rms_fwd(x, p["attn_norm"], n("rms1"))
    if callable(p["win"]):
        p["win"] = p["win"](h)
    proj = _matmul(h, p["win"], out_dtype=BF16, name=n("mm_in"))
    out_a = _sgu_fwd(proj, p["lg"], p["lb"], p["wc"], p["bst"], n("sgu_fwd"))
    qr, kr, vr = _rope_fwd(proj, cos, sin, n("rope_fwd"))
    out_b = _swa_fwd(qr, kr, vr, p["sinks"], n("swa_fwd"))
    q, k, v, bg = _dn_pre_fwd(proj, p["conv"], p["alog"], p["dtb"], n("dn_pre_fwd"))
    o, dn = _dn_core_fwd(q, k, v, bg, n("dn_core_fwd"))
    out_c = _dn_post_fwd(o, proj, p["ng"], n("dn_post_fwd"))
    outs = (out_a, out_b, out_c)
    rest = p.pop("rest")(out_c)
    p.update(wb=rest["w_branch"], wout=rest["w_out"], wgu=rest["w_gate_up"], wdown=rest["w_down"])
    bds = [_matmul(outs[j], p["wb"][j], out_dtype=BF16, name=n(f"mm_branch{j}")) for j in range(3)]
    merged = _merge_fwd(proj, bds, n("merge_fwd"))
    x1 = _matmul(merged, p["wout"], add=x, name=n("mm_out"))
    h2 = _rms_fwd(x1, p["ffn_norm"], n("rms2"))
    gu = _matmul(h2, p["wgu"], out_dtype=BF16, name=n("mm_gu"))
    act = _swiglu_fwd(gu, n("swiglu_fwd"))
    x2 = _matmul(act, p["wdown"], add=x1, name=n("mm_down"))
    saved = dict(x=x, h=h, proj=proj, outs=outs, qr=qr, kr=kr, vr=vr, q=q, k=k, v=v, bg=bg, o=o, dn=dn, bds=bds,
                 merged=merged, x1=x1, h2=h2, gu=gu, act=act)
    return x2, saved


def _layer_bwd(dx2, dx2_b, s, p, cos, sin, l, early=None):
    n = lambda t: f"l{l}_{t}"
    proj = s["proj"]
    g = {}
    g["w_down"] = _matmul(s["act"], dx2_b, ta=True, out_dtype=BF16, name=n("wg_down"))
    dact = _matmul(dx2_b, p["wdown"], tb=True, out_dtype=BF16, name=n("dg_down"))
    dgu = _swiglu_bwd(s["gu"], dact, n("swiglu_bwd"))
    g["w_gate_up"] = _matmul(s["h2"], dgu, ta=True, out_dtype=BF16, name=n("wg_gu"))
    dh2 = _matmul(dgu, p["wgu"], tb=True, name=n("dg_gu"))
    dx1, dx1_b, g["ffn_norm"] = _rms_bwd_add(s["x1"], p["ffn_norm"], dh2, dx2, n("rms2_bwd"))
    g["w_out"] = _matmul(s["merged"], dx1_b, ta=True, out_dtype=BF16, name=n("wg_out"))
    dm = _matmul(dx1_b, p["wout"], tb=True, name=n("dg_out"))
    dbd0, dbd1, dbd2, dproj = _merge_bwd(proj, s["bds"], dm, n("merge_bwd"))
    dbds = (dbd0, dbd1, dbd2)
    g["w_branch"] = jnp.stack([_matmul(s["outs"][j], dbds[j], ta=True, out_dtype=BF16, name=n(f"wg_branch{j}"))
                               for j in range(3)])
    douts = [_matmul(dbds[j], p["wb"][j], tb=True, name=n(f"dg_branch{j}")) for j in range(3)]
    lg = p["lg"]
    if early is not None:
        token = early({k: g.pop(k) for k in ("w_down", "w_gate_up", "w_out", "w_branch")})
        lg = lg if token is None else lg + token[0, 0]
    dproj, g["sgu_ln_g"], g["sgu_ln_b"], dwc, dbs = _sgu_bwd(proj, lg, p["lb"], p["wc"], p["bst"], douts[0], dproj,
                                                             n("sgu_bwd"))
    g["sgu_w"] = dwc.reshape(SGU_G, SGU_T, SGU_T)
    g["sgu_b"] = dbs.T
    dqr, dkr, dvr, dsink = _swa_bwd(s["qr"], s["kr"], s["vr"], p["sinks"], douts[1], n("swa_bwd"))
    g["attn_sinks"] = dsink[0, :SWA_H]
    dproj = _rope_bwd(dqr, dkr, dvr, cos, sin, dproj, n("rope_bwd"))
    do, dproj, dng = _dn_post_bwd(s["o"], proj, p["ng"], douts[2], dproj, n("dn_post_bwd"))
    g["dn_norm"] = dng[0]
    dq, dk, dv, dbg = _dn_core_bwd(s["q"], s["k"], s["v"], s["bg"], s["dn"], do, n("dn_core_bwd"))
    dpre, dproj, g["dn_conv_w"], dal, ddb = _dn_pre_bwd1(proj, p["conv"], p["alog"], p["dtb"], dq, dk, dv, dbg, dproj,
                                                         n("dn_pre_bwd1"))
    g["dn_a_log"] = dal[0, DN_H:2 * DN_H]
    g["dn_dt_bias"] = ddb[0, DN_H:2 * DN_H]
    dproj = _dn_pre_bwd2(dpre, p["conv"], dproj, n("dn_pre_bwd2"))
    g["w_in"] = _matmul(s["h"], dproj, ta=True, out_dtype=BF16, name=n("wg_in"))
    attn_norm = p["attn_norm"]
    if early is not None:
        token = early({"w_in": g.pop("w_in")})
        attn_norm = attn_norm if token is None else attn_norm + token[0, 0]
    dh = _matmul(dproj, p["win"], tb=True, name=n("dg_in"))
    dx, dx_b, g["attn_norm"] = _rms_bwd_add(s["x"], attn_norm, dh, dx1, n("rms1_bwd"))
    g["attn_norm"], g["ffn_norm"] = g["attn_norm"][0], g["ffn_norm"][0]
    g["sgu_ln_g"], g["sgu_ln_b"] = g["sgu_ln_g"][0], g["sgu_ln_b"][0]
    return dx, dx_b, g


def _local_step(x, positions, target, w, big_of_layer, on_grads):
    cos, sin = _rope_tables(positions)
    params, saves, xs = [], [], x
    for l in range(DEPTH):
        params.append(_layer_params(w, big_of_layer(l, xs), l))
        xs, sv = _layer_fwd(xs, params[l], cos, sin, l)
        saves.append(sv)
    dx, dx_b, loss_row, dgf = _final_loss(xs, w["final_norm"].reshape(1, -1), target)
    grads = [None] * DEPTH
    for l in reversed(range(DEPTH)):
        early = functools.partial(on_grads, l) if l == 0 else None
        dx, dx_b, grads[l] = _layer_bwd(dx, dx_b, saves[l], params[l], cos, sin, l, early)
        left = {k: grads[l].pop(k) for k in BIG if k in grads[l]}
        token = on_grads(l, left) if left else None
        if token is not None and l > 0:
            params[l - 1] = dict(params[l - 1], ffn_norm=params[l - 1]["ffn_norm"] + token[0, 0])
    stacked = {k: jnp.stack([grads[l][k] for l in range(DEPTH)]) for k in grads[0]}
    stacked["final_norm"] = dgf[0]
    return loss_row[0, 0], dx, stacked


MESH = pl.DeviceIdType.MESH
HBM_SPEC = pl.BlockSpec(memory_space=pltpu.HBM)
VMEM_SPEC = pl.BlockSpec(memory_space=pltpu.VMEM)
N_CHIPS = 4
FLIPS = tuple((fx, fy, fc) for fx in (0, 1) for fy in (0, 1) for fc in (0, 1))[1:]
BIG = ("w_in", "w_branch", "w_out", "w_gate_up", "w_down")
BIG_SPEC = {
    "w_in": dict(rows=1024, cols=1792, axis=1, keep=1730, down=8),
    "w_branch": dict(rows=1536, cols=256, axis=1, keep=256, down=2),
    "w_out": dict(rows=256, cols=1024, axis=0, keep=1024, down=1),
    "w_gate_up": dict(rows=1024, cols=1408, axis=1, keep=1408, down=8),
    "w_down": dict(rows=704, cols=1024, axis=0, keep=1024, down=4),
}
CONV_ROWS, CONV_COLS = DEPTH * DN_CONV, 3 * MIX // N_CHIPS


def _full_shape(k):
    sp = BIG_SPEC[k]
    return (sp["rows"], N_CHIPS * sp["cols"]) if sp["axis"] == 1 else (N_CHIPS * sp["rows"], sp["cols"])


def _me():
    return lax.axis_index("x"), lax.axis_index("y"), lax.axis_index("c")


def _peer(x, y, c, flip):
    fx, fy, fc = flip
    return (1 - x if fx else x, 1 - y if fy else y, 1 - c if fc else c)


class _Copies:
    def __init__(self, send_sems, recv_sems):
        self.send_sems, self.recv_sems, self.k, self.sent, self.landing = send_sems, recv_sems, 0, [], []

    def _copy(self, k, src, dst, to):
        return pltpu.make_async_remote_copy(src_ref=src, dst_ref=dst, send_sem=self.send_sems.at[k],
                                            recv_sem=self.recv_sems.at[k], device_id=to, device_id_type=MESH)

    def send(self, src, dst, to, lands):
        k = self.k
        self.k += 1
        cp = self._copy(k, src, dst, to)
        cp.start()
        self.sent.append(cp)
        self.landing.append(self._copy(k, lands, lands, to))
        return k

    def wait_landed(self, k):
        self.landing[k].wait_recv()

    def finish(self, landed=()):
        for k, cp in enumerate(self.landing):
            if k not in landed:
                cp.wait_recv()
        for cp in self.sent:
            cp.wait_send()


def _place_shard(shard, k, chip, layer, name):
    sp = BIG_SPEC[k]
    rows, cols, keep = sp["rows"], sp["cols"], sp["keep"]
    tr = _pick(rows, (256, 64))
    nb = rows // tr
    if sp["axis"] == 1:
        out_spec = pl.BlockSpec((tr, cols), lambda i, ch: (i, ch[0]))
    else:
        out_spec = pl.BlockSpec((tr, cols), lambda i, ch: (ch[0] * nb + i, 0))

    def kern(ch_ref, x_ref, o_ref):
        v = x_ref[0].astype(BF16)
        if keep == cols:
            o_ref[...] = v
        else:
            o_ref[:, :keep] = v
            o_ref[:, keep:] = jnp.zeros((tr, cols - keep), BF16)

    return pl.pallas_call(
        kern, name=name, out_shape=jax.ShapeDtypeStruct(_full_shape(k), BF16),
        grid_spec=pltpu.PrefetchScalarGridSpec(
            num_scalar_prefetch=1, grid=(nb,),
            in_specs=[pl.BlockSpec((1, tr, keep), lambda i, ch: (layer, i, 0))], out_specs=out_spec),
        compiler_params=_cparams(("parallel",)),
    )(chip, shard)


def _half_block(ref, k, s, half):
    sp = BIG_SPEC[k]
    hr = sp["rows"] // 2
    if sp["axis"] == 1:
        return ref.at[pl.ds(pl.multiple_of(half * hr, 16), hr), pl.ds(pl.multiple_of(s * sp["cols"], 128), sp["cols"])]
    return ref.at[pl.ds(pl.multiple_of(s * sp["rows"] + half * hr, 16), hr), :]


def _other_chips(x, y):
    return [(1 - x, y), (x, 1 - y), (1 - x, 1 - y)]


ALL_BIG = BIG


def _present(d):
    return tuple(k for k in ALL_BIG if k in d)


def _gather_layer(placed, conv):
    BIG = _present(placed)
    n = len(BIG)
    n_sem = 6 * n + 3

    def body(*refs):
        conv_ref = refs[n]
        out = dict(zip(BIG, refs[n + 1:2 * n + 1]))
        conv_out, send_sems, recv_sems, local_sem = refs[2 * n + 1:]
        x, y, c = _me()
        me = 2 * x + y
        chips = _other_chips(x, y)
        net = _Copies(send_sems, recv_sems)

        def conv_block(s):
            return conv_out.at[:, pl.ds(pl.multiple_of(s * CONV_COLS, 128), CONV_COLS)]

        local = pltpu.make_async_copy(conv_ref, conv_block(me), local_sem)
        local.start()
        first = {}
        for k in BIG:
            for j, (px, py) in enumerate(chips):
                first[k, j] = net.send(_half_block(out[k], k, me, c), _half_block(out[k], k, me, c), (px, py, c),
                                       _half_block(out[k], k, 2 * px + py, c))
        for px, py in chips:
            net.send(conv_ref, conv_block(me), (px, py, c), conv_block(2 * px + py))
        for k in BIG:
            for j, (px, py) in enumerate(chips):
                net.wait_landed(first[k, j])
                net.send(_half_block(out[k], k, 2 * px + py, c), _half_block(out[k], k, 2 * px + py, c), (x, y, 1 - c),
                         _half_block(out[k], k, 2 * px + py, 1 - c))
        net.finish(landed=set(first.values()))
        local.wait()

    out_shape = [jax.ShapeDtypeStruct(_full_shape(k), BF16) for k in BIG]
    out_shape.append(jax.ShapeDtypeStruct((CONV_ROWS, N_CHIPS * CONV_COLS), F32))
    outs = pl.pallas_call(
        body, name="gather_layer", out_shape=out_shape, in_specs=[HBM_SPEC] * (n + 1), out_specs=[HBM_SPEC] * (n + 1),
        input_output_aliases={i: i for i in range(n)},
        scratch_shapes=[pltpu.SemaphoreType.DMA((n_sem,)), pltpu.SemaphoreType.DMA((n_sem,)), pltpu.SemaphoreType.DMA],
    )(*[placed[k] for k in BIG], conv)
    return dict(zip(BIG, outs[:n])), outs[n]


SEM_SPEC = pl.BlockSpec(memory_space=pltpu.SEMAPHORE)


def _behind_copies(arrs, send_sems, recv_sems):
    x, y, c = _me()
    copies = []
    for i, k in enumerate(_present(arrs)):
        for j, (px, py) in enumerate(_other_chips(x, y)):
            copies.append(pltpu.make_async_remote_copy(
                src_ref=_half_block(arrs[k], k, 2 * x + y, c), dst_ref=_half_block(arrs[k], k, 2 * x + y, c),
                send_sem=send_sems.at[3 * i + j], recv_sem=recv_sems.at[3 * i + j], device_id=(px, py, c),
                device_id_type=MESH))
    return copies


def _gather_start(placed, after, tag):
    BIG = _present(placed)
    n = len(BIG)
    N_BEHIND = 3 * n

    def body(*refs):
        arrs = dict(zip(BIG, refs[n + 3:2 * n + 3]))
        send_sems, recv_sems = refs[n + 1], refs[n + 2]
        for cp in _behind_copies(arrs, send_sems, recv_sems):
            cp.start()
        refs[2 * n + 3][...] = jnp.zeros((8, 128), F32)

    outs = pl.pallas_call(
        body, name="gather_start" + tag,
        out_shape=(pltpu.SemaphoreType.DMA((N_BEHIND,)), pltpu.SemaphoreType.DMA((N_BEHIND,)),
                   *[pltpu.HBM(_full_shape(k), BF16) for k in BIG], jax.ShapeDtypeStruct((8, 128), F32)),
        in_specs=[HBM_SPEC] * n + [pl.BlockSpec(memory_space=pl.ANY)],
        out_specs=(SEM_SPEC, SEM_SPEC, *[HBM_SPEC] * n, VMEM_SPEC),
        input_output_aliases={i: i + 2 for i in range(n)},
        compiler_params=pltpu.CompilerParams(has_side_effects=pltpu.SideEffectType.DATAFLOW_SIDE_EFFECTING),
    )(*[pltpu.with_memory_space_constraint(placed[k], pltpu.HBM) for k in BIG], after)
    return outs[0], outs[1], dict(zip(BIG, outs[2:n + 2])), outs[n + 2]


def _gather_wait(send_sems, recv_sems, inflight, after, tag):
    BIG = _present(inflight)
    n = len(BIG)

    def body(*refs):
        arrs = dict(zip(BIG, refs[:n]))
        for cp in _behind_copies(arrs, refs[n], refs[n + 1]):
            cp.wait_send()
            cp.wait_recv()

    outs = pl.pallas_call(
        body, name="gather_wait" + tag, out_shape=tuple(pltpu.HBM(_full_shape(k), BF16) for k in BIG),
        in_specs=[HBM_SPEC] * n + [SEM_SPEC, SEM_SPEC, pl.BlockSpec(memory_space=pl.ANY)], out_specs=(HBM_SPEC,) * n,
        input_output_aliases={i: i for i in range(n)},
        compiler_params=pltpu.CompilerParams(has_side_effects=pltpu.SideEffectType.DATAFLOW_SIDE_EFFECTING),
    )(*[inflight[k] for k in BIG], send_sems, recv_sems, after)
    return dict(zip(BIG, outs))


def _gather_finish(arrs, tag):
    BIG = _present(arrs)
    n = len(BIG)
    N_BEHIND = 3 * n

    def body(*refs):
        out = dict(zip(BIG, refs[n:2 * n]))
        send_sems, recv_sems = refs[2 * n:]
        x, y, c = _me()
        net = _Copies(send_sems, recv_sems)
        for k in BIG:
            for px, py in _other_chips(x, y):
                net.send(_half_block(out[k], k, 2 * px + py, c), _half_block(out[k], k, 2 * px + py, c), (x, y, 1 - c),
                         _half_block(out[k], k, 2 * px + py, 1 - c))
        net.finish()

    outs = pl.pallas_call(
        body, name="gather_finish" + tag, out_shape=[jax.ShapeDtypeStruct(_full_shape(k), BF16) for k in BIG],
        in_specs=[HBM_SPEC] * n, out_specs=[HBM_SPEC] * n, input_output_aliases={i: i for i in range(n)},
        scratch_shapes=[pltpu.SemaphoreType.DMA((N_BEHIND,)), pltpu.SemaphoreType.DMA((N_BEHIND,))],
    )(*[arrs[k] for k in BIG])
    return dict(zip(BIG, outs))


def _row_chunks(ref, rows, n):
    step = rows // n
    return [ref.at[pl.ds(i * step, step), :] for i in range(n)]


def _half_pieces(ref, k, half):
    sp = BIG_SPEC[k]
    hr = sp["rows"] // 2
    if sp["axis"] == 1:
        return [ref.at[pl.ds(pl.multiple_of(half * hr, 16), hr), :]]
    return [ref.at[pl.ds(pl.multiple_of(s * sp["rows"] + half * hr, 16), hr), :] for s in range(N_CHIPS)]


def _half_shape(k):
    rows, cols = _full_shape(k)
    return rows // 2, cols


def _stacked_pieces(ref, k):
    sp = BIG_SPEC[k]
    hr = sp["rows"] // 2
    return [ref] if sp["axis"] == 1 else [ref.at[pl.ds(s * hr, hr), :] for s in range(N_CHIPS)]


def _chip_part(ref, k, s):
    sp = BIG_SPEC[k]
    hr = sp["rows"] // 2
    if sp["axis"] == 1:
        return ref.at[:, pl.ds(pl.multiple_of(s * sp["cols"], 128), sp["cols"])]
    return ref.at[pl.ds(pl.multiple_of(s * hr, 16), hr), :]


def _halves_to_sibling(grads, name):
    BIG = _present(grads)
    n = len(BIG)
    chunks = {k: max(BIG_SPEC[k]["down"] // 2, 1) if BIG_SPEC[k]["axis"] == 1 else 1 for k in BIG}
    n_sem = sum(chunks[k] if BIG_SPEC[k]["axis"] == 1 else N_CHIPS for k in BIG)

    def body(*refs):
        g = dict(zip(BIG, refs[:n]))
        out = dict(zip(BIG, refs[n:2 * n]))
        send_sems, recv_sems = refs[2 * n:]
        x, y, c = _me()
        net = _Copies(send_sems, recv_sems)
        for k in BIG:
            hr = BIG_SPEC[k]["rows"] // 2
            for src, dst in zip(_half_pieces(g[k], k, 1 - c), _stacked_pieces(out[k], k)):
                for s, d in zip(_row_chunks(src, hr, chunks[k]), _row_chunks(dst, hr, chunks[k])):
                    net.send(s, d, (x, y, 1 - c), d)
        net.finish()

    outs = pl.pallas_call(
        body, name=name, out_shape=[jax.ShapeDtypeStruct(_half_shape(k), BF16) for k in BIG],
        in_specs=[HBM_SPEC] * n, out_specs=[HBM_SPEC] * n,
        scratch_shapes=[pltpu.SemaphoreType.DMA((n_sem,)), pltpu.SemaphoreType.DMA((n_sem,))],
    )(*[grads[k] for k in BIG])
    return dict(zip(BIG, outs))


def _add_half(g, other, k, core, name):
    sp = BIG_SPEC[k]
    hr, cols = sp["rows"] // 2, _full_shape(k)[1]
    tr = _pick(hr, (256, 352, 128))
    nb = hr // tr
    if sp["axis"] == 1:
        grid = (nb,)
        g_spec = pl.BlockSpec((tr, cols), lambda i, c: (c[0] * nb + i, 0))
        h_spec = pl.BlockSpec((tr, cols), lambda i, c: (i, 0))
    else:
        grid = (N_CHIPS, nb)
        g_spec = pl.BlockSpec((tr, cols), lambda s, i, c: ((2 * s + c[0]) * nb + i, 0))
        h_spec = pl.BlockSpec((tr, cols), lambda s, i, c: (s * nb + i, 0))

    def kern(c_ref, a_ref, b_ref, o_ref):
        o_ref[...] = (a_ref[...].astype(F32) + b_ref[...].astype(F32)).astype(BF16)

    return pl.pallas_call(
        kern, name=name, out_shape=jax.ShapeDtypeStruct(_half_shape(k), BF16),
        grid_spec=pltpu.PrefetchScalarGridSpec(num_scalar_prefetch=1, grid=grid, in_specs=[g_spec, h_spec],
                                               out_specs=h_spec),
        compiler_params=_cparams(("parallel",) * len(grid)),
    )(core, g, other)


def _part_shape(k):
    return N_CHIPS - 1, BIG_SPEC[k]["rows"] // 2, BIG_SPEC[k]["cols"]


def _scatter_copies(sums, parts, send_sems, recv_sems):
    x, y, c = _me()
    copies = []
    for i, k in enumerate(_present(sums)):
        for j, (px, py) in enumerate(_other_chips(x, y)):
            copies.append(pltpu.make_async_remote_copy(
                src_ref=_chip_part(sums[k], k, 2 * px + py), dst_ref=parts[k].at[j], send_sem=send_sems.at[3 * i + j],
                recv_sem=recv_sems.at[3 * i + j], device_id=(px, py, c), device_id_type=MESH))
    return copies


def _scatter_start(sums, tag):
    BIG = _present(sums)
    n = len(BIG)
    N_BEHIND = 3 * n
    lands = [pltpu.with_memory_space_constraint(lax.empty(_part_shape(k), BF16), pltpu.HBM) for k in BIG]

    def body(*refs):
        outs = refs[2 * n + 2:4 * n + 2]
        for cp in _scatter_copies(dict(zip(BIG, outs[:n])), dict(zip(BIG, outs[n:])), refs[2 * n], refs[2 * n + 1]):
            cp.start()
        refs[4 * n + 2][...] = jnp.zeros((8, 128), F32)

    outs = pl.pallas_call(
        body, name="scatter_start" + tag,
        out_shape=(pltpu.SemaphoreType.DMA((N_BEHIND,)), pltpu.SemaphoreType.DMA((N_BEHIND,)),
                   *[pltpu.HBM(_half_shape(k), BF16) for k in BIG], *[pltpu.HBM(_part_shape(k), BF16) for k in BIG],
                   jax.ShapeDtypeStruct((8, 128), F32)),
        in_specs=[HBM_SPEC] * (2 * n), out_specs=(SEM_SPEC, SEM_SPEC, *[HBM_SPEC] * (2 * n), VMEM_SPEC),
        input_output_aliases={i: i + 2 for i in range(2 * n)},
        compiler_params=pltpu.CompilerParams(has_side_effects=pltpu.SideEffectType.DATAFLOW_SIDE_EFFECTING),
    )(*[pltpu.with_memory_space_constraint(sums[k], pltpu.HBM) for k in BIG], *lands)
    return outs[0], outs[1], outs[2:2 * n + 2], outs[2 * n + 2]


def _scatter_wait(send_sems, recv_sems, inflight, keys, after, tag):
    BIG = keys
    n = len(BIG)

    def body(*refs):
        for cp in _scatter_copies(dict(zip(BIG, refs[:n])), dict(zip(BIG, refs[n:2 * n])), refs[2 * n], refs[2 * n + 1]):
            cp.wait_send()
            cp.wait_recv()

    outs = pl.pallas_call(
        body, name="scatter_wait" + tag,
        out_shape=(*[pltpu.HBM(_half_shape(k), BF16) for k in BIG], *[pltpu.HBM(_part_shape(k), BF16) for k in BIG]),
        in_specs=[HBM_SPEC] * (2 * n) + [SEM_SPEC, SEM_SPEC, pl.BlockSpec(memory_space=pl.ANY)],
        out_specs=(HBM_SPEC,) * (2 * n), input_output_aliases={i: i for i in range(2 * n)},
        compiler_params=pltpu.CompilerParams(has_side_effects=pltpu.SideEffectType.DATAFLOW_SIDE_EFFECTING),
    )(*inflight, send_sems, recv_sems, after)
    return dict(zip(BIG, outs[:n])), dict(zip(BIG, outs[n:]))


def _sum_half(parts, own, k, where, layer, into, name):
    sp = BIG_SPEC[k]
    rows, cols, keep = sp["rows"], sp["cols"], sp["keep"]
    hr = rows // 2
    tr = _pick(hr, (256, 352, 128))
    nb = hr // tr
    if sp["axis"] == 1:
        own_spec = pl.BlockSpec((tr, cols), lambda i, w: (i, w[0]))
    else:
        own_spec = pl.BlockSpec((tr, cols), lambda i, w: (w[0] * nb + i, 0))

    def kern(w_ref, p_ref, own_ref, *rest):
        tot = own_ref[...].astype(F32)
        for j in range(N_CHIPS - 1):
            tot = tot + p_ref[j].astype(F32)
        rest[-1][0] = tot[:, :keep]

    in_specs = [pl.BlockSpec((N_CHIPS - 1, tr, cols), lambda i, w: (0, i, 0)), own_spec]
    args = [where, parts, own]
    if into is not None:
        in_specs.append(pl.BlockSpec(memory_space=pl.ANY))
        args.append(into)
    return pl.pallas_call(
        kern, name=name, out_shape=jax.ShapeDtypeStruct((DEPTH, rows, keep), F32),
        grid_spec=pltpu.PrefetchScalarGridSpec(
            num_scalar_prefetch=1, grid=(nb,), in_specs=in_specs,
            out_specs=pl.BlockSpec((1, tr, keep), lambda i, w: (layer, w[1] * nb + i, 0))),
        input_output_aliases={} if into is None else {3: 0},
        compiler_params=_cparams(("parallel",)),
    )(*args)


def _exchange_halves(red):
    n = len(BIG)

    def body(*refs):
        out = dict(zip(BIG, refs[n:2 * n]))
        send_sems, recv_sems = refs[2 * n:]
        x, y, c = _me()
        net = _Copies(send_sems, recv_sems)
        for k in BIG:
            hr = BIG_SPEC[k]["rows"] // 2
            for l in range(DEPTH):
                mine = out[k].at[l, pl.ds(pl.multiple_of(c * hr, 8), hr), :]
                theirs = out[k].at[l, pl.ds(pl.multiple_of((1 - c) * hr, 8), hr), :]
                net.send(mine, mine, (x, y, 1 - c), theirs)
        net.finish()

    outs = pl.pallas_call(
        body, name="exchange_halves",
        out_shape=[jax.ShapeDtypeStruct((DEPTH, BIG_SPEC[k]["rows"], BIG_SPEC[k]["keep"]), F32) for k in BIG],
        in_specs=[HBM_SPEC] * n, out_specs=[HBM_SPEC] * n, input_output_aliases={i: i for i in range(n)},
        scratch_shapes=[pltpu.SemaphoreType.DMA((DEPTH * n,)), pltpu.SemaphoreType.DMA((DEPTH * n,))],
    )(*[red[k] for k in BIG])
    return dict(zip(BIG, outs))


def _adam_vals(g, w, m, v):
    m2 = ADAM_B1 * m + (1.0 - ADAM_B1) * g
    v2 = ADAM_B2 * v + (1.0 - ADAM_B2) * (g * g)
    m_hat = m2 / (1.0 - ADAM_B1 ** ADAM_STEP)
    v_hat = v2 / (1.0 - ADAM_B2 ** ADAM_STEP)
    return -ADAM_LR * (m_hat / (jnp.sqrt(v_hat) + ADAM_EPS) + ADAM_WD * w), m2, v2


def _allreduce_small_adam(groups):
    ng = len(groups)

    def body(*refs):
        ins = [refs[4 * i:4 * i + 4] for i in range(ng)]
        outs = [refs[4 * ng + 4 * i:4 * ng + 4 * i + 4] for i in range(ng)]
        bufs = refs[8 * ng:9 * ng]
        send_sems, recv_sems = refs[9 * ng:]
        x, y, c = _me()
        me = 4 * x + 2 * y + c
        net = _Copies(send_sems, recv_sems)
        for (g_ref, _, _, _), buf in zip(ins, bufs):
            buf[me] = g_ref[...]
            for f in FLIPS:
                px, py, pc = _peer(x, y, c, f)
                net.send(g_ref, buf.at[me], (px, py, pc), buf.at[4 * px + 2 * py + pc])
        net.finish()
        for (_, w_ref, m_ref, v_ref), (gs_ref, d_ref, nm_ref, nv_ref), buf in zip(ins, outs, bufs):
            tot = buf[0]
            for d in range(1, 8):
                tot = tot + buf[d]
            gs_ref[...] = tot
            d_ref[...], nm_ref[...], nv_ref[...] = _adam_vals(tot, w_ref[...], m_ref[...], v_ref[...])

    shapes = [jax.ShapeDtypeStruct(g[0].shape, F32) for g in groups for _ in range(4)]
    outs = pl.pallas_call(
        body, name="allreduce_small", out_shape=shapes, in_specs=[VMEM_SPEC] * (4 * ng), out_specs=[VMEM_SPEC] * (4 * ng),
        scratch_shapes=[pltpu.VMEM((8,) + g[0].shape, F32) for g in groups]
        + [pltpu.SemaphoreType.DMA((7 * ng,)), pltpu.SemaphoreType.DMA((7 * ng,))],
        compiler_params=pltpu.CompilerParams(vmem_limit_bytes=VMEM_LIMIT),
    )(*[t for g in groups for t in g])
    return [outs[4 * i:4 * i + 4] for i in range(ng)]


def _adam(g, w, m, v, name, lead_block=1):
    shape = w.shape
    lead, rows, cols = math.prod(shape[:-2]), shape[-2], shape[-1]
    tr = _pick(rows, (256, 352, 64, 8, rows))
    spec = pl.BlockSpec((lead_block, tr, cols), lambda l, i: (l, i, 0))

    def kern(g_ref, w_ref, m_ref, v_ref, d_ref, nm_ref, nv_ref):
        d_ref[...], nm_ref[...], nv_ref[...] = _adam_vals(g_ref[...], w_ref[...], m_ref[...], v_ref[...])

    outs = pl.pallas_call(
        kern, name=name, grid=(lead // lead_block, rows // tr), in_specs=[spec] * 4, out_specs=[spec] * 3,
        out_shape=[jax.ShapeDtypeStruct((lead, rows, cols), F32)] * 3, compiler_params=_cparams(("parallel", "parallel")),
    )(*[t.reshape(lead, rows, cols) for t in (g, w, m, v)])
    return [o.reshape(shape) for o in outs]


SMALL = ("attn_norm", "sgu_ln_g", "sgu_ln_b", "sgu_w", "sgu_b", "attn_sinks", "dn_a_log", "dn_dt_bias", "dn_norm",
         "ffn_norm", "final_norm")
SMALL_2D = {"attn_norm": (DEPTH, D_MODEL), "ffn_norm": (DEPTH, D_MODEL), "final_norm": (1, D_MODEL),
            "sgu_ln_g": (DEPTH, MIX), "sgu_ln_b": (DEPTH, MIX), "sgu_w": (DEPTH * SGU_G * SGU_T, SGU_T),
            "sgu_b": (DEPTH * SGU_G, SGU_T), "dn_norm": (DEPTH, DN_HD)}
TINY = ("attn_sinks", "dn_a_log", "dn_dt_bias")


def _pack_tiny(vals, extra=None):
    flat = [vals[k].astype(F32).reshape(-1) for k in TINY] + ([] if extra is None else [extra.astype(F32).reshape(-1)])
    n = sum(f.shape[0] for f in flat)
    return jnp.concatenate(flat + [jnp.zeros((8 * 128 - n,), F32)]).reshape(8, 128)


def _unpack_tiny(tile, shapes):
    flat, out, o = tile.reshape(-1), {}, 0
    for k in TINY:
        n = math.prod(shapes[k])
        out[k] = flat[o:o + n].reshape(shapes[k])
        o += n
    return out, flat[o]


def _in_col_segments():
    shard, padded = IN_COLS // N_CHIPS, BIG_SPEC["w_in"]["cols"]
    segs, mine = [], 0
    for a, n in IN_PIECES:
        o = a
        while o < a + n:
            end = min(a + n, (o // shard + 1) * shard)
            segs.append(((o // shard) * padded + o % shard, mine + o - a, end - o))
            o = end
        mine += n
    return segs


def _move_cols(x, segs, out_cols, name):
    layers, rows, cols = x.shape
    tr = _pick(rows, (256, rows))
    gaps, at = [], 0
    for d, w in sorted((d, w) for _, d, w in segs):
        if d > at:
            gaps.append((at, d - at))
        at = d + w
    if at < out_cols:
        gaps.append((at, out_cols - at))

    def kern(x_ref, o_ref):
        for s, d, w in segs:
            o_ref[0, :, d:d + w] = x_ref[0, :, s:s + w]
        for d, w in gaps:
            o_ref[0, :, d:d + w] = jnp.zeros((tr, w), x.dtype)

    return pl.pallas_call(
        kern, name=name, grid=(layers, rows // tr), in_specs=[pl.BlockSpec((1, tr, cols), lambda l, i: (l, i, 0))],
        out_specs=pl.BlockSpec((1, tr, out_cols), lambda l, i: (l, i, 0)),
        out_shape=jax.ShapeDtypeStruct((layers, rows, out_cols), x.dtype), compiler_params=_cparams(("parallel", "parallel")),
    )(x)


WEIGHTS = ("attn_norm", "w_in", "sgu_ln_g", "sgu_ln_b", "sgu_w", "sgu_b", "attn_sinks", "dn_conv_w", "dn_a_log",
           "dn_dt_bias", "dn_norm", "w_branch", "w_out", "ffn_norm", "w_gate_up", "w_down", "final_norm")


def kernel(x, positions, attn_norm, w_in, sgu_ln_g, sgu_ln_b, sgu_w, sgu_b, attn_sinks, dn_conv_w, dn_a_log, dn_dt_bias, dn_norm, w_branch, w_out, ffn_norm, w_gate_up, w_down, final_norm, loss_target, m_attn_norm, m_w_in, m_sgu_ln_g, m_sgu_ln_b, m_sgu_w, m_sgu_b, m_attn_sinks, m_dn_conv_w, m_dn_a_log, m_dn_dt_bias, m_dn_norm, m_w_branch, m_w_out, m_ffn_norm, m_w_gate_up, m_w_down, m_final_norm, v_attn_norm, v_w_in, v_sgu_ln_g, v_sgu_ln_b, v_sgu_w, v_sgu_b, v_attn_sinks, v_dn_conv_w, v_dn_a_log, v_dn_dt_bias, v_dn_norm, v_w_branch, v_w_out, v_ffn_norm, v_w_gate_up, v_w_down, v_final_norm):
    given = dict(locals())
    W = {k: given[k] for k in WEIGHTS}
    M = {k: given["m_" + k] for k in WEIGHTS}
    V = {k: given["v_" + k] for k in WEIGHTS}
    chip = 2 * lax.axis_index("x") + lax.axis_index("y")
    core = lax.axis_index("c")
    chip1 = chip.astype(jnp.int32).reshape(1)
    where = jnp.stack([chip, core]).astype(jnp.int32)

    placed = [{k: _place_shard(W[k].reshape(DEPTH, BIG_SPEC[k]["rows"], BIG_SPEC[k]["keep"]), k, chip1, l,
                               f"place{l}_{k}") for k in BIG} for l in range(DEPTH)]
    _, conv_full = _gather_layer({}, dn_conv_w.reshape(CONV_ROWS, CONV_COLS))
    behind = {"in": _gather_start({"w_in": placed[0]["w_in"]}, conv_full, "in")}
    behind["0"] = _gather_start({k: placed[0][k] for k in BIG if k != "w_in"}, behind["in"][3], "0")
    behind["1"] = _gather_start(placed[1], behind["0"][3], "1")
    segs = _in_col_segments()

    def arrived(tag, after):
        send_sems, recv_sems, inflight, _ = behind[tag]
        return _gather_finish(_gather_wait(send_sems, recv_sems, inflight, after, tag), tag)

    def big_of_layer(l, x_l):
        got = {} if l == 0 else arrived("1", x_l)
        cols = lambda t: _move_cols(t[None], segs, IN_R, f"w_in_cols{l}")[0]

        def rest(after):
            full = got or arrived("0", after)
            return dict(full, w_branch=full["w_branch"].reshape(3, MIX, D_MODEL))

        return dict(w_in=cols(got["w_in"]) if got else (lambda after: cols(arrived("in", after)["w_in"])), rest=rest)

    w = {k: W[k] for k in SMALL}
    w["attn_norm"] = attn_norm + behind["1"][3][0, 0]
    w["dn_conv_w"] = conv_full.reshape(DEPTH, DN_CONV, 3 * MIX)

    core1 = core.astype(jnp.int32).reshape(1)
    back_segs = [(d, s, n) for s, d, n in segs]
    travelling, started, sums, parts = [], [], [{}, {}], [{}, {}]

    def on_grads(l, gl):
        gl, tag = dict(gl), f"{l}_{len(gl)}"
        if "w_in" in gl:
            gl["w_in"] = _move_cols(gl["w_in"][None], back_segs, _full_shape("w_in")[1], f"g_in_cols{l}")[0]
        if "w_branch" in gl:
            gl["w_branch"] = gl["w_branch"].reshape(3 * MIX, D_MODEL)
        sibling = _halves_to_sibling(gl, "halves_to_sibling" + tag)
        chip_sums = {k: _add_half(gl[k], sibling[k], k, core1, f"chip_sum{l}_{k}") for k in gl}
        send_sems, recv_sems, inflight, token = _scatter_start(chip_sums, tag)
        travelling.append((l, send_sems, recv_sems, inflight, _present(gl), tag))
        started.append(token)
        return token

    loss, dx, g = _local_step(x[0], positions[0], loss_target[0], w, big_of_layer, on_grads)

    grads = {}
    conv_2d = (CONV_ROWS, N_CHIPS * CONV_COLS)
    conv_zero = jnp.zeros(conv_2d, F32)
    groups = [tuple(d[k].reshape(SMALL_2D[k]) for d in (g, W, M, V)) for k in SMALL_2D]
    groups.append((g["dn_conv_w"].reshape(conv_2d), conv_zero, conv_zero, conv_zero))
    loss = loss + started[-1][0, 0]
    groups.append((_pack_tiny(g, loss), _pack_tiny(W), _pack_tiny(M), _pack_tiny(V)))
    summed = _allreduce_small_adam(groups)
    delta, new_m, new_v = {}, {}, {}
    for k, outs in zip(SMALL_2D, summed):
        for d, t in zip((grads, delta, new_m, new_v), outs):
            d[k] = t.reshape(W[k].shape)
    conv_sum = summed[len(SMALL_2D)][0].reshape(g["dn_conv_w"].shape)
    grads["dn_conv_w"] = lax.dynamic_slice_in_dim(conv_sum, chip * dn_conv_w.shape[2], dn_conv_w.shape[2], axis=2)
    tiny_shapes = {k: W[k].shape for k in TINY}
    tiny, loss_total = _unpack_tiny(summed[-1][0], tiny_shapes)
    grads.update(tiny)
    for d, t in zip((delta, new_m, new_v), summed[-1][1:]):
        d.update(_unpack_tiny(t, tiny_shapes)[0])

    for l, send_sems, recv_sems, inflight, keys, tag in travelling:
        landed = _scatter_wait(send_sems, recv_sems, inflight, keys, summed[0][0], tag)
        sums[l].update(landed[0])
        parts[l].update(landed[1])
    red = {k: _sum_half(parts[1][k], sums[1][k], k, where, 1, None, f"sum1_{k}") for k in BIG}
    red = {k: _sum_half(parts[0][k], sums[0][k], k, where, 0, red[k], f"sum0_{k}") for k in BIG}
    reduced = _exchange_halves(red)
    grads.update({k: reduced[k].reshape(W[k].shape) for k in BIG})
    for k in ("w_branch", "w_out", "w_gate_up", "w_down", "dn_conv_w"):
        delta[k], new_m[k], new_v[k] = _adam(grads[k], W[k], M[k], V[k], "adam_" + k)
    lead_first = lambda t: jnp.transpose(t, (2, 0, 1))
    outs = _adam(*[lead_first(d["w_in"]) for d in (grads, W, M, V)], "adam_w_in", lead_block=IN_COLS // N_CHIPS // 10)
    delta["w_in"], new_m["w_in"], new_v["w_in"] = (jnp.transpose(o, (1, 2, 0)) for o in outs)

    return (loss_total, dx[None], *[grads[k] for k in WEIGHTS], *[delta[k] for k in WEIGHTS],
            *[new_m[k] for k in WEIGHTS], *[new_v[k] for k in WEIGHTS])
```

```python
import functools
import math

import jax
import jax.numpy as jnp
from jax import lax
from jax.experimental import pallas as pl
from jax.experimental.pallas import tpu as pltpu

F32 = jnp.float32
BF16 = jnp.bfloat16
HI = lax.Precision.HIGHEST

D_MODEL = 1024
DEPTH = 2
MIX = 512
EPS = 1e-6
SGU_G, SGU_T = 4, 128
SWA_H, SWA_KV, SWA_HD, WINDOW = 8, 2, 64, 128
ROPE_THETA, ROPE_DIM = 500000.0, 16
DN_H, DN_HD, DN_CONV, DN_C = 4, 128, 4, 64
D_FF = 2816
IN_COLS = 6920
IN_PIECES = ((3848, 3072), (1792, 1536), (3328, 512), (0, 512), (512, 512), (1024, 512), (1536, 128), (1664, 128),
             (3840, 8))
IN_PAD = 120
IN_R = 7040
C_GATE, C_QKV, C_ZC, C_UA, C_VA, C_QB, C_KB, C_VB, C_SM = 0, 3072, 4608, 5120, 5632, 6144, 6656, 6784, 6912

ADAM_LR, ADAM_B1, ADAM_B2, ADAM_EPS, ADAM_WD, ADAM_STEP = 0.001, 0.9, 0.999, 1e-08, 0.01, 10
VMEM_LIMIT = 56 * 1024 * 1024


def _cparams(sem):
    return pltpu.CompilerParams(dimension_semantics=sem, vmem_limit_bytes=VMEM_LIMIT)


def _dg(a, b, ca, cb, prec=None):
    return lax.dot_general(a, b, (((ca,), (cb,)), ((), ())), precision=prec, preferred_element_type=F32)


def _split(x):
    hi = x.astype(BF16)
    return hi, (x - hi.astype(F32)).astype(BF16)


def _dg3_many(as_, bs, ca, cb):
    sa = [_split(a) for a in as_]
    sb = [_split(b) for b in bs]
    hh = [_dg(a[0], b[0], ca, cb) for a, b in zip(sa, sb)]
    hl = [_dg(a[0], b[1], ca, cb) for a, b in zip(sa, sb)]
    lh = [_dg(a[1], b[0], ca, cb) for a, b in zip(sa, sb)]
    return [x + (y + z) for x, y, z in zip(hh, hl, lh)]


def _dg_exact_lhs_many(a01, bs, ca, cb):
    a = a01.astype(BF16)
    b1 = [b.astype(BF16) for b in bs]
    r1 = [b - t.astype(F32) for b, t in zip(bs, b1)]
    b2 = [r.astype(BF16) for r in r1]
    b3 = [(r - t.astype(F32)).astype(BF16) for r, t in zip(r1, b2)]
    d1 = [_dg(a, t, ca, cb) for t in b1]
    d2 = [_dg(a, t, ca, cb) for t in b2]
    d3 = [_dg(a, t, ca, cb) for t in b3]
    return [x + (y + z) for x, y, z in zip(d1, d2, d3)]


def _mm(a, b):
    return _dg(a.astype(BF16), b.astype(BF16), 1, 0)


def _mm_nt(a, b):
    return _dg(a.astype(BF16), b.astype(BF16), 1, 1)


def _mm_tn(a, b):
    return _dg(a.astype(BF16), b.astype(BF16), 0, 0)


def _sigmoid(x):
    return 0.5 * jnp.tanh(0.5 * x) + 0.5


def _silu(x):
    return x * _sigmoid(x)


def _dsilu(x):
    s = _sigmoid(x)
    return s * (1.0 + x * (1.0 - s))


_GC = math.sqrt(2.0 / math.pi)


def _gelu(x):
    return 0.5 * x * (1.0 + jnp.tanh(_GC * (x + 0.044715 * x * x * x)))


def _dgelu(x):
    t = jnp.tanh(_GC * (x + 0.044715 * x * x * x))
    return 0.5 * (1.0 + t) + 0.5 * x * (1.0 - t * t) * _GC * (1.0 + 3.0 * 0.044715 * x * x)


def _softplus(x):
    return jnp.maximum(x, 0.0) + jnp.log(1.0 + jnp.exp(-jnp.abs(x)))


def _acc(ref, val, i):
    @pl.when(i == 0)
    def _():
        ref[...] = val

    @pl.when(i > 0)
    def _():
        ref[...] += val


def _halo_rows(dtype):
    return 8 * 4 // jnp.dtype(dtype).itemsize


def _tok_call(body, name, S, TB, tok_in, const_in=(), tok_out=(), acc_out=(), prev_in=(), next_in=(), smem_in=()):
    nb = S // TB
    in_specs, args = [], []
    for a, w, cb in tok_in:
        in_specs.append(pl.BlockSpec((TB, w), functools.partial(lambda i, cb: (i, cb), cb=cb)))
        args.append(a)
    for a, w, cb in prev_in:
        hr = _halo_rows(a.dtype)
        in_specs.append(pl.BlockSpec((hr, w), functools.partial(
            lambda i, cb, r: (jnp.maximum(i * r - 1, 0), cb), cb=cb, r=TB // hr)))
        args.append(a)
    for a, w, cb in next_in:
        hr = _halo_rows(a.dtype)
        in_specs.append(pl.BlockSpec((hr, w), functools.partial(
            lambda i, cb, r, last: (jnp.minimum((i + 1) * r, last), cb), cb=cb, r=TB // hr, last=S // hr - 1)))
        args.append(a)
    for a in const_in:
        in_specs.append(pl.BlockSpec(a.shape, lambda i: (0, 0)))
        args.append(a)
    for a in smem_in:
        in_specs.append(pl.BlockSpec(memory_space=pltpu.SMEM))
        args.append(a)
    out_specs, out_shape, aliases, shared = [], [], {}, {}
    for o, (w, dt, *dest) in enumerate(tok_out):
        if not dest:
            out_specs.append(pl.BlockSpec((TB, w), lambda i: (i, 0)))
            out_shape.append(jax.ShapeDtypeStruct((S, w), dt))
            continue
        cb, wide = dest
        out_specs.append(pl.BlockSpec((TB, w), functools.partial(lambda i, cb: (i, cb), cb=cb)))
        out_shape.append(jax.ShapeDtypeStruct((S, wide if isinstance(wide, int) else wide.shape[1]), dt))
        if not isinstance(wide, int):
            if id(wide) not in shared:
                shared[id(wide)] = len(args)
                in_specs.append(pl.BlockSpec(memory_space=pl.ANY))
                args.append(wide)
            aliases[shared[id(wide)]] = o
    for shp, dt in acc_out:
        out_specs.append(pl.BlockSpec(shp, lambda i: (0, 0)))
        out_shape.append(jax.ShapeDtypeStruct(shp, dt))
    n_extra = len(shared)

    def kern(*refs):
        n_in = len(in_specs) - n_extra
        body(pl.program_id(0), *refs[:n_in], *refs[n_in + n_extra:])

    return pl.pallas_call(
        kern, name=name, grid=(nb,), in_specs=in_specs, out_specs=out_specs, out_shape=out_shape,
        input_output_aliases=aliases, compiler_params=_cparams(("arbitrary",)),
    )(*args)


MM_BLOCKS = (1024, 1408, 640, 512, 256, 128)


def _pick(n, cands):
    for c in cands:
        if n % c == 0:
            return c
    return n


MM_VMEM_BUDGET = 44 * 1024 * 1024


def _mm_blocks(M, N, K, a_bytes, b_bytes, o_bytes, add_bytes):
    bn = _pick(N, MM_BLOCKS)
    fits = None
    for bk in [K] + [c for c in (2816, 2048) + MM_BLOCKS if c < K and K % c == 0]:
        for bm in [c for c in (2048,) + MM_BLOCKS if M % c == 0 and c >= min(M, 512)]:
            b_bufs = 1 if (bk == K and bn == N) else 2
            need = 2 * bm * bk * a_bytes + b_bufs * bk * bn * b_bytes + 2 * bm * bn * (o_bytes + add_bytes)
            need += bm * bn * 4 if bk < K else 0
            if need <= MM_VMEM_BUDGET:
                fits = fits or (bm, bn, bk)
                if (M // bm) * (N // bn) * (K // bk) >= 4:
                    return bm, bn, bk
    if fits is None:
        raise ValueError(f"no matmul blocks for {(M, N, K)}")
    return fits


def _matmul(a, b, *, ta=False, tb=False, add=None, out_dtype=F32, name):
    M, K = (a.shape[1], a.shape[0]) if ta else a.shape
    N = b.shape[0] if tb else b.shape[1]
    bm, bn, bk = _mm_blocks(M, N, K, a.dtype.itemsize, b.dtype.itemsize, jnp.dtype(out_dtype).itemsize,
                            0 if add is None else add.dtype.itemsize)
    nk = K // bk
    b_mode = dict(pipeline_mode=pl.Buffered(1)) if (bk == K and bn == N) else {}
    a_spec = pl.BlockSpec((bk, bm), lambda i, j, k: (k, i)) if ta else pl.BlockSpec((bm, bk), lambda i, j, k: (i, k))
    b_spec = (pl.BlockSpec((bn, bk), lambda i, j, k: (j, k), **b_mode) if tb
              else pl.BlockSpec((bk, bn), lambda i, j, k: (k, j), **b_mode))
    o_spec = pl.BlockSpec((bm, bn), lambda i, j, k: (i, j))
    ca, cb = (0 if ta else 1), (1 if tb else 0)

    def kern(*refs):
        a_ref, b_ref = refs[:2]
        add_ref = refs[2] if add is not None else None
        o_ref = refs[3] if add is not None else refs[2]
        p = _dg(a_ref[...].astype(BF16), b_ref[...].astype(BF16), ca, cb)

        def finish(r):
            if add is not None:
                r = r + add_ref[...].astype(F32)
            o_ref[...] = r.astype(out_dtype)

        if nk == 1:
            finish(p)
            return
        acc_ref = refs[-1]
        k = pl.program_id(2)

        @pl.when(k == 0)
        def _():
            acc_ref[...] = p

        @pl.when((k > 0) & (k < nk - 1))
        def _():
            acc_ref[...] += p

        @pl.when(k == nk - 1)
        def _():
            finish(acc_ref[...] + p)

    in_specs = [a_spec, b_spec] + ([o_spec] if add is not None else [])
    args = (a, b) + ((add,) if add is not None else ())
    return pl.pallas_call(
        kern, name=name, grid=(M // bm, N // bn, nk), in_specs=in_specs, out_specs=o_spec,
        out_shape=jax.ShapeDtypeStruct((M, N), out_dtype),
        scratch_shapes=[pltpu.VMEM((bm, bn), F32)] if nk > 1 else [],
        compiler_params=_cparams(("parallel", "parallel", "arbitrary")),
    )(*args)


def _rms_fwd(x, g, name):
    S = x.shape[0]

    def body(i, x_ref, g_ref, h_ref):
        xv = x_ref[...]
        r = lax.rsqrt(jnp.mean(xv * xv, axis=-1, keepdims=True) + EPS)
        h_ref[...] = (xv * r * g_ref[...]).astype(BF16)

    return _tok_call(body, name, S, min(S, 512), [(x, D_MODEL, 0)], [g], [(D_MODEL, BF16)])[0]


def _rms_bwd_vals(xv, g, dh):
    r = lax.rsqrt(jnp.mean(xv * xv, axis=-1, keepdims=True) + EPS)
    u = dh * g
    dx = r * u - xv * (r * r * r) * jnp.mean(u * xv, axis=-1, keepdims=True)
    dg = jnp.sum(dh * xv * r, axis=0, keepdims=True)
    return dx, dg


def _rms_bwd_add(x, g, dh, dres, name):
    S = x.shape[0]

    def body(i, x_ref, dh_ref, dr_ref, g_ref, dx_ref, dg_ref):
        dx, dg = _rms_bwd_vals(x_ref[...], g_ref[...], dh_ref[...].astype(F32))
        dx_ref[...] = dr_ref[...] + dx
        _acc(dg_ref, dg, i)

    return _tok_call(body, name, S, min(S, 512), [(x, D_MODEL, 0), (dh, D_MODEL, 0), (dres, D_MODEL, 0)], [g],
                     [(D_MODEL, F32)], [((1, D_MODEL), F32)])


def _final_loss(x, g, target):
    S = x.shape[0]

    def body(i, x_ref, t_ref, g_ref, dx_ref, loss_ref, dg_ref):
        xv, gv = x_ref[...], g_ref[...]
        r = lax.rsqrt(jnp.mean(xv * xv, axis=-1, keepdims=True) + EPS)
        e = xv * r * gv - t_ref[...]
        part = 0.5 * jnp.sum(jnp.mean(e * e, axis=-1, keepdims=True), axis=0, keepdims=True)
        dx, dg = _rms_bwd_vals(xv, gv, e * (1.0 / D_MODEL))
        dx_ref[...] = dx
        _acc(loss_ref, jnp.broadcast_to(part, (1, 128)), i)
        _acc(dg_ref, dg, i)

    return _tok_call(body, "final_loss", S, min(S, 512), [(x, D_MODEL, 0), (target, D_MODEL, 0)], [g],
                     [(D_MODEL, F32)], [((1, 128), F32), ((1, D_MODEL), F32)])


def _swiglu_fwd(gu, name):
    S = gu.shape[0]

    def body(i, gu_ref, a_ref):
        a_ref[...] = (_silu(gu_ref[:, :D_FF].astype(F32)) * gu_ref[:, D_FF:].astype(F32)).astype(BF16)

    return _tok_call(body, name, S, min(S, 256), [(gu, 2 * D_FF, 0)], [], [(D_FF, BF16)])[0]


def _swiglu_bwd(gu, dact, name):
    S = gu.shape[0]

    def body(i, gu_ref, da_ref, dgu_ref):
        gg, uu, da = gu_ref[:, :D_FF].astype(F32), gu_ref[:, D_FF:].astype(F32), da_ref[...].astype(F32)
        dgu_ref[:, :D_FF] = (da * uu * _dsilu(gg)).astype(BF16)
        dgu_ref[:, D_FF:] = (da * _silu(gg)).astype(BF16)

    return _tok_call(body, name, S, min(S, 256), [(gu, 2 * D_FF, 0), (dact, D_FF, 0)], [], [(2 * D_FF, BF16)])[0]


def _merge_fwd(proj, bds, name):
    S = proj.shape[0]

    def body(i, g0, g1, g2, b0, b1, b2, m_ref):
        m = jnp.zeros(m_ref.shape, F32)
        for gr, br in ((g0, b0), (g1, b1), (g2, b2)):
            m = m + _sigmoid(gr[...].astype(F32)) * br[...].astype(F32)
        m_ref[...] = m.astype(BF16)

    tok = [(proj, D_MODEL, n) for n in range(3)] + [(b, D_MODEL, 0) for b in bds]
    return _tok_call(body, name, S, min(S, 512), tok, [], [(D_MODEL, BF16)])[0]


def _merge_bwd(proj, bds, dm, name):
    S = proj.shape[0]

    def body(i, g0, g1, g2, b0, b1, b2, dm_ref, d0, d1, d2, dgp_ref):
        dmv = dm_ref[...]
        for n, (gr, br, dr) in enumerate(((g0, b0, d0), (g1, b1, d1), (g2, b2, d2))):
            s = _sigmoid(gr[...].astype(F32))
            dr[...] = (dmv * s).astype(BF16)
            dgp_ref[:, n * D_MODEL:(n + 1) * D_MODEL] = (dmv * br[...].astype(F32) * s * (1.0 - s)).astype(BF16)

    tok = [(proj, D_MODEL, n) for n in range(3)] + [(b, D_MODEL, 0) for b in bds] + [(dm, D_MODEL, 0)]
    return _tok_call(body, name, S, min(S, 512), tok, [],
                     [(D_MODEL, BF16)] * 3 + [(3 * D_MODEL, BF16, C_GATE // (3 * D_MODEL), IN_R)])


def _sgu_ln(v, lg, lb):
    mu = jnp.mean(v, axis=-1, keepdims=True)
    vc = v - mu
    rstd = lax.rsqrt(jnp.mean(vc * vc, axis=-1, keepdims=True) + EPS)
    vhat = vc * rstd
    return vhat, rstd, vhat * lg + lb


def _sgu_fwd(proj, lg, lb, wc, bst, name):
    S = proj.shape[0]

    def body(i, ua_ref, va_ref, lg_ref, lb_ref, wc_ref, bs_ref, o_ref):
        u = _gelu(ua_ref[...].astype(F32))
        _, _, vn = _sgu_ln(_gelu(va_ref[...].astype(F32)), lg_ref[...], lb_ref[...])
        for g in range(SGU_G):
            sl = slice(g * 128, (g + 1) * 128)
            mixed = _mm(wc_ref[sl, :], vn[:, sl]) + bs_ref[:, g:g + 1]
            o_ref[:, sl] = (u[:, sl] * mixed).astype(BF16)

    return _tok_call(body, name, S, SGU_T, [(proj, MIX, C_UA // MIX), (proj, MIX, C_VA // MIX)], [lg, lb, wc, bst],
                     [(MIX, BF16)])[0]


def _sgu_bwd(proj, lg, lb, wc, bst, dout, dproj, name):
    S = proj.shape[0]

    def body(i, ua_ref, va_ref, do_ref, lg_ref, lb_ref, wc_ref, bs_ref, duv_ref, dlg_ref, dlb_ref, dwc_ref,
             dbs_ref):
        ua, va, do = ua_ref[...].astype(F32), va_ref[...].astype(F32), do_ref[...].astype(F32)
        u = _gelu(ua)
        lgv = lg_ref[...]
        vhat, rstd, vn = _sgu_ln(_gelu(va), lgv, lb_ref[...])
        tril = lax.broadcasted_iota(jnp.int32, (128, 128), 0) >= lax.broadcasted_iota(jnp.int32, (128, 128), 1)
        lane4 = lax.broadcasted_iota(jnp.int32, (128, 4), 1)
        gs = range(SGU_G)
        sls = [slice(g * 128, (g + 1) * 128) for g in gs]
        wgs = [wc_ref[sl, :] for sl in sls]
        mixed = [_mm(wgs[g], vn[:, sls[g]]) for g in gs]
        dmix = [do[:, sl] * u[:, sl] for sl in sls]
        dwg = [_mm_nt(dmix[g], vn[:, sls[g]]) for g in gs]
        dvn = jnp.concatenate([_mm_tn(wgs[g], dmix[g]) for g in gs], axis=1)
        dbs = jnp.zeros((128, 4), F32)
        for g in gs:
            duv_ref[:, sls[g]] = (do[:, sls[g]] * (mixed[g] + bs_ref[:, g:g + 1]) * _dgelu(ua[:, sls[g]])).astype(BF16)
            dbs = dbs + jnp.where(lane4 == g, jnp.sum(dmix[g], axis=-1, keepdims=True), 0.0)
            _acc(dwc_ref.at[sls[g], :], jnp.where(tril, dwg[g], 0.0), i)
        _acc(dbs_ref, dbs, i)
        _acc(dlg_ref, jnp.sum(dvn * vhat, axis=0, keepdims=True), i)
        _acc(dlb_ref, jnp.sum(dvn, axis=0, keepdims=True), i)
        dvh = dvn * lgv
        dv = rstd * (dvh - jnp.mean(dvh, axis=-1, keepdims=True) - vhat * jnp.mean(dvh * vhat, axis=-1, keepdims=True))
        duv_ref[:, MIX:] = (dv * _dgelu(va)).astype(BF16)

    return _tok_call(body, name, S, SGU_T, [(proj, MIX, C_UA // MIX), (proj, MIX, C_VA // MIX), (dout, MIX, 0)],
                     [lg, lb, wc, bst], [(2 * MIX, BF16, C_UA // (2 * MIX), dproj)],
                     [((1, MIX), F32), ((1, MIX), F32), ((SGU_G * 128, 128), F32), ((128, 4), F32)])


def _rope_tables(positions):
    S = positions.shape[0]
    inv_freq = ROPE_THETA ** (-jnp.arange(0, ROPE_DIM, 2, dtype=F32) / ROPE_DIM)
    ang = positions.astype(F32)[:, None] * inv_freq
    c, s = jnp.cos(ang), jnp.sin(ang)
    c64 = jnp.concatenate([c, c, jnp.ones((S, SWA_HD - ROPE_DIM), F32)], axis=1)
    s64 = jnp.concatenate([-s, s, jnp.zeros((S, SWA_HD - ROPE_DIM), F32)], axis=1)
    return jnp.tile(c64, (1, 2)), jnp.tile(s64, (1, 2))


def _rope128(x, c, s):
    lane = lax.broadcasted_iota(jnp.int32, x.shape, 1) % SWA_HD
    swapped = jnp.where(lane < ROPE_DIM // 2, pltpu.roll(x, 128 - ROPE_DIM // 2, 1), pltpu.roll(x, ROPE_DIM // 2, 1))
    return x * c + swapped * s


def _rope_t128(y, c, s):
    ys = y * s
    lane = lax.broadcasted_iota(jnp.int32, y.shape, 1) % SWA_HD
    swapped = jnp.where(lane < ROPE_DIM // 2, pltpu.roll(ys, 128 - ROPE_DIM // 2, 1), pltpu.roll(ys, ROPE_DIM // 2, 1))
    return y * c + jnp.where(lane < ROPE_DIM, swapped, 0.0)


def _rope_fwd(proj, cos, sin, name):
    S = proj.shape[0]
    scale = SWA_HD ** -0.5

    def body(i, q_ref, k_ref, v_ref, c_ref, s_ref, qo_ref, ko_ref, vo_ref):
        c, s = c_ref[...], s_ref[...]
        for j in range(4):
            sl = slice(j * 128, (j + 1) * 128)
            qo_ref[:, sl] = (_rope128(q_ref[:, sl].astype(F32), c, s) * scale).astype(BF16)
        ko_ref[...] = _rope128(k_ref[...].astype(F32), c, s).astype(BF16)
        vo_ref[...] = v_ref[...].astype(BF16)

    return _tok_call(body, name, S, min(S, 512),
                     [(proj, MIX, C_QB // MIX), (proj, 128, C_KB // 128), (proj, 128, C_VB // 128), (cos, 128, 0),
                      (sin, 128, 0)], [], [(MIX, BF16), (128, BF16), (128, BF16)])


def _rope_bwd(dq, dk, dv, cos, sin, dproj, name):
    S = dq.shape[0]
    scale = SWA_HD ** -0.5
    width = C_SM - C_QB

    def body(i, dq_ref, dk_ref, dv_ref, c_ref, s_ref, o_ref):
        c, s = c_ref[...], s_ref[...]
        for j in range(4):
            sl = slice(j * 128, (j + 1) * 128)
            o_ref[:, sl] = _rope_t128(dq_ref[:, sl] * scale, c, s).astype(BF16)
        o_ref[:, C_KB - C_QB:C_VB - C_QB] = _rope_t128(dk_ref[...], c, s).astype(BF16)
        o_ref[:, C_VB - C_QB:] = dv_ref[...].astype(BF16)

    return _tok_call(body, name, S, min(S, 512),
                     [(dq, MIX, 0), (dk, 128, 0), (dv, 128, 0), (cos, 128, 0), (sin, 128, 0)], [],
                     [(width, BF16, C_QB // width, dproj)])[0]


def _swa_band(i, k_ref, v_ref):
    pstart = pl.multiple_of(jnp.maximum(i - 1, 0) * WINDOW, WINDOW)
    cstart = pl.multiple_of(i * WINDOW, WINDOW)
    kb = jnp.concatenate([k_ref[pl.ds(pstart, WINDOW), :], k_ref[pl.ds(cstart, WINDOW), :]], axis=0)
    vb = jnp.concatenate([v_ref[pl.ds(pstart, WINDOW), :], v_ref[pl.ds(cstart, WINDOW), :]], axis=0)
    qi = lax.broadcasted_iota(jnp.int32, (WINDOW, 2 * WINDOW), 0)
    sj = lax.broadcasted_iota(jnp.int32, (WINDOW, 2 * WINDOW), 1)
    mask = (sj > qi) & (sj <= qi + WINDOW) & ((i > 0) | (sj >= WINDOW))
    return kb, vb, mask, pstart, cstart


def _swa_probs(qs, kh, mask, sinks):
    logits = [jnp.where(mask, _dg(qh, kh, 1, 1), -1e30) for qh in qs]
    m = [jnp.maximum(jnp.max(l, axis=-1, keepdims=True), s) for l, s in zip(logits, sinks)]
    p = [jnp.exp(l - mm) for l, mm in zip(logits, m)]
    ps = [jnp.exp(s - mm) for s, mm in zip(sinks, m)]
    inv = [1.0 / (jnp.sum(pp, axis=-1, keepdims=True) + s) for pp, s in zip(p, ps)]
    return [pp * iv for pp, iv in zip(p, inv)], [s * iv for s, iv in zip(ps, inv)]


def _swa_fwd(q, k, v, sinks, name):
    S = q.shape[0]
    G = SWA_H // SWA_KV

    def body(i, q_ref, k_ref, v_ref, s_ref, o_ref):
        kb, vb, mask, _, _ = _swa_band(i, k_ref, v_ref)
        qv = q_ref[...]
        for kv in range(SWA_KV):
            ksl = slice(kv * SWA_HD, (kv + 1) * SWA_HD)
            heads = range(kv * G, (kv + 1) * G)
            pn, _ = _swa_probs([qv[:, h * SWA_HD:(h + 1) * SWA_HD] for h in heads], kb[:, ksl], mask,
                               [s_ref[0, h] for h in heads])
            outs = [_dg(p.astype(BF16), vb[:, ksl], 1, 0) for p in pn]
            for h, o in zip(heads, outs):
                o_ref[:, h * SWA_HD:(h + 1) * SWA_HD] = o.astype(BF16)

    return _tok_call(body, name, S, WINDOW, [(q, MIX, 0)], [k, v], [(MIX, BF16)], smem_in=[sinks])[0]


def _swa_bwd(q, k, v, sinks, dout, name):
    S = q.shape[0]

    def body(i, q_ref, do_ref, k_ref, v_ref, s_ref, dq_ref, dk_ref, dv_ref, ds_ref):
        kb, vb, mask, pstart, cstart = _swa_band(i, k_ref, v_ref)
        qv, dov = q_ref[...], do_ref[...]
        lane = lax.broadcasted_iota(jnp.int32, (1, 128), 1)
        dsink = jnp.zeros((1, 128), F32)
        dkb, dvb = [], []
        G = SWA_H // SWA_KV
        for kv in range(SWA_KV):
            ksl = slice(kv * SWA_HD, (kv + 1) * SWA_HD)
            heads = range(kv * G, (kv + 1) * G)
            qs = [qv[:, h * SWA_HD:(h + 1) * SWA_HD] for h in heads]
            dos = [dov[:, h * SWA_HD:(h + 1) * SWA_HD].astype(BF16) for h in heads]
            pn, psn = _swa_probs(qs, kb[:, ksl], mask, [s_ref[0, h] for h in heads])
            dp = [_dg(d, vb[:, ksl], 1, 1) for d in dos]
            delta = [jnp.sum(a * b, axis=-1, keepdims=True) for a, b in zip(dp, pn)]
            dsc = [(p * (a - d)).astype(BF16) for p, a, d in zip(pn, dp, delta)]
            dqs = [_dg(s, kb[:, ksl], 1, 0) for s in dsc]
            dks = [_dg(s, qh, 0, 0) for s, qh in zip(dsc, qs)]
            dvs = [_dg(p.astype(BF16), d, 0, 0) for p, d in zip(pn, dos)]
            for n_, h in enumerate(heads):
                dq_ref[:, h * SWA_HD:(h + 1) * SWA_HD] = dqs[n_]
                dsink = dsink + jnp.where(lane == h, -jnp.sum(psn[n_] * delta[n_], axis=0, keepdims=True), 0.0)
            dkb.append((dks[0] + dks[1]) + (dks[2] + dks[3]))
            dvb.append((dvs[0] + dvs[1]) + (dvs[2] + dvs[3]))
        dkb = jnp.concatenate(dkb, axis=1)
        dvb = jnp.concatenate(dvb, axis=1)

        @pl.when(i == 0)
        def _():
            dk_ref[...] = jnp.zeros_like(dk_ref)
            dv_ref[...] = jnp.zeros_like(dv_ref)

        dk_ref[pl.ds(pstart, WINDOW), :] += dkb[:WINDOW]
        dv_ref[pl.ds(pstart, WINDOW), :] += dvb[:WINDOW]
        dk_ref[pl.ds(cstart, WINDOW), :] += dkb[WINDOW:]
        dv_ref[pl.ds(cstart, WINDOW), :] += dvb[WINDOW:]
        _acc(ds_ref, dsink, i)

    return _tok_call(body, name, S, WINDOW, [(q, MIX, 0), (dout, MIX, 0)], [k, v], [(MIX, F32)],
                     [((S, 128), F32), ((S, 128), F32), ((1, 128), F32)], smem_in=[sinks])


def _shift_rows(xs, k):
    return xs if k == 0 else pltpu.roll(xs, k, 0)


def _dn_conv(x_ref, p_ref, w_ref, i):
    hr = p_ref.shape[0]
    halo = jnp.where(i > 0, p_ref[...].astype(F32), 0.0)
    xs = jnp.concatenate([halo, x_ref[...].astype(F32)], axis=0)
    sh = [_shift_rows(xs, DN_CONV - 1 - t)[hr:] for t in range(DN_CONV)]
    pre = sh[0] * w_ref[0:1, :]
    for t in range(1, DN_CONV):
        pre = pre + sh[t] * w_ref[t:t + 1, :]
    return pre, sh


def _dn_gates(sm, alog, dtb):
    lane = lax.broadcasted_iota(jnp.int32, sm.shape, 1)
    return jnp.where(lane < DN_H, _sigmoid(sm), -jnp.exp(alog) * _softplus(sm + dtb))


def _dn_pre_fwd(proj, conv_w, alog_l, dtb_l, name):
    S = proj.shape[0]
    scale = DN_HD ** -0.5

    def body(i, x_ref, sm_ref, p_ref, w_ref, al_ref, db_ref, q_ref, k_ref, v_ref, bg_ref):
        pre, _ = _dn_conv(x_ref, p_ref, w_ref, i)
        a = _silu(pre)
        for h in range(DN_H):
            sl = slice(h * DN_HD, (h + 1) * DN_HD)
            qh, kh = a[:, sl], a[:, MIX + h * DN_HD:MIX + (h + 1) * DN_HD]
            q_ref[:, sl] = qh * (lax.rsqrt(jnp.sum(qh * qh, axis=-1, keepdims=True) + EPS) * scale)
            k_ref[:, sl] = kh * lax.rsqrt(jnp.sum(kh * kh, axis=-1, keepdims=True) + EPS)
        v_ref[...] = a[:, 2 * MIX:]
        bg_ref[...] = _dn_gates(sm_ref[...].astype(F32), al_ref[...], db_ref[...])

    TB = min(S, 256)
    return _tok_call(body, name, S, TB, [(proj, 3 * MIX, C_QKV // (3 * MIX)), (proj, 128, C_SM // 128)],
                     [conv_w, alog_l, dtb_l], [(MIX, F32), (MIX, F32), (MIX, F32), (128, F32)],
                     prev_in=[(proj, 3 * MIX, C_QKV // (3 * MIX))])


def _dn_pre_bwd1(proj, conv_w, alog_l, dtb_l, dq, dk, dv, dbg, dproj, name):
    S = proj.shape[0]
    scale = DN_HD ** -0.5

    def body(i, x_ref, sm_ref, dq_ref, dk_ref, dv_ref, dbg_ref, p_ref, w_ref, al_ref, db_ref, dpre_ref, dsm_ref,
             dw_ref, dal_ref, ddb_ref):
        pre, sh = _dn_conv(x_ref, p_ref, w_ref, i)
        a = _silu(pre)
        da_parts = []
        for part, (g_ref, sc) in enumerate(((dq_ref, scale), (dk_ref, 1.0))):
            for h in range(DN_H):
                xh = a[:, part * MIX + h * DN_HD:part * MIX + (h + 1) * DN_HD]
                rs = lax.rsqrt(jnp.sum(xh * xh, axis=-1, keepdims=True) + EPS)
                y = xh * rs
                dy = g_ref[:, h * DN_HD:(h + 1) * DN_HD] * sc
                da_parts.append(rs * (dy - y * jnp.sum(dy * y, axis=-1, keepdims=True)))
        da_parts.append(dv_ref[...])
        dpre = jnp.concatenate(da_parts, axis=1) * _dsilu(pre)
        dpre_ref[...] = dpre
        dw = jnp.concatenate([jnp.sum(dpre * sh[t], axis=0, keepdims=True) for t in range(DN_CONV)], axis=0)
        _acc(dw_ref, dw, i)
        sm, al, db, dbg_v = sm_ref[...].astype(F32), al_ref[...], db_ref[...], dbg_ref[...]
        lane = lax.broadcasted_iota(jnp.int32, sm.shape, 1)
        sg = _sigmoid(sm)
        gneg = -jnp.exp(al)
        is_g = (lane >= DN_H) & (lane < 2 * DN_H)
        d_al = jnp.where(is_g, dbg_v * gneg * _sigmoid(sm + db), 0.0)
        dsm_ref[...] = jnp.where(lane < DN_H, dbg_v * sg * (1.0 - sg), d_al).astype(BF16)
        _acc(ddb_ref, jnp.sum(d_al, axis=0, keepdims=True), i)
        _acc(dal_ref, jnp.sum(jnp.where(is_g, dbg_v * gneg * _softplus(sm + db), 0.0), axis=0, keepdims=True), i)

    TB = min(S, 256)
    return _tok_call(body, name, S, TB,
                     [(proj, 3 * MIX, C_QKV // (3 * MIX)), (proj, 128, C_SM // 128), (dq, MIX, 0), (dk, MIX, 0),
                      (dv, MIX, 0), (dbg, 128, 0)], [conv_w, alog_l, dtb_l],
                     [(3 * MIX, F32), (128, BF16, C_SM // 128, dproj)],
                     [((DN_CONV, 3 * MIX), F32), ((1, 128), F32), ((1, 128), F32)],
                     prev_in=[(proj, 3 * MIX, C_QKV // (3 * MIX))])


def _dn_pre_bwd2(dpre, conv_w, dproj, name):
    S = dpre.shape[0]
    TB = min(S, 256)
    nb = S // TB

    def body(i, d_ref, n_ref, w_ref, o_ref):
        halo = jnp.where(i < nb - 1, n_ref[...], 0.0)
        ds = jnp.concatenate([d_ref[...], halo], axis=0)
        out = ds[:TB] * w_ref[DN_CONV - 1:DN_CONV, :]
        for t in range(DN_CONV - 1):
            k = DN_CONV - 1 - t
            out = out + pltpu.roll(ds, TB + 8 - k, 0)[:TB] * w_ref[t:t + 1, :]
        o_ref[...] = out.astype(BF16)

    return _tok_call(body, name, S, TB, [(dpre, 3 * MIX, 0)], [conv_w],
                     [(3 * MIX, BF16, C_QKV // (3 * MIX), dproj)], next_in=[(dpre, 3 * MIX, 0)])[0]


def _dn_decay_terms(bgs, heads):
    C = DN_C
    ri = lax.broadcasted_iota(jnp.int32, (C, C), 0)
    ci = lax.broadcasted_iota(jnp.int32, (C, C), 1)
    tril, eye = ri >= ci, ri == ci
    beta = [b[:, h:h + 1] for b, h in zip(bgs, heads)]
    gcol = _dg_exact_lhs_many(tril, [jnp.broadcast_to(b[:, DN_H + h:DN_H + h + 1], (C, C))
                                     for b, h in zip(bgs, heads)], 1, 0)
    grow = [jnp.sum(jnp.where(eye, g, 0.0), axis=0, keepdims=True) for g in gcol]
    decay = [jnp.exp(jnp.where(tril, g - r, -1e30)) for g, r in zip(gcol, grow)]
    e_gc = [jnp.exp(g[:, 0:1]) for g in gcol]
    e_kd = [jnp.exp(g[C - 1:C, 0:1] - g[:, 0:1]) for g in gcol]
    cdec = [jnp.exp(g[C - 1:C, 0:1]) for g in gcol]
    return beta, decay, e_gc, e_kd, cdec


def _dn_nb(S):
    return 4 if S % (4 * DN_C) == 0 else 1


def _dn_prep_fwd(q, k, v, bg, name):
    S = q.shape[0]
    C, NB = DN_C, _dn_nb(S)
    TB = NB * C

    def kern(q_ref, k_ref, v_ref, bg_ref, t_ref, uw_ref, at_ref, qd_ref, kd_ref, dec_ref):
        lane = lax.broadcasted_iota(jnp.int32, (C, 128), 1)
        ri = lax.broadcasted_iota(jnp.int32, (C, C), 0)
        ci = lax.broadcasted_iota(jnp.int32, (C, C), 1)
        tril, eye = ri >= ci, ri == ci
        chains = [(cb, h) for cb in range(NB) for h in range(DN_H)]
        rows = lambda cb: slice(cb * C, (cb + 1) * C)
        head = lambda h: slice(h * DN_HD, (h + 1) * DN_HD)
        beta, decay, e_gc, e_kd, cdec = _dn_decay_terms([bg_ref[rows(cb), :] for cb, _ in chains],
                                                        [h for _, h in chains])
        qs = [q_ref[rows(cb), head(h)] for cb, h in chains]
        ks = [k_ref[rows(cb), head(h)] for cb, h in chains]
        kb = [kh * b for kh, b in zip(ks, beta)]
        x = [-jnp.where(ri > ci, _mm_nt(a, kh) * d, 0.0) for a, kh, d in zip(kb, ks, decay)]
        tm = [jnp.where(eye, 1.0, 0.0) + xi for xi in x]
        p = x
        p = _dg3_many(p, p, 1, 0)
        for it in range(5):
            if it == 4:
                tm = [t + tp for t, tp in zip(tm, _dg3_many(tm, p, 1, 0))]
                break
            both = _dg3_many([jnp.concatenate([t, pp], axis=0) for t, pp in zip(tm, p)], p, 1, 0)
            tm = [t + b[:C] for t, b in zip(tm, both)]
            p = [b[C:] for b in both]
        rhs = [jnp.concatenate([v_ref[rows(cb), head(h)] * b, a * e], axis=1)
               for (cb, h), b, a, e in zip(chains, beta, kb, e_gc)]
        sol = _dg3_many(tm, rhs, 1, 0)
        attn = [_mm_nt(qh, kh) * d for qh, kh, d in zip(qs, ks, decay)]
        for n_, (cb, h) in enumerate(chains):
            rs, sl, hc = rows(cb), head(h), slice(h * C, (h + 1) * C)
            t_ref[rs, hc] = tm[n_]
            uw_ref[rs, sl] = sol[n_][:, :DN_HD]
            uw_ref[rs, MIX + h * DN_HD:MIX + (h + 1) * DN_HD] = sol[n_][:, DN_HD:]
            at_ref[rs, hc] = attn[n_]
            qd_ref[rs, sl] = (qs[n_] * e_gc[n_]).astype(BF16)
            kd_ref[rs, sl] = (ks[n_] * e_kd[n_]).astype(BF16)
        for cb in range(NB):
            dec = jnp.zeros((C, 128), F32)
            for h in range(DN_H):
                dec = dec + jnp.where(lane == h, cdec[cb * DN_H + h], 0.0)
            dec_ref[rows(cb), :] = dec

    tok = lambda w: pl.BlockSpec((TB, w), lambda i: (i, 0))
    return pl.pallas_call(
        kern, name=name, grid=(S // TB,), in_specs=[tok(MIX), tok(MIX), tok(MIX), tok(128)],
        out_specs=[tok(DN_H * C), tok(2 * MIX), tok(DN_H * C), tok(MIX), tok(MIX), tok(128)],
        out_shape=[jax.ShapeDtypeStruct((S, DN_H * C), F32), jax.ShapeDtypeStruct((S, 2 * MIX), F32),
                   jax.ShapeDtypeStruct((S, DN_H * C), F32), jax.ShapeDtypeStruct((S, MIX), BF16),
                   jax.ShapeDtypeStruct((S, MIX), BF16), jax.ShapeDtypeStruct((S, 128), F32)],
        compiler_params=_cparams(("parallel",)),
    )(q, k, v, bg)


def _dn_scan_fwd(uw, at, qd, kd, dec, name):
    S = uw.shape[0]
    C, NB = DN_C, _dn_nb(S)
    TB = NB * C
    SR = DN_H * DN_HD

    def kern(uw_ref, at_ref, qd_ref, kd_ref, dec_ref, o_ref, vn_ref, st_ref, state):
        @pl.when(pl.program_id(0) == 0)
        def _():
            state[...] = jnp.zeros_like(state)

        for cb in range(NB):
            rs = slice(cb * C, (cb + 1) * C)
            hs = range(DN_H)
            sls = [slice(h * DN_HD, (h + 1) * DN_HD) for h in hs]
            s_in = [state[sl, :] for sl in sls]
            ws = [_mm(uw_ref[rs, MIX + h * DN_HD:MIX + (h + 1) * DN_HD], s_in[h]) for h in hs]
            os_ = [_mm(qd_ref[rs, sls[h]], s_in[h]) for h in hs]
            vnew = [uw_ref[rs, sls[h]] - ws[h] for h in hs]
            oa = [_mm(at_ref[rs, h * C:(h + 1) * C], vnew[h]) for h in hs]
            kv = [_mm_tn(kd_ref[rs, sls[h]], vnew[h]) for h in hs]
            for h in hs:
                o_ref[rs, sls[h]] = os_[h] + oa[h]
                state[sls[h], :] = s_in[h] * dec_ref[cb * C:cb * C + 1, h:h + 1] + kv[h]
                st_ref[cb * SR + h * DN_HD:cb * SR + (h + 1) * DN_HD, :] = s_in[h].astype(BF16)
                vn_ref[rs, sls[h]] = vnew[h]

    tok = lambda w: pl.BlockSpec((TB, w), lambda i: (i, 0))
    return pl.pallas_call(
        kern, name=name, grid=(S // TB,), in_specs=[tok(2 * MIX), tok(DN_H * C), tok(MIX), tok(MIX), tok(128)],
        out_specs=[tok(MIX), tok(MIX), pl.BlockSpec((NB * SR, DN_HD), lambda i: (i, 0))],
        out_shape=[jax.ShapeDtypeStruct((S, MIX), F32), jax.ShapeDtypeStruct((S, MIX), F32),
                   jax.ShapeDtypeStruct((S // C * SR, DN_HD), BF16)],
        scratch_shapes=[pltpu.VMEM((SR, DN_HD), F32)],
        compiler_params=_cparams(("arbitrary",)),
    )(uw, at, qd, kd, dec)


def _dn_core_fwd(q, k, v, bg, name):
    tm, uw, at, qd, kd, dec = _dn_prep_fwd(q, k, v, bg, name + "_prep")
    o, vn, st = _dn_scan_fwd(uw, at, qd, kd, dec, name + "_scan")
    return o, dict(tm=tm, uw=uw, at=at, qd=qd, kd=kd, dec=dec, vn=vn, st=st)


def _dn_scan_bwd(sv, do, name):
    S = do.shape[0]
    C, NB = DN_C, _dn_nb(S)
    TB = NB * C
    SR = DN_H * DN_HD
    nb = S // TB

    def kern(do_ref, uw_ref, at_ref, qd_ref, kd_ref, dec_ref, vn_ref, st_ref, dvn_ref, dw_ref, dkd_ref, dc_ref, dstate):
        @pl.when(pl.program_id(0) == 0)
        def _():
            dstate[...] = jnp.zeros_like(dstate)

        lane = lax.broadcasted_iota(jnp.int32, (C, 128), 1)
        for cb in reversed(range(NB)):
            rs = slice(cb * C, (cb + 1) * C)
            dcrow = jnp.zeros((C, 128), F32)
            for h in range(DN_H):
                sl = slice(h * DN_HD, (h + 1) * DN_HD)
                doh, ds_o = do_ref[rs, sl], dstate[sl, :]
                s_in = st_ref[cb * SR + h * DN_HD:cb * SR + (h + 1) * DN_HD, :]
                d_vnew = _mm_tn(at_ref[rs, h * C:(h + 1) * C], doh) + _mm(kd_ref[rs, sl], ds_o)
                dvn_ref[rs, sl] = d_vnew
                dw_ref[rs, sl] = -_mm_nt(d_vnew, s_in)
                dkd_ref[rs, sl] = _mm_nt(vn_ref[rs, sl], ds_o)
                d_c = jnp.sum(jnp.sum(ds_o * s_in.astype(F32), axis=1, keepdims=True), axis=0, keepdims=True)
                dcrow = dcrow + jnp.where(lane == h, d_c, 0.0)
                dstate[sl, :] = (ds_o * dec_ref[cb * C:cb * C + 1, h:h + 1] + _mm_tn(qd_ref[rs, sl], doh)
                                 - _mm_tn(uw_ref[rs, MIX + h * DN_HD:MIX + (h + 1) * DN_HD], d_vnew))
            dc_ref[rs, :] = dcrow

    tok = lambda w: pl.BlockSpec((TB, w), lambda i: (nb - 1 - i, 0))
    return pl.pallas_call(
        kern, name=name, grid=(nb,),
        in_specs=[tok(MIX), tok(2 * MIX), tok(DN_H * C), tok(MIX), tok(MIX), tok(128), tok(MIX),
                  pl.BlockSpec((NB * SR, DN_HD), lambda i: (nb - 1 - i, 0))],
        out_specs=[tok(MIX), tok(MIX), tok(MIX), tok(128)],
        out_shape=[jax.ShapeDtypeStruct((S, MIX), F32)] * 3 + [jax.ShapeDtypeStruct((S, 128), F32)],
        scratch_shapes=[pltpu.VMEM((SR, DN_HD), F32)],
        compiler_params=_cparams(("arbitrary",)),
    )(do, sv["uw"], sv["at"], sv["qd"], sv["kd"], sv["dec"], sv["vn"], sv["st"])


def _dn_chunk_bwd(q, k, v, bg, sv, do, dvn, dw, dkd, dc, name):
    S = q.shape[0]
    C, NB = DN_C, _dn_nb(S)
    TB = NB * C
    SR = DN_H * DN_HD

    def kern(q_ref, k_ref, v_ref, bg_ref, t_ref, uw_ref, vn_ref, st_ref, do_ref, dvn_ref, dw_ref, dkd_ref, dc_ref,
             dq_ref, dk_ref, dv_ref, dbg_ref):
        lane = lax.broadcasted_iota(jnp.int32, (C, 128), 1)
        ri = lax.broadcasted_iota(jnp.int32, (C, C), 0)
        ci = lax.broadcasted_iota(jnp.int32, (C, C), 1)
        tril, eye, last = ri >= ci, ri == ci, ri[:, 0:1] == C - 1
        chains = [(cb, h) for cb in range(NB) for h in range(DN_H)]
        each = lambda f, *ls: [f(*a) for a in zip(*ls)]
        rsum = lambda t: jnp.sum(t, axis=-1, keepdims=True)
        rows = lambda cb: slice(cb * C, (cb + 1) * C)
        head = lambda h: slice(h * DN_HD, (h + 1) * DN_HD)
        tok = lambda ref: [ref[rows(cb), head(h)] for cb, h in chains]
        beta, decay, e_gc, e_kd, cdec = _dn_decay_terms([bg_ref[rows(cb), :] for cb, _ in chains],
                                                        [h for _, h in chains])
        qs, ks, vs, dos, vnew, d_kd = tok(q_ref), tok(k_ref), tok(v_ref), tok(do_ref), tok(vn_ref), tok(dkd_ref)
        s_in = [st_ref[cb * SR + h * DN_HD:cb * SR + (h + 1) * DN_HD, :] for cb, h in chains]
        d_c = [dc_ref[cb * C:cb * C + 1, h:h + 1] for cb, h in chains]
        kb = each(lambda a, b: a * b, ks, beta)
        kk = each(_mm_nt, kb, ks)
        attn = each(lambda a, b, d: _mm_nt(a, b) * d, qs, ks, decay)
        d_qd = each(_mm_nt, dos, s_in)
        d_attn = each(_mm_nt, dos, vnew)
        d_sol = [jnp.concatenate([dvn_ref[rows(cb), head(h)], dw_ref[rows(cb), head(h)]], axis=1) for cb, h in chains]
        sol = [jnp.concatenate([uw_ref[rows(cb), head(h)], uw_ref[rows(cb), MIX + h * DN_HD:MIX + (h + 1) * DN_HD]],
                               axis=1) for cb, h in chains]
        d_rhs = _dg3_many([t_ref[rows(cb), h * C:(h + 1) * C] for cb, h in chains], d_sol, 0, 0)
        d_a = _dg3_many(d_rhs, sol, 1, 1)
        d_kk = each(lambda a, d: jnp.where(ri > ci, -a, 0.0) * d, d_a, decay)
        d_qk = each(lambda a, d: a * d, d_attn, decay)
        dm = each(lambda a, b, c_, d: a * b + c_ * d, d_kk, kk, d_attn, attn)
        d_vb = [t[:, :DN_HD] for t in d_rhs]
        dz = [t[:, DN_HD:] for t in d_rhs]
        d_kb = each(lambda z, e, a, kh: z * e + _mm(a, kh), dz, e_gc, d_kk, ks)
        d_k = each(lambda a, b, c_, q: _mm_tn(a, b) + _mm_tn(c_, q), d_kk, kb, d_qk, qs)
        d_q = each(lambda a, kh, b, e: _mm(a, kh) + b * e, d_qk, ks, d_qd, e_gc)
        t_kd = each(lambda a, kh, e: rsum(a * kh * e), d_kd, ks, e_kd)
        d_gl = each(lambda t, c_, cd: jnp.sum(t, axis=0, keepdims=True) + c_ * cd, t_kd, d_c, cdec)
        d_gc = each(lambda z, a, e, m, b, q, t, gl:
                    rsum(z * a) * e + rsum(m) - rsum(jnp.where(eye, jnp.sum(m, axis=0, keepdims=True), 0.0))
                    + rsum(b * q) * e - t + jnp.where(last, gl, 0.0),
                    dz, kb, e_gc, dm, d_qd, qs, t_kd, d_gl)
        d_g = _dg_exact_lhs_many(ri <= ci, [jnp.broadcast_to(t, (C, 128)) for t in d_gc], 1, 0)
        d_beta = each(lambda a, v_, b, kh: rsum(a * v_) + rsum(b * kh), d_vb, vs, d_kb, ks)
        for n_, (cb, h) in enumerate(chains):
            dq_ref[rows(cb), head(h)] = d_q[n_]
            dk_ref[rows(cb), head(h)] = d_k[n_] + d_kd[n_] * e_kd[n_] + d_kb[n_] * beta[n_]
            dv_ref[rows(cb), head(h)] = d_vb[n_] * beta[n_]
        for cb in range(NB):
            dbg = jnp.zeros((C, 128), F32)
            for h in range(DN_H):
                n_ = cb * DN_H + h
                dbg = dbg + jnp.where(lane == h, d_beta[n_], 0.0) + jnp.where(lane == DN_H + h, d_g[n_], 0.0)
            dbg_ref[rows(cb), :] = dbg

    tok = lambda w: pl.BlockSpec((TB, w), lambda i: (i, 0))
    return pl.pallas_call(
        kern, name=name, grid=(S // TB,),
        in_specs=[tok(MIX), tok(MIX), tok(MIX), tok(128), tok(DN_H * C), tok(2 * MIX), tok(MIX),
                  pl.BlockSpec((NB * SR, DN_HD), lambda i: (i, 0)), tok(MIX), tok(MIX), tok(MIX), tok(MIX), tok(128)],
        out_specs=[tok(MIX), tok(MIX), tok(MIX), tok(128)],
        out_shape=[jax.ShapeDtypeStruct((S, MIX), F32)] * 3 + [jax.ShapeDtypeStruct((S, 128), F32)],
        compiler_params=_cparams(("parallel",)),
    )(q, k, v, bg, sv["tm"], sv["uw"], sv["vn"], sv["st"], do, dvn, dw, dkd, dc)


def _dn_core_bwd(q, k, v, bg, sv, do, name):
    dvn, dw, dkd, dc = _dn_scan_bwd(sv, do, name + "_scan")
    return _dn_chunk_bwd(q, k, v, bg, sv, do, dvn, dw, dkd, dc, name + "_chunk")


def _dn_post_fwd(o, proj, ng, name):
    S = o.shape[0]

    def body(i, o_ref, z_ref, g_ref, out_ref):
        gv = g_ref[...]
        for h in range(DN_H):
            sl = slice(h * DN_HD, (h + 1) * DN_HD)
            oh = o_ref[:, sl]
            r = lax.rsqrt(jnp.mean(oh * oh, axis=-1, keepdims=True) + EPS)
            out_ref[:, sl] = (oh * r * gv * _silu(z_ref[:, sl].astype(F32))).astype(BF16)

    return _tok_call(body, name, S, min(S, 512), [(o, MIX, 0), (proj, MIX, C_ZC // MIX)], [ng], [(MIX, BF16)])[0]


def _dn_post_bwd(o, proj, ng, dout, dproj, name):
    S = o.shape[0]

    def body(i, o_ref, z_ref, do_ref, g_ref, dov_ref, dz_ref, dg_ref):
        gv = g_ref[...]
        dg = jnp.zeros((1, DN_HD), F32)
        for h in range(DN_H):
            sl = slice(h * DN_HD, (h + 1) * DN_HD)
            oh, zh, dh = o_ref[:, sl], z_ref[:, sl].astype(F32), do_ref[:, sl].astype(F32)
            r = lax.rsqrt(jnp.mean(oh * oh, axis=-1, keepdims=True) + EPS)
            dz_ref[:, sl] = (dh * oh * r * gv * _dsilu(zh)).astype(BF16)
            dx, dgh = _rms_bwd_vals(oh, gv, dh * _silu(zh))
            dov_ref[:, sl] = dx
            dg = dg + dgh
        _acc(dg_ref, dg, i)

    return _tok_call(body, name, S, min(S, 512), [(o, MIX, 0), (proj, MIX, C_ZC // MIX), (dout, MIX, 0)], [ng],
                     [(MIX, F32), (MIX, BF16, C_ZC // MIX, dproj)], [((1, DN_HD), F32)])


def _layer_params(w, big, l):
    lane = jnp.arange(128)
    is_g = (lane >= DN_H) & (lane < 2 * DN_H)
    spread = lambda t: jnp.where(is_g, jnp.tile(t, 128 // DN_H), 0.0).reshape(1, 128)
    tril = jnp.tril(jnp.ones((SGU_T, SGU_T), bool))
    return dict(
        win=big["w_in"], rest=big["rest"], conv=w["dn_conv_w"][l], attn_norm=w["attn_norm"][l].reshape(1, -1), ffn_norm=w["ffn_norm"][l].reshape(1, -1),
        lg=w["sgu_ln_g"][l].reshape(1, -1), lb=w["sgu_ln_b"][l].reshape(1, -1),
        wc=jnp.where(tril, w["sgu_w"][l], 0.0).reshape(SGU_G * SGU_T, SGU_T), bst=w["sgu_b"][l].T,
        sinks=w["attn_sinks"][l].reshape(1, -1), alog=spread(w["dn_a_log"][l]), dtb=spread(w["dn_dt_bias"][l]),
        ng=w["dn_norm"][l].reshape(1, -1))


def _layer_fwd(x, p, cos, sin, l):
    n = lambda s: f"l{l}_{s}"
    h = _rms_fwd(x, p["attn_norm"], n("rms1"))
    if callable(p["win"]):
        p["win"] = p["win"](h)
    proj = _matmul(h, p["win"], out_dtype=BF16, name=n("mm_in"))
    out_a = _sgu_fwd(proj, p["lg"], p["lb"], p["wc"], p["bst"], n("sgu_fwd"))
    qr, kr, vr = _rope_fwd(proj, cos, sin, n("rope_fwd"))
    out_b = _swa_fwd(qr, kr, vr, p["sinks"], n("swa_fwd"))
    q, k, v, bg = _dn_pre_fwd(proj, p["conv"], p["alog"], p["dtb"], n("dn_pre_fwd"))
    o, dn = _dn_core_fwd(q, k, v, bg, n("dn_core_fwd"))
    out_c = _dn_post_fwd(o, proj, p["ng"], n("dn_post_fwd"))
    outs = (out_a, out_b, out_c)
    rest = p.pop("rest")(out_c)
    p.update(wb=rest["w_branch"], wout=rest["w_out"], wgu=rest["w_gate_up"], wdown=rest["w_down"])
    bds = [_matmul(outs[j], p["wb"][j], out_dtype=BF16, name=n(f"mm_branch{j}")) for j in range(3)]
    merged = _merge_fwd(proj, bds, n("merge_fwd"))
    x1 = _matmul(merged, p["wout"], add=x, name=n("mm_out"))
    h2 = _rms_fwd(x1, p["ffn_norm"], n("rms2"))
    gu = _matmul(h2, p["wgu"], out_dtype=BF16, name=n("mm_gu"))
    act = _swiglu_fwd(gu, n("swiglu_fwd"))
    x2 = _matmul(act, p["wdown"], add=x1, name=n("mm_down"))
    saved = dict(x=x, h=h, proj=proj, outs=outs, qr=qr, kr=kr, vr=vr, q=q, k=k, v=v, bg=bg, o=o, dn=dn, bds=bds,
                 merged=merged, x1=x1, h2=h2, gu=gu, act=act)
    return x2, saved


def _layer_bwd(dx2, s, p, cos, sin, l, early=None):
    n = lambda t: f"l{l}_{t}"
    proj = s["proj"]
    g = {}
    g["w_down"] = _matmul(s["act"], dx2, ta=True, out_dtype=BF16, name=n("wg_down"))
    dact = _matmul(dx2, p["wdown"], tb=True, out_dtype=BF16, name=n("dg_down"))
    dgu = _swiglu_bwd(s["gu"], dact, n("swiglu_bwd"))
    g["w_gate_up"] = _matmul(s["h2"], dgu, ta=True, out_dtype=BF16, name=n("wg_gu"))
    dh2 = _matmul(dgu, p["wgu"], tb=True, name=n("dg_gu"))
    dx1, g["ffn_norm"] = _rms_bwd_add(s["x1"], p["ffn_norm"], dh2, dx2, n("rms2_bwd"))
    g["w_out"] = _matmul(s["merged"], dx1, ta=True, out_dtype=BF16, name=n("wg_out"))
    dm = _matmul(dx1, p["wout"], tb=True, name=n("dg_out"))
    dbd0, dbd1, dbd2, dproj = _merge_bwd(proj, s["bds"], dm, n("merge_bwd"))
    dbds = (dbd0, dbd1, dbd2)
    g["w_branch"] = jnp.stack([_matmul(s["outs"][j], dbds[j], ta=True, out_dtype=BF16, name=n(f"wg_branch{j}"))
                               for j in range(3)])
    douts = [_matmul(dbds[j], p["wb"][j], tb=True, name=n(f"dg_branch{j}")) for j in range(3)]
    lg = p["lg"]
    if early is not None:
        token = early({k: g.pop(k) for k in ("w_down", "w_gate_up", "w_out", "w_branch")})
        lg = lg if token is None else lg + token[0, 0]
    dproj, g["sgu_ln_g"], g["sgu_ln_b"], dwc, dbs = _sgu_bwd(proj, lg, p["lb"], p["wc"], p["bst"], douts[0], dproj,
                                                             n("sgu_bwd"))
    g["sgu_w"] = dwc.reshape(SGU_G, SGU_T, SGU_T)
    g["sgu_b"] = dbs.T
    dqr, dkr, dvr, dsink = _swa_bwd(s["qr"], s["kr"], s["vr"], p["sinks"], douts[1], n("swa_bwd"))
    g["attn_sinks"] = dsink[0, :SWA_H]
    dproj = _rope_bwd(dqr, dkr, dvr, cos, sin, dproj, n("rope_bwd"))
    do, dproj, dng = _dn_post_bwd(s["o"], proj, p["ng"], douts[2], dproj, n("dn_post_bwd"))
    g["dn_norm"] = dng[0]
    dq, dk, dv, dbg = _dn_core_bwd(s["q"], s["k"], s["v"], s["bg"], s["dn"], do, n("dn_core_bwd"))
    dpre, dproj, g["dn_conv_w"], dal, ddb = _dn_pre_bwd1(proj, p["conv"], p["alog"], p["dtb"], dq, dk, dv, dbg, dproj,
                                                         n("dn_pre_bwd1"))
    g["dn_a_log"] = dal[0, DN_H:2 * DN_H]
    g["dn_dt_bias"] = ddb[0, DN_H:2 * DN_H]
    dproj = _dn_pre_bwd2(dpre, p["conv"], dproj, n("dn_pre_bwd2"))
    g["w_in"] = _matmul(s["h"], dproj, ta=True, out_dtype=BF16, name=n("wg_in"))
    attn_norm = p["attn_norm"]
    if early is not None:
        token = early({"w_in": g.pop("w_in")})
        attn_norm = attn_norm if token is None else attn_norm + token[0, 0]
    dh = _matmul(dproj, p["win"], tb=True, name=n("dg_in"))
    dx, g["attn_norm"] = _rms_bwd_add(s["x"], attn_norm, dh, dx1, n("rms1_bwd"))
    g["attn_norm"], g["ffn_norm"] = g["attn_norm"][0], g["ffn_norm"][0]
    g["sgu_ln_g"], g["sgu_ln_b"] = g["sgu_ln_g"][0], g["sgu_ln_b"][0]
    return dx, g


def _local_step(x, positions, target, w, big_of_layer, on_grads):
    cos, sin = _rope_tables(positions)
    params, saves, xs = [], [], x
    for l in range(DEPTH):
        params.append(_layer_params(w, big_of_layer(l, xs), l))
        xs, sv = _layer_fwd(xs, params[l], cos, sin, l)
        saves.append(sv)
    dx, loss_row, dgf = _final_loss(xs, w["final_norm"].reshape(1, -1), target)
    grads = [None] * DEPTH
    for l in reversed(range(DEPTH)):
        early = functools.partial(on_grads, l) if l == 0 else None
        dx, grads[l] = _layer_bwd(dx, saves[l], params[l], cos, sin, l, early)
        left = {k: grads[l].pop(k) for k in BIG if k in grads[l]}
        token = on_grads(l, left) if left else None
        if token is not None and l > 0:
            params[l - 1] = dict(params[l - 1], ffn_norm=params[l - 1]["ffn_norm"] + token[0, 0])
    stacked = {k: jnp.stack([grads[l][k] for l in range(DEPTH)]) for k in grads[0]}
    stacked["final_norm"] = dgf[0]
    return loss_row[0, 0], dx, stacked


MESH = pl.DeviceIdType.MESH
HBM_SPEC = pl.BlockSpec(memory_space=pltpu.HBM)
VMEM_SPEC = pl.BlockSpec(memory_space=pltpu.VMEM)
N_CHIPS = 4
FLIPS = tuple((fx, fy, fc) for fx in (0, 1) for fy in (0, 1) for fc in (0, 1))[1:]
BIG = ("w_in", "w_branch", "w_out", "w_gate_up", "w_down")
BIG_SPEC = {
    "w_in": dict(rows=1024, cols=1792, axis=1, keep=1730, down=8),
    "w_branch": dict(rows=1536, cols=256, axis=1, keep=256, down=2),
    "w_out": dict(rows=256, cols=1024, axis=0, keep=1024, down=1),
    "w_gate_up": dict(rows=1024, cols=1408, axis=1, keep=1408, down=8),
    "w_down": dict(rows=704, cols=1024, axis=0, keep=1024, down=4),
}
CONV_ROWS, CONV_COLS = DEPTH * DN_CONV, 3 * MIX // N_CHIPS


def _full_shape(k):
    sp = BIG_SPEC[k]
    return (sp["rows"], N_CHIPS * sp["cols"]) if sp["axis"] == 1 else (N_CHIPS * sp["rows"], sp["cols"])


def _me():
    return lax.axis_index("x"), lax.axis_index("y"), lax.axis_index("c")


def _peer(x, y, c, flip):
    fx, fy, fc = flip
    return (1 - x if fx else x, 1 - y if fy else y, 1 - c if fc else c)


class _Copies:
    def __init__(self, send_sems, recv_sems):
        self.send_sems, self.recv_sems, self.k, self.sent, self.landing = send_sems, recv_sems, 0, [], []

    def _copy(self, k, src, dst, to):
        return pltpu.make_async_remote_copy(src_ref=src, dst_ref=dst, send_sem=self.send_sems.at[k],
                                            recv_sem=self.recv_sems.at[k], device_id=to, device_id_type=MESH)

    def send(self, src, dst, to, lands):
        k = self.k
        self.k += 1
        cp = self._copy(k, src, dst, to)
        cp.start()
        self.sent.append(cp)
        self.landing.append(self._copy(k, lands, lands, to))
        return k

    def wait_landed(self, k):
        self.landing[k].wait_recv()

    def finish(self, landed=()):
        for k, cp in enumerate(self.landing):
            if k not in landed:
                cp.wait_recv()
        for cp in self.sent:
            cp.wait_send()


def _place_shard(shard, k, chip, layer, name):
    sp = BIG_SPEC[k]
    rows, cols, keep = sp["rows"], sp["cols"], sp["keep"]
    tr = _pick(rows, (256, 64))
    nb = rows // tr
    if sp["axis"] == 1:
        out_spec = pl.BlockSpec((tr, cols), lambda i, ch: (i, ch[0]))
    else:
        out_spec = pl.BlockSpec((tr, cols), lambda i, ch: (ch[0] * nb + i, 0))

    def kern(ch_ref, x_ref, o_ref):
        v = x_ref[0].astype(BF16)
        if keep == cols:
            o_ref[...] = v
        else:
            o_ref[:, :keep] = v
            o_ref[:, keep:] = jnp.zeros((tr, cols - keep), BF16)

    return pl.pallas_call(
        kern, name=name, out_shape=jax.ShapeDtypeStruct(_full_shape(k), BF16),
        grid_spec=pltpu.PrefetchScalarGridSpec(
            num_scalar_prefetch=1, grid=(nb,),
            in_specs=[pl.BlockSpec((1, tr, keep), lambda i, ch: (layer, i, 0))], out_specs=out_spec),
        compiler_params=_cparams(("parallel",)),
    )(chip, shard)


def _half_block(ref, k, s, half):
    sp = BIG_SPEC[k]
    hr = sp["rows"] // 2
    if sp["axis"] == 1:
        return ref.at[pl.ds(pl.multiple_of(half * hr, 16), hr), pl.ds(pl.multiple_of(s * sp["cols"], 128), sp["cols"])]
    return ref.at[pl.ds(pl.multiple_of(s * sp["rows"] + half * hr, 16), hr), :]


def _other_chips(x, y):
    return [(1 - x, y), (x, 1 - y), (1 - x, 1 - y)]


ALL_BIG = BIG


def _present(d):
    return tuple(k for k in ALL_BIG if k in d)


def _gather_layer(placed, conv):
    BIG = _present(placed)
    n = len(BIG)
    n_sem = 6 * n + 3

    def body(*refs):
        conv_ref = refs[n]
        out = dict(zip(BIG, refs[n + 1:2 * n + 1]))
        conv_out, send_sems, recv_sems, local_sem = refs[2 * n + 1:]
        x, y, c = _me()
        me = 2 * x + y
        chips = _other_chips(x, y)
        net = _Copies(send_sems, recv_sems)

        def conv_block(s):
            return conv_out.at[:, pl.ds(pl.multiple_of(s * CONV_COLS, 128), CONV_COLS)]

        local = pltpu.make_async_copy(conv_ref, conv_block(me), local_sem)
        local.start()
        first = {}
        for k in BIG:
            for j, (px, py) in enumerate(chips):
                first[k, j] = net.send(_half_block(out[k], k, me, c), _half_block(out[k], k, me, c), (px, py, c),
                                       _half_block(out[k], k, 2 * px + py, c))
        for px, py in chips:
            net.send(conv_ref, conv_block(me), (px, py, c), conv_block(2 * px + py))
        for k in BIG:
            for j, (px, py) in enumerate(chips):
                net.wait_landed(first[k, j])
                net.send(_half_block(out[k], k, 2 * px + py, c), _half_block(out[k], k, 2 * px + py, c), (x, y, 1 - c),
                         _half_block(out[k], k, 2 * px + py, 1 - c))
        net.finish(landed=set(first.values()))
        local.wait()

    out_shape = [jax.ShapeDtypeStruct(_full_shape(k), BF16) for k in BIG]
    out_shape.append(jax.ShapeDtypeStruct((CONV_ROWS, N_CHIPS * CONV_COLS), F32))
    outs = pl.pallas_call(
        body, name="gather_layer", out_shape=out_shape, in_specs=[HBM_SPEC] * (n + 1), out_specs=[HBM_SPEC] * (n + 1),
        input_output_aliases={i: i for i in range(n)},
        scratch_shapes=[pltpu.SemaphoreType.DMA((n_sem,)), pltpu.SemaphoreType.DMA((n_sem,)), pltpu.SemaphoreType.DMA],
    )(*[placed[k] for k in BIG], conv)
    return dict(zip(BIG, outs[:n])), outs[n]


SEM_SPEC = pl.BlockSpec(memory_space=pltpu.SEMAPHORE)


def _behind_copies(arrs, send_sems, recv_sems):
    x, y, c = _me()
    copies = []
    for i, k in enumerate(_present(arrs)):
        for j, (px, py) in enumerate(_other_chips(x, y)):
            copies.append(pltpu.make_async_remote_copy(
                src_ref=_half_block(arrs[k], k, 2 * x + y, c), dst_ref=_half_block(arrs[k], k, 2 * x + y, c),
                send_sem=send_sems.at[3 * i + j], recv_sem=recv_sems.at[3 * i + j], device_id=(px, py, c),
                device_id_type=MESH))
    return copies


def _gather_start(placed, after, tag):
    BIG = _present(placed)
    n = len(BIG)
    N_BEHIND = 3 * n

    def body(*refs):
        arrs = dict(zip(BIG, refs[n + 3:2 * n + 3]))
        send_sems, recv_sems = refs[n + 1], refs[n + 2]
        for cp in _behind_copies(arrs, send_sems, recv_sems):
            cp.start()
        refs[2 * n + 3][...] = jnp.zeros((8, 128), F32)

    outs = pl.pallas_call(
        body, name="gather_start" + tag,
        out_shape=(pltpu.SemaphoreType.DMA((N_BEHIND,)), pltpu.SemaphoreType.DMA((N_BEHIND,)),
                   *[pltpu.HBM(_full_shape(k), BF16) for k in BIG], jax.ShapeDtypeStruct((8, 128), F32)),
        in_specs=[HBM_SPEC] * n + [pl.BlockSpec(memory_space=pl.ANY)],
        out_specs=(SEM_SPEC, SEM_SPEC, *[HBM_SPEC] * n, VMEM_SPEC),
        input_output_aliases={i: i + 2 for i in range(n)},
        compiler_params=pltpu.CompilerParams(has_side_effects=pltpu.SideEffectType.DATAFLOW_SIDE_EFFECTING),
    )(*[pltpu.with_memory_space_constraint(placed[k], pltpu.HBM) for k in BIG], after)
    return outs[0], outs[1], dict(zip(BIG, outs[2:n + 2])), outs[n + 2]


def _gather_wait(send_sems, recv_sems, inflight, after, tag):
    BIG = _present(inflight)
    n = len(BIG)

    def body(*refs):
        arrs = dict(zip(BIG, refs[:n]))
        for cp in _behind_copies(arrs, refs[n], refs[n + 1]):
            cp.wait_send()
            cp.wait_recv()

    outs = pl.pallas_call(
        body, name="gather_wait" + tag, out_shape=tuple(pltpu.HBM(_full_shape(k), BF16) for k in BIG),
        in_specs=[HBM_SPEC] * n + [SEM_SPEC, SEM_SPEC, pl.BlockSpec(memory_space=pl.ANY)], out_specs=(HBM_SPEC,) * n,
        input_output_aliases={i: i for i in range(n)},
        compiler_params=pltpu.CompilerParams(has_side_effects=pltpu.SideEffectType.DATAFLOW_SIDE_EFFECTING),
    )(*[inflight[k] for k in BIG], send_sems, recv_sems, after)
    return dict(zip(BIG, outs))


def _gather_finish(arrs, tag):
    BIG = _present(arrs)
    n = len(BIG)
    N_BEHIND = 3 * n

    def body(*refs):
        out = dict(zip(BIG, refs[n:2 * n]))
        send_sems, recv_sems = refs[2 * n:]
        x, y, c = _me()
        net = _Copies(send_sems, recv_sems)
        for k in BIG:
            for px, py in _other_chips(x, y):
                net.send(_half_block(out[k], k, 2 * px + py, c), _half_block(out[k], k, 2 * px + py, c), (x, y, 1 - c),
                         _half_block(out[k], k, 2 * px + py, 1 - c))
        net.finish()

    outs = pl.pallas_call(
        body, name="gather_finish" + tag, out_shape=[jax.ShapeDtypeStruct(_full_shape(k), BF16) for k in BIG],
        in_specs=[HBM_SPEC] * n, out_specs=[HBM_SPEC] * n, input_output_aliases={i: i for i in range(n)},
        scratch_shapes=[pltpu.SemaphoreType.DMA((N_BEHIND,)), pltpu.SemaphoreType.DMA((N_BEHIND,))],
    )(*[arrs[k] for k in BIG])
    return dict(zip(BIG, outs))


def _row_chunks(ref, rows, n):
    step = rows // n
    return [ref.at[pl.ds(i * step, step), :] for i in range(n)]


def _half_pieces(ref, k, half):
    sp = BIG_SPEC[k]
    hr = sp["rows"] // 2
    if sp["axis"] == 1:
        return [ref.at[pl.ds(pl.multiple_of(half * hr, 16), hr), :]]
    return [ref.at[pl.ds(pl.multiple_of(s * sp["rows"] + half * hr, 16), hr), :] for s in range(N_CHIPS)]


def _half_shape(k):
    rows, cols = _full_shape(k)
    return rows // 2, cols


def _stacked_pieces(ref, k):
    sp = BIG_SPEC[k]
    hr = sp["rows"] // 2
    return [ref] if sp["axis"] == 1 else [ref.at[pl.ds(s * hr, hr), :] for s in range(N_CHIPS)]


def _chip_part(ref, k, s):
    sp = BIG_SPEC[k]
    hr = sp["rows"] // 2
    if sp["axis"] == 1:
        return ref.at[:, pl.ds(pl.multiple_of(s * sp["cols"], 128), sp["cols"])]
    return ref.at[pl.ds(pl.multiple_of(s * hr, 16), hr), :]


def _halves_to_sibling(grads, name):
    BIG = _present(grads)
    n = len(BIG)
    chunks = {k: max(BIG_SPEC[k]["down"] // 2, 1) if BIG_SPEC[k]["axis"] == 1 else 1 for k in BIG}
    n_sem = sum(chunks[k] if BIG_SPEC[k]["axis"] == 1 else N_CHIPS for k in BIG)

    def body(*refs):
        g = dict(zip(BIG, refs[:n]))
        out = dict(zip(BIG, refs[n:2 * n]))
        send_sems, recv_sems = refs[2 * n:]
        x, y, c = _me()
        net = _Copies(send_sems, recv_sems)
        for k in BIG:
            hr = BIG_SPEC[k]["rows"] // 2
            for src, dst in zip(_half_pieces(g[k], k, 1 - c), _stacked_pieces(out[k], k)):
                for s, d in zip(_row_chunks(src, hr, chunks[k]), _row_chunks(dst, hr, chunks[k])):
                    net.send(s, d, (x, y, 1 - c), d)
        net.finish()

    outs = pl.pallas_call(
        body, name=name, out_shape=[jax.ShapeDtypeStruct(_half_shape(k), BF16) for k in BIG],
        in_specs=[HBM_SPEC] * n, out_specs=[HBM_SPEC] * n,
        scratch_shapes=[pltpu.SemaphoreType.DMA((n_sem,)), pltpu.SemaphoreType.DMA((n_sem,))],
    )(*[grads[k] for k in BIG])
    return dict(zip(BIG, outs))


def _add_half(g, other, k, core, name):
    sp = BIG_SPEC[k]
    hr, cols = sp["rows"] // 2, _full_shape(k)[1]
    tr = _pick(hr, (256, 352, 128))
    nb = hr // tr
    if sp["axis"] == 1:
        grid = (nb,)
        g_spec = pl.BlockSpec((tr, cols), lambda i, c: (c[0] * nb + i, 0))
        h_spec = pl.BlockSpec((tr, cols), lambda i, c: (i, 0))
    else:
        grid = (N_CHIPS, nb)
        g_spec = pl.BlockSpec((tr, cols), lambda s, i, c: ((2 * s + c[0]) * nb + i, 0))
        h_spec = pl.BlockSpec((tr, cols), lambda s, i, c: (s * nb + i, 0))

    def kern(c_ref, a_ref, b_ref, o_ref):
        o_ref[...] = (a_ref[...].astype(F32) + b_ref[...].astype(F32)).astype(BF16)

    return pl.pallas_call(
        kern, name=name, out_shape=jax.ShapeDtypeStruct(_half_shape(k), BF16),
        grid_spec=pltpu.PrefetchScalarGridSpec(num_scalar_prefetch=1, grid=grid, in_specs=[g_spec, h_spec],
                                               out_specs=h_spec),
        compiler_params=_cparams(("parallel",) * len(grid)),
    )(core, g, other)


def _part_shape(k):
    return N_CHIPS - 1, BIG_SPEC[k]["rows"] // 2, BIG_SPEC[k]["cols"]


def _scatter_copies(sums, parts, send_sems, recv_sems):
    x, y, c = _me()
    copies = []
    for i, k in enumerate(_present(sums)):
        for j, (px, py) in enumerate(_other_chips(x, y)):
            copies.append(pltpu.make_async_remote_copy(
                src_ref=_chip_part(sums[k], k, 2 * px + py), dst_ref=parts[k].at[j], send_sem=send_sems.at[3 * i + j],
                recv_sem=recv_sems.at[3 * i + j], device_id=(px, py, c), device_id_type=MESH))
    return copies


def _scatter_start(sums, tag):
    BIG = _present(sums)
    n = len(BIG)
    N_BEHIND = 3 * n
    lands = [pltpu.with_memory_space_constraint(lax.empty(_part_shape(k), BF16), pltpu.HBM) for k in BIG]

    def body(*refs):
        outs = refs[2 * n + 2:4 * n + 2]
        for cp in _scatter_copies(dict(zip(BIG, outs[:n])), dict(zip(BIG, outs[n:])), refs[2 * n], refs[2 * n + 1]):
            cp.start()
        refs[4 * n + 2][...] = jnp.zeros((8, 128), F32)

    outs = pl.pallas_call(
        body, name="scatter_start" + tag,
        out_shape=(pltpu.SemaphoreType.DMA((N_BEHIND,)), pltpu.SemaphoreType.DMA((N_BEHIND,)),
                   *[pltpu.HBM(_half_shape(k), BF16) for k in BIG], *[pltpu.HBM(_part_shape(k), BF16) for k in BIG],
                   jax.ShapeDtypeStruct((8, 128), F32)),
        in_specs=[HBM_SPEC] * (2 * n), out_specs=(SEM_SPEC, SEM_SPEC, *[HBM_SPEC] * (2 * n), VMEM_SPEC),
        input_output_aliases={i: i + 2 for i in range(2 * n)},
        compiler_params=pltpu.CompilerParams(has_side_effects=pltpu.SideEffectType.DATAFLOW_SIDE_EFFECTING),
    )(*[pltpu.with_memory_space_constraint(sums[k], pltpu.HBM) for k in BIG], *lands)
    return outs[0], outs[1], outs[2:2 * n + 2], outs[2 * n + 2]


def _scatter_wait(send_sems, recv_sems, inflight, keys, after, tag):
    BIG = keys
    n = len(BIG)

    def body(*refs):
        for cp in _scatter_copies(dict(zip(BIG, refs[:n])), dict(zip(BIG, refs[n:2 * n])), refs[2 * n], refs[2 * n + 1]):
            cp.wait_send()
            cp.wait_recv()

    outs = pl.pallas_call(
        body, name="scatter_wait" + tag,
        out_shape=(*[pltpu.HBM(_half_shape(k), BF16) for k in BIG], *[pltpu.HBM(_part_shape(k), BF16) for k in BIG]),
        in_specs=[HBM_SPEC] * (2 * n) + [SEM_SPEC, SEM_SPEC, pl.BlockSpec(memory_space=pl.ANY)],
        out_specs=(HBM_SPEC,) * (2 * n), input_output_aliases={i: i for i in range(2 * n)},
        compiler_params=pltpu.CompilerParams(has_side_effects=pltpu.SideEffectType.DATAFLOW_SIDE_EFFECTING),
    )(*inflight, send_sems, recv_sems, after)
    return dict(zip(BIG, outs[:n])), dict(zip(BIG, outs[n:]))


def _sum_half(parts, own, k, where, layer, into, name):
    sp = BIG_SPEC[k]
    rows, cols, keep = sp["rows"], sp["cols"], sp["keep"]
    hr = rows // 2
    tr = _pick(hr, (256, 352, 128))
    nb = hr // tr
    if sp["axis"] == 1:
        own_spec = pl.BlockSpec((tr, cols), lambda i, w: (i, w[0]))
    else:
        own_spec = pl.BlockSpec((tr, cols), lambda i, w: (w[0] * nb + i, 0))

    def kern(w_ref, p_ref, own_ref, *rest):
        tot = own_ref[...].astype(F32)
        for j in range(N_CHIPS - 1):
            tot = tot + p_ref[j].astype(F32)
        rest[-1][0] = tot[:, :keep]

    in_specs = [pl.BlockSpec((N_CHIPS - 1, tr, cols), lambda i, w: (0, i, 0)), own_spec]
    args = [where, parts, own]
    if into is not None:
        in_specs.append(pl.BlockSpec(memory_space=pl.ANY))
        args.append(into)
    return pl.pallas_call(
        kern, name=name, out_shape=jax.ShapeDtypeStruct((DEPTH, rows, keep), F32),
        grid_spec=pltpu.PrefetchScalarGridSpec(
            num_scalar_prefetch=1, grid=(nb,), in_specs=in_specs,
            out_specs=pl.BlockSpec((1, tr, keep), lambda i, w: (layer, w[1] * nb + i, 0))),
        input_output_aliases={} if into is None else {3: 0},
        compiler_params=_cparams(("parallel",)),
    )(*args)


def _exchange_halves(red):
    n = len(BIG)

    def body(*refs):
        out = dict(zip(BIG, refs[n:2 * n]))
        send_sems, recv_sems = refs[2 * n:]
        x, y, c = _me()
        net = _Copies(send_sems, recv_sems)
        for k in BIG:
            hr = BIG_SPEC[k]["rows"] // 2
            for l in range(DEPTH):
                mine = out[k].at[l, pl.ds(pl.multiple_of(c * hr, 8), hr), :]
                theirs = out[k].at[l, pl.ds(pl.multiple_of((1 - c) * hr, 8), hr), :]
                net.send(mine, mine, (x, y, 1 - c), theirs)
        net.finish()

    outs = pl.pallas_call(
        body, name="exchange_halves",
        out_shape=[jax.ShapeDtypeStruct((DEPTH, BIG_SPEC[k]["rows"], BIG_SPEC[k]["keep"]), F32) for k in BIG],
        in_specs=[HBM_SPEC] * n, out_specs=[HBM_SPEC] * n, input_output_aliases={i: i for i in range(n)},
        scratch_shapes=[pltpu.SemaphoreType.DMA((DEPTH * n,)), pltpu.SemaphoreType.DMA((DEPTH * n,))],
    )(*[red[k] for k in BIG])
    return dict(zip(BIG, outs))


def _adam_vals(g, w, m, v):
    m2 = ADAM_B1 * m + (1.0 - ADAM_B1) * g
    v2 = ADAM_B2 * v + (1.0 - ADAM_B2) * (g * g)
    m_hat = m2 / (1.0 - ADAM_B1 ** ADAM_STEP)
    v_hat = v2 / (1.0 - ADAM_B2 ** ADAM_STEP)
    return -ADAM_LR * (m_hat / (jnp.sqrt(v_hat) + ADAM_EPS) + ADAM_WD * w), m2, v2


def _allreduce_small_adam(groups):
    ng = len(groups)

    def body(*refs):
        ins = [refs[4 * i:4 * i + 4] for i in range(ng)]
        outs = [refs[4 * ng + 4 * i:4 * ng + 4 * i + 4] for i in range(ng)]
        bufs = refs[8 * ng:9 * ng]
        send_sems, recv_sems = refs[9 * ng:]
        x, y, c = _me()
        me = 4 * x + 2 * y + c
        net = _Copies(send_sems, recv_sems)
        for (g_ref, _, _, _), buf in zip(ins, bufs):
            buf[me] = g_ref[...]
            for f in FLIPS:
                px, py, pc = _peer(x, y, c, f)
                net.send(g_ref, buf.at[me], (px, py, pc), buf.at[4 * px + 2 * py + pc])
        net.finish()
        for (_, w_ref, m_ref, v_ref), (gs_ref, d_ref, nm_ref, nv_ref), buf in zip(ins, outs, bufs):
            tot = buf[0]
            for d in range(1, 8):
                tot = tot + buf[d]
            gs_ref[...] = tot
            d_ref[...], nm_ref[...], nv_ref[...] = _adam_vals(tot, w_ref[...], m_ref[...], v_ref[...])

    shapes = [jax.ShapeDtypeStruct(g[0].shape, F32) for g in groups for _ in range(4)]
    outs = pl.pallas_call(
        body, name="allreduce_small", out_shape=shapes, in_specs=[VMEM_SPEC] * (4 * ng), out_specs=[VMEM_SPEC] * (4 * ng),
        scratch_shapes=[pltpu.VMEM((8,) + g[0].shape, F32) for g in groups]
        + [pltpu.SemaphoreType.DMA((7 * ng,)), pltpu.SemaphoreType.DMA((7 * ng,))],
        compiler_params=pltpu.CompilerParams(vmem_limit_bytes=VMEM_LIMIT),
    )(*[t for g in groups for t in g])
    return [outs[4 * i:4 * i + 4] for i in range(ng)]


def _adam(g, w, m, v, name, lead_block=1):
    shape = w.shape
    lead, rows, cols = math.prod(shape[:-2]), shape[-2], shape[-1]
    tr = _pick(rows, (256, 352, 64, 8, rows))
    spec = pl.BlockSpec((lead_block, tr, cols), lambda l, i: (l, i, 0))

    def kern(g_ref, w_ref, m_ref, v_ref, d_ref, nm_ref, nv_ref):
        d_ref[...], nm_ref[...], nv_ref[...] = _adam_vals(g_ref[...], w_ref[...], m_ref[...], v_ref[...])

    outs = pl.pallas_call(
        kern, name=name, grid=(lead // lead_block, rows // tr), in_specs=[spec] * 4, out_specs=[spec] * 3,
        out_shape=[jax.ShapeDtypeStruct((lead, rows, cols), F32)] * 3, compiler_params=_cparams(("parallel", "parallel")),
    )(*[t.reshape(lead, rows, cols) for t in (g, w, m, v)])
    return [o.reshape(shape) for o in outs]


SMALL = ("attn_norm", "sgu_ln_g", "sgu_ln_b", "sgu_w", "sgu_b", "attn_sinks", "dn_a_log", "dn_dt_bias", "dn_norm",
         "ffn_norm", "final_norm")
SMALL_2D = {"attn_norm": (DEPTH, D_MODEL), "ffn_norm": (DEPTH, D_MODEL), "final_norm": (1, D_MODEL),
            "sgu_ln_g": (DEPTH, MIX), "sgu_ln_b": (DEPTH, MIX), "sgu_w": (DEPTH * SGU_G * SGU_T, SGU_T),
            "sgu_b": (DEPTH * SGU_G, SGU_T), "dn_norm": (DEPTH, DN_HD)}
TINY = ("attn_sinks", "dn_a_log", "dn_dt_bias")


def _pack_tiny(vals, extra=None):
    flat = [vals[k].astype(F32).reshape(-1) for k in TINY] + ([] if extra is None else [extra.astype(F32).reshape(-1)])
    n = sum(f.shape[0] for f in flat)
    return jnp.concatenate(flat + [jnp.zeros((8 * 128 - n,), F32)]).reshape(8, 128)


def _unpack_tiny(tile, shapes):
    flat, out, o = tile.reshape(-1), {}, 0
    for k in TINY:
        n = math.prod(shapes[k])
        out[k] = flat[o:o + n].reshape(shapes[k])
        o += n
    return out, flat[o]


def _in_col_segments():
    shard, padded = IN_COLS // N_CHIPS, BIG_SPEC["w_in"]["cols"]
    segs, mine = [], 0
    for a, n in IN_PIECES:
        o = a
        while o < a + n:
            end = min(a + n, (o // shard + 1) * shard)
            segs.append(((o // shard) * padded + o % shard, mine + o - a, end - o))
            o = end
        mine += n
    return segs


def _move_cols(x, segs, out_cols, name):
    layers, rows, cols = x.shape
    tr = _pick(rows, (256, rows))
    gaps, at = [], 0
    for d, w in sorted((d, w) for _, d, w in segs):
        if d > at:
            gaps.append((at, d - at))
        at = d + w
    if at < out_cols:
        gaps.append((at, out_cols - at))

    def kern(x_ref, o_ref):
        for s, d, w in segs:
            o_ref[0, :, d:d + w] = x_ref[0, :, s:s + w]
        for d, w in gaps:
            o_ref[0, :, d:d + w] = jnp.zeros((tr, w), x.dtype)

    return pl.pallas_call(
        kern, name=name, grid=(layers, rows // tr), in_specs=[pl.BlockSpec((1, tr, cols), lambda l, i: (l, i, 0))],
        out_specs=pl.BlockSpec((1, tr, out_cols), lambda l, i: (l, i, 0)),
        out_shape=jax.ShapeDtypeStruct((layers, rows, out_cols), x.dtype), compiler_params=_cparams(("parallel", "parallel")),
    )(x)


WEIGHTS = ("attn_norm", "w_in", "sgu_ln_g", "sgu_ln_b", "sgu_w", "sgu_b", "attn_sinks", "dn_conv_w", "dn_a_log",
           "dn_dt_bias", "dn_norm", "w_branch", "w_out", "ffn_norm", "w_gate_up", "w_down", "final_norm")


def kernel(x, positions, attn_norm, w_in, sgu_ln_g, sgu_ln_b, sgu_w, sgu_b, attn_sinks, dn_conv_w, dn_a_log, dn_dt_bias, dn_norm, w_branch, w_out, ffn_norm, w_gate_up, w_down, final_norm, loss_target, m_attn_norm, m_w_in, m_sgu_ln_g, m_sgu_ln_b, m_sgu_w, m_sgu_b, m_attn_sinks, m_dn_conv_w, m_dn_a_log, m_dn_dt_bias, m_dn_norm, m_w_branch, m_w_out, m_ffn_norm, m_w_gate_up, m_w_down, m_final_norm, v_attn_norm, v_w_in, v_sgu_ln_g, v_sgu_ln_b, v_sgu_w, v_sgu_b, v_attn_sinks, v_dn_conv_w, v_dn_a_log, v_dn_dt_bias, v_dn_norm, v_w_branch, v_w_out, v_ffn_norm, v_w_gate_up, v_w_down, v_final_norm):
    given = dict(locals())
    W = {k: given[k] for k in WEIGHTS}
    M = {k: given["m_" + k] for k in WEIGHTS}
    V = {k: given["v_" + k] for k in WEIGHTS}
    chip = 2 * lax.axis_index("x") + lax.axis_index("y")
    core = lax.axis_index("c")
    chip1 = chip.astype(jnp.int32).reshape(1)
    where = jnp.stack([chip, core]).astype(jnp.int32)

    placed = [{k: _place_shard(W[k].reshape(DEPTH, BIG_SPEC[k]["rows"], BIG_SPEC[k]["keep"]), k, chip1, l,
                               f"place{l}_{k}") for k in BIG} for l in range(DEPTH)]
    _, conv_full = _gather_layer({}, dn_conv_w.reshape(CONV_ROWS, CONV_COLS))
    behind = {"in": _gather_start({"w_in": placed[0]["w_in"]}, conv_full, "in")}
    behind["0"] = _gather_start({k: placed[0][k] for k in BIG if k != "w_in"}, behind["in"][3], "0")
    behind["1"] = _gather_start(placed[1], behind["0"][3], "1")
    segs = _in_col_segments()

    def arrived(tag, after):
        send_sems, recv_sems, inflight, _ = behind[tag]
        return _gather_finish(_gather_wait(send_sems, recv_sems, inflight, after, tag), tag)

    def big_of_layer(l, x_l):
        got = {} if l == 0 else arrived("1", x_l)
        cols = lambda t: _move_cols(t[None], segs, IN_R, f"w_in_cols{l}")[0]

        def rest(after):
            full = got or arrived("0", after)
            return dict(full, w_branch=full["w_branch"].reshape(3, MIX, D_MODEL))

        return dict(w_in=cols(got["w_in"]) if got else (lambda after: cols(arrived("in", after)["w_in"])), rest=rest)

    w = {k: W[k] for k in SMALL}
    w["attn_norm"] = attn_norm + behind["1"][3][0, 0]
    w["dn_conv_w"] = conv_full.reshape(DEPTH, DN_CONV, 3 * MIX)

    core1 = core.astype(jnp.int32).reshape(1)
    back_segs = [(d, s, n) for s, d, n in segs]
    travelling, started, sums, parts = [], [], [{}, {}], [{}, {}]

    def on_grads(l, gl):
        gl, tag = dict(gl), f"{l}_{len(gl)}"
        if "w_in" in gl:
            gl["w_in"] = _move_cols(gl["w_in"][None], back_segs, _full_shape("w_in")[1], f"g_in_cols{l}")[0]
        if "w_branch" in gl:
            gl["w_branch"] = gl["w_branch"].reshape(3 * MIX, D_MODEL)
        sibling = _halves_to_sibling(gl, "halves_to_sibling" + tag)
        chip_sums = {k: _add_half(gl[k], sibling[k], k, core1, f"chip_sum{l}_{k}") for k in gl}
        send_sems, recv_sems, inflight, token = _scatter_start(chip_sums, tag)
        travelling.append((l, send_sems, recv_sems, inflight, _present(gl), tag))
        started.append(token)
        return token

    loss, dx, g = _local_step(x[0], positions[0], loss_target[0], w, big_of_layer, on_grads)

    grads = {}
    conv_2d = (CONV_ROWS, N_CHIPS * CONV_COLS)
    conv_zero = jnp.zeros(conv_2d, F32)
    groups = [tuple(d[k].reshape(SMALL_2D[k]) for d in (g, W, M, V)) for k in SMALL_2D]
    groups.append((g["dn_conv_w"].reshape(conv_2d), conv_zero, conv_zero, conv_zero))
    loss = loss + started[-1][0, 0]
    groups.append((_pack_tiny(g, loss), _pack_tiny(W), _pack_tiny(M), _pack_tiny(V)))
    summed = _allreduce_small_adam(groups)
    delta, new_m, new_v = {}, {}, {}
    for k, outs in zip(SMALL_2D, summed):
        for d, t in zip((grads, delta, new_m, new_v), outs):
            d[k] = t.reshape(W[k].shape)
    conv_sum = summed[len(SMALL_2D)][0].reshape(g["dn_conv_w"].shape)
    grads["dn_conv_w"] = lax.dynamic_slice_in_dim(conv_sum, chip * dn_conv_w.shape[2], dn_conv_w.shape[2], axis=2)
    tiny_shapes = {k: W[k].shape for k in TINY}
    tiny, loss_total = _unpack_tiny(summed[-1][0], tiny_shapes)
    grads.update(tiny)
    for d, t in zip((delta, new_m, new_v), summed[-1][1:]):
        d.update(_unpack_tiny(t, tiny_shapes)[0])

    for l, send_sems, recv_sems, inflight, keys, tag in travelling:
        landed = _scatter_wait(send_sems, recv_sems, inflight, keys, summed[0][0], tag)
        sums[l].update(landed[0])
        parts[l].update(landed[1])
    red = {k: _sum_half(parts[1][k], sums[1][k], k, where, 1, None, f"sum1_{k}") for k in BIG}
    red = {k: _sum_half(parts[0][k], sums[0][k], k, where, 0, red[k], f"sum0_{k}") for k in BIG}
    reduced = _exchange_halves(red)
    grads.update({k: reduced[k].reshape(W[k].shape) for k in BIG})
    for k in ("w_branch", "w_out", "w_gate_up", "w_down", "dn_conv_w"):
        delta[k], new_m[k], new_v[k] = _adam(grads[k], W[k], M[k], V[k], "adam_" + k)
    lead_first = lambda t: jnp.transpose(t, (2, 0, 1))
    outs = _adam(*[lead_first(d["w_in"]) for d in (grads, W, M, V)], "adam_w_in", lead_block=IN_COLS // N_CHIPS // 10)
    delta["w_in"], new_m["w_in"], new_v["w_in"] = (jnp.transpose(o, (1, 2, 0)) for o in outs)

    return (loss_total, dx[None], *[grads[k] for k in WEIGHTS], *[delta[k] for k in WEIGHTS],
            *[new_m[k] for k in WEIGHTS], *[new_v[k] for k in WEIGHTS])
```

```python
import functools
import math

import jax
import jax.numpy as jnp
from jax import lax
from jax.experimental import pallas as pl
from jax.experimental.pallas import tpu as pltpu

F32 = jnp.float32
BF16 = jnp.bfloat16
HI = lax.Precision.HIGHEST

D_MODEL = 1024
DEPTH = 2
MIX = 512
EPS = 1e-6
SGU_G, SGU_T = 4, 128
SWA_H, SWA_KV, SWA_HD, WINDOW = 8, 2, 64, 128
ROPE_THETA, ROPE_DIM = 500000.0, 16
DN_H, DN_HD, DN_CONV, DN_C = 4, 128, 4, 64
D_FF = 2816
IN_COLS = 6920
IN_PIECES = ((3848, 3072), (1792, 1536), (3328, 512), (0, 512), (512, 512), (1024, 512), (1536, 128), (1664, 128),
             (3840, 8))
IN_PAD = 120
IN_R = 7040
C_GATE, C_QKV, C_ZC, C_UA, C_VA, C_QB, C_KB, C_VB, C_SM = 0, 3072, 4608, 5120, 5632, 6144, 6656, 6784, 6912

ADAM_LR, ADAM_B1, ADAM_B2, ADAM_EPS, ADAM_WD, ADAM_STEP = 0.001, 0.9, 0.999, 1e-08, 0.01, 10
VMEM_LIMIT = 56 * 1024 * 1024


def _cparams(sem):
    return pltpu.CompilerParams(dimension_semantics=sem, vmem_limit_bytes=VMEM_LIMIT)


def _dg(a, b, ca, cb, prec=None):
    return lax.dot_general(a, b, (((ca,), (cb,)), ((), ())), precision=prec, preferred_element_type=F32)


def _split(x):
    hi = x.astype(BF16)
    return hi, (x - hi.astype(F32)).astype(BF16)


def _dg3_many(as_, bs, ca, cb):
    sa = [_split(a) for a in as_]
    sb = [_split(b) for b in bs]
    hh = [_dg(a[0], b[0], ca, cb) for a, b in zip(sa, sb)]
    hl = [_dg(a[0], b[1], ca, cb) for a, b in zip(sa, sb)]
    lh = [_dg(a[1], b[0], ca, cb) for a, b in zip(sa, sb)]
    return [x + (y + z) for x, y, z in zip(hh, hl, lh)]


def _dg_exact_lhs_many(a01, bs, ca, cb):
    a = a01.astype(BF16)
    b1 = [b.astype(BF16) for b in bs]
    r1 = [b - t.astype(F32) for b, t in zip(bs, b1)]
    b2 = [r.astype(BF16) for r in r1]
    b3 = [(r - t.astype(F32)).astype(BF16) for r, t in zip(r1, b2)]
    d1 = [_dg(a, t, ca, cb) for t in b1]
    d2 = [_dg(a, t, ca, cb) for t in b2]
    d3 = [_dg(a, t, ca, cb) for t in b3]
    return [x + (y + z) for x, y, z in zip(d1, d2, d3)]


def _mm(a, b):
    return _dg(a.astype(BF16), b.astype(BF16), 1, 0)


def _mm_nt(a, b):
    return _dg(a.astype(BF16), b.astype(BF16), 1, 1)


def _mm_tn(a, b):
    return _dg(a.astype(BF16), b.astype(BF16), 0, 0)


def _sigmoid(x):
    return 0.5 * jnp.tanh(0.5 * x) + 0.5


def _silu(x):
    return x * _sigmoid(x)


def _dsilu(x):
    s = _sigmoid(x)
    return s * (1.0 + x * (1.0 - s))


_GC = math.sqrt(2.0 / math.pi)


def _gelu(x):
    return 0.5 * x * (1.0 + jnp.tanh(_GC * (x + 0.044715 * x * x * x)))


def _dgelu(x):
    t = jnp.tanh(_GC * (x + 0.044715 * x * x * x))
    return 0.5 * (1.0 + t) + 0.5 * x * (1.0 - t * t) * _GC * (1.0 + 3.0 * 0.044715 * x * x)


def _softplus(x):
    return jnp.maximum(x, 0.0) + jnp.log(1.0 + jnp.exp(-jnp.abs(x)))


def _acc(ref, val, i):
    @pl.when(i == 0)
    def _():
        ref[...] = val

    @pl.when(i > 0)
    def _():
        ref[...] += val


def _halo_rows(dtype):
    return 8 * 4 // jnp.dtype(dtype).itemsize


def _tok_call(body, name, S, TB, tok_in, const_in=(), tok_out=(), acc_out=(), prev_in=(), next_in=(), smem_in=()):
    nb = S // TB
    in_specs, args = [], []
    for a, w, cb in tok_in:
        in_specs.append(pl.BlockSpec((TB, w), functools.partial(lambda i, cb: (i, cb), cb=cb)))
        args.append(a)
    for a, w, cb in prev_in:
        hr = _halo_rows(a.dtype)
        in_specs.append(pl.BlockSpec((hr, w), functools.partial(
            lambda i, cb, r: (jnp.maximum(i * r - 1, 0), cb), cb=cb, r=TB // hr)))
        args.append(a)
    for a, w, cb in next_in:
        hr = _halo_rows(a.dtype)
        in_specs.append(pl.BlockSpec((hr, w), functools.partial(
            lambda i, cb, r, last: (jnp.minimum((i + 1) * r, last), cb), cb=cb, r=TB // hr, last=S // hr - 1)))
        args.append(a)
    for a in const_in:
        in_specs.append(pl.BlockSpec(a.shape, lambda i: (0, 0)))
        args.append(a)
    for a in smem_in:
        in_specs.append(pl.BlockSpec(memory_space=pltpu.SMEM))
        args.append(a)
    out_specs, out_shape, aliases, shared = [], [], {}, {}
    for o, (w, dt, *dest) in enumerate(tok_out):
        if not dest:
            out_specs.append(pl.BlockSpec((TB, w), lambda i: (i, 0)))
            out_shape.append(jax.ShapeDtypeStruct((S, w), dt))
            continue
        cb, wide = dest
        out_specs.append(pl.BlockSpec((TB, w), functools.partial(lambda i, cb: (i, cb), cb=cb)))
        out_shape.append(jax.ShapeDtypeStruct((S, wide if isinstance(wide, int) else wide.shape[1]), dt))
        if not isinstance(wide, int):
            if id(wide) not in shared:
                shared[id(wide)] = len(args)
                in_specs.append(pl.BlockSpec(memory_space=pl.ANY))
                args.append(wide)
            aliases[shared[id(wide)]] = o
    for shp, dt in acc_out:
        out_specs.append(pl.BlockSpec(shp, lambda i: (0, 0)))
        out_shape.append(jax.ShapeDtypeStruct(shp, dt))
    n_extra = len(shared)

    def kern(*refs):
        n_in = len(in_specs) - n_extra
        body(pl.program_id(0), *refs[:n_in], *refs[n_in + n_extra:])

    return pl.pallas_call(
        kern, name=name, grid=(nb,), in_specs=in_specs, out_specs=out_specs, out_shape=out_shape,
        input_output_aliases=aliases, compiler_params=_cparams(("arbitrary",)),
    )(*args)


MM_BLOCKS = (1024, 1408, 640, 512, 256, 128)


def _pick(n, cands):
    for c in cands:
        if n % c == 0:
            return c
    return n


MM_VMEM_BUDGET = 44 * 1024 * 1024


def _mm_blocks(M, N, K, a_bytes, b_bytes, o_bytes, add_bytes):
    bn = _pick(N, MM_BLOCKS)
    fits = None
    for bk in [K] + [c for c in (2816, 2048) + MM_BLOCKS if c < K and K % c == 0]:
        for bm in [c for c in (2048,) + MM_BLOCKS if M % c == 0 and c >= min(M, 512)]:
            b_bufs = 1 if (bk == K and bn == N) else 2
            need = 2 * bm * bk * a_bytes + b_bufs * bk * bn * b_bytes + 2 * bm * bn * (o_bytes + add_bytes)
            need += bm * bn * 4 if bk < K else 0
            if need <= MM_VMEM_BUDGET:
                fits = fits or (bm, bn, bk)
                if (M // bm) * (N // bn) * (K // bk) >= 4:
                    return bm, bn, bk
    if fits is None:
        raise ValueError(f"no matmul blocks for {(M, N, K)}")
    return fits


def _matmul(a, b, *, ta=False, tb=False, add=None, out_dtype=F32, name):
    M, K = (a.shape[1], a.shape[0]) if ta else a.shape
    N = b.shape[0] if tb else b.shape[1]
    bm, bn, bk = _mm_blocks(M, N, K, a.dtype.itemsize, b.dtype.itemsize, jnp.dtype(out_dtype).itemsize,
                            0 if add is None else add.dtype.itemsize)
    nk = K // bk
    b_mode = dict(pipeline_mode=pl.Buffered(1)) if (bk == K and bn == N) else {}
    a_spec = pl.BlockSpec((bk, bm), lambda i, j, k: (k, i)) if ta else pl.BlockSpec((bm, bk), lambda i, j, k: (i, k))
    b_spec = (pl.BlockSpec((bn, bk), lambda i, j, k: (j, k), **b_mode) if tb
              else pl.BlockSpec((bk, bn), lambda i, j, k: (k, j), **b_mode))
    o_spec = pl.BlockSpec((bm, bn), lambda i, j, k: (i, j))
    ca, cb = (0 if ta else 1), (1 if tb else 0)

    def kern(*refs):
        a_ref, b_ref = refs[:2]
        add_ref = refs[2] if add is not None else None
        o_ref = refs[3] if add is not None else refs[2]
        p = _dg(a_ref[...].astype(BF16), b_ref[...].astype(BF16), ca, cb)

        def finish(r):
            if add is not None:
                r = r + add_ref[...].astype(F32)
            o_ref[...] = r.astype(out_dtype)

        if nk == 1:
            finish(p)
            return
        acc_ref = refs[-1]
        k = pl.program_id(2)

        @pl.when(k == 0)
        def _():
            acc_ref[...] = p

        @pl.when((k > 0) & (k < nk - 1))
        def _():
            acc_ref[...] += p

        @pl.when(k == nk - 1)
        def _():
            finish(acc_ref[...] + p)

    in_specs = [a_spec, b_spec] + ([o_spec] if add is not None else [])
    args = (a, b) + ((add,) if add is not None else ())
    return pl.pallas_call(
        kern, name=name, grid=(M // bm, N // bn, nk), in_specs=in_specs, out_specs=o_spec,
        out_shape=jax.ShapeDtypeStruct((M, N), out_dtype),
        scratch_shapes=[pltpu.VMEM((bm, bn), F32)] if nk > 1 else [],
        compiler_params=_cparams(("parallel", "parallel", "arbitrary")),
    )(*args)


def _rms_fwd(x, g, name):
    S = x.shape[0]

    def body(i, x_ref, g_ref, h_ref):
        xv = x_ref[...]
        r = lax.rsqrt(jnp.mean(xv * xv, axis=-1, keepdims=True) + EPS)
        h_ref[...] = (xv * r * g_ref[...]).astype(BF16)

    return _tok_call(body, name, S, min(S, 512), [(x, D_MODEL, 0)], [g], [(D_MODEL, BF16)])[0]


def _rms_bwd_vals(xv, g, dh):
    r = lax.rsqrt(jnp.mean(xv * xv, axis=-1, keepdims=True) + EPS)
    u = dh * g
    dx = r * u - xv * (r * r * r) * jnp.mean(u * xv, axis=-1, keepdims=True)
    dg = jnp.sum(dh * xv * r, axis=0, keepdims=True)
    return dx, dg


def _rms_bwd_add(x, g, dh, dres, name):
    S = x.shape[0]

    def body(i, x_ref, dh_ref, dr_ref, g_ref, dx_ref, dg_ref):
        dx, dg = _rms_bwd_vals(x_ref[...], g_ref[...], dh_ref[...].astype(F32))
        dx_ref[...] = dr_ref[...] + dx
        _acc(dg_ref, dg, i)

    return _tok_call(body, name, S, min(S, 512), [(x, D_MODEL, 0), (dh, D_MODEL, 0), (dres, D_MODEL, 0)], [g],
                     [(D_MODEL, F32)], [((1, D_MODEL), F32)])


def _final_loss(x, g, target):
    S = x.shape[0]

    def body(i, x_ref, t_ref, g_ref, dx_ref, loss_ref, dg_ref):
        xv, gv = x_ref[...], g_ref[...]
        r = lax.rsqrt(jnp.mean(xv * xv, axis=-1, keepdims=True) + EPS)
        e = xv * r * gv - t_ref[...]
        part = 0.5 * jnp.sum(jnp.mean(e * e, axis=-1, keepdims=True), axis=0, keepdims=True)
        dx, dg = _rms_bwd_vals(xv, gv, e * (1.0 / D_MODEL))
        dx_ref[...] = dx
        _acc(loss_ref, jnp.broadcast_to(part, (1, 128)), i)
        _acc(dg_ref, dg, i)

    return _tok_call(body, "final_loss", S, min(S, 512), [(x, D_MODEL, 0), (target, D_MODEL, 0)], [g],
                     [(D_MODEL, F32)], [((1, 128), F32), ((1, D_MODEL), F32)])


def _swiglu_fwd(gu, name):
    S = gu.shape[0]

    def body(i, gu_ref, a_ref):
        a_ref[...] = (_silu(gu_ref[:, :D_FF].astype(F32)) * gu_ref[:, D_FF:].astype(F32)).astype(BF16)

    return _tok_call(body, name, S, min(S, 256), [(gu, 2 * D_FF, 0)], [], [(D_FF, BF16)])[0]


def _swiglu_bwd(gu, dact, name):
    S = gu.shape[0]

    def body(i, gu_ref, da_ref, dgu_ref):
        gg, uu, da = gu_ref[:, :D_FF].astype(F32), gu_ref[:, D_FF:].astype(F32), da_ref[...].astype(F32)
        dgu_ref[:, :D_FF] = (da * uu * _dsilu(gg)).astype(BF16)
        dgu_ref[:, D_FF:] = (da * _silu(gg)).astype(BF16)

    return _tok_call(body, name, S, min(S, 256), [(gu, 2 * D_FF, 0), (dact, D_FF, 0)], [], [(2 * D_FF, BF16)])[0]


def _merge_fwd(proj, bds, name):
    S = proj.shape[0]

    def body(i, g0, g1, g2, b0, b1, b2, m_ref):
        m = jnp.zeros(m_ref.shape, F32)
        for gr, br in ((g0, b0), (g1, b1), (g2, b2)):
            m = m + _sigmoid(gr[...].astype(F32)) * br[...].astype(F32)
        m_ref[...] = m.astype(BF16)

    tok = [(proj, D_MODEL, n) for n in range(3)] + [(b, D_MODEL, 0) for b in bds]
    return _tok_call(body, name, S, min(S, 512), tok, [], [(D_MODEL, BF16)])[0]


def _merge_bwd(proj, bds, dm, name):
    S = proj.shape[0]

    def body(i, g0, g1, g2, b0, b1, b2, dm_ref, d0, d1, d2, dgp_ref):
        dmv = dm_ref[...]
        for n, (gr, br, dr) in enumerate(((g0, b0, d0), (g1, b1, d1), (g2, b2, d2))):
            s = _sigmoid(gr[...].astype(F32))
            dr[...] = (dmv * s).astype(BF16)
            dgp_ref[:, n * D_MODEL:(n + 1) * D_MODEL] = (dmv * br[...].astype(F32) * s * (1.0 - s)).astype(BF16)

    tok = [(proj, D_MODEL, n) for n in range(3)] + [(b, D_MODEL, 0) for b in bds] + [(dm, D_MODEL, 0)]
    return _tok_call(body, name, S, min(S, 512), tok, [],
                     [(D_MODEL, BF16)] * 3 + [(3 * D_MODEL, BF16, C_GATE // (3 * D_MODEL), IN_R)])


def _sgu_ln(v, lg, lb):
    mu = jnp.mean(v, axis=-1, keepdims=True)
    vc = v - mu
    rstd = lax.rsqrt(jnp.mean(vc * vc, axis=-1, keepdims=True) + EPS)
    vhat = vc * rstd
    return vhat, rstd, vhat * lg + lb


def _sgu_fwd(proj, lg, lb, wc, bst, name):
    S = proj.shape[0]

    def body(i, ua_ref, va_ref, lg_ref, lb_ref, wc_ref, bs_ref, o_ref):
        u = _gelu(ua_ref[...].astype(F32))
        _, _, vn = _sgu_ln(_gelu(va_ref[...].astype(F32)), lg_ref[...], lb_ref[...])
        for g in range(SGU_G):
            sl = slice(g * 128, (g + 1) * 128)
            mixed = _mm(wc_ref[sl, :], vn[:, sl]) + bs_ref[:, g:g + 1]
            o_ref[:, sl] = (u[:, sl] * mixed).astype(BF16)

    return _tok_call(body, name, S, SGU_T, [(proj, MIX, C_UA // MIX), (proj, MIX, C_VA // MIX)], [lg, lb, wc, bst],
                     [(MIX, BF16)])[0]


def _sgu_bwd(proj, lg, lb, wc, bst, dout, dproj, name):
    S = proj.shape[0]

    def body(i, ua_ref, va_ref, do_ref, lg_ref, lb_ref, wc_ref, bs_ref, duv_ref, dlg_ref, dlb_ref, dwc_ref,
             dbs_ref):
        ua, va, do = ua_ref[...].astype(F32), va_ref[...].astype(F32), do_ref[...].astype(F32)
        u = _gelu(ua)
        lgv = lg_ref[...]
        vhat, rstd, vn = _sgu_ln(_gelu(va), lgv, lb_ref[...])
        tril = lax.broadcasted_iota(jnp.int32, (128, 128), 0) >= lax.broadcasted_iota(jnp.int32, (128, 128), 1)
        lane4 = lax.broadcasted_iota(jnp.int32, (128, 4), 1)
        gs = range(SGU_G)
        sls = [slice(g * 128, (g + 1) * 128) for g in gs]
        wgs = [wc_ref[sl, :] for sl in sls]
        mixed = [_mm(wgs[g], vn[:, sls[g]]) for g in gs]
        dmix = [do[:, sl] * u[:, sl] for sl in sls]
        dwg = [_mm_nt(dmix[g], vn[:, sls[g]]) for g in gs]
        dvn = jnp.concatenate([_mm_tn(wgs[g], dmix[g]) for g in gs], axis=1)
        dbs = jnp.zeros((128, 4), F32)
        for g in gs:
            duv_ref[:, sls[g]] = (do[:, sls[g]] * (mixed[g] + bs_ref[:, g:g + 1]) * _dgelu(ua[:, sls[g]])).astype(BF16)
            dbs = dbs + jnp.where(lane4 == g, jnp.sum(dmix[g], axis=-1, keepdims=True), 0.0)
            _acc(dwc_ref.at[sls[g], :], jnp.where(tril, dwg[g], 0.0), i)
        _acc(dbs_ref, dbs, i)
        _acc(dlg_ref, jnp.sum(dvn * vhat, axis=0, keepdims=True), i)
        _acc(dlb_ref, jnp.sum(dvn, axis=0, keepdims=True), i)
        dvh = dvn * lgv
        dv = rstd * (dvh - jnp.mean(dvh, axis=-1, keepdims=True) - vhat * jnp.mean(dvh * vhat, axis=-1, keepdims=True))
        duv_ref[:, MIX:] = (dv * _dgelu(va)).astype(BF16)

    return _tok_call(body, name, S, SGU_T, [(proj, MIX, C_UA // MIX), (proj, MIX, C_VA // MIX), (dout, MIX, 0)],
                     [lg, lb, wc, bst], [(2 * MIX, BF16, C_UA // (2 * MIX), dproj)],
                     [((1, MIX), F32), ((1, MIX), F32), ((SGU_G * 128, 128), F32), ((128, 4), F32)])


def _rope_tables(positions):
    S = positions.shape[0]
    inv_freq = ROPE_THETA ** (-jnp.arange(0, ROPE_DIM, 2, dtype=F32) / ROPE_DIM)
    ang = positions.astype(F32)[:, None] * inv_freq
    c, s = jnp.cos(ang), jnp.sin(ang)
    c64 = jnp.concatenate([c, c, jnp.ones((S, SWA_HD - ROPE_DIM), F32)], axis=1)
    s64 = jnp.concatenate([-s, s, jnp.zeros((S, SWA_HD - ROPE_DIM), F32)], axis=1)
    return jnp.tile(c64, (1, 2)), jnp.tile(s64, (1, 2))


def _rope128(x, c, s):
    lane = lax.broadcasted_iota(jnp.int32, x.shape, 1) % SWA_HD
    swapped = jnp.where(lane < ROPE_DIM // 2, pltpu.roll(x, 128 - ROPE_DIM // 2, 1), pltpu.roll(x, ROPE_DIM // 2, 1))
    return x * c + swapped * s


def _rope_t128(y, c, s):
    ys = y * s
    lane = lax.broadcasted_iota(jnp.int32, y.shape, 1) % SWA_HD
    swapped = jnp.where(lane < ROPE_DIM // 2, pltpu.roll(ys, 128 - ROPE_DIM // 2, 1), pltpu.roll(ys, ROPE_DIM // 2, 1))
    return y * c + jnp.where(lane < ROPE_DIM, swapped, 0.0)


def _rope_fwd(proj, cos, sin, name):
    S = proj.shape[0]
    scale = SWA_HD ** -0.5

    def body(i, q_ref, k_ref, v_ref, c_ref, s_ref, qo_ref, ko_ref, vo_ref):
        c, s = c_ref[...], s_ref[...]
        for j in range(4):
            sl = slice(j * 128, (j + 1) * 128)
            qo_ref[:, sl] = (_rope128(q_ref[:, sl].astype(F32), c, s) * scale).astype(BF16)
        ko_ref[...] = _rope128(k_ref[...].astype(F32), c, s).astype(BF16)
        vo_ref[...] = v_ref[...].astype(BF16)

    return _tok_call(body, name, S, min(S, 512),
                     [(proj, MIX, C_QB // MIX), (proj, 128, C_KB // 128), (proj, 128, C_VB // 128), (cos, 128, 0),
                      (sin, 128, 0)], [], [(MIX, BF16), (128, BF16), (128, BF16)])


def _rope_bwd(dq, dk, dv, cos, sin, dproj, name):
    S = dq.shape[0]
    scale = SWA_HD ** -0.5
    width = C_SM - C_QB

    def body(i, dq_ref, dk_ref, dv_ref, c_ref, s_ref, o_ref):
        c, s = c_ref[...], s_ref[...]
        for j in range(4):
            sl = slice(j * 128, (j + 1) * 128)
            o_ref[:, sl] = _rope_t128(dq_ref[:, sl] * scale, c, s).astype(BF16)
        o_ref[:, C_KB - C_QB:C_VB - C_QB] = _rope_t128(dk_ref[...], c, s).astype(BF16)
        o_ref[:, C_VB - C_QB:] = dv_ref[...].astype(BF16)

    return _tok_call(body, name, S, min(S, 512),
                     [(dq, MIX, 0), (dk, 128, 0), (dv, 128, 0), (cos, 128, 0), (sin, 128, 0)], [],
                     [(width, BF16, C_QB // width, dproj)])[0]


def _swa_band(i, k_ref, v_ref):
    pstart = pl.multiple_of(jnp.maximum(i - 1, 0) * WINDOW, WINDOW)
    cstart = pl.multiple_of(i * WINDOW, WINDOW)
    kb = jnp.concatenate([k_ref[pl.ds(pstart, WINDOW), :], k_ref[pl.ds(cstart, WINDOW), :]], axis=0)
    vb = jnp.concatenate([v_ref[pl.ds(pstart, WINDOW), :], v_ref[pl.ds(cstart, WINDOW), :]], axis=0)
    qi = lax.broadcasted_iota(jnp.int32, (WINDOW, 2 * WINDOW), 0)
    sj = lax.broadcasted_iota(jnp.int32, (WINDOW, 2 * WINDOW), 1)
    mask = (sj > qi) & (sj <= qi + WINDOW) & ((i > 0) | (sj >= WINDOW))
    return kb, vb, mask, pstart, cstart


def _swa_probs(qs, kh, mask, sinks):
    logits = [jnp.where(mask, _dg(qh, kh, 1, 1), -1e30) for qh in qs]
    m = [jnp.maximum(jnp.max(l, axis=-1, keepdims=True), s) for l, s in zip(logits, sinks)]
    p = [jnp.exp(l - mm) for l, mm in zip(logits, m)]
    ps = [jnp.exp(s - mm) for s, mm in zip(sinks, m)]
    inv = [1.0 / (jnp.sum(pp, axis=-1, keepdims=True) + s) for pp, s in zip(p, ps)]
    return [pp * iv for pp, iv in zip(p, inv)], [s * iv for s, iv in zip(ps, inv)]


def _swa_fwd(q, k, v, sinks, name):
    S = q.shape[0]
    G = SWA_H // SWA_KV

    def body(i, q_ref, k_ref, v_ref, s_ref, o_ref):
        kb, vb, mask, _, _ = _swa_band(i, k_ref, v_ref)
        qv = q_ref[...]
        for kv in range(SWA_KV):
            ksl = slice(kv * SWA_HD, (kv + 1) * SWA_HD)
            heads = range(kv * G, (kv + 1) * G)
            pn, _ = _swa_probs([qv[:, h * SWA_HD:(h + 1) * SWA_HD] for h in heads], kb[:, ksl], mask,
                               [s_ref[0, h] for h in heads])
            outs = [_dg(p.astype(BF16), vb[:, ksl], 1, 0) for p in pn]
            for h, o in zip(heads, outs):
                o_ref[:, h * SWA_HD:(h + 1) * SWA_HD] = o.astype(BF16)

    return _tok_call(body, name, S, WINDOW, [(q, MIX, 0)], [k, v], [(MIX, BF16)], smem_in=[sinks])[0]


def _swa_bwd(q, k, v, sinks, dout, name):
    S = q.shape[0]

    def body(i, q_ref, do_ref, k_ref, v_ref, s_ref, dq_ref, dk_ref, dv_ref, ds_ref):
        kb, vb, mask, pstart, cstart = _swa_band(i, k_ref, v_ref)
        qv, dov = q_ref[...], do_ref[...]
        lane = lax.broadcasted_iota(jnp.int32, (1, 128), 1)
        dsink = jnp.zeros((1, 128), F32)
        dkb, dvb = [], []
        G = SWA_H // SWA_KV
        for kv in range(SWA_KV):
            ksl = slice(kv * SWA_HD, (kv + 1) * SWA_HD)
            heads = range(kv * G, (kv + 1) * G)
            qs = [qv[:, h * SWA_HD:(h + 1) * SWA_HD] for h in heads]
            dos = [dov[:, h * SWA_HD:(h + 1) * SWA_HD].astype(BF16) for h in heads]
            pn, psn = _swa_probs(qs, kb[:, ksl], mask, [s_ref[0, h] for h in heads])
            dp = [_dg(d, vb[:, ksl], 1, 1) for d in dos]
            delta = [jnp.sum(a * b, axis=-1, keepdims=True) for a, b in zip(dp, pn)]
            dsc = [(p * (a - d)).astype(BF16) for p, a, d in zip(pn, dp, delta)]
            dqs = [_dg(s, kb[:, ksl], 1, 0) for s in dsc]
            dks = [_dg(s, qh, 0, 0) for s, qh in zip(dsc, qs)]
            dvs = [_dg(p.astype(BF16), d, 0, 0) for p, d in zip(pn, dos)]
            for n_, h in enumerate(heads):
                dq_ref[:, h * SWA_HD:(h + 1) * SWA_HD] = dqs[n_]
                dsink = dsink + jnp.where(lane == h, -jnp.sum(psn[n_] * delta[n_], axis=0, keepdims=True), 0.0)
            dkb.append((dks[0] + dks[1]) + (dks[2] + dks[3]))
            dvb.append((dvs[0] + dvs[1]) + (dvs[2] + dvs[3]))
        dkb = jnp.concatenate(dkb, axis=1)
        dvb = jnp.concatenate(dvb, axis=1)

        @pl.when(i == 0)
        def _():
            dk_ref[...] = jnp.zeros_like(dk_ref)
            dv_ref[...] = jnp.zeros_like(dv_ref)

        dk_ref[pl.ds(pstart, WINDOW), :] += dkb[:WINDOW]
        dv_ref[pl.ds(pstart, WINDOW), :] += dvb[:WINDOW]
        dk_ref[pl.ds(cstart, WINDOW), :] += dkb[WINDOW:]
        dv_ref[pl.ds(cstart, WINDOW), :] += dvb[WINDOW:]
        _acc(ds_ref, dsink, i)

    return _tok_call(body, name, S, WINDOW, [(q, MIX, 0), (dout, MIX, 0)], [k, v], [(MIX, F32)],
                     [((S, 128), F32), ((S, 128), F32), ((1, 128), F32)], smem_in=[sinks])


def _shift_rows(xs, k):
    return xs if k == 0 else pltpu.roll(xs, k, 0)


def _dn_conv(x_ref, p_ref, w_ref, i):
    hr = p_ref.shape[0]
    halo = jnp.where(i > 0, p_ref[...].astype(F32), 0.0)
    xs = jnp.concatenate([halo, x_ref[...].astype(F32)], axis=0)
    sh = [_shift_rows(xs, DN_CONV - 1 - t)[hr:] for t in range(DN_CONV)]
    pre = sh[0] * w_ref[0:1, :]
    for t in range(1, DN_CONV):
        pre = pre + sh[t] * w_ref[t:t + 1, :]
    return pre, sh


def _dn_gates(sm, alog, dtb):
    lane = lax.broadcasted_iota(jnp.int32, sm.shape, 1)
    return jnp.where(lane < DN_H, _sigmoid(sm), -jnp.exp(alog) * _softplus(sm + dtb))


def _dn_pre_fwd(proj, conv_w, alog_l, dtb_l, name):
    S = proj.shape[0]
    scale = DN_HD ** -0.5

    def body(i, x_ref, sm_ref, p_ref, w_ref, al_ref, db_ref, q_ref, k_ref, v_ref, bg_ref):
        pre, _ = _dn_conv(x_ref, p_ref, w_ref, i)
        a = _silu(pre)
        for h in range(DN_H):
            sl = slice(h * DN_HD, (h + 1) * DN_HD)
            qh, kh = a[:, sl], a[:, MIX + h * DN_HD:MIX + (h + 1) * DN_HD]
            q_ref[:, sl] = qh * (lax.rsqrt(jnp.sum(qh * qh, axis=-1, keepdims=True) + EPS) * scale)
            k_ref[:, sl] = kh * lax.rsqrt(jnp.sum(kh * kh, axis=-1, keepdims=True) + EPS)
        v_ref[...] = a[:, 2 * MIX:]
        bg_ref[...] = _dn_gates(sm_ref[...].astype(F32), al_ref[...], db_ref[...])

    TB = min(S, 256)
    return _tok_call(body, name, S, TB, [(proj, 3 * MIX, C_QKV // (3 * MIX)), (proj, 128, C_SM // 128)],
                     [conv_w, alog_l, dtb_l], [(MIX, F32), (MIX, F32), (MIX, F32), (128, F32)],
                     prev_in=[(proj, 3 * MIX, C_QKV // (3 * MIX))])


def _dn_pre_bwd1(proj, conv_w, alog_l, dtb_l, dq, dk, dv, dbg, dproj, name):
    S = proj.shape[0]
    scale = DN_HD ** -0.5

    def body(i, x_ref, sm_ref, dq_ref, dk_ref, dv_ref, dbg_ref, p_ref, w_ref, al_ref, db_ref, dpre_ref, dsm_ref,
             dw_ref, dal_ref, ddb_ref):
        pre, sh = _dn_conv(x_ref, p_ref, w_ref, i)
        a = _silu(pre)
        da_parts = []
        for part, (g_ref, sc) in enumerate(((dq_ref, scale), (dk_ref, 1.0))):
            for h in range(DN_H):
                xh = a[:, part * MIX + h * DN_HD:part * MIX + (h + 1) * DN_HD]
                rs = lax.rsqrt(jnp.sum(xh * xh, axis=-1, keepdims=True) + EPS)
                y = xh * rs
                dy = g_ref[:, h * DN_HD:(h + 1) * DN_HD] * sc
                da_parts.append(rs * (dy - y * jnp.sum(dy * y, axis=-1, keepdims=True)))
        da_parts.append(dv_ref[...])
        dpre = jnp.concatenate(da_parts, axis=1) * _dsilu(pre)
        dpre_ref[...] = dpre
        dw = jnp.concatenate([jnp.sum(dpre * sh[t], axis=0, keepdims=True) for t in range(DN_CONV)], axis=0)
        _acc(dw_ref, dw, i)
        sm, al, db, dbg_v = sm_ref[...].astype(F32), al_ref[...], db_ref[...], dbg_ref[...]
        lane = lax.broadcasted_iota(jnp.int32, sm.shape, 1)
        sg = _sigmoid(sm)
        gneg = -jnp.exp(al)
        is_g = (lane >= DN_H) & (lane < 2 * DN_H)
        d_al = jnp.where(is_g, dbg_v * gneg * _sigmoid(sm + db), 0.0)
        dsm_ref[...] = jnp.where(lane < DN_H, dbg_v * sg * (1.0 - sg), d_al).astype(BF16)
        _acc(ddb_ref, jnp.sum(d_al, axis=0, keepdims=True), i)
        _acc(dal_ref, jnp.sum(jnp.where(is_g, dbg_v * gneg * _softplus(sm + db), 0.0), axis=0, keepdims=True), i)

    TB = min(S, 256)
    return _tok_call(body, name, S, TB,
                     [(proj, 3 * MIX, C_QKV // (3 * MIX)), (proj, 128, C_SM // 128), (dq, MIX, 0), (dk, MIX, 0),
                      (dv, MIX, 0), (dbg, 128, 0)], [conv_w, alog_l, dtb_l],
                     [(3 * MIX, F32), (128, BF16, C_SM // 128, dproj)],
                     [((DN_CONV, 3 * MIX), F32), ((1, 128), F32), ((1, 128), F32)],
                     prev_in=[(proj, 3 * MIX, C_QKV // (3 * MIX))])


def _dn_pre_bwd2(dpre, conv_w, dproj, name):
    S = dpre.shape[0]
    TB = min(S, 256)
    nb = S // TB

    def body(i, d_ref, n_ref, w_ref, o_ref):
        halo = jnp.where(i < nb - 1, n_ref[...], 0.0)
        ds = jnp.concatenate([d_ref[...], halo], axis=0)
        out = ds[:TB] * w_ref[DN_CONV - 1:DN_CONV, :]
        for t in range(DN_CONV - 1):
            k = DN_CONV - 1 - t
            out = out + pltpu.roll(ds, TB + 8 - k, 0)[:TB] * w_ref[t:t + 1, :]
        o_ref[...] = out.astype(BF16)

    return _tok_call(body, name, S, TB, [(dpre, 3 * MIX, 0)], [conv_w],
                     [(3 * MIX, BF16, C_QKV // (3 * MIX), dproj)], next_in=[(dpre, 3 * MIX, 0)])[0]


def _dn_decay_terms(bgs, heads):
    C = DN_C
    ri = lax.broadcasted_iota(jnp.int32, (C, C), 0)
    ci = lax.broadcasted_iota(jnp.int32, (C, C), 1)
    tril, eye = ri >= ci, ri == ci
    beta = [b[:, h:h + 1] for b, h in zip(bgs, heads)]
    gcol = _dg_exact_lhs_many(tril, [jnp.broadcast_to(b[:, DN_H + h:DN_H + h + 1], (C, C))
                                     for b, h in zip(bgs, heads)], 1, 0)
    grow = [jnp.sum(jnp.where(eye, g, 0.0), axis=0, keepdims=True) for g in gcol]
    decay = [jnp.exp(jnp.where(tril, g - r, -1e30)) for g, r in zip(gcol, grow)]
    e_gc = [jnp.exp(g[:, 0:1]) for g in gcol]
    e_kd = [jnp.exp(g[C - 1:C, 0:1] - g[:, 0:1]) for g in gcol]
    cdec = [jnp.exp(g[C - 1:C, 0:1]) for g in gcol]
    return beta, decay, e_gc, e_kd, cdec


def _dn_nb(S):
    return 4 if S % (4 * DN_C) == 0 else 1


def _dn_prep_fwd(q, k, v, bg, name):
    S = q.shape[0]
    C, NB = DN_C, _dn_nb(S)
    TB = NB * C

    def kern(q_ref, k_ref, v_ref, bg_ref, t_ref, uw_ref, at_ref, qd_ref, kd_ref, dec_ref):
        lane = lax.broadcasted_iota(jnp.int32, (C, 128), 1)
        ri = lax.broadcasted_iota(jnp.int32, (C, C), 0)
        ci = lax.broadcasted_iota(jnp.int32, (C, C), 1)
        tril, eye = ri >= ci, ri == ci
        chains = [(cb, h) for cb in range(NB) for h in range(DN_H)]
        rows = lambda cb: slice(cb * C, (cb + 1) * C)
        head = lambda h: slice(h * DN_HD, (h + 1) * DN_HD)
        beta, decay, e_gc, e_kd, cdec = _dn_decay_terms([bg_ref[rows(cb), :] for cb, _ in chains],
                                                        [h for _, h in chains])
        qs = [q_ref[rows(cb), head(h)] for cb, h in chains]
        ks = [k_ref[rows(cb), head(h)] for cb, h in chains]
        kb = [kh * b for kh, b in zip(ks, beta)]
        x = [-jnp.where(ri > ci, _mm_nt(a, kh) * d, 0.0) for a, kh, d in zip(kb, ks, decay)]
        tm = [jnp.where(eye, 1.0, 0.0) + xi for xi in x]
        p = x
        p = _dg3_many(p, p, 1, 0)
        for it in range(5):
            if it == 4:
                tm = [t + tp for t, tp in zip(tm, _dg3_many(tm, p, 1, 0))]
                break
            both = _dg3_many([jnp.concatenate([t, pp], axis=0) for t, pp in zip(tm, p)], p, 1, 0)
            tm = [t + b[:C] for t, b in zip(tm, both)]
            p = [b[C:] for b in both]
        rhs = [jnp.concatenate([v_ref[rows(cb), head(h)] * b, a * e], axis=1)
               for (cb, h), b, a, e in zip(chains, beta, kb, e_gc)]
        sol = _dg3_many(tm, rhs, 1, 0)
        attn = [_mm_nt(qh, kh) * d for qh, kh, d in zip(qs, ks, decay)]
        for n_, (cb, h) in enumerate(chains):
            rs, sl, hc = rows(cb), head(h), slice(h * C, (h + 1) * C)
            t_ref[rs, hc] = tm[n_]
            uw_ref[rs, sl] = sol[n_][:, :DN_HD]
            uw_ref[rs, MIX + h * DN_HD:MIX + (h + 1) * DN_HD] = sol[n_][:, DN_HD:]
            at_ref[rs, hc] = attn[n_]
            qd_ref[rs, sl] = (qs[n_] * e_gc[n_]).astype(BF16)
            kd_ref[rs, sl] = (ks[n_] * e_kd[n_]).astype(BF16)
        for cb in range(NB):
            dec = jnp.zeros((C, 128), F32)
            for h in range(DN_H):
                dec = dec + jnp.where(lane == h, cdec[cb * DN_H + h], 0.0)
            dec_ref[rows(cb), :] = dec

    tok = lambda w: pl.BlockSpec((TB, w), lambda i: (i, 0))
    return pl.pallas_call(
        kern, name=name, grid=(S // TB,), in_specs=[tok(MIX), tok(MIX), tok(MIX), tok(128)],
        out_specs=[tok(DN_H * C), tok(2 * MIX), tok(DN_H * C), tok(MIX), tok(MIX), tok(128)],
        out_shape=[jax.ShapeDtypeStruct((S, DN_H * C), F32), jax.ShapeDtypeStruct((S, 2 * MIX), F32),
                   jax.ShapeDtypeStruct((S, DN_H * C), F32), jax.ShapeDtypeStruct((S, MIX), BF16),
                   jax.ShapeDtypeStruct((S, MIX), BF16), jax.ShapeDtypeStruct((S, 128), F32)],
        compiler_params=_cparams(("parallel",)),
    )(q, k, v, bg)


def _dn_scan_fwd(uw, at, qd, kd, dec, name):
    S = uw.shape[0]
    C, NB = DN_C, _dn_nb(S)
    TB = NB * C
    SR = DN_H * DN_HD

    def kern(uw_ref, at_ref, qd_ref, kd_ref, dec_ref, o_ref, vn_ref, st_ref, state):
        @pl.when(pl.program_id(0) == 0)
        def _():
            state[...] = jnp.zeros_like(state)

        for cb in range(NB):
            rs = slice(cb * C, (cb + 1) * C)
            hs = range(DN_H)
            sls = [slice(h * DN_HD, (h + 1) * DN_HD) for h in hs]
            s_in = [state[sl, :] for sl in sls]
            ws = [_mm(uw_ref[rs, MIX + h * DN_HD:MIX + (h + 1) * DN_HD], s_in[h]) for h in hs]
            os_ = [_mm(qd_ref[rs, sls[h]], s_in[h]) for h in hs]
            vnew = [uw_ref[rs, sls[h]] - ws[h] for h in hs]
            oa = [_mm(at_ref[rs, h * C:(h + 1) * C], vnew[h]) for h in hs]
            kv = [_mm_tn(kd_ref[rs, sls[h]], vnew[h]) for h in hs]
            for h in hs:
                o_ref[rs, sls[h]] = os_[h] + oa[h]
                state[sls[h], :] = s_in[h] * dec_ref[cb * C:cb * C + 1, h:h + 1] + kv[h]
                st_ref[cb * SR + h * DN_HD:cb * SR + (h + 1) * DN_HD, :] = s_in[h].astype(BF16)
                vn_ref[rs, sls[h]] = vnew[h].astype(BF16)

    tok = lambda w: pl.BlockSpec((TB, w), lambda i: (i, 0))
    return pl.pallas_call(
        kern, name=name, grid=(S // TB,), in_specs=[tok(2 * MIX), tok(DN_H * C), tok(MIX), tok(MIX), tok(128)],
        out_specs=[tok(MIX), tok(MIX), pl.BlockSpec((NB * SR, DN_HD), lambda i: (i, 0))],
        out_shape=[jax.ShapeDtypeStruct((S, MIX), F32), jax.ShapeDtypeStruct((S, MIX), BF16),
                   jax.ShapeDtypeStruct((S // C * SR, DN_HD), BF16)],
        scratch_shapes=[pltpu.VMEM((SR, DN_HD), F32)],
        compiler_params=_cparams(("arbitrary",)),
    )(uw, at, qd, kd, dec)


def _dn_core_fwd(q, k, v, bg, name):
    tm, uw, at, qd, kd, dec = _dn_prep_fwd(q, k, v, bg, name + "_prep")
    o, vn, st = _dn_scan_fwd(uw, at, qd, kd, dec, name + "_scan")
    return o, dict(tm=tm, uw=uw, at=at, qd=qd, kd=kd, dec=dec, vn=vn, st=st)


def _dn_scan_bwd(sv, do, name):
    S = do.shape[0]
    C, NB = DN_C, _dn_nb(S)
    TB = NB * C
    SR = DN_H * DN_HD
    nb = S // TB

    def kern(do_ref, uw_ref, at_ref, qd_ref, kd_ref, dec_ref, vn_ref, st_ref, dvn_ref, dw_ref, dkd_ref, dc_ref, dstate):
        @pl.when(pl.program_id(0) == 0)
        def _():
            dstate[...] = jnp.zeros_like(dstate)

        lane = lax.broadcasted_iota(jnp.int32, (C, 128), 1)
        for cb in reversed(range(NB)):
            rs = slice(cb * C, (cb + 1) * C)
            dcrow = jnp.zeros((C, 128), F32)
            for h in range(DN_H):
                sl = slice(h * DN_HD, (h + 1) * DN_HD)
                doh, ds_o = do_ref[rs, sl], dstate[sl, :]
                s_in = st_ref[cb * SR + h * DN_HD:cb * SR + (h + 1) * DN_HD, :]
                d_vnew = _mm_tn(at_ref[rs, h * C:(h + 1) * C], doh) + _mm(kd_ref[rs, sl], ds_o)
                dvn_ref[rs, sl] = d_vnew
                dw_ref[rs, sl] = -_mm_nt(d_vnew, s_in)
                dkd_ref[rs, sl] = _mm_nt(vn_ref[rs, sl], ds_o)
                d_c = jnp.sum(jnp.sum(ds_o * s_in.astype(F32), axis=1, keepdims=True), axis=0, keepdims=True)
                dcrow = dcrow + jnp.where(lane == h, d_c, 0.0)
                dstate[sl, :] = (ds_o * dec_ref[cb * C:cb * C + 1, h:h + 1] + _mm_tn(qd_ref[rs, sl], doh)
                                 - _mm_tn(uw_ref[rs, MIX + h * DN_HD:MIX + (h + 1) * DN_HD], d_vnew))
            dc_ref[rs, :] = dcrow

    tok = lambda w: pl.BlockSpec((TB, w), lambda i: (nb - 1 - i, 0))
    return pl.pallas_call(
        kern, name=name, grid=(nb,),
        in_specs=[tok(MIX), tok(2 * MIX), tok(DN_H * C), tok(MIX), tok(MIX), tok(128), tok(MIX),
                  pl.BlockSpec((NB * SR, DN_HD), lambda i: (nb - 1 - i, 0))],
        out_specs=[tok(MIX), tok(MIX), tok(MIX), tok(128)],
        out_shape=[jax.ShapeDtypeStruct((S, MIX), F32)] * 3 + [jax.ShapeDtypeStruct((S, 128), F32)],
        scratch_shapes=[pltpu.VMEM((SR, DN_HD), F32)],
        compiler_params=_cparams(("arbitrary",)),
    )(do, sv["uw"], sv["at"], sv["qd"], sv["kd"], sv["dec"], sv["vn"], sv["st"])


def _dn_chunk_bwd(q, k, v, bg, sv, do, dvn, dw, dkd, dc, name):
    S = q.shape[0]
    C, NB = DN_C, _dn_nb(S)
    TB = NB * C
    SR = DN_H * DN_HD

    def kern(q_ref, k_ref, v_ref, bg_ref, t_ref, uw_ref, vn_ref, st_ref, do_ref, dvn_ref, dw_ref, dkd_ref, dc_ref,
             dq_ref, dk_ref, dv_ref, dbg_ref):
        lane = lax.broadcasted_iota(jnp.int32, (C, 128), 1)
        ri = lax.broadcasted_iota(jnp.int32, (C, C), 0)
        ci = lax.broadcasted_iota(jnp.int32, (C, C), 1)
        tril, eye, last = ri >= ci, ri == ci, ri[:, 0:1] == C - 1
        chains = [(cb, h) for cb in range(NB) for h in range(DN_H)]
        each = lambda f, *ls: [f(*a) for a in zip(*ls)]
        rsum = lambda t: jnp.sum(t, axis=-1, keepdims=True)
        rows = lambda cb: slice(cb * C, (cb + 1) * C)
        head = lambda h: slice(h * DN_HD, (h + 1) * DN_HD)
        tok = lambda ref: [ref[rows(cb), head(h)] for cb, h in chains]
        beta, decay, e_gc, e_kd, cdec = _dn_decay_terms([bg_ref[rows(cb), :] for cb, _ in chains],
                                                        [h for _, h in chains])
        qs, ks, vs, dos, vnew, d_kd = tok(q_ref), tok(k_ref), tok(v_ref), tok(do_ref), tok(vn_ref), tok(dkd_ref)
        s_in = [st_ref[cb * SR + h * DN_HD:cb * SR + (h + 1) * DN_HD, :] for cb, h in chains]
        d_c = [dc_ref[cb * C:cb * C + 1, h:h + 1] for cb, h in chains]
        kb = each(lambda a, b: a * b, ks, beta)
        kk = each(_mm_nt, kb, ks)
        attn = each(lambda a, b, d: _mm_nt(a, b) * d, qs, ks, decay)
        d_qd = each(_mm_nt, dos, s_in)
        d_attn = each(_mm_nt, dos, vnew)
        d_sol = [jnp.concatenate([dvn_ref[rows(cb), head(h)], dw_ref[rows(cb), head(h)]], axis=1) for cb, h in chains]
        sol = [jnp.concatenate([uw_ref[rows(cb), head(h)], uw_ref[rows(cb), MIX + h * DN_HD:MIX + (h + 1) * DN_HD]],
                               axis=1) for cb, h in chains]
        d_rhs = _dg3_many([t_ref[rows(cb), h * C:(h + 1) * C] for cb, h in chains], d_sol, 0, 0)
        d_a = _dg3_many(d_rhs, sol, 1, 1)
        d_kk = each(lambda a, d: jnp.where(ri > ci, -a, 0.0) * d, d_a, decay)
        d_qk = each(lambda a, d: a * d, d_attn, decay)
        dm = each(lambda a, b, c_, d: a * b + c_ * d, d_kk, kk, d_attn, attn)
        d_vb = [t[:, :DN_HD] for t in d_rhs]
        dz = [t[:, DN_HD:] for t in d_rhs]
        d_kb = each(lambda z, e, a, kh: z * e + _mm(a, kh), dz, e_gc, d_kk, ks)
        d_k = each(lambda a, b, c_, q: _mm_tn(a, b) + _mm_tn(c_, q), d_kk, kb, d_qk, qs)
        d_q = each(lambda a, kh, b, e: _mm(a, kh) + b * e, d_qk, ks, d_qd, e_gc)
        t_kd = each(lambda a, kh, e: rsum(a * kh * e), d_kd, ks, e_kd)
        d_gl = each(lambda t, c_, cd: jnp.sum(t, axis=0, keepdims=True) + c_ * cd, t_kd, d_c, cdec)
        d_gc = each(lambda z, a, e, m, b, q, t, gl:
                    rsum(z * a) * e + rsum(m) - rsum(jnp.where(eye, jnp.sum(m, axis=0, keepdims=True), 0.0))
                    + rsum(b * q) * e - t + jnp.where(last, gl, 0.0),
                    dz, kb, e_gc, dm, d_qd, qs, t_kd, d_gl)
        d_g = _dg_exact_lhs_many(ri <= ci, [jnp.broadcast_to(t, (C, 128)) for t in d_gc], 1, 0)
        d_beta = each(lambda a, v_, b, kh: rsum(a * v_) + rsum(b * kh), d_vb, vs, d_kb, ks)
        for n_, (cb, h) in enumerate(chains):
            dq_ref[rows(cb), head(h)] = d_q[n_]
            dk_ref[rows(cb), head(h)] = d_k[n_] + d_kd[n_] * e_kd[n_] + d_kb[n_] * beta[n_]
            dv_ref[rows(cb), head(h)] = d_vb[n_] * beta[n_]
        for cb in range(NB):
            dbg = jnp.zeros((C, 128), F32)
            for h in range(DN_H):
                n_ = cb * DN_H + h
                dbg = dbg + jnp.where(lane == h, d_beta[n_], 0.0) + jnp.where(lane == DN_H + h, d_g[n_], 0.0)
            dbg_ref[rows(cb), :] = dbg

    tok = lambda w: pl.BlockSpec((TB, w), lambda i: (i, 0))
    return pl.pallas_call(
        kern, name=name, grid=(S // TB,),
        in_specs=[tok(MIX), tok(MIX), tok(MIX), tok(128), tok(DN_H * C), tok(2 * MIX), tok(MIX),
                  pl.BlockSpec((NB * SR, DN_HD), lambda i: (i, 0)), tok(MIX), tok(MIX), tok(MIX), tok(MIX), tok(128)],
        out_specs=[tok(MIX), tok(MIX), tok(MIX), tok(128)],
        out_shape=[jax.ShapeDtypeStruct((S, MIX), F32)] * 3 + [jax.ShapeDtypeStruct((S, 128), F32)],
        compiler_params=_cparams(("parallel",)),
    )(q, k, v, bg, sv["tm"], sv["uw"], sv["vn"], sv["st"], do, dvn, dw, dkd, dc)


def _dn_core_bwd(q, k, v, bg, sv, do, name):
    dvn, dw, dkd, dc = _dn_scan_bwd(sv, do, name + "_scan")
    return _dn_chunk_bwd(q, k, v, bg, sv, do, dvn, dw, dkd, dc, name + "_chunk")


def _dn_post_fwd(o, proj, ng, name):
    S = o.shape[0]

    def body(i, o_ref, z_ref, g_ref, out_ref):
        gv = g_ref[...]
        for h in range(DN_H):
            sl = slice(h * DN_HD, (h + 1) * DN_HD)
            oh = o_ref[:, sl]
            r = lax.rsqrt(jnp.mean(oh * oh, axis=-1, keepdims=True) + EPS)
            out_ref[:, sl] = (oh * r * gv * _silu(z_ref[:, sl].astype(F32))).astype(BF16)

    return _tok_call(body, name, S, min(S, 512), [(o, MIX, 0), (proj, MIX, C_ZC // MIX)], [ng], [(MIX, BF16)])[0]


def _dn_post_bwd(o, proj, ng, dout, dproj, name):
    S = o.shape[0]

    def body(i, o_ref, z_ref, do_ref, g_ref, dov_ref, dz_ref, dg_ref):
        gv = g_ref[...]
        dg = jnp.zeros((1, DN_HD), F32)
        for h in range(DN_H):
            sl = slice(h * DN_HD, (h + 1) * DN_HD)
            oh, zh, dh = o_ref[:, sl], z_ref[:, sl].astype(F32), do_ref[:, sl].astype(F32)
            r = lax.rsqrt(jnp.mean(oh * oh, axis=-1, keepdims=True) + EPS)
            dz_ref[:, sl] = (dh * oh * r * gv * _dsilu(zh)).astype(BF16)
            dx, dgh = _rms_bwd_vals(oh, gv, dh * _silu(zh))
            dov_ref[:, sl] = dx
            dg = dg + dgh
        _acc(dg_ref, dg, i)

    return _tok_call(body, name, S, min(S, 512), [(o, MIX, 0), (proj, MIX, C_ZC // MIX), (dout, MIX, 0)], [ng],
                     [(MIX, F32), (MIX, BF16, C_ZC // MIX, dproj)], [((1, DN_HD), F32)])


def _layer_params(w, big, l):
    lane = jnp.arange(128)
    is_g = (lane >= DN_H) & (lane < 2 * DN_H)
    spread = lambda t: jnp.where(is_g, jnp.tile(t, 128 // DN_H), 0.0).reshape(1, 128)
    tril = jnp.tril(jnp.ones((SGU_T, SGU_T), bool))
    return dict(
        win=big["w_in"], rest=big["rest"], conv=w["dn_conv_w"][l], attn_norm=w["attn_norm"][l].reshape(1, -1), ffn_norm=w["ffn_norm"][l].reshape(1, -1),
        lg=w["sgu_ln_g"][l].reshape(1, -1), lb=w["sgu_ln_b"][l].reshape(1, -1),
        wc=jnp.where(tril, w["sgu_w"][l], 0.0).reshape(SGU_G * SGU_T, SGU_T), bst=w["sgu_b"][l].T,
        sinks=w["attn_sinks"][l].reshape(1, -1), alog=spread(w["dn_a_log"][l]), dtb=spread(w["dn_dt_bias"][l]),
        ng=w["dn_norm"][l].reshape(1, -1))


def _layer_fwd(x, p, cos, sin, l):
    n = lambda s: f"l{l}_{s}"
    h = _rms_fwd(x, p["attn_norm"], n("rms1"))
    if callable(p["win"]):
        p["win"] = p["win"](h)
    proj = _matmul(h, p["win"], out_dtype=BF16, name=n("mm_in"))
    out_a = _sgu_fwd(proj, p["lg"], p["lb"], p["wc"], p["bst"], n("sgu_fwd"))
    qr, kr, vr = _rope_fwd(proj, cos, sin, n("rope_fwd"))
    out_b = _swa_fwd(qr, kr, vr, p["sinks"], n("swa_fwd"))
    q, k, v, bg = _dn_pre_fwd(proj, p["conv"], p["alog"], p["dtb"], n("dn_pre_fwd"))
    o, dn = _dn_core_fwd(q, k, v, bg, n("dn_core_fwd"))
    out_c = _dn_post_fwd(o, proj, p["ng"], n("dn_post_fwd"))
    outs = (out_a, out_b, out_c)
    rest = p.pop("rest")(out_c)
    p.update(wb=rest["w_branch"], wout=rest["w_out"], wgu=rest["w_gate_up"], wdown=rest["w_down"])
    bds = [_matmul(outs[j], p["wb"][j], out_dtype=BF16, name=n(f"mm_branch{j}")) for j in range(3)]
    merged = _merge_fwd(proj, bds, n("merge_fwd"))
    x1 = _matmul(merged, p["wout"], add=x, name=n("mm_out"))
    h2 = _rms_fwd(x1, p["ffn_norm"], n("rms2"))
    gu = _matmul(h2, p["wgu"], out_dtype=BF16, name=n("mm_gu"))
    act = _swiglu_fwd(gu, n("swiglu_fwd"))
    x2 = _matmul(act, p["wdown"], add=x1, name=n("mm_down"))
    saved = dict(x=x, h=h, proj=proj, outs=outs, qr=qr, kr=kr, vr=vr, q=q, k=k, v=v, bg=bg, o=o, dn=dn, bds=bds,
                 merged=merged, x1=x1, h2=h2, gu=gu, act=act)
    return x2, saved


def _layer_bwd(dx2, s, p, cos, sin, l, early=None):
    n = lambda t: f"l{l}_{t}"
    proj = s["proj"]
    g = {}
    g["w_down"] = _matmul(s["act"], dx2, ta=True, out_dtype=BF16, name=n("wg_down"))
    dact = _matmul(dx2, p["wdown"], tb=True, out_dtype=BF16, name=n("dg_down"))
    dgu = _swiglu_bwd(s["gu"], dact, n("swiglu_bwd"))
    g["w_gate_up"] = _matmul(s["h2"], dgu, ta=True, out_dtype=BF16, name=n("wg_gu"))
    dh2 = _matmul(dgu, p["wgu"], tb=True, name=n("dg_gu"))
    dx1, g["ffn_norm"] = _rms_bwd_add(s["x1"], p["ffn_norm"], dh2, dx2, n("rms2_bwd"))
    g["w_out"] = _matmul(s["merged"], dx1, ta=True, out_dtype=BF16, name=n("wg_out"))
    dm = _matmul(dx1, p["wout"], tb=True, name=n("dg_out"))
    dbd0, dbd1, dbd2, dproj = _merge_bwd(proj, s["bds"], dm, n("merge_bwd"))
    dbds = (dbd0, dbd1, dbd2)
    g["w_branch"] = jnp.stack([_matmul(s["outs"][j], dbds[j], ta=True, out_dtype=BF16, name=n(f"wg_branch{j}"))
                               for j in range(3)])
    douts = [_matmul(dbds[j], p["wb"][j], tb=True, name=n(f"dg_branch{j}")) for j in range(3)]
    lg = p["lg"]
    if early is not None:
        token = early({k: g.pop(k) for k in ("w_down", "w_gate_up", "w_out", "w_branch")})
        lg = lg if token is None else lg + token[0, 0]
    dproj, g["sgu_ln_g"], g["sgu_ln_b"], dwc, dbs = _sgu_bwd(proj, lg, p["lb"], p["wc"], p["bst"], douts[0], dproj,
                                                             n("sgu_bwd"))
    g["sgu_w"] = dwc.reshape(SGU_G, SGU_T, SGU_T)
    g["sgu_b"] = dbs.T
    dqr, dkr, dvr, dsink = _swa_bwd(s["qr"], s["kr"], s["vr"], p["sinks"], douts[1], n("swa_bwd"))
    g["attn_sinks"] = dsink[0, :SWA_H]
    dproj = _rope_bwd(dqr, dkr, dvr, cos, sin, dproj, n("rope_bwd"))
    do, dproj, dng = _dn_post_bwd(s["o"], proj, p["ng"], douts[2], dproj, n("dn_post_bwd"))
    g["dn_norm"] = dng[0]
    dq, dk, dv, dbg = _dn_core_bwd(s["q"], s["k"], s["v"], s["bg"], s["dn"], do, n("dn_core_bwd"))
    dpre, dproj, g["dn_conv_w"], dal, ddb = _dn_pre_bwd1(proj, p["conv"], p["alog"], p["dtb"], dq, dk, dv, dbg, dproj,
                                                         n("dn_pre_bwd1"))
    g["dn_a_log"] = dal[0, DN_H:2 * DN_H]
    g["dn_dt_bias"] = ddb[0, DN_H:2 * DN_H]
    dproj = _dn_pre_bwd2(dpre, p["conv"], dproj, n("dn_pre_bwd2"))
    g["w_in"] = _matmul(s["h"], dproj, ta=True, out_dtype=BF16, name=n("wg_in"))
    attn_norm = p["attn_norm"]
    if early is not None:
        token = early({"w_in": g.pop("w_in")})
        attn_norm = attn_norm if token is None else attn_norm + token[0, 0]
    dh = _matmul(dproj, p["win"], tb=True, name=n("dg_in"))
    dx, g["attn_norm"] = _rms_bwd_add(s["x"], attn_norm, dh, dx1, n("rms1_bwd"))
    g["attn_norm"], g["ffn_norm"] = g["attn_norm"][0], g["ffn_norm"][0]
    g["sgu_ln_g"], g["sgu_ln_b"] = g["sgu_ln_g"][0], g["sgu_ln_b"][0]
    return dx, g


def _local_step(x, positions, target, w, big_of_layer, on_grads):
    cos, sin = _rope_tables(positions)
    params, saves, xs = [], [], x
    for l in range(DEPTH):
        params.append(_layer_params(w, big_of_layer(l, xs), l))
        xs, sv = _layer_fwd(xs, params[l], cos, sin, l)
        saves.append(sv)
    dx, loss_row, dgf = _final_loss(xs, w["final_norm"].reshape(1, -1), target)
    grads = [None] * DEPTH
    for l in reversed(range(DEPTH)):
        early = functools.partial(on_grads, l) if l == 0 else None
        dx, grads[l] = _layer_bwd(dx, saves[l], params[l], cos, sin, l, early)
        left = {k: grads[l].pop(k) for k in BIG if k in grads[l]}
        token = on_grads(l, left) if left else None
        if token is not None and l > 0:
            params[l - 1] = dict(params[l - 1], ffn_norm=params[l - 1]["ffn_norm"] + token[0, 0])
    stacked = {k: jnp.stack([grads[l][k] for l in range(DEPTH)]) for k in grads[0]}
    stacked["final_norm"] = dgf[0]
    return loss_row[0, 0], dx, stacked


MESH = pl.DeviceIdType.MESH
HBM_SPEC = pl.BlockSpec(memory_space=pltpu.HBM)
VMEM_SPEC = pl.BlockSpec(memory_space=pltpu.VMEM)
N_CHIPS = 4
FLIPS = tuple((fx, fy, fc) for fx in (0, 1) for fy in (0, 1) for fc in (0, 1))[1:]
BIG = ("w_in", "w_branch", "w_out", "w_gate_up", "w_down")
BIG_SPEC = {
    "w_in": dict(rows=1024, cols=1792, axis=1, keep=1730, down=8),
    "w_branch": dict(rows=1536, cols=256, axis=1, keep=256, down=2),
    "w_out": dict(rows=256, cols=1024, axis=0, keep=1024, down=1),
    "w_gate_up": dict(rows=1024, cols=1408, axis=1, keep=1408, down=8),
    "w_down": dict(rows=704, cols=1024, axis=0, keep=1024, down=4),
}
CONV_ROWS, CONV_COLS = DEPTH * DN_CONV, 3 * MIX // N_CHIPS


def _full_shape(k):
    sp = BIG_SPEC[k]
    return (sp["rows"], N_CHIPS * sp["cols"]) if sp["axis"] == 1 else (N_CHIPS * sp["rows"], sp["cols"])


def _me():
    return lax.axis_index("x"), lax.axis_index("y"), lax.axis_index("c")


def _peer(x, y, c, flip):
    fx, fy, fc = flip
    return (1 - x if fx else x, 1 - y if fy else y, 1 - c if fc else c)


class _Copies:
    def __init__(self, send_sems, recv_sems):
        self.send_sems, self.recv_sems, self.k, self.sent, self.landing = send_sems, recv_sems, 0, [], []

    def _copy(self, k, src, dst, to):
        return pltpu.make_async_remote_copy(src_ref=src, dst_ref=dst, send_sem=self.send_sems.at[k],
                                            recv_sem=self.recv_sems.at[k], device_id=to, device_id_type=MESH)

    def send(self, src, dst, to, lands):
        k = self.k
        self.k += 1
        cp = self._copy(k, src, dst, to)
        cp.start()
        self.sent.append(cp)
        self.landing.append(self._copy(k, lands, lands, to))
        return k

    def wait_landed(self, k):
        self.landing[k].wait_recv()

    def finish(self, landed=()):
        for k, cp in enumerate(self.landing):
            if k not in landed:
                cp.wait_recv()
        for cp in self.sent:
            cp.wait_send()


def _place_shard(shard, k, chip, layer, name):
    sp = BIG_SPEC[k]
    rows, cols, keep = sp["rows"], sp["cols"], sp["keep"]
    tr = _pick(rows, (256, 64))
    nb = rows // tr
    if sp["axis"] == 1:
        out_spec = pl.BlockSpec((tr, cols), lambda i, ch: (i, ch[0]))
    else:
        out_spec = pl.BlockSpec((tr, cols), lambda i, ch: (ch[0] * nb + i, 0))

    def kern(ch_ref, x_ref, o_ref):
        v = x_ref[0].astype(BF16)
        if keep == cols:
            o_ref[...] = v
        else:
            o_ref[:, :keep] = v
            o_ref[:, keep:] = jnp.zeros((tr, cols - keep), BF16)

    return pl.pallas_call(
        kern, name=name, out_shape=jax.ShapeDtypeStruct(_full_shape(k), BF16),
        grid_spec=pltpu.PrefetchScalarGridSpec(
            num_scalar_prefetch=1, grid=(nb,),
            in_specs=[pl.BlockSpec((1, tr, keep), lambda i, ch: (layer, i, 0))], out_specs=out_spec),
        compiler_params=_cparams(("parallel",)),
    )(chip, shard)


def _half_block(ref, k, s, half):
    sp = BIG_SPEC[k]
    hr = sp["rows"] // 2
    if sp["axis"] == 1:
        return ref.at[pl.ds(pl.multiple_of(half * hr, 16), hr), pl.ds(pl.multiple_of(s * sp["cols"], 128), sp["cols"])]
    return ref.at[pl.ds(pl.multiple_of(s * sp["rows"] + half * hr, 16), hr), :]


def _other_chips(x, y):
    return [(1 - x, y), (x, 1 - y), (1 - x, 1 - y)]


ALL_BIG = BIG


def _present(d):
    return tuple(k for k in ALL_BIG if k in d)


def _gather_layer(placed, conv):
    BIG = _present(placed)
    n = len(BIG)
    n_sem = 6 * n + 3

    def body(*refs):
        conv_ref = refs[n]
        out = dict(zip(BIG, refs[n + 1:2 * n + 1]))
        conv_out, send_sems, recv_sems, local_sem = refs[2 * n + 1:]
        x, y, c = _me()
        me = 2 * x + y
        chips = _other_chips(x, y)
        net = _Copies(send_sems, recv_sems)

        def conv_block(s):
            return conv_out.at[:, pl.ds(pl.multiple_of(s * CONV_COLS, 128), CONV_COLS)]

        local = pltpu.make_async_copy(conv_ref, conv_block(me), local_sem)
        local.start()
        first = {}
        for k in BIG:
            for j, (px, py) in enumerate(chips):
                first[k, j] = net.send(_half_block(out[k], k, me, c), _half_block(out[k], k, me, c), (px, py, c),
                                       _half_block(out[k], k, 2 * px + py, c))
        for px, py in chips:
            net.send(conv_ref, conv_block(me), (px, py, c), conv_block(2 * px + py))
        for k in BIG:
            for j, (px, py) in enumerate(chips):
                net.wait_landed(first[k, j])
                net.send(_half_block(out[k], k, 2 * px + py, c), _half_block(out[k], k, 2 * px + py, c), (x, y, 1 - c),
                         _half_block(out[k], k, 2 * px + py, 1 - c))
        net.finish(landed=set(first.values()))
        local.wait()

    out_shape = [jax.ShapeDtypeStruct(_full_shape(k), BF16) for k in BIG]
    out_shape.append(jax.ShapeDtypeStruct((CONV_ROWS, N_CHIPS * CONV_COLS), F32))
    outs = pl.pallas_call(
        body, name="gather_layer", out_shape=out_shape, in_specs=[HBM_SPEC] * (n + 1), out_specs=[HBM_SPEC] * (n + 1),
        input_output_aliases={i: i for i in range(n)},
        scratch_shapes=[pltpu.SemaphoreType.DMA((n_sem,)), pltpu.SemaphoreType.DMA((n_sem,)), pltpu.SemaphoreType.DMA],
    )(*[placed[k] for k in BIG], conv)
    return dict(zip(BIG, outs[:n])), outs[n]


SEM_SPEC = pl.BlockSpec(memory_space=pltpu.SEMAPHORE)


def _behind_copies(arrs, send_sems, recv_sems):
    x, y, c = _me()
    copies = []
    for i, k in enumerate(_present(arrs)):
        for j, (px, py) in enumerate(_other_chips(x, y)):
            copies.append(pltpu.make_async_remote_copy(
                src_ref=_half_block(arrs[k], k, 2 * x + y, c), dst_ref=_half_block(arrs[k], k, 2 * x + y, c),
                send_sem=send_sems.at[3 * i + j], recv_sem=recv_sems.at[3 * i + j], device_id=(px, py, c),
                device_id_type=MESH))
    return copies


def _gather_start(placed, after, tag):
    BIG = _present(placed)
    n = len(BIG)
    N_BEHIND = 3 * n

    def body(*refs):
        arrs = dict(zip(BIG, refs[n + 3:2 * n + 3]))
        send_sems, recv_sems = refs[n + 1], refs[n + 2]
        for cp in _behind_copies(arrs, send_sems, recv_sems):
            cp.start()
        refs[2 * n + 3][...] = jnp.zeros((8, 128), F32)

    outs = pl.pallas_call(
        body, name="gather_start" + tag,
        out_shape=(pltpu.SemaphoreType.DMA((N_BEHIND,)), pltpu.SemaphoreType.DMA((N_BEHIND,)),
                   *[pltpu.HBM(_full_shape(k), BF16) for k in BIG], jax.ShapeDtypeStruct((8, 128), F32)),
        in_specs=[HBM_SPEC] * n + [pl.BlockSpec(memory_space=pl.ANY)],
        out_specs=(SEM_SPEC, SEM_SPEC, *[HBM_SPEC] * n, VMEM_SPEC),
        input_output_aliases={i: i + 2 for i in range(n)},
        compiler_params=pltpu.CompilerParams(has_side_effects=pltpu.SideEffectType.DATAFLOW_SIDE_EFFECTING),
    )(*[pltpu.with_memory_space_constraint(placed[k], pltpu.HBM) for k in BIG], after)
    return outs[0], outs[1], dict(zip(BIG, outs[2:n + 2])), outs[n + 2]


def _gather_wait(send_sems, recv_sems, inflight, after, tag):
    BIG = _present(inflight)
    n = len(BIG)

    def body(*refs):
        arrs = dict(zip(BIG, refs[:n]))
        for cp in _behind_copies(arrs, refs[n], refs[n + 1]):
            cp.wait_send()
            cp.wait_recv()

    outs = pl.pallas_call(
        body, name="gather_wait" + tag, out_shape=tuple(pltpu.HBM(_full_shape(k), BF16) for k in BIG),
        in_specs=[HBM_SPEC] * n + [SEM_SPEC, SEM_SPEC, pl.BlockSpec(memory_space=pl.ANY)], out_specs=(HBM_SPEC,) * n,
        input_output_aliases={i: i for i in range(n)},
        compiler_params=pltpu.CompilerParams(has_side_effects=pltpu.SideEffectType.DATAFLOW_SIDE_EFFECTING),
    )(*[inflight[k] for k in BIG], send_sems, recv_sems, after)
    return dict(zip(BIG, outs))


def _gather_finish(arrs, tag):
    BIG = _present(arrs)
    n = len(BIG)
    N_BEHIND = 3 * n

    def body(*refs):
        out = dict(zip(BIG, refs[n:2 * n]))
        send_sems, recv_sems = refs[2 * n:]
        x, y, c = _me()
        net = _Copies(send_sems, recv_sems)
        for k in BIG:
            for px, py in _other_chips(x, y):
                net.send(_half_block(out[k], k, 2 * px + py, c), _half_block(out[k], k, 2 * px + py, c), (x, y, 1 - c),
                         _half_block(out[k], k, 2 * px + py, 1 - c))
        net.finish()

    outs = pl.pallas_call(
        body, name="gather_finish" + tag, out_shape=[jax.ShapeDtypeStruct(_full_shape(k), BF16) for k in BIG],
        in_specs=[HBM_SPEC] * n, out_specs=[HBM_SPEC] * n, input_output_aliases={i: i for i in range(n)},
        scratch_shapes=[pltpu.SemaphoreType.DMA((N_BEHIND,)), pltpu.SemaphoreType.DMA((N_BEHIND,))],
    )(*[arrs[k] for k in BIG])
    return dict(zip(BIG, outs))


def _row_chunks(ref, rows, n):
    step = rows // n
    return [ref.at[pl.ds(i * step, step), :] for i in range(n)]


def _half_pieces(ref, k, half):
    sp = BIG_SPEC[k]
    hr = sp["rows"] // 2
    if sp["axis"] == 1:
        return [ref.at[pl.ds(pl.multiple_of(half * hr, 16), hr), :]]
    return [ref.at[pl.ds(pl.multiple_of(s * sp["rows"] + half * hr, 16), hr), :] for s in range(N_CHIPS)]


def _half_shape(k):
    rows, cols = _full_shape(k)
    return rows // 2, cols


def _stacked_pieces(ref, k):
    sp = BIG_SPEC[k]
    hr = sp["rows"] // 2
    return [ref] if sp["axis"] == 1 else [ref.at[pl.ds(s * hr, hr), :] for s in range(N_CHIPS)]


def _chip_part(ref, k, s):
    sp = BIG_SPEC[k]
    hr = sp["rows"] // 2
    if sp["axis"] == 1:
        return ref.at[:, pl.ds(pl.multiple_of(s * sp["cols"], 128), sp["cols"])]
    return ref.at[pl.ds(pl.multiple_of(s * hr, 16), hr), :]


def _halves_to_sibling(grads, name):
    BIG = _present(grads)
    n = len(BIG)
    chunks = {k: max(BIG_SPEC[k]["down"] // 2, 1) if BIG_SPEC[k]["axis"] == 1 else 1 for k in BIG}
    n_sem = sum(chunks[k] if BIG_SPEC[k]["axis"] == 1 else N_CHIPS for k in BIG)

    def body(*refs):
        g = dict(zip(BIG, refs[:n]))
        out = dict(zip(BIG, refs[n:2 * n]))
        send_sems, recv_sems = refs[2 * n:]
        x, y, c = _me()
        net = _Copies(send_sems, recv_sems)
        for k in BIG:
            hr = BIG_SPEC[k]["rows"] // 2
            for src, dst in zip(_half_pieces(g[k], k, 1 - c), _stacked_pieces(out[k], k)):
                for s, d in zip(_row_chunks(src, hr, chunks[k]), _row_chunks(dst, hr, chunks[k])):
                    net.send(s, d, (x, y, 1 - c), d)
        net.finish()

    outs = pl.pallas_call(
        body, name=name, out_shape=[jax.ShapeDtypeStruct(_half_shape(k), BF16) for k in BIG],
        in_specs=[HBM_SPEC] * n, out_specs=[HBM_SPEC] * n,
        scratch_shapes=[pltpu.SemaphoreType.DMA((n_sem,)), pltpu.SemaphoreType.DMA((n_sem,))],
    )(*[grads[k] for k in BIG])
    return dict(zip(BIG, outs))


def _add_half(g, other, k, core, name):
    sp = BIG_SPEC[k]
    hr, cols = sp["rows"] // 2, _full_shape(k)[1]
    tr = _pick(hr, (256, 352, 128))
    nb = hr // tr
    if sp["axis"] == 1:
        grid = (nb,)
        g_spec = pl.BlockSpec((tr, cols), lambda i, c: (c[0] * nb + i, 0))
        h_spec = pl.BlockSpec((tr, cols), lambda i, c: (i, 0))
    else:
        grid = (N_CHIPS, nb)
        g_spec = pl.BlockSpec((tr, cols), lambda s, i, c: ((2 * s + c[0]) * nb + i, 0))
        h_spec = pl.BlockSpec((tr, cols), lambda s, i, c: (s * nb + i, 0))

    def kern(c_ref, a_ref, b_ref, o_ref):
        o_ref[...] = (a_ref[...].astype(F32) + b_ref[...].astype(F32)).astype(BF16)

    return pl.pallas_call(
        kern, name=name, out_shape=jax.ShapeDtypeStruct(_half_shape(k), BF16),
        grid_spec=pltpu.PrefetchScalarGridSpec(num_scalar_prefetch=1, grid=grid, in_specs=[g_spec, h_spec],
                                               out_specs=h_spec),
        compiler_params=_cparams(("parallel",) * len(grid)),
    )(core, g, other)


def _part_shape(k):
    return N_CHIPS - 1, BIG_SPEC[k]["rows"] // 2, BIG_SPEC[k]["cols"]


def _scatter_copies(sums, parts, send_sems, recv_sems):
    x, y, c = _me()
    copies = []
    for i, k in enumerate(_present(sums)):
        for j, (px, py) in enumerate(_other_chips(x, y)):
            copies.append(pltpu.make_async_remote_copy(
                src_ref=_chip_part(sums[k], k, 2 * px + py), dst_ref=parts[k].at[j], send_sem=send_sems.at[3 * i + j],
                recv_sem=recv_sems.at[3 * i + j], device_id=(px, py, c), device_id_type=MESH))
    return copies


def _scatter_start(sums, tag):
    BIG = _present(sums)
    n = len(BIG)
    N_BEHIND = 3 * n
    lands = [pltpu.with_memory_space_constraint(lax.empty(_part_shape(k), BF16), pltpu.HBM) for k in BIG]

    def body(*refs):
        outs = refs[2 * n + 2:4 * n + 2]
        for cp in _scatter_copies(dict(zip(BIG, outs[:n])), dict(zip(BIG, outs[n:])), refs[2 * n], refs[2 * n + 1]):
            cp.start()
        refs[4 * n + 2][...] = jnp.zeros((8, 128), F32)

    outs = pl.pallas_call(
        body, name="scatter_start" + tag,
        out_shape=(pltpu.SemaphoreType.DMA((N_BEHIND,)), pltpu.SemaphoreType.DMA((N_BEHIND,)),
                   *[pltpu.HBM(_half_shape(k), BF16) for k in BIG], *[pltpu.HBM(_part_shape(k), BF16) for k in BIG],
                   jax.ShapeDtypeStruct((8, 128), F32)),
        in_specs=[HBM_SPEC] * (2 * n), out_specs=(SEM_SPEC, SEM_SPEC, *[HBM_SPEC] * (2 * n), VMEM_SPEC),
        input_output_aliases={i: i + 2 for i in range(2 * n)},
        compiler_params=pltpu.CompilerParams(has_side_effects=pltpu.SideEffectType.DATAFLOW_SIDE_EFFECTING),
    )(*[pltpu.with_memory_space_constraint(sums[k], pltpu.HBM) for k in BIG], *lands)
    return outs[0], outs[1], outs[2:2 * n + 2], outs[2 * n + 2]


def _scatter_wait(send_sems, recv_sems, inflight, keys, after, tag):
    BIG = keys
    n = len(BIG)

    def body(*refs):
        for cp in _scatter_copies(dict(zip(BIG, refs[:n])), dict(zip(BIG, refs[n:2 * n])), refs[2 * n], refs[2 * n + 1]):
            cp.wait_send()
            cp.wait_recv()

    outs = pl.pallas_call(
        body, name="scatter_wait" + tag,
        out_shape=(*[pltpu.HBM(_half_shape(k), BF16) for k in BIG], *[pltpu.HBM(_part_shape(k), BF16) for k in BIG]),
        in_specs=[HBM_SPEC] * (2 * n) + [SEM_SPEC, SEM_SPEC, pl.BlockSpec(memory_space=pl.ANY)],
        out_specs=(HBM_SPEC,) * (2 * n), input_output_aliases={i: i for i in range(2 * n)},
        compiler_params=pltpu.CompilerParams(has_side_effects=pltpu.SideEffectType.DATAFLOW_SIDE_EFFECTING),
    )(*inflight, send_sems, recv_sems, after)
    return dict(zip(BIG, outs[:n])), dict(zip(BIG, outs[n:]))


def _sum_half(parts, own, k, where, layer, into, name):
    sp = BIG_SPEC[k]
    rows, cols, keep = sp["rows"], sp["cols"], sp["keep"]
    hr = rows // 2
    tr = _pick(hr, (256, 352, 128))
    nb = hr // tr
    if sp["axis"] == 1:
        own_spec = pl.BlockSpec((tr, cols), lambda i, w: (i, w[0]))
    else:
        own_spec = pl.BlockSpec((tr, cols), lambda i, w: (w[0] * nb + i, 0))

    def kern(w_ref, p_ref, own_ref, *rest):
        tot = own_ref[...].astype(F32)
        for j in range(N_CHIPS - 1):
            tot = tot + p_ref[j].astype(F32)
        rest[-1][0] = tot[:, :keep]

    in_specs = [pl.BlockSpec((N_CHIPS - 1, tr, cols), lambda i, w: (0, i, 0)), own_spec]
    args = [where, parts, own]
    if into is not None:
        in_specs.append(pl.BlockSpec(memory_space=pl.ANY))
        args.append(into)
    return pl.pallas_call(
        kern, name=name, out_shape=jax.ShapeDtypeStruct((DEPTH, rows, keep), F32),
        grid_spec=pltpu.PrefetchScalarGridSpec(
            num_scalar_prefetch=1, grid=(nb,), in_specs=in_specs,
            out_specs=pl.BlockSpec((1, tr, keep), lambda i, w: (layer, w[1] * nb + i, 0))),
        input_output_aliases={} if into is None else {3: 0},
        compiler_params=_cparams(("parallel",)),
    )(*args)


def _exchange_halves(red):
    n = len(BIG)

    def body(*refs):
        out = dict(zip(BIG, refs[n:2 * n]))
        send_sems, recv_sems = refs[2 * n:]
        x, y, c = _me()
        net = _Copies(send_sems, recv_sems)
        for k in BIG:
            hr = BIG_SPEC[k]["rows"] // 2
            for l in range(DEPTH):
                mine = out[k].at[l, pl.ds(pl.multiple_of(c * hr, 8), hr), :]
                theirs = out[k].at[l, pl.ds(pl.multiple_of((1 - c) * hr, 8), hr), :]
                net.send(mine, mine, (x, y, 1 - c), theirs)
        net.finish()

    outs = pl.pallas_call(
        body, name="exchange_halves",
        out_shape=[jax.ShapeDtypeStruct((DEPTH, BIG_SPEC[k]["rows"], BIG_SPEC[k]["keep"]), F32) for k in BIG],
        in_specs=[HBM_SPEC] * n, out_specs=[HBM_SPEC] * n, input_output_aliases={i: i for i in range(n)},
        scratch_shapes=[pltpu.SemaphoreType.DMA((DEPTH * n,)), pltpu.SemaphoreType.DMA((DEPTH * n,))],
    )(*[red[k] for k in BIG])
    return dict(zip(BIG, outs))


def _adam_vals(g, w, m, v):
    m2 = ADAM_B1 * m + (1.0 - ADAM_B1) * g
    v2 = ADAM_B2 * v + (1.0 - ADAM_B2) * (g * g)
    m_hat = m2 / (1.0 - ADAM_B1 ** ADAM_STEP)
    v_hat = v2 / (1.0 - ADAM_B2 ** ADAM_STEP)
    return -ADAM_LR * (m_hat / (jnp.sqrt(v_hat) + ADAM_EPS) + ADAM_WD * w), m2, v2


def _allreduce_small_adam(groups):
    ng = len(groups)

    def body(*refs):
        ins = [refs[4 * i:4 * i + 4] for i in range(ng)]
        outs = [refs[4 * ng + 4 * i:4 * ng + 4 * i + 4] for i in range(ng)]
        bufs = refs[8 * ng:9 * ng]
        send_sems, recv_sems = refs[9 * ng:]
        x, y, c = _me()
        me = 4 * x + 2 * y + c
        net = _Copies(send_sems, recv_sems)
        for (g_ref, _, _, _), buf in zip(ins, bufs):
            buf[me] = g_ref[...]
            for f in FLIPS:
                px, py, pc = _peer(x, y, c, f)
                net.send(g_ref, buf.at[me], (px, py, pc), buf.at[4 * px + 2 * py + pc])
        net.finish()
        for (_, w_ref, m_ref, v_ref), (gs_ref, d_ref, nm_ref, nv_ref), buf in zip(ins, outs, bufs):
            tot = buf[0]
            for d in range(1, 8):
                tot = tot + buf[d]
            gs_ref[...] = tot
            d_ref[...], nm_ref[...], nv_ref[...] = _adam_vals(tot, w_ref[...], m_ref[...], v_ref[...])

    shapes = [jax.ShapeDtypeStruct(g[0].shape, F32) for g in groups for _ in range(4)]
    outs = pl.pallas_call(
        body, name="allreduce_small", out_shape=shapes, in_specs=[VMEM_SPEC] * (4 * ng), out_specs=[VMEM_SPEC] * (4 * ng),
        scratch_shapes=[pltpu.VMEM((8,) + g[0].shape, F32) for g in groups]
        + [pltpu.SemaphoreType.DMA((7 * ng,)), pltpu.SemaphoreType.DMA((7 * ng,))],
        compiler_params=pltpu.CompilerParams(vmem_limit_bytes=VMEM_LIMIT),
    )(*[t for g in groups for t in g])
    return [outs[4 * i:4 * i + 4] for i in range(ng)]


def _adam(g, w, m, v, name, lead_block=1):
    shape = w.shape
    lead, rows, cols = math.prod(shape[:-2]), shape[-2], shape[-1]
    tr = _pick(rows, (256, 352, 64, 8, rows))
    spec = pl.BlockSpec((lead_block, tr, cols), lambda l, i: (l, i, 0))

    def kern(g_ref, w_ref, m_ref, v_ref, d_ref, nm_ref, nv_ref):
        d_ref[...], nm_ref[...], nv_ref[...] = _adam_vals(g_ref[...], w_ref[...], m_ref[...], v_ref[...])

    outs = pl.pallas_call(
        kern, name=name, grid=(lead // lead_block, rows // tr), in_specs=[spec] * 4, out_specs=[spec] * 3,
        out_shape=[jax.ShapeDtypeStruct((lead, rows, cols), F32)] * 3, compiler_params=_cparams(("parallel", "parallel")),
    )(*[t.reshape(lead, rows, cols) for t in (g, w, m, v)])
    return [o.reshape(shape) for o in outs]


SMALL = ("attn_norm", "sgu_ln_g", "sgu_ln_b", "sgu_w", "sgu_b", "attn_sinks", "dn_a_log", "dn_dt_bias", "dn_norm",
         "ffn_norm", "final_norm")
SMALL_2D = {"attn_norm": (DEPTH, D_MODEL), "ffn_norm": (DEPTH, D_MODEL), "final_norm": (1, D_MODEL),
            "sgu_ln_g": (DEPTH, MIX), "sgu_ln_b": (DEPTH, MIX), "sgu_w": (DEPTH * SGU_G * SGU_T, SGU_T),
            "sgu_b": (DEPTH * SGU_G, SGU_T), "dn_norm": (DEPTH, DN_HD)}
TINY = ("attn_sinks", "dn_a_log", "dn_dt_bias")


def _pack_tiny(vals, extra=None):
    flat = [vals[k].astype(F32).reshape(-1) for k in TINY] + ([] if extra is None else [extra.astype(F32).reshape(-1)])
    n = sum(f.shape[0] for f in flat)
    return jnp.concatenate(flat + [jnp.zeros((8 * 128 - n,), F32)]).reshape(8, 128)


def _unpack_tiny(tile, shapes):
    flat, out, o = tile.reshape(-1), {}, 0
    for k in TINY:
        n = math.prod(shapes[k])
        out[k] = flat[o:o + n].reshape(shapes[k])
        o += n
    return out, flat[o]


def _in_col_segments():
    shard, padded = IN_COLS // N_CHIPS, BIG_SPEC["w_in"]["cols"]
    segs, mine = [], 0
    for a, n in IN_PIECES:
        o = a
        while o < a + n:
            end = min(a + n, (o // shard + 1) * shard)
            segs.append(((o // shard) * padded + o % shard, mine + o - a, end - o))
            o = end
        mine += n
    return segs


def _move_cols(x, segs, out_cols, name):
    layers, rows, cols = x.shape
    tr = _pick(rows, (256, rows))
    gaps, at = [], 0
    for d, w in sorted((d, w) for _, d, w in segs):
        if d > at:
            gaps.append((at, d - at))
        at = d + w
    if at < out_cols:
        gaps.append((at, out_cols - at))

    def kern(x_ref, o_ref):
        for s, d, w in segs:
            o_ref[0, :, d:d + w] = x_ref[0, :, s:s + w]
        for d, w in gaps:
            o_ref[0, :, d:d + w] = jnp.zeros((tr, w), x.dtype)

    return pl.pallas_call(
        kern, name=name, grid=(layers, rows // tr), in_specs=[pl.BlockSpec((1, tr, cols), lambda l, i: (l, i, 0))],
        out_specs=pl.BlockSpec((1, tr, out_cols), lambda l, i: (l, i, 0)),
        out_shape=jax.ShapeDtypeStruct((layers, rows, out_cols), x.dtype), compiler_params=_cparams(("parallel", "parallel")),
    )(x)


WEIGHTS = ("attn_norm", "w_in", "sgu_ln_g", "sgu_ln_b", "sgu_w", "sgu_b", "attn_sinks", "dn_conv_w", "dn_a_log",
           "dn_dt_bias", "dn_norm", "w_branch", "w_out", "ffn_norm", "w_gate_up", "w_down", "final_norm")


def kernel(x, positions, attn_norm, w_in, sgu_ln_g, sgu_ln_b, sgu_w, sgu_b, attn_sinks, dn_conv_w, dn_a_log, dn_dt_bias, dn_norm, w_branch, w_out, ffn_norm, w_gate_up, w_down, final_norm, loss_target, m_attn_norm, m_w_in, m_sgu_ln_g, m_sgu_ln_b, m_sgu_w, m_sgu_b, m_attn_sinks, m_dn_conv_w, m_dn_a_log, m_dn_dt_bias, m_dn_norm, m_w_branch, m_w_out, m_ffn_norm, m_w_gate_up, m_w_down, m_final_norm, v_attn_norm, v_w_in, v_sgu_ln_g, v_sgu_ln_b, v_sgu_w, v_sgu_b, v_attn_sinks, v_dn_conv_w, v_dn_a_log, v_dn_dt_bias, v_dn_norm, v_w_branch, v_w_out, v_ffn_norm, v_w_gate_up, v_w_down, v_final_norm):
    given = dict(locals())
    W = {k: given[k] for k in WEIGHTS}
    M = {k: given["m_" + k] for k in WEIGHTS}
    V = {k: given["v_" + k] for k in WEIGHTS}
    chip = 2 * lax.axis_index("x") + lax.axis_index("y")
    core = lax.axis_index("c")
    chip1 = chip.astype(jnp.int32).reshape(1)
    where = jnp.stack([chip, core]).astype(jnp.int32)

    placed = [{k: _place_shard(W[k].reshape(DEPTH, BIG_SPEC[k]["rows"], BIG_SPEC[k]["keep"]), k, chip1, l,
                               f"place{l}_{k}") for k in BIG} for l in range(DEPTH)]
    _, conv_full = _gather_layer({}, dn_conv_w.reshape(CONV_ROWS, CONV_COLS))
    behind = {"in": _gather_start({"w_in": placed[0]["w_in"]}, conv_full, "in")}
    behind["0"] = _gather_start({k: placed[0][k] for k in BIG if k != "w_in"}, behind["in"][3], "0")
    behind["1"] = _gather_start(placed[1], behind["0"][3], "1")
    segs = _in_col_segments()

    def arrived(tag, after):
        send_sems, recv_sems, inflight, _ = behind[tag]
        return _gather_finish(_gather_wait(send_sems, recv_sems, inflight, after, tag), tag)

    def big_of_layer(l, x_l):
        got = {} if l == 0 else arrived("1", x_l)
        cols = lambda t: _move_cols(t[None], segs, IN_R, f"w_in_cols{l}")[0]

        def rest(after):
            full = got or arrived("0", after)
            return dict(full, w_branch=full["w_branch"].reshape(3, MIX, D_MODEL))

        return dict(w_in=cols(got["w_in"]) if got else (lambda after: cols(arrived("in", after)["w_in"])), rest=rest)

    w = {k: W[k] for k in SMALL}
    w["attn_norm"] = attn_norm + behind["1"][3][0, 0]
    w["dn_conv_w"] = conv_full.reshape(DEPTH, DN_CONV, 3 * MIX)

    core1 = core.astype(jnp.int32).reshape(1)
    back_segs = [(d, s, n) for s, d, n in segs]
    travelling, started, sums, parts = [], [], [{}, {}], [{}, {}]

    def on_grads(l, gl):
        gl, tag = dict(gl), f"{l}_{len(gl)}"
        if "w_in" in gl:
            gl["w_in"] = _move_cols(gl["w_in"][None], back_segs, _full_shape("w_in")[1], f"g_in_cols{l}")[0]
        if "w_branch" in gl:
            gl["w_branch"] = gl["w_branch"].reshape(3 * MIX, D_MODEL)
        sibling = _halves_to_sibling(gl, "halves_to_sibling" + tag)
        chip_sums = {k: _add_half(gl[k], sibling[k], k, core1, f"chip_sum{l}_{k}") for k in gl}
        send_sems, recv_sems, inflight, token = _scatter_start(chip_sums, tag)
        travelling.append((l, send_sems, recv_sems, inflight, _present(gl), tag))
        started.append(token)
        return token

    loss, dx, g = _local_step(x[0], positions[0], loss_target[0], w, big_of_layer, on_grads)

    grads = {}
    conv_2d = (CONV_ROWS, N_CHIPS * CONV_COLS)
    conv_zero = jnp.zeros(conv_2d, F32)
    groups = [tuple(d[k].reshape(SMALL_2D[k]) for d in (g, W, M, V)) for k in SMALL_2D]
    groups.append((g["dn_conv_w"].reshape(conv_2d), conv_zero, conv_zero, conv_zero))
    loss = loss + started[-1][0, 0]
    groups.append((_pack_tiny(g, loss), _pack_tiny(W), _pack_tiny(M), _pack_tiny(V)))
    summed = _allreduce_small_adam(groups)
    delta, new_m, new_v = {}, {}, {}
    for k, outs in zip(SMALL_2D, summed):
        for d, t in zip((grads, delta, new_m, new_v), outs):
            d[k] = t.reshape(W[k].shape)
    conv_sum = summed[len(SMALL_2D)][0].reshape(g["dn_conv_w"].shape)
    grads["dn_conv_w"] = lax.dynamic_slice_in_dim(conv_sum, chip * dn_conv_w.shape[2], dn_conv_w.shape[2], axis=2)
    tiny_shapes = {k: W[k].shape for k in TINY}
    tiny, loss_total = _unpack_tiny(summed[-1][0], tiny_shapes)
    grads.update(tiny)
    for d, t in zip((delta, new_m, new_v), summed[-1][1:]):
        d.update(_unpack_tiny(t, tiny_shapes)[0])

    for l, send_sems, recv_sems, inflight, keys, tag in travelling:
        landed = _scatter_wait(send_sems, recv_sems, inflight, keys, summed[0][0], tag)
        sums[l].update(landed[0])
        parts[l].update(landed[1])
    red = {k: _sum_half(parts[1][k], sums[1][k], k, where, 1, None, f"sum1_{k}") for k in BIG}
    red = {k: _sum_half(parts[0][k], sums[0][k], k, where, 0, red[k], f"sum0_{k}") for k in BIG}
    reduced = _exchange_halves(red)
    grads.update({k: reduced[k].reshape(W[k].shape) for k in BIG})
    for k in ("w_branch", "w_out", "w_gate_up", "w_down", "dn_conv_w"):
        delta[k], new_m[k], new_v[k] = _adam(grads[k], W[k], M[k], V[k], "adam_" + k)
    lead_first = lambda t: jnp.transpose(t, (2, 0, 1))
    outs = _adam(*[lead_first(d["w_in"]) for d in (grads, W, M, V)], "adam_w_in", lead_block=IN_COLS // N_CHIPS // 10)
    delta["w_in"], new_m["w_in"], new_v["w_in"] = (jnp.transpose(o, (1, 2, 0)) for o in outs)

    return (loss_total, dx[None], *[grads[k] for k in WEIGHTS], *[delta[k] for k in WEIGHTS],
            *[new_m[k] for k in WEIGHTS], *[new_v[k] for k in WEIGHTS])
```
